```python
import math
import numpy as np
import jax
import jax.numpy as jnp
from jax import lax

D_MODEL = 1024
BATCH = 8
SEQ = 2048
DEPTH = 1

CHUNK = 64
HEAD_DIM = 64
D_FF = 2816
EPS = 1e-6
NEG_INF = -1e30
H_A = 8
KV_A = 2
G_A = H_A // KV_A
WINDOW_A = 128
BACK_A = WINDOW_A // CHUNK
H_B = 8
BACK_B = 8
REL_CLIP = 128
N_REL = 2 * REL_CLIP + 1
QA = H_A * HEAD_DIM
KVA = KV_A * HEAD_DIM
QB = H_B * HEAD_DIM
D_IN = QA + 2 * KVA + 3 * QB
SPLITS = (QA, QA + KVA, QA + 2 * KVA, QA + 2 * KVA + QB, QA + 2 * KVA + 2 * QB)
D_MIX = QA + QB
N_MOD = 9

kernel_name = "hybrid_chunk_causal_swa_sink_relbias_macaron"


def rms_norm(x, g):
    x32 = x.astype(jnp.float32)
    y = x32 * lax.rsqrt(jnp.mean(x32 * x32, axis=-1, keepdims=True) + EPS)
    return y.astype(x.dtype) * g


def swiglu(h, w_gate, w_up, w_down):
    return (jax.nn.silu(h @ w_gate) * (h @ w_up)) @ w_down


def chunk_band(t, n_back):
    b, s, h, d = t.shape
    n = s // CHUNK
    tc = t.reshape(b, n, CHUNK, h, d)
    tp = jnp.pad(tc, ((0, 0), (n_back, 0), (0, 0), (0, 0), (0, 0)))
    return jnp.concatenate([tp[:, j:j + n] for j in range(n_back + 1)], axis=2)


def band_distance(n_back):
    i = jnp.arange(CHUNK, dtype=jnp.int32)[:, None]
    j = jnp.arange((n_back + 1) * CHUNK, dtype=jnp.int32)[None, :]
    return n_back * CHUNK + i - j


def band_valid(n_chunks, n_back):
    kc = (jnp.arange(n_chunks, dtype=jnp.int32)[:, None] - n_back
          + jnp.arange((n_back + 1) * CHUNK, dtype=jnp.int32)[None, :] // CHUNK)
    return kc >= 0


def band_attention(q, k_band, v_band, bias, valid, sink):
    scale = q.shape[-1] ** -0.5
    s = jnp.einsum('bnqhgd,bnshd->bnhgqs', q, k_band).astype(jnp.float32) * scale
    s = jnp.where(valid[None, :, None, None, None, :], s + bias[None, None], NEG_INF)
    if sink is None:
        p = jax.nn.softmax(s, axis=-1)
    else:
        sk = sink.astype(jnp.float32)[None, None, :, :, None, None]
        m = jnp.maximum(jnp.max(s, axis=-1, keepdims=True), sk)
        e = jnp.exp(s - m)
        p = e / (jnp.sum(e, axis=-1, keepdims=True) + jnp.exp(sk - m))
    return jnp.einsum('bnhgqs,bnshd->bnqhgd', p.astype(v_band.dtype), v_band)


def alibi_slopes(n_heads):
    return jnp.asarray(np.array([2.0 ** (-8.0 * (i + 1) / n_heads) for i in range(n_heads)], dtype=np.float32))


def hybrid_mixer(h, w_in, b_in, sinks_a, rel_bias_b, g_grp_a, g_grp_b, w_out, b_out):
    b, s, _ = h.shape
    n = s // CHUNK
    proj = h @ w_in + b_in
    qa, ka, va, qb, kb, vb = jnp.split(proj, SPLITS, axis=-1)

    qa = qa.reshape(b, n, CHUNK, KV_A, G_A, HEAD_DIM)
    ka_band = chunk_band(ka.reshape(b, s, KV_A, HEAD_DIM), BACK_A)
    va_band = chunk_band(va.reshape(b, s, KV_A, HEAD_DIM), BACK_A)
    dist_a = jnp.abs(band_distance(BACK_A)).astype(jnp.float32)
    bias_a = (-alibi_slopes(H_A)[:, None, None] * dist_a[None]).reshape(KV_A, G_A, CHUNK, -1)
    oa = band_attention(qa, ka_band, va_band, bias_a, band_valid(n, BACK_A),
                        sinks_a.reshape(KV_A, G_A)).reshape(b, s, QA)

    qb = qb.reshape(b, n, CHUNK, H_B, 1, HEAD_DIM)
    kb_band = chunk_band(kb.reshape(b, s, H_B, HEAD_DIM), BACK_B)
    vb_band = chunk_band(vb.reshape(b, s, H_B, HEAD_DIM), BACK_B)
    rel_idx = jnp.clip(band_distance(BACK_B), -REL_CLIP, REL_CLIP) + REL_CLIP
    bias_b = rel_bias_b.astype(jnp.float32)[:, rel_idx][:, None]
    ob = band_attention(qb, kb_band, vb_band, bias_b, band_valid(n, BACK_B), None).reshape(b, s, QB)

    y = jnp.concatenate([rms_norm(oa, g_grp_a), rms_norm(ob, g_grp_b)], axis=-1)
    return y @ w_out + b_out


def sandwich(x, y_fn, g_pre, g_post, shift, scale, gate, weight):
    h = rms_norm(x, g_pre) * (1.0 + scale[:, None, :]) + shift[:, None, :]
    y = rms_norm(y_fn(h), g_post)
    return x + weight * gate[:, None, :] * y


def _fwd_setup_inputs(seed: int = 0) -> dict:
    key = jax.random.key(seed)
    ks = jax.random.split(key, 32)
    f32 = jnp.float32

    def w(k, shape, fan_in, mult=1.0):
        return jax.random.normal(k, shape, f32) * (mult * fan_in ** -0.5)

    def gain(k, n):
        return 1.0 + 0.05 * jax.random.normal(k, (DEPTH, n), f32)

    def bias(k, n):
        return 0.02 * jax.random.normal(k, (DEPTH, n), f32)

    return {
        "x": jax.random.normal(ks[0], (BATCH, SEQ, D_MODEL), f32),
        "c": jax.random.normal(ks[1], (BATCH, D_MODEL), f32),
        "w_ada": w(ks[2], (DEPTH, D_MODEL, N_MOD * D_MODEL), D_MODEL, 0.3),
        "b_ada": bias(ks[3], N_MOD * D_MODEL),
        "g_pre_ffn1": gain(ks[4], D_MODEL),
        "w_gate1": w(ks[5], (DEPTH, D_MODEL, D_FF), D_MODEL),
        "w_up1": w(ks[6], (DEPTH, D_MODEL, D_FF), D_MODEL),
        "w_down1": w(ks[7], (DEPTH, D_FF, D_MODEL), D_FF),
        "g_post_ffn1": gain(ks[8], D_MODEL),
        "g_pre_mix": gain(ks[9], D_MODEL),
        "w_in": w(ks[10], (DEPTH, D_MODEL, D_IN), D_MODEL),
        "b_in": bias(ks[11], D_IN),
        "sinks_a": jax.random.normal(ks[12], (DEPTH, H_A), f32),
        "rel_bias_b": 0.5 * jax.random.normal(ks[13], (DEPTH, H_B, N_REL), f32),
        "g_grp_a": gain(ks[14], QA),
        "g_grp_b": gain(ks[15], QB),
        "w_out": w(ks[16], (DEPTH, D_MIX, D_MODEL), D_MIX),
        "b_out": bias(ks[17], D_MODEL),
        "g_post_mix": gain(ks[18], D_MODEL),
        "g_pre_ffn2": gain(ks[19], D_MODEL),
        "w_gate2": w(ks[20], (DEPTH, D_MODEL, D_FF), D_MODEL),
        "w_up2": w(ks[21], (DEPTH, D_MODEL, D_FF), D_MODEL),
        "w_down2": w(ks[22], (DEPTH, D_FF, D_MODEL), D_FF),
        "g_post_ffn2": gain(ks[23], D_MODEL),
    }


def _fwd_reference(x, c, w_ada, b_ada, g_pre_ffn1, w_gate1, w_up1, w_down1, g_post_ffn1,
              g_pre_mix, w_in, b_in, sinks_a, rel_bias_b, g_grp_a, g_grp_b, w_out, b_out,
              g_post_mix, g_pre_ffn2, w_gate2, w_up2, w_down2, g_post_ffn2):
    bsz = c.shape[0]
    for l in range(DEPTH):
        mod = (jax.nn.silu(c) @ w_ada[l] + b_ada[l]).reshape(bsz, N_MOD, D_MODEL)
        x = sandwich(x, lambda h: swiglu(h, w_gate1[l], w_up1[l], w_down1[l]),
                     g_pre_ffn1[l], g_post_ffn1[l], mod[:, 0], mod[:, 1], mod[:, 2], 0.5)
        x = sandwich(x, lambda h: hybrid_mixer(h, w_in[l], b_in[l], sinks_a[l], rel_bias_b[l],
                                               g_grp_a[l], g_grp_b[l], w_out[l], b_out[l]),
                     g_pre_mix[l], g_post_mix[l], mod[:, 3], mod[:, 4], mod[:, 5], 1.0)
        x = sandwich(x, lambda h: swiglu(h, w_gate2[l], w_up2[l], w_down2[l]),
                     g_pre_ffn2[l], g_post_ffn2[l], mod[:, 6], mod[:, 7], mod[:, 8], 0.5)
    return x


import jax as _jax
import jax.numpy as _jnp

TWIN_FORMAT = 'train_step'
FWD_PARAMS = ['x', 'c', 'w_ada', 'b_ada', 'g_pre_ffn1', 'w_gate1', 'w_up1', 'w_down1', 'g_post_ffn1', 'g_pre_mix', 'w_in', 'b_in', 'sinks_a', 'rel_bias_b', 'g_grp_a', 'g_grp_b', 'w_out', 'b_out', 'g_post_mix', 'g_pre_ffn2', 'w_gate2', 'w_up2', 'w_down2', 'g_post_ffn2']
TWIN_WEIGHTS = ['w_ada', 'b_ada', 'g_pre_ffn1', 'w_gate1', 'w_up1', 'w_down1', 'g_post_ffn1', 'g_pre_mix', 'w_in', 'b_in', 'sinks_a', 'rel_bias_b', 'g_grp_a', 'g_grp_b', 'w_out', 'b_out', 'g_post_mix', 'g_pre_ffn2', 'w_gate2', 'w_up2', 'w_down2', 'g_post_ffn2']
TWIN_DIFF_INPUT = 'x'
TWIN_INPUTS = ['x', 'c', 'w_ada', 'b_ada', 'g_pre_ffn1', 'w_gate1', 'w_up1', 'w_down1', 'g_post_ffn1', 'g_pre_mix', 'w_in', 'b_in', 'sinks_a', 'rel_bias_b', 'g_grp_a', 'g_grp_b', 'w_out', 'b_out', 'g_post_mix', 'g_pre_ffn2', 'w_gate2', 'w_up2', 'w_down2', 'g_post_ffn2', 'loss_target', 'm_w_ada', 'm_b_ada', 'm_g_pre_ffn1', 'm_w_gate1', 'm_w_up1', 'm_w_down1', 'm_g_post_ffn1', 'm_g_pre_mix', 'm_w_in', 'm_b_in', 'm_sinks_a', 'm_rel_bias_b', 'm_g_grp_a', 'm_g_grp_b', 'm_w_out', 'm_b_out', 'm_g_post_mix', 'm_g_pre_ffn2', 'm_w_gate2', 'm_w_up2', 'm_w_down2', 'm_g_post_ffn2', 'v_w_ada', 'v_b_ada', 'v_g_pre_ffn1', 'v_w_gate1', 'v_w_up1', 'v_w_down1', 'v_g_post_ffn1', 'v_g_pre_mix', 'v_w_in', 'v_b_in', 'v_sinks_a', 'v_rel_bias_b', 'v_g_grp_a', 'v_g_grp_b', 'v_w_out', 'v_b_out', 'v_g_post_mix', 'v_g_pre_ffn2', 'v_w_gate2', 'v_w_up2', 'v_w_down2', 'v_g_post_ffn2']
TWIN_OUTPUTS = ['loss', 'grad_x', 'grad_w_ada', 'grad_b_ada', 'grad_g_pre_ffn1', 'grad_w_gate1', 'grad_w_up1', 'grad_w_down1', 'grad_g_post_ffn1', 'grad_g_pre_mix', 'grad_w_in', 'grad_b_in', 'grad_sinks_a', 'grad_rel_bias_b', 'grad_g_grp_a', 'grad_g_grp_b', 'grad_w_out', 'grad_b_out', 'grad_g_post_mix', 'grad_g_pre_ffn2', 'grad_w_gate2', 'grad_w_up2', 'grad_w_down2', 'grad_g_post_ffn2', 'delta_w_ada', 'delta_b_ada', 'delta_g_pre_ffn1', 'delta_w_gate1', 'delta_w_up1', 'delta_w_down1', 'delta_g_post_ffn1', 'delta_g_pre_mix', 'delta_w_in', 'delta_b_in', 'delta_sinks_a', 'delta_rel_bias_b', 'delta_g_grp_a', 'delta_g_grp_b', 'delta_w_out', 'delta_b_out', 'delta_g_post_mix', 'delta_g_pre_ffn2', 'delta_w_gate2', 'delta_w_up2', 'delta_w_down2', 'delta_g_post_ffn2', 'new_m_w_ada', 'new_m_b_ada', 'new_m_g_pre_ffn1', 'new_m_w_gate1', 'new_m_w_up1', 'new_m_w_down1', 'new_m_g_post_ffn1', 'new_m_g_pre_mix', 'new_m_w_in', 'new_m_b_in', 'new_m_sinks_a', 'new_m_rel_bias_b', 'new_m_g_grp_a', 'new_m_g_grp_b', 'new_m_w_out', 'new_m_b_out', 'new_m_g_post_mix', 'new_m_g_pre_ffn2', 'new_m_w_gate2', 'new_m_w_up2', 'new_m_w_down2', 'new_m_g_post_ffn2', 'new_v_w_ada', 'new_v_b_ada', 'new_v_g_pre_ffn1', 'new_v_w_gate1', 'new_v_w_up1', 'new_v_w_down1', 'new_v_g_post_ffn1', 'new_v_g_pre_mix', 'new_v_w_in', 'new_v_b_in', 'new_v_sinks_a', 'new_v_rel_bias_b', 'new_v_g_grp_a', 'new_v_g_grp_b', 'new_v_w_out', 'new_v_b_out', 'new_v_g_post_mix', 'new_v_g_pre_ffn2', 'new_v_w_gate2', 'new_v_w_up2', 'new_v_w_down2', 'new_v_g_post_ffn2']
TWIN_LEAF_KINDS = {'loss': 'loss', 'grad_x': 'grad_x', 'grad_w_ada': 'grad_w', 'grad_b_ada': 'grad_w', 'grad_g_pre_ffn1': 'grad_w', 'grad_w_gate1': 'grad_w', 'grad_w_up1': 'grad_w', 'grad_w_down1': 'grad_w', 'grad_g_post_ffn1': 'grad_w', 'grad_g_pre_mix': 'grad_w', 'grad_w_in': 'grad_w', 'grad_b_in': 'grad_w', 'grad_sinks_a': 'grad_w', 'grad_rel_bias_b': 'grad_w', 'grad_g_grp_a': 'grad_w', 'grad_g_grp_b': 'grad_w', 'grad_w_out': 'grad_w', 'grad_b_out': 'grad_w', 'grad_g_post_mix': 'grad_w', 'grad_g_pre_ffn2': 'grad_w', 'grad_w_gate2': 'grad_w', 'grad_w_up2': 'grad_w', 'grad_w_down2': 'grad_w', 'grad_g_post_ffn2': 'grad_w', 'delta_w_ada': 'delta_w', 'delta_b_ada': 'delta_w', 'delta_g_pre_ffn1': 'delta_w', 'delta_w_gate1': 'delta_w', 'delta_w_up1': 'delta_w', 'delta_w_down1': 'delta_w', 'delta_g_post_ffn1': 'delta_w', 'delta_g_pre_mix': 'delta_w', 'delta_w_in': 'delta_w', 'delta_b_in': 'delta_w', 'delta_sinks_a': 'delta_w', 'delta_rel_bias_b': 'delta_w', 'delta_g_grp_a': 'delta_w', 'delta_g_grp_b': 'delta_w', 'delta_w_out': 'delta_w', 'delta_b_out': 'delta_w', 'delta_g_post_mix': 'delta_w', 'delta_g_pre_ffn2': 'delta_w', 'delta_w_gate2': 'delta_w', 'delta_w_up2': 'delta_w', 'delta_w_down2': 'delta_w', 'delta_g_post_ffn2': 'delta_w', 'new_m_w_ada': 'new_m', 'new_m_b_ada': 'new_m', 'new_m_g_pre_ffn1': 'new_m', 'new_m_w_gate1': 'new_m', 'new_m_w_up1': 'new_m', 'new_m_w_down1': 'new_m', 'new_m_g_post_ffn1': 'new_m', 'new_m_g_pre_mix': 'new_m', 'new_m_w_in': 'new_m', 'new_m_b_in': 'new_m', 'new_m_sinks_a': 'new_m', 'new_m_rel_bias_b': 'new_m', 'new_m_g_grp_a': 'new_m', 'new_m_g_grp_b': 'new_m', 'new_m_w_out': 'new_m', 'new_m_b_out': 'new_m', 'new_m_g_post_mix': 'new_m', 'new_m_g_pre_ffn2': 'new_m', 'new_m_w_gate2': 'new_m', 'new_m_w_up2': 'new_m', 'new_m_w_down2': 'new_m', 'new_m_g_post_ffn2': 'new_m', 'new_v_w_ada': 'new_v', 'new_v_b_ada': 'new_v', 'new_v_g_pre_ffn1': 'new_v', 'new_v_w_gate1': 'new_v', 'new_v_w_up1': 'new_v', 'new_v_w_down1': 'new_v', 'new_v_g_post_ffn1': 'new_v', 'new_v_g_pre_mix': 'new_v', 'new_v_w_in': 'new_v', 'new_v_b_in': 'new_v', 'new_v_sinks_a': 'new_v', 'new_v_rel_bias_b': 'new_v', 'new_v_g_grp_a': 'new_v', 'new_v_g_grp_b': 'new_v', 'new_v_w_out': 'new_v', 'new_v_b_out': 'new_v', 'new_v_g_post_mix': 'new_v', 'new_v_g_pre_ffn2': 'new_v', 'new_v_w_gate2': 'new_v', 'new_v_w_up2': 'new_v', 'new_v_w_down2': 'new_v', 'new_v_g_post_ffn2': 'new_v'}


def _forward(args):
    return _fwd_reference(*[args[k] for k in FWD_PARAMS])


def _output_shape():
    out = _jax.eval_shape(lambda: _forward(_fwd_setup_inputs(0)))
    return out.shape, out.dtype

N_MICROBATCH = 1
ADAM_LR = 0.001
ADAM_B1 = 0.9
ADAM_B2 = 0.999
ADAM_EPS = 1e-08
ADAM_WD = 0.01
ADAM_STEP = 10
PER_EXAMPLE_BATCH_AXIS = {'x': 0, 'c': 0, 'loss_target': 0}
SHARED_INPUTS = []
_WEIGHT_DTYPES = {'w_ada': _jnp.float32, 'b_ada': _jnp.float32, 'g_pre_ffn1': _jnp.float32, 'w_gate1': _jnp.float32, 'w_up1': _jnp.float32, 'w_down1': _jnp.float32, 'g_post_ffn1': _jnp.float32, 'g_pre_mix': _jnp.float32, 'w_in': _jnp.float32, 'b_in': _jnp.float32, 'sinks_a': _jnp.float32, 'rel_bias_b': _jnp.float32, 'g_grp_a': _jnp.float32, 'g_grp_b': _jnp.float32, 'w_out': _jnp.float32, 'b_out': _jnp.float32, 'g_post_mix': _jnp.float32, 'g_pre_ffn2': _jnp.float32, 'w_gate2': _jnp.float32, 'w_up2': _jnp.float32, 'w_down2': _jnp.float32, 'g_post_ffn2': _jnp.float32}
MOMENT_SCALE = {'w_ada': 5.001402e-01, 'b_ada': 8.852533e-01, 'g_pre_ffn1': 2.490668e-02, 'w_gate1': 1.126027e-02, 'w_up1': 1.223456e-02, 'w_down1': 2.031469e-02, 'g_post_ffn1': 1.590837e-01, 'g_pre_mix': 4.687456e-02, 'w_in': 1.405745e-01, 'b_in': 7.959352e-01, 'sinks_a': 1.030841e-01, 'rel_bias_b': 9.614861e-03, 'g_grp_a': 1.633687e-01, 'g_grp_b': 2.592013e-01, 'w_out': 2.244294e-01, 'b_out': 2.931301e-01, 'g_post_mix': 7.964124e-01, 'g_pre_ffn2': 2.453731e-02, 'w_gate2': 1.095700e-02, 'w_up2': 1.191653e-02, 'w_down2': 1.980372e-02, 'g_post_ffn2': 1.598118e-01}


def _to_microbatches(a, axis):
    t = _jnp.moveaxis(a, axis, 0)
    t = t.reshape((N_MICROBATCH, t.shape[0] // N_MICROBATCH) + t.shape[1:])
    return _jnp.moveaxis(t, 1, axis + 1)


def setup_inputs(seed: int = 0) -> dict:
    inp = _fwd_setup_inputs(seed)
    key = _jax.random.fold_in(_jax.random.key(seed), 7919)
    shape, _ = _output_shape()
    out = dict(inp)
    out["loss_target"] = _jax.random.normal(_jax.random.fold_in(key, 0), shape, _jnp.float32)
    for i, name in enumerate(TWIN_WEIGHTS):
        w = inp[name].astype(_jnp.float32)
        if MOMENT_SCALE is None:
            s = _jnp.sqrt(_jnp.mean(_jnp.square(w)) + 1e-30)
        else:
            s = MOMENT_SCALE[name]
        km, kv = _jax.random.split(_jax.random.fold_in(key, i + 1))
        out[name] = w
        out["m_" + name] = s * _jax.random.normal(km, w.shape, _jnp.float32)
        out["v_" + name] = (s * s) * _jax.random.uniform(kv, w.shape, _jnp.float32, 0.5, 1.5)
    if N_MICROBATCH > 1:
        for name, axis in PER_EXAMPLE_BATCH_AXIS.items():
            out[name] = _to_microbatches(out[name], axis)
    return {'x': out['x'], 'c': out['c'], 'w_ada': out['w_ada'], 'b_ada': out['b_ada'], 'g_pre_ffn1': out['g_pre_ffn1'], 'w_gate1': out['w_gate1'], 'w_up1': out['w_up1'], 'w_down1': out['w_down1'], 'g_post_ffn1': out['g_post_ffn1'], 'g_pre_mix': out['g_pre_mix'], 'w_in': out['w_in'], 'b_in': out['b_in'], 'sinks_a': out['sinks_a'], 'rel_bias_b': out['rel_bias_b'], 'g_grp_a': out['g_grp_a'], 'g_grp_b': out['g_grp_b'], 'w_out': out['w_out'], 'b_out': out['b_out'], 'g_post_mix': out['g_post_mix'], 'g_pre_ffn2': out['g_pre_ffn2'], 'w_gate2': out['w_gate2'], 'w_up2': out['w_up2'], 'w_down2': out['w_down2'], 'g_post_ffn2': out['g_post_ffn2'], 'loss_target': out['loss_target'], 'm_w_ada': out['m_w_ada'], 'm_b_ada': out['m_b_ada'], 'm_g_pre_ffn1': out['m_g_pre_ffn1'], 'm_w_gate1': out['m_w_gate1'], 'm_w_up1': out['m_w_up1'], 'm_w_down1': out['m_w_down1'], 'm_g_post_ffn1': out['m_g_post_ffn1'], 'm_g_pre_mix': out['m_g_pre_mix'], 'm_w_in': out['m_w_in'], 'm_b_in': out['m_b_in'], 'm_sinks_a': out['m_sinks_a'], 'm_rel_bias_b': out['m_rel_bias_b'], 'm_g_grp_a': out['m_g_grp_a'], 'm_g_grp_b': out['m_g_grp_b'], 'm_w_out': out['m_w_out'], 'm_b_out': out['m_b_out'], 'm_g_post_mix': out['m_g_post_mix'], 'm_g_pre_ffn2': out['m_g_pre_ffn2'], 'm_w_gate2': out['m_w_gate2'], 'm_w_up2': out['m_w_up2'], 'm_w_down2': out['m_w_down2'], 'm_g_post_ffn2': out['m_g_post_ffn2'], 'v_w_ada': out['v_w_ada'], 'v_b_ada': out['v_b_ada'], 'v_g_pre_ffn1': out['v_g_pre_ffn1'], 'v_w_gate1': out['v_w_gate1'], 'v_w_up1': out['v_w_up1'], 'v_w_down1': out['v_w_down1'], 'v_g_post_ffn1': out['v_g_post_ffn1'], 'v_g_pre_mix': out['v_g_pre_mix'], 'v_w_in': out['v_w_in'], 'v_b_in': out['v_b_in'], 'v_sinks_a': out['v_sinks_a'], 'v_rel_bias_b': out['v_rel_bias_b'], 'v_g_grp_a': out['v_g_grp_a'], 'v_g_grp_b': out['v_g_grp_b'], 'v_w_out': out['v_w_out'], 'v_b_out': out['v_b_out'], 'v_g_post_mix': out['v_g_post_mix'], 'v_g_pre_ffn2': out['v_g_pre_ffn2'], 'v_w_gate2': out['v_w_gate2'], 'v_w_up2': out['v_w_up2'], 'v_w_down2': out['v_w_down2'], 'v_g_post_ffn2': out['v_g_post_ffn2']}


def _loss(weights, diff, rest, loss_target):
    with _jax.named_scope("forward"):
        args = {**rest, TWIN_DIFF_INPUT: diff, **{k: w.astype(_WEIGHT_DTYPES[k]) for k, w in weights.items()}}
        y = _forward(args)
    with _jax.named_scope("loss_head"):
        err = _jnp.square(y.astype(_jnp.float32) - loss_target)
        return 0.5 * _jnp.sum(_jnp.mean(err, axis=-1)) if err.ndim else 0.5 * err


def _adamw(w, g, m, v):
    m = ADAM_B1 * m + (1.0 - ADAM_B1) * g
    v = ADAM_B2 * v + (1.0 - ADAM_B2) * _jnp.square(g)
    m_hat = m / (1.0 - ADAM_B1 ** ADAM_STEP)
    v_hat = v / (1.0 - ADAM_B2 ** ADAM_STEP)
    delta = -ADAM_LR * (m_hat / (_jnp.sqrt(v_hat) + ADAM_EPS) + ADAM_WD * w)
    return delta, m, v


def reference(x, c, w_ada, b_ada, g_pre_ffn1, w_gate1, w_up1, w_down1, g_post_ffn1, g_pre_mix, w_in, b_in, sinks_a, rel_bias_b, g_grp_a, g_grp_b, w_out, b_out, g_post_mix, g_pre_ffn2, w_gate2, w_up2, w_down2, g_post_ffn2, loss_target, m_w_ada, m_b_ada, m_g_pre_ffn1, m_w_gate1, m_w_up1, m_w_down1, m_g_post_ffn1, m_g_pre_mix, m_w_in, m_b_in, m_sinks_a, m_rel_bias_b, m_g_grp_a, m_g_grp_b, m_w_out, m_b_out, m_g_post_mix, m_g_pre_ffn2, m_w_gate2, m_w_up2, m_w_down2, m_g_post_ffn2, v_w_ada, v_b_ada, v_g_pre_ffn1, v_w_gate1, v_w_up1, v_w_down1, v_g_post_ffn1, v_g_pre_mix, v_w_in, v_b_in, v_sinks_a, v_rel_bias_b, v_g_grp_a, v_g_grp_b, v_w_out, v_b_out, v_g_post_mix, v_g_pre_ffn2, v_w_gate2, v_w_up2, v_w_down2, v_g_post_ffn2):
    given = dict(x=x, c=c, w_ada=w_ada, b_ada=b_ada, g_pre_ffn1=g_pre_ffn1, w_gate1=w_gate1, w_up1=w_up1, w_down1=w_down1, g_post_ffn1=g_post_ffn1, g_pre_mix=g_pre_mix, w_in=w_in, b_in=b_in, sinks_a=sinks_a, rel_bias_b=rel_bias_b, g_grp_a=g_grp_a, g_grp_b=g_grp_b, w_out=w_out, b_out=b_out, g_post_mix=g_post_mix, g_pre_ffn2=g_pre_ffn2, w_gate2=w_gate2, w_up2=w_up2, w_down2=w_down2, g_post_ffn2=g_post_ffn2, loss_target=loss_target, m_w_ada=m_w_ada, m_b_ada=m_b_ada, m_g_pre_ffn1=m_g_pre_ffn1, m_w_gate1=m_w_gate1, m_w_up1=m_w_up1, m_w_down1=m_w_down1, m_g_post_ffn1=m_g_post_ffn1, m_g_pre_mix=m_g_pre_mix, m_w_in=m_w_in, m_b_in=m_b_in, m_sinks_a=m_sinks_a, m_rel_bias_b=m_rel_bias_b, m_g_grp_a=m_g_grp_a, m_g_grp_b=m_g_grp_b, m_w_out=m_w_out, m_b_out=m_b_out, m_g_post_mix=m_g_post_mix, m_g_pre_ffn2=m_g_pre_ffn2, m_w_gate2=m_w_gate2, m_w_up2=m_w_up2, m_w_down2=m_w_down2, m_g_post_ffn2=m_g_post_ffn2, v_w_ada=v_w_ada, v_b_ada=v_b_ada, v_g_pre_ffn1=v_g_pre_ffn1, v_w_gate1=v_w_gate1, v_w_up1=v_w_up1, v_w_down1=v_w_down1, v_g_post_ffn1=v_g_post_ffn1, v_g_pre_mix=v_g_pre_mix, v_w_in=v_w_in, v_b_in=v_b_in, v_sinks_a=v_sinks_a, v_rel_bias_b=v_rel_bias_b, v_g_grp_a=v_g_grp_a, v_g_grp_b=v_g_grp_b, v_w_out=v_w_out, v_b_out=v_b_out, v_g_post_mix=v_g_post_mix, v_g_pre_ffn2=v_g_pre_ffn2, v_w_gate2=v_w_gate2, v_w_up2=v_w_up2, v_w_down2=v_w_down2, v_g_post_ffn2=v_g_post_ffn2)
    weights = {n: given[n] for n in TWIN_WEIGHTS}
    shared = {n: given[n] for n in SHARED_INPUTS}
    per_example = {n: given[n] for n in ['x', 'c']}
    grad_fn = _jax.value_and_grad(_loss, argnums=(0, 1))

    def one_microbatch(ex, loss_target):
        ex = dict(ex)
        diff = ex.pop(TWIN_DIFF_INPUT)
        return grad_fn(weights, diff, {**shared, **ex}, loss_target)

    if N_MICROBATCH == 1:
        loss, (grad_w, grad_x) = one_microbatch(per_example, given["loss_target"])
    else:
        def body(carry, xs):
            loss_sum, grad_sum = carry
            l_k, (gw_k, gx_k) = one_microbatch(xs[0], xs[1])
            with _jax.named_scope("update"):
                return (loss_sum + l_k, _jax.tree.map(_jnp.add, grad_sum, gw_k)), gx_k

        init = (_jnp.zeros((), _jnp.float32), _jax.tree.map(_jnp.zeros_like, weights))
        (loss, grad_w), grad_x = _jax.lax.scan(body, init, (per_example, given["loss_target"]))
    with _jax.named_scope("update"):
        delta_w, new_m, new_v = {}, {}, {}
        for n in TWIN_WEIGHTS:
            delta_w[n], new_m[n], new_v[n] = _adamw(weights[n], grad_w[n], given["m_" + n], given["v_" + n])
    return (loss, grad_x, *[grad_w[n] for n in TWIN_WEIGHTS], *[delta_w[n] for n in TWIN_WEIGHTS],
            *[new_m[n] for n in TWIN_WEIGHTS], *[new_v[n] for n in TWIN_WEIGHTS])
```

```python
import numpy as np
import jax
import jax.numpy as jnp
from jax import lax
from jax.experimental import pallas as pl
from jax.experimental.pallas import tpu as pltpu

F32 = jnp.float32
BF16 = jnp.bfloat16
MESH = pl.DeviceIdType.MESH
ANY = pl.BlockSpec(memory_space=pl.ANY)
VMEM_SPEC = pl.BlockSpec(memory_space=pltpu.VMEM)
SMEM_SPEC = pl.BlockSpec(memory_space=pltpu.SMEM)

N_DEV = 8
CHUNK = 64
HEAD_DIM = 64
LANES = 128
H_A, KV_A, H_B = 8, 2, 8
BACK_A, BACK_B = 2, 8
REL_CLIP = 128
N_REL = 2 * REL_CLIP + 1
QA, KVA, QB = H_A * HEAD_DIM, KV_A * HEAD_DIM, H_B * HEAD_DIM
D_IN = QA + 2 * KVA + 3 * QB
N_MOD = 9
EPS = 1e-6
NEG_INF = -1e30
QG = 4
QROWS = QG * CHUNK
SKEW = 1024
ADAM_LR, ADAM_B1, ADAM_B2, ADAM_EPS, ADAM_WD, ADAM_STEP = 0.001, 0.9, 0.999, 1e-08, 0.01, 10
VMEM_LIMIT = 56 * 2 ** 20


def _pick(n, cands):
    for c in cands:
        if n % c == 0:
            return c
    return n


def _params(sem=None):
    return pltpu.CompilerParams(dimension_semantics=sem, vmem_limit_bytes=VMEM_LIMIT)


def _dot_nt(a, b):
    return lax.dot_general(a, b, (((1,), (1,)), ((), ())), preferred_element_type=F32)


def _dot_tn(a, b):
    return lax.dot_general(a, b, (((0,), (0,)), ((), ())), preferred_element_type=F32)


def _dot(a, b):
    return jnp.dot(a, b, preferred_element_type=F32)


def _sigmoid(a):
    return 1.0 / (1.0 + jnp.exp(-a))


def _mesh_pos():
    return lax.axis_index("x"), lax.axis_index("y"), lax.axis_index("c")


def _peer(x, y, c, r):
    px = 1 - x if r & 4 else x
    py = 1 - y if r & 2 else y
    pc = 1 - c if r & 1 else c
    return px, py, pc


def _ada_forward(c_row, w_ada, b_cols):
    d = c_row.shape[1]
    wcols = w_ada.shape[1]

    def body(c_ref, w_ref, b_ref, sc_ref, mod_ref, rows_ref, send_sems, recv_sems):
        x, y, c = _mesh_pos()
        me = 4 * x + 2 * y + c
        cv = c_ref[...]
        sc_ref[me] = cv * _sigmoid(cv)

        sends = []
        for r in range(1, N_DEV):
            px, py, pc = _peer(x, y, c, r)
            cp = pltpu.make_async_remote_copy(
                src_ref=sc_ref.at[me], dst_ref=sc_ref.at[me], send_sem=send_sems.at[0, r - 1],
                recv_sem=recv_sems.at[0, r - 1], device_id=(px, py, pc), device_id_type=MESH)
            cp.start()
            sends.append(cp)
        for r in range(1, N_DEV):
            px, py, pc = _peer(x, y, c, r)
            pid = 4 * px + 2 * py + pc
            pltpu.make_async_remote_copy(
                src_ref=sc_ref.at[pid], dst_ref=sc_ref.at[pid], send_sem=send_sems.at[0, r - 1],
                recv_sem=recv_sems.at[0, r - 1], device_id=(px, py, pc), device_id_type=MESH).wait_recv()
        for cp in sends:
            cp.wait_send()

        sc_all = jnp.concatenate([sc_ref[j] for j in range(N_DEV)], axis=0)
        rows = _dot(sc_all.astype(BF16), w_ref[...].astype(BF16)) + b_ref[...]
        for j in range(N_DEV):
            rows_ref[j] = rows[j:j + 1, :]
        mod_ref[me] = rows_ref[me]

        sends = []
        for r in range(1, N_DEV):
            px, py, pc = _peer(x, y, c, r)
            pid = 4 * px + 2 * py + pc
            cp = pltpu.make_async_remote_copy(
                src_ref=rows_ref.at[pid], dst_ref=mod_ref.at[me], send_sem=send_sems.at[1, r - 1],
                recv_sem=recv_sems.at[1, r - 1], device_id=(px, py, pc), device_id_type=MESH)
            cp.start()
            sends.append(cp)
        for r in range(1, N_DEV):
            px, py, pc = _peer(x, y, c, r)
            pid = 4 * px + 2 * py + pc
            pltpu.make_async_remote_copy(
                src_ref=rows_ref.at[pid], dst_ref=mod_ref.at[pid], send_sem=send_sems.at[1, r - 1],
                recv_sem=recv_sems.at[1, r - 1], device_id=(px, py, pc), device_id_type=MESH).wait_recv()
        for cp in sends:
            cp.wait_send()

    return pl.pallas_call(
        body, name="ada_forward",
        out_shape=(jax.ShapeDtypeStruct((N_DEV, 1, d), F32), jax.ShapeDtypeStruct((N_DEV, 1, wcols), F32)),
        in_specs=[VMEM_SPEC, VMEM_SPEC, VMEM_SPEC], out_specs=(VMEM_SPEC, VMEM_SPEC),
        scratch_shapes=[pltpu.VMEM((N_DEV, 1, wcols), F32), pltpu.SemaphoreType.DMA((2, N_DEV - 1)),
                        pltpu.SemaphoreType.DMA((2, N_DEV - 1))],
        compiler_params=_params(),
    )(c_row, w_ada, b_cols)


def _all_gather_weights(shards):
    n_w = len(shards)
    rows = [s.shape[0] for s in shards]

    def body(*refs):
        ins, outs = refs[:n_w], refs[n_w:2 * n_w]
        send_sems, recv_sems, local_sems = refs[2 * n_w:]
        x, y, c = _mesh_pos()
        me, sibling = (x, y, c), (x, y, 1 - c)
        chips = [(1 - x, y), (x, 1 - y), (1 - x, 1 - y)]

        def block(w, dev):
            start = pl.multiple_of((4 * dev[0] + 2 * dev[1] + dev[2]) * rows[w], 16)
            return outs[w].at[pl.ds(start, rows[w]), :]

        def copy(w, k, dev, to, src=None):
            return pltpu.make_async_remote_copy(
                src_ref=block(w, dev) if src is None else src, dst_ref=block(w, dev),
                send_sem=send_sems.at[w, k], recv_sem=recv_sems.at[w, k], device_id=to, device_id_type=MESH)

        mine = [pltpu.make_async_copy(ins[w], block(w, me), local_sems.at[w]) for w in range(n_w)]
        for cp in mine:
            cp.start()
        first = []
        for j, chip in enumerate(chips):
            first += [copy(w, 1 + j, me, (*chip, c), src=ins[w]) for w in range(n_w)]
        first += [copy(w, 0, me, sibling, src=ins[w]) for w in range(n_w)]
        for cp in first:
            cp.start()
        passed = []
        for j, chip in enumerate(chips):
            for w in range(n_w):
                copy(w, 1 + j, (*chip, c), me).wait_recv()
                cp = copy(w, 4 + j, (*chip, c), sibling)
                cp.start()
                passed.append(cp)
        for w in range(n_w):
            copy(w, 0, sibling, me).wait_recv()
        for j, chip in enumerate(chips):
            for w in range(n_w):
                copy(w, 4 + j, (*chip, 1 - c), me).wait_recv()
        for cp in first + passed:
            cp.wait_send()
        for cp in mine:
            cp.wait()

    return pl.pallas_call(
        body, name="all_gather_weights",
        out_shape=tuple(jax.ShapeDtypeStruct((N_DEV * s.shape[0], s.shape[1]), s.dtype) for s in shards),
        in_specs=[ANY] * n_w, out_specs=tuple([ANY] * n_w),
        scratch_shapes=[pltpu.SemaphoreType.DMA((n_w, N_DEV - 1)), pltpu.SemaphoreType.DMA((n_w, N_DEV - 1)),
                        pltpu.SemaphoreType.DMA((n_w,))],
        compiler_params=_params(),
    )(*shards)


def _reduce_scatter_send(grads):
    n_w = len(grads)
    rows = [g.shape[0] // N_DEV for g in grads]
    offs = [sum(rows[:w]) for w in range(n_w)]
    total, kdim = sum(rows), grads[0].shape[1]

    def body(*refs):
        ins, out = refs[:n_w], refs[n_w]
        send_sems, recv_sems, local_sems = refs[n_w + 1:]
        x, y, c = _mesh_pos()
        me = 4 * x + 2 * y + c

        def src(w, dev_index):
            return ins[w].at[pl.ds(pl.multiple_of(dev_index * rows[w], 16), rows[w]), :]

        def dst(w, r):
            return out.at[r, pl.ds(offs[w], rows[w]), :]

        mine = [pltpu.make_async_copy(src(w, me), dst(w, 0), local_sems.at[w]) for w in range(n_w)]
        for cp in mine:
            cp.start()
        copies = []
        for r in (6, 7, 4, 5, 2, 3, 1):
            px, py, pc = _peer(x, y, c, r)
            for w in range(n_w):
                cp = pltpu.make_async_remote_copy(
                    src_ref=src(w, 4 * px + 2 * py + pc), dst_ref=dst(w, r), send_sem=send_sems.at[w, r - 1],
                    recv_sem=recv_sems.at[w, r - 1], device_id=(px, py, pc), device_id_type=MESH)
                cp.start()
                copies.append(cp)
        for cp in copies:
            cp.wait_recv()
        for cp in copies:
            cp.wait_send()
        for cp in mine:
            cp.wait()

    return pl.pallas_call(
        body, name="reduce_scatter_send",
        out_shape=jax.ShapeDtypeStruct((N_DEV, total, kdim), grads[0].dtype),
        in_specs=[ANY] * n_w, out_specs=ANY,
        scratch_shapes=[pltpu.SemaphoreType.DMA((n_w, N_DEV - 1)), pltpu.SemaphoreType.DMA((n_w, N_DEV - 1)),
                        pltpu.SemaphoreType.DMA((n_w,))],
        compiler_params=_params(),
    )(*grads)


def _all_gather_small(v):
    n = v.shape[1]

    def body(v_ref, out_ref, send_sems, recv_sems):
        x, y, c = _mesh_pos()
        me = 4 * x + 2 * y + c
        out_ref[me] = v_ref[...]
        sends = []
        for r in range(1, N_DEV):
            px, py, pc = _peer(x, y, c, r)
            cp = pltpu.make_async_remote_copy(
                src_ref=v_ref, dst_ref=out_ref.at[me], send_sem=send_sems.at[r - 1],
                recv_sem=recv_sems.at[r - 1], device_id=(px, py, pc), device_id_type=MESH)
            cp.start()
            sends.append(cp)
        for r in range(1, N_DEV):
            px, py, pc = _peer(x, y, c, r)
            pid = 4 * px + 2 * py + pc
            pltpu.make_async_remote_copy(
                src_ref=v_ref, dst_ref=out_ref.at[pid], send_sem=send_sems.at[r - 1],
                recv_sem=recv_sems.at[r - 1], device_id=(px, py, pc), device_id_type=MESH).wait_recv()
        for cp in sends:
            cp.wait_send()

    return pl.pallas_call(
        body, name="all_gather_small",
        out_shape=jax.ShapeDtypeStruct((N_DEV, 1, n), F32),
        in_specs=[VMEM_SPEC], out_specs=VMEM_SPEC,
        scratch_shapes=[pltpu.SemaphoreType.DMA((N_DEV - 1,)), pltpu.SemaphoreType.DMA((N_DEV - 1,))],
        compiler_params=_params(),
    )(v)


def _mm_nt(a, b, name, out_dtype, bias=None):
    m, k = a.shape
    n = b.shape[0]
    tm = _pick(m, (512, 256, 128))
    tn = _pick(n, (1408, 1152, 1024, 768, 512, 256, 128))

    def body(*refs):
        acc = _dot_nt(refs[0][...], refs[1][...])
        if bias is not None:
            acc = acc + refs[2][...]
        refs[-1][...] = acc.astype(out_dtype)

    in_specs = [pl.BlockSpec((tm, k), lambda i, j: (i, 0)), pl.BlockSpec((tn, k), lambda i, j: (j, 0))]
    args = [a, b]
    if bias is not None:
        in_specs.append(pl.BlockSpec((1, tn), lambda i, j: (0, j)))
        args.append(bias)
    return pl.pallas_call(
        body, name=name, grid=(m // tm, n // tn), in_specs=in_specs,
        out_specs=pl.BlockSpec((tm, tn), lambda i, j: (i, j)),
        out_shape=jax.ShapeDtypeStruct((m, n), out_dtype),
        compiler_params=_params(("parallel", "parallel")),
    )(*args)


def _mm_nn(pairs, name, out_dtype, bias=None):
    m, k = pairs[0][0].shape
    n = pairs[0][1].shape[1]
    tm = _pick(m, (512, 256, 128))
    tk = _pick(k, (1408, 1152, 1024, 768, 512, 256, 128))
    nk = k // tk
    n_p = len(pairs)

    def body(*refs):
        o_ref, acc_ref = refs[-2], refs[-1]
        kk = pl.program_id(1)

        @pl.when(kk == 0)
        def _():
            acc_ref[...] = jnp.zeros_like(acc_ref)

        for p in range(n_p):
            acc_ref[...] += _dot(refs[2 * p][...], refs[2 * p + 1][...])

        @pl.when(kk == nk - 1)
        def _():
            acc = acc_ref[...]
            if bias is not None:
                acc = acc + refs[2 * n_p][...]
            o_ref[...] = acc.astype(out_dtype)

    in_specs, args = [], []
    for a, b in pairs:
        in_specs += [pl.BlockSpec((tm, tk), lambda i, kk: (i, kk)), pl.BlockSpec((tk, n), lambda i, kk: (kk, 0))]
        args += [a, b]
    if bias is not None:
        in_specs.append(pl.BlockSpec((1, n), lambda i, kk: (0, 0)))
        args.append(bias)
    return pl.pallas_call(
        body, name=name, grid=(m // tm, nk), in_specs=in_specs,
        out_specs=pl.BlockSpec((tm, n), lambda i, kk: (i, 0)),
        out_shape=jax.ShapeDtypeStruct((m, n), out_dtype),
        scratch_shapes=[pltpu.VMEM((tm, n), F32)],
        compiler_params=_params(("parallel", "arbitrary")),
    )(*args)


def _mm_tn(a, b, name, out_dtype=BF16):
    k, m = a.shape
    n = b.shape[1]
    tm = _pick(m, (1408, 1152, 1024, 768, 512, 256, 128))
    tk = _pick(k, (512, 256, 128))
    nk = k // tk

    def body(a_ref, b_ref, o_ref, acc_ref):
        kk = pl.program_id(1)

        @pl.when(kk == 0)
        def _():
            acc_ref[...] = jnp.zeros_like(acc_ref)

        acc_ref[...] += _dot_tn(a_ref[...], b_ref[...])

        @pl.when(kk == nk - 1)
        def _():
            o_ref[...] = acc_ref[...].astype(out_dtype)

    return pl.pallas_call(
        body, name=name, grid=(m // tm, nk),
        in_specs=[pl.BlockSpec((tk, tm), lambda i, kk: (kk, i)), pl.BlockSpec((tk, n), lambda i, kk: (kk, 0))],
        out_specs=pl.BlockSpec((tm, n), lambda i, kk: (i, 0)),
        out_shape=jax.ShapeDtypeStruct((m, n), out_dtype),
        scratch_shapes=[pltpu.VMEM((tm, n), F32)],
        compiler_params=_params(("parallel", "arbitrary")),
    )(a, b)


def _ffn_up(h, wg_t, wu_t, name):
    s, d = h.shape
    f = wg_t.shape[0]
    tm = _pick(s, (512, 256, 128))
    tf = _pick(f, (1408, 1024, 512, 256, 128))

    def body(h_ref, wg_ref, wu_ref, a_ref, b_ref, u_ref):
        hh = h_ref[...]
        a = _dot_nt(hh, wg_ref[...])
        b = _dot_nt(hh, wu_ref[...])
        a_ref[...] = a.astype(BF16)
        b_ref[...] = b.astype(BF16)
        u_ref[...] = ((a * _sigmoid(a)) * b).astype(BF16)

    w_spec = pl.BlockSpec((tf, d), lambda i, j: (j, 0))
    o_spec = pl.BlockSpec((tm, tf), lambda i, j: (i, j))
    o_shape = jax.ShapeDtypeStruct((s, f), BF16)
    return pl.pallas_call(
        body, name=name, grid=(s // tm, f // tf),
        in_specs=[pl.BlockSpec((tm, d), lambda i, j: (i, 0)), w_spec, w_spec],
        out_specs=(o_spec, o_spec, o_spec), out_shape=(o_shape, o_shape, o_shape),
        compiler_params=_params(("parallel", "parallel")),
    )(h, wg_t, wu_t)


def _ffn_down_bwd(dy, wd, a, b, name):
    s, d = dy.shape
    f = wd.shape[0]
    tm = _pick(s, (512, 256, 128))
    tf = _pick(f, (1408, 1024, 512, 256, 128))

    def body(dy_ref, wd_ref, a_ref, b_ref, da_ref, db_ref):
        du = _dot_nt(dy_ref[...], wd_ref[...])
        a = a_ref[...].astype(F32)
        b = b_ref[...].astype(F32)
        sig = _sigmoid(a)
        da_ref[...] = (du * b * (sig * (1.0 + a * (1.0 - sig)))).astype(BF16)
        db_ref[...] = (du * (a * sig)).astype(BF16)

    t_spec = pl.BlockSpec((tm, tf), lambda i, j: (i, j))
    o_shape = jax.ShapeDtypeStruct((s, f), BF16)
    return pl.pallas_call(
        body, name=name, grid=(s // tm, f // tf),
        in_specs=[pl.BlockSpec((tm, d), lambda i, j: (i, 0)), pl.BlockSpec((tf, d), lambda i, j: (j, 0)),
                  t_spec, t_spec],
        out_specs=(t_spec, t_spec), out_shape=(o_shape, o_shape),
        compiler_params=_params(("parallel", "parallel")),
    )(dy, wd, a, b)


def _row_tile(s):
    return _pick(s, (256, 128, 64))


def _vec_spec(d):
    return pl.BlockSpec((1, d), lambda i: (0, 0))


def _pre_norm(x, g, scale, shift, name):
    s, d = x.shape
    ts = _row_tile(s)

    def body(x_ref, g_ref, sc_ref, sh_ref, h_ref):
        xv = x_ref[...]
        r = lax.rsqrt(jnp.mean(xv * xv, axis=-1, keepdims=True) + EPS)
        h_ref[...] = (((xv * r) * g_ref[...]) * (1.0 + sc_ref[...]) + sh_ref[...]).astype(BF16)

    row = pl.BlockSpec((ts, d), lambda i: (i, 0))
    return pl.pallas_call(
        body, name=name, grid=(s // ts,), in_specs=[row, _vec_spec(d), _vec_spec(d), _vec_spec(d)],
        out_specs=row, out_shape=jax.ShapeDtypeStruct((s, d), BF16), compiler_params=_params(("parallel",)),
    )(x, g, scale, shift)


def _post_norm_residual(x, y, g, gate, weight, name):
    s, d = x.shape
    ts = _row_tile(s)

    def body(x_ref, y_ref, g_ref, gate_ref, o_ref):
        yv = y_ref[...]
        r = lax.rsqrt(jnp.mean(yv * yv, axis=-1, keepdims=True) + EPS)
        o_ref[...] = x_ref[...] + (weight * gate_ref[...]) * ((yv * r) * g_ref[...])

    row = pl.BlockSpec((ts, d), lambda i: (i, 0))
    return pl.pallas_call(
        body, name=name, grid=(s // ts,), in_specs=[row, row, _vec_spec(d), _vec_spec(d)],
        out_specs=row, out_shape=jax.ShapeDtypeStruct((s, d), F32), compiler_params=_params(("parallel",)),
    )(x, y, g, gate)


def _post_norm_bwd(dout, y, g, gate, weight, name):
    s, d = y.shape
    ts = _row_tile(s)

    def body(do_ref, y_ref, g_ref, gate_ref, dy_ref, s1_ref, cs_ref):
        @pl.when(pl.program_id(0) == 0)
        def _():
            s1_ref[...] = jnp.zeros_like(s1_ref)
            cs_ref[...] = jnp.zeros_like(cs_ref)

        yv = y_ref[...]
        do = do_ref[...]
        r = lax.rsqrt(jnp.mean(yv * yv, axis=-1, keepdims=True) + EPS)
        yn = yv * r
        dyn = do * ((weight * gate_ref[...]) * g_ref[...])
        dy = r * (dyn - yn * jnp.mean(dyn * yn, axis=-1, keepdims=True))
        dy_ref[...] = dy.astype(BF16)
        s1_ref[...] += jnp.sum(do * yn, axis=0, keepdims=True)
        cs_ref[...] += jnp.sum(dy, axis=0, keepdims=True)

    row = pl.BlockSpec((ts, d), lambda i: (i, 0))
    vec = jax.ShapeDtypeStruct((1, d), F32)
    return pl.pallas_call(
        body, name=name, grid=(s // ts,), in_specs=[row, row, _vec_spec(d), _vec_spec(d)],
        out_specs=(row, _vec_spec(d), _vec_spec(d)),
        out_shape=(jax.ShapeDtypeStruct((s, d), BF16), vec, vec), compiler_params=_params(("arbitrary",)),
    )(dout, y, g, gate)


def _pre_norm_bwd(dh, x, g, scale, dres, name):
    s, d = x.shape
    ts = _row_tile(s)

    def body(dh_ref, x_ref, g_ref, sc_ref, dr_ref, dx_ref, s2_ref, s3_ref):
        @pl.when(pl.program_id(0) == 0)
        def _():
            s2_ref[...] = jnp.zeros_like(s2_ref)
            s3_ref[...] = jnp.zeros_like(s3_ref)

        xv = x_ref[...]
        dh = dh_ref[...]
        r = lax.rsqrt(jnp.mean(xv * xv, axis=-1, keepdims=True) + EPS)
        n = xv * r
        dn = dh * (g_ref[...] * (1.0 + sc_ref[...]))
        dx_ref[...] = dr_ref[...] + r * (dn - n * jnp.mean(dn * n, axis=-1, keepdims=True))
        s2_ref[...] += jnp.sum(dh * n, axis=0, keepdims=True)
        s3_ref[...] += jnp.sum(dh, axis=0, keepdims=True)

    row = pl.BlockSpec((ts, d), lambda i: (i, 0))
    vec = jax.ShapeDtypeStruct((1, d), F32)
    return pl.pallas_call(
        body, name=name, grid=(s // ts,), in_specs=[row, row, _vec_spec(d), _vec_spec(d), row],
        out_specs=(row, _vec_spec(d), _vec_spec(d)),
        out_shape=(jax.ShapeDtypeStruct((s, d), F32), vec, vec), compiler_params=_params(("arbitrary",)),
    )(dh, x, g, scale, dres)


def _group_norm_cat(oa, ob, ga, gb):
    s = oa.shape[0]
    ts = _row_tile(s)

    def body(oa_ref, ob_ref, ga_ref, gb_ref, y_ref):
        for o_ref, g_ref, lo, w in ((oa_ref, ga_ref, 0, QA), (ob_ref, gb_ref, QA, QB)):
            ov = o_ref[...]
            r = lax.rsqrt(jnp.mean(ov * ov, axis=-1, keepdims=True) + EPS)
            y_ref[:, lo:lo + w] = ((ov * r) * g_ref[...]).astype(BF16)

    return pl.pallas_call(
        body, name="group_norm_cat", grid=(s // ts,),
        in_specs=[pl.BlockSpec((ts, QA), lambda i: (i, 0)), pl.BlockSpec((ts, QB), lambda i: (i, 0)),
                  _vec_spec(QA), _vec_spec(QB)],
        out_specs=pl.BlockSpec((ts, QA + QB), lambda i: (i, 0)),
        out_shape=jax.ShapeDtypeStruct((s, QA + QB), BF16), compiler_params=_params(("parallel",)),
    )(oa, ob, ga, gb)


def _group_norm_bwd(dy, oa, ob, ga, gb):
    s = oa.shape[0]
    ts = _row_tile(s)

    def body(dy_ref, oa_ref, ob_ref, ga_ref, gb_ref, doa_ref, dob_ref, dga_ref, dgb_ref):
        @pl.when(pl.program_id(0) == 0)
        def _():
            dga_ref[...] = jnp.zeros_like(dga_ref)
            dgb_ref[...] = jnp.zeros_like(dgb_ref)

        for o_ref, g_ref, do_ref, dg_ref, lo, w in ((oa_ref, ga_ref, doa_ref, dga_ref, 0, QA),
                                                    (ob_ref, gb_ref, dob_ref, dgb_ref, QA, QB)):
            ov = o_ref[...]
            dyv = dy_ref[:, lo:lo + w]
            r = lax.rsqrt(jnp.mean(ov * ov, axis=-1, keepdims=True) + EPS)
            n = ov * r
            dn = dyv * g_ref[...]
            do_ref[...] = r * (dn - n * jnp.mean(dn * n, axis=-1, keepdims=True))
            dg_ref[...] += jnp.sum(dyv * n, axis=0, keepdims=True)

    ra = pl.BlockSpec((ts, QA), lambda i: (i, 0))
    rb = pl.BlockSpec((ts, QB), lambda i: (i, 0))
    return pl.pallas_call(
        body, name="group_norm_bwd", grid=(s // ts,),
        in_specs=[pl.BlockSpec((ts, QA + QB), lambda i: (i, 0)), ra, rb, _vec_spec(QA), _vec_spec(QB)],
        out_specs=(ra, rb, _vec_spec(QA), _vec_spec(QB)),
        out_shape=(jax.ShapeDtypeStruct((s, QA), F32), jax.ShapeDtypeStruct((s, QB), F32),
                   jax.ShapeDtypeStruct((1, QA), F32), jax.ShapeDtypeStruct((1, QB), F32)),
        compiler_params=_params(("arbitrary",)),
    )(dy, oa, ob, ga, gb)


def _loss_and_grad(y, target):
    s, d = y.shape
    ts = _row_tile(s)

    def body(y_ref, t_ref, l_ref, g_ref):
        @pl.when(pl.program_id(0) == 0)
        def _():
            l_ref[...] = jnp.zeros_like(l_ref)

        err = y_ref[...] - t_ref[...]
        g_ref[...] = err * (1.0 / d)
        row = jnp.mean(err * err, axis=-1, keepdims=True)
        l_ref[...] += 0.5 * jnp.sum(row, axis=0, keepdims=True)

    row = pl.BlockSpec((ts, d), lambda i: (i, 0))
    return pl.pallas_call(
        body, name="loss_and_grad", grid=(s // ts,), in_specs=[row, row],
        out_specs=(pl.BlockSpec((1, 1), lambda i: (0, 0)), row),
        out_shape=(jax.ShapeDtypeStruct((1, 1), F32), jax.ShapeDtypeStruct((s, d), F32)),
        compiler_params=_params(("arbitrary",)),
    )(y, target)


def _col_sum(x, name):
    s, n = x.shape
    ts = _row_tile(s)

    def body(x_ref, o_ref):
        @pl.when(pl.program_id(0) == 0)
        def _():
            o_ref[...] = jnp.zeros_like(o_ref)

        o_ref[...] += jnp.sum(x_ref[...].astype(F32), axis=0, keepdims=True)

    return pl.pallas_call(
        body, name=name, grid=(s // ts,), in_specs=[pl.BlockSpec((ts, n), lambda i: (i, 0))],
        out_specs=pl.BlockSpec((1, n), lambda i: (0, 0)), out_shape=jax.ShapeDtypeStruct((1, n), F32),
        compiler_params=_params(("arbitrary",)),
    )(x)


def _alibi_bias():
    i = np.arange(QROWS)[:, None]
    j = np.arange((QG + BACK_A) * CHUNK)[None, :]
    dist = np.abs(BACK_A * CHUNK + i - j).astype(np.float32)
    dc = j // CHUNK - i // CHUNK
    valid = (dc >= 0) & (dc <= BACK_A)
    slopes = np.array([2.0 ** (-8.0 * (h + 1) / H_A) for h in range(H_A)], dtype=np.float32)
    bias = -slopes[:, None, None] * dist[None]
    return jnp.asarray(np.where(valid[None], bias, np.float32(NEG_INF)).astype(np.float32))


def _rel_index_matrix():
    cc = np.arange(SKEW)
    dist = np.where(cc < SKEW - QROWS, BACK_B * CHUNK - cc, BACK_B * CHUNK + SKEW - cc)
    idx = np.clip(dist, -REL_CLIP, REL_CLIP) + REL_CLIP
    m = np.zeros((SKEW, N_REL), np.float32)
    m[cc, idx] = 1.0
    return jnp.asarray(m)


def _toeplitz_bias(vec):
    lk = (QG + BACK_B) * CHUNK

    def body(v_ref, o_ref):
        xv = jnp.broadcast_to(v_ref[0], (QROWS, SKEW))
        row = lax.broadcasted_iota(jnp.int32, (QROWS, SKEW), 0)
        for bit in range(QROWS.bit_length() - 1):
            xv = jnp.where((row >> bit) & 1 == 1, pltpu.roll(xv, 1 << bit, 1), xv)
        ri = lax.broadcasted_iota(jnp.int32, (QROWS, lk), 0) // CHUNK
        ci = lax.broadcasted_iota(jnp.int32, (QROWS, lk), 1) // CHUNK
        valid = (ci - ri >= 0) & (ci - ri <= BACK_B)
        o_ref[0] = jnp.where(valid, xv[:, :lk], NEG_INF)

    return pl.pallas_call(
        body, name="toeplitz_bias", grid=(H_B,),
        in_specs=[pl.BlockSpec((1, 1, SKEW), lambda h: (h, 0, 0))],
        out_specs=pl.BlockSpec((1, QROWS, lk), lambda h: (h, 0, 0)),
        out_shape=jax.ShapeDtypeStruct((H_B, QROWS, lk), F32), compiler_params=_params(("parallel",)),
    )(vec)


def _diagonal_sums(dbias):
    lk = dbias.shape[2]

    def body(d_ref, o_ref):
        xv = jnp.concatenate([d_ref[0], jnp.zeros((QROWS, SKEW - lk), F32)], axis=1)
        row = lax.broadcasted_iota(jnp.int32, (QROWS, SKEW), 0)
        for bit in range(QROWS.bit_length() - 1):
            xv = jnp.where((row >> bit) & 1 == 1, pltpu.roll(xv, SKEW - (1 << bit), 1), xv)
        o_ref[0] = jnp.sum(xv, axis=0, keepdims=True)

    return pl.pallas_call(
        body, name="diagonal_sums", grid=(H_B,),
        in_specs=[pl.BlockSpec((1, QROWS, lk), lambda h: (h, 0, 0))],
        out_specs=pl.BlockSpec((1, 1, SKEW), lambda h: (h, 0, 0)),
        out_shape=jax.ShapeDtypeStruct((H_B, 1, SKEW), F32), compiler_params=_params(("parallel",)),
    )(dbias)


def _attn_common(s, n_back, gqa, q_col, k_col, v_col):
    lk = (QG + n_back) * CHUNK
    pad = n_back * CHUNK
    q_spec = pl.BlockSpec((QROWS, LANES), lambda t, g: (g, q_col + t))
    if gqa:
        k_spec = pl.BlockSpec((s, LANES), lambda t, g: (0, k_col))
        v_spec = pl.BlockSpec((s, LANES), lambda t, g: (0, v_col))
    else:
        k_spec = pl.BlockSpec((s, LANES), lambda t, g: (0, k_col + t))
        v_spec = pl.BlockSpec((s, LANES), lambda t, g: (0, v_col + t))
    bias_spec = pl.BlockSpec((2, QROWS, lk), lambda t, g: (t, 0, 0))
    tile_spec = pl.BlockSpec((QROWS, LANES), lambda t, g: (g, t))
    return lk, pad, q_spec, k_spec, v_spec, bias_spec, tile_spec


def _attention_fwd(proj, bias, sinks, *, n_back, gqa, q_col, k_col, v_col, name):
    s = proj.shape[0]
    lk, pad, q_spec, k_spec, v_spec, bias_spec, tile_spec = _attn_common(s, n_back, gqa, q_col, k_col, v_col)
    n_t, n_g = 512 // LANES, s // QROWS

    def body(*refs):
        if gqa:
            q_ref, k_ref, v_ref, bias_ref, sink_ref, o_ref, l_ref, kpad, vpad = refs
        else:
            q_ref, k_ref, v_ref, bias_ref, o_ref, l_ref, kpad, vpad = refs
        t, g = pl.program_id(0), pl.program_id(1)

        @pl.when(g == 0)
        def _():
            kpad[0:pad, :] = jnp.zeros((pad, LANES), BF16)
            vpad[0:pad, :] = jnp.zeros((pad, LANES), BF16)
            kpad[pad:, :] = k_ref[...]
            vpad[pad:, :] = v_ref[...]

        start = pl.multiple_of(g * QROWS, QROWS)
        kb = kpad[pl.ds(start, lk), :]
        vb = vpad[pl.ds(start, lk), :]
        half = lax.broadcasted_iota(jnp.int32, (QROWS, LANES), 1) // HEAD_DIM
        col_ok = lax.broadcasted_iota(jnp.int32, (QROWS, lk), 1) >= (n_back - QG * g) * CHUNK
        q = q_ref[...]
        if gqa:
            hk = t // 2
            q_rolled = pltpu.roll(q.astype(F32), HEAD_DIM, 1).astype(BF16)
        outs, lses = [], []
        for e in range(2):
            if gqa:
                kv_half = hk
                src = jnp.where(hk == e, q, q_rolled)
            else:
                kv_half = e
                src = q
            qm = jnp.where(half == kv_half, src, jnp.zeros_like(src))
            sc = _dot_nt(qm, kb) * (HEAD_DIM ** -0.5) + bias_ref[e]
            sc = jnp.where(col_ok, sc, NEG_INF)
            m = jnp.max(sc, axis=-1, keepdims=True)
            if gqa:
                sk = sink_ref[2 * t + e]
                m = jnp.maximum(m, sk)
            p = jnp.exp(sc - m)
            l = jnp.sum(p, axis=-1, keepdims=True)
            if gqa:
                l = l + jnp.exp(sk - m)
            pn = p / l
            outs.append(_dot(pn.astype(BF16), vb))
            lses.append(m + jnp.log(l))
        if gqa:
            same = jnp.where(hk == 0, outs[0], outs[1])
            other = jnp.where(hk == 0, outs[1], outs[0])
            o_ref[...] = jnp.where(half == hk, same, pltpu.roll(other, HEAD_DIM, 1))
        else:
            o_ref[...] = jnp.where(half == 0, outs[0], outs[1])
        l_ref[...] = jnp.where(half == 0, lses[0], lses[1])

    in_specs = [q_spec, k_spec, v_spec, bias_spec] + ([SMEM_SPEC] if gqa else [])
    args = [proj, proj, proj, bias] + ([sinks] if gqa else [])
    o_shape = jax.ShapeDtypeStruct((s, 512), F32)
    return pl.pallas_call(
        body, name=name, grid=(n_t, n_g), in_specs=in_specs, out_specs=(tile_spec, tile_spec),
        out_shape=(o_shape, o_shape),
        scratch_shapes=[pltpu.VMEM((s + pad, LANES), BF16), pltpu.VMEM((s + pad, LANES), BF16)],
        compiler_params=_params(("arbitrary", "arbitrary")),
    )(*args)


def _attention_bwd(proj, bias, sinks, do, lse, *, n_back, gqa, q_col, k_col, v_col, name):
    s = proj.shape[0]
    lk, pad, q_spec, k_spec, v_spec, bias_spec, tile_spec = _attn_common(s, n_back, gqa, q_col, k_col, v_col)
    n_t, n_g = 512 // LANES, s // QROWS

    def body(*refs):
        if gqa:
            (q_ref, k_ref, v_ref, bias_ref, sink_ref, do_ref, l_ref,
             dq_ref, dk_ref, dv_ref, dsink_ref, kpad, vpad, dkpad, dvpad) = refs
        else:
            (q_ref, k_ref, v_ref, bias_ref, do_ref, l_ref,
             dq_ref, dk_ref, dv_ref, dbias_ref, kpad, vpad, dkpad, dvpad) = refs
        t, g = pl.program_id(0), pl.program_id(1)

        @pl.when(g == 0)
        def _():
            kpad[0:pad, :] = jnp.zeros((pad, LANES), BF16)
            vpad[0:pad, :] = jnp.zeros((pad, LANES), BF16)
            kpad[pad:, :] = k_ref[...]
            vpad[pad:, :] = v_ref[...]
            if gqa:
                dsink_ref[...] = jnp.zeros_like(dsink_ref)
            else:
                dbias_ref[...] = jnp.zeros_like(dbias_ref)

        @pl.when((g == 0) & (t == 0) if gqa else g == 0)
        def _():
            dkpad[...] = jnp.zeros_like(dkpad)
            dvpad[...] = jnp.zeros_like(dvpad)

        start = pl.multiple_of(g * QROWS, QROWS)
        kb = kpad[pl.ds(start, lk), :]
        vb = vpad[pl.ds(start, lk), :]
        half = lax.broadcasted_iota(jnp.int32, (QROWS, LANES), 1) // HEAD_DIM
        col_ok = lax.broadcasted_iota(jnp.int32, (QROWS, lk), 1) >= (n_back - QG * g) * CHUNK
        q = q_ref[...]
        dov = do_ref[...]
        lv = l_ref[...]
        if gqa:
            hk = t // 2
            q_rolled = pltpu.roll(q.astype(F32), HEAD_DIM, 1).astype(BF16)
            do_rolled = pltpu.roll(dov, HEAD_DIM, 1)
        dqs = []
        dk_acc = jnp.zeros((lk, LANES), F32)
        dv_acc = jnp.zeros((lk, LANES), F32)
        for e in range(2):
            if gqa:
                kv_half = hk
                src = jnp.where(hk == e, q, q_rolled)
                do_src = jnp.where(hk == e, dov, do_rolled)
            else:
                kv_half = e
                src = q
                do_src = dov
            qm = jnp.where(half == kv_half, src, jnp.zeros_like(src))
            dom = jnp.where(half == kv_half, do_src, 0.0).astype(BF16)
            lcol = jnp.max(jnp.where(half == e, lv, -jnp.inf), axis=-1, keepdims=True)
            sc = _dot_nt(qm, kb) * (HEAD_DIM ** -0.5) + bias_ref[e]
            sc = jnp.where(col_ok, sc, NEG_INF)
            pn = jnp.exp(sc - lcol)
            dp = _dot_nt(dom, vb)
            delta = jnp.sum(pn * dp, axis=-1, keepdims=True)
            ds = pn * (dp - delta)
            if gqa:
                p_sink = jnp.exp(sink_ref[2 * t + e] - lcol)
                dsk = -jnp.sum(p_sink * delta, axis=0, keepdims=True)
                dsink_ref[0, e:e + 1, :] += jnp.broadcast_to(dsk, (1, LANES))
            else:
                dbias_ref[e] += ds
            dsb = (ds * (HEAD_DIM ** -0.5)).astype(BF16)
            dqs.append(_dot(dsb, kb))
            dk_acc = dk_acc + _dot_tn(dsb, qm)
            dv_acc = dv_acc + _dot_tn(pn.astype(BF16), dom)
        dkpad[pl.ds(start, lk), :] += dk_acc
        dvpad[pl.ds(start, lk), :] += dv_acc
        if gqa:
            same = jnp.where(hk == 0, dqs[0], dqs[1])
            other = jnp.where(hk == 0, dqs[1], dqs[0])
            dq_ref[...] = jnp.where(half == hk, same, pltpu.roll(other, HEAD_DIM, 1)).astype(BF16)
        else:
            dq_ref[...] = jnp.where(half == 0, dqs[0], dqs[1]).astype(BF16)

        @pl.when((g == n_g - 1) & (t == n_t - 1) if gqa else g == n_g - 1)
        def _():
            dk_ref[...] = dkpad[pad:, :].astype(BF16)
            dv_ref[...] = dvpad[pad:, :].astype(BF16)

    in_specs = [q_spec, k_spec, v_spec, bias_spec] + ([SMEM_SPEC] if gqa else []) + [tile_spec, tile_spec]
    args = [proj, proj, proj, bias] + ([sinks] if gqa else []) + [do, lse]
    if gqa:
        kv_out = pl.BlockSpec((s, LANES), lambda t, g: (0, 0))
        kv_shape = jax.ShapeDtypeStruct((s, LANES), BF16)
        extra_spec = pl.BlockSpec((1, 8, LANES), lambda t, g: (t, 0, 0))
        extra_shape = jax.ShapeDtypeStruct((n_t, 8, LANES), F32)
    else:
        kv_out = pl.BlockSpec((s, LANES), lambda t, g: (0, t))
        kv_shape = jax.ShapeDtypeStruct((s, 512), BF16)
        extra_spec = bias_spec
        extra_shape = jax.ShapeDtypeStruct(bias.shape, F32)
    return pl.pallas_call(
        body, name=name, grid=(n_t, n_g), in_specs=in_specs,
        out_specs=(tile_spec, kv_out, kv_out, extra_spec),
        out_shape=(jax.ShapeDtypeStruct((s, 512), BF16), kv_shape, kv_shape, extra_shape),
        scratch_shapes=[pltpu.VMEM((s + pad, LANES), BF16), pltpu.VMEM((s + pad, LANES), BF16),
                        pltpu.VMEM((s + pad, LANES), F32), pltpu.VMEM((s + pad, LANES), F32)],
        compiler_params=_params(("arbitrary", "arbitrary")),
    )(*args)


def _sum_slots(r, name):
    _, rows, k = r.shape

    def body(r_ref, o_ref):
        acc = r_ref[0].astype(F32)
        for j in range(1, N_DEV):
            acc = acc + r_ref[j].astype(F32)
        o_ref[...] = acc

    return pl.pallas_call(
        body, name=name, grid=(k // LANES,), in_specs=[pl.BlockSpec((N_DEV, rows, LANES), lambda i: (0, 0, i))],
        out_specs=pl.BlockSpec((rows, LANES), lambda i: (0, i)), out_shape=jax.ShapeDtypeStruct((rows, k), F32),
        compiler_params=_params(("parallel",)),
    )(r)


def _sum_rows8(g):
    n = g.shape[2]

    def body(g_ref, o_ref):
        acc = g_ref[0]
        for j in range(1, N_DEV):
            acc = acc + g_ref[j]
        o_ref[...] = acc

    return pl.pallas_call(
        body, name="sum_small_grads", in_specs=[VMEM_SPEC], out_specs=VMEM_SPEC,
        out_shape=jax.ShapeDtypeStruct((1, n), F32), compiler_params=_params(),
    )(g)


def _ada_weight_grad(sc_t, dmod_cols):
    d = sc_t.shape[0]
    w = dmod_cols.shape[1]
    td = _pick(d, (256, 128))

    def body(sc_ref, dm_ref, o_ref):
        scv = sc_ref[...]
        dmv = dm_ref[...]
        acc = scv[:, 0:1] * dmv[0:1, :]
        for b in range(1, N_DEV):
            acc = acc + scv[:, b:b + 1] * dmv[b:b + 1, :]
        o_ref[...] = acc

    return pl.pallas_call(
        body, name="ada_weight_grad", grid=(d // td,),
        in_specs=[pl.BlockSpec((td, N_DEV), lambda i: (i, 0)), pl.BlockSpec((N_DEV, w), lambda i: (0, 0))],
        out_specs=pl.BlockSpec((td, w), lambda i: (i, 0)), out_shape=jax.ShapeDtypeStruct((d, w), F32),
        compiler_params=_params(("parallel",)),
    )(sc_t, dmod_cols)


def _adamw(w, g, m, v, name):
    rows, cols = w.shape
    tr = _pick(rows, (256, 176, 128, 88, 64)) if rows > 256 else rows

    def body(w_ref, g_ref, m_ref, v_ref, d_ref, nm_ref, nv_ref):
        gv = g_ref[...]
        nm = ADAM_B1 * m_ref[...] + (1.0 - ADAM_B1) * gv
        nv = ADAM_B2 * v_ref[...] + (1.0 - ADAM_B2) * (gv * gv)
        m_hat = nm / (1.0 - ADAM_B1 ** ADAM_STEP)
        v_hat = nv / (1.0 - ADAM_B2 ** ADAM_STEP)
        d_ref[...] = -ADAM_LR * (m_hat / (jnp.sqrt(v_hat) + ADAM_EPS) + ADAM_WD * w_ref[...])
        nm_ref[...] = nm
        nv_ref[...] = nv

    spec = pl.BlockSpec((tr, cols), lambda i: (i, 0))
    shape = jax.ShapeDtypeStruct((rows, cols), F32)
    return pl.pallas_call(
        body, name=name, grid=(rows // tr,), in_specs=[spec] * 4, out_specs=(spec, spec, spec),
        out_shape=(shape, shape, shape), compiler_params=_params(("parallel",)),
    )(w, g, m, v)


SMALL = ("b_ada", "g_pre_ffn1", "g_post_ffn1", "g_pre_mix", "b_in", "sinks_a", "rel_bias_b", "g_grp_a",
         "g_grp_b", "b_out", "g_post_mix", "g_pre_ffn2", "g_post_ffn2")
WEIGHTS = ("w_ada", "b_ada", "g_pre_ffn1", "w_gate1", "w_up1", "w_down1", "g_post_ffn1", "g_pre_mix", "w_in",
           "b_in", "sinks_a", "rel_bias_b", "g_grp_a", "g_grp_b", "w_out", "b_out", "g_post_mix", "g_pre_ffn2",
           "w_gate2", "w_up2", "w_down2", "g_post_ffn2")


def kernel(x, c, w_ada, b_ada, g_pre_ffn1, w_gate1, w_up1, w_down1, g_post_ffn1, g_pre_mix, w_in, b_in, sinks_a, rel_bias_b, g_grp_a, g_grp_b, w_out, b_out, g_post_mix, g_pre_ffn2, w_gate2, w_up2, w_down2, g_post_ffn2, loss_target, m_w_ada, m_b_ada, m_g_pre_ffn1, m_w_gate1, m_w_up1, m_w_down1, m_g_post_ffn1, m_g_pre_mix, m_w_in, m_b_in, m_sinks_a, m_rel_bias_b, m_g_grp_a, m_g_grp_b, m_w_out, m_b_out, m_g_post_mix, m_g_pre_ffn2, m_w_gate2, m_w_up2, m_w_down2, m_g_post_ffn2, v_w_ada, v_b_ada, v_g_pre_ffn1, v_w_gate1, v_w_up1, v_w_down1, v_g_post_ffn1, v_g_pre_mix, v_w_in, v_b_in, v_sinks_a, v_rel_bias_b, v_g_grp_a, v_g_grp_b, v_w_out, v_b_out, v_g_post_mix, v_g_pre_ffn2, v_w_gate2, v_w_up2, v_w_down2, v_g_post_ffn2):
    given = dict(locals())
    weights = {n: given[n] for n in WEIGHTS}
    mom_m = {n: given["m_" + n] for n in WEIGHTS}
    mom_v = {n: given["v_" + n] for n in WEIGHTS}

    me = 4 * lax.axis_index("x") + 2 * lax.axis_index("y") + lax.axis_index("c")
    xs = x[0]
    tgt = loss_target[0]
    d_model = xs.shape[1]
    ada_cols = w_ada.shape[2]

    b_cols = lax.dynamic_slice(b_ada, (0, me * ada_cols), (1, ada_cols))
    sc_all, mod_rows = _ada_forward(c, w_ada[0], b_cols)
    mod = mod_rows.reshape(N_MOD, d_model)
    shift1, scale1, gate1, shift2, scale2, gate2, shift3, scale3, gate3 = (mod[i:i + 1] for i in range(N_MOD))

    shards = [w_gate1[0].T, w_up1[0].T, w_down1[0], w_in[0].T, w_out[0], w_gate2[0].T, w_up2[0].T, w_down2[0]]
    wg1, wu1, wd1, win, wo, wg2, wu2, wd2 = _all_gather_weights([s.astype(BF16) for s in shards])

    def ffn_forward(xin, g_pre, g_post, shift, scale, gate, wg, wu, wd, tag):
        h = _pre_norm(xin, g_pre, scale, shift, "pre_norm_" + tag)
        a, b, u = _ffn_up(h, wg, wu, "ffn_up_" + tag)
        y = _mm_nn([(u, wd)], "ffn_down_" + tag, F32)
        xout = _post_norm_residual(xin, y, g_post, gate, 0.5, "post_norm_" + tag)
        return xout, (xin, h, a, b, u, y)

    def ffn_backward(dout, saved, g_pre, g_post, scale, gate, wg, wu, wd, tag):
        xin, h, a, b, u, y = saved
        dy, s1, _ = _post_norm_bwd(dout, y, g_post, gate, 0.5, "post_norm_bwd_" + tag)
        da, db = _ffn_down_bwd(dy, wd, a, b, "ffn_down_bwd_" + tag)
        dwd = _mm_tn(u, dy, "grad_wd_" + tag)
        dwg = _mm_tn(da, h, "grad_wg_" + tag)
        dwu = _mm_tn(db, h, "grad_wu_" + tag)
        dh = _mm_nn([(da, wg), (db, wu)], "ffn_up_bwd_" + tag, F32)
        dx, s2, s3 = _pre_norm_bwd(dh, xin, g_pre, scale, dout, "pre_norm_bwd_" + tag)
        small = dict(shift=s3, scale=s2 * g_pre, gate=0.5 * g_post * s1,
                     g_pre=(1.0 + scale) * s2, g_post=(0.5 * gate) * s1)
        return dx, (dwg, dwu, dwd), small

    x1, saved1 = ffn_forward(xs, g_pre_ffn1, g_post_ffn1, shift1, scale1, gate1, wg1, wu1, wd1, "ffn1")

    h2 = _pre_norm(x1, g_pre_mix, scale2, shift2, "pre_norm_mix")
    proj = _mm_nt(h2, win, "in_proj", BF16, bias=b_in)
    bias_a = _alibi_bias()
    rel_m = _rel_index_matrix()
    rel_vec = jnp.dot(rel_bias_b[0], rel_m.T, precision=lax.Precision.HIGHEST)
    bias_b = _toeplitz_bias(rel_vec.reshape(H_B, 1, SKEW))
    sinks = sinks_a[0]
    cfg_a = dict(n_back=BACK_A, gqa=True, q_col=0, k_col=QA // LANES, v_col=(QA + KVA) // LANES)
    cfg_b = dict(n_back=BACK_B, gqa=False, q_col=(QA + 2 * KVA) // LANES, k_col=(QA + 2 * KVA + QB) // LANES,
                 v_col=(QA + 2 * KVA + 2 * QB) // LANES)
    oa, lse_a = _attention_fwd(proj, bias_a, sinks, name="attn_a", **cfg_a)
    ob, lse_b = _attention_fwd(proj, bias_b, None, name="attn_b", **cfg_b)
    ycat = _group_norm_cat(oa, ob, g_grp_a, g_grp_b)
    ymix = _mm_nn([(ycat, wo)], "out_proj", F32, bias=b_out)
    x2 = _post_norm_residual(x1, ymix, g_post_mix, gate2, 1.0, "post_norm_mix")

    x3, saved3 = ffn_forward(x2, g_pre_ffn2, g_post_ffn2, shift3, scale3, gate3, wg2, wu2, wd2, "ffn2")

    loss_part, dx3 = _loss_and_grad(x3, tgt)
    loss = lax.psum(loss_part[0, 0], ("x", "y", "c"))

    dx2, (dwg2, dwu2, dwd2), sm3 = ffn_backward(dx3, saved3, g_pre_ffn2, g_post_ffn2, scale3, gate3,
                                                wg2, wu2, wd2, "ffn2")

    dymix, s1m, db_out = _post_norm_bwd(dx2, ymix, g_post_mix, gate2, 1.0, "post_norm_bwd_mix")
    dycat = _mm_nt(dymix, wo, "out_proj_bwd", F32)
    dwo = _mm_tn(ycat, dymix, "grad_wo")
    doa, dob, dg_a, dg_b = _group_norm_bwd(dycat, oa, ob, g_grp_a, g_grp_b)
    dqa, dka, dva, dsink = _attention_bwd(proj, bias_a, sinks, doa, lse_a, name="attn_a_bwd", **cfg_a)
    dqb, dkb, dvb, dbias = _attention_bwd(proj, bias_b, None, dob, lse_b, name="attn_b_bwd", **cfg_b)
    dproj = jnp.concatenate([dqa, dka, dva, dqb, dkb, dvb], axis=1)
    db_in = _col_sum(dproj, "grad_b_in")
    dwin = _mm_tn(dproj, h2, "grad_win")
    dh2 = _mm_nn([(dproj, win)], "in_proj_bwd", F32)
    dx1, s2m, s3m = _pre_norm_bwd(dh2, x1, g_pre_mix, scale2, dx2, "pre_norm_bwd_mix")
    d_rel = jnp.dot(_diagonal_sums(dbias).reshape(H_B, SKEW), rel_m, precision=lax.Precision.HIGHEST)
    d_sinks = dsink[:, :2, 0].reshape(1, H_A)

    dx0, (dwg1, dwu1, dwd1), sm1 = ffn_backward(dx1, saved1, g_pre_ffn1, g_post_ffn1, scale1, gate1,
                                                wg1, wu1, wd1, "ffn1")

    slots = _reduce_scatter_send([dwg1, dwu1, dwd1, dwin, dwo, dwg2, dwu2, dwd2])
    gsum = _sum_slots(slots, "sum_weight_grads")
    row_counts = [s.shape[0] for s in shards]
    pieces, off = [], 0
    for n_rows in row_counts:
        pieces.append(gsum[off:off + n_rows])
        off += n_rows
    grads = {"w_gate1": pieces[0].T, "w_up1": pieces[1].T, "w_down1": pieces[2], "w_in": pieces[3].T,
             "w_out": pieces[4], "w_gate2": pieces[5].T, "w_up2": pieces[6].T, "w_down2": pieces[7]}

    dmod = jnp.concatenate([sm1["shift"], sm1["scale"], sm1["gate"],
                            s3m, s2m * g_pre_mix, g_post_mix * s1m,
                            sm3["shift"], sm3["scale"], sm3["gate"]], axis=1)
    small_parts = {
        "b_ada": dmod, "g_pre_ffn1": sm1["g_pre"], "g_post_ffn1": sm1["g_post"],
        "g_pre_mix": (1.0 + scale2) * s2m, "b_in": db_in, "sinks_a": d_sinks,
        "rel_bias_b": d_rel.reshape(1, H_B * N_REL), "g_grp_a": dg_a, "g_grp_b": dg_b, "b_out": db_out,
        "g_post_mix": gate2 * s1m, "g_pre_ffn2": sm3["g_pre"], "g_post_ffn2": sm3["g_post"]}
    sizes = [small_parts[n].shape[1] for n in SMALL]
    n_small = sum(sizes)
    n_pad = -n_small % LANES
    packed = jnp.concatenate([small_parts[n] for n in SMALL] + [jnp.zeros((1, n_pad), F32)], axis=1)
    gathered = _all_gather_small(packed)
    small_sum = _sum_rows8(gathered)
    dmod_cols = lax.dynamic_slice(gathered.reshape(N_DEV, n_small + n_pad), (0, me * ada_cols),
                                  (N_DEV, ada_cols))
    grads["w_ada"] = _ada_weight_grad(sc_all.reshape(N_DEV, d_model).T, dmod_cols)

    out_g, out_d, out_m, out_v = {}, {}, {}, {}
    for n in ("w_ada", "w_gate1", "w_up1", "w_down1", "w_in", "w_out", "w_gate2", "w_up2", "w_down2"):
        d_, m_, v_ = _adamw(weights[n][0], grads[n], mom_m[n][0], mom_v[n][0], "adamw_" + n)
        out_g[n], out_d[n], out_m[n], out_v[n] = grads[n][None], d_[None], m_[None], v_[None]

    def pack(tree):
        return jnp.concatenate([tree[n].reshape(1, -1) for n in SMALL] + [jnp.zeros((1, n_pad), F32)], axis=1)

    d_s, m_s, v_s = _adamw(pack(weights), small_sum, pack(mom_m), pack(mom_v), "adamw_small")
    off = 0
    for n, size in zip(SMALL, sizes):
        shape = weights[n].shape
        out_g[n] = small_sum[:, off:off + size].reshape(shape)
        out_d[n] = d_s[:, off:off + size].reshape(shape)
        out_m[n] = m_s[:, off:off + size].reshape(shape)
        out_v[n] = v_s[:, off:off + size].reshape(shape)
        off += size

    return (loss, dx0[None], *[out_g[n] for n in WEIGHTS], *[out_d[n] for n in WEIGHTS],
            *[out_m[n] for n in WEIGHTS], *[out_v[n] for n in WEIGHTS])
```

```python
import numpy as np
import jax
import jax.numpy as jnp
from jax import lax
from jax.experimental import pallas as pl
from jax.experimental.pallas import tpu as pltpu

F32 = jnp.float32
BF16 = jnp.bfloat16
MESH = pl.DeviceIdType.MESH
ANY = pl.BlockSpec(memory_space=pl.ANY)
VMEM_SPEC = pl.BlockSpec(memory_space=pltpu.VMEM)
SMEM_SPEC = pl.BlockSpec(memory_space=pltpu.SMEM)

N_DEV = 8
CHUNK = 64
HEAD_DIM = 64
LANES = 128
H_A, KV_A, H_B = 8, 2, 8
BACK_A, BACK_B = 2, 8
REL_CLIP = 128
N_REL = 2 * REL_CLIP + 1
QA, KVA, QB = H_A * HEAD_DIM, KV_A * HEAD_DIM, H_B * HEAD_DIM
D_IN = QA + 2 * KVA + 3 * QB
N_MOD = 9
EPS = 1e-6
NEG_INF = -1e30
QG = 4
QROWS = QG * CHUNK
SKEW = 1024
ADAM_LR, ADAM_B1, ADAM_B2, ADAM_EPS, ADAM_WD, ADAM_STEP = 0.001, 0.9, 0.999, 1e-08, 0.01, 10
VMEM_LIMIT = 56 * 2 ** 20


def _pick(n, cands):
    for c in cands:
        if n % c == 0:
            return c
    return n


def _params(sem=None):
    return pltpu.CompilerParams(dimension_semantics=sem, vmem_limit_bytes=VMEM_LIMIT)


def _dot_nt(a, b):
    return lax.dot_general(a, b, (((1,), (1,)), ((), ())), preferred_element_type=F32)


def _dot_tn(a, b):
    return lax.dot_general(a, b, (((0,), (0,)), ((), ())), preferred_element_type=F32)


def _dot(a, b):
    return jnp.dot(a, b, preferred_element_type=F32)


def _sigmoid(a):
    return 1.0 / (1.0 + jnp.exp(-a))


def _mesh_pos():
    return lax.axis_index("x"), lax.axis_index("y"), lax.axis_index("c")


def _peer(x, y, c, r):
    px = 1 - x if r & 4 else x
    py = 1 - y if r & 2 else y
    pc = 1 - c if r & 1 else c
    return px, py, pc


class _Carry:
    def __init__(self, ins, out_shapes, scratch, start, finish):
        self.ins, self.out_shapes, self.scratch = list(ins), list(out_shapes), list(scratch)
        self.start, self.finish = start, finish


def _call(body, *, name, grid, in_specs, out_specs, out_shape, args, scratch=(), sem=None, carry=None):
    single = not isinstance(out_shape, (tuple, list))
    out_specs = (out_specs,) if single else tuple(out_specs)
    out_shape = (out_shape,) if single else tuple(out_shape)
    if carry is None:
        res = pl.pallas_call(body, name=name, grid=grid, in_specs=list(in_specs), out_specs=out_specs,
                             out_shape=out_shape, scratch_shapes=list(scratch), compiler_params=_params(sem))(*args)
        return res[0] if single else res
    n_in, n_out, n_s = len(in_specs), len(out_shape), len(scratch)
    ci, co = len(carry.ins), len(carry.out_shapes)

    def wrapped(*refs):
        ins, cins = refs[:n_in], refs[n_in:n_in + ci]
        outs = refs[n_in + ci:n_in + ci + n_out]
        couts = refs[n_in + ci + n_out:n_in + ci + n_out + co]
        scr = refs[n_in + ci + n_out + co:n_in + ci + n_out + co + n_s]
        cscr = refs[n_in + ci + n_out + co + n_s:]
        first, last = None, None
        for ax, n in enumerate(grid):
            f, l = pl.program_id(ax) == 0, pl.program_id(ax) == n - 1
            first = f if first is None else first & f
            last = l if last is None else last & l
        pl.when(first)(lambda: carry.start(cins, couts, cscr))
        body(*ins, *outs, *scr)
        pl.when(last)(lambda: carry.finish(cins, couts, cscr))

    res = pl.pallas_call(
        wrapped, name=name, grid=grid, in_specs=list(in_specs) + [ANY] * ci, out_specs=out_specs + (ANY,) * co,
        out_shape=out_shape + tuple(carry.out_shapes), scratch_shapes=list(scratch) + carry.scratch,
        compiler_params=_params(("arbitrary",) * len(grid)))(*args, *carry.ins)
    main = res[:n_out]
    return (main[0] if single else main), res[n_out:]


def _run_carry(carry, name):
    ci, co = len(carry.ins), len(carry.out_shapes)

    def body(*refs):
        carry.start(refs[:ci], refs[ci:ci + co], refs[ci + co:])
        carry.finish(refs[:ci], refs[ci:ci + co], refs[ci + co:])

    return pl.pallas_call(body, name=name, in_specs=[ANY] * ci, out_specs=(ANY,) * co,
                          out_shape=tuple(carry.out_shapes), scratch_shapes=carry.scratch,
                          compiler_params=_params())(*carry.ins)


def _gather_carry(shards):
    n_w = len(shards)
    rows = [s.shape[0] for s in shards]

    def plan(ins, outs, scr):
        send_sems, recv_sems, local_sems = scr
        x, y, c = _mesh_pos()
        me, sibling = (x, y, c), (x, y, 1 - c)
        chips = [(1 - x, y), (x, 1 - y), (1 - x, 1 - y)]

        def block(w, dev):
            start = pl.multiple_of((4 * dev[0] + 2 * dev[1] + dev[2]) * rows[w], 16)
            return outs[w].at[pl.ds(start, rows[w]), :]

        def copy(w, k, dev, to, src=None):
            return pltpu.make_async_remote_copy(
                src_ref=block(w, dev) if src is None else src, dst_ref=block(w, dev),
                send_sem=send_sems.at[w, k], recv_sem=recv_sems.at[w, k], device_id=to, device_id_type=MESH)

        mine = [pltpu.make_async_copy(ins[w], block(w, me), local_sems.at[w]) for w in range(n_w)]
        first = []
        for j, chip in enumerate(chips):
            first += [copy(w, 1 + j, me, (*chip, c), src=ins[w]) for w in range(n_w)]
        first += [copy(w, 0, me, sibling, src=ins[w]) for w in range(n_w)]
        return c, me, sibling, chips, copy, mine, first

    def start(ins, outs, scr):
        _, _, _, _, _, mine, first = plan(ins, outs, scr)
        for cp in mine + first:
            cp.start()

    def finish(ins, outs, scr):
        c, me, sibling, chips, copy, mine, first = plan(ins, outs, scr)
        passed = []
        for j, chip in enumerate(chips):
            for w in range(n_w):
                copy(w, 1 + j, (*chip, c), me).wait_recv()
                cp = copy(w, 4 + j, (*chip, c), sibling)
                cp.start()
                passed.append(cp)
        for w in range(n_w):
            copy(w, 0, sibling, me).wait_recv()
        for j, chip in enumerate(chips):
            for w in range(n_w):
                copy(w, 4 + j, (*chip, 1 - c), me).wait_recv()
        for cp in first + passed:
            cp.wait_send()
        for cp in mine:
            cp.wait()

    return _Carry(
        shards, [jax.ShapeDtypeStruct((N_DEV * s.shape[0], s.shape[1]), s.dtype) for s in shards],
        [pltpu.SemaphoreType.DMA((n_w, N_DEV - 1)), pltpu.SemaphoreType.DMA((n_w, N_DEV - 1)),
         pltpu.SemaphoreType.DMA((n_w,))], start, finish)


def _scatter_carry(grads):
    n_w = len(grads)
    rows = [g.shape[0] // N_DEV for g in grads]

    def plan(ins, outs, scr):
        send_sems, recv_sems, local_sems = scr
        x, y, c = _mesh_pos()
        me = 4 * x + 2 * y + c

        def src(w, dev_index):
            return ins[w].at[pl.ds(pl.multiple_of(dev_index * rows[w], 16), rows[w]), :]

        mine = [pltpu.make_async_copy(src(w, me), outs[w].at[0], local_sems.at[w]) for w in range(n_w)]
        copies = []
        for r in (6, 7, 4, 5, 2, 3, 1):
            px, py, pc = _peer(x, y, c, r)
            for w in range(n_w):
                copies.append(pltpu.make_async_remote_copy(
                    src_ref=src(w, 4 * px + 2 * py + pc), dst_ref=outs[w].at[r], send_sem=send_sems.at[w, r - 1],
                    recv_sem=recv_sems.at[w, r - 1], device_id=(px, py, pc), device_id_type=MESH))
        return mine, copies

    def start(ins, outs, scr):
        mine, copies = plan(ins, outs, scr)
        for cp in mine + copies:
            cp.start()

    def finish(ins, outs, scr):
        mine, copies = plan(ins, outs, scr)
        for cp in copies:
            cp.wait_recv()
        for cp in copies:
            cp.wait_send()
        for cp in mine:
            cp.wait()

    return _Carry(
        grads, [jax.ShapeDtypeStruct((N_DEV, r, g.shape[1]), g.dtype) for r, g in zip(rows, grads)],
        [pltpu.SemaphoreType.DMA((n_w, N_DEV - 1)), pltpu.SemaphoreType.DMA((n_w, N_DEV - 1)),
         pltpu.SemaphoreType.DMA((n_w,))], start, finish)


def _ada_forward(c_row, w_ada, b_cols):
    d = c_row.shape[1]
    wcols = w_ada.shape[1]

    def body(c_ref, w_ref, b_ref, sc_ref, mod_ref, rows_ref, send_sems, recv_sems):
        x, y, c = _mesh_pos()
        me = 4 * x + 2 * y + c
        cv = c_ref[...]
        sc_ref[me] = cv * _sigmoid(cv)

        sends = []
        for r in range(1, N_DEV):
            px, py, pc = _peer(x, y, c, r)
            cp = pltpu.make_async_remote_copy(
                src_ref=sc_ref.at[me], dst_ref=sc_ref.at[me], send_sem=send_sems.at[0, r - 1],
                recv_sem=recv_sems.at[0, r - 1], device_id=(px, py, pc), device_id_type=MESH)
            cp.start()
            sends.append(cp)
        for r in range(1, N_DEV):
            px, py, pc = _peer(x, y, c, r)
            pid = 4 * px + 2 * py + pc
            pltpu.make_async_remote_copy(
                src_ref=sc_ref.at[pid], dst_ref=sc_ref.at[pid], send_sem=send_sems.at[0, r - 1],
                recv_sem=recv_sems.at[0, r - 1], device_id=(px, py, pc), device_id_type=MESH).wait_recv()
        for cp in sends:
            cp.wait_send()

        sc_all = jnp.concatenate([sc_ref[j] for j in range(N_DEV)], axis=0)
        rows = _dot(sc_all.astype(BF16), w_ref[...].astype(BF16)) + b_ref[...]
        for j in range(N_DEV):
            rows_ref[j] = rows[j:j + 1, :]
        mod_ref[me] = rows_ref[me]

        sends = []
        for r in range(1, N_DEV):
            px, py, pc = _peer(x, y, c, r)
            pid = 4 * px + 2 * py + pc
            cp = pltpu.make_async_remote_copy(
                src_ref=rows_ref.at[pid], dst_ref=mod_ref.at[me], send_sem=send_sems.at[1, r - 1],
                recv_sem=recv_sems.at[1, r - 1], device_id=(px, py, pc), device_id_type=MESH)
            cp.start()
            sends.append(cp)
        for r in range(1, N_DEV):
            px, py, pc = _peer(x, y, c, r)
            pid = 4 * px + 2 * py + pc
            pltpu.make_async_remote_copy(
                src_ref=rows_ref.at[pid], dst_ref=mod_ref.at[pid], send_sem=send_sems.at[1, r - 1],
                recv_sem=recv_sems.at[1, r - 1], device_id=(px, py, pc), device_id_type=MESH).wait_recv()
        for cp in sends:
            cp.wait_send()

    return pl.pallas_call(
        body, name="ada_forward",
        out_shape=(jax.ShapeDtypeStruct((N_DEV, 1, d), F32), jax.ShapeDtypeStruct((N_DEV, 1, wcols), F32)),
        in_specs=[VMEM_SPEC, VMEM_SPEC, VMEM_SPEC], out_specs=(VMEM_SPEC, VMEM_SPEC),
        scratch_shapes=[pltpu.VMEM((N_DEV, 1, wcols), F32), pltpu.SemaphoreType.DMA((2, N_DEV - 1)),
                        pltpu.SemaphoreType.DMA((2, N_DEV - 1))],
        compiler_params=_params(),
    )(c_row, w_ada, b_cols)


def _all_gather_small(v):
    n = v.shape[1]

    def body(v_ref, out_ref, send_sems, recv_sems):
        x, y, c = _mesh_pos()
        me = 4 * x + 2 * y + c
        out_ref[me] = v_ref[...]
        sends = []
        for r in range(1, N_DEV):
            px, py, pc = _peer(x, y, c, r)
            cp = pltpu.make_async_remote_copy(
                src_ref=v_ref, dst_ref=out_ref.at[me], send_sem=send_sems.at[r - 1],
                recv_sem=recv_sems.at[r - 1], device_id=(px, py, pc), device_id_type=MESH)
            cp.start()
            sends.append(cp)
        for r in range(1, N_DEV):
            px, py, pc = _peer(x, y, c, r)
            pid = 4 * px + 2 * py + pc
            pltpu.make_async_remote_copy(
                src_ref=v_ref, dst_ref=out_ref.at[pid], send_sem=send_sems.at[r - 1],
                recv_sem=recv_sems.at[r - 1], device_id=(px, py, pc), device_id_type=MESH).wait_recv()
        for cp in sends:
            cp.wait_send()

    return pl.pallas_call(
        body, name="all_gather_small",
        out_shape=jax.ShapeDtypeStruct((N_DEV, 1, n), F32),
        in_specs=[VMEM_SPEC], out_specs=VMEM_SPEC,
        scratch_shapes=[pltpu.SemaphoreType.DMA((N_DEV - 1,)), pltpu.SemaphoreType.DMA((N_DEV - 1,))],
        compiler_params=_params(),
    )(v)


def _mm_nt(a, b, name, out_dtype, bias=None, carry=None):
    m, k = a.shape
    n = b.shape[0]
    tm = _pick(m, (512, 256, 128))
    tn = _pick(n, (1408, 1152, 1024, 768, 512, 256, 128))

    def body(*refs):
        acc = _dot_nt(refs[0][...], refs[1][...])
        if bias is not None:
            acc = acc + refs[2][...]
        refs[-1][...] = acc.astype(out_dtype)

    in_specs = [pl.BlockSpec((tm, k), lambda i, j: (i, 0)), pl.BlockSpec((tn, k), lambda i, j: (j, 0))]
    args = [a, b]
    if bias is not None:
        in_specs.append(pl.BlockSpec((1, tn), lambda i, j: (0, j)))
        args.append(bias)
    return _call(body, name=name, grid=(m // tm, n // tn), in_specs=in_specs,
                 out_specs=pl.BlockSpec((tm, tn), lambda i, j: (i, j)),
                 out_shape=jax.ShapeDtypeStruct((m, n), out_dtype), args=args,
                 sem=("parallel", "parallel"), carry=carry)


def _mm_nn(pairs, name, out_dtype, bias=None, carry=None):
    m, k = pairs[0][0].shape
    n = pairs[0][1].shape[1]
    tm = _pick(m, (512, 256, 128))
    tk = _pick(k, (1408, 1152, 1024, 768, 512, 256, 128))
    nk = k // tk
    n_p = len(pairs)

    def body(*refs):
        o_ref, acc_ref = refs[-2], refs[-1]
        kk = pl.program_id(1)

        @pl.when(kk == 0)
        def _():
            acc_ref[...] = jnp.zeros_like(acc_ref)

        for p in range(n_p):
            acc_ref[...] += _dot(refs[2 * p][...], refs[2 * p + 1][...])

        @pl.when(kk == nk - 1)
        def _():
            acc = acc_ref[...]
            if bias is not None:
                acc = acc + refs[2 * n_p][...]
            o_ref[...] = acc.astype(out_dtype)

    in_specs, args = [], []
    for a, b in pairs:
        in_specs += [pl.BlockSpec((tm, tk), lambda i, kk: (i, kk)), pl.BlockSpec((tk, n), lambda i, kk: (kk, 0))]
        args += [a, b]
    if bias is not None:
        in_specs.append(pl.BlockSpec((1, n), lambda i, kk: (0, 0)))
        args.append(bias)
    return _call(body, name=name, grid=(m // tm, nk), in_specs=in_specs,
                 out_specs=pl.BlockSpec((tm, n), lambda i, kk: (i, 0)),
                 out_shape=jax.ShapeDtypeStruct((m, n), out_dtype), args=args,
                 scratch=[pltpu.VMEM((tm, n), F32)], sem=("parallel", "arbitrary"), carry=carry)


def _mm_tn(a, b, name, out_dtype=BF16, carry=None):
    k, m = a.shape
    n = b.shape[1]
    tm = _pick(m, (1408, 1152, 1024, 768, 512, 256, 128))
    tk = _pick(k, (512, 256, 128))
    nk = k // tk

    def body(a_ref, b_ref, o_ref, acc_ref):
        kk = pl.program_id(1)

        @pl.when(kk == 0)
        def _():
            acc_ref[...] = jnp.zeros_like(acc_ref)

        acc_ref[...] += _dot_tn(a_ref[...], b_ref[...])

        @pl.when(kk == nk - 1)
        def _():
            o_ref[...] = acc_ref[...].astype(out_dtype)

    return _call(body, name=name, grid=(m // tm, nk),
                 in_specs=[pl.BlockSpec((tk, tm), lambda i, kk: (kk, i)), pl.BlockSpec((tk, n), lambda i, kk: (kk, 0))],
                 out_specs=pl.BlockSpec((tm, n), lambda i, kk: (i, 0)),
                 out_shape=jax.ShapeDtypeStruct((m, n), out_dtype), args=[a, b],
                 scratch=[pltpu.VMEM((tm, n), F32)], sem=("parallel", "arbitrary"), carry=carry)


def _ffn_up(h, wg_t, wu_t, name, carry=None):
    s, d = h.shape
    f = wg_t.shape[0]
    tm = _pick(s, (512, 256, 128))
    tf = _pick(f, (1408, 1024, 512, 256, 128))

    def body(h_ref, wg_ref, wu_ref, a_ref, b_ref, u_ref):
        hh = h_ref[...]
        a = _dot_nt(hh, wg_ref[...])
        b = _dot_nt(hh, wu_ref[...])
        a_ref[...] = a.astype(BF16)
        b_ref[...] = b.astype(BF16)
        u_ref[...] = ((a * _sigmoid(a)) * b).astype(BF16)

    w_spec = pl.BlockSpec((tf, d), lambda i, j: (j, 0))
    o_spec = pl.BlockSpec((tm, tf), lambda i, j: (i, j))
    o_shape = jax.ShapeDtypeStruct((s, f), BF16)
    return _call(body, name=name, grid=(s // tm, f // tf),
                 in_specs=[pl.BlockSpec((tm, d), lambda i, j: (i, 0)), w_spec, w_spec],
                 out_specs=(o_spec, o_spec, o_spec), out_shape=(o_shape, o_shape, o_shape),
                 args=[h, wg_t, wu_t], sem=("parallel", "parallel"), carry=carry)


def _ffn_down_bwd(dy, wd, a, b, name, carry=None):
    s, d = dy.shape
    f = wd.shape[0]
    tm = _pick(s, (512, 256, 128))
    tf = _pick(f, (1408, 1024, 512, 256, 128))

    def body(dy_ref, wd_ref, a_ref, b_ref, da_ref, db_ref):
        du = _dot_nt(dy_ref[...], wd_ref[...])
        a = a_ref[...].astype(F32)
        b = b_ref[...].astype(F32)
        sig = _sigmoid(a)
        da_ref[...] = (du * b * (sig * (1.0 + a * (1.0 - sig)))).astype(BF16)
        db_ref[...] = (du * (a * sig)).astype(BF16)

    t_spec = pl.BlockSpec((tm, tf), lambda i, j: (i, j))
    o_shape = jax.ShapeDtypeStruct((s, f), BF16)
    return _call(body, name=name, grid=(s // tm, f // tf),
                 in_specs=[pl.BlockSpec((tm, d), lambda i, j: (i, 0)), pl.BlockSpec((tf, d), lambda i, j: (j, 0)),
                           t_spec, t_spec],
                 out_specs=(t_spec, t_spec), out_shape=(o_shape, o_shape), args=[dy, wd, a, b],
                 sem=("parallel", "parallel"), carry=carry)


def _row_tile(s):
    return _pick(s, (256, 128, 64))


def _vec_spec(d):
    return pl.BlockSpec((1, d), lambda i: (0, 0))


def _pre_norm(x, g, scale, shift, name):
    s, d = x.shape
    ts = _row_tile(s)

    def body(x_ref, g_ref, sc_ref, sh_ref, h_ref):
        xv = x_ref[...]
        r = lax.rsqrt(jnp.mean(xv * xv, axis=-1, keepdims=True) + EPS)
        h_ref[...] = (((xv * r) * g_ref[...]) * (1.0 + sc_ref[...]) + sh_ref[...]).astype(BF16)

    row = pl.BlockSpec((ts, d), lambda i: (i, 0))
    return _call(body, name=name, grid=(s // ts,), in_specs=[row, _vec_spec(d), _vec_spec(d), _vec_spec(d)],
                 out_specs=row, out_shape=jax.ShapeDtypeStruct((s, d), BF16), args=[x, g, scale, shift],
                 sem=("parallel",))


def _post_norm_residual(x, y, g, gate, weight, name):
    s, d = x.shape
    ts = _row_tile(s)

    def body(x_ref, y_ref, g_ref, gate_ref, o_ref):
        yv = y_ref[...]
        r = lax.rsqrt(jnp.mean(yv * yv, axis=-1, keepdims=True) + EPS)
        o_ref[...] = x_ref[...] + (weight * gate_ref[...]) * ((yv * r) * g_ref[...])

    row = pl.BlockSpec((ts, d), lambda i: (i, 0))
    return _call(body, name=name, grid=(s // ts,), in_specs=[row, row, _vec_spec(d), _vec_spec(d)],
                 out_specs=row, out_shape=jax.ShapeDtypeStruct((s, d), F32), args=[x, y, g, gate],
                 sem=("parallel",))


def _post_norm_bwd(dout, y, g, gate, weight, name):
    s, d = y.shape
    ts = _row_tile(s)

    def body(do_ref, y_ref, g_ref, gate_ref, dy_ref, s1_ref, cs_ref):
        @pl.when(pl.program_id(0) == 0)
        def _():
            s1_ref[...] = jnp.zeros_like(s1_ref)
            cs_ref[...] = jnp.zeros_like(cs_ref)

        yv = y_ref[...]
        do = do_ref[...]
        r = lax.rsqrt(jnp.mean(yv * yv, axis=-1, keepdims=True) + EPS)
        yn = yv * r
        dyn = do * ((weight * gate_ref[...]) * g_ref[...])
        dy = r * (dyn - yn * jnp.mean(dyn * yn, axis=-1, keepdims=True))
        dy_ref[...] = dy.astype(BF16)
        s1_ref[...] += jnp.sum(do * yn, axis=0, keepdims=True)
        cs_ref[...] += jnp.sum(dy, axis=0, keepdims=True)

    row = pl.BlockSpec((ts, d), lambda i: (i, 0))
    vec = jax.ShapeDtypeStruct((1, d), F32)
    return _call(body, name=name, grid=(s // ts,), in_specs=[row, row, _vec_spec(d), _vec_spec(d)],
                 out_specs=(row, _vec_spec(d), _vec_spec(d)),
                 out_shape=(jax.ShapeDtypeStruct((s, d), BF16), vec, vec), args=[dout, y, g, gate],
                 sem=("arbitrary",))


def _pre_norm_bwd(dh, x, g, scale, dres, name):
    s, d = x.shape
    ts = _row_tile(s)

    def body(dh_ref, x_ref, g_ref, sc_ref, dr_ref, dx_ref, s2_ref, s3_ref):
        @pl.when(pl.program_id(0) == 0)
        def _():
            s2_ref[...] = jnp.zeros_like(s2_ref)
            s3_ref[...] = jnp.zeros_like(s3_ref)

        xv = x_ref[...]
        dh = dh_ref[...]
        r = lax.rsqrt(jnp.mean(xv * xv, axis=-1, keepdims=True) + EPS)
        n = xv * r
        dn = dh * (g_ref[...] * (1.0 + sc_ref[...]))
        dx_ref[...] = dr_ref[...] + r * (dn - n * jnp.mean(dn * n, axis=-1, keepdims=True))
        s2_ref[...] += jnp.sum(dh * n, axis=0, keepdims=True)
        s3_ref[...] += jnp.sum(dh, axis=0, keepdims=True)

    row = pl.BlockSpec((ts, d), lambda i: (i, 0))
    vec = jax.ShapeDtypeStruct((1, d), F32)
    return _call(body, name=name, grid=(s // ts,), in_specs=[row, row, _vec_spec(d), _vec_spec(d), row],
                 out_specs=(row, _vec_spec(d), _vec_spec(d)),
                 out_shape=(jax.ShapeDtypeStruct((s, d), F32), vec, vec), args=[dh, x, g, scale, dres],
                 sem=("arbitrary",))


def _group_norm_cat(oa, ob, ga, gb):
    s = oa.shape[0]
    ts = _row_tile(s)

    def body(oa_ref, ob_ref, ga_ref, gb_ref, y_ref):
        for o_ref, g_ref, lo, w in ((oa_ref, ga_ref, 0, QA), (ob_ref, gb_ref, QA, QB)):
            ov = o_ref[...]
            r = lax.rsqrt(jnp.mean(ov * ov, axis=-1, keepdims=True) + EPS)
            y_ref[:, lo:lo + w] = ((ov * r) * g_ref[...]).astype(BF16)

    return _call(body, name="group_norm_cat", grid=(s // ts,),
                 in_specs=[pl.BlockSpec((ts, QA), lambda i: (i, 0)), pl.BlockSpec((ts, QB), lambda i: (i, 0)),
                           _vec_spec(QA), _vec_spec(QB)],
                 out_specs=pl.BlockSpec((ts, QA + QB), lambda i: (i, 0)),
                 out_shape=jax.ShapeDtypeStruct((s, QA + QB), BF16), args=[oa, ob, ga, gb], sem=("parallel",))


def _group_norm_bwd(dy, oa, ob, ga, gb):
    s = oa.shape[0]
    ts = _row_tile(s)

    def body(dy_ref, oa_ref, ob_ref, ga_ref, gb_ref, doa_ref, dob_ref, dga_ref, dgb_ref):
        @pl.when(pl.program_id(0) == 0)
        def _():
            dga_ref[...] = jnp.zeros_like(dga_ref)
            dgb_ref[...] = jnp.zeros_like(dgb_ref)

        for o_ref, g_ref, do_ref, dg_ref, lo, w in ((oa_ref, ga_ref, doa_ref, dga_ref, 0, QA),
                                                    (ob_ref, gb_ref, dob_ref, dgb_ref, QA, QB)):
            ov = o_ref[...]
            dyv = dy_ref[:, lo:lo + w]
            r = lax.rsqrt(jnp.mean(ov * ov, axis=-1, keepdims=True) + EPS)
            n = ov * r
            dn = dyv * g_ref[...]
            do_ref[...] = r * (dn - n * jnp.mean(dn * n, axis=-1, keepdims=True))
            dg_ref[...] += jnp.sum(dyv * n, axis=0, keepdims=True)

    ra = pl.BlockSpec((ts, QA), lambda i: (i, 0))
    rb = pl.BlockSpec((ts, QB), lambda i: (i, 0))
    return _call(body, name="group_norm_bwd", grid=(s // ts,),
                 in_specs=[pl.BlockSpec((ts, QA + QB), lambda i: (i, 0)), ra, rb, _vec_spec(QA), _vec_spec(QB)],
                 out_specs=(ra, rb, _vec_spec(QA), _vec_spec(QB)),
                 out_shape=(jax.ShapeDtypeStruct((s, QA), F32), jax.ShapeDtypeStruct((s, QB), F32),
                            jax.ShapeDtypeStruct((1, QA), F32), jax.ShapeDtypeStruct((1, QB), F32)),
                 args=[dy, oa, ob, ga, gb], sem=("arbitrary",))


def _loss_and_grad(y, target):
    s, d = y.shape
    ts = _row_tile(s)

    def body(y_ref, t_ref, l_ref, g_ref):
        @pl.when(pl.program_id(0) == 0)
        def _():
            l_ref[...] = jnp.zeros_like(l_ref)

        err = y_ref[...] - t_ref[...]
        g_ref[...] = err * (1.0 / d)
        row = jnp.mean(err * err, axis=-1, keepdims=True)
        l_ref[...] += 0.5 * jnp.sum(row, axis=0, keepdims=True)

    row = pl.BlockSpec((ts, d), lambda i: (i, 0))
    return _call(body, name="loss_and_grad", grid=(s // ts,), in_specs=[row, row],
                 out_specs=(pl.BlockSpec((1, 1), lambda i: (0, 0)), row),
                 out_shape=(jax.ShapeDtypeStruct((1, 1), F32), jax.ShapeDtypeStruct((s, d), F32)),
                 args=[y, target], sem=("arbitrary",))


def _col_sum(x, name):
    s, n = x.shape
    ts = _row_tile(s)

    def body(x_ref, o_ref):
        @pl.when(pl.program_id(0) == 0)
        def _():
            o_ref[...] = jnp.zeros_like(o_ref)

        o_ref[...] += jnp.sum(x_ref[...].astype(F32), axis=0, keepdims=True)

    return _call(body, name=name, grid=(s // ts,), in_specs=[pl.BlockSpec((ts, n), lambda i: (i, 0))],
                 out_specs=pl.BlockSpec((1, n), lambda i: (0, 0)), out_shape=jax.ShapeDtypeStruct((1, n), F32),
                 args=[x], sem=("arbitrary",))


def _alibi_bias():
    i = np.arange(QROWS)[:, None]
    j = np.arange((QG + BACK_A) * CHUNK)[None, :]
    dist = np.abs(BACK_A * CHUNK + i - j).astype(np.float32)
    dc = j // CHUNK - i // CHUNK
    valid = (dc >= 0) & (dc <= BACK_A)
    slopes = np.array([2.0 ** (-8.0 * (h + 1) / H_A) for h in range(H_A)], dtype=np.float32)
    bias = -slopes[:, None, None] * dist[None]
    return jnp.asarray(np.where(valid[None], bias, np.float32(NEG_INF)).astype(np.float32))


def _rel_index_matrix():
    cc = np.arange(SKEW)
    dist = np.where(cc < SKEW - QROWS, BACK_B * CHUNK - cc, BACK_B * CHUNK + SKEW - cc)
    idx = np.clip(dist, -REL_CLIP, REL_CLIP) + REL_CLIP
    m = np.zeros((SKEW, N_REL), np.float32)
    m[cc, idx] = 1.0
    return jnp.asarray(m)


def _toeplitz_bias(vec):
    lk = (QG + BACK_B) * CHUNK

    def body(v_ref, o_ref):
        xv = jnp.broadcast_to(v_ref[0], (QROWS, SKEW))
        row = lax.broadcasted_iota(jnp.int32, (QROWS, SKEW), 0)
        for bit in range(QROWS.bit_length() - 1):
            xv = jnp.where((row >> bit) & 1 == 1, pltpu.roll(xv, 1 << bit, 1), xv)
        ri = lax.broadcasted_iota(jnp.int32, (QROWS, lk), 0) // CHUNK
        ci = lax.broadcasted_iota(jnp.int32, (QROWS, lk), 1) // CHUNK
        valid = (ci - ri >= 0) & (ci - ri <= BACK_B)
        o_ref[0] = jnp.where(valid, xv[:, :lk], NEG_INF)

    return _call(body, name="toeplitz_bias", grid=(H_B,),
                 in_specs=[pl.BlockSpec((1, 1, SKEW), lambda h: (h, 0, 0))],
                 out_specs=pl.BlockSpec((1, QROWS, lk), lambda h: (h, 0, 0)),
                 out_shape=jax.ShapeDtypeStruct((H_B, QROWS, lk), F32), args=[vec], sem=("parallel",))


def _diagonal_sums(dbias):
    lk = dbias.shape[2]

    def body(d_ref, o_ref):
        xv = jnp.concatenate([d_ref[0], jnp.zeros((QROWS, SKEW - lk), F32)], axis=1)
        row = lax.broadcasted_iota(jnp.int32, (QROWS, SKEW), 0)
        for bit in range(QROWS.bit_length() - 1):
            xv = jnp.where((row >> bit) & 1 == 1, pltpu.roll(xv, SKEW - (1 << bit), 1), xv)
        o_ref[0] = jnp.sum(xv, axis=0, keepdims=True)

    return _call(body, name="diagonal_sums", grid=(H_B,),
                 in_specs=[pl.BlockSpec((1, QROWS, lk), lambda h: (h, 0, 0))],
                 out_specs=pl.BlockSpec((1, 1, SKEW), lambda h: (h, 0, 0)),
                 out_shape=jax.ShapeDtypeStruct((H_B, 1, SKEW), F32), args=[dbias], sem=("parallel",))


def _attn_common(s, n_back, gqa, q_col, k_col, v_col):
    lk = (QG + n_back) * CHUNK
    pad = n_back * CHUNK
    q_spec = pl.BlockSpec((QROWS, LANES), lambda t, g: (g, q_col + t))
    if gqa:
        k_spec = pl.BlockSpec((s, LANES), lambda t, g: (0, k_col))
        v_spec = pl.BlockSpec((s, LANES), lambda t, g: (0, v_col))
    else:
        k_spec = pl.BlockSpec((s, LANES), lambda t, g: (0, k_col + t))
        v_spec = pl.BlockSpec((s, LANES), lambda t, g: (0, v_col + t))
    bias_spec = pl.BlockSpec((2, QROWS, lk), lambda t, g: (t, 0, 0))
    tile_spec = pl.BlockSpec((QROWS, LANES), lambda t, g: (g, t))
    return lk, pad, q_spec, k_spec, v_spec, bias_spec, tile_spec


def _attention_fwd(proj, bias, sinks, *, n_back, gqa, q_col, k_col, v_col, name, carry=None):
    s = proj.shape[0]
    lk, pad, q_spec, k_spec, v_spec, bias_spec, tile_spec = _attn_common(s, n_back, gqa, q_col, k_col, v_col)
    n_t, n_g = 512 // LANES, s // QROWS

    def body(*refs):
        if gqa:
            q_ref, k_ref, v_ref, bias_ref, sink_ref, o_ref, l_ref, kpad, vpad = refs
        else:
            q_ref, k_ref, v_ref, bias_ref, o_ref, l_ref, kpad, vpad = refs
        t, g = pl.program_id(0), pl.program_id(1)

        @pl.when(g == 0)
        def _():
            kpad[0:pad, :] = jnp.zeros((pad, LANES), BF16)
            vpad[0:pad, :] = jnp.zeros((pad, LANES), BF16)
            kpad[pad:, :] = k_ref[...]
            vpad[pad:, :] = v_ref[...]

        start = pl.multiple_of(g * QROWS, QROWS)
        kb = kpad[pl.ds(start, lk), :]
        vb = vpad[pl.ds(start, lk), :]
        half = lax.broadcasted_iota(jnp.int32, (QROWS, LANES), 1) // HEAD_DIM
        col_ok = lax.broadcasted_iota(jnp.int32, (QROWS, lk), 1) >= (n_back - QG * g) * CHUNK
        q = q_ref[...]
        if gqa:
            hk = t // 2
            q_rolled = pltpu.roll(q.astype(F32), HEAD_DIM, 1).astype(BF16)
        outs, lses = [], []
        for e in range(2):
            if gqa:
                kv_half = hk
                src = jnp.where(hk == e, q, q_rolled)
            else:
                kv_half = e
                src = q
            qm = jnp.where(half == kv_half, src, jnp.zeros_like(src))
            sc = _dot_nt(qm, kb) * (HEAD_DIM ** -0.5) + bias_ref[e]
            sc = jnp.where(col_ok, sc, NEG_INF)
            m = jnp.max(sc, axis=-1, keepdims=True)
            if gqa:
                sk = sink_ref[2 * t + e]
                m = jnp.maximum(m, sk)
            p = jnp.exp(sc - m)
            l = jnp.sum(p, axis=-1, keepdims=True)
            if gqa:
                l = l + jnp.exp(sk - m)
            pn = p / l
            outs.append(_dot(pn.astype(BF16), vb))
            lses.append(m + jnp.log(l))
        if gqa:
            same = jnp.where(hk == 0, outs[0], outs[1])
            other = jnp.where(hk == 0, outs[1], outs[0])
            o_ref[...] = jnp.where(half == hk, same, pltpu.roll(other, HEAD_DIM, 1))
        else:
            o_ref[...] = jnp.where(half == 0, outs[0], outs[1])
        l_ref[...] = jnp.where(half == 0, lses[0], lses[1])

    in_specs = [q_spec, k_spec, v_spec, bias_spec] + ([SMEM_SPEC] if gqa else [])
    args = [proj, proj, proj, bias] + ([sinks] if gqa else [])
    o_shape = jax.ShapeDtypeStruct((s, 512), F32)
    return _call(body, name=name, grid=(n_t, n_g), in_specs=in_specs, out_specs=(tile_spec, tile_spec),
                 out_shape=(o_shape, o_shape), args=args,
                 scratch=[pltpu.VMEM((s + pad, LANES), BF16), pltpu.VMEM((s + pad, LANES), BF16)],
                 sem=("arbitrary", "arbitrary"), carry=carry)


def _attention_bwd(proj, bias, sinks, do, lse, *, n_back, gqa, q_col, k_col, v_col, name, carry=None):
    s = proj.shape[0]
    lk, pad, q_spec, k_spec, v_spec, bias_spec, tile_spec = _attn_common(s, n_back, gqa, q_col, k_col, v_col)
    n_t, n_g = 512 // LANES, s // QROWS

    def body(*refs):
        if gqa:
            (q_ref, k_ref, v_ref, bias_ref, sink_ref, do_ref, l_ref,
             dq_ref, dk_ref, dv_ref, dsink_ref, kpad, vpad, dkpad, dvpad) = refs
        else:
            (q_ref, k_ref, v_ref, bias_ref, do_ref, l_ref,
             dq_ref, dk_ref, dv_ref, dbias_ref, kpad, vpad, dkpad, dvpad) = refs
        t, g = pl.program_id(0), pl.program_id(1)

        @pl.when(g == 0)
        def _():
            kpad[0:pad, :] = jnp.zeros((pad, LANES), BF16)
            vpad[0:pad, :] = jnp.zeros((pad, LANES), BF16)
            kpad[pad:, :] = k_ref[...]
            vpad[pad:, :] = v_ref[...]
            if gqa:
                dsink_ref[...] = jnp.zeros_like(dsink_ref)
            else:
                dbias_ref[...] = jnp.zeros_like(dbias_ref)

        @pl.when((g == 0) & (t == 0) if gqa else g == 0)
        def _():
            dkpad[...] = jnp.zeros_like(dkpad)
            dvpad[...] = jnp.zeros_like(dvpad)

        start = pl.multiple_of(g * QROWS, QROWS)
        kb = kpad[pl.ds(start, lk), :]
        vb = vpad[pl.ds(start, lk), :]
        half = lax.broadcasted_iota(jnp.int32, (QROWS, LANES), 1) // HEAD_DIM
        col_ok = lax.broadcasted_iota(jnp.int32, (QROWS, lk), 1) >= (n_back - QG * g) * CHUNK
        q = q_ref[...]
        dov = do_ref[...]
        lv = l_ref[...]
        if gqa:
            hk = t // 2
            q_rolled = pltpu.roll(q.astype(F32), HEAD_DIM, 1).astype(BF16)
            do_rolled = pltpu.roll(dov, HEAD_DIM, 1)
        dqs = []
        dk_acc = jnp.zeros((lk, LANES), F32)
        dv_acc = jnp.zeros((lk, LANES), F32)
        for e in range(2):
            if gqa:
                kv_half = hk
                src = jnp.where(hk == e, q, q_rolled)
                do_src = jnp.where(hk == e, dov, do_rolled)
            else:
                kv_half = e
                src = q
                do_src = dov
            qm = jnp.where(half == kv_half, src, jnp.zeros_like(src))
            dom = jnp.where(half == kv_half, do_src, 0.0).astype(BF16)
            lcol = jnp.max(jnp.where(half == e, lv, -jnp.inf), axis=-1, keepdims=True)
            sc = _dot_nt(qm, kb) * (HEAD_DIM ** -0.5) + bias_ref[e]
            sc = jnp.where(col_ok, sc, NEG_INF)
            pn = jnp.exp(sc - lcol)
            dp = _dot_nt(dom, vb)
            delta = jnp.sum(pn * dp, axis=-1, keepdims=True)
            ds = pn * (dp - delta)
            if gqa:
                p_sink = jnp.exp(sink_ref[2 * t + e] - lcol)
                dsk = -jnp.sum(p_sink * delta, axis=0, keepdims=True)
                dsink_ref[0, e:e + 1, :] += jnp.broadcast_to(dsk, (1, LANES))
            else:
                dbias_ref[e] += ds
            dsb = (ds * (HEAD_DIM ** -0.5)).astype(BF16)
            dqs.append(_dot(dsb, kb))
            dk_acc = dk_acc + _dot_tn(dsb, qm)
            dv_acc = dv_acc + _dot_tn(pn.astype(BF16), dom)
        dkpad[pl.ds(start, lk), :] += dk_acc
        dvpad[pl.ds(start, lk), :] += dv_acc
        if gqa:
            same = jnp.where(hk == 0, dqs[0], dqs[1])
            other = jnp.where(hk == 0, dqs[1], dqs[0])
            dq_ref[...] = jnp.where(half == hk, same, pltpu.roll(other, HEAD_DIM, 1)).astype(BF16)
        else:
            dq_ref[...] = jnp.where(half == 0, dqs[0], dqs[1]).astype(BF16)

        @pl.when((g == n_g - 1) & (t == n_t - 1) if gqa else g == n_g - 1)
        def _():
            dk_ref[...] = dkpad[pad:, :].astype(BF16)
            dv_ref[...] = dvpad[pad:, :].astype(BF16)

    in_specs = [q_spec, k_spec, v_spec, bias_spec] + ([SMEM_SPEC] if gqa else []) + [tile_spec, tile_spec]
    args = [proj, proj, proj, bias] + ([sinks] if gqa else []) + [do, lse]
    if gqa:
        kv_out = pl.BlockSpec((s, LANES), lambda t, g: (0, 0))
        kv_shape = jax.ShapeDtypeStruct((s, LANES), BF16)
        extra_spec = pl.BlockSpec((1, 8, LANES), lambda t, g: (t, 0, 0))
        extra_shape = jax.ShapeDtypeStruct((n_t, 8, LANES), F32)
    else:
        kv_out = pl.BlockSpec((s, LANES), lambda t, g: (0, t))
        kv_shape = jax.ShapeDtypeStruct((s, 512), BF16)
        extra_spec = bias_spec
        extra_shape = jax.ShapeDtypeStruct(bias.shape, F32)
    return _call(body, name=name, grid=(n_t, n_g), in_specs=in_specs,
                 out_specs=(tile_spec, kv_out, kv_out, extra_spec),
                 out_shape=(jax.ShapeDtypeStruct((s, 512), BF16), kv_shape, kv_shape, extra_shape), args=args,
                 scratch=[pltpu.VMEM((s + pad, LANES), BF16), pltpu.VMEM((s + pad, LANES), BF16),
                          pltpu.VMEM((s + pad, LANES), F32), pltpu.VMEM((s + pad, LANES), F32)],
                 sem=("arbitrary", "arbitrary"), carry=carry)


def _sum_slots(r, name):
    _, rows, k = r.shape

    def body(r_ref, o_ref):
        acc = r_ref[0].astype(F32)
        for j in range(1, N_DEV):
            acc = acc + r_ref[j].astype(F32)
        o_ref[...] = acc

    return _call(body, name=name, grid=(k // LANES,),
                 in_specs=[pl.BlockSpec((N_DEV, rows, LANES), lambda i: (0, 0, i))],
                 out_specs=pl.BlockSpec((rows, LANES), lambda i: (0, i)),
                 out_shape=jax.ShapeDtypeStruct((rows, k), F32), args=[r], sem=("parallel",))


def _sum_rows8(g):
    n = g.shape[2]

    def body(g_ref, o_ref):
        acc = g_ref[0]
        for j in range(1, N_DEV):
            acc = acc + g_ref[j]
        o_ref[...] = acc

    return pl.pallas_call(
        body, name="sum_small_grads", in_specs=[VMEM_SPEC], out_specs=VMEM_SPEC,
        out_shape=jax.ShapeDtypeStruct((1, n), F32), compiler_params=_params(),
    )(g)


def _ada_weight_grad(sc_t, dmod_cols):
    d = sc_t.shape[0]
    w = dmod_cols.shape[1]
    td = _pick(d, (256, 128))

    def body(sc_ref, dm_ref, o_ref):
        scv = sc_ref[...]
        dmv = dm_ref[...]
        acc = scv[:, 0:1] * dmv[0:1, :]
        for b in range(1, N_DEV):
            acc = acc + scv[:, b:b + 1] * dmv[b:b + 1, :]
        o_ref[...] = acc

    return _call(body, name="ada_weight_grad", grid=(d // td,),
                 in_specs=[pl.BlockSpec((td, N_DEV), lambda i: (i, 0)), pl.BlockSpec((N_DEV, w), lambda i: (0, 0))],
                 out_specs=pl.BlockSpec((td, w), lambda i: (i, 0)), out_shape=jax.ShapeDtypeStruct((d, w), F32),
                 args=[sc_t, dmod_cols], sem=("parallel",))


def _adamw(w, g, m, v, name):
    rows, cols = w.shape
    tr = _pick(rows, (256, 176, 128, 88, 64)) if rows > 256 else rows

    def body(w_ref, g_ref, m_ref, v_ref, d_ref, nm_ref, nv_ref):
        gv = g_ref[...]
        nm = ADAM_B1 * m_ref[...] + (1.0 - ADAM_B1) * gv
        nv = ADAM_B2 * v_ref[...] + (1.0 - ADAM_B2) * (gv * gv)
        m_hat = nm / (1.0 - ADAM_B1 ** ADAM_STEP)
        v_hat = nv / (1.0 - ADAM_B2 ** ADAM_STEP)
        d_ref[...] = -ADAM_LR * (m_hat / (jnp.sqrt(v_hat) + ADAM_EPS) + ADAM_WD * w_ref[...])
        nm_ref[...] = nm
        nv_ref[...] = nv

    spec = pl.BlockSpec((tr, cols), lambda i: (i, 0))
    shape = jax.ShapeDtypeStruct((rows, cols), F32)
    return _call(body, name=name, grid=(rows // tr,), in_specs=[spec] * 4, out_specs=(spec, spec, spec),
                 out_shape=(shape, shape, shape), args=[w, g, m, v], sem=("parallel",))


SMALL = ("b_ada", "g_pre_ffn1", "g_post_ffn1", "g_pre_mix", "b_in", "sinks_a", "rel_bias_b", "g_grp_a",
         "g_grp_b", "b_out", "g_post_mix", "g_pre_ffn2", "g_post_ffn2")
WEIGHTS = ("w_ada", "b_ada", "g_pre_ffn1", "w_gate1", "w_up1", "w_down1", "g_post_ffn1", "g_pre_mix", "w_in",
           "b_in", "sinks_a", "rel_bias_b", "g_grp_a", "g_grp_b", "w_out", "b_out", "g_post_mix", "g_pre_ffn2",
           "w_gate2", "w_up2", "w_down2", "g_post_ffn2")


def kernel(x, c, w_ada, b_ada, g_pre_ffn1, w_gate1, w_up1, w_down1, g_post_ffn1, g_pre_mix, w_in, b_in, sinks_a, rel_bias_b, g_grp_a, g_grp_b, w_out, b_out, g_post_mix, g_pre_ffn2, w_gate2, w_up2, w_down2, g_post_ffn2, loss_target, m_w_ada, m_b_ada, m_g_pre_ffn1, m_w_gate1, m_w_up1, m_w_down1, m_g_post_ffn1, m_g_pre_mix, m_w_in, m_b_in, m_sinks_a, m_rel_bias_b, m_g_grp_a, m_g_grp_b, m_w_out, m_b_out, m_g_post_mix, m_g_pre_ffn2, m_w_gate2, m_w_up2, m_w_down2, m_g_post_ffn2, v_w_ada, v_b_ada, v_g_pre_ffn1, v_w_gate1, v_w_up1, v_w_down1, v_g_post_ffn1, v_g_pre_mix, v_w_in, v_b_in, v_sinks_a, v_rel_bias_b, v_g_grp_a, v_g_grp_b, v_w_out, v_b_out, v_g_post_mix, v_g_pre_ffn2, v_w_gate2, v_w_up2, v_w_down2, v_g_post_ffn2):
    given = dict(locals())
    weights = {n: given[n] for n in WEIGHTS}
    mom_m = {n: given["m_" + n] for n in WEIGHTS}
    mom_v = {n: given["v_" + n] for n in WEIGHTS}

    me = 4 * lax.axis_index("x") + 2 * lax.axis_index("y") + lax.axis_index("c")
    xs = x[0]
    tgt = loss_target[0]
    d_model = xs.shape[1]
    ada_cols = w_ada.shape[2]

    b_cols = lax.dynamic_slice(b_ada, (0, me * ada_cols), (1, ada_cols))
    sc_all, mod_rows = _ada_forward(c, w_ada[0], b_cols)
    mod = mod_rows.reshape(N_MOD, d_model)
    shift1, scale1, gate1, shift2, scale2, gate2, shift3, scale3, gate3 = (mod[i:i + 1] for i in range(N_MOD))

    sh = {"wg1": w_gate1[0].T, "wu1": w_up1[0].T, "wd1": w_down1[0], "win": w_in[0].T, "wo": w_out[0],
          "wg2": w_gate2[0].T, "wu2": w_up2[0].T, "wd2": w_down2[0]}
    sh = {k: v.astype(BF16) for k, v in sh.items()}

    def gather(*names):
        return _gather_carry([sh[n] for n in names])

    def ffn_forward(xin, g_pre, g_post, shift, scale, gate, wg, wu, wd_name, next_name, tag):
        h = _pre_norm(xin, g_pre, scale, shift, "pre_norm_" + tag)
        (a, b, u), (wd,) = _ffn_up(h, wg, wu, "ffn_up_" + tag, carry=gather(wd_name))
        if next_name is None:
            y, nxt = _mm_nn([(u, wd)], "ffn_down_" + tag, F32), None
        else:
            y, (nxt,) = _mm_nn([(u, wd)], "ffn_down_" + tag, F32, carry=gather(next_name))
        xout = _post_norm_residual(xin, y, g_post, gate, 0.5, "post_norm_" + tag)
        return xout, (xin, h, a, b, u, y), wd, nxt

    wg1, wu1 = _run_carry(gather("wg1", "wu1"), "gather_ffn1")
    x1, saved1, wd1, win = ffn_forward(xs, g_pre_ffn1, g_post_ffn1, shift1, scale1, gate1, wg1, wu1,
                                       "wd1", "win", "ffn1")

    h2 = _pre_norm(x1, g_pre_mix, scale2, shift2, "pre_norm_mix")
    proj, (wo,) = _mm_nt(h2, win, "in_proj", BF16, bias=b_in, carry=gather("wo"))
    bias_a = _alibi_bias()
    rel_m = _rel_index_matrix()
    rel_vec = jnp.dot(rel_bias_b[0], rel_m.T, precision=lax.Precision.HIGHEST)
    bias_b = _toeplitz_bias(rel_vec.reshape(H_B, 1, SKEW))
    sinks = sinks_a[0]
    cfg_a = dict(n_back=BACK_A, gqa=True, q_col=0, k_col=QA // LANES, v_col=(QA + KVA) // LANES)
    cfg_b = dict(n_back=BACK_B, gqa=False, q_col=(QA + 2 * KVA) // LANES, k_col=(QA + 2 * KVA + QB) // LANES,
                 v_col=(QA + 2 * KVA + 2 * QB) // LANES)
    (oa, lse_a), (wg2,) = _attention_fwd(proj, bias_a, sinks, name="attn_a", carry=gather("wg2"), **cfg_a)
    (ob, lse_b), (wu2,) = _attention_fwd(proj, bias_b, None, name="attn_b", carry=gather("wu2"), **cfg_b)
    ycat = _group_norm_cat(oa, ob, g_grp_a, g_grp_b)
    ymix = _mm_nn([(ycat, wo)], "out_proj", F32, bias=b_out)
    x2 = _post_norm_residual(x1, ymix, g_post_mix, gate2, 1.0, "post_norm_mix")

    x3, saved3, wd2, _ = ffn_forward(x2, g_pre_ffn2, g_post_ffn2, shift3, scale3, gate3, wg2, wu2,
                                     "wd2", None, "ffn2")

    loss_part, dx3 = _loss_and_grad(x3, tgt)

    def scatter(*grads):
        return _scatter_carry(list(grads))

    slots = {}

    xin, h, a, b, u, y = saved3
    dy, s1, _ = _post_norm_bwd(dx3, y, g_post_ffn2, gate3, 0.5, "post_norm_bwd_ffn2")
    da, db = _ffn_down_bwd(dy, wd2, a, b, "ffn_down_bwd_ffn2")
    dwd2 = _mm_tn(u, dy, "grad_wd_ffn2")
    dwg2 = _mm_tn(da, h, "grad_wg_ffn2")
    dwu2 = _mm_tn(db, h, "grad_wu_ffn2")
    dh, (slots["wd2"],) = _mm_nn([(da, wg2), (db, wu2)], "ffn_up_bwd_ffn2", F32, carry=scatter(dwd2))
    dx2, s2, s3 = _pre_norm_bwd(dh, xin, g_pre_ffn2, scale3, dx3, "pre_norm_bwd_ffn2")
    sm3 = dict(shift=s3, scale=s2 * g_pre_ffn2, gate=0.5 * g_post_ffn2 * s1,
               g_pre=(1.0 + scale3) * s2, g_post=(0.5 * gate3) * s1)

    dymix, s1m, db_out = _post_norm_bwd(dx2, ymix, g_post_mix, gate2, 1.0, "post_norm_bwd_mix")
    dycat = _mm_nt(dymix, wo, "out_proj_bwd", F32)
    dwo = _mm_tn(ycat, dymix, "grad_wo")
    doa, dob, dg_a, dg_b = _group_norm_bwd(dycat, oa, ob, g_grp_a, g_grp_b)
    (dqa, dka, dva, dsink), (slots["wg2"],) = _attention_bwd(
        proj, bias_a, sinks, doa, lse_a, name="attn_a_bwd", carry=scatter(dwg2), **cfg_a)
    (dqb, dkb, dvb, dbias), (slots["wu2"],) = _attention_bwd(
        proj, bias_b, None, dob, lse_b, name="attn_b_bwd", carry=scatter(dwu2), **cfg_b)
    dproj = jnp.concatenate([dqa, dka, dva, dqb, dkb, dvb], axis=1)
    db_in = _col_sum(dproj, "grad_b_in")
    dwin = _mm_tn(dproj, h2, "grad_win")
    dh2, (slots["wo"],) = _mm_nn([(dproj, win)], "in_proj_bwd", F32, carry=scatter(dwo))
    dx1, s2m, s3m = _pre_norm_bwd(dh2, x1, g_pre_mix, scale2, dx2, "pre_norm_bwd_mix")
    d_rel = jnp.dot(_diagonal_sums(dbias).reshape(H_B, SKEW), rel_m, precision=lax.Precision.HIGHEST)
    d_sinks = dsink[:, :2, 0].reshape(1, H_A)

    xin, h, a, b, u, y = saved1
    dy, s1, _ = _post_norm_bwd(dx1, y, g_post_ffn1, gate1, 0.5, "post_norm_bwd_ffn1")
    (da, db), (slots["win"],) = _ffn_down_bwd(dy, wd1, a, b, "ffn_down_bwd_ffn1", carry=scatter(dwin))
    dwd1 = _mm_tn(u, dy, "grad_wd_ffn1")
    dwg1, (slots["wd1"],) = _mm_tn(da, h, "grad_wg_ffn1", carry=scatter(dwd1))
    dwu1 = _mm_tn(db, h, "grad_wu_ffn1")
    dh, (slots["wg1"],) = _mm_nn([(da, wg1), (db, wu1)], "ffn_up_bwd_ffn1", F32, carry=scatter(dwg1))
    dx0, s2, s3 = _pre_norm_bwd(dh, xin, g_pre_ffn1, scale1, dx1, "pre_norm_bwd_ffn1")
    sm1 = dict(shift=s3, scale=s2 * g_pre_ffn1, gate=0.5 * g_post_ffn1 * s1,
               g_pre=(1.0 + scale1) * s2, g_post=(0.5 * gate1) * s1)
    (slots["wu1"],) = _run_carry(scatter(dwu1), "scatter_wu1")

    gsum = {k: _sum_slots(v, "sum_grad_" + k) for k, v in slots.items()}
    grads = {"w_gate1": gsum["wg1"].T, "w_up1": gsum["wu1"].T, "w_down1": gsum["wd1"], "w_in": gsum["win"].T,
             "w_out": gsum["wo"], "w_gate2": gsum["wg2"].T, "w_up2": gsum["wu2"].T, "w_down2": gsum["wd2"]}

    dmod = jnp.concatenate([sm1["shift"], sm1["scale"], sm1["gate"],
                            s3m, s2m * g_pre_mix, g_post_mix * s1m,
                            sm3["shift"], sm3["scale"], sm3["gate"]], axis=1)
    small_parts = {
        "b_ada": dmod, "g_pre_ffn1": sm1["g_pre"], "g_post_ffn1": sm1["g_post"],
        "g_pre_mix": (1.0 + scale2) * s2m, "b_in": db_in, "sinks_a": d_sinks,
        "rel_bias_b": d_rel.reshape(1, H_B * N_REL), "g_grp_a": dg_a, "g_grp_b": dg_b, "b_out": db_out,
        "g_post_mix": gate2 * s1m, "g_pre_ffn2": sm3["g_pre"], "g_post_ffn2": sm3["g_post"]}
    sizes = [small_parts[n].shape[1] for n in SMALL]
    n_small = sum(sizes)
    n_pad = -(n_small + 1) % LANES
    packed = jnp.concatenate([small_parts[n] for n in SMALL] + [loss_part, jnp.zeros((1, n_pad), F32)], axis=1)
    gathered = _all_gather_small(packed)
    small_sum = _sum_rows8(gathered)
    loss = small_sum[0, n_small]
    dmod_cols = lax.dynamic_slice(gathered.reshape(N_DEV, n_small + 1 + n_pad), (0, me * ada_cols),
                                  (N_DEV, ada_cols))
    grads["w_ada"] = _ada_weight_grad(sc_all.reshape(N_DEV, d_model).T, dmod_cols)

    out_g, out_d, out_m, out_v = {}, {}, {}, {}
    for n in ("w_ada", "w_gate1", "w_up1", "w_down1", "w_in", "w_out", "w_gate2", "w_up2", "w_down2"):
        d_, m_, v_ = _adamw(weights[n][0], grads[n], mom_m[n][0], mom_v[n][0], "adamw_" + n)
        out_g[n], out_d[n], out_m[n], out_v[n] = grads[n][None], d_[None], m_[None], v_[None]

    def pack(tree):
        return jnp.concatenate([tree[n].reshape(1, -1) for n in SMALL], axis=1)

    g_small = small_sum[:, :n_small]
    d_s, m_s, v_s = _adamw(pack(weights), g_small, pack(mom_m), pack(mom_v), "adamw_small")
    off = 0
    for n, size in zip(SMALL, sizes):
        shape = weights[n].shape
        out_g[n] = g_small[:, off:off + size].reshape(shape)
        out_d[n] = d_s[:, off:off + size].reshape(shape)
        out_m[n] = m_s[:, off:off + size].reshape(shape)
        out_v[n] = v_s[:, off:off + size].reshape(shape)
        off += size

    return (loss, dx0[None], *[out_g[n] for n in WEIGHTS], *[out_d[n] for n in WEIGHTS],
            *[out_m[n] for n in WEIGHTS], *[out_v[n] for n in WEIGHTS])
```

```python
import numpy as np
import jax
import jax.numpy as jnp
from jax import lax
from jax.experimental import pallas as pl
from jax.experimental.pallas import tpu as pltpu

F32 = jnp.float32
BF16 = jnp.bfloat16
MESH = pl.DeviceIdType.MESH
ANY = pl.BlockSpec(memory_space=pl.ANY)
VMEM_SPEC = pl.BlockSpec(memory_space=pltpu.VMEM)
SMEM_SPEC = pl.BlockSpec(memory_space=pltpu.SMEM)

N_DEV = 8
CHUNK = 64
HEAD_DIM = 64
LANES = 128
H_A, KV_A, H_B = 8, 2, 8
BACK_A, BACK_B = 2, 8
REL_CLIP = 128
N_REL = 2 * REL_CLIP + 1
QA, KVA, QB = H_A * HEAD_DIM, KV_A * HEAD_DIM, H_B * HEAD_DIM
D_IN = QA + 2 * KVA + 3 * QB
N_MOD = 9
EPS = 1e-6
NEG_INF = -1e30
QG = 4
QROWS = QG * CHUNK
SKEW = 1024
ADAM_LR, ADAM_B1, ADAM_B2, ADAM_EPS, ADAM_WD, ADAM_STEP = 0.001, 0.9, 0.999, 1e-08, 0.01, 10
VMEM_LIMIT = 56 * 2 ** 20


def _pick(n, cands):
    for c in cands:
        if n % c == 0:
            return c
    return n


def _params(sem=None):
    return pltpu.CompilerParams(dimension_semantics=sem, vmem_limit_bytes=VMEM_LIMIT)


def _dot_nt(a, b):
    return lax.dot_general(a, b, (((1,), (1,)), ((), ())), preferred_element_type=F32)


def _dot_tn(a, b):
    return lax.dot_general(a, b, (((0,), (0,)), ((), ())), preferred_element_type=F32)


def _dot(a, b):
    return jnp.dot(a, b, preferred_element_type=F32)


def _sigmoid(a):
    return 1.0 / (1.0 + jnp.exp(-a))


def _mesh_pos():
    return lax.axis_index("x"), lax.axis_index("y"), lax.axis_index("c")


def _peer(x, y, c, r):
    px = 1 - x if r & 4 else x
    py = 1 - y if r & 2 else y
    pc = 1 - c if r & 1 else c
    return px, py, pc


class _Carry:
    def __init__(self, ins, out_shapes, scratch, start, finish):
        self.ins, self.out_shapes, self.scratch = list(ins), list(out_shapes), list(scratch)
        self.start, self.finish = start, finish


def _call(body, *, name, grid, in_specs, out_specs, out_shape, args, scratch=(), sem=None, carry=None):
    single = not isinstance(out_shape, (tuple, list))
    out_specs = (out_specs,) if single else tuple(out_specs)
    out_shape = (out_shape,) if single else tuple(out_shape)
    if carry is None:
        res = pl.pallas_call(body, name=name, grid=grid, in_specs=list(in_specs), out_specs=out_specs,
                             out_shape=out_shape, scratch_shapes=list(scratch), compiler_params=_params(sem))(*args)
        return res[0] if single else res
    n_in, n_out, n_s = len(in_specs), len(out_shape), len(scratch)
    ci, co = len(carry.ins), len(carry.out_shapes)

    def wrapped(*refs):
        ins, cins = refs[:n_in], refs[n_in:n_in + ci]
        outs = refs[n_in + ci:n_in + ci + n_out]
        couts = refs[n_in + ci + n_out:n_in + ci + n_out + co]
        scr = refs[n_in + ci + n_out + co:n_in + ci + n_out + co + n_s]
        cscr = refs[n_in + ci + n_out + co + n_s:]
        first, last = None, None
        for ax, n in enumerate(grid):
            f, l = pl.program_id(ax) == 0, pl.program_id(ax) == n - 1
            first = f if first is None else first & f
            last = l if last is None else last & l
        pl.when(first)(lambda: carry.start(cins, couts, cscr))
        body(*ins, *outs, *scr)
        pl.when(last)(lambda: carry.finish(cins, couts, cscr))

    res = pl.pallas_call(
        wrapped, name=name, grid=grid, in_specs=list(in_specs) + [ANY] * ci, out_specs=out_specs + (ANY,) * co,
        out_shape=out_shape + tuple(carry.out_shapes), scratch_shapes=list(scratch) + carry.scratch,
        compiler_params=_params(("arbitrary",) * len(grid)))(*args, *carry.ins)
    main = res[:n_out]
    return (main[0] if single else main), res[n_out:]


def _run_carry(carry, name):
    ci, co = len(carry.ins), len(carry.out_shapes)

    def body(*refs):
        carry.start(refs[:ci], refs[ci:ci + co], refs[ci + co:])
        carry.finish(refs[:ci], refs[ci:ci + co], refs[ci + co:])

    return pl.pallas_call(body, name=name, in_specs=[ANY] * ci, out_specs=(ANY,) * co,
                          out_shape=tuple(carry.out_shapes), scratch_shapes=carry.scratch,
                          compiler_params=_params())(*carry.ins)


def _gather_carry(shards):
    n_w = len(shards)
    rows = [s.shape[0] for s in shards]

    def plan(ins, outs, scr):
        send_sems, recv_sems, local_sems = scr
        x, y, c = _mesh_pos()
        me, sibling = (x, y, c), (x, y, 1 - c)
        chips = [(1 - x, y), (x, 1 - y), (1 - x, 1 - y)]

        def block(w, dev):
            start = pl.multiple_of((4 * dev[0] + 2 * dev[1] + dev[2]) * rows[w], 16)
            return outs[w].at[pl.ds(start, rows[w]), :]

        def copy(w, k, dev, to, src=None):
            return pltpu.make_async_remote_copy(
                src_ref=block(w, dev) if src is None else src, dst_ref=block(w, dev),
                send_sem=send_sems.at[w, k], recv_sem=recv_sems.at[w, k], device_id=to, device_id_type=MESH)

        mine = [pltpu.make_async_copy(ins[w], block(w, me), local_sems.at[w]) for w in range(n_w)]
        first = []
        for j, chip in enumerate(chips):
            first += [copy(w, 1 + j, me, (*chip, c), src=ins[w]) for w in range(n_w)]
        first += [copy(w, 0, me, sibling, src=ins[w]) for w in range(n_w)]
        return c, me, sibling, chips, copy, mine, first

    def start(ins, outs, scr):
        _, _, _, _, _, mine, first = plan(ins, outs, scr)
        for cp in mine + first:
            cp.start()

    def finish(ins, outs, scr):
        c, me, sibling, chips, copy, mine, first = plan(ins, outs, scr)
        passed = []
        for j, chip in enumerate(chips):
            for w in range(n_w):
                copy(w, 1 + j, (*chip, c), me).wait_recv()
                cp = copy(w, 4 + j, (*chip, c), sibling)
                cp.start()
                passed.append(cp)
        for w in range(n_w):
            copy(w, 0, sibling, me).wait_recv()
        for j, chip in enumerate(chips):
            for w in range(n_w):
                copy(w, 4 + j, (*chip, 1 - c), me).wait_recv()
        for cp in first + passed:
            cp.wait_send()
        for cp in mine:
            cp.wait()

    return _Carry(
        shards, [jax.ShapeDtypeStruct((N_DEV * s.shape[0], s.shape[1]), s.dtype) for s in shards],
        [pltpu.SemaphoreType.DMA((n_w, N_DEV - 1)), pltpu.SemaphoreType.DMA((n_w, N_DEV - 1)),
         pltpu.SemaphoreType.DMA((n_w,))], start, finish)


def _scatter_carry(parts):
    n_w = len(parts)
    n_chip = N_DEV // 2
    rows = [g.shape[0] // n_chip for g in parts]

    def plan(ins, outs, scr):
        send_sems, recv_sems, local_sems = scr
        x, y, c = _mesh_pos()

        def src(w, chip_index):
            return ins[w].at[pl.ds(pl.multiple_of(chip_index * rows[w], 16), rows[w]), :]

        mine = [pltpu.make_async_copy(src(w, 2 * x + y), outs[w].at[0], local_sems.at[w]) for w in range(n_w)]
        copies = []
        for r in (3, 2, 1):
            px, py, _ = _peer(x, y, c, 2 * r)
            for w in range(n_w):
                copies.append(pltpu.make_async_remote_copy(
                    src_ref=src(w, 2 * px + py), dst_ref=outs[w].at[r], send_sem=send_sems.at[w, r - 1],
                    recv_sem=recv_sems.at[w, r - 1], device_id=(px, py, c), device_id_type=MESH))
        return mine, copies

    def start(ins, outs, scr):
        mine, copies = plan(ins, outs, scr)
        for cp in mine + copies:
            cp.start()

    def finish(ins, outs, scr):
        mine, copies = plan(ins, outs, scr)
        for cp in copies:
            cp.wait_recv()
        for cp in copies:
            cp.wait_send()
        for cp in mine:
            cp.wait()

    return _Carry(
        parts, [jax.ShapeDtypeStruct((n_chip, r, g.shape[1]), g.dtype) for r, g in zip(rows, parts)],
        [pltpu.SemaphoreType.DMA((n_w, n_chip - 1)), pltpu.SemaphoreType.DMA((n_w, n_chip - 1)),
         pltpu.SemaphoreType.DMA((n_w,))], start, finish)


def _ada_forward(c_row, w_ada, b_cols):
    d = c_row.shape[1]
    wcols = w_ada.shape[1]

    def body(c_ref, w_ref, b_ref, sc_ref, mod_ref, rows_ref, send_sems, recv_sems):
        x, y, c = _mesh_pos()
        me = 4 * x + 2 * y + c
        cv = c_ref[...]
        sc_ref[me] = cv * _sigmoid(cv)

        sends = []
        for r in range(1, N_DEV):
            px, py, pc = _peer(x, y, c, r)
            cp = pltpu.make_async_remote_copy(
                src_ref=sc_ref.at[me], dst_ref=sc_ref.at[me], send_sem=send_sems.at[0, r - 1],
                recv_sem=recv_sems.at[0, r - 1], device_id=(px, py, pc), device_id_type=MESH)
            cp.start()
            sends.append(cp)
        for r in range(1, N_DEV):
            px, py, pc = _peer(x, y, c, r)
            pid = 4 * px + 2 * py + pc
            pltpu.make_async_remote_copy(
                src_ref=sc_ref.at[pid], dst_ref=sc_ref.at[pid], send_sem=send_sems.at[0, r - 1],
                recv_sem=recv_sems.at[0, r - 1], device_id=(px, py, pc), device_id_type=MESH).wait_recv()
        for cp in sends:
            cp.wait_send()

        sc_all = jnp.concatenate([sc_ref[j] for j in range(N_DEV)], axis=0)
        rows = _dot(sc_all.astype(BF16), w_ref[...].astype(BF16)) + b_ref[...]
        for j in range(N_DEV):
            rows_ref[j] = rows[j:j + 1, :]
        mod_ref[me] = rows_ref[me]

        sends = []
        for r in range(1, N_DEV):
            px, py, pc = _peer(x, y, c, r)
            pid = 4 * px + 2 * py + pc
            cp = pltpu.make_async_remote_copy(
                src_ref=rows_ref.at[pid], dst_ref=mod_ref.at[me], send_sem=send_sems.at[1, r - 1],
                recv_sem=recv_sems.at[1, r - 1], device_id=(px, py, pc), device_id_type=MESH)
            cp.start()
            sends.append(cp)
        for r in range(1, N_DEV):
            px, py, pc = _peer(x, y, c, r)
            pid = 4 * px + 2 * py + pc
            pltpu.make_async_remote_copy(
                src_ref=rows_ref.at[pid], dst_ref=mod_ref.at[pid], send_sem=send_sems.at[1, r - 1],
                recv_sem=recv_sems.at[1, r - 1], device_id=(px, py, pc), device_id_type=MESH).wait_recv()
        for cp in sends:
            cp.wait_send()

    return pl.pallas_call(
        body, name="ada_forward",
        out_shape=(jax.ShapeDtypeStruct((N_DEV, 1, d), F32), jax.ShapeDtypeStruct((N_DEV, 1, wcols), F32)),
        in_specs=[VMEM_SPEC, VMEM_SPEC, VMEM_SPEC], out_specs=(VMEM_SPEC, VMEM_SPEC),
        scratch_shapes=[pltpu.VMEM((N_DEV, 1, wcols), F32), pltpu.SemaphoreType.DMA((2, N_DEV - 1)),
                        pltpu.SemaphoreType.DMA((2, N_DEV - 1))],
        compiler_params=_params(),
    )(c_row, w_ada, b_cols)


def _all_gather_small(v):
    n = v.shape[1]

    def body(v_ref, out_ref, send_sems, recv_sems):
        x, y, c = _mesh_pos()
        me = 4 * x + 2 * y + c
        out_ref[me] = v_ref[...]
        sends = []
        for r in range(1, N_DEV):
            px, py, pc = _peer(x, y, c, r)
            cp = pltpu.make_async_remote_copy(
                src_ref=v_ref, dst_ref=out_ref.at[me], send_sem=send_sems.at[r - 1],
                recv_sem=recv_sems.at[r - 1], device_id=(px, py, pc), device_id_type=MESH)
            cp.start()
            sends.append(cp)
        for r in range(1, N_DEV):
            px, py, pc = _peer(x, y, c, r)
            pid = 4 * px + 2 * py + pc
            pltpu.make_async_remote_copy(
                src_ref=v_ref, dst_ref=out_ref.at[pid], send_sem=send_sems.at[r - 1],
                recv_sem=recv_sems.at[r - 1], device_id=(px, py, pc), device_id_type=MESH).wait_recv()
        for cp in sends:
            cp.wait_send()

    return pl.pallas_call(
        body, name="all_gather_small",
        out_shape=jax.ShapeDtypeStruct((N_DEV, 1, n), F32),
        in_specs=[VMEM_SPEC], out_specs=VMEM_SPEC,
        scratch_shapes=[pltpu.SemaphoreType.DMA((N_DEV - 1,)), pltpu.SemaphoreType.DMA((N_DEV - 1,))],
        compiler_params=_params(),
    )(v)


def _mm_nt(a, b, name, out_dtype, bias=None, carry=None):
    m, k = a.shape
    n = b.shape[0]
    tm = _pick(m, (512, 256, 128))
    tn = _pick(n, (1408, 1152, 1024, 768, 512, 256, 128))

    def body(*refs):
        acc = _dot_nt(refs[0][...], refs[1][...])
        if bias is not None:
            acc = acc + refs[2][...]
        refs[-1][...] = acc.astype(out_dtype)

    in_specs = [pl.BlockSpec((tm, k), lambda i, j: (i, 0)), pl.BlockSpec((tn, k), lambda i, j: (j, 0))]
    args = [a, b]
    if bias is not None:
        in_specs.append(pl.BlockSpec((1, tn), lambda i, j: (0, j)))
        args.append(bias)
    return _call(body, name=name, grid=(m // tm, n // tn), in_specs=in_specs,
                 out_specs=pl.BlockSpec((tm, tn), lambda i, j: (i, j)),
                 out_shape=jax.ShapeDtypeStruct((m, n), out_dtype), args=args,
                 sem=("parallel", "parallel"), carry=carry)


def _mm_nn(pairs, name, out_dtype, bias=None, carry=None):
    m, k = pairs[0][0].shape
    n = pairs[0][1].shape[1]
    tm = _pick(m, (512, 256, 128))
    tk = _pick(k, (1408, 1152, 1024, 768, 512, 256, 128))
    nk = k // tk
    n_p = len(pairs)

    def body(*refs):
        o_ref, acc_ref = refs[-2], refs[-1]
        kk = pl.program_id(1)

        @pl.when(kk == 0)
        def _():
            acc_ref[...] = jnp.zeros_like(acc_ref)

        for p in range(n_p):
            acc_ref[...] += _dot(refs[2 * p][...], refs[2 * p + 1][...])

        @pl.when(kk == nk - 1)
        def _():
            acc = acc_ref[...]
            if bias is not None:
                acc = acc + refs[2 * n_p][...]
            o_ref[...] = acc.astype(out_dtype)

    in_specs, args = [], []
    for a, b in pairs:
        in_specs += [pl.BlockSpec((tm, tk), lambda i, kk: (i, kk)), pl.BlockSpec((tk, n), lambda i, kk: (kk, 0))]
        args += [a, b]
    if bias is not None:
        in_specs.append(pl.BlockSpec((1, n), lambda i, kk: (0, 0)))
        args.append(bias)
    return _call(body, name=name, grid=(m // tm, nk), in_specs=in_specs,
                 out_specs=pl.BlockSpec((tm, n), lambda i, kk: (i, 0)),
                 out_shape=jax.ShapeDtypeStruct((m, n), out_dtype), args=args,
                 scratch=[pltpu.VMEM((tm, n), F32)], sem=("parallel", "arbitrary"), carry=carry)


def _mm_tn(a, b, name, out_dtype=BF16, carry=None):
    k, m = a.shape
    n = b.shape[1]
    tm = _pick(m, (1408, 1152, 1024, 768, 512, 256, 128))
    tk = _pick(k, (512, 256, 128))
    nk = k // tk

    def body(a_ref, b_ref, o_ref, acc_ref):
        kk = pl.program_id(1)

        @pl.when(kk == 0)
        def _():
            acc_ref[...] = jnp.zeros_like(acc_ref)

        acc_ref[...] += _dot_tn(a_ref[...], b_ref[...])

        @pl.when(kk == nk - 1)
        def _():
            o_ref[...] = acc_ref[...].astype(out_dtype)

    return _call(body, name=name, grid=(m // tm, nk),
                 in_specs=[pl.BlockSpec((tk, tm), lambda i, kk: (kk, i)), pl.BlockSpec((tk, n), lambda i, kk: (kk, 0))],
                 out_specs=pl.BlockSpec((tm, n), lambda i, kk: (i, 0)),
                 out_shape=jax.ShapeDtypeStruct((m, n), out_dtype), args=[a, b],
                 scratch=[pltpu.VMEM((tm, n), F32)], sem=("parallel", "arbitrary"), carry=carry)


def _mm_tn_pair(a, b, name, carry=None):
    k, m = a.shape
    n = b.shape[1]
    rows = m // N_DEV
    n_chip = N_DEV // 2
    tm = 4 * rows
    tk = _pick(k, (512, 256, 128))
    nk = k // tk

    def body(a_ref, b_ref, p_ref, acc_ref, keep_ref, send_ref, land_ref, send_sems, recv_sems):
        i, kk = pl.program_id(0), pl.program_id(1)
        x, y, c = _mesh_pos()

        def push(chip):
            return pltpu.make_async_remote_copy(
                src_ref=send_ref.at[chip], dst_ref=land_ref.at[chip], send_sem=send_sems.at[chip],
                recv_sem=recv_sems.at[chip], device_id=(x, y, 1 - c), device_id_type=MESH)

        @pl.when(kk == 0)
        def _():
            acc_ref[...] = jnp.zeros_like(acc_ref)

        acc_ref[...] += _dot_tn(a_ref[...], b_ref[...])

        for t in range(2):
            @pl.when((kk == nk - 1) & (i == t))
            def _(t=t):
                for ob in range(4):
                    chip, core = 2 * t + ob // 2, ob % 2
                    blk = acc_ref[ob * rows:(ob + 1) * rows, :]

                    @pl.when(c == core)
                    def _(chip=chip, blk=blk):
                        keep_ref[chip] = blk

                    @pl.when(c != core)
                    def _(chip=chip, blk=blk):
                        send_ref[chip] = blk.astype(BF16)
                        push(chip).start()

        @pl.when((kk == nk - 1) & (i == 1))
        def _():
            for chip in range(n_chip):
                push(chip).wait_recv()
                p_ref[chip * rows:(chip + 1) * rows, :] = (
                    keep_ref[chip] + land_ref[chip].astype(F32)).astype(BF16)
            for chip in range(n_chip):
                push(chip).wait_send()

    return _call(body, name=name, grid=(2, nk),
                 in_specs=[pl.BlockSpec((tk, tm), lambda i, kk: (kk, i)), pl.BlockSpec((tk, n), lambda i, kk: (kk, 0))],
                 out_specs=pl.BlockSpec((n_chip * rows, n), lambda i, kk: (0, 0)),
                 out_shape=jax.ShapeDtypeStruct((n_chip * rows, n), BF16), args=[a, b],
                 scratch=[pltpu.VMEM((tm, n), F32), pltpu.VMEM((n_chip, rows, n), F32),
                          pltpu.VMEM((n_chip, rows, n), BF16), pltpu.VMEM((n_chip, rows, n), BF16),
                          pltpu.SemaphoreType.DMA((n_chip,)), pltpu.SemaphoreType.DMA((n_chip,))],
                 sem=("arbitrary", "arbitrary"), carry=carry)


def _ffn_up(h, wg_t, wu_t, name, carry=None):
    s, d = h.shape
    f = wg_t.shape[0]
    tm = _pick(s, (512, 256, 128))
    tf = _pick(f, (1408, 1024, 512, 256, 128))

    def body(h_ref, wg_ref, wu_ref, a_ref, b_ref, u_ref):
        hh = h_ref[...]
        a = _dot_nt(hh, wg_ref[...])
        b = _dot_nt(hh, wu_ref[...])
        a_ref[...] = a.astype(BF16)
        b_ref[...] = b.astype(BF16)
        u_ref[...] = ((a * _sigmoid(a)) * b).astype(BF16)

    w_spec = pl.BlockSpec((tf, d), lambda i, j: (j, 0))
    o_spec = pl.BlockSpec((tm, tf), lambda i, j: (i, j))
    o_shape = jax.ShapeDtypeStruct((s, f), BF16)
    return _call(body, name=name, grid=(s // tm, f // tf),
                 in_specs=[pl.BlockSpec((tm, d), lambda i, j: (i, 0)), w_spec, w_spec],
                 out_specs=(o_spec, o_spec, o_spec), out_shape=(o_shape, o_shape, o_shape),
                 args=[h, wg_t, wu_t], sem=("parallel", "parallel"), carry=carry)


def _ffn_down_bwd(dy, wd, a, b, name, carry=None):
    s, d = dy.shape
    f = wd.shape[0]
    tm = _pick(s, (512, 256, 128))
    tf = _pick(f, (1408, 1024, 512, 256, 128))

    def body(dy_ref, wd_ref, a_ref, b_ref, da_ref, db_ref):
        du = _dot_nt(dy_ref[...], wd_ref[...])
        a = a_ref[...].astype(F32)
        b = b_ref[...].astype(F32)
        sig = _sigmoid(a)
        da_ref[...] = (du * b * (sig * (1.0 + a * (1.0 - sig)))).astype(BF16)
        db_ref[...] = (du * (a * sig)).astype(BF16)

    t_spec = pl.BlockSpec((tm, tf), lambda i, j: (i, j))
    o_shape = jax.ShapeDtypeStruct((s, f), BF16)
    return _call(body, name=name, grid=(s // tm, f // tf),
                 in_specs=[pl.BlockSpec((tm, d), lambda i, j: (i, 0)), pl.BlockSpec((tf, d), lambda i, j: (j, 0)),
                           t_spec, t_spec],
                 out_specs=(t_spec, t_spec), out_shape=(o_shape, o_shape), args=[dy, wd, a, b],
                 sem=("parallel", "parallel"), carry=carry)


def _row_tile(s):
    return _pick(s, (256, 128, 64))


def _vec_spec(d):
    return pl.BlockSpec((1, d), lambda i: (0, 0))


def _pre_norm(x, g, scale, shift, name):
    s, d = x.shape
    ts = _row_tile(s)

    def body(x_ref, g_ref, sc_ref, sh_ref, h_ref):
        xv = x_ref[...]
        r = lax.rsqrt(jnp.mean(xv * xv, axis=-1, keepdims=True) + EPS)
        h_ref[...] = (((xv * r) * g_ref[...]) * (1.0 + sc_ref[...]) + sh_ref[...]).astype(BF16)

    row = pl.BlockSpec((ts, d), lambda i: (i, 0))
    return _call(body, name=name, grid=(s // ts,), in_specs=[row, _vec_spec(d), _vec_spec(d), _vec_spec(d)],
                 out_specs=row, out_shape=jax.ShapeDtypeStruct((s, d), BF16), args=[x, g, scale, shift],
                 sem=("parallel",))


def _post_norm_residual(x, y, g, gate, weight, name):
    s, d = x.shape
    ts = _row_tile(s)

    def body(x_ref, y_ref, g_ref, gate_ref, o_ref):
        yv = y_ref[...]
        r = lax.rsqrt(jnp.mean(yv * yv, axis=-1, keepdims=True) + EPS)
        o_ref[...] = x_ref[...] + (weight * gate_ref[...]) * ((yv * r) * g_ref[...])

    row = pl.BlockSpec((ts, d), lambda i: (i, 0))
    return _call(body, name=name, grid=(s // ts,), in_specs=[row, row, _vec_spec(d), _vec_spec(d)],
                 out_specs=row, out_shape=jax.ShapeDtypeStruct((s, d), F32), args=[x, y, g, gate],
                 sem=("parallel",))


def _post_norm_bwd(dout, y, g, gate, weight, name):
    s, d = y.shape
    ts = _row_tile(s)

    def body(do_ref, y_ref, g_ref, gate_ref, dy_ref, s1_ref, cs_ref):
        @pl.when(pl.program_id(0) == 0)
        def _():
            s1_ref[...] = jnp.zeros_like(s1_ref)
            cs_ref[...] = jnp.zeros_like(cs_ref)

        yv = y_ref[...]
        do = do_ref[...]
        r = lax.rsqrt(jnp.mean(yv * yv, axis=-1, keepdims=True) + EPS)
        yn = yv * r
        dyn = do * ((weight * gate_ref[...]) * g_ref[...])
        dy = r * (dyn - yn * jnp.mean(dyn * yn, axis=-1, keepdims=True))
        dy_ref[...] = dy.astype(BF16)
        s1_ref[...] += jnp.sum(do * yn, axis=0, keepdims=True)
        cs_ref[...] += jnp.sum(dy, axis=0, keepdims=True)

    row = pl.BlockSpec((ts, d), lambda i: (i, 0))
    vec = jax.ShapeDtypeStruct((1, d), F32)
    return _call(body, name=name, grid=(s // ts,), in_specs=[row, row, _vec_spec(d), _vec_spec(d)],
                 out_specs=(row, _vec_spec(d), _vec_spec(d)),
                 out_shape=(jax.ShapeDtypeStruct((s, d), BF16), vec, vec), args=[dout, y, g, gate],
                 sem=("arbitrary",))


def _pre_norm_bwd(dh, x, g, scale, dres, name):
    s, d = x.shape
    ts = _row_tile(s)

    def body(dh_ref, x_ref, g_ref, sc_ref, dr_ref, dx_ref, s2_ref, s3_ref):
        @pl.when(pl.program_id(0) == 0)
        def _():
            s2_ref[...] = jnp.zeros_like(s2_ref)
            s3_ref[...] = jnp.zeros_like(s3_ref)

        xv = x_ref[...]
        dh = dh_ref[...]
        r = lax.rsqrt(jnp.mean(xv * xv, axis=-1, keepdims=True) + EPS)
        n = xv * r
        dn = dh * (g_ref[...] * (1.0 + sc_ref[...]))
        dx_ref[...] = dr_ref[...] + r * (dn - n * jnp.mean(dn * n, axis=-1, keepdims=True))
        s2_ref[...] += jnp.sum(dh * n, axis=0, keepdims=True)
        s3_ref[...] += jnp.sum(dh, axis=0, keepdims=True)

    row = pl.BlockSpec((ts, d), lambda i: (i, 0))
    vec = jax.ShapeDtypeStruct((1, d), F32)
    return _call(body, name=name, grid=(s // ts,), in_specs=[row, row, _vec_spec(d), _vec_spec(d), row],
                 out_specs=(row, _vec_spec(d), _vec_spec(d)),
                 out_shape=(jax.ShapeDtypeStruct((s, d), F32), vec, vec), args=[dh, x, g, scale, dres],
                 sem=("arbitrary",))


def _group_norm_cat(oa, ob, ga, gb):
    s = oa.shape[0]
    ts = _row_tile(s)

    def body(oa_ref, ob_ref, ga_ref, gb_ref, y_ref):
        for o_ref, g_ref, lo, w in ((oa_ref, ga_ref, 0, QA), (ob_ref, gb_ref, QA, QB)):
            ov = o_ref[...]
            r = lax.rsqrt(jnp.mean(ov * ov, axis=-1, keepdims=True) + EPS)
            y_ref[:, lo:lo + w] = ((ov * r) * g_ref[...]).astype(BF16)

    return _call(body, name="group_norm_cat", grid=(s // ts,),
                 in_specs=[pl.BlockSpec((ts, QA), lambda i: (i, 0)), pl.BlockSpec((ts, QB), lambda i: (i, 0)),
                           _vec_spec(QA), _vec_spec(QB)],
                 out_specs=pl.BlockSpec((ts, QA + QB), lambda i: (i, 0)),
                 out_shape=jax.ShapeDtypeStruct((s, QA + QB), BF16), args=[oa, ob, ga, gb], sem=("parallel",))


def _group_norm_bwd(dy, oa, ob, ga, gb):
    s = oa.shape[0]
    ts = _row_tile(s)

    def body(dy_ref, oa_ref, ob_ref, ga_ref, gb_ref, doa_ref, dob_ref, dga_ref, dgb_ref):
        @pl.when(pl.program_id(0) == 0)
        def _():
            dga_ref[...] = jnp.zeros_like(dga_ref)
            dgb_ref[...] = jnp.zeros_like(dgb_ref)

        for o_ref, g_ref, do_ref, dg_ref, lo, w in ((oa_ref, ga_ref, doa_ref, dga_ref, 0, QA),
                                                    (ob_ref, gb_ref, dob_ref, dgb_ref, QA, QB)):
            ov = o_ref[...]
            dyv = dy_ref[:, lo:lo + w]
            r = lax.rsqrt(jnp.mean(ov * ov, axis=-1, keepdims=True) + EPS)
            n = ov * r
            dn = dyv * g_ref[...]
            do_ref[...] = r * (dn - n * jnp.mean(dn * n, axis=-1, keepdims=True))
            dg_ref[...] += jnp.sum(dyv * n, axis=0, keepdims=True)

    ra = pl.BlockSpec((ts, QA), lambda i: (i, 0))
    rb = pl.BlockSpec((ts, QB), lambda i: (i, 0))
    return _call(body, name="group_norm_bwd", grid=(s // ts,),
                 in_specs=[pl.BlockSpec((ts, QA + QB), lambda i: (i, 0)), ra, rb, _vec_spec(QA), _vec_spec(QB)],
                 out_specs=(ra, rb, _vec_spec(QA), _vec_spec(QB)),
                 out_shape=(jax.ShapeDtypeStruct((s, QA), F32), jax.ShapeDtypeStruct((s, QB), F32),
                            jax.ShapeDtypeStruct((1, QA), F32), jax.ShapeDtypeStruct((1, QB), F32)),
                 args=[dy, oa, ob, ga, gb], sem=("arbitrary",))


def _loss_and_grad(y, target):
    s, d = y.shape
    ts = _row_tile(s)

    def body(y_ref, t_ref, l_ref, g_ref):
        @pl.when(pl.program_id(0) == 0)
        def _():
            l_ref[...] = jnp.zeros_like(l_ref)

        err = y_ref[...] - t_ref[...]
        g_ref[...] = err * (1.0 / d)
        row = jnp.mean(err * err, axis=-1, keepdims=True)
        l_ref[...] += 0.5 * jnp.sum(row, axis=0, keepdims=True)

    row = pl.BlockSpec((ts, d), lambda i: (i, 0))
    return _call(body, name="loss_and_grad", grid=(s // ts,), in_specs=[row, row],
                 out_specs=(pl.BlockSpec((1, 1), lambda i: (0, 0)), row),
                 out_shape=(jax.ShapeDtypeStruct((1, 1), F32), jax.ShapeDtypeStruct((s, d), F32)),
                 args=[y, target], sem=("arbitrary",))


def _col_sum(x, name):
    s, n = x.shape
    ts = _row_tile(s)

    def body(x_ref, o_ref):
        @pl.when(pl.program_id(0) == 0)
        def _():
            o_ref[...] = jnp.zeros_like(o_ref)

        o_ref[...] += jnp.sum(x_ref[...].astype(F32), axis=0, keepdims=True)

    return _call(body, name=name, grid=(s // ts,), in_specs=[pl.BlockSpec((ts, n), lambda i: (i, 0))],
                 out_specs=pl.BlockSpec((1, n), lambda i: (0, 0)), out_shape=jax.ShapeDtypeStruct((1, n), F32),
                 args=[x], sem=("arbitrary",))


def _alibi_bias():
    i = np.arange(QROWS)[:, None]
    j = np.arange((QG + BACK_A) * CHUNK)[None, :]
    dist = np.abs(BACK_A * CHUNK + i - j).astype(np.float32)
    dc = j // CHUNK - i // CHUNK
    valid = (dc >= 0) & (dc <= BACK_A)
    slopes = np.array([2.0 ** (-8.0 * (h + 1) / H_A) for h in range(H_A)], dtype=np.float32)
    bias = -slopes[:, None, None] * dist[None]
    return jnp.asarray(np.where(valid[None], bias, np.float32(NEG_INF)).astype(np.float32))


def _rel_index_matrix():
    cc = np.arange(SKEW)
    dist = np.where(cc < SKEW - QROWS, BACK_B * CHUNK - cc, BACK_B * CHUNK + SKEW - cc)
    idx = np.clip(dist, -REL_CLIP, REL_CLIP) + REL_CLIP
    m = np.zeros((SKEW, N_REL), np.float32)
    m[cc, idx] = 1.0
    return jnp.asarray(m)


def _toeplitz_bias(vec):
    lk = (QG + BACK_B) * CHUNK

    def body(v_ref, o_ref):
        xv = jnp.broadcast_to(v_ref[0], (QROWS, SKEW))
        row = lax.broadcasted_iota(jnp.int32, (QROWS, SKEW), 0)
        for bit in range(QROWS.bit_length() - 1):
            xv = jnp.where((row >> bit) & 1 == 1, pltpu.roll(xv, 1 << bit, 1), xv)
        ri = lax.broadcasted_iota(jnp.int32, (QROWS, lk), 0) // CHUNK
        ci = lax.broadcasted_iota(jnp.int32, (QROWS, lk), 1) // CHUNK
        valid = (ci - ri >= 0) & (ci - ri <= BACK_B)
        o_ref[0] = jnp.where(valid, xv[:, :lk], NEG_INF)

    return _call(body, name="toeplitz_bias", grid=(H_B,),
                 in_specs=[pl.BlockSpec((1, 1, SKEW), lambda h: (h, 0, 0))],
                 out_specs=pl.BlockSpec((1, QROWS, lk), lambda h: (h, 0, 0)),
                 out_shape=jax.ShapeDtypeStruct((H_B, QROWS, lk), F32), args=[vec], sem=("parallel",))


def _diagonal_sums(dbias):
    lk = dbias.shape[2]

    def body(d_ref, o_ref):
        xv = jnp.concatenate([d_ref[0], jnp.zeros((QROWS, SKEW - lk), F32)], axis=1)
        row = lax.broadcasted_iota(jnp.int32, (QROWS, SKEW), 0)
        for bit in range(QROWS.bit_length() - 1):
            xv = jnp.where((row >> bit) & 1 == 1, pltpu.roll(xv, SKEW - (1 << bit), 1), xv)
        o_ref[0] = jnp.sum(xv, axis=0, keepdims=True)

    return _call(body, name="diagonal_sums", grid=(H_B,),
                 in_specs=[pl.BlockSpec((1, QROWS, lk), lambda h: (h, 0, 0))],
                 out_specs=pl.BlockSpec((1, 1, SKEW), lambda h: (h, 0, 0)),
                 out_shape=jax.ShapeDtypeStruct((H_B, 1, SKEW), F32), args=[dbias], sem=("parallel",))


def _attn_common(s, n_back, gqa, q_col, k_col, v_col):
    lk = (QG + n_back) * CHUNK
    pad = n_back * CHUNK
    q_spec = pl.BlockSpec((QROWS, LANES), lambda t, g: (g, q_col + t))
    if gqa:
        k_spec = pl.BlockSpec((s, LANES), lambda t, g: (0, k_col))
        v_spec = pl.BlockSpec((s, LANES), lambda t, g: (0, v_col))
    else:
        k_spec = pl.BlockSpec((s, LANES), lambda t, g: (0, k_col + t))
        v_spec = pl.BlockSpec((s, LANES), lambda t, g: (0, v_col + t))
    bias_spec = pl.BlockSpec((2, QROWS, lk), lambda t, g: (t, 0, 0))
    tile_spec = pl.BlockSpec((QROWS, LANES), lambda t, g: (g, t))
    return lk, pad, q_spec, k_spec, v_spec, bias_spec, tile_spec


def _attention_fwd(proj, bias, sinks, *, n_back, gqa, q_col, k_col, v_col, name, carry=None):
    s = proj.shape[0]
    lk, pad, q_spec, k_spec, v_spec, bias_spec, tile_spec = _attn_common(s, n_back, gqa, q_col, k_col, v_col)
    n_t, n_g = 512 // LANES, s // QROWS

    def body(*refs):
        if gqa:
            q_ref, k_ref, v_ref, bias_ref, sink_ref, o_ref, l_ref, kpad, vpad = refs
        else:
            q_ref, k_ref, v_ref, bias_ref, o_ref, l_ref, kpad, vpad = refs
        t, g = pl.program_id(0), pl.program_id(1)

        @pl.when(g == 0)
        def _():
            kpad[0:pad, :] = jnp.zeros((pad, LANES), BF16)
            vpad[0:pad, :] = jnp.zeros((pad, LANES), BF16)
            kpad[pad:, :] = k_ref[...]
            vpad[pad:, :] = v_ref[...]

        start = pl.multiple_of(g * QROWS, QROWS)
        kb = kpad[pl.ds(start, lk), :]
        vb = vpad[pl.ds(start, lk), :]
        half = lax.broadcasted_iota(jnp.int32, (QROWS, LANES), 1) // HEAD_DIM
        col_ok = lax.broadcasted_iota(jnp.int32, (QROWS, lk), 1) >= (n_back - QG * g) * CHUNK
        q = q_ref[...]
        if gqa:
            hk = t // 2
            q_rolled = pltpu.roll(q.astype(F32), HEAD_DIM, 1).astype(BF16)
        outs, lses = [], []
        for e in range(2):
            if gqa:
                kv_half = hk
                src = jnp.where(hk == e, q, q_rolled)
            else:
                kv_half = e
                src = q
            qm = jnp.where(half == kv_half, src, jnp.zeros_like(src))
            sc = _dot_nt(qm, kb) * (HEAD_DIM ** -0.5) + bias_ref[e]
            sc = jnp.where(col_ok, sc, NEG_INF)
            m = jnp.max(sc, axis=-1, keepdims=True)
            if gqa:
                sk = sink_ref[2 * t + e]
                m = jnp.maximum(m, sk)
            p = jnp.exp(sc - m)
            l = jnp.sum(p, axis=-1, keepdims=True)
            if gqa:
                l = l + jnp.exp(sk - m)
            pn = p / l
            outs.append(_dot(pn.astype(BF16), vb))
            lses.append(m + jnp.log(l))
        if gqa:
            same = jnp.where(hk == 0, outs[0], outs[1])
            other = jnp.where(hk == 0, outs[1], outs[0])
            o_ref[...] = jnp.where(half == hk, same, pltpu.roll(other, HEAD_DIM, 1))
        else:
            o_ref[...] = jnp.where(half == 0, outs[0], outs[1])
        l_ref[...] = jnp.where(half == 0, lses[0], lses[1])

    in_specs = [q_spec, k_spec, v_spec, bias_spec] + ([SMEM_SPEC] if gqa else [])
    args = [proj, proj, proj, bias] + ([sinks] if gqa else [])
    o_shape = jax.ShapeDtypeStruct((s, 512), F32)
    return _call(body, name=name, grid=(n_t, n_g), in_specs=in_specs, out_specs=(tile_spec, tile_spec),
                 out_shape=(o_shape, o_shape), args=args,
                 scratch=[pltpu.VMEM((s + pad, LANES), BF16), pltpu.VMEM((s + pad, LANES), BF16)],
                 sem=("arbitrary", "arbitrary"), carry=carry)


def _attention_bwd(proj, bias, sinks, do, lse, *, n_back, gqa, q_col, k_col, v_col, name, carry=None):
    s = proj.shape[0]
    lk, pad, q_spec, k_spec, v_spec, bias_spec, tile_spec = _attn_common(s, n_back, gqa, q_col, k_col, v_col)
    n_t, n_g = 512 // LANES, s // QROWS

    def body(*refs):
        if gqa:
            (q_ref, k_ref, v_ref, bias_ref, sink_ref, do_ref, l_ref,
             dq_ref, dk_ref, dv_ref, dsink_ref, kpad, vpad, dkpad, dvpad) = refs
        else:
            (q_ref, k_ref, v_ref, bias_ref, do_ref, l_ref,
             dq_ref, dk_ref, dv_ref, dbias_ref, kpad, vpad, dkpad, dvpad) = refs
        t, g = pl.program_id(0), pl.program_id(1)

        @pl.when(g == 0)
        def _():
            kpad[0:pad, :] = jnp.zeros((pad, LANES), BF16)
            vpad[0:pad, :] = jnp.zeros((pad, LANES), BF16)
            kpad[pad:, :] = k_ref[...]
            vpad[pad:, :] = v_ref[...]
            if gqa:
                dsink_ref[...] = jnp.zeros_like(dsink_ref)
            else:
                dbias_ref[...] = jnp.zeros_like(dbias_ref)

        @pl.when((g == 0) & (t == 0) if gqa else g == 0)
        def _():
            dkpad[...] = jnp.zeros_like(dkpad)
            dvpad[...] = jnp.zeros_like(dvpad)

        start = pl.multiple_of(g * QROWS, QROWS)
        kb = kpad[pl.ds(start, lk), :]
        vb = vpad[pl.ds(start, lk), :]
        half = lax.broadcasted_iota(jnp.int32, (QROWS, LANES), 1) // HEAD_DIM
        col_ok = lax.broadcasted_iota(jnp.int32, (QROWS, lk), 1) >= (n_back - QG * g) * CHUNK
        q = q_ref[...]
        dov = do_ref[...]
        lv = l_ref[...]
        if gqa:
            hk = t // 2
            q_rolled = pltpu.roll(q.astype(F32), HEAD_DIM, 1).astype(BF16)
            do_rolled = pltpu.roll(dov, HEAD_DIM, 1)
        dqs = []
        dk_acc = jnp.zeros((lk, LANES), F32)
        dv_acc = jnp.zeros((lk, LANES), F32)
        for e in range(2):
            if gqa:
                kv_half = hk
                src = jnp.where(hk == e, q, q_rolled)
                do_src = jnp.where(hk == e, dov, do_rolled)
            else:
                kv_half = e
                src = q
                do_src = dov
            qm = jnp.where(half == kv_half, src, jnp.zeros_like(src))
            dom = jnp.where(half == kv_half, do_src, 0.0).astype(BF16)
            lcol = jnp.max(jnp.where(half == e, lv, -jnp.inf), axis=-1, keepdims=True)
            sc = _dot_nt(qm, kb) * (HEAD_DIM ** -0.5) + bias_ref[e]
            sc = jnp.where(col_ok, sc, NEG_INF)
            pn = jnp.exp(sc - lcol)
            dp = _dot_nt(dom, vb)
            delta = jnp.sum(pn * dp, axis=-1, keepdims=True)
            ds = pn * (dp - delta)
            if gqa:
                p_sink = jnp.exp(sink_ref[2 * t + e] - lcol)
                dsk = -jnp.sum(p_sink * delta, axis=0, keepdims=True)
                dsink_ref[0, e:e + 1, :] += jnp.broadcast_to(dsk, (1, LANES))
            else:
                dbias_ref[e] += ds
            dsb = (ds * (HEAD_DIM ** -0.5)).astype(BF16)
            dqs.append(_dot(dsb, kb))
            dk_acc = dk_acc + _dot_tn(dsb, qm)
            dv_acc = dv_acc + _dot_tn(pn.astype(BF16), dom)
        dkpad[pl.ds(start, lk), :] += dk_acc
        dvpad[pl.ds(start, lk), :] += dv_acc
        if gqa:
            same = jnp.where(hk == 0, dqs[0], dqs[1])
            other = jnp.where(hk == 0, dqs[1], dqs[0])
            dq_ref[...] = jnp.where(half == hk, same, pltpu.roll(other, HEAD_DIM, 1)).astype(BF16)
        else:
            dq_ref[...] = jnp.where(half == 0, dqs[0], dqs[1]).astype(BF16)

        @pl.when((g == n_g - 1) & (t == n_t - 1) if gqa else g == n_g - 1)
        def _():
            dk_ref[...] = dkpad[pad:, :].astype(BF16)
            dv_ref[...] = dvpad[pad:, :].astype(BF16)

    in_specs = [q_spec, k_spec, v_spec, bias_spec] + ([SMEM_SPEC] if gqa else []) + [tile_spec, tile_spec]
    args = [proj, proj, proj, bias] + ([sinks] if gqa else []) + [do, lse]
    if gqa:
        kv_out = pl.BlockSpec((s, LANES), lambda t, g: (0, 0))
        kv_shape = jax.ShapeDtypeStruct((s, LANES), BF16)
        extra_spec = pl.BlockSpec((1, 8, LANES), lambda t, g: (t, 0, 0))
        extra_shape = jax.ShapeDtypeStruct((n_t, 8, LANES), F32)
    else:
        kv_out = pl.BlockSpec((s, LANES), lambda t, g: (0, t))
        kv_shape = jax.ShapeDtypeStruct((s, 512), BF16)
        extra_spec = bias_spec
        extra_shape = jax.ShapeDtypeStruct(bias.shape, F32)
    return _call(body, name=name, grid=(n_t, n_g), in_specs=in_specs,
                 out_specs=(tile_spec, kv_out, kv_out, extra_spec),
                 out_shape=(jax.ShapeDtypeStruct((s, 512), BF16), kv_shape, kv_shape, extra_shape), args=args,
                 scratch=[pltpu.VMEM((s + pad, LANES), BF16), pltpu.VMEM((s + pad, LANES), BF16),
                          pltpu.VMEM((s + pad, LANES), F32), pltpu.VMEM((s + pad, LANES), F32)],
                 sem=("arbitrary", "arbitrary"), carry=carry)


def _sum_slots(r, name):
    n_slots, rows, k = r.shape

    def body(r_ref, o_ref):
        acc = r_ref[0].astype(F32)
        for j in range(1, n_slots):
            acc = acc + r_ref[j].astype(F32)
        o_ref[...] = acc

    return _call(body, name=name, grid=(k // LANES,),
                 in_specs=[pl.BlockSpec((n_slots, rows, LANES), lambda i: (0, 0, i))],
                 out_specs=pl.BlockSpec((rows, LANES), lambda i: (0, i)),
                 out_shape=jax.ShapeDtypeStruct((rows, k), F32), args=[r], sem=("parallel",))


def _sum_rows8(g):
    n = g.shape[2]

    def body(g_ref, o_ref):
        acc = g_ref[0]
        for j in range(1, N_DEV):
            acc = acc + g_ref[j]
        o_ref[...] = acc

    return pl.pallas_call(
        body, name="sum_small_grads", in_specs=[VMEM_SPEC], out_specs=VMEM_SPEC,
        out_shape=jax.ShapeDtypeStruct((1, n), F32), compiler_params=_params(),
    )(g)


def _ada_weight_grad(sc_t, dmod_cols):
    d = sc_t.shape[0]
    w = dmod_cols.shape[1]
    td = _pick(d, (256, 128))

    def body(sc_ref, dm_ref, o_ref):
        scv = sc_ref[...]
        dmv = dm_ref[...]
        acc = scv[:, 0:1] * dmv[0:1, :]
        for b in range(1, N_DEV):
            acc = acc + scv[:, b:b + 1] * dmv[b:b + 1, :]
        o_ref[...] = acc

    return _call(body, name="ada_weight_grad", grid=(d // td,),
                 in_specs=[pl.BlockSpec((td, N_DEV), lambda i: (i, 0)), pl.BlockSpec((N_DEV, w), lambda i: (0, 0))],
                 out_specs=pl.BlockSpec((td, w), lambda i: (i, 0)), out_shape=jax.ShapeDtypeStruct((d, w), F32),
                 args=[sc_t, dmod_cols], sem=("parallel",))


def _adamw(w, g, m, v, name):
    rows, cols = w.shape
    tr = _pick(rows, (256, 176, 128, 88, 64)) if rows > 256 else rows

    def body(w_ref, g_ref, m_ref, v_ref, d_ref, nm_ref, nv_ref):
        gv = g_ref[...]
        nm = ADAM_B1 * m_ref[...] + (1.0 - ADAM_B1) * gv
        nv = ADAM_B2 * v_ref[...] + (1.0 - ADAM_B2) * (gv * gv)
        m_hat = nm / (1.0 - ADAM_B1 ** ADAM_STEP)
        v_hat = nv / (1.0 - ADAM_B2 ** ADAM_STEP)
        d_ref[...] = -ADAM_LR * (m_hat / (jnp.sqrt(v_hat) + ADAM_EPS) + ADAM_WD * w_ref[...])
        nm_ref[...] = nm
        nv_ref[...] = nv

    spec = pl.BlockSpec((tr, cols), lambda i: (i, 0))
    shape = jax.ShapeDtypeStruct((rows, cols), F32)
    return _call(body, name=name, grid=(rows // tr,), in_specs=[spec] * 4, out_specs=(spec, spec, spec),
                 out_shape=(shape, shape, shape), args=[w, g, m, v], sem=("parallel",))


SMALL = ("b_ada", "g_pre_ffn1", "g_post_ffn1", "g_pre_mix", "b_in", "sinks_a", "rel_bias_b", "g_grp_a",
         "g_grp_b", "b_out", "g_post_mix", "g_pre_ffn2", "g_post_ffn2")
WEIGHTS = ("w_ada", "b_ada", "g_pre_ffn1", "w_gate1", "w_up1", "w_down1", "g_post_ffn1", "g_pre_mix", "w_in",
           "b_in", "sinks_a", "rel_bias_b", "g_grp_a", "g_grp_b", "w_out", "b_out", "g_post_mix", "g_pre_ffn2",
           "w_gate2", "w_up2", "w_down2", "g_post_ffn2")


def kernel(x, c, w_ada, b_ada, g_pre_ffn1, w_gate1, w_up1, w_down1, g_post_ffn1, g_pre_mix, w_in, b_in, sinks_a, rel_bias_b, g_grp_a, g_grp_b, w_out, b_out, g_post_mix, g_pre_ffn2, w_gate2, w_up2, w_down2, g_post_ffn2, loss_target, m_w_ada, m_b_ada, m_g_pre_ffn1, m_w_gate1, m_w_up1, m_w_down1, m_g_post_ffn1, m_g_pre_mix, m_w_in, m_b_in, m_sinks_a, m_rel_bias_b, m_g_grp_a, m_g_grp_b, m_w_out, m_b_out, m_g_post_mix, m_g_pre_ffn2, m_w_gate2, m_w_up2, m_w_down2, m_g_post_ffn2, v_w_ada, v_b_ada, v_g_pre_ffn1, v_w_gate1, v_w_up1, v_w_down1, v_g_post_ffn1, v_g_pre_mix, v_w_in, v_b_in, v_sinks_a, v_rel_bias_b, v_g_grp_a, v_g_grp_b, v_w_out, v_b_out, v_g_post_mix, v_g_pre_ffn2, v_w_gate2, v_w_up2, v_w_down2, v_g_post_ffn2):
    given = dict(locals())
    weights = {n: given[n] for n in WEIGHTS}
    mom_m = {n: given["m_" + n] for n in WEIGHTS}
    mom_v = {n: given["v_" + n] for n in WEIGHTS}

    me = 4 * lax.axis_index("x") + 2 * lax.axis_index("y") + lax.axis_index("c")
    xs = x[0]
    tgt = loss_target[0]
    d_model = xs.shape[1]
    ada_cols = w_ada.shape[2]

    b_cols = lax.dynamic_slice(b_ada, (0, me * ada_cols), (1, ada_cols))
    sc_all, mod_rows = _ada_forward(c, w_ada[0], b_cols)
    mod = mod_rows.reshape(N_MOD, d_model)
    shift1, scale1, gate1, shift2, scale2, gate2, shift3, scale3, gate3 = (mod[i:i + 1] for i in range(N_MOD))

    sh = {"wg1": w_gate1[0].T, "wu1": w_up1[0].T, "wd1": w_down1[0], "win": w_in[0].T, "wo": w_out[0],
          "wg2": w_gate2[0].T, "wu2": w_up2[0].T, "wd2": w_down2[0]}
    sh = {k: v.astype(BF16) for k, v in sh.items()}

    def gather(*names):
        return _gather_carry([sh[n] for n in names])

    def ffn_forward(xin, g_pre, g_post, shift, scale, gate, wg, wu, wd_name, next_name, tag):
        h = _pre_norm(xin, g_pre, scale, shift, "pre_norm_" + tag)
        (a, b, u), (wd,) = _ffn_up(h, wg, wu, "ffn_up_" + tag, carry=gather(wd_name))
        if next_name is None:
            y, nxt = _mm_nn([(u, wd)], "ffn_down_" + tag, F32), None
        else:
            y, (nxt,) = _mm_nn([(u, wd)], "ffn_down_" + tag, F32, carry=gather(next_name))
        xout = _post_norm_residual(xin, y, g_post, gate, 0.5, "post_norm_" + tag)
        return xout, (xin, h, a, b, u, y), wd, nxt

    wg1, wu1 = _run_carry(gather("wg1", "wu1"), "gather_ffn1")
    x1, saved1, wd1, win = ffn_forward(xs, g_pre_ffn1, g_post_ffn1, shift1, scale1, gate1, wg1, wu1,
                                       "wd1", "win", "ffn1")

    h2 = _pre_norm(x1, g_pre_mix, scale2, shift2, "pre_norm_mix")
    proj, (wo,) = _mm_nt(h2, win, "in_proj", BF16, bias=b_in, carry=gather("wo"))
    bias_a = _alibi_bias()
    rel_m = _rel_index_matrix()
    rel_vec = jnp.dot(rel_bias_b[0], rel_m.T, precision=lax.Precision.HIGHEST)
    bias_b = _toeplitz_bias(rel_vec.reshape(H_B, 1, SKEW))
    sinks = sinks_a[0]
    cfg_a = dict(n_back=BACK_A, gqa=True, q_col=0, k_col=QA // LANES, v_col=(QA + KVA) // LANES)
    cfg_b = dict(n_back=BACK_B, gqa=False, q_col=(QA + 2 * KVA) // LANES, k_col=(QA + 2 * KVA + QB) // LANES,
                 v_col=(QA + 2 * KVA + 2 * QB) // LANES)
    (oa, lse_a), (wg2,) = _attention_fwd(proj, bias_a, sinks, name="attn_a", carry=gather("wg2"), **cfg_a)
    (ob, lse_b), (wu2,) = _attention_fwd(proj, bias_b, None, name="attn_b", carry=gather("wu2"), **cfg_b)
    ycat = _group_norm_cat(oa, ob, g_grp_a, g_grp_b)
    ymix = _mm_nn([(ycat, wo)], "out_proj", F32, bias=b_out)
    x2 = _post_norm_residual(x1, ymix, g_post_mix, gate2, 1.0, "post_norm_mix")

    x3, saved3, wd2, _ = ffn_forward(x2, g_pre_ffn2, g_post_ffn2, shift3, scale3, gate3, wg2, wu2,
                                     "wd2", None, "ffn2")

    loss_part, dx3 = _loss_and_grad(x3, tgt)

    def scatter(*grads):
        return _scatter_carry(list(grads))

    slots = {}

    xin, h, a, b, u, y = saved3
    dy, s1, _ = _post_norm_bwd(dx3, y, g_post_ffn2, gate3, 0.5, "post_norm_bwd_ffn2")
    da, db = _ffn_down_bwd(dy, wd2, a, b, "ffn_down_bwd_ffn2")
    dwd2 = _mm_tn_pair(u, dy, "grad_wd_ffn2")
    dwg2 = _mm_tn_pair(da, h, "grad_wg_ffn2")
    dwu2 = _mm_tn_pair(db, h, "grad_wu_ffn2")
    dh, (slots["wd2"],) = _mm_nn([(da, wg2), (db, wu2)], "ffn_up_bwd_ffn2", F32, carry=scatter(dwd2))
    dx2, s2, s3 = _pre_norm_bwd(dh, xin, g_pre_ffn2, scale3, dx3, "pre_norm_bwd_ffn2")
    sm3 = dict(shift=s3, scale=s2 * g_pre_ffn2, gate=0.5 * g_post_ffn2 * s1,
               g_pre=(1.0 + scale3) * s2, g_post=(0.5 * gate3) * s1)

    dymix, s1m, db_out = _post_norm_bwd(dx2, ymix, g_post_mix, gate2, 1.0, "post_norm_bwd_mix")
    dycat = _mm_nt(dymix, wo, "out_proj_bwd", F32)
    dwo = _mm_tn_pair(ycat, dymix, "grad_wo")
    doa, dob, dg_a, dg_b = _group_norm_bwd(dycat, oa, ob, g_grp_a, g_grp_b)
    (dqa, dka, dva, dsink), (slots["wg2"],) = _attention_bwd(
        proj, bias_a, sinks, doa, lse_a, name="attn_a_bwd", carry=scatter(dwg2), **cfg_a)
    (dqb, dkb, dvb, dbias), (slots["wu2"], slots["wo"]) = _attention_bwd(
        proj, bias_b, None, dob, lse_b, name="attn_b_bwd", carry=scatter(dwu2, dwo), **cfg_b)
    dproj = jnp.concatenate([dqa, dka, dva, dqb, dkb, dvb], axis=1)
    db_in = _col_sum(dproj, "grad_b_in")
    dwin = _mm_tn_pair(dproj, h2, "grad_win")
    dh2 = _mm_nn([(dproj, win)], "in_proj_bwd", F32)
    dx1, s2m, s3m = _pre_norm_bwd(dh2, x1, g_pre_mix, scale2, dx2, "pre_norm_bwd_mix")
    d_rel = jnp.dot(_diagonal_sums(dbias).reshape(H_B, SKEW), rel_m, precision=lax.Precision.HIGHEST)
    d_sinks = dsink[:, :2, 0].reshape(1, H_A)

    xin, h, a, b, u, y = saved1
    dy, s1, _ = _post_norm_bwd(dx1, y, g_post_ffn1, gate1, 0.5, "post_norm_bwd_ffn1")
    (da, db), (slots["win"],) = _ffn_down_bwd(dy, wd1, a, b, "ffn_down_bwd_ffn1", carry=scatter(dwin))
    dwd1 = _mm_tn_pair(u, dy, "grad_wd_ffn1")
    dwg1, (slots["wd1"],) = _mm_tn_pair(da, h, "grad_wg_ffn1", carry=scatter(dwd1))
    dwu1, (slots["wg1"],) = _mm_tn_pair(db, h, "grad_wu_ffn1", carry=scatter(dwg1))
    dh, (slots["wu1"],) = _mm_nn([(da, wg1), (db, wu1)], "ffn_up_bwd_ffn1", F32, carry=scatter(dwu1))
    dx0, s2, s3 = _pre_norm_bwd(dh, xin, g_pre_ffn1, scale1, dx1, "pre_norm_bwd_ffn1")
    sm1 = dict(shift=s3, scale=s2 * g_pre_ffn1, gate=0.5 * g_post_ffn1 * s1,
               g_pre=(1.0 + scale1) * s2, g_post=(0.5 * gate1) * s1)

    gsum = {k: _sum_slots(v, "sum_grad_" + k) for k, v in slots.items()}
    grads = {"w_gate1": gsum["wg1"].T, "w_up1": gsum["wu1"].T, "w_down1": gsum["wd1"], "w_in": gsum["win"].T,
             "w_out": gsum["wo"], "w_gate2": gsum["wg2"].T, "w_up2": gsum["wu2"].T, "w_down2": gsum["wd2"]}

    dmod = jnp.concatenate([sm1["shift"], sm1["scale"], sm1["gate"],
                            s3m, s2m * g_pre_mix, g_post_mix * s1m,
                            sm3["shift"], sm3["scale"], sm3["gate"]], axis=1)
    small_parts = {
        "b_ada": dmod, "g_pre_ffn1": sm1["g_pre"], "g_post_ffn1": sm1["g_post"],
        "g_pre_mix": (1.0 + scale2) * s2m, "b_in": db_in, "sinks_a": d_sinks,
        "rel_bias_b": d_rel.reshape(1, H_B * N_REL), "g_grp_a": dg_a, "g_grp_b": dg_b, "b_out": db_out,
        "g_post_mix": gate2 * s1m, "g_pre_ffn2": sm3["g_pre"], "g_post_ffn2": sm3["g_post"]}
    sizes = [small_parts[n].shape[1] for n in SMALL]
    n_small = sum(sizes)
    n_pad = -(n_small + 1) % LANES
    packed = jnp.concatenate([small_parts[n] for n in SMALL] + [loss_part, jnp.zeros((1, n_pad), F32)], axis=1)
    gathered = _all_gather_small(packed)
    small_sum = _sum_rows8(gathered)
    loss = small_sum[0, n_small]
    dmod_cols = lax.dynamic_slice(gathered.reshape(N_DEV, n_small + 1 + n_pad), (0, me * ada_cols),
                                  (N_DEV, ada_cols))
    grads["w_ada"] = _ada_weight_grad(sc_all.reshape(N_DEV, d_model).T, dmod_cols)

    out_g, out_d, out_m, out_v = {}, {}, {}, {}
    for n in ("w_ada", "w_gate1", "w_up1", "w_down1", "w_in", "w_out", "w_gate2", "w_up2", "w_down2"):
        d_, m_, v_ = _adamw(weights[n][0], grads[n], mom_m[n][0], mom_v[n][0], "adamw_" + n)
        out_g[n], out_d[n], out_m[n], out_v[n] = grads[n][None], d_[None], m_[None], v_[None]

    def pack(tree):
        return jnp.concatenate([tree[n].reshape(1, -1) for n in SMALL], axis=1)

    g_small = small_sum[:, :n_small]
    d_s, m_s, v_s = _adamw(pack(weights), g_small, pack(mom_m), pack(mom_v), "adamw_small")
    off = 0
    for n, size in zip(SMALL, sizes):
        shape = weights[n].shape
        out_g[n] = g_small[:, off:off + size].reshape(shape)
        out_d[n] = d_s[:, off:off + size].reshape(shape)
        out_m[n] = m_s[:, off:off + size].reshape(shape)
        out_v[n] = v_s[:, off:off + size].reshape(shape)
        off += size

    return (loss, dx0[None], *[out_g[n] for n in WEIGHTS], *[out_d[n] for n in WEIGHTS],
            *[out_m[n] for n in WEIGHTS], *[out_v[n] for n in WEIGHTS])
```

```python
import numpy as np
import jax
import jax.numpy as jnp
from jax import lax
from jax.experimental import pallas as pl
from jax.experimental.pallas import tpu as pltpu

F32 = jnp.float32
BF16 = jnp.bfloat16
MESH = pl.DeviceIdType.MESH
ANY = pl.BlockSpec(memory_space=pl.ANY)
VMEM_SPEC = pl.BlockSpec(memory_space=pltpu.VMEM)
SMEM_SPEC = pl.BlockSpec(memory_space=pltpu.SMEM)

N_DEV = 8
CHUNK = 64
HEAD_DIM = 64
LANES = 128
H_A, KV_A, H_B = 8, 2, 8
BACK_A, BACK_B = 2, 8
REL_CLIP = 128
N_REL = 2 * REL_CLIP + 1
QA, KVA, QB = H_A * HEAD_DIM, KV_A * HEAD_DIM, H_B * HEAD_DIM
D_IN = QA + 2 * KVA + 3 * QB
N_MOD = 9
EPS = 1e-6
NEG_INF = -1e30
QG = 4
QROWS = QG * CHUNK
SKEW = 1024
ADAM_LR, ADAM_B1, ADAM_B2, ADAM_EPS, ADAM_WD, ADAM_STEP = 0.001, 0.9, 0.999, 1e-08, 0.01, 10
VMEM_LIMIT = 56 * 2 ** 20


def _pick(n, cands):
    for c in cands:
        if n % c == 0:
            return c
    return n


def _params(sem=None):
    return pltpu.CompilerParams(dimension_semantics=sem, vmem_limit_bytes=VMEM_LIMIT)


def _dot_nt(a, b):
    return lax.dot_general(a, b, (((1,), (1,)), ((), ())), preferred_element_type=F32)


def _dot_tn(a, b):
    return lax.dot_general(a, b, (((0,), (0,)), ((), ())), preferred_element_type=F32)


def _dot(a, b):
    return jnp.dot(a, b, preferred_element_type=F32)


def _sigmoid(a):
    return 1.0 / (1.0 + jnp.exp(-a))


def _mesh_pos():
    return lax.axis_index("x"), lax.axis_index("y"), lax.axis_index("c")


def _peer(x, y, c, r):
    px = 1 - x if r & 4 else x
    py = 1 - y if r & 2 else y
    pc = 1 - c if r & 1 else c
    return px, py, pc


class _Carry:
    def __init__(self, ins, out_shapes, scratch, start, finish):
        self.ins, self.out_shapes, self.scratch = list(ins), list(out_shapes), list(scratch)
        self.start, self.finish = start, finish


def _call(body, *, name, grid, in_specs, out_specs, out_shape, args, scratch=(), sem=None, carry=None):
    single = not isinstance(out_shape, (tuple, list))
    out_specs = (out_specs,) if single else tuple(out_specs)
    out_shape = (out_shape,) if single else tuple(out_shape)
    if carry is None:
        res = pl.pallas_call(body, name=name, grid=grid, in_specs=list(in_specs), out_specs=out_specs,
                             out_shape=out_shape, scratch_shapes=list(scratch), compiler_params=_params(sem))(*args)
        return res[0] if single else res
    n_in, n_out, n_s = len(in_specs), len(out_shape), len(scratch)
    ci, co = len(carry.ins), len(carry.out_shapes)

    def wrapped(*refs):
        ins, cins = refs[:n_in], refs[n_in:n_in + ci]
        outs = refs[n_in + ci:n_in + ci + n_out]
        couts = refs[n_in + ci + n_out:n_in + ci + n_out + co]
        scr = refs[n_in + ci + n_out + co:n_in + ci + n_out + co + n_s]
        cscr = refs[n_in + ci + n_out + co + n_s:]
        first, last = None, None
        for ax, n in enumerate(grid):
            f, l = pl.program_id(ax) == 0, pl.program_id(ax) == n - 1
            first = f if first is None else first & f
            last = l if last is None else last & l
        pl.when(first)(lambda: carry.start(cins, couts, cscr))
        body(*ins, *outs, *scr)
        pl.when(last)(lambda: carry.finish(cins, couts, cscr))

    res = pl.pallas_call(
        wrapped, name=name, grid=grid, in_specs=list(in_specs) + [ANY] * ci, out_specs=out_specs + (ANY,) * co,
        out_shape=out_shape + tuple(carry.out_shapes), scratch_shapes=list(scratch) + carry.scratch,
        compiler_params=_params(("arbitrary",) * len(grid)))(*args, *carry.ins)
    main = res[:n_out]
    return (main[0] if single else main), res[n_out:]


def _run_carry(carry, name):
    ci, co = len(carry.ins), len(carry.out_shapes)

    def body(*refs):
        carry.start(refs[:ci], refs[ci:ci + co], refs[ci + co:])
        carry.finish(refs[:ci], refs[ci:ci + co], refs[ci + co:])

    return pl.pallas_call(body, name=name, in_specs=[ANY] * ci, out_specs=(ANY,) * co,
                          out_shape=tuple(carry.out_shapes), scratch_shapes=carry.scratch,
                          compiler_params=_params())(*carry.ins)


def _gather_carry(shards):
    n_w = len(shards)
    rows = [s.shape[0] for s in shards]

    def plan(ins, outs, scr):
        send_sems, recv_sems, local_sems = scr
        x, y, c = _mesh_pos()
        me, sibling = (x, y, c), (x, y, 1 - c)
        chips = [(1 - x, y), (x, 1 - y), (1 - x, 1 - y)]

        def block(w, dev):
            start = pl.multiple_of((4 * dev[0] + 2 * dev[1] + dev[2]) * rows[w], 16)
            return outs[w].at[pl.ds(start, rows[w]), :]

        def copy(w, k, dev, to, src=None):
            return pltpu.make_async_remote_copy(
                src_ref=block(w, dev) if src is None else src, dst_ref=block(w, dev),
                send_sem=send_sems.at[w, k], recv_sem=recv_sems.at[w, k], device_id=to, device_id_type=MESH)

        mine = [pltpu.make_async_copy(ins[w], block(w, me), local_sems.at[w]) for w in range(n_w)]
        first = []
        for j, chip in enumerate(chips):
            first += [copy(w, 1 + j, me, (*chip, c), src=ins[w]) for w in range(n_w)]
        first += [copy(w, 0, me, sibling, src=ins[w]) for w in range(n_w)]
        return c, me, sibling, chips, copy, mine, first

    def start(ins, outs, scr):
        _, _, _, _, _, mine, first = plan(ins, outs, scr)
        for cp in mine + first:
            cp.start()

    def finish(ins, outs, scr):
        c, me, sibling, chips, copy, mine, first = plan(ins, outs, scr)
        passed = []
        for j, chip in enumerate(chips):
            for w in range(n_w):
                copy(w, 1 + j, (*chip, c), me).wait_recv()
                cp = copy(w, 4 + j, (*chip, c), sibling)
                cp.start()
                passed.append(cp)
        for w in range(n_w):
            copy(w, 0, sibling, me).wait_recv()
        for j, chip in enumerate(chips):
            for w in range(n_w):
                copy(w, 4 + j, (*chip, 1 - c), me).wait_recv()
        for cp in first + passed:
            cp.wait_send()
        for cp in mine:
            cp.wait()

    return _Carry(
        shards, [jax.ShapeDtypeStruct((N_DEV * s.shape[0], s.shape[1]), s.dtype) for s in shards],
        [pltpu.SemaphoreType.DMA((n_w, N_DEV - 1)), pltpu.SemaphoreType.DMA((n_w, N_DEV - 1)),
         pltpu.SemaphoreType.DMA((n_w,))], start, finish)


def _scatter_carry(parts):
    n_w = len(parts)
    n_chip = N_DEV // 2
    rows = [g.shape[0] // n_chip for g in parts]

    def plan(ins, outs, scr):
        send_sems, recv_sems, local_sems = scr
        x, y, c = _mesh_pos()

        def src(w, chip_index):
            return ins[w].at[pl.ds(pl.multiple_of(chip_index * rows[w], 16), rows[w]), :]

        mine = [pltpu.make_async_copy(src(w, 2 * x + y), outs[w].at[0], local_sems.at[w]) for w in range(n_w)]
        copies = []
        for r in (3, 2, 1):
            px, py, _ = _peer(x, y, c, 2 * r)
            for w in range(n_w):
                copies.append(pltpu.make_async_remote_copy(
                    src_ref=src(w, 2 * px + py), dst_ref=outs[w].at[r], send_sem=send_sems.at[w, r - 1],
                    recv_sem=recv_sems.at[w, r - 1], device_id=(px, py, c), device_id_type=MESH))
        return mine, copies

    def start(ins, outs, scr):
        mine, copies = plan(ins, outs, scr)
        for cp in mine + copies:
            cp.start()

    def finish(ins, outs, scr):
        mine, copies = plan(ins, outs, scr)
        for cp in copies:
            cp.wait_recv()
        for cp in copies:
            cp.wait_send()
        for cp in mine:
            cp.wait()

    return _Carry(
        parts, [jax.ShapeDtypeStruct((n_chip, r, g.shape[1]), g.dtype) for r, g in zip(rows, parts)],
        [pltpu.SemaphoreType.DMA((n_w, n_chip - 1)), pltpu.SemaphoreType.DMA((n_w, n_chip - 1)),
         pltpu.SemaphoreType.DMA((n_w,))], start, finish)


def _ada_forward(c_row, w_ada, b_cols, carry):
    d = c_row.shape[1]
    wcols = w_ada.shape[1]
    ci, co = len(carry.ins), len(carry.out_shapes)

    def body(*refs):
        c_ref, w_ref, b_ref = refs[:3]
        cins = refs[3:3 + ci]
        sc_ref, mod_ref = refs[3 + ci:5 + ci]
        couts = refs[5 + ci:5 + ci + co]
        rows_ref, send_sems, recv_sems = refs[5 + ci + co:8 + ci + co]
        cscr = refs[8 + ci + co:]
        carry.start(cins, couts, cscr)
        x, y, c = _mesh_pos()
        me = 4 * x + 2 * y + c
        cv = c_ref[...]
        sc_ref[me] = cv * _sigmoid(cv)

        sends = []
        for r in range(1, N_DEV):
            px, py, pc = _peer(x, y, c, r)
            cp = pltpu.make_async_remote_copy(
                src_ref=sc_ref.at[me], dst_ref=sc_ref.at[me], send_sem=send_sems.at[0, r - 1],
                recv_sem=recv_sems.at[0, r - 1], device_id=(px, py, pc), device_id_type=MESH)
            cp.start()
            sends.append(cp)
        for r in range(1, N_DEV):
            px, py, pc = _peer(x, y, c, r)
            pid = 4 * px + 2 * py + pc
            pltpu.make_async_remote_copy(
                src_ref=sc_ref.at[pid], dst_ref=sc_ref.at[pid], send_sem=send_sems.at[0, r - 1],
                recv_sem=recv_sems.at[0, r - 1], device_id=(px, py, pc), device_id_type=MESH).wait_recv()
        for cp in sends:
            cp.wait_send()

        sc_all = jnp.concatenate([sc_ref[j] for j in range(N_DEV)], axis=0)
        rows = _dot(sc_all.astype(BF16), w_ref[...].astype(BF16)) + b_ref[...]
        for j in range(N_DEV):
            rows_ref[j] = rows[j:j + 1, :]
        mod_ref[me] = rows_ref[me]

        sends = []
        for r in range(1, N_DEV):
            px, py, pc = _peer(x, y, c, r)
            pid = 4 * px + 2 * py + pc
            cp = pltpu.make_async_remote_copy(
                src_ref=rows_ref.at[pid], dst_ref=mod_ref.at[me], send_sem=send_sems.at[1, r - 1],
                recv_sem=recv_sems.at[1, r - 1], device_id=(px, py, pc), device_id_type=MESH)
            cp.start()
            sends.append(cp)
        for r in range(1, N_DEV):
            px, py, pc = _peer(x, y, c, r)
            pid = 4 * px + 2 * py + pc
            pltpu.make_async_remote_copy(
                src_ref=rows_ref.at[pid], dst_ref=mod_ref.at[pid], send_sem=send_sems.at[1, r - 1],
                recv_sem=recv_sems.at[1, r - 1], device_id=(px, py, pc), device_id_type=MESH).wait_recv()
        for cp in sends:
            cp.wait_send()
        carry.finish(cins, couts, cscr)

    res = pl.pallas_call(
        body, name="ada_forward",
        out_shape=(jax.ShapeDtypeStruct((N_DEV, 1, d), F32), jax.ShapeDtypeStruct((N_DEV, 1, wcols), F32),
                   *carry.out_shapes),
        in_specs=[VMEM_SPEC, VMEM_SPEC, VMEM_SPEC] + [ANY] * ci, out_specs=(VMEM_SPEC, VMEM_SPEC) + (ANY,) * co,
        scratch_shapes=[pltpu.VMEM((N_DEV, 1, wcols), F32), pltpu.SemaphoreType.DMA((2, N_DEV - 1)),
                        pltpu.SemaphoreType.DMA((2, N_DEV - 1))] + carry.scratch,
        compiler_params=_params(),
    )(c_row, w_ada, b_cols, *carry.ins)
    return res[:2], res[2:]


def _all_gather_small(v):
    n = v.shape[1]

    def body(v_ref, out_ref, send_sems, recv_sems):
        x, y, c = _mesh_pos()
        me = 4 * x + 2 * y + c
        out_ref[me] = v_ref[...]
        sends = []
        for r in range(1, N_DEV):
            px, py, pc = _peer(x, y, c, r)
            cp = pltpu.make_async_remote_copy(
                src_ref=v_ref, dst_ref=out_ref.at[me], send_sem=send_sems.at[r - 1],
                recv_sem=recv_sems.at[r - 1], device_id=(px, py, pc), device_id_type=MESH)
            cp.start()
            sends.append(cp)
        for r in range(1, N_DEV):
            px, py, pc = _peer(x, y, c, r)
            pid = 4 * px + 2 * py + pc
            pltpu.make_async_remote_copy(
                src_ref=v_ref, dst_ref=out_ref.at[pid], send_sem=send_sems.at[r - 1],
                recv_sem=recv_sems.at[r - 1], device_id=(px, py, pc), device_id_type=MESH).wait_recv()
        for cp in sends:
            cp.wait_send()

    return pl.pallas_call(
        body, name="all_gather_small",
        out_shape=jax.ShapeDtypeStruct((N_DEV, 1, n), F32),
        in_specs=[VMEM_SPEC], out_specs=VMEM_SPEC,
        scratch_shapes=[pltpu.SemaphoreType.DMA((N_DEV - 1,)), pltpu.SemaphoreType.DMA((N_DEV - 1,))],
        compiler_params=_params(),
    )(v)


def _mm_nt(a, b, name, out_dtype, bias=None, carry=None):
    m, k = a.shape
    n = b.shape[0]
    tm = _pick(m, (512, 256, 128))
    tn = _pick(n, (1408, 1152, 1024, 768, 512, 256, 128))

    def body(*refs):
        acc = _dot_nt(refs[0][...], refs[1][...])
        if bias is not None:
            acc = acc + refs[2][...]
        refs[-1][...] = acc.astype(out_dtype)

    in_specs = [pl.BlockSpec((tm, k), lambda i, j: (i, 0)), pl.BlockSpec((tn, k), lambda i, j: (j, 0))]
    args = [a, b]
    if bias is not None:
        in_specs.append(pl.BlockSpec((1, tn), lambda i, j: (0, j)))
        args.append(bias)
    return _call(body, name=name, grid=(m // tm, n // tn), in_specs=in_specs,
                 out_specs=pl.BlockSpec((tm, tn), lambda i, j: (i, j)),
                 out_shape=jax.ShapeDtypeStruct((m, n), out_dtype), args=args,
                 sem=("parallel", "parallel"), carry=carry)


def _mm_nn(pairs, name, out_dtype, bias=None, carry=None):
    m, k = pairs[0][0].shape
    n = pairs[0][1].shape[1]
    tm = _pick(m, (512, 256, 128))
    tk = _pick(k, (1408, 1152, 1024, 768, 512, 256, 128))
    nk = k // tk
    n_p = len(pairs)

    def body(*refs):
        o_ref, acc_ref = refs[-2], refs[-1]
        kk = pl.program_id(1)

        @pl.when(kk == 0)
        def _():
            acc_ref[...] = jnp.zeros_like(acc_ref)

        for p in range(n_p):
            acc_ref[...] += _dot(refs[2 * p][...], refs[2 * p + 1][...])

        @pl.when(kk == nk - 1)
        def _():
            acc = acc_ref[...]
            if bias is not None:
                acc = acc + refs[2 * n_p][...]
            o_ref[...] = acc.astype(out_dtype)

    in_specs, args = [], []
    for a, b in pairs:
        in_specs += [pl.BlockSpec((tm, tk), lambda i, kk: (i, kk)), pl.BlockSpec((tk, n), lambda i, kk: (kk, 0))]
        args += [a, b]
    if bias is not None:
        in_specs.append(pl.BlockSpec((1, n), lambda i, kk: (0, 0)))
        args.append(bias)
    return _call(body, name=name, grid=(m // tm, nk), in_specs=in_specs,
                 out_specs=pl.BlockSpec((tm, n), lambda i, kk: (i, 0)),
                 out_shape=jax.ShapeDtypeStruct((m, n), out_dtype), args=args,
                 scratch=[pltpu.VMEM((tm, n), F32)], sem=("parallel", "arbitrary"), carry=carry)


def _mm_tn(a, b, name, out_dtype=BF16, carry=None):
    k, m = a.shape
    n = b.shape[1]
    tm = _pick(m, (1408, 1152, 1024, 768, 512, 256, 128))
    tk = _pick(k, (512, 256, 128))
    nk = k // tk

    def body(a_ref, b_ref, o_ref, acc_ref):
        kk = pl.program_id(1)

        @pl.when(kk == 0)
        def _():
            acc_ref[...] = jnp.zeros_like(acc_ref)

        acc_ref[...] += _dot_tn(a_ref[...], b_ref[...])

        @pl.when(kk == nk - 1)
        def _():
            o_ref[...] = acc_ref[...].astype(out_dtype)

    return _call(body, name=name, grid=(m // tm, nk),
                 in_specs=[pl.BlockSpec((tk, tm), lambda i, kk: (kk, i)), pl.BlockSpec((tk, n), lambda i, kk: (kk, 0))],
                 out_specs=pl.BlockSpec((tm, n), lambda i, kk: (i, 0)),
                 out_shape=jax.ShapeDtypeStruct((m, n), out_dtype), args=[a, b],
                 scratch=[pltpu.VMEM((tm, n), F32)], sem=("parallel", "arbitrary"), carry=carry)


def _mm_tn_pair(a, b, name, carry=None):
    k, m = a.shape
    n = b.shape[1]
    rows = m // N_DEV
    n_chip = N_DEV // 2
    tm = 4 * rows
    tk = _pick(k, (512, 256, 128))
    nk = k // tk

    def body(a_ref, b_ref, p_ref, acc_ref, keep_ref, send_ref, land_ref, send_sems, recv_sems):
        i, kk = pl.program_id(0), pl.program_id(1)
        x, y, c = _mesh_pos()

        def push(chip):
            return pltpu.make_async_remote_copy(
                src_ref=send_ref.at[chip], dst_ref=land_ref.at[chip], send_sem=send_sems.at[chip],
                recv_sem=recv_sems.at[chip], device_id=(x, y, 1 - c), device_id_type=MESH)

        @pl.when(kk == 0)
        def _():
            acc_ref[...] = jnp.zeros_like(acc_ref)

        acc_ref[...] += _dot_tn(a_ref[...], b_ref[...])

        for t in range(2):
            @pl.when((kk == nk - 1) & (i == t))
            def _(t=t):
                for ob in range(4):
                    chip, core = 2 * t + ob // 2, ob % 2
                    blk = acc_ref[ob * rows:(ob + 1) * rows, :]

                    @pl.when(c == core)
                    def _(chip=chip, blk=blk):
                        keep_ref[chip] = blk

                    @pl.when(c != core)
                    def _(chip=chip, blk=blk):
                        send_ref[chip] = blk.astype(BF16)
                        push(chip).start()

        @pl.when((kk == nk - 1) & (i == 1))
        def _():
            for chip in range(n_chip):
                push(chip).wait_recv()
                p_ref[chip * rows:(chip + 1) * rows, :] = (
                    keep_ref[chip] + land_ref[chip].astype(F32)).astype(BF16)
            for chip in range(n_chip):
                push(chip).wait_send()

    return _call(body, name=name, grid=(2, nk),
                 in_specs=[pl.BlockSpec((tk, tm), lambda i, kk: (kk, i)), pl.BlockSpec((tk, n), lambda i, kk: (kk, 0))],
                 out_specs=pl.BlockSpec((n_chip * rows, n), lambda i, kk: (0, 0)),
                 out_shape=jax.ShapeDtypeStruct((n_chip * rows, n), BF16), args=[a, b],
                 scratch=[pltpu.VMEM((tm, n), F32), pltpu.VMEM((n_chip, rows, n), F32),
                          pltpu.VMEM((n_chip, rows, n), BF16), pltpu.VMEM((n_chip, rows, n), BF16),
                          pltpu.SemaphoreType.DMA((n_chip,)), pltpu.SemaphoreType.DMA((n_chip,))],
                 sem=("arbitrary", "arbitrary"), carry=carry)


def _ffn_up(h, wg_t, wu_t, name, carry=None):
    s, d = h.shape
    f = wg_t.shape[0]
    tm = _pick(s, (512, 256, 128))
    tf = _pick(f, (1408, 1024, 512, 256, 128))

    def body(h_ref, wg_ref, wu_ref, a_ref, b_ref, u_ref):
        hh = h_ref[...]
        a = _dot_nt(hh, wg_ref[...])
        b = _dot_nt(hh, wu_ref[...])
        a_ref[...] = a.astype(BF16)
        b_ref[...] = b.astype(BF16)
        u_ref[...] = ((a * _sigmoid(a)) * b).astype(BF16)

    w_spec = pl.BlockSpec((tf, d), lambda i, j: (j, 0))
    o_spec = pl.BlockSpec((tm, tf), lambda i, j: (i, j))
    o_shape = jax.ShapeDtypeStruct((s, f), BF16)
    return _call(body, name=name, grid=(s // tm, f // tf),
                 in_specs=[pl.BlockSpec((tm, d), lambda i, j: (i, 0)), w_spec, w_spec],
                 out_specs=(o_spec, o_spec, o_spec), out_shape=(o_shape, o_shape, o_shape),
                 args=[h, wg_t, wu_t], sem=("parallel", "parallel"), carry=carry)


def _ffn_down_bwd(dy, wd, a, b, name, carry=None):
    s, d = dy.shape
    f = wd.shape[0]
    tm = _pick(s, (512, 256, 128))
    tf = _pick(f, (1408, 1024, 512, 256, 128))

    def body(dy_ref, wd_ref, a_ref, b_ref, da_ref, db_ref):
        du = _dot_nt(dy_ref[...], wd_ref[...])
        a = a_ref[...].astype(F32)
        b = b_ref[...].astype(F32)
        sig = _sigmoid(a)
        da_ref[...] = (du * b * (sig * (1.0 + a * (1.0 - sig)))).astype(BF16)
        db_ref[...] = (du * (a * sig)).astype(BF16)

    t_spec = pl.BlockSpec((tm, tf), lambda i, j: (i, j))
    o_shape = jax.ShapeDtypeStruct((s, f), BF16)
    return _call(body, name=name, grid=(s // tm, f // tf),
                 in_specs=[pl.BlockSpec((tm, d), lambda i, j: (i, 0)), pl.BlockSpec((tf, d), lambda i, j: (j, 0)),
                           t_spec, t_spec],
                 out_specs=(t_spec, t_spec), out_shape=(o_shape, o_shape), args=[dy, wd, a, b],
                 sem=("parallel", "parallel"), carry=carry)


def _row_tile(s):
    return _pick(s, (256, 128, 64))


def _vec_spec(d):
    return pl.BlockSpec((1, d), lambda i: (0, 0))


def _pre_norm(x, g, scale, shift, name):
    s, d = x.shape
    ts = _row_tile(s)

    def body(x_ref, g_ref, sc_ref, sh_ref, h_ref):
        xv = x_ref[...]
        r = lax.rsqrt(jnp.mean(xv * xv, axis=-1, keepdims=True) + EPS)
        h_ref[...] = (((xv * r) * g_ref[...]) * (1.0 + sc_ref[...]) + sh_ref[...]).astype(BF16)

    row = pl.BlockSpec((ts, d), lambda i: (i, 0))
    return _call(body, name=name, grid=(s // ts,), in_specs=[row, _vec_spec(d), _vec_spec(d), _vec_spec(d)],
                 out_specs=row, out_shape=jax.ShapeDtypeStruct((s, d), BF16), args=[x, g, scale, shift],
                 sem=("parallel",))


def _post_norm_residual(x, y, g, gate, weight, name):
    s, d = x.shape
    ts = _row_tile(s)

    def body(x_ref, y_ref, g_ref, gate_ref, o_ref):
        yv = y_ref[...]
        r = lax.rsqrt(jnp.mean(yv * yv, axis=-1, keepdims=True) + EPS)
        o_ref[...] = x_ref[...] + (weight * gate_ref[...]) * ((yv * r) * g_ref[...])

    row = pl.BlockSpec((ts, d), lambda i: (i, 0))
    return _call(body, name=name, grid=(s // ts,), in_specs=[row, row, _vec_spec(d), _vec_spec(d)],
                 out_specs=row, out_shape=jax.ShapeDtypeStruct((s, d), F32), args=[x, y, g, gate],
                 sem=("parallel",))


def _post_norm_bwd(dout, y, g, gate, weight, name):
    s, d = y.shape
    ts = _row_tile(s)

    def body(do_ref, y_ref, g_ref, gate_ref, dy_ref, s1_ref, cs_ref):
        @pl.when(pl.program_id(0) == 0)
        def _():
            s1_ref[...] = jnp.zeros_like(s1_ref)
            cs_ref[...] = jnp.zeros_like(cs_ref)

        yv = y_ref[...]
        do = do_ref[...]
        r = lax.rsqrt(jnp.mean(yv * yv, axis=-1, keepdims=True) + EPS)
        yn = yv * r
        dyn = do * ((weight * gate_ref[...]) * g_ref[...])
        dy = r * (dyn - yn * jnp.mean(dyn * yn, axis=-1, keepdims=True))
        dy_ref[...] = dy.astype(BF16)
        s1_ref[...] += jnp.sum(do * yn, axis=0, keepdims=True)
        cs_ref[...] += jnp.sum(dy, axis=0, keepdims=True)

    row = pl.BlockSpec((ts, d), lambda i: (i, 0))
    vec = jax.ShapeDtypeStruct((1, d), F32)
    return _call(body, name=name, grid=(s // ts,), in_specs=[row, row, _vec_spec(d), _vec_spec(d)],
                 out_specs=(row, _vec_spec(d), _vec_spec(d)),
                 out_shape=(jax.ShapeDtypeStruct((s, d), BF16), vec, vec), args=[dout, y, g, gate],
                 sem=("arbitrary",))


def _pre_norm_bwd(dh, x, g, scale, dres, name):
    s, d = x.shape
    ts = _row_tile(s)

    def body(dh_ref, x_ref, g_ref, sc_ref, dr_ref, dx_ref, s2_ref, s3_ref):
        @pl.when(pl.program_id(0) == 0)
        def _():
            s2_ref[...] = jnp.zeros_like(s2_ref)
            s3_ref[...] = jnp.zeros_like(s3_ref)

        xv = x_ref[...]
        dh = dh_ref[...]
        r = lax.rsqrt(jnp.mean(xv * xv, axis=-1, keepdims=True) + EPS)
        n = xv * r
        dn = dh * (g_ref[...] * (1.0 + sc_ref[...]))
        dx_ref[...] = dr_ref[...] + r * (dn - n * jnp.mean(dn * n, axis=-1, keepdims=True))
        s2_ref[...] += jnp.sum(dh * n, axis=0, keepdims=True)
        s3_ref[...] += jnp.sum(dh, axis=0, keepdims=True)

    row = pl.BlockSpec((ts, d), lambda i: (i, 0))
    vec = jax.ShapeDtypeStruct((1, d), F32)
    return _call(body, name=name, grid=(s // ts,), in_specs=[row, row, _vec_spec(d), _vec_spec(d), row],
                 out_specs=(row, _vec_spec(d), _vec_spec(d)),
                 out_shape=(jax.ShapeDtypeStruct((s, d), F32), vec, vec), args=[dh, x, g, scale, dres],
                 sem=("arbitrary",))


def _group_norm_cat(oa, ob, ga, gb):
    s = oa.shape[0]
    ts = _row_tile(s)

    def body(oa_ref, ob_ref, ga_ref, gb_ref, y_ref):
        for o_ref, g_ref, lo, w in ((oa_ref, ga_ref, 0, QA), (ob_ref, gb_ref, QA, QB)):
            ov = o_ref[...]
            r = lax.rsqrt(jnp.mean(ov * ov, axis=-1, keepdims=True) + EPS)
            y_ref[:, lo:lo + w] = ((ov * r) * g_ref[...]).astype(BF16)

    return _call(body, name="group_norm_cat", grid=(s // ts,),
                 in_specs=[pl.BlockSpec((ts, QA), lambda i: (i, 0)), pl.BlockSpec((ts, QB), lambda i: (i, 0)),
                           _vec_spec(QA), _vec_spec(QB)],
                 out_specs=pl.BlockSpec((ts, QA + QB), lambda i: (i, 0)),
                 out_shape=jax.ShapeDtypeStruct((s, QA + QB), BF16), args=[oa, ob, ga, gb], sem=("parallel",))


def _group_norm_bwd(dy, oa, ob, ga, gb):
    s = oa.shape[0]
    ts = _row_tile(s)

    def body(dy_ref, oa_ref, ob_ref, ga_ref, gb_ref, doa_ref, dob_ref, dga_ref, dgb_ref):
        @pl.when(pl.program_id(0) == 0)
        def _():
            dga_ref[...] = jnp.zeros_like(dga_ref)
            dgb_ref[...] = jnp.zeros_like(dgb_ref)

        for o_ref, g_ref, do_ref, dg_ref, lo, w in ((oa_ref, ga_ref, doa_ref, dga_ref, 0, QA),
                                                    (ob_ref, gb_ref, dob_ref, dgb_ref, QA, QB)):
            ov = o_ref[...]
            dyv = dy_ref[:, lo:lo + w]
            r = lax.rsqrt(jnp.mean(ov * ov, axis=-1, keepdims=True) + EPS)
            n = ov * r
            dn = dyv * g_ref[...]
            do_ref[...] = r * (dn - n * jnp.mean(dn * n, axis=-1, keepdims=True))
            dg_ref[...] += jnp.sum(dyv * n, axis=0, keepdims=True)

    ra = pl.BlockSpec((ts, QA), lambda i: (i, 0))
    rb = pl.BlockSpec((ts, QB), lambda i: (i, 0))
    return _call(body, name="group_norm_bwd", grid=(s // ts,),
                 in_specs=[pl.BlockSpec((ts, QA + QB), lambda i: (i, 0)), ra, rb, _vec_spec(QA), _vec_spec(QB)],
                 out_specs=(ra, rb, _vec_spec(QA), _vec_spec(QB)),
                 out_shape=(jax.ShapeDtypeStruct((s, QA), F32), jax.ShapeDtypeStruct((s, QB), F32),
                            jax.ShapeDtypeStruct((1, QA), F32), jax.ShapeDtypeStruct((1, QB), F32)),
                 args=[dy, oa, ob, ga, gb], sem=("arbitrary",))


def _loss_and_grad(y, target):
    s, d = y.shape
    ts = _row_tile(s)

    def body(y_ref, t_ref, l_ref, g_ref):
        @pl.when(pl.program_id(0) == 0)
        def _():
            l_ref[...] = jnp.zeros_like(l_ref)

        err = y_ref[...] - t_ref[...]
        g_ref[...] = err * (1.0 / d)
        row = jnp.mean(err * err, axis=-1, keepdims=True)
        l_ref[...] += 0.5 * jnp.sum(row, axis=0, keepdims=True)

    row = pl.BlockSpec((ts, d), lambda i: (i, 0))
    return _call(body, name="loss_and_grad", grid=(s // ts,), in_specs=[row, row],
                 out_specs=(pl.BlockSpec((1, 1), lambda i: (0, 0)), row),
                 out_shape=(jax.ShapeDtypeStruct((1, 1), F32), jax.ShapeDtypeStruct((s, d), F32)),
                 args=[y, target], sem=("arbitrary",))


def _col_sum(x, name):
    s, n = x.shape
    ts = _row_tile(s)

    def body(x_ref, o_ref):
        @pl.when(pl.program_id(0) == 0)
        def _():
            o_ref[...] = jnp.zeros_like(o_ref)

        o_ref[...] += jnp.sum(x_ref[...].astype(F32), axis=0, keepdims=True)

    return _call(body, name=name, grid=(s // ts,), in_specs=[pl.BlockSpec((ts, n), lambda i: (i, 0))],
                 out_specs=pl.BlockSpec((1, n), lambda i: (0, 0)), out_shape=jax.ShapeDtypeStruct((1, n), F32),
                 args=[x], sem=("arbitrary",))


def _alibi_bias():
    i = np.arange(QROWS)[:, None]
    j = np.arange((QG + BACK_A) * CHUNK)[None, :]
    dist = np.abs(BACK_A * CHUNK + i - j).astype(np.float32)
    dc = j // CHUNK - i // CHUNK
    valid = (dc >= 0) & (dc <= BACK_A)
    slopes = np.array([2.0 ** (-8.0 * (h + 1) / H_A) for h in range(H_A)], dtype=np.float32)
    bias = -slopes[:, None, None] * dist[None]
    return jnp.asarray(np.where(valid[None], bias, np.float32(NEG_INF)).astype(np.float32))


def _rel_index_matrix():
    cc = np.arange(SKEW)
    dist = np.where(cc < SKEW - QROWS, BACK_B * CHUNK - cc, BACK_B * CHUNK + SKEW - cc)
    idx = np.clip(dist, -REL_CLIP, REL_CLIP) + REL_CLIP
    m = np.zeros((SKEW, N_REL), np.float32)
    m[cc, idx] = 1.0
    return jnp.asarray(m)


def _toeplitz_bias(vec, carry=None):
    lk = (QG + BACK_B) * CHUNK

    def body(v_ref, o_ref):
        xv = jnp.broadcast_to(v_ref[0], (QROWS, SKEW))
        row = lax.broadcasted_iota(jnp.int32, (QROWS, SKEW), 0)
        for bit in range(QROWS.bit_length() - 1):
            xv = jnp.where((row >> bit) & 1 == 1, pltpu.roll(xv, 1 << bit, 1), xv)
        ri = lax.broadcasted_iota(jnp.int32, (QROWS, lk), 0) // CHUNK
        ci = lax.broadcasted_iota(jnp.int32, (QROWS, lk), 1) // CHUNK
        valid = (ci - ri >= 0) & (ci - ri <= BACK_B)
        o_ref[0] = jnp.where(valid, xv[:, :lk], NEG_INF)

    return _call(body, name="toeplitz_bias", grid=(H_B,),
                 in_specs=[pl.BlockSpec((1, 1, SKEW), lambda h: (h, 0, 0))],
                 out_specs=pl.BlockSpec((1, QROWS, lk), lambda h: (h, 0, 0)),
                 out_shape=jax.ShapeDtypeStruct((H_B, QROWS, lk), F32), args=[vec], sem=("parallel",),
                 carry=carry)


def _diagonal_sums(dbias):
    lk = dbias.shape[2]

    def body(d_ref, o_ref):
        xv = jnp.concatenate([d_ref[0], jnp.zeros((QROWS, SKEW - lk), F32)], axis=1)
        row = lax.broadcasted_iota(jnp.int32, (QROWS, SKEW), 0)
        for bit in range(QROWS.bit_length() - 1):
            xv = jnp.where((row >> bit) & 1 == 1, pltpu.roll(xv, SKEW - (1 << bit), 1), xv)
        o_ref[0] = jnp.sum(xv, axis=0, keepdims=True)

    return _call(body, name="diagonal_sums", grid=(H_B,),
                 in_specs=[pl.BlockSpec((1, QROWS, lk), lambda h: (h, 0, 0))],
                 out_specs=pl.BlockSpec((1, 1, SKEW), lambda h: (h, 0, 0)),
                 out_shape=jax.ShapeDtypeStruct((H_B, 1, SKEW), F32), args=[dbias], sem=("parallel",))


def _attn_common(s, n_back, gqa, q_col, k_col, v_col):
    lk = (QG + n_back) * CHUNK
    pad = n_back * CHUNK
    q_spec = pl.BlockSpec((QROWS, LANES), lambda t, g: (g, q_col + t))
    if gqa:
        k_spec = pl.BlockSpec((s, LANES), lambda t, g: (0, k_col))
        v_spec = pl.BlockSpec((s, LANES), lambda t, g: (0, v_col))
    else:
        k_spec = pl.BlockSpec((s, LANES), lambda t, g: (0, k_col + t))
        v_spec = pl.BlockSpec((s, LANES), lambda t, g: (0, v_col + t))
    bias_spec = pl.BlockSpec((2, QROWS, lk), lambda t, g: (t, 0, 0))
    tile_spec = pl.BlockSpec((QROWS, LANES), lambda t, g: (g, t))
    return lk, pad, q_spec, k_spec, v_spec, bias_spec, tile_spec


def _attention_fwd(proj, bias, sinks, *, n_back, gqa, q_col, k_col, v_col, name, carry=None):
    s = proj.shape[0]
    lk, pad, q_spec, k_spec, v_spec, bias_spec, tile_spec = _attn_common(s, n_back, gqa, q_col, k_col, v_col)
    n_t, n_g = 512 // LANES, s // QROWS

    def body(*refs):
        if gqa:
            q_ref, k_ref, v_ref, bias_ref, sink_ref, o_ref, l_ref, kpad, vpad = refs
        else:
            q_ref, k_ref, v_ref, bias_ref, o_ref, l_ref, kpad, vpad = refs
        t, g = pl.program_id(0), pl.program_id(1)

        @pl.when(g == 0)
        def _():
            kpad[0:pad, :] = jnp.zeros((pad, LANES), BF16)
            vpad[0:pad, :] = jnp.zeros((pad, LANES), BF16)
            kpad[pad:, :] = k_ref[...]
            vpad[pad:, :] = v_ref[...]

        start = pl.multiple_of(g * QROWS, QROWS)
        kb = kpad[pl.ds(start, lk), :]
        vb = vpad[pl.ds(start, lk), :]
        half = lax.broadcasted_iota(jnp.int32, (QROWS, LANES), 1) // HEAD_DIM
        col_ok = lax.broadcasted_iota(jnp.int32, (QROWS, lk), 1) >= (n_back - QG * g) * CHUNK
        q = q_ref[...]
        if gqa:
            hk = t // 2
            q_rolled = pltpu.roll(q.astype(F32), HEAD_DIM, 1).astype(BF16)
        outs, lses = [], []
        for e in range(2):
            if gqa:
                kv_half = hk
                src = jnp.where(hk == e, q, q_rolled)
            else:
                kv_half = e
                src = q
            qm = jnp.where(half == kv_half, src, jnp.zeros_like(src))
            sc = _dot_nt(qm, kb) * (HEAD_DIM ** -0.5) + bias_ref[e]
            sc = jnp.where(col_ok, sc, NEG_INF)
            m = jnp.max(sc, axis=-1, keepdims=True)
            if gqa:
                sk = sink_ref[2 * t + e]
                m = jnp.maximum(m, sk)
            p = jnp.exp(sc - m)
            l = jnp.sum(p, axis=-1, keepdims=True)
            if gqa:
                l = l + jnp.exp(sk - m)
            pn = p / l
            outs.append(_dot(pn.astype(BF16), vb))
            lses.append(m + jnp.log(l))
        if gqa:
            same = jnp.where(hk == 0, outs[0], outs[1])
            other = jnp.where(hk == 0, outs[1], outs[0])
            o_ref[...] = jnp.where(half == hk, same, pltpu.roll(other, HEAD_DIM, 1))
        else:
            o_ref[...] = jnp.where(half == 0, outs[0], outs[1])
        l_ref[...] = jnp.where(half == 0, lses[0], lses[1])

    in_specs = [q_spec, k_spec, v_spec, bias_spec] + ([SMEM_SPEC] if gqa else [])
    args = [proj, proj, proj, bias] + ([sinks] if gqa else [])
    o_shape = jax.ShapeDtypeStruct((s, 512), F32)
    return _call(body, name=name, grid=(n_t, n_g), in_specs=in_specs, out_specs=(tile_spec, tile_spec),
                 out_shape=(o_shape, o_shape), args=args,
                 scratch=[pltpu.VMEM((s + pad, LANES), BF16), pltpu.VMEM((s + pad, LANES), BF16)],
                 sem=("arbitrary", "arbitrary"), carry=carry)


def _attention_bwd(proj, bias, sinks, do, lse, *, n_back, gqa, q_col, k_col, v_col, name, carry=None):
    s = proj.shape[0]
    lk, pad, q_spec, k_spec, v_spec, bias_spec, tile_spec = _attn_common(s, n_back, gqa, q_col, k_col, v_col)
    n_t, n_g = 512 // LANES, s // QROWS

    def body(*refs):
        if gqa:
            (q_ref, k_ref, v_ref, bias_ref, sink_ref, do_ref, l_ref,
             dq_ref, dk_ref, dv_ref, dsink_ref, kpad, vpad, dkpad, dvpad) = refs
        else:
            (q_ref, k_ref, v_ref, bias_ref, do_ref, l_ref,
             dq_ref, dk_ref, dv_ref, dbias_ref, kpad, vpad, dkpad, dvpad) = refs
        t, g = pl.program_id(0), pl.program_id(1)

        @pl.when(g == 0)
        def _():
            kpad[0:pad, :] = jnp.zeros((pad, LANES), BF16)
            vpad[0:pad, :] = jnp.zeros((pad, LANES), BF16)
            kpad[pad:, :] = k_ref[...]
            vpad[pad:, :] = v_ref[...]
            if gqa:
                dsink_ref[...] = jnp.zeros_like(dsink_ref)
            else:
                dbias_ref[...] = jnp.zeros_like(dbias_ref)

        @pl.when((g == 0) & (t == 0) if gqa else g == 0)
        def _():
            dkpad[...] = jnp.zeros_like(dkpad)
            dvpad[...] = jnp.zeros_like(dvpad)

        start = pl.multiple_of(g * QROWS, QROWS)
        kb = kpad[pl.ds(start, lk), :]
        vb = vpad[pl.ds(start, lk), :]
        half = lax.broadcasted_iota(jnp.int32, (QROWS, LANES), 1) // HEAD_DIM
        col_ok = lax.broadcasted_iota(jnp.int32, (QROWS, lk), 1) >= (n_back - QG * g) * CHUNK
        q = q_ref[...]
        dov = do_ref[...]
        lv = l_ref[...]
        if gqa:
            hk = t // 2
            q_rolled = pltpu.roll(q.astype(F32), HEAD_DIM, 1).astype(BF16)
            do_rolled = pltpu.roll(dov, HEAD_DIM, 1)
        dqs = []
        dk_acc = jnp.zeros((lk, LANES), F32)
        dv_acc = jnp.zeros((lk, LANES), F32)
        for e in range(2):
            if gqa:
                kv_half = hk
                src = jnp.where(hk == e, q, q_rolled)
                do_src = jnp.where(hk == e, dov, do_rolled)
            else:
                kv_half = e
                src = q
                do_src = dov
            qm = jnp.where(half == kv_half, src, jnp.zeros_like(src))
            dom = jnp.where(half == kv_half, do_src, 0.0).astype(BF16)
            lcol = jnp.max(jnp.where(half == e, lv, -jnp.inf), axis=-1, keepdims=True)
            sc = _dot_nt(qm, kb) * (HEAD_DIM ** -0.5) + bias_ref[e]
            sc = jnp.where(col_ok, sc, NEG_INF)
            pn = jnp.exp(sc - lcol)
            dp = _dot_nt(dom, vb)
            delta = jnp.sum(pn * dp, axis=-1, keepdims=True)
            ds = pn * (dp - delta)
            if gqa:
                p_sink = jnp.exp(sink_ref[2 * t + e] - lcol)
                dsk = -jnp.sum(p_sink * delta, axis=0, keepdims=True)
                dsink_ref[0, e:e + 1, :] += jnp.broadcast_to(dsk, (1, LANES))
            else:
                dbias_ref[e] += ds
            dsb = (ds * (HEAD_DIM ** -0.5)).astype(BF16)
            dqs.append(_dot(dsb, kb))
            dk_acc = dk_acc + _dot_tn(dsb, qm)
            dv_acc = dv_acc + _dot_tn(pn.astype(BF16), dom)
        dkpad[pl.ds(start, lk), :] += dk_acc
        dvpad[pl.ds(start, lk), :] += dv_acc
        if gqa:
            same = jnp.where(hk == 0, dqs[0], dqs[1])
            other = jnp.where(hk == 0, dqs[1], dqs[0])
            dq_ref[...] = jnp.where(half == hk, same, pltpu.roll(other, HEAD_DIM, 1)).astype(BF16)
        else:
            dq_ref[...] = jnp.where(half == 0, dqs[0], dqs[1]).astype(BF16)

        @pl.when((g == n_g - 1) & (t == n_t - 1) if gqa else g == n_g - 1)
        def _():
            dk_ref[...] = dkpad[pad:, :].astype(BF16)
            dv_ref[...] = dvpad[pad:, :].astype(BF16)

    in_specs = [q_spec, k_spec, v_spec, bias_spec] + ([SMEM_SPEC] if gqa else []) + [tile_spec, tile_spec]
    args = [proj, proj, proj, bias] + ([sinks] if gqa else []) + [do, lse]
    if gqa:
        kv_out = pl.BlockSpec((s, LANES), lambda t, g: (0, 0))
        kv_shape = jax.ShapeDtypeStruct((s, LANES), BF16)
        extra_spec = pl.BlockSpec((1, 8, LANES), lambda t, g: (t, 0, 0))
        extra_shape = jax.ShapeDtypeStruct((n_t, 8, LANES), F32)
    else:
        kv_out = pl.BlockSpec((s, LANES), lambda t, g: (0, t))
        kv_shape = jax.ShapeDtypeStruct((s, 512), BF16)
        extra_spec = bias_spec
        extra_shape = jax.ShapeDtypeStruct(bias.shape, F32)
    return _call(body, name=name, grid=(n_t, n_g), in_specs=in_specs,
                 out_specs=(tile_spec, kv_out, kv_out, extra_spec),
                 out_shape=(jax.ShapeDtypeStruct((s, 512), BF16), kv_shape, kv_shape, extra_shape), args=args,
                 scratch=[pltpu.VMEM((s + pad, LANES), BF16), pltpu.VMEM((s + pad, LANES), BF16),
                          pltpu.VMEM((s + pad, LANES), F32), pltpu.VMEM((s + pad, LANES), F32)],
                 sem=("arbitrary", "arbitrary"), carry=carry)


def _sum_slots(r, name):
    n_slots, rows, k = r.shape

    def body(r_ref, o_ref):
        acc = r_ref[0].astype(F32)
        for j in range(1, n_slots):
            acc = acc + r_ref[j].astype(F32)
        o_ref[...] = acc

    return _call(body, name=name, grid=(k // LANES,),
                 in_specs=[pl.BlockSpec((n_slots, rows, LANES), lambda i: (0, 0, i))],
                 out_specs=pl.BlockSpec((rows, LANES), lambda i: (0, i)),
                 out_shape=jax.ShapeDtypeStruct((rows, k), F32), args=[r], sem=("parallel",))


def _sum_rows8(g):
    n = g.shape[2]

    def body(g_ref, o_ref):
        acc = g_ref[0]
        for j in range(1, N_DEV):
            acc = acc + g_ref[j]
        o_ref[...] = acc

    return pl.pallas_call(
        body, name="sum_small_grads", in_specs=[VMEM_SPEC], out_specs=VMEM_SPEC,
        out_shape=jax.ShapeDtypeStruct((1, n), F32), compiler_params=_params(),
    )(g)


def _ada_weight_grad(sc_t, dmod_cols):
    d = sc_t.shape[0]
    w = dmod_cols.shape[1]
    td = _pick(d, (256, 128))

    def body(sc_ref, dm_ref, o_ref):
        scv = sc_ref[...]
        dmv = dm_ref[...]
        acc = scv[:, 0:1] * dmv[0:1, :]
        for b in range(1, N_DEV):
            acc = acc + scv[:, b:b + 1] * dmv[b:b + 1, :]
        o_ref[...] = acc

    return _call(body, name="ada_weight_grad", grid=(d // td,),
                 in_specs=[pl.BlockSpec((td, N_DEV), lambda i: (i, 0)), pl.BlockSpec((N_DEV, w), lambda i: (0, 0))],
                 out_specs=pl.BlockSpec((td, w), lambda i: (i, 0)), out_shape=jax.ShapeDtypeStruct((d, w), F32),
                 args=[sc_t, dmod_cols], sem=("parallel",))


def _adamw_update(w, gv, m, v):
    nm = ADAM_B1 * m + (1.0 - ADAM_B1) * gv
    nv = ADAM_B2 * v + (1.0 - ADAM_B2) * (gv * gv)
    m_hat = nm / (1.0 - ADAM_B1 ** ADAM_STEP)
    v_hat = nv / (1.0 - ADAM_B2 ** ADAM_STEP)
    return -ADAM_LR * (m_hat / (jnp.sqrt(v_hat) + ADAM_EPS) + ADAM_WD * w), nm, nv


def _adamw(w, g, m, v, name):
    rows, cols = w.shape
    tr = _pick(rows, (256, 176, 128, 88, 64)) if rows > 256 else rows

    def body(w_ref, g_ref, m_ref, v_ref, d_ref, nm_ref, nv_ref):
        d_ref[...], nm_ref[...], nv_ref[...] = _adamw_update(w_ref[...], g_ref[...], m_ref[...], v_ref[...])

    spec = pl.BlockSpec((tr, cols), lambda i: (i, 0))
    shape = jax.ShapeDtypeStruct((rows, cols), F32)
    return _call(body, name=name, grid=(rows // tr,), in_specs=[spec] * 4, out_specs=(spec, spec, spec),
                 out_shape=(shape, shape, shape), args=[w, g, m, v], sem=("parallel",))


def _adamw_from_slots(w, slots, m, v, transposed, name):
    n_slots, rows, k = slots.shape
    rows_pad = -rows % LANES

    def body(s_ref, w_ref, m_ref, v_ref, g_ref, d_ref, nm_ref, nv_ref):
        gv = s_ref[0].astype(F32)
        for j in range(1, n_slots):
            gv = gv + s_ref[j].astype(F32)
        if transposed:
            if rows_pad:
                gv = jnp.concatenate([gv, jnp.zeros((rows_pad, LANES), F32)], axis=0)
            gv = gv.T[:, :rows]
        g_ref[...] = gv
        d_ref[...], nm_ref[...], nv_ref[...] = _adamw_update(w_ref[...], gv, m_ref[...], v_ref[...])

    if transposed:
        spec = pl.BlockSpec((LANES, rows), lambda i: (i, 0))
        shape = jax.ShapeDtypeStruct((k, rows), F32)
    else:
        spec = pl.BlockSpec((rows, LANES), lambda i: (0, i))
        shape = jax.ShapeDtypeStruct((rows, k), F32)
    return _call(body, name=name, grid=(k // LANES,),
                 in_specs=[pl.BlockSpec((n_slots, rows, LANES), lambda i: (0, 0, i)), spec, spec, spec],
                 out_specs=(spec, spec, spec, spec), out_shape=(shape, shape, shape, shape),
                 args=[slots, w, m, v], sem=("parallel",))


SMALL = ("b_ada", "g_pre_ffn1", "g_post_ffn1", "g_pre_mix", "b_in", "sinks_a", "rel_bias_b", "g_grp_a",
         "g_grp_b", "b_out", "g_post_mix", "g_pre_ffn2", "g_post_ffn2")
WEIGHTS = ("w_ada", "b_ada", "g_pre_ffn1", "w_gate1", "w_up1", "w_down1", "g_post_ffn1", "g_pre_mix", "w_in",
           "b_in", "sinks_a", "rel_bias_b", "g_grp_a", "g_grp_b", "w_out", "b_out", "g_post_mix", "g_pre_ffn2",
           "w_gate2", "w_up2", "w_down2", "g_post_ffn2")


def kernel(x, c, w_ada, b_ada, g_pre_ffn1, w_gate1, w_up1, w_down1, g_post_ffn1, g_pre_mix, w_in, b_in, sinks_a, rel_bias_b, g_grp_a, g_grp_b, w_out, b_out, g_post_mix, g_pre_ffn2, w_gate2, w_up2, w_down2, g_post_ffn2, loss_target, m_w_ada, m_b_ada, m_g_pre_ffn1, m_w_gate1, m_w_up1, m_w_down1, m_g_post_ffn1, m_g_pre_mix, m_w_in, m_b_in, m_sinks_a, m_rel_bias_b, m_g_grp_a, m_g_grp_b, m_w_out, m_b_out, m_g_post_mix, m_g_pre_ffn2, m_w_gate2, m_w_up2, m_w_down2, m_g_post_ffn2, v_w_ada, v_b_ada, v_g_pre_ffn1, v_w_gate1, v_w_up1, v_w_down1, v_g_post_ffn1, v_g_pre_mix, v_w_in, v_b_in, v_sinks_a, v_rel_bias_b, v_g_grp_a, v_g_grp_b, v_w_out, v_b_out, v_g_post_mix, v_g_pre_ffn2, v_w_gate2, v_w_up2, v_w_down2, v_g_post_ffn2):
    given = dict(locals())
    weights = {n: given[n] for n in WEIGHTS}
    mom_m = {n: given["m_" + n] for n in WEIGHTS}
    mom_v = {n: given["v_" + n] for n in WEIGHTS}

    me = 4 * lax.axis_index("x") + 2 * lax.axis_index("y") + lax.axis_index("c")
    xs = x[0]
    tgt = loss_target[0]
    d_model = xs.shape[1]
    ada_cols = w_ada.shape[2]

    sh = {"wg1": w_gate1[0].T, "wu1": w_up1[0].T, "wd1": w_down1[0], "win": w_in[0].T, "wo": w_out[0],
          "wg2": w_gate2[0].T, "wu2": w_up2[0].T, "wd2": w_down2[0]}
    sh = {k: v.astype(BF16) for k, v in sh.items()}

    def gather(*names):
        return _gather_carry([sh[n] for n in names])

    bias_a = _alibi_bias()
    rel_m = _rel_index_matrix()
    rel_vec = jnp.dot(rel_bias_b[0], rel_m.T, precision=lax.Precision.HIGHEST)
    bias_b, (wg1,) = _toeplitz_bias(rel_vec.reshape(H_B, 1, SKEW), carry=gather("wg1"))

    b_cols = lax.dynamic_slice(b_ada, (0, me * ada_cols), (1, ada_cols))
    (sc_all, mod_rows), (wu1,) = _ada_forward(c, w_ada[0], b_cols, gather("wu1"))
    mod = mod_rows.reshape(N_MOD, d_model)
    shift1, scale1, gate1, shift2, scale2, gate2, shift3, scale3, gate3 = (mod[i:i + 1] for i in range(N_MOD))

    def ffn_forward(xin, g_pre, g_post, shift, scale, gate, wg, wu, wd_name, next_name, tag):
        h = _pre_norm(xin, g_pre, scale, shift, "pre_norm_" + tag)
        (a, b, u), (wd,) = _ffn_up(h, wg, wu, "ffn_up_" + tag, carry=gather(wd_name))
        if next_name is None:
            y, nxt = _mm_nn([(u, wd)], "ffn_down_" + tag, F32), None
        else:
            y, (nxt,) = _mm_nn([(u, wd)], "ffn_down_" + tag, F32, carry=gather(next_name))
        xout = _post_norm_residual(xin, y, g_post, gate, 0.5, "post_norm_" + tag)
        return xout, (xin, h, a, b, u, y), wd, nxt

    x1, saved1, wd1, win = ffn_forward(xs, g_pre_ffn1, g_post_ffn1, shift1, scale1, gate1, wg1, wu1,
                                       "wd1", "win", "ffn1")

    h2 = _pre_norm(x1, g_pre_mix, scale2, shift2, "pre_norm_mix")
    proj, (wo,) = _mm_nt(h2, win, "in_proj", BF16, bias=b_in, carry=gather("wo"))
    sinks = sinks_a[0]
    cfg_a = dict(n_back=BACK_A, gqa=True, q_col=0, k_col=QA // LANES, v_col=(QA + KVA) // LANES)
    cfg_b = dict(n_back=BACK_B, gqa=False, q_col=(QA + 2 * KVA) // LANES, k_col=(QA + 2 * KVA + QB) // LANES,
                 v_col=(QA + 2 * KVA + 2 * QB) // LANES)
    (oa, lse_a), (wg2,) = _attention_fwd(proj, bias_a, sinks, name="attn_a", carry=gather("wg2"), **cfg_a)
    (ob, lse_b), (wu2,) = _attention_fwd(proj, bias_b, None, name="attn_b", carry=gather("wu2"), **cfg_b)
    ycat = _group_norm_cat(oa, ob, g_grp_a, g_grp_b)
    ymix = _mm_nn([(ycat, wo)], "out_proj", F32, bias=b_out)
    x2 = _post_norm_residual(x1, ymix, g_post_mix, gate2, 1.0, "post_norm_mix")

    x3, saved3, wd2, _ = ffn_forward(x2, g_pre_ffn2, g_post_ffn2, shift3, scale3, gate3, wg2, wu2,
                                     "wd2", None, "ffn2")

    loss_part, dx3 = _loss_and_grad(x3, tgt)

    def scatter(*grads):
        return _scatter_carry(list(grads))

    slots = {}

    xin, h, a, b, u, y = saved3
    dy, s1, _ = _post_norm_bwd(dx3, y, g_post_ffn2, gate3, 0.5, "post_norm_bwd_ffn2")
    da, db = _ffn_down_bwd(dy, wd2, a, b, "ffn_down_bwd_ffn2")
    dwd2 = _mm_tn_pair(u, dy, "grad_wd_ffn2")
    dwg2 = _mm_tn_pair(da, h, "grad_wg_ffn2")
    dwu2 = _mm_tn_pair(db, h, "grad_wu_ffn2")
    dh, (slots["wd2"],) = _mm_nn([(da, wg2), (db, wu2)], "ffn_up_bwd_ffn2", F32, carry=scatter(dwd2))
    dx2, s2, s3 = _pre_norm_bwd(dh, xin, g_pre_ffn2, scale3, dx3, "pre_norm_bwd_ffn2")
    sm3 = dict(shift=s3, scale=s2 * g_pre_ffn2, gate=0.5 * g_post_ffn2 * s1,
               g_pre=(1.0 + scale3) * s2, g_post=(0.5 * gate3) * s1)

    dymix, s1m, db_out = _post_norm_bwd(dx2, ymix, g_post_mix, gate2, 1.0, "post_norm_bwd_mix")
    dycat = _mm_nt(dymix, wo, "out_proj_bwd", F32)
    dwo = _mm_tn_pair(ycat, dymix, "grad_wo")
    doa, dob, dg_a, dg_b = _group_norm_bwd(dycat, oa, ob, g_grp_a, g_grp_b)
    (dqa, dka, dva, dsink), (slots["wg2"],) = _attention_bwd(
        proj, bias_a, sinks, doa, lse_a, name="attn_a_bwd", carry=scatter(dwg2), **cfg_a)
    (dqb, dkb, dvb, dbias), (slots["wu2"], slots["wo"]) = _attention_bwd(
        proj, bias_b, None, dob, lse_b, name="attn_b_bwd", carry=scatter(dwu2, dwo), **cfg_b)
    dproj = jnp.concatenate([dqa, dka, dva, dqb, dkb, dvb], axis=1)
    db_in = _col_sum(dproj, "grad_b_in")
    dwin = _mm_tn_pair(dproj, h2, "grad_win")
    dh2 = _mm_nn([(dproj, win)], "in_proj_bwd", F32)
    dx1, s2m, s3m = _pre_norm_bwd(dh2, x1, g_pre_mix, scale2, dx2, "pre_norm_bwd_mix")
    d_rel = jnp.dot(_diagonal_sums(dbias).reshape(H_B, SKEW), rel_m, precision=lax.Precision.HIGHEST)
    d_sinks = dsink[:, :2, 0].reshape(1, H_A)

    xin, h, a, b, u, y = saved1
    dy, s1, _ = _post_norm_bwd(dx1, y, g_post_ffn1, gate1, 0.5, "post_norm_bwd_ffn1")
    (da, db), (slots["win"],) = _ffn_down_bwd(dy, wd1, a, b, "ffn_down_bwd_ffn1", carry=scatter(dwin))
    dwd1 = _mm_tn_pair(u, dy, "grad_wd_ffn1")
    dwg1, (slots["wd1"],) = _mm_tn_pair(da, h, "grad_wg_ffn1", carry=scatter(dwd1))
    dwu1, (slots["wg1"],) = _mm_tn_pair(db, h, "grad_wu_ffn1", carry=scatter(dwg1))
    dh, (slots["wu1"],) = _mm_nn([(da, wg1), (db, wu1)], "ffn_up_bwd_ffn1", F32, carry=scatter(dwu1))
    dx0, s2, s3 = _pre_norm_bwd(dh, xin, g_pre_ffn1, scale1, dx1, "pre_norm_bwd_ffn1")
    sm1 = dict(shift=s3, scale=s2 * g_pre_ffn1, gate=0.5 * g_post_ffn1 * s1,
               g_pre=(1.0 + scale1) * s2, g_post=(0.5 * gate1) * s1)

    dmod = jnp.concatenate([sm1["shift"], sm1["scale"], sm1["gate"],
                            s3m, s2m * g_pre_mix, g_post_mix * s1m,
                            sm3["shift"], sm3["scale"], sm3["gate"]], axis=1)
    small_parts = {
        "b_ada": dmod, "g_pre_ffn1": sm1["g_pre"], "g_post_ffn1": sm1["g_post"],
        "g_pre_mix": (1.0 + scale2) * s2m, "b_in": db_in, "sinks_a": d_sinks,
        "rel_bias_b": d_rel.reshape(1, H_B * N_REL), "g_grp_a": dg_a, "g_grp_b": dg_b, "b_out": db_out,
        "g_post_mix": gate2 * s1m, "g_pre_ffn2": sm3["g_pre"], "g_post_ffn2": sm3["g_post"]}
    sizes = [small_parts[n].shape[1] for n in SMALL]
    n_small = sum(sizes)
    n_pad = -(n_small + 1) % LANES
    packed = jnp.concatenate([small_parts[n] for n in SMALL] + [loss_part, jnp.zeros((1, n_pad), F32)], axis=1)
    gathered = _all_gather_small(packed)
    small_sum = _sum_rows8(gathered)
    loss = small_sum[0, n_small]
    dmod_cols = lax.dynamic_slice(gathered.reshape(N_DEV, n_small + 1 + n_pad), (0, me * ada_cols),
                                  (N_DEV, ada_cols))
    g_ada = _ada_weight_grad(sc_all.reshape(N_DEV, d_model).T, dmod_cols)

    out_g, out_d, out_m, out_v = {}, {}, {}, {}
    d_, m_, v_ = _adamw(w_ada[0], g_ada, m_w_ada[0], v_w_ada[0], "adamw_w_ada")
    out_g["w_ada"], out_d["w_ada"], out_m["w_ada"], out_v["w_ada"] = g_ada[None], d_[None], m_[None], v_[None]
    for n, key, transposed in (("w_gate1", "wg1", True), ("w_up1", "wu1", True), ("w_down1", "wd1", False),
                               ("w_in", "win", True), ("w_out", "wo", False), ("w_gate2", "wg2", True),
                               ("w_up2", "wu2", True), ("w_down2", "wd2", False)):
        g_, d_, m_, v_ = _adamw_from_slots(weights[n][0], slots[key], mom_m[n][0], mom_v[n][0], transposed,
                                           "adamw_" + n)
        out_g[n], out_d[n], out_m[n], out_v[n] = g_[None], d_[None], m_[None], v_[None]

    def pack(tree):
        return jnp.concatenate([tree[n].reshape(1, -1) for n in SMALL], axis=1)

    g_small = small_sum[:, :n_small]
    d_s, m_s, v_s = _adamw(pack(weights), g_small, pack(mom_m), pack(mom_v), "adamw_small")
    off = 0
    for n, size in zip(SMALL, sizes):
        shape = weights[n].shape
        out_g[n] = g_small[:, off:off + size].reshape(shape)
        out_d[n] = d_s[:, off:off + size].reshape(shape)
        out_m[n] = m_s[:, off:off + size].reshape(shape)
        out_v[n] = v_s[:, off:off + size].reshape(shape)
        off += size

    return (loss, dx0[None], *[out_g[n] for n in WEIGHTS], *[out_d[n] for n in WEIGHTS],
            *[out_m[n] for n in WEIGHTS], *[out_v[n] for n in WEIGHTS])
```

```python
import numpy as np
import jax
import jax.numpy as jnp
from jax import lax
from jax.experimental import pallas as pl
from jax.experimental.pallas import tpu as pltpu

F32 = jnp.float32
BF16 = jnp.bfloat16
MESH = pl.DeviceIdType.MESH
ANY = pl.BlockSpec(memory_space=pl.ANY)
VMEM_SPEC = pl.BlockSpec(memory_space=pltpu.VMEM)
SMEM_SPEC = pl.BlockSpec(memory_space=pltpu.SMEM)

N_DEV = 8
CHUNK = 64
HEAD_DIM = 64
LANES = 128
H_A, KV_A, H_B = 8, 2, 8
BACK_A, BACK_B = 2, 8
REL_CLIP = 128
N_REL = 2 * REL_CLIP + 1
QA, KVA, QB = H_A * HEAD_DIM, KV_A * HEAD_DIM, H_B * HEAD_DIM
D_IN = QA + 2 * KVA + 3 * QB
N_MOD = 9
EPS = 1e-6
NEG_INF = -1e30
QG = 4
QROWS = QG * CHUNK
SKEW = 1024
ADAM_LR, ADAM_B1, ADAM_B2, ADAM_EPS, ADAM_WD, ADAM_STEP = 0.001, 0.9, 0.999, 1e-08, 0.01, 10
VMEM_LIMIT = 56 * 2 ** 20


def _pick(n, cands):
    for c in cands:
        if n % c == 0:
            return c
    return n


def _params(sem=None):
    return pltpu.CompilerParams(dimension_semantics=sem, vmem_limit_bytes=VMEM_LIMIT)


def _dot_nt(a, b):
    return lax.dot_general(a, b, (((1,), (1,)), ((), ())), preferred_element_type=F32)


def _dot_tn(a, b):
    return lax.dot_general(a, b, (((0,), (0,)), ((), ())), preferred_element_type=F32)


def _dot(a, b):
    return jnp.dot(a, b, preferred_element_type=F32)


def _sigmoid(a):
    return 1.0 / (1.0 + jnp.exp(-a))


def _mesh_pos():
    return lax.axis_index("x"), lax.axis_index("y"), lax.axis_index("c")


def _peer(x, y, c, r):
    px = 1 - x if r & 4 else x
    py = 1 - y if r & 2 else y
    pc = 1 - c if r & 1 else c
    return px, py, pc


class _Carry:
    def __init__(self, ins, out_shapes, scratch, start, finish):
        self.ins, self.out_shapes, self.scratch = list(ins), list(out_shapes), list(scratch)
        self.start, self.finish = start, finish


def _call(body, *, name, grid, in_specs, out_specs, out_shape, args, scratch=(), sem=None, carry=None):
    single = not isinstance(out_shape, (tuple, list))
    out_specs = (out_specs,) if single else tuple(out_specs)
    out_shape = (out_shape,) if single else tuple(out_shape)
    if carry is None:
        res = pl.pallas_call(body, name=name, grid=grid, in_specs=list(in_specs), out_specs=out_specs,
                             out_shape=out_shape, scratch_shapes=list(scratch), compiler_params=_params(sem))(*args)
        return res[0] if single else res
    n_in, n_out, n_s = len(in_specs), len(out_shape), len(scratch)
    ci, co = len(carry.ins), len(carry.out_shapes)

    def wrapped(*refs):
        ins, cins = refs[:n_in], refs[n_in:n_in + ci]
        outs = refs[n_in + ci:n_in + ci + n_out]
        couts = refs[n_in + ci + n_out:n_in + ci + n_out + co]
        scr = refs[n_in + ci + n_out + co:n_in + ci + n_out + co + n_s]
        cscr = refs[n_in + ci + n_out + co + n_s:]
        first, last = None, None
        for ax, n in enumerate(grid):
            f, l = pl.program_id(ax) == 0, pl.program_id(ax) == n - 1
            first = f if first is None else first & f
            last = l if last is None else last & l
        pl.when(first)(lambda: carry.start(cins, couts, cscr))
        body(*ins, *outs, *scr)
        pl.when(last)(lambda: carry.finish(cins, couts, cscr))

    res = pl.pallas_call(
        wrapped, name=name, grid=grid, in_specs=list(in_specs) + [ANY] * ci, out_specs=out_specs + (ANY,) * co,
        out_shape=out_shape + tuple(carry.out_shapes), scratch_shapes=list(scratch) + carry.scratch,
        compiler_params=_params(("arbitrary",) * len(grid)))(*args, *carry.ins)
    main = res[:n_out]
    return (main[0] if single else main), res[n_out:]


def _run_carry(carry, name):
    ci, co = len(carry.ins), len(carry.out_shapes)

    def body(*refs):
        carry.start(refs[:ci], refs[ci:ci + co], refs[ci + co:])
        carry.finish(refs[:ci], refs[ci:ci + co], refs[ci + co:])

    return pl.pallas_call(body, name=name, in_specs=[ANY] * ci, out_specs=(ANY,) * co,
                          out_shape=tuple(carry.out_shapes), scratch_shapes=carry.scratch,
                          compiler_params=_params())(*carry.ins)


def _gather_carry(shards):
    n_w = len(shards)
    rows = [s.shape[0] for s in shards]

    def plan(ins, outs, scr):
        send_sems, recv_sems, local_sems = scr
        x, y, c = _mesh_pos()
        me, sibling = (x, y, c), (x, y, 1 - c)
        chips = [(1 - x, y), (x, 1 - y), (1 - x, 1 - y)]

        def block(w, dev):
            start = pl.multiple_of((4 * dev[0] + 2 * dev[1] + dev[2]) * rows[w], 16)
            return outs[w].at[pl.ds(start, rows[w]), :]

        def copy(w, k, dev, to, src=None):
            return pltpu.make_async_remote_copy(
                src_ref=block(w, dev) if src is None else src, dst_ref=block(w, dev),
                send_sem=send_sems.at[w, k], recv_sem=recv_sems.at[w, k], device_id=to, device_id_type=MESH)

        mine = [pltpu.make_async_copy(ins[w], block(w, me), local_sems.at[w]) for w in range(n_w)]
        first = []
        for j, chip in enumerate(chips):
            first += [copy(w, 1 + j, me, (*chip, c), src=ins[w]) for w in range(n_w)]
        first += [copy(w, 0, me, sibling, src=ins[w]) for w in range(n_w)]
        return c, me, sibling, chips, copy, mine, first

    def start(ins, outs, scr):
        _, _, _, _, _, mine, first = plan(ins, outs, scr)
        for cp in mine + first:
            cp.start()

    def finish(ins, outs, scr):
        c, me, sibling, chips, copy, mine, first = plan(ins, outs, scr)
        passed = []
        for j, chip in enumerate(chips):
            for w in range(n_w):
                copy(w, 1 + j, (*chip, c), me).wait_recv()
                cp = copy(w, 4 + j, (*chip, c), sibling)
                cp.start()
                passed.append(cp)
        for w in range(n_w):
            copy(w, 0, sibling, me).wait_recv()
        for j, chip in enumerate(chips):
            for w in range(n_w):
                copy(w, 4 + j, (*chip, 1 - c), me).wait_recv()
        for cp in first + passed:
            cp.wait_send()
        for cp in mine:
            cp.wait()

    return _Carry(
        shards, [jax.ShapeDtypeStruct((N_DEV * s.shape[0], s.shape[1]), s.dtype) for s in shards],
        [pltpu.SemaphoreType.DMA((n_w, N_DEV - 1)), pltpu.SemaphoreType.DMA((n_w, N_DEV - 1)),
         pltpu.SemaphoreType.DMA((n_w,))], start, finish)


def _scatter_carry(parts):
    n_w = len(parts)
    n_chip = N_DEV // 2
    rows = [g.shape[0] // n_chip for g in parts]

    def plan(ins, outs, scr):
        send_sems, recv_sems, local_sems = scr
        x, y, c = _mesh_pos()

        def src(w, chip_index):
            return ins[w].at[pl.ds(pl.multiple_of(chip_index * rows[w], 16), rows[w]), :]

        mine = [pltpu.make_async_copy(src(w, 2 * x + y), outs[w].at[0], local_sems.at[w]) for w in range(n_w)]
        copies = []
        for r in (3, 2, 1):
            px, py, _ = _peer(x, y, c, 2 * r)
            for w in range(n_w):
                copies.append(pltpu.make_async_remote_copy(
                    src_ref=src(w, 2 * px + py), dst_ref=outs[w].at[r], send_sem=send_sems.at[w, r - 1],
                    recv_sem=recv_sems.at[w, r - 1], device_id=(px, py, c), device_id_type=MESH))
        return mine, copies

    def start(ins, outs, scr):
        mine, copies = plan(ins, outs, scr)
        for cp in mine + copies:
            cp.start()

    def finish(ins, outs, scr):
        mine, copies = plan(ins, outs, scr)
        for cp in copies:
            cp.wait_recv()
        for cp in copies:
            cp.wait_send()
        for cp in mine:
            cp.wait()

    return _Carry(
        parts, [jax.ShapeDtypeStruct((n_chip, r, g.shape[1]), g.dtype) for r, g in zip(rows, parts)],
        [pltpu.SemaphoreType.DMA((n_w, n_chip - 1)), pltpu.SemaphoreType.DMA((n_w, n_chip - 1)),
         pltpu.SemaphoreType.DMA((n_w,))], start, finish)


def _ada_forward(c_row, w_ada, b_cols, carry):
    d = c_row.shape[1]
    wcols = w_ada.shape[1]
    ci, co = len(carry.ins), len(carry.out_shapes)

    def body(*refs):
        c_ref, w_ref, b_ref = refs[:3]
        cins = refs[3:3 + ci]
        sc_ref, mod_ref = refs[3 + ci:5 + ci]
        couts = refs[5 + ci:5 + ci + co]
        rows_ref, send_sems, recv_sems = refs[5 + ci + co:8 + ci + co]
        cscr = refs[8 + ci + co:]
        carry.start(cins, couts, cscr)
        x, y, c = _mesh_pos()
        me = 4 * x + 2 * y + c
        cv = c_ref[...]
        sc_ref[me] = cv * _sigmoid(cv)

        sends = []
        for r in range(1, N_DEV):
            px, py, pc = _peer(x, y, c, r)
            cp = pltpu.make_async_remote_copy(
                src_ref=sc_ref.at[me], dst_ref=sc_ref.at[me], send_sem=send_sems.at[0, r - 1],
                recv_sem=recv_sems.at[0, r - 1], device_id=(px, py, pc), device_id_type=MESH)
            cp.start()
            sends.append(cp)
        for r in range(1, N_DEV):
            px, py, pc = _peer(x, y, c, r)
            pid = 4 * px + 2 * py + pc
            pltpu.make_async_remote_copy(
                src_ref=sc_ref.at[pid], dst_ref=sc_ref.at[pid], send_sem=send_sems.at[0, r - 1],
                recv_sem=recv_sems.at[0, r - 1], device_id=(px, py, pc), device_id_type=MESH).wait_recv()
        for cp in sends:
            cp.wait_send()

        sc_all = jnp.concatenate([sc_ref[j] for j in range(N_DEV)], axis=0)
        rows = _dot(sc_all.astype(BF16), w_ref[...].astype(BF16)) + b_ref[...]
        for j in range(N_DEV):
            rows_ref[j] = rows[j:j + 1, :]
        mod_ref[me] = rows_ref[me]

        sends = []
        for r in range(1, N_DEV):
            px, py, pc = _peer(x, y, c, r)
            pid = 4 * px + 2 * py + pc
            cp = pltpu.make_async_remote_copy(
                src_ref=rows_ref.at[pid], dst_ref=mod_ref.at[me], send_sem=send_sems.at[1, r - 1],
                recv_sem=recv_sems.at[1, r - 1], device_id=(px, py, pc), device_id_type=MESH)
            cp.start()
            sends.append(cp)
        for r in range(1, N_DEV):
            px, py, pc = _peer(x, y, c, r)
            pid = 4 * px + 2 * py + pc
            pltpu.make_async_remote_copy(
                src_ref=rows_ref.at[pid], dst_ref=mod_ref.at[pid], send_sem=send_sems.at[1, r - 1],
                recv_sem=recv_sems.at[1, r - 1], device_id=(px, py, pc), device_id_type=MESH).wait_recv()
        for cp in sends:
            cp.wait_send()
        carry.finish(cins, couts, cscr)

    res = pl.pallas_call(
        body, name="ada_forward",
        out_shape=(jax.ShapeDtypeStruct((N_DEV, 1, d), F32), jax.ShapeDtypeStruct((N_DEV, 1, wcols), F32),
                   *carry.out_shapes),
        in_specs=[VMEM_SPEC, VMEM_SPEC, VMEM_SPEC] + [ANY] * ci, out_specs=(VMEM_SPEC, VMEM_SPEC) + (ANY,) * co,
        scratch_shapes=[pltpu.VMEM((N_DEV, 1, wcols), F32), pltpu.SemaphoreType.DMA((2, N_DEV - 1)),
                        pltpu.SemaphoreType.DMA((2, N_DEV - 1))] + carry.scratch,
        compiler_params=_params(),
    )(c_row, w_ada, b_cols, *carry.ins)
    return res[:2], res[2:]


def _all_gather_small(v):
    n = v.shape[1]

    def body(v_ref, out_ref, send_sems, recv_sems):
        x, y, c = _mesh_pos()
        me = 4 * x + 2 * y + c
        out_ref[me] = v_ref[...]
        sends = []
        for r in range(1, N_DEV):
            px, py, pc = _peer(x, y, c, r)
            cp = pltpu.make_async_remote_copy(
                src_ref=v_ref, dst_ref=out_ref.at[me], send_sem=send_sems.at[r - 1],
                recv_sem=recv_sems.at[r - 1], device_id=(px, py, pc), device_id_type=MESH)
            cp.start()
            sends.append(cp)
        for r in range(1, N_DEV):
            px, py, pc = _peer(x, y, c, r)
            pid = 4 * px + 2 * py + pc
            pltpu.make_async_remote_copy(
                src_ref=v_ref, dst_ref=out_ref.at[pid], send_sem=send_sems.at[r - 1],
                recv_sem=recv_sems.at[r - 1], device_id=(px, py, pc), device_id_type=MESH).wait_recv()
        for cp in sends:
            cp.wait_send()

    return pl.pallas_call(
        body, name="all_gather_small",
        out_shape=jax.ShapeDtypeStruct((N_DEV, 1, n), F32),
        in_specs=[VMEM_SPEC], out_specs=VMEM_SPEC,
        scratch_shapes=[pltpu.SemaphoreType.DMA((N_DEV - 1,)), pltpu.SemaphoreType.DMA((N_DEV - 1,))],
        compiler_params=_params(),
    )(v)


def _mm_nt(a, b, name, out_dtype, bias=None, carry=None):
    m, k = a.shape
    n = b.shape[0]
    tm = _pick(m, (512, 256, 128))
    tn = _pick(n, (1408, 1152, 1024, 768, 512, 256, 128))

    def body(*refs):
        acc = _dot_nt(refs[0][...], refs[1][...])
        if bias is not None:
            acc = acc + refs[2][...]
        refs[-1][...] = acc.astype(out_dtype)

    in_specs = [pl.BlockSpec((tm, k), lambda i, j: (i, 0)), pl.BlockSpec((tn, k), lambda i, j: (j, 0))]
    args = [a, b]
    if bias is not None:
        in_specs.append(pl.BlockSpec((1, tn), lambda i, j: (0, j)))
        args.append(bias)
    return _call(body, name=name, grid=(m // tm, n // tn), in_specs=in_specs,
                 out_specs=pl.BlockSpec((tm, tn), lambda i, j: (i, j)),
                 out_shape=jax.ShapeDtypeStruct((m, n), out_dtype), args=args,
                 sem=("parallel", "parallel"), carry=carry)


def _mm_nn(pairs, name, out_dtype, bias=None, carry=None):
    m, k = pairs[0][0].shape
    n = pairs[0][1].shape[1]
    tm = _pick(m, (512, 256, 128))
    tk = _pick(k, (1408, 1152, 1024, 768, 512, 256, 128))
    nk = k // tk
    n_p = len(pairs)

    def body(*refs):
        o_ref, acc_ref = refs[-2], refs[-1]
        kk = pl.program_id(1)

        @pl.when(kk == 0)
        def _():
            acc_ref[...] = jnp.zeros_like(acc_ref)

        for p in range(n_p):
            acc_ref[...] += _dot(refs[2 * p][...], refs[2 * p + 1][...])

        @pl.when(kk == nk - 1)
        def _():
            acc = acc_ref[...]
            if bias is not None:
                acc = acc + refs[2 * n_p][...]
            o_ref[...] = acc.astype(out_dtype)

    in_specs, args = [], []
    for a, b in pairs:
        in_specs += [pl.BlockSpec((tm, tk), lambda i, kk: (i, kk)), pl.BlockSpec((tk, n), lambda i, kk: (kk, 0))]
        args += [a, b]
    if bias is not None:
        in_specs.append(pl.BlockSpec((1, n), lambda i, kk: (0, 0)))
        args.append(bias)
    return _call(body, name=name, grid=(m // tm, nk), in_specs=in_specs,
                 out_specs=pl.BlockSpec((tm, n), lambda i, kk: (i, 0)),
                 out_shape=jax.ShapeDtypeStruct((m, n), out_dtype), args=args,
                 scratch=[pltpu.VMEM((tm, n), F32)], sem=("parallel", "arbitrary"), carry=carry)


def _mm_tn(a, b, name, out_dtype=BF16, carry=None):
    k, m = a.shape
    n = b.shape[1]
    tm = _pick(m, (1408, 1152, 1024, 768, 512, 256, 128))
    tk = _pick(k, (512, 256, 128))
    nk = k // tk

    def body(a_ref, b_ref, o_ref, acc_ref):
        kk = pl.program_id(1)

        @pl.when(kk == 0)
        def _():
            acc_ref[...] = jnp.zeros_like(acc_ref)

        acc_ref[...] += _dot_tn(a_ref[...], b_ref[...])

        @pl.when(kk == nk - 1)
        def _():
            o_ref[...] = acc_ref[...].astype(out_dtype)

    return _call(body, name=name, grid=(m // tm, nk),
                 in_specs=[pl.BlockSpec((tk, tm), lambda i, kk: (kk, i)), pl.BlockSpec((tk, n), lambda i, kk: (kk, 0))],
                 out_specs=pl.BlockSpec((tm, n), lambda i, kk: (i, 0)),
                 out_shape=jax.ShapeDtypeStruct((m, n), out_dtype), args=[a, b],
                 scratch=[pltpu.VMEM((tm, n), F32)], sem=("parallel", "arbitrary"), carry=carry)


def _mm_tn_pair(a, b, name, carry=None):
    k, m = a.shape
    n = b.shape[1]
    rows = m // N_DEV
    n_chip = N_DEV // 2
    tm = 4 * rows
    tk = _pick(k, (512, 256, 128))
    nk = k // tk

    def body(a_ref, b_ref, p_ref, acc_ref, keep_ref, send_ref, land_ref, send_sems, recv_sems):
        i, kk = pl.program_id(0), pl.program_id(1)
        x, y, c = _mesh_pos()

        def push(chip):
            return pltpu.make_async_remote_copy(
                src_ref=send_ref.at[chip], dst_ref=land_ref.at[chip], send_sem=send_sems.at[chip],
                recv_sem=recv_sems.at[chip], device_id=(x, y, 1 - c), device_id_type=MESH)

        @pl.when(kk == 0)
        def _():
            acc_ref[...] = jnp.zeros_like(acc_ref)

        acc_ref[...] += _dot_tn(a_ref[...], b_ref[...])

        for t in range(2):
            @pl.when((kk == nk - 1) & (i == t))
            def _(t=t):
                for ob in range(4):
                    chip, core = 2 * t + ob // 2, ob % 2
                    blk = acc_ref[ob * rows:(ob + 1) * rows, :]

                    @pl.when(c == core)
                    def _(chip=chip, blk=blk):
                        keep_ref[chip] = blk

                    @pl.when(c != core)
                    def _(chip=chip, blk=blk):
                        send_ref[chip] = blk.astype(BF16)
                        push(chip).start()

        @pl.when((kk == nk - 1) & (i == 1))
        def _():
            for chip in range(n_chip):
                push(chip).wait_recv()
                p_ref[chip * rows:(chip + 1) * rows, :] = (
                    keep_ref[chip] + land_ref[chip].astype(F32)).astype(BF16)
            for chip in range(n_chip):
                push(chip).wait_send()

    return _call(body, name=name, grid=(2, nk),
                 in_specs=[pl.BlockSpec((tk, tm), lambda i, kk: (kk, i)), pl.BlockSpec((tk, n), lambda i, kk: (kk, 0))],
                 out_specs=pl.BlockSpec((n_chip * rows, n), lambda i, kk: (0, 0)),
                 out_shape=jax.ShapeDtypeStruct((n_chip * rows, n), BF16), args=[a, b],
                 scratch=[pltpu.VMEM((tm, n), F32), pltpu.VMEM((n_chip, rows, n), F32),
                          pltpu.VMEM((n_chip, rows, n), BF16), pltpu.VMEM((n_chip, rows, n), BF16),
                          pltpu.SemaphoreType.DMA((n_chip,)), pltpu.SemaphoreType.DMA((n_chip,))],
                 sem=("arbitrary", "arbitrary"), carry=carry)


def _ffn_up(h, wg_t, wu_t, name, carry=None):
    s, d = h.shape
    f = wg_t.shape[0]
    tm = _pick(s, (512, 256, 128))
    tf = _pick(f, (1408, 1024, 512, 256, 128))

    def body(h_ref, wg_ref, wu_ref, a_ref, b_ref, u_ref):
        hh = h_ref[...]
        a = _dot_nt(hh, wg_ref[...])
        b = _dot_nt(hh, wu_ref[...])
        a_ref[...] = a.astype(BF16)
        b_ref[...] = b.astype(BF16)
        u_ref[...] = ((a * _sigmoid(a)) * b).astype(BF16)

    w_spec = pl.BlockSpec((tf, d), lambda i, j: (j, 0))
    o_spec = pl.BlockSpec((tm, tf), lambda i, j: (i, j))
    o_shape = jax.ShapeDtypeStruct((s, f), BF16)
    return _call(body, name=name, grid=(s // tm, f // tf),
                 in_specs=[pl.BlockSpec((tm, d), lambda i, j: (i, 0)), w_spec, w_spec],
                 out_specs=(o_spec, o_spec, o_spec), out_shape=(o_shape, o_shape, o_shape),
                 args=[h, wg_t, wu_t], sem=("parallel", "parallel"), carry=carry)


def _ffn_down_bwd(dy, wd, a, b, name, carry=None):
    s, d = dy.shape
    f = wd.shape[0]
    tm = _pick(s, (512, 256, 128))
    tf = _pick(f, (1408, 1024, 512, 256, 128))

    def body(dy_ref, wd_ref, a_ref, b_ref, da_ref, db_ref):
        du = _dot_nt(dy_ref[...], wd_ref[...])
        a = a_ref[...].astype(F32)
        b = b_ref[...].astype(F32)
        sig = _sigmoid(a)
        da_ref[...] = (du * b * (sig * (1.0 + a * (1.0 - sig)))).astype(BF16)
        db_ref[...] = (du * (a * sig)).astype(BF16)

    t_spec = pl.BlockSpec((tm, tf), lambda i, j: (i, j))
    o_shape = jax.ShapeDtypeStruct((s, f), BF16)
    return _call(body, name=name, grid=(s // tm, f // tf),
                 in_specs=[pl.BlockSpec((tm, d), lambda i, j: (i, 0)), pl.BlockSpec((tf, d), lambda i, j: (j, 0)),
                           t_spec, t_spec],
                 out_specs=(t_spec, t_spec), out_shape=(o_shape, o_shape), args=[dy, wd, a, b],
                 sem=("parallel", "parallel"), carry=carry)


def _row_tile(s):
    return _pick(s, (256, 128, 64))


def _vec_spec(d):
    return pl.BlockSpec((1, d), lambda i: (0, 0))


def _pre_norm(x, g, scale, shift, name):
    s, d = x.shape
    ts = _row_tile(s)

    def body(x_ref, g_ref, sc_ref, sh_ref, h_ref):
        xv = x_ref[...]
        r = lax.rsqrt(jnp.mean(xv * xv, axis=-1, keepdims=True) + EPS)
        h_ref[...] = (((xv * r) * g_ref[...]) * (1.0 + sc_ref[...]) + sh_ref[...]).astype(BF16)

    row = pl.BlockSpec((ts, d), lambda i: (i, 0))
    return _call(body, name=name, grid=(s // ts,), in_specs=[row, _vec_spec(d), _vec_spec(d), _vec_spec(d)],
                 out_specs=row, out_shape=jax.ShapeDtypeStruct((s, d), BF16), args=[x, g, scale, shift],
                 sem=("parallel",))


def _post_norm_residual(x, y, g, gate, weight, name):
    s, d = x.shape
    ts = _row_tile(s)

    def body(x_ref, y_ref, g_ref, gate_ref, o_ref):
        yv = y_ref[...]
        r = lax.rsqrt(jnp.mean(yv * yv, axis=-1, keepdims=True) + EPS)
        o_ref[...] = x_ref[...] + (weight * gate_ref[...]) * ((yv * r) * g_ref[...])

    row = pl.BlockSpec((ts, d), lambda i: (i, 0))
    return _call(body, name=name, grid=(s // ts,), in_specs=[row, row, _vec_spec(d), _vec_spec(d)],
                 out_specs=row, out_shape=jax.ShapeDtypeStruct((s, d), F32), args=[x, y, g, gate],
                 sem=("parallel",))


def _post_norm_bwd(dout, y, g, gate, weight, name):
    s, d = y.shape
    ts = _row_tile(s)

    def body(do_ref, y_ref, g_ref, gate_ref, dy_ref, s1_ref, cs_ref):
        @pl.when(pl.program_id(0) == 0)
        def _():
            s1_ref[...] = jnp.zeros_like(s1_ref)
            cs_ref[...] = jnp.zeros_like(cs_ref)

        yv = y_ref[...]
        do = do_ref[...]
        r = lax.rsqrt(jnp.mean(yv * yv, axis=-1, keepdims=True) + EPS)
        yn = yv * r
        dyn = do * ((weight * gate_ref[...]) * g_ref[...])
        dy = r * (dyn - yn * jnp.mean(dyn * yn, axis=-1, keepdims=True))
        dy_ref[...] = dy.astype(BF16)
        s1_ref[...] += jnp.sum(do * yn, axis=0, keepdims=True)
        cs_ref[...] += jnp.sum(dy, axis=0, keepdims=True)

    row = pl.BlockSpec((ts, d), lambda i: (i, 0))
    vec = jax.ShapeDtypeStruct((1, d), F32)
    return _call(body, name=name, grid=(s // ts,), in_specs=[row, row, _vec_spec(d), _vec_spec(d)],
                 out_specs=(row, _vec_spec(d), _vec_spec(d)),
                 out_shape=(jax.ShapeDtypeStruct((s, d), BF16), vec, vec), args=[dout, y, g, gate],
                 sem=("arbitrary",))


def _pre_norm_bwd(dh, x, g, scale, dres, name):
    s, d = x.shape
    ts = _row_tile(s)

    def body(dh_ref, x_ref, g_ref, sc_ref, dr_ref, dx_ref, s2_ref, s3_ref):
        @pl.when(pl.program_id(0) == 0)
        def _():
            s2_ref[...] = jnp.zeros_like(s2_ref)
            s3_ref[...] = jnp.zeros_like(s3_ref)

        xv = x_ref[...]
        dh = dh_ref[...]
        r = lax.rsqrt(jnp.mean(xv * xv, axis=-1, keepdims=True) + EPS)
        n = xv * r
        dn = dh * (g_ref[...] * (1.0 + sc_ref[...]))
        dx_ref[...] = dr_ref[...] + r * (dn - n * jnp.mean(dn * n, axis=-1, keepdims=True))
        s2_ref[...] += jnp.sum(dh * n, axis=0, keepdims=True)
        s3_ref[...] += jnp.sum(dh, axis=0, keepdims=True)

    row = pl.BlockSpec((ts, d), lambda i: (i, 0))
    vec = jax.ShapeDtypeStruct((1, d), F32)
    return _call(body, name=name, grid=(s // ts,), in_specs=[row, row, _vec_spec(d), _vec_spec(d), row],
                 out_specs=(row, _vec_spec(d), _vec_spec(d)),
                 out_shape=(jax.ShapeDtypeStruct((s, d), F32), vec, vec), args=[dh, x, g, scale, dres],
                 sem=("arbitrary",))


def _post_pre_norm(x, y, g_post, gate, weight, g_pre, scale, shift, name):
    s, d = x.shape
    ts = _row_tile(s)

    def body(x_ref, y_ref, gp_ref, gate_ref, g_ref, sc_ref, sh_ref, o_ref, h_ref):
        yv = y_ref[...]
        r = lax.rsqrt(jnp.mean(yv * yv, axis=-1, keepdims=True) + EPS)
        xv = x_ref[...] + (weight * gate_ref[...]) * ((yv * r) * gp_ref[...])
        o_ref[...] = xv
        r2 = lax.rsqrt(jnp.mean(xv * xv, axis=-1, keepdims=True) + EPS)
        h_ref[...] = (((xv * r2) * g_ref[...]) * (1.0 + sc_ref[...]) + sh_ref[...]).astype(BF16)

    row = pl.BlockSpec((ts, d), lambda i: (i, 0))
    return _call(body, name=name, grid=(s // ts,), in_specs=[row, row] + [_vec_spec(d)] * 5,
                 out_specs=(row, row),
                 out_shape=(jax.ShapeDtypeStruct((s, d), F32), jax.ShapeDtypeStruct((s, d), BF16)),
                 args=[x, y, g_post, gate, g_pre, scale, shift], sem=("parallel",))


def _post_norm_loss_bwd(x, y, g, gate, weight, target, name):
    s, d = y.shape
    ts = _row_tile(s)

    def body(x_ref, y_ref, g_ref, gate_ref, t_ref, dx_ref, dy_ref, l_ref, s1_ref):
        @pl.when(pl.program_id(0) == 0)
        def _():
            l_ref[...] = jnp.zeros_like(l_ref)
            s1_ref[...] = jnp.zeros_like(s1_ref)

        yv = y_ref[...]
        r = lax.rsqrt(jnp.mean(yv * yv, axis=-1, keepdims=True) + EPS)
        yn = yv * r
        err = (x_ref[...] + (weight * gate_ref[...]) * (yn * g_ref[...])) - t_ref[...]
        do = err * (1.0 / d)
        dx_ref[...] = do
        l_ref[...] += 0.5 * jnp.sum(jnp.mean(err * err, axis=-1, keepdims=True), axis=0, keepdims=True)
        dyn = do * ((weight * gate_ref[...]) * g_ref[...])
        dy_ref[...] = (r * (dyn - yn * jnp.mean(dyn * yn, axis=-1, keepdims=True))).astype(BF16)
        s1_ref[...] += jnp.sum(do * yn, axis=0, keepdims=True)

    row = pl.BlockSpec((ts, d), lambda i: (i, 0))
    return _call(body, name=name, grid=(s // ts,), in_specs=[row, row, _vec_spec(d), _vec_spec(d), row],
                 out_specs=(row, row, pl.BlockSpec((1, 1), lambda i: (0, 0)), _vec_spec(d)),
                 out_shape=(jax.ShapeDtypeStruct((s, d), F32), jax.ShapeDtypeStruct((s, d), BF16),
                            jax.ShapeDtypeStruct((1, 1), F32), jax.ShapeDtypeStruct((1, d), F32)),
                 args=[x, y, g, gate, target], sem=("arbitrary",))


def _pre_post_norm_bwd(dh, x, g_pre, scale, dres, y, g_post, gate, weight, name):
    s, d = x.shape
    ts = _row_tile(s)

    def body(dh_ref, x_ref, g_ref, sc_ref, dr_ref, y_ref, gp_ref, gate_ref,
             dx_ref, dy_ref, s2_ref, s3_ref, s1_ref, cs_ref):
        @pl.when(pl.program_id(0) == 0)
        def _():
            for ref in (s2_ref, s3_ref, s1_ref, cs_ref):
                ref[...] = jnp.zeros_like(ref)

        xv = x_ref[...]
        dh = dh_ref[...]
        r = lax.rsqrt(jnp.mean(xv * xv, axis=-1, keepdims=True) + EPS)
        n = xv * r
        dn = dh * (g_ref[...] * (1.0 + sc_ref[...]))
        dx = dr_ref[...] + r * (dn - n * jnp.mean(dn * n, axis=-1, keepdims=True))
        dx_ref[...] = dx
        s2_ref[...] += jnp.sum(dh * n, axis=0, keepdims=True)
        s3_ref[...] += jnp.sum(dh, axis=0, keepdims=True)
        yv = y_ref[...]
        ry = lax.rsqrt(jnp.mean(yv * yv, axis=-1, keepdims=True) + EPS)
        yn = yv * ry
        dyn = dx * ((weight * gate_ref[...]) * gp_ref[...])
        dy = ry * (dyn - yn * jnp.mean(dyn * yn, axis=-1, keepdims=True))
        dy_ref[...] = dy.astype(BF16)
        s1_ref[...] += jnp.sum(dx * yn, axis=0, keepdims=True)
        cs_ref[...] += jnp.sum(dy, axis=0, keepdims=True)

    row = pl.BlockSpec((ts, d), lambda i: (i, 0))
    vec = jax.ShapeDtypeStruct((1, d), F32)
    return _call(body, name=name, grid=(s // ts,),
                 in_specs=[row, row, _vec_spec(d), _vec_spec(d), row, row, _vec_spec(d), _vec_spec(d)],
                 out_specs=(row, row) + (_vec_spec(d),) * 4,
                 out_shape=(jax.ShapeDtypeStruct((s, d), F32), jax.ShapeDtypeStruct((s, d), BF16), vec, vec, vec, vec),
                 args=[dh, x, g_pre, scale, dres, y, g_post, gate], sem=("arbitrary",))


def _group_norm_cat(oa, ob, ga, gb):
    s = oa.shape[0]
    ts = _row_tile(s)

    def body(oa_ref, ob_ref, ga_ref, gb_ref, y_ref):
        for o_ref, g_ref, lo, w in ((oa_ref, ga_ref, 0, QA), (ob_ref, gb_ref, QA, QB)):
            ov = o_ref[...]
            r = lax.rsqrt(jnp.mean(ov * ov, axis=-1, keepdims=True) + EPS)
            y_ref[:, lo:lo + w] = ((ov * r) * g_ref[...]).astype(BF16)

    return _call(body, name="group_norm_cat", grid=(s // ts,),
                 in_specs=[pl.BlockSpec((ts, QA), lambda i: (i, 0)), pl.BlockSpec((ts, QB), lambda i: (i, 0)),
                           _vec_spec(QA), _vec_spec(QB)],
                 out_specs=pl.BlockSpec((ts, QA + QB), lambda i: (i, 0)),
                 out_shape=jax.ShapeDtypeStruct((s, QA + QB), BF16), args=[oa, ob, ga, gb], sem=("parallel",))


def _group_norm_bwd(dy, oa, ob, ga, gb):
    s = oa.shape[0]
    ts = _row_tile(s)

    def body(dy_ref, oa_ref, ob_ref, ga_ref, gb_ref, doa_ref, dob_ref, dga_ref, dgb_ref):
        @pl.when(pl.program_id(0) == 0)
        def _():
            dga_ref[...] = jnp.zeros_like(dga_ref)
            dgb_ref[...] = jnp.zeros_like(dgb_ref)

        for o_ref, g_ref, do_ref, dg_ref, lo, w in ((oa_ref, ga_ref, doa_ref, dga_ref, 0, QA),
                                                    (ob_ref, gb_ref, dob_ref, dgb_ref, QA, QB)):
            ov = o_ref[...]
            dyv = dy_ref[:, lo:lo + w]
            r = lax.rsqrt(jnp.mean(ov * ov, axis=-1, keepdims=True) + EPS)
            n = ov * r
            dn = dyv * g_ref[...]
            do_ref[...] = r * (dn - n * jnp.mean(dn * n, axis=-1, keepdims=True))
            dg_ref[...] += jnp.sum(dyv * n, axis=0, keepdims=True)

    ra = pl.BlockSpec((ts, QA), lambda i: (i, 0))
    rb = pl.BlockSpec((ts, QB), lambda i: (i, 0))
    return _call(body, name="group_norm_bwd", grid=(s // ts,),
                 in_specs=[pl.BlockSpec((ts, QA + QB), lambda i: (i, 0)), ra, rb, _vec_spec(QA), _vec_spec(QB)],
                 out_specs=(ra, rb, _vec_spec(QA), _vec_spec(QB)),
                 out_shape=(jax.ShapeDtypeStruct((s, QA), F32), jax.ShapeDtypeStruct((s, QB), F32),
                            jax.ShapeDtypeStruct((1, QA), F32), jax.ShapeDtypeStruct((1, QB), F32)),
                 args=[dy, oa, ob, ga, gb], sem=("arbitrary",))


def _loss_and_grad(y, target):
    s, d = y.shape
    ts = _row_tile(s)

    def body(y_ref, t_ref, l_ref, g_ref):
        @pl.when(pl.program_id(0) == 0)
        def _():
            l_ref[...] = jnp.zeros_like(l_ref)

        err = y_ref[...] - t_ref[...]
        g_ref[...] = err * (1.0 / d)
        row = jnp.mean(err * err, axis=-1, keepdims=True)
        l_ref[...] += 0.5 * jnp.sum(row, axis=0, keepdims=True)

    row = pl.BlockSpec((ts, d), lambda i: (i, 0))
    return _call(body, name="loss_and_grad", grid=(s // ts,), in_specs=[row, row],
                 out_specs=(pl.BlockSpec((1, 1), lambda i: (0, 0)), row),
                 out_shape=(jax.ShapeDtypeStruct((1, 1), F32), jax.ShapeDtypeStruct((s, d), F32)),
                 args=[y, target], sem=("arbitrary",))


def _col_sum(x, name):
    s, n = x.shape
    ts = _row_tile(s)

    def body(x_ref, o_ref):
        @pl.when(pl.program_id(0) == 0)
        def _():
            o_ref[...] = jnp.zeros_like(o_ref)

        o_ref[...] += jnp.sum(x_ref[...].astype(F32), axis=0, keepdims=True)

    return _call(body, name=name, grid=(s // ts,), in_specs=[pl.BlockSpec((ts, n), lambda i: (i, 0))],
                 out_specs=pl.BlockSpec((1, n), lambda i: (0, 0)), out_shape=jax.ShapeDtypeStruct((1, n), F32),
                 args=[x], sem=("arbitrary",))


def _alibi_bias():
    i = np.arange(QROWS)[:, None]
    j = np.arange((QG + BACK_A) * CHUNK)[None, :]
    dist = np.abs(BACK_A * CHUNK + i - j).astype(np.float32)
    dc = j // CHUNK - i // CHUNK
    valid = (dc >= 0) & (dc <= BACK_A)
    slopes = np.array([2.0 ** (-8.0 * (h + 1) / H_A) for h in range(H_A)], dtype=np.float32)
    bias = -slopes[:, None, None] * dist[None]
    return jnp.asarray(np.where(valid[None], bias, np.float32(NEG_INF)).astype(np.float32))


def _rel_index_matrix():
    cc = np.arange(SKEW)
    dist = np.where(cc < SKEW - QROWS, BACK_B * CHUNK - cc, BACK_B * CHUNK + SKEW - cc)
    idx = np.clip(dist, -REL_CLIP, REL_CLIP) + REL_CLIP
    m = np.zeros((SKEW, N_REL), np.float32)
    m[cc, idx] = 1.0
    return jnp.asarray(m)


def _toeplitz_bias(vec, carry=None):
    lk = (QG + BACK_B) * CHUNK

    def body(v_ref, o_ref):
        xv = jnp.broadcast_to(v_ref[0], (QROWS, SKEW))
        row = lax.broadcasted_iota(jnp.int32, (QROWS, SKEW), 0)
        for bit in range(QROWS.bit_length() - 1):
            xv = jnp.where((row >> bit) & 1 == 1, pltpu.roll(xv, 1 << bit, 1), xv)
        ri = lax.broadcasted_iota(jnp.int32, (QROWS, lk), 0) // CHUNK
        ci = lax.broadcasted_iota(jnp.int32, (QROWS, lk), 1) // CHUNK
        valid = (ci - ri >= 0) & (ci - ri <= BACK_B)
        o_ref[0] = jnp.where(valid, xv[:, :lk], NEG_INF)

    return _call(body, name="toeplitz_bias", grid=(H_B,),
                 in_specs=[pl.BlockSpec((1, 1, SKEW), lambda h: (h, 0, 0))],
                 out_specs=pl.BlockSpec((1, QROWS, lk), lambda h: (h, 0, 0)),
                 out_shape=jax.ShapeDtypeStruct((H_B, QROWS, lk), F32), args=[vec], sem=("parallel",),
                 carry=carry)


def _diagonal_sums(dbias):
    lk = dbias.shape[2]

    def body(d_ref, o_ref):
        xp = jnp.concatenate([d_ref[0], jnp.zeros((QROWS, SKEW - lk), F32)], axis=1)
        xv = xp[0:CHUNK]
        for q in range(1, QG):
            xv = xv + pltpu.roll(xp[q * CHUNK:(q + 1) * CHUNK], SKEW - q * CHUNK, 1)
        row = lax.broadcasted_iota(jnp.int32, (CHUNK, SKEW), 0)
        for bit in range(CHUNK.bit_length() - 1):
            xv = jnp.where((row >> bit) & 1 == 1, pltpu.roll(xv, SKEW - (1 << bit), 1), xv)
        o_ref[0] = jnp.sum(xv, axis=0, keepdims=True)

    return _call(body, name="diagonal_sums", grid=(H_B,),
                 in_specs=[pl.BlockSpec((1, QROWS, lk), lambda h: (h, 0, 0))],
                 out_specs=pl.BlockSpec((1, 1, SKEW), lambda h: (h, 0, 0)),
                 out_shape=jax.ShapeDtypeStruct((H_B, 1, SKEW), F32), args=[dbias], sem=("parallel",))


def _attn_common(s, n_back, gqa, q_col, k_col, v_col):
    lk = (QG + n_back) * CHUNK
    pad = n_back * CHUNK
    q_spec = pl.BlockSpec((QROWS, LANES), lambda t, g: (g, q_col + t))
    if gqa:
        k_spec = pl.BlockSpec((s, LANES), lambda t, g: (0, k_col))
        v_spec = pl.BlockSpec((s, LANES), lambda t, g: (0, v_col))
    else:
        k_spec = pl.BlockSpec((s, LANES), lambda t, g: (0, k_col + t))
        v_spec = pl.BlockSpec((s, LANES), lambda t, g: (0, v_col + t))
    bias_spec = pl.BlockSpec((2, QROWS, lk), lambda t, g: (t, 0, 0))
    tile_spec = pl.BlockSpec((QROWS, LANES), lambda t, g: (g, t))
    return lk, pad, q_spec, k_spec, v_spec, bias_spec, tile_spec


def _attention_fwd(proj, bias, sinks, *, n_back, gqa, q_col, k_col, v_col, name, carry=None):
    s = proj.shape[0]
    lk, pad, q_spec, k_spec, v_spec, bias_spec, tile_spec = _attn_common(s, n_back, gqa, q_col, k_col, v_col)
    n_t, n_g = 512 // LANES, s // QROWS

    def body(*refs):
        if gqa:
            q_ref, k_ref, v_ref, bias_ref, sink_ref, o_ref, l_ref, kpad, vpad = refs
        else:
            q_ref, k_ref, v_ref, bias_ref, o_ref, l_ref, kpad, vpad = refs
        t, g = pl.program_id(0), pl.program_id(1)

        @pl.when(g == 0)
        def _():
            kpad[0:pad, :] = jnp.zeros((pad, LANES), BF16)
            vpad[0:pad, :] = jnp.zeros((pad, LANES), BF16)
            kpad[pad:, :] = k_ref[...]
            vpad[pad:, :] = v_ref[...]

        start = pl.multiple_of(g * QROWS, QROWS)
        kb = kpad[pl.ds(start, lk), :]
        vb = vpad[pl.ds(start, lk), :]
        half = lax.broadcasted_iota(jnp.int32, (QROWS, LANES), 1) // HEAD_DIM
        col_ok = lax.broadcasted_iota(jnp.int32, (QROWS, lk), 1) >= (n_back - QG * g) * CHUNK
        q = q_ref[...]
        if gqa:
            hk = t // 2
            q_rolled = pltpu.roll(q.astype(F32), HEAD_DIM, 1).astype(BF16)
        outs, lses = [], []
        for e in range(2):
            if gqa:
                kv_half = hk
                src = jnp.where(hk == e, q, q_rolled)
            else:
                kv_half = e
                src = q
            qm = jnp.where(half == kv_half, src, jnp.zeros_like(src))
            sc = _dot_nt(qm, kb) * (HEAD_DIM ** -0.5) + bias_ref[e]
            sc = jnp.where(col_ok, sc, NEG_INF)
            m = jnp.max(sc, axis=-1, keepdims=True)
            if gqa:
                sk = sink_ref[2 * t + e]
                m = jnp.maximum(m, sk)
            p = jnp.exp(sc - m)
            l = jnp.sum(p, axis=-1, keepdims=True)
            if gqa:
                l = l + jnp.exp(sk - m)
            pn = p / l
            outs.append(_dot(pn.astype(BF16), vb))
            lses.append(m + jnp.log(l))
        if gqa:
            same = jnp.where(hk == 0, outs[0], outs[1])
            other = jnp.where(hk == 0, outs[1], outs[0])
            o_ref[...] = jnp.where(half == hk, same, pltpu.roll(other, HEAD_DIM, 1))
        else:
            o_ref[...] = jnp.where(half == 0, outs[0], outs[1])
        l_ref[...] = jnp.where(half == 0, lses[0], lses[1])

    in_specs = [q_spec, k_spec, v_spec, bias_spec] + ([SMEM_SPEC] if gqa else [])
    args = [proj, proj, proj, bias] + ([sinks] if gqa else [])
    o_shape = jax.ShapeDtypeStruct((s, 512), F32)
    return _call(body, name=name, grid=(n_t, n_g), in_specs=in_specs, out_specs=(tile_spec, tile_spec),
                 out_shape=(o_shape, o_shape), args=args,
                 scratch=[pltpu.VMEM((s + pad, LANES), BF16), pltpu.VMEM((s + pad, LANES), BF16)],
                 sem=("arbitrary", "arbitrary"), carry=carry)


def _attention_bwd(proj, bias, sinks, do, lse, *, n_back, gqa, q_col, k_col, v_col, name, carry=None):
    s = proj.shape[0]
    lk, pad, q_spec, k_spec, v_spec, bias_spec, tile_spec = _attn_common(s, n_back, gqa, q_col, k_col, v_col)
    n_t, n_g = 512 // LANES, s // QROWS

    def body(*refs):
        if gqa:
            (q_ref, k_ref, v_ref, bias_ref, sink_ref, do_ref, l_ref,
             dq_ref, dk_ref, dv_ref, dsink_ref, kpad, vpad, dkpad, dvpad) = refs
        else:
            (q_ref, k_ref, v_ref, bias_ref, do_ref, l_ref,
             dq_ref, dk_ref, dv_ref, dbias_ref, kpad, vpad, dkpad, dvpad) = refs
        t, g = pl.program_id(0), pl.program_id(1)

        @pl.when(g == 0)
        def _():
            kpad[0:pad, :] = jnp.zeros((pad, LANES), BF16)
            vpad[0:pad, :] = jnp.zeros((pad, LANES), BF16)
            kpad[pad:, :] = k_ref[...]
            vpad[pad:, :] = v_ref[...]
            if gqa:
                dsink_ref[...] = jnp.zeros_like(dsink_ref)
            else:
                dbias_ref[...] = jnp.zeros_like(dbias_ref)

        @pl.when((g == 0) & (t == 0) if gqa else g == 0)
        def _():
            dkpad[...] = jnp.zeros_like(dkpad)
            dvpad[...] = jnp.zeros_like(dvpad)

        start = pl.multiple_of(g * QROWS, QROWS)
        kb = kpad[pl.ds(start, lk), :]
        vb = vpad[pl.ds(start, lk), :]
        half = lax.broadcasted_iota(jnp.int32, (QROWS, LANES), 1) // HEAD_DIM
        col_ok = lax.broadcasted_iota(jnp.int32, (QROWS, lk), 1) >= (n_back - QG * g) * CHUNK
        q = q_ref[...]
        dov = do_ref[...]
        lv = l_ref[...]
        if gqa:
            hk = t // 2
            q_rolled = pltpu.roll(q.astype(F32), HEAD_DIM, 1).astype(BF16)
            do_rolled = pltpu.roll(dov, HEAD_DIM, 1)
        dqs = []
        dk_acc = jnp.zeros((lk, LANES), F32)
        dv_acc = jnp.zeros((lk, LANES), F32)
        for e in range(2):
            if gqa:
                kv_half = hk
                src = jnp.where(hk == e, q, q_rolled)
                do_src = jnp.where(hk == e, dov, do_rolled)
            else:
                kv_half = e
                src = q
                do_src = dov
            qm = jnp.where(half == kv_half, src, jnp.zeros_like(src))
            dom = jnp.where(half == kv_half, do_src, 0.0).astype(BF16)
            lcol = jnp.max(jnp.where(half == e, lv, -jnp.inf), axis=-1, keepdims=True)
            sc = _dot_nt(qm, kb) * (HEAD_DIM ** -0.5) + bias_ref[e]
            sc = jnp.where(col_ok, sc, NEG_INF)
            pn = jnp.exp(sc - lcol)
            dp = _dot_nt(dom, vb)
            delta = jnp.sum(pn * dp, axis=-1, keepdims=True)
            ds = pn * (dp - delta)
            if gqa:
                p_sink = jnp.exp(sink_ref[2 * t + e] - lcol)
                dsk = -jnp.sum(p_sink * delta, axis=0, keepdims=True)
                dsink_ref[0, e:e + 1, :] += jnp.broadcast_to(dsk, (1, LANES))
            else:
                dbias_ref[e] += ds
            dsb = (ds * (HEAD_DIM ** -0.5)).astype(BF16)
            dqs.append(_dot(dsb, kb))
            dk_acc = dk_acc + _dot_tn(dsb, qm)
            dv_acc = dv_acc + _dot_tn(pn.astype(BF16), dom)
        dkpad[pl.ds(start, lk), :] += dk_acc
        dvpad[pl.ds(start, lk), :] += dv_acc
        if gqa:
            same = jnp.where(hk == 0, dqs[0], dqs[1])
            other = jnp.where(hk == 0, dqs[1], dqs[0])
            dq_ref[...] = jnp.where(half == hk, same, pltpu.roll(other, HEAD_DIM, 1)).astype(BF16)
        else:
            dq_ref[...] = jnp.where(half == 0, dqs[0], dqs[1]).astype(BF16)

        @pl.when((g == n_g - 1) & (t == n_t - 1) if gqa else g == n_g - 1)
        def _():
            dk_ref[...] = dkpad[pad:, :].astype(BF16)
            dv_ref[...] = dvpad[pad:, :].astype(BF16)

    in_specs = [q_spec, k_spec, v_spec, bias_spec] + ([SMEM_SPEC] if gqa else []) + [tile_spec, tile_spec]
    args = [proj, proj, proj, bias] + ([sinks] if gqa else []) + [do, lse]
    if gqa:
        kv_out = pl.BlockSpec((s, LANES), lambda t, g: (0, 0))
        kv_shape = jax.ShapeDtypeStruct((s, LANES), BF16)
        extra_spec = pl.BlockSpec((1, 8, LANES), lambda t, g: (t, 0, 0))
        extra_shape = jax.ShapeDtypeStruct((n_t, 8, LANES), F32)
    else:
        kv_out = pl.BlockSpec((s, LANES), lambda t, g: (0, t))
        kv_shape = jax.ShapeDtypeStruct((s, 512), BF16)
        extra_spec = bias_spec
        extra_shape = jax.ShapeDtypeStruct(bias.shape, F32)
    return _call(body, name=name, grid=(n_t, n_g), in_specs=in_specs,
                 out_specs=(tile_spec, kv_out, kv_out, extra_spec),
                 out_shape=(jax.ShapeDtypeStruct((s, 512), BF16), kv_shape, kv_shape, extra_shape), args=args,
                 scratch=[pltpu.VMEM((s + pad, LANES), BF16), pltpu.VMEM((s + pad, LANES), BF16),
                          pltpu.VMEM((s + pad, LANES), F32), pltpu.VMEM((s + pad, LANES), F32)],
                 sem=("arbitrary", "arbitrary"), carry=carry)


def _sum_slots(r, name):
    n_slots, rows, k = r.shape

    def body(r_ref, o_ref):
        acc = r_ref[0].astype(F32)
        for j in range(1, n_slots):
            acc = acc + r_ref[j].astype(F32)
        o_ref[...] = acc

    return _call(body, name=name, grid=(k // LANES,),
                 in_specs=[pl.BlockSpec((n_slots, rows, LANES), lambda i: (0, 0, i))],
                 out_specs=pl.BlockSpec((rows, LANES), lambda i: (0, i)),
                 out_shape=jax.ShapeDtypeStruct((rows, k), F32), args=[r], sem=("parallel",))


def _sum_rows8(g):
    n = g.shape[2]

    def body(g_ref, o_ref):
        acc = g_ref[0]
        for j in range(1, N_DEV):
            acc = acc + g_ref[j]
        o_ref[...] = acc

    return pl.pallas_call(
        body, name="sum_small_grads", in_specs=[VMEM_SPEC], out_specs=VMEM_SPEC,
        out_shape=jax.ShapeDtypeStruct((1, n), F32), compiler_params=_params(),
    )(g)


def _ada_weight_grad(sc_t, dmod_cols):
    d = sc_t.shape[0]
    w = dmod_cols.shape[1]
    td = _pick(d, (256, 128))

    def body(sc_ref, dm_ref, o_ref):
        scv = sc_ref[...]
        dmv = dm_ref[...]
        acc = scv[:, 0:1] * dmv[0:1, :]
        for b in range(1, N_DEV):
            acc = acc + scv[:, b:b + 1] * dmv[b:b + 1, :]
        o_ref[...] = acc

    return _call(body, name="ada_weight_grad", grid=(d // td,),
                 in_specs=[pl.BlockSpec((td, N_DEV), lambda i: (i, 0)), pl.BlockSpec((N_DEV, w), lambda i: (0, 0))],
                 out_specs=pl.BlockSpec((td, w), lambda i: (i, 0)), out_shape=jax.ShapeDtypeStruct((d, w), F32),
                 args=[sc_t, dmod_cols], sem=("parallel",))


def _adamw_update(w, gv, m, v):
    nm = ADAM_B1 * m + (1.0 - ADAM_B1) * gv
    nv = ADAM_B2 * v + (1.0 - ADAM_B2) * (gv * gv)
    m_hat = nm / (1.0 - ADAM_B1 ** ADAM_STEP)
    v_hat = nv / (1.0 - ADAM_B2 ** ADAM_STEP)
    return -ADAM_LR * (m_hat / (jnp.sqrt(v_hat) + ADAM_EPS) + ADAM_WD * w), nm, nv


def _adamw(w, g, m, v, name):
    rows, cols = w.shape
    tr = _pick(rows, (256, 176, 128, 88, 64)) if rows > 256 else rows

    def body(w_ref, g_ref, m_ref, v_ref, d_ref, nm_ref, nv_ref):
        d_ref[...], nm_ref[...], nv_ref[...] = _adamw_update(w_ref[...], g_ref[...], m_ref[...], v_ref[...])

    spec = pl.BlockSpec((tr, cols), lambda i: (i, 0))
    shape = jax.ShapeDtypeStruct((rows, cols), F32)
    return _call(body, name=name, grid=(rows // tr,), in_specs=[spec] * 4, out_specs=(spec, spec, spec),
                 out_shape=(shape, shape, shape), args=[w, g, m, v], sem=("parallel",))


def _adamw_from_slots(w, slots, m, v, name):
    n_slots, rows, k = slots.shape

    def body(s_ref, w_ref, m_ref, v_ref, g_ref, d_ref, nm_ref, nv_ref):
        gv = s_ref[0].astype(F32)
        for j in range(1, n_slots):
            gv = gv + s_ref[j].astype(F32)
        g_ref[...] = gv
        d_ref[...], nm_ref[...], nv_ref[...] = _adamw_update(w_ref[...], gv, m_ref[...], v_ref[...])

    spec = pl.BlockSpec((rows, LANES), lambda i: (0, i))
    shape = jax.ShapeDtypeStruct((rows, k), F32)
    return _call(body, name=name, grid=(k // LANES,),
                 in_specs=[pl.BlockSpec((n_slots, rows, LANES), lambda i: (0, 0, i)), spec, spec, spec],
                 out_specs=(spec, spec, spec, spec), out_shape=(shape, shape, shape, shape),
                 args=[slots, w, m, v], sem=("parallel",))


SMALL = ("b_ada", "g_pre_ffn1", "g_post_ffn1", "g_pre_mix", "b_in", "sinks_a", "rel_bias_b", "g_grp_a",
         "g_grp_b", "b_out", "g_post_mix", "g_pre_ffn2", "g_post_ffn2")
WEIGHTS = ("w_ada", "b_ada", "g_pre_ffn1", "w_gate1", "w_up1", "w_down1", "g_post_ffn1", "g_pre_mix", "w_in",
           "b_in", "sinks_a", "rel_bias_b", "g_grp_a", "g_grp_b", "w_out", "b_out", "g_post_mix", "g_pre_ffn2",
           "w_gate2", "w_up2", "w_down2", "g_post_ffn2")


def kernel(x, c, w_ada, b_ada, g_pre_ffn1, w_gate1, w_up1, w_down1, g_post_ffn1, g_pre_mix, w_in, b_in, sinks_a, rel_bias_b, g_grp_a, g_grp_b, w_out, b_out, g_post_mix, g_pre_ffn2, w_gate2, w_up2, w_down2, g_post_ffn2, loss_target, m_w_ada, m_b_ada, m_g_pre_ffn1, m_w_gate1, m_w_up1, m_w_down1, m_g_post_ffn1, m_g_pre_mix, m_w_in, m_b_in, m_sinks_a, m_rel_bias_b, m_g_grp_a, m_g_grp_b, m_w_out, m_b_out, m_g_post_mix, m_g_pre_ffn2, m_w_gate2, m_w_up2, m_w_down2, m_g_post_ffn2, v_w_ada, v_b_ada, v_g_pre_ffn1, v_w_gate1, v_w_up1, v_w_down1, v_g_post_ffn1, v_g_pre_mix, v_w_in, v_b_in, v_sinks_a, v_rel_bias_b, v_g_grp_a, v_g_grp_b, v_w_out, v_b_out, v_g_post_mix, v_g_pre_ffn2, v_w_gate2, v_w_up2, v_w_down2, v_g_post_ffn2):
    given = dict(locals())
    weights = {n: given[n] for n in WEIGHTS}
    mom_m = {n: given["m_" + n] for n in WEIGHTS}
    mom_v = {n: given["v_" + n] for n in WEIGHTS}

    me = 4 * lax.axis_index("x") + 2 * lax.axis_index("y") + lax.axis_index("c")
    xs = x[0]
    tgt = loss_target[0]
    d_model = xs.shape[1]
    ada_cols = w_ada.shape[2]

    sh = {"wg1": w_gate1[0].T, "wu1": w_up1[0].T, "wd1": w_down1[0], "win": w_in[0].T, "wo": w_out[0],
          "wg2": w_gate2[0].T, "wu2": w_up2[0].T, "wd2": w_down2[0]}
    sh = {k: v.astype(BF16) for k, v in sh.items()}

    def gather(*names):
        return _gather_carry([sh[n] for n in names])

    bias_a = _alibi_bias()
    rel_m = _rel_index_matrix()
    rel_vec = jnp.dot(rel_bias_b[0], rel_m.T, precision=lax.Precision.HIGHEST)
    bias_b, (wg1,) = _toeplitz_bias(rel_vec.reshape(H_B, 1, SKEW), carry=gather("wg1"))

    b_cols = lax.dynamic_slice(b_ada, (0, me * ada_cols), (1, ada_cols))
    (sc_all, mod_rows), (wu1,) = _ada_forward(c, w_ada[0], b_cols, gather("wu1"))
    mod = mod_rows.reshape(N_MOD, d_model)
    shift1, scale1, gate1, shift2, scale2, gate2, shift3, scale3, gate3 = (mod[i:i + 1] for i in range(N_MOD))

    h1 = _pre_norm(xs, g_pre_ffn1, scale1, shift1, "pre_norm_ffn1")
    (a1, b1, u1), (wd1,) = _ffn_up(h1, wg1, wu1, "ffn_up_ffn1", carry=gather("wd1"))
    y1, (win,) = _mm_nn([(u1, wd1)], "ffn_down_ffn1", F32, carry=gather("win"))
    x1, h2 = _post_pre_norm(xs, y1, g_post_ffn1, gate1, 0.5, g_pre_mix, scale2, shift2, "post_ffn1_pre_mix")

    proj, (wo,) = _mm_nt(h2, win, "in_proj", BF16, bias=b_in, carry=gather("wo"))
    sinks = sinks_a[0]
    cfg_a = dict(n_back=BACK_A, gqa=True, q_col=0, k_col=QA // LANES, v_col=(QA + KVA) // LANES)
    cfg_b = dict(n_back=BACK_B, gqa=False, q_col=(QA + 2 * KVA) // LANES, k_col=(QA + 2 * KVA + QB) // LANES,
                 v_col=(QA + 2 * KVA + 2 * QB) // LANES)
    (oa, lse_a), (wg2,) = _attention_fwd(proj, bias_a, sinks, name="attn_a", carry=gather("wg2"), **cfg_a)
    (ob, lse_b), (wu2,) = _attention_fwd(proj, bias_b, None, name="attn_b", carry=gather("wu2"), **cfg_b)
    ycat = _group_norm_cat(oa, ob, g_grp_a, g_grp_b)
    ymix = _mm_nn([(ycat, wo)], "out_proj", F32, bias=b_out)
    x2, h3 = _post_pre_norm(x1, ymix, g_post_mix, gate2, 1.0, g_pre_ffn2, scale3, shift3, "post_mix_pre_ffn2")

    (a3, b3, u3), (wd2,) = _ffn_up(h3, wg2, wu2, "ffn_up_ffn2", carry=gather("wd2"))
    y3 = _mm_nn([(u3, wd2)], "ffn_down_ffn2", F32)

    def scatter(*grads):
        return _scatter_carry(list(grads))

    slots = {}

    dx3, dy, loss_part, s1 = _post_norm_loss_bwd(x2, y3, g_post_ffn2, gate3, 0.5, tgt, "post_ffn2_loss_bwd")
    da, db = _ffn_down_bwd(dy, wd2, a3, b3, "ffn_down_bwd_ffn2")
    dwd2 = _mm_tn_pair(u3, dy, "grad_wd_ffn2")
    dwg2 = _mm_tn_pair(da, h3, "grad_wg_ffn2")
    dwu2 = _mm_tn_pair(db, h3, "grad_wu_ffn2")
    dh, (slots["wd2"],) = _mm_nn([(da, wg2), (db, wu2)], "ffn_up_bwd_ffn2", F32, carry=scatter(dwd2))
    dx2, dymix, s2, s3, s1m, db_out = _pre_post_norm_bwd(dh, x2, g_pre_ffn2, scale3, dx3, ymix, g_post_mix, gate2,
                                                         1.0, "pre_ffn2_post_mix_bwd")
    sm3 = dict(shift=s3, scale=s2 * g_pre_ffn2, gate=0.5 * g_post_ffn2 * s1,
               g_pre=(1.0 + scale3) * s2, g_post=(0.5 * gate3) * s1)

    dycat = _mm_nt(dymix, wo, "out_proj_bwd", F32)
    dwo = _mm_tn_pair(ycat, dymix, "grad_wo")
    doa, dob, dg_a, dg_b = _group_norm_bwd(dycat, oa, ob, g_grp_a, g_grp_b)
    (dqa, dka, dva, dsink), (slots["wg2"],) = _attention_bwd(
        proj, bias_a, sinks, doa, lse_a, name="attn_a_bwd", carry=scatter(dwg2), **cfg_a)
    (dqb, dkb, dvb, dbias), (slots["wu2"], slots["wo"]) = _attention_bwd(
        proj, bias_b, None, dob, lse_b, name="attn_b_bwd", carry=scatter(dwu2, dwo), **cfg_b)
    dproj = jnp.concatenate([dqa, dka, dva, dqb, dkb, dvb], axis=1)
    db_in = _col_sum(dproj, "grad_b_in")
    dwin = _mm_tn_pair(dproj, h2, "grad_win")
    dh2 = _mm_nn([(dproj, win)], "in_proj_bwd", F32)
    dx1, dy, s2m, s3m, s1, _ = _pre_post_norm_bwd(dh2, x1, g_pre_mix, scale2, dx2, y1, g_post_ffn1, gate1, 0.5,
                                                  "pre_mix_post_ffn1_bwd")
    d_rel = jnp.dot(_diagonal_sums(dbias).reshape(H_B, SKEW), rel_m, precision=lax.Precision.HIGHEST)
    d_sinks = dsink[:, :2, 0].reshape(1, H_A)

    (da, db), (slots["win"],) = _ffn_down_bwd(dy, wd1, a1, b1, "ffn_down_bwd_ffn1", carry=scatter(dwin))
    dwd1 = _mm_tn_pair(u1, dy, "grad_wd_ffn1")
    dwg1, (slots["wd1"],) = _mm_tn_pair(da, h1, "grad_wg_ffn1", carry=scatter(dwd1))
    dwu1, (slots["wg1"],) = _mm_tn_pair(db, h1, "grad_wu_ffn1", carry=scatter(dwg1))
    dh, (slots["wu1"],) = _mm_nn([(da, wg1), (db, wu1)], "ffn_up_bwd_ffn1", F32, carry=scatter(dwu1))
    dx0, s2, s3 = _pre_norm_bwd(dh, xs, g_pre_ffn1, scale1, dx1, "pre_norm_bwd_ffn1")
    sm1 = dict(shift=s3, scale=s2 * g_pre_ffn1, gate=0.5 * g_post_ffn1 * s1,
               g_pre=(1.0 + scale1) * s2, g_post=(0.5 * gate1) * s1)

    dmod = jnp.concatenate([sm1["shift"], sm1["scale"], sm1["gate"],
                            s3m, s2m * g_pre_mix, g_post_mix * s1m,
                            sm3["shift"], sm3["scale"], sm3["gate"]], axis=1)
    small_parts = {
        "b_ada": dmod, "g_pre_ffn1": sm1["g_pre"], "g_post_ffn1": sm1["g_post"],
        "g_pre_mix": (1.0 + scale2) * s2m, "b_in": db_in, "sinks_a": d_sinks,
        "rel_bias_b": d_rel.reshape(1, H_B * N_REL), "g_grp_a": dg_a, "g_grp_b": dg_b, "b_out": db_out,
        "g_post_mix": gate2 * s1m, "g_pre_ffn2": sm3["g_pre"], "g_post_ffn2": sm3["g_post"]}
    sizes = [small_parts[n].shape[1] for n in SMALL]
    n_small = sum(sizes)
    n_pad = -(n_small + 1) % LANES
    packed = jnp.concatenate([small_parts[n] for n in SMALL] + [loss_part, jnp.zeros((1, n_pad), F32)], axis=1)
    gathered = _all_gather_small(packed)
    small_sum = _sum_rows8(gathered)
    loss = small_sum[0, n_small]
    dmod_cols = lax.dynamic_slice(gathered.reshape(N_DEV, n_small + 1 + n_pad), (0, me * ada_cols),
                                  (N_DEV, ada_cols))
    g_ada = _ada_weight_grad(sc_all.reshape(N_DEV, d_model).T, dmod_cols)

    out_g, out_d, out_m, out_v = {}, {}, {}, {}
    d_, m_, v_ = _adamw(w_ada[0], g_ada, m_w_ada[0], v_w_ada[0], "adamw_w_ada")
    out_g["w_ada"], out_d["w_ada"], out_m["w_ada"], out_v["w_ada"] = g_ada[None], d_[None], m_[None], v_[None]
    for n, key, transposed in (("w_gate1", "wg1", True), ("w_up1", "wu1", True), ("w_down1", "wd1", False),
                               ("w_in", "win", True), ("w_out", "wo", False), ("w_gate2", "wg2", True),
                               ("w_up2", "wu2", True), ("w_down2", "wd2", False)):
        view = (lambda t: t.T) if transposed else (lambda t: t)
        res = _adamw_from_slots(view(weights[n][0]), slots[key], view(mom_m[n][0]), view(mom_v[n][0]),
                                "adamw_" + n)
        out_g[n], out_d[n], out_m[n], out_v[n] = (view(t)[None] for t in res)

    def pack(tree):
        return jnp.concatenate([tree[n].reshape(1, -1) for n in SMALL], axis=1)

    g_small = small_sum[:, :n_small]
    d_s, m_s, v_s = _adamw(pack(weights), g_small, pack(mom_m), pack(mom_v), "adamw_small")
    off = 0
    for n, size in zip(SMALL, sizes):
        shape = weights[n].shape
        out_g[n] = g_small[:, off:off + size].reshape(shape)
        out_d[n] = d_s[:, off:off + size].reshape(shape)
        out_m[n] = m_s[:, off:off + size].reshape(shape)
        out_v[n] = v_s[:, off:off + size].reshape(shape)
        off += size

    return (loss, dx0[None], *[out_g[n] for n in WEIGHTS], *[out_d[n] for n in WEIGHTS],
            *[out_m[n] for n in WEIGHTS], *[out_v[n] for n in WEIGHTS])
```

```python
import numpy as np
import jax
import jax.numpy as jnp
from jax import lax
from jax.experimental import pallas as pl
from jax.experimental.pallas import tpu as pltpu

F32 = jnp.float32
BF16 = jnp.bfloat16
MESH = pl.DeviceIdType.MESH
ANY = pl.BlockSpec(memory_space=pl.ANY)
VMEM_SPEC = pl.BlockSpec(memory_space=pltpu.VMEM)
SMEM_SPEC = pl.BlockSpec(memory_space=pltpu.SMEM)

N_DEV = 8
CHUNK = 64
HEAD_DIM = 64
LANES = 128
H_A, KV_A, H_B = 8, 2, 8
BACK_A, BACK_B = 2, 8
REL_CLIP = 128
N_REL = 2 * REL_CLIP + 1
QA, KVA, QB = H_A * HEAD_DIM, KV_A * HEAD_DIM, H_B * HEAD_DIM
D_IN = QA + 2 * KVA + 3 * QB
N_MOD = 9
EPS = 1e-6
NEG_INF = -1e30
QG = 4
QROWS = QG * CHUNK
SKEW = 1024
ADAM_LR, ADAM_B1, ADAM_B2, ADAM_EPS, ADAM_WD, ADAM_STEP = 0.001, 0.9, 0.999, 1e-08, 0.01, 10
VMEM_LIMIT = 56 * 2 ** 20


def _pick(n, cands):
    for c in cands:
        if n % c == 0:
            return c
    return n


def _params(sem=None):
    return pltpu.CompilerParams(dimension_semantics=sem, vmem_limit_bytes=VMEM_LIMIT)


def _dot_nt(a, b):
    return lax.dot_general(a, b, (((1,), (1,)), ((), ())), preferred_element_type=F32)


def _dot_tn(a, b):
    return lax.dot_general(a, b, (((0,), (0,)), ((), ())), preferred_element_type=F32)


def _dot(a, b):
    return jnp.dot(a, b, preferred_element_type=F32)


def _sigmoid(a):
    return 0.5 * (jnp.tanh(0.5 * a) + 1.0)


def _mesh_pos():
    return lax.axis_index("x"), lax.axis_index("y"), lax.axis_index("c")


def _peer(x, y, c, r):
    px = 1 - x if r & 4 else x
    py = 1 - y if r & 2 else y
    pc = 1 - c if r & 1 else c
    return px, py, pc


class _Carry:
    def __init__(self, ins, out_shapes, scratch, start, finish, middle=None):
        self.ins, self.out_shapes, self.scratch = list(ins), list(out_shapes), list(scratch)
        self.start, self.finish, self.middle = start, finish, middle


def _call(body, *, name, grid, in_specs, out_specs, out_shape, args, scratch=(), sem=None, carry=None):
    single = not isinstance(out_shape, (tuple, list))
    out_specs = (out_specs,) if single else tuple(out_specs)
    out_shape = (out_shape,) if single else tuple(out_shape)
    if carry is None:
        res = pl.pallas_call(body, name=name, grid=grid, in_specs=list(in_specs), out_specs=out_specs,
                             out_shape=out_shape, scratch_shapes=list(scratch), compiler_params=_params(sem))(*args)
        return res[0] if single else res
    n_in, n_out, n_s = len(in_specs), len(out_shape), len(scratch)
    ci, co = len(carry.ins), len(carry.out_shapes)

    def wrapped(*refs):
        ins, cins = refs[:n_in], refs[n_in:n_in + ci]
        outs = refs[n_in + ci:n_in + ci + n_out]
        couts = refs[n_in + ci + n_out:n_in + ci + n_out + co]
        scr = refs[n_in + ci + n_out + co:n_in + ci + n_out + co + n_s]
        cscr = refs[n_in + ci + n_out + co + n_s:]
        first, last, mid = None, None, None
        for ax, n in enumerate(grid):
            f, l = pl.program_id(ax) == 0, pl.program_id(ax) == n - 1
            h = pl.program_id(ax) == (n // 2 if ax == 0 else 0)
            first = f if first is None else first & f
            last = l if last is None else last & l
            mid = h if mid is None else mid & h
        pl.when(first)(lambda: carry.start(cins, couts, cscr))
        if carry.middle is not None and grid[0] > 1:
            pl.when(mid)(lambda: carry.middle(cins, couts, cscr))
        body(*ins, *outs, *scr)
        if carry.middle is not None and grid[0] == 1:
            pl.when(last)(lambda: carry.middle(cins, couts, cscr))
        pl.when(last)(lambda: carry.finish(cins, couts, cscr))

    res = pl.pallas_call(
        wrapped, name=name, grid=grid, in_specs=list(in_specs) + [ANY] * ci, out_specs=out_specs + (ANY,) * co,
        out_shape=out_shape + tuple(carry.out_shapes), scratch_shapes=list(scratch) + carry.scratch,
        compiler_params=_params(("arbitrary",) * len(grid)))(*args, *carry.ins)
    main = res[:n_out]
    return (main[0] if single else main), res[n_out:]


def _gather_carry(shards):
    n_w = len(shards)
    rows = [s.shape[0] for s in shards]

    def plan(ins, outs, scr):
        send_sems, recv_sems, local_sems = scr
        x, y, c = _mesh_pos()
        me, sibling = (x, y, c), (x, y, 1 - c)
        chips = [(1 - x, y), (x, 1 - y), (1 - x, 1 - y)]

        def block(w, dev):
            start = pl.multiple_of((4 * dev[0] + 2 * dev[1] + dev[2]) * rows[w], 16)
            return outs[w].at[pl.ds(start, rows[w]), :]

        def copy(w, k, dev, to, src=None):
            return pltpu.make_async_remote_copy(
                src_ref=block(w, dev) if src is None else src, dst_ref=block(w, dev),
                send_sem=send_sems.at[w, k], recv_sem=recv_sems.at[w, k], device_id=to, device_id_type=MESH)

        mine = [pltpu.make_async_copy(ins[w], block(w, me), local_sems.at[w]) for w in range(n_w)]
        first = []
        for j, chip in enumerate(chips):
            first += [copy(w, 1 + j, me, (*chip, c), src=ins[w]) for w in range(n_w)]
        first += [copy(w, 0, me, sibling, src=ins[w]) for w in range(n_w)]
        return c, me, sibling, chips, copy, mine, first

    def start(ins, outs, scr):
        _, _, _, _, _, mine, first = plan(ins, outs, scr)
        for cp in mine + first:
            cp.start()

    def middle(ins, outs, scr):
        c, me, sibling, chips, copy, _, _ = plan(ins, outs, scr)
        for j, chip in enumerate(chips):
            for w in range(n_w):
                copy(w, 1 + j, (*chip, c), me).wait_recv()
                copy(w, 4 + j, (*chip, c), sibling).start()

    def finish(ins, outs, scr):
        c, me, sibling, chips, copy, mine, first = plan(ins, outs, scr)
        passed = [copy(w, 4 + j, (*chip, c), sibling) for j, chip in enumerate(chips) for w in range(n_w)]
        for w in range(n_w):
            copy(w, 0, sibling, me).wait_recv()
        for j, chip in enumerate(chips):
            for w in range(n_w):
                copy(w, 4 + j, (*chip, 1 - c), me).wait_recv()
        for cp in first + passed:
            cp.wait_send()
        for cp in mine:
            cp.wait()

    return _Carry(
        shards, [jax.ShapeDtypeStruct((N_DEV * s.shape[0], s.shape[1]), s.dtype) for s in shards],
        [pltpu.SemaphoreType.DMA((n_w, N_DEV - 1)), pltpu.SemaphoreType.DMA((n_w, N_DEV - 1)),
         pltpu.SemaphoreType.DMA((n_w,))], start, finish, middle)


def _scatter_carry(parts):
    n_w = len(parts)
    n_chip = N_DEV // 2
    rows = [g.shape[0] // n_chip for g in parts]

    def plan(ins, outs, scr):
        send_sems, recv_sems, local_sems = scr
        x, y, c = _mesh_pos()

        def src(w, chip_index):
            return ins[w].at[pl.ds(pl.multiple_of(chip_index * rows[w], 16), rows[w]), :]

        mine = [pltpu.make_async_copy(src(w, 2 * x + y), outs[w].at[0], local_sems.at[w]) for w in range(n_w)]
        copies = []
        for r in (3, 2, 1):
            px, py, _ = _peer(x, y, c, 2 * r)
            for w in range(n_w):
                copies.append(pltpu.make_async_remote_copy(
                    src_ref=src(w, 2 * px + py), dst_ref=outs[w].at[r], send_sem=send_sems.at[w, r - 1],
                    recv_sem=recv_sems.at[w, r - 1], device_id=(px, py, c), device_id_type=MESH))
        return mine, copies

    def start(ins, outs, scr):
        mine, copies = plan(ins, outs, scr)
        for cp in mine + copies:
            cp.start()

    def finish(ins, outs, scr):
        mine, copies = plan(ins, outs, scr)
        for cp in copies:
            cp.wait_recv()
        for cp in copies:
            cp.wait_send()
        for cp in mine:
            cp.wait()

    return _Carry(
        parts, [jax.ShapeDtypeStruct((n_chip, r, g.shape[1]), g.dtype) for r, g in zip(rows, parts)],
        [pltpu.SemaphoreType.DMA((n_w, n_chip - 1)), pltpu.SemaphoreType.DMA((n_w, n_chip - 1)),
         pltpu.SemaphoreType.DMA((n_w,))], start, finish)


def _ada_forward(c_row, w_ada, b_cols, carry):
    d = c_row.shape[1]
    wcols = w_ada.shape[1]
    ci, co = len(carry.ins), len(carry.out_shapes)

    def body(*refs):
        c_ref, w_ref, b_ref = refs[:3]
        cins = refs[3:3 + ci]
        sc_ref, mod_ref = refs[3 + ci:5 + ci]
        couts = refs[5 + ci:5 + ci + co]
        rows_ref, send_sems, recv_sems = refs[5 + ci + co:8 + ci + co]
        cscr = refs[8 + ci + co:]
        carry.start(cins, couts, cscr)
        x, y, c = _mesh_pos()
        me = 4 * x + 2 * y + c
        cv = c_ref[...]
        sc_ref[me] = cv * _sigmoid(cv)

        sends = []
        for r in range(1, N_DEV):
            px, py, pc = _peer(x, y, c, r)
            cp = pltpu.make_async_remote_copy(
                src_ref=sc_ref.at[me], dst_ref=sc_ref.at[me], send_sem=send_sems.at[0, r - 1],
                recv_sem=recv_sems.at[0, r - 1], device_id=(px, py, pc), device_id_type=MESH)
            cp.start()
            sends.append(cp)
        for r in range(1, N_DEV):
            px, py, pc = _peer(x, y, c, r)
            pid = 4 * px + 2 * py + pc
            pltpu.make_async_remote_copy(
                src_ref=sc_ref.at[pid], dst_ref=sc_ref.at[pid], send_sem=send_sems.at[0, r - 1],
                recv_sem=recv_sems.at[0, r - 1], device_id=(px, py, pc), device_id_type=MESH).wait_recv()
        for cp in sends:
            cp.wait_send()

        sc_all = jnp.concatenate([sc_ref[j] for j in range(N_DEV)], axis=0)
        rows = _dot(sc_all.astype(BF16), w_ref[...].astype(BF16)) + b_ref[...]
        for j in range(N_DEV):
            rows_ref[j] = rows[j:j + 1, :]
        mod_ref[me] = rows_ref[me]

        sends = []
        for r in range(1, N_DEV):
            px, py, pc = _peer(x, y, c, r)
            pid = 4 * px + 2 * py + pc
            cp = pltpu.make_async_remote_copy(
                src_ref=rows_ref.at[pid], dst_ref=mod_ref.at[me], send_sem=send_sems.at[1, r - 1],
                recv_sem=recv_sems.at[1, r - 1], device_id=(px, py, pc), device_id_type=MESH)
            cp.start()
            sends.append(cp)
        for r in range(1, N_DEV):
            px, py, pc = _peer(x, y, c, r)
            pid = 4 * px + 2 * py + pc
            pltpu.make_async_remote_copy(
                src_ref=rows_ref.at[pid], dst_ref=mod_ref.at[pid], send_sem=send_sems.at[1, r - 1],
                recv_sem=recv_sems.at[1, r - 1], device_id=(px, py, pc), device_id_type=MESH).wait_recv()
        for cp in sends:
            cp.wait_send()
        if carry.middle is not None:
            carry.middle(cins, couts, cscr)
        carry.finish(cins, couts, cscr)

    res = pl.pallas_call(
        body, name="ada_forward",
        out_shape=(jax.ShapeDtypeStruct((N_DEV, 1, d), F32), jax.ShapeDtypeStruct((N_DEV, 1, wcols), F32),
                   *carry.out_shapes),
        in_specs=[VMEM_SPEC, VMEM_SPEC, VMEM_SPEC] + [ANY] * ci, out_specs=(VMEM_SPEC, VMEM_SPEC) + (ANY,) * co,
        scratch_shapes=[pltpu.VMEM((N_DEV, 1, wcols), F32), pltpu.SemaphoreType.DMA((2, N_DEV - 1)),
                        pltpu.SemaphoreType.DMA((2, N_DEV - 1))] + carry.scratch,
        compiler_params=_params(),
    )(c_row, w_ada, b_cols, *carry.ins)
    return res[:2], res[2:]


def _all_gather_small(v):
    n = v.shape[1]

    def body(v_ref, out_ref, send_sems, recv_sems):
        x, y, c = _mesh_pos()
        me = 4 * x + 2 * y + c
        out_ref[me] = v_ref[...]
        sends = []
        for r in range(1, N_DEV):
            px, py, pc = _peer(x, y, c, r)
            cp = pltpu.make_async_remote_copy(
                src_ref=v_ref, dst_ref=out_ref.at[me], send_sem=send_sems.at[r - 1],
                recv_sem=recv_sems.at[r - 1], device_id=(px, py, pc), device_id_type=MESH)
            cp.start()
            sends.append(cp)
        for r in range(1, N_DEV):
            px, py, pc = _peer(x, y, c, r)
            pid = 4 * px + 2 * py + pc
            pltpu.make_async_remote_copy(
                src_ref=v_ref, dst_ref=out_ref.at[pid], send_sem=send_sems.at[r - 1],
                recv_sem=recv_sems.at[r - 1], device_id=(px, py, pc), device_id_type=MESH).wait_recv()
        for cp in sends:
            cp.wait_send()

    return pl.pallas_call(
        body, name="all_gather_small",
        out_shape=jax.ShapeDtypeStruct((N_DEV, 1, n), F32),
        in_specs=[VMEM_SPEC], out_specs=VMEM_SPEC,
        scratch_shapes=[pltpu.SemaphoreType.DMA((N_DEV - 1,)), pltpu.SemaphoreType.DMA((N_DEV - 1,))],
        compiler_params=_params(),
    )(v)


def _mm_nt(a, b, name, out_dtype, bias=None, carry=None):
    m, k = a.shape
    n = b.shape[0]
    tm = _pick(m, (512, 256, 128))
    tn = _pick(n, (1408, 1152, 1024, 768, 512, 256, 128))

    def body(*refs):
        acc = _dot_nt(refs[0][...], refs[1][...])
        if bias is not None:
            acc = acc + refs[2][...]
        refs[-1][...] = acc.astype(out_dtype)

    in_specs = [pl.BlockSpec((tm, k), lambda i, j: (i, 0)), pl.BlockSpec((tn, k), lambda i, j: (j, 0))]
    args = [a, b]
    if bias is not None:
        in_specs.append(pl.BlockSpec((1, tn), lambda i, j: (0, j)))
        args.append(bias)
    return _call(body, name=name, grid=(m // tm, n // tn), in_specs=in_specs,
                 out_specs=pl.BlockSpec((tm, tn), lambda i, j: (i, j)),
                 out_shape=jax.ShapeDtypeStruct((m, n), out_dtype), args=args,
                 sem=("parallel", "parallel"), carry=carry)


def _mm_nn(pairs, name, out_dtype, bias=None, carry=None):
    m, k = pairs[0][0].shape
    n = pairs[0][1].shape[1]
    tm = _pick(m, (512, 256, 128))
    tk = _pick(k, (1408, 1152, 1024, 768, 512, 256, 128))
    nk = k // tk
    n_p = len(pairs)

    def body(*refs):
        o_ref, acc_ref = refs[-2], refs[-1]
        kk = pl.program_id(1)

        @pl.when(kk == 0)
        def _():
            acc_ref[...] = jnp.zeros_like(acc_ref)

        for p in range(n_p):
            acc_ref[...] += _dot(refs[2 * p][...], refs[2 * p + 1][...])

        @pl.when(kk == nk - 1)
        def _():
            acc = acc_ref[...]
            if bias is not None:
                acc = acc + refs[2 * n_p][...]
            o_ref[...] = acc.astype(out_dtype)

    in_specs, args = [], []
    for a, b in pairs:
        in_specs += [pl.BlockSpec((tm, tk), lambda i, kk: (i, kk)), pl.BlockSpec((tk, n), lambda i, kk: (kk, 0))]
        args += [a, b]
    if bias is not None:
        in_specs.append(pl.BlockSpec((1, n), lambda i, kk: (0, 0)))
        args.append(bias)
    return _call(body, name=name, grid=(m // tm, nk), in_specs=in_specs,
                 out_specs=pl.BlockSpec((tm, n), lambda i, kk: (i, 0)),
                 out_shape=jax.ShapeDtypeStruct((m, n), out_dtype), args=args,
                 scratch=[pltpu.VMEM((tm, n), F32)], sem=("parallel", "arbitrary"), carry=carry)


def _mm_tn(a, b, name, out_dtype=BF16, carry=None):
    k, m = a.shape
    n = b.shape[1]
    tm = _pick(m, (1408, 1152, 1024, 768, 512, 256, 128))
    tk = _pick(k, (512, 256, 128))
    nk = k // tk

    def body(a_ref, b_ref, o_ref, acc_ref):
        kk = pl.program_id(1)

        @pl.when(kk == 0)
        def _():
            acc_ref[...] = jnp.zeros_like(acc_ref)

        acc_ref[...] += _dot_tn(a_ref[...], b_ref[...])

        @pl.when(kk == nk - 1)
        def _():
            o_ref[...] = acc_ref[...].astype(out_dtype)

    return _call(body, name=name, grid=(m // tm, nk),
                 in_specs=[pl.BlockSpec((tk, tm), lambda i, kk: (kk, i)), pl.BlockSpec((tk, n), lambda i, kk: (kk, 0))],
                 out_specs=pl.BlockSpec((tm, n), lambda i, kk: (i, 0)),
                 out_shape=jax.ShapeDtypeStruct((m, n), out_dtype), args=[a, b],
                 scratch=[pltpu.VMEM((tm, n), F32)], sem=("parallel", "arbitrary"), carry=carry)


def _mm_tn_pair(a, b, name, carry=None):
    k, m = a.shape
    n = b.shape[1]
    rows = m // N_DEV
    n_chip = N_DEV // 2
    tm = 4 * rows
    tk = _pick(k, (512, 256, 128))
    nk = k // tk

    def body(a_ref, b_ref, p_ref, acc_ref, keep_ref, send_ref, land_ref, send_sems, recv_sems):
        i, kk = pl.program_id(0), pl.program_id(1)
        x, y, c = _mesh_pos()

        def push(chip):
            return pltpu.make_async_remote_copy(
                src_ref=send_ref.at[chip], dst_ref=land_ref.at[chip], send_sem=send_sems.at[chip],
                recv_sem=recv_sems.at[chip], device_id=(x, y, 1 - c), device_id_type=MESH)

        @pl.when(kk == 0)
        def _():
            acc_ref[...] = jnp.zeros_like(acc_ref)

        acc_ref[...] += _dot_tn(a_ref[...], b_ref[...])

        for t in range(2):
            @pl.when((kk == nk - 1) & (i == t))
            def _(t=t):
                for ob in range(4):
                    chip, core = 2 * t + ob // 2, ob % 2
                    blk = acc_ref[ob * rows:(ob + 1) * rows, :]

                    @pl.when(c == core)
                    def _(chip=chip, blk=blk):
                        keep_ref[chip] = blk

                    @pl.when(c != core)
                    def _(chip=chip, blk=blk):
                        send_ref[chip] = blk.astype(BF16)
                        push(chip).start()

        @pl.when((kk == nk - 1) & (i == 1))
        def _():
            for chip in range(n_chip):
                push(chip).wait_recv()
                p_ref[chip * rows:(chip + 1) * rows, :] = (
                    keep_ref[chip] + land_ref[chip].astype(F32)).astype(BF16)
            for chip in range(n_chip):
                push(chip).wait_send()

    return _call(body, name=name, grid=(2, nk),
                 in_specs=[pl.BlockSpec((tk, tm), lambda i, kk: (kk, i)), pl.BlockSpec((tk, n), lambda i, kk: (kk, 0))],
                 out_specs=pl.BlockSpec((n_chip * rows, n), lambda i, kk: (0, 0)),
                 out_shape=jax.ShapeDtypeStruct((n_chip * rows, n), BF16), args=[a, b],
                 scratch=[pltpu.VMEM((tm, n), F32), pltpu.VMEM((n_chip, rows, n), F32),
                          pltpu.VMEM((n_chip, rows, n), BF16), pltpu.VMEM((n_chip, rows, n), BF16),
                          pltpu.SemaphoreType.DMA((n_chip,)), pltpu.SemaphoreType.DMA((n_chip,))],
                 sem=("arbitrary", "arbitrary"), carry=carry)


def _ffn_up(h, wg_t, wu_t, name, carry=None):
    s, d = h.shape
    f = wg_t.shape[0]
    tm = _pick(s, (512, 256, 128))
    tf = _pick(f, (1408, 1024, 512, 256, 128))

    def body(h_ref, wg_ref, wu_ref, a_ref, b_ref, u_ref):
        hh = h_ref[...]
        a = _dot_nt(hh, wg_ref[...])
        b = _dot_nt(hh, wu_ref[...])
        a_ref[...] = a.astype(BF16)
        b_ref[...] = b.astype(BF16)
        u_ref[...] = ((a * _sigmoid(a)) * b).astype(BF16)

    w_spec = pl.BlockSpec((tf, d), lambda i, j: (j, 0))
    o_spec = pl.BlockSpec((tm, tf), lambda i, j: (i, j))
    o_shape = jax.ShapeDtypeStruct((s, f), BF16)
    return _call(body, name=name, grid=(s // tm, f // tf),
                 in_specs=[pl.BlockSpec((tm, d), lambda i, j: (i, 0)), w_spec, w_spec],
                 out_specs=(o_spec, o_spec, o_spec), out_shape=(o_shape, o_shape, o_shape),
                 args=[h, wg_t, wu_t], sem=("parallel", "parallel"), carry=carry)


def _ffn_down_bwd(dy, wd, a, b, name, carry=None):
    s, d = dy.shape
    f = wd.shape[0]
    tm = _pick(s, (512, 256, 128))
    tf = _pick(f, (1408, 1024, 512, 256, 128))

    def body(dy_ref, wd_ref, a_ref, b_ref, da_ref, db_ref):
        du = _dot_nt(dy_ref[...], wd_ref[...])
        a = a_ref[...].astype(F32)
        b = b_ref[...].astype(F32)
        sig = _sigmoid(a)
        da_ref[...] = (du * b * (sig * (1.0 + a * (1.0 - sig)))).astype(BF16)
        db_ref[...] = (du * (a * sig)).astype(BF16)

    t_spec = pl.BlockSpec((tm, tf), lambda i, j: (i, j))
    o_shape = jax.ShapeDtypeStruct((s, f), BF16)
    return _call(body, name=name, grid=(s // tm, f // tf),
                 in_specs=[pl.BlockSpec((tm, d), lambda i, j: (i, 0)), pl.BlockSpec((tf, d), lambda i, j: (j, 0)),
                           t_spec, t_spec],
                 out_specs=(t_spec, t_spec), out_shape=(o_shape, o_shape), args=[dy, wd, a, b],
                 sem=("parallel", "parallel"), carry=carry)


def _row_tile(s):
    return _pick(s, (256, 128, 64))


def _vec_spec(d):
    return pl.BlockSpec((1, d), lambda i: (0, 0))


def _pre_norm(x, g, scale, shift, name):
    s, d = x.shape
    ts = _row_tile(s)

    def body(x_ref, g_ref, sc_ref, sh_ref, h_ref):
        xv = x_ref[...]
        r = lax.rsqrt(jnp.mean(xv * xv, axis=-1, keepdims=True) + EPS)
        h_ref[...] = (((xv * r) * g_ref[...]) * (1.0 + sc_ref[...]) + sh_ref[...]).astype(BF16)

    row = pl.BlockSpec((ts, d), lambda i: (i, 0))
    return _call(body, name=name, grid=(s // ts,), in_specs=[row, _vec_spec(d), _vec_spec(d), _vec_spec(d)],
                 out_specs=row, out_shape=jax.ShapeDtypeStruct((s, d), BF16), args=[x, g, scale, shift],
                 sem=("parallel",))


def _post_norm_residual(x, y, g, gate, weight, name):
    s, d = x.shape
    ts = _row_tile(s)

    def body(x_ref, y_ref, g_ref, gate_ref, o_ref):
        yv = y_ref[...]
        r = lax.rsqrt(jnp.mean(yv * yv, axis=-1, keepdims=True) + EPS)
        o_ref[...] = x_ref[...] + (weight * gate_ref[...]) * ((yv * r) * g_ref[...])

    row = pl.BlockSpec((ts, d), lambda i: (i, 0))
    return _call(body, name=name, grid=(s // ts,), in_specs=[row, row, _vec_spec(d), _vec_spec(d)],
                 out_specs=row, out_shape=jax.ShapeDtypeStruct((s, d), F32), args=[x, y, g, gate],
                 sem=("parallel",))


def _post_norm_bwd(dout, y, g, gate, weight, name):
    s, d = y.shape
    ts = _row_tile(s)

    def body(do_ref, y_ref, g_ref, gate_ref, dy_ref, s1_ref, cs_ref):
        @pl.when(pl.program_id(0) == 0)
        def _():
            s1_ref[...] = jnp.zeros_like(s1_ref)
            cs_ref[...] = jnp.zeros_like(cs_ref)

        yv = y_ref[...]
        do = do_ref[...]
        r = lax.rsqrt(jnp.mean(yv * yv, axis=-1, keepdims=True) + EPS)
        yn = yv * r
        dyn = do * ((weight * gate_ref[...]) * g_ref[...])
        dy = r * (dyn - yn * jnp.mean(dyn * yn, axis=-1, keepdims=True))
        dy_ref[...] = dy.astype(BF16)
        s1_ref[...] += jnp.sum(do * yn, axis=0, keepdims=True)
        cs_ref[...] += jnp.sum(dy, axis=0, keepdims=True)

    row = pl.BlockSpec((ts, d), lambda i: (i, 0))
    vec = jax.ShapeDtypeStruct((1, d), F32)
    return _call(body, name=name, grid=(s // ts,), in_specs=[row, row, _vec_spec(d), _vec_spec(d)],
                 out_specs=(row, _vec_spec(d), _vec_spec(d)),
                 out_shape=(jax.ShapeDtypeStruct((s, d), BF16), vec, vec), args=[dout, y, g, gate],
                 sem=("arbitrary",))


def _pre_norm_bwd(dh, x, g, scale, dres, name):
    s, d = x.shape
    ts = _row_tile(s)

    def body(dh_ref, x_ref, g_ref, sc_ref, dr_ref, dx_ref, s2_ref, s3_ref):
        @pl.when(pl.program_id(0) == 0)
        def _():
            s2_ref[...] = jnp.zeros_like(s2_ref)
            s3_ref[...] = jnp.zeros_like(s3_ref)

        xv = x_ref[...]
        dh = dh_ref[...]
        r = lax.rsqrt(jnp.mean(xv * xv, axis=-1, keepdims=True) + EPS)
        n = xv * r
        dn = dh * (g_ref[...] * (1.0 + sc_ref[...]))
        dx_ref[...] = dr_ref[...] + r * (dn - n * jnp.mean(dn * n, axis=-1, keepdims=True))
        s2_ref[...] += jnp.sum(dh * n, axis=0, keepdims=True)
        s3_ref[...] += jnp.sum(dh, axis=0, keepdims=True)

    row = pl.BlockSpec((ts, d), lambda i: (i, 0))
    vec = jax.ShapeDtypeStruct((1, d), F32)
    return _call(body, name=name, grid=(s // ts,), in_specs=[row, row, _vec_spec(d), _vec_spec(d), row],
                 out_specs=(row, _vec_spec(d), _vec_spec(d)),
                 out_shape=(jax.ShapeDtypeStruct((s, d), F32), vec, vec), args=[dh, x, g, scale, dres],
                 sem=("arbitrary",))


def _post_pre_norm(x, y, g_post, gate, weight, g_pre, scale, shift, name):
    s, d = x.shape
    ts = _row_tile(s)

    def body(x_ref, y_ref, gp_ref, gate_ref, g_ref, sc_ref, sh_ref, o_ref, h_ref):
        yv = y_ref[...]
        r = lax.rsqrt(jnp.mean(yv * yv, axis=-1, keepdims=True) + EPS)
        xv = x_ref[...] + (weight * gate_ref[...]) * ((yv * r) * gp_ref[...])
        o_ref[...] = xv
        r2 = lax.rsqrt(jnp.mean(xv * xv, axis=-1, keepdims=True) + EPS)
        h_ref[...] = (((xv * r2) * g_ref[...]) * (1.0 + sc_ref[...]) + sh_ref[...]).astype(BF16)

    row = pl.BlockSpec((ts, d), lambda i: (i, 0))
    return _call(body, name=name, grid=(s // ts,), in_specs=[row, row] + [_vec_spec(d)] * 5,
                 out_specs=(row, row),
                 out_shape=(jax.ShapeDtypeStruct((s, d), F32), jax.ShapeDtypeStruct((s, d), BF16)),
                 args=[x, y, g_post, gate, g_pre, scale, shift], sem=("parallel",))


def _post_norm_loss_bwd(x, y, g, gate, weight, target, name):
    s, d = y.shape
    ts = _row_tile(s)

    def body(x_ref, y_ref, g_ref, gate_ref, t_ref, dx_ref, dy_ref, l_ref, s1_ref):
        @pl.when(pl.program_id(0) == 0)
        def _():
            l_ref[...] = jnp.zeros_like(l_ref)
            s1_ref[...] = jnp.zeros_like(s1_ref)

        yv = y_ref[...]
        r = lax.rsqrt(jnp.mean(yv * yv, axis=-1, keepdims=True) + EPS)
        yn = yv * r
        err = (x_ref[...] + (weight * gate_ref[...]) * (yn * g_ref[...])) - t_ref[...]
        do = err * (1.0 / d)
        dx_ref[...] = do
        l_ref[...] += 0.5 * jnp.sum(jnp.mean(err * err, axis=-1, keepdims=True), axis=0, keepdims=True)
        dyn = do * ((weight * gate_ref[...]) * g_ref[...])
        dy_ref[...] = (r * (dyn - yn * jnp.mean(dyn * yn, axis=-1, keepdims=True))).astype(BF16)
        s1_ref[...] += jnp.sum(do * yn, axis=0, keepdims=True)

    row = pl.BlockSpec((ts, d), lambda i: (i, 0))
    return _call(body, name=name, grid=(s // ts,), in_specs=[row, row, _vec_spec(d), _vec_spec(d), row],
                 out_specs=(row, row, pl.BlockSpec((1, 1), lambda i: (0, 0)), _vec_spec(d)),
                 out_shape=(jax.ShapeDtypeStruct((s, d), F32), jax.ShapeDtypeStruct((s, d), BF16),
                            jax.ShapeDtypeStruct((1, 1), F32), jax.ShapeDtypeStruct((1, d), F32)),
                 args=[x, y, g, gate, target], sem=("arbitrary",))


def _pre_post_norm_bwd(dh, x, g_pre, scale, dres, y, g_post, gate, weight, name):
    s, d = x.shape
    ts = _row_tile(s)

    def body(dh_ref, x_ref, g_ref, sc_ref, dr_ref, y_ref, gp_ref, gate_ref,
             dx_ref, dy_ref, s2_ref, s3_ref, s1_ref, cs_ref):
        @pl.when(pl.program_id(0) == 0)
        def _():
            for ref in (s2_ref, s3_ref, s1_ref, cs_ref):
                ref[...] = jnp.zeros_like(ref)

        xv = x_ref[...]
        dh = dh_ref[...]
        r = lax.rsqrt(jnp.mean(xv * xv, axis=-1, keepdims=True) + EPS)
        n = xv * r
        dn = dh * (g_ref[...] * (1.0 + sc_ref[...]))
        dx = dr_ref[...] + r * (dn - n * jnp.mean(dn * n, axis=-1, keepdims=True))
        dx_ref[...] = dx
        s2_ref[...] += jnp.sum(dh * n, axis=0, keepdims=True)
        s3_ref[...] += jnp.sum(dh, axis=0, keepdims=True)
        yv = y_ref[...]
        ry = lax.rsqrt(jnp.mean(yv * yv, axis=-1, keepdims=True) + EPS)
        yn = yv * ry
        dyn = dx * ((weight * gate_ref[...]) * gp_ref[...])
        dy = ry * (dyn - yn * jnp.mean(dyn * yn, axis=-1, keepdims=True))
        dy_ref[...] = dy.astype(BF16)
        s1_ref[...] += jnp.sum(dx * yn, axis=0, keepdims=True)
        cs_ref[...] += jnp.sum(dy, axis=0, keepdims=True)

    row = pl.BlockSpec((ts, d), lambda i: (i, 0))
    vec = jax.ShapeDtypeStruct((1, d), F32)
    return _call(body, name=name, grid=(s // ts,),
                 in_specs=[row, row, _vec_spec(d), _vec_spec(d), row, row, _vec_spec(d), _vec_spec(d)],
                 out_specs=(row, row) + (_vec_spec(d),) * 4,
                 out_shape=(jax.ShapeDtypeStruct((s, d), F32), jax.ShapeDtypeStruct((s, d), BF16), vec, vec, vec, vec),
                 args=[dh, x, g_pre, scale, dres, y, g_post, gate], sem=("arbitrary",))


def _group_norm_cat(oa, ob, ga, gb):
    s = oa.shape[0]
    ts = _row_tile(s)

    def body(oa_ref, ob_ref, ga_ref, gb_ref, y_ref):
        for o_ref, g_ref, lo, w in ((oa_ref, ga_ref, 0, QA), (ob_ref, gb_ref, QA, QB)):
            ov = o_ref[...]
            r = lax.rsqrt(jnp.mean(ov * ov, axis=-1, keepdims=True) + EPS)
            y_ref[:, lo:lo + w] = ((ov * r) * g_ref[...]).astype(BF16)

    return _call(body, name="group_norm_cat", grid=(s // ts,),
                 in_specs=[pl.BlockSpec((ts, QA), lambda i: (i, 0)), pl.BlockSpec((ts, QB), lambda i: (i, 0)),
                           _vec_spec(QA), _vec_spec(QB)],
                 out_specs=pl.BlockSpec((ts, QA + QB), lambda i: (i, 0)),
                 out_shape=jax.ShapeDtypeStruct((s, QA + QB), BF16), args=[oa, ob, ga, gb], sem=("parallel",))


def _group_norm_bwd(dy, oa, ob, ga, gb):
    s = oa.shape[0]
    ts = _row_tile(s)

    def body(dy_ref, oa_ref, ob_ref, ga_ref, gb_ref, doa_ref, dob_ref, dga_ref, dgb_ref):
        @pl.when(pl.program_id(0) == 0)
        def _():
            dga_ref[...] = jnp.zeros_like(dga_ref)
            dgb_ref[...] = jnp.zeros_like(dgb_ref)

        for o_ref, g_ref, do_ref, dg_ref, lo, w in ((oa_ref, ga_ref, doa_ref, dga_ref, 0, QA),
                                                    (ob_ref, gb_ref, dob_ref, dgb_ref, QA, QB)):
            ov = o_ref[...]
            dyv = dy_ref[:, lo:lo + w]
            r = lax.rsqrt(jnp.mean(ov * ov, axis=-1, keepdims=True) + EPS)
            n = ov * r
            dn = dyv * g_ref[...]
            do_ref[...] = r * (dn - n * jnp.mean(dn * n, axis=-1, keepdims=True))
            dg_ref[...] += jnp.sum(dyv * n, axis=0, keepdims=True)

    ra = pl.BlockSpec((ts, QA), lambda i: (i, 0))
    rb = pl.BlockSpec((ts, QB), lambda i: (i, 0))
    return _call(body, name="group_norm_bwd", grid=(s // ts,),
                 in_specs=[pl.BlockSpec((ts, QA + QB), lambda i: (i, 0)), ra, rb, _vec_spec(QA), _vec_spec(QB)],
                 out_specs=(ra, rb, _vec_spec(QA), _vec_spec(QB)),
                 out_shape=(jax.ShapeDtypeStruct((s, QA), F32), jax.ShapeDtypeStruct((s, QB), F32),
                            jax.ShapeDtypeStruct((1, QA), F32), jax.ShapeDtypeStruct((1, QB), F32)),
                 args=[dy, oa, ob, ga, gb], sem=("arbitrary",))


def _loss_and_grad(y, target):
    s, d = y.shape
    ts = _row_tile(s)

    def body(y_ref, t_ref, l_ref, g_ref):
        @pl.when(pl.program_id(0) == 0)
        def _():
            l_ref[...] = jnp.zeros_like(l_ref)

        err = y_ref[...] - t_ref[...]
        g_ref[...] = err * (1.0 / d)
        row = jnp.mean(err * err, axis=-1, keepdims=True)
        l_ref[...] += 0.5 * jnp.sum(row, axis=0, keepdims=True)

    row = pl.BlockSpec((ts, d), lambda i: (i, 0))
    return _call(body, name="loss_and_grad", grid=(s // ts,), in_specs=[row, row],
                 out_specs=(pl.BlockSpec((1, 1), lambda i: (0, 0)), row),
                 out_shape=(jax.ShapeDtypeStruct((1, 1), F32), jax.ShapeDtypeStruct((s, d), F32)),
                 args=[y, target], sem=("arbitrary",))


def _col_sum(x, name):
    s, n = x.shape
    ts = _row_tile(s)

    def body(x_ref, o_ref):
        @pl.when(pl.program_id(0) == 0)
        def _():
            o_ref[...] = jnp.zeros_like(o_ref)

        o_ref[...] += jnp.sum(x_ref[...].astype(F32), axis=0, keepdims=True)

    return _call(body, name=name, grid=(s // ts,), in_specs=[pl.BlockSpec((ts, n), lambda i: (i, 0))],
                 out_specs=pl.BlockSpec((1, n), lambda i: (0, 0)), out_shape=jax.ShapeDtypeStruct((1, n), F32),
                 args=[x], sem=("arbitrary",))


def _n_variants(n_back):
    return -(-n_back // QG) + 1


def _alibi_bias():
    i = np.arange(QROWS)[:, None]
    j = np.arange((QG + BACK_A) * CHUNK)[None, :]
    dist = np.abs(BACK_A * CHUNK + i - j).astype(np.float32)
    dc = j // CHUNK - i // CHUNK
    valid = (dc >= 0) & (dc <= BACK_A)
    slopes = np.array([2.0 ** (-8.0 * (h + 1) / H_A) for h in range(H_A)], dtype=np.float32)
    bias = -slopes[:, None, None] * dist[None]
    out = [np.where((valid & (j >= (BACK_A - QG * v) * CHUNK))[None], bias, np.float32(NEG_INF))
           for v in range(_n_variants(BACK_A))]
    return jnp.asarray(np.stack(out).astype(np.float32))


def _rel_index_matrix():
    cc = np.arange(SKEW)
    dist = np.where(cc < SKEW - QROWS, BACK_B * CHUNK - cc, BACK_B * CHUNK + SKEW - cc)
    idx = np.clip(dist, -REL_CLIP, REL_CLIP) + REL_CLIP
    m = np.zeros((SKEW, N_REL), np.float32)
    m[cc, idx] = 1.0
    return jnp.asarray(m)


def _toeplitz_bias(vec, carry=None):
    lk = (QG + BACK_B) * CHUNK
    nv = _n_variants(BACK_B)

    def body(v_ref, o_ref):
        xv = jnp.broadcast_to(v_ref[0], (QROWS, SKEW))
        row = lax.broadcasted_iota(jnp.int32, (QROWS, SKEW), 0)
        for bit in range(QROWS.bit_length() - 1):
            xv = jnp.where((row >> bit) & 1 == 1, pltpu.roll(xv, 1 << bit, 1), xv)
        ri = lax.broadcasted_iota(jnp.int32, (QROWS, lk), 0) // CHUNK
        col = lax.broadcasted_iota(jnp.int32, (QROWS, lk), 1)
        ci = col // CHUNK
        valid = (ci - ri >= 0) & (ci - ri <= BACK_B)
        for v in range(nv):
            o_ref[v, 0] = jnp.where(valid & (col >= (BACK_B - QG * v) * CHUNK), xv[:, :lk], NEG_INF)

    return _call(body, name="toeplitz_bias", grid=(H_B,),
                 in_specs=[pl.BlockSpec((1, 1, SKEW), lambda h: (h, 0, 0))],
                 out_specs=pl.BlockSpec((nv, 1, QROWS, lk), lambda h: (0, h, 0, 0)),
                 out_shape=jax.ShapeDtypeStruct((nv, H_B, QROWS, lk), F32), args=[vec], sem=("parallel",),
                 carry=carry)


def _diagonal_sums(dbias):
    lk = dbias.shape[2]

    def body(d_ref, o_ref):
        xp = jnp.concatenate([d_ref[0], jnp.zeros((QROWS, SKEW - lk), F32)], axis=1)
        xv = xp[0:CHUNK]
        for q in range(1, QG):
            xv = xv + pltpu.roll(xp[q * CHUNK:(q + 1) * CHUNK], SKEW - q * CHUNK, 1)
        row = lax.broadcasted_iota(jnp.int32, (CHUNK, SKEW), 0)
        for bit in range(CHUNK.bit_length() - 1):
            xv = jnp.where((row >> bit) & 1 == 1, pltpu.roll(xv, SKEW - (1 << bit), 1), xv)
        o_ref[0] = jnp.sum(xv, axis=0, keepdims=True)

    return _call(body, name="diagonal_sums", grid=(H_B,),
                 in_specs=[pl.BlockSpec((1, QROWS, lk), lambda h: (h, 0, 0))],
                 out_specs=pl.BlockSpec((1, 1, SKEW), lambda h: (h, 0, 0)),
                 out_shape=jax.ShapeDtypeStruct((H_B, 1, SKEW), F32), args=[dbias], sem=("parallel",))


def _attn_common(s, n_back, gqa, q_col, k_col, v_col):
    lk = (QG + n_back) * CHUNK
    pad = n_back * CHUNK
    q_spec = pl.BlockSpec((QROWS, LANES), lambda t, g: (g, q_col + t))
    if gqa:
        k_spec = pl.BlockSpec((s, LANES), lambda t, g: (0, k_col))
        v_spec = pl.BlockSpec((s, LANES), lambda t, g: (0, v_col))
    else:
        k_spec = pl.BlockSpec((s, LANES), lambda t, g: (0, k_col + t))
        v_spec = pl.BlockSpec((s, LANES), lambda t, g: (0, v_col + t))
    last_variant = _n_variants(n_back) - 1
    bias_spec = pl.BlockSpec((None, 2, QROWS, lk), lambda t, g: (jnp.minimum(g, last_variant), t, 0, 0))
    tile_spec = pl.BlockSpec((QROWS, LANES), lambda t, g: (g, t))
    return lk, pad, q_spec, k_spec, v_spec, bias_spec, tile_spec


def _attention_fwd(proj, bias, sinks, *, n_back, gqa, q_col, k_col, v_col, name, carry=None):
    s = proj.shape[0]
    lk, pad, q_spec, k_spec, v_spec, bias_spec, tile_spec = _attn_common(s, n_back, gqa, q_col, k_col, v_col)
    n_t, n_g = 512 // LANES, s // QROWS

    def body(*refs):
        if gqa:
            q_ref, k_ref, v_ref, bias_ref, sink_ref, o_ref, l_ref, kpad, vpad = refs
        else:
            q_ref, k_ref, v_ref, bias_ref, o_ref, l_ref, kpad, vpad = refs
        t, g = pl.program_id(0), pl.program_id(1)

        @pl.when(g == 0)
        def _():
            kpad[0:pad, :] = jnp.zeros((pad, LANES), BF16)
            vpad[0:pad, :] = jnp.zeros((pad, LANES), BF16)
            kpad[pad:, :] = k_ref[...]
            vpad[pad:, :] = v_ref[...]

        start = pl.multiple_of(g * QROWS, QROWS)
        kb = kpad[pl.ds(start, lk), :]
        vb = vpad[pl.ds(start, lk), :]
        half = lax.broadcasted_iota(jnp.int32, (QROWS, LANES), 1) // HEAD_DIM
        q = q_ref[...] * (HEAD_DIM ** -0.5)
        if gqa:
            hk = t // 2
            q_rolled = pltpu.roll(q.astype(F32), HEAD_DIM, 1).astype(BF16)
        outs, lses = [], []
        for e in range(2):
            if gqa:
                kv_half = hk
                src = jnp.where(hk == e, q, q_rolled)
            else:
                kv_half = e
                src = q
            qm = jnp.where(half == kv_half, src, jnp.zeros_like(src))
            sc = _dot_nt(qm, kb) + bias_ref[e]
            m = jnp.max(sc, axis=-1, keepdims=True)
            if gqa:
                sk = sink_ref[2 * t + e]
                m = jnp.maximum(m, sk)
            p = jnp.exp(sc - m)
            l = jnp.sum(p, axis=-1, keepdims=True)
            if gqa:
                l = l + jnp.exp(sk - m)
            pn = p / l
            outs.append(_dot(pn.astype(BF16), vb))
            lses.append(m + jnp.log(l))
        if gqa:
            same = jnp.where(hk == 0, outs[0], outs[1])
            other = jnp.where(hk == 0, outs[1], outs[0])
            o_ref[...] = jnp.where(half == hk, same, pltpu.roll(other, HEAD_DIM, 1))
        else:
            o_ref[...] = jnp.where(half == 0, outs[0], outs[1])
        l_ref[...] = jnp.where(half == 0, lses[0], lses[1])

    in_specs = [q_spec, k_spec, v_spec, bias_spec] + ([SMEM_SPEC] if gqa else [])
    args = [proj, proj, proj, bias] + ([sinks] if gqa else [])
    o_shape = jax.ShapeDtypeStruct((s, 512), F32)
    return _call(body, name=name, grid=(n_t, n_g), in_specs=in_specs, out_specs=(tile_spec, tile_spec),
                 out_shape=(o_shape, o_shape), args=args,
                 scratch=[pltpu.VMEM((s + pad, LANES), BF16), pltpu.VMEM((s + pad, LANES), BF16)],
                 sem=("arbitrary", "arbitrary"), carry=carry)


def _attention_bwd(proj, bias, sinks, do, lse, *, n_back, gqa, q_col, k_col, v_col, name, carry=None):
    s = proj.shape[0]
    lk, pad, q_spec, k_spec, v_spec, bias_spec, tile_spec = _attn_common(s, n_back, gqa, q_col, k_col, v_col)
    n_t, n_g = 512 // LANES, s // QROWS

    def body(*refs):
        if gqa:
            (q_ref, k_ref, v_ref, bias_ref, sink_ref, do_ref, l_ref,
             dq_ref, dk_ref, dv_ref, dsink_ref, kpad, vpad, dkpad, dvpad) = refs
        else:
            (q_ref, k_ref, v_ref, bias_ref, do_ref, l_ref,
             dq_ref, dk_ref, dv_ref, dbias_ref, kpad, vpad, dkpad, dvpad) = refs
        t, g = pl.program_id(0), pl.program_id(1)

        @pl.when(g == 0)
        def _():
            kpad[0:pad, :] = jnp.zeros((pad, LANES), BF16)
            vpad[0:pad, :] = jnp.zeros((pad, LANES), BF16)
            kpad[pad:, :] = k_ref[...]
            vpad[pad:, :] = v_ref[...]
            if gqa:
                dsink_ref[...] = jnp.zeros_like(dsink_ref)
            else:
                dbias_ref[...] = jnp.zeros_like(dbias_ref)

        @pl.when((g == 0) & (t == 0) if gqa else g == 0)
        def _():
            dkpad[...] = jnp.zeros_like(dkpad)
            dvpad[...] = jnp.zeros_like(dvpad)

        start = pl.multiple_of(g * QROWS, QROWS)
        kb = kpad[pl.ds(start, lk), :]
        vb = vpad[pl.ds(start, lk), :]
        half = lax.broadcasted_iota(jnp.int32, (QROWS, LANES), 1) // HEAD_DIM
        q = q_ref[...]
        dov = do_ref[...]
        lv = l_ref[...]
        if gqa:
            hk = t // 2
            q_rolled = pltpu.roll(q.astype(F32), HEAD_DIM, 1).astype(BF16)
            do_rolled = pltpu.roll(dov, HEAD_DIM, 1)
        dqs = []
        dk_acc = jnp.zeros((lk, LANES), F32)
        dv_acc = jnp.zeros((lk, LANES), F32)
        for e in range(2):
            if gqa:
                kv_half = hk
                src = jnp.where(hk == e, q, q_rolled)
                do_src = jnp.where(hk == e, dov, do_rolled)
            else:
                kv_half = e
                src = q
                do_src = dov
            qm = jnp.where(half == kv_half, src, jnp.zeros_like(src))
            dom = jnp.where(half == kv_half, do_src, 0.0).astype(BF16)
            lcol = jnp.max(jnp.where(half == e, lv, -jnp.inf), axis=-1, keepdims=True)
            sc = _dot_nt(qm * (HEAD_DIM ** -0.5), kb) + bias_ref[e]
            pn = jnp.exp(sc - lcol)
            dp = _dot_nt(dom, vb)
            delta = jnp.sum(pn * dp, axis=-1, keepdims=True)
            ds = pn * (dp - delta)
            if gqa:
                p_sink = jnp.exp(sink_ref[2 * t + e] - lcol)
                dsk = -jnp.sum(p_sink * delta, axis=0, keepdims=True)
                dsink_ref[0, e:e + 1, :] += jnp.broadcast_to(dsk, (1, LANES))
            else:
                dbias_ref[e] += ds
            dsb = (ds * (HEAD_DIM ** -0.5)).astype(BF16)
            dqs.append(_dot(dsb, kb))
            dk_acc = dk_acc + _dot_tn(dsb, qm)
            dv_acc = dv_acc + _dot_tn(pn.astype(BF16), dom)
        dkpad[pl.ds(start, lk), :] += dk_acc
        dvpad[pl.ds(start, lk), :] += dv_acc
        if gqa:
            same = jnp.where(hk == 0, dqs[0], dqs[1])
            other = jnp.where(hk == 0, dqs[1], dqs[0])
            dq_ref[...] = jnp.where(half == hk, same, pltpu.roll(other, HEAD_DIM, 1)).astype(BF16)
        else:
            dq_ref[...] = jnp.where(half == 0, dqs[0], dqs[1]).astype(BF16)

        @pl.when((g == n_g - 1) & (t == n_t - 1) if gqa else g == n_g - 1)
        def _():
            dk_ref[...] = dkpad[pad:, :].astype(BF16)
            dv_ref[...] = dvpad[pad:, :].astype(BF16)

    in_specs = [q_spec, k_spec, v_spec, bias_spec] + ([SMEM_SPEC] if gqa else []) + [tile_spec, tile_spec]
    args = [proj, proj, proj, bias] + ([sinks] if gqa else []) + [do, lse]
    if gqa:
        kv_out = pl.BlockSpec((s, LANES), lambda t, g: (0, 0))
        kv_shape = jax.ShapeDtypeStruct((s, LANES), BF16)
        extra_spec = pl.BlockSpec((1, 8, LANES), lambda t, g: (t, 0, 0))
        extra_shape = jax.ShapeDtypeStruct((n_t, 8, LANES), F32)
    else:
        kv_out = pl.BlockSpec((s, LANES), lambda t, g: (0, t))
        kv_shape = jax.ShapeDtypeStruct((s, 512), BF16)
        extra_spec = pl.BlockSpec((2, QROWS, lk), lambda t, g: (t, 0, 0))
        extra_shape = jax.ShapeDtypeStruct(bias.shape[1:], F32)
    return _call(body, name=name, grid=(n_t, n_g), in_specs=in_specs,
                 out_specs=(tile_spec, kv_out, kv_out, extra_spec),
                 out_shape=(jax.ShapeDtypeStruct((s, 512), BF16), kv_shape, kv_shape, extra_shape), args=args,
                 scratch=[pltpu.VMEM((s + pad, LANES), BF16), pltpu.VMEM((s + pad, LANES), BF16),
                          pltpu.VMEM((s + pad, LANES), F32), pltpu.VMEM((s + pad, LANES), F32)],
                 sem=("arbitrary", "arbitrary"), carry=carry)


def _sum_slots(r, name):
    n_slots, rows, k = r.shape

    def body(r_ref, o_ref):
        acc = r_ref[0].astype(F32)
        for j in range(1, n_slots):
            acc = acc + r_ref[j].astype(F32)
        o_ref[...] = acc

    return _call(body, name=name, grid=(k // LANES,),
                 in_specs=[pl.BlockSpec((n_slots, rows, LANES), lambda i: (0, 0, i))],
                 out_specs=pl.BlockSpec((rows, LANES), lambda i: (0, i)),
                 out_shape=jax.ShapeDtypeStruct((rows, k), F32), args=[r], sem=("parallel",))


def _sum_rows8(g):
    n = g.shape[2]

    def body(g_ref, o_ref):
        acc = g_ref[0]
        for j in range(1, N_DEV):
            acc = acc + g_ref[j]
        o_ref[...] = acc

    return pl.pallas_call(
        body, name="sum_small_grads", in_specs=[VMEM_SPEC], out_specs=VMEM_SPEC,
        out_shape=jax.ShapeDtypeStruct((1, n), F32), compiler_params=_params(),
    )(g)


def _ada_weight_grad(sc_t, dmod_cols):
    d = sc_t.shape[0]
    w = dmod_cols.shape[1]
    td = _pick(d, (256, 128))

    def body(sc_ref, dm_ref, o_ref):
        scv = sc_ref[...]
        dmv = dm_ref[...]
        acc = scv[:, 0:1] * dmv[0:1, :]
        for b in range(1, N_DEV):
            acc = acc + scv[:, b:b + 1] * dmv[b:b + 1, :]
        o_ref[...] = acc

    return _call(body, name="ada_weight_grad", grid=(d // td,),
                 in_specs=[pl.BlockSpec((td, N_DEV), lambda i: (i, 0)), pl.BlockSpec((N_DEV, w), lambda i: (0, 0))],
                 out_specs=pl.BlockSpec((td, w), lambda i: (i, 0)), out_shape=jax.ShapeDtypeStruct((d, w), F32),
                 args=[sc_t, dmod_cols], sem=("parallel",))


def _adamw_update(w, gv, m, v):
    nm = ADAM_B1 * m + (1.0 - ADAM_B1) * gv
    nv = ADAM_B2 * v + (1.0 - ADAM_B2) * (gv * gv)
    m_hat = nm / (1.0 - ADAM_B1 ** ADAM_STEP)
    v_hat = nv / (1.0 - ADAM_B2 ** ADAM_STEP)
    return -ADAM_LR * (m_hat / (jnp.sqrt(v_hat) + ADAM_EPS) + ADAM_WD * w), nm, nv


def _adamw(w, g, m, v, name):
    rows, cols = w.shape
    tr = _pick(rows, (256, 176, 128, 88, 64)) if rows > 256 else rows

    def body(w_ref, g_ref, m_ref, v_ref, d_ref, nm_ref, nv_ref):
        d_ref[...], nm_ref[...], nv_ref[...] = _adamw_update(w_ref[...], g_ref[...], m_ref[...], v_ref[...])

    spec = pl.BlockSpec((tr, cols), lambda i: (i, 0))
    shape = jax.ShapeDtypeStruct((rows, cols), F32)
    return _call(body, name=name, grid=(rows // tr,), in_specs=[spec] * 4, out_specs=(spec, spec, spec),
                 out_shape=(shape, shape, shape), args=[w, g, m, v], sem=("parallel",))


def _adamw_from_slots(w, slots, m, v, name):
    n_slots, rows, k = slots.shape

    def body(s_ref, w_ref, m_ref, v_ref, g_ref, d_ref, nm_ref, nv_ref):
        gv = s_ref[0].astype(F32)
        for j in range(1, n_slots):
            gv = gv + s_ref[j].astype(F32)
        g_ref[...] = gv
        d_ref[...], nm_ref[...], nv_ref[...] = _adamw_update(w_ref[...], gv, m_ref[...], v_ref[...])

    tr = rows // 2 if rows % 32 == 0 else rows
    spec = pl.BlockSpec((tr, k), lambda i: (i, 0))
    shape = jax.ShapeDtypeStruct((rows, k), F32)
    return _call(body, name=name, grid=(rows // tr,),
                 in_specs=[pl.BlockSpec((n_slots, tr, k), lambda i: (0, i, 0)), spec, spec, spec],
                 out_specs=(spec, spec, spec, spec), out_shape=(shape, shape, shape, shape),
                 args=[slots, w, m, v], sem=("parallel",))


SMALL = ("b_ada", "g_pre_ffn1", "g_post_ffn1", "g_pre_mix", "b_in", "sinks_a", "rel_bias_b", "g_grp_a",
         "g_grp_b", "b_out", "g_post_mix", "g_pre_ffn2", "g_post_ffn2")
WEIGHTS = ("w_ada", "b_ada", "g_pre_ffn1", "w_gate1", "w_up1", "w_down1", "g_post_ffn1", "g_pre_mix", "w_in",
           "b_in", "sinks_a", "rel_bias_b", "g_grp_a", "g_grp_b", "w_out", "b_out", "g_post_mix", "g_pre_ffn2",
           "w_gate2", "w_up2", "w_down2", "g_post_ffn2")


def kernel(x, c, w_ada, b_ada, g_pre_ffn1, w_gate1, w_up1, w_down1, g_post_ffn1, g_pre_mix, w_in, b_in, sinks_a, rel_bias_b, g_grp_a, g_grp_b, w_out, b_out, g_post_mix, g_pre_ffn2, w_gate2, w_up2, w_down2, g_post_ffn2, loss_target, m_w_ada, m_b_ada, m_g_pre_ffn1, m_w_gate1, m_w_up1, m_w_down1, m_g_post_ffn1, m_g_pre_mix, m_w_in, m_b_in, m_sinks_a, m_rel_bias_b, m_g_grp_a, m_g_grp_b, m_w_out, m_b_out, m_g_post_mix, m_g_pre_ffn2, m_w_gate2, m_w_up2, m_w_down2, m_g_post_ffn2, v_w_ada, v_b_ada, v_g_pre_ffn1, v_w_gate1, v_w_up1, v_w_down1, v_g_post_ffn1, v_g_pre_mix, v_w_in, v_b_in, v_sinks_a, v_rel_bias_b, v_g_grp_a, v_g_grp_b, v_w_out, v_b_out, v_g_post_mix, v_g_pre_ffn2, v_w_gate2, v_w_up2, v_w_down2, v_g_post_ffn2):
    given = dict(locals())
    weights = {n: given[n] for n in WEIGHTS}
    mom_m = {n: given["m_" + n] for n in WEIGHTS}
    mom_v = {n: given["v_" + n] for n in WEIGHTS}

    me = 4 * lax.axis_index("x") + 2 * lax.axis_index("y") + lax.axis_index("c")
    xs = x[0]
    tgt = loss_target[0]
    d_model = xs.shape[1]
    ada_cols = w_ada.shape[2]

    sh = {"wg1": w_gate1[0].T, "wu1": w_up1[0].T, "wd1": w_down1[0], "win": w_in[0].T, "wo": w_out[0],
          "wg2": w_gate2[0].T, "wu2": w_up2[0].T, "wd2": w_down2[0]}
    sh = {k: v.astype(BF16) for k, v in sh.items()}

    def gather(*names):
        return _gather_carry([sh[n] for n in names])

    bias_a = _alibi_bias()
    rel_m = _rel_index_matrix()
    rel_vec = jnp.dot(rel_bias_b[0], rel_m.T, precision=lax.Precision.HIGHEST)
    bias_b, (wg1,) = _toeplitz_bias(rel_vec.reshape(H_B, 1, SKEW), carry=gather("wg1"))

    b_cols = lax.dynamic_slice(b_ada, (0, me * ada_cols), (1, ada_cols))
    (sc_all, mod_rows), (wu1,) = _ada_forward(c, w_ada[0], b_cols, gather("wu1"))
    mod = mod_rows.reshape(N_MOD, d_model)
    shift1, scale1, gate1, shift2, scale2, gate2, shift3, scale3, gate3 = (mod[i:i + 1] for i in range(N_MOD))

    h1 = _pre_norm(xs, g_pre_ffn1, scale1, shift1, "pre_norm_ffn1")
    (a1, b1, u1), (wd1,) = _ffn_up(h1, wg1, wu1, "ffn_up_ffn1", carry=gather("wd1"))
    y1, (win,) = _mm_nn([(u1, wd1)], "ffn_down_ffn1", F32, carry=gather("win"))
    x1, h2 = _post_pre_norm(xs, y1, g_post_ffn1, gate1, 0.5, g_pre_mix, scale2, shift2, "post_ffn1_pre_mix")

    proj, (wo,) = _mm_nt(h2, win, "in_proj", BF16, bias=b_in, carry=gather("wo"))
    sinks = sinks_a[0]
    cfg_a = dict(n_back=BACK_A, gqa=True, q_col=0, k_col=QA // LANES, v_col=(QA + KVA) // LANES)
    cfg_b = dict(n_back=BACK_B, gqa=False, q_col=(QA + 2 * KVA) // LANES, k_col=(QA + 2 * KVA + QB) // LANES,
                 v_col=(QA + 2 * KVA + 2 * QB) // LANES)
    (oa, lse_a), (wg2,) = _attention_fwd(proj, bias_a, sinks, name="attn_a", carry=gather("wg2"), **cfg_a)
    (ob, lse_b), (wu2,) = _attention_fwd(proj, bias_b, None, name="attn_b", carry=gather("wu2"), **cfg_b)
    ycat = _group_norm_cat(oa, ob, g_grp_a, g_grp_b)
    ymix = _mm_nn([(ycat, wo)], "out_proj", F32, bias=b_out)
    x2, h3 = _post_pre_norm(x1, ymix, g_post_mix, gate2, 1.0, g_pre_ffn2, scale3, shift3, "post_mix_pre_ffn2")

    (a3, b3, u3), (wd2,) = _ffn_up(h3, wg2, wu2, "ffn_up_ffn2", carry=gather("wd2"))
    y3 = _mm_nn([(u3, wd2)], "ffn_down_ffn2", F32)

    def scatter(*grads):
        return _scatter_carry(list(grads))

    slots = {}

    dx3, dy, loss_part, s1 = _post_norm_loss_bwd(x2, y3, g_post_ffn2, gate3, 0.5, tgt, "post_ffn2_loss_bwd")
    da, db = _ffn_down_bwd(dy, wd2, a3, b3, "ffn_down_bwd_ffn2")
    dwd2 = _mm_tn_pair(u3, dy, "grad_wd_ffn2")
    dwg2 = _mm_tn_pair(da, h3, "grad_wg_ffn2")
    dwu2 = _mm_tn_pair(db, h3, "grad_wu_ffn2")
    dh, (slots["wd2"],) = _mm_nn([(da, wg2), (db, wu2)], "ffn_up_bwd_ffn2", F32, carry=scatter(dwd2))
    dx2, dymix, s2, s3, s1m, db_out = _pre_post_norm_bwd(dh, x2, g_pre_ffn2, scale3, dx3, ymix, g_post_mix, gate2,
                                                         1.0, "pre_ffn2_post_mix_bwd")
    sm3 = dict(shift=s3, scale=s2 * g_pre_ffn2, gate=0.5 * g_post_ffn2 * s1,
               g_pre=(1.0 + scale3) * s2, g_post=(0.5 * gate3) * s1)

    dycat = _mm_nt(dymix, wo, "out_proj_bwd", F32)
    dwo = _mm_tn_pair(ycat, dymix, "grad_wo")
    doa, dob, dg_a, dg_b = _group_norm_bwd(dycat, oa, ob, g_grp_a, g_grp_b)
    (dqa, dka, dva, dsink), (slots["wg2"],) = _attention_bwd(
        proj, bias_a, sinks, doa, lse_a, name="attn_a_bwd", carry=scatter(dwg2), **cfg_a)
    (dqb, dkb, dvb, dbias), (slots["wu2"], slots["wo"]) = _attention_bwd(
        proj, bias_b, None, dob, lse_b, name="attn_b_bwd", carry=scatter(dwu2, dwo), **cfg_b)
    dproj = jnp.concatenate([dqa, dka, dva, dqb, dkb, dvb], axis=1)
    db_in = _col_sum(dproj, "grad_b_in")
    dwin = _mm_tn_pair(dproj, h2, "grad_win")
    dh2 = _mm_nn([(dproj, win)], "in_proj_bwd", F32)
    dx1, dy, s2m, s3m, s1, _ = _pre_post_norm_bwd(dh2, x1, g_pre_mix, scale2, dx2, y1, g_post_ffn1, gate1, 0.5,
                                                  "pre_mix_post_ffn1_bwd")
    d_rel = jnp.dot(_diagonal_sums(dbias).reshape(H_B, SKEW), rel_m, precision=lax.Precision.HIGHEST)
    d_sinks = dsink[:, :2, 0].reshape(1, H_A)

    (da, db), (slots["win"],) = _ffn_down_bwd(dy, wd1, a1, b1, "ffn_down_bwd_ffn1", carry=scatter(dwin))
    dwd1 = _mm_tn_pair(u1, dy, "grad_wd_ffn1")
    dwg1, (slots["wd1"],) = _mm_tn_pair(da, h1, "grad_wg_ffn1", carry=scatter(dwd1))
    dwu1, (slots["wg1"],) = _mm_tn_pair(db, h1, "grad_wu_ffn1", carry=scatter(dwg1))
    dh, (slots["wu1"],) = _mm_nn([(da, wg1), (db, wu1)], "ffn_up_bwd_ffn1", F32, carry=scatter(dwu1))
    dx0, s2, s3 = _pre_norm_bwd(dh, xs, g_pre_ffn1, scale1, dx1, "pre_norm_bwd_ffn1")
    sm1 = dict(shift=s3, scale=s2 * g_pre_ffn1, gate=0.5 * g_post_ffn1 * s1,
               g_pre=(1.0 + scale1) * s2, g_post=(0.5 * gate1) * s1)

    dmod = jnp.concatenate([sm1["shift"], sm1["scale"], sm1["gate"],
                            s3m, s2m * g_pre_mix, g_post_mix * s1m,
                            sm3["shift"], sm3["scale"], sm3["gate"]], axis=1)
    small_parts = {
        "b_ada": dmod, "g_pre_ffn1": sm1["g_pre"], "g_post_ffn1": sm1["g_post"],
        "g_pre_mix": (1.0 + scale2) * s2m, "b_in": db_in, "sinks_a": d_sinks,
        "rel_bias_b": d_rel.reshape(1, H_B * N_REL), "g_grp_a": dg_a, "g_grp_b": dg_b, "b_out": db_out,
        "g_post_mix": gate2 * s1m, "g_pre_ffn2": sm3["g_pre"], "g_post_ffn2": sm3["g_post"]}
    sizes = [small_parts[n].shape[1] for n in SMALL]
    n_small = sum(sizes)
    n_pad = -(n_small + 1) % LANES
    packed = jnp.concatenate([small_parts[n] for n in SMALL] + [loss_part, jnp.zeros((1, n_pad), F32)], axis=1)
    gathered = _all_gather_small(packed)
    small_sum = _sum_rows8(gathered)
    loss = small_sum[0, n_small]
    dmod_cols = lax.dynamic_slice(gathered.reshape(N_DEV, n_small + 1 + n_pad), (0, me * ada_cols),
                                  (N_DEV, ada_cols))
    g_ada = _ada_weight_grad(sc_all.reshape(N_DEV, d_model).T, dmod_cols)

    out_g, out_d, out_m, out_v = {}, {}, {}, {}
    d_, m_, v_ = _adamw(w_ada[0], g_ada, m_w_ada[0], v_w_ada[0], "adamw_w_ada")
    out_g["w_ada"], out_d["w_ada"], out_m["w_ada"], out_v["w_ada"] = g_ada[None], d_[None], m_[None], v_[None]
    for n, key, transposed in (("w_gate1", "wg1", True), ("w_up1", "wu1", True), ("w_down1", "wd1", False),
                               ("w_in", "win", True), ("w_out", "wo", False), ("w_gate2", "wg2", True),
                               ("w_up2", "wu2", True), ("w_down2", "wd2", False)):
        view = (lambda t: t.T) if transposed else (lambda t: t)
        res = _adamw_from_slots(view(weights[n][0]), slots[key], view(mom_m[n][0]), view(mom_v[n][0]),
                                "adamw_" + n)
        out_g[n], out_d[n], out_m[n], out_v[n] = (view(t)[None] for t in res)

    def pack(tree):
        return jnp.concatenate([tree[n].reshape(1, -1) for n in SMALL], axis=1)

    g_small = small_sum[:, :n_small]
    d_s, m_s, v_s = _adamw(pack(weights), g_small, pack(mom_m), pack(mom_v), "adamw_small")
    off = 0
    for n, size in zip(SMALL, sizes):
        shape = weights[n].shape
        out_g[n] = g_small[:, off:off + size].reshape(shape)
        out_d[n] = d_s[:, off:off + size].reshape(shape)
        out_m[n] = m_s[:, off:off + size].reshape(shape)
        out_v[n] = v_s[:, off:off + size].reshape(shape)
        off += size

    return (loss, dx0[None], *[out_g[n] for n in WEIGHTS], *[out_d[n] for n in WEIGHTS],
            *[out_m[n] for n in WEIGHTS], *[out_v[n] for n in WEIGHTS])
```

```python
import numpy as np
import jax
import jax.numpy as jnp
from jax import lax
from jax.experimental import pallas as pl
from jax.experimental.pallas import tpu as pltpu

F32 = jnp.float32
BF16 = jnp.bfloat16
MESH = pl.DeviceIdType.MESH
ANY = pl.BlockSpec(memory_space=pl.ANY)
VMEM_SPEC = pl.BlockSpec(memory_space=pltpu.VMEM)
SMEM_SPEC = pl.BlockSpec(memory_space=pltpu.SMEM)

N_DEV = 8
CHUNK = 64
HEAD_DIM = 64
LANES = 128
H_A, KV_A, H_B = 8, 2, 8
BACK_A, BACK_B = 2, 8
REL_CLIP = 128
N_REL = 2 * REL_CLIP + 1
QA, KVA, QB = H_A * HEAD_DIM, KV_A * HEAD_DIM, H_B * HEAD_DIM
D_IN = QA + 2 * KVA + 3 * QB
N_MOD = 9
EPS = 1e-6
NEG_INF = -1e30
QG = 4
QROWS = QG * CHUNK
TPS = 2
SKEW = 1024
ADAM_LR, ADAM_B1, ADAM_B2, ADAM_EPS, ADAM_WD, ADAM_STEP = 0.001, 0.9, 0.999, 1e-08, 0.01, 10
VMEM_LIMIT = 56 * 2 ** 20


def _pick(n, cands):
    for c in cands:
        if n % c == 0:
            return c
    return n


def _params(sem=None):
    return pltpu.CompilerParams(dimension_semantics=sem, vmem_limit_bytes=VMEM_LIMIT)


def _dot_nt(a, b):
    return lax.dot_general(a, b, (((1,), (1,)), ((), ())), preferred_element_type=F32)


def _dot_tn(a, b):
    return lax.dot_general(a, b, (((0,), (0,)), ((), ())), preferred_element_type=F32)


def _dot(a, b):
    return jnp.dot(a, b, preferred_element_type=F32)


def _sigmoid(a):
    return 0.5 * (jnp.tanh(0.5 * a) + 1.0)


def _mesh_pos():
    return lax.axis_index("x"), lax.axis_index("y"), lax.axis_index("c")


def _peer(x, y, c, r):
    px = 1 - x if r & 4 else x
    py = 1 - y if r & 2 else y
    pc = 1 - c if r & 1 else c
    return px, py, pc


class _Carry:
    def __init__(self, ins, out_shapes, scratch, start, finish):
        self.ins, self.out_shapes, self.scratch = list(ins), list(out_shapes), list(scratch)
        self.start, self.finish = start, finish


def _call(body, *, name, grid, in_specs, out_specs, out_shape, args, scratch=(), sem=None, carry=None):
    single = not isinstance(out_shape, (tuple, list))
    out_specs = (out_specs,) if single else tuple(out_specs)
    out_shape = (out_shape,) if single else tuple(out_shape)
    if carry is None:
        res = pl.pallas_call(body, name=name, grid=grid, in_specs=list(in_specs), out_specs=out_specs,
                             out_shape=out_shape, scratch_shapes=list(scratch), compiler_params=_params(sem))(*args)
        return res[0] if single else res
    n_in, n_out, n_s = len(in_specs), len(out_shape), len(scratch)
    ci, co = len(carry.ins), len(carry.out_shapes)

    def wrapped(*refs):
        ins, cins = refs[:n_in], refs[n_in:n_in + ci]
        outs = refs[n_in + ci:n_in + ci + n_out]
        couts = refs[n_in + ci + n_out:n_in + ci + n_out + co]
        scr = refs[n_in + ci + n_out + co:n_in + ci + n_out + co + n_s]
        cscr = refs[n_in + ci + n_out + co + n_s:]
        first, last = None, None
        for ax, n in enumerate(grid):
            f, l = pl.program_id(ax) == 0, pl.program_id(ax) == n - 1
            first = f if first is None else first & f
            last = l if last is None else last & l
        pl.when(first)(lambda: carry.start(cins, couts, cscr))
        body(*ins, *outs, *scr)
        pl.when(last)(lambda: carry.finish(cins, couts, cscr))

    res = pl.pallas_call(
        wrapped, name=name, grid=grid, in_specs=list(in_specs) + [ANY] * ci, out_specs=out_specs + (ANY,) * co,
        out_shape=out_shape + tuple(carry.out_shapes), scratch_shapes=list(scratch) + carry.scratch,
        compiler_params=_params(("arbitrary",) * len(grid)))(*args, *carry.ins)
    main = res[:n_out]
    return (main[0] if single else main), res[n_out:]


def _gather_carry(shards):
    n_w = len(shards)
    rows = [s.shape[0] for s in shards]

    def plan(ins, outs, scr):
        send_sems, recv_sems, local_sems = scr
        x, y, c = _mesh_pos()
        me, sibling = (x, y, c), (x, y, 1 - c)
        chips = [(1 - x, y), (x, 1 - y), (1 - x, 1 - y)]

        def block(w, dev):
            start = pl.multiple_of((4 * dev[0] + 2 * dev[1] + dev[2]) * rows[w], 16)
            return outs[w].at[pl.ds(start, rows[w]), :]

        def copy(w, k, dev, to, src=None):
            return pltpu.make_async_remote_copy(
                src_ref=block(w, dev) if src is None else src, dst_ref=block(w, dev),
                send_sem=send_sems.at[w, k], recv_sem=recv_sems.at[w, k], device_id=to, device_id_type=MESH)

        mine = [pltpu.make_async_copy(ins[w], block(w, me), local_sems.at[w]) for w in range(n_w)]
        first = []
        for j, chip in enumerate(chips):
            first += [copy(w, 1 + j, me, (*chip, c), src=ins[w]) for w in range(n_w)]
        first += [copy(w, 0, me, sibling, src=ins[w]) for w in range(n_w)]
        return c, me, sibling, chips, copy, mine, first

    def start(ins, outs, scr):
        _, _, _, _, _, mine, first = plan(ins, outs, scr)
        for cp in mine + first:
            cp.start()

    def finish(ins, outs, scr):
        c, me, sibling, chips, copy, mine, first = plan(ins, outs, scr)
        passed = []
        for j, chip in enumerate(chips):
            for w in range(n_w):
                copy(w, 1 + j, (*chip, c), me).wait_recv()
                cp = copy(w, 4 + j, (*chip, c), sibling)
                cp.start()
                passed.append(cp)
        for w in range(n_w):
            copy(w, 0, sibling, me).wait_recv()
        for j, chip in enumerate(chips):
            for w in range(n_w):
                copy(w, 4 + j, (*chip, 1 - c), me).wait_recv()
        for cp in first + passed:
            cp.wait_send()
        for cp in mine:
            cp.wait()

    return _Carry(
        shards, [jax.ShapeDtypeStruct((N_DEV * s.shape[0], s.shape[1]), s.dtype) for s in shards],
        [pltpu.SemaphoreType.DMA((n_w, N_DEV - 1)), pltpu.SemaphoreType.DMA((n_w, N_DEV - 1)),
         pltpu.SemaphoreType.DMA((n_w,))], start, finish)


def _scatter_carry(parts):
    n_w = len(parts)
    n_chip = N_DEV // 2
    rows = [g.shape[0] // n_chip for g in parts]

    def plan(ins, outs, scr):
        send_sems, recv_sems, local_sems = scr
        x, y, c = _mesh_pos()

        def src(w, chip_index):
            return ins[w].at[pl.ds(pl.multiple_of(chip_index * rows[w], 16), rows[w]), :]

        mine = [pltpu.make_async_copy(src(w, 2 * x + y), outs[w].at[0], local_sems.at[w]) for w in range(n_w)]
        copies = []
        for r in (3, 2, 1):
            px, py, _ = _peer(x, y, c, 2 * r)
            for w in range(n_w):
                copies.append(pltpu.make_async_remote_copy(
                    src_ref=src(w, 2 * px + py), dst_ref=outs[w].at[r], send_sem=send_sems.at[w, r - 1],
                    recv_sem=recv_sems.at[w, r - 1], device_id=(px, py, c), device_id_type=MESH))
        return mine, copies

    def start(ins, outs, scr):
        mine, copies = plan(ins, outs, scr)
        for cp in mine + copies:
            cp.start()

    def finish(ins, outs, scr):
        mine, copies = plan(ins, outs, scr)
        for cp in copies:
            cp.wait_recv()
        for cp in copies:
            cp.wait_send()
        for cp in mine:
            cp.wait()

    return _Carry(
        parts, [jax.ShapeDtypeStruct((n_chip, r, g.shape[1]), g.dtype) for r, g in zip(rows, parts)],
        [pltpu.SemaphoreType.DMA((n_w, n_chip - 1)), pltpu.SemaphoreType.DMA((n_w, n_chip - 1)),
         pltpu.SemaphoreType.DMA((n_w,))], start, finish)


def _ada_forward(c_row, w_ada, b_cols, carry):
    d = c_row.shape[1]
    wcols = w_ada.shape[1]
    ci, co = len(carry.ins), len(carry.out_shapes)

    def body(*refs):
        c_ref, w_ref, b_ref = refs[:3]
        cins = refs[3:3 + ci]
        sc_ref, mod_ref = refs[3 + ci:5 + ci]
        couts = refs[5 + ci:5 + ci + co]
        rows_ref, send_sems, recv_sems = refs[5 + ci + co:8 + ci + co]
        cscr = refs[8 + ci + co:]
        carry.start(cins, couts, cscr)
        x, y, c = _mesh_pos()
        me = 4 * x + 2 * y + c
        cv = c_ref[...]
        sc_ref[me] = cv * _sigmoid(cv)

        sends = []
        for r in range(1, N_DEV):
            px, py, pc = _peer(x, y, c, r)
            cp = pltpu.make_async_remote_copy(
                src_ref=sc_ref.at[me], dst_ref=sc_ref.at[me], send_sem=send_sems.at[0, r - 1],
                recv_sem=recv_sems.at[0, r - 1], device_id=(px, py, pc), device_id_type=MESH)
            cp.start()
            sends.append(cp)
        for r in range(1, N_DEV):
            px, py, pc = _peer(x, y, c, r)
            pid = 4 * px + 2 * py + pc
            pltpu.make_async_remote_copy(
                src_ref=sc_ref.at[pid], dst_ref=sc_ref.at[pid], send_sem=send_sems.at[0, r - 1],
                recv_sem=recv_sems.at[0, r - 1], device_id=(px, py, pc), device_id_type=MESH).wait_recv()
        for cp in sends:
            cp.wait_send()

        sc_all = jnp.concatenate([sc_ref[j] for j in range(N_DEV)], axis=0)
        rows = _dot(sc_all.astype(BF16), w_ref[...].astype(BF16)) + b_ref[...]
        for j in range(N_DEV):
            rows_ref[j] = rows[j:j + 1, :]
        mod_ref[me] = rows_ref[me]

        sends = []
        for r in range(1, N_DEV):
            px, py, pc = _peer(x, y, c, r)
            pid = 4 * px + 2 * py + pc
            cp = pltpu.make_async_remote_copy(
                src_ref=rows_ref.at[pid], dst_ref=mod_ref.at[me], send_sem=send_sems.at[1, r - 1],
                recv_sem=recv_sems.at[1, r - 1], device_id=(px, py, pc), device_id_type=MESH)
            cp.start()
            sends.append(cp)
        for r in range(1, N_DEV):
            px, py, pc = _peer(x, y, c, r)
            pid = 4 * px + 2 * py + pc
            pltpu.make_async_remote_copy(
                src_ref=rows_ref.at[pid], dst_ref=mod_ref.at[pid], send_sem=send_sems.at[1, r - 1],
                recv_sem=recv_sems.at[1, r - 1], device_id=(px, py, pc), device_id_type=MESH).wait_recv()
        for cp in sends:
            cp.wait_send()
        carry.finish(cins, couts, cscr)

    res = pl.pallas_call(
        body, name="ada_forward",
        out_shape=(jax.ShapeDtypeStruct((N_DEV, 1, d), F32), jax.ShapeDtypeStruct((N_DEV, 1, wcols), F32),
                   *carry.out_shapes),
        in_specs=[VMEM_SPEC, VMEM_SPEC, VMEM_SPEC] + [ANY] * ci, out_specs=(VMEM_SPEC, VMEM_SPEC) + (ANY,) * co,
        scratch_shapes=[pltpu.VMEM((N_DEV, 1, wcols), F32), pltpu.SemaphoreType.DMA((2, N_DEV - 1)),
                        pltpu.SemaphoreType.DMA((2, N_DEV - 1))] + carry.scratch,
        compiler_params=_params(),
    )(c_row, w_ada, b_cols, *carry.ins)
    return res[:2], res[2:]


def _all_gather_small(v):
    n = v.shape[1]

    def body(v_ref, out_ref, send_sems, recv_sems):
        x, y, c = _mesh_pos()
        me = 4 * x + 2 * y + c
        out_ref[me] = v_ref[...]
        sends = []
        for r in range(1, N_DEV):
            px, py, pc = _peer(x, y, c, r)
            cp = pltpu.make_async_remote_copy(
                src_ref=v_ref, dst_ref=out_ref.at[me], send_sem=send_sems.at[r - 1],
                recv_sem=recv_sems.at[r - 1], device_id=(px, py, pc), device_id_type=MESH)
            cp.start()
            sends.append(cp)
        for r in range(1, N_DEV):
            px, py, pc = _peer(x, y, c, r)
            pid = 4 * px + 2 * py + pc
            pltpu.make_async_remote_copy(
                src_ref=v_ref, dst_ref=out_ref.at[pid], send_sem=send_sems.at[r - 1],
                recv_sem=recv_sems.at[r - 1], device_id=(px, py, pc), device_id_type=MESH).wait_recv()
        for cp in sends:
            cp.wait_send()

    return pl.pallas_call(
        body, name="all_gather_small",
        out_shape=jax.ShapeDtypeStruct((N_DEV, 1, n), F32),
        in_specs=[VMEM_SPEC], out_specs=VMEM_SPEC,
        scratch_shapes=[pltpu.SemaphoreType.DMA((N_DEV - 1,)), pltpu.SemaphoreType.DMA((N_DEV - 1,))],
        compiler_params=_params(),
    )(v)


def _mm_nt(a, b, name, out_dtype, bias=None, carry=None):
    m, k = a.shape
    n = b.shape[0]
    tm = _pick(m, (512, 256, 128))
    tn = _pick(n, (1408, 1152, 1024, 768, 512, 256, 128))

    def body(*refs):
        acc = _dot_nt(refs[0][...], refs[1][...])
        if bias is not None:
            acc = acc + refs[2][...]
        refs[-1][...] = acc.astype(out_dtype)

    in_specs = [pl.BlockSpec((tm, k), lambda i, j: (i, 0)), pl.BlockSpec((tn, k), lambda i, j: (j, 0))]
    args = [a, b]
    if bias is not None:
        in_specs.append(pl.BlockSpec((1, tn), lambda i, j: (0, j)))
        args.append(bias)
    return _call(body, name=name, grid=(m // tm, n // tn), in_specs=in_specs,
                 out_specs=pl.BlockSpec((tm, tn), lambda i, j: (i, j)),
                 out_shape=jax.ShapeDtypeStruct((m, n), out_dtype), args=args,
                 sem=("parallel", "parallel"), carry=carry)


def _mm_nn(pairs, name, out_dtype, bias=None, carry=None):
    m, k = pairs[0][0].shape
    n = pairs[0][1].shape[1]
    tm = _pick(m, (512, 256, 128))
    tk = _pick(k, (1408, 1152, 1024, 768, 512, 256, 128))
    nk = k // tk
    n_p = len(pairs)

    def body(*refs):
        o_ref, acc_ref = refs[-2], refs[-1]
        kk = pl.program_id(1)

        @pl.when(kk == 0)
        def _():
            acc_ref[...] = jnp.zeros_like(acc_ref)

        for p in range(n_p):
            acc_ref[...] += _dot(refs[2 * p][...], refs[2 * p + 1][...])

        @pl.when(kk == nk - 1)
        def _():
            acc = acc_ref[...]
            if bias is not None:
                acc = acc + refs[2 * n_p][...]
            o_ref[...] = acc.astype(out_dtype)

    in_specs, args = [], []
    for a, b in pairs:
        in_specs += [pl.BlockSpec((tm, tk), lambda i, kk: (i, kk)), pl.BlockSpec((tk, n), lambda i, kk: (kk, 0))]
        args += [a, b]
    if bias is not None:
        in_specs.append(pl.BlockSpec((1, n), lambda i, kk: (0, 0)))
        args.append(bias)
    return _call(body, name=name, grid=(m // tm, nk), in_specs=in_specs,
                 out_specs=pl.BlockSpec((tm, n), lambda i, kk: (i, 0)),
                 out_shape=jax.ShapeDtypeStruct((m, n), out_dtype), args=args,
                 scratch=[pltpu.VMEM((tm, n), F32)], sem=("parallel", "arbitrary"), carry=carry)


def _mm_tn(a, b, name, out_dtype=BF16, carry=None):
    k, m = a.shape
    n = b.shape[1]
    tm = _pick(m, (1408, 1152, 1024, 768, 512, 256, 128))
    tk = _pick(k, (512, 256, 128))
    nk = k // tk

    def body(a_ref, b_ref, o_ref, acc_ref):
        kk = pl.program_id(1)

        @pl.when(kk == 0)
        def _():
            acc_ref[...] = jnp.zeros_like(acc_ref)

        acc_ref[...] += _dot_tn(a_ref[...], b_ref[...])

        @pl.when(kk == nk - 1)
        def _():
            o_ref[...] = acc_ref[...].astype(out_dtype)

    return _call(body, name=name, grid=(m // tm, nk),
                 in_specs=[pl.BlockSpec((tk, tm), lambda i, kk: (kk, i)), pl.BlockSpec((tk, n), lambda i, kk: (kk, 0))],
                 out_specs=pl.BlockSpec((tm, n), lambda i, kk: (i, 0)),
                 out_shape=jax.ShapeDtypeStruct((m, n), out_dtype), args=[a, b],
                 scratch=[pltpu.VMEM((tm, n), F32)], sem=("parallel", "arbitrary"), carry=carry)


def _mm_tn_pair(a, b, name, carry=None):
    k, m = a.shape
    n = b.shape[1]
    rows = m // N_DEV
    n_chip = N_DEV // 2
    tm = 4 * rows
    tk = _pick(k, (512, 256, 128))
    nk = k // tk

    def body(a_ref, b_ref, p_ref, acc_ref, keep_ref, send_ref, land_ref, send_sems, recv_sems):
        i, kk = pl.program_id(0), pl.program_id(1)
        x, y, c = _mesh_pos()

        def push(chip):
            return pltpu.make_async_remote_copy(
                src_ref=send_ref.at[chip], dst_ref=land_ref.at[chip], send_sem=send_sems.at[chip],
                recv_sem=recv_sems.at[chip], device_id=(x, y, 1 - c), device_id_type=MESH)

        @pl.when(kk == 0)
        def _():
            acc_ref[...] = jnp.zeros_like(acc_ref)

        acc_ref[...] += _dot_tn(a_ref[...], b_ref[...])

        for t in range(2):
            @pl.when((kk == nk - 1) & (i == t))
            def _(t=t):
                for ob in range(4):
                    chip, core = 2 * t + ob // 2, ob % 2
                    blk = acc_ref[ob * rows:(ob + 1) * rows, :]

                    @pl.when(c == core)
                    def _(chip=chip, blk=blk):
                        keep_ref[chip] = blk

                    @pl.when(c != core)
                    def _(chip=chip, blk=blk):
                        send_ref[chip] = blk.astype(BF16)
                        push(chip).start()

        @pl.when((kk == nk - 1) & (i == 1))
        def _():
            for chip in range(n_chip):
                push(chip).wait_recv()
                p_ref[chip * rows:(chip + 1) * rows, :] = (
                    keep_ref[chip] + land_ref[chip].astype(F32)).astype(BF16)
            for chip in range(n_chip):
                push(chip).wait_send()

    return _call(body, name=name, grid=(2, nk),
                 in_specs=[pl.BlockSpec((tk, tm), lambda i, kk: (kk, i)), pl.BlockSpec((tk, n), lambda i, kk: (kk, 0))],
                 out_specs=pl.BlockSpec((n_chip * rows, n), lambda i, kk: (0, 0)),
                 out_shape=jax.ShapeDtypeStruct((n_chip * rows, n), BF16), args=[a, b],
                 scratch=[pltpu.VMEM((tm, n), F32), pltpu.VMEM((n_chip, rows, n), F32),
                          pltpu.VMEM((n_chip, rows, n), BF16), pltpu.VMEM((n_chip, rows, n), BF16),
                          pltpu.SemaphoreType.DMA((n_chip,)), pltpu.SemaphoreType.DMA((n_chip,))],
                 sem=("arbitrary", "arbitrary"), carry=carry)


def _ffn_up(h, wg_t, wu_t, name, carry=None):
    s, d = h.shape
    f = wg_t.shape[0]
    tm = _pick(s, (512, 256, 128))
    tf = _pick(f, (1408, 1024, 512, 256, 128))

    def body(h_ref, wg_ref, wu_ref, a_ref, b_ref, u_ref):
        hh = h_ref[...]
        a = _dot_nt(hh, wg_ref[...])
        b = _dot_nt(hh, wu_ref[...])
        a_ref[...] = a.astype(BF16)
        b_ref[...] = b.astype(BF16)
        u_ref[...] = ((a * _sigmoid(a)) * b).astype(BF16)

    w_spec = pl.BlockSpec((tf, d), lambda i, j: (j, 0))
    o_spec = pl.BlockSpec((tm, tf), lambda i, j: (i, j))
    o_shape = jax.ShapeDtypeStruct((s, f), BF16)
    return _call(body, name=name, grid=(s // tm, f // tf),
                 in_specs=[pl.BlockSpec((tm, d), lambda i, j: (i, 0)), w_spec, w_spec],
                 out_specs=(o_spec, o_spec, o_spec), out_shape=(o_shape, o_shape, o_shape),
                 args=[h, wg_t, wu_t], sem=("parallel", "parallel"), carry=carry)


def _ffn_down_bwd(dy, wd, a, b, name, carry=None):
    s, d = dy.shape
    f = wd.shape[0]
    tm = _pick(s, (512, 256, 128))
    tf = _pick(f, (1408, 1024, 512, 256, 128))

    def body(dy_ref, wd_ref, a_ref, b_ref, da_ref, db_ref):
        du = _dot_nt(dy_ref[...], wd_ref[...])
        a = a_ref[...].astype(F32)
        b = b_ref[...].astype(F32)
        sig = _sigmoid(a)
        da_ref[...] = (du * b * (sig * (1.0 + a * (1.0 - sig)))).astype(BF16)
        db_ref[...] = (du * (a * sig)).astype(BF16)

    t_spec = pl.BlockSpec((tm, tf), lambda i, j: (i, j))
    o_shape = jax.ShapeDtypeStruct((s, f), BF16)
    return _call(body, name=name, grid=(s // tm, f // tf),
                 in_specs=[pl.BlockSpec((tm, d), lambda i, j: (i, 0)), pl.BlockSpec((tf, d), lambda i, j: (j, 0)),
                           t_spec, t_spec],
                 out_specs=(t_spec, t_spec), out_shape=(o_shape, o_shape), args=[dy, wd, a, b],
                 sem=("parallel", "parallel"), carry=carry)


def _row_tile(s):
    return _pick(s, (256, 128, 64))


def _vec_spec(d):
    return pl.BlockSpec((1, d), lambda i: (0, 0))


def _pre_norm(x, g, scale, shift, name):
    s, d = x.shape
    ts = _row_tile(s)

    def body(x_ref, g_ref, sc_ref, sh_ref, h_ref):
        xv = x_ref[...]
        r = lax.rsqrt(jnp.mean(xv * xv, axis=-1, keepdims=True) + EPS)
        h_ref[...] = (((xv * r) * g_ref[...]) * (1.0 + sc_ref[...]) + sh_ref[...]).astype(BF16)

    row = pl.BlockSpec((ts, d), lambda i: (i, 0))
    return _call(body, name=name, grid=(s // ts,), in_specs=[row, _vec_spec(d), _vec_spec(d), _vec_spec(d)],
                 out_specs=row, out_shape=jax.ShapeDtypeStruct((s, d), BF16), args=[x, g, scale, shift],
                 sem=("parallel",))


def _post_norm_residual(x, y, g, gate, weight, name):
    s, d = x.shape
    ts = _row_tile(s)

    def body(x_ref, y_ref, g_ref, gate_ref, o_ref):
        yv = y_ref[...]
        r = lax.rsqrt(jnp.mean(yv * yv, axis=-1, keepdims=True) + EPS)
        o_ref[...] = x_ref[...] + (weight * gate_ref[...]) * ((yv * r) * g_ref[...])

    row = pl.BlockSpec((ts, d), lambda i: (i, 0))
    return _call(body, name=name, grid=(s // ts,), in_specs=[row, row, _vec_spec(d), _vec_spec(d)],
                 out_specs=row, out_shape=jax.ShapeDtypeStruct((s, d), F32), args=[x, y, g, gate],
                 sem=("parallel",))


def _post_norm_bwd(dout, y, g, gate, weight, name):
    s, d = y.shape
    ts = _row_tile(s)

    def body(do_ref, y_ref, g_ref, gate_ref, dy_ref, s1_ref, cs_ref):
        @pl.when(pl.program_id(0) == 0)
        def _():
            s1_ref[...] = jnp.zeros_like(s1_ref)
            cs_ref[...] = jnp.zeros_like(cs_ref)

        yv = y_ref[...]
        do = do_ref[...]
        r = lax.rsqrt(jnp.mean(yv * yv, axis=-1, keepdims=True) + EPS)
        yn = yv * r
        dyn = do * ((weight * gate_ref[...]) * g_ref[...])
        dy = r * (dyn - yn * jnp.mean(dyn * yn, axis=-1, keepdims=True))
        dy_ref[...] = dy.astype(BF16)
        s1_ref[...] += jnp.sum(do * yn, axis=0, keepdims=True)
        cs_ref[...] += jnp.sum(dy, axis=0, keepdims=True)

    row = pl.BlockSpec((ts, d), lambda i: (i, 0))
    vec = jax.ShapeDtypeStruct((1, d), F32)
    return _call(body, name=name, grid=(s // ts,), in_specs=[row, row, _vec_spec(d), _vec_spec(d)],
                 out_specs=(row, _vec_spec(d), _vec_spec(d)),
                 out_shape=(jax.ShapeDtypeStruct((s, d), BF16), vec, vec), args=[dout, y, g, gate],
                 sem=("arbitrary",))


def _pre_norm_bwd(dh, x, g, scale, dres, name):
    s, d = x.shape
    ts = _row_tile(s)

    def body(dh_ref, x_ref, g_ref, sc_ref, dr_ref, dx_ref, s2_ref, s3_ref):
        @pl.when(pl.program_id(0) == 0)
        def _():
            s2_ref[...] = jnp.zeros_like(s2_ref)
            s3_ref[...] = jnp.zeros_like(s3_ref)

        xv = x_ref[...]
        dh = dh_ref[...]
        r = lax.rsqrt(jnp.mean(xv * xv, axis=-1, keepdims=True) + EPS)
        n = xv * r
        dn = dh * (g_ref[...] * (1.0 + sc_ref[...]))
        dx_ref[...] = dr_ref[...] + r * (dn - n * jnp.mean(dn * n, axis=-1, keepdims=True))
        s2_ref[...] += jnp.sum(dh * n, axis=0, keepdims=True)
        s3_ref[...] += jnp.sum(dh, axis=0, keepdims=True)

    row = pl.BlockSpec((ts, d), lambda i: (i, 0))
    vec = jax.ShapeDtypeStruct((1, d), F32)
    return _call(body, name=name, grid=(s // ts,), in_specs=[row, row, _vec_spec(d), _vec_spec(d), row],
                 out_specs=(row, _vec_spec(d), _vec_spec(d)),
                 out_shape=(jax.ShapeDtypeStruct((s, d), F32), vec, vec), args=[dh, x, g, scale, dres],
                 sem=("arbitrary",))


def _post_pre_norm(x, y, g_post, gate, weight, g_pre, scale, shift, name):
    s, d = x.shape
    ts = _row_tile(s)

    def body(x_ref, y_ref, gp_ref, gate_ref, g_ref, sc_ref, sh_ref, o_ref, h_ref):
        yv = y_ref[...]
        r = lax.rsqrt(jnp.mean(yv * yv, axis=-1, keepdims=True) + EPS)
        xv = x_ref[...] + (weight * gate_ref[...]) * ((yv * r) * gp_ref[...])
        o_ref[...] = xv
        r2 = lax.rsqrt(jnp.mean(xv * xv, axis=-1, keepdims=True) + EPS)
        h_ref[...] = (((xv * r2) * g_ref[...]) * (1.0 + sc_ref[...]) + sh_ref[...]).astype(BF16)

    row = pl.BlockSpec((ts, d), lambda i: (i, 0))
    return _call(body, name=name, grid=(s // ts,), in_specs=[row, row] + [_vec_spec(d)] * 5,
                 out_specs=(row, row),
                 out_shape=(jax.ShapeDtypeStruct((s, d), F32), jax.ShapeDtypeStruct((s, d), BF16)),
                 args=[x, y, g_post, gate, g_pre, scale, shift], sem=("parallel",))


def _post_norm_loss_bwd(x, y, g, gate, weight, target, name):
    s, d = y.shape
    ts = _row_tile(s)

    def body(x_ref, y_ref, g_ref, gate_ref, t_ref, dx_ref, dy_ref, l_ref, s1_ref):
        @pl.when(pl.program_id(0) == 0)
        def _():
            l_ref[...] = jnp.zeros_like(l_ref)
            s1_ref[...] = jnp.zeros_like(s1_ref)

        yv = y_ref[...]
        r = lax.rsqrt(jnp.mean(yv * yv, axis=-1, keepdims=True) + EPS)
        yn = yv * r
        err = (x_ref[...] + (weight * gate_ref[...]) * (yn * g_ref[...])) - t_ref[...]
        do = err * (1.0 / d)
        dx_ref[...] = do
        l_ref[...] += 0.5 * jnp.sum(jnp.mean(err * err, axis=-1, keepdims=True), axis=0, keepdims=True)
        dyn = do * ((weight * gate_ref[...]) * g_ref[...])
        dy_ref[...] = (r * (dyn - yn * jnp.mean(dyn * yn, axis=-1, keepdims=True))).astype(BF16)
        s1_ref[...] += jnp.sum(do * yn, axis=0, keepdims=True)

    row = pl.BlockSpec((ts, d), lambda i: (i, 0))
    return _call(body, name=name, grid=(s // ts,), in_specs=[row, row, _vec_spec(d), _vec_spec(d), row],
                 out_specs=(row, row, pl.BlockSpec((1, 1), lambda i: (0, 0)), _vec_spec(d)),
                 out_shape=(jax.ShapeDtypeStruct((s, d), F32), jax.ShapeDtypeStruct((s, d), BF16),
                            jax.ShapeDtypeStruct((1, 1), F32), jax.ShapeDtypeStruct((1, d), F32)),
                 args=[x, y, g, gate, target], sem=("arbitrary",))


def _pre_post_norm_bwd(dh, x, g_pre, scale, dres, y, g_post, gate, weight, name):
    s, d = x.shape
    ts = _row_tile(s)

    def body(dh_ref, x_ref, g_ref, sc_ref, dr_ref, y_ref, gp_ref, gate_ref,
             dx_ref, dy_ref, s2_ref, s3_ref, s1_ref, cs_ref):
        @pl.when(pl.program_id(0) == 0)
        def _():
            for ref in (s2_ref, s3_ref, s1_ref, cs_ref):
                ref[...] = jnp.zeros_like(ref)

        xv = x_ref[...]
        dh = dh_ref[...]
        r = lax.rsqrt(jnp.mean(xv * xv, axis=-1, keepdims=True) + EPS)
        n = xv * r
        dn = dh * (g_ref[...] * (1.0 + sc_ref[...]))
        dx = dr_ref[...] + r * (dn - n * jnp.mean(dn * n, axis=-1, keepdims=True))
        dx_ref[...] = dx
        s2_ref[...] += jnp.sum(dh * n, axis=0, keepdims=True)
        s3_ref[...] += jnp.sum(dh, axis=0, keepdims=True)
        yv = y_ref[...]
        ry = lax.rsqrt(jnp.mean(yv * yv, axis=-1, keepdims=True) + EPS)
        yn = yv * ry
        dyn = dx * ((weight * gate_ref[...]) * gp_ref[...])
        dy = ry * (dyn - yn * jnp.mean(dyn * yn, axis=-1, keepdims=True))
        dy_ref[...] = dy.astype(BF16)
        s1_ref[...] += jnp.sum(dx * yn, axis=0, keepdims=True)
        cs_ref[...] += jnp.sum(dy, axis=0, keepdims=True)

    row = pl.BlockSpec((ts, d), lambda i: (i, 0))
    vec = jax.ShapeDtypeStruct((1, d), F32)
    return _call(body, name=name, grid=(s // ts,),
                 in_specs=[row, row, _vec_spec(d), _vec_spec(d), row, row, _vec_spec(d), _vec_spec(d)],
                 out_specs=(row, row) + (_vec_spec(d),) * 4,
                 out_shape=(jax.ShapeDtypeStruct((s, d), F32), jax.ShapeDtypeStruct((s, d), BF16), vec, vec, vec, vec),
                 args=[dh, x, g_pre, scale, dres, y, g_post, gate], sem=("arbitrary",))


def _group_norm_cat(oa, ob, ga, gb):
    s = oa.shape[0]
    ts = _row_tile(s)

    def body(oa_ref, ob_ref, ga_ref, gb_ref, y_ref):
        for o_ref, g_ref, lo, w in ((oa_ref, ga_ref, 0, QA), (ob_ref, gb_ref, QA, QB)):
            ov = o_ref[...]
            r = lax.rsqrt(jnp.mean(ov * ov, axis=-1, keepdims=True) + EPS)
            y_ref[:, lo:lo + w] = ((ov * r) * g_ref[...]).astype(BF16)

    return _call(body, name="group_norm_cat", grid=(s // ts,),
                 in_specs=[pl.BlockSpec((ts, QA), lambda i: (i, 0)), pl.BlockSpec((ts, QB), lambda i: (i, 0)),
                           _vec_spec(QA), _vec_spec(QB)],
                 out_specs=pl.BlockSpec((ts, QA + QB), lambda i: (i, 0)),
                 out_shape=jax.ShapeDtypeStruct((s, QA + QB), BF16), args=[oa, ob, ga, gb], sem=("parallel",))


def _group_norm_bwd(dy, oa, ob, ga, gb):
    s = oa.shape[0]
    ts = _row_tile(s)

    def body(dy_ref, oa_ref, ob_ref, ga_ref, gb_ref, doa_ref, dob_ref, dga_ref, dgb_ref):
        @pl.when(pl.program_id(0) == 0)
        def _():
            dga_ref[...] = jnp.zeros_like(dga_ref)
            dgb_ref[...] = jnp.zeros_like(dgb_ref)

        for o_ref, g_ref, do_ref, dg_ref, lo, w in ((oa_ref, ga_ref, doa_ref, dga_ref, 0, QA),
                                                    (ob_ref, gb_ref, dob_ref, dgb_ref, QA, QB)):
            ov = o_ref[...]
            dyv = dy_ref[:, lo:lo + w]
            r = lax.rsqrt(jnp.mean(ov * ov, axis=-1, keepdims=True) + EPS)
            n = ov * r
            dn = dyv * g_ref[...]
            do_ref[...] = r * (dn - n * jnp.mean(dn * n, axis=-1, keepdims=True))
            dg_ref[...] += jnp.sum(dyv * n, axis=0, keepdims=True)

    ra = pl.BlockSpec((ts, QA), lambda i: (i, 0))
    rb = pl.BlockSpec((ts, QB), lambda i: (i, 0))
    return _call(body, name="group_norm_bwd", grid=(s // ts,),
                 in_specs=[pl.BlockSpec((ts, QA + QB), lambda i: (i, 0)), ra, rb, _vec_spec(QA), _vec_spec(QB)],
                 out_specs=(ra, rb, _vec_spec(QA), _vec_spec(QB)),
                 out_shape=(jax.ShapeDtypeStruct((s, QA), F32), jax.ShapeDtypeStruct((s, QB), F32),
                            jax.ShapeDtypeStruct((1, QA), F32), jax.ShapeDtypeStruct((1, QB), F32)),
                 args=[dy, oa, ob, ga, gb], sem=("arbitrary",))


def _loss_and_grad(y, target):
    s, d = y.shape
    ts = _row_tile(s)

    def body(y_ref, t_ref, l_ref, g_ref):
        @pl.when(pl.program_id(0) == 0)
        def _():
            l_ref[...] = jnp.zeros_like(l_ref)

        err = y_ref[...] - t_ref[...]
        g_ref[...] = err * (1.0 / d)
        row = jnp.mean(err * err, axis=-1, keepdims=True)
        l_ref[...] += 0.5 * jnp.sum(row, axis=0, keepdims=True)

    row = pl.BlockSpec((ts, d), lambda i: (i, 0))
    return _call(body, name="loss_and_grad", grid=(s // ts,), in_specs=[row, row],
                 out_specs=(pl.BlockSpec((1, 1), lambda i: (0, 0)), row),
                 out_shape=(jax.ShapeDtypeStruct((1, 1), F32), jax.ShapeDtypeStruct((s, d), F32)),
                 args=[y, target], sem=("arbitrary",))


def _col_sum(x, name):
    s, n = x.shape
    ts = _row_tile(s)

    def body(x_ref, o_ref):
        @pl.when(pl.program_id(0) == 0)
        def _():
            o_ref[...] = jnp.zeros_like(o_ref)

        o_ref[...] += jnp.sum(x_ref[...].astype(F32), axis=0, keepdims=True)

    return _call(body, name=name, grid=(s // ts,), in_specs=[pl.BlockSpec((ts, n), lambda i: (i, 0))],
                 out_specs=pl.BlockSpec((1, n), lambda i: (0, 0)), out_shape=jax.ShapeDtypeStruct((1, n), F32),
                 args=[x], sem=("arbitrary",))


def _n_variants(n_back):
    return -(-n_back // QG) + 1


def _alibi_bias():
    i = np.arange(QROWS)[:, None]
    j = np.arange((QG + BACK_A) * CHUNK)[None, :]
    dist = np.abs(BACK_A * CHUNK + i - j).astype(np.float32)
    dc = j // CHUNK - i // CHUNK
    valid = (dc >= 0) & (dc <= BACK_A)
    slopes = np.array([2.0 ** (-8.0 * (h + 1) / H_A) for h in range(H_A)], dtype=np.float32)
    bias = -slopes[:, None, None] * dist[None]
    out = [np.where((valid & (j >= (BACK_A - QG * v) * CHUNK))[None], bias, np.float32(NEG_INF))
           for v in range(_n_variants(BACK_A))]
    return jnp.asarray(np.stack(out).astype(np.float32))


def _rel_index_matrix():
    cc = np.arange(SKEW)
    dist = np.where(cc < SKEW - QROWS, BACK_B * CHUNK - cc, BACK_B * CHUNK + SKEW - cc)
    idx = np.clip(dist, -REL_CLIP, REL_CLIP) + REL_CLIP
    m = np.zeros((SKEW, N_REL), np.float32)
    m[cc, idx] = 1.0
    return jnp.asarray(m)


def _toeplitz_bias(vec, carry=None):
    lk = (QG + BACK_B) * CHUNK
    nv = _n_variants(BACK_B)

    def body(v_ref, o_ref):
        xv = jnp.broadcast_to(v_ref[0], (QROWS, SKEW))
        row = lax.broadcasted_iota(jnp.int32, (QROWS, SKEW), 0)
        for bit in range(QROWS.bit_length() - 1):
            xv = jnp.where((row >> bit) & 1 == 1, pltpu.roll(xv, 1 << bit, 1), xv)
        ri = lax.broadcasted_iota(jnp.int32, (QROWS, lk), 0) // CHUNK
        col = lax.broadcasted_iota(jnp.int32, (QROWS, lk), 1)
        ci = col // CHUNK
        valid = (ci - ri >= 0) & (ci - ri <= BACK_B)
        for v in range(nv):
            o_ref[v, 0] = jnp.where(valid & (col >= (BACK_B - QG * v) * CHUNK), xv[:, :lk], NEG_INF)

    return _call(body, name="toeplitz_bias", grid=(H_B,),
                 in_specs=[pl.BlockSpec((1, 1, SKEW), lambda h: (h, 0, 0))],
                 out_specs=pl.BlockSpec((nv, 1, QROWS, lk), lambda h: (0, h, 0, 0)),
                 out_shape=jax.ShapeDtypeStruct((nv, H_B, QROWS, lk), F32), args=[vec], sem=("parallel",),
                 carry=carry)


def _diagonal_sums(dbias):
    lk = dbias.shape[2]

    def body(d_ref, o_ref):
        xp = jnp.concatenate([d_ref[0], jnp.zeros((QROWS, SKEW - lk), F32)], axis=1)
        xv = xp[0:CHUNK]
        for q in range(1, QG):
            xv = xv + pltpu.roll(xp[q * CHUNK:(q + 1) * CHUNK], SKEW - q * CHUNK, 1)
        row = lax.broadcasted_iota(jnp.int32, (CHUNK, SKEW), 0)
        for bit in range(CHUNK.bit_length() - 1):
            xv = jnp.where((row >> bit) & 1 == 1, pltpu.roll(xv, SKEW - (1 << bit), 1), xv)
        o_ref[0] = jnp.sum(xv, axis=0, keepdims=True)

    return _call(body, name="diagonal_sums", grid=(H_B,),
                 in_specs=[pl.BlockSpec((1, QROWS, lk), lambda h: (h, 0, 0))],
                 out_specs=pl.BlockSpec((1, 1, SKEW), lambda h: (h, 0, 0)),
                 out_shape=jax.ShapeDtypeStruct((H_B, 1, SKEW), F32), args=[dbias], sem=("parallel",))


def _attn_common(s, n_back, gqa, q_col, k_col, v_col):
    lk = (QG + n_back) * CHUNK
    pad = n_back * CHUNK
    wide = TPS * LANES
    q_spec = pl.BlockSpec((QROWS, wide), lambda t, g: (g, q_col // TPS + t))
    if gqa:
        k_spec = pl.BlockSpec((s, LANES), lambda t, g: (0, k_col))
        v_spec = pl.BlockSpec((s, LANES), lambda t, g: (0, v_col))
    else:
        k_spec = pl.BlockSpec((s, wide), lambda t, g: (0, k_col // TPS + t))
        v_spec = pl.BlockSpec((s, wide), lambda t, g: (0, v_col // TPS + t))
    last_variant = _n_variants(n_back) - 1
    bias_spec = pl.BlockSpec((None, 2 * TPS, QROWS, lk), lambda t, g: (jnp.minimum(g, last_variant), t, 0, 0))
    tile_spec = pl.BlockSpec((QROWS, wide), lambda t, g: (g, t))
    return lk, pad, q_spec, k_spec, v_spec, bias_spec, tile_spec


def _attention_fwd(proj, bias, sinks, *, n_back, gqa, q_col, k_col, v_col, name, carry=None):
    s = proj.shape[0]
    lk, pad, q_spec, k_spec, v_spec, bias_spec, tile_spec = _attn_common(s, n_back, gqa, q_col, k_col, v_col)
    n_t, n_g = 512 // (TPS * LANES), s // QROWS
    kv_wide = LANES if gqa else TPS * LANES

    def body(*refs):
        if gqa:
            q_ref, k_ref, v_ref, bias_ref, sink_ref, o_ref, l_ref, kpad, vpad = refs
        else:
            q_ref, k_ref, v_ref, bias_ref, o_ref, l_ref, kpad, vpad = refs
        t, g = pl.program_id(0), pl.program_id(1)

        @pl.when(g == 0)
        def _():
            kpad[0:pad, :] = jnp.zeros((pad, kv_wide), BF16)
            vpad[0:pad, :] = jnp.zeros((pad, kv_wide), BF16)
            kpad[pad:, :] = k_ref[...]
            vpad[pad:, :] = v_ref[...]

        start = pl.multiple_of(g * QROWS, QROWS)
        half = lax.broadcasted_iota(jnp.int32, (QROWS, LANES), 1) // HEAD_DIM
        for tt in range(TPS):
            lanes = slice(tt * LANES, (tt + 1) * LANES)
            kv_lanes = slice(0, LANES) if gqa else lanes
            kb = kpad[pl.ds(start, lk), kv_lanes]
            vb = vpad[pl.ds(start, lk), kv_lanes]
            q = q_ref[:, lanes] * (HEAD_DIM ** -0.5)
            if gqa:
                hk = (TPS * t + tt) // 2
                q_rolled = pltpu.roll(q.astype(F32), HEAD_DIM, 1).astype(BF16)
            outs, lses = [], []
            for e in range(2):
                if gqa:
                    kv_half = hk
                    src = jnp.where(hk == e, q, q_rolled)
                else:
                    kv_half = e
                    src = q
                qm = jnp.where(half == kv_half, src, jnp.zeros_like(src))
                sc = _dot_nt(qm, kb) + bias_ref[2 * tt + e]
                m = jnp.max(sc, axis=-1, keepdims=True)
                if gqa:
                    sk = sink_ref[2 * (TPS * t + tt) + e]
                    m = jnp.maximum(m, sk)
                p = jnp.exp(sc - m)
                l = jnp.sum(p, axis=-1, keepdims=True)
                if gqa:
                    l = l + jnp.exp(sk - m)
                pn = p / l
                outs.append(_dot(pn.astype(BF16), vb))
                lses.append(m + jnp.log(l))
            if gqa:
                same = jnp.where(hk == 0, outs[0], outs[1])
                other = jnp.where(hk == 0, outs[1], outs[0])
                o_ref[:, lanes] = jnp.where(half == hk, same, pltpu.roll(other, HEAD_DIM, 1))
            else:
                o_ref[:, lanes] = jnp.where(half == 0, outs[0], outs[1])
            l_ref[:, lanes] = jnp.where(half == 0, lses[0], lses[1])

    in_specs = [q_spec, k_spec, v_spec, bias_spec] + ([SMEM_SPEC] if gqa else [])
    args = [proj, proj, proj, bias] + ([sinks] if gqa else [])
    o_shape = jax.ShapeDtypeStruct((s, 512), F32)
    return _call(body, name=name, grid=(n_t, n_g), in_specs=in_specs, out_specs=(tile_spec, tile_spec),
                 out_shape=(o_shape, o_shape), args=args,
                 scratch=[pltpu.VMEM((s + pad, kv_wide), BF16), pltpu.VMEM((s + pad, kv_wide), BF16)],
                 sem=("arbitrary", "arbitrary"), carry=carry)


def _attention_bwd(proj, bias, sinks, do, lse, *, n_back, gqa, q_col, k_col, v_col, name, carry=None):
    s = proj.shape[0]
    lk, pad, q_spec, k_spec, v_spec, bias_spec, tile_spec = _attn_common(s, n_back, gqa, q_col, k_col, v_col)
    n_t, n_g = 512 // (TPS * LANES), s // QROWS
    kv_wide = LANES if gqa else TPS * LANES

    def body(*refs):
        if gqa:
            (q_ref, k_ref, v_ref, bias_ref, sink_ref, do_ref, l_ref,
             dq_ref, dk_ref, dv_ref, dsink_ref, kpad, vpad, dkpad, dvpad) = refs
        else:
            (q_ref, k_ref, v_ref, bias_ref, do_ref, l_ref,
             dq_ref, dk_ref, dv_ref, dbias_ref, kpad, vpad, dkpad, dvpad) = refs
        t, g = pl.program_id(0), pl.program_id(1)

        @pl.when(g == 0)
        def _():
            kpad[0:pad, :] = jnp.zeros((pad, kv_wide), BF16)
            vpad[0:pad, :] = jnp.zeros((pad, kv_wide), BF16)
            kpad[pad:, :] = k_ref[...]
            vpad[pad:, :] = v_ref[...]
            if gqa:
                dsink_ref[...] = jnp.zeros_like(dsink_ref)
            else:
                dbias_ref[...] = jnp.zeros_like(dbias_ref)

        @pl.when((g == 0) & (t == 0) if gqa else g == 0)
        def _():
            dkpad[...] = jnp.zeros_like(dkpad)
            dvpad[...] = jnp.zeros_like(dvpad)

        start = pl.multiple_of(g * QROWS, QROWS)
        half = lax.broadcasted_iota(jnp.int32, (QROWS, LANES), 1) // HEAD_DIM
        for tt in range(TPS):
            lanes = slice(tt * LANES, (tt + 1) * LANES)
            kv_lanes = slice(0, LANES) if gqa else lanes
            kb = kpad[pl.ds(start, lk), kv_lanes]
            vb = vpad[pl.ds(start, lk), kv_lanes]
            q = q_ref[:, lanes]
            dov = do_ref[:, lanes]
            lv = l_ref[:, lanes]
            if gqa:
                hk = (TPS * t + tt) // 2
                q_rolled = pltpu.roll(q.astype(F32), HEAD_DIM, 1).astype(BF16)
                do_rolled = pltpu.roll(dov, HEAD_DIM, 1)
            dqs = []
            dk_acc = jnp.zeros((lk, LANES), F32)
            dv_acc = jnp.zeros((lk, LANES), F32)
            for e in range(2):
                if gqa:
                    kv_half = hk
                    src = jnp.where(hk == e, q, q_rolled)
                    do_src = jnp.where(hk == e, dov, do_rolled)
                else:
                    kv_half = e
                    src = q
                    do_src = dov
                qm = jnp.where(half == kv_half, src, jnp.zeros_like(src))
                dom = jnp.where(half == kv_half, do_src, 0.0).astype(BF16)
                lcol = jnp.max(jnp.where(half == e, lv, -jnp.inf), axis=-1, keepdims=True)
                sc = _dot_nt(qm * (HEAD_DIM ** -0.5), kb) + bias_ref[2 * tt + e]
                pn = jnp.exp(sc - lcol)
                dp = _dot_nt(dom, vb)
                delta = jnp.sum(pn * dp, axis=-1, keepdims=True)
                ds = pn * (dp - delta)
                if gqa:
                    p_sink = jnp.exp(sink_ref[2 * (TPS * t + tt) + e] - lcol)
                    dsk = -jnp.sum(p_sink * delta, axis=0, keepdims=True)
                    row = 2 * tt + e
                    dsink_ref[0, row:row + 1, :] += jnp.broadcast_to(dsk, (1, LANES))
                else:
                    dbias_ref[2 * tt + e] += ds
                dsb = (ds * (HEAD_DIM ** -0.5)).astype(BF16)
                dqs.append(_dot(dsb, kb))
                dk_acc = dk_acc + _dot_tn(dsb, qm)
                dv_acc = dv_acc + _dot_tn(pn.astype(BF16), dom)
            dkpad[pl.ds(start, lk), kv_lanes] += dk_acc
            dvpad[pl.ds(start, lk), kv_lanes] += dv_acc
            if gqa:
                same = jnp.where(hk == 0, dqs[0], dqs[1])
                other = jnp.where(hk == 0, dqs[1], dqs[0])
                dq_ref[:, lanes] = jnp.where(half == hk, same, pltpu.roll(other, HEAD_DIM, 1)).astype(BF16)
            else:
                dq_ref[:, lanes] = jnp.where(half == 0, dqs[0], dqs[1]).astype(BF16)

        @pl.when((g == n_g - 1) & (t == n_t - 1) if gqa else g == n_g - 1)
        def _():
            dk_ref[...] = dkpad[pad:, :].astype(BF16)
            dv_ref[...] = dvpad[pad:, :].astype(BF16)

    in_specs = [q_spec, k_spec, v_spec, bias_spec] + ([SMEM_SPEC] if gqa else []) + [tile_spec, tile_spec]
    args = [proj, proj, proj, bias] + ([sinks] if gqa else []) + [do, lse]
    if gqa:
        kv_out = pl.BlockSpec((s, LANES), lambda t, g: (0, 0))
        kv_shape = jax.ShapeDtypeStruct((s, LANES), BF16)
        extra_spec = pl.BlockSpec((1, 8, LANES), lambda t, g: (t, 0, 0))
        extra_shape = jax.ShapeDtypeStruct((n_t, 8, LANES), F32)
    else:
        kv_out = pl.BlockSpec((s, kv_wide), lambda t, g: (0, t))
        kv_shape = jax.ShapeDtypeStruct((s, 512), BF16)
        extra_spec = pl.BlockSpec((2 * TPS, QROWS, lk), lambda t, g: (t, 0, 0))
        extra_shape = jax.ShapeDtypeStruct(bias.shape[1:], F32)
    return _call(body, name=name, grid=(n_t, n_g), in_specs=in_specs,
                 out_specs=(tile_spec, kv_out, kv_out, extra_spec),
                 out_shape=(jax.ShapeDtypeStruct((s, 512), BF16), kv_shape, kv_shape, extra_shape), args=args,
                 scratch=[pltpu.VMEM((s + pad, kv_wide), BF16), pltpu.VMEM((s + pad, kv_wide), BF16),
                          pltpu.VMEM((s + pad, kv_wide), F32), pltpu.VMEM((s + pad, kv_wide), F32)],
                 sem=("arbitrary", "arbitrary"), carry=carry)


def _sum_slots(r, name):
    n_slots, rows, k = r.shape

    def body(r_ref, o_ref):
        acc = r_ref[0].astype(F32)
        for j in range(1, n_slots):
            acc = acc + r_ref[j].astype(F32)
        o_ref[...] = acc

    return _call(body, name=name, grid=(k // LANES,),
                 in_specs=[pl.BlockSpec((n_slots, rows, LANES), lambda i: (0, 0, i))],
                 out_specs=pl.BlockSpec((rows, LANES), lambda i: (0, i)),
                 out_shape=jax.ShapeDtypeStruct((rows, k), F32), args=[r], sem=("parallel",))


def _sum_rows8(g):
    n = g.shape[2]

    def body(g_ref, o_ref):
        acc = g_ref[0]
        for j in range(1, N_DEV):
            acc = acc + g_ref[j]
        o_ref[...] = acc

    return pl.pallas_call(
        body, name="sum_small_grads", in_specs=[VMEM_SPEC], out_specs=VMEM_SPEC,
        out_shape=jax.ShapeDtypeStruct((1, n), F32), compiler_params=_params(),
    )(g)


def _ada_weight_grad(sc_t, dmod_cols):
    d = sc_t.shape[0]
    w = dmod_cols.shape[1]
    td = _pick(d, (256, 128))

    def body(sc_ref, dm_ref, o_ref):
        scv = sc_ref[...]
        dmv = dm_ref[...]
        acc = scv[:, 0:1] * dmv[0:1, :]
        for b in range(1, N_DEV):
            acc = acc + scv[:, b:b + 1] * dmv[b:b + 1, :]
        o_ref[...] = acc

    return _call(body, name="ada_weight_grad", grid=(d // td,),
                 in_specs=[pl.BlockSpec((td, N_DEV), lambda i: (i, 0)), pl.BlockSpec((N_DEV, w), lambda i: (0, 0))],
                 out_specs=pl.BlockSpec((td, w), lambda i: (i, 0)), out_shape=jax.ShapeDtypeStruct((d, w), F32),
                 args=[sc_t, dmod_cols], sem=("parallel",))


def _adamw_update(w, gv, m, v):
    nm = ADAM_B1 * m + (1.0 - ADAM_B1) * gv
    nv = ADAM_B2 * v + (1.0 - ADAM_B2) * (gv * gv)
    m_hat = nm / (1.0 - ADAM_B1 ** ADAM_STEP)
    v_hat = nv / (1.0 - ADAM_B2 ** ADAM_STEP)
    return -ADAM_LR * (m_hat / (jnp.sqrt(v_hat) + ADAM_EPS) + ADAM_WD * w), nm, nv


def _adamw(w, g, m, v, name):
    rows, cols = w.shape
    tr = _pick(rows, (256, 176, 128, 88, 64)) if rows > 256 else rows

    def body(w_ref, g_ref, m_ref, v_ref, d_ref, nm_ref, nv_ref):
        d_ref[...], nm_ref[...], nv_ref[...] = _adamw_update(w_ref[...], g_ref[...], m_ref[...], v_ref[...])

    spec = pl.BlockSpec((tr, cols), lambda i: (i, 0))
    shape = jax.ShapeDtypeStruct((rows, cols), F32)
    return _call(body, name=name, grid=(rows // tr,), in_specs=[spec] * 4, out_specs=(spec, spec, spec),
                 out_shape=(shape, shape, shape), args=[w, g, m, v], sem=("parallel",))


def _adamw_from_slots(w, slots, m, v, name):
    n_slots, rows, k = slots.shape

    def body(s_ref, w_ref, m_ref, v_ref, g_ref, d_ref, nm_ref, nv_ref):
        gv = s_ref[0].astype(F32)
        for j in range(1, n_slots):
            gv = gv + s_ref[j].astype(F32)
        g_ref[...] = gv
        d_ref[...], nm_ref[...], nv_ref[...] = _adamw_update(w_ref[...], gv, m_ref[...], v_ref[...])

    tr = rows // 2 if rows % 32 == 0 else rows
    spec = pl.BlockSpec((tr, k), lambda i: (i, 0))
    shape = jax.ShapeDtypeStruct((rows, k), F32)
    return _call(body, name=name, grid=(rows // tr,),
                 in_specs=[pl.BlockSpec((n_slots, tr, k), lambda i: (0, i, 0)), spec, spec, spec],
                 out_specs=(spec, spec, spec, spec), out_shape=(shape, shape, shape, shape),
                 args=[slots, w, m, v], sem=("parallel",))


SMALL = ("b_ada", "g_pre_ffn1", "g_post_ffn1", "g_pre_mix", "b_in", "sinks_a", "rel_bias_b", "g_grp_a",
         "g_grp_b", "b_out", "g_post_mix", "g_pre_ffn2", "g_post_ffn2")
WEIGHTS = ("w_ada", "b_ada", "g_pre_ffn1", "w_gate1", "w_up1", "w_down1", "g_post_ffn1", "g_pre_mix", "w_in",
           "b_in", "sinks_a", "rel_bias_b", "g_grp_a", "g_grp_b", "w_out", "b_out", "g_post_mix", "g_pre_ffn2",
           "w_gate2", "w_up2", "w_down2", "g_post_ffn2")


def kernel(x, c, w_ada, b_ada, g_pre_ffn1, w_gate1, w_up1, w_down1, g_post_ffn1, g_pre_mix, w_in, b_in, sinks_a, rel_bias_b, g_grp_a, g_grp_b, w_out, b_out, g_post_mix, g_pre_ffn2, w_gate2, w_up2, w_down2, g_post_ffn2, loss_target, m_w_ada, m_b_ada, m_g_pre_ffn1, m_w_gate1, m_w_up1, m_w_down1, m_g_post_ffn1, m_g_pre_mix, m_w_in, m_b_in, m_sinks_a, m_rel_bias_b, m_g_grp_a, m_g_grp_b, m_w_out, m_b_out, m_g_post_mix, m_g_pre_ffn2, m_w_gate2, m_w_up2, m_w_down2, m_g_post_ffn2, v_w_ada, v_b_ada, v_g_pre_ffn1, v_w_gate1, v_w_up1, v_w_down1, v_g_post_ffn1, v_g_pre_mix, v_w_in, v_b_in, v_sinks_a, v_rel_bias_b, v_g_grp_a, v_g_grp_b, v_w_out, v_b_out, v_g_post_mix, v_g_pre_ffn2, v_w_gate2, v_w_up2, v_w_down2, v_g_post_ffn2):
    given = dict(locals())
    weights = {n: given[n] for n in WEIGHTS}
    mom_m = {n: given["m_" + n] for n in WEIGHTS}
    mom_v = {n: given["v_" + n] for n in WEIGHTS}

    me = 4 * lax.axis_index("x") + 2 * lax.axis_index("y") + lax.axis_index("c")
    xs = x[0]
    tgt = loss_target[0]
    d_model = xs.shape[1]
    ada_cols = w_ada.shape[2]

    sh = {"wg1": w_gate1[0].T, "wu1": w_up1[0].T, "wd1": w_down1[0], "win": w_in[0].T, "wo": w_out[0],
          "wg2": w_gate2[0].T, "wu2": w_up2[0].T, "wd2": w_down2[0]}
    sh = {k: v.astype(BF16) for k, v in sh.items()}

    def gather(*names):
        return _gather_carry([sh[n] for n in names])

    bias_a = _alibi_bias()
    rel_m = _rel_index_matrix()
    rel_vec = jnp.dot(rel_bias_b[0], rel_m.T, precision=lax.Precision.HIGHEST)
    bias_b, (wg1,) = _toeplitz_bias(rel_vec.reshape(H_B, 1, SKEW), carry=gather("wg1"))

    b_cols = lax.dynamic_slice(b_ada, (0, me * ada_cols), (1, ada_cols))
    (sc_all, mod_rows), (wu1,) = _ada_forward(c, w_ada[0], b_cols, gather("wu1"))
    mod = mod_rows.reshape(N_MOD, d_model)
    shift1, scale1, gate1, shift2, scale2, gate2, shift3, scale3, gate3 = (mod[i:i + 1] for i in range(N_MOD))

    h1 = _pre_norm(xs, g_pre_ffn1, scale1, shift1, "pre_norm_ffn1")
    (a1, b1, u1), (wd1,) = _ffn_up(h1, wg1, wu1, "ffn_up_ffn1", carry=gather("wd1"))
    y1, (win,) = _mm_nn([(u1, wd1)], "ffn_down_ffn1", F32, carry=gather("win"))
    x1, h2 = _post_pre_norm(xs, y1, g_post_ffn1, gate1, 0.5, g_pre_mix, scale2, shift2, "post_ffn1_pre_mix")

    proj, (wo,) = _mm_nt(h2, win, "in_proj", BF16, bias=b_in, carry=gather("wo"))
    sinks = sinks_a[0]
    cfg_a = dict(n_back=BACK_A, gqa=True, q_col=0, k_col=QA // LANES, v_col=(QA + KVA) // LANES)
    cfg_b = dict(n_back=BACK_B, gqa=False, q_col=(QA + 2 * KVA) // LANES, k_col=(QA + 2 * KVA + QB) // LANES,
                 v_col=(QA + 2 * KVA + 2 * QB) // LANES)
    (oa, lse_a), (wg2,) = _attention_fwd(proj, bias_a, sinks, name="attn_a", carry=gather("wg2"), **cfg_a)
    (ob, lse_b), (wu2,) = _attention_fwd(proj, bias_b, None, name="attn_b", carry=gather("wu2"), **cfg_b)
    ycat = _group_norm_cat(oa, ob, g_grp_a, g_grp_b)
    ymix = _mm_nn([(ycat, wo)], "out_proj", F32, bias=b_out)
    x2, h3 = _post_pre_norm(x1, ymix, g_post_mix, gate2, 1.0, g_pre_ffn2, scale3, shift3, "post_mix_pre_ffn2")

    (a3, b3, u3), (wd2,) = _ffn_up(h3, wg2, wu2, "ffn_up_ffn2", carry=gather("wd2"))
    y3 = _mm_nn([(u3, wd2)], "ffn_down_ffn2", F32)

    def scatter(*grads):
        return _scatter_carry(list(grads))

    slots = {}

    dx3, dy, loss_part, s1 = _post_norm_loss_bwd(x2, y3, g_post_ffn2, gate3, 0.5, tgt, "post_ffn2_loss_bwd")
    da, db = _ffn_down_bwd(dy, wd2, a3, b3, "ffn_down_bwd_ffn2")
    dwd2 = _mm_tn_pair(u3, dy, "grad_wd_ffn2")
    dwg2 = _mm_tn_pair(da, h3, "grad_wg_ffn2")
    dwu2 = _mm_tn_pair(db, h3, "grad_wu_ffn2")
    dh, (slots["wd2"],) = _mm_nn([(da, wg2), (db, wu2)], "ffn_up_bwd_ffn2", F32, carry=scatter(dwd2))
    dx2, dymix, s2, s3, s1m, db_out = _pre_post_norm_bwd(dh, x2, g_pre_ffn2, scale3, dx3, ymix, g_post_mix, gate2,
                                                         1.0, "pre_ffn2_post_mix_bwd")
    sm3 = dict(shift=s3, scale=s2 * g_pre_ffn2, gate=0.5 * g_post_ffn2 * s1,
               g_pre=(1.0 + scale3) * s2, g_post=(0.5 * gate3) * s1)

    dycat = _mm_nt(dymix, wo, "out_proj_bwd", F32)
    dwo = _mm_tn_pair(ycat, dymix, "grad_wo")
    doa, dob, dg_a, dg_b = _group_norm_bwd(dycat, oa, ob, g_grp_a, g_grp_b)
    (dqa, dka, dva, dsink), (slots["wg2"],) = _attention_bwd(
        proj, bias_a, sinks, doa, lse_a, name="attn_a_bwd", carry=scatter(dwg2), **cfg_a)
    (dqb, dkb, dvb, dbias), (slots["wu2"], slots["wo"]) = _attention_bwd(
        proj, bias_b, None, dob, lse_b, name="attn_b_bwd", carry=scatter(dwu2, dwo), **cfg_b)
    dproj = jnp.concatenate([dqa, dka, dva, dqb, dkb, dvb], axis=1)
    db_in = _col_sum(dproj, "grad_b_in")
    dwin = _mm_tn_pair(dproj, h2, "grad_win")
    dh2 = _mm_nn([(dproj, win)], "in_proj_bwd", F32)
    dx1, dy, s2m, s3m, s1, _ = _pre_post_norm_bwd(dh2, x1, g_pre_mix, scale2, dx2, y1, g_post_ffn1, gate1, 0.5,
                                                  "pre_mix_post_ffn1_bwd")
    d_rel = jnp.dot(_diagonal_sums(dbias).reshape(H_B, SKEW), rel_m, precision=lax.Precision.HIGHEST)
    d_sinks = dsink[:, :2 * TPS, 0].reshape(1, H_A)

    (da, db), (slots["win"],) = _ffn_down_bwd(dy, wd1, a1, b1, "ffn_down_bwd_ffn1", carry=scatter(dwin))
    dwd1 = _mm_tn_pair(u1, dy, "grad_wd_ffn1")
    dwg1, (slots["wd1"],) = _mm_tn_pair(da, h1, "grad_wg_ffn1", carry=scatter(dwd1))
    dwu1, (slots["wg1"],) = _mm_tn_pair(db, h1, "grad_wu_ffn1", carry=scatter(dwg1))
    dh, (slots["wu1"],) = _mm_nn([(da, wg1), (db, wu1)], "ffn_up_bwd_ffn1", F32, carry=scatter(dwu1))
    dx0, s2, s3 = _pre_norm_bwd(dh, xs, g_pre_ffn1, scale1, dx1, "pre_norm_bwd_ffn1")
    sm1 = dict(shift=s3, scale=s2 * g_pre_ffn1, gate=0.5 * g_post_ffn1 * s1,
               g_pre=(1.0 + scale1) * s2, g_post=(0.5 * gate1) * s1)

    dmod = jnp.concatenate([sm1["shift"], sm1["scale"], sm1["gate"],
                            s3m, s2m * g_pre_mix, g_post_mix * s1m,
                            sm3["shift"], sm3["scale"], sm3["gate"]], axis=1)
    small_parts = {
        "b_ada": dmod, "g_pre_ffn1": sm1["g_pre"], "g_post_ffn1": sm1["g_post"],
        "g_pre_mix": (1.0 + scale2) * s2m, "b_in": db_in, "sinks_a": d_sinks,
        "rel_bias_b": d_rel.reshape(1, H_B * N_REL), "g_grp_a": dg_a, "g_grp_b": dg_b, "b_out": db_out,
        "g_post_mix": gate2 * s1m, "g_pre_ffn2": sm3["g_pre"], "g_post_ffn2": sm3["g_post"]}
    sizes = [small_parts[n].shape[1] for n in SMALL]
    n_small = sum(sizes)
    n_pad = -(n_small + 1) % LANES
    packed = jnp.concatenate([small_parts[n] for n in SMALL] + [loss_part, jnp.zeros((1, n_pad), F32)], axis=1)
    gathered = _all_gather_small(packed)
    small_sum = _sum_rows8(gathered)
    loss = small_sum[0, n_small]
    dmod_cols = lax.dynamic_slice(gathered.reshape(N_DEV, n_small + 1 + n_pad), (0, me * ada_cols),
                                  (N_DEV, ada_cols))
    g_ada = _ada_weight_grad(sc_all.reshape(N_DEV, d_model).T, dmod_cols)

    out_g, out_d, out_m, out_v = {}, {}, {}, {}
    d_, m_, v_ = _adamw(w_ada[0], g_ada, m_w_ada[0], v_w_ada[0], "adamw_w_ada")
    out_g["w_ada"], out_d["w_ada"], out_m["w_ada"], out_v["w_ada"] = g_ada[None], d_[None], m_[None], v_[None]
    for n, key, transposed in (("w_gate1", "wg1", True), ("w_up1", "wu1", True), ("w_down1", "wd1", False),
                               ("w_in", "win", True), ("w_out", "wo", False), ("w_gate2", "wg2", True),
                               ("w_up2", "wu2", True), ("w_down2", "wd2", False)):
        view = (lambda t: t.T) if transposed else (lambda t: t)
        res = _adamw_from_slots(view(weights[n][0]), slots[key], view(mom_m[n][0]), view(mom_v[n][0]),
                                "adamw_" + n)
        out_g[n], out_d[n], out_m[n], out_v[n] = (view(t)[None] for t in res)

    def pack(tree):
        return jnp.concatenate([tree[n].reshape(1, -1) for n in SMALL], axis=1)

    g_small = small_sum[:, :n_small]
    d_s, m_s, v_s = _adamw(pack(weights), g_small, pack(mom_m), pack(mom_v), "adamw_small")
    off = 0
    for n, size in zip(SMALL, sizes):
        shape = weights[n].shape
        out_g[n] = g_small[:, off:off + size].reshape(shape)
        out_d[n] = d_s[:, off:off + size].reshape(shape)
        out_m[n] = m_s[:, off:off + size].reshape(shape)
        out_v[n] = v_s[:, off:off + size].reshape(shape)
        off += size

    return (loss, dx0[None], *[out_g[n] for n in WEIGHTS], *[out_d[n] for n in WEIGHTS],
            *[out_m[n] for n in WEIGHTS], *[out_v[n] for n in WEIGHTS])
```

```python
import numpy as np
import jax
import jax.numpy as jnp
from jax import lax
from jax.experimental import pallas as pl
from jax.experimental.pallas import tpu as pltpu

F32 = jnp.float32
BF16 = jnp.bfloat16
MESH = pl.DeviceIdType.MESH
ANY = pl.BlockSpec(memory_space=pl.ANY)
VMEM_SPEC = pl.BlockSpec(memory_space=pltpu.VMEM)
SMEM_SPEC = pl.BlockSpec(memory_space=pltpu.SMEM)

N_DEV = 8
CHUNK = 64
HEAD_DIM = 64
LANES = 128
H_A, KV_A, H_B = 8, 2, 8
BACK_A, BACK_B = 2, 8
REL_CLIP = 128
N_REL = 2 * REL_CLIP + 1
QA, KVA, QB = H_A * HEAD_DIM, KV_A * HEAD_DIM, H_B * HEAD_DIM
D_IN = QA + 2 * KVA + 3 * QB
N_MOD = 9
EPS = 1e-6
NEG_INF = -1e30
QG = 4
QROWS = QG * CHUNK
TPS = 2
SKEW = 1024
ADAM_LR, ADAM_B1, ADAM_B2, ADAM_EPS, ADAM_WD, ADAM_STEP = 0.001, 0.9, 0.999, 1e-08, 0.01, 10
VMEM_LIMIT = 56 * 2 ** 20


def _pick(n, cands):
    for c in cands:
        if n % c == 0:
            return c
    return n


def _params(sem=None):
    return pltpu.CompilerParams(dimension_semantics=sem, vmem_limit_bytes=VMEM_LIMIT)


def _dot_nt(a, b):
    return lax.dot_general(a, b, (((1,), (1,)), ((), ())), preferred_element_type=F32)


def _dot_tn(a, b):
    return lax.dot_general(a, b, (((0,), (0,)), ((), ())), preferred_element_type=F32)


def _dot(a, b):
    return jnp.dot(a, b, preferred_element_type=F32)


def _sigmoid(a):
    return 0.5 * (jnp.tanh(0.5 * a) + 1.0)


def _mesh_pos():
    return lax.axis_index("x"), lax.axis_index("y"), lax.axis_index("c")


def _peer(x, y, c, r):
    px = 1 - x if r & 4 else x
    py = 1 - y if r & 2 else y
    pc = 1 - c if r & 1 else c
    return px, py, pc


class _Carry:
    def __init__(self, ins, out_shapes, scratch, start, finish):
        self.ins, self.out_shapes, self.scratch = list(ins), list(out_shapes), list(scratch)
        self.start, self.finish = start, finish


def _call(body, *, name, grid, in_specs, out_specs, out_shape, args, scratch=(), sem=None, carry=None):
    single = not isinstance(out_shape, (tuple, list))
    out_specs = (out_specs,) if single else tuple(out_specs)
    out_shape = (out_shape,) if single else tuple(out_shape)
    if carry is None:
        res = pl.pallas_call(body, name=name, grid=grid, in_specs=list(in_specs), out_specs=out_specs,
                             out_shape=out_shape, scratch_shapes=list(scratch), compiler_params=_params(sem))(*args)
        return res[0] if single else res
    n_in, n_out, n_s = len(in_specs), len(out_shape), len(scratch)
    ci, co = len(carry.ins), len(carry.out_shapes)

    def wrapped(*refs):
        ins, cins = refs[:n_in], refs[n_in:n_in + ci]
        outs = refs[n_in + ci:n_in + ci + n_out]
        couts = refs[n_in + ci + n_out:n_in + ci + n_out + co]
        scr = refs[n_in + ci + n_out + co:n_in + ci + n_out + co + n_s]
        cscr = refs[n_in + ci + n_out + co + n_s:]
        first, last = None, None
        for ax, n in enumerate(grid):
            f, l = pl.program_id(ax) == 0, pl.program_id(ax) == n - 1
            first = f if first is None else first & f
            last = l if last is None else last & l
        pl.when(first)(lambda: carry.start(cins, couts, cscr))
        body(*ins, *outs, *scr)
        pl.when(last)(lambda: carry.finish(cins, couts, cscr))

    res = pl.pallas_call(
        wrapped, name=name, grid=grid, in_specs=list(in_specs) + [ANY] * ci, out_specs=out_specs + (ANY,) * co,
        out_shape=out_shape + tuple(carry.out_shapes), scratch_shapes=list(scratch) + carry.scratch,
        compiler_params=_params(("arbitrary",) * len(grid)))(*args, *carry.ins)
    main = res[:n_out]
    return (main[0] if single else main), res[n_out:]


def _gather_carry(shards):
    n_w = len(shards)
    rows = [s.shape[0] for s in shards]

    def plan(ins, outs, scr):
        send_sems, recv_sems, local_sems = scr
        x, y, c = _mesh_pos()
        me, sibling = (x, y, c), (x, y, 1 - c)
        chips = [(1 - x, y), (x, 1 - y), (1 - x, 1 - y)]

        def block(w, dev):
            start = pl.multiple_of((4 * dev[0] + 2 * dev[1] + dev[2]) * rows[w], 16)
            return outs[w].at[pl.ds(start, rows[w]), :]

        def copy(w, k, dev, to, src=None):
            return pltpu.make_async_remote_copy(
                src_ref=block(w, dev) if src is None else src, dst_ref=block(w, dev),
                send_sem=send_sems.at[w, k], recv_sem=recv_sems.at[w, k], device_id=to, device_id_type=MESH)

        mine = [pltpu.make_async_copy(ins[w], block(w, me), local_sems.at[w]) for w in range(n_w)]
        first = []
        for j, chip in enumerate(chips):
            first += [copy(w, 1 + j, me, (*chip, c), src=ins[w]) for w in range(n_w)]
        first += [copy(w, 0, me, sibling, src=ins[w]) for w in range(n_w)]
        return c, me, sibling, chips, copy, mine, first

    def start(ins, outs, scr):
        _, _, _, _, _, mine, first = plan(ins, outs, scr)
        for cp in mine + first:
            cp.start()

    def finish(ins, outs, scr):
        c, me, sibling, chips, copy, mine, first = plan(ins, outs, scr)
        passed = []
        for j, chip in enumerate(chips):
            for w in range(n_w):
                copy(w, 1 + j, (*chip, c), me).wait_recv()
                cp = copy(w, 4 + j, (*chip, c), sibling)
                cp.start()
                passed.append(cp)
        for w in range(n_w):
            copy(w, 0, sibling, me).wait_recv()
        for j, chip in enumerate(chips):
            for w in range(n_w):
                copy(w, 4 + j, (*chip, 1 - c), me).wait_recv()
        for cp in first + passed:
            cp.wait_send()
        for cp in mine:
            cp.wait()

    return _Carry(
        shards, [jax.ShapeDtypeStruct((N_DEV * s.shape[0], s.shape[1]), s.dtype) for s in shards],
        [pltpu.SemaphoreType.DMA((n_w, N_DEV - 1)), pltpu.SemaphoreType.DMA((n_w, N_DEV - 1)),
         pltpu.SemaphoreType.DMA((n_w,))], start, finish)


def _scatter_carry(parts):
    n_w = len(parts)
    n_chip = N_DEV // 2
    rows = [g.shape[0] // n_chip for g in parts]

    def plan(ins, outs, scr):
        send_sems, recv_sems, local_sems = scr
        x, y, c = _mesh_pos()

        def src(w, chip_index):
            return ins[w].at[pl.ds(pl.multiple_of(chip_index * rows[w], 16), rows[w]), :]

        mine = [pltpu.make_async_copy(src(w, 2 * x + y), outs[w].at[0], local_sems.at[w]) for w in range(n_w)]
        copies = []
        for r in (3, 2, 1):
            px, py, _ = _peer(x, y, c, 2 * r)
            for w in range(n_w):
                copies.append(pltpu.make_async_remote_copy(
                    src_ref=src(w, 2 * px + py), dst_ref=outs[w].at[r], send_sem=send_sems.at[w, r - 1],
                    recv_sem=recv_sems.at[w, r - 1], device_id=(px, py, c), device_id_type=MESH))
        return mine, copies

    def start(ins, outs, scr):
        mine, copies = plan(ins, outs, scr)
        for cp in mine + copies:
            cp.start()

    def finish(ins, outs, scr):
        mine, copies = plan(ins, outs, scr)
        for cp in copies:
            cp.wait_recv()
        for cp in copies:
            cp.wait_send()
        for cp in mine:
            cp.wait()

    return _Carry(
        parts, [jax.ShapeDtypeStruct((n_chip, r, g.shape[1]), g.dtype) for r, g in zip(rows, parts)],
        [pltpu.SemaphoreType.DMA((n_w, n_chip - 1)), pltpu.SemaphoreType.DMA((n_w, n_chip - 1)),
         pltpu.SemaphoreType.DMA((n_w,))], start, finish)


def _ada_forward(c_row, w_ada, b_cols, carry):
    d = c_row.shape[1]
    wcols = w_ada.shape[1]
    ci, co = len(carry.ins), len(carry.out_shapes)

    def body(*refs):
        c_ref, w_ref, b_ref = refs[:3]
        cins = refs[3:3 + ci]
        sc_ref, mod_ref = refs[3 + ci:5 + ci]
        couts = refs[5 + ci:5 + ci + co]
        rows_ref, send_sems, recv_sems = refs[5 + ci + co:8 + ci + co]
        cscr = refs[8 + ci + co:]
        carry.start(cins, couts, cscr)
        x, y, c = _mesh_pos()
        me = 4 * x + 2 * y + c
        cv = c_ref[...]
        sc_ref[me] = cv * _sigmoid(cv)

        sends = []
        for r in range(1, N_DEV):
            px, py, pc = _peer(x, y, c, r)
            cp = pltpu.make_async_remote_copy(
                src_ref=sc_ref.at[me], dst_ref=sc_ref.at[me], send_sem=send_sems.at[0, r - 1],
                recv_sem=recv_sems.at[0, r - 1], device_id=(px, py, pc), device_id_type=MESH)
            cp.start()
            sends.append(cp)
        for r in range(1, N_DEV):
            px, py, pc = _peer(x, y, c, r)
            pid = 4 * px + 2 * py + pc
            pltpu.make_async_remote_copy(
                src_ref=sc_ref.at[pid], dst_ref=sc_ref.at[pid], send_sem=send_sems.at[0, r - 1],
                recv_sem=recv_sems.at[0, r - 1], device_id=(px, py, pc), device_id_type=MESH).wait_recv()
        for cp in sends:
            cp.wait_send()

        sc_all = jnp.concatenate([sc_ref[j] for j in range(N_DEV)], axis=0)
        rows = _dot(sc_all.astype(BF16), w_ref[...].astype(BF16)) + b_ref[...]
        for j in range(N_DEV):
            rows_ref[j] = rows[j:j + 1, :]
        mod_ref[me] = rows_ref[me]

        sends = []
        for r in range(1, N_DEV):
            px, py, pc = _peer(x, y, c, r)
            pid = 4 * px + 2 * py + pc
            cp = pltpu.make_async_remote_copy(
                src_ref=rows_ref.at[pid], dst_ref=mod_ref.at[me], send_sem=send_sems.at[1, r - 1],
                recv_sem=recv_sems.at[1, r - 1], device_id=(px, py, pc), device_id_type=MESH)
            cp.start()
            sends.append(cp)
        for r in range(1, N_DEV):
            px, py, pc = _peer(x, y, c, r)
            pid = 4 * px + 2 * py + pc
            pltpu.make_async_remote_copy(
                src_ref=rows_ref.at[pid], dst_ref=mod_ref.at[pid], send_sem=send_sems.at[1, r - 1],
                recv_sem=recv_sems.at[1, r - 1], device_id=(px, py, pc), device_id_type=MESH).wait_recv()
        for cp in sends:
            cp.wait_send()
        carry.finish(cins, couts, cscr)

    res = pl.pallas_call(
        body, name="ada_forward",
        out_shape=(jax.ShapeDtypeStruct((N_DEV, 1, d), F32), jax.ShapeDtypeStruct((N_DEV, 1, wcols), F32),
                   *carry.out_shapes),
        in_specs=[VMEM_SPEC, VMEM_SPEC, VMEM_SPEC] + [ANY] * ci, out_specs=(VMEM_SPEC, VMEM_SPEC) + (ANY,) * co,
        scratch_shapes=[pltpu.VMEM((N_DEV, 1, wcols), F32), pltpu.SemaphoreType.DMA((2, N_DEV - 1)),
                        pltpu.SemaphoreType.DMA((2, N_DEV - 1))] + carry.scratch,
        compiler_params=_params(),
    )(c_row, w_ada, b_cols, *carry.ins)
    return res[:2], res[2:]


def _all_gather_small(v):
    n = v.shape[1]

    def body(v_ref, out_ref, send_sems, recv_sems):
        x, y, c = _mesh_pos()
        me = 4 * x + 2 * y + c
        out_ref[me] = v_ref[...]
        sends = []
        for r in range(1, N_DEV):
            px, py, pc = _peer(x, y, c, r)
            cp = pltpu.make_async_remote_copy(
                src_ref=v_ref, dst_ref=out_ref.at[me], send_sem=send_sems.at[r - 1],
                recv_sem=recv_sems.at[r - 1], device_id=(px, py, pc), device_id_type=MESH)
            cp.start()
            sends.append(cp)
        for r in range(1, N_DEV):
            px, py, pc = _peer(x, y, c, r)
            pid = 4 * px + 2 * py + pc
            pltpu.make_async_remote_copy(
                src_ref=v_ref, dst_ref=out_ref.at[pid], send_sem=send_sems.at[r - 1],
                recv_sem=recv_sems.at[r - 1], device_id=(px, py, pc), device_id_type=MESH).wait_recv()
        for cp in sends:
            cp.wait_send()

    return pl.pallas_call(
        body, name="all_gather_small",
        out_shape=jax.ShapeDtypeStruct((N_DEV, 1, n), F32),
        in_specs=[VMEM_SPEC], out_specs=VMEM_SPEC,
        scratch_shapes=[pltpu.SemaphoreType.DMA((N_DEV - 1,)), pltpu.SemaphoreType.DMA((N_DEV - 1,))],
        compiler_params=_params(),
    )(v)


def _mm_nt(a, b, name, out_dtype, bias=None, carry=None):
    m, k = a.shape
    n = b.shape[0]
    tm = _pick(m, (512, 256, 128))
    tn = _pick(n, (1408, 1152, 1024, 768, 512, 256, 128))

    def body(*refs):
        acc = _dot_nt(refs[0][...], refs[1][...])
        if bias is not None:
            acc = acc + refs[2][...]
        refs[-1][...] = acc.astype(out_dtype)

    in_specs = [pl.BlockSpec((tm, k), lambda i, j: (i, 0)), pl.BlockSpec((tn, k), lambda i, j: (j, 0))]
    args = [a, b]
    if bias is not None:
        in_specs.append(pl.BlockSpec((1, tn), lambda i, j: (0, j)))
        args.append(bias)
    return _call(body, name=name, grid=(m // tm, n // tn), in_specs=in_specs,
                 out_specs=pl.BlockSpec((tm, tn), lambda i, j: (i, j)),
                 out_shape=jax.ShapeDtypeStruct((m, n), out_dtype), args=args,
                 sem=("parallel", "parallel"), carry=carry)


def _mm_nn(pairs, name, out_dtype, bias=None, carry=None):
    m, k = pairs[0][0].shape
    n = pairs[0][1].shape[1]
    n_p = len(pairs)
    tm = _pick(m, (512, 256, 128))
    tk = k if n_p == 1 else _pick(k, (1408, 1152, 1024, 768, 512, 256, 128))
    nk = k // tk

    def body(*refs):
        o_ref, acc_ref = refs[-2], refs[-1]
        kk = pl.program_id(1)

        if nk == 1:
            acc = _dot(refs[0][...], refs[1][...])
            for p in range(1, n_p):
                acc = acc + _dot(refs[2 * p][...], refs[2 * p + 1][...])
            if bias is not None:
                acc = acc + refs[2 * n_p][...]
            o_ref[...] = acc.astype(out_dtype)
            return

        @pl.when(kk == 0)
        def _():
            acc_ref[...] = jnp.zeros_like(acc_ref)

        for p in range(n_p):
            acc_ref[...] += _dot(refs[2 * p][...], refs[2 * p + 1][...])

        @pl.when(kk == nk - 1)
        def _():
            acc = acc_ref[...]
            if bias is not None:
                acc = acc + refs[2 * n_p][...]
            o_ref[...] = acc.astype(out_dtype)

    in_specs, args = [], []
    for a, b in pairs:
        in_specs += [pl.BlockSpec((tm, tk), lambda i, kk: (i, kk)), pl.BlockSpec((tk, n), lambda i, kk: (kk, 0))]
        args += [a, b]
    if bias is not None:
        in_specs.append(pl.BlockSpec((1, n), lambda i, kk: (0, 0)))
        args.append(bias)
    return _call(body, name=name, grid=(m // tm, nk), in_specs=in_specs,
                 out_specs=pl.BlockSpec((tm, n), lambda i, kk: (i, 0)),
                 out_shape=jax.ShapeDtypeStruct((m, n), out_dtype), args=args,
                 scratch=[pltpu.VMEM((tm, n), F32)], sem=("parallel", "arbitrary"), carry=carry)


def _mm_tn(a, b, name, out_dtype=BF16, carry=None):
    k, m = a.shape
    n = b.shape[1]
    tm = _pick(m, (1408, 1152, 1024, 768, 512, 256, 128))
    tk = _pick(k, (512, 256, 128))
    nk = k // tk

    def body(a_ref, b_ref, o_ref, acc_ref):
        kk = pl.program_id(1)

        @pl.when(kk == 0)
        def _():
            acc_ref[...] = jnp.zeros_like(acc_ref)

        acc_ref[...] += _dot_tn(a_ref[...], b_ref[...])

        @pl.when(kk == nk - 1)
        def _():
            o_ref[...] = acc_ref[...].astype(out_dtype)

    return _call(body, name=name, grid=(m // tm, nk),
                 in_specs=[pl.BlockSpec((tk, tm), lambda i, kk: (kk, i)), pl.BlockSpec((tk, n), lambda i, kk: (kk, 0))],
                 out_specs=pl.BlockSpec((tm, n), lambda i, kk: (i, 0)),
                 out_shape=jax.ShapeDtypeStruct((m, n), out_dtype), args=[a, b],
                 scratch=[pltpu.VMEM((tm, n), F32)], sem=("parallel", "arbitrary"), carry=carry)


def _mm_tn_pair(a, b, name, carry=None):
    k, m = a.shape
    n = b.shape[1]
    rows = m // N_DEV
    n_chip = N_DEV // 2
    tm = 4 * rows
    tk = _pick(k, (1024, 512, 256, 128))
    nk = k // tk

    def body(a_ref, b_ref, p_ref, acc_ref, keep_ref, send_ref, land_ref, send_sems, recv_sems):
        i, kk = pl.program_id(0), pl.program_id(1)
        x, y, c = _mesh_pos()

        def push(chip):
            return pltpu.make_async_remote_copy(
                src_ref=send_ref.at[chip], dst_ref=land_ref.at[chip], send_sem=send_sems.at[chip],
                recv_sem=recv_sems.at[chip], device_id=(x, y, 1 - c), device_id_type=MESH)

        @pl.when(kk == 0)
        def _():
            acc_ref[...] = jnp.zeros_like(acc_ref)

        acc_ref[...] += _dot_tn(a_ref[...], b_ref[...])

        for t in range(2):
            @pl.when((kk == nk - 1) & (i == t))
            def _(t=t):
                for ob in range(4):
                    chip, core = 2 * t + ob // 2, ob % 2
                    blk = acc_ref[ob * rows:(ob + 1) * rows, :]

                    @pl.when(c == core)
                    def _(chip=chip, blk=blk):
                        keep_ref[chip] = blk

                    @pl.when(c != core)
                    def _(chip=chip, blk=blk):
                        send_ref[chip] = blk.astype(BF16)
                        push(chip).start()

        @pl.when((kk == nk - 1) & (i == 1))
        def _():
            for chip in range(n_chip):
                push(chip).wait_recv()
                p_ref[chip * rows:(chip + 1) * rows, :] = (
                    keep_ref[chip] + land_ref[chip].astype(F32)).astype(BF16)
            for chip in range(n_chip):
                push(chip).wait_send()

    return _call(body, name=name, grid=(2, nk),
                 in_specs=[pl.BlockSpec((tk, tm), lambda i, kk: (kk, i)), pl.BlockSpec((tk, n), lambda i, kk: (kk, 0))],
                 out_specs=pl.BlockSpec((n_chip * rows, n), lambda i, kk: (0, 0)),
                 out_shape=jax.ShapeDtypeStruct((n_chip * rows, n), BF16), args=[a, b],
                 scratch=[pltpu.VMEM((tm, n), F32), pltpu.VMEM((n_chip, rows, n), F32),
                          pltpu.VMEM((n_chip, rows, n), BF16), pltpu.VMEM((n_chip, rows, n), BF16),
                          pltpu.SemaphoreType.DMA((n_chip,)), pltpu.SemaphoreType.DMA((n_chip,))],
                 sem=("arbitrary", "arbitrary"), carry=carry)


def _ffn_up(h, wg_t, wu_t, name, carry=None):
    s, d = h.shape
    f = wg_t.shape[0]
    tm = _pick(s, (512, 256, 128))
    tf = _pick(f, (1408, 1024, 512, 256, 128))

    def body(h_ref, wg_ref, wu_ref, a_ref, b_ref, u_ref):
        hh = h_ref[...]
        a = _dot_nt(hh, wg_ref[...])
        b = _dot_nt(hh, wu_ref[...])
        a_ref[...] = a.astype(BF16)
        b_ref[...] = b.astype(BF16)
        u_ref[...] = ((a * _sigmoid(a)) * b).astype(BF16)

    w_spec = pl.BlockSpec((tf, d), lambda i, j: (j, 0))
    o_spec = pl.BlockSpec((tm, tf), lambda i, j: (i, j))
    o_shape = jax.ShapeDtypeStruct((s, f), BF16)
    return _call(body, name=name, grid=(s // tm, f // tf),
                 in_specs=[pl.BlockSpec((tm, d), lambda i, j: (i, 0)), w_spec, w_spec],
                 out_specs=(o_spec, o_spec, o_spec), out_shape=(o_shape, o_shape, o_shape),
                 args=[h, wg_t, wu_t], sem=("parallel", "parallel"), carry=carry)


def _ffn_down_bwd(dy, wd, a, b, name, carry=None):
    s, d = dy.shape
    f = wd.shape[0]
    tm = _pick(s, (512, 256, 128))
    tf = _pick(f, (1408, 1024, 512, 256, 128))

    def body(dy_ref, wd_ref, a_ref, b_ref, da_ref, db_ref):
        du = _dot_nt(dy_ref[...], wd_ref[...])
        a = a_ref[...].astype(F32)
        b = b_ref[...].astype(F32)
        sig = _sigmoid(a)
        da_ref[...] = (du * b * (sig * (1.0 + a * (1.0 - sig)))).astype(BF16)
        db_ref[...] = (du * (a * sig)).astype(BF16)

    t_spec = pl.BlockSpec((tm, tf), lambda i, j: (i, j))
    o_shape = jax.ShapeDtypeStruct((s, f), BF16)
    return _call(body, name=name, grid=(s // tm, f // tf),
                 in_specs=[pl.BlockSpec((tm, d), lambda i, j: (i, 0)), pl.BlockSpec((tf, d), lambda i, j: (j, 0)),
                           t_spec, t_spec],
                 out_specs=(t_spec, t_spec), out_shape=(o_shape, o_shape), args=[dy, wd, a, b],
                 sem=("parallel", "parallel"), carry=carry)


def _row_tile(s):
    return _pick(s, (256, 128, 64))


def _vec_spec(d):
    return pl.BlockSpec((1, d), lambda i: (0, 0))


def _pre_norm(x, g, scale, shift, name):
    s, d = x.shape
    ts = _row_tile(s)

    def body(x_ref, g_ref, sc_ref, sh_ref, h_ref):
        xv = x_ref[...]
        r = lax.rsqrt(jnp.mean(xv * xv, axis=-1, keepdims=True) + EPS)
        h_ref[...] = (((xv * r) * g_ref[...]) * (1.0 + sc_ref[...]) + sh_ref[...]).astype(BF16)

    row = pl.BlockSpec((ts, d), lambda i: (i, 0))
    return _call(body, name=name, grid=(s // ts,), in_specs=[row, _vec_spec(d), _vec_spec(d), _vec_spec(d)],
                 out_specs=row, out_shape=jax.ShapeDtypeStruct((s, d), BF16), args=[x, g, scale, shift],
                 sem=("parallel",))


def _post_norm_residual(x, y, g, gate, weight, name):
    s, d = x.shape
    ts = _row_tile(s)

    def body(x_ref, y_ref, g_ref, gate_ref, o_ref):
        yv = y_ref[...]
        r = lax.rsqrt(jnp.mean(yv * yv, axis=-1, keepdims=True) + EPS)
        o_ref[...] = x_ref[...] + (weight * gate_ref[...]) * ((yv * r) * g_ref[...])

    row = pl.BlockSpec((ts, d), lambda i: (i, 0))
    return _call(body, name=name, grid=(s // ts,), in_specs=[row, row, _vec_spec(d), _vec_spec(d)],
                 out_specs=row, out_shape=jax.ShapeDtypeStruct((s, d), F32), args=[x, y, g, gate],
                 sem=("parallel",))


def _post_norm_bwd(dout, y, g, gate, weight, name):
    s, d = y.shape
    ts = _row_tile(s)

    def body(do_ref, y_ref, g_ref, gate_ref, dy_ref, s1_ref, cs_ref):
        @pl.when(pl.program_id(0) == 0)
        def _():
            s1_ref[...] = jnp.zeros_like(s1_ref)
            cs_ref[...] = jnp.zeros_like(cs_ref)

        yv = y_ref[...]
        do = do_ref[...]
        r = lax.rsqrt(jnp.mean(yv * yv, axis=-1, keepdims=True) + EPS)
        yn = yv * r
        dyn = do * ((weight * gate_ref[...]) * g_ref[...])
        dy = r * (dyn - yn * jnp.mean(dyn * yn, axis=-1, keepdims=True))
        dy_ref[...] = dy.astype(BF16)
        s1_ref[...] += jnp.sum(do * yn, axis=0, keepdims=True)
        cs_ref[...] += jnp.sum(dy, axis=0, keepdims=True)

    row = pl.BlockSpec((ts, d), lambda i: (i, 0))
    vec = jax.ShapeDtypeStruct((1, d), F32)
    return _call(body, name=name, grid=(s // ts,), in_specs=[row, row, _vec_spec(d), _vec_spec(d)],
                 out_specs=(row, _vec_spec(d), _vec_spec(d)),
                 out_shape=(jax.ShapeDtypeStruct((s, d), BF16), vec, vec), args=[dout, y, g, gate],
                 sem=("arbitrary",))


def _pre_norm_bwd(dh, x, g, scale, dres, name):
    s, d = x.shape
    ts = _row_tile(s)

    def body(dh_ref, x_ref, g_ref, sc_ref, dr_ref, dx_ref, s2_ref, s3_ref):
        @pl.when(pl.program_id(0) == 0)
        def _():
            s2_ref[...] = jnp.zeros_like(s2_ref)
            s3_ref[...] = jnp.zeros_like(s3_ref)

        xv = x_ref[...]
        dh = dh_ref[...]
        r = lax.rsqrt(jnp.mean(xv * xv, axis=-1, keepdims=True) + EPS)
        n = xv * r
        dn = dh * (g_ref[...] * (1.0 + sc_ref[...]))
        dx_ref[...] = dr_ref[...] + r * (dn - n * jnp.mean(dn * n, axis=-1, keepdims=True))
        s2_ref[...] += jnp.sum(dh * n, axis=0, keepdims=True)
        s3_ref[...] += jnp.sum(dh, axis=0, keepdims=True)

    row = pl.BlockSpec((ts, d), lambda i: (i, 0))
    vec = jax.ShapeDtypeStruct((1, d), F32)
    return _call(body, name=name, grid=(s // ts,), in_specs=[row, row, _vec_spec(d), _vec_spec(d), row],
                 out_specs=(row, _vec_spec(d), _vec_spec(d)),
                 out_shape=(jax.ShapeDtypeStruct((s, d), F32), vec, vec), args=[dh, x, g, scale, dres],
                 sem=("arbitrary",))


def _post_pre_norm(x, y, g_post, gate, weight, g_pre, scale, shift, name):
    s, d = x.shape
    ts = _row_tile(s)

    def body(x_ref, y_ref, gp_ref, gate_ref, g_ref, sc_ref, sh_ref, o_ref, h_ref):
        yv = y_ref[...]
        r = lax.rsqrt(jnp.mean(yv * yv, axis=-1, keepdims=True) + EPS)
        xv = x_ref[...] + (weight * gate_ref[...]) * ((yv * r) * gp_ref[...])
        o_ref[...] = xv
        r2 = lax.rsqrt(jnp.mean(xv * xv, axis=-1, keepdims=True) + EPS)
        h_ref[...] = (((xv * r2) * g_ref[...]) * (1.0 + sc_ref[...]) + sh_ref[...]).astype(BF16)

    row = pl.BlockSpec((ts, d), lambda i: (i, 0))
    return _call(body, name=name, grid=(s // ts,), in_specs=[row, row] + [_vec_spec(d)] * 5,
                 out_specs=(row, row),
                 out_shape=(jax.ShapeDtypeStruct((s, d), F32), jax.ShapeDtypeStruct((s, d), BF16)),
                 args=[x, y, g_post, gate, g_pre, scale, shift], sem=("parallel",))


def _post_norm_loss_bwd(x, y, g, gate, weight, target, name):
    s, d = y.shape
    ts = _row_tile(s)

    def body(x_ref, y_ref, g_ref, gate_ref, t_ref, dx_ref, dy_ref, l_ref, s1_ref):
        @pl.when(pl.program_id(0) == 0)
        def _():
            l_ref[...] = jnp.zeros_like(l_ref)
            s1_ref[...] = jnp.zeros_like(s1_ref)

        yv = y_ref[...]
        r = lax.rsqrt(jnp.mean(yv * yv, axis=-1, keepdims=True) + EPS)
        yn = yv * r
        err = (x_ref[...] + (weight * gate_ref[...]) * (yn * g_ref[...])) - t_ref[...]
        do = err * (1.0 / d)
        dx_ref[...] = do
        l_ref[...] += 0.5 * jnp.sum(jnp.mean(err * err, axis=-1, keepdims=True), axis=0, keepdims=True)
        dyn = do * ((weight * gate_ref[...]) * g_ref[...])
        dy_ref[...] = (r * (dyn - yn * jnp.mean(dyn * yn, axis=-1, keepdims=True))).astype(BF16)
        s1_ref[...] += jnp.sum(do * yn, axis=0, keepdims=True)

    row = pl.BlockSpec((ts, d), lambda i: (i, 0))
    return _call(body, name=name, grid=(s // ts,), in_specs=[row, row, _vec_spec(d), _vec_spec(d), row],
                 out_specs=(row, row, pl.BlockSpec((1, 1), lambda i: (0, 0)), _vec_spec(d)),
                 out_shape=(jax.ShapeDtypeStruct((s, d), F32), jax.ShapeDtypeStruct((s, d), BF16),
                            jax.ShapeDtypeStruct((1, 1), F32), jax.ShapeDtypeStruct((1, d), F32)),
                 args=[x, y, g, gate, target], sem=("arbitrary",))


def _pre_post_norm_bwd(dh, x, g_pre, scale, dres, y, g_post, gate, weight, name):
    s, d = x.shape
    ts = _row_tile(s)

    def body(dh_ref, x_ref, g_ref, sc_ref, dr_ref, y_ref, gp_ref, gate_ref,
             dx_ref, dy_ref, s2_ref, s3_ref, s1_ref, cs_ref):
        @pl.when(pl.program_id(0) == 0)
        def _():
            for ref in (s2_ref, s3_ref, s1_ref, cs_ref):
                ref[...] = jnp.zeros_like(ref)

        xv = x_ref[...]
        dh = dh_ref[...]
        r = lax.rsqrt(jnp.mean(xv * xv, axis=-1, keepdims=True) + EPS)
        n = xv * r
        dn = dh * (g_ref[...] * (1.0 + sc_ref[...]))
        dx = dr_ref[...] + r * (dn - n * jnp.mean(dn * n, axis=-1, keepdims=True))
        dx_ref[...] = dx
        s2_ref[...] += jnp.sum(dh * n, axis=0, keepdims=True)
        s3_ref[...] += jnp.sum(dh, axis=0, keepdims=True)
        yv = y_ref[...]
        ry = lax.rsqrt(jnp.mean(yv * yv, axis=-1, keepdims=True) + EPS)
        yn = yv * ry
        dyn = dx * ((weight * gate_ref[...]) * gp_ref[...])
        dy = ry * (dyn - yn * jnp.mean(dyn * yn, axis=-1, keepdims=True))
        dy_ref[...] = dy.astype(BF16)
        s1_ref[...] += jnp.sum(dx * yn, axis=0, keepdims=True)
        cs_ref[...] += jnp.sum(dy, axis=0, keepdims=True)

    row = pl.BlockSpec((ts, d), lambda i: (i, 0))
    vec = jax.ShapeDtypeStruct((1, d), F32)
    return _call(body, name=name, grid=(s // ts,),
                 in_specs=[row, row, _vec_spec(d), _vec_spec(d), row, row, _vec_spec(d), _vec_spec(d)],
                 out_specs=(row, row) + (_vec_spec(d),) * 4,
                 out_shape=(jax.ShapeDtypeStruct((s, d), F32), jax.ShapeDtypeStruct((s, d), BF16), vec, vec, vec, vec),
                 args=[dh, x, g_pre, scale, dres, y, g_post, gate], sem=("arbitrary",))


def _group_norm_cat(oa, ob, ga, gb):
    s = oa.shape[0]
    ts = _row_tile(s)

    def body(oa_ref, ob_ref, ga_ref, gb_ref, y_ref):
        for o_ref, g_ref, lo, w in ((oa_ref, ga_ref, 0, QA), (ob_ref, gb_ref, QA, QB)):
            ov = o_ref[...]
            r = lax.rsqrt(jnp.mean(ov * ov, axis=-1, keepdims=True) + EPS)
            y_ref[:, lo:lo + w] = ((ov * r) * g_ref[...]).astype(BF16)

    return _call(body, name="group_norm_cat", grid=(s // ts,),
                 in_specs=[pl.BlockSpec((ts, QA), lambda i: (i, 0)), pl.BlockSpec((ts, QB), lambda i: (i, 0)),
                           _vec_spec(QA), _vec_spec(QB)],
                 out_specs=pl.BlockSpec((ts, QA + QB), lambda i: (i, 0)),
                 out_shape=jax.ShapeDtypeStruct((s, QA + QB), BF16), args=[oa, ob, ga, gb], sem=("parallel",))


def _group_norm_bwd(dy, oa, ob, ga, gb):
    s = oa.shape[0]
    ts = _row_tile(s)

    def body(dy_ref, oa_ref, ob_ref, ga_ref, gb_ref, doa_ref, dob_ref, dga_ref, dgb_ref):
        @pl.when(pl.program_id(0) == 0)
        def _():
            dga_ref[...] = jnp.zeros_like(dga_ref)
            dgb_ref[...] = jnp.zeros_like(dgb_ref)

        for o_ref, g_ref, do_ref, dg_ref, lo, w in ((oa_ref, ga_ref, doa_ref, dga_ref, 0, QA),
                                                    (ob_ref, gb_ref, dob_ref, dgb_ref, QA, QB)):
            ov = o_ref[...]
            dyv = dy_ref[:, lo:lo + w]
            r = lax.rsqrt(jnp.mean(ov * ov, axis=-1, keepdims=True) + EPS)
            n = ov * r
            dn = dyv * g_ref[...]
            do_ref[...] = r * (dn - n * jnp.mean(dn * n, axis=-1, keepdims=True))
            dg_ref[...] += jnp.sum(dyv * n, axis=0, keepdims=True)

    ra = pl.BlockSpec((ts, QA), lambda i: (i, 0))
    rb = pl.BlockSpec((ts, QB), lambda i: (i, 0))
    return _call(body, name="group_norm_bwd", grid=(s // ts,),
                 in_specs=[pl.BlockSpec((ts, QA + QB), lambda i: (i, 0)), ra, rb, _vec_spec(QA), _vec_spec(QB)],
                 out_specs=(ra, rb, _vec_spec(QA), _vec_spec(QB)),
                 out_shape=(jax.ShapeDtypeStruct((s, QA), F32), jax.ShapeDtypeStruct((s, QB), F32),
                            jax.ShapeDtypeStruct((1, QA), F32), jax.ShapeDtypeStruct((1, QB), F32)),
                 args=[dy, oa, ob, ga, gb], sem=("arbitrary",))


def _loss_and_grad(y, target):
    s, d = y.shape
    ts = _row_tile(s)

    def body(y_ref, t_ref, l_ref, g_ref):
        @pl.when(pl.program_id(0) == 0)
        def _():
            l_ref[...] = jnp.zeros_like(l_ref)

        err = y_ref[...] - t_ref[...]
        g_ref[...] = err * (1.0 / d)
        row = jnp.mean(err * err, axis=-1, keepdims=True)
        l_ref[...] += 0.5 * jnp.sum(row, axis=0, keepdims=True)

    row = pl.BlockSpec((ts, d), lambda i: (i, 0))
    return _call(body, name="loss_and_grad", grid=(s // ts,), in_specs=[row, row],
                 out_specs=(pl.BlockSpec((1, 1), lambda i: (0, 0)), row),
                 out_shape=(jax.ShapeDtypeStruct((1, 1), F32), jax.ShapeDtypeStruct((s, d), F32)),
                 args=[y, target], sem=("arbitrary",))


def _col_sum(x, name):
    s, n = x.shape
    ts = _row_tile(s)

    def body(x_ref, o_ref):
        @pl.when(pl.program_id(0) == 0)
        def _():
            o_ref[...] = jnp.zeros_like(o_ref)

        o_ref[...] += jnp.sum(x_ref[...].astype(F32), axis=0, keepdims=True)

    return _call(body, name=name, grid=(s // ts,), in_specs=[pl.BlockSpec((ts, n), lambda i: (i, 0))],
                 out_specs=pl.BlockSpec((1, n), lambda i: (0, 0)), out_shape=jax.ShapeDtypeStruct((1, n), F32),
                 args=[x], sem=("arbitrary",))


def _n_variants(n_back):
    return -(-n_back // QG) + 1


def _alibi_bias():
    i = np.arange(QROWS)[:, None]
    j = np.arange((QG + BACK_A) * CHUNK)[None, :]
    dist = np.abs(BACK_A * CHUNK + i - j).astype(np.float32)
    dc = j // CHUNK - i // CHUNK
    valid = (dc >= 0) & (dc <= BACK_A)
    slopes = np.array([2.0 ** (-8.0 * (h + 1) / H_A) for h in range(H_A)], dtype=np.float32)
    bias = -slopes[:, None, None] * dist[None]
    out = [np.where((valid & (j >= (BACK_A - QG * v) * CHUNK))[None], bias, np.float32(NEG_INF))
           for v in range(_n_variants(BACK_A))]
    return jnp.asarray(np.stack(out).astype(np.float32))


def _rel_index_matrix():
    cc = np.arange(SKEW)
    dist = np.where(cc < SKEW - QROWS, BACK_B * CHUNK - cc, BACK_B * CHUNK + SKEW - cc)
    idx = np.clip(dist, -REL_CLIP, REL_CLIP) + REL_CLIP
    m = np.zeros((SKEW, N_REL), np.float32)
    m[cc, idx] = 1.0
    return jnp.asarray(m)


def _toeplitz_bias(vec, carry=None):
    lk = (QG + BACK_B) * CHUNK
    nv = _n_variants(BACK_B)

    def body(v_ref, o_ref):
        xv = jnp.broadcast_to(v_ref[0], (QROWS, SKEW))
        row = lax.broadcasted_iota(jnp.int32, (QROWS, SKEW), 0)
        for bit in range(QROWS.bit_length() - 1):
            xv = jnp.where((row >> bit) & 1 == 1, pltpu.roll(xv, 1 << bit, 1), xv)
        ri = lax.broadcasted_iota(jnp.int32, (QROWS, lk), 0) // CHUNK
        col = lax.broadcasted_iota(jnp.int32, (QROWS, lk), 1)
        ci = col // CHUNK
        valid = (ci - ri >= 0) & (ci - ri <= BACK_B)
        for v in range(nv):
            o_ref[v, 0] = jnp.where(valid & (col >= (BACK_B - QG * v) * CHUNK), xv[:, :lk], NEG_INF)

    return _call(body, name="toeplitz_bias", grid=(H_B,),
                 in_specs=[pl.BlockSpec((1, 1, SKEW), lambda h: (h, 0, 0))],
                 out_specs=pl.BlockSpec((nv, 1, QROWS, lk), lambda h: (0, h, 0, 0)),
                 out_shape=jax.ShapeDtypeStruct((nv, H_B, QROWS, lk), F32), args=[vec], sem=("parallel",),
                 carry=carry)


def _diagonal_sums(dbias):
    lk = dbias.shape[2]

    def body(d_ref, o_ref):
        xp = jnp.concatenate([d_ref[0], jnp.zeros((QROWS, SKEW - lk), F32)], axis=1)
        xv = xp[0:CHUNK]
        for q in range(1, QG):
            xv = xv + pltpu.roll(xp[q * CHUNK:(q + 1) * CHUNK], SKEW - q * CHUNK, 1)
        row = lax.broadcasted_iota(jnp.int32, (CHUNK, SKEW), 0)
        for bit in range(CHUNK.bit_length() - 1):
            xv = jnp.where((row >> bit) & 1 == 1, pltpu.roll(xv, SKEW - (1 << bit), 1), xv)
        o_ref[0] = jnp.sum(xv, axis=0, keepdims=True)

    return _call(body, name="diagonal_sums", grid=(H_B,),
                 in_specs=[pl.BlockSpec((1, QROWS, lk), lambda h: (h, 0, 0))],
                 out_specs=pl.BlockSpec((1, 1, SKEW), lambda h: (h, 0, 0)),
                 out_shape=jax.ShapeDtypeStruct((H_B, 1, SKEW), F32), args=[dbias], sem=("parallel",))


def _attn_common(s, n_back, gqa, q_col, k_col, v_col):
    lk = (QG + n_back) * CHUNK
    pad = n_back * CHUNK
    wide = TPS * LANES
    q_spec = pl.BlockSpec((QROWS, wide), lambda t, g: (g, q_col // TPS + t))
    if gqa:
        k_spec = pl.BlockSpec((s, LANES), lambda t, g: (0, k_col))
        v_spec = pl.BlockSpec((s, LANES), lambda t, g: (0, v_col))
    else:
        k_spec = pl.BlockSpec((s, wide), lambda t, g: (0, k_col // TPS + t))
        v_spec = pl.BlockSpec((s, wide), lambda t, g: (0, v_col // TPS + t))
    last_variant = _n_variants(n_back) - 1
    bias_spec = pl.BlockSpec((None, 2 * TPS, QROWS, lk), lambda t, g: (jnp.minimum(g, last_variant), t, 0, 0))
    tile_spec = pl.BlockSpec((QROWS, wide), lambda t, g: (g, t))
    return lk, pad, q_spec, k_spec, v_spec, bias_spec, tile_spec


def _attention_fwd(proj, bias, sinks, *, n_back, gqa, q_col, k_col, v_col, name, carry=None):
    s = proj.shape[0]
    lk, pad, q_spec, k_spec, v_spec, bias_spec, tile_spec = _attn_common(s, n_back, gqa, q_col, k_col, v_col)
    n_t, n_g = 512 // (TPS * LANES), s // QROWS
    kv_wide = LANES if gqa else TPS * LANES

    def body(*refs):
        if gqa:
            q_ref, k_ref, v_ref, bias_ref, sink_ref, o_ref, l_ref, kpad, vpad = refs
        else:
            q_ref, k_ref, v_ref, bias_ref, o_ref, l_ref, kpad, vpad = refs
        t, g = pl.program_id(0), pl.program_id(1)

        @pl.when(g == 0)
        def _():
            kpad[0:pad, :] = jnp.zeros((pad, kv_wide), BF16)
            vpad[0:pad, :] = jnp.zeros((pad, kv_wide), BF16)
            kpad[pad:, :] = k_ref[...]
            vpad[pad:, :] = v_ref[...]

        start = pl.multiple_of(g * QROWS, QROWS)
        half = lax.broadcasted_iota(jnp.int32, (QROWS, LANES), 1) // HEAD_DIM
        for tt in range(TPS):
            lanes = slice(tt * LANES, (tt + 1) * LANES)
            kv_lanes = slice(0, LANES) if gqa else lanes
            kb = kpad[pl.ds(start, lk), kv_lanes]
            vb = vpad[pl.ds(start, lk), kv_lanes]
            q = q_ref[:, lanes] * (HEAD_DIM ** -0.5)
            if gqa:
                hk = (TPS * t + tt) // 2
                q_rolled = pltpu.roll(q.astype(F32), HEAD_DIM, 1).astype(BF16)
            outs, lses = [], []
            for e in range(2):
                if gqa:
                    kv_half = hk
                    src = jnp.where(hk == e, q, q_rolled)
                else:
                    kv_half = e
                    src = q
                qm = jnp.where(half == kv_half, src, jnp.zeros_like(src))
                sc = _dot_nt(qm, kb) + bias_ref[2 * tt + e]
                m = jnp.max(sc, axis=-1, keepdims=True)
                if gqa:
                    sk = sink_ref[2 * (TPS * t + tt) + e]
                    m = jnp.maximum(m, sk)
                p = jnp.exp(sc - m)
                l = jnp.sum(p, axis=-1, keepdims=True)
                if gqa:
                    l = l + jnp.exp(sk - m)
                pn = p / l
                outs.append(_dot(pn.astype(BF16), vb))
                lses.append(m + jnp.log(l))
            if gqa:
                same = jnp.where(hk == 0, outs[0], outs[1])
                other = jnp.where(hk == 0, outs[1], outs[0])
                o_ref[:, lanes] = jnp.where(half == hk, same, pltpu.roll(other, HEAD_DIM, 1))
            else:
                o_ref[:, lanes] = jnp.where(half == 0, outs[0], outs[1])
            l_ref[:, lanes] = jnp.where(half == 0, lses[0], lses[1])

    in_specs = [q_spec, k_spec, v_spec, bias_spec] + ([SMEM_SPEC] if gqa else [])
    args = [proj, proj, proj, bias] + ([sinks] if gqa else [])
    o_shape = jax.ShapeDtypeStruct((s, 512), F32)
    return _call(body, name=name, grid=(n_t, n_g), in_specs=in_specs, out_specs=(tile_spec, tile_spec),
                 out_shape=(o_shape, o_shape), args=args,
                 scratch=[pltpu.VMEM((s + pad, kv_wide), BF16), pltpu.VMEM((s + pad, kv_wide), BF16)],
                 sem=("arbitrary", "arbitrary"), carry=carry)


def _attention_bwd(proj, bias, sinks, do, lse, *, n_back, gqa, q_col, k_col, v_col, name, carry=None):
    s = proj.shape[0]
    lk, pad, q_spec, k_spec, v_spec, bias_spec, tile_spec = _attn_common(s, n_back, gqa, q_col, k_col, v_col)
    n_t, n_g = 512 // (TPS * LANES), s // QROWS
    kv_wide = LANES if gqa else TPS * LANES

    def body(*refs):
        if gqa:
            (q_ref, k_ref, v_ref, bias_ref, sink_ref, do_ref, l_ref,
             dq_ref, dk_ref, dv_ref, dsink_ref, kpad, vpad, dkpad, dvpad) = refs
        else:
            (q_ref, k_ref, v_ref, bias_ref, do_ref, l_ref,
             dq_ref, dk_ref, dv_ref, dbias_ref, kpad, vpad, dkpad, dvpad) = refs
        t, g = pl.program_id(0), pl.program_id(1)

        @pl.when(g == 0)
        def _():
            kpad[0:pad, :] = jnp.zeros((pad, kv_wide), BF16)
            vpad[0:pad, :] = jnp.zeros((pad, kv_wide), BF16)
            kpad[pad:, :] = k_ref[...]
            vpad[pad:, :] = v_ref[...]
            if gqa:
                dsink_ref[...] = jnp.zeros_like(dsink_ref)
            else:
                dbias_ref[...] = jnp.zeros_like(dbias_ref)

        @pl.when((g == 0) & (t == 0) if gqa else g == 0)
        def _():
            dkpad[...] = jnp.zeros_like(dkpad)
            dvpad[...] = jnp.zeros_like(dvpad)

        start = pl.multiple_of(g * QROWS, QROWS)
        half = lax.broadcasted_iota(jnp.int32, (QROWS, LANES), 1) // HEAD_DIM
        for tt in range(TPS):
            lanes = slice(tt * LANES, (tt + 1) * LANES)
            kv_lanes = slice(0, LANES) if gqa else lanes
            kb = kpad[pl.ds(start, lk), kv_lanes]
            vb = vpad[pl.ds(start, lk), kv_lanes]
            q = q_ref[:, lanes]
            dov = do_ref[:, lanes]
            lv = l_ref[:, lanes]
            if gqa:
                hk = (TPS * t + tt) // 2
                q_rolled = pltpu.roll(q.astype(F32), HEAD_DIM, 1).astype(BF16)
                do_rolled = pltpu.roll(dov, HEAD_DIM, 1)
            dqs = []
            dk_acc = jnp.zeros((lk, LANES), F32)
            dv_acc = jnp.zeros((lk, LANES), F32)
            for e in range(2):
                if gqa:
                    kv_half = hk
                    src = jnp.where(hk == e, q, q_rolled)
                    do_src = jnp.where(hk == e, dov, do_rolled)
                else:
                    kv_half = e
                    src = q
                    do_src = dov
                qm = jnp.where(half == kv_half, src, jnp.zeros_like(src))
                dom = jnp.where(half == kv_half, do_src, 0.0).astype(BF16)
                lcol = jnp.max(jnp.where(half == e, lv, -jnp.inf), axis=-1, keepdims=True)
                sc = _dot_nt(qm * (HEAD_DIM ** -0.5), kb) + bias_ref[2 * tt + e]
                pn = jnp.exp(sc - lcol)
                dp = _dot_nt(dom, vb)
                delta = jnp.sum(pn * dp, axis=-1, keepdims=True)
                ds = pn * (dp - delta)
                if gqa:
                    p_sink = jnp.exp(sink_ref[2 * (TPS * t + tt) + e] - lcol)
                    dsk = -jnp.sum(p_sink * delta, axis=0, keepdims=True)
                    row = 2 * tt + e
                    dsink_ref[0, row:row + 1, :] += jnp.broadcast_to(dsk, (1, LANES))
                else:
                    dbias_ref[2 * tt + e] += ds
                dsb = (ds * (HEAD_DIM ** -0.5)).astype(BF16)
                dqs.append(_dot(dsb, kb))
                dk_acc = dk_acc + _dot_tn(dsb, qm)
                dv_acc = dv_acc + _dot_tn(pn.astype(BF16), dom)
            dkpad[pl.ds(start, lk), kv_lanes] += dk_acc
            dvpad[pl.ds(start, lk), kv_lanes] += dv_acc
            if gqa:
                same = jnp.where(hk == 0, dqs[0], dqs[1])
                other = jnp.where(hk == 0, dqs[1], dqs[0])
                dq_ref[:, lanes] = jnp.where(half == hk, same, pltpu.roll(other, HEAD_DIM, 1)).astype(BF16)
            else:
                dq_ref[:, lanes] = jnp.where(half == 0, dqs[0], dqs[1]).astype(BF16)

        @pl.when((g == n_g - 1) & (t == n_t - 1) if gqa else g == n_g - 1)
        def _():
            dk_ref[...] = dkpad[pad:, :].astype(BF16)
            dv_ref[...] = dvpad[pad:, :].astype(BF16)

    in_specs = [q_spec, k_spec, v_spec, bias_spec] + ([SMEM_SPEC] if gqa else []) + [tile_spec, tile_spec]
    args = [proj, proj, proj, bias] + ([sinks] if gqa else []) + [do, lse]
    if gqa:
        kv_out = pl.BlockSpec((s, LANES), lambda t, g: (0, 0))
        kv_shape = jax.ShapeDtypeStruct((s, LANES), BF16)
        extra_spec = pl.BlockSpec((1, 8, LANES), lambda t, g: (t, 0, 0))
        extra_shape = jax.ShapeDtypeStruct((n_t, 8, LANES), F32)
    else:
        kv_out = pl.BlockSpec((s, kv_wide), lambda t, g: (0, t))
        kv_shape = jax.ShapeDtypeStruct((s, 512), BF16)
        extra_spec = pl.BlockSpec((2 * TPS, QROWS, lk), lambda t, g: (t, 0, 0))
        extra_shape = jax.ShapeDtypeStruct(bias.shape[1:], F32)
    return _call(body, name=name, grid=(n_t, n_g), in_specs=in_specs,
                 out_specs=(tile_spec, kv_out, kv_out, extra_spec),
                 out_shape=(jax.ShapeDtypeStruct((s, 512), BF16), kv_shape, kv_shape, extra_shape), args=args,
                 scratch=[pltpu.VMEM((s + pad, kv_wide), BF16), pltpu.VMEM((s + pad, kv_wide), BF16),
                          pltpu.VMEM((s + pad, kv_wide), F32), pltpu.VMEM((s + pad, kv_wide), F32)],
                 sem=("arbitrary", "arbitrary"), carry=carry)


def _sum_slots(r, name):
    n_slots, rows, k = r.shape

    def body(r_ref, o_ref):
        acc = r_ref[0].astype(F32)
        for j in range(1, n_slots):
            acc = acc + r_ref[j].astype(F32)
        o_ref[...] = acc

    return _call(body, name=name, grid=(k // LANES,),
                 in_specs=[pl.BlockSpec((n_slots, rows, LANES), lambda i: (0, 0, i))],
                 out_specs=pl.BlockSpec((rows, LANES), lambda i: (0, i)),
                 out_shape=jax.ShapeDtypeStruct((rows, k), F32), args=[r], sem=("parallel",))


def _sum_rows8(g):
    n = g.shape[2]

    def body(g_ref, o_ref):
        acc = g_ref[0]
        for j in range(1, N_DEV):
            acc = acc + g_ref[j]
        o_ref[...] = acc

    return pl.pallas_call(
        body, name="sum_small_grads", in_specs=[VMEM_SPEC], out_specs=VMEM_SPEC,
        out_shape=jax.ShapeDtypeStruct((1, n), F32), compiler_params=_params(),
    )(g)


def _ada_weight_grad(sc_t, dmod_cols):
    d = sc_t.shape[0]
    w = dmod_cols.shape[1]
    td = _pick(d, (256, 128))

    def body(sc_ref, dm_ref, o_ref):
        scv = sc_ref[...]
        dmv = dm_ref[...]
        acc = scv[:, 0:1] * dmv[0:1, :]
        for b in range(1, N_DEV):
            acc = acc + scv[:, b:b + 1] * dmv[b:b + 1, :]
        o_ref[...] = acc

    return _call(body, name="ada_weight_grad", grid=(d // td,),
                 in_specs=[pl.BlockSpec((td, N_DEV), lambda i: (i, 0)), pl.BlockSpec((N_DEV, w), lambda i: (0, 0))],
                 out_specs=pl.BlockSpec((td, w), lambda i: (i, 0)), out_shape=jax.ShapeDtypeStruct((d, w), F32),
                 args=[sc_t, dmod_cols], sem=("parallel",))


def _adamw_update(w, gv, m, v):
    nm = ADAM_B1 * m + (1.0 - ADAM_B1) * gv
    nv = ADAM_B2 * v + (1.0 - ADAM_B2) * (gv * gv)
    m_hat = nm / (1.0 - ADAM_B1 ** ADAM_STEP)
    v_hat = nv / (1.0 - ADAM_B2 ** ADAM_STEP)
    return -ADAM_LR * (m_hat / (jnp.sqrt(v_hat) + ADAM_EPS) + ADAM_WD * w), nm, nv


def _adamw(w, g, m, v, name):
    rows, cols = w.shape
    tr = _pick(rows, (256, 176, 128, 88, 64)) if rows > 256 else rows

    def body(w_ref, g_ref, m_ref, v_ref, d_ref, nm_ref, nv_ref):
        d_ref[...], nm_ref[...], nv_ref[...] = _adamw_update(w_ref[...], g_ref[...], m_ref[...], v_ref[...])

    spec = pl.BlockSpec((tr, cols), lambda i: (i, 0))
    shape = jax.ShapeDtypeStruct((rows, cols), F32)
    return _call(body, name=name, grid=(rows // tr,), in_specs=[spec] * 4, out_specs=(spec, spec, spec),
                 out_shape=(shape, shape, shape), args=[w, g, m, v], sem=("parallel",))


def _adamw_from_slots(w, slots, m, v, name):
    n_slots, rows, k = slots.shape

    def body(s_ref, w_ref, m_ref, v_ref, g_ref, d_ref, nm_ref, nv_ref):
        gv = s_ref[0].astype(F32)
        for j in range(1, n_slots):
            gv = gv + s_ref[j].astype(F32)
        g_ref[...] = gv
        d_ref[...], nm_ref[...], nv_ref[...] = _adamw_update(w_ref[...], gv, m_ref[...], v_ref[...])

    tr = rows // 2 if rows % 32 == 0 else rows
    spec = pl.BlockSpec((tr, k), lambda i: (i, 0))
    shape = jax.ShapeDtypeStruct((rows, k), F32)
    return _call(body, name=name, grid=(rows // tr,),
                 in_specs=[pl.BlockSpec((n_slots, tr, k), lambda i: (0, i, 0)), spec, spec, spec],
                 out_specs=(spec, spec, spec, spec), out_shape=(shape, shape, shape, shape),
                 args=[slots, w, m, v], sem=("parallel",))


SMALL = ("b_ada", "g_pre_ffn1", "g_post_ffn1", "g_pre_mix", "b_in", "sinks_a", "rel_bias_b", "g_grp_a",
         "g_grp_b", "b_out", "g_post_mix", "g_pre_ffn2", "g_post_ffn2")
WEIGHTS = ("w_ada", "b_ada", "g_pre_ffn1", "w_gate1", "w_up1", "w_down1", "g_post_ffn1", "g_pre_mix", "w_in",
           "b_in", "sinks_a", "rel_bias_b", "g_grp_a", "g_grp_b", "w_out", "b_out", "g_post_mix", "g_pre_ffn2",
           "w_gate2", "w_up2", "w_down2", "g_post_ffn2")


def kernel(x, c, w_ada, b_ada, g_pre_ffn1, w_gate1, w_up1, w_down1, g_post_ffn1, g_pre_mix, w_in, b_in, sinks_a, rel_bias_b, g_grp_a, g_grp_b, w_out, b_out, g_post_mix, g_pre_ffn2, w_gate2, w_up2, w_down2, g_post_ffn2, loss_target, m_w_ada, m_b_ada, m_g_pre_ffn1, m_w_gate1, m_w_up1, m_w_down1, m_g_post_ffn1, m_g_pre_mix, m_w_in, m_b_in, m_sinks_a, m_rel_bias_b, m_g_grp_a, m_g_grp_b, m_w_out, m_b_out, m_g_post_mix, m_g_pre_ffn2, m_w_gate2, m_w_up2, m_w_down2, m_g_post_ffn2, v_w_ada, v_b_ada, v_g_pre_ffn1, v_w_gate1, v_w_up1, v_w_down1, v_g_post_ffn1, v_g_pre_mix, v_w_in, v_b_in, v_sinks_a, v_rel_bias_b, v_g_grp_a, v_g_grp_b, v_w_out, v_b_out, v_g_post_mix, v_g_pre_ffn2, v_w_gate2, v_w_up2, v_w_down2, v_g_post_ffn2):
    given = dict(locals())
    weights = {n: given[n] for n in WEIGHTS}
    mom_m = {n: given["m_" + n] for n in WEIGHTS}
    mom_v = {n: given["v_" + n] for n in WEIGHTS}

    me = 4 * lax.axis_index("x") + 2 * lax.axis_index("y") + lax.axis_index("c")
    xs = x[0]
    tgt = loss_target[0]
    d_model = xs.shape[1]
    ada_cols = w_ada.shape[2]

    sh = {"wg1": w_gate1[0].T, "wu1": w_up1[0].T, "wd1": w_down1[0], "win": w_in[0].T, "wo": w_out[0],
          "wg2": w_gate2[0].T, "wu2": w_up2[0].T, "wd2": w_down2[0]}
    sh = {k: v.astype(BF16) for k, v in sh.items()}

    def gather(*names):
        return _gather_carry([sh[n] for n in names])

    bias_a = _alibi_bias()
    rel_m = _rel_index_matrix()
    rel_vec = jnp.dot(rel_bias_b[0], rel_m.T, precision=lax.Precision.HIGHEST)
    bias_b, (wg1,) = _toeplitz_bias(rel_vec.reshape(H_B, 1, SKEW), carry=gather("wg1"))

    b_cols = lax.dynamic_slice(b_ada, (0, me * ada_cols), (1, ada_cols))
    (sc_all, mod_rows), (wu1,) = _ada_forward(c, w_ada[0], b_cols, gather("wu1"))
    mod = mod_rows.reshape(N_MOD, d_model)
    shift1, scale1, gate1, shift2, scale2, gate2, shift3, scale3, gate3 = (mod[i:i + 1] for i in range(N_MOD))

    h1 = _pre_norm(xs, g_pre_ffn1, scale1, shift1, "pre_norm_ffn1")
    (a1, b1, u1), (wd1,) = _ffn_up(h1, wg1, wu1, "ffn_up_ffn1", carry=gather("wd1"))
    y1, (win,) = _mm_nn([(u1, wd1)], "ffn_down_ffn1", F32, carry=gather("win"))
    x1, h2 = _post_pre_norm(xs, y1, g_post_ffn1, gate1, 0.5, g_pre_mix, scale2, shift2, "post_ffn1_pre_mix")

    proj, (wo,) = _mm_nt(h2, win, "in_proj", BF16, bias=b_in, carry=gather("wo"))
    sinks = sinks_a[0]
    cfg_a = dict(n_back=BACK_A, gqa=True, q_col=0, k_col=QA // LANES, v_col=(QA + KVA) // LANES)
    cfg_b = dict(n_back=BACK_B, gqa=False, q_col=(QA + 2 * KVA) // LANES, k_col=(QA + 2 * KVA + QB) // LANES,
                 v_col=(QA + 2 * KVA + 2 * QB) // LANES)
    (oa, lse_a), (wg2,) = _attention_fwd(proj, bias_a, sinks, name="attn_a", carry=gather("wg2"), **cfg_a)
    (ob, lse_b), (wu2,) = _attention_fwd(proj, bias_b, None, name="attn_b", carry=gather("wu2"), **cfg_b)
    ycat = _group_norm_cat(oa, ob, g_grp_a, g_grp_b)
    ymix = _mm_nn([(ycat, wo)], "out_proj", F32, bias=b_out)
    x2, h3 = _post_pre_norm(x1, ymix, g_post_mix, gate2, 1.0, g_pre_ffn2, scale3, shift3, "post_mix_pre_ffn2")

    (a3, b3, u3), (wd2,) = _ffn_up(h3, wg2, wu2, "ffn_up_ffn2", carry=gather("wd2"))
    y3 = _mm_nn([(u3, wd2)], "ffn_down_ffn2", F32)

    def scatter(*grads):
        return _scatter_carry(list(grads))

    slots = {}

    dx3, dy, loss_part, s1 = _post_norm_loss_bwd(x2, y3, g_post_ffn2, gate3, 0.5, tgt, "post_ffn2_loss_bwd")
    da, db = _ffn_down_bwd(dy, wd2, a3, b3, "ffn_down_bwd_ffn2")
    dwd2 = _mm_tn_pair(u3, dy, "grad_wd_ffn2")
    dwg2 = _mm_tn_pair(da, h3, "grad_wg_ffn2")
    dwu2 = _mm_tn_pair(db, h3, "grad_wu_ffn2")
    dh, (slots["wd2"],) = _mm_nn([(da, wg2), (db, wu2)], "ffn_up_bwd_ffn2", F32, carry=scatter(dwd2))
    dx2, dymix, s2, s3, s1m, db_out = _pre_post_norm_bwd(dh, x2, g_pre_ffn2, scale3, dx3, ymix, g_post_mix, gate2,
                                                         1.0, "pre_ffn2_post_mix_bwd")
    sm3 = dict(shift=s3, scale=s2 * g_pre_ffn2, gate=0.5 * g_post_ffn2 * s1,
               g_pre=(1.0 + scale3) * s2, g_post=(0.5 * gate3) * s1)

    dycat = _mm_nt(dymix, wo, "out_proj_bwd", F32)
    dwo = _mm_tn_pair(ycat, dymix, "grad_wo")
    doa, dob, dg_a, dg_b = _group_norm_bwd(dycat, oa, ob, g_grp_a, g_grp_b)
    (dqa, dka, dva, dsink), (slots["wg2"],) = _attention_bwd(
        proj, bias_a, sinks, doa, lse_a, name="attn_a_bwd", carry=scatter(dwg2), **cfg_a)
    (dqb, dkb, dvb, dbias), (slots["wu2"], slots["wo"]) = _attention_bwd(
        proj, bias_b, None, dob, lse_b, name="attn_b_bwd", carry=scatter(dwu2, dwo), **cfg_b)
    dproj = jnp.concatenate([dqa, dka, dva, dqb, dkb, dvb], axis=1)
    db_in = _col_sum(dproj, "grad_b_in")
    dwin = _mm_tn_pair(dproj, h2, "grad_win")
    dh2 = _mm_nn([(dproj, win)], "in_proj_bwd", F32)
    dx1, dy, s2m, s3m, s1, _ = _pre_post_norm_bwd(dh2, x1, g_pre_mix, scale2, dx2, y1, g_post_ffn1, gate1, 0.5,
                                                  "pre_mix_post_ffn1_bwd")
    d_rel = jnp.dot(_diagonal_sums(dbias).reshape(H_B, SKEW), rel_m, precision=lax.Precision.HIGHEST)
    d_sinks = dsink[:, :2 * TPS, 0].reshape(1, H_A)

    (da, db), (slots["win"],) = _ffn_down_bwd(dy, wd1, a1, b1, "ffn_down_bwd_ffn1", carry=scatter(dwin))
    dwd1 = _mm_tn_pair(u1, dy, "grad_wd_ffn1")
    dwg1, (slots["wd1"],) = _mm_tn_pair(da, h1, "grad_wg_ffn1", carry=scatter(dwd1))
    dwu1, (slots["wg1"],) = _mm_tn_pair(db, h1, "grad_wu_ffn1", carry=scatter(dwg1))
    dh, (slots["wu1"],) = _mm_nn([(da, wg1), (db, wu1)], "ffn_up_bwd_ffn1", F32, carry=scatter(dwu1))
    dx0, s2, s3 = _pre_norm_bwd(dh, xs, g_pre_ffn1, scale1, dx1, "pre_norm_bwd_ffn1")
    sm1 = dict(shift=s3, scale=s2 * g_pre_ffn1, gate=0.5 * g_post_ffn1 * s1,
               g_pre=(1.0 + scale1) * s2, g_post=(0.5 * gate1) * s1)

    dmod = jnp.concatenate([sm1["shift"], sm1["scale"], sm1["gate"],
                            s3m, s2m * g_pre_mix, g_post_mix * s1m,
                            sm3["shift"], sm3["scale"], sm3["gate"]], axis=1)
    small_parts = {
        "b_ada": dmod, "g_pre_ffn1": sm1["g_pre"], "g_post_ffn1": sm1["g_post"],
        "g_pre_mix": (1.0 + scale2) * s2m, "b_in": db_in, "sinks_a": d_sinks,
        "rel_bias_b": d_rel.reshape(1, H_B * N_REL), "g_grp_a": dg_a, "g_grp_b": dg_b, "b_out": db_out,
        "g_post_mix": gate2 * s1m, "g_pre_ffn2": sm3["g_pre"], "g_post_ffn2": sm3["g_post"]}
    sizes = [small_parts[n].shape[1] for n in SMALL]
    n_small = sum(sizes)
    n_pad = -(n_small + 1) % LANES
    packed = jnp.concatenate([small_parts[n] for n in SMALL] + [loss_part, jnp.zeros((1, n_pad), F32)], axis=1)
    gathered = _all_gather_small(packed)
    small_sum = _sum_rows8(gathered)
    loss = small_sum[0, n_small]
    dmod_cols = lax.dynamic_slice(gathered.reshape(N_DEV, n_small + 1 + n_pad), (0, me * ada_cols),
                                  (N_DEV, ada_cols))
    g_ada = _ada_weight_grad(sc_all.reshape(N_DEV, d_model).T, dmod_cols)

    out_g, out_d, out_m, out_v = {}, {}, {}, {}
    d_, m_, v_ = _adamw(w_ada[0], g_ada, m_w_ada[0], v_w_ada[0], "adamw_w_ada")
    out_g["w_ada"], out_d["w_ada"], out_m["w_ada"], out_v["w_ada"] = g_ada[None], d_[None], m_[None], v_[None]
    for n, key, transposed in (("w_gate1", "wg1", True), ("w_up1", "wu1", True), ("w_down1", "wd1", False),
                               ("w_in", "win", True), ("w_out", "wo", False), ("w_gate2", "wg2", True),
                               ("w_up2", "wu2", True), ("w_down2", "wd2", False)):
        view = (lambda t: t.T) if transposed else (lambda t: t)
        res = _adamw_from_slots(view(weights[n][0]), slots[key], view(mom_m[n][0]), view(mom_v[n][0]),
                                "adamw_" + n)
        out_g[n], out_d[n], out_m[n], out_v[n] = (view(t)[None] for t in res)

    def pack(tree):
        return jnp.concatenate([tree[n].reshape(1, -1) for n in SMALL], axis=1)

    g_small = small_sum[:, :n_small]
    d_s, m_s, v_s = _adamw(pack(weights), g_small, pack(mom_m), pack(mom_v), "adamw_small")
    off = 0
    for n, size in zip(SMALL, sizes):
        shape = weights[n].shape
        out_g[n] = g_small[:, off:off + size].reshape(shape)
        out_d[n] = d_s[:, off:off + size].reshape(shape)
        out_m[n] = m_s[:, off:off + size].reshape(shape)
        out_v[n] = v_s[:, off:off + size].reshape(shape)
        off += size

    return (loss, dx0[None], *[out_g[n] for n in WEIGHTS], *[out_d[n] for n in WEIGHTS],
            *[out_m[n] for n in WEIGHTS], *[out_v[n] for n in WEIGHTS])
```

```python
import numpy as np
import jax
import jax.numpy as jnp
from jax import lax
from jax.experimental import pallas as pl
from jax.experimental.pallas import tpu as pltpu

F32 = jnp.float32
BF16 = jnp.bfloat16
MESH = pl.DeviceIdType.MESH
ANY = pl.BlockSpec(memory_space=pl.ANY)
VMEM_SPEC = pl.BlockSpec(memory_space=pltpu.VMEM)
SMEM_SPEC = pl.BlockSpec(memory_space=pltpu.SMEM)

N_DEV = 8
CHUNK = 64
HEAD_DIM = 64
LANES = 128
H_A, KV_A, H_B = 8, 2, 8
BACK_A, BACK_B = 2, 8
REL_CLIP = 128
N_REL = 2 * REL_CLIP + 1
QA, KVA, QB = H_A * HEAD_DIM, KV_A * HEAD_DIM, H_B * HEAD_DIM
D_IN = QA + 2 * KVA + 3 * QB
N_MOD = 9
EPS = 1e-6
NEG_INF = -1e30
QG = 4
QROWS = QG * CHUNK
TPS = 2
SKEW = 1024
ADAM_LR, ADAM_B1, ADAM_B2, ADAM_EPS, ADAM_WD, ADAM_STEP = 0.001, 0.9, 0.999, 1e-08, 0.01, 10
VMEM_LIMIT = 56 * 2 ** 20


def _pick(n, cands):
    for c in cands:
        if n % c == 0:
            return c
    return n


def _params(sem=None):
    return pltpu.CompilerParams(dimension_semantics=sem, vmem_limit_bytes=VMEM_LIMIT)


def _dot_nt(a, b):
    return lax.dot_general(a, b, (((1,), (1,)), ((), ())), preferred_element_type=F32)


def _dot_tn(a, b):
    return lax.dot_general(a, b, (((0,), (0,)), ((), ())), preferred_element_type=F32)


def _dot(a, b):
    return jnp.dot(a, b, preferred_element_type=F32)


def _sigmoid(a):
    return 0.5 * (jnp.tanh(0.5 * a) + 1.0)


def _mesh_pos():
    return lax.axis_index("x"), lax.axis_index("y"), lax.axis_index("c")


def _peer(x, y, c, r):
    px = 1 - x if r & 4 else x
    py = 1 - y if r & 2 else y
    pc = 1 - c if r & 1 else c
    return px, py, pc


class _Carry:
    def __init__(self, ins, out_shapes, scratch, start, finish):
        self.ins, self.out_shapes, self.scratch = list(ins), list(out_shapes), list(scratch)
        self.start, self.finish = start, finish


def _call(body, *, name, grid, in_specs, out_specs, out_shape, args, scratch=(), sem=None, carry=None):
    single = not isinstance(out_shape, (tuple, list))
    out_specs = (out_specs,) if single else tuple(out_specs)
    out_shape = (out_shape,) if single else tuple(out_shape)
    if carry is None:
        res = pl.pallas_call(body, name=name, grid=grid, in_specs=list(in_specs), out_specs=out_specs,
                             out_shape=out_shape, scratch_shapes=list(scratch), compiler_params=_params(sem))(*args)
        return res[0] if single else res
    n_in, n_out, n_s = len(in_specs), len(out_shape), len(scratch)
    ci, co = len(carry.ins), len(carry.out_shapes)

    def wrapped(*refs):
        ins, cins = refs[:n_in], refs[n_in:n_in + ci]
        outs = refs[n_in + ci:n_in + ci + n_out]
        couts = refs[n_in + ci + n_out:n_in + ci + n_out + co]
        scr = refs[n_in + ci + n_out + co:n_in + ci + n_out + co + n_s]
        cscr = refs[n_in + ci + n_out + co + n_s:]
        first, last = None, None
        for ax, n in enumerate(grid):
            f, l = pl.program_id(ax) == 0, pl.program_id(ax) == n - 1
            first = f if first is None else first & f
            last = l if last is None else last & l
        pl.when(first)(lambda: carry.start(cins, couts, cscr))
        body(*ins, *outs, *scr)
        pl.when(last)(lambda: carry.finish(cins, couts, cscr))

    res = pl.pallas_call(
        wrapped, name=name, grid=grid, in_specs=list(in_specs) + [ANY] * ci, out_specs=out_specs + (ANY,) * co,
        out_shape=out_shape + tuple(carry.out_shapes), scratch_shapes=list(scratch) + carry.scratch,
        compiler_params=_params(("arbitrary",) * len(grid)))(*args, *carry.ins)
    main = res[:n_out]
    return (main[0] if single else main), res[n_out:]


def _gather_carry(shards):
    n_w = len(shards)
    rows = [s.shape[0] for s in shards]

    def plan(ins, outs, scr):
        send_sems, recv_sems, local_sems = scr
        x, y, c = _mesh_pos()
        me, sibling = (x, y, c), (x, y, 1 - c)
        chips = [(1 - x, y), (x, 1 - y), (1 - x, 1 - y)]

        def block(w, dev):
            start = pl.multiple_of((4 * dev[0] + 2 * dev[1] + dev[2]) * rows[w], 16)
            return outs[w].at[pl.ds(start, rows[w]), :]

        def copy(w, k, dev, to, src=None):
            return pltpu.make_async_remote_copy(
                src_ref=block(w, dev) if src is None else src, dst_ref=block(w, dev),
                send_sem=send_sems.at[w, k], recv_sem=recv_sems.at[w, k], device_id=to, device_id_type=MESH)

        mine = [pltpu.make_async_copy(ins[w], block(w, me), local_sems.at[w]) for w in range(n_w)]
        first = []
        for j, chip in enumerate(chips):
            first += [copy(w, 1 + j, me, (*chip, c), src=ins[w]) for w in range(n_w)]
        first += [copy(w, 0, me, sibling, src=ins[w]) for w in range(n_w)]
        return c, me, sibling, chips, copy, mine, first

    def start(ins, outs, scr):
        _, _, _, _, _, mine, first = plan(ins, outs, scr)
        for cp in mine + first:
            cp.start()

    def finish(ins, outs, scr):
        c, me, sibling, chips, copy, mine, first = plan(ins, outs, scr)
        passed = []
        for j, chip in enumerate(chips):
            for w in range(n_w):
                copy(w, 1 + j, (*chip, c), me).wait_recv()
                cp = copy(w, 4 + j, (*chip, c), sibling)
                cp.start()
                passed.append(cp)
        for w in range(n_w):
            copy(w, 0, sibling, me).wait_recv()
        for j, chip in enumerate(chips):
            for w in range(n_w):
                copy(w, 4 + j, (*chip, 1 - c), me).wait_recv()
        for cp in first + passed:
            cp.wait_send()
        for cp in mine:
            cp.wait()

    return _Carry(
        shards, [jax.ShapeDtypeStruct((N_DEV * s.shape[0], s.shape[1]), s.dtype) for s in shards],
        [pltpu.SemaphoreType.DMA((n_w, N_DEV - 1)), pltpu.SemaphoreType.DMA((n_w, N_DEV - 1)),
         pltpu.SemaphoreType.DMA((n_w,))], start, finish)


def _scatter_carry(parts):
    n_w = len(parts)
    n_chip = N_DEV // 2
    rows = [g.shape[0] // n_chip for g in parts]

    def plan(ins, outs, scr):
        send_sems, recv_sems, local_sems = scr
        x, y, c = _mesh_pos()

        def src(w, chip_index):
            return ins[w].at[pl.ds(pl.multiple_of(chip_index * rows[w], 16), rows[w]), :]

        mine = [pltpu.make_async_copy(src(w, 2 * x + y), outs[w].at[0], local_sems.at[w]) for w in range(n_w)]
        copies = []
        for r in (3, 2, 1):
            px, py, _ = _peer(x, y, c, 2 * r)
            for w in range(n_w):
                copies.append(pltpu.make_async_remote_copy(
                    src_ref=src(w, 2 * px + py), dst_ref=outs[w].at[r], send_sem=send_sems.at[w, r - 1],
                    recv_sem=recv_sems.at[w, r - 1], device_id=(px, py, c), device_id_type=MESH))
        return mine, copies

    def start(ins, outs, scr):
        mine, copies = plan(ins, outs, scr)
        for cp in mine + copies:
            cp.start()

    def finish(ins, outs, scr):
        mine, copies = plan(ins, outs, scr)
        for cp in copies:
            cp.wait_recv()
        for cp in copies:
            cp.wait_send()
        for cp in mine:
            cp.wait()

    return _Carry(
        parts, [jax.ShapeDtypeStruct((n_chip, r, g.shape[1]), g.dtype) for r, g in zip(rows, parts)],
        [pltpu.SemaphoreType.DMA((n_w, n_chip - 1)), pltpu.SemaphoreType.DMA((n_w, n_chip - 1)),
         pltpu.SemaphoreType.DMA((n_w,))], start, finish)


def _ada_forward(c_row, w_ada, b_cols, carry):
    d = c_row.shape[1]
    wcols = w_ada.shape[1]
    ci, co = len(carry.ins), len(carry.out_shapes)

    def body(*refs):
        c_ref, w_ref, b_ref = refs[:3]
        cins = refs[3:3 + ci]
        sc_ref, mod_ref = refs[3 + ci:5 + ci]
        couts = refs[5 + ci:5 + ci + co]
        rows_ref, send_sems, recv_sems = refs[5 + ci + co:8 + ci + co]
        cscr = refs[8 + ci + co:]
        carry.start(cins, couts, cscr)
        x, y, c = _mesh_pos()
        me = 4 * x + 2 * y + c
        cv = c_ref[...]
        sc_ref[me] = cv * _sigmoid(cv)

        sends = []
        for r in range(1, N_DEV):
            px, py, pc = _peer(x, y, c, r)
            cp = pltpu.make_async_remote_copy(
                src_ref=sc_ref.at[me], dst_ref=sc_ref.at[me], send_sem=send_sems.at[0, r - 1],
                recv_sem=recv_sems.at[0, r - 1], device_id=(px, py, pc), device_id_type=MESH)
            cp.start()
            sends.append(cp)
        for r in range(1, N_DEV):
            px, py, pc = _peer(x, y, c, r)
            pid = 4 * px + 2 * py + pc
            pltpu.make_async_remote_copy(
                src_ref=sc_ref.at[pid], dst_ref=sc_ref.at[pid], send_sem=send_sems.at[0, r - 1],
                recv_sem=recv_sems.at[0, r - 1], device_id=(px, py, pc), device_id_type=MESH).wait_recv()
        for cp in sends:
            cp.wait_send()

        sc_all = jnp.concatenate([sc_ref[j] for j in range(N_DEV)], axis=0)
        rows = _dot(sc_all.astype(BF16), w_ref[...].astype(BF16)) + b_ref[...]
        for j in range(N_DEV):
            rows_ref[j] = rows[j:j + 1, :]
        mod_ref[me] = rows_ref[me]

        sends = []
        for r in range(1, N_DEV):
            px, py, pc = _peer(x, y, c, r)
            pid = 4 * px + 2 * py + pc
            cp = pltpu.make_async_remote_copy(
                src_ref=rows_ref.at[pid], dst_ref=mod_ref.at[me], send_sem=send_sems.at[1, r - 1],
                recv_sem=recv_sems.at[1, r - 1], device_id=(px, py, pc), device_id_type=MESH)
            cp.start()
            sends.append(cp)
        for r in range(1, N_DEV):
            px, py, pc = _peer(x, y, c, r)
            pid = 4 * px + 2 * py + pc
            pltpu.make_async_remote_copy(
                src_ref=rows_ref.at[pid], dst_ref=mod_ref.at[pid], send_sem=send_sems.at[1, r - 1],
                recv_sem=recv_sems.at[1, r - 1], device_id=(px, py, pc), device_id_type=MESH).wait_recv()
        for cp in sends:
            cp.wait_send()
        carry.finish(cins, couts, cscr)

    res = pl.pallas_call(
        body, name="ada_forward",
        out_shape=(jax.ShapeDtypeStruct((N_DEV, 1, d), F32), jax.ShapeDtypeStruct((N_DEV, 1, wcols), F32),
                   *carry.out_shapes),
        in_specs=[VMEM_SPEC, VMEM_SPEC, VMEM_SPEC] + [ANY] * ci, out_specs=(VMEM_SPEC, VMEM_SPEC) + (ANY,) * co,
        scratch_shapes=[pltpu.VMEM((N_DEV, 1, wcols), F32), pltpu.SemaphoreType.DMA((2, N_DEV - 1)),
                        pltpu.SemaphoreType.DMA((2, N_DEV - 1))] + carry.scratch,
        compiler_params=_params(),
    )(c_row, w_ada, b_cols, *carry.ins)
    return res[:2], res[2:]


def _all_gather_small(v):
    n = v.shape[1]

    def body(v_ref, out_ref, send_sems, recv_sems):
        x, y, c = _mesh_pos()
        me = 4 * x + 2 * y + c
        out_ref[me] = v_ref[...]
        sends = []
        for r in range(1, N_DEV):
            px, py, pc = _peer(x, y, c, r)
            cp = pltpu.make_async_remote_copy(
                src_ref=v_ref, dst_ref=out_ref.at[me], send_sem=send_sems.at[r - 1],
                recv_sem=recv_sems.at[r - 1], device_id=(px, py, pc), device_id_type=MESH)
            cp.start()
            sends.append(cp)
        for r in range(1, N_DEV):
            px, py, pc = _peer(x, y, c, r)
            pid = 4 * px + 2 * py + pc
            pltpu.make_async_remote_copy(
                src_ref=v_ref, dst_ref=out_ref.at[pid], send_sem=send_sems.at[r - 1],
                recv_sem=recv_sems.at[r - 1], device_id=(px, py, pc), device_id_type=MESH).wait_recv()
        for cp in sends:
            cp.wait_send()

    return pl.pallas_call(
        body, name="all_gather_small",
        out_shape=jax.ShapeDtypeStruct((N_DEV, 1, n), F32),
        in_specs=[VMEM_SPEC], out_specs=VMEM_SPEC,
        scratch_shapes=[pltpu.SemaphoreType.DMA((N_DEV - 1,)), pltpu.SemaphoreType.DMA((N_DEV - 1,))],
        compiler_params=_params(),
    )(v)


def _mm_nt(a, b, name, out_dtype, bias=None, carry=None):
    m, k = a.shape
    n = b.shape[0]
    tm = _pick(m, (512, 256, 128))
    tn = _pick(n, (1408, 1152, 1024, 768, 512, 256, 128))

    def body(*refs):
        acc = _dot_nt(refs[0][...], refs[1][...])
        if bias is not None:
            acc = acc + refs[2][...]
        refs[-1][...] = acc.astype(out_dtype)

    in_specs = [pl.BlockSpec((tm, k), lambda i, j: (i, 0)), pl.BlockSpec((tn, k), lambda i, j: (j, 0))]
    args = [a, b]
    if bias is not None:
        in_specs.append(pl.BlockSpec((1, tn), lambda i, j: (0, j)))
        args.append(bias)
    return _call(body, name=name, grid=(m // tm, n // tn), in_specs=in_specs,
                 out_specs=pl.BlockSpec((tm, tn), lambda i, j: (i, j)),
                 out_shape=jax.ShapeDtypeStruct((m, n), out_dtype), args=args,
                 sem=("parallel", "parallel"), carry=carry)


def _mm_nn(pairs, name, out_dtype, bias=None, carry=None):
    m, k = pairs[0][0].shape
    n = pairs[0][1].shape[1]
    n_p = len(pairs)
    tm = _pick(m, (512, 256, 128))
    tk = k if n_p * k * n <= 6 * 2 ** 20 else _pick(k, (1408, 1152, 1024, 768, 512, 256, 128))
    nk = k // tk

    def body(*refs):
        o_ref, acc_ref = refs[-2], refs[-1]
        kk = pl.program_id(1)

        if nk == 1:
            acc = _dot(refs[0][...], refs[1][...])
            for p in range(1, n_p):
                acc = acc + _dot(refs[2 * p][...], refs[2 * p + 1][...])
            if bias is not None:
                acc = acc + refs[2 * n_p][...]
            o_ref[...] = acc.astype(out_dtype)
            return

        @pl.when(kk == 0)
        def _():
            acc_ref[...] = jnp.zeros_like(acc_ref)

        for p in range(n_p):
            acc_ref[...] += _dot(refs[2 * p][...], refs[2 * p + 1][...])

        @pl.when(kk == nk - 1)
        def _():
            acc = acc_ref[...]
            if bias is not None:
                acc = acc + refs[2 * n_p][...]
            o_ref[...] = acc.astype(out_dtype)

    in_specs, args = [], []
    for a, b in pairs:
        in_specs += [pl.BlockSpec((tm, tk), lambda i, kk: (i, kk)), pl.BlockSpec((tk, n), lambda i, kk: (kk, 0))]
        args += [a, b]
    if bias is not None:
        in_specs.append(pl.BlockSpec((1, n), lambda i, kk: (0, 0)))
        args.append(bias)
    return _call(body, name=name, grid=(m // tm, nk), in_specs=in_specs,
                 out_specs=pl.BlockSpec((tm, n), lambda i, kk: (i, 0)),
                 out_shape=jax.ShapeDtypeStruct((m, n), out_dtype), args=args,
                 scratch=[pltpu.VMEM((tm, n) if nk > 1 else (8, LANES), F32)], sem=("parallel", "arbitrary"),
                 carry=carry)


def _mm_tn(a, b, name, out_dtype=BF16, carry=None):
    k, m = a.shape
    n = b.shape[1]
    tm = _pick(m, (1408, 1152, 1024, 768, 512, 256, 128))
    tk = _pick(k, (512, 256, 128))
    nk = k // tk

    def body(a_ref, b_ref, o_ref, acc_ref):
        kk = pl.program_id(1)

        @pl.when(kk == 0)
        def _():
            acc_ref[...] = jnp.zeros_like(acc_ref)

        acc_ref[...] += _dot_tn(a_ref[...], b_ref[...])

        @pl.when(kk == nk - 1)
        def _():
            o_ref[...] = acc_ref[...].astype(out_dtype)

    return _call(body, name=name, grid=(m // tm, nk),
                 in_specs=[pl.BlockSpec((tk, tm), lambda i, kk: (kk, i)), pl.BlockSpec((tk, n), lambda i, kk: (kk, 0))],
                 out_specs=pl.BlockSpec((tm, n), lambda i, kk: (i, 0)),
                 out_shape=jax.ShapeDtypeStruct((m, n), out_dtype), args=[a, b],
                 scratch=[pltpu.VMEM((tm, n), F32)], sem=("parallel", "arbitrary"), carry=carry)


def _mm_tn_pair(a, b, name, carry=None):
    k, m = a.shape
    n = b.shape[1]
    rows = m // N_DEV
    n_chip = N_DEV // 2
    tm = 4 * rows
    tk = _pick(k, (2048, 1024, 512, 256, 128))
    nk = k // tk

    def body(a_ref, b_ref, p_ref, acc_ref, keep_ref, send_ref, land_ref, send_sems, recv_sems):
        i, kk = pl.program_id(0), pl.program_id(1)
        x, y, c = _mesh_pos()

        def push(chip):
            return pltpu.make_async_remote_copy(
                src_ref=send_ref.at[chip], dst_ref=land_ref.at[chip], send_sem=send_sems.at[chip],
                recv_sem=recv_sems.at[chip], device_id=(x, y, 1 - c), device_id_type=MESH)

        if nk == 1:
            acc = _dot_tn(a_ref[...], b_ref[...])
        else:
            @pl.when(kk == 0)
            def _():
                acc_ref[...] = jnp.zeros_like(acc_ref)

            acc_ref[...] += _dot_tn(a_ref[...], b_ref[...])
            acc = acc_ref

        for t in range(2):
            @pl.when((kk == nk - 1) & (i == t))
            def _(t=t):
                for ob in range(4):
                    chip, core = 2 * t + ob // 2, ob % 2
                    blk = acc[ob * rows:(ob + 1) * rows, :]

                    @pl.when(c == core)
                    def _(chip=chip, blk=blk):
                        keep_ref[chip] = blk

                    @pl.when(c != core)
                    def _(chip=chip, blk=blk):
                        send_ref[chip] = blk.astype(BF16)
                        push(chip).start()

        @pl.when((kk == nk - 1) & (i == 1))
        def _():
            for chip in range(n_chip):
                push(chip).wait_recv()
                p_ref[chip * rows:(chip + 1) * rows, :] = (
                    keep_ref[chip] + land_ref[chip].astype(F32)).astype(BF16)
            for chip in range(n_chip):
                push(chip).wait_send()

    return _call(body, name=name, grid=(2, nk),
                 in_specs=[pl.BlockSpec((tk, tm), lambda i, kk: (kk, i)), pl.BlockSpec((tk, n), lambda i, kk: (kk, 0))],
                 out_specs=pl.BlockSpec((n_chip * rows, n), lambda i, kk: (0, 0)),
                 out_shape=jax.ShapeDtypeStruct((n_chip * rows, n), BF16), args=[a, b],
                 scratch=[pltpu.VMEM((tm, n) if nk > 1 else (8, LANES), F32), pltpu.VMEM((n_chip, rows, n), F32),
                          pltpu.VMEM((n_chip, rows, n), BF16), pltpu.VMEM((n_chip, rows, n), BF16),
                          pltpu.SemaphoreType.DMA((n_chip,)), pltpu.SemaphoreType.DMA((n_chip,))],
                 sem=("arbitrary", "arbitrary"), carry=carry)


def _ffn_up(h, wg_t, wu_t, name, carry=None):
    s, d = h.shape
    f = wg_t.shape[0]
    tm = _pick(s, (512, 256, 128))
    tf = _pick(f, (1408, 1024, 512, 256, 128))

    def body(h_ref, wg_ref, wu_ref, a_ref, b_ref, u_ref):
        hh = h_ref[...]
        a = _dot_nt(hh, wg_ref[...])
        b = _dot_nt(hh, wu_ref[...])
        a_ref[...] = a.astype(BF16)
        b_ref[...] = b.astype(BF16)
        u_ref[...] = ((a * _sigmoid(a)) * b).astype(BF16)

    w_spec = pl.BlockSpec((tf, d), lambda i, j: (j, 0))
    o_spec = pl.BlockSpec((tm, tf), lambda i, j: (i, j))
    o_shape = jax.ShapeDtypeStruct((s, f), BF16)
    return _call(body, name=name, grid=(s // tm, f // tf),
                 in_specs=[pl.BlockSpec((tm, d), lambda i, j: (i, 0)), w_spec, w_spec],
                 out_specs=(o_spec, o_spec, o_spec), out_shape=(o_shape, o_shape, o_shape),
                 args=[h, wg_t, wu_t], sem=("parallel", "parallel"), carry=carry)


def _ffn_down_bwd(dy, wd, a, b, name, carry=None):
    s, d = dy.shape
    f = wd.shape[0]
    tm = _pick(s, (512, 256, 128))
    tf = _pick(f, (1408, 1024, 512, 256, 128))

    def body(dy_ref, wd_ref, a_ref, b_ref, da_ref, db_ref):
        du = _dot_nt(dy_ref[...], wd_ref[...])
        a = a_ref[...].astype(F32)
        b = b_ref[...].astype(F32)
        sig = _sigmoid(a)
        da_ref[...] = (du * b * (sig * (1.0 + a * (1.0 - sig)))).astype(BF16)
        db_ref[...] = (du * (a * sig)).astype(BF16)

    t_spec = pl.BlockSpec((tm, tf), lambda i, j: (i, j))
    o_shape = jax.ShapeDtypeStruct((s, f), BF16)
    return _call(body, name=name, grid=(s // tm, f // tf),
                 in_specs=[pl.BlockSpec((tm, d), lambda i, j: (i, 0)), pl.BlockSpec((tf, d), lambda i, j: (j, 0)),
                           t_spec, t_spec],
                 out_specs=(t_spec, t_spec), out_shape=(o_shape, o_shape), args=[dy, wd, a, b],
                 sem=("parallel", "parallel"), carry=carry)


def _row_tile(s):
    return _pick(s, (256, 128, 64))


def _vec_spec(d):
    return pl.BlockSpec((1, d), lambda i: (0, 0))


def _pre_norm(x, g, scale, shift, name):
    s, d = x.shape
    ts = _row_tile(s)

    def body(x_ref, g_ref, sc_ref, sh_ref, h_ref):
        xv = x_ref[...]
        r = lax.rsqrt(jnp.mean(xv * xv, axis=-1, keepdims=True) + EPS)
        h_ref[...] = (((xv * r) * g_ref[...]) * (1.0 + sc_ref[...]) + sh_ref[...]).astype(BF16)

    row = pl.BlockSpec((ts, d), lambda i: (i, 0))
    return _call(body, name=name, grid=(s // ts,), in_specs=[row, _vec_spec(d), _vec_spec(d), _vec_spec(d)],
                 out_specs=row, out_shape=jax.ShapeDtypeStruct((s, d), BF16), args=[x, g, scale, shift],
                 sem=("parallel",))


def _post_norm_residual(x, y, g, gate, weight, name):
    s, d = x.shape
    ts = _row_tile(s)

    def body(x_ref, y_ref, g_ref, gate_ref, o_ref):
        yv = y_ref[...]
        r = lax.rsqrt(jnp.mean(yv * yv, axis=-1, keepdims=True) + EPS)
        o_ref[...] = x_ref[...] + (weight * gate_ref[...]) * ((yv * r) * g_ref[...])

    row = pl.BlockSpec((ts, d), lambda i: (i, 0))
    return _call(body, name=name, grid=(s // ts,), in_specs=[row, row, _vec_spec(d), _vec_spec(d)],
                 out_specs=row, out_shape=jax.ShapeDtypeStruct((s, d), F32), args=[x, y, g, gate],
                 sem=("parallel",))


def _post_norm_bwd(dout, y, g, gate, weight, name):
    s, d = y.shape
    ts = _row_tile(s)

    def body(do_ref, y_ref, g_ref, gate_ref, dy_ref, s1_ref, cs_ref):
        @pl.when(pl.program_id(0) == 0)
        def _():
            s1_ref[...] = jnp.zeros_like(s1_ref)
            cs_ref[...] = jnp.zeros_like(cs_ref)

        yv = y_ref[...]
        do = do_ref[...]
        r = lax.rsqrt(jnp.mean(yv * yv, axis=-1, keepdims=True) + EPS)
        yn = yv * r
        dyn = do * ((weight * gate_ref[...]) * g_ref[...])
        dy = r * (dyn - yn * jnp.mean(dyn * yn, axis=-1, keepdims=True))
        dy_ref[...] = dy.astype(BF16)
        s1_ref[...] += jnp.sum(do * yn, axis=0, keepdims=True)
        cs_ref[...] += jnp.sum(dy, axis=0, keepdims=True)

    row = pl.BlockSpec((ts, d), lambda i: (i, 0))
    vec = jax.ShapeDtypeStruct((1, d), F32)
    return _call(body, name=name, grid=(s // ts,), in_specs=[row, row, _vec_spec(d), _vec_spec(d)],
                 out_specs=(row, _vec_spec(d), _vec_spec(d)),
                 out_shape=(jax.ShapeDtypeStruct((s, d), BF16), vec, vec), args=[dout, y, g, gate],
                 sem=("arbitrary",))


def _pre_norm_bwd(dh, x, g, scale, dres, name):
    s, d = x.shape
    ts = _row_tile(s)

    def body(dh_ref, x_ref, g_ref, sc_ref, dr_ref, dx_ref, s2_ref, s3_ref):
        @pl.when(pl.program_id(0) == 0)
        def _():
            s2_ref[...] = jnp.zeros_like(s2_ref)
            s3_ref[...] = jnp.zeros_like(s3_ref)

        xv = x_ref[...]
        dh = dh_ref[...]
        r = lax.rsqrt(jnp.mean(xv * xv, axis=-1, keepdims=True) + EPS)
        n = xv * r
        dn = dh * (g_ref[...] * (1.0 + sc_ref[...]))
        dx_ref[...] = dr_ref[...] + r * (dn - n * jnp.mean(dn * n, axis=-1, keepdims=True))
        s2_ref[...] += jnp.sum(dh * n, axis=0, keepdims=True)
        s3_ref[...] += jnp.sum(dh, axis=0, keepdims=True)

    row = pl.BlockSpec((ts, d), lambda i: (i, 0))
    vec = jax.ShapeDtypeStruct((1, d), F32)
    return _call(body, name=name, grid=(s // ts,), in_specs=[row, row, _vec_spec(d), _vec_spec(d), row],
                 out_specs=(row, _vec_spec(d), _vec_spec(d)),
                 out_shape=(jax.ShapeDtypeStruct((s, d), F32), vec, vec), args=[dh, x, g, scale, dres],
                 sem=("arbitrary",))


def _post_pre_norm(x, y, g_post, gate, weight, g_pre, scale, shift, name):
    s, d = x.shape
    ts = _row_tile(s)

    def body(x_ref, y_ref, gp_ref, gate_ref, g_ref, sc_ref, sh_ref, o_ref, h_ref):
        yv = y_ref[...]
        r = lax.rsqrt(jnp.mean(yv * yv, axis=-1, keepdims=True) + EPS)
        xv = x_ref[...] + (weight * gate_ref[...]) * ((yv * r) * gp_ref[...])
        o_ref[...] = xv
        r2 = lax.rsqrt(jnp.mean(xv * xv, axis=-1, keepdims=True) + EPS)
        h_ref[...] = (((xv * r2) * g_ref[...]) * (1.0 + sc_ref[...]) + sh_ref[...]).astype(BF16)

    row = pl.BlockSpec((ts, d), lambda i: (i, 0))
    return _call(body, name=name, grid=(s // ts,), in_specs=[row, row] + [_vec_spec(d)] * 5,
                 out_specs=(row, row),
                 out_shape=(jax.ShapeDtypeStruct((s, d), F32), jax.ShapeDtypeStruct((s, d), BF16)),
                 args=[x, y, g_post, gate, g_pre, scale, shift], sem=("parallel",))


def _post_norm_loss_bwd(x, y, g, gate, weight, target, name):
    s, d = y.shape
    ts = _row_tile(s)

    def body(x_ref, y_ref, g_ref, gate_ref, t_ref, dx_ref, dy_ref, l_ref, s1_ref):
        @pl.when(pl.program_id(0) == 0)
        def _():
            l_ref[...] = jnp.zeros_like(l_ref)
            s1_ref[...] = jnp.zeros_like(s1_ref)

        yv = y_ref[...]
        r = lax.rsqrt(jnp.mean(yv * yv, axis=-1, keepdims=True) + EPS)
        yn = yv * r
        err = (x_ref[...] + (weight * gate_ref[...]) * (yn * g_ref[...])) - t_ref[...]
        do = err * (1.0 / d)
        dx_ref[...] = do
        l_ref[...] += 0.5 * jnp.sum(jnp.mean(err * err, axis=-1, keepdims=True), axis=0, keepdims=True)
        dyn = do * ((weight * gate_ref[...]) * g_ref[...])
        dy_ref[...] = (r * (dyn - yn * jnp.mean(dyn * yn, axis=-1, keepdims=True))).astype(BF16)
        s1_ref[...] += jnp.sum(do * yn, axis=0, keepdims=True)

    row = pl.BlockSpec((ts, d), lambda i: (i, 0))
    return _call(body, name=name, grid=(s // ts,), in_specs=[row, row, _vec_spec(d), _vec_spec(d), row],
                 out_specs=(row, row, pl.BlockSpec((1, 1), lambda i: (0, 0)), _vec_spec(d)),
                 out_shape=(jax.ShapeDtypeStruct((s, d), F32), jax.ShapeDtypeStruct((s, d), BF16),
                            jax.ShapeDtypeStruct((1, 1), F32), jax.ShapeDtypeStruct((1, d), F32)),
                 args=[x, y, g, gate, target], sem=("arbitrary",))


def _pre_post_norm_bwd(dh, x, g_pre, scale, dres, y, g_post, gate, weight, name):
    s, d = x.shape
    ts = _row_tile(s)

    def body(dh_ref, x_ref, g_ref, sc_ref, dr_ref, y_ref, gp_ref, gate_ref,
             dx_ref, dy_ref, s2_ref, s3_ref, s1_ref, cs_ref):
        @pl.when(pl.program_id(0) == 0)
        def _():
            for ref in (s2_ref, s3_ref, s1_ref, cs_ref):
                ref[...] = jnp.zeros_like(ref)

        xv = x_ref[...]
        dh = dh_ref[...]
        r = lax.rsqrt(jnp.mean(xv * xv, axis=-1, keepdims=True) + EPS)
        n = xv * r
        dn = dh * (g_ref[...] * (1.0 + sc_ref[...]))
        dx = dr_ref[...] + r * (dn - n * jnp.mean(dn * n, axis=-1, keepdims=True))
        dx_ref[...] = dx
        s2_ref[...] += jnp.sum(dh * n, axis=0, keepdims=True)
        s3_ref[...] += jnp.sum(dh, axis=0, keepdims=True)
        yv = y_ref[...]
        ry = lax.rsqrt(jnp.mean(yv * yv, axis=-1, keepdims=True) + EPS)
        yn = yv * ry
        dyn = dx * ((weight * gate_ref[...]) * gp_ref[...])
        dy = ry * (dyn - yn * jnp.mean(dyn * yn, axis=-1, keepdims=True))
        dy_ref[...] = dy.astype(BF16)
        s1_ref[...] += jnp.sum(dx * yn, axis=0, keepdims=True)
        cs_ref[...] += jnp.sum(dy, axis=0, keepdims=True)

    row = pl.BlockSpec((ts, d), lambda i: (i, 0))
    vec = jax.ShapeDtypeStruct((1, d), F32)
    return _call(body, name=name, grid=(s // ts,),
                 in_specs=[row, row, _vec_spec(d), _vec_spec(d), row, row, _vec_spec(d), _vec_spec(d)],
                 out_specs=(row, row) + (_vec_spec(d),) * 4,
                 out_shape=(jax.ShapeDtypeStruct((s, d), F32), jax.ShapeDtypeStruct((s, d), BF16), vec, vec, vec, vec),
                 args=[dh, x, g_pre, scale, dres, y, g_post, gate], sem=("arbitrary",))


def _group_norm_cat(oa, ob, ga, gb):
    s = oa.shape[0]
    ts = _row_tile(s)

    def body(oa_ref, ob_ref, ga_ref, gb_ref, y_ref):
        for o_ref, g_ref, lo, w in ((oa_ref, ga_ref, 0, QA), (ob_ref, gb_ref, QA, QB)):
            ov = o_ref[...]
            r = lax.rsqrt(jnp.mean(ov * ov, axis=-1, keepdims=True) + EPS)
            y_ref[:, lo:lo + w] = ((ov * r) * g_ref[...]).astype(BF16)

    return _call(body, name="group_norm_cat", grid=(s // ts,),
                 in_specs=[pl.BlockSpec((ts, QA), lambda i: (i, 0)), pl.BlockSpec((ts, QB), lambda i: (i, 0)),
                           _vec_spec(QA), _vec_spec(QB)],
                 out_specs=pl.BlockSpec((ts, QA + QB), lambda i: (i, 0)),
                 out_shape=jax.ShapeDtypeStruct((s, QA + QB), BF16), args=[oa, ob, ga, gb], sem=("parallel",))


def _group_norm_bwd(dy, oa, ob, ga, gb):
    s = oa.shape[0]
    ts = _row_tile(s)

    def body(dy_ref, oa_ref, ob_ref, ga_ref, gb_ref, doa_ref, dob_ref, dga_ref, dgb_ref):
        @pl.when(pl.program_id(0) == 0)
        def _():
            dga_ref[...] = jnp.zeros_like(dga_ref)
            dgb_ref[...] = jnp.zeros_like(dgb_ref)

        for o_ref, g_ref, do_ref, dg_ref, lo, w in ((oa_ref, ga_ref, doa_ref, dga_ref, 0, QA),
                                                    (ob_ref, gb_ref, dob_ref, dgb_ref, QA, QB)):
            ov = o_ref[...]
            dyv = dy_ref[:, lo:lo + w]
            r = lax.rsqrt(jnp.mean(ov * ov, axis=-1, keepdims=True) + EPS)
            n = ov * r
            dn = dyv * g_ref[...]
            do_ref[...] = r * (dn - n * jnp.mean(dn * n, axis=-1, keepdims=True))
            dg_ref[...] += jnp.sum(dyv * n, axis=0, keepdims=True)

    ra = pl.BlockSpec((ts, QA), lambda i: (i, 0))
    rb = pl.BlockSpec((ts, QB), lambda i: (i, 0))
    return _call(body, name="group_norm_bwd", grid=(s // ts,),
                 in_specs=[pl.BlockSpec((ts, QA + QB), lambda i: (i, 0)), ra, rb, _vec_spec(QA), _vec_spec(QB)],
                 out_specs=(ra, rb, _vec_spec(QA), _vec_spec(QB)),
                 out_shape=(jax.ShapeDtypeStruct((s, QA), F32), jax.ShapeDtypeStruct((s, QB), F32),
                            jax.ShapeDtypeStruct((1, QA), F32), jax.ShapeDtypeStruct((1, QB), F32)),
                 args=[dy, oa, ob, ga, gb], sem=("arbitrary",))


def _loss_and_grad(y, target):
    s, d = y.shape
    ts = _row_tile(s)

    def body(y_ref, t_ref, l_ref, g_ref):
        @pl.when(pl.program_id(0) == 0)
        def _():
            l_ref[...] = jnp.zeros_like(l_ref)

        err = y_ref[...] - t_ref[...]
        g_ref[...] = err * (1.0 / d)
        row = jnp.mean(err * err, axis=-1, keepdims=True)
        l_ref[...] += 0.5 * jnp.sum(row, axis=0, keepdims=True)

    row = pl.BlockSpec((ts, d), lambda i: (i, 0))
    return _call(body, name="loss_and_grad", grid=(s // ts,), in_specs=[row, row],
                 out_specs=(pl.BlockSpec((1, 1), lambda i: (0, 0)), row),
                 out_shape=(jax.ShapeDtypeStruct((1, 1), F32), jax.ShapeDtypeStruct((s, d), F32)),
                 args=[y, target], sem=("arbitrary",))


def _col_sum(x, name):
    s, n = x.shape
    ts = _row_tile(s)

    def body(x_ref, o_ref):
        @pl.when(pl.program_id(0) == 0)
        def _():
            o_ref[...] = jnp.zeros_like(o_ref)

        o_ref[...] += jnp.sum(x_ref[...].astype(F32), axis=0, keepdims=True)

    return _call(body, name=name, grid=(s // ts,), in_specs=[pl.BlockSpec((ts, n), lambda i: (i, 0))],
                 out_specs=pl.BlockSpec((1, n), lambda i: (0, 0)), out_shape=jax.ShapeDtypeStruct((1, n), F32),
                 args=[x], sem=("arbitrary",))


def _n_variants(n_back):
    return -(-n_back // QG) + 1


def _alibi_bias():
    i = np.arange(QROWS)[:, None]
    j = np.arange((QG + BACK_A) * CHUNK)[None, :]
    dist = np.abs(BACK_A * CHUNK + i - j).astype(np.float32)
    dc = j // CHUNK - i // CHUNK
    valid = (dc >= 0) & (dc <= BACK_A)
    slopes = np.array([2.0 ** (-8.0 * (h + 1) / H_A) for h in range(H_A)], dtype=np.float32)
    bias = -slopes[:, None, None] * dist[None]
    out = [np.where((valid & (j >= (BACK_A - QG * v) * CHUNK))[None], bias, np.float32(NEG_INF))
           for v in range(_n_variants(BACK_A))]
    return jnp.asarray(np.stack(out).astype(np.float32))


def _rel_index_matrix():
    cc = np.arange(SKEW)
    dist = np.where(cc < SKEW - QROWS, BACK_B * CHUNK - cc, BACK_B * CHUNK + SKEW - cc)
    idx = np.clip(dist, -REL_CLIP, REL_CLIP) + REL_CLIP
    m = np.zeros((SKEW, N_REL), np.float32)
    m[cc, idx] = 1.0
    return jnp.asarray(m)


def _toeplitz_bias(vec, carry=None):
    lk = (QG + BACK_B) * CHUNK
    nv = _n_variants(BACK_B)

    def body(v_ref, o_ref):
        xv = jnp.broadcast_to(v_ref[0], (QROWS, SKEW))
        row = lax.broadcasted_iota(jnp.int32, (QROWS, SKEW), 0)
        for bit in range(QROWS.bit_length() - 1):
            xv = jnp.where((row >> bit) & 1 == 1, pltpu.roll(xv, 1 << bit, 1), xv)
        ri = lax.broadcasted_iota(jnp.int32, (QROWS, lk), 0) // CHUNK
        col = lax.broadcasted_iota(jnp.int32, (QROWS, lk), 1)
        ci = col // CHUNK
        valid = (ci - ri >= 0) & (ci - ri <= BACK_B)
        for v in range(nv):
            o_ref[v, 0] = jnp.where(valid & (col >= (BACK_B - QG * v) * CHUNK), xv[:, :lk], NEG_INF)

    return _call(body, name="toeplitz_bias", grid=(H_B,),
                 in_specs=[pl.BlockSpec((1, 1, SKEW), lambda h: (h, 0, 0))],
                 out_specs=pl.BlockSpec((nv, 1, QROWS, lk), lambda h: (0, h, 0, 0)),
                 out_shape=jax.ShapeDtypeStruct((nv, H_B, QROWS, lk), F32), args=[vec], sem=("parallel",),
                 carry=carry)


def _diagonal_sums(dbias):
    lk = dbias.shape[2]

    def body(d_ref, o_ref):
        xp = jnp.concatenate([d_ref[0], jnp.zeros((QROWS, SKEW - lk), F32)], axis=1)
        xv = xp[0:CHUNK]
        for q in range(1, QG):
            xv = xv + pltpu.roll(xp[q * CHUNK:(q + 1) * CHUNK], SKEW - q * CHUNK, 1)
        row = lax.broadcasted_iota(jnp.int32, (CHUNK, SKEW), 0)
        for bit in range(CHUNK.bit_length() - 1):
            xv = jnp.where((row >> bit) & 1 == 1, pltpu.roll(xv, SKEW - (1 << bit), 1), xv)
        o_ref[0] = jnp.sum(xv, axis=0, keepdims=True)

    return _call(body, name="diagonal_sums", grid=(H_B,),
                 in_specs=[pl.BlockSpec((1, QROWS, lk), lambda h: (h, 0, 0))],
                 out_specs=pl.BlockSpec((1, 1, SKEW), lambda h: (h, 0, 0)),
                 out_shape=jax.ShapeDtypeStruct((H_B, 1, SKEW), F32), args=[dbias], sem=("parallel",))


def _attn_common(s, n_back, gqa, q_col, k_col, v_col):
    lk = (QG + n_back) * CHUNK
    pad = n_back * CHUNK
    wide = TPS * LANES
    q_spec = pl.BlockSpec((QROWS, wide), lambda t, g: (g, q_col // TPS + t))
    if gqa:
        k_spec = pl.BlockSpec((s, LANES), lambda t, g: (0, k_col))
        v_spec = pl.BlockSpec((s, LANES), lambda t, g: (0, v_col))
    else:
        k_spec = pl.BlockSpec((s, wide), lambda t, g: (0, k_col // TPS + t))
        v_spec = pl.BlockSpec((s, wide), lambda t, g: (0, v_col // TPS + t))
    last_variant = _n_variants(n_back) - 1
    bias_spec = pl.BlockSpec((None, 2 * TPS, QROWS, lk), lambda t, g: (jnp.minimum(g, last_variant), t, 0, 0))
    tile_spec = pl.BlockSpec((QROWS, wide), lambda t, g: (g, t))
    return lk, pad, q_spec, k_spec, v_spec, bias_spec, tile_spec


def _attention_fwd(proj, bias, sinks, *, n_back, gqa, q_col, k_col, v_col, name, carry=None):
    s = proj.shape[0]
    lk, pad, q_spec, k_spec, v_spec, bias_spec, tile_spec = _attn_common(s, n_back, gqa, q_col, k_col, v_col)
    n_t, n_g = 512 // (TPS * LANES), s // QROWS
    kv_wide = LANES if gqa else TPS * LANES

    def body(*refs):
        if gqa:
            q_ref, k_ref, v_ref, bias_ref, sink_ref, o_ref, l_ref, kpad, vpad = refs
        else:
            q_ref, k_ref, v_ref, bias_ref, o_ref, l_ref, kpad, vpad = refs
        t, g = pl.program_id(0), pl.program_id(1)

        @pl.when(g == 0)
        def _():
            kpad[0:pad, :] = jnp.zeros((pad, kv_wide), BF16)
            vpad[0:pad, :] = jnp.zeros((pad, kv_wide), BF16)
            kpad[pad:, :] = k_ref[...]
            vpad[pad:, :] = v_ref[...]

        start = pl.multiple_of(g * QROWS, QROWS)
        half = lax.broadcasted_iota(jnp.int32, (QROWS, LANES), 1) // HEAD_DIM
        for tt in range(TPS):
            lanes = slice(tt * LANES, (tt + 1) * LANES)
            kv_lanes = slice(0, LANES) if gqa else lanes
            kb = kpad[pl.ds(start, lk), kv_lanes]
            vb = vpad[pl.ds(start, lk), kv_lanes]
            q = q_ref[:, lanes] * (HEAD_DIM ** -0.5)
            if gqa:
                hk = (TPS * t + tt) // 2
                q_rolled = pltpu.roll(q.astype(F32), HEAD_DIM, 1).astype(BF16)
            outs, lses = [], []
            for e in range(2):
                if gqa:
                    kv_half = hk
                    src = jnp.where(hk == e, q, q_rolled)
                else:
                    kv_half = e
                    src = q
                qm = jnp.where(half == kv_half, src, jnp.zeros_like(src))
                sc = _dot_nt(qm, kb) + bias_ref[2 * tt + e]
                m = jnp.max(sc, axis=-1, keepdims=True)
                if gqa:
                    sk = sink_ref[2 * (TPS * t + tt) + e]
                    m = jnp.maximum(m, sk)
                p = jnp.exp(sc - m)
                l = jnp.sum(p, axis=-1, keepdims=True)
                if gqa:
                    l = l + jnp.exp(sk - m)
                pn = p / l
                outs.append(_dot(pn.astype(BF16), vb))
                lses.append(m + jnp.log(l))
            if gqa:
                same = jnp.where(hk == 0, outs[0], outs[1])
                other = jnp.where(hk == 0, outs[1], outs[0])
                o_ref[:, lanes] = jnp.where(half == hk, same, pltpu.roll(other, HEAD_DIM, 1))
            else:
                o_ref[:, lanes] = jnp.where(half == 0, outs[0], outs[1])
            l_ref[:, lanes] = jnp.where(half == 0, lses[0], lses[1])

    in_specs = [q_spec, k_spec, v_spec, bias_spec] + ([SMEM_SPEC] if gqa else [])
    args = [proj, proj, proj, bias] + ([sinks] if gqa else [])
    o_shape = jax.ShapeDtypeStruct((s, 512), F32)
    return _call(body, name=name, grid=(n_t, n_g), in_specs=in_specs, out_specs=(tile_spec, tile_spec),
                 out_shape=(o_shape, o_shape), args=args,
                 scratch=[pltpu.VMEM((s + pad, kv_wide), BF16), pltpu.VMEM((s + pad, kv_wide), BF16)],
                 sem=("arbitrary", "arbitrary"), carry=carry)


def _attention_bwd(proj, bias, sinks, do, lse, *, n_back, gqa, q_col, k_col, v_col, name, carry=None):
    s = proj.shape[0]
    lk, pad, q_spec, k_spec, v_spec, bias_spec, tile_spec = _attn_common(s, n_back, gqa, q_col, k_col, v_col)
    n_t, n_g = 512 // (TPS * LANES), s // QROWS
    kv_wide = LANES if gqa else TPS * LANES

    def body(*refs):
        if gqa:
            (q_ref, k_ref, v_ref, bias_ref, sink_ref, do_ref, l_ref,
             dq_ref, dk_ref, dv_ref, dsink_ref, kpad, vpad, dkpad, dvpad) = refs
        else:
            (q_ref, k_ref, v_ref, bias_ref, do_ref, l_ref,
             dq_ref, dk_ref, dv_ref, dbias_ref, kpad, vpad, dkpad, dvpad) = refs
        t, g = pl.program_id(0), pl.program_id(1)

        @pl.when(g == 0)
        def _():
            kpad[0:pad, :] = jnp.zeros((pad, kv_wide), BF16)
            vpad[0:pad, :] = jnp.zeros((pad, kv_wide), BF16)
            kpad[pad:, :] = k_ref[...]
            vpad[pad:, :] = v_ref[...]
            if gqa:
                dsink_ref[...] = jnp.zeros_like(dsink_ref)
            else:
                dbias_ref[...] = jnp.zeros_like(dbias_ref)

        @pl.when((g == 0) & (t == 0) if gqa else g == 0)
        def _():
            dkpad[...] = jnp.zeros_like(dkpad)
            dvpad[...] = jnp.zeros_like(dvpad)

        start = pl.multiple_of(g * QROWS, QROWS)
        half = lax.broadcasted_iota(jnp.int32, (QROWS, LANES), 1) // HEAD_DIM
        for tt in range(TPS):
            lanes = slice(tt * LANES, (tt + 1) * LANES)
            kv_lanes = slice(0, LANES) if gqa else lanes
            kb = kpad[pl.ds(start, lk), kv_lanes]
            vb = vpad[pl.ds(start, lk), kv_lanes]
            q = q_ref[:, lanes]
            dov = do_ref[:, lanes]
            lv = l_ref[:, lanes]
            if gqa:
                hk = (TPS * t + tt) // 2
                q_rolled = pltpu.roll(q.astype(F32), HEAD_DIM, 1).astype(BF16)
                do_rolled = pltpu.roll(dov, HEAD_DIM, 1)
            dqs = []
            dk_acc = jnp.zeros((lk, LANES), F32)
            dv_acc = jnp.zeros((lk, LANES), F32)
            for e in range(2):
                if gqa:
                    kv_half = hk
                    src = jnp.where(hk == e, q, q_rolled)
                    do_src = jnp.where(hk == e, dov, do_rolled)
                else:
                    kv_half = e
                    src = q
                    do_src = dov
                qm = jnp.where(half == kv_half, src, jnp.zeros_like(src))
                dom = jnp.where(half == kv_half, do_src, 0.0).astype(BF16)
                lcol = jnp.max(jnp.where(half == e, lv, -jnp.inf), axis=-1, keepdims=True)
                sc = _dot_nt(qm * (HEAD_DIM ** -0.5), kb) + bias_ref[2 * tt + e]
                pn = jnp.exp(sc - lcol)
                dp = _dot_nt(dom, vb)
                delta = jnp.sum(pn * dp, axis=-1, keepdims=True)
                ds = pn * (dp - delta)
                if gqa:
                    p_sink = jnp.exp(sink_ref[2 * (TPS * t + tt) + e] - lcol)
                    dsk = -jnp.sum(p_sink * delta, axis=0, keepdims=True)
                    row = 2 * tt + e
                    dsink_ref[0, row:row + 1, :] += jnp.broadcast_to(dsk, (1, LANES))
                else:
                    dbias_ref[2 * tt + e] += ds
                dsb = (ds * (HEAD_DIM ** -0.5)).astype(BF16)
                dqs.append(_dot(dsb, kb))
                dk_acc = dk_acc + _dot_tn(dsb, qm)
                dv_acc = dv_acc + _dot_tn(pn.astype(BF16), dom)
            dkpad[pl.ds(start, lk), kv_lanes] += dk_acc
            dvpad[pl.ds(start, lk), kv_lanes] += dv_acc
            if gqa:
                same = jnp.where(hk == 0, dqs[0], dqs[1])
                other = jnp.where(hk == 0, dqs[1], dqs[0])
                dq_ref[:, lanes] = jnp.where(half == hk, same, pltpu.roll(other, HEAD_DIM, 1)).astype(BF16)
            else:
                dq_ref[:, lanes] = jnp.where(half == 0, dqs[0], dqs[1]).astype(BF16)

        @pl.when((g == n_g - 1) & (t == n_t - 1) if gqa else g == n_g - 1)
        def _():
            dk_ref[...] = dkpad[pad:, :].astype(BF16)
            dv_ref[...] = dvpad[pad:, :].astype(BF16)

    in_specs = [q_spec, k_spec, v_spec, bias_spec] + ([SMEM_SPEC] if gqa else []) + [tile_spec, tile_spec]
    args = [proj, proj, proj, bias] + ([sinks] if gqa else []) + [do, lse]
    if gqa:
        kv_out = pl.BlockSpec((s, LANES), lambda t, g: (0, 0))
        kv_shape = jax.ShapeDtypeStruct((s, LANES), BF16)
        extra_spec = pl.BlockSpec((1, 8, LANES), lambda t, g: (t, 0, 0))
        extra_shape = jax.ShapeDtypeStruct((n_t, 8, LANES), F32)
    else:
        kv_out = pl.BlockSpec((s, kv_wide), lambda t, g: (0, t))
        kv_shape = jax.ShapeDtypeStruct((s, 512), BF16)
        extra_spec = pl.BlockSpec((2 * TPS, QROWS, lk), lambda t, g: (t, 0, 0))
        extra_shape = jax.ShapeDtypeStruct(bias.shape[1:], F32)
    return _call(body, name=name, grid=(n_t, n_g), in_specs=in_specs,
                 out_specs=(tile_spec, kv_out, kv_out, extra_spec),
                 out_shape=(jax.ShapeDtypeStruct((s, 512), BF16), kv_shape, kv_shape, extra_shape), args=args,
                 scratch=[pltpu.VMEM((s + pad, kv_wide), BF16), pltpu.VMEM((s + pad, kv_wide), BF16),
                          pltpu.VMEM((s + pad, kv_wide), F32), pltpu.VMEM((s + pad, kv_wide), F32)],
                 sem=("arbitrary", "arbitrary"), carry=carry)


def _sum_slots(r, name):
    n_slots, rows, k = r.shape

    def body(r_ref, o_ref):
        acc = r_ref[0].astype(F32)
        for j in range(1, n_slots):
            acc = acc + r_ref[j].astype(F32)
        o_ref[...] = acc

    return _call(body, name=name, grid=(k // LANES,),
                 in_specs=[pl.BlockSpec((n_slots, rows, LANES), lambda i: (0, 0, i))],
                 out_specs=pl.BlockSpec((rows, LANES), lambda i: (0, i)),
                 out_shape=jax.ShapeDtypeStruct((rows, k), F32), args=[r], sem=("parallel",))


def _sum_rows8(g):
    n = g.shape[2]

    def body(g_ref, o_ref):
        acc = g_ref[0]
        for j in range(1, N_DEV):
            acc = acc + g_ref[j]
        o_ref[...] = acc

    return pl.pallas_call(
        body, name="sum_small_grads", in_specs=[VMEM_SPEC], out_specs=VMEM_SPEC,
        out_shape=jax.ShapeDtypeStruct((1, n), F32), compiler_params=_params(),
    )(g)


def _ada_weight_grad(sc_t, dmod_cols):
    d = sc_t.shape[0]
    w = dmod_cols.shape[1]
    td = _pick(d, (256, 128))

    def body(sc_ref, dm_ref, o_ref):
        scv = sc_ref[...]
        dmv = dm_ref[...]
        acc = scv[:, 0:1] * dmv[0:1, :]
        for b in range(1, N_DEV):
            acc = acc + scv[:, b:b + 1] * dmv[b:b + 1, :]
        o_ref[...] = acc

    return _call(body, name="ada_weight_grad", grid=(d // td,),
                 in_specs=[pl.BlockSpec((td, N_DEV), lambda i: (i, 0)), pl.BlockSpec((N_DEV, w), lambda i: (0, 0))],
                 out_specs=pl.BlockSpec((td, w), lambda i: (i, 0)), out_shape=jax.ShapeDtypeStruct((d, w), F32),
                 args=[sc_t, dmod_cols], sem=("parallel",))


def _adamw_update(w, gv, m, v):
    nm = ADAM_B1 * m + (1.0 - ADAM_B1) * gv
    nv = ADAM_B2 * v + (1.0 - ADAM_B2) * (gv * gv)
    m_hat = nm / (1.0 - ADAM_B1 ** ADAM_STEP)
    v_hat = nv / (1.0 - ADAM_B2 ** ADAM_STEP)
    return -ADAM_LR * (m_hat / (jnp.sqrt(v_hat) + ADAM_EPS) + ADAM_WD * w), nm, nv


def _adamw(w, g, m, v, name):
    rows, cols = w.shape
    tr = _pick(rows, (256, 176, 128, 88, 64)) if rows > 256 else rows

    def body(w_ref, g_ref, m_ref, v_ref, d_ref, nm_ref, nv_ref):
        d_ref[...], nm_ref[...], nv_ref[...] = _adamw_update(w_ref[...], g_ref[...], m_ref[...], v_ref[...])

    spec = pl.BlockSpec((tr, cols), lambda i: (i, 0))
    shape = jax.ShapeDtypeStruct((rows, cols), F32)
    return _call(body, name=name, grid=(rows // tr,), in_specs=[spec] * 4, out_specs=(spec, spec, spec),
                 out_shape=(shape, shape, shape), args=[w, g, m, v], sem=("parallel",))


def _adamw_from_slots(w, slots, m, v, name):
    n_slots, rows, k = slots.shape

    def body(s_ref, w_ref, m_ref, v_ref, g_ref, d_ref, nm_ref, nv_ref):
        gv = s_ref[0].astype(F32)
        for j in range(1, n_slots):
            gv = gv + s_ref[j].astype(F32)
        g_ref[...] = gv
        d_ref[...], nm_ref[...], nv_ref[...] = _adamw_update(w_ref[...], gv, m_ref[...], v_ref[...])

    tr = rows // 2 if rows % 32 == 0 else rows
    spec = pl.BlockSpec((tr, k), lambda i: (i, 0))
    shape = jax.ShapeDtypeStruct((rows, k), F32)
    return _call(body, name=name, grid=(rows // tr,),
                 in_specs=[pl.BlockSpec((n_slots, tr, k), lambda i: (0, i, 0)), spec, spec, spec],
                 out_specs=(spec, spec, spec, spec), out_shape=(shape, shape, shape, shape),
                 args=[slots, w, m, v], sem=("parallel",))


SMALL = ("b_ada", "g_pre_ffn1", "g_post_ffn1", "g_pre_mix", "b_in", "sinks_a", "rel_bias_b", "g_grp_a",
         "g_grp_b", "b_out", "g_post_mix", "g_pre_ffn2", "g_post_ffn2")
WEIGHTS = ("w_ada", "b_ada", "g_pre_ffn1", "w_gate1", "w_up1", "w_down1", "g_post_ffn1", "g_pre_mix", "w_in",
           "b_in", "sinks_a", "rel_bias_b", "g_grp_a", "g_grp_b", "w_out", "b_out", "g_post_mix", "g_pre_ffn2",
           "w_gate2", "w_up2", "w_down2", "g_post_ffn2")


def kernel(x, c, w_ada, b_ada, g_pre_ffn1, w_gate1, w_up1, w_down1, g_post_ffn1, g_pre_mix, w_in, b_in, sinks_a, rel_bias_b, g_grp_a, g_grp_b, w_out, b_out, g_post_mix, g_pre_ffn2, w_gate2, w_up2, w_down2, g_post_ffn2, loss_target, m_w_ada, m_b_ada, m_g_pre_ffn1, m_w_gate1, m_w_up1, m_w_down1, m_g_post_ffn1, m_g_pre_mix, m_w_in, m_b_in, m_sinks_a, m_rel_bias_b, m_g_grp_a, m_g_grp_b, m_w_out, m_b_out, m_g_post_mix, m_g_pre_ffn2, m_w_gate2, m_w_up2, m_w_down2, m_g_post_ffn2, v_w_ada, v_b_ada, v_g_pre_ffn1, v_w_gate1, v_w_up1, v_w_down1, v_g_post_ffn1, v_g_pre_mix, v_w_in, v_b_in, v_sinks_a, v_rel_bias_b, v_g_grp_a, v_g_grp_b, v_w_out, v_b_out, v_g_post_mix, v_g_pre_ffn2, v_w_gate2, v_w_up2, v_w_down2, v_g_post_ffn2):
    given = dict(locals())
    weights = {n: given[n] for n in WEIGHTS}
    mom_m = {n: given["m_" + n] for n in WEIGHTS}
    mom_v = {n: given["v_" + n] for n in WEIGHTS}

    me = 4 * lax.axis_index("x") + 2 * lax.axis_index("y") + lax.axis_index("c")
    xs = x[0]
    tgt = loss_target[0]
    d_model = xs.shape[1]
    ada_cols = w_ada.shape[2]

    sh = {"wg1": w_gate1[0].T, "wu1": w_up1[0].T, "wd1": w_down1[0], "win": w_in[0].T, "wo": w_out[0],
          "wg2": w_gate2[0].T, "wu2": w_up2[0].T, "wd2": w_down2[0]}
    sh = {k: v.astype(BF16) for k, v in sh.items()}

    def gather(*names):
        return _gather_carry([sh[n] for n in names])

    bias_a = _alibi_bias()
    rel_m = _rel_index_matrix()
    rel_vec = jnp.dot(rel_bias_b[0], rel_m.T, precision=lax.Precision.HIGHEST)
    bias_b, (wg1, wu1) = _toeplitz_bias(rel_vec.reshape(H_B, 1, SKEW), carry=gather("wg1", "wu1"))

    b_cols = lax.dynamic_slice(b_ada, (0, me * ada_cols), (1, ada_cols))
    (sc_all, mod_rows), _ = _ada_forward(c, w_ada[0], b_cols, _Carry([], [], [], lambda *a: None, lambda *a: None))
    mod = mod_rows.reshape(N_MOD, d_model)
    shift1, scale1, gate1, shift2, scale2, gate2, shift3, scale3, gate3 = (mod[i:i + 1] for i in range(N_MOD))

    h1 = _pre_norm(xs, g_pre_ffn1, scale1, shift1, "pre_norm_ffn1")
    (a1, b1, u1), (wd1,) = _ffn_up(h1, wg1, wu1, "ffn_up_ffn1", carry=gather("wd1"))
    y1, (win,) = _mm_nn([(u1, wd1)], "ffn_down_ffn1", F32, carry=gather("win"))
    x1, h2 = _post_pre_norm(xs, y1, g_post_ffn1, gate1, 0.5, g_pre_mix, scale2, shift2, "post_ffn1_pre_mix")

    proj, (wo,) = _mm_nt(h2, win, "in_proj", BF16, bias=b_in, carry=gather("wo"))
    sinks = sinks_a[0]
    cfg_a = dict(n_back=BACK_A, gqa=True, q_col=0, k_col=QA // LANES, v_col=(QA + KVA) // LANES)
    cfg_b = dict(n_back=BACK_B, gqa=False, q_col=(QA + 2 * KVA) // LANES, k_col=(QA + 2 * KVA + QB) // LANES,
                 v_col=(QA + 2 * KVA + 2 * QB) // LANES)
    (oa, lse_a), (wg2,) = _attention_fwd(proj, bias_a, sinks, name="attn_a", carry=gather("wg2"), **cfg_a)
    (ob, lse_b), (wu2,) = _attention_fwd(proj, bias_b, None, name="attn_b", carry=gather("wu2"), **cfg_b)
    ycat = _group_norm_cat(oa, ob, g_grp_a, g_grp_b)
    ymix = _mm_nn([(ycat, wo)], "out_proj", F32, bias=b_out)
    x2, h3 = _post_pre_norm(x1, ymix, g_post_mix, gate2, 1.0, g_pre_ffn2, scale3, shift3, "post_mix_pre_ffn2")

    (a3, b3, u3), (wd2,) = _ffn_up(h3, wg2, wu2, "ffn_up_ffn2", carry=gather("wd2"))
    y3 = _mm_nn([(u3, wd2)], "ffn_down_ffn2", F32)

    def scatter(*grads):
        return _scatter_carry(list(grads))

    slots = {}

    dx3, dy, loss_part, s1 = _post_norm_loss_bwd(x2, y3, g_post_ffn2, gate3, 0.5, tgt, "post_ffn2_loss_bwd")
    da, db = _ffn_down_bwd(dy, wd2, a3, b3, "ffn_down_bwd_ffn2")
    dwd2 = _mm_tn_pair(u3, dy, "grad_wd_ffn2")
    dwg2 = _mm_tn_pair(da, h3, "grad_wg_ffn2")
    dwu2 = _mm_tn_pair(db, h3, "grad_wu_ffn2")
    dh, (slots["wd2"],) = _mm_nn([(da, wg2), (db, wu2)], "ffn_up_bwd_ffn2", F32, carry=scatter(dwd2))
    dx2, dymix, s2, s3, s1m, db_out = _pre_post_norm_bwd(dh, x2, g_pre_ffn2, scale3, dx3, ymix, g_post_mix, gate2,
                                                         1.0, "pre_ffn2_post_mix_bwd")
    sm3 = dict(shift=s3, scale=s2 * g_pre_ffn2, gate=0.5 * g_post_ffn2 * s1,
               g_pre=(1.0 + scale3) * s2, g_post=(0.5 * gate3) * s1)

    dycat = _mm_nt(dymix, wo, "out_proj_bwd", F32)
    dwo = _mm_tn_pair(ycat, dymix, "grad_wo")
    doa, dob, dg_a, dg_b = _group_norm_bwd(dycat, oa, ob, g_grp_a, g_grp_b)
    (dqa, dka, dva, dsink), (slots["wg2"],) = _attention_bwd(
        proj, bias_a, sinks, doa, lse_a, name="attn_a_bwd", carry=scatter(dwg2), **cfg_a)
    (dqb, dkb, dvb, dbias), (slots["wu2"], slots["wo"]) = _attention_bwd(
        proj, bias_b, None, dob, lse_b, name="attn_b_bwd", carry=scatter(dwu2, dwo), **cfg_b)
    dproj = jnp.concatenate([dqa, dka, dva, dqb, dkb, dvb], axis=1)
    db_in = _col_sum(dproj, "grad_b_in")
    dwin = _mm_tn_pair(dproj, h2, "grad_win")
    dh2 = _mm_nn([(dproj, win)], "in_proj_bwd", F32)
    dx1, dy, s2m, s3m, s1, _ = _pre_post_norm_bwd(dh2, x1, g_pre_mix, scale2, dx2, y1, g_post_ffn1, gate1, 0.5,
                                                  "pre_mix_post_ffn1_bwd")
    d_rel = jnp.dot(_diagonal_sums(dbias).reshape(H_B, SKEW), rel_m, precision=lax.Precision.HIGHEST)
    d_sinks = dsink[:, :2 * TPS, 0].reshape(1, H_A)

    (da, db), (slots["win"],) = _ffn_down_bwd(dy, wd1, a1, b1, "ffn_down_bwd_ffn1", carry=scatter(dwin))
    dwd1 = _mm_tn_pair(u1, dy, "grad_wd_ffn1")
    dwg1, (slots["wd1"],) = _mm_tn_pair(da, h1, "grad_wg_ffn1", carry=scatter(dwd1))
    dwu1, (slots["wg1"],) = _mm_tn_pair(db, h1, "grad_wu_ffn1", carry=scatter(dwg1))
    dh, (slots["wu1"],) = _mm_nn([(da, wg1), (db, wu1)], "ffn_up_bwd_ffn1", F32, carry=scatter(dwu1))
    dx0, s2, s3 = _pre_norm_bwd(dh, xs, g_pre_ffn1, scale1, dx1, "pre_norm_bwd_ffn1")
    sm1 = dict(shift=s3, scale=s2 * g_pre_ffn1, gate=0.5 * g_post_ffn1 * s1,
               g_pre=(1.0 + scale1) * s2, g_post=(0.5 * gate1) * s1)

    dmod = jnp.concatenate([sm1["shift"], sm1["scale"], sm1["gate"],
                            s3m, s2m * g_pre_mix, g_post_mix * s1m,
                            sm3["shift"], sm3["scale"], sm3["gate"]], axis=1)
    small_parts = {
        "b_ada": dmod, "g_pre_ffn1": sm1["g_pre"], "g_post_ffn1": sm1["g_post"],
        "g_pre_mix": (1.0 + scale2) * s2m, "b_in": db_in, "sinks_a": d_sinks,
        "rel_bias_b": d_rel.reshape(1, H_B * N_REL), "g_grp_a": dg_a, "g_grp_b": dg_b, "b_out": db_out,
        "g_post_mix": gate2 * s1m, "g_pre_ffn2": sm3["g_pre"], "g_post_ffn2": sm3["g_post"]}
    sizes = [small_parts[n].shape[1] for n in SMALL]
    n_small = sum(sizes)
    n_pad = -(n_small + 1) % LANES
    packed = jnp.concatenate([small_parts[n] for n in SMALL] + [loss_part, jnp.zeros((1, n_pad), F32)], axis=1)
    gathered = _all_gather_small(packed)
    small_sum = _sum_rows8(gathered)
    loss = small_sum[0, n_small]
    dmod_cols = lax.dynamic_slice(gathered.reshape(N_DEV, n_small + 1 + n_pad), (0, me * ada_cols),
                                  (N_DEV, ada_cols))
    g_ada = _ada_weight_grad(sc_all.reshape(N_DEV, d_model).T, dmod_cols)

    out_g, out_d, out_m, out_v = {}, {}, {}, {}
    d_, m_, v_ = _adamw(w_ada[0], g_ada, m_w_ada[0], v_w_ada[0], "adamw_w_ada")
    out_g["w_ada"], out_d["w_ada"], out_m["w_ada"], out_v["w_ada"] = g_ada[None], d_[None], m_[None], v_[None]
    for n, key, transposed in (("w_gate1", "wg1", True), ("w_up1", "wu1", True), ("w_down1", "wd1", False),
                               ("w_in", "win", True), ("w_out", "wo", False), ("w_gate2", "wg2", True),
                               ("w_up2", "wu2", True), ("w_down2", "wd2", False)):
        view = (lambda t: t.T) if transposed else (lambda t: t)
        res = _adamw_from_slots(view(weights[n][0]), slots[key], view(mom_m[n][0]), view(mom_v[n][0]),
                                "adamw_" + n)
        out_g[n], out_d[n], out_m[n], out_v[n] = (view(t)[None] for t in res)

    def pack(tree):
        return jnp.concatenate([tree[n].reshape(1, -1) for n in SMALL], axis=1)

    g_small = small_sum[:, :n_small]
    d_s, m_s, v_s = _adamw(pack(weights), g_small, pack(mom_m), pack(mom_v), "adamw_small")
    off = 0
    for n, size in zip(SMALL, sizes):
        shape = weights[n].shape
        out_g[n] = g_small[:, off:off + size].reshape(shape)
        out_d[n] = d_s[:, off:off + size].reshape(shape)
        out_m[n] = m_s[:, off:off + size].reshape(shape)
        out_v[n] = v_s[:, off:off + size].reshape(shape)
        off += size

    return (loss, dx0[None], *[out_g[n] for n in WEIGHTS], *[out_d[n] for n in WEIGHTS],
            *[out_m[n] for n in WEIGHTS], *[out_v[n] for n in WEIGHTS])
```

```python
import numpy as np
import jax
import jax.numpy as jnp
from jax import lax
from jax.experimental import pallas as pl
from jax.experimental.pallas import tpu as pltpu

F32 = jnp.float32
BF16 = jnp.bfloat16
MESH = pl.DeviceIdType.MESH
ANY = pl.BlockSpec(memory_space=pl.ANY)
VMEM_SPEC = pl.BlockSpec(memory_space=pltpu.VMEM)
SMEM_SPEC = pl.BlockSpec(memory_space=pltpu.SMEM)

N_DEV = 8
CHUNK = 64
HEAD_DIM = 64
LANES = 128
H_A, KV_A, H_B = 8, 2, 8
BACK_A, BACK_B = 2, 8
REL_CLIP = 128
N_REL = 2 * REL_CLIP + 1
QA, KVA, QB = H_A * HEAD_DIM, KV_A * HEAD_DIM, H_B * HEAD_DIM
D_IN = QA + 2 * KVA + 3 * QB
N_MOD = 9
EPS = 1e-6
NEG_INF = -1e30
QG = 4
QROWS = QG * CHUNK
TPS = 2
SKEW = 1024
ADAM_LR, ADAM_B1, ADAM_B2, ADAM_EPS, ADAM_WD, ADAM_STEP = 0.001, 0.9, 0.999, 1e-08, 0.01, 10
VMEM_LIMIT = 56 * 2 ** 20


def _pick(n, cands):
    for c in cands:
        if n % c == 0:
            return c
    return n


def _pieces(n, width=2 * LANES):
    return [(lo, min(lo + width, n)) for lo in range(0, n, width)]


def _params(sem=None):
    return pltpu.CompilerParams(dimension_semantics=sem, vmem_limit_bytes=VMEM_LIMIT)


def _dot_nt(a, b):
    return lax.dot_general(a, b, (((1,), (1,)), ((), ())), preferred_element_type=F32)


def _dot_tn(a, b):
    return lax.dot_general(a, b, (((0,), (0,)), ((), ())), preferred_element_type=F32)


def _dot(a, b):
    return jnp.dot(a, b, preferred_element_type=F32)


def _sigmoid(a):
    return 0.5 * (jnp.tanh(0.5 * a) + 1.0)


def _mesh_pos():
    return lax.axis_index("x"), lax.axis_index("y"), lax.axis_index("c")


def _peer(x, y, c, r):
    px = 1 - x if r & 4 else x
    py = 1 - y if r & 2 else y
    pc = 1 - c if r & 1 else c
    return px, py, pc


class _Carry:
    def __init__(self, ins, out_shapes, scratch, start, finish):
        self.ins, self.out_shapes, self.scratch = list(ins), list(out_shapes), list(scratch)
        self.start, self.finish = start, finish


def _call(body, *, name, grid, in_specs, out_specs, out_shape, args, scratch=(), sem=None, carry=None):
    single = not isinstance(out_shape, (tuple, list))
    out_specs = (out_specs,) if single else tuple(out_specs)
    out_shape = (out_shape,) if single else tuple(out_shape)
    if carry is None:
        res = pl.pallas_call(body, name=name, grid=grid, in_specs=list(in_specs), out_specs=out_specs,
                             out_shape=out_shape, scratch_shapes=list(scratch), compiler_params=_params(sem))(*args)
        return res[0] if single else res
    n_in, n_out, n_s = len(in_specs), len(out_shape), len(scratch)
    ci, co = len(carry.ins), len(carry.out_shapes)

    def wrapped(*refs):
        ins, cins = refs[:n_in], refs[n_in:n_in + ci]
        outs = refs[n_in + ci:n_in + ci + n_out]
        couts = refs[n_in + ci + n_out:n_in + ci + n_out + co]
        scr = refs[n_in + ci + n_out + co:n_in + ci + n_out + co + n_s]
        cscr = refs[n_in + ci + n_out + co + n_s:]
        first, last = None, None
        for ax, n in enumerate(grid):
            f, l = pl.program_id(ax) == 0, pl.program_id(ax) == n - 1
            first = f if first is None else first & f
            last = l if last is None else last & l
        pl.when(first)(lambda: carry.start(cins, couts, cscr))
        body(*ins, *outs, *scr)
        pl.when(last)(lambda: carry.finish(cins, couts, cscr))

    res = pl.pallas_call(
        wrapped, name=name, grid=grid, in_specs=list(in_specs) + [ANY] * ci, out_specs=out_specs + (ANY,) * co,
        out_shape=out_shape + tuple(carry.out_shapes), scratch_shapes=list(scratch) + carry.scratch,
        compiler_params=_params(("arbitrary",) * len(grid)))(*args, *carry.ins)
    main = res[:n_out]
    return (main[0] if single else main), res[n_out:]


def _gather_carry(shards):
    n_w = len(shards)
    rows = [s.shape[0] for s in shards]

    def plan(ins, outs, scr):
        send_sems, recv_sems, local_sems = scr
        x, y, c = _mesh_pos()
        me, sibling = (x, y, c), (x, y, 1 - c)
        chips = [(1 - x, y), (x, 1 - y), (1 - x, 1 - y)]

        def block(w, dev):
            start = pl.multiple_of((4 * dev[0] + 2 * dev[1] + dev[2]) * rows[w], 16)
            return outs[w].at[pl.ds(start, rows[w]), :]

        def copy(w, k, dev, to, src=None):
            return pltpu.make_async_remote_copy(
                src_ref=block(w, dev) if src is None else src, dst_ref=block(w, dev),
                send_sem=send_sems.at[w, k], recv_sem=recv_sems.at[w, k], device_id=to, device_id_type=MESH)

        mine = [pltpu.make_async_copy(ins[w], block(w, me), local_sems.at[w]) for w in range(n_w)]
        first = []
        for j, chip in enumerate(chips):
            first += [copy(w, 1 + j, me, (*chip, c), src=ins[w]) for w in range(n_w)]
        first += [copy(w, 0, me, sibling, src=ins[w]) for w in range(n_w)]
        return c, me, sibling, chips, copy, mine, first

    def start(ins, outs, scr):
        _, _, _, _, _, mine, first = plan(ins, outs, scr)
        for cp in mine + first:
            cp.start()

    def finish(ins, outs, scr):
        c, me, sibling, chips, copy, mine, first = plan(ins, outs, scr)
        passed = []
        for j, chip in enumerate(chips):
            for w in range(n_w):
                copy(w, 1 + j, (*chip, c), me).wait_recv()
                cp = copy(w, 4 + j, (*chip, c), sibling)
                cp.start()
                passed.append(cp)
        for w in range(n_w):
            copy(w, 0, sibling, me).wait_recv()
        for j, chip in enumerate(chips):
            for w in range(n_w):
                copy(w, 4 + j, (*chip, 1 - c), me).wait_recv()
        for cp in first + passed:
            cp.wait_send()
        for cp in mine:
            cp.wait()

    return _Carry(
        shards, [jax.ShapeDtypeStruct((N_DEV * s.shape[0], s.shape[1]), s.dtype) for s in shards],
        [pltpu.SemaphoreType.DMA((n_w, N_DEV - 1)), pltpu.SemaphoreType.DMA((n_w, N_DEV - 1)),
         pltpu.SemaphoreType.DMA((n_w,))], start, finish)


def _scatter_carry(parts):
    n_w = len(parts)
    n_chip = N_DEV // 2
    rows = [g.shape[0] // n_chip for g in parts]

    def plan(ins, outs, scr):
        send_sems, recv_sems, local_sems = scr
        x, y, c = _mesh_pos()

        def src(w, chip_index):
            return ins[w].at[pl.ds(pl.multiple_of(chip_index * rows[w], 16), rows[w]), :]

        mine = [pltpu.make_async_copy(src(w, 2 * x + y), outs[w].at[0], local_sems.at[w]) for w in range(n_w)]
        copies = []
        for r in (3, 2, 1):
            px, py, _ = _peer(x, y, c, 2 * r)
            for w in range(n_w):
                copies.append(pltpu.make_async_remote_copy(
                    src_ref=src(w, 2 * px + py), dst_ref=outs[w].at[r], send_sem=send_sems.at[w, r - 1],
                    recv_sem=recv_sems.at[w, r - 1], device_id=(px, py, c), device_id_type=MESH))
        return mine, copies

    def start(ins, outs, scr):
        mine, copies = plan(ins, outs, scr)
        for cp in mine + copies:
            cp.start()

    def finish(ins, outs, scr):
        mine, copies = plan(ins, outs, scr)
        for cp in copies:
            cp.wait_recv()
        for cp in copies:
            cp.wait_send()
        for cp in mine:
            cp.wait()

    return _Carry(
        parts, [jax.ShapeDtypeStruct((n_chip, r, g.shape[1]), g.dtype) for r, g in zip(rows, parts)],
        [pltpu.SemaphoreType.DMA((n_w, n_chip - 1)), pltpu.SemaphoreType.DMA((n_w, n_chip - 1)),
         pltpu.SemaphoreType.DMA((n_w,))], start, finish)


def _ada_forward(c_row, w_ada, b_cols, carry):
    d = c_row.shape[1]
    wcols = w_ada.shape[1]
    ci, co = len(carry.ins), len(carry.out_shapes)

    def body(*refs):
        c_ref, w_ref, b_ref = refs[:3]
        cins = refs[3:3 + ci]
        sc_ref, mod_ref = refs[3 + ci:5 + ci]
        couts = refs[5 + ci:5 + ci + co]
        rows_ref, send_sems, recv_sems = refs[5 + ci + co:8 + ci + co]
        cscr = refs[8 + ci + co:]
        carry.start(cins, couts, cscr)
        x, y, c = _mesh_pos()
        me = 4 * x + 2 * y + c
        cv = c_ref[...]
        sc_ref[me] = cv * _sigmoid(cv)

        sends = []
        for r in range(1, N_DEV):
            px, py, pc = _peer(x, y, c, r)
            cp = pltpu.make_async_remote_copy(
                src_ref=sc_ref.at[me], dst_ref=sc_ref.at[me], send_sem=send_sems.at[0, r - 1],
                recv_sem=recv_sems.at[0, r - 1], device_id=(px, py, pc), device_id_type=MESH)
            cp.start()
            sends.append(cp)
        for r in range(1, N_DEV):
            px, py, pc = _peer(x, y, c, r)
            pid = 4 * px + 2 * py + pc
            pltpu.make_async_remote_copy(
                src_ref=sc_ref.at[pid], dst_ref=sc_ref.at[pid], send_sem=send_sems.at[0, r - 1],
                recv_sem=recv_sems.at[0, r - 1], device_id=(px, py, pc), device_id_type=MESH).wait_recv()
        for cp in sends:
            cp.wait_send()

        sc_all = jnp.concatenate([sc_ref[j] for j in range(N_DEV)], axis=0)
        rows = _dot(sc_all.astype(BF16), w_ref[...].astype(BF16)) + b_ref[...]
        for j in range(N_DEV):
            rows_ref[j] = rows[j:j + 1, :]
        mod_ref[me] = rows_ref[me]

        sends = []
        for r in range(1, N_DEV):
            px, py, pc = _peer(x, y, c, r)
            pid = 4 * px + 2 * py + pc
            cp = pltpu.make_async_remote_copy(
                src_ref=rows_ref.at[pid], dst_ref=mod_ref.at[me], send_sem=send_sems.at[1, r - 1],
                recv_sem=recv_sems.at[1, r - 1], device_id=(px, py, pc), device_id_type=MESH)
            cp.start()
            sends.append(cp)
        for r in range(1, N_DEV):
            px, py, pc = _peer(x, y, c, r)
            pid = 4 * px + 2 * py + pc
            pltpu.make_async_remote_copy(
                src_ref=rows_ref.at[pid], dst_ref=mod_ref.at[pid], send_sem=send_sems.at[1, r - 1],
                recv_sem=recv_sems.at[1, r - 1], device_id=(px, py, pc), device_id_type=MESH).wait_recv()
        for cp in sends:
            cp.wait_send()
        carry.finish(cins, couts, cscr)

    res = pl.pallas_call(
        body, name="ada_forward",
        out_shape=(jax.ShapeDtypeStruct((N_DEV, 1, d), F32), jax.ShapeDtypeStruct((N_DEV, 1, wcols), F32),
                   *carry.out_shapes),
        in_specs=[VMEM_SPEC, VMEM_SPEC, VMEM_SPEC] + [ANY] * ci, out_specs=(VMEM_SPEC, VMEM_SPEC) + (ANY,) * co,
        scratch_shapes=[pltpu.VMEM((N_DEV, 1, wcols), F32), pltpu.SemaphoreType.DMA((2, N_DEV - 1)),
                        pltpu.SemaphoreType.DMA((2, N_DEV - 1))] + carry.scratch,
        compiler_params=_params(),
    )(c_row, w_ada, b_cols, *carry.ins)
    return res[:2], res[2:]


def _all_gather_small(v):
    n = v.shape[1]

    def body(v_ref, out_ref, send_sems, recv_sems):
        x, y, c = _mesh_pos()
        me = 4 * x + 2 * y + c
        out_ref[me] = v_ref[...]
        sends = []
        for r in range(1, N_DEV):
            px, py, pc = _peer(x, y, c, r)
            cp = pltpu.make_async_remote_copy(
                src_ref=v_ref, dst_ref=out_ref.at[me], send_sem=send_sems.at[r - 1],
                recv_sem=recv_sems.at[r - 1], device_id=(px, py, pc), device_id_type=MESH)
            cp.start()
            sends.append(cp)
        for r in range(1, N_DEV):
            px, py, pc = _peer(x, y, c, r)
            pid = 4 * px + 2 * py + pc
            pltpu.make_async_remote_copy(
                src_ref=v_ref, dst_ref=out_ref.at[pid], send_sem=send_sems.at[r - 1],
                recv_sem=recv_sems.at[r - 1], device_id=(px, py, pc), device_id_type=MESH).wait_recv()
        for cp in sends:
            cp.wait_send()

    return pl.pallas_call(
        body, name="all_gather_small",
        out_shape=jax.ShapeDtypeStruct((N_DEV, 1, n), F32),
        in_specs=[VMEM_SPEC], out_specs=VMEM_SPEC,
        scratch_shapes=[pltpu.SemaphoreType.DMA((N_DEV - 1,)), pltpu.SemaphoreType.DMA((N_DEV - 1,))],
        compiler_params=_params(),
    )(v)


def _mm_nt(a, b, name, out_dtype, bias=None, carry=None):
    m, k = a.shape
    n = b.shape[0]
    tm = _pick(m, (512, 256, 128))
    tn = _pick(n, (1408, 1152, 1024, 768, 512, 256, 128))

    def body(*refs):
        acc = _dot_nt(refs[0][...], refs[1][...])
        if bias is not None:
            acc = acc + refs[2][...]
        refs[-1][...] = acc.astype(out_dtype)

    in_specs = [pl.BlockSpec((tm, k), lambda i, j: (i, 0)), pl.BlockSpec((tn, k), lambda i, j: (j, 0))]
    args = [a, b]
    if bias is not None:
        in_specs.append(pl.BlockSpec((1, tn), lambda i, j: (0, j)))
        args.append(bias)
    return _call(body, name=name, grid=(m // tm, n // tn), in_specs=in_specs,
                 out_specs=pl.BlockSpec((tm, tn), lambda i, j: (i, j)),
                 out_shape=jax.ShapeDtypeStruct((m, n), out_dtype), args=args,
                 sem=("parallel", "parallel"), carry=carry)


class _Tail:
    def __init__(self, rows, vecs, outs, fn):
        self.rows, self.vecs, self.outs, self.fn = list(rows), list(vecs), list(outs), fn


def _mm_nn(pairs, name, out_dtype, bias=None, carry=None, tail=None):
    m, k = pairs[0][0].shape
    n = pairs[0][1].shape[1]
    n_p = len(pairs)
    tm = _pick(m, (512, 256, 128))
    tk = k if n_p == 1 else _pick(k, (1408, 1152, 1024, 768, 512, 256, 128))
    nk = k // tk
    n_b = 0 if bias is None else 1
    n_r, n_v = (len(tail.rows), len(tail.vecs)) if tail else (0, 0)
    n_in = 2 * n_p + n_b + n_r + n_v
    n_main = 0 if out_dtype is None else 1

    def finish(acc, refs, first_tile):
        if bias is not None:
            acc = acc + refs[2 * n_p][...]
        outs = refs[n_in:-1]
        if n_main:
            outs[0][...] = acc.astype(out_dtype)
        if tail is None:
            return
        rows = [r[...] for r in refs[2 * n_p + n_b:2 * n_p + n_b + n_r]]
        vecs = [v[...] for v in refs[2 * n_p + n_b + n_r:n_in]]
        vals = tail.fn(acc, rows, vecs)
        for ref, val, (dtype, kind) in zip(outs[n_main:], vals, tail.outs):
            if kind == "row":
                ref[...] = val.astype(dtype)
            else:
                @pl.when(first_tile)
                def _(ref=ref):
                    ref[...] = jnp.zeros_like(ref)

                ref[...] += val

    def body(*refs):
        acc_ref = refs[-1]
        kk = pl.program_id(1)
        first_tile = pl.program_id(0) == 0

        if nk == 1:
            acc = _dot(refs[0][...], refs[1][...])
            for p in range(1, n_p):
                acc = acc + _dot(refs[2 * p][...], refs[2 * p + 1][...])
            finish(acc, refs, first_tile)
            return

        @pl.when(kk == 0)
        def _():
            acc_ref[...] = jnp.zeros_like(acc_ref)

        for p in range(n_p):
            acc_ref[...] += _dot(refs[2 * p][...], refs[2 * p + 1][...])

        @pl.when(kk == nk - 1)
        def _():
            finish(acc_ref[...], refs, first_tile)

    row_spec = pl.BlockSpec((tm, n), lambda i, kk: (i, 0))
    in_specs, args = [], []
    for a, b in pairs:
        in_specs += [pl.BlockSpec((tm, tk), lambda i, kk: (i, kk)), pl.BlockSpec((tk, n), lambda i, kk: (kk, 0))]
        args += [a, b]
    if bias is not None:
        in_specs.append(pl.BlockSpec((1, n), lambda i, kk: (0, 0)))
        args.append(bias)
    out_specs = [row_spec] * n_main
    out_shape = [jax.ShapeDtypeStruct((m, n), out_dtype)] if n_main else []
    sums = False
    if tail:
        in_specs += [row_spec] * n_r + [pl.BlockSpec((1, n), lambda i, kk: (0, 0))] * n_v
        args += tail.rows + tail.vecs
        for dtype, kind in tail.outs:
            if kind == "row":
                out_specs.append(row_spec)
                out_shape.append(jax.ShapeDtypeStruct((m, n), dtype))
            else:
                sums = True
                width = n if kind == "sum" else 1
                out_specs.append(pl.BlockSpec((1, width), lambda i, kk: (0, 0)))
                out_shape.append(jax.ShapeDtypeStruct((1, width), dtype))
    if tail is None:
        out_specs, out_shape = out_specs[0], out_shape[0]
    return _call(body, name=name, grid=(m // tm, nk), in_specs=in_specs, out_specs=out_specs,
                 out_shape=out_shape, args=args,
                 scratch=[pltpu.VMEM((tm, n) if nk > 1 else (8, LANES), F32)],
                 sem=("arbitrary" if sums else "parallel", "arbitrary"), carry=carry)


def _rms(v):
    return lax.rsqrt(jnp.mean(v * v, axis=-1, keepdims=True) + EPS)


def _col(v):
    return jnp.sum(v, axis=0, keepdims=True)


def _tail_post_pre(x, g_post, gate, weight, g_pre, scale, shift):
    def fn(y, rows, vecs):
        (xv,), (gp, gt, g, sc, sh) = rows, vecs
        xo = xv + (weight * gt) * ((y * _rms(y)) * gp)
        return xo, ((xo * _rms(xo)) * g) * (1.0 + sc) + sh

    return _Tail([x], [g_post, gate, g_pre, scale, shift], [(F32, "row"), (BF16, "row")], fn)


def _tail_post_loss(x, target, g, gate, weight):
    def fn(y, rows, vecs):
        (xv, tv), (gv, gt) = rows, vecs
        r = _rms(y)
        yn = y * r
        err = (xv + (weight * gt) * (yn * gv)) - tv
        do = err * (1.0 / y.shape[1])
        dyn = do * ((weight * gt) * gv)
        dy = r * (dyn - yn * jnp.mean(dyn * yn, axis=-1, keepdims=True))
        return do, dy, 0.5 * _col(jnp.mean(err * err, axis=-1, keepdims=True)), _col(do * yn)

    return _Tail([x, target], [g, gate], [(F32, "row"), (BF16, "row"), (F32, "one"), (F32, "sum")], fn)


def _tail_pre_bwd(x, dres, g_pre, scale):
    def fn(dh, rows, vecs):
        (xv, dr), (g, sc) = rows, vecs
        r = _rms(xv)
        n = xv * r
        dn = dh * (g * (1.0 + sc))
        return dr + r * (dn - n * jnp.mean(dn * n, axis=-1, keepdims=True)), _col(dh * n), _col(dh)

    return _Tail([x, dres], [g_pre, scale], [(F32, "row"), (F32, "sum"), (F32, "sum")], fn)


def _tail_pre_post_bwd(x, dres, y, g_pre, scale, g_post, gate, weight):
    def fn(dh, rows, vecs):
        (xv, dr, yv), (g, sc, gp, gt) = rows, vecs
        r = _rms(xv)
        n = xv * r
        dn = dh * (g * (1.0 + sc))
        dx = dr + r * (dn - n * jnp.mean(dn * n, axis=-1, keepdims=True))
        ry = _rms(yv)
        yn = yv * ry
        dyn = dx * ((weight * gt) * gp)
        dy = ry * (dyn - yn * jnp.mean(dyn * yn, axis=-1, keepdims=True))
        return dx, dy, _col(dh * n), _col(dh), _col(dx * yn), _col(dy)

    return _Tail([x, dres, y], [g_pre, scale, g_post, gate],
                 [(F32, "row"), (BF16, "row")] + [(F32, "sum")] * 4, fn)


def _mm_tn(a, b, name, out_dtype=BF16, carry=None):
    k, m = a.shape
    n = b.shape[1]
    tm = _pick(m, (1408, 1152, 1024, 768, 512, 256, 128))
    tk = _pick(k, (512, 256, 128))
    nk = k // tk

    def body(a_ref, b_ref, o_ref, acc_ref):
        kk = pl.program_id(1)

        @pl.when(kk == 0)
        def _():
            acc_ref[...] = jnp.zeros_like(acc_ref)

        acc_ref[...] += _dot_tn(a_ref[...], b_ref[...])

        @pl.when(kk == nk - 1)
        def _():
            o_ref[...] = acc_ref[...].astype(out_dtype)

    return _call(body, name=name, grid=(m // tm, nk),
                 in_specs=[pl.BlockSpec((tk, tm), lambda i, kk: (kk, i)), pl.BlockSpec((tk, n), lambda i, kk: (kk, 0))],
                 out_specs=pl.BlockSpec((tm, n), lambda i, kk: (i, 0)),
                 out_shape=jax.ShapeDtypeStruct((m, n), out_dtype), args=[a, b],
                 scratch=[pltpu.VMEM((tm, n), F32)], sem=("parallel", "arbitrary"), carry=carry)


def _mm_tn_pair(a, b, name, carry=None):
    k, m = a.shape
    n = b.shape[1]
    rows = m // N_DEV
    n_chip = N_DEV // 2
    tm = 4 * rows
    tk = _pick(k, (1024, 512, 256, 128))
    nk = k // tk

    def body(a_ref, b_ref, p_ref, acc_ref, keep_ref, send_ref, land_ref, send_sems, recv_sems):
        i, kk = pl.program_id(0), pl.program_id(1)
        x, y, c = _mesh_pos()

        def push(chip):
            return pltpu.make_async_remote_copy(
                src_ref=send_ref.at[chip], dst_ref=land_ref.at[chip], send_sem=send_sems.at[chip],
                recv_sem=recv_sems.at[chip], device_id=(x, y, 1 - c), device_id_type=MESH)

        if nk == 1:
            acc = _dot_tn(a_ref[...], b_ref[...])
        else:
            @pl.when(kk == 0)
            def _():
                acc_ref[...] = jnp.zeros_like(acc_ref)

            acc_ref[...] += _dot_tn(a_ref[...], b_ref[...])
            acc = acc_ref

        for t in range(2):
            @pl.when((kk == nk - 1) & (i == t))
            def _(t=t):
                for ob in range(4):
                    chip, core = 2 * t + ob // 2, ob % 2
                    blk = acc[ob * rows:(ob + 1) * rows, :]

                    @pl.when(c == core)
                    def _(chip=chip, blk=blk):
                        keep_ref[chip] = blk

                    @pl.when(c != core)
                    def _(chip=chip, blk=blk):
                        send_ref[chip] = blk.astype(BF16)
                        push(chip).start()

        @pl.when((kk == nk - 1) & (i == 1))
        def _():
            for chip in range(n_chip):
                push(chip).wait_recv()
                p_ref[chip * rows:(chip + 1) * rows, :] = (
                    keep_ref[chip] + land_ref[chip].astype(F32)).astype(BF16)
            for chip in range(n_chip):
                push(chip).wait_send()

    return _call(body, name=name, grid=(2, nk),
                 in_specs=[pl.BlockSpec((tk, tm), lambda i, kk: (kk, i)), pl.BlockSpec((tk, n), lambda i, kk: (kk, 0))],
                 out_specs=pl.BlockSpec((n_chip * rows, n), lambda i, kk: (0, 0)),
                 out_shape=jax.ShapeDtypeStruct((n_chip * rows, n), BF16), args=[a, b],
                 scratch=[pltpu.VMEM((tm, n) if nk > 1 else (8, LANES), F32), pltpu.VMEM((n_chip, rows, n), F32),
                          pltpu.VMEM((n_chip, rows, n), BF16), pltpu.VMEM((n_chip, rows, n), BF16),
                          pltpu.SemaphoreType.DMA((n_chip,)), pltpu.SemaphoreType.DMA((n_chip,))],
                 sem=("arbitrary", "arbitrary"), carry=carry)


def _ffn_up(h, wg_t, wu_t, name, carry=None):
    s, d = h.shape
    f = wg_t.shape[0]
    tm = _pick(s, (512, 256, 128))
    tf = _pick(f, (1408, 1024, 512, 256, 128))

    def body(h_ref, wg_ref, wu_ref, a_ref, b_ref, u_ref):
        hh = h_ref[...]
        for lo, hi in _pieces(tf):
            a = _dot_nt(hh, wg_ref[lo:hi, :])
            b = _dot_nt(hh, wu_ref[lo:hi, :])
            a_ref[:, lo:hi] = a.astype(BF16)
            b_ref[:, lo:hi] = b.astype(BF16)
            u_ref[:, lo:hi] = ((a * _sigmoid(a)) * b).astype(BF16)

    w_spec = pl.BlockSpec((tf, d), lambda i, j: (j, 0))
    o_spec = pl.BlockSpec((tm, tf), lambda i, j: (i, j))
    o_shape = jax.ShapeDtypeStruct((s, f), BF16)
    return _call(body, name=name, grid=(s // tm, f // tf),
                 in_specs=[pl.BlockSpec((tm, d), lambda i, j: (i, 0)), w_spec, w_spec],
                 out_specs=(o_spec, o_spec, o_spec), out_shape=(o_shape, o_shape, o_shape),
                 args=[h, wg_t, wu_t], sem=("parallel", "parallel"), carry=carry)


def _ffn_down_bwd(dy, wd, a, b, name, carry=None):
    s, d = dy.shape
    f = wd.shape[0]
    tm = _pick(s, (512, 256, 128))
    tf = _pick(f, (1408, 1024, 512, 256, 128))

    def body(dy_ref, wd_ref, a_ref, b_ref, da_ref, db_ref):
        dyv = dy_ref[...]
        for lo, hi in _pieces(tf):
            du = _dot_nt(dyv, wd_ref[lo:hi, :])
            a = a_ref[:, lo:hi].astype(F32)
            b = b_ref[:, lo:hi].astype(F32)
            sig = _sigmoid(a)
            da_ref[:, lo:hi] = (du * b * (sig * (1.0 + a * (1.0 - sig)))).astype(BF16)
            db_ref[:, lo:hi] = (du * (a * sig)).astype(BF16)

    t_spec = pl.BlockSpec((tm, tf), lambda i, j: (i, j))
    o_shape = jax.ShapeDtypeStruct((s, f), BF16)
    return _call(body, name=name, grid=(s // tm, f // tf),
                 in_specs=[pl.BlockSpec((tm, d), lambda i, j: (i, 0)), pl.BlockSpec((tf, d), lambda i, j: (j, 0)),
                           t_spec, t_spec],
                 out_specs=(t_spec, t_spec), out_shape=(o_shape, o_shape), args=[dy, wd, a, b],
                 sem=("parallel", "parallel"), carry=carry)


def _row_tile(s):
    return _pick(s, (256, 128, 64))


def _vec_spec(d):
    return pl.BlockSpec((1, d), lambda i: (0, 0))


def _pre_norm(x, g, scale, shift, name):
    s, d = x.shape
    ts = _row_tile(s)

    def body(x_ref, g_ref, sc_ref, sh_ref, h_ref):
        xv = x_ref[...]
        r = lax.rsqrt(jnp.mean(xv * xv, axis=-1, keepdims=True) + EPS)
        h_ref[...] = (((xv * r) * g_ref[...]) * (1.0 + sc_ref[...]) + sh_ref[...]).astype(BF16)

    row = pl.BlockSpec((ts, d), lambda i: (i, 0))
    return _call(body, name=name, grid=(s // ts,), in_specs=[row, _vec_spec(d), _vec_spec(d), _vec_spec(d)],
                 out_specs=row, out_shape=jax.ShapeDtypeStruct((s, d), BF16), args=[x, g, scale, shift],
                 sem=("parallel",))


def _post_norm_residual(x, y, g, gate, weight, name):
    s, d = x.shape
    ts = _row_tile(s)

    def body(x_ref, y_ref, g_ref, gate_ref, o_ref):
        yv = y_ref[...]
        r = lax.rsqrt(jnp.mean(yv * yv, axis=-1, keepdims=True) + EPS)
        o_ref[...] = x_ref[...] + (weight * gate_ref[...]) * ((yv * r) * g_ref[...])

    row = pl.BlockSpec((ts, d), lambda i: (i, 0))
    return _call(body, name=name, grid=(s // ts,), in_specs=[row, row, _vec_spec(d), _vec_spec(d)],
                 out_specs=row, out_shape=jax.ShapeDtypeStruct((s, d), F32), args=[x, y, g, gate],
                 sem=("parallel",))


def _post_norm_bwd(dout, y, g, gate, weight, name):
    s, d = y.shape
    ts = _row_tile(s)

    def body(do_ref, y_ref, g_ref, gate_ref, dy_ref, s1_ref, cs_ref):
        @pl.when(pl.program_id(0) == 0)
        def _():
            s1_ref[...] = jnp.zeros_like(s1_ref)
            cs_ref[...] = jnp.zeros_like(cs_ref)

        yv = y_ref[...]
        do = do_ref[...]
        r = lax.rsqrt(jnp.mean(yv * yv, axis=-1, keepdims=True) + EPS)
        yn = yv * r
        dyn = do * ((weight * gate_ref[...]) * g_ref[...])
        dy = r * (dyn - yn * jnp.mean(dyn * yn, axis=-1, keepdims=True))
        dy_ref[...] = dy.astype(BF16)
        s1_ref[...] += jnp.sum(do * yn, axis=0, keepdims=True)
        cs_ref[...] += jnp.sum(dy, axis=0, keepdims=True)

    row = pl.BlockSpec((ts, d), lambda i: (i, 0))
    vec = jax.ShapeDtypeStruct((1, d), F32)
    return _call(body, name=name, grid=(s // ts,), in_specs=[row, row, _vec_spec(d), _vec_spec(d)],
                 out_specs=(row, _vec_spec(d), _vec_spec(d)),
                 out_shape=(jax.ShapeDtypeStruct((s, d), BF16), vec, vec), args=[dout, y, g, gate],
                 sem=("arbitrary",))


def _pre_norm_bwd(dh, x, g, scale, dres, name):
    s, d = x.shape
    ts = _row_tile(s)

    def body(dh_ref, x_ref, g_ref, sc_ref, dr_ref, dx_ref, s2_ref, s3_ref):
        @pl.when(pl.program_id(0) == 0)
        def _():
            s2_ref[...] = jnp.zeros_like(s2_ref)
            s3_ref[...] = jnp.zeros_like(s3_ref)

        xv = x_ref[...]
        dh = dh_ref[...]
        r = lax.rsqrt(jnp.mean(xv * xv, axis=-1, keepdims=True) + EPS)
        n = xv * r
        dn = dh * (g_ref[...] * (1.0 + sc_ref[...]))
        dx_ref[...] = dr_ref[...] + r * (dn - n * jnp.mean(dn * n, axis=-1, keepdims=True))
        s2_ref[...] += jnp.sum(dh * n, axis=0, keepdims=True)
        s3_ref[...] += jnp.sum(dh, axis=0, keepdims=True)

    row = pl.BlockSpec((ts, d), lambda i: (i, 0))
    vec = jax.ShapeDtypeStruct((1, d), F32)
    return _call(body, name=name, grid=(s // ts,), in_specs=[row, row, _vec_spec(d), _vec_spec(d), row],
                 out_specs=(row, _vec_spec(d), _vec_spec(d)),
                 out_shape=(jax.ShapeDtypeStruct((s, d), F32), vec, vec), args=[dh, x, g, scale, dres],
                 sem=("arbitrary",))


def _post_pre_norm(x, y, g_post, gate, weight, g_pre, scale, shift, name):
    s, d = x.shape
    ts = _row_tile(s)

    def body(x_ref, y_ref, gp_ref, gate_ref, g_ref, sc_ref, sh_ref, o_ref, h_ref):
        yv = y_ref[...]
        r = lax.rsqrt(jnp.mean(yv * yv, axis=-1, keepdims=True) + EPS)
        xv = x_ref[...] + (weight * gate_ref[...]) * ((yv * r) * gp_ref[...])
        o_ref[...] = xv
        r2 = lax.rsqrt(jnp.mean(xv * xv, axis=-1, keepdims=True) + EPS)
        h_ref[...] = (((xv * r2) * g_ref[...]) * (1.0 + sc_ref[...]) + sh_ref[...]).astype(BF16)

    row = pl.BlockSpec((ts, d), lambda i: (i, 0))
    return _call(body, name=name, grid=(s // ts,), in_specs=[row, row] + [_vec_spec(d)] * 5,
                 out_specs=(row, row),
                 out_shape=(jax.ShapeDtypeStruct((s, d), F32), jax.ShapeDtypeStruct((s, d), BF16)),
                 args=[x, y, g_post, gate, g_pre, scale, shift], sem=("parallel",))


def _post_norm_loss_bwd(x, y, g, gate, weight, target, name):
    s, d = y.shape
    ts = _row_tile(s)

    def body(x_ref, y_ref, g_ref, gate_ref, t_ref, dx_ref, dy_ref, l_ref, s1_ref):
        @pl.when(pl.program_id(0) == 0)
        def _():
            l_ref[...] = jnp.zeros_like(l_ref)
            s1_ref[...] = jnp.zeros_like(s1_ref)

        yv = y_ref[...]
        r = lax.rsqrt(jnp.mean(yv * yv, axis=-1, keepdims=True) + EPS)
        yn = yv * r
        err = (x_ref[...] + (weight * gate_ref[...]) * (yn * g_ref[...])) - t_ref[...]
        do = err * (1.0 / d)
        dx_ref[...] = do
        l_ref[...] += 0.5 * jnp.sum(jnp.mean(err * err, axis=-1, keepdims=True), axis=0, keepdims=True)
        dyn = do * ((weight * gate_ref[...]) * g_ref[...])
        dy_ref[...] = (r * (dyn - yn * jnp.mean(dyn * yn, axis=-1, keepdims=True))).astype(BF16)
        s1_ref[...] += jnp.sum(do * yn, axis=0, keepdims=True)

    row = pl.BlockSpec((ts, d), lambda i: (i, 0))
    return _call(body, name=name, grid=(s // ts,), in_specs=[row, row, _vec_spec(d), _vec_spec(d), row],
                 out_specs=(row, row, pl.BlockSpec((1, 1), lambda i: (0, 0)), _vec_spec(d)),
                 out_shape=(jax.ShapeDtypeStruct((s, d), F32), jax.ShapeDtypeStruct((s, d), BF16),
                            jax.ShapeDtypeStruct((1, 1), F32), jax.ShapeDtypeStruct((1, d), F32)),
                 args=[x, y, g, gate, target], sem=("arbitrary",))


def _pre_post_norm_bwd(dh, x, g_pre, scale, dres, y, g_post, gate, weight, name):
    s, d = x.shape
    ts = _row_tile(s)

    def body(dh_ref, x_ref, g_ref, sc_ref, dr_ref, y_ref, gp_ref, gate_ref,
             dx_ref, dy_ref, s2_ref, s3_ref, s1_ref, cs_ref):
        @pl.when(pl.program_id(0) == 0)
        def _():
            for ref in (s2_ref, s3_ref, s1_ref, cs_ref):
                ref[...] = jnp.zeros_like(ref)

        xv = x_ref[...]
        dh = dh_ref[...]
        r = lax.rsqrt(jnp.mean(xv * xv, axis=-1, keepdims=True) + EPS)
        n = xv * r
        dn = dh * (g_ref[...] * (1.0 + sc_ref[...]))
        dx = dr_ref[...] + r * (dn - n * jnp.mean(dn * n, axis=-1, keepdims=True))
        dx_ref[...] = dx
        s2_ref[...] += jnp.sum(dh * n, axis=0, keepdims=True)
        s3_ref[...] += jnp.sum(dh, axis=0, keepdims=True)
        yv = y_ref[...]
        ry = lax.rsqrt(jnp.mean(yv * yv, axis=-1, keepdims=True) + EPS)
        yn = yv * ry
        dyn = dx * ((weight * gate_ref[...]) * gp_ref[...])
        dy = ry * (dyn - yn * jnp.mean(dyn * yn, axis=-1, keepdims=True))
        dy_ref[...] = dy.astype(BF16)
        s1_ref[...] += jnp.sum(dx * yn, axis=0, keepdims=True)
        cs_ref[...] += jnp.sum(dy, axis=0, keepdims=True)

    row = pl.BlockSpec((ts, d), lambda i: (i, 0))
    vec = jax.ShapeDtypeStruct((1, d), F32)
    return _call(body, name=name, grid=(s // ts,),
                 in_specs=[row, row, _vec_spec(d), _vec_spec(d), row, row, _vec_spec(d), _vec_spec(d)],
                 out_specs=(row, row) + (_vec_spec(d),) * 4,
                 out_shape=(jax.ShapeDtypeStruct((s, d), F32), jax.ShapeDtypeStruct((s, d), BF16), vec, vec, vec, vec),
                 args=[dh, x, g_pre, scale, dres, y, g_post, gate], sem=("arbitrary",))


def _group_norm_cat(oa, ob, ga, gb):
    s = oa.shape[0]
    ts = _row_tile(s)

    def body(oa_ref, ob_ref, ga_ref, gb_ref, y_ref):
        for o_ref, g_ref, lo, w in ((oa_ref, ga_ref, 0, QA), (ob_ref, gb_ref, QA, QB)):
            ov = o_ref[...]
            r = lax.rsqrt(jnp.mean(ov * ov, axis=-1, keepdims=True) + EPS)
            y_ref[:, lo:lo + w] = ((ov * r) * g_ref[...]).astype(BF16)

    return _call(body, name="group_norm_cat", grid=(s // ts,),
                 in_specs=[pl.BlockSpec((ts, QA), lambda i: (i, 0)), pl.BlockSpec((ts, QB), lambda i: (i, 0)),
                           _vec_spec(QA), _vec_spec(QB)],
                 out_specs=pl.BlockSpec((ts, QA + QB), lambda i: (i, 0)),
                 out_shape=jax.ShapeDtypeStruct((s, QA + QB), BF16), args=[oa, ob, ga, gb], sem=("parallel",))


def _group_norm_bwd(dy, oa, ob, ga, gb):
    s = oa.shape[0]
    ts = _row_tile(s)

    def body(dy_ref, oa_ref, ob_ref, ga_ref, gb_ref, doa_ref, dob_ref, dga_ref, dgb_ref):
        @pl.when(pl.program_id(0) == 0)
        def _():
            dga_ref[...] = jnp.zeros_like(dga_ref)
            dgb_ref[...] = jnp.zeros_like(dgb_ref)

        for o_ref, g_ref, do_ref, dg_ref, lo, w in ((oa_ref, ga_ref, doa_ref, dga_ref, 0, QA),
                                                    (ob_ref, gb_ref, dob_ref, dgb_ref, QA, QB)):
            ov = o_ref[...]
            dyv = dy_ref[:, lo:lo + w]
            r = lax.rsqrt(jnp.mean(ov * ov, axis=-1, keepdims=True) + EPS)
            n = ov * r
            dn = dyv * g_ref[...]
            do_ref[...] = r * (dn - n * jnp.mean(dn * n, axis=-1, keepdims=True))
            dg_ref[...] += jnp.sum(dyv * n, axis=0, keepdims=True)

    ra = pl.BlockSpec((ts, QA), lambda i: (i, 0))
    rb = pl.BlockSpec((ts, QB), lambda i: (i, 0))
    return _call(body, name="group_norm_bwd", grid=(s // ts,),
                 in_specs=[pl.BlockSpec((ts, QA + QB), lambda i: (i, 0)), ra, rb, _vec_spec(QA), _vec_spec(QB)],
                 out_specs=(ra, rb, _vec_spec(QA), _vec_spec(QB)),
                 out_shape=(jax.ShapeDtypeStruct((s, QA), F32), jax.ShapeDtypeStruct((s, QB), F32),
                            jax.ShapeDtypeStruct((1, QA), F32), jax.ShapeDtypeStruct((1, QB), F32)),
                 args=[dy, oa, ob, ga, gb], sem=("arbitrary",))


def _loss_and_grad(y, target):
    s, d = y.shape
    ts = _row_tile(s)

    def body(y_ref, t_ref, l_ref, g_ref):
        @pl.when(pl.program_id(0) == 0)
        def _():
            l_ref[...] = jnp.zeros_like(l_ref)

        err = y_ref[...] - t_ref[...]
        g_ref[...] = err * (1.0 / d)
        row = jnp.mean(err * err, axis=-1, keepdims=True)
        l_ref[...] += 0.5 * jnp.sum(row, axis=0, keepdims=True)

    row = pl.BlockSpec((ts, d), lambda i: (i, 0))
    return _call(body, name="loss_and_grad", grid=(s // ts,), in_specs=[row, row],
                 out_specs=(pl.BlockSpec((1, 1), lambda i: (0, 0)), row),
                 out_shape=(jax.ShapeDtypeStruct((1, 1), F32), jax.ShapeDtypeStruct((s, d), F32)),
                 args=[y, target], sem=("arbitrary",))


def _col_sum(x, name):
    s, n = x.shape
    ts = _row_tile(s)

    def body(x_ref, o_ref):
        @pl.when(pl.program_id(0) == 0)
        def _():
            o_ref[...] = jnp.zeros_like(o_ref)

        o_ref[...] += jnp.sum(x_ref[...].astype(F32), axis=0, keepdims=True)

    return _call(body, name=name, grid=(s // ts,), in_specs=[pl.BlockSpec((ts, n), lambda i: (i, 0))],
                 out_specs=pl.BlockSpec((1, n), lambda i: (0, 0)), out_shape=jax.ShapeDtypeStruct((1, n), F32),
                 args=[x], sem=("arbitrary",))


def _n_variants(n_back):
    return -(-n_back // QG) + 1


def _alibi_bias():
    i = np.arange(QROWS)[:, None]
    j = np.arange((QG + BACK_A) * CHUNK)[None, :]
    dist = np.abs(BACK_A * CHUNK + i - j).astype(np.float32)
    dc = j // CHUNK - i // CHUNK
    valid = (dc >= 0) & (dc <= BACK_A)
    slopes = np.array([2.0 ** (-8.0 * (h + 1) / H_A) for h in range(H_A)], dtype=np.float32)
    bias = -slopes[:, None, None] * dist[None]
    out = [np.where((valid & (j >= (BACK_A - QG * v) * CHUNK))[None], bias, np.float32(NEG_INF))
           for v in range(_n_variants(BACK_A))]
    return jnp.asarray(np.stack(out).astype(np.float32))


def _rel_index_matrix():
    cc = np.arange(SKEW)
    dist = np.where(cc < SKEW - QROWS, BACK_B * CHUNK - cc, BACK_B * CHUNK + SKEW - cc)
    idx = np.clip(dist, -REL_CLIP, REL_CLIP) + REL_CLIP
    m = np.zeros((SKEW, N_REL), np.float32)
    m[cc, idx] = 1.0
    return jnp.asarray(m)


def _toeplitz_bias(vec, carry=None):
    lk = (QG + BACK_B) * CHUNK
    nv = _n_variants(BACK_B)

    def body(v_ref, o_ref):
        xv = jnp.broadcast_to(v_ref[0], (QROWS, SKEW))
        row = lax.broadcasted_iota(jnp.int32, (QROWS, SKEW), 0)
        for bit in range(QROWS.bit_length() - 1):
            xv = jnp.where((row >> bit) & 1 == 1, pltpu.roll(xv, 1 << bit, 1), xv)
        ri = lax.broadcasted_iota(jnp.int32, (QROWS, lk), 0) // CHUNK
        col = lax.broadcasted_iota(jnp.int32, (QROWS, lk), 1)
        ci = col // CHUNK
        valid = (ci - ri >= 0) & (ci - ri <= BACK_B)
        for v in range(nv):
            o_ref[v, 0] = jnp.where(valid & (col >= (BACK_B - QG * v) * CHUNK), xv[:, :lk], NEG_INF)

    return _call(body, name="toeplitz_bias", grid=(H_B,),
                 in_specs=[pl.BlockSpec((1, 1, SKEW), lambda h: (h, 0, 0))],
                 out_specs=pl.BlockSpec((nv, 1, QROWS, lk), lambda h: (0, h, 0, 0)),
                 out_shape=jax.ShapeDtypeStruct((nv, H_B, QROWS, lk), F32), args=[vec], sem=("parallel",),
                 carry=carry)


def _diagonal_sums(dbias):
    lk = dbias.shape[2]

    def body(d_ref, o_ref):
        xp = jnp.concatenate([d_ref[0], jnp.zeros((QROWS, SKEW - lk), F32)], axis=1)
        xv = xp[0:CHUNK]
        for q in range(1, QG):
            xv = xv + pltpu.roll(xp[q * CHUNK:(q + 1) * CHUNK], SKEW - q * CHUNK, 1)
        row = lax.broadcasted_iota(jnp.int32, (CHUNK, SKEW), 0)
        for bit in range(CHUNK.bit_length() - 1):
            xv = jnp.where((row >> bit) & 1 == 1, pltpu.roll(xv, SKEW - (1 << bit), 1), xv)
        o_ref[0] = jnp.sum(xv, axis=0, keepdims=True)

    return _call(body, name="diagonal_sums", grid=(H_B,),
                 in_specs=[pl.BlockSpec((1, QROWS, lk), lambda h: (h, 0, 0))],
                 out_specs=pl.BlockSpec((1, 1, SKEW), lambda h: (h, 0, 0)),
                 out_shape=jax.ShapeDtypeStruct((H_B, 1, SKEW), F32), args=[dbias], sem=("parallel",))


def _attn_common(s, n_back, gqa, q_col, k_col, v_col):
    lk = (QG + n_back) * CHUNK
    pad = n_back * CHUNK
    wide = TPS * LANES
    q_spec = pl.BlockSpec((QROWS, wide), lambda t, g: (g, q_col // TPS + t))
    if gqa:
        k_spec = pl.BlockSpec((s, LANES), lambda t, g: (0, k_col))
        v_spec = pl.BlockSpec((s, LANES), lambda t, g: (0, v_col))
    else:
        k_spec = pl.BlockSpec((s, wide), lambda t, g: (0, k_col // TPS + t))
        v_spec = pl.BlockSpec((s, wide), lambda t, g: (0, v_col // TPS + t))
    last_variant = _n_variants(n_back) - 1
    bias_spec = pl.BlockSpec((None, 2 * TPS, QROWS, lk), lambda t, g: (jnp.minimum(g, last_variant), t, 0, 0))
    tile_spec = pl.BlockSpec((QROWS, wide), lambda t, g: (g, t))
    return lk, pad, q_spec, k_spec, v_spec, bias_spec, tile_spec


def _attention_fwd(proj, bias, sinks, *, n_back, gqa, q_col, k_col, v_col, name, carry=None):
    s = proj.shape[0]
    lk, pad, q_spec, k_spec, v_spec, bias_spec, tile_spec = _attn_common(s, n_back, gqa, q_col, k_col, v_col)
    n_t, n_g = 512 // (TPS * LANES), s // QROWS
    kv_wide = LANES if gqa else TPS * LANES

    def body(*refs):
        if gqa:
            q_ref, k_ref, v_ref, bias_ref, sink_ref, o_ref, l_ref, kpad, vpad = refs
        else:
            q_ref, k_ref, v_ref, bias_ref, o_ref, l_ref, kpad, vpad = refs
        t, g = pl.program_id(0), pl.program_id(1)

        @pl.when(g == 0)
        def _():
            kpad[0:pad, :] = jnp.zeros((pad, kv_wide), BF16)
            vpad[0:pad, :] = jnp.zeros((pad, kv_wide), BF16)
            kpad[pad:, :] = k_ref[...]
            vpad[pad:, :] = v_ref[...]

        start = pl.multiple_of(g * QROWS, QROWS)
        half = lax.broadcasted_iota(jnp.int32, (QROWS, LANES), 1) // HEAD_DIM
        for tt in range(TPS):
            lanes = slice(tt * LANES, (tt + 1) * LANES)
            kv_lanes = slice(0, LANES) if gqa else lanes
            kb = kpad[pl.ds(start, lk), kv_lanes]
            vb = vpad[pl.ds(start, lk), kv_lanes]
            q = q_ref[:, lanes] * (HEAD_DIM ** -0.5)
            if gqa:
                hk = (TPS * t + tt) // 2
                q_rolled = pltpu.roll(q.astype(F32), HEAD_DIM, 1).astype(BF16)
            outs, lses = [], []
            for e in range(2):
                if gqa:
                    kv_half = hk
                    src = jnp.where(hk == e, q, q_rolled)
                else:
                    kv_half = e
                    src = q
                qm = jnp.where(half == kv_half, src, jnp.zeros_like(src))
                sc = _dot_nt(qm, kb) + bias_ref[2 * tt + e]
                m = jnp.max(sc, axis=-1, keepdims=True)
                if gqa:
                    sk = sink_ref[2 * (TPS * t + tt) + e]
                    m = jnp.maximum(m, sk)
                p = jnp.exp(sc - m)
                l = jnp.sum(p, axis=-1, keepdims=True)
                if gqa:
                    l = l + jnp.exp(sk - m)
                pn = p / l
                outs.append(_dot(pn.astype(BF16), vb))
                lses.append(m + jnp.log(l))
            if gqa:
                same = jnp.where(hk == 0, outs[0], outs[1])
                other = jnp.where(hk == 0, outs[1], outs[0])
                o_ref[:, lanes] = jnp.where(half == hk, same, pltpu.roll(other, HEAD_DIM, 1))
            else:
                o_ref[:, lanes] = jnp.where(half == 0, outs[0], outs[1])
            l_ref[:, lanes] = jnp.where(half == 0, lses[0], lses[1])

    in_specs = [q_spec, k_spec, v_spec, bias_spec] + ([SMEM_SPEC] if gqa else [])
    args = [proj, proj, proj, bias] + ([sinks] if gqa else [])
    o_shape = jax.ShapeDtypeStruct((s, 512), F32)
    return _call(body, name=name, grid=(n_t, n_g), in_specs=in_specs, out_specs=(tile_spec, tile_spec),
                 out_shape=(o_shape, o_shape), args=args,
                 scratch=[pltpu.VMEM((s + pad, kv_wide), BF16), pltpu.VMEM((s + pad, kv_wide), BF16)],
                 sem=("arbitrary", "arbitrary"), carry=carry)


def _attention_bwd(proj, bias, sinks, do, lse, *, n_back, gqa, q_col, k_col, v_col, name, carry=None):
    s = proj.shape[0]
    lk, pad, q_spec, k_spec, v_spec, bias_spec, tile_spec = _attn_common(s, n_back, gqa, q_col, k_col, v_col)
    n_t, n_g = 512 // (TPS * LANES), s // QROWS
    kv_wide = LANES if gqa else TPS * LANES

    def body(*refs):
        if gqa:
            (q_ref, k_ref, v_ref, bias_ref, sink_ref, do_ref, l_ref,
             dq_ref, dk_ref, dv_ref, dsink_ref, kpad, vpad, dkpad, dvpad) = refs
        else:
            (q_ref, k_ref, v_ref, bias_ref, do_ref, l_ref,
             dq_ref, dk_ref, dv_ref, dbias_ref, kpad, vpad, dkpad, dvpad) = refs
        t, g = pl.program_id(0), pl.program_id(1)

        @pl.when(g == 0)
        def _():
            kpad[0:pad, :] = jnp.zeros((pad, kv_wide), BF16)
            vpad[0:pad, :] = jnp.zeros((pad, kv_wide), BF16)
            kpad[pad:, :] = k_ref[...]
            vpad[pad:, :] = v_ref[...]
            if gqa:
                dsink_ref[...] = jnp.zeros_like(dsink_ref)
            else:
                dbias_ref[...] = jnp.zeros_like(dbias_ref)

        @pl.when((g == 0) & (t == 0) if gqa else g == 0)
        def _():
            dkpad[...] = jnp.zeros_like(dkpad)
            dvpad[...] = jnp.zeros_like(dvpad)

        start = pl.multiple_of(g * QROWS, QROWS)
        half = lax.broadcasted_iota(jnp.int32, (QROWS, LANES), 1) // HEAD_DIM
        for tt in range(TPS):
            lanes = slice(tt * LANES, (tt + 1) * LANES)
            kv_lanes = slice(0, LANES) if gqa else lanes
            kb = kpad[pl.ds(start, lk), kv_lanes]
            vb = vpad[pl.ds(start, lk), kv_lanes]
            q = q_ref[:, lanes]
            dov = do_ref[:, lanes]
            lv = l_ref[:, lanes]
            if gqa:
                hk = (TPS * t + tt) // 2
                q_rolled = pltpu.roll(q.astype(F32), HEAD_DIM, 1).astype(BF16)
                do_rolled = pltpu.roll(dov, HEAD_DIM, 1)
            dqs = []
            dk_acc = jnp.zeros((lk, LANES), F32)
            dv_acc = jnp.zeros((lk, LANES), F32)
            for e in range(2):
                if gqa:
                    kv_half = hk
                    src = jnp.where(hk == e, q, q_rolled)
                    do_src = jnp.where(hk == e, dov, do_rolled)
                else:
                    kv_half = e
                    src = q
                    do_src = dov
                qm = jnp.where(half == kv_half, src, jnp.zeros_like(src))
                dom = jnp.where(half == kv_half, do_src, 0.0).astype(BF16)
                lcol = jnp.max(jnp.where(half == e, lv, -jnp.inf), axis=-1, keepdims=True)
                sc = _dot_nt(qm * (HEAD_DIM ** -0.5), kb) + bias_ref[2 * tt + e]
                pn = jnp.exp(sc - lcol)
                dp = _dot_nt(dom, vb)
                delta = jnp.sum(pn * dp, axis=-1, keepdims=True)
                ds = pn * (dp - delta)
                if gqa:
                    p_sink = jnp.exp(sink_ref[2 * (TPS * t + tt) + e] - lcol)
                    dsk = -jnp.sum(p_sink * delta, axis=0, keepdims=True)
                    row = 2 * tt + e
                    dsink_ref[0, row:row + 1, :] += jnp.broadcast_to(dsk, (1, LANES))
                else:
                    dbias_ref[2 * tt + e] += ds
                dsb = (ds * (HEAD_DIM ** -0.5)).astype(BF16)
                dqs.append(_dot(dsb, kb))
                dk_acc = dk_acc + _dot_tn(dsb, qm)
                dv_acc = dv_acc + _dot_tn(pn.astype(BF16), dom)
            dkpad[pl.ds(start, lk), kv_lanes] += dk_acc
            dvpad[pl.ds(start, lk), kv_lanes] += dv_acc
            if gqa:
                same = jnp.where(hk == 0, dqs[0], dqs[1])
                other = jnp.where(hk == 0, dqs[1], dqs[0])
                dq_ref[:, lanes] = jnp.where(half == hk, same, pltpu.roll(other, HEAD_DIM, 1)).astype(BF16)
            else:
                dq_ref[:, lanes] = jnp.where(half == 0, dqs[0], dqs[1]).astype(BF16)

        @pl.when((g == n_g - 1) & (t == n_t - 1) if gqa else g == n_g - 1)
        def _():
            dk_ref[...] = dkpad[pad:, :].astype(BF16)
            dv_ref[...] = dvpad[pad:, :].astype(BF16)

    in_specs = [q_spec, k_spec, v_spec, bias_spec] + ([SMEM_SPEC] if gqa else []) + [tile_spec, tile_spec]
    args = [proj, proj, proj, bias] + ([sinks] if gqa else []) + [do, lse]
    if gqa:
        kv_out = pl.BlockSpec((s, LANES), lambda t, g: (0, 0))
        kv_shape = jax.ShapeDtypeStruct((s, LANES), BF16)
        extra_spec = pl.BlockSpec((1, 8, LANES), lambda t, g: (t, 0, 0))
        extra_shape = jax.ShapeDtypeStruct((n_t, 8, LANES), F32)
    else:
        kv_out = pl.BlockSpec((s, kv_wide), lambda t, g: (0, t))
        kv_shape = jax.ShapeDtypeStruct((s, 512), BF16)
        extra_spec = pl.BlockSpec((2 * TPS, QROWS, lk), lambda t, g: (t, 0, 0))
        extra_shape = jax.ShapeDtypeStruct(bias.shape[1:], F32)
    return _call(body, name=name, grid=(n_t, n_g), in_specs=in_specs,
                 out_specs=(tile_spec, kv_out, kv_out, extra_spec),
                 out_shape=(jax.ShapeDtypeStruct((s, 512), BF16), kv_shape, kv_shape, extra_shape), args=args,
                 scratch=[pltpu.VMEM((s + pad, kv_wide), BF16), pltpu.VMEM((s + pad, kv_wide), BF16),
                          pltpu.VMEM((s + pad, kv_wide), F32), pltpu.VMEM((s + pad, kv_wide), F32)],
                 sem=("arbitrary", "arbitrary"), carry=carry)


def _sum_slots(r, name):
    n_slots, rows, k = r.shape

    def body(r_ref, o_ref):
        acc = r_ref[0].astype(F32)
        for j in range(1, n_slots):
            acc = acc + r_ref[j].astype(F32)
        o_ref[...] = acc

    return _call(body, name=name, grid=(k // LANES,),
                 in_specs=[pl.BlockSpec((n_slots, rows, LANES), lambda i: (0, 0, i))],
                 out_specs=pl.BlockSpec((rows, LANES), lambda i: (0, i)),
                 out_shape=jax.ShapeDtypeStruct((rows, k), F32), args=[r], sem=("parallel",))


def _sum_rows8(g):
    n = g.shape[2]

    def body(g_ref, o_ref):
        acc = g_ref[0]
        for j in range(1, N_DEV):
            acc = acc + g_ref[j]
        o_ref[...] = acc

    return pl.pallas_call(
        body, name="sum_small_grads", in_specs=[VMEM_SPEC], out_specs=VMEM_SPEC,
        out_shape=jax.ShapeDtypeStruct((1, n), F32), compiler_params=_params(),
    )(g)


def _ada_weight_grad(sc_t, dmod_cols):
    d = sc_t.shape[0]
    w = dmod_cols.shape[1]
    td = _pick(d, (256, 128))

    def body(sc_ref, dm_ref, o_ref):
        scv = sc_ref[...]
        dmv = dm_ref[...]
        acc = scv[:, 0:1] * dmv[0:1, :]
        for b in range(1, N_DEV):
            acc = acc + scv[:, b:b + 1] * dmv[b:b + 1, :]
        o_ref[...] = acc

    return _call(body, name="ada_weight_grad", grid=(d // td,),
                 in_specs=[pl.BlockSpec((td, N_DEV), lambda i: (i, 0)), pl.BlockSpec((N_DEV, w), lambda i: (0, 0))],
                 out_specs=pl.BlockSpec((td, w), lambda i: (i, 0)), out_shape=jax.ShapeDtypeStruct((d, w), F32),
                 args=[sc_t, dmod_cols], sem=("parallel",))


def _adamw_update(w, gv, m, v):
    nm = ADAM_B1 * m + (1.0 - ADAM_B1) * gv
    nv = ADAM_B2 * v + (1.0 - ADAM_B2) * (gv * gv)
    m_hat = nm / (1.0 - ADAM_B1 ** ADAM_STEP)
    v_hat = nv / (1.0 - ADAM_B2 ** ADAM_STEP)
    return -ADAM_LR * (m_hat / (jnp.sqrt(v_hat) + ADAM_EPS) + ADAM_WD * w), nm, nv


def _adamw(w, g, m, v, name):
    rows, cols = w.shape
    tr = _pick(rows, (256, 176, 128, 88, 64)) if rows > 256 else rows

    def body(w_ref, g_ref, m_ref, v_ref, d_ref, nm_ref, nv_ref):
        d_ref[...], nm_ref[...], nv_ref[...] = _adamw_update(w_ref[...], g_ref[...], m_ref[...], v_ref[...])

    spec = pl.BlockSpec((tr, cols), lambda i: (i, 0))
    shape = jax.ShapeDtypeStruct((rows, cols), F32)
    return _call(body, name=name, grid=(rows // tr,), in_specs=[spec] * 4, out_specs=(spec, spec, spec),
                 out_shape=(shape, shape, shape), args=[w, g, m, v], sem=("parallel",))


def _adamw_from_slots(w, slots, m, v, name):
    n_slots, rows, k = slots.shape

    def body(s_ref, w_ref, m_ref, v_ref, g_ref, d_ref, nm_ref, nv_ref):
        gv = s_ref[0].astype(F32)
        for j in range(1, n_slots):
            gv = gv + s_ref[j].astype(F32)
        g_ref[...] = gv
        d_ref[...], nm_ref[...], nv_ref[...] = _adamw_update(w_ref[...], gv, m_ref[...], v_ref[...])

    tr = rows // 2 if rows % 32 == 0 else rows
    spec = pl.BlockSpec((tr, k), lambda i: (i, 0))
    shape = jax.ShapeDtypeStruct((rows, k), F32)
    return _call(body, name=name, grid=(rows // tr,),
                 in_specs=[pl.BlockSpec((n_slots, tr, k), lambda i: (0, i, 0)), spec, spec, spec],
                 out_specs=(spec, spec, spec, spec), out_shape=(shape, shape, shape, shape),
                 args=[slots, w, m, v], sem=("parallel",))


SMALL = ("b_ada", "g_pre_ffn1", "g_post_ffn1", "g_pre_mix", "b_in", "sinks_a", "rel_bias_b", "g_grp_a",
         "g_grp_b", "b_out", "g_post_mix", "g_pre_ffn2", "g_post_ffn2")
WEIGHTS = ("w_ada", "b_ada", "g_pre_ffn1", "w_gate1", "w_up1", "w_down1", "g_post_ffn1", "g_pre_mix", "w_in",
           "b_in", "sinks_a", "rel_bias_b", "g_grp_a", "g_grp_b", "w_out", "b_out", "g_post_mix", "g_pre_ffn2",
           "w_gate2", "w_up2", "w_down2", "g_post_ffn2")


def kernel(x, c, w_ada, b_ada, g_pre_ffn1, w_gate1, w_up1, w_down1, g_post_ffn1, g_pre_mix, w_in, b_in, sinks_a, rel_bias_b, g_grp_a, g_grp_b, w_out, b_out, g_post_mix, g_pre_ffn2, w_gate2, w_up2, w_down2, g_post_ffn2, loss_target, m_w_ada, m_b_ada, m_g_pre_ffn1, m_w_gate1, m_w_up1, m_w_down1, m_g_post_ffn1, m_g_pre_mix, m_w_in, m_b_in, m_sinks_a, m_rel_bias_b, m_g_grp_a, m_g_grp_b, m_w_out, m_b_out, m_g_post_mix, m_g_pre_ffn2, m_w_gate2, m_w_up2, m_w_down2, m_g_post_ffn2, v_w_ada, v_b_ada, v_g_pre_ffn1, v_w_gate1, v_w_up1, v_w_down1, v_g_post_ffn1, v_g_pre_mix, v_w_in, v_b_in, v_sinks_a, v_rel_bias_b, v_g_grp_a, v_g_grp_b, v_w_out, v_b_out, v_g_post_mix, v_g_pre_ffn2, v_w_gate2, v_w_up2, v_w_down2, v_g_post_ffn2):
    given = dict(locals())
    weights = {n: given[n] for n in WEIGHTS}
    mom_m = {n: given["m_" + n] for n in WEIGHTS}
    mom_v = {n: given["v_" + n] for n in WEIGHTS}

    me = 4 * lax.axis_index("x") + 2 * lax.axis_index("y") + lax.axis_index("c")
    xs = x[0]
    tgt = loss_target[0]
    d_model = xs.shape[1]
    ada_cols = w_ada.shape[2]

    sh = {"wg1": w_gate1[0].T, "wu1": w_up1[0].T, "wd1": w_down1[0], "win": w_in[0].T, "wo": w_out[0],
          "wg2": w_gate2[0].T, "wu2": w_up2[0].T, "wd2": w_down2[0]}
    sh = {k: v.astype(BF16) for k, v in sh.items()}

    def gather(*names):
        return _gather_carry([sh[n] for n in names])

    bias_a = _alibi_bias()
    rel_m = _rel_index_matrix()
    rel_vec = jnp.dot(rel_bias_b[0], rel_m.T, precision=lax.Precision.HIGHEST)
    bias_b, (wg1, wu1) = _toeplitz_bias(rel_vec.reshape(H_B, 1, SKEW), carry=gather("wg1", "wu1"))

    b_cols = lax.dynamic_slice(b_ada, (0, me * ada_cols), (1, ada_cols))
    (sc_all, mod_rows), _ = _ada_forward(c, w_ada[0], b_cols, _Carry([], [], [], lambda *a: None, lambda *a: None))
    mod = mod_rows.reshape(N_MOD, d_model)
    shift1, scale1, gate1, shift2, scale2, gate2, shift3, scale3, gate3 = (mod[i:i + 1] for i in range(N_MOD))

    h1 = _pre_norm(xs, g_pre_ffn1, scale1, shift1, "pre_norm_ffn1")
    (a1, b1, u1), (wd1,) = _ffn_up(h1, wg1, wu1, "ffn_up_ffn1", carry=gather("wd1"))
    (y1, x1, h2), (win,) = _mm_nn(
        [(u1, wd1)], "ffn_down_ffn1", F32, carry=gather("win"),
        tail=_tail_post_pre(xs, g_post_ffn1, gate1, 0.5, g_pre_mix, scale2, shift2))

    proj, (wo,) = _mm_nt(h2, win, "in_proj", BF16, bias=b_in, carry=gather("wo"))
    sinks = sinks_a[0]
    cfg_a = dict(n_back=BACK_A, gqa=True, q_col=0, k_col=QA // LANES, v_col=(QA + KVA) // LANES)
    cfg_b = dict(n_back=BACK_B, gqa=False, q_col=(QA + 2 * KVA) // LANES, k_col=(QA + 2 * KVA + QB) // LANES,
                 v_col=(QA + 2 * KVA + 2 * QB) // LANES)
    (oa, lse_a), (wg2,) = _attention_fwd(proj, bias_a, sinks, name="attn_a", carry=gather("wg2"), **cfg_a)
    (ob, lse_b), (wu2,) = _attention_fwd(proj, bias_b, None, name="attn_b", carry=gather("wu2"), **cfg_b)
    ycat = _group_norm_cat(oa, ob, g_grp_a, g_grp_b)
    ymix, x2, h3 = _mm_nn([(ycat, wo)], "out_proj", F32, bias=b_out,
                          tail=_tail_post_pre(x1, g_post_mix, gate2, 1.0, g_pre_ffn2, scale3, shift3))

    (a3, b3, u3), (wd2,) = _ffn_up(h3, wg2, wu2, "ffn_up_ffn2", carry=gather("wd2"))

    def scatter(*grads):
        return _scatter_carry(list(grads))

    slots = {}

    dx3, dy, loss_part, s1 = _mm_nn([(u3, wd2)], "ffn_down_ffn2", None,
                                    tail=_tail_post_loss(x2, tgt, g_post_ffn2, gate3, 0.5))
    da, db = _ffn_down_bwd(dy, wd2, a3, b3, "ffn_down_bwd_ffn2")
    dwd2 = _mm_tn_pair(u3, dy, "grad_wd_ffn2")
    dwg2 = _mm_tn_pair(da, h3, "grad_wg_ffn2")
    dwu2 = _mm_tn_pair(db, h3, "grad_wu_ffn2")
    (dx2, dymix, s2, s3, s1m, db_out), (slots["wd2"],) = _mm_nn(
        [(da, wg2), (db, wu2)], "ffn_up_bwd_ffn2", None, carry=scatter(dwd2),
        tail=_tail_pre_post_bwd(x2, dx3, ymix, g_pre_ffn2, scale3, g_post_mix, gate2, 1.0))
    sm3 = dict(shift=s3, scale=s2 * g_pre_ffn2, gate=0.5 * g_post_ffn2 * s1,
               g_pre=(1.0 + scale3) * s2, g_post=(0.5 * gate3) * s1)

    dycat = _mm_nt(dymix, wo, "out_proj_bwd", F32)
    dwo = _mm_tn_pair(ycat, dymix, "grad_wo")
    doa, dob, dg_a, dg_b = _group_norm_bwd(dycat, oa, ob, g_grp_a, g_grp_b)
    (dqa, dka, dva, dsink), (slots["wg2"],) = _attention_bwd(
        proj, bias_a, sinks, doa, lse_a, name="attn_a_bwd", carry=scatter(dwg2), **cfg_a)
    (dqb, dkb, dvb, dbias), (slots["wu2"], slots["wo"]) = _attention_bwd(
        proj, bias_b, None, dob, lse_b, name="attn_b_bwd", carry=scatter(dwu2, dwo), **cfg_b)
    dproj = jnp.concatenate([dqa, dka, dva, dqb, dkb, dvb], axis=1)
    db_in = _col_sum(dproj, "grad_b_in")
    dwin = _mm_tn_pair(dproj, h2, "grad_win")
    dx1, dy, s2m, s3m, s1, _ = _mm_nn(
        [(dproj, win)], "in_proj_bwd", None,
        tail=_tail_pre_post_bwd(x1, dx2, y1, g_pre_mix, scale2, g_post_ffn1, gate1, 0.5))
    d_rel = jnp.dot(_diagonal_sums(dbias).reshape(H_B, SKEW), rel_m, precision=lax.Precision.HIGHEST)
    d_sinks = dsink[:, :2 * TPS, 0].reshape(1, H_A)

    (da, db), (slots["win"],) = _ffn_down_bwd(dy, wd1, a1, b1, "ffn_down_bwd_ffn1", carry=scatter(dwin))
    dwd1 = _mm_tn_pair(u1, dy, "grad_wd_ffn1")
    dwg1, (slots["wd1"],) = _mm_tn_pair(da, h1, "grad_wg_ffn1", carry=scatter(dwd1))
    dwu1, (slots["wg1"],) = _mm_tn_pair(db, h1, "grad_wu_ffn1", carry=scatter(dwg1))
    (dx0, s2, s3), (slots["wu1"],) = _mm_nn(
        [(da, wg1), (db, wu1)], "ffn_up_bwd_ffn1", None, carry=scatter(dwu1),
        tail=_tail_pre_bwd(xs, dx1, g_pre_ffn1, scale1))
    sm1 = dict(shift=s3, scale=s2 * g_pre_ffn1, gate=0.5 * g_post_ffn1 * s1,
               g_pre=(1.0 + scale1) * s2, g_post=(0.5 * gate1) * s1)

    dmod = jnp.concatenate([sm1["shift"], sm1["scale"], sm1["gate"],
                            s3m, s2m * g_pre_mix, g_post_mix * s1m,
                            sm3["shift"], sm3["scale"], sm3["gate"]], axis=1)
    small_parts = {
        "b_ada": dmod, "g_pre_ffn1": sm1["g_pre"], "g_post_ffn1": sm1["g_post"],
        "g_pre_mix": (1.0 + scale2) * s2m, "b_in": db_in, "sinks_a": d_sinks,
        "rel_bias_b": d_rel.reshape(1, H_B * N_REL), "g_grp_a": dg_a, "g_grp_b": dg_b, "b_out": db_out,
        "g_post_mix": gate2 * s1m, "g_pre_ffn2": sm3["g_pre"], "g_post_ffn2": sm3["g_post"]}
    sizes = [small_parts[n].shape[1] for n in SMALL]
    n_small = sum(sizes)
    n_pad = -(n_small + 1) % LANES
    packed = jnp.concatenate([small_parts[n] for n in SMALL] + [loss_part, jnp.zeros((1, n_pad), F32)], axis=1)
    gathered = _all_gather_small(packed)
    small_sum = _sum_rows8(gathered)
    loss = small_sum[0, n_small]
    dmod_cols = lax.dynamic_slice(gathered.reshape(N_DEV, n_small + 1 + n_pad), (0, me * ada_cols),
                                  (N_DEV, ada_cols))
    g_ada = _ada_weight_grad(sc_all.reshape(N_DEV, d_model).T, dmod_cols)

    out_g, out_d, out_m, out_v = {}, {}, {}, {}
    d_, m_, v_ = _adamw(w_ada[0], g_ada, m_w_ada[0], v_w_ada[0], "adamw_w_ada")
    out_g["w_ada"], out_d["w_ada"], out_m["w_ada"], out_v["w_ada"] = g_ada[None], d_[None], m_[None], v_[None]
    for n, key, transposed in (("w_gate1", "wg1", True), ("w_up1", "wu1", True), ("w_down1", "wd1", False),
                               ("w_in", "win", True), ("w_out", "wo", False), ("w_gate2", "wg2", True),
                               ("w_up2", "wu2", True), ("w_down2", "wd2", False)):
        view = (lambda t: t.T) if transposed else (lambda t: t)
        res = _adamw_from_slots(view(weights[n][0]), slots[key], view(mom_m[n][0]), view(mom_v[n][0]),
                                "adamw_" + n)
        out_g[n], out_d[n], out_m[n], out_v[n] = (view(t)[None] for t in res)

    def pack(tree):
        return jnp.concatenate([tree[n].reshape(1, -1) for n in SMALL], axis=1)

    g_small = small_sum[:, :n_small]
    d_s, m_s, v_s = _adamw(pack(weights), g_small, pack(mom_m), pack(mom_v), "adamw_small")
    off = 0
    for n, size in zip(SMALL, sizes):
        shape = weights[n].shape
        out_g[n] = g_small[:, off:off + size].reshape(shape)
        out_d[n] = d_s[:, off:off + size].reshape(shape)
        out_m[n] = m_s[:, off:off + size].reshape(shape)
        out_v[n] = v_s[:, off:off + size].reshape(shape)
        off += size

    return (loss, dx0[None], *[out_g[n] for n in WEIGHTS], *[out_d[n] for n in WEIGHTS],
            *[out_m[n] for n in WEIGHTS], *[out_v[n] for n in WEIGHTS])
```

```python
import numpy as np
import jax
import jax.numpy as jnp
from jax import lax
from jax.experimental import pallas as pl
from jax.experimental.pallas import tpu as pltpu

F32 = jnp.float32
BF16 = jnp.bfloat16
MESH = pl.DeviceIdType.MESH
ANY = pl.BlockSpec(memory_space=pl.ANY)
VMEM_SPEC = pl.BlockSpec(memory_space=pltpu.VMEM)
SMEM_SPEC = pl.BlockSpec(memory_space=pltpu.SMEM)

N_DEV = 8
CHUNK = 64
HEAD_DIM = 64
LANES = 128
H_A, KV_A, H_B = 8, 2, 8
BACK_A, BACK_B = 2, 8
REL_CLIP = 128
N_REL = 2 * REL_CLIP + 1
QA, KVA, QB = H_A * HEAD_DIM, KV_A * HEAD_DIM, H_B * HEAD_DIM
D_IN = QA + 2 * KVA + 3 * QB
N_MOD = 9
EPS = 1e-6
NEG_INF = -1e30
QG = 4
QROWS = QG * CHUNK
TPS = 2
SKEW = 1024
ADAM_LR, ADAM_B1, ADAM_B2, ADAM_EPS, ADAM_WD, ADAM_STEP = 0.001, 0.9, 0.999, 1e-08, 0.01, 10
VMEM_LIMIT = 56 * 2 ** 20


def _pick(n, cands):
    for c in cands:
        if n % c == 0:
            return c
    return n


def _pieces(n, width=2 * LANES):
    return [(lo, min(lo + width, n)) for lo in range(0, n, width)]


def _params(sem=None):
    return pltpu.CompilerParams(dimension_semantics=sem, vmem_limit_bytes=VMEM_LIMIT)


def _dot_nt(a, b):
    return lax.dot_general(a, b, (((1,), (1,)), ((), ())), preferred_element_type=F32)


def _dot_tn(a, b):
    return lax.dot_general(a, b, (((0,), (0,)), ((), ())), preferred_element_type=F32)


def _dot(a, b):
    return jnp.dot(a, b, preferred_element_type=F32)


def _sigmoid(a):
    return 0.5 * (jnp.tanh(0.5 * a) + 1.0)


def _mesh_pos():
    return lax.axis_index("x"), lax.axis_index("y"), lax.axis_index("c")


def _peer(x, y, c, r):
    px = 1 - x if r & 4 else x
    py = 1 - y if r & 2 else y
    pc = 1 - c if r & 1 else c
    return px, py, pc


class _Carry:
    def __init__(self, ins, out_shapes, scratch, start, finish):
        self.ins, self.out_shapes, self.scratch = list(ins), list(out_shapes), list(scratch)
        self.start, self.finish = start, finish


def _call(body, *, name, grid, in_specs, out_specs, out_shape, args, scratch=(), sem=None, carry=None):
    single = not isinstance(out_shape, (tuple, list))
    out_specs = (out_specs,) if single else tuple(out_specs)
    out_shape = (out_shape,) if single else tuple(out_shape)
    if carry is None:
        res = pl.pallas_call(body, name=name, grid=grid, in_specs=list(in_specs), out_specs=out_specs,
                             out_shape=out_shape, scratch_shapes=list(scratch), compiler_params=_params(sem))(*args)
        return res[0] if single else res
    n_in, n_out, n_s = len(in_specs), len(out_shape), len(scratch)
    ci, co = len(carry.ins), len(carry.out_shapes)

    def wrapped(*refs):
        ins, cins = refs[:n_in], refs[n_in:n_in + ci]
        outs = refs[n_in + ci:n_in + ci + n_out]
        couts = refs[n_in + ci + n_out:n_in + ci + n_out + co]
        scr = refs[n_in + ci + n_out + co:n_in + ci + n_out + co + n_s]
        cscr = refs[n_in + ci + n_out + co + n_s:]
        first, last = None, None
        for ax, n in enumerate(grid):
            f, l = pl.program_id(ax) == 0, pl.program_id(ax) == n - 1
            first = f if first is None else first & f
            last = l if last is None else last & l
        pl.when(first)(lambda: carry.start(cins, couts, cscr))
        body(*ins, *outs, *scr)
        pl.when(last)(lambda: carry.finish(cins, couts, cscr))

    res = pl.pallas_call(
        wrapped, name=name, grid=grid, in_specs=list(in_specs) + [ANY] * ci, out_specs=out_specs + (ANY,) * co,
        out_shape=out_shape + tuple(carry.out_shapes), scratch_shapes=list(scratch) + carry.scratch,
        compiler_params=_params(("arbitrary",) * len(grid)))(*args, *carry.ins)
    main = res[:n_out]
    return (main[0] if single else main), res[n_out:]


def _gather_carry(shards):
    n_w = len(shards)
    rows = [s.shape[0] for s in shards]

    def plan(ins, outs, scr):
        send_sems, recv_sems, local_sems = scr
        x, y, c = _mesh_pos()
        me, sibling = (x, y, c), (x, y, 1 - c)
        chips = [(1 - x, y), (x, 1 - y), (1 - x, 1 - y)]

        def block(w, dev):
            start = pl.multiple_of((4 * dev[0] + 2 * dev[1] + dev[2]) * rows[w], 16)
            return outs[w].at[pl.ds(start, rows[w]), :]

        def copy(w, k, dev, to, src=None):
            return pltpu.make_async_remote_copy(
                src_ref=block(w, dev) if src is None else src, dst_ref=block(w, dev),
                send_sem=send_sems.at[w, k], recv_sem=recv_sems.at[w, k], device_id=to, device_id_type=MESH)

        mine = [pltpu.make_async_copy(ins[w], block(w, me), local_sems.at[w]) for w in range(n_w)]
        first = []
        for j, chip in enumerate(chips):
            first += [copy(w, 1 + j, me, (*chip, c), src=ins[w]) for w in range(n_w)]
        first += [copy(w, 0, me, sibling, src=ins[w]) for w in range(n_w)]
        return c, me, sibling, chips, copy, mine, first

    def start(ins, outs, scr):
        _, _, _, _, _, mine, first = plan(ins, outs, scr)
        for cp in mine + first:
            cp.start()

    def finish(ins, outs, scr):
        c, me, sibling, chips, copy, mine, first = plan(ins, outs, scr)
        passed = []
        for j, chip in enumerate(chips):
            for w in range(n_w):
                copy(w, 1 + j, (*chip, c), me).wait_recv()
                cp = copy(w, 4 + j, (*chip, c), sibling)
                cp.start()
                passed.append(cp)
        for w in range(n_w):
            copy(w, 0, sibling, me).wait_recv()
        for j, chip in enumerate(chips):
            for w in range(n_w):
                copy(w, 4 + j, (*chip, 1 - c), me).wait_recv()
        for cp in first + passed:
            cp.wait_send()
        for cp in mine:
            cp.wait()

    return _Carry(
        shards, [jax.ShapeDtypeStruct((N_DEV * s.shape[0], s.shape[1]), s.dtype) for s in shards],
        [pltpu.SemaphoreType.DMA((n_w, N_DEV - 1)), pltpu.SemaphoreType.DMA((n_w, N_DEV - 1)),
         pltpu.SemaphoreType.DMA((n_w,))], start, finish)


def _scatter_carry(parts):
    n_w = len(parts)
    n_chip = N_DEV // 2
    rows = [g.shape[0] // n_chip for g in parts]

    def plan(ins, outs, scr):
        send_sems, recv_sems, local_sems = scr
        x, y, c = _mesh_pos()

        def src(w, chip_index):
            return ins[w].at[pl.ds(pl.multiple_of(chip_index * rows[w], 16), rows[w]), :]

        mine = [pltpu.make_async_copy(src(w, 2 * x + y), outs[w].at[0], local_sems.at[w]) for w in range(n_w)]
        copies = []
        for r in (3, 2, 1):
            px, py, _ = _peer(x, y, c, 2 * r)
            for w in range(n_w):
                copies.append(pltpu.make_async_remote_copy(
                    src_ref=src(w, 2 * px + py), dst_ref=outs[w].at[r], send_sem=send_sems.at[w, r - 1],
                    recv_sem=recv_sems.at[w, r - 1], device_id=(px, py, c), device_id_type=MESH))
        return mine, copies

    def start(ins, outs, scr):
        mine, copies = plan(ins, outs, scr)
        for cp in mine + copies:
            cp.start()

    def finish(ins, outs, scr):
        mine, copies = plan(ins, outs, scr)
        for cp in copies:
            cp.wait_recv()
        for cp in copies:
            cp.wait_send()
        for cp in mine:
            cp.wait()

    return _Carry(
        parts, [jax.ShapeDtypeStruct((n_chip, r, g.shape[1]), g.dtype) for r, g in zip(rows, parts)],
        [pltpu.SemaphoreType.DMA((n_w, n_chip - 1)), pltpu.SemaphoreType.DMA((n_w, n_chip - 1)),
         pltpu.SemaphoreType.DMA((n_w,))], start, finish)


def _ada_forward(c_row, w_ada, b_cols, carry):
    d = c_row.shape[1]
    wcols = w_ada.shape[1]
    ci, co = len(carry.ins), len(carry.out_shapes)

    def body(*refs):
        c_ref, w_ref, b_ref = refs[:3]
        cins = refs[3:3 + ci]
        sc_ref, mod_ref = refs[3 + ci:5 + ci]
        couts = refs[5 + ci:5 + ci + co]
        rows_ref, send_sems, recv_sems = refs[5 + ci + co:8 + ci + co]
        cscr = refs[8 + ci + co:]
        carry.start(cins, couts, cscr)
        x, y, c = _mesh_pos()
        me = 4 * x + 2 * y + c
        cv = c_ref[...]
        sc_ref[me] = cv * _sigmoid(cv)

        sends = []
        for r in range(1, N_DEV):
            px, py, pc = _peer(x, y, c, r)
            cp = pltpu.make_async_remote_copy(
                src_ref=sc_ref.at[me], dst_ref=sc_ref.at[me], send_sem=send_sems.at[0, r - 1],
                recv_sem=recv_sems.at[0, r - 1], device_id=(px, py, pc), device_id_type=MESH)
            cp.start()
            sends.append(cp)
        for r in range(1, N_DEV):
            px, py, pc = _peer(x, y, c, r)
            pid = 4 * px + 2 * py + pc
            pltpu.make_async_remote_copy(
                src_ref=sc_ref.at[pid], dst_ref=sc_ref.at[pid], send_sem=send_sems.at[0, r - 1],
                recv_sem=recv_sems.at[0, r - 1], device_id=(px, py, pc), device_id_type=MESH).wait_recv()
        for cp in sends:
            cp.wait_send()

        sc_all = jnp.concatenate([sc_ref[j] for j in range(N_DEV)], axis=0)
        rows = _dot(sc_all.astype(BF16), w_ref[...].astype(BF16)) + b_ref[...]
        for j in range(N_DEV):
            rows_ref[j] = rows[j:j + 1, :]
        mod_ref[me] = rows_ref[me]

        sends = []
        for r in range(1, N_DEV):
            px, py, pc = _peer(x, y, c, r)
            pid = 4 * px + 2 * py + pc
            cp = pltpu.make_async_remote_copy(
                src_ref=rows_ref.at[pid], dst_ref=mod_ref.at[me], send_sem=send_sems.at[1, r - 1],
                recv_sem=recv_sems.at[1, r - 1], device_id=(px, py, pc), device_id_type=MESH)
            cp.start()
            sends.append(cp)
        for r in range(1, N_DEV):
            px, py, pc = _peer(x, y, c, r)
            pid = 4 * px + 2 * py + pc
            pltpu.make_async_remote_copy(
                src_ref=rows_ref.at[pid], dst_ref=mod_ref.at[pid], send_sem=send_sems.at[1, r - 1],
                recv_sem=recv_sems.at[1, r - 1], device_id=(px, py, pc), device_id_type=MESH).wait_recv()
        for cp in sends:
            cp.wait_send()
        carry.finish(cins, couts, cscr)

    res = pl.pallas_call(
        body, name="ada_forward",
        out_shape=(jax.ShapeDtypeStruct((N_DEV, 1, d), F32), jax.ShapeDtypeStruct((N_DEV, 1, wcols), F32),
                   *carry.out_shapes),
        in_specs=[VMEM_SPEC, VMEM_SPEC, VMEM_SPEC] + [ANY] * ci, out_specs=(VMEM_SPEC, VMEM_SPEC) + (ANY,) * co,
        scratch_shapes=[pltpu.VMEM((N_DEV, 1, wcols), F32), pltpu.SemaphoreType.DMA((2, N_DEV - 1)),
                        pltpu.SemaphoreType.DMA((2, N_DEV - 1))] + carry.scratch,
        compiler_params=_params(),
    )(c_row, w_ada, b_cols, *carry.ins)
    return res[:2], res[2:]


def _all_gather_small(v):
    n = v.shape[1]

    def body(v_ref, out_ref, send_sems, recv_sems):
        x, y, c = _mesh_pos()
        me = 4 * x + 2 * y + c
        out_ref[me] = v_ref[...]
        sends = []
        for r in range(1, N_DEV):
            px, py, pc = _peer(x, y, c, r)
            cp = pltpu.make_async_remote_copy(
                src_ref=v_ref, dst_ref=out_ref.at[me], send_sem=send_sems.at[r - 1],
                recv_sem=recv_sems.at[r - 1], device_id=(px, py, pc), device_id_type=MESH)
            cp.start()
            sends.append(cp)
        for r in range(1, N_DEV):
            px, py, pc = _peer(x, y, c, r)
            pid = 4 * px + 2 * py + pc
            pltpu.make_async_remote_copy(
                src_ref=v_ref, dst_ref=out_ref.at[pid], send_sem=send_sems.at[r - 1],
                recv_sem=recv_sems.at[r - 1], device_id=(px, py, pc), device_id_type=MESH).wait_recv()
        for cp in sends:
            cp.wait_send()

    return pl.pallas_call(
        body, name="all_gather_small",
        out_shape=jax.ShapeDtypeStruct((N_DEV, 1, n), F32),
        in_specs=[VMEM_SPEC], out_specs=VMEM_SPEC,
        scratch_shapes=[pltpu.SemaphoreType.DMA((N_DEV - 1,)), pltpu.SemaphoreType.DMA((N_DEV - 1,))],
        compiler_params=_params(),
    )(v)


def _mm_nt(a, b, name, out_dtype, bias=None, carry=None):
    m, k = a.shape
    n = b.shape[0]
    tm = _pick(m, (512, 256, 128))
    tn = _pick(n, (1408, 1152, 1024, 768, 512, 256, 128))

    def body(*refs):
        acc = _dot_nt(refs[0][...], refs[1][...])
        if bias is not None:
            acc = acc + refs[2][...]
        refs[-1][...] = acc.astype(out_dtype)

    in_specs = [pl.BlockSpec((tm, k), lambda i, j: (i, 0)), pl.BlockSpec((tn, k), lambda i, j: (j, 0))]
    args = [a, b]
    if bias is not None:
        in_specs.append(pl.BlockSpec((1, tn), lambda i, j: (0, j)))
        args.append(bias)
    return _call(body, name=name, grid=(m // tm, n // tn), in_specs=in_specs,
                 out_specs=pl.BlockSpec((tm, tn), lambda i, j: (i, j)),
                 out_shape=jax.ShapeDtypeStruct((m, n), out_dtype), args=args,
                 sem=("parallel", "parallel"), carry=carry)


class _Tail:
    def __init__(self, rows, vecs, outs, fn):
        self.rows, self.vecs, self.outs, self.fn = list(rows), list(vecs), list(outs), fn


def _mm_nn(pairs, name, out_dtype, bias=None, carry=None, tail=None):
    m, k = pairs[0][0].shape
    n = pairs[0][1].shape[1]
    n_p = len(pairs)
    tm = _pick(m, (512, 256, 128))
    tk = k if n_p == 1 else _pick(k, (1408, 1152, 1024, 768, 512, 256, 128))
    nk = k // tk
    n_b = 0 if bias is None else 1
    n_r, n_v = (len(tail.rows), len(tail.vecs)) if tail else (0, 0)
    n_in = 2 * n_p + n_b + n_r + n_v
    n_main = 0 if out_dtype is None else 1

    def finish(acc, refs, first_tile):
        if bias is not None:
            acc = acc + refs[2 * n_p][...]
        outs = refs[n_in:-1]
        if n_main:
            outs[0][...] = acc.astype(out_dtype)
        if tail is None:
            return
        rows = [r[...] for r in refs[2 * n_p + n_b:2 * n_p + n_b + n_r]]
        vecs = [v[...] for v in refs[2 * n_p + n_b + n_r:n_in]]
        vals = tail.fn(acc, rows, vecs)
        for ref, val, (dtype, kind) in zip(outs[n_main:], vals, tail.outs):
            if kind == "row":
                ref[...] = val.astype(dtype)
            else:
                @pl.when(first_tile)
                def _(ref=ref):
                    ref[...] = jnp.zeros_like(ref)

                ref[...] += val

    def body(*refs):
        acc_ref = refs[-1]
        kk = pl.program_id(1)
        first_tile = pl.program_id(0) == 0

        if nk == 1:
            acc = _dot(refs[0][...], refs[1][...])
            for p in range(1, n_p):
                acc = acc + _dot(refs[2 * p][...], refs[2 * p + 1][...])
            finish(acc, refs, first_tile)
            return

        @pl.when(kk == 0)
        def _():
            acc_ref[...] = jnp.zeros_like(acc_ref)

        for p in range(n_p):
            acc_ref[...] += _dot(refs[2 * p][...], refs[2 * p + 1][...])

        @pl.when(kk == nk - 1)
        def _():
            finish(acc_ref[...], refs, first_tile)

    row_spec = pl.BlockSpec((tm, n), lambda i, kk: (i, 0))
    in_specs, args = [], []
    for a, b in pairs:
        in_specs += [pl.BlockSpec((tm, tk), lambda i, kk: (i, kk)), pl.BlockSpec((tk, n), lambda i, kk: (kk, 0))]
        args += [a, b]
    if bias is not None:
        in_specs.append(pl.BlockSpec((1, n), lambda i, kk: (0, 0)))
        args.append(bias)
    out_specs = [row_spec] * n_main
    out_shape = [jax.ShapeDtypeStruct((m, n), out_dtype)] if n_main else []
    sums = False
    if tail:
        in_specs += [row_spec] * n_r + [pl.BlockSpec((1, n), lambda i, kk: (0, 0))] * n_v
        args += tail.rows + tail.vecs
        for dtype, kind in tail.outs:
            if kind == "row":
                out_specs.append(row_spec)
                out_shape.append(jax.ShapeDtypeStruct((m, n), dtype))
            else:
                sums = True
                width = n if kind == "sum" else 1
                out_specs.append(pl.BlockSpec((1, width), lambda i, kk: (0, 0)))
                out_shape.append(jax.ShapeDtypeStruct((1, width), dtype))
    if tail is None:
        out_specs, out_shape = out_specs[0], out_shape[0]
    return _call(body, name=name, grid=(m // tm, nk), in_specs=in_specs, out_specs=out_specs,
                 out_shape=out_shape, args=args,
                 scratch=[pltpu.VMEM((tm, n) if nk > 1 else (8, LANES), F32)],
                 sem=("arbitrary" if sums else "parallel", "arbitrary"), carry=carry)


def _rms(v):
    return lax.rsqrt(jnp.mean(v * v, axis=-1, keepdims=True) + EPS)


def _col(v):
    return jnp.sum(v, axis=0, keepdims=True)


def _tail_post_pre(x, g_post, gate, weight, g_pre, scale, shift):
    def fn(y, rows, vecs):
        (xv,), (gp, gt, g, sc, sh) = rows, vecs
        xo = xv + (weight * gt) * ((y * _rms(y)) * gp)
        return xo, ((xo * _rms(xo)) * g) * (1.0 + sc) + sh

    return _Tail([x], [g_post, gate, g_pre, scale, shift], [(F32, "row"), (BF16, "row")], fn)


def _tail_post_loss(x, target, g, gate, weight):
    def fn(y, rows, vecs):
        (xv, tv), (gv, gt) = rows, vecs
        r = _rms(y)
        yn = y * r
        err = (xv + (weight * gt) * (yn * gv)) - tv
        do = err * (1.0 / y.shape[1])
        dyn = do * ((weight * gt) * gv)
        dy = r * (dyn - yn * jnp.mean(dyn * yn, axis=-1, keepdims=True))
        return do, dy, 0.5 * _col(jnp.mean(err * err, axis=-1, keepdims=True)), _col(do * yn)

    return _Tail([x, target], [g, gate], [(F32, "row"), (BF16, "row"), (F32, "one"), (F32, "sum")], fn)


def _tail_pre_bwd(x, dres, g_pre, scale):
    def fn(dh, rows, vecs):
        (xv, dr), (g, sc) = rows, vecs
        r = _rms(xv)
        n = xv * r
        dn = dh * (g * (1.0 + sc))
        return dr + r * (dn - n * jnp.mean(dn * n, axis=-1, keepdims=True)), _col(dh * n), _col(dh)

    return _Tail([x, dres], [g_pre, scale], [(F32, "row"), (F32, "sum"), (F32, "sum")], fn)


def _tail_pre_post_bwd(x, dres, y, g_pre, scale, g_post, gate, weight):
    def fn(dh, rows, vecs):
        (xv, dr, yv), (g, sc, gp, gt) = rows, vecs
        r = _rms(xv)
        n = xv * r
        dn = dh * (g * (1.0 + sc))
        dx = dr + r * (dn - n * jnp.mean(dn * n, axis=-1, keepdims=True))
        ry = _rms(yv)
        yn = yv * ry
        dyn = dx * ((weight * gt) * gp)
        dy = ry * (dyn - yn * jnp.mean(dyn * yn, axis=-1, keepdims=True))
        return dx, dy, _col(dh * n), _col(dh), _col(dx * yn), _col(dy)

    return _Tail([x, dres, y], [g_pre, scale, g_post, gate],
                 [(F32, "row"), (BF16, "row")] + [(F32, "sum")] * 4, fn)


def _mm_tn(a, b, name, out_dtype=BF16, carry=None):
    k, m = a.shape
    n = b.shape[1]
    tm = _pick(m, (1408, 1152, 1024, 768, 512, 256, 128))
    tk = _pick(k, (512, 256, 128))
    nk = k // tk

    def body(a_ref, b_ref, o_ref, acc_ref):
        kk = pl.program_id(1)

        @pl.when(kk == 0)
        def _():
            acc_ref[...] = jnp.zeros_like(acc_ref)

        acc_ref[...] += _dot_tn(a_ref[...], b_ref[...])

        @pl.when(kk == nk - 1)
        def _():
            o_ref[...] = acc_ref[...].astype(out_dtype)

    return _call(body, name=name, grid=(m // tm, nk),
                 in_specs=[pl.BlockSpec((tk, tm), lambda i, kk: (kk, i)), pl.BlockSpec((tk, n), lambda i, kk: (kk, 0))],
                 out_specs=pl.BlockSpec((tm, n), lambda i, kk: (i, 0)),
                 out_shape=jax.ShapeDtypeStruct((m, n), out_dtype), args=[a, b],
                 scratch=[pltpu.VMEM((tm, n), F32)], sem=("parallel", "arbitrary"), carry=carry)


def _mm_tn_pair(a, b, name, carry=None):
    k, m = a.shape
    n = b.shape[1]
    rows = m // N_DEV
    n_chip = N_DEV // 2
    tm = 4 * rows
    tk = _pick(k, (1024, 512, 256, 128))
    nk = k // tk

    def body(a_ref, b_ref, p_ref, acc_ref, keep_ref, send_ref, land_ref, send_sems, recv_sems):
        i, kk = pl.program_id(0), pl.program_id(1)
        x, y, c = _mesh_pos()

        def push(chip):
            return pltpu.make_async_remote_copy(
                src_ref=send_ref.at[chip], dst_ref=land_ref.at[chip], send_sem=send_sems.at[chip],
                recv_sem=recv_sems.at[chip], device_id=(x, y, 1 - c), device_id_type=MESH)

        if nk == 1:
            acc = _dot_tn(a_ref[...], b_ref[...])
        else:
            @pl.when(kk == 0)
            def _():
                acc_ref[...] = jnp.zeros_like(acc_ref)

            acc_ref[...] += _dot_tn(a_ref[...], b_ref[...])
            acc = acc_ref

        for t in range(2):
            @pl.when((kk == nk - 1) & (i == t))
            def _(t=t):
                for ob in range(4):
                    chip, core = 2 * t + ob // 2, ob % 2
                    blk = acc[ob * rows:(ob + 1) * rows, :]

                    @pl.when(c == core)
                    def _(chip=chip, blk=blk):
                        keep_ref[chip] = blk

                    @pl.when(c != core)
                    def _(chip=chip, blk=blk):
                        send_ref[chip] = blk.astype(BF16)
                        push(chip).start()

        @pl.when((kk == nk - 1) & (i == 1))
        def _():
            for chip in range(n_chip):
                push(chip).wait_recv()
                p_ref[chip * rows:(chip + 1) * rows, :] = (
                    keep_ref[chip] + land_ref[chip].astype(F32)).astype(BF16)
            for chip in range(n_chip):
                push(chip).wait_send()

    return _call(body, name=name, grid=(2, nk),
                 in_specs=[pl.BlockSpec((tk, tm), lambda i, kk: (kk, i)), pl.BlockSpec((tk, n), lambda i, kk: (kk, 0))],
                 out_specs=pl.BlockSpec((n_chip * rows, n), lambda i, kk: (0, 0)),
                 out_shape=jax.ShapeDtypeStruct((n_chip * rows, n), BF16), args=[a, b],
                 scratch=[pltpu.VMEM((tm, n) if nk > 1 else (8, LANES), F32), pltpu.VMEM((n_chip, rows, n), F32),
                          pltpu.VMEM((n_chip, rows, n), BF16), pltpu.VMEM((n_chip, rows, n), BF16),
                          pltpu.SemaphoreType.DMA((n_chip,)), pltpu.SemaphoreType.DMA((n_chip,))],
                 sem=("arbitrary", "arbitrary"), carry=carry)


def _ffn_up(h, wg_t, wu_t, name, carry=None):
    s, d = h.shape
    f = wg_t.shape[0]
    tm = _pick(s, (512, 256, 128))
    tf = _pick(f, (1408, 1024, 512, 256, 128))

    def body(h_ref, wg_ref, wu_ref, a_ref, b_ref, u_ref):
        hh = h_ref[...]
        for lo, hi in _pieces(tf):
            a = _dot_nt(hh, wg_ref[lo:hi, :])
            b = _dot_nt(hh, wu_ref[lo:hi, :])
            a_ref[:, lo:hi] = a.astype(BF16)
            b_ref[:, lo:hi] = b.astype(BF16)
            u_ref[:, lo:hi] = ((a * _sigmoid(a)) * b).astype(BF16)

    w_spec = pl.BlockSpec((tf, d), lambda i, j: (j, 0))
    o_spec = pl.BlockSpec((tm, tf), lambda i, j: (i, j))
    o_shape = jax.ShapeDtypeStruct((s, f), BF16)
    return _call(body, name=name, grid=(s // tm, f // tf),
                 in_specs=[pl.BlockSpec((tm, d), lambda i, j: (i, 0)), w_spec, w_spec],
                 out_specs=(o_spec, o_spec, o_spec), out_shape=(o_shape, o_shape, o_shape),
                 args=[h, wg_t, wu_t], sem=("parallel", "parallel"), carry=carry)


def _ffn_down_bwd(dy, wd, a, b, name, carry=None):
    s, d = dy.shape
    f = wd.shape[0]
    tm = _pick(s, (512, 256, 128))
    tf = _pick(f, (1408, 1024, 512, 256, 128))

    def body(dy_ref, wd_ref, a_ref, b_ref, da_ref, db_ref):
        dyv = dy_ref[...]
        for lo, hi in _pieces(tf):
            du = _dot_nt(dyv, wd_ref[lo:hi, :])
            a = a_ref[:, lo:hi].astype(F32)
            b = b_ref[:, lo:hi].astype(F32)
            sig = _sigmoid(a)
            da_ref[:, lo:hi] = (du * b * (sig * (1.0 + a * (1.0 - sig)))).astype(BF16)
            db_ref[:, lo:hi] = (du * (a * sig)).astype(BF16)

    t_spec = pl.BlockSpec((tm, tf), lambda i, j: (i, j))
    o_shape = jax.ShapeDtypeStruct((s, f), BF16)
    return _call(body, name=name, grid=(s // tm, f // tf),
                 in_specs=[pl.BlockSpec((tm, d), lambda i, j: (i, 0)), pl.BlockSpec((tf, d), lambda i, j: (j, 0)),
                           t_spec, t_spec],
                 out_specs=(t_spec, t_spec), out_shape=(o_shape, o_shape), args=[dy, wd, a, b],
                 sem=("parallel", "parallel"), carry=carry)


def _row_tile(s):
    return _pick(s, (256, 128, 64))


def _vec_spec(d):
    return pl.BlockSpec((1, d), lambda i: (0, 0))


def _pre_norm(x, g, scale, shift, name):
    s, d = x.shape
    ts = _row_tile(s)

    def body(x_ref, g_ref, sc_ref, sh_ref, h_ref):
        xv = x_ref[...]
        r = lax.rsqrt(jnp.mean(xv * xv, axis=-1, keepdims=True) + EPS)
        h_ref[...] = (((xv * r) * g_ref[...]) * (1.0 + sc_ref[...]) + sh_ref[...]).astype(BF16)

    row = pl.BlockSpec((ts, d), lambda i: (i, 0))
    return _call(body, name=name, grid=(s // ts,), in_specs=[row, _vec_spec(d), _vec_spec(d), _vec_spec(d)],
                 out_specs=row, out_shape=jax.ShapeDtypeStruct((s, d), BF16), args=[x, g, scale, shift],
                 sem=("parallel",))


def _post_norm_residual(x, y, g, gate, weight, name):
    s, d = x.shape
    ts = _row_tile(s)

    def body(x_ref, y_ref, g_ref, gate_ref, o_ref):
        yv = y_ref[...]
        r = lax.rsqrt(jnp.mean(yv * yv, axis=-1, keepdims=True) + EPS)
        o_ref[...] = x_ref[...] + (weight * gate_ref[...]) * ((yv * r) * g_ref[...])

    row = pl.BlockSpec((ts, d), lambda i: (i, 0))
    return _call(body, name=name, grid=(s // ts,), in_specs=[row, row, _vec_spec(d), _vec_spec(d)],
                 out_specs=row, out_shape=jax.ShapeDtypeStruct((s, d), F32), args=[x, y, g, gate],
                 sem=("parallel",))


def _post_norm_bwd(dout, y, g, gate, weight, name):
    s, d = y.shape
    ts = _row_tile(s)

    def body(do_ref, y_ref, g_ref, gate_ref, dy_ref, s1_ref, cs_ref):
        @pl.when(pl.program_id(0) == 0)
        def _():
            s1_ref[...] = jnp.zeros_like(s1_ref)
            cs_ref[...] = jnp.zeros_like(cs_ref)

        yv = y_ref[...]
        do = do_ref[...]
        r = lax.rsqrt(jnp.mean(yv * yv, axis=-1, keepdims=True) + EPS)
        yn = yv * r
        dyn = do * ((weight * gate_ref[...]) * g_ref[...])
        dy = r * (dyn - yn * jnp.mean(dyn * yn, axis=-1, keepdims=True))
        dy_ref[...] = dy.astype(BF16)
        s1_ref[...] += jnp.sum(do * yn, axis=0, keepdims=True)
        cs_ref[...] += jnp.sum(dy, axis=0, keepdims=True)

    row = pl.BlockSpec((ts, d), lambda i: (i, 0))
    vec = jax.ShapeDtypeStruct((1, d), F32)
    return _call(body, name=name, grid=(s // ts,), in_specs=[row, row, _vec_spec(d), _vec_spec(d)],
                 out_specs=(row, _vec_spec(d), _vec_spec(d)),
                 out_shape=(jax.ShapeDtypeStruct((s, d), BF16), vec, vec), args=[dout, y, g, gate],
                 sem=("arbitrary",))


def _pre_norm_bwd(dh, x, g, scale, dres, name):
    s, d = x.shape
    ts = _row_tile(s)

    def body(dh_ref, x_ref, g_ref, sc_ref, dr_ref, dx_ref, s2_ref, s3_ref):
        @pl.when(pl.program_id(0) == 0)
        def _():
            s2_ref[...] = jnp.zeros_like(s2_ref)
            s3_ref[...] = jnp.zeros_like(s3_ref)

        xv = x_ref[...]
        dh = dh_ref[...]
        r = lax.rsqrt(jnp.mean(xv * xv, axis=-1, keepdims=True) + EPS)
        n = xv * r
        dn = dh * (g_ref[...] * (1.0 + sc_ref[...]))
        dx_ref[...] = dr_ref[...] + r * (dn - n * jnp.mean(dn * n, axis=-1, keepdims=True))
        s2_ref[...] += jnp.sum(dh * n, axis=0, keepdims=True)
        s3_ref[...] += jnp.sum(dh, axis=0, keepdims=True)

    row = pl.BlockSpec((ts, d), lambda i: (i, 0))
    vec = jax.ShapeDtypeStruct((1, d), F32)
    return _call(body, name=name, grid=(s // ts,), in_specs=[row, row, _vec_spec(d), _vec_spec(d), row],
                 out_specs=(row, _vec_spec(d), _vec_spec(d)),
                 out_shape=(jax.ShapeDtypeStruct((s, d), F32), vec, vec), args=[dh, x, g, scale, dres],
                 sem=("arbitrary",))


def _post_pre_norm(x, y, g_post, gate, weight, g_pre, scale, shift, name):
    s, d = x.shape
    ts = _row_tile(s)

    def body(x_ref, y_ref, gp_ref, gate_ref, g_ref, sc_ref, sh_ref, o_ref, h_ref):
        yv = y_ref[...]
        r = lax.rsqrt(jnp.mean(yv * yv, axis=-1, keepdims=True) + EPS)
        xv = x_ref[...] + (weight * gate_ref[...]) * ((yv * r) * gp_ref[...])
        o_ref[...] = xv
        r2 = lax.rsqrt(jnp.mean(xv * xv, axis=-1, keepdims=True) + EPS)
        h_ref[...] = (((xv * r2) * g_ref[...]) * (1.0 + sc_ref[...]) + sh_ref[...]).astype(BF16)

    row = pl.BlockSpec((ts, d), lambda i: (i, 0))
    return _call(body, name=name, grid=(s // ts,), in_specs=[row, row] + [_vec_spec(d)] * 5,
                 out_specs=(row, row),
                 out_shape=(jax.ShapeDtypeStruct((s, d), F32), jax.ShapeDtypeStruct((s, d), BF16)),
                 args=[x, y, g_post, gate, g_pre, scale, shift], sem=("parallel",))


def _post_norm_loss_bwd(x, y, g, gate, weight, target, name):
    s, d = y.shape
    ts = _row_tile(s)

    def body(x_ref, y_ref, g_ref, gate_ref, t_ref, dx_ref, dy_ref, l_ref, s1_ref):
        @pl.when(pl.program_id(0) == 0)
        def _():
            l_ref[...] = jnp.zeros_like(l_ref)
            s1_ref[...] = jnp.zeros_like(s1_ref)

        yv = y_ref[...]
        r = lax.rsqrt(jnp.mean(yv * yv, axis=-1, keepdims=True) + EPS)
        yn = yv * r
        err = (x_ref[...] + (weight * gate_ref[...]) * (yn * g_ref[...])) - t_ref[...]
        do = err * (1.0 / d)
        dx_ref[...] = do
        l_ref[...] += 0.5 * jnp.sum(jnp.mean(err * err, axis=-1, keepdims=True), axis=0, keepdims=True)
        dyn = do * ((weight * gate_ref[...]) * g_ref[...])
        dy_ref[...] = (r * (dyn - yn * jnp.mean(dyn * yn, axis=-1, keepdims=True))).astype(BF16)
        s1_ref[...] += jnp.sum(do * yn, axis=0, keepdims=True)

    row = pl.BlockSpec((ts, d), lambda i: (i, 0))
    return _call(body, name=name, grid=(s // ts,), in_specs=[row, row, _vec_spec(d), _vec_spec(d), row],
                 out_specs=(row, row, pl.BlockSpec((1, 1), lambda i: (0, 0)), _vec_spec(d)),
                 out_shape=(jax.ShapeDtypeStruct((s, d), F32), jax.ShapeDtypeStruct((s, d), BF16),
                            jax.ShapeDtypeStruct((1, 1), F32), jax.ShapeDtypeStruct((1, d), F32)),
                 args=[x, y, g, gate, target], sem=("arbitrary",))


def _pre_post_norm_bwd(dh, x, g_pre, scale, dres, y, g_post, gate, weight, name):
    s, d = x.shape
    ts = _row_tile(s)

    def body(dh_ref, x_ref, g_ref, sc_ref, dr_ref, y_ref, gp_ref, gate_ref,
             dx_ref, dy_ref, s2_ref, s3_ref, s1_ref, cs_ref):
        @pl.when(pl.program_id(0) == 0)
        def _():
            for ref in (s2_ref, s3_ref, s1_ref, cs_ref):
                ref[...] = jnp.zeros_like(ref)

        xv = x_ref[...]
        dh = dh_ref[...]
        r = lax.rsqrt(jnp.mean(xv * xv, axis=-1, keepdims=True) + EPS)
        n = xv * r
        dn = dh * (g_ref[...] * (1.0 + sc_ref[...]))
        dx = dr_ref[...] + r * (dn - n * jnp.mean(dn * n, axis=-1, keepdims=True))
        dx_ref[...] = dx
        s2_ref[...] += jnp.sum(dh * n, axis=0, keepdims=True)
        s3_ref[...] += jnp.sum(dh, axis=0, keepdims=True)
        yv = y_ref[...]
        ry = lax.rsqrt(jnp.mean(yv * yv, axis=-1, keepdims=True) + EPS)
        yn = yv * ry
        dyn = dx * ((weight * gate_ref[...]) * gp_ref[...])
        dy = ry * (dyn - yn * jnp.mean(dyn * yn, axis=-1, keepdims=True))
        dy_ref[...] = dy.astype(BF16)
        s1_ref[...] += jnp.sum(dx * yn, axis=0, keepdims=True)
        cs_ref[...] += jnp.sum(dy, axis=0, keepdims=True)

    row = pl.BlockSpec((ts, d), lambda i: (i, 0))
    vec = jax.ShapeDtypeStruct((1, d), F32)
    return _call(body, name=name, grid=(s // ts,),
                 in_specs=[row, row, _vec_spec(d), _vec_spec(d), row, row, _vec_spec(d), _vec_spec(d)],
                 out_specs=(row, row) + (_vec_spec(d),) * 4,
                 out_shape=(jax.ShapeDtypeStruct((s, d), F32), jax.ShapeDtypeStruct((s, d), BF16), vec, vec, vec, vec),
                 args=[dh, x, g_pre, scale, dres, y, g_post, gate], sem=("arbitrary",))


def _group_norm_cat(oa, ob, ga, gb):
    s = oa.shape[0]
    ts = _row_tile(s)

    def body(oa_ref, ob_ref, ga_ref, gb_ref, y_ref):
        for o_ref, g_ref, lo, w in ((oa_ref, ga_ref, 0, QA), (ob_ref, gb_ref, QA, QB)):
            ov = o_ref[...]
            r = lax.rsqrt(jnp.mean(ov * ov, axis=-1, keepdims=True) + EPS)
            y_ref[:, lo:lo + w] = ((ov * r) * g_ref[...]).astype(BF16)

    return _call(body, name="group_norm_cat", grid=(s // ts,),
                 in_specs=[pl.BlockSpec((ts, QA), lambda i: (i, 0)), pl.BlockSpec((ts, QB), lambda i: (i, 0)),
                           _vec_spec(QA), _vec_spec(QB)],
                 out_specs=pl.BlockSpec((ts, QA + QB), lambda i: (i, 0)),
                 out_shape=jax.ShapeDtypeStruct((s, QA + QB), BF16), args=[oa, ob, ga, gb], sem=("parallel",))


def _group_norm_bwd(dy, oa, ob, ga, gb):
    s = oa.shape[0]
    ts = _row_tile(s)

    def body(dy_ref, oa_ref, ob_ref, ga_ref, gb_ref, doa_ref, dob_ref, dga_ref, dgb_ref):
        @pl.when(pl.program_id(0) == 0)
        def _():
            dga_ref[...] = jnp.zeros_like(dga_ref)
            dgb_ref[...] = jnp.zeros_like(dgb_ref)

        for o_ref, g_ref, do_ref, dg_ref, lo, w in ((oa_ref, ga_ref, doa_ref, dga_ref, 0, QA),
                                                    (ob_ref, gb_ref, dob_ref, dgb_ref, QA, QB)):
            ov = o_ref[...]
            dyv = dy_ref[:, lo:lo + w]
            r = lax.rsqrt(jnp.mean(ov * ov, axis=-1, keepdims=True) + EPS)
            n = ov * r
            dn = dyv * g_ref[...]
            do_ref[...] = r * (dn - n * jnp.mean(dn * n, axis=-1, keepdims=True))
            dg_ref[...] += jnp.sum(dyv * n, axis=0, keepdims=True)

    ra = pl.BlockSpec((ts, QA), lambda i: (i, 0))
    rb = pl.BlockSpec((ts, QB), lambda i: (i, 0))
    return _call(body, name="group_norm_bwd", grid=(s // ts,),
                 in_specs=[pl.BlockSpec((ts, QA + QB), lambda i: (i, 0)), ra, rb, _vec_spec(QA), _vec_spec(QB)],
                 out_specs=(ra, rb, _vec_spec(QA), _vec_spec(QB)),
                 out_shape=(jax.ShapeDtypeStruct((s, QA), F32), jax.ShapeDtypeStruct((s, QB), F32),
                            jax.ShapeDtypeStruct((1, QA), F32), jax.ShapeDtypeStruct((1, QB), F32)),
                 args=[dy, oa, ob, ga, gb], sem=("arbitrary",))


def _loss_and_grad(y, target):
    s, d = y.shape
    ts = _row_tile(s)

    def body(y_ref, t_ref, l_ref, g_ref):
        @pl.when(pl.program_id(0) == 0)
        def _():
            l_ref[...] = jnp.zeros_like(l_ref)

        err = y_ref[...] - t_ref[...]
        g_ref[...] = err * (1.0 / d)
        row = jnp.mean(err * err, axis=-1, keepdims=True)
        l_ref[...] += 0.5 * jnp.sum(row, axis=0, keepdims=True)

    row = pl.BlockSpec((ts, d), lambda i: (i, 0))
    return _call(body, name="loss_and_grad", grid=(s // ts,), in_specs=[row, row],
                 out_specs=(pl.BlockSpec((1, 1), lambda i: (0, 0)), row),
                 out_shape=(jax.ShapeDtypeStruct((1, 1), F32), jax.ShapeDtypeStruct((s, d), F32)),
                 args=[y, target], sem=("arbitrary",))


def _col_sum(x, name):
    s, n = x.shape
    ts = _row_tile(s)

    def body(x_ref, o_ref):
        @pl.when(pl.program_id(0) == 0)
        def _():
            o_ref[...] = jnp.zeros_like(o_ref)

        o_ref[...] += jnp.sum(x_ref[...].astype(F32), axis=0, keepdims=True)

    return _call(body, name=name, grid=(s // ts,), in_specs=[pl.BlockSpec((ts, n), lambda i: (i, 0))],
                 out_specs=pl.BlockSpec((1, n), lambda i: (0, 0)), out_shape=jax.ShapeDtypeStruct((1, n), F32),
                 args=[x], sem=("arbitrary",))


def _n_variants(n_back):
    return -(-n_back // QG) + 1


def _alibi_bias():
    i = np.arange(QROWS)[:, None]
    j = np.arange((QG + BACK_A) * CHUNK)[None, :]
    dist = np.abs(BACK_A * CHUNK + i - j).astype(np.float32)
    dc = j // CHUNK - i // CHUNK
    valid = (dc >= 0) & (dc <= BACK_A)
    slopes = np.array([2.0 ** (-8.0 * (h + 1) / H_A) for h in range(H_A)], dtype=np.float32)
    bias = -slopes[:, None, None] * dist[None]
    out = [np.where((valid & (j >= (BACK_A - QG * v) * CHUNK))[None], bias, np.float32(NEG_INF))
           for v in range(_n_variants(BACK_A))]
    return jnp.asarray(np.stack(out).astype(np.float32))


def _rel_index_matrix():
    cc = np.arange(SKEW)
    dist = np.where(cc < SKEW - QROWS, BACK_B * CHUNK - cc, BACK_B * CHUNK + SKEW - cc)
    idx = np.clip(dist, -REL_CLIP, REL_CLIP) + REL_CLIP
    m = np.zeros((SKEW, N_REL), np.float32)
    m[cc, idx] = 1.0
    return jnp.asarray(m)


def _toeplitz_bias(vec, carry=None):
    lk = (QG + BACK_B) * CHUNK
    nv = _n_variants(BACK_B)

    def body(v_ref, o_ref):
        xv = jnp.broadcast_to(v_ref[0], (QROWS, SKEW))
        row = lax.broadcasted_iota(jnp.int32, (QROWS, SKEW), 0)
        for bit in range(QROWS.bit_length() - 1):
            xv = jnp.where((row >> bit) & 1 == 1, pltpu.roll(xv, 1 << bit, 1), xv)
        ri = lax.broadcasted_iota(jnp.int32, (QROWS, lk), 0) // CHUNK
        col = lax.broadcasted_iota(jnp.int32, (QROWS, lk), 1)
        ci = col // CHUNK
        valid = (ci - ri >= 0) & (ci - ri <= BACK_B)
        for v in range(nv):
            o_ref[v, 0] = jnp.where(valid & (col >= (BACK_B - QG * v) * CHUNK), xv[:, :lk], NEG_INF)

    return _call(body, name="toeplitz_bias", grid=(H_B,),
                 in_specs=[pl.BlockSpec((1, 1, SKEW), lambda h: (h, 0, 0))],
                 out_specs=pl.BlockSpec((nv, 1, QROWS, lk), lambda h: (0, h, 0, 0)),
                 out_shape=jax.ShapeDtypeStruct((nv, H_B, QROWS, lk), F32), args=[vec], sem=("parallel",),
                 carry=carry)


def _diagonal_sums(dbias):
    lk = dbias.shape[2]

    def body(d_ref, o_ref):
        xp = jnp.concatenate([d_ref[0], jnp.zeros((QROWS, SKEW - lk), F32)], axis=1)
        xv = xp[0:CHUNK]
        for q in range(1, QG):
            xv = xv + pltpu.roll(xp[q * CHUNK:(q + 1) * CHUNK], SKEW - q * CHUNK, 1)
        row = lax.broadcasted_iota(jnp.int32, (CHUNK, SKEW), 0)
        for bit in range(CHUNK.bit_length() - 1):
            xv = jnp.where((row >> bit) & 1 == 1, pltpu.roll(xv, SKEW - (1 << bit), 1), xv)
        o_ref[0] = jnp.sum(xv, axis=0, keepdims=True)

    return _call(body, name="diagonal_sums", grid=(H_B,),
                 in_specs=[pl.BlockSpec((1, QROWS, lk), lambda h: (h, 0, 0))],
                 out_specs=pl.BlockSpec((1, 1, SKEW), lambda h: (h, 0, 0)),
                 out_shape=jax.ShapeDtypeStruct((H_B, 1, SKEW), F32), args=[dbias], sem=("parallel",))


def _attn_common(s, n_back, gqa, q_col, k_col, v_col):
    lk = (QG + n_back) * CHUNK
    pad = n_back * CHUNK
    wide = TPS * LANES
    q_spec = pl.BlockSpec((QROWS, wide), lambda t, g: (g, q_col // TPS + t))
    if gqa:
        k_spec = pl.BlockSpec((s, LANES), lambda t, g: (0, k_col))
        v_spec = pl.BlockSpec((s, LANES), lambda t, g: (0, v_col))
    else:
        k_spec = pl.BlockSpec((s, wide), lambda t, g: (0, k_col // TPS + t))
        v_spec = pl.BlockSpec((s, wide), lambda t, g: (0, v_col // TPS + t))
    last_variant = _n_variants(n_back) - 1
    bias_spec = pl.BlockSpec((None, 2 * TPS, QROWS, lk), lambda t, g: (jnp.minimum(g, last_variant), t, 0, 0))
    tile_spec = pl.BlockSpec((QROWS, wide), lambda t, g: (g, t))
    return lk, pad, q_spec, k_spec, v_spec, bias_spec, tile_spec


def _attention_fwd(proj, bias, sinks, *, n_back, gqa, q_col, k_col, v_col, name, carry=None):
    s = proj.shape[0]
    lk, pad, q_spec, k_spec, v_spec, bias_spec, tile_spec = _attn_common(s, n_back, gqa, q_col, k_col, v_col)
    n_t, n_g = 512 // (TPS * LANES), s // QROWS
    kv_wide = LANES if gqa else TPS * LANES

    def body(*refs):
        if gqa:
            q_ref, k_ref, v_ref, bias_ref, sink_ref, o_ref, l_ref, kpad, vpad = refs
        else:
            q_ref, k_ref, v_ref, bias_ref, o_ref, l_ref, kpad, vpad = refs
        t, g = pl.program_id(0), pl.program_id(1)

        @pl.when(g == 0)
        def _():
            kpad[0:pad, :] = jnp.zeros((pad, kv_wide), BF16)
            vpad[0:pad, :] = jnp.zeros((pad, kv_wide), BF16)
            kpad[pad:, :] = k_ref[...]
            vpad[pad:, :] = v_ref[...]

        start = pl.multiple_of(g * QROWS, QROWS)
        half = lax.broadcasted_iota(jnp.int32, (QROWS, LANES), 1) // HEAD_DIM
        for tt in range(TPS):
            lanes = slice(tt * LANES, (tt + 1) * LANES)
            kv_lanes = slice(0, LANES) if gqa else lanes
            kb = kpad[pl.ds(start, lk), kv_lanes]
            vb = vpad[pl.ds(start, lk), kv_lanes]
            q = q_ref[:, lanes] * (HEAD_DIM ** -0.5)
            if gqa:
                hk = (TPS * t + tt) // 2
                q_rolled = pltpu.roll(q.astype(F32), HEAD_DIM, 1).astype(BF16)
            outs, lses = [], []
            for e in range(2):
                if gqa:
                    kv_half = hk
                    src = jnp.where(hk == e, q, q_rolled)
                else:
                    kv_half = e
                    src = q
                qm = jnp.where(half == kv_half, src, jnp.zeros_like(src))
                sc = _dot_nt(qm, kb) + bias_ref[2 * tt + e]
                m = jnp.max(sc, axis=-1, keepdims=True)
                if gqa:
                    sk = sink_ref[2 * (TPS * t + tt) + e]
                    m = jnp.maximum(m, sk)
                p = jnp.exp(sc - m)
                l = jnp.sum(p, axis=-1, keepdims=True)
                if gqa:
                    l = l + jnp.exp(sk - m)
                pn = p / l
                outs.append(_dot(pn.astype(BF16), vb))
                lses.append(m + jnp.log(l))
            if gqa:
                same = jnp.where(hk == 0, outs[0], outs[1])
                other = jnp.where(hk == 0, outs[1], outs[0])
                o_ref[:, lanes] = jnp.where(half == hk, same, pltpu.roll(other, HEAD_DIM, 1))
            else:
                o_ref[:, lanes] = jnp.where(half == 0, outs[0], outs[1])
            l_ref[:, lanes] = jnp.where(half == 0, lses[0], lses[1])

    in_specs = [q_spec, k_spec, v_spec, bias_spec] + ([SMEM_SPEC] if gqa else [])
    args = [proj, proj, proj, bias] + ([sinks] if gqa else [])
    o_shape = jax.ShapeDtypeStruct((s, 512), F32)
    return _call(body, name=name, grid=(n_t, n_g), in_specs=in_specs, out_specs=(tile_spec, tile_spec),
                 out_shape=(o_shape, o_shape), args=args,
                 scratch=[pltpu.VMEM((s + pad, kv_wide), BF16), pltpu.VMEM((s + pad, kv_wide), BF16)],
                 sem=("arbitrary", "arbitrary"), carry=carry)


def _attention_bwd(proj, bias, sinks, do, lse, *, n_back, gqa, q_col, k_col, v_col, name, carry=None):
    s = proj.shape[0]
    lk, pad, q_spec, k_spec, v_spec, bias_spec, tile_spec = _attn_common(s, n_back, gqa, q_col, k_col, v_col)
    n_t, n_g = 512 // (TPS * LANES), s // QROWS
    kv_wide = LANES if gqa else TPS * LANES

    def body(*refs):
        if gqa:
            (q_ref, k_ref, v_ref, bias_ref, sink_ref, do_ref, l_ref,
             dq_ref, dk_ref, dv_ref, dsink_ref, kpad, vpad, dkpad, dvpad) = refs
        else:
            (q_ref, k_ref, v_ref, bias_ref, do_ref, l_ref,
             dq_ref, dk_ref, dv_ref, dbias_ref, kpad, vpad, dkpad, dvpad) = refs
        t, g = pl.program_id(0), pl.program_id(1)

        @pl.when(g == 0)
        def _():
            kpad[0:pad, :] = jnp.zeros((pad, kv_wide), BF16)
            vpad[0:pad, :] = jnp.zeros((pad, kv_wide), BF16)
            kpad[pad:, :] = k_ref[...]
            vpad[pad:, :] = v_ref[...]
            if gqa:
                dsink_ref[...] = jnp.zeros_like(dsink_ref)
            else:
                dbias_ref[...] = jnp.zeros_like(dbias_ref)

        @pl.when((g == 0) & (t == 0) if gqa else g == 0)
        def _():
            dkpad[...] = jnp.zeros_like(dkpad)
            dvpad[...] = jnp.zeros_like(dvpad)

        start = pl.multiple_of(g * QROWS, QROWS)
        half = lax.broadcasted_iota(jnp.int32, (QROWS, LANES), 1) // HEAD_DIM
        for tt in range(TPS):
            lanes = slice(tt * LANES, (tt + 1) * LANES)
            kv_lanes = slice(0, LANES) if gqa else lanes
            kb = kpad[pl.ds(start, lk), kv_lanes]
            vb = vpad[pl.ds(start, lk), kv_lanes]
            q = q_ref[:, lanes]
            dov = do_ref[:, lanes]
            lv = l_ref[:, lanes]
            if gqa:
                hk = (TPS * t + tt) // 2
                q_rolled = pltpu.roll(q.astype(F32), HEAD_DIM, 1).astype(BF16)
                do_rolled = pltpu.roll(dov, HEAD_DIM, 1)
            dqs = []
            dk_acc = jnp.zeros((lk, LANES), F32)
            dv_acc = jnp.zeros((lk, LANES), F32)
            for e in range(2):
                if gqa:
                    kv_half = hk
                    src = jnp.where(hk == e, q, q_rolled)
                    do_src = jnp.where(hk == e, dov, do_rolled)
                else:
                    kv_half = e
                    src = q
                    do_src = dov
                qm = jnp.where(half == kv_half, src, jnp.zeros_like(src))
                dom = jnp.where(half == kv_half, do_src, 0.0).astype(BF16)
                lcol = jnp.max(jnp.where(half == e, lv, -jnp.inf), axis=-1, keepdims=True)
                sc = _dot_nt(qm * (HEAD_DIM ** -0.5), kb) + bias_ref[2 * tt + e]
                pn = jnp.exp(sc - lcol)
                dp = _dot_nt(dom, vb)
                delta = jnp.sum(pn * dp, axis=-1, keepdims=True)
                ds = pn * (dp - delta)
                if gqa:
                    p_sink = jnp.exp(sink_ref[2 * (TPS * t + tt) + e] - lcol)
                    dsk = -jnp.sum(p_sink * delta, axis=0, keepdims=True)
                    row = 2 * tt + e
                    dsink_ref[0, row:row + 1, :] += jnp.broadcast_to(dsk, (1, LANES))
                else:
                    dbias_ref[2 * tt + e] += ds
                dsb = (ds * (HEAD_DIM ** -0.5)).astype(BF16)
                dqs.append(_dot(dsb, kb))
                dk_acc = dk_acc + _dot_tn(dsb, qm)
                dv_acc = dv_acc + _dot_tn(pn.astype(BF16), dom)
            dkpad[pl.ds(start, lk), kv_lanes] += dk_acc
            dvpad[pl.ds(start, lk), kv_lanes] += dv_acc
            if gqa:
                same = jnp.where(hk == 0, dqs[0], dqs[1])
                other = jnp.where(hk == 0, dqs[1], dqs[0])
                dq_ref[:, lanes] = jnp.where(half == hk, same, pltpu.roll(other, HEAD_DIM, 1)).astype(BF16)
            else:
                dq_ref[:, lanes] = jnp.where(half == 0, dqs[0], dqs[1]).astype(BF16)

        @pl.when((g == n_g - 1) & (t == n_t - 1) if gqa else g == n_g - 1)
        def _():
            dk_ref[...] = dkpad[pad:, :].astype(BF16)
            dv_ref[...] = dvpad[pad:, :].astype(BF16)

    in_specs = [q_spec, k_spec, v_spec, bias_spec] + ([SMEM_SPEC] if gqa else []) + [tile_spec, tile_spec]
    args = [proj, proj, proj, bias] + ([sinks] if gqa else []) + [do, lse]
    if gqa:
        kv_out = pl.BlockSpec((s, LANES), lambda t, g: (0, 0))
        kv_shape = jax.ShapeDtypeStruct((s, LANES), BF16)
        extra_spec = pl.BlockSpec((1, 8, LANES), lambda t, g: (t, 0, 0))
        extra_shape = jax.ShapeDtypeStruct((n_t, 8, LANES), F32)
    else:
        kv_out = pl.BlockSpec((s, kv_wide), lambda t, g: (0, t))
        kv_shape = jax.ShapeDtypeStruct((s, 512), BF16)
        extra_spec = pl.BlockSpec((2 * TPS, QROWS, lk), lambda t, g: (t, 0, 0))
        extra_shape = jax.ShapeDtypeStruct(bias.shape[1:], F32)
    return _call(body, name=name, grid=(n_t, n_g), in_specs=in_specs,
                 out_specs=(tile_spec, kv_out, kv_out, extra_spec),
                 out_shape=(jax.ShapeDtypeStruct((s, 512), BF16), kv_shape, kv_shape, extra_shape), args=args,
                 scratch=[pltpu.VMEM((s + pad, kv_wide), BF16), pltpu.VMEM((s + pad, kv_wide), BF16),
                          pltpu.VMEM((s + pad, kv_wide), F32), pltpu.VMEM((s + pad, kv_wide), F32)],
                 sem=("arbitrary", "arbitrary"), carry=carry)


def _sum_slots(r, name):
    n_slots, rows, k = r.shape

    def body(r_ref, o_ref):
        acc = r_ref[0].astype(F32)
        for j in range(1, n_slots):
            acc = acc + r_ref[j].astype(F32)
        o_ref[...] = acc

    return _call(body, name=name, grid=(k // LANES,),
                 in_specs=[pl.BlockSpec((n_slots, rows, LANES), lambda i: (0, 0, i))],
                 out_specs=pl.BlockSpec((rows, LANES), lambda i: (0, i)),
                 out_shape=jax.ShapeDtypeStruct((rows, k), F32), args=[r], sem=("parallel",))


def _sum_rows8(g):
    n = g.shape[2]

    def body(g_ref, o_ref):
        acc = g_ref[0]
        for j in range(1, N_DEV):
            acc = acc + g_ref[j]
        o_ref[...] = acc

    return pl.pallas_call(
        body, name="sum_small_grads", in_specs=[VMEM_SPEC], out_specs=VMEM_SPEC,
        out_shape=jax.ShapeDtypeStruct((1, n), F32), compiler_params=_params(),
    )(g)


def _ada_weight_grad(sc_t, dmod_cols):
    d = sc_t.shape[0]
    w = dmod_cols.shape[1]
    td = _pick(d, (256, 128))

    def body(sc_ref, dm_ref, o_ref):
        scv = sc_ref[...]
        dmv = dm_ref[...]
        acc = scv[:, 0:1] * dmv[0:1, :]
        for b in range(1, N_DEV):
            acc = acc + scv[:, b:b + 1] * dmv[b:b + 1, :]
        o_ref[...] = acc

    return _call(body, name="ada_weight_grad", grid=(d // td,),
                 in_specs=[pl.BlockSpec((td, N_DEV), lambda i: (i, 0)), pl.BlockSpec((N_DEV, w), lambda i: (0, 0))],
                 out_specs=pl.BlockSpec((td, w), lambda i: (i, 0)), out_shape=jax.ShapeDtypeStruct((d, w), F32),
                 args=[sc_t, dmod_cols], sem=("parallel",))


def _adamw_update(w, gv, m, v):
    nm = ADAM_B1 * m + (1.0 - ADAM_B1) * gv
    nv = ADAM_B2 * v + (1.0 - ADAM_B2) * (gv * gv)
    m_hat = nm / (1.0 - ADAM_B1 ** ADAM_STEP)
    v_hat = nv / (1.0 - ADAM_B2 ** ADAM_STEP)
    return -ADAM_LR * (m_hat / (jnp.sqrt(v_hat) + ADAM_EPS) + ADAM_WD * w), nm, nv


def _adamw(w, g, m, v, name):
    rows, cols = w.shape
    tr = _pick(rows, (256, 176, 128, 88, 64)) if rows > 256 else rows

    def body(w_ref, g_ref, m_ref, v_ref, d_ref, nm_ref, nv_ref):
        d_ref[...], nm_ref[...], nv_ref[...] = _adamw_update(w_ref[...], g_ref[...], m_ref[...], v_ref[...])

    spec = pl.BlockSpec((tr, cols), lambda i: (i, 0))
    shape = jax.ShapeDtypeStruct((rows, cols), F32)
    return _call(body, name=name, grid=(rows // tr,), in_specs=[spec] * 4, out_specs=(spec, spec, spec),
                 out_shape=(shape, shape, shape), args=[w, g, m, v], sem=("parallel",))


def _adamw_from_slots(w, slots, m, v, name):
    n_slots, rows, k = slots.shape

    def body(s_ref, w_ref, m_ref, v_ref, g_ref, d_ref, nm_ref, nv_ref):
        gv = s_ref[0].astype(F32)
        for j in range(1, n_slots):
            gv = gv + s_ref[j].astype(F32)
        g_ref[...] = gv
        d_ref[...], nm_ref[...], nv_ref[...] = _adamw_update(w_ref[...], gv, m_ref[...], v_ref[...])

    tr = rows // 2 if rows % 32 == 0 else rows
    spec = pl.BlockSpec((tr, k), lambda i: (i, 0))
    shape = jax.ShapeDtypeStruct((rows, k), F32)
    return _call(body, name=name, grid=(rows // tr,),
                 in_specs=[pl.BlockSpec((n_slots, tr, k), lambda i: (0, i, 0)), spec, spec, spec],
                 out_specs=(spec, spec, spec, spec), out_shape=(shape, shape, shape, shape),
                 args=[slots, w, m, v], sem=("parallel",))


def _adamw_small(g, w, m, v, sizes):
    n = w.shape[1]
    offs, off = [], 0
    for size in sizes:
        offs.append(off)
        off += size + (-size % LANES)

    def body(g_ref, w_ref, m_ref, v_ref, *out_refs):
        gv = g_ref[:, 0:n]
        dv, nm, nv = _adamw_update(w_ref[...], gv, m_ref[...], v_ref[...])
        for j, (o, size) in enumerate(zip(offs, sizes)):
            for k, val in enumerate((gv, dv, nm, nv)):
                out_refs[4 * j + k][...] = val[:, o:o + size]

    shapes = [jax.ShapeDtypeStruct((1, size), F32) for size in sizes for _ in range(4)]
    return pl.pallas_call(
        body, name="adamw_small", in_specs=[VMEM_SPEC] * 4, out_specs=tuple([VMEM_SPEC] * len(shapes)),
        out_shape=tuple(shapes), compiler_params=_params(),
    )(g, w, m, v)


SMALL = ("b_ada", "g_pre_ffn1", "g_post_ffn1", "g_pre_mix", "b_in", "sinks_a", "rel_bias_b", "g_grp_a",
         "g_grp_b", "b_out", "g_post_mix", "g_pre_ffn2", "g_post_ffn2")
WEIGHTS = ("w_ada", "b_ada", "g_pre_ffn1", "w_gate1", "w_up1", "w_down1", "g_post_ffn1", "g_pre_mix", "w_in",
           "b_in", "sinks_a", "rel_bias_b", "g_grp_a", "g_grp_b", "w_out", "b_out", "g_post_mix", "g_pre_ffn2",
           "w_gate2", "w_up2", "w_down2", "g_post_ffn2")


def kernel(x, c, w_ada, b_ada, g_pre_ffn1, w_gate1, w_up1, w_down1, g_post_ffn1, g_pre_mix, w_in, b_in, sinks_a, rel_bias_b, g_grp_a, g_grp_b, w_out, b_out, g_post_mix, g_pre_ffn2, w_gate2, w_up2, w_down2, g_post_ffn2, loss_target, m_w_ada, m_b_ada, m_g_pre_ffn1, m_w_gate1, m_w_up1, m_w_down1, m_g_post_ffn1, m_g_pre_mix, m_w_in, m_b_in, m_sinks_a, m_rel_bias_b, m_g_grp_a, m_g_grp_b, m_w_out, m_b_out, m_g_post_mix, m_g_pre_ffn2, m_w_gate2, m_w_up2, m_w_down2, m_g_post_ffn2, v_w_ada, v_b_ada, v_g_pre_ffn1, v_w_gate1, v_w_up1, v_w_down1, v_g_post_ffn1, v_g_pre_mix, v_w_in, v_b_in, v_sinks_a, v_rel_bias_b, v_g_grp_a, v_g_grp_b, v_w_out, v_b_out, v_g_post_mix, v_g_pre_ffn2, v_w_gate2, v_w_up2, v_w_down2, v_g_post_ffn2):
    given = dict(locals())
    weights = {n: given[n] for n in WEIGHTS}
    mom_m = {n: given["m_" + n] for n in WEIGHTS}
    mom_v = {n: given["v_" + n] for n in WEIGHTS}

    me = 4 * lax.axis_index("x") + 2 * lax.axis_index("y") + lax.axis_index("c")
    xs = x[0]
    tgt = loss_target[0]
    d_model = xs.shape[1]
    ada_cols = w_ada.shape[2]

    sh = {"wg1": w_gate1[0].T, "wu1": w_up1[0].T, "wd1": w_down1[0], "win": w_in[0].T, "wo": w_out[0],
          "wg2": w_gate2[0].T, "wu2": w_up2[0].T, "wd2": w_down2[0]}
    sh = {k: v.astype(BF16) for k, v in sh.items()}

    def gather(*names):
        return _gather_carry([sh[n] for n in names])

    bias_a = _alibi_bias()
    rel_m = _rel_index_matrix()
    rel_vec = jnp.dot(rel_bias_b[0], rel_m.T, precision=lax.Precision.HIGHEST)
    bias_b, (wg1, wu1) = _toeplitz_bias(rel_vec.reshape(H_B, 1, SKEW), carry=gather("wg1", "wu1"))

    b_cols = lax.dynamic_slice(b_ada, (0, me * ada_cols), (1, ada_cols))
    (sc_all, mod_rows), _ = _ada_forward(c, w_ada[0], b_cols, _Carry([], [], [], lambda *a: None, lambda *a: None))
    mod = mod_rows.reshape(N_MOD, d_model)
    shift1, scale1, gate1, shift2, scale2, gate2, shift3, scale3, gate3 = (mod[i:i + 1] for i in range(N_MOD))

    h1 = _pre_norm(xs, g_pre_ffn1, scale1, shift1, "pre_norm_ffn1")
    (a1, b1, u1), (wd1,) = _ffn_up(h1, wg1, wu1, "ffn_up_ffn1", carry=gather("wd1"))
    (y1, x1, h2), (win,) = _mm_nn(
        [(u1, wd1)], "ffn_down_ffn1", F32, carry=gather("win"),
        tail=_tail_post_pre(xs, g_post_ffn1, gate1, 0.5, g_pre_mix, scale2, shift2))

    proj, (wo,) = _mm_nt(h2, win, "in_proj", BF16, bias=b_in, carry=gather("wo"))
    sinks = sinks_a[0]
    cfg_a = dict(n_back=BACK_A, gqa=True, q_col=0, k_col=QA // LANES, v_col=(QA + KVA) // LANES)
    cfg_b = dict(n_back=BACK_B, gqa=False, q_col=(QA + 2 * KVA) // LANES, k_col=(QA + 2 * KVA + QB) // LANES,
                 v_col=(QA + 2 * KVA + 2 * QB) // LANES)
    (oa, lse_a), (wg2,) = _attention_fwd(proj, bias_a, sinks, name="attn_a", carry=gather("wg2"), **cfg_a)
    (ob, lse_b), (wu2,) = _attention_fwd(proj, bias_b, None, name="attn_b", carry=gather("wu2"), **cfg_b)
    ycat = _group_norm_cat(oa, ob, g_grp_a, g_grp_b)
    ymix, x2, h3 = _mm_nn([(ycat, wo)], "out_proj", F32, bias=b_out,
                          tail=_tail_post_pre(x1, g_post_mix, gate2, 1.0, g_pre_ffn2, scale3, shift3))

    (a3, b3, u3), (wd2,) = _ffn_up(h3, wg2, wu2, "ffn_up_ffn2", carry=gather("wd2"))

    def scatter(*grads):
        return _scatter_carry(list(grads))

    slots = {}

    dx3, dy, loss_part, s1 = _mm_nn([(u3, wd2)], "ffn_down_ffn2", None,
                                    tail=_tail_post_loss(x2, tgt, g_post_ffn2, gate3, 0.5))
    da, db = _ffn_down_bwd(dy, wd2, a3, b3, "ffn_down_bwd_ffn2")
    dwd2 = _mm_tn_pair(u3, dy, "grad_wd_ffn2")
    dwg2 = _mm_tn_pair(da, h3, "grad_wg_ffn2")
    dwu2 = _mm_tn_pair(db, h3, "grad_wu_ffn2")
    (dx2, dymix, s2, s3, s1m, db_out), (slots["wd2"],) = _mm_nn(
        [(da, wg2), (db, wu2)], "ffn_up_bwd_ffn2", None, carry=scatter(dwd2),
        tail=_tail_pre_post_bwd(x2, dx3, ymix, g_pre_ffn2, scale3, g_post_mix, gate2, 1.0))
    sm3 = dict(shift=s3, scale=s2 * g_pre_ffn2, gate=0.5 * g_post_ffn2 * s1,
               g_pre=(1.0 + scale3) * s2, g_post=(0.5 * gate3) * s1)

    dycat = _mm_nt(dymix, wo, "out_proj_bwd", F32)
    dwo = _mm_tn_pair(ycat, dymix, "grad_wo")
    doa, dob, dg_a, dg_b = _group_norm_bwd(dycat, oa, ob, g_grp_a, g_grp_b)
    (dqa, dka, dva, dsink), (slots["wg2"],) = _attention_bwd(
        proj, bias_a, sinks, doa, lse_a, name="attn_a_bwd", carry=scatter(dwg2), **cfg_a)
    (dqb, dkb, dvb, dbias), (slots["wu2"], slots["wo"]) = _attention_bwd(
        proj, bias_b, None, dob, lse_b, name="attn_b_bwd", carry=scatter(dwu2, dwo), **cfg_b)
    dproj = jnp.concatenate([dqa, dka, dva, dqb, dkb, dvb], axis=1)
    db_in = _col_sum(dproj, "grad_b_in")
    dwin = _mm_tn_pair(dproj, h2, "grad_win")
    dx1, dy, s2m, s3m, s1, _ = _mm_nn(
        [(dproj, win)], "in_proj_bwd", None,
        tail=_tail_pre_post_bwd(x1, dx2, y1, g_pre_mix, scale2, g_post_ffn1, gate1, 0.5))
    d_rel = jnp.dot(_diagonal_sums(dbias).reshape(H_B, SKEW), rel_m, precision=lax.Precision.HIGHEST)
    d_sinks = dsink[:, :2 * TPS, 0].reshape(1, H_A)

    (da, db), (slots["win"],) = _ffn_down_bwd(dy, wd1, a1, b1, "ffn_down_bwd_ffn1", carry=scatter(dwin))
    dwd1 = _mm_tn_pair(u1, dy, "grad_wd_ffn1")
    dwg1, (slots["wd1"],) = _mm_tn_pair(da, h1, "grad_wg_ffn1", carry=scatter(dwd1))
    dwu1, (slots["wg1"],) = _mm_tn_pair(db, h1, "grad_wu_ffn1", carry=scatter(dwg1))
    (dx0, s2, s3), (slots["wu1"],) = _mm_nn(
        [(da, wg1), (db, wu1)], "ffn_up_bwd_ffn1", None, carry=scatter(dwu1),
        tail=_tail_pre_bwd(xs, dx1, g_pre_ffn1, scale1))
    sm1 = dict(shift=s3, scale=s2 * g_pre_ffn1, gate=0.5 * g_post_ffn1 * s1,
               g_pre=(1.0 + scale1) * s2, g_post=(0.5 * gate1) * s1)

    dmod = jnp.concatenate([sm1["shift"], sm1["scale"], sm1["gate"],
                            s3m, s2m * g_pre_mix, g_post_mix * s1m,
                            sm3["shift"], sm3["scale"], sm3["gate"]], axis=1)
    small_parts = {
        "b_ada": dmod, "g_pre_ffn1": sm1["g_pre"], "g_post_ffn1": sm1["g_post"],
        "g_pre_mix": (1.0 + scale2) * s2m, "b_in": db_in, "sinks_a": d_sinks,
        "rel_bias_b": d_rel.reshape(1, H_B * N_REL), "g_grp_a": dg_a, "g_grp_b": dg_b, "b_out": db_out,
        "g_post_mix": gate2 * s1m, "g_pre_ffn2": sm3["g_pre"], "g_post_ffn2": sm3["g_post"]}
    sizes = [small_parts[n].shape[1] for n in SMALL]

    def pack(parts):
        cells = []
        for p in parts:
            cells.append(p)
            if p.shape[1] % LANES:
                cells.append(jnp.zeros((1, -p.shape[1] % LANES), F32))
        return jnp.concatenate(cells, axis=1)

    packed = pack([small_parts[n] for n in SMALL] + [loss_part])
    n_packed = packed.shape[1]
    gathered = _all_gather_small(packed)
    small_sum = _sum_rows8(gathered)
    loss = small_sum[0, n_packed - LANES]
    dmod_cols = lax.dynamic_slice(gathered.reshape(N_DEV, n_packed), (0, me * ada_cols), (N_DEV, ada_cols))
    g_ada = _ada_weight_grad(sc_all.reshape(N_DEV, d_model).T, dmod_cols)

    out_g, out_d, out_m, out_v = {}, {}, {}, {}
    d_, m_, v_ = _adamw(w_ada[0], g_ada, m_w_ada[0], v_w_ada[0], "adamw_w_ada")
    out_g["w_ada"], out_d["w_ada"], out_m["w_ada"], out_v["w_ada"] = g_ada[None], d_[None], m_[None], v_[None]
    for n, key, transposed in (("w_gate1", "wg1", True), ("w_up1", "wu1", True), ("w_down1", "wd1", False),
                               ("w_in", "win", True), ("w_out", "wo", False), ("w_gate2", "wg2", True),
                               ("w_up2", "wu2", True), ("w_down2", "wd2", False)):
        view = (lambda t: t.T) if transposed else (lambda t: t)
        res = _adamw_from_slots(view(weights[n][0]), slots[key], view(mom_m[n][0]), view(mom_v[n][0]),
                                "adamw_" + n)
        out_g[n], out_d[n], out_m[n], out_v[n] = (view(t)[None] for t in res)

    small_out = _adamw_small(small_sum, *(pack([tree[n].reshape(1, -1) for n in SMALL])
                                          for tree in (weights, mom_m, mom_v)), sizes)
    for j, n in enumerate(SMALL):
        shape = weights[n].shape
        out_g[n], out_d[n], out_m[n], out_v[n] = (t.reshape(shape) for t in small_out[4 * j:4 * j + 4])

    return (loss, dx0[None], *[out_g[n] for n in WEIGHTS], *[out_d[n] for n in WEIGHTS],
            *[out_m[n] for n in WEIGHTS], *[out_v[n] for n in WEIGHTS])
```

```python
import numpy as np
import jax
import jax.numpy as jnp
from jax import lax
from jax.experimental import pallas as pl
from jax.experimental.pallas import tpu as pltpu

F32 = jnp.float32
BF16 = jnp.bfloat16
MESH = pl.DeviceIdType.MESH
ANY = pl.BlockSpec(memory_space=pl.ANY)
VMEM_SPEC = pl.BlockSpec(memory_space=pltpu.VMEM)
SMEM_SPEC = pl.BlockSpec(memory_space=pltpu.SMEM)

N_DEV = 8
CHUNK = 64
HEAD_DIM = 64
LANES = 128
H_A, KV_A, H_B = 8, 2, 8
BACK_A, BACK_B = 2, 8
REL_CLIP = 128
N_REL = 2 * REL_CLIP + 1
QA, KVA, QB = H_A * HEAD_DIM, KV_A * HEAD_DIM, H_B * HEAD_DIM
D_IN = QA + 2 * KVA + 3 * QB
N_MOD = 9
EPS = 1e-6
NEG_INF = -1e30
QG = 4
QROWS = QG * CHUNK
TPS = 2
SKEW = 1024
ADAM_LR, ADAM_B1, ADAM_B2, ADAM_EPS, ADAM_WD, ADAM_STEP = 0.001, 0.9, 0.999, 1e-08, 0.01, 10
VMEM_LIMIT = 56 * 2 ** 20


def _pick(n, cands):
    for c in cands:
        if n % c == 0:
            return c
    return n


def _pieces(n, width=2 * LANES):
    return [(lo, min(lo + width, n)) for lo in range(0, n, width)]


def _params(sem=None):
    return pltpu.CompilerParams(dimension_semantics=sem, vmem_limit_bytes=VMEM_LIMIT)


def _dot_nt(a, b):
    return lax.dot_general(a, b, (((1,), (1,)), ((), ())), preferred_element_type=F32)


def _dot_tn(a, b):
    return lax.dot_general(a, b, (((0,), (0,)), ((), ())), preferred_element_type=F32)


def _dot(a, b):
    return jnp.dot(a, b, preferred_element_type=F32)


def _sigmoid(a):
    return 0.5 * (jnp.tanh(0.5 * a) + 1.0)


def _mesh_pos():
    return lax.axis_index("x"), lax.axis_index("y"), lax.axis_index("c")


def _peer(x, y, c, r):
    px = 1 - x if r & 4 else x
    py = 1 - y if r & 2 else y
    pc = 1 - c if r & 1 else c
    return px, py, pc


class _Carry:
    def __init__(self, ins, out_shapes, scratch, start, finish):
        self.ins, self.out_shapes, self.scratch = list(ins), list(out_shapes), list(scratch)
        self.start, self.finish = start, finish


def _call(body, *, name, grid, in_specs, out_specs, out_shape, args, scratch=(), sem=None, carry=None):
    single = not isinstance(out_shape, (tuple, list))
    out_specs = (out_specs,) if single else tuple(out_specs)
    out_shape = (out_shape,) if single else tuple(out_shape)
    if carry is None:
        res = pl.pallas_call(body, name=name, grid=grid, in_specs=list(in_specs), out_specs=out_specs,
                             out_shape=out_shape, scratch_shapes=list(scratch), compiler_params=_params(sem))(*args)
        return res[0] if single else res
    n_in, n_out, n_s = len(in_specs), len(out_shape), len(scratch)
    ci, co = len(carry.ins), len(carry.out_shapes)

    def wrapped(*refs):
        ins, cins = refs[:n_in], refs[n_in:n_in + ci]
        outs = refs[n_in + ci:n_in + ci + n_out]
        couts = refs[n_in + ci + n_out:n_in + ci + n_out + co]
        scr = refs[n_in + ci + n_out + co:n_in + ci + n_out + co + n_s]
        cscr = refs[n_in + ci + n_out + co + n_s:]
        first, last = None, None
        for ax, n in enumerate(grid):
            f, l = pl.program_id(ax) == 0, pl.program_id(ax) == n - 1
            first = f if first is None else first & f
            last = l if last is None else last & l
        pl.when(first)(lambda: carry.start(cins, couts, cscr))
        body(*ins, *outs, *scr)
        pl.when(last)(lambda: carry.finish(cins, couts, cscr))

    res = pl.pallas_call(
        wrapped, name=name, grid=grid, in_specs=list(in_specs) + [ANY] * ci, out_specs=out_specs + (ANY,) * co,
        out_shape=out_shape + tuple(carry.out_shapes), scratch_shapes=list(scratch) + carry.scratch,
        compiler_params=_params(("arbitrary",) * len(grid)))(*args, *carry.ins)
    main = res[:n_out]
    return (main[0] if single else main), res[n_out:]


def _gather_carry(shards):
    n_w = len(shards)
    rows = [s.shape[0] for s in shards]

    def plan(ins, outs, scr):
        send_sems, recv_sems, local_sems = scr
        x, y, c = _mesh_pos()
        me, sibling = (x, y, c), (x, y, 1 - c)
        chips = [(1 - x, y), (x, 1 - y), (1 - x, 1 - y)]

        def block(w, dev):
            start = pl.multiple_of((4 * dev[0] + 2 * dev[1] + dev[2]) * rows[w], 16)
            return outs[w].at[pl.ds(start, rows[w]), :]

        def copy(w, k, dev, to, src=None):
            return pltpu.make_async_remote_copy(
                src_ref=block(w, dev) if src is None else src, dst_ref=block(w, dev),
                send_sem=send_sems.at[w, k], recv_sem=recv_sems.at[w, k], device_id=to, device_id_type=MESH)

        mine = [pltpu.make_async_copy(ins[w], block(w, me), local_sems.at[w]) for w in range(n_w)]
        first = []
        for j, chip in enumerate(chips):
            first += [copy(w, 1 + j, me, (*chip, c), src=ins[w]) for w in range(n_w)]
        first += [copy(w, 0, me, sibling, src=ins[w]) for w in range(n_w)]
        return c, me, sibling, chips, copy, mine, first

    def start(ins, outs, scr):
        _, _, _, _, _, mine, first = plan(ins, outs, scr)
        for cp in mine + first:
            cp.start()

    def finish(ins, outs, scr):
        c, me, sibling, chips, copy, mine, first = plan(ins, outs, scr)
        passed = []
        for j, chip in enumerate(chips):
            for w in range(n_w):
                copy(w, 1 + j, (*chip, c), me).wait_recv()
                cp = copy(w, 4 + j, (*chip, c), sibling)
                cp.start()
                passed.append(cp)
        for w in range(n_w):
            copy(w, 0, sibling, me).wait_recv()
        for j, chip in enumerate(chips):
            for w in range(n_w):
                copy(w, 4 + j, (*chip, 1 - c), me).wait_recv()
        for cp in first + passed:
            cp.wait_send()
        for cp in mine:
            cp.wait()

    return _Carry(
        shards, [jax.ShapeDtypeStruct((N_DEV * s.shape[0], s.shape[1]), s.dtype) for s in shards],
        [pltpu.SemaphoreType.DMA((n_w, N_DEV - 1)), pltpu.SemaphoreType.DMA((n_w, N_DEV - 1)),
         pltpu.SemaphoreType.DMA((n_w,))], start, finish)


def _scatter_carry(parts):
    n_w = len(parts)
    n_chip = N_DEV // 2
    rows = [g.shape[0] // n_chip for g in parts]

    def plan(ins, outs, scr):
        send_sems, recv_sems, local_sems = scr
        x, y, c = _mesh_pos()

        def src(w, chip_index):
            return ins[w].at[pl.ds(pl.multiple_of(chip_index * rows[w], 16), rows[w]), :]

        mine = [pltpu.make_async_copy(src(w, 2 * x + y), outs[w].at[0], local_sems.at[w]) for w in range(n_w)]
        copies = []
        for r in (3, 2, 1):
            px, py, _ = _peer(x, y, c, 2 * r)
            for w in range(n_w):
                copies.append(pltpu.make_async_remote_copy(
                    src_ref=src(w, 2 * px + py), dst_ref=outs[w].at[r], send_sem=send_sems.at[w, r - 1],
                    recv_sem=recv_sems.at[w, r - 1], device_id=(px, py, c), device_id_type=MESH))
        return mine, copies

    def start(ins, outs, scr):
        mine, copies = plan(ins, outs, scr)
        for cp in mine + copies:
            cp.start()

    def finish(ins, outs, scr):
        mine, copies = plan(ins, outs, scr)
        for cp in copies:
            cp.wait_recv()
        for cp in copies:
            cp.wait_send()
        for cp in mine:
            cp.wait()

    return _Carry(
        parts, [jax.ShapeDtypeStruct((n_chip, r, g.shape[1]), g.dtype) for r, g in zip(rows, parts)],
        [pltpu.SemaphoreType.DMA((n_w, n_chip - 1)), pltpu.SemaphoreType.DMA((n_w, n_chip - 1)),
         pltpu.SemaphoreType.DMA((n_w,))], start, finish)


def _ada_forward(c_row, w_ada, b_cols, carry):
    d = c_row.shape[1]
    wcols = w_ada.shape[1]
    ci, co = len(carry.ins), len(carry.out_shapes)

    def body(*refs):
        c_ref, w_ref, b_ref = refs[:3]
        cins = refs[3:3 + ci]
        sc_ref, mod_ref = refs[3 + ci:5 + ci]
        couts = refs[5 + ci:5 + ci + co]
        rows_ref, send_sems, recv_sems = refs[5 + ci + co:8 + ci + co]
        cscr = refs[8 + ci + co:]
        carry.start(cins, couts, cscr)
        x, y, c = _mesh_pos()
        me = 4 * x + 2 * y + c
        cv = c_ref[...]
        sc_ref[me] = cv * _sigmoid(cv)

        sends = []
        for r in range(1, N_DEV):
            px, py, pc = _peer(x, y, c, r)
            cp = pltpu.make_async_remote_copy(
                src_ref=sc_ref.at[me], dst_ref=sc_ref.at[me], send_sem=send_sems.at[0, r - 1],
                recv_sem=recv_sems.at[0, r - 1], device_id=(px, py, pc), device_id_type=MESH)
            cp.start()
            sends.append(cp)
        for r in range(1, N_DEV):
            px, py, pc = _peer(x, y, c, r)
            pid = 4 * px + 2 * py + pc
            pltpu.make_async_remote_copy(
                src_ref=sc_ref.at[pid], dst_ref=sc_ref.at[pid], send_sem=send_sems.at[0, r - 1],
                recv_sem=recv_sems.at[0, r - 1], device_id=(px, py, pc), device_id_type=MESH).wait_recv()
        for cp in sends:
            cp.wait_send()

        sc_all = jnp.concatenate([sc_ref[j] for j in range(N_DEV)], axis=0)
        rows = _dot(sc_all.astype(BF16), w_ref[...].astype(BF16)) + b_ref[...]
        for j in range(N_DEV):
            rows_ref[j] = rows[j:j + 1, :]
        mod_ref[me] = rows_ref[me]

        sends = []
        for r in range(1, N_DEV):
            px, py, pc = _peer(x, y, c, r)
            pid = 4 * px + 2 * py + pc
            cp = pltpu.make_async_remote_copy(
                src_ref=rows_ref.at[pid], dst_ref=mod_ref.at[me], send_sem=send_sems.at[1, r - 1],
                recv_sem=recv_sems.at[1, r - 1], device_id=(px, py, pc), device_id_type=MESH)
            cp.start()
            sends.append(cp)
        for r in range(1, N_DEV):
            px, py, pc = _peer(x, y, c, r)
            pid = 4 * px + 2 * py + pc
            pltpu.make_async_remote_copy(
                src_ref=rows_ref.at[pid], dst_ref=mod_ref.at[pid], send_sem=send_sems.at[1, r - 1],
                recv_sem=recv_sems.at[1, r - 1], device_id=(px, py, pc), device_id_type=MESH).wait_recv()
        for cp in sends:
            cp.wait_send()
        carry.finish(cins, couts, cscr)

    res = pl.pallas_call(
        body, name="ada_forward",
        out_shape=(jax.ShapeDtypeStruct((N_DEV, 1, d), F32), jax.ShapeDtypeStruct((N_DEV, 1, wcols), F32),
                   *carry.out_shapes),
        in_specs=[VMEM_SPEC, VMEM_SPEC, VMEM_SPEC] + [ANY] * ci, out_specs=(VMEM_SPEC, VMEM_SPEC) + (ANY,) * co,
        scratch_shapes=[pltpu.VMEM((N_DEV, 1, wcols), F32), pltpu.SemaphoreType.DMA((2, N_DEV - 1)),
                        pltpu.SemaphoreType.DMA((2, N_DEV - 1))] + carry.scratch,
        compiler_params=_params(),
    )(c_row, w_ada, b_cols, *carry.ins)
    return res[:2], res[2:]


def _all_gather_small(v):
    n = v.shape[1]

    def body(v_ref, out_ref, send_sems, recv_sems):
        x, y, c = _mesh_pos()
        me = 4 * x + 2 * y + c
        out_ref[me] = v_ref[...]
        sends = []
        for r in range(1, N_DEV):
            px, py, pc = _peer(x, y, c, r)
            cp = pltpu.make_async_remote_copy(
                src_ref=v_ref, dst_ref=out_ref.at[me], send_sem=send_sems.at[r - 1],
                recv_sem=recv_sems.at[r - 1], device_id=(px, py, pc), device_id_type=MESH)
            cp.start()
            sends.append(cp)
        for r in range(1, N_DEV):
            px, py, pc = _peer(x, y, c, r)
            pid = 4 * px + 2 * py + pc
            pltpu.make_async_remote_copy(
                src_ref=v_ref, dst_ref=out_ref.at[pid], send_sem=send_sems.at[r - 1],
                recv_sem=recv_sems.at[r - 1], device_id=(px, py, pc), device_id_type=MESH).wait_recv()
        for cp in sends:
            cp.wait_send()

    return pl.pallas_call(
        body, name="all_gather_small",
        out_shape=jax.ShapeDtypeStruct((N_DEV, 1, n), F32),
        in_specs=[VMEM_SPEC], out_specs=VMEM_SPEC,
        scratch_shapes=[pltpu.SemaphoreType.DMA((N_DEV - 1,)), pltpu.SemaphoreType.DMA((N_DEV - 1,))],
        compiler_params=_params(),
    )(v)


def _mm_nt(a, b, name, out_dtype, bias=None, carry=None):
    m, k = a.shape
    n = b.shape[0]
    tm = _pick(m, (512, 256, 128))
    tn = _pick(n, (1408, 1152, 1024, 768, 512, 256, 128))

    def body(*refs):
        acc = _dot_nt(refs[0][...], refs[1][...])
        if bias is not None:
            acc = acc + refs[2][...]
        refs[-1][...] = acc.astype(out_dtype)

    in_specs = [pl.BlockSpec((tm, k), lambda j, i: (i, 0)), pl.BlockSpec((tn, k), lambda j, i: (j, 0))]
    args = [a, b]
    if bias is not None:
        in_specs.append(pl.BlockSpec((1, tn), lambda j, i: (0, j)))
        args.append(bias)
    return _call(body, name=name, grid=(n // tn, m // tm), in_specs=in_specs,
                 out_specs=pl.BlockSpec((tm, tn), lambda j, i: (i, j)),
                 out_shape=jax.ShapeDtypeStruct((m, n), out_dtype), args=args,
                 sem=("parallel", "parallel"), carry=carry)


class _Tail:
    def __init__(self, rows, vecs, outs, fn):
        self.rows, self.vecs, self.outs, self.fn = list(rows), list(vecs), list(outs), fn


def _mm_nn(pairs, name, out_dtype, bias=None, carry=None, tail=None):
    m, k = pairs[0][0].shape
    n = pairs[0][1].shape[1]
    n_p = len(pairs)
    tm = _pick(m, (512, 256, 128))
    tk = k if n_p == 1 else _pick(k, (1408, 1152, 1024, 768, 512, 256, 128))
    nk = k // tk
    n_b = 0 if bias is None else 1
    n_r, n_v = (len(tail.rows), len(tail.vecs)) if tail else (0, 0)
    n_in = 2 * n_p + n_b + n_r + n_v
    n_main = 0 if out_dtype is None else 1

    def finish(acc, refs, first_tile):
        if bias is not None:
            acc = acc + refs[2 * n_p][...]
        outs = refs[n_in:-1]
        if n_main:
            outs[0][...] = acc.astype(out_dtype)
        if tail is None:
            return
        rows = [r[...] for r in refs[2 * n_p + n_b:2 * n_p + n_b + n_r]]
        vecs = [v[...] for v in refs[2 * n_p + n_b + n_r:n_in]]
        vals = tail.fn(acc, rows, vecs)
        for ref, val, (dtype, kind) in zip(outs[n_main:], vals, tail.outs):
            if kind == "row":
                ref[...] = val.astype(dtype)
            else:
                @pl.when(first_tile)
                def _(ref=ref):
                    ref[...] = jnp.zeros_like(ref)

                ref[...] += val

    def body(*refs):
        acc_ref = refs[-1]
        kk, i = pl.program_id(0), pl.program_id(1)
        part = _dot(refs[0][...], refs[1][...])
        for p in range(1, n_p):
            part = part + _dot(refs[2 * p][...], refs[2 * p + 1][...])
        if nk == 1:
            finish(part, refs, i == 0)
            return
        rows = pl.ds(pl.multiple_of(i * tm, tm), tm)

        @pl.when(kk == 0)
        def _():
            acc_ref[rows, :] = part

        if nk > 2:
            @pl.when((kk > 0) & (kk < nk - 1))
            def _():
                acc_ref[rows, :] += part

        @pl.when(kk == nk - 1)
        def _():
            finish(acc_ref[rows, :] + part, refs, i == 0)

    def last_only(kk, i):
        return (jnp.where(kk == nk - 1, i, 0), 0)

    row_spec = pl.BlockSpec((tm, n), last_only)
    vec_spec = pl.BlockSpec((1, n), lambda kk, i: (0, 0))
    in_specs, args = [], []
    for a, b in pairs:
        in_specs += [pl.BlockSpec((tm, tk), lambda kk, i: (i, kk)), pl.BlockSpec((tk, n), lambda kk, i: (kk, 0))]
        args += [a, b]
    if bias is not None:
        in_specs.append(vec_spec)
        args.append(bias)
    out_specs = [row_spec] * n_main
    out_shape = [jax.ShapeDtypeStruct((m, n), out_dtype)] if n_main else []
    if tail:
        in_specs += [row_spec] * n_r + [vec_spec] * n_v
        args += tail.rows + tail.vecs
        for dtype, kind in tail.outs:
            if kind == "row":
                out_specs.append(row_spec)
                out_shape.append(jax.ShapeDtypeStruct((m, n), dtype))
            else:
                width = n if kind == "sum" else 1
                out_specs.append(pl.BlockSpec((1, width), lambda kk, i: (0, 0)))
                out_shape.append(jax.ShapeDtypeStruct((1, width), dtype))
    if tail is None:
        out_specs, out_shape = out_specs[0], out_shape[0]
    return _call(body, name=name, grid=(nk, m // tm), in_specs=in_specs, out_specs=out_specs,
                 out_shape=out_shape, args=args,
                 scratch=[pltpu.VMEM((m, n) if nk > 1 else (8, LANES), F32)],
                 sem=("arbitrary", "arbitrary"), carry=carry)


def _rms(v):
    return lax.rsqrt(jnp.mean(v * v, axis=-1, keepdims=True) + EPS)


def _col(v):
    return jnp.sum(v, axis=0, keepdims=True)


def _tail_post_pre(x, g_post, gate, weight, g_pre, scale, shift):
    def fn(y, rows, vecs):
        (xv,), (gp, gt, g, sc, sh) = rows, vecs
        xo = xv + (weight * gt) * ((y * _rms(y)) * gp)
        return xo, ((xo * _rms(xo)) * g) * (1.0 + sc) + sh

    return _Tail([x], [g_post, gate, g_pre, scale, shift], [(F32, "row"), (BF16, "row")], fn)


def _tail_post_loss(x, target, g, gate, weight):
    def fn(y, rows, vecs):
        (xv, tv), (gv, gt) = rows, vecs
        r = _rms(y)
        yn = y * r
        err = (xv + (weight * gt) * (yn * gv)) - tv
        do = err * (1.0 / y.shape[1])
        dyn = do * ((weight * gt) * gv)
        dy = r * (dyn - yn * jnp.mean(dyn * yn, axis=-1, keepdims=True))
        return do, dy, 0.5 * _col(jnp.mean(err * err, axis=-1, keepdims=True)), _col(do * yn)

    return _Tail([x, target], [g, gate], [(F32, "row"), (BF16, "row"), (F32, "one"), (F32, "sum")], fn)


def _tail_pre_bwd(x, dres, g_pre, scale):
    def fn(dh, rows, vecs):
        (xv, dr), (g, sc) = rows, vecs
        r = _rms(xv)
        n = xv * r
        dn = dh * (g * (1.0 + sc))
        return dr + r * (dn - n * jnp.mean(dn * n, axis=-1, keepdims=True)), _col(dh * n), _col(dh)

    return _Tail([x, dres], [g_pre, scale], [(F32, "row"), (F32, "sum"), (F32, "sum")], fn)


def _tail_pre_post_bwd(x, dres, y, g_pre, scale, g_post, gate, weight):
    def fn(dh, rows, vecs):
        (xv, dr, yv), (g, sc, gp, gt) = rows, vecs
        r = _rms(xv)
        n = xv * r
        dn = dh * (g * (1.0 + sc))
        dx = dr + r * (dn - n * jnp.mean(dn * n, axis=-1, keepdims=True))
        ry = _rms(yv)
        yn = yv * ry
        dyn = dx * ((weight * gt) * gp)
        dy = ry * (dyn - yn * jnp.mean(dyn * yn, axis=-1, keepdims=True))
        return dx, dy, _col(dh * n), _col(dh), _col(dx * yn), _col(dy)

    return _Tail([x, dres, y], [g_pre, scale, g_post, gate],
                 [(F32, "row"), (BF16, "row")] + [(F32, "sum")] * 4, fn)


def _mm_tn(a, b, name, out_dtype=BF16, carry=None):
    k, m = a.shape
    n = b.shape[1]
    tm = _pick(m, (1408, 1152, 1024, 768, 512, 256, 128))
    tk = _pick(k, (512, 256, 128))
    nk = k // tk

    def body(a_ref, b_ref, o_ref, acc_ref):
        kk = pl.program_id(1)

        @pl.when(kk == 0)
        def _():
            acc_ref[...] = jnp.zeros_like(acc_ref)

        acc_ref[...] += _dot_tn(a_ref[...], b_ref[...])

        @pl.when(kk == nk - 1)
        def _():
            o_ref[...] = acc_ref[...].astype(out_dtype)

    return _call(body, name=name, grid=(m // tm, nk),
                 in_specs=[pl.BlockSpec((tk, tm), lambda i, kk: (kk, i)), pl.BlockSpec((tk, n), lambda i, kk: (kk, 0))],
                 out_specs=pl.BlockSpec((tm, n), lambda i, kk: (i, 0)),
                 out_shape=jax.ShapeDtypeStruct((m, n), out_dtype), args=[a, b],
                 scratch=[pltpu.VMEM((tm, n), F32)], sem=("parallel", "arbitrary"), carry=carry)


def _mm_tn_pair(a, b, name, carry=None):
    k, m = a.shape
    n = b.shape[1]
    rows = m // N_DEV
    n_chip = N_DEV // 2
    tm = 4 * rows
    tk = _pick(k, (1024, 512, 256, 128))
    nk = k // tk

    def body(a_ref, b_ref, p_ref, acc_ref, keep_ref, send_ref, land_ref, send_sems, recv_sems):
        i, kk = pl.program_id(0), pl.program_id(1)
        x, y, c = _mesh_pos()

        def push(chip):
            return pltpu.make_async_remote_copy(
                src_ref=send_ref.at[chip], dst_ref=land_ref.at[chip], send_sem=send_sems.at[chip],
                recv_sem=recv_sems.at[chip], device_id=(x, y, 1 - c), device_id_type=MESH)

        if nk == 1:
            acc = _dot_tn(a_ref[...], b_ref[...])
        else:
            @pl.when(kk == 0)
            def _():
                acc_ref[...] = jnp.zeros_like(acc_ref)

            acc_ref[...] += _dot_tn(a_ref[...], b_ref[...])
            acc = acc_ref

        for t in range(2):
            @pl.when((kk == nk - 1) & (i == t))
            def _(t=t):
                for ob in range(4):
                    chip, core = 2 * t + ob // 2, ob % 2
                    blk = acc[ob * rows:(ob + 1) * rows, :]

                    @pl.when(c == core)
                    def _(chip=chip, blk=blk):
                        keep_ref[chip] = blk

                    @pl.when(c != core)
                    def _(chip=chip, blk=blk):
                        send_ref[chip] = blk.astype(BF16)
                        push(chip).start()

        @pl.when((kk == nk - 1) & (i == 1))
        def _():
            for chip in range(n_chip):
                push(chip).wait_recv()
                p_ref[chip * rows:(chip + 1) * rows, :] = (
                    keep_ref[chip] + land_ref[chip].astype(F32)).astype(BF16)
            for chip in range(n_chip):
                push(chip).wait_send()

    return _call(body, name=name, grid=(2, nk),
                 in_specs=[pl.BlockSpec((tk, tm), lambda i, kk: (kk, i)), pl.BlockSpec((tk, n), lambda i, kk: (kk, 0))],
                 out_specs=pl.BlockSpec((n_chip * rows, n), lambda i, kk: (0, 0)),
                 out_shape=jax.ShapeDtypeStruct((n_chip * rows, n), BF16), args=[a, b],
                 scratch=[pltpu.VMEM((tm, n) if nk > 1 else (8, LANES), F32), pltpu.VMEM((n_chip, rows, n), F32),
                          pltpu.VMEM((n_chip, rows, n), BF16), pltpu.VMEM((n_chip, rows, n), BF16),
                          pltpu.SemaphoreType.DMA((n_chip,)), pltpu.SemaphoreType.DMA((n_chip,))],
                 sem=("arbitrary", "arbitrary"), carry=carry)


def _ffn_up(h, wg_t, wu_t, name, carry=None):
    s, d = h.shape
    f = wg_t.shape[0]
    tm = _pick(s, (512, 256, 128))
    tf = _pick(f, (1408, 1024, 512, 256, 128))

    def body(h_ref, wg_ref, wu_ref, a_ref, b_ref, u_ref):
        hh = h_ref[...]
        for lo, hi in _pieces(tf):
            a = _dot_nt(hh, wg_ref[lo:hi, :])
            b = _dot_nt(hh, wu_ref[lo:hi, :])
            a_ref[:, lo:hi] = a.astype(BF16)
            b_ref[:, lo:hi] = b.astype(BF16)
            u_ref[:, lo:hi] = ((a * _sigmoid(a)) * b).astype(BF16)

    w_spec = pl.BlockSpec((tf, d), lambda j, i: (j, 0))
    o_spec = pl.BlockSpec((tm, tf), lambda j, i: (i, j))
    o_shape = jax.ShapeDtypeStruct((s, f), BF16)
    return _call(body, name=name, grid=(f // tf, s // tm),
                 in_specs=[pl.BlockSpec((tm, d), lambda j, i: (i, 0)), w_spec, w_spec],
                 out_specs=(o_spec, o_spec, o_spec), out_shape=(o_shape, o_shape, o_shape),
                 args=[h, wg_t, wu_t], sem=("parallel", "parallel"), carry=carry)


def _ffn_down_bwd(dy, wd, a, b, name, carry=None):
    s, d = dy.shape
    f = wd.shape[0]
    tm = _pick(s, (512, 256, 128))
    tf = _pick(f, (1408, 1024, 512, 256, 128))

    def body(dy_ref, wd_ref, a_ref, b_ref, da_ref, db_ref):
        dyv = dy_ref[...]
        for lo, hi in _pieces(tf):
            du = _dot_nt(dyv, wd_ref[lo:hi, :])
            a = a_ref[:, lo:hi].astype(F32)
            b = b_ref[:, lo:hi].astype(F32)
            sig = _sigmoid(a)
            da_ref[:, lo:hi] = (du * b * (sig * (1.0 + a * (1.0 - sig)))).astype(BF16)
            db_ref[:, lo:hi] = (du * (a * sig)).astype(BF16)

    t_spec = pl.BlockSpec((tm, tf), lambda j, i: (i, j))
    o_shape = jax.ShapeDtypeStruct((s, f), BF16)
    return _call(body, name=name, grid=(f // tf, s // tm),
                 in_specs=[pl.BlockSpec((tm, d), lambda j, i: (i, 0)), pl.BlockSpec((tf, d), lambda j, i: (j, 0)),
                           t_spec, t_spec],
                 out_specs=(t_spec, t_spec), out_shape=(o_shape, o_shape), args=[dy, wd, a, b],
                 sem=("parallel", "parallel"), carry=carry)


def _row_tile(s):
    return _pick(s, (256, 128, 64))


def _vec_spec(d):
    return pl.BlockSpec((1, d), lambda i: (0, 0))


def _pre_norm(x, g, scale, shift, name):
    s, d = x.shape
    ts = _row_tile(s)

    def body(x_ref, g_ref, sc_ref, sh_ref, h_ref):
        xv = x_ref[...]
        r = lax.rsqrt(jnp.mean(xv * xv, axis=-1, keepdims=True) + EPS)
        h_ref[...] = (((xv * r) * g_ref[...]) * (1.0 + sc_ref[...]) + sh_ref[...]).astype(BF16)

    row = pl.BlockSpec((ts, d), lambda i: (i, 0))
    return _call(body, name=name, grid=(s // ts,), in_specs=[row, _vec_spec(d), _vec_spec(d), _vec_spec(d)],
                 out_specs=row, out_shape=jax.ShapeDtypeStruct((s, d), BF16), args=[x, g, scale, shift],
                 sem=("parallel",))


def _post_norm_residual(x, y, g, gate, weight, name):
    s, d = x.shape
    ts = _row_tile(s)

    def body(x_ref, y_ref, g_ref, gate_ref, o_ref):
        yv = y_ref[...]
        r = lax.rsqrt(jnp.mean(yv * yv, axis=-1, keepdims=True) + EPS)
        o_ref[...] = x_ref[...] + (weight * gate_ref[...]) * ((yv * r) * g_ref[...])

    row = pl.BlockSpec((ts, d), lambda i: (i, 0))
    return _call(body, name=name, grid=(s // ts,), in_specs=[row, row, _vec_spec(d), _vec_spec(d)],
                 out_specs=row, out_shape=jax.ShapeDtypeStruct((s, d), F32), args=[x, y, g, gate],
                 sem=("parallel",))


def _post_norm_bwd(dout, y, g, gate, weight, name):
    s, d = y.shape
    ts = _row_tile(s)

    def body(do_ref, y_ref, g_ref, gate_ref, dy_ref, s1_ref, cs_ref):
        @pl.when(pl.program_id(0) == 0)
        def _():
            s1_ref[...] = jnp.zeros_like(s1_ref)
            cs_ref[...] = jnp.zeros_like(cs_ref)

        yv = y_ref[...]
        do = do_ref[...]
        r = lax.rsqrt(jnp.mean(yv * yv, axis=-1, keepdims=True) + EPS)
        yn = yv * r
        dyn = do * ((weight * gate_ref[...]) * g_ref[...])
        dy = r * (dyn - yn * jnp.mean(dyn * yn, axis=-1, keepdims=True))
        dy_ref[...] = dy.astype(BF16)
        s1_ref[...] += jnp.sum(do * yn, axis=0, keepdims=True)
        cs_ref[...] += jnp.sum(dy, axis=0, keepdims=True)

    row = pl.BlockSpec((ts, d), lambda i: (i, 0))
    vec = jax.ShapeDtypeStruct((1, d), F32)
    return _call(body, name=name, grid=(s // ts,), in_specs=[row, row, _vec_spec(d), _vec_spec(d)],
                 out_specs=(row, _vec_spec(d), _vec_spec(d)),
                 out_shape=(jax.ShapeDtypeStruct((s, d), BF16), vec, vec), args=[dout, y, g, gate],
                 sem=("arbitrary",))


def _pre_norm_bwd(dh, x, g, scale, dres, name):
    s, d = x.shape
    ts = _row_tile(s)

    def body(dh_ref, x_ref, g_ref, sc_ref, dr_ref, dx_ref, s2_ref, s3_ref):
        @pl.when(pl.program_id(0) == 0)
        def _():
            s2_ref[...] = jnp.zeros_like(s2_ref)
            s3_ref[...] = jnp.zeros_like(s3_ref)

        xv = x_ref[...]
        dh = dh_ref[...]
        r = lax.rsqrt(jnp.mean(xv * xv, axis=-1, keepdims=True) + EPS)
        n = xv * r
        dn = dh * (g_ref[...] * (1.0 + sc_ref[...]))
        dx_ref[...] = dr_ref[...] + r * (dn - n * jnp.mean(dn * n, axis=-1, keepdims=True))
        s2_ref[...] += jnp.sum(dh * n, axis=0, keepdims=True)
        s3_ref[...] += jnp.sum(dh, axis=0, keepdims=True)

    row = pl.BlockSpec((ts, d), lambda i: (i, 0))
    vec = jax.ShapeDtypeStruct((1, d), F32)
    return _call(body, name=name, grid=(s // ts,), in_specs=[row, row, _vec_spec(d), _vec_spec(d), row],
                 out_specs=(row, _vec_spec(d), _vec_spec(d)),
                 out_shape=(jax.ShapeDtypeStruct((s, d), F32), vec, vec), args=[dh, x, g, scale, dres],
                 sem=("arbitrary",))


def _post_pre_norm(x, y, g_post, gate, weight, g_pre, scale, shift, name):
    s, d = x.shape
    ts = _row_tile(s)

    def body(x_ref, y_ref, gp_ref, gate_ref, g_ref, sc_ref, sh_ref, o_ref, h_ref):
        yv = y_ref[...]
        r = lax.rsqrt(jnp.mean(yv * yv, axis=-1, keepdims=True) + EPS)
        xv = x_ref[...] + (weight * gate_ref[...]) * ((yv * r) * gp_ref[...])
        o_ref[...] = xv
        r2 = lax.rsqrt(jnp.mean(xv * xv, axis=-1, keepdims=True) + EPS)
        h_ref[...] = (((xv * r2) * g_ref[...]) * (1.0 + sc_ref[...]) + sh_ref[...]).astype(BF16)

    row = pl.BlockSpec((ts, d), lambda i: (i, 0))
    return _call(body, name=name, grid=(s // ts,), in_specs=[row, row] + [_vec_spec(d)] * 5,
                 out_specs=(row, row),
                 out_shape=(jax.ShapeDtypeStruct((s, d), F32), jax.ShapeDtypeStruct((s, d), BF16)),
                 args=[x, y, g_post, gate, g_pre, scale, shift], sem=("parallel",))


def _post_norm_loss_bwd(x, y, g, gate, weight, target, name):
    s, d = y.shape
    ts = _row_tile(s)

    def body(x_ref, y_ref, g_ref, gate_ref, t_ref, dx_ref, dy_ref, l_ref, s1_ref):
        @pl.when(pl.program_id(0) == 0)
        def _():
            l_ref[...] = jnp.zeros_like(l_ref)
            s1_ref[...] = jnp.zeros_like(s1_ref)

        yv = y_ref[...]
        r = lax.rsqrt(jnp.mean(yv * yv, axis=-1, keepdims=True) + EPS)
        yn = yv * r
        err = (x_ref[...] + (weight * gate_ref[...]) * (yn * g_ref[...])) - t_ref[...]
        do = err * (1.0 / d)
        dx_ref[...] = do
        l_ref[...] += 0.5 * jnp.sum(jnp.mean(err * err, axis=-1, keepdims=True), axis=0, keepdims=True)
        dyn = do * ((weight * gate_ref[...]) * g_ref[...])
        dy_ref[...] = (r * (dyn - yn * jnp.mean(dyn * yn, axis=-1, keepdims=True))).astype(BF16)
        s1_ref[...] += jnp.sum(do * yn, axis=0, keepdims=True)

    row = pl.BlockSpec((ts, d), lambda i: (i, 0))
    return _call(body, name=name, grid=(s // ts,), in_specs=[row, row, _vec_spec(d), _vec_spec(d), row],
                 out_specs=(row, row, pl.BlockSpec((1, 1), lambda i: (0, 0)), _vec_spec(d)),
                 out_shape=(jax.ShapeDtypeStruct((s, d), F32), jax.ShapeDtypeStruct((s, d), BF16),
                            jax.ShapeDtypeStruct((1, 1), F32), jax.ShapeDtypeStruct((1, d), F32)),
                 args=[x, y, g, gate, target], sem=("arbitrary",))


def _pre_post_norm_bwd(dh, x, g_pre, scale, dres, y, g_post, gate, weight, name):
    s, d = x.shape
    ts = _row_tile(s)

    def body(dh_ref, x_ref, g_ref, sc_ref, dr_ref, y_ref, gp_ref, gate_ref,
             dx_ref, dy_ref, s2_ref, s3_ref, s1_ref, cs_ref):
        @pl.when(pl.program_id(0) == 0)
        def _():
            for ref in (s2_ref, s3_ref, s1_ref, cs_ref):
                ref[...] = jnp.zeros_like(ref)

        xv = x_ref[...]
        dh = dh_ref[...]
        r = lax.rsqrt(jnp.mean(xv * xv, axis=-1, keepdims=True) + EPS)
        n = xv * r
        dn = dh * (g_ref[...] * (1.0 + sc_ref[...]))
        dx = dr_ref[...] + r * (dn - n * jnp.mean(dn * n, axis=-1, keepdims=True))
        dx_ref[...] = dx
        s2_ref[...] += jnp.sum(dh * n, axis=0, keepdims=True)
        s3_ref[...] += jnp.sum(dh, axis=0, keepdims=True)
        yv = y_ref[...]
        ry = lax.rsqrt(jnp.mean(yv * yv, axis=-1, keepdims=True) + EPS)
        yn = yv * ry
        dyn = dx * ((weight * gate_ref[...]) * gp_ref[...])
        dy = ry * (dyn - yn * jnp.mean(dyn * yn, axis=-1, keepdims=True))
        dy_ref[...] = dy.astype(BF16)
        s1_ref[...] += jnp.sum(dx * yn, axis=0, keepdims=True)
        cs_ref[...] += jnp.sum(dy, axis=0, keepdims=True)

    row = pl.BlockSpec((ts, d), lambda i: (i, 0))
    vec = jax.ShapeDtypeStruct((1, d), F32)
    return _call(body, name=name, grid=(s // ts,),
                 in_specs=[row, row, _vec_spec(d), _vec_spec(d), row, row, _vec_spec(d), _vec_spec(d)],
                 out_specs=(row, row) + (_vec_spec(d),) * 4,
                 out_shape=(jax.ShapeDtypeStruct((s, d), F32), jax.ShapeDtypeStruct((s, d), BF16), vec, vec, vec, vec),
                 args=[dh, x, g_pre, scale, dres, y, g_post, gate], sem=("arbitrary",))


def _group_norm_cat(oa, ob, ga, gb):
    s = oa.shape[0]
    ts = _row_tile(s)

    def body(oa_ref, ob_ref, ga_ref, gb_ref, y_ref):
        for o_ref, g_ref, lo, w in ((oa_ref, ga_ref, 0, QA), (ob_ref, gb_ref, QA, QB)):
            ov = o_ref[...]
            r = lax.rsqrt(jnp.mean(ov * ov, axis=-1, keepdims=True) + EPS)
            y_ref[:, lo:lo + w] = ((ov * r) * g_ref[...]).astype(BF16)

    return _call(body, name="group_norm_cat", grid=(s // ts,),
                 in_specs=[pl.BlockSpec((ts, QA), lambda i: (i, 0)), pl.BlockSpec((ts, QB), lambda i: (i, 0)),
                           _vec_spec(QA), _vec_spec(QB)],
                 out_specs=pl.BlockSpec((ts, QA + QB), lambda i: (i, 0)),
                 out_shape=jax.ShapeDtypeStruct((s, QA + QB), BF16), args=[oa, ob, ga, gb], sem=("parallel",))


def _group_norm_bwd(dy, oa, ob, ga, gb):
    s = oa.shape[0]
    ts = _row_tile(s)

    def body(dy_ref, oa_ref, ob_ref, ga_ref, gb_ref, doa_ref, dob_ref, dga_ref, dgb_ref):
        @pl.when(pl.program_id(0) == 0)
        def _():
            dga_ref[...] = jnp.zeros_like(dga_ref)
            dgb_ref[...] = jnp.zeros_like(dgb_ref)

        for o_ref, g_ref, do_ref, dg_ref, lo, w in ((oa_ref, ga_ref, doa_ref, dga_ref, 0, QA),
                                                    (ob_ref, gb_ref, dob_ref, dgb_ref, QA, QB)):
            ov = o_ref[...]
            dyv = dy_ref[:, lo:lo + w]
            r = lax.rsqrt(jnp.mean(ov * ov, axis=-1, keepdims=True) + EPS)
            n = ov * r
            dn = dyv * g_ref[...]
            do_ref[...] = r * (dn - n * jnp.mean(dn * n, axis=-1, keepdims=True))
            dg_ref[...] += jnp.sum(dyv * n, axis=0, keepdims=True)

    ra = pl.BlockSpec((ts, QA), lambda i: (i, 0))
    rb = pl.BlockSpec((ts, QB), lambda i: (i, 0))
    return _call(body, name="group_norm_bwd", grid=(s // ts,),
                 in_specs=[pl.BlockSpec((ts, QA + QB), lambda i: (i, 0)), ra, rb, _vec_spec(QA), _vec_spec(QB)],
                 out_specs=(ra, rb, _vec_spec(QA), _vec_spec(QB)),
                 out_shape=(jax.ShapeDtypeStruct((s, QA), F32), jax.ShapeDtypeStruct((s, QB), F32),
                            jax.ShapeDtypeStruct((1, QA), F32), jax.ShapeDtypeStruct((1, QB), F32)),
                 args=[dy, oa, ob, ga, gb], sem=("arbitrary",))


def _loss_and_grad(y, target):
    s, d = y.shape
    ts = _row_tile(s)

    def body(y_ref, t_ref, l_ref, g_ref):
        @pl.when(pl.program_id(0) == 0)
        def _():
            l_ref[...] = jnp.zeros_like(l_ref)

        err = y_ref[...] - t_ref[...]
        g_ref[...] = err * (1.0 / d)
        row = jnp.mean(err * err, axis=-1, keepdims=True)
        l_ref[...] += 0.5 * jnp.sum(row, axis=0, keepdims=True)

    row = pl.BlockSpec((ts, d), lambda i: (i, 0))
    return _call(body, name="loss_and_grad", grid=(s // ts,), in_specs=[row, row],
                 out_specs=(pl.BlockSpec((1, 1), lambda i: (0, 0)), row),
                 out_shape=(jax.ShapeDtypeStruct((1, 1), F32), jax.ShapeDtypeStruct((s, d), F32)),
                 args=[y, target], sem=("arbitrary",))


def _col_sum(x, name):
    s, n = x.shape
    ts = _row_tile(s)

    def body(x_ref, o_ref):
        @pl.when(pl.program_id(0) == 0)
        def _():
            o_ref[...] = jnp.zeros_like(o_ref)

        o_ref[...] += jnp.sum(x_ref[...].astype(F32), axis=0, keepdims=True)

    return _call(body, name=name, grid=(s // ts,), in_specs=[pl.BlockSpec((ts, n), lambda i: (i, 0))],
                 out_specs=pl.BlockSpec((1, n), lambda i: (0, 0)), out_shape=jax.ShapeDtypeStruct((1, n), F32),
                 args=[x], sem=("arbitrary",))


def _n_variants(n_back):
    return -(-n_back // QG) + 1


def _alibi_bias():
    i = np.arange(QROWS)[:, None]
    j = np.arange((QG + BACK_A) * CHUNK)[None, :]
    dist = np.abs(BACK_A * CHUNK + i - j).astype(np.float32)
    dc = j // CHUNK - i // CHUNK
    valid = (dc >= 0) & (dc <= BACK_A)
    slopes = np.array([2.0 ** (-8.0 * (h + 1) / H_A) for h in range(H_A)], dtype=np.float32)
    bias = -slopes[:, None, None] * dist[None]
    out = [np.where((valid & (j >= (BACK_A - QG * v) * CHUNK))[None], bias, np.float32(NEG_INF))
           for v in range(_n_variants(BACK_A))]
    return jnp.asarray(np.stack(out).astype(np.float32))


def _rel_index_matrix():
    cc = np.arange(SKEW)
    dist = np.where(cc < SKEW - QROWS, BACK_B * CHUNK - cc, BACK_B * CHUNK + SKEW - cc)
    idx = np.clip(dist, -REL_CLIP, REL_CLIP) + REL_CLIP
    m = np.zeros((SKEW, N_REL), np.float32)
    m[cc, idx] = 1.0
    return jnp.asarray(m)


def _toeplitz_bias(vec, carry=None):
    lk = (QG + BACK_B) * CHUNK
    nv = _n_variants(BACK_B)

    def body(v_ref, o_ref):
        xv = jnp.broadcast_to(v_ref[0], (QROWS, SKEW))
        row = lax.broadcasted_iota(jnp.int32, (QROWS, SKEW), 0)
        for bit in range(QROWS.bit_length() - 1):
            xv = jnp.where((row >> bit) & 1 == 1, pltpu.roll(xv, 1 << bit, 1), xv)
        ri = lax.broadcasted_iota(jnp.int32, (QROWS, lk), 0) // CHUNK
        col = lax.broadcasted_iota(jnp.int32, (QROWS, lk), 1)
        ci = col // CHUNK
        valid = (ci - ri >= 0) & (ci - ri <= BACK_B)
        for v in range(nv):
            o_ref[v, 0] = jnp.where(valid & (col >= (BACK_B - QG * v) * CHUNK), xv[:, :lk], NEG_INF)

    return _call(body, name="toeplitz_bias", grid=(H_B,),
                 in_specs=[pl.BlockSpec((1, 1, SKEW), lambda h: (h, 0, 0))],
                 out_specs=pl.BlockSpec((nv, 1, QROWS, lk), lambda h: (0, h, 0, 0)),
                 out_shape=jax.ShapeDtypeStruct((nv, H_B, QROWS, lk), F32), args=[vec], sem=("parallel",),
                 carry=carry)


def _diagonal_sums(dbias):
    lk = dbias.shape[2]

    def body(d_ref, o_ref):
        xp = jnp.concatenate([d_ref[0], jnp.zeros((QROWS, SKEW - lk), F32)], axis=1)
        xv = xp[0:CHUNK]
        for q in range(1, QG):
            xv = xv + pltpu.roll(xp[q * CHUNK:(q + 1) * CHUNK], SKEW - q * CHUNK, 1)
        row = lax.broadcasted_iota(jnp.int32, (CHUNK, SKEW), 0)
        for bit in range(CHUNK.bit_length() - 1):
            xv = jnp.where((row >> bit) & 1 == 1, pltpu.roll(xv, SKEW - (1 << bit), 1), xv)
        o_ref[0] = jnp.sum(xv, axis=0, keepdims=True)

    return _call(body, name="diagonal_sums", grid=(H_B,),
                 in_specs=[pl.BlockSpec((1, QROWS, lk), lambda h: (h, 0, 0))],
                 out_specs=pl.BlockSpec((1, 1, SKEW), lambda h: (h, 0, 0)),
                 out_shape=jax.ShapeDtypeStruct((H_B, 1, SKEW), F32), args=[dbias], sem=("parallel",))


def _attn_common(s, n_back, gqa, q_col, k_col, v_col):
    lk = (QG + n_back) * CHUNK
    pad = n_back * CHUNK
    wide = TPS * LANES
    q_spec = pl.BlockSpec((QROWS, wide), lambda t, g: (g, q_col // TPS + t))
    if gqa:
        k_spec = pl.BlockSpec((s, LANES), lambda t, g: (0, k_col))
        v_spec = pl.BlockSpec((s, LANES), lambda t, g: (0, v_col))
    else:
        k_spec = pl.BlockSpec((s, wide), lambda t, g: (0, k_col // TPS + t))
        v_spec = pl.BlockSpec((s, wide), lambda t, g: (0, v_col // TPS + t))
    last_variant = _n_variants(n_back) - 1
    bias_spec = pl.BlockSpec((None, 2 * TPS, QROWS, lk), lambda t, g: (jnp.minimum(g, last_variant), t, 0, 0))
    tile_spec = pl.BlockSpec((QROWS, wide), lambda t, g: (g, t))
    return lk, pad, q_spec, k_spec, v_spec, bias_spec, tile_spec


def _attention_fwd(proj, bias, sinks, *, n_back, gqa, q_col, k_col, v_col, name, carry=None):
    s = proj.shape[0]
    lk, pad, q_spec, k_spec, v_spec, bias_spec, tile_spec = _attn_common(s, n_back, gqa, q_col, k_col, v_col)
    n_t, n_g = 512 // (TPS * LANES), s // QROWS
    kv_wide = LANES if gqa else TPS * LANES

    def body(*refs):
        if gqa:
            q_ref, k_ref, v_ref, bias_ref, sink_ref, o_ref, l_ref, kpad, vpad = refs
        else:
            q_ref, k_ref, v_ref, bias_ref, o_ref, l_ref, kpad, vpad = refs
        t, g = pl.program_id(0), pl.program_id(1)

        @pl.when(g == 0)
        def _():
            kpad[0:pad, :] = jnp.zeros((pad, kv_wide), BF16)
            vpad[0:pad, :] = jnp.zeros((pad, kv_wide), BF16)
            kpad[pad:, :] = k_ref[...]
            vpad[pad:, :] = v_ref[...]

        start = pl.multiple_of(g * QROWS, QROWS)
        half = lax.broadcasted_iota(jnp.int32, (QROWS, LANES), 1) // HEAD_DIM
        for tt in range(TPS):
            lanes = slice(tt * LANES, (tt + 1) * LANES)
            kv_lanes = slice(0, LANES) if gqa else lanes
            kb = kpad[pl.ds(start, lk), kv_lanes]
            vb = vpad[pl.ds(start, lk), kv_lanes]
            q = q_ref[:, lanes] * (HEAD_DIM ** -0.5)
            if gqa:
                hk = (TPS * t + tt) // 2
                q_rolled = pltpu.roll(q.astype(F32), HEAD_DIM, 1).astype(BF16)
            outs, lses = [], []
            for e in range(2):
                if gqa:
                    kv_half = hk
                    src = jnp.where(hk == e, q, q_rolled)
                else:
                    kv_half = e
                    src = q
                qm = jnp.where(half == kv_half, src, jnp.zeros_like(src))
                sc = _dot_nt(qm, kb) + bias_ref[2 * tt + e]
                m = jnp.max(sc, axis=-1, keepdims=True)
                if gqa:
                    sk = sink_ref[2 * (TPS * t + tt) + e]
                    m = jnp.maximum(m, sk)
                p = jnp.exp(sc - m)
                l = jnp.sum(p, axis=-1, keepdims=True)
                if gqa:
                    l = l + jnp.exp(sk - m)
                pn = p / l
                outs.append(_dot(pn.astype(BF16), vb))
                lses.append(m + jnp.log(l))
            if gqa:
                same = jnp.where(hk == 0, outs[0], outs[1])
                other = jnp.where(hk == 0, outs[1], outs[0])
                o_ref[:, lanes] = jnp.where(half == hk, same, pltpu.roll(other, HEAD_DIM, 1))
            else:
                o_ref[:, lanes] = jnp.where(half == 0, outs[0], outs[1])
            l_ref[:, lanes] = jnp.where(half == 0, lses[0], lses[1])

    in_specs = [q_spec, k_spec, v_spec, bias_spec] + ([SMEM_SPEC] if gqa else [])
    args = [proj, proj, proj, bias] + ([sinks] if gqa else [])
    o_shape = jax.ShapeDtypeStruct((s, 512), F32)
    return _call(body, name=name, grid=(n_t, n_g), in_specs=in_specs, out_specs=(tile_spec, tile_spec),
                 out_shape=(o_shape, o_shape), args=args,
                 scratch=[pltpu.VMEM((s + pad, kv_wide), BF16), pltpu.VMEM((s + pad, kv_wide), BF16)],
                 sem=("arbitrary", "arbitrary"), carry=carry)


def _attention_bwd(proj, bias, sinks, do, lse, *, n_back, gqa, q_col, k_col, v_col, name, carry=None):
    s = proj.shape[0]
    lk, pad, q_spec, k_spec, v_spec, bias_spec, tile_spec = _attn_common(s, n_back, gqa, q_col, k_col, v_col)
    n_t, n_g = 512 // (TPS * LANES), s // QROWS
    kv_wide = LANES if gqa else TPS * LANES

    def body(*refs):
        if gqa:
            (q_ref, k_ref, v_ref, bias_ref, sink_ref, do_ref, l_ref,
             dq_ref, dk_ref, dv_ref, dsink_ref, kpad, vpad, dkpad, dvpad) = refs
        else:
            (q_ref, k_ref, v_ref, bias_ref, do_ref, l_ref,
             dq_ref, dk_ref, dv_ref, dbias_ref, kpad, vpad, dkpad, dvpad) = refs
        t, g = pl.program_id(0), pl.program_id(1)

        @pl.when(g == 0)
        def _():
            kpad[0:pad, :] = jnp.zeros((pad, kv_wide), BF16)
            vpad[0:pad, :] = jnp.zeros((pad, kv_wide), BF16)
            kpad[pad:, :] = k_ref[...]
            vpad[pad:, :] = v_ref[...]
            if gqa:
                dsink_ref[...] = jnp.zeros_like(dsink_ref)
            else:
                dbias_ref[...] = jnp.zeros_like(dbias_ref)

        @pl.when((g == 0) & (t == 0) if gqa else g == 0)
        def _():
            dkpad[...] = jnp.zeros_like(dkpad)
            dvpad[...] = jnp.zeros_like(dvpad)

        start = pl.multiple_of(g * QROWS, QROWS)
        half = lax.broadcasted_iota(jnp.int32, (QROWS, LANES), 1) // HEAD_DIM
        for tt in range(TPS):
            lanes = slice(tt * LANES, (tt + 1) * LANES)
            kv_lanes = slice(0, LANES) if gqa else lanes
            kb = kpad[pl.ds(start, lk), kv_lanes]
            vb = vpad[pl.ds(start, lk), kv_lanes]
            q = q_ref[:, lanes]
            dov = do_ref[:, lanes]
            lv = l_ref[:, lanes]
            if gqa:
                hk = (TPS * t + tt) // 2
                q_rolled = pltpu.roll(q.astype(F32), HEAD_DIM, 1).astype(BF16)
                do_rolled = pltpu.roll(dov, HEAD_DIM, 1)
            dqs = []
            dk_acc = jnp.zeros((lk, LANES), F32)
            dv_acc = jnp.zeros((lk, LANES), F32)
            for e in range(2):
                if gqa:
                    kv_half = hk
                    src = jnp.where(hk == e, q, q_rolled)
                    do_src = jnp.where(hk == e, dov, do_rolled)
                else:
                    kv_half = e
                    src = q
                    do_src = dov
                qm = jnp.where(half == kv_half, src, jnp.zeros_like(src))
                dom = jnp.where(half == kv_half, do_src, 0.0).astype(BF16)
                lcol = jnp.max(jnp.where(half == e, lv, -jnp.inf), axis=-1, keepdims=True)
                sc = _dot_nt(qm * (HEAD_DIM ** -0.5), kb) + bias_ref[2 * tt + e]
                pn = jnp.exp(sc - lcol)
                dp = _dot_nt(dom, vb)
                delta = jnp.sum(pn * dp, axis=-1, keepdims=True)
                ds = pn * (dp - delta)
                if gqa:
                    p_sink = jnp.exp(sink_ref[2 * (TPS * t + tt) + e] - lcol)
                    dsk = -jnp.sum(p_sink * delta, axis=0, keepdims=True)
                    row = 2 * tt + e
                    dsink_ref[0, row:row + 1, :] += jnp.broadcast_to(dsk, (1, LANES))
                else:
                    dbias_ref[2 * tt + e] += ds
                dsb = (ds * (HEAD_DIM ** -0.5)).astype(BF16)
                dqs.append(_dot(dsb, kb))
                dk_acc = dk_acc + _dot_tn(dsb, qm)
                dv_acc = dv_acc + _dot_tn(pn.astype(BF16), dom)
            dkpad[pl.ds(start, lk), kv_lanes] += dk_acc
            dvpad[pl.ds(start, lk), kv_lanes] += dv_acc
            if gqa:
                same = jnp.where(hk == 0, dqs[0], dqs[1])
                other = jnp.where(hk == 0, dqs[1], dqs[0])
                dq_ref[:, lanes] = jnp.where(half == hk, same, pltpu.roll(other, HEAD_DIM, 1)).astype(BF16)
            else:
                dq_ref[:, lanes] = jnp.where(half == 0, dqs[0], dqs[1]).astype(BF16)

        @pl.when((g == n_g - 1) & (t == n_t - 1) if gqa else g == n_g - 1)
        def _():
            dk_ref[...] = dkpad[pad:, :].astype(BF16)
            dv_ref[...] = dvpad[pad:, :].astype(BF16)

    in_specs = [q_spec, k_spec, v_spec, bias_spec] + ([SMEM_SPEC] if gqa else []) + [tile_spec, tile_spec]
    args = [proj, proj, proj, bias] + ([sinks] if gqa else []) + [do, lse]
    if gqa:
        kv_out = pl.BlockSpec((s, LANES), lambda t, g: (0, 0))
        kv_shape = jax.ShapeDtypeStruct((s, LANES), BF16)
        extra_spec = pl.BlockSpec((1, 8, LANES), lambda t, g: (t, 0, 0))
        extra_shape = jax.ShapeDtypeStruct((n_t, 8, LANES), F32)
    else:
        kv_out = pl.BlockSpec((s, kv_wide), lambda t, g: (0, t))
        kv_shape = jax.ShapeDtypeStruct((s, 512), BF16)
        extra_spec = pl.BlockSpec((2 * TPS, QROWS, lk), lambda t, g: (t, 0, 0))
        extra_shape = jax.ShapeDtypeStruct(bias.shape[1:], F32)
    return _call(body, name=name, grid=(n_t, n_g), in_specs=in_specs,
                 out_specs=(tile_spec, kv_out, kv_out, extra_spec),
                 out_shape=(jax.ShapeDtypeStruct((s, 512), BF16), kv_shape, kv_shape, extra_shape), args=args,
                 scratch=[pltpu.VMEM((s + pad, kv_wide), BF16), pltpu.VMEM((s + pad, kv_wide), BF16),
                          pltpu.VMEM((s + pad, kv_wide), F32), pltpu.VMEM((s + pad, kv_wide), F32)],
                 sem=("arbitrary", "arbitrary"), carry=carry)


def _sum_slots(r, name):
    n_slots, rows, k = r.shape

    def body(r_ref, o_ref):
        acc = r_ref[0].astype(F32)
        for j in range(1, n_slots):
            acc = acc + r_ref[j].astype(F32)
        o_ref[...] = acc

    return _call(body, name=name, grid=(k // LANES,),
                 in_specs=[pl.BlockSpec((n_slots, rows, LANES), lambda i: (0, 0, i))],
                 out_specs=pl.BlockSpec((rows, LANES), lambda i: (0, i)),
                 out_shape=jax.ShapeDtypeStruct((rows, k), F32), args=[r], sem=("parallel",))


def _sum_rows8(g):
    n = g.shape[2]

    def body(g_ref, o_ref):
        acc = g_ref[0]
        for j in range(1, N_DEV):
            acc = acc + g_ref[j]
        o_ref[...] = acc

    return pl.pallas_call(
        body, name="sum_small_grads", in_specs=[VMEM_SPEC], out_specs=VMEM_SPEC,
        out_shape=jax.ShapeDtypeStruct((1, n), F32), compiler_params=_params(),
    )(g)


def _ada_weight_grad(sc_t, dmod_cols):
    d = sc_t.shape[0]
    w = dmod_cols.shape[1]
    td = _pick(d, (256, 128))

    def body(sc_ref, dm_ref, o_ref):
        scv = sc_ref[...]
        dmv = dm_ref[...]
        acc = scv[:, 0:1] * dmv[0:1, :]
        for b in range(1, N_DEV):
            acc = acc + scv[:, b:b + 1] * dmv[b:b + 1, :]
        o_ref[...] = acc

    return _call(body, name="ada_weight_grad", grid=(d // td,),
                 in_specs=[pl.BlockSpec((td, N_DEV), lambda i: (i, 0)), pl.BlockSpec((N_DEV, w), lambda i: (0, 0))],
                 out_specs=pl.BlockSpec((td, w), lambda i: (i, 0)), out_shape=jax.ShapeDtypeStruct((d, w), F32),
                 args=[sc_t, dmod_cols], sem=("parallel",))


def _adamw_update(w, gv, m, v):
    nm = ADAM_B1 * m + (1.0 - ADAM_B1) * gv
    nv = ADAM_B2 * v + (1.0 - ADAM_B2) * (gv * gv)
    m_hat = nm / (1.0 - ADAM_B1 ** ADAM_STEP)
    v_hat = nv / (1.0 - ADAM_B2 ** ADAM_STEP)
    return -ADAM_LR * (m_hat / (jnp.sqrt(v_hat) + ADAM_EPS) + ADAM_WD * w), nm, nv


def _adamw(w, g, m, v, name):
    rows, cols = w.shape
    tr = _pick(rows, (256, 176, 128, 88, 64)) if rows > 256 else rows

    def body(w_ref, g_ref, m_ref, v_ref, d_ref, nm_ref, nv_ref):
        d_ref[...], nm_ref[...], nv_ref[...] = _adamw_update(w_ref[...], g_ref[...], m_ref[...], v_ref[...])

    spec = pl.BlockSpec((tr, cols), lambda i: (i, 0))
    shape = jax.ShapeDtypeStruct((rows, cols), F32)
    return _call(body, name=name, grid=(rows // tr,), in_specs=[spec] * 4, out_specs=(spec, spec, spec),
                 out_shape=(shape, shape, shape), args=[w, g, m, v], sem=("parallel",))


def _adamw_from_slots(w, slots, m, v, name):
    n_slots, rows, k = slots.shape

    def body(s_ref, w_ref, m_ref, v_ref, g_ref, d_ref, nm_ref, nv_ref):
        gv = s_ref[0].astype(F32)
        for j in range(1, n_slots):
            gv = gv + s_ref[j].astype(F32)
        g_ref[...] = gv
        d_ref[...], nm_ref[...], nv_ref[...] = _adamw_update(w_ref[...], gv, m_ref[...], v_ref[...])

    tr = rows // 2 if rows % 32 == 0 else rows
    spec = pl.BlockSpec((tr, k), lambda i: (i, 0))
    shape = jax.ShapeDtypeStruct((rows, k), F32)
    return _call(body, name=name, grid=(rows // tr,),
                 in_specs=[pl.BlockSpec((n_slots, tr, k), lambda i: (0, i, 0)), spec, spec, spec],
                 out_specs=(spec, spec, spec, spec), out_shape=(shape, shape, shape, shape),
                 args=[slots, w, m, v], sem=("parallel",))


def _adamw_small(g, w, m, v, sizes):
    n = w.shape[1]
    offs, off = [], 0
    for size in sizes:
        offs.append(off)
        off += size + (-size % LANES)

    def body(g_ref, w_ref, m_ref, v_ref, *out_refs):
        gv = g_ref[:, 0:n]
        dv, nm, nv = _adamw_update(w_ref[...], gv, m_ref[...], v_ref[...])
        for j, (o, size) in enumerate(zip(offs, sizes)):
            for k, val in enumerate((gv, dv, nm, nv)):
                out_refs[4 * j + k][...] = val[:, o:o + size]

    shapes = [jax.ShapeDtypeStruct((1, size), F32) for size in sizes for _ in range(4)]
    return pl.pallas_call(
        body, name="adamw_small", in_specs=[VMEM_SPEC] * 4, out_specs=tuple([VMEM_SPEC] * len(shapes)),
        out_shape=tuple(shapes), compiler_params=_params(),
    )(g, w, m, v)


SMALL = ("b_ada", "g_pre_ffn1", "g_post_ffn1", "g_pre_mix", "b_in", "sinks_a", "rel_bias_b", "g_grp_a",
         "g_grp_b", "b_out", "g_post_mix", "g_pre_ffn2", "g_post_ffn2")
WEIGHTS = ("w_ada", "b_ada", "g_pre_ffn1", "w_gate1", "w_up1", "w_down1", "g_post_ffn1", "g_pre_mix", "w_in",
           "b_in", "sinks_a", "rel_bias_b", "g_grp_a", "g_grp_b", "w_out", "b_out", "g_post_mix", "g_pre_ffn2",
           "w_gate2", "w_up2", "w_down2", "g_post_ffn2")


def kernel(x, c, w_ada, b_ada, g_pre_ffn1, w_gate1, w_up1, w_down1, g_post_ffn1, g_pre_mix, w_in, b_in, sinks_a, rel_bias_b, g_grp_a, g_grp_b, w_out, b_out, g_post_mix, g_pre_ffn2, w_gate2, w_up2, w_down2, g_post_ffn2, loss_target, m_w_ada, m_b_ada, m_g_pre_ffn1, m_w_gate1, m_w_up1, m_w_down1, m_g_post_ffn1, m_g_pre_mix, m_w_in, m_b_in, m_sinks_a, m_rel_bias_b, m_g_grp_a, m_g_grp_b, m_w_out, m_b_out, m_g_post_mix, m_g_pre_ffn2, m_w_gate2, m_w_up2, m_w_down2, m_g_post_ffn2, v_w_ada, v_b_ada, v_g_pre_ffn1, v_w_gate1, v_w_up1, v_w_down1, v_g_post_ffn1, v_g_pre_mix, v_w_in, v_b_in, v_sinks_a, v_rel_bias_b, v_g_grp_a, v_g_grp_b, v_w_out, v_b_out, v_g_post_mix, v_g_pre_ffn2, v_w_gate2, v_w_up2, v_w_down2, v_g_post_ffn2):
    given = dict(locals())
    weights = {n: given[n] for n in WEIGHTS}
    mom_m = {n: given["m_" + n] for n in WEIGHTS}
    mom_v = {n: given["v_" + n] for n in WEIGHTS}

    me = 4 * lax.axis_index("x") + 2 * lax.axis_index("y") + lax.axis_index("c")
    xs = x[0]
    tgt = loss_target[0]
    d_model = xs.shape[1]
    ada_cols = w_ada.shape[2]

    sh = {"wg1": w_gate1[0].T, "wu1": w_up1[0].T, "wd1": w_down1[0], "win": w_in[0].T, "wo": w_out[0],
          "wg2": w_gate2[0].T, "wu2": w_up2[0].T, "wd2": w_down2[0]}
    sh = {k: v.astype(BF16) for k, v in sh.items()}

    def gather(*names):
        return _gather_carry([sh[n] for n in names])

    bias_a = _alibi_bias()
    rel_m = _rel_index_matrix()
    rel_vec = jnp.dot(rel_bias_b[0], rel_m.T, precision=lax.Precision.HIGHEST)
    bias_b, (wg1, wu1) = _toeplitz_bias(rel_vec.reshape(H_B, 1, SKEW), carry=gather("wg1", "wu1"))

    b_cols = lax.dynamic_slice(b_ada, (0, me * ada_cols), (1, ada_cols))
    (sc_all, mod_rows), _ = _ada_forward(c, w_ada[0], b_cols, _Carry([], [], [], lambda *a: None, lambda *a: None))
    mod = mod_rows.reshape(N_MOD, d_model)
    shift1, scale1, gate1, shift2, scale2, gate2, shift3, scale3, gate3 = (mod[i:i + 1] for i in range(N_MOD))

    h1 = _pre_norm(xs, g_pre_ffn1, scale1, shift1, "pre_norm_ffn1")
    (a1, b1, u1), (wd1,) = _ffn_up(h1, wg1, wu1, "ffn_up_ffn1", carry=gather("wd1"))
    (y1, x1, h2), (win,) = _mm_nn(
        [(u1, wd1)], "ffn_down_ffn1", F32, carry=gather("win"),
        tail=_tail_post_pre(xs, g_post_ffn1, gate1, 0.5, g_pre_mix, scale2, shift2))

    proj, (wo,) = _mm_nt(h2, win, "in_proj", BF16, bias=b_in, carry=gather("wo"))
    sinks = sinks_a[0]
    cfg_a = dict(n_back=BACK_A, gqa=True, q_col=0, k_col=QA // LANES, v_col=(QA + KVA) // LANES)
    cfg_b = dict(n_back=BACK_B, gqa=False, q_col=(QA + 2 * KVA) // LANES, k_col=(QA + 2 * KVA + QB) // LANES,
                 v_col=(QA + 2 * KVA + 2 * QB) // LANES)
    (oa, lse_a), (wg2,) = _attention_fwd(proj, bias_a, sinks, name="attn_a", carry=gather("wg2"), **cfg_a)
    (ob, lse_b), (wu2,) = _attention_fwd(proj, bias_b, None, name="attn_b", carry=gather("wu2"), **cfg_b)
    ycat = _group_norm_cat(oa, ob, g_grp_a, g_grp_b)
    ymix, x2, h3 = _mm_nn([(ycat, wo)], "out_proj", F32, bias=b_out,
                          tail=_tail_post_pre(x1, g_post_mix, gate2, 1.0, g_pre_ffn2, scale3, shift3))

    (a3, b3, u3), (wd2,) = _ffn_up(h3, wg2, wu2, "ffn_up_ffn2", carry=gather("wd2"))

    def scatter(*grads):
        return _scatter_carry(list(grads))

    slots = {}

    dx3, dy, loss_part, s1 = _mm_nn([(u3, wd2)], "ffn_down_ffn2", None,
                                    tail=_tail_post_loss(x2, tgt, g_post_ffn2, gate3, 0.5))
    da, db = _ffn_down_bwd(dy, wd2, a3, b3, "ffn_down_bwd_ffn2")
    dwd2 = _mm_tn_pair(u3, dy, "grad_wd_ffn2")
    dwg2 = _mm_tn_pair(da, h3, "grad_wg_ffn2")
    dwu2 = _mm_tn_pair(db, h3, "grad_wu_ffn2")
    (dx2, dymix, s2, s3, s1m, db_out), (slots["wd2"],) = _mm_nn(
        [(da, wg2), (db, wu2)], "ffn_up_bwd_ffn2", None, carry=scatter(dwd2),
        tail=_tail_pre_post_bwd(x2, dx3, ymix, g_pre_ffn2, scale3, g_post_mix, gate2, 1.0))
    sm3 = dict(shift=s3, scale=s2 * g_pre_ffn2, gate=0.5 * g_post_ffn2 * s1,
               g_pre=(1.0 + scale3) * s2, g_post=(0.5 * gate3) * s1)

    dycat = _mm_nt(dymix, wo, "out_proj_bwd", F32)
    dwo = _mm_tn_pair(ycat, dymix, "grad_wo")
    doa, dob, dg_a, dg_b = _group_norm_bwd(dycat, oa, ob, g_grp_a, g_grp_b)
    (dqa, dka, dva, dsink), (slots["wg2"],) = _attention_bwd(
        proj, bias_a, sinks, doa, lse_a, name="attn_a_bwd", carry=scatter(dwg2), **cfg_a)
    (dqb, dkb, dvb, dbias), (slots["wu2"], slots["wo"]) = _attention_bwd(
        proj, bias_b, None, dob, lse_b, name="attn_b_bwd", carry=scatter(dwu2, dwo), **cfg_b)
    dproj = jnp.concatenate([dqa, dka, dva, dqb, dkb, dvb], axis=1)
    db_in = _col_sum(dproj, "grad_b_in")
    dwin = _mm_tn_pair(dproj, h2, "grad_win")
    dx1, dy, s2m, s3m, s1, _ = _mm_nn(
        [(dproj, win)], "in_proj_bwd", None,
        tail=_tail_pre_post_bwd(x1, dx2, y1, g_pre_mix, scale2, g_post_ffn1, gate1, 0.5))
    d_rel = jnp.dot(_diagonal_sums(dbias).reshape(H_B, SKEW), rel_m, precision=lax.Precision.HIGHEST)
    d_sinks = dsink[:, :2 * TPS, 0].reshape(1, H_A)

    (da, db), (slots["win"],) = _ffn_down_bwd(dy, wd1, a1, b1, "ffn_down_bwd_ffn1", carry=scatter(dwin))
    dwd1 = _mm_tn_pair(u1, dy, "grad_wd_ffn1")
    dwg1, (slots["wd1"],) = _mm_tn_pair(da, h1, "grad_wg_ffn1", carry=scatter(dwd1))
    dwu1, (slots["wg1"],) = _mm_tn_pair(db, h1, "grad_wu_ffn1", carry=scatter(dwg1))
    (dx0, s2, s3), (slots["wu1"],) = _mm_nn(
        [(da, wg1), (db, wu1)], "ffn_up_bwd_ffn1", None, carry=scatter(dwu1),
        tail=_tail_pre_bwd(xs, dx1, g_pre_ffn1, scale1))
    sm1 = dict(shift=s3, scale=s2 * g_pre_ffn1, gate=0.5 * g_post_ffn1 * s1,
               g_pre=(1.0 + scale1) * s2, g_post=(0.5 * gate1) * s1)

    dmod = jnp.concatenate([sm1["shift"], sm1["scale"], sm1["gate"],
                            s3m, s2m * g_pre_mix, g_post_mix * s1m,
                            sm3["shift"], sm3["scale"], sm3["gate"]], axis=1)
    small_parts = {
        "b_ada": dmod, "g_pre_ffn1": sm1["g_pre"], "g_post_ffn1": sm1["g_post"],
        "g_pre_mix": (1.0 + scale2) * s2m, "b_in": db_in, "sinks_a": d_sinks,
        "rel_bias_b": d_rel.reshape(1, H_B * N_REL), "g_grp_a": dg_a, "g_grp_b": dg_b, "b_out": db_out,
        "g_post_mix": gate2 * s1m, "g_pre_ffn2": sm3["g_pre"], "g_post_ffn2": sm3["g_post"]}
    sizes = [small_parts[n].shape[1] for n in SMALL]

    def pack(parts):
        cells = []
        for p in parts:
            cells.append(p)
            if p.shape[1] % LANES:
                cells.append(jnp.zeros((1, -p.shape[1] % LANES), F32))
        return jnp.concatenate(cells, axis=1)

    packed = pack([small_parts[n] for n in SMALL] + [loss_part])
    n_packed = packed.shape[1]
    gathered = _all_gather_small(packed)
    small_sum = _sum_rows8(gathered)
    loss = small_sum[0, n_packed - LANES]
    dmod_cols = lax.dynamic_slice(gathered.reshape(N_DEV, n_packed), (0, me * ada_cols), (N_DEV, ada_cols))
    g_ada = _ada_weight_grad(sc_all.reshape(N_DEV, d_model).T, dmod_cols)

    out_g, out_d, out_m, out_v = {}, {}, {}, {}
    d_, m_, v_ = _adamw(w_ada[0], g_ada, m_w_ada[0], v_w_ada[0], "adamw_w_ada")
    out_g["w_ada"], out_d["w_ada"], out_m["w_ada"], out_v["w_ada"] = g_ada[None], d_[None], m_[None], v_[None]
    for n, key, transposed in (("w_gate1", "wg1", True), ("w_up1", "wu1", True), ("w_down1", "wd1", False),
                               ("w_in", "win", True), ("w_out", "wo", False), ("w_gate2", "wg2", True),
                               ("w_up2", "wu2", True), ("w_down2", "wd2", False)):
        view = (lambda t: t.T) if transposed else (lambda t: t)
        res = _adamw_from_slots(view(weights[n][0]), slots[key], view(mom_m[n][0]), view(mom_v[n][0]),
                                "adamw_" + n)
        out_g[n], out_d[n], out_m[n], out_v[n] = (view(t)[None] for t in res)

    small_out = _adamw_small(small_sum, *(pack([tree[n].reshape(1, -1) for n in SMALL])
                                          for tree in (weights, mom_m, mom_v)), sizes)
    for j, n in enumerate(SMALL):
        shape = weights[n].shape
        out_g[n], out_d[n], out_m[n], out_v[n] = (t.reshape(shape) for t in small_out[4 * j:4 * j + 4])

    return (loss, dx0[None], *[out_g[n] for n in WEIGHTS], *[out_d[n] for n in WEIGHTS],
            *[out_m[n] for n in WEIGHTS], *[out_v[n] for n in WEIGHTS])
```

```python
import numpy as np
import jax
import jax.numpy as jnp
from jax import lax
from jax.experimental import pallas as pl
from jax.experimental.pallas import tpu as pltpu

F32 = jnp.float32
BF16 = jnp.bfloat16
MESH = pl.DeviceIdType.MESH
ANY = pl.BlockSpec(memory_space=pl.ANY)
VMEM_SPEC = pl.BlockSpec(memory_space=pltpu.VMEM)
SMEM_SPEC = pl.BlockSpec(memory_space=pltpu.SMEM)

N_DEV = 8
CHUNK = 64
HEAD_DIM = 64
LANES = 128
H_A, KV_A, H_B = 8, 2, 8
BACK_A, BACK_B = 2, 8
REL_CLIP = 128
N_REL = 2 * REL_CLIP + 1
QA, KVA, QB = H_A * HEAD_DIM, KV_A * HEAD_DIM, H_B * HEAD_DIM
D_IN = QA + 2 * KVA + 3 * QB
N_MOD = 9
EPS = 1e-6
NEG_INF = -1e30
QG = 4
QROWS = QG * CHUNK
TPS = 2
SKEW = 1024
ADAM_LR, ADAM_B1, ADAM_B2, ADAM_EPS, ADAM_WD, ADAM_STEP = 0.001, 0.9, 0.999, 1e-08, 0.01, 10
VMEM_LIMIT = 56 * 2 ** 20


def _pick(n, cands):
    for c in cands:
        if n % c == 0:
            return c
    return n


def _pieces(n, width=2 * LANES):
    return [(lo, min(lo + width, n)) for lo in range(0, n, width)]


def _params(sem=None):
    return pltpu.CompilerParams(dimension_semantics=sem, vmem_limit_bytes=VMEM_LIMIT)


def _dot_nt(a, b):
    return lax.dot_general(a, b, (((1,), (1,)), ((), ())), preferred_element_type=F32)


def _dot_tn(a, b):
    return lax.dot_general(a, b, (((0,), (0,)), ((), ())), preferred_element_type=F32)


def _dot(a, b):
    return jnp.dot(a, b, preferred_element_type=F32)


def _sigmoid(a):
    return 0.5 * (jnp.tanh(0.5 * a) + 1.0)


def _mesh_pos():
    return lax.axis_index("x"), lax.axis_index("y"), lax.axis_index("c")


def _peer(x, y, c, r):
    px = 1 - x if r & 4 else x
    py = 1 - y if r & 2 else y
    pc = 1 - c if r & 1 else c
    return px, py, pc


class _Carry:
    def __init__(self, ins, out_shapes, scratch, start, finish):
        self.ins, self.out_shapes, self.scratch = list(ins), list(out_shapes), list(scratch)
        self.start, self.finish = start, finish


def _call(body, *, name, grid, in_specs, out_specs, out_shape, args, scratch=(), sem=None, carry=None):
    single = not isinstance(out_shape, (tuple, list))
    out_specs = (out_specs,) if single else tuple(out_specs)
    out_shape = (out_shape,) if single else tuple(out_shape)
    if carry is None:
        res = pl.pallas_call(body, name=name, grid=grid, in_specs=list(in_specs), out_specs=out_specs,
                             out_shape=out_shape, scratch_shapes=list(scratch), compiler_params=_params(sem))(*args)
        return res[0] if single else res
    n_in, n_out, n_s = len(in_specs), len(out_shape), len(scratch)
    ci, co = len(carry.ins), len(carry.out_shapes)

    def wrapped(*refs):
        ins, cins = refs[:n_in], refs[n_in:n_in + ci]
        outs = refs[n_in + ci:n_in + ci + n_out]
        couts = refs[n_in + ci + n_out:n_in + ci + n_out + co]
        scr = refs[n_in + ci + n_out + co:n_in + ci + n_out + co + n_s]
        cscr = refs[n_in + ci + n_out + co + n_s:]
        first, last = None, None
        for ax, n in enumerate(grid):
            f, l = pl.program_id(ax) == 0, pl.program_id(ax) == n - 1
            first = f if first is None else first & f
            last = l if last is None else last & l
        pl.when(first)(lambda: carry.start(cins, couts, cscr))
        body(*ins, *outs, *scr)
        pl.when(last)(lambda: carry.finish(cins, couts, cscr))

    res = pl.pallas_call(
        wrapped, name=name, grid=grid, in_specs=list(in_specs) + [ANY] * ci, out_specs=out_specs + (ANY,) * co,
        out_shape=out_shape + tuple(carry.out_shapes), scratch_shapes=list(scratch) + carry.scratch,
        compiler_params=_params(("arbitrary",) * len(grid)))(*args, *carry.ins)
    main = res[:n_out]
    return (main[0] if single else main), res[n_out:]


def _gather_carry(shards):
    n_w = len(shards)
    rows = [s.shape[0] for s in shards]

    def plan(ins, outs, scr):
        send_sems, recv_sems, local_sems = scr
        x, y, c = _mesh_pos()
        me, sibling = (x, y, c), (x, y, 1 - c)
        chips = [(1 - x, y), (x, 1 - y), (1 - x, 1 - y)]

        def block(w, dev):
            start = pl.multiple_of((4 * dev[0] + 2 * dev[1] + dev[2]) * rows[w], 16)
            return outs[w].at[pl.ds(start, rows[w]), :]

        def copy(w, k, dev, to, src=None):
            return pltpu.make_async_remote_copy(
                src_ref=block(w, dev) if src is None else src, dst_ref=block(w, dev),
                send_sem=send_sems.at[w, k], recv_sem=recv_sems.at[w, k], device_id=to, device_id_type=MESH)

        mine = [pltpu.make_async_copy(ins[w], block(w, me), local_sems.at[w]) for w in range(n_w)]
        first = []
        for j, chip in enumerate(chips):
            first += [copy(w, 1 + j, me, (*chip, c), src=ins[w]) for w in range(n_w)]
        first += [copy(w, 0, me, sibling, src=ins[w]) for w in range(n_w)]
        return c, me, sibling, chips, copy, mine, first

    def start(ins, outs, scr):
        _, _, _, _, _, mine, first = plan(ins, outs, scr)
        for cp in mine + first:
            cp.start()

    def finish(ins, outs, scr):
        c, me, sibling, chips, copy, mine, first = plan(ins, outs, scr)
        passed = []
        for j, chip in enumerate(chips):
            for w in range(n_w):
                copy(w, 1 + j, (*chip, c), me).wait_recv()
                cp = copy(w, 4 + j, (*chip, c), sibling)
                cp.start()
                passed.append(cp)
        for w in range(n_w):
            copy(w, 0, sibling, me).wait_recv()
        for j, chip in enumerate(chips):
            for w in range(n_w):
                copy(w, 4 + j, (*chip, 1 - c), me).wait_recv()
        for cp in first + passed:
            cp.wait_send()
        for cp in mine:
            cp.wait()

    return _Carry(
        shards, [jax.ShapeDtypeStruct((N_DEV * s.shape[0], s.shape[1]), s.dtype) for s in shards],
        [pltpu.SemaphoreType.DMA((n_w, N_DEV - 1)), pltpu.SemaphoreType.DMA((n_w, N_DEV - 1)),
         pltpu.SemaphoreType.DMA((n_w,))], start, finish)


def _scatter_carry(parts):
    n_w = len(parts)
    n_chip = N_DEV // 2
    rows = [g.shape[0] // n_chip for g in parts]

    def plan(ins, outs, scr):
        send_sems, recv_sems, local_sems = scr
        x, y, c = _mesh_pos()

        def src(w, chip_index):
            return ins[w].at[pl.ds(pl.multiple_of(chip_index * rows[w], 16), rows[w]), :]

        mine = [pltpu.make_async_copy(src(w, 2 * x + y), outs[w].at[0], local_sems.at[w]) for w in range(n_w)]
        copies = []
        for r in (3, 2, 1):
            px, py, _ = _peer(x, y, c, 2 * r)
            for w in range(n_w):
                copies.append(pltpu.make_async_remote_copy(
                    src_ref=src(w, 2 * px + py), dst_ref=outs[w].at[r], send_sem=send_sems.at[w, r - 1],
                    recv_sem=recv_sems.at[w, r - 1], device_id=(px, py, c), device_id_type=MESH))
        return mine, copies

    def start(ins, outs, scr):
        mine, copies = plan(ins, outs, scr)
        for cp in mine + copies:
            cp.start()

    def finish(ins, outs, scr):
        mine, copies = plan(ins, outs, scr)
        for cp in copies:
            cp.wait_recv()
        for cp in copies:
            cp.wait_send()
        for cp in mine:
            cp.wait()

    return _Carry(
        parts, [jax.ShapeDtypeStruct((n_chip, r, g.shape[1]), g.dtype) for r, g in zip(rows, parts)],
        [pltpu.SemaphoreType.DMA((n_w, n_chip - 1)), pltpu.SemaphoreType.DMA((n_w, n_chip - 1)),
         pltpu.SemaphoreType.DMA((n_w,))], start, finish)


def _ada_forward(c_row, w_ada, b_cols, carry):
    d = c_row.shape[1]
    wcols = w_ada.shape[1]
    ci, co = len(carry.ins), len(carry.out_shapes)

    def body(*refs):
        c_ref, w_ref, b_ref = refs[:3]
        cins = refs[3:3 + ci]
        sc_ref, mod_ref = refs[3 + ci:5 + ci]
        couts = refs[5 + ci:5 + ci + co]
        rows_ref, send_sems, recv_sems = refs[5 + ci + co:8 + ci + co]
        cscr = refs[8 + ci + co:]
        carry.start(cins, couts, cscr)
        x, y, c = _mesh_pos()
        me = 4 * x + 2 * y + c
        cv = c_ref[...]
        sc_ref[me] = cv * _sigmoid(cv)

        sends = []
        for r in range(1, N_DEV):
            px, py, pc = _peer(x, y, c, r)
            cp = pltpu.make_async_remote_copy(
                src_ref=sc_ref.at[me], dst_ref=sc_ref.at[me], send_sem=send_sems.at[0, r - 1],
                recv_sem=recv_sems.at[0, r - 1], device_id=(px, py, pc), device_id_type=MESH)
            cp.start()
            sends.append(cp)
        for r in range(1, N_DEV):
            px, py, pc = _peer(x, y, c, r)
            pid = 4 * px + 2 * py + pc
            pltpu.make_async_remote_copy(
                src_ref=sc_ref.at[pid], dst_ref=sc_ref.at[pid], send_sem=send_sems.at[0, r - 1],
                recv_sem=recv_sems.at[0, r - 1], device_id=(px, py, pc), device_id_type=MESH).wait_recv()
        for cp in sends:
            cp.wait_send()

        sc_all = jnp.concatenate([sc_ref[j] for j in range(N_DEV)], axis=0)
        rows = _dot(sc_all.astype(BF16), w_ref[...].astype(BF16)) + b_ref[...]
        for j in range(N_DEV):
            rows_ref[j] = rows[j:j + 1, :]
        mod_ref[me] = rows_ref[me]

        sends = []
        for r in range(1, N_DEV):
            px, py, pc = _peer(x, y, c, r)
            pid = 4 * px + 2 * py + pc
            cp = pltpu.make_async_remote_copy(
                src_ref=rows_ref.at[pid], dst_ref=mod_ref.at[me], send_sem=send_sems.at[1, r - 1],
                recv_sem=recv_sems.at[1, r - 1], device_id=(px, py, pc), device_id_type=MESH)
            cp.start()
            sends.append(cp)
        for r in range(1, N_DEV):
            px, py, pc = _peer(x, y, c, r)
            pid = 4 * px + 2 * py + pc
            pltpu.make_async_remote_copy(
                src_ref=rows_ref.at[pid], dst_ref=mod_ref.at[pid], send_sem=send_sems.at[1, r - 1],
                recv_sem=recv_sems.at[1, r - 1], device_id=(px, py, pc), device_id_type=MESH).wait_recv()
        for cp in sends:
            cp.wait_send()
        carry.finish(cins, couts, cscr)

    res = pl.pallas_call(
        body, name="ada_forward",
        out_shape=(jax.ShapeDtypeStruct((N_DEV, 1, d), F32), jax.ShapeDtypeStruct((N_DEV, 1, wcols), F32),
                   *carry.out_shapes),
        in_specs=[VMEM_SPEC, VMEM_SPEC, VMEM_SPEC] + [ANY] * ci, out_specs=(VMEM_SPEC, VMEM_SPEC) + (ANY,) * co,
        scratch_shapes=[pltpu.VMEM((N_DEV, 1, wcols), F32), pltpu.SemaphoreType.DMA((2, N_DEV - 1)),
                        pltpu.SemaphoreType.DMA((2, N_DEV - 1))] + carry.scratch,
        compiler_params=_params(),
    )(c_row, w_ada, b_cols, *carry.ins)
    return res[:2], res[2:]


def _all_gather_small(v):
    n = v.shape[1]

    def body(v_ref, out_ref, send_sems, recv_sems):
        x, y, c = _mesh_pos()
        me = 4 * x + 2 * y + c
        out_ref[me] = v_ref[...]
        sends = []
        for r in range(1, N_DEV):
            px, py, pc = _peer(x, y, c, r)
            cp = pltpu.make_async_remote_copy(
                src_ref=v_ref, dst_ref=out_ref.at[me], send_sem=send_sems.at[r - 1],
                recv_sem=recv_sems.at[r - 1], device_id=(px, py, pc), device_id_type=MESH)
            cp.start()
            sends.append(cp)
        for r in range(1, N_DEV):
            px, py, pc = _peer(x, y, c, r)
            pid = 4 * px + 2 * py + pc
            pltpu.make_async_remote_copy(
                src_ref=v_ref, dst_ref=out_ref.at[pid], send_sem=send_sems.at[r - 1],
                recv_sem=recv_sems.at[r - 1], device_id=(px, py, pc), device_id_type=MESH).wait_recv()
        for cp in sends:
            cp.wait_send()

    return pl.pallas_call(
        body, name="all_gather_small",
        out_shape=jax.ShapeDtypeStruct((N_DEV, 1, n), F32),
        in_specs=[VMEM_SPEC], out_specs=VMEM_SPEC,
        scratch_shapes=[pltpu.SemaphoreType.DMA((N_DEV - 1,)), pltpu.SemaphoreType.DMA((N_DEV - 1,))],
        compiler_params=_params(),
    )(v)


def _mm_nt(a, b, name, out_dtype, bias=None, carry=None):
    m, k = a.shape
    n = b.shape[0]
    tm = _pick(m, (512, 256, 128))
    tn = _pick(n, (1408, 1152, 1024, 768, 512, 256, 128))

    def body(*refs):
        acc = _dot_nt(refs[0][...], refs[1][...])
        if bias is not None:
            acc = acc + refs[2][...]
        refs[-1][...] = acc.astype(out_dtype)

    in_specs = [pl.BlockSpec((tm, k), lambda j, i: (i, 0)), pl.BlockSpec((tn, k), lambda j, i: (j, 0))]
    args = [a, b]
    if bias is not None:
        in_specs.append(pl.BlockSpec((1, tn), lambda j, i: (0, j)))
        args.append(bias)
    return _call(body, name=name, grid=(n // tn, m // tm), in_specs=in_specs,
                 out_specs=pl.BlockSpec((tm, tn), lambda j, i: (i, j)),
                 out_shape=jax.ShapeDtypeStruct((m, n), out_dtype), args=args,
                 sem=("parallel", "parallel"), carry=carry)


class _Tail:
    def __init__(self, rows, vecs, outs, fn):
        self.rows, self.vecs, self.outs, self.fn = list(rows), list(vecs), list(outs), fn


def _mm_nn(pairs, name, out_dtype, bias=None, carry=None, tail=None):
    m, k = pairs[0][0].shape
    n = pairs[0][1].shape[1]
    n_p = len(pairs)
    tm = _pick(m, (512, 256, 128))
    tk = k if n_p == 1 else _pick(k, (1408, 1152, 1024, 768, 512, 256, 128))
    nk = k // tk
    n_b = 0 if bias is None else 1
    n_r, n_v = (len(tail.rows), len(tail.vecs)) if tail else (0, 0)
    n_in = 2 * n_p + n_b + n_r + n_v
    n_main = 0 if out_dtype is None else 1

    def finish(acc, refs, first_tile):
        if bias is not None:
            acc = acc + refs[2 * n_p][...]
        outs = refs[n_in:-1]
        if n_main:
            outs[0][...] = acc.astype(out_dtype)
        if tail is None:
            return
        rows = [r[...] for r in refs[2 * n_p + n_b:2 * n_p + n_b + n_r]]
        vecs = [v[...] for v in refs[2 * n_p + n_b + n_r:n_in]]
        vals = tail.fn(acc, rows, vecs)
        for ref, val, (dtype, kind) in zip(outs[n_main:], vals, tail.outs):
            if kind == "row":
                ref[...] = val.astype(dtype)
            else:
                @pl.when(first_tile)
                def _(ref=ref):
                    ref[...] = jnp.zeros_like(ref)

                ref[...] += val

    def body(*refs):
        acc_ref = refs[-1]
        kk, i = pl.program_id(0), pl.program_id(1)
        part = _dot(refs[0][...], refs[1][...])
        for p in range(1, n_p):
            part = part + _dot(refs[2 * p][...], refs[2 * p + 1][...])
        if nk == 1:
            finish(part, refs, i == 0)
            return
        rows = pl.ds(pl.multiple_of(i * tm, tm), tm)

        @pl.when(kk == 0)
        def _():
            acc_ref[rows, :] = part

        if nk > 2:
            @pl.when((kk > 0) & (kk < nk - 1))
            def _():
                acc_ref[rows, :] += part

        @pl.when(kk == nk - 1)
        def _():
            finish(acc_ref[rows, :] + part, refs, i == 0)

    def last_only(kk, i):
        return (jnp.where(kk == nk - 1, i, 0), 0)

    row_spec = pl.BlockSpec((tm, n), last_only)
    vec_spec = pl.BlockSpec((1, n), lambda kk, i: (0, 0))
    in_specs, args = [], []
    for a, b in pairs:
        in_specs += [pl.BlockSpec((tm, tk), lambda kk, i: (i, kk)), pl.BlockSpec((tk, n), lambda kk, i: (kk, 0))]
        args += [a, b]
    if bias is not None:
        in_specs.append(vec_spec)
        args.append(bias)
    out_specs = [row_spec] * n_main
    out_shape = [jax.ShapeDtypeStruct((m, n), out_dtype)] if n_main else []
    if tail:
        in_specs += [row_spec] * n_r + [vec_spec] * n_v
        args += tail.rows + tail.vecs
        for dtype, kind in tail.outs:
            if kind == "row":
                out_specs.append(row_spec)
                out_shape.append(jax.ShapeDtypeStruct((m, n), dtype))
            else:
                width = n if kind == "sum" else 1
                out_specs.append(pl.BlockSpec((1, width), lambda kk, i: (0, 0)))
                out_shape.append(jax.ShapeDtypeStruct((1, width), dtype))
    if tail is None:
        out_specs, out_shape = out_specs[0], out_shape[0]
    return _call(body, name=name, grid=(nk, m // tm), in_specs=in_specs, out_specs=out_specs,
                 out_shape=out_shape, args=args,
                 scratch=[pltpu.VMEM((m, n) if nk > 1 else (8, LANES), F32)],
                 sem=("arbitrary", "arbitrary"), carry=carry)


def _rms(v):
    return lax.rsqrt(jnp.mean(v * v, axis=-1, keepdims=True) + EPS)


def _col(v):
    return jnp.sum(v, axis=0, keepdims=True)


def _tail_post_pre(x, g_post, gate, weight, g_pre, scale, shift):
    def fn(y, rows, vecs):
        (xv,), (gp, gt, g, sc, sh) = rows, vecs
        xo = xv + (weight * gt) * ((y * _rms(y)) * gp)
        return xo, ((xo * _rms(xo)) * g) * (1.0 + sc) + sh

    return _Tail([x], [g_post, gate, g_pre, scale, shift], [(F32, "row"), (BF16, "row")], fn)


def _tail_post_loss(x, target, g, gate, weight):
    def fn(y, rows, vecs):
        (xv, tv), (gv, gt) = rows, vecs
        r = _rms(y)
        yn = y * r
        err = (xv + (weight * gt) * (yn * gv)) - tv
        do = err * (1.0 / y.shape[1])
        dyn = do * ((weight * gt) * gv)
        dy = r * (dyn - yn * jnp.mean(dyn * yn, axis=-1, keepdims=True))
        return do, dy, 0.5 * _col(jnp.mean(err * err, axis=-1, keepdims=True)), _col(do * yn)

    return _Tail([x, target], [g, gate], [(F32, "row"), (BF16, "row"), (F32, "one"), (F32, "sum")], fn)


def _tail_pre_bwd(x, dres, g_pre, scale):
    def fn(dh, rows, vecs):
        (xv, dr), (g, sc) = rows, vecs
        r = _rms(xv)
        n = xv * r
        dn = dh * (g * (1.0 + sc))
        return dr + r * (dn - n * jnp.mean(dn * n, axis=-1, keepdims=True)), _col(dh * n), _col(dh)

    return _Tail([x, dres], [g_pre, scale], [(F32, "row"), (F32, "sum"), (F32, "sum")], fn)


def _tail_pre_post_bwd(x, dres, y, g_pre, scale, g_post, gate, weight):
    def fn(dh, rows, vecs):
        (xv, dr, yv), (g, sc, gp, gt) = rows, vecs
        r = _rms(xv)
        n = xv * r
        dn = dh * (g * (1.0 + sc))
        dx = dr + r * (dn - n * jnp.mean(dn * n, axis=-1, keepdims=True))
        ry = _rms(yv)
        yn = yv * ry
        dyn = dx * ((weight * gt) * gp)
        dy = ry * (dyn - yn * jnp.mean(dyn * yn, axis=-1, keepdims=True))
        return dx, dy, _col(dh * n), _col(dh), _col(dx * yn), _col(dy)

    return _Tail([x, dres, y], [g_pre, scale, g_post, gate],
                 [(F32, "row"), (BF16, "row")] + [(F32, "sum")] * 4, fn)


def _mm_tn(a, b, name, out_dtype=BF16, carry=None):
    k, m = a.shape
    n = b.shape[1]
    tm = _pick(m, (1408, 1152, 1024, 768, 512, 256, 128))
    tk = _pick(k, (512, 256, 128))
    nk = k // tk

    def body(a_ref, b_ref, o_ref, acc_ref):
        kk = pl.program_id(1)

        @pl.when(kk == 0)
        def _():
            acc_ref[...] = jnp.zeros_like(acc_ref)

        acc_ref[...] += _dot_tn(a_ref[...], b_ref[...])

        @pl.when(kk == nk - 1)
        def _():
            o_ref[...] = acc_ref[...].astype(out_dtype)

    return _call(body, name=name, grid=(m // tm, nk),
                 in_specs=[pl.BlockSpec((tk, tm), lambda i, kk: (kk, i)), pl.BlockSpec((tk, n), lambda i, kk: (kk, 0))],
                 out_specs=pl.BlockSpec((tm, n), lambda i, kk: (i, 0)),
                 out_shape=jax.ShapeDtypeStruct((m, n), out_dtype), args=[a, b],
                 scratch=[pltpu.VMEM((tm, n), F32)], sem=("parallel", "arbitrary"), carry=carry)


def _mm_tn_pair(a, b, name, carry=None):
    k, m = a.shape
    n = b.shape[1]
    rows = m // N_DEV
    n_chip = N_DEV // 2
    tm = 4 * rows
    tk = _pick(k, (1024, 512, 256, 128))
    nk = k // tk

    def body(a_ref, b_ref, p_ref, acc_ref, keep_ref, send_ref, land_ref, send_sems, recv_sems):
        i, kk = pl.program_id(0), pl.program_id(1)
        x, y, c = _mesh_pos()

        def push(chip):
            return pltpu.make_async_remote_copy(
                src_ref=send_ref.at[chip], dst_ref=land_ref.at[chip], send_sem=send_sems.at[chip],
                recv_sem=recv_sems.at[chip], device_id=(x, y, 1 - c), device_id_type=MESH)

        if nk == 1:
            acc = _dot_tn(a_ref[...], b_ref[...])
        else:
            @pl.when(kk == 0)
            def _():
                acc_ref[...] = jnp.zeros_like(acc_ref)

            acc_ref[...] += _dot_tn(a_ref[...], b_ref[...])
            acc = acc_ref

        for t in range(2):
            @pl.when((kk == nk - 1) & (i == t))
            def _(t=t):
                for ob in range(4):
                    chip, core = 2 * t + ob // 2, ob % 2
                    blk = acc[ob * rows:(ob + 1) * rows, :]

                    @pl.when(c == core)
                    def _(chip=chip, blk=blk):
                        keep_ref[chip] = blk

                    @pl.when(c != core)
                    def _(chip=chip, blk=blk):
                        send_ref[chip] = blk.astype(BF16)
                        push(chip).start()

        @pl.when((kk == nk - 1) & (i == 1))
        def _():
            for chip in range(n_chip):
                push(chip).wait_recv()
                p_ref[chip * rows:(chip + 1) * rows, :] = (
                    keep_ref[chip] + land_ref[chip].astype(F32)).astype(BF16)
            for chip in range(n_chip):
                push(chip).wait_send()

    return _call(body, name=name, grid=(2, nk),
                 in_specs=[pl.BlockSpec((tk, tm), lambda i, kk: (kk, i)), pl.BlockSpec((tk, n), lambda i, kk: (kk, 0))],
                 out_specs=pl.BlockSpec((n_chip * rows, n), lambda i, kk: (0, 0)),
                 out_shape=jax.ShapeDtypeStruct((n_chip * rows, n), BF16), args=[a, b],
                 scratch=[pltpu.VMEM((tm, n) if nk > 1 else (8, LANES), F32), pltpu.VMEM((n_chip, rows, n), F32),
                          pltpu.VMEM((n_chip, rows, n), BF16), pltpu.VMEM((n_chip, rows, n), BF16),
                          pltpu.SemaphoreType.DMA((n_chip,)), pltpu.SemaphoreType.DMA((n_chip,))],
                 sem=("arbitrary", "arbitrary"), carry=carry)


def _ffn_up(h, wg_t, wu_t, name, carry=None):
    s, d = h.shape
    f = wg_t.shape[0]
    tm = _pick(s, (512, 256, 128))
    tf = _pick(f, (1408, 1024, 512, 256, 128))

    def body(h_ref, wg_ref, wu_ref, t_ref, gp_ref, u_ref):
        hh = h_ref[...]
        for lo, hi in _pieces(tf):
            a = _dot_nt(hh, wg_ref[lo:hi, :])
            b = _dot_nt(hh, wu_ref[lo:hi, :])
            sig = _sigmoid(a)
            t = a * sig
            t_ref[:, lo:hi] = t.astype(BF16)
            gp_ref[:, lo:hi] = (b * (sig + t * (1.0 - sig))).astype(BF16)
            u_ref[:, lo:hi] = (t * b).astype(BF16)

    w_spec = pl.BlockSpec((tf, d), lambda j, i: (j, 0))
    o_spec = pl.BlockSpec((tm, tf), lambda j, i: (i, j))
    o_shape = jax.ShapeDtypeStruct((s, f), BF16)
    return _call(body, name=name, grid=(f // tf, s // tm),
                 in_specs=[pl.BlockSpec((tm, d), lambda j, i: (i, 0)), w_spec, w_spec],
                 out_specs=(o_spec, o_spec, o_spec), out_shape=(o_shape, o_shape, o_shape),
                 args=[h, wg_t, wu_t], sem=("parallel", "parallel"), carry=carry)


def _ffn_down_bwd(dy, wd, t, gp, name, carry=None):
    s, d = dy.shape
    f = wd.shape[0]
    tm = _pick(s, (512, 256, 128))
    tf = _pick(f, (1408, 1024, 512, 256, 128))

    def body(dy_ref, wd_ref, t_ref, gp_ref, da_ref, db_ref):
        dyv = dy_ref[...]
        for lo, hi in _pieces(tf):
            du = _dot_nt(dyv, wd_ref[lo:hi, :])
            da_ref[:, lo:hi] = (du * gp_ref[:, lo:hi].astype(F32)).astype(BF16)
            db_ref[:, lo:hi] = (du * t_ref[:, lo:hi].astype(F32)).astype(BF16)

    t_spec = pl.BlockSpec((tm, tf), lambda j, i: (i, j))
    o_shape = jax.ShapeDtypeStruct((s, f), BF16)
    return _call(body, name=name, grid=(f // tf, s // tm),
                 in_specs=[pl.BlockSpec((tm, d), lambda j, i: (i, 0)), pl.BlockSpec((tf, d), lambda j, i: (j, 0)),
                           t_spec, t_spec],
                 out_specs=(t_spec, t_spec), out_shape=(o_shape, o_shape), args=[dy, wd, t, gp],
                 sem=("parallel", "parallel"), carry=carry)


def _row_tile(s):
    return _pick(s, (256, 128, 64))


def _vec_spec(d):
    return pl.BlockSpec((1, d), lambda i: (0, 0))


def _pre_norm(x, g, scale, shift, name):
    s, d = x.shape
    ts = _row_tile(s)

    def body(x_ref, g_ref, sc_ref, sh_ref, h_ref):
        xv = x_ref[...]
        r = lax.rsqrt(jnp.mean(xv * xv, axis=-1, keepdims=True) + EPS)
        h_ref[...] = (((xv * r) * g_ref[...]) * (1.0 + sc_ref[...]) + sh_ref[...]).astype(BF16)

    row = pl.BlockSpec((ts, d), lambda i: (i, 0))
    return _call(body, name=name, grid=(s // ts,), in_specs=[row, _vec_spec(d), _vec_spec(d), _vec_spec(d)],
                 out_specs=row, out_shape=jax.ShapeDtypeStruct((s, d), BF16), args=[x, g, scale, shift],
                 sem=("parallel",))


def _post_norm_residual(x, y, g, gate, weight, name):
    s, d = x.shape
    ts = _row_tile(s)

    def body(x_ref, y_ref, g_ref, gate_ref, o_ref):
        yv = y_ref[...]
        r = lax.rsqrt(jnp.mean(yv * yv, axis=-1, keepdims=True) + EPS)
        o_ref[...] = x_ref[...] + (weight * gate_ref[...]) * ((yv * r) * g_ref[...])

    row = pl.BlockSpec((ts, d), lambda i: (i, 0))
    return _call(body, name=name, grid=(s // ts,), in_specs=[row, row, _vec_spec(d), _vec_spec(d)],
                 out_specs=row, out_shape=jax.ShapeDtypeStruct((s, d), F32), args=[x, y, g, gate],
                 sem=("parallel",))


def _post_norm_bwd(dout, y, g, gate, weight, name):
    s, d = y.shape
    ts = _row_tile(s)

    def body(do_ref, y_ref, g_ref, gate_ref, dy_ref, s1_ref, cs_ref):
        @pl.when(pl.program_id(0) == 0)
        def _():
            s1_ref[...] = jnp.zeros_like(s1_ref)
            cs_ref[...] = jnp.zeros_like(cs_ref)

        yv = y_ref[...]
        do = do_ref[...]
        r = lax.rsqrt(jnp.mean(yv * yv, axis=-1, keepdims=True) + EPS)
        yn = yv * r
        dyn = do * ((weight * gate_ref[...]) * g_ref[...])
        dy = r * (dyn - yn * jnp.mean(dyn * yn, axis=-1, keepdims=True))
        dy_ref[...] = dy.astype(BF16)
        s1_ref[...] += jnp.sum(do * yn, axis=0, keepdims=True)
        cs_ref[...] += jnp.sum(dy, axis=0, keepdims=True)

    row = pl.BlockSpec((ts, d), lambda i: (i, 0))
    vec = jax.ShapeDtypeStruct((1, d), F32)
    return _call(body, name=name, grid=(s // ts,), in_specs=[row, row, _vec_spec(d), _vec_spec(d)],
                 out_specs=(row, _vec_spec(d), _vec_spec(d)),
                 out_shape=(jax.ShapeDtypeStruct((s, d), BF16), vec, vec), args=[dout, y, g, gate],
                 sem=("arbitrary",))


def _pre_norm_bwd(dh, x, g, scale, dres, name):
    s, d = x.shape
    ts = _row_tile(s)

    def body(dh_ref, x_ref, g_ref, sc_ref, dr_ref, dx_ref, s2_ref, s3_ref):
        @pl.when(pl.program_id(0) == 0)
        def _():
            s2_ref[...] = jnp.zeros_like(s2_ref)
            s3_ref[...] = jnp.zeros_like(s3_ref)

        xv = x_ref[...]
        dh = dh_ref[...]
        r = lax.rsqrt(jnp.mean(xv * xv, axis=-1, keepdims=True) + EPS)
        n = xv * r
        dn = dh * (g_ref[...] * (1.0 + sc_ref[...]))
        dx_ref[...] = dr_ref[...] + r * (dn - n * jnp.mean(dn * n, axis=-1, keepdims=True))
        s2_ref[...] += jnp.sum(dh * n, axis=0, keepdims=True)
        s3_ref[...] += jnp.sum(dh, axis=0, keepdims=True)

    row = pl.BlockSpec((ts, d), lambda i: (i, 0))
    vec = jax.ShapeDtypeStruct((1, d), F32)
    return _call(body, name=name, grid=(s // ts,), in_specs=[row, row, _vec_spec(d), _vec_spec(d), row],
                 out_specs=(row, _vec_spec(d), _vec_spec(d)),
                 out_shape=(jax.ShapeDtypeStruct((s, d), F32), vec, vec), args=[dh, x, g, scale, dres],
                 sem=("arbitrary",))


def _post_pre_norm(x, y, g_post, gate, weight, g_pre, scale, shift, name):
    s, d = x.shape
    ts = _row_tile(s)

    def body(x_ref, y_ref, gp_ref, gate_ref, g_ref, sc_ref, sh_ref, o_ref, h_ref):
        yv = y_ref[...]
        r = lax.rsqrt(jnp.mean(yv * yv, axis=-1, keepdims=True) + EPS)
        xv = x_ref[...] + (weight * gate_ref[...]) * ((yv * r) * gp_ref[...])
        o_ref[...] = xv
        r2 = lax.rsqrt(jnp.mean(xv * xv, axis=-1, keepdims=True) + EPS)
        h_ref[...] = (((xv * r2) * g_ref[...]) * (1.0 + sc_ref[...]) + sh_ref[...]).astype(BF16)

    row = pl.BlockSpec((ts, d), lambda i: (i, 0))
    return _call(body, name=name, grid=(s // ts,), in_specs=[row, row] + [_vec_spec(d)] * 5,
                 out_specs=(row, row),
                 out_shape=(jax.ShapeDtypeStruct((s, d), F32), jax.ShapeDtypeStruct((s, d), BF16)),
                 args=[x, y, g_post, gate, g_pre, scale, shift], sem=("parallel",))


def _post_norm_loss_bwd(x, y, g, gate, weight, target, name):
    s, d = y.shape
    ts = _row_tile(s)

    def body(x_ref, y_ref, g_ref, gate_ref, t_ref, dx_ref, dy_ref, l_ref, s1_ref):
        @pl.when(pl.program_id(0) == 0)
        def _():
            l_ref[...] = jnp.zeros_like(l_ref)
            s1_ref[...] = jnp.zeros_like(s1_ref)

        yv = y_ref[...]
        r = lax.rsqrt(jnp.mean(yv * yv, axis=-1, keepdims=True) + EPS)
        yn = yv * r
        err = (x_ref[...] + (weight * gate_ref[...]) * (yn * g_ref[...])) - t_ref[...]
        do = err * (1.0 / d)
        dx_ref[...] = do
        l_ref[...] += 0.5 * jnp.sum(jnp.mean(err * err, axis=-1, keepdims=True), axis=0, keepdims=True)
        dyn = do * ((weight * gate_ref[...]) * g_ref[...])
        dy_ref[...] = (r * (dyn - yn * jnp.mean(dyn * yn, axis=-1, keepdims=True))).astype(BF16)
        s1_ref[...] += jnp.sum(do * yn, axis=0, keepdims=True)

    row = pl.BlockSpec((ts, d), lambda i: (i, 0))
    return _call(body, name=name, grid=(s // ts,), in_specs=[row, row, _vec_spec(d), _vec_spec(d), row],
                 out_specs=(row, row, pl.BlockSpec((1, 1), lambda i: (0, 0)), _vec_spec(d)),
                 out_shape=(jax.ShapeDtypeStruct((s, d), F32), jax.ShapeDtypeStruct((s, d), BF16),
                            jax.ShapeDtypeStruct((1, 1), F32), jax.ShapeDtypeStruct((1, d), F32)),
                 args=[x, y, g, gate, target], sem=("arbitrary",))


def _pre_post_norm_bwd(dh, x, g_pre, scale, dres, y, g_post, gate, weight, name):
    s, d = x.shape
    ts = _row_tile(s)

    def body(dh_ref, x_ref, g_ref, sc_ref, dr_ref, y_ref, gp_ref, gate_ref,
             dx_ref, dy_ref, s2_ref, s3_ref, s1_ref, cs_ref):
        @pl.when(pl.program_id(0) == 0)
        def _():
            for ref in (s2_ref, s3_ref, s1_ref, cs_ref):
                ref[...] = jnp.zeros_like(ref)

        xv = x_ref[...]
        dh = dh_ref[...]
        r = lax.rsqrt(jnp.mean(xv * xv, axis=-1, keepdims=True) + EPS)
        n = xv * r
        dn = dh * (g_ref[...] * (1.0 + sc_ref[...]))
        dx = dr_ref[...] + r * (dn - n * jnp.mean(dn * n, axis=-1, keepdims=True))
        dx_ref[...] = dx
        s2_ref[...] += jnp.sum(dh * n, axis=0, keepdims=True)
        s3_ref[...] += jnp.sum(dh, axis=0, keepdims=True)
        yv = y_ref[...]
        ry = lax.rsqrt(jnp.mean(yv * yv, axis=-1, keepdims=True) + EPS)
        yn = yv * ry
        dyn = dx * ((weight * gate_ref[...]) * gp_ref[...])
        dy = ry * (dyn - yn * jnp.mean(dyn * yn, axis=-1, keepdims=True))
        dy_ref[...] = dy.astype(BF16)
        s1_ref[...] += jnp.sum(dx * yn, axis=0, keepdims=True)
        cs_ref[...] += jnp.sum(dy, axis=0, keepdims=True)

    row = pl.BlockSpec((ts, d), lambda i: (i, 0))
    vec = jax.ShapeDtypeStruct((1, d), F32)
    return _call(body, name=name, grid=(s // ts,),
                 in_specs=[row, row, _vec_spec(d), _vec_spec(d), row, row, _vec_spec(d), _vec_spec(d)],
                 out_specs=(row, row) + (_vec_spec(d),) * 4,
                 out_shape=(jax.ShapeDtypeStruct((s, d), F32), jax.ShapeDtypeStruct((s, d), BF16), vec, vec, vec, vec),
                 args=[dh, x, g_pre, scale, dres, y, g_post, gate], sem=("arbitrary",))


def _group_norm_cat(oa, ob, ga, gb):
    s = oa.shape[0]
    ts = _row_tile(s)

    def body(oa_ref, ob_ref, ga_ref, gb_ref, y_ref):
        for o_ref, g_ref, lo, w in ((oa_ref, ga_ref, 0, QA), (ob_ref, gb_ref, QA, QB)):
            ov = o_ref[...]
            r = lax.rsqrt(jnp.mean(ov * ov, axis=-1, keepdims=True) + EPS)
            y_ref[:, lo:lo + w] = ((ov * r) * g_ref[...]).astype(BF16)

    return _call(body, name="group_norm_cat", grid=(s // ts,),
                 in_specs=[pl.BlockSpec((ts, QA), lambda i: (i, 0)), pl.BlockSpec((ts, QB), lambda i: (i, 0)),
                           _vec_spec(QA), _vec_spec(QB)],
                 out_specs=pl.BlockSpec((ts, QA + QB), lambda i: (i, 0)),
                 out_shape=jax.ShapeDtypeStruct((s, QA + QB), BF16), args=[oa, ob, ga, gb], sem=("parallel",))


def _group_norm_bwd(dy, oa, ob, ga, gb):
    s = oa.shape[0]
    ts = _row_tile(s)

    def body(dy_ref, oa_ref, ob_ref, ga_ref, gb_ref, doa_ref, dob_ref, dga_ref, dgb_ref):
        @pl.when(pl.program_id(0) == 0)
        def _():
            dga_ref[...] = jnp.zeros_like(dga_ref)
            dgb_ref[...] = jnp.zeros_like(dgb_ref)

        for o_ref, g_ref, do_ref, dg_ref, lo, w in ((oa_ref, ga_ref, doa_ref, dga_ref, 0, QA),
                                                    (ob_ref, gb_ref, dob_ref, dgb_ref, QA, QB)):
            ov = o_ref[...]
            dyv = dy_ref[:, lo:lo + w]
            r = lax.rsqrt(jnp.mean(ov * ov, axis=-1, keepdims=True) + EPS)
            n = ov * r
            dn = dyv * g_ref[...]
            do_ref[...] = r * (dn - n * jnp.mean(dn * n, axis=-1, keepdims=True))
            dg_ref[...] += jnp.sum(dyv * n, axis=0, keepdims=True)

    ra = pl.BlockSpec((ts, QA), lambda i: (i, 0))
    rb = pl.BlockSpec((ts, QB), lambda i: (i, 0))
    return _call(body, name="group_norm_bwd", grid=(s // ts,),
                 in_specs=[pl.BlockSpec((ts, QA + QB), lambda i: (i, 0)), ra, rb, _vec_spec(QA), _vec_spec(QB)],
                 out_specs=(ra, rb, _vec_spec(QA), _vec_spec(QB)),
                 out_shape=(jax.ShapeDtypeStruct((s, QA), F32), jax.ShapeDtypeStruct((s, QB), F32),
                            jax.ShapeDtypeStruct((1, QA), F32), jax.ShapeDtypeStruct((1, QB), F32)),
                 args=[dy, oa, ob, ga, gb], sem=("arbitrary",))


def _loss_and_grad(y, target):
    s, d = y.shape
    ts = _row_tile(s)

    def body(y_ref, t_ref, l_ref, g_ref):
        @pl.when(pl.program_id(0) == 0)
        def _():
            l_ref[...] = jnp.zeros_like(l_ref)

        err = y_ref[...] - t_ref[...]
        g_ref[...] = err * (1.0 / d)
        row = jnp.mean(err * err, axis=-1, keepdims=True)
        l_ref[...] += 0.5 * jnp.sum(row, axis=0, keepdims=True)

    row = pl.BlockSpec((ts, d), lambda i: (i, 0))
    return _call(body, name="loss_and_grad", grid=(s // ts,), in_specs=[row, row],
                 out_specs=(pl.BlockSpec((1, 1), lambda i: (0, 0)), row),
                 out_shape=(jax.ShapeDtypeStruct((1, 1), F32), jax.ShapeDtypeStruct((s, d), F32)),
                 args=[y, target], sem=("arbitrary",))


def _col_sum(x, name):
    s, n = x.shape
    ts = _row_tile(s)

    def body(x_ref, o_ref):
        @pl.when(pl.program_id(0) == 0)
        def _():
            o_ref[...] = jnp.zeros_like(o_ref)

        o_ref[...] += jnp.sum(x_ref[...].astype(F32), axis=0, keepdims=True)

    return _call(body, name=name, grid=(s // ts,), in_specs=[pl.BlockSpec((ts, n), lambda i: (i, 0))],
                 out_specs=pl.BlockSpec((1, n), lambda i: (0, 0)), out_shape=jax.ShapeDtypeStruct((1, n), F32),
                 args=[x], sem=("arbitrary",))


def _n_variants(n_back):
    return -(-n_back // QG) + 1


def _alibi_bias():
    i = np.arange(QROWS)[:, None]
    j = np.arange((QG + BACK_A) * CHUNK)[None, :]
    dist = np.abs(BACK_A * CHUNK + i - j).astype(np.float32)
    dc = j // CHUNK - i // CHUNK
    valid = (dc >= 0) & (dc <= BACK_A)
    slopes = np.array([2.0 ** (-8.0 * (h + 1) / H_A) for h in range(H_A)], dtype=np.float32)
    bias = -slopes[:, None, None] * dist[None]
    out = [np.where((valid & (j >= (BACK_A - QG * v) * CHUNK))[None], bias, np.float32(NEG_INF))
           for v in range(_n_variants(BACK_A))]
    return jnp.asarray(np.stack(out).astype(np.float32))


def _rel_index_matrix():
    cc = np.arange(SKEW)
    dist = np.where(cc < SKEW - QROWS, BACK_B * CHUNK - cc, BACK_B * CHUNK + SKEW - cc)
    idx = np.clip(dist, -REL_CLIP, REL_CLIP) + REL_CLIP
    m = np.zeros((SKEW, N_REL), np.float32)
    m[cc, idx] = 1.0
    return jnp.asarray(m)


def _toeplitz_bias(vec, carry=None):
    lk = (QG + BACK_B) * CHUNK
    nv = _n_variants(BACK_B)

    def body(v_ref, o_ref):
        xv = jnp.broadcast_to(v_ref[0], (QROWS, SKEW))
        row = lax.broadcasted_iota(jnp.int32, (QROWS, SKEW), 0)
        for bit in range(QROWS.bit_length() - 1):
            xv = jnp.where((row >> bit) & 1 == 1, pltpu.roll(xv, 1 << bit, 1), xv)
        ri = lax.broadcasted_iota(jnp.int32, (QROWS, lk), 0) // CHUNK
        col = lax.broadcasted_iota(jnp.int32, (QROWS, lk), 1)
        ci = col // CHUNK
        valid = (ci - ri >= 0) & (ci - ri <= BACK_B)
        for v in range(nv):
            o_ref[v, 0] = jnp.where(valid & (col >= (BACK_B - QG * v) * CHUNK), xv[:, :lk], NEG_INF)

    return _call(body, name="toeplitz_bias", grid=(H_B,),
                 in_specs=[pl.BlockSpec((1, 1, SKEW), lambda h: (h, 0, 0))],
                 out_specs=pl.BlockSpec((nv, 1, QROWS, lk), lambda h: (0, h, 0, 0)),
                 out_shape=jax.ShapeDtypeStruct((nv, H_B, QROWS, lk), F32), args=[vec], sem=("parallel",),
                 carry=carry)


def _diagonal_sums(dbias):
    lk = dbias.shape[2]

    def body(d_ref, o_ref):
        xp = jnp.concatenate([d_ref[0], jnp.zeros((QROWS, SKEW - lk), F32)], axis=1)
        xv = xp[0:CHUNK]
        for q in range(1, QG):
            xv = xv + pltpu.roll(xp[q * CHUNK:(q + 1) * CHUNK], SKEW - q * CHUNK, 1)
        row = lax.broadcasted_iota(jnp.int32, (CHUNK, SKEW), 0)
        for bit in range(CHUNK.bit_length() - 1):
            xv = jnp.where((row >> bit) & 1 == 1, pltpu.roll(xv, SKEW - (1 << bit), 1), xv)
        o_ref[0] = jnp.sum(xv, axis=0, keepdims=True)

    return _call(body, name="diagonal_sums", grid=(H_B,),
                 in_specs=[pl.BlockSpec((1, QROWS, lk), lambda h: (h, 0, 0))],
                 out_specs=pl.BlockSpec((1, 1, SKEW), lambda h: (h, 0, 0)),
                 out_shape=jax.ShapeDtypeStruct((H_B, 1, SKEW), F32), args=[dbias], sem=("parallel",))


def _attn_common(s, n_back, gqa, q_col, k_col, v_col):
    lk = (QG + n_back) * CHUNK
    pad = n_back * CHUNK
    wide = TPS * LANES
    q_spec = pl.BlockSpec((QROWS, wide), lambda t, g: (g, q_col // TPS + t))
    if gqa:
        k_spec = pl.BlockSpec((s, LANES), lambda t, g: (0, k_col))
        v_spec = pl.BlockSpec((s, LANES), lambda t, g: (0, v_col))
    else:
        k_spec = pl.BlockSpec((s, wide), lambda t, g: (0, k_col // TPS + t))
        v_spec = pl.BlockSpec((s, wide), lambda t, g: (0, v_col // TPS + t))
    last_variant = _n_variants(n_back) - 1
    bias_spec = pl.BlockSpec((None, 2 * TPS, QROWS, lk), lambda t, g: (jnp.minimum(g, last_variant), t, 0, 0))
    tile_spec = pl.BlockSpec((QROWS, wide), lambda t, g: (g, t))
    return lk, pad, q_spec, k_spec, v_spec, bias_spec, tile_spec


def _attention_fwd(proj, bias, sinks, *, n_back, gqa, q_col, k_col, v_col, name, carry=None):
    s = proj.shape[0]
    lk, pad, q_spec, k_spec, v_spec, bias_spec, tile_spec = _attn_common(s, n_back, gqa, q_col, k_col, v_col)
    n_t, n_g = 512 // (TPS * LANES), s // QROWS
    kv_wide = LANES if gqa else TPS * LANES

    def body(*refs):
        if gqa:
            q_ref, k_ref, v_ref, bias_ref, sink_ref, o_ref, l_ref, kpad, vpad = refs
        else:
            q_ref, k_ref, v_ref, bias_ref, o_ref, l_ref, kpad, vpad = refs
        t, g = pl.program_id(0), pl.program_id(1)

        @pl.when(g == 0)
        def _():
            kpad[0:pad, :] = jnp.zeros((pad, kv_wide), BF16)
            vpad[0:pad, :] = jnp.zeros((pad, kv_wide), BF16)
            kpad[pad:, :] = k_ref[...]
            vpad[pad:, :] = v_ref[...]

        start = pl.multiple_of(g * QROWS, QROWS)
        half = lax.broadcasted_iota(jnp.int32, (QROWS, LANES), 1) // HEAD_DIM
        for tt in range(TPS):
            lanes = slice(tt * LANES, (tt + 1) * LANES)
            kv_lanes = slice(0, LANES) if gqa else lanes
            kb = kpad[pl.ds(start, lk), kv_lanes]
            vb = vpad[pl.ds(start, lk), kv_lanes]
            q = q_ref[:, lanes] * (HEAD_DIM ** -0.5)
            if gqa:
                hk = (TPS * t + tt) // 2
                q_rolled = pltpu.roll(q.astype(F32), HEAD_DIM, 1).astype(BF16)
            outs, lses = [], []
            for e in range(2):
                if gqa:
                    kv_half = hk
                    src = jnp.where(hk == e, q, q_rolled)
                else:
                    kv_half = e
                    src = q
                qm = jnp.where(half == kv_half, src, jnp.zeros_like(src))
                sc = _dot_nt(qm, kb) + bias_ref[2 * tt + e]
                m = jnp.max(sc, axis=-1, keepdims=True)
                if gqa:
                    sk = sink_ref[2 * (TPS * t + tt) + e]
                    m = jnp.maximum(m, sk)
                p = jnp.exp(sc - m)
                l = jnp.sum(p, axis=-1, keepdims=True)
                if gqa:
                    l = l + jnp.exp(sk - m)
                pn = p / l
                outs.append(_dot(pn.astype(BF16), vb))
                lses.append(m + jnp.log(l))
            if gqa:
                same = jnp.where(hk == 0, outs[0], outs[1])
                other = jnp.where(hk == 0, outs[1], outs[0])
                o_ref[:, lanes] = jnp.where(half == hk, same, pltpu.roll(other, HEAD_DIM, 1))
            else:
                o_ref[:, lanes] = jnp.where(half == 0, outs[0], outs[1])
            l_ref[:, lanes] = jnp.where(half == 0, lses[0], lses[1])

    in_specs = [q_spec, k_spec, v_spec, bias_spec] + ([SMEM_SPEC] if gqa else [])
    args = [proj, proj, proj, bias] + ([sinks] if gqa else [])
    o_shape = jax.ShapeDtypeStruct((s, 512), F32)
    return _call(body, name=name, grid=(n_t, n_g), in_specs=in_specs, out_specs=(tile_spec, tile_spec),
                 out_shape=(o_shape, o_shape), args=args,
                 scratch=[pltpu.VMEM((s + pad, kv_wide), BF16), pltpu.VMEM((s + pad, kv_wide), BF16)],
                 sem=("arbitrary", "arbitrary"), carry=carry)


def _attention_bwd(proj, bias, sinks, do, lse, *, n_back, gqa, q_col, k_col, v_col, name, carry=None):
    s = proj.shape[0]
    lk, pad, q_spec, k_spec, v_spec, bias_spec, tile_spec = _attn_common(s, n_back, gqa, q_col, k_col, v_col)
    n_t, n_g = 512 // (TPS * LANES), s // QROWS
    kv_wide = LANES if gqa else TPS * LANES

    def body(*refs):
        if gqa:
            (q_ref, k_ref, v_ref, bias_ref, sink_ref, do_ref, l_ref,
             dq_ref, dk_ref, dv_ref, dsink_ref, kpad, vpad, dkpad, dvpad) = refs
        else:
            (q_ref, k_ref, v_ref, bias_ref, do_ref, l_ref,
             dq_ref, dk_ref, dv_ref, dbias_ref, kpad, vpad, dkpad, dvpad) = refs
        t, g = pl.program_id(0), pl.program_id(1)

        @pl.when(g == 0)
        def _():
            kpad[0:pad, :] = jnp.zeros((pad, kv_wide), BF16)
            vpad[0:pad, :] = jnp.zeros((pad, kv_wide), BF16)
            kpad[pad:, :] = k_ref[...]
            vpad[pad:, :] = v_ref[...]
            if gqa:
                dsink_ref[...] = jnp.zeros_like(dsink_ref)
            else:
                dbias_ref[...] = jnp.zeros_like(dbias_ref)

        @pl.when((g == 0) & (t == 0) if gqa else g == 0)
        def _():
            dkpad[...] = jnp.zeros_like(dkpad)
            dvpad[...] = jnp.zeros_like(dvpad)

        start = pl.multiple_of(g * QROWS, QROWS)
        half = lax.broadcasted_iota(jnp.int32, (QROWS, LANES), 1) // HEAD_DIM
        for tt in range(TPS):
            lanes = slice(tt * LANES, (tt + 1) * LANES)
            kv_lanes = slice(0, LANES) if gqa else lanes
            kb = kpad[pl.ds(start, lk), kv_lanes]
            vb = vpad[pl.ds(start, lk), kv_lanes]
            q = q_ref[:, lanes]
            dov = do_ref[:, lanes]
            lv = l_ref[:, lanes]
            if gqa:
                hk = (TPS * t + tt) // 2
                q_rolled = pltpu.roll(q.astype(F32), HEAD_DIM, 1).astype(BF16)
                do_rolled = pltpu.roll(dov, HEAD_DIM, 1)
            dqs = []
            dk_acc = jnp.zeros((lk, LANES), F32)
            dv_acc = jnp.zeros((lk, LANES), F32)
            for e in range(2):
                if gqa:
                    kv_half = hk
                    src = jnp.where(hk == e, q, q_rolled)
                    do_src = jnp.where(hk == e, dov, do_rolled)
                else:
                    kv_half = e
                    src = q
                    do_src = dov
                qm = jnp.where(half == kv_half, src, jnp.zeros_like(src))
                dom = jnp.where(half == kv_half, do_src, 0.0).astype(BF16)
                lcol = jnp.max(jnp.where(half == e, lv, -jnp.inf), axis=-1, keepdims=True)
                sc = _dot_nt(qm * (HEAD_DIM ** -0.5), kb) + bias_ref[2 * tt + e]
                pn = jnp.exp(sc - lcol)
                dp = _dot_nt(dom, vb)
                delta = jnp.sum(pn * dp, axis=-1, keepdims=True)
                ds = pn * (dp - delta)
                if gqa:
                    p_sink = jnp.exp(sink_ref[2 * (TPS * t + tt) + e] - lcol)
                    dsk = -jnp.sum(p_sink * delta, axis=0, keepdims=True)
                    row = 2 * tt + e
                    dsink_ref[0, row:row + 1, :] += jnp.broadcast_to(dsk, (1, LANES))
                else:
                    dbias_ref[2 * tt + e] += ds
                dsb = (ds * (HEAD_DIM ** -0.5)).astype(BF16)
                dqs.append(_dot(dsb, kb))
                dk_acc = dk_acc + _dot_tn(dsb, qm)
                dv_acc = dv_acc + _dot_tn(pn.astype(BF16), dom)
            dkpad[pl.ds(start, lk), kv_lanes] += dk_acc
            dvpad[pl.ds(start, lk), kv_lanes] += dv_acc
            if gqa:
                same = jnp.where(hk == 0, dqs[0], dqs[1])
                other = jnp.where(hk == 0, dqs[1], dqs[0])
                dq_ref[:, lanes] = jnp.where(half == hk, same, pltpu.roll(other, HEAD_DIM, 1)).astype(BF16)
            else:
                dq_ref[:, lanes] = jnp.where(half == 0, dqs[0], dqs[1]).astype(BF16)

        @pl.when((g == n_g - 1) & (t == n_t - 1) if gqa else g == n_g - 1)
        def _():
            dk_ref[...] = dkpad[pad:, :].astype(BF16)
            dv_ref[...] = dvpad[pad:, :].astype(BF16)

    in_specs = [q_spec, k_spec, v_spec, bias_spec] + ([SMEM_SPEC] if gqa else []) + [tile_spec, tile_spec]
    args = [proj, proj, proj, bias] + ([sinks] if gqa else []) + [do, lse]
    if gqa:
        kv_out = pl.BlockSpec((s, LANES), lambda t, g: (0, 0))
        kv_shape = jax.ShapeDtypeStruct((s, LANES), BF16)
        extra_spec = pl.BlockSpec((1, 8, LANES), lambda t, g: (t, 0, 0))
        extra_shape = jax.ShapeDtypeStruct((n_t, 8, LANES), F32)
    else:
        kv_out = pl.BlockSpec((s, kv_wide), lambda t, g: (0, t))
        kv_shape = jax.ShapeDtypeStruct((s, 512), BF16)
        extra_spec = pl.BlockSpec((2 * TPS, QROWS, lk), lambda t, g: (t, 0, 0))
        extra_shape = jax.ShapeDtypeStruct(bias.shape[1:], F32)
    return _call(body, name=name, grid=(n_t, n_g), in_specs=in_specs,
                 out_specs=(tile_spec, kv_out, kv_out, extra_spec),
                 out_shape=(jax.ShapeDtypeStruct((s, 512), BF16), kv_shape, kv_shape, extra_shape), args=args,
                 scratch=[pltpu.VMEM((s + pad, kv_wide), BF16), pltpu.VMEM((s + pad, kv_wide), BF16),
                          pltpu.VMEM((s + pad, kv_wide), F32), pltpu.VMEM((s + pad, kv_wide), F32)],
                 sem=("arbitrary", "arbitrary"), carry=carry)


def _sum_slots(r, name):
    n_slots, rows, k = r.shape

    def body(r_ref, o_ref):
        acc = r_ref[0].astype(F32)
        for j in range(1, n_slots):
            acc = acc + r_ref[j].astype(F32)
        o_ref[...] = acc

    return _call(body, name=name, grid=(k // LANES,),
                 in_specs=[pl.BlockSpec((n_slots, rows, LANES), lambda i: (0, 0, i))],
                 out_specs=pl.BlockSpec((rows, LANES), lambda i: (0, i)),
                 out_shape=jax.ShapeDtypeStruct((rows, k), F32), args=[r], sem=("parallel",))


def _sum_rows8(g):
    n = g.shape[2]

    def body(g_ref, o_ref):
        acc = g_ref[0]
        for j in range(1, N_DEV):
            acc = acc + g_ref[j]
        o_ref[...] = acc

    return pl.pallas_call(
        body, name="sum_small_grads", in_specs=[VMEM_SPEC], out_specs=VMEM_SPEC,
        out_shape=jax.ShapeDtypeStruct((1, n), F32), compiler_params=_params(),
    )(g)


def _ada_weight_grad(sc_t, dmod_cols):
    d = sc_t.shape[0]
    w = dmod_cols.shape[1]
    td = _pick(d, (256, 128))

    def body(sc_ref, dm_ref, o_ref):
        scv = sc_ref[...]
        dmv = dm_ref[...]
        acc = scv[:, 0:1] * dmv[0:1, :]
        for b in range(1, N_DEV):
            acc = acc + scv[:, b:b + 1] * dmv[b:b + 1, :]
        o_ref[...] = acc

    return _call(body, name="ada_weight_grad", grid=(d // td,),
                 in_specs=[pl.BlockSpec((td, N_DEV), lambda i: (i, 0)), pl.BlockSpec((N_DEV, w), lambda i: (0, 0))],
                 out_specs=pl.BlockSpec((td, w), lambda i: (i, 0)), out_shape=jax.ShapeDtypeStruct((d, w), F32),
                 args=[sc_t, dmod_cols], sem=("parallel",))


def _adamw_update(w, gv, m, v):
    nm = ADAM_B1 * m + (1.0 - ADAM_B1) * gv
    nv = ADAM_B2 * v + (1.0 - ADAM_B2) * (gv * gv)
    m_hat = nm / (1.0 - ADAM_B1 ** ADAM_STEP)
    v_hat = nv / (1.0 - ADAM_B2 ** ADAM_STEP)
    return -ADAM_LR * (m_hat / (jnp.sqrt(v_hat) + ADAM_EPS) + ADAM_WD * w), nm, nv


def _adamw(w, g, m, v, name):
    rows, cols = w.shape
    tr = _pick(rows, (256, 176, 128, 88, 64)) if rows > 256 else rows

    def body(w_ref, g_ref, m_ref, v_ref, d_ref, nm_ref, nv_ref):
        d_ref[...], nm_ref[...], nv_ref[...] = _adamw_update(w_ref[...], g_ref[...], m_ref[...], v_ref[...])

    spec = pl.BlockSpec((tr, cols), lambda i: (i, 0))
    shape = jax.ShapeDtypeStruct((rows, cols), F32)
    return _call(body, name=name, grid=(rows // tr,), in_specs=[spec] * 4, out_specs=(spec, spec, spec),
                 out_shape=(shape, shape, shape), args=[w, g, m, v], sem=("parallel",))


def _adamw_from_slots(w, slots, m, v, name):
    n_slots, rows, k = slots.shape

    def body(s_ref, w_ref, m_ref, v_ref, g_ref, d_ref, nm_ref, nv_ref):
        gv = s_ref[0].astype(F32)
        for j in range(1, n_slots):
            gv = gv + s_ref[j].astype(F32)
        g_ref[...] = gv
        d_ref[...], nm_ref[...], nv_ref[...] = _adamw_update(w_ref[...], gv, m_ref[...], v_ref[...])

    tr = rows // 2 if rows % 32 == 0 else rows
    spec = pl.BlockSpec((tr, k), lambda i: (i, 0))
    shape = jax.ShapeDtypeStruct((rows, k), F32)
    return _call(body, name=name, grid=(rows // tr,),
                 in_specs=[pl.BlockSpec((n_slots, tr, k), lambda i: (0, i, 0)), spec, spec, spec],
                 out_specs=(spec, spec, spec, spec), out_shape=(shape, shape, shape, shape),
                 args=[slots, w, m, v], sem=("parallel",))


def _adamw_small(g, w, m, v, sizes):
    n = w.shape[1]
    offs, off = [], 0
    for size in sizes:
        offs.append(off)
        off += size + (-size % LANES)

    def body(g_ref, w_ref, m_ref, v_ref, *out_refs):
        gv = g_ref[:, 0:n]
        dv, nm, nv = _adamw_update(w_ref[...], gv, m_ref[...], v_ref[...])
        for j, (o, size) in enumerate(zip(offs, sizes)):
            for k, val in enumerate((gv, dv, nm, nv)):
                out_refs[4 * j + k][...] = val[:, o:o + size]

    shapes = [jax.ShapeDtypeStruct((1, size), F32) for size in sizes for _ in range(4)]
    return pl.pallas_call(
        body, name="adamw_small", in_specs=[VMEM_SPEC] * 4, out_specs=tuple([VMEM_SPEC] * len(shapes)),
        out_shape=tuple(shapes), compiler_params=_params(),
    )(g, w, m, v)


SMALL = ("b_ada", "g_pre_ffn1", "g_post_ffn1", "g_pre_mix", "b_in", "sinks_a", "rel_bias_b", "g_grp_a",
         "g_grp_b", "b_out", "g_post_mix", "g_pre_ffn2", "g_post_ffn2")
WEIGHTS = ("w_ada", "b_ada", "g_pre_ffn1", "w_gate1", "w_up1", "w_down1", "g_post_ffn1", "g_pre_mix", "w_in",
           "b_in", "sinks_a", "rel_bias_b", "g_grp_a", "g_grp_b", "w_out", "b_out", "g_post_mix", "g_pre_ffn2",
           "w_gate2", "w_up2", "w_down2", "g_post_ffn2")


def kernel(x, c, w_ada, b_ada, g_pre_ffn1, w_gate1, w_up1, w_down1, g_post_ffn1, g_pre_mix, w_in, b_in, sinks_a, rel_bias_b, g_grp_a, g_grp_b, w_out, b_out, g_post_mix, g_pre_ffn2, w_gate2, w_up2, w_down2, g_post_ffn2, loss_target, m_w_ada, m_b_ada, m_g_pre_ffn1, m_w_gate1, m_w_up1, m_w_down1, m_g_post_ffn1, m_g_pre_mix, m_w_in, m_b_in, m_sinks_a, m_rel_bias_b, m_g_grp_a, m_g_grp_b, m_w_out, m_b_out, m_g_post_mix, m_g_pre_ffn2, m_w_gate2, m_w_up2, m_w_down2, m_g_post_ffn2, v_w_ada, v_b_ada, v_g_pre_ffn1, v_w_gate1, v_w_up1, v_w_down1, v_g_post_ffn1, v_g_pre_mix, v_w_in, v_b_in, v_sinks_a, v_rel_bias_b, v_g_grp_a, v_g_grp_b, v_w_out, v_b_out, v_g_post_mix, v_g_pre_ffn2, v_w_gate2, v_w_up2, v_w_down2, v_g_post_ffn2):
    given = dict(locals())
    weights = {n: given[n] for n in WEIGHTS}
    mom_m = {n: given["m_" + n] for n in WEIGHTS}
    mom_v = {n: given["v_" + n] for n in WEIGHTS}

    me = 4 * lax.axis_index("x") + 2 * lax.axis_index("y") + lax.axis_index("c")
    xs = x[0]
    tgt = loss_target[0]
    d_model = xs.shape[1]
    ada_cols = w_ada.shape[2]

    sh = {"wg1": w_gate1[0].T, "wu1": w_up1[0].T, "wd1": w_down1[0], "win": w_in[0].T, "wo": w_out[0],
          "wg2": w_gate2[0].T, "wu2": w_up2[0].T, "wd2": w_down2[0]}
    sh = {k: v.astype(BF16) for k, v in sh.items()}

    def gather(*names):
        return _gather_carry([sh[n] for n in names])

    bias_a = _alibi_bias()
    rel_m = _rel_index_matrix()
    rel_vec = jnp.dot(rel_bias_b[0], rel_m.T, precision=lax.Precision.HIGHEST)
    bias_b, (wg1, wu1) = _toeplitz_bias(rel_vec.reshape(H_B, 1, SKEW), carry=gather("wg1", "wu1"))

    b_cols = lax.dynamic_slice(b_ada, (0, me * ada_cols), (1, ada_cols))
    (sc_all, mod_rows), _ = _ada_forward(c, w_ada[0], b_cols, _Carry([], [], [], lambda *a: None, lambda *a: None))
    mod = mod_rows.reshape(N_MOD, d_model)
    shift1, scale1, gate1, shift2, scale2, gate2, shift3, scale3, gate3 = (mod[i:i + 1] for i in range(N_MOD))

    h1 = _pre_norm(xs, g_pre_ffn1, scale1, shift1, "pre_norm_ffn1")
    (t1, gp1, u1), (wd1,) = _ffn_up(h1, wg1, wu1, "ffn_up_ffn1", carry=gather("wd1"))
    (y1, x1, h2), (win,) = _mm_nn(
        [(u1, wd1)], "ffn_down_ffn1", F32, carry=gather("win"),
        tail=_tail_post_pre(xs, g_post_ffn1, gate1, 0.5, g_pre_mix, scale2, shift2))

    proj, (wo,) = _mm_nt(h2, win, "in_proj", BF16, bias=b_in, carry=gather("wo"))
    sinks = sinks_a[0]
    cfg_a = dict(n_back=BACK_A, gqa=True, q_col=0, k_col=QA // LANES, v_col=(QA + KVA) // LANES)
    cfg_b = dict(n_back=BACK_B, gqa=False, q_col=(QA + 2 * KVA) // LANES, k_col=(QA + 2 * KVA + QB) // LANES,
                 v_col=(QA + 2 * KVA + 2 * QB) // LANES)
    (oa, lse_a), (wg2,) = _attention_fwd(proj, bias_a, sinks, name="attn_a", carry=gather("wg2"), **cfg_a)
    (ob, lse_b), (wu2,) = _attention_fwd(proj, bias_b, None, name="attn_b", carry=gather("wu2"), **cfg_b)
    ycat = _group_norm_cat(oa, ob, g_grp_a, g_grp_b)
    ymix, x2, h3 = _mm_nn([(ycat, wo)], "out_proj", F32, bias=b_out,
                          tail=_tail_post_pre(x1, g_post_mix, gate2, 1.0, g_pre_ffn2, scale3, shift3))

    (t3, gp3, u3), (wd2,) = _ffn_up(h3, wg2, wu2, "ffn_up_ffn2", carry=gather("wd2"))

    def scatter(*grads):
        return _scatter_carry(list(grads))

    slots = {}

    dx3, dy, loss_part, s1 = _mm_nn([(u3, wd2)], "ffn_down_ffn2", None,
                                    tail=_tail_post_loss(x2, tgt, g_post_ffn2, gate3, 0.5))
    da, db = _ffn_down_bwd(dy, wd2, t3, gp3, "ffn_down_bwd_ffn2")
    dwd2 = _mm_tn_pair(u3, dy, "grad_wd_ffn2")
    dwg2 = _mm_tn_pair(da, h3, "grad_wg_ffn2")
    dwu2 = _mm_tn_pair(db, h3, "grad_wu_ffn2")
    (dx2, dymix, s2, s3, s1m, db_out), (slots["wd2"],) = _mm_nn(
        [(da, wg2), (db, wu2)], "ffn_up_bwd_ffn2", None, carry=scatter(dwd2),
        tail=_tail_pre_post_bwd(x2, dx3, ymix, g_pre_ffn2, scale3, g_post_mix, gate2, 1.0))
    sm3 = dict(shift=s3, scale=s2 * g_pre_ffn2, gate=0.5 * g_post_ffn2 * s1,
               g_pre=(1.0 + scale3) * s2, g_post=(0.5 * gate3) * s1)

    dycat = _mm_nt(dymix, wo, "out_proj_bwd", F32)
    dwo = _mm_tn_pair(ycat, dymix, "grad_wo")
    doa, dob, dg_a, dg_b = _group_norm_bwd(dycat, oa, ob, g_grp_a, g_grp_b)
    (dqa, dka, dva, dsink), (slots["wg2"],) = _attention_bwd(
        proj, bias_a, sinks, doa, lse_a, name="attn_a_bwd", carry=scatter(dwg2), **cfg_a)
    (dqb, dkb, dvb, dbias), (slots["wu2"], slots["wo"]) = _attention_bwd(
        proj, bias_b, None, dob, lse_b, name="attn_b_bwd", carry=scatter(dwu2, dwo), **cfg_b)
    dproj = jnp.concatenate([dqa, dka, dva, dqb, dkb, dvb], axis=1)
    db_in = _col_sum(dproj, "grad_b_in")
    dwin = _mm_tn_pair(dproj, h2, "grad_win")
    dx1, dy, s2m, s3m, s1, _ = _mm_nn(
        [(dproj, win)], "in_proj_bwd", None,
        tail=_tail_pre_post_bwd(x1, dx2, y1, g_pre_mix, scale2, g_post_ffn1, gate1, 0.5))
    d_rel = jnp.dot(_diagonal_sums(dbias).reshape(H_B, SKEW), rel_m, precision=lax.Precision.HIGHEST)
    d_sinks = dsink[:, :2 * TPS, 0].reshape(1, H_A)

    (da, db), (slots["win"],) = _ffn_down_bwd(dy, wd1, t1, gp1, "ffn_down_bwd_ffn1", carry=scatter(dwin))
    dwd1 = _mm_tn_pair(u1, dy, "grad_wd_ffn1")
    dwg1, (slots["wd1"],) = _mm_tn_pair(da, h1, "grad_wg_ffn1", carry=scatter(dwd1))
    dwu1, (slots["wg1"],) = _mm_tn_pair(db, h1, "grad_wu_ffn1", carry=scatter(dwg1))
    (dx0, s2, s3), (slots["wu1"],) = _mm_nn(
        [(da, wg1), (db, wu1)], "ffn_up_bwd_ffn1", None, carry=scatter(dwu1),
        tail=_tail_pre_bwd(xs, dx1, g_pre_ffn1, scale1))
    sm1 = dict(shift=s3, scale=s2 * g_pre_ffn1, gate=0.5 * g_post_ffn1 * s1,
               g_pre=(1.0 + scale1) * s2, g_post=(0.5 * gate1) * s1)

    dmod = jnp.concatenate([sm1["shift"], sm1["scale"], sm1["gate"],
                            s3m, s2m * g_pre_mix, g_post_mix * s1m,
                            sm3["shift"], sm3["scale"], sm3["gate"]], axis=1)
    small_parts = {
        "b_ada": dmod, "g_pre_ffn1": sm1["g_pre"], "g_post_ffn1": sm1["g_post"],
        "g_pre_mix": (1.0 + scale2) * s2m, "b_in": db_in, "sinks_a": d_sinks,
        "rel_bias_b": d_rel.reshape(1, H_B * N_REL), "g_grp_a": dg_a, "g_grp_b": dg_b, "b_out": db_out,
        "g_post_mix": gate2 * s1m, "g_pre_ffn2": sm3["g_pre"], "g_post_ffn2": sm3["g_post"]}
    sizes = [small_parts[n].shape[1] for n in SMALL]

    def pack(parts):
        cells = []
        for p in parts:
            cells.append(p)
            if p.shape[1] % LANES:
                cells.append(jnp.zeros((1, -p.shape[1] % LANES), F32))
        return jnp.concatenate(cells, axis=1)

    packed = pack([small_parts[n] for n in SMALL] + [loss_part])
    n_packed = packed.shape[1]
    gathered = _all_gather_small(packed)
    small_sum = _sum_rows8(gathered)
    loss = small_sum[0, n_packed - LANES]
    dmod_cols = lax.dynamic_slice(gathered.reshape(N_DEV, n_packed), (0, me * ada_cols), (N_DEV, ada_cols))
    g_ada = _ada_weight_grad(sc_all.reshape(N_DEV, d_model).T, dmod_cols)

    out_g, out_d, out_m, out_v = {}, {}, {}, {}
    d_, m_, v_ = _adamw(w_ada[0], g_ada, m_w_ada[0], v_w_ada[0], "adamw_w_ada")
    out_g["w_ada"], out_d["w_ada"], out_m["w_ada"], out_v["w_ada"] = g_ada[None], d_[None], m_[None], v_[None]
    for n, key, transposed in (("w_gate1", "wg1", True), ("w_up1", "wu1", True), ("w_down1", "wd1", False),
                               ("w_in", "win", True), ("w_out", "wo", False), ("w_gate2", "wg2", True),
                               ("w_up2", "wu2", True), ("w_down2", "wd2", False)):
        view = (lambda t: t.T) if transposed else (lambda t: t)
        res = _adamw_from_slots(view(weights[n][0]), slots[key], view(mom_m[n][0]), view(mom_v[n][0]),
                                "adamw_" + n)
        out_g[n], out_d[n], out_m[n], out_v[n] = (view(t)[None] for t in res)

    small_out = _adamw_small(small_sum, *(pack([tree[n].reshape(1, -1) for n in SMALL])
                                          for tree in (weights, mom_m, mom_v)), sizes)
    for j, n in enumerate(SMALL):
        shape = weights[n].shape
        out_g[n], out_d[n], out_m[n], out_v[n] = (t.reshape(shape) for t in small_out[4 * j:4 * j + 4])

    return (loss, dx0[None], *[out_g[n] for n in WEIGHTS], *[out_d[n] for n in WEIGHTS],
            *[out_m[n] for n in WEIGHTS], *[out_v[n] for n in WEIGHTS])
```

```python
import numpy as np
import jax
import jax.numpy as jnp
from jax import lax
from jax.experimental import pallas as pl
from jax.experimental.pallas import tpu as pltpu

F32 = jnp.float32
BF16 = jnp.bfloat16
MESH = pl.DeviceIdType.MESH
ANY = pl.BlockSpec(memory_space=pl.ANY)
VMEM_SPEC = pl.BlockSpec(memory_space=pltpu.VMEM)
SMEM_SPEC = pl.BlockSpec(memory_space=pltpu.SMEM)

N_DEV = 8
CHUNK = 64
HEAD_DIM = 64
LANES = 128
H_A, KV_A, H_B = 8, 2, 8
BACK_A, BACK_B = 2, 8
REL_CLIP = 128
N_REL = 2 * REL_CLIP + 1
QA, KVA, QB = H_A * HEAD_DIM, KV_A * HEAD_DIM, H_B * HEAD_DIM
D_IN = QA + 2 * KVA + 3 * QB
N_MOD = 9
EPS = 1e-6
NEG_INF = -1e30
QG = 4
QROWS = QG * CHUNK
TPS = 2
SKEW = 1024
ADAM_LR, ADAM_B1, ADAM_B2, ADAM_EPS, ADAM_WD, ADAM_STEP = 0.001, 0.9, 0.999, 1e-08, 0.01, 10
VMEM_LIMIT = 56 * 2 ** 20


def _pick(n, cands):
    for c in cands:
        if n % c == 0:
            return c
    return n


def _pieces(n, width=2 * LANES):
    return [(lo, min(lo + width, n)) for lo in range(0, n, width)]


def _params(sem=None):
    return pltpu.CompilerParams(dimension_semantics=sem, vmem_limit_bytes=VMEM_LIMIT)


def _dot_nt(a, b):
    return lax.dot_general(a, b, (((1,), (1,)), ((), ())), preferred_element_type=F32)


def _dot_tn(a, b):
    return lax.dot_general(a, b, (((0,), (0,)), ((), ())), preferred_element_type=F32)


def _dot(a, b):
    return jnp.dot(a, b, preferred_element_type=F32)


def _sigmoid(a):
    return 0.5 * (jnp.tanh(0.5 * a) + 1.0)


def _mesh_pos():
    return lax.axis_index("x"), lax.axis_index("y"), lax.axis_index("c")


def _peer(x, y, c, r):
    px = 1 - x if r & 4 else x
    py = 1 - y if r & 2 else y
    pc = 1 - c if r & 1 else c
    return px, py, pc


class _Carry:
    def __init__(self, ins, out_shapes, scratch, start, finish):
        self.ins, self.out_shapes, self.scratch = list(ins), list(out_shapes), list(scratch)
        self.start, self.finish = start, finish


def _call(body, *, name, grid, in_specs, out_specs, out_shape, args, scratch=(), sem=None, carry=None):
    single = not isinstance(out_shape, (tuple, list))
    out_specs = (out_specs,) if single else tuple(out_specs)
    out_shape = (out_shape,) if single else tuple(out_shape)
    if carry is None:
        res = pl.pallas_call(body, name=name, grid=grid, in_specs=list(in_specs), out_specs=out_specs,
                             out_shape=out_shape, scratch_shapes=list(scratch), compiler_params=_params(sem))(*args)
        return res[0] if single else res
    n_in, n_out, n_s = len(in_specs), len(out_shape), len(scratch)
    ci, co = len(carry.ins), len(carry.out_shapes)

    def wrapped(*refs):
        ins, cins = refs[:n_in], refs[n_in:n_in + ci]
        outs = refs[n_in + ci:n_in + ci + n_out]
        couts = refs[n_in + ci + n_out:n_in + ci + n_out + co]
        scr = refs[n_in + ci + n_out + co:n_in + ci + n_out + co + n_s]
        cscr = refs[n_in + ci + n_out + co + n_s:]
        first, last = None, None
        for ax, n in enumerate(grid):
            f, l = pl.program_id(ax) == 0, pl.program_id(ax) == n - 1
            first = f if first is None else first & f
            last = l if last is None else last & l
        pl.when(first)(lambda: carry.start(cins, couts, cscr))
        body(*ins, *outs, *scr)
        pl.when(last)(lambda: carry.finish(cins, couts, cscr))

    res = pl.pallas_call(
        wrapped, name=name, grid=grid, in_specs=list(in_specs) + [ANY] * ci, out_specs=out_specs + (ANY,) * co,
        out_shape=out_shape + tuple(carry.out_shapes), scratch_shapes=list(scratch) + carry.scratch,
        compiler_params=_params(("arbitrary",) * len(grid)))(*args, *carry.ins)
    main = res[:n_out]
    return (main[0] if single else main), res[n_out:]


def _gather_carry(shards):
    n_w = len(shards)
    rows = [s.shape[0] for s in shards]

    def plan(ins, outs, scr):
        send_sems, recv_sems, local_sems = scr
        x, y, c = _mesh_pos()
        me, sibling = (x, y, c), (x, y, 1 - c)
        chips = [(1 - x, y), (x, 1 - y), (1 - x, 1 - y)]

        def block(w, dev):
            start = pl.multiple_of((4 * dev[0] + 2 * dev[1] + dev[2]) * rows[w], 16)
            return outs[w].at[pl.ds(start, rows[w]), :]

        def copy(w, k, dev, to, src=None):
            return pltpu.make_async_remote_copy(
                src_ref=block(w, dev) if src is None else src, dst_ref=block(w, dev),
                send_sem=send_sems.at[w, k], recv_sem=recv_sems.at[w, k], device_id=to, device_id_type=MESH)

        mine = [pltpu.make_async_copy(ins[w], block(w, me), local_sems.at[w]) for w in range(n_w)]
        first = []
        for j, chip in enumerate(chips):
            first += [copy(w, 1 + j, me, (*chip, c), src=ins[w]) for w in range(n_w)]
        first += [copy(w, 0, me, sibling, src=ins[w]) for w in range(n_w)]
        return c, me, sibling, chips, copy, mine, first

    def start(ins, outs, scr):
        _, _, _, _, _, mine, first = plan(ins, outs, scr)
        for cp in mine + first:
            cp.start()

    def finish(ins, outs, scr):
        c, me, sibling, chips, copy, mine, first = plan(ins, outs, scr)
        passed = []
        for j, chip in enumerate(chips):
            for w in range(n_w):
                copy(w, 1 + j, (*chip, c), me).wait_recv()
                cp = copy(w, 4 + j, (*chip, c), sibling)
                cp.start()
                passed.append(cp)
        for w in range(n_w):
            copy(w, 0, sibling, me).wait_recv()
        for j, chip in enumerate(chips):
            for w in range(n_w):
                copy(w, 4 + j, (*chip, 1 - c), me).wait_recv()
        for cp in first + passed:
            cp.wait_send()
        for cp in mine:
            cp.wait()

    return _Carry(
        shards, [jax.ShapeDtypeStruct((N_DEV * s.shape[0], s.shape[1]), s.dtype) for s in shards],
        [pltpu.SemaphoreType.DMA((n_w, N_DEV - 1)), pltpu.SemaphoreType.DMA((n_w, N_DEV - 1)),
         pltpu.SemaphoreType.DMA((n_w,))], start, finish)


def _scatter_carry(parts):
    n_w = len(parts)
    n_chip = N_DEV // 2
    rows = [g.shape[0] // n_chip for g in parts]

    def plan(ins, outs, scr):
        send_sems, recv_sems, local_sems = scr
        x, y, c = _mesh_pos()

        def src(w, chip_index):
            return ins[w].at[pl.ds(pl.multiple_of(chip_index * rows[w], 16), rows[w]), :]

        mine = [pltpu.make_async_copy(src(w, 2 * x + y), outs[w].at[0], local_sems.at[w]) for w in range(n_w)]
        copies = []
        for r in (3, 2, 1):
            px, py, _ = _peer(x, y, c, 2 * r)
            for w in range(n_w):
                copies.append(pltpu.make_async_remote_copy(
                    src_ref=src(w, 2 * px + py), dst_ref=outs[w].at[r], send_sem=send_sems.at[w, r - 1],
                    recv_sem=recv_sems.at[w, r - 1], device_id=(px, py, c), device_id_type=MESH))
        return mine, copies

    def start(ins, outs, scr):
        mine, copies = plan(ins, outs, scr)
        for cp in mine + copies:
            cp.start()

    def finish(ins, outs, scr):
        mine, copies = plan(ins, outs, scr)
        for cp in copies:
            cp.wait_recv()
        for cp in copies:
            cp.wait_send()
        for cp in mine:
            cp.wait()

    return _Carry(
        parts, [jax.ShapeDtypeStruct((n_chip, r, g.shape[1]), g.dtype) for r, g in zip(rows, parts)],
        [pltpu.SemaphoreType.DMA((n_w, n_chip - 1)), pltpu.SemaphoreType.DMA((n_w, n_chip - 1)),
         pltpu.SemaphoreType.DMA((n_w,))], start, finish)


HBM_SPEC = pl.BlockSpec(memory_space=pltpu.HBM)
SEM_SPEC = pl.BlockSpec(memory_space=pltpu.SEMAPHORE)
N_CHIP = N_DEV // 2


def _scatter_copy(part_ref, land_ref, send_sem, recv_sem, r, rows):
    x, y, c = _mesh_pos()
    px, py, _ = _peer(x, y, c, 2 * r)
    src = part_ref.at[pl.ds(pl.multiple_of((2 * px + py) * rows, 16), rows), :]
    return pltpu.make_async_remote_copy(
        src_ref=src, dst_ref=land_ref.at[r - 1], send_sem=send_sem, recv_sem=recv_sem,
        device_id=(px, py, c), device_id_type=MESH)


def _scatter_order(n_w):
    return [(w, r) for r in (3, 2, 1) for w in range(n_w)]


def _scatter_start(parts, name):
    n_w = len(parts)
    rows = [p.shape[0] // N_CHIP for p in parts]
    order = _scatter_order(n_w)
    lands = [pltpu.with_memory_space_constraint(lax.empty((N_CHIP - 1, r, p.shape[1]), p.dtype), pltpu.HBM)
             for r, p in zip(rows, parts)]

    def body(*refs):
        part_refs, land_refs = refs[:n_w], refs[n_w:2 * n_w]
        sems = refs[2 * n_w:2 * n_w + 2 * len(order)]
        token = refs[-1]
        for j, (w, r) in enumerate(order):
            _scatter_copy(part_refs[w], land_refs[w], sems[2 * j], sems[2 * j + 1], r, rows[w]).start()
        token[...] = jnp.zeros_like(token)

    n_sem = 2 * len(order)
    res = pl.pallas_call(
        body, name=name,
        out_shape=(*[pltpu.SemaphoreType.DMA(())] * n_sem, *[pltpu.HBM(p.shape, p.dtype) for p in parts],
                   *[pltpu.HBM(l.shape, l.dtype) for l in lands], jax.ShapeDtypeStruct((8, LANES), F32)),
        in_specs=[HBM_SPEC] * (2 * n_w), out_specs=(*[SEM_SPEC] * n_sem, *[HBM_SPEC] * (2 * n_w), VMEM_SPEC),
        input_output_aliases={i: n_sem + i for i in range(2 * n_w)},
        compiler_params=pltpu.CompilerParams(has_side_effects=pltpu.SideEffectType.DATAFLOW_SIDE_EFFECTING),
    )(*[pltpu.with_memory_space_constraint(p, pltpu.HBM) for p in parts], *lands)
    return (list(res[:n_sem]), list(res[n_sem:n_sem + n_w]), list(res[n_sem + n_w:n_sem + 2 * n_w]), res[-1])


def _scatter_wait(sems, parts, lands, after, name):
    n_w = len(parts)
    rows = [p.shape[0] // N_CHIP for p in parts]
    order = _scatter_order(n_w)

    def body(*refs):
        part_refs, land_refs = refs[:n_w], refs[n_w:2 * n_w]
        sem_refs = refs[2 * n_w:2 * n_w + 2 * len(order)]
        for j, (w, r) in enumerate(order):
            cp = _scatter_copy(part_refs[w], land_refs[w], sem_refs[2 * j], sem_refs[2 * j + 1], r, rows[w])
            cp.wait_send()
            cp.wait_recv()

    res = pl.pallas_call(
        body, name=name,
        out_shape=(*[pltpu.HBM(p.shape, p.dtype) for p in parts], *[pltpu.HBM(l.shape, l.dtype) for l in lands]),
        in_specs=[HBM_SPEC] * (2 * n_w) + [SEM_SPEC] * len(sems) + [ANY],
        out_specs=tuple([HBM_SPEC] * (2 * n_w)),
        input_output_aliases={i: i for i in range(2 * n_w)},
        compiler_params=pltpu.CompilerParams(has_side_effects=pltpu.SideEffectType.DATAFLOW_SIDE_EFFECTING),
    )(*parts, *lands, *sems, after)
    return list(res[:n_w]), list(res[n_w:])


def _ada_forward(c_row, w_ada, b_cols, carry):
    d = c_row.shape[1]
    wcols = w_ada.shape[1]
    ci, co = len(carry.ins), len(carry.out_shapes)

    def body(*refs):
        c_ref, w_ref, b_ref = refs[:3]
        cins = refs[3:3 + ci]
        sc_ref, mod_ref = refs[3 + ci:5 + ci]
        couts = refs[5 + ci:5 + ci + co]
        rows_ref, send_sems, recv_sems = refs[5 + ci + co:8 + ci + co]
        cscr = refs[8 + ci + co:]
        carry.start(cins, couts, cscr)
        x, y, c = _mesh_pos()
        me = 4 * x + 2 * y + c
        cv = c_ref[...]
        sc_ref[me] = cv * _sigmoid(cv)

        sends = []
        for r in range(1, N_DEV):
            px, py, pc = _peer(x, y, c, r)
            cp = pltpu.make_async_remote_copy(
                src_ref=sc_ref.at[me], dst_ref=sc_ref.at[me], send_sem=send_sems.at[0, r - 1],
                recv_sem=recv_sems.at[0, r - 1], device_id=(px, py, pc), device_id_type=MESH)
            cp.start()
            sends.append(cp)
        for r in range(1, N_DEV):
            px, py, pc = _peer(x, y, c, r)
            pid = 4 * px + 2 * py + pc
            pltpu.make_async_remote_copy(
                src_ref=sc_ref.at[pid], dst_ref=sc_ref.at[pid], send_sem=send_sems.at[0, r - 1],
                recv_sem=recv_sems.at[0, r - 1], device_id=(px, py, pc), device_id_type=MESH).wait_recv()
        for cp in sends:
            cp.wait_send()

        sc_all = jnp.concatenate([sc_ref[j] for j in range(N_DEV)], axis=0)
        rows = _dot(sc_all.astype(BF16), w_ref[...].astype(BF16)) + b_ref[...]
        for j in range(N_DEV):
            rows_ref[j] = rows[j:j + 1, :]
        mod_ref[me] = rows_ref[me]

        sends = []
        for r in range(1, N_DEV):
            px, py, pc = _peer(x, y, c, r)
            pid = 4 * px + 2 * py + pc
            cp = pltpu.make_async_remote_copy(
                src_ref=rows_ref.at[pid], dst_ref=mod_ref.at[me], send_sem=send_sems.at[1, r - 1],
                recv_sem=recv_sems.at[1, r - 1], device_id=(px, py, pc), device_id_type=MESH)
            cp.start()
            sends.append(cp)
        for r in range(1, N_DEV):
            px, py, pc = _peer(x, y, c, r)
            pid = 4 * px + 2 * py + pc
            pltpu.make_async_remote_copy(
                src_ref=rows_ref.at[pid], dst_ref=mod_ref.at[pid], send_sem=send_sems.at[1, r - 1],
                recv_sem=recv_sems.at[1, r - 1], device_id=(px, py, pc), device_id_type=MESH).wait_recv()
        for cp in sends:
            cp.wait_send()
        carry.finish(cins, couts, cscr)

    res = pl.pallas_call(
        body, name="ada_forward",
        out_shape=(jax.ShapeDtypeStruct((N_DEV, 1, d), F32), jax.ShapeDtypeStruct((N_DEV, 1, wcols), F32),
                   *carry.out_shapes),
        in_specs=[VMEM_SPEC, VMEM_SPEC, VMEM_SPEC] + [ANY] * ci, out_specs=(VMEM_SPEC, VMEM_SPEC) + (ANY,) * co,
        scratch_shapes=[pltpu.VMEM((N_DEV, 1, wcols), F32), pltpu.SemaphoreType.DMA((2, N_DEV - 1)),
                        pltpu.SemaphoreType.DMA((2, N_DEV - 1))] + carry.scratch,
        compiler_params=_params(),
    )(c_row, w_ada, b_cols, *carry.ins)
    return res[:2], res[2:]


def _all_gather_small(v):
    n = v.shape[1]

    def body(v_ref, out_ref, send_sems, recv_sems):
        x, y, c = _mesh_pos()
        me = 4 * x + 2 * y + c
        out_ref[me] = v_ref[...]
        sends = []
        for r in range(1, N_DEV):
            px, py, pc = _peer(x, y, c, r)
            cp = pltpu.make_async_remote_copy(
                src_ref=v_ref, dst_ref=out_ref.at[me], send_sem=send_sems.at[r - 1],
                recv_sem=recv_sems.at[r - 1], device_id=(px, py, pc), device_id_type=MESH)
            cp.start()
            sends.append(cp)
        for r in range(1, N_DEV):
            px, py, pc = _peer(x, y, c, r)
            pid = 4 * px + 2 * py + pc
            pltpu.make_async_remote_copy(
                src_ref=v_ref, dst_ref=out_ref.at[pid], send_sem=send_sems.at[r - 1],
                recv_sem=recv_sems.at[r - 1], device_id=(px, py, pc), device_id_type=MESH).wait_recv()
        for cp in sends:
            cp.wait_send()

    return pl.pallas_call(
        body, name="all_gather_small",
        out_shape=jax.ShapeDtypeStruct((N_DEV, 1, n), F32),
        in_specs=[VMEM_SPEC], out_specs=VMEM_SPEC,
        scratch_shapes=[pltpu.SemaphoreType.DMA((N_DEV - 1,)), pltpu.SemaphoreType.DMA((N_DEV - 1,))],
        compiler_params=_params(),
    )(v)


def _mm_nt(a, b, name, out_dtype, bias=None, carry=None):
    m, k = a.shape
    n = b.shape[0]
    tm = _pick(m, (512, 256, 128))
    tn = _pick(n, (1408, 1152, 1024, 768, 512, 256, 128))

    def body(*refs):
        acc = _dot_nt(refs[0][...], refs[1][...])
        if bias is not None:
            acc = acc + refs[2][...]
        refs[-1][...] = acc.astype(out_dtype)

    in_specs = [pl.BlockSpec((tm, k), lambda j, i: (i, 0)), pl.BlockSpec((tn, k), lambda j, i: (j, 0))]
    args = [a, b]
    if bias is not None:
        in_specs.append(pl.BlockSpec((1, tn), lambda j, i: (0, j)))
        args.append(bias)
    return _call(body, name=name, grid=(n // tn, m // tm), in_specs=in_specs,
                 out_specs=pl.BlockSpec((tm, tn), lambda j, i: (i, j)),
                 out_shape=jax.ShapeDtypeStruct((m, n), out_dtype), args=args,
                 sem=("parallel", "parallel"), carry=carry)


class _Tail:
    def __init__(self, rows, vecs, outs, fn):
        self.rows, self.vecs, self.outs, self.fn = list(rows), list(vecs), list(outs), fn


def _mm_nn(pairs, name, out_dtype, bias=None, carry=None, tail=None):
    m, k = pairs[0][0].shape
    n = pairs[0][1].shape[1]
    n_p = len(pairs)
    tm = _pick(m, (512, 256, 128))
    tk = k if n_p == 1 else _pick(k, (1408, 1152, 1024, 768, 512, 256, 128))
    nk = k // tk
    n_b = 0 if bias is None else 1
    n_r, n_v = (len(tail.rows), len(tail.vecs)) if tail else (0, 0)
    n_in = 2 * n_p + n_b + n_r + n_v
    n_main = 0 if out_dtype is None else 1

    def finish(acc, refs, first_tile):
        if bias is not None:
            acc = acc + refs[2 * n_p][...]
        outs = refs[n_in:-1]
        if n_main:
            outs[0][...] = acc.astype(out_dtype)
        if tail is None:
            return
        rows = [r[...] for r in refs[2 * n_p + n_b:2 * n_p + n_b + n_r]]
        vecs = [v[...] for v in refs[2 * n_p + n_b + n_r:n_in]]
        vals = tail.fn(acc, rows, vecs)
        for ref, val, (dtype, kind) in zip(outs[n_main:], vals, tail.outs):
            if kind == "row":
                ref[...] = val.astype(dtype)
            else:
                @pl.when(first_tile)
                def _(ref=ref):
                    ref[...] = jnp.zeros_like(ref)

                ref[...] += val

    def body(*refs):
        acc_ref = refs[-1]
        kk, i = pl.program_id(0), pl.program_id(1)
        part = _dot(refs[0][...], refs[1][...])
        for p in range(1, n_p):
            part = part + _dot(refs[2 * p][...], refs[2 * p + 1][...])
        if nk == 1:
            finish(part, refs, i == 0)
            return
        rows = pl.ds(pl.multiple_of(i * tm, tm), tm)

        @pl.when(kk == 0)
        def _():
            acc_ref[rows, :] = part

        if nk > 2:
            @pl.when((kk > 0) & (kk < nk - 1))
            def _():
                acc_ref[rows, :] += part

        @pl.when(kk == nk - 1)
        def _():
            finish(acc_ref[rows, :] + part, refs, i == 0)

    def last_only(kk, i):
        return (jnp.where(kk == nk - 1, i, 0), 0)

    row_spec = pl.BlockSpec((tm, n), last_only)
    vec_spec = pl.BlockSpec((1, n), lambda kk, i: (0, 0))
    in_specs, args = [], []
    for a, b in pairs:
        in_specs += [pl.BlockSpec((tm, tk), lambda kk, i: (i, kk)), pl.BlockSpec((tk, n), lambda kk, i: (kk, 0))]
        args += [a, b]
    if bias is not None:
        in_specs.append(vec_spec)
        args.append(bias)
    out_specs = [row_spec] * n_main
    out_shape = [jax.ShapeDtypeStruct((m, n), out_dtype)] if n_main else []
    if tail:
        in_specs += [row_spec] * n_r + [vec_spec] * n_v
        args += tail.rows + tail.vecs
        for dtype, kind in tail.outs:
            if kind == "row":
                out_specs.append(row_spec)
                out_shape.append(jax.ShapeDtypeStruct((m, n), dtype))
            else:
                width = n if kind == "sum" else 1
                out_specs.append(pl.BlockSpec((1, width), lambda kk, i: (0, 0)))
                out_shape.append(jax.ShapeDtypeStruct((1, width), dtype))
    if tail is None:
        out_specs, out_shape = out_specs[0], out_shape[0]
    return _call(body, name=name, grid=(nk, m // tm), in_specs=in_specs, out_specs=out_specs,
                 out_shape=out_shape, args=args,
                 scratch=[pltpu.VMEM((m, n) if nk > 1 else (8, LANES), F32)],
                 sem=("arbitrary", "arbitrary"), carry=carry)


def _rms(v):
    return lax.rsqrt(jnp.mean(v * v, axis=-1, keepdims=True) + EPS)


def _col(v):
    return jnp.sum(v, axis=0, keepdims=True)


def _tail_post_pre(x, g_post, gate, weight, g_pre, scale, shift):
    def fn(y, rows, vecs):
        (xv,), (gp, gt, g, sc, sh) = rows, vecs
        xo = xv + (weight * gt) * ((y * _rms(y)) * gp)
        return xo, ((xo * _rms(xo)) * g) * (1.0 + sc) + sh

    return _Tail([x], [g_post, gate, g_pre, scale, shift], [(F32, "row"), (BF16, "row")], fn)


def _tail_post_loss(x, target, g, gate, weight):
    def fn(y, rows, vecs):
        (xv, tv), (gv, gt) = rows, vecs
        r = _rms(y)
        yn = y * r
        err = (xv + (weight * gt) * (yn * gv)) - tv
        do = err * (1.0 / y.shape[1])
        dyn = do * ((weight * gt) * gv)
        dy = r * (dyn - yn * jnp.mean(dyn * yn, axis=-1, keepdims=True))
        return do, dy, 0.5 * _col(jnp.mean(err * err, axis=-1, keepdims=True)), _col(do * yn)

    return _Tail([x, target], [g, gate], [(F32, "row"), (BF16, "row"), (F32, "one"), (F32, "sum")], fn)


def _tail_pre_bwd(x, dres, g_pre, scale):
    def fn(dh, rows, vecs):
        (xv, dr), (g, sc) = rows, vecs
        r = _rms(xv)
        n = xv * r
        dn = dh * (g * (1.0 + sc))
        return dr + r * (dn - n * jnp.mean(dn * n, axis=-1, keepdims=True)), _col(dh * n), _col(dh)

    return _Tail([x, dres], [g_pre, scale], [(F32, "row"), (F32, "sum"), (F32, "sum")], fn)


def _tail_pre_post_bwd(x, dres, y, g_pre, scale, g_post, gate, weight):
    def fn(dh, rows, vecs):
        (xv, dr, yv), (g, sc, gp, gt) = rows, vecs
        r = _rms(xv)
        n = xv * r
        dn = dh * (g * (1.0 + sc))
        dx = dr + r * (dn - n * jnp.mean(dn * n, axis=-1, keepdims=True))
        ry = _rms(yv)
        yn = yv * ry
        dyn = dx * ((weight * gt) * gp)
        dy = ry * (dyn - yn * jnp.mean(dyn * yn, axis=-1, keepdims=True))
        return dx, dy, _col(dh * n), _col(dh), _col(dx * yn), _col(dy)

    return _Tail([x, dres, y], [g_pre, scale, g_post, gate],
                 [(F32, "row"), (BF16, "row")] + [(F32, "sum")] * 4, fn)


def _mm_tn(a, b, name, out_dtype=BF16, carry=None):
    k, m = a.shape
    n = b.shape[1]
    tm = _pick(m, (1408, 1152, 1024, 768, 512, 256, 128))
    tk = _pick(k, (512, 256, 128))
    nk = k // tk

    def body(a_ref, b_ref, o_ref, acc_ref):
        kk = pl.program_id(1)

        @pl.when(kk == 0)
        def _():
            acc_ref[...] = jnp.zeros_like(acc_ref)

        acc_ref[...] += _dot_tn(a_ref[...], b_ref[...])

        @pl.when(kk == nk - 1)
        def _():
            o_ref[...] = acc_ref[...].astype(out_dtype)

    return _call(body, name=name, grid=(m // tm, nk),
                 in_specs=[pl.BlockSpec((tk, tm), lambda i, kk: (kk, i)), pl.BlockSpec((tk, n), lambda i, kk: (kk, 0))],
                 out_specs=pl.BlockSpec((tm, n), lambda i, kk: (i, 0)),
                 out_shape=jax.ShapeDtypeStruct((m, n), out_dtype), args=[a, b],
                 scratch=[pltpu.VMEM((tm, n), F32)], sem=("parallel", "arbitrary"), carry=carry)


def _mm_tn_pair(a, b, name, carry=None):
    k, m = a.shape
    n = b.shape[1]
    rows = m // N_DEV
    n_chip = N_DEV // 2
    tm = 4 * rows
    tk = _pick(k, (1024, 512, 256, 128))
    nk = k // tk

    def body(a_ref, b_ref, p_ref, acc_ref, keep_ref, send_ref, land_ref, send_sems, recv_sems):
        i, kk = pl.program_id(0), pl.program_id(1)
        x, y, c = _mesh_pos()

        def push(chip):
            return pltpu.make_async_remote_copy(
                src_ref=send_ref.at[chip], dst_ref=land_ref.at[chip], send_sem=send_sems.at[chip],
                recv_sem=recv_sems.at[chip], device_id=(x, y, 1 - c), device_id_type=MESH)

        if nk == 1:
            acc = _dot_tn(a_ref[...], b_ref[...])
        else:
            @pl.when(kk == 0)
            def _():
                acc_ref[...] = jnp.zeros_like(acc_ref)

            acc_ref[...] += _dot_tn(a_ref[...], b_ref[...])
            acc = acc_ref

        for t in range(2):
            @pl.when((kk == nk - 1) & (i == t))
            def _(t=t):
                for ob in range(4):
                    chip, core = 2 * t + ob // 2, ob % 2
                    blk = acc[ob * rows:(ob + 1) * rows, :]

                    @pl.when(c == core)
                    def _(chip=chip, blk=blk):
                        keep_ref[chip] = blk

                    @pl.when(c != core)
                    def _(chip=chip, blk=blk):
                        send_ref[chip] = blk.astype(BF16)
                        push(chip).start()

        @pl.when((kk == nk - 1) & (i == 1))
        def _():
            for chip in range(n_chip):
                push(chip).wait_recv()
                p_ref[chip * rows:(chip + 1) * rows, :] = (
                    keep_ref[chip] + land_ref[chip].astype(F32)).astype(BF16)
            for chip in range(n_chip):
                push(chip).wait_send()

    return _call(body, name=name, grid=(2, nk),
                 in_specs=[pl.BlockSpec((tk, tm), lambda i, kk: (kk, i)), pl.BlockSpec((tk, n), lambda i, kk: (kk, 0))],
                 out_specs=pl.BlockSpec((n_chip * rows, n), lambda i, kk: (0, 0)),
                 out_shape=jax.ShapeDtypeStruct((n_chip * rows, n), BF16), args=[a, b],
                 scratch=[pltpu.VMEM((tm, n) if nk > 1 else (8, LANES), F32), pltpu.VMEM((n_chip, rows, n), F32),
                          pltpu.VMEM((n_chip, rows, n), BF16), pltpu.VMEM((n_chip, rows, n), BF16),
                          pltpu.SemaphoreType.DMA((n_chip,)), pltpu.SemaphoreType.DMA((n_chip,))],
                 sem=("arbitrary", "arbitrary"), carry=carry)


def _ffn_up(h, wg_t, wu_t, name, carry=None):
    s, d = h.shape
    f = wg_t.shape[0]
    tm = _pick(s, (512, 256, 128))
    tf = _pick(f, (1408, 1024, 512, 256, 128))

    def body(h_ref, wg_ref, wu_ref, a_ref, b_ref, u_ref):
        hh = h_ref[...]
        for lo, hi in _pieces(tf):
            a = _dot_nt(hh, wg_ref[lo:hi, :])
            b = _dot_nt(hh, wu_ref[lo:hi, :])
            a_ref[:, lo:hi] = a.astype(BF16)
            b_ref[:, lo:hi] = b.astype(BF16)
            u_ref[:, lo:hi] = ((a * _sigmoid(a)) * b).astype(BF16)

    w_spec = pl.BlockSpec((tf, d), lambda j, i: (j, 0))
    o_spec = pl.BlockSpec((tm, tf), lambda j, i: (i, j))
    o_shape = jax.ShapeDtypeStruct((s, f), BF16)
    return _call(body, name=name, grid=(f // tf, s // tm),
                 in_specs=[pl.BlockSpec((tm, d), lambda j, i: (i, 0)), w_spec, w_spec],
                 out_specs=(o_spec, o_spec, o_spec), out_shape=(o_shape, o_shape, o_shape),
                 args=[h, wg_t, wu_t], sem=("parallel", "parallel"), carry=carry)


def _ffn_down_bwd(dy, wd, a, b, name, carry=None):
    s, d = dy.shape
    f = wd.shape[0]
    tm = _pick(s, (512, 256, 128))
    tf = _pick(f, (1408, 1024, 512, 256, 128))

    def body(dy_ref, wd_ref, a_ref, b_ref, da_ref, db_ref):
        dyv = dy_ref[...]
        for lo, hi in _pieces(tf):
            du = _dot_nt(dyv, wd_ref[lo:hi, :])
            a = a_ref[:, lo:hi].astype(F32)
            b = b_ref[:, lo:hi].astype(F32)
            sig = _sigmoid(a)
            da_ref[:, lo:hi] = (du * b * (sig * (1.0 + a * (1.0 - sig)))).astype(BF16)
            db_ref[:, lo:hi] = (du * (a * sig)).astype(BF16)

    t_spec = pl.BlockSpec((tm, tf), lambda j, i: (i, j))
    o_shape = jax.ShapeDtypeStruct((s, f), BF16)
    return _call(body, name=name, grid=(f // tf, s // tm),
                 in_specs=[pl.BlockSpec((tm, d), lambda j, i: (i, 0)), pl.BlockSpec((tf, d), lambda j, i: (j, 0)),
                           t_spec, t_spec],
                 out_specs=(t_spec, t_spec), out_shape=(o_shape, o_shape), args=[dy, wd, a, b],
                 sem=("parallel", "parallel"), carry=carry)


def _row_tile(s):
    return _pick(s, (256, 128, 64))


def _vec_spec(d):
    return pl.BlockSpec((1, d), lambda i: (0, 0))


def _pre_norm(x, g, scale, shift, name):
    s, d = x.shape
    ts = _row_tile(s)

    def body(x_ref, g_ref, sc_ref, sh_ref, h_ref):
        xv = x_ref[...]
        r = lax.rsqrt(jnp.mean(xv * xv, axis=-1, keepdims=True) + EPS)
        h_ref[...] = (((xv * r) * g_ref[...]) * (1.0 + sc_ref[...]) + sh_ref[...]).astype(BF16)

    row = pl.BlockSpec((ts, d), lambda i: (i, 0))
    return _call(body, name=name, grid=(s // ts,), in_specs=[row, _vec_spec(d), _vec_spec(d), _vec_spec(d)],
                 out_specs=row, out_shape=jax.ShapeDtypeStruct((s, d), BF16), args=[x, g, scale, shift],
                 sem=("parallel",))


def _post_norm_residual(x, y, g, gate, weight, name):
    s, d = x.shape
    ts = _row_tile(s)

    def body(x_ref, y_ref, g_ref, gate_ref, o_ref):
        yv = y_ref[...]
        r = lax.rsqrt(jnp.mean(yv * yv, axis=-1, keepdims=True) + EPS)
        o_ref[...] = x_ref[...] + (weight * gate_ref[...]) * ((yv * r) * g_ref[...])

    row = pl.BlockSpec((ts, d), lambda i: (i, 0))
    return _call(body, name=name, grid=(s // ts,), in_specs=[row, row, _vec_spec(d), _vec_spec(d)],
                 out_specs=row, out_shape=jax.ShapeDtypeStruct((s, d), F32), args=[x, y, g, gate],
                 sem=("parallel",))


def _post_norm_bwd(dout, y, g, gate, weight, name):
    s, d = y.shape
    ts = _row_tile(s)

    def body(do_ref, y_ref, g_ref, gate_ref, dy_ref, s1_ref, cs_ref):
        @pl.when(pl.program_id(0) == 0)
        def _():
            s1_ref[...] = jnp.zeros_like(s1_ref)
            cs_ref[...] = jnp.zeros_like(cs_ref)

        yv = y_ref[...]
        do = do_ref[...]
        r = lax.rsqrt(jnp.mean(yv * yv, axis=-1, keepdims=True) + EPS)
        yn = yv * r
        dyn = do * ((weight * gate_ref[...]) * g_ref[...])
        dy = r * (dyn - yn * jnp.mean(dyn * yn, axis=-1, keepdims=True))
        dy_ref[...] = dy.astype(BF16)
        s1_ref[...] += jnp.sum(do * yn, axis=0, keepdims=True)
        cs_ref[...] += jnp.sum(dy, axis=0, keepdims=True)

    row = pl.BlockSpec((ts, d), lambda i: (i, 0))
    vec = jax.ShapeDtypeStruct((1, d), F32)
    return _call(body, name=name, grid=(s // ts,), in_specs=[row, row, _vec_spec(d), _vec_spec(d)],
                 out_specs=(row, _vec_spec(d), _vec_spec(d)),
                 out_shape=(jax.ShapeDtypeStruct((s, d), BF16), vec, vec), args=[dout, y, g, gate],
                 sem=("arbitrary",))


def _pre_norm_bwd(dh, x, g, scale, dres, name):
    s, d = x.shape
    ts = _row_tile(s)

    def body(dh_ref, x_ref, g_ref, sc_ref, dr_ref, dx_ref, s2_ref, s3_ref):
        @pl.when(pl.program_id(0) == 0)
        def _():
            s2_ref[...] = jnp.zeros_like(s2_ref)
            s3_ref[...] = jnp.zeros_like(s3_ref)

        xv = x_ref[...]
        dh = dh_ref[...]
        r = lax.rsqrt(jnp.mean(xv * xv, axis=-1, keepdims=True) + EPS)
        n = xv * r
        dn = dh * (g_ref[...] * (1.0 + sc_ref[...]))
        dx_ref[...] = dr_ref[...] + r * (dn - n * jnp.mean(dn * n, axis=-1, keepdims=True))
        s2_ref[...] += jnp.sum(dh * n, axis=0, keepdims=True)
        s3_ref[...] += jnp.sum(dh, axis=0, keepdims=True)

    row = pl.BlockSpec((ts, d), lambda i: (i, 0))
    vec = jax.ShapeDtypeStruct((1, d), F32)
    return _call(body, name=name, grid=(s // ts,), in_specs=[row, row, _vec_spec(d), _vec_spec(d), row],
                 out_specs=(row, _vec_spec(d), _vec_spec(d)),
                 out_shape=(jax.ShapeDtypeStruct((s, d), F32), vec, vec), args=[dh, x, g, scale, dres],
                 sem=("arbitrary",))


def _post_pre_norm(x, y, g_post, gate, weight, g_pre, scale, shift, name):
    s, d = x.shape
    ts = _row_tile(s)

    def body(x_ref, y_ref, gp_ref, gate_ref, g_ref, sc_ref, sh_ref, o_ref, h_ref):
        yv = y_ref[...]
        r = lax.rsqrt(jnp.mean(yv * yv, axis=-1, keepdims=True) + EPS)
        xv = x_ref[...] + (weight * gate_ref[...]) * ((yv * r) * gp_ref[...])
        o_ref[...] = xv
        r2 = lax.rsqrt(jnp.mean(xv * xv, axis=-1, keepdims=True) + EPS)
        h_ref[...] = (((xv * r2) * g_ref[...]) * (1.0 + sc_ref[...]) + sh_ref[...]).astype(BF16)

    row = pl.BlockSpec((ts, d), lambda i: (i, 0))
    return _call(body, name=name, grid=(s // ts,), in_specs=[row, row] + [_vec_spec(d)] * 5,
                 out_specs=(row, row),
                 out_shape=(jax.ShapeDtypeStruct((s, d), F32), jax.ShapeDtypeStruct((s, d), BF16)),
                 args=[x, y, g_post, gate, g_pre, scale, shift], sem=("parallel",))


def _post_norm_loss_bwd(x, y, g, gate, weight, target, name):
    s, d = y.shape
    ts = _row_tile(s)

    def body(x_ref, y_ref, g_ref, gate_ref, t_ref, dx_ref, dy_ref, l_ref, s1_ref):
        @pl.when(pl.program_id(0) == 0)
        def _():
            l_ref[...] = jnp.zeros_like(l_ref)
            s1_ref[...] = jnp.zeros_like(s1_ref)

        yv = y_ref[...]
        r = lax.rsqrt(jnp.mean(yv * yv, axis=-1, keepdims=True) + EPS)
        yn = yv * r
        err = (x_ref[...] + (weight * gate_ref[...]) * (yn * g_ref[...])) - t_ref[...]
        do = err * (1.0 / d)
        dx_ref[...] = do
        l_ref[...] += 0.5 * jnp.sum(jnp.mean(err * err, axis=-1, keepdims=True), axis=0, keepdims=True)
        dyn = do * ((weight * gate_ref[...]) * g_ref[...])
        dy_ref[...] = (r * (dyn - yn * jnp.mean(dyn * yn, axis=-1, keepdims=True))).astype(BF16)
        s1_ref[...] += jnp.sum(do * yn, axis=0, keepdims=True)

    row = pl.BlockSpec((ts, d), lambda i: (i, 0))
    return _call(body, name=name, grid=(s // ts,), in_specs=[row, row, _vec_spec(d), _vec_spec(d), row],
                 out_specs=(row, row, pl.BlockSpec((1, 1), lambda i: (0, 0)), _vec_spec(d)),
                 out_shape=(jax.ShapeDtypeStruct((s, d), F32), jax.ShapeDtypeStruct((s, d), BF16),
                            jax.ShapeDtypeStruct((1, 1), F32), jax.ShapeDtypeStruct((1, d), F32)),
                 args=[x, y, g, gate, target], sem=("arbitrary",))


def _pre_post_norm_bwd(dh, x, g_pre, scale, dres, y, g_post, gate, weight, name):
    s, d = x.shape
    ts = _row_tile(s)

    def body(dh_ref, x_ref, g_ref, sc_ref, dr_ref, y_ref, gp_ref, gate_ref,
             dx_ref, dy_ref, s2_ref, s3_ref, s1_ref, cs_ref):
        @pl.when(pl.program_id(0) == 0)
        def _():
            for ref in (s2_ref, s3_ref, s1_ref, cs_ref):
                ref[...] = jnp.zeros_like(ref)

        xv = x_ref[...]
        dh = dh_ref[...]
        r = lax.rsqrt(jnp.mean(xv * xv, axis=-1, keepdims=True) + EPS)
        n = xv * r
        dn = dh * (g_ref[...] * (1.0 + sc_ref[...]))
        dx = dr_ref[...] + r * (dn - n * jnp.mean(dn * n, axis=-1, keepdims=True))
        dx_ref[...] = dx
        s2_ref[...] += jnp.sum(dh * n, axis=0, keepdims=True)
        s3_ref[...] += jnp.sum(dh, axis=0, keepdims=True)
        yv = y_ref[...]
        ry = lax.rsqrt(jnp.mean(yv * yv, axis=-1, keepdims=True) + EPS)
        yn = yv * ry
        dyn = dx * ((weight * gate_ref[...]) * gp_ref[...])
        dy = ry * (dyn - yn * jnp.mean(dyn * yn, axis=-1, keepdims=True))
        dy_ref[...] = dy.astype(BF16)
        s1_ref[...] += jnp.sum(dx * yn, axis=0, keepdims=True)
        cs_ref[...] += jnp.sum(dy, axis=0, keepdims=True)

    row = pl.BlockSpec((ts, d), lambda i: (i, 0))
    vec = jax.ShapeDtypeStruct((1, d), F32)
    return _call(body, name=name, grid=(s // ts,),
                 in_specs=[row, row, _vec_spec(d), _vec_spec(d), row, row, _vec_spec(d), _vec_spec(d)],
                 out_specs=(row, row) + (_vec_spec(d),) * 4,
                 out_shape=(jax.ShapeDtypeStruct((s, d), F32), jax.ShapeDtypeStruct((s, d), BF16), vec, vec, vec, vec),
                 args=[dh, x, g_pre, scale, dres, y, g_post, gate], sem=("arbitrary",))


def _group_norm_cat(oa, ob, ga, gb):
    s = oa.shape[0]
    ts = _row_tile(s)

    def body(oa_ref, ob_ref, ga_ref, gb_ref, y_ref):
        for o_ref, g_ref, lo, w in ((oa_ref, ga_ref, 0, QA), (ob_ref, gb_ref, QA, QB)):
            ov = o_ref[...]
            r = lax.rsqrt(jnp.mean(ov * ov, axis=-1, keepdims=True) + EPS)
            y_ref[:, lo:lo + w] = ((ov * r) * g_ref[...]).astype(BF16)

    return _call(body, name="group_norm_cat", grid=(s // ts,),
                 in_specs=[pl.BlockSpec((ts, QA), lambda i: (i, 0)), pl.BlockSpec((ts, QB), lambda i: (i, 0)),
                           _vec_spec(QA), _vec_spec(QB)],
                 out_specs=pl.BlockSpec((ts, QA + QB), lambda i: (i, 0)),
                 out_shape=jax.ShapeDtypeStruct((s, QA + QB), BF16), args=[oa, ob, ga, gb], sem=("parallel",))


def _group_norm_bwd(dy, oa, ob, ga, gb):
    s = oa.shape[0]
    ts = _row_tile(s)

    def body(dy_ref, oa_ref, ob_ref, ga_ref, gb_ref, doa_ref, dob_ref, dga_ref, dgb_ref):
        @pl.when(pl.program_id(0) == 0)
        def _():
            dga_ref[...] = jnp.zeros_like(dga_ref)
            dgb_ref[...] = jnp.zeros_like(dgb_ref)

        for o_ref, g_ref, do_ref, dg_ref, lo, w in ((oa_ref, ga_ref, doa_ref, dga_ref, 0, QA),
                                                    (ob_ref, gb_ref, dob_ref, dgb_ref, QA, QB)):
            ov = o_ref[...]
            dyv = dy_ref[:, lo:lo + w]
            r = lax.rsqrt(jnp.mean(ov * ov, axis=-1, keepdims=True) + EPS)
            n = ov * r
            dn = dyv * g_ref[...]
            do_ref[...] = r * (dn - n * jnp.mean(dn * n, axis=-1, keepdims=True))
            dg_ref[...] += jnp.sum(dyv * n, axis=0, keepdims=True)

    ra = pl.BlockSpec((ts, QA), lambda i: (i, 0))
    rb = pl.BlockSpec((ts, QB), lambda i: (i, 0))
    return _call(body, name="group_norm_bwd", grid=(s // ts,),
                 in_specs=[pl.BlockSpec((ts, QA + QB), lambda i: (i, 0)), ra, rb, _vec_spec(QA), _vec_spec(QB)],
                 out_specs=(ra, rb, _vec_spec(QA), _vec_spec(QB)),
                 out_shape=(jax.ShapeDtypeStruct((s, QA), F32), jax.ShapeDtypeStruct((s, QB), F32),
                            jax.ShapeDtypeStruct((1, QA), F32), jax.ShapeDtypeStruct((1, QB), F32)),
                 args=[dy, oa, ob, ga, gb], sem=("arbitrary",))


def _loss_and_grad(y, target):
    s, d = y.shape
    ts = _row_tile(s)

    def body(y_ref, t_ref, l_ref, g_ref):
        @pl.when(pl.program_id(0) == 0)
        def _():
            l_ref[...] = jnp.zeros_like(l_ref)

        err = y_ref[...] - t_ref[...]
        g_ref[...] = err * (1.0 / d)
        row = jnp.mean(err * err, axis=-1, keepdims=True)
        l_ref[...] += 0.5 * jnp.sum(row, axis=0, keepdims=True)

    row = pl.BlockSpec((ts, d), lambda i: (i, 0))
    return _call(body, name="loss_and_grad", grid=(s // ts,), in_specs=[row, row],
                 out_specs=(pl.BlockSpec((1, 1), lambda i: (0, 0)), row),
                 out_shape=(jax.ShapeDtypeStruct((1, 1), F32), jax.ShapeDtypeStruct((s, d), F32)),
                 args=[y, target], sem=("arbitrary",))


def _col_sum(x, name):
    s, n = x.shape
    ts = _row_tile(s)

    def body(x_ref, o_ref):
        @pl.when(pl.program_id(0) == 0)
        def _():
            o_ref[...] = jnp.zeros_like(o_ref)

        o_ref[...] += jnp.sum(x_ref[...].astype(F32), axis=0, keepdims=True)

    return _call(body, name=name, grid=(s // ts,), in_specs=[pl.BlockSpec((ts, n), lambda i: (i, 0))],
                 out_specs=pl.BlockSpec((1, n), lambda i: (0, 0)), out_shape=jax.ShapeDtypeStruct((1, n), F32),
                 args=[x], sem=("arbitrary",))


def _n_variants(n_back):
    return -(-n_back // QG) + 1


def _alibi_bias():
    i = np.arange(QROWS)[:, None]
    j = np.arange((QG + BACK_A) * CHUNK)[None, :]
    dist = np.abs(BACK_A * CHUNK + i - j).astype(np.float32)
    dc = j // CHUNK - i // CHUNK
    valid = (dc >= 0) & (dc <= BACK_A)
    slopes = np.array([2.0 ** (-8.0 * (h + 1) / H_A) for h in range(H_A)], dtype=np.float32)
    bias = -slopes[:, None, None] * dist[None]
    out = [np.where((valid & (j >= (BACK_A - QG * v) * CHUNK))[None], bias, np.float32(NEG_INF))
           for v in range(_n_variants(BACK_A))]
    return jnp.asarray(np.stack(out).astype(np.float32))


def _rel_index_matrix():
    cc = np.arange(SKEW)
    dist = np.where(cc < SKEW - QROWS, BACK_B * CHUNK - cc, BACK_B * CHUNK + SKEW - cc)
    idx = np.clip(dist, -REL_CLIP, REL_CLIP) + REL_CLIP
    m = np.zeros((SKEW, N_REL), np.float32)
    m[cc, idx] = 1.0
    return jnp.asarray(m)


def _toeplitz_bias(vec, carry=None):
    lk = (QG + BACK_B) * CHUNK
    nv = _n_variants(BACK_B)

    def body(v_ref, o_ref):
        xv = jnp.broadcast_to(v_ref[0], (QROWS, SKEW))
        row = lax.broadcasted_iota(jnp.int32, (QROWS, SKEW), 0)
        for bit in range(QROWS.bit_length() - 1):
            xv = jnp.where((row >> bit) & 1 == 1, pltpu.roll(xv, 1 << bit, 1), xv)
        ri = lax.broadcasted_iota(jnp.int32, (QROWS, lk), 0) // CHUNK
        col = lax.broadcasted_iota(jnp.int32, (QROWS, lk), 1)
        ci = col // CHUNK
        valid = (ci - ri >= 0) & (ci - ri <= BACK_B)
        for v in range(nv):
            o_ref[v, 0] = jnp.where(valid & (col >= (BACK_B - QG * v) * CHUNK), xv[:, :lk], NEG_INF)

    return _call(body, name="toeplitz_bias", grid=(H_B,),
                 in_specs=[pl.BlockSpec((1, 1, SKEW), lambda h: (h, 0, 0))],
                 out_specs=pl.BlockSpec((nv, 1, QROWS, lk), lambda h: (0, h, 0, 0)),
                 out_shape=jax.ShapeDtypeStruct((nv, H_B, QROWS, lk), F32), args=[vec], sem=("parallel",),
                 carry=carry)


def _diagonal_sums(dbias):
    lk = dbias.shape[2]

    def body(d_ref, o_ref):
        xp = jnp.concatenate([d_ref[0], jnp.zeros((QROWS, SKEW - lk), F32)], axis=1)
        xv = xp[0:CHUNK]
        for q in range(1, QG):
            xv = xv + pltpu.roll(xp[q * CHUNK:(q + 1) * CHUNK], SKEW - q * CHUNK, 1)
        row = lax.broadcasted_iota(jnp.int32, (CHUNK, SKEW), 0)
        for bit in range(CHUNK.bit_length() - 1):
            xv = jnp.where((row >> bit) & 1 == 1, pltpu.roll(xv, SKEW - (1 << bit), 1), xv)
        o_ref[0] = jnp.sum(xv, axis=0, keepdims=True)

    return _call(body, name="diagonal_sums", grid=(H_B,),
                 in_specs=[pl.BlockSpec((1, QROWS, lk), lambda h: (h, 0, 0))],
                 out_specs=pl.BlockSpec((1, 1, SKEW), lambda h: (h, 0, 0)),
                 out_shape=jax.ShapeDtypeStruct((H_B, 1, SKEW), F32), args=[dbias], sem=("parallel",))


def _attn_common(s, n_back, gqa, q_col, k_col, v_col):
    lk = (QG + n_back) * CHUNK
    pad = n_back * CHUNK
    wide = TPS * LANES
    q_spec = pl.BlockSpec((QROWS, wide), lambda t, g: (g, q_col // TPS + t))
    if gqa:
        k_spec = pl.BlockSpec((s, LANES), lambda t, g: (0, k_col))
        v_spec = pl.BlockSpec((s, LANES), lambda t, g: (0, v_col))
    else:
        k_spec = pl.BlockSpec((s, wide), lambda t, g: (0, k_col // TPS + t))
        v_spec = pl.BlockSpec((s, wide), lambda t, g: (0, v_col // TPS + t))
    last_variant = _n_variants(n_back) - 1
    bias_spec = pl.BlockSpec((None, 2 * TPS, QROWS, lk), lambda t, g: (jnp.minimum(g, last_variant), t, 0, 0))
    tile_spec = pl.BlockSpec((QROWS, wide), lambda t, g: (g, t))
    return lk, pad, q_spec, k_spec, v_spec, bias_spec, tile_spec


def _attention_fwd(proj, bias, sinks, *, n_back, gqa, q_col, k_col, v_col, name, carry=None):
    s = proj.shape[0]
    lk, pad, q_spec, k_spec, v_spec, bias_spec, tile_spec = _attn_common(s, n_back, gqa, q_col, k_col, v_col)
    n_t, n_g = 512 // (TPS * LANES), s // QROWS
    kv_wide = LANES if gqa else TPS * LANES

    def body(*refs):
        if gqa:
            q_ref, k_ref, v_ref, bias_ref, sink_ref, o_ref, l_ref, kpad, vpad = refs
        else:
            q_ref, k_ref, v_ref, bias_ref, o_ref, l_ref, kpad, vpad = refs
        t, g = pl.program_id(0), pl.program_id(1)

        @pl.when(g == 0)
        def _():
            kpad[0:pad, :] = jnp.zeros((pad, kv_wide), BF16)
            vpad[0:pad, :] = jnp.zeros((pad, kv_wide), BF16)
            kpad[pad:, :] = k_ref[...]
            vpad[pad:, :] = v_ref[...]

        start = pl.multiple_of(g * QROWS, QROWS)
        half = lax.broadcasted_iota(jnp.int32, (QROWS, LANES), 1) // HEAD_DIM
        for tt in range(TPS):
            lanes = slice(tt * LANES, (tt + 1) * LANES)
            kv_lanes = slice(0, LANES) if gqa else lanes
            kb = kpad[pl.ds(start, lk), kv_lanes]
            vb = vpad[pl.ds(start, lk), kv_lanes]
            q = q_ref[:, lanes] * (HEAD_DIM ** -0.5)
            if gqa:
                hk = (TPS * t + tt) // 2
                q_rolled = pltpu.roll(q.astype(F32), HEAD_DIM, 1).astype(BF16)
            outs, lses = [], []
            for e in range(2):
                if gqa:
                    kv_half = hk
                    src = jnp.where(hk == e, q, q_rolled)
                else:
                    kv_half = e
                    src = q
                qm = jnp.where(half == kv_half, src, jnp.zeros_like(src))
                sc = _dot_nt(qm, kb) + bias_ref[2 * tt + e]
                m = jnp.max(sc, axis=-1, keepdims=True)
                if gqa:
                    sk = sink_ref[2 * (TPS * t + tt) + e]
                    m = jnp.maximum(m, sk)
                p = jnp.exp(sc - m)
                l = jnp.sum(p, axis=-1, keepdims=True)
                if gqa:
                    l = l + jnp.exp(sk - m)
                pn = p / l
                outs.append(_dot(pn.astype(BF16), vb))
                lses.append(m + jnp.log(l))
            if gqa:
                same = jnp.where(hk == 0, outs[0], outs[1])
                other = jnp.where(hk == 0, outs[1], outs[0])
                o_ref[:, lanes] = jnp.where(half == hk, same, pltpu.roll(other, HEAD_DIM, 1))
            else:
                o_ref[:, lanes] = jnp.where(half == 0, outs[0], outs[1])
            l_ref[:, lanes] = jnp.where(half == 0, lses[0], lses[1])

    in_specs = [q_spec, k_spec, v_spec, bias_spec] + ([SMEM_SPEC] if gqa else [])
    args = [proj, proj, proj, bias] + ([sinks] if gqa else [])
    o_shape = jax.ShapeDtypeStruct((s, 512), F32)
    return _call(body, name=name, grid=(n_t, n_g), in_specs=in_specs, out_specs=(tile_spec, tile_spec),
                 out_shape=(o_shape, o_shape), args=args,
                 scratch=[pltpu.VMEM((s + pad, kv_wide), BF16), pltpu.VMEM((s + pad, kv_wide), BF16)],
                 sem=("arbitrary", "arbitrary"), carry=carry)


def _attention_bwd(proj, bias, sinks, do, lse, *, n_back, gqa, q_col, k_col, v_col, name, carry=None):
    s = proj.shape[0]
    lk, pad, q_spec, k_spec, v_spec, bias_spec, tile_spec = _attn_common(s, n_back, gqa, q_col, k_col, v_col)
    n_t, n_g = 512 // (TPS * LANES), s // QROWS
    kv_wide = LANES if gqa else TPS * LANES

    def body(*refs):
        if gqa:
            (q_ref, k_ref, v_ref, bias_ref, sink_ref, do_ref, l_ref,
             dq_ref, dk_ref, dv_ref, dsink_ref, kpad, vpad, dkpad, dvpad) = refs
        else:
            (q_ref, k_ref, v_ref, bias_ref, do_ref, l_ref,
             dq_ref, dk_ref, dv_ref, dbias_ref, kpad, vpad, dkpad, dvpad) = refs
        t, g = pl.program_id(0), pl.program_id(1)

        @pl.when(g == 0)
        def _():
            kpad[0:pad, :] = jnp.zeros((pad, kv_wide), BF16)
            vpad[0:pad, :] = jnp.zeros((pad, kv_wide), BF16)
            kpad[pad:, :] = k_ref[...]
            vpad[pad:, :] = v_ref[...]
            if gqa:
                dsink_ref[...] = jnp.zeros_like(dsink_ref)
            else:
                dbias_ref[...] = jnp.zeros_like(dbias_ref)

        @pl.when((g == 0) & (t == 0) if gqa else g == 0)
        def _():
            dkpad[...] = jnp.zeros_like(dkpad)
            dvpad[...] = jnp.zeros_like(dvpad)

        start = pl.multiple_of(g * QROWS, QROWS)
        half = lax.broadcasted_iota(jnp.int32, (QROWS, LANES), 1) // HEAD_DIM
        for tt in range(TPS):
            lanes = slice(tt * LANES, (tt + 1) * LANES)
            kv_lanes = slice(0, LANES) if gqa else lanes
            kb = kpad[pl.ds(start, lk), kv_lanes]
            vb = vpad[pl.ds(start, lk), kv_lanes]
            q = q_ref[:, lanes]
            dov = do_ref[:, lanes]
            lv = l_ref[:, lanes]
            if gqa:
                hk = (TPS * t + tt) // 2
                q_rolled = pltpu.roll(q.astype(F32), HEAD_DIM, 1).astype(BF16)
                do_rolled = pltpu.roll(dov, HEAD_DIM, 1)
            dqs = []
            dk_acc = jnp.zeros((lk, LANES), F32)
            dv_acc = jnp.zeros((lk, LANES), F32)
            for e in range(2):
                if gqa:
                    kv_half = hk
                    src = jnp.where(hk == e, q, q_rolled)
                    do_src = jnp.where(hk == e, dov, do_rolled)
                else:
                    kv_half = e
                    src = q
                    do_src = dov
                qm = jnp.where(half == kv_half, src, jnp.zeros_like(src))
                dom = jnp.where(half == kv_half, do_src, 0.0).astype(BF16)
                lcol = jnp.max(jnp.where(half == e, lv, -jnp.inf), axis=-1, keepdims=True)
                sc = _dot_nt(qm * (HEAD_DIM ** -0.5), kb) + bias_ref[2 * tt + e]
                pn = jnp.exp(sc - lcol)
                dp = _dot_nt(dom, vb)
                delta = jnp.sum(pn * dp, axis=-1, keepdims=True)
                ds = pn * (dp - delta)
                if gqa:
                    p_sink = jnp.exp(sink_ref[2 * (TPS * t + tt) + e] - lcol)
                    dsk = -jnp.sum(p_sink * delta, axis=0, keepdims=True)
                    row = 2 * tt + e
                    dsink_ref[0, row:row + 1, :] += jnp.broadcast_to(dsk, (1, LANES))
                else:
                    dbias_ref[2 * tt + e] += ds
                dsb = (ds * (HEAD_DIM ** -0.5)).astype(BF16)
                dqs.append(_dot(dsb, kb))
                dk_acc = dk_acc + _dot_tn(dsb, qm)
                dv_acc = dv_acc + _dot_tn(pn.astype(BF16), dom)
            dkpad[pl.ds(start, lk), kv_lanes] += dk_acc
            dvpad[pl.ds(start, lk), kv_lanes] += dv_acc
            if gqa:
                same = jnp.where(hk == 0, dqs[0], dqs[1])
                other = jnp.where(hk == 0, dqs[1], dqs[0])
                dq_ref[:, lanes] = jnp.where(half == hk, same, pltpu.roll(other, HEAD_DIM, 1)).astype(BF16)
            else:
                dq_ref[:, lanes] = jnp.where(half == 0, dqs[0], dqs[1]).astype(BF16)

        @pl.when((g == n_g - 1) & (t == n_t - 1) if gqa else g == n_g - 1)
        def _():
            dk_ref[...] = dkpad[pad:, :].astype(BF16)
            dv_ref[...] = dvpad[pad:, :].astype(BF16)

    in_specs = [q_spec, k_spec, v_spec, bias_spec] + ([SMEM_SPEC] if gqa else []) + [tile_spec, tile_spec]
    args = [proj, proj, proj, bias] + ([sinks] if gqa else []) + [do, lse]
    if gqa:
        kv_out = pl.BlockSpec((s, LANES), lambda t, g: (0, 0))
        kv_shape = jax.ShapeDtypeStruct((s, LANES), BF16)
        extra_spec = pl.BlockSpec((1, 8, LANES), lambda t, g: (t, 0, 0))
        extra_shape = jax.ShapeDtypeStruct((n_t, 8, LANES), F32)
    else:
        kv_out = pl.BlockSpec((s, kv_wide), lambda t, g: (0, t))
        kv_shape = jax.ShapeDtypeStruct((s, 512), BF16)
        extra_spec = pl.BlockSpec((2 * TPS, QROWS, lk), lambda t, g: (t, 0, 0))
        extra_shape = jax.ShapeDtypeStruct(bias.shape[1:], F32)
    return _call(body, name=name, grid=(n_t, n_g), in_specs=in_specs,
                 out_specs=(tile_spec, kv_out, kv_out, extra_spec),
                 out_shape=(jax.ShapeDtypeStruct((s, 512), BF16), kv_shape, kv_shape, extra_shape), args=args,
                 scratch=[pltpu.VMEM((s + pad, kv_wide), BF16), pltpu.VMEM((s + pad, kv_wide), BF16),
                          pltpu.VMEM((s + pad, kv_wide), F32), pltpu.VMEM((s + pad, kv_wide), F32)],
                 sem=("arbitrary", "arbitrary"), carry=carry)


def _sum_slots(r, name):
    n_slots, rows, k = r.shape

    def body(r_ref, o_ref):
        acc = r_ref[0].astype(F32)
        for j in range(1, n_slots):
            acc = acc + r_ref[j].astype(F32)
        o_ref[...] = acc

    return _call(body, name=name, grid=(k // LANES,),
                 in_specs=[pl.BlockSpec((n_slots, rows, LANES), lambda i: (0, 0, i))],
                 out_specs=pl.BlockSpec((rows, LANES), lambda i: (0, i)),
                 out_shape=jax.ShapeDtypeStruct((rows, k), F32), args=[r], sem=("parallel",))


def _sum_rows8(g):
    n = g.shape[2]

    def body(g_ref, o_ref):
        acc = g_ref[0]
        for j in range(1, N_DEV):
            acc = acc + g_ref[j]
        o_ref[...] = acc

    return pl.pallas_call(
        body, name="sum_small_grads", in_specs=[VMEM_SPEC], out_specs=VMEM_SPEC,
        out_shape=jax.ShapeDtypeStruct((1, n), F32), compiler_params=_params(),
    )(g)


def _ada_weight_grad(sc_t, dmod_cols):
    d = sc_t.shape[0]
    w = dmod_cols.shape[1]
    td = _pick(d, (256, 128))

    def body(sc_ref, dm_ref, o_ref):
        scv = sc_ref[...]
        dmv = dm_ref[...]
        acc = scv[:, 0:1] * dmv[0:1, :]
        for b in range(1, N_DEV):
            acc = acc + scv[:, b:b + 1] * dmv[b:b + 1, :]
        o_ref[...] = acc

    return _call(body, name="ada_weight_grad", grid=(d // td,),
                 in_specs=[pl.BlockSpec((td, N_DEV), lambda i: (i, 0)), pl.BlockSpec((N_DEV, w), lambda i: (0, 0))],
                 out_specs=pl.BlockSpec((td, w), lambda i: (i, 0)), out_shape=jax.ShapeDtypeStruct((d, w), F32),
                 args=[sc_t, dmod_cols], sem=("parallel",))


def _adamw_update(w, gv, m, v):
    nm = ADAM_B1 * m + (1.0 - ADAM_B1) * gv
    nv = ADAM_B2 * v + (1.0 - ADAM_B2) * (gv * gv)
    m_hat = nm / (1.0 - ADAM_B1 ** ADAM_STEP)
    v_hat = nv / (1.0 - ADAM_B2 ** ADAM_STEP)
    return -ADAM_LR * (m_hat / (jnp.sqrt(v_hat) + ADAM_EPS) + ADAM_WD * w), nm, nv


def _adamw(w, g, m, v, name):
    rows, cols = w.shape
    tr = _pick(rows, (256, 176, 128, 88, 64)) if rows > 256 else rows

    def body(w_ref, g_ref, m_ref, v_ref, d_ref, nm_ref, nv_ref):
        d_ref[...], nm_ref[...], nv_ref[...] = _adamw_update(w_ref[...], g_ref[...], m_ref[...], v_ref[...])

    spec = pl.BlockSpec((tr, cols), lambda i: (i, 0))
    shape = jax.ShapeDtypeStruct((rows, cols), F32)
    return _call(body, name=name, grid=(rows // tr,), in_specs=[spec] * 4, out_specs=(spec, spec, spec),
                 out_shape=(shape, shape, shape), args=[w, g, m, v], sem=("parallel",))


def _adamw_from_slots(w, own, slots, m, v, name):
    n_slots, rows, k = slots.shape

    def body(o_ref, s_ref, w_ref, m_ref, v_ref, g_ref, d_ref, nm_ref, nv_ref):
        gv = o_ref[...].astype(F32)
        for j in range(n_slots):
            gv = gv + s_ref[j].astype(F32)
        g_ref[...] = gv
        d_ref[...], nm_ref[...], nv_ref[...] = _adamw_update(w_ref[...], gv, m_ref[...], v_ref[...])

    tr = rows // 2 if rows % 32 == 0 else rows
    spec = pl.BlockSpec((tr, k), lambda i: (i, 0))
    shape = jax.ShapeDtypeStruct((rows, k), F32)
    return _call(body, name=name, grid=(rows // tr,),
                 in_specs=[spec, pl.BlockSpec((n_slots, tr, k), lambda i: (0, i, 0)), spec, spec, spec],
                 out_specs=(spec, spec, spec, spec), out_shape=(shape, shape, shape, shape),
                 args=[own, slots, w, m, v], sem=("parallel",))


def _adamw_small(g, w, m, v, sizes):
    n = w.shape[1]
    offs, off = [], 0
    for size in sizes:
        offs.append(off)
        off += size + (-size % LANES)

    def body(g_ref, w_ref, m_ref, v_ref, *out_refs):
        gv = g_ref[:, 0:n]
        dv, nm, nv = _adamw_update(w_ref[...], gv, m_ref[...], v_ref[...])
        for j, (o, size) in enumerate(zip(offs, sizes)):
            for k, val in enumerate((gv, dv, nm, nv)):
                out_refs[4 * j + k][...] = val[:, o:o + size]

    shapes = [jax.ShapeDtypeStruct((1, size), F32) for size in sizes for _ in range(4)]
    return pl.pallas_call(
        body, name="adamw_small", in_specs=[VMEM_SPEC] * 4, out_specs=tuple([VMEM_SPEC] * len(shapes)),
        out_shape=tuple(shapes), compiler_params=_params(),
    )(g, w, m, v)


SMALL = ("b_ada", "g_pre_ffn1", "g_post_ffn1", "g_pre_mix", "b_in", "sinks_a", "rel_bias_b", "g_grp_a",
         "g_grp_b", "b_out", "g_post_mix", "g_pre_ffn2", "g_post_ffn2")
WEIGHTS = ("w_ada", "b_ada", "g_pre_ffn1", "w_gate1", "w_up1", "w_down1", "g_post_ffn1", "g_pre_mix", "w_in",
           "b_in", "sinks_a", "rel_bias_b", "g_grp_a", "g_grp_b", "w_out", "b_out", "g_post_mix", "g_pre_ffn2",
           "w_gate2", "w_up2", "w_down2", "g_post_ffn2")


def kernel(x, c, w_ada, b_ada, g_pre_ffn1, w_gate1, w_up1, w_down1, g_post_ffn1, g_pre_mix, w_in, b_in, sinks_a, rel_bias_b, g_grp_a, g_grp_b, w_out, b_out, g_post_mix, g_pre_ffn2, w_gate2, w_up2, w_down2, g_post_ffn2, loss_target, m_w_ada, m_b_ada, m_g_pre_ffn1, m_w_gate1, m_w_up1, m_w_down1, m_g_post_ffn1, m_g_pre_mix, m_w_in, m_b_in, m_sinks_a, m_rel_bias_b, m_g_grp_a, m_g_grp_b, m_w_out, m_b_out, m_g_post_mix, m_g_pre_ffn2, m_w_gate2, m_w_up2, m_w_down2, m_g_post_ffn2, v_w_ada, v_b_ada, v_g_pre_ffn1, v_w_gate1, v_w_up1, v_w_down1, v_g_post_ffn1, v_g_pre_mix, v_w_in, v_b_in, v_sinks_a, v_rel_bias_b, v_g_grp_a, v_g_grp_b, v_w_out, v_b_out, v_g_post_mix, v_g_pre_ffn2, v_w_gate2, v_w_up2, v_w_down2, v_g_post_ffn2):
    given = dict(locals())
    weights = {n: given[n] for n in WEIGHTS}
    mom_m = {n: given["m_" + n] for n in WEIGHTS}
    mom_v = {n: given["v_" + n] for n in WEIGHTS}

    me = 4 * lax.axis_index("x") + 2 * lax.axis_index("y") + lax.axis_index("c")
    xs = x[0]
    tgt = loss_target[0]
    d_model = xs.shape[1]
    ada_cols = w_ada.shape[2]

    sh = {"wg1": w_gate1[0].T, "wu1": w_up1[0].T, "wd1": w_down1[0], "win": w_in[0].T, "wo": w_out[0],
          "wg2": w_gate2[0].T, "wu2": w_up2[0].T, "wd2": w_down2[0]}
    sh = {k: v.astype(BF16) for k, v in sh.items()}

    def gather(*names):
        return _gather_carry([sh[n] for n in names])

    bias_a = _alibi_bias()
    rel_m = _rel_index_matrix()
    rel_vec = jnp.dot(rel_bias_b[0], rel_m.T, precision=lax.Precision.HIGHEST)
    bias_b, (wg1, wu1) = _toeplitz_bias(rel_vec.reshape(H_B, 1, SKEW), carry=gather("wg1", "wu1"))

    b_cols = lax.dynamic_slice(b_ada, (0, me * ada_cols), (1, ada_cols))
    (sc_all, mod_rows), _ = _ada_forward(c, w_ada[0], b_cols, _Carry([], [], [], lambda *a: None, lambda *a: None))
    mod = mod_rows.reshape(N_MOD, d_model)
    shift1, scale1, gate1, shift2, scale2, gate2, shift3, scale3, gate3 = (mod[i:i + 1] for i in range(N_MOD))

    h1 = _pre_norm(xs, g_pre_ffn1, scale1, shift1, "pre_norm_ffn1")
    (a1, b1, u1), (wd1,) = _ffn_up(h1, wg1, wu1, "ffn_up_ffn1", carry=gather("wd1"))
    (y1, x1, h2), (win,) = _mm_nn(
        [(u1, wd1)], "ffn_down_ffn1", F32, carry=gather("win"),
        tail=_tail_post_pre(xs, g_post_ffn1, gate1, 0.5, g_pre_mix, scale2, shift2))

    proj, (wo,) = _mm_nt(h2, win, "in_proj", BF16, bias=b_in, carry=gather("wo"))
    sinks = sinks_a[0]
    cfg_a = dict(n_back=BACK_A, gqa=True, q_col=0, k_col=QA // LANES, v_col=(QA + KVA) // LANES)
    cfg_b = dict(n_back=BACK_B, gqa=False, q_col=(QA + 2 * KVA) // LANES, k_col=(QA + 2 * KVA + QB) // LANES,
                 v_col=(QA + 2 * KVA + 2 * QB) // LANES)
    (oa, lse_a), (wg2,) = _attention_fwd(proj, bias_a, sinks, name="attn_a", carry=gather("wg2"), **cfg_a)
    (ob, lse_b), (wu2,) = _attention_fwd(proj, bias_b, None, name="attn_b", carry=gather("wu2"), **cfg_b)
    ycat = _group_norm_cat(oa, ob, g_grp_a, g_grp_b)
    ymix, x2, h3 = _mm_nn([(ycat, wo)], "out_proj", F32, bias=b_out,
                          tail=_tail_post_pre(x1, g_post_mix, gate2, 1.0, g_pre_ffn2, scale3, shift3))

    (a3, b3, u3), (wd2,) = _ffn_up(h3, wg2, wu2, "ffn_up_ffn2", carry=gather("wd2"))

    flights = {}

    def scatter_start(tag, after_vec, **parts):
        names = list(parts)
        sems, p_thru, lands, token = _scatter_start([parts[n] for n in names], "scatter_start_" + tag)
        flights[tag] = (names, sems, p_thru, lands)
        return after_vec + token[0:1, 0:1]

    dx3, dy, loss_part, s1 = _mm_nn([(u3, wd2)], "ffn_down_ffn2", None,
                                    tail=_tail_post_loss(x2, tgt, g_post_ffn2, gate3, 0.5))
    da, db = _ffn_down_bwd(dy, wd2, a3, b3, "ffn_down_bwd_ffn2")
    dwd2 = _mm_tn_pair(u3, dy, "grad_wd_ffn2")
    dwg2 = _mm_tn_pair(da, h3, "grad_wg_ffn2")
    dwu2 = _mm_tn_pair(db, h3, "grad_wu_ffn2")
    g_pre_tied = scatter_start("ffn2", g_pre_ffn2, wd2=dwd2, wg2=dwg2, wu2=dwu2)
    dx2, dymix, s2, s3, s1m, db_out = _mm_nn(
        [(da, wg2), (db, wu2)], "ffn_up_bwd_ffn2", None,
        tail=_tail_pre_post_bwd(x2, dx3, ymix, g_pre_tied, scale3, g_post_mix, gate2, 1.0))
    sm3 = dict(shift=s3, scale=s2 * g_pre_ffn2, gate=0.5 * g_post_ffn2 * s1,
               g_pre=(1.0 + scale3) * s2, g_post=(0.5 * gate3) * s1)

    dycat = _mm_nt(dymix, wo, "out_proj_bwd", F32)
    dwo = _mm_tn_pair(ycat, dymix, "grad_wo")
    doa, dob, dg_a, dg_b = _group_norm_bwd(dycat, oa, ob, g_grp_a, g_grp_b)
    dqa, dka, dva, dsink = _attention_bwd(proj, bias_a, sinks, doa, lse_a, name="attn_a_bwd", **cfg_a)
    dqb, dkb, dvb, dbias = _attention_bwd(proj, bias_b, None, dob, lse_b, name="attn_b_bwd", **cfg_b)
    dproj = jnp.concatenate([dqa, dka, dva, dqb, dkb, dvb], axis=1)
    db_in = _col_sum(dproj, "grad_b_in")
    dwin = _mm_tn_pair(dproj, h2, "grad_win")
    g_pre_tied = scatter_start("mix", g_pre_mix, wo=dwo, win=dwin)
    dx1, dy, s2m, s3m, s1, _ = _mm_nn(
        [(dproj, win)], "in_proj_bwd", None,
        tail=_tail_pre_post_bwd(x1, dx2, y1, g_pre_tied, scale2, g_post_ffn1, gate1, 0.5))
    d_rel = jnp.dot(_diagonal_sums(dbias).reshape(H_B, SKEW), rel_m, precision=lax.Precision.HIGHEST)
    d_sinks = dsink[:, :2 * TPS, 0].reshape(1, H_A)

    da, db = _ffn_down_bwd(dy, wd1, a1, b1, "ffn_down_bwd_ffn1")
    dwd1 = _mm_tn_pair(u1, dy, "grad_wd_ffn1")
    dwg1 = _mm_tn_pair(da, h1, "grad_wg_ffn1")
    dwu1 = _mm_tn_pair(db, h1, "grad_wu_ffn1")
    g_pre_tied = scatter_start("ffn1", g_pre_ffn1, wd1=dwd1, wg1=dwg1, wu1=dwu1)
    dx0, s2, s3 = _mm_nn([(da, wg1), (db, wu1)], "ffn_up_bwd_ffn1", None,
                         tail=_tail_pre_bwd(xs, dx1, g_pre_tied, scale1))
    sm1 = dict(shift=s3, scale=s2 * g_pre_ffn1, gate=0.5 * g_post_ffn1 * s1,
               g_pre=(1.0 + scale1) * s2, g_post=(0.5 * gate1) * s1)

    dmod = jnp.concatenate([sm1["shift"], sm1["scale"], sm1["gate"],
                            s3m, s2m * g_pre_mix, g_post_mix * s1m,
                            sm3["shift"], sm3["scale"], sm3["gate"]], axis=1)
    small_parts = {
        "b_ada": dmod, "g_pre_ffn1": sm1["g_pre"], "g_post_ffn1": sm1["g_post"],
        "g_pre_mix": (1.0 + scale2) * s2m, "b_in": db_in, "sinks_a": d_sinks,
        "rel_bias_b": d_rel.reshape(1, H_B * N_REL), "g_grp_a": dg_a, "g_grp_b": dg_b, "b_out": db_out,
        "g_post_mix": gate2 * s1m, "g_pre_ffn2": sm3["g_pre"], "g_post_ffn2": sm3["g_post"]}
    sizes = [small_parts[n].shape[1] for n in SMALL]

    def pack(parts):
        cells = []
        for p in parts:
            cells.append(p)
            if p.shape[1] % LANES:
                cells.append(jnp.zeros((1, -p.shape[1] % LANES), F32))
        return jnp.concatenate(cells, axis=1)

    packed = pack([small_parts[n] for n in SMALL] + [loss_part])
    n_packed = packed.shape[1]
    gathered = _all_gather_small(packed)
    small_sum = _sum_rows8(gathered)
    loss = small_sum[0, n_packed - LANES]
    dmod_cols = lax.dynamic_slice(gathered.reshape(N_DEV, n_packed), (0, me * ada_cols), (N_DEV, ada_cols))
    g_ada = _ada_weight_grad(sc_all.reshape(N_DEV, d_model).T, dmod_cols)

    chip = 2 * lax.axis_index("x") + lax.axis_index("y")
    own, slots = {}, {}
    for tag, after in (("ffn2", dx0), ("mix", dx0), ("ffn1", small_sum)):
        names, sems, p_thru, lands = flights[tag]
        p_done, l_done = _scatter_wait(sems, p_thru, lands, after, "scatter_wait_" + tag)
        for n, p, l in zip(names, p_done, l_done):
            own[n] = lax.dynamic_slice(p, (chip * l.shape[1], 0), (l.shape[1], p.shape[1]))
            slots[n] = l

    out_g, out_d, out_m, out_v = {}, {}, {}, {}
    d_, m_, v_ = _adamw(w_ada[0], g_ada, m_w_ada[0], v_w_ada[0], "adamw_w_ada")
    out_g["w_ada"], out_d["w_ada"], out_m["w_ada"], out_v["w_ada"] = g_ada[None], d_[None], m_[None], v_[None]
    for n, key, transposed in (("w_gate1", "wg1", True), ("w_up1", "wu1", True), ("w_down1", "wd1", False),
                               ("w_in", "win", True), ("w_out", "wo", False), ("w_gate2", "wg2", True),
                               ("w_up2", "wu2", True), ("w_down2", "wd2", False)):
        view = (lambda t: t.T) if transposed else (lambda t: t)
        res = _adamw_from_slots(view(weights[n][0]), own[key], slots[key], view(mom_m[n][0]), view(mom_v[n][0]),
                                "adamw_" + n)
        out_g[n], out_d[n], out_m[n], out_v[n] = (view(t)[None] for t in res)

    small_out = _adamw_small(small_sum, *(pack([tree[n].reshape(1, -1) for n in SMALL])
                                          for tree in (weights, mom_m, mom_v)), sizes)
    for j, n in enumerate(SMALL):
        shape = weights[n].shape
        out_g[n], out_d[n], out_m[n], out_v[n] = (t.reshape(shape) for t in small_out[4 * j:4 * j + 4])

    return (loss, dx0[None], *[out_g[n] for n in WEIGHTS], *[out_d[n] for n in WEIGHTS],
            *[out_m[n] for n in WEIGHTS], *[out_v[n] for n in WEIGHTS])
```

```python
import numpy as np
import jax
import jax.numpy as jnp
from jax import lax
from jax.experimental import pallas as pl
from jax.experimental.pallas import tpu as pltpu

F32 = jnp.float32
BF16 = jnp.bfloat16
MESH = pl.DeviceIdType.MESH
ANY = pl.BlockSpec(memory_space=pl.ANY)
VMEM_SPEC = pl.BlockSpec(memory_space=pltpu.VMEM)
SMEM_SPEC = pl.BlockSpec(memory_space=pltpu.SMEM)

N_DEV = 8
CHUNK = 64
HEAD_DIM = 64
LANES = 128
H_A, KV_A, H_B = 8, 2, 8
BACK_A, BACK_B = 2, 8
REL_CLIP = 128
N_REL = 2 * REL_CLIP + 1
QA, KVA, QB = H_A * HEAD_DIM, KV_A * HEAD_DIM, H_B * HEAD_DIM
D_IN = QA + 2 * KVA + 3 * QB
N_MOD = 9
EPS = 1e-6
NEG_INF = -1e30
QG = 4
QROWS = QG * CHUNK
TPS = 2
SKEW = 1024
ADAM_LR, ADAM_B1, ADAM_B2, ADAM_EPS, ADAM_WD, ADAM_STEP = 0.001, 0.9, 0.999, 1e-08, 0.01, 10
VMEM_LIMIT = 56 * 2 ** 20


def _pick(n, cands):
    for c in cands:
        if n % c == 0:
            return c
    return n


def _pieces(n, width=2 * LANES):
    return [(lo, min(lo + width, n)) for lo in range(0, n, width)]


def _params(sem=None):
    return pltpu.CompilerParams(dimension_semantics=sem, vmem_limit_bytes=VMEM_LIMIT)


def _dot_nt(a, b):
    return lax.dot_general(a, b, (((1,), (1,)), ((), ())), preferred_element_type=F32)


def _dot_tn(a, b):
    return lax.dot_general(a, b, (((0,), (0,)), ((), ())), preferred_element_type=F32)


def _dot(a, b):
    return jnp.dot(a, b, preferred_element_type=F32)


def _sigmoid(a):
    return 0.5 * (jnp.tanh(0.5 * a) + 1.0)


def _mesh_pos():
    return lax.axis_index("x"), lax.axis_index("y"), lax.axis_index("c")


def _peer(x, y, c, r):
    px = 1 - x if r & 4 else x
    py = 1 - y if r & 2 else y
    pc = 1 - c if r & 1 else c
    return px, py, pc


class _Carry:
    def __init__(self, ins, out_shapes, scratch, start, finish):
        self.ins, self.out_shapes, self.scratch = list(ins), list(out_shapes), list(scratch)
        self.start, self.finish = start, finish


def _call(body, *, name, grid, in_specs, out_specs, out_shape, args, scratch=(), sem=None, carry=None):
    single = not isinstance(out_shape, (tuple, list))
    out_specs = (out_specs,) if single else tuple(out_specs)
    out_shape = (out_shape,) if single else tuple(out_shape)
    if carry is None:
        res = pl.pallas_call(body, name=name, grid=grid, in_specs=list(in_specs), out_specs=out_specs,
                             out_shape=out_shape, scratch_shapes=list(scratch), compiler_params=_params(sem))(*args)
        return res[0] if single else res
    n_in, n_out, n_s = len(in_specs), len(out_shape), len(scratch)
    ci, co = len(carry.ins), len(carry.out_shapes)

    def wrapped(*refs):
        ins, cins = refs[:n_in], refs[n_in:n_in + ci]
        outs = refs[n_in + ci:n_in + ci + n_out]
        couts = refs[n_in + ci + n_out:n_in + ci + n_out + co]
        scr = refs[n_in + ci + n_out + co:n_in + ci + n_out + co + n_s]
        cscr = refs[n_in + ci + n_out + co + n_s:]
        first, last = None, None
        for ax, n in enumerate(grid):
            f, l = pl.program_id(ax) == 0, pl.program_id(ax) == n - 1
            first = f if first is None else first & f
            last = l if last is None else last & l
        pl.when(first)(lambda: carry.start(cins, couts, cscr))
        body(*ins, *outs, *scr)
        pl.when(last)(lambda: carry.finish(cins, couts, cscr))

    res = pl.pallas_call(
        wrapped, name=name, grid=grid, in_specs=list(in_specs) + [ANY] * ci, out_specs=out_specs + (ANY,) * co,
        out_shape=out_shape + tuple(carry.out_shapes), scratch_shapes=list(scratch) + carry.scratch,
        compiler_params=_params(("arbitrary",) * len(grid)))(*args, *carry.ins)
    main = res[:n_out]
    return (main[0] if single else main), res[n_out:]


def _gather_carry(shards):
    n_w = len(shards)
    rows = [s.shape[0] for s in shards]

    def plan(ins, outs, scr):
        send_sems, recv_sems, local_sems = scr
        x, y, c = _mesh_pos()
        me, sibling = (x, y, c), (x, y, 1 - c)
        chips = [(1 - x, y), (x, 1 - y), (1 - x, 1 - y)]

        def block(w, dev):
            start = pl.multiple_of((4 * dev[0] + 2 * dev[1] + dev[2]) * rows[w], 16)
            return outs[w].at[pl.ds(start, rows[w]), :]

        def copy(w, k, dev, to, src=None):
            return pltpu.make_async_remote_copy(
                src_ref=block(w, dev) if src is None else src, dst_ref=block(w, dev),
                send_sem=send_sems.at[w, k], recv_sem=recv_sems.at[w, k], device_id=to, device_id_type=MESH)

        mine = [pltpu.make_async_copy(ins[w], block(w, me), local_sems.at[w]) for w in range(n_w)]
        first = []
        for j, chip in enumerate(chips):
            first += [copy(w, 1 + j, me, (*chip, c), src=ins[w]) for w in range(n_w)]
        first += [copy(w, 0, me, sibling, src=ins[w]) for w in range(n_w)]
        return c, me, sibling, chips, copy, mine, first

    def start(ins, outs, scr):
        _, _, _, _, _, mine, first = plan(ins, outs, scr)
        for cp in mine + first:
            cp.start()

    def finish(ins, outs, scr):
        c, me, sibling, chips, copy, mine, first = plan(ins, outs, scr)
        passed = []
        for j, chip in enumerate(chips):
            for w in range(n_w):
                copy(w, 1 + j, (*chip, c), me).wait_recv()
                cp = copy(w, 4 + j, (*chip, c), sibling)
                cp.start()
                passed.append(cp)
        for w in range(n_w):
            copy(w, 0, sibling, me).wait_recv()
        for j, chip in enumerate(chips):
            for w in range(n_w):
                copy(w, 4 + j, (*chip, 1 - c), me).wait_recv()
        for cp in first + passed:
            cp.wait_send()
        for cp in mine:
            cp.wait()

    return _Carry(
        shards, [jax.ShapeDtypeStruct((N_DEV * s.shape[0], s.shape[1]), s.dtype) for s in shards],
        [pltpu.SemaphoreType.DMA((n_w, N_DEV - 1)), pltpu.SemaphoreType.DMA((n_w, N_DEV - 1)),
         pltpu.SemaphoreType.DMA((n_w,))], start, finish)


def _scatter_carry(parts):
    n_w = len(parts)
    n_chip = N_DEV // 2
    rows = [g.shape[0] // n_chip for g in parts]

    def plan(ins, outs, scr):
        send_sems, recv_sems, local_sems = scr
        x, y, c = _mesh_pos()

        def src(w, chip_index):
            return ins[w].at[pl.ds(pl.multiple_of(chip_index * rows[w], 16), rows[w]), :]

        mine = [pltpu.make_async_copy(src(w, 2 * x + y), outs[w].at[0], local_sems.at[w]) for w in range(n_w)]
        copies = []
        for r in (3, 2, 1):
            px, py, _ = _peer(x, y, c, 2 * r)
            for w in range(n_w):
                copies.append(pltpu.make_async_remote_copy(
                    src_ref=src(w, 2 * px + py), dst_ref=outs[w].at[r], send_sem=send_sems.at[w, r - 1],
                    recv_sem=recv_sems.at[w, r - 1], device_id=(px, py, c), device_id_type=MESH))
        return mine, copies

    def start(ins, outs, scr):
        mine, copies = plan(ins, outs, scr)
        for cp in mine + copies:
            cp.start()

    def finish(ins, outs, scr):
        mine, copies = plan(ins, outs, scr)
        for cp in copies:
            cp.wait_recv()
        for cp in copies:
            cp.wait_send()
        for cp in mine:
            cp.wait()

    return _Carry(
        parts, [jax.ShapeDtypeStruct((n_chip, r, g.shape[1]), g.dtype) for r, g in zip(rows, parts)],
        [pltpu.SemaphoreType.DMA((n_w, n_chip - 1)), pltpu.SemaphoreType.DMA((n_w, n_chip - 1)),
         pltpu.SemaphoreType.DMA((n_w,))], start, finish)


HBM_SPEC = pl.BlockSpec(memory_space=pltpu.HBM)
SEM_SPEC = pl.BlockSpec(memory_space=pltpu.SEMAPHORE)
N_CHIP = N_DEV // 2


def _scatter_copy(part_ref, land_ref, send_sem, recv_sem, r, rows):
    x, y, c = _mesh_pos()
    px, py, _ = _peer(x, y, c, 2 * r)
    src = part_ref.at[pl.ds(pl.multiple_of((2 * px + py) * rows, 16), rows), :]
    return pltpu.make_async_remote_copy(
        src_ref=src, dst_ref=land_ref.at[r - 1], send_sem=send_sem, recv_sem=recv_sem,
        device_id=(px, py, c), device_id_type=MESH)


def _scatter_order(n_w):
    return [(w, r) for r in (3, 2, 1) for w in range(n_w)]


def _scatter_start(parts, name):
    n_w = len(parts)
    rows = [p.shape[0] // N_CHIP for p in parts]
    order = _scatter_order(n_w)
    lands = [pltpu.with_memory_space_constraint(lax.empty((N_CHIP - 1, r, p.shape[1]), p.dtype), pltpu.HBM)
             for r, p in zip(rows, parts)]

    def body(*refs):
        part_refs, land_refs = refs[:n_w], refs[n_w:2 * n_w]
        sems = refs[2 * n_w:2 * n_w + 2 * len(order)]
        token = refs[-1]
        for j, (w, r) in enumerate(order):
            _scatter_copy(part_refs[w], land_refs[w], sems[2 * j], sems[2 * j + 1], r, rows[w]).start()
        token[...] = jnp.zeros_like(token)

    n_sem = 2 * len(order)
    res = pl.pallas_call(
        body, name=name,
        out_shape=(*[pltpu.SemaphoreType.DMA(())] * n_sem, *[pltpu.HBM(p.shape, p.dtype) for p in parts],
                   *[pltpu.HBM(l.shape, l.dtype) for l in lands], jax.ShapeDtypeStruct((8, LANES), F32)),
        in_specs=[HBM_SPEC] * (2 * n_w), out_specs=(*[SEM_SPEC] * n_sem, *[HBM_SPEC] * (2 * n_w), VMEM_SPEC),
        input_output_aliases={i: n_sem + i for i in range(2 * n_w)},
        compiler_params=pltpu.CompilerParams(has_side_effects=pltpu.SideEffectType.DATAFLOW_SIDE_EFFECTING),
    )(*[pltpu.with_memory_space_constraint(p, pltpu.HBM) for p in parts], *lands)
    return (list(res[:n_sem]), list(res[n_sem:n_sem + n_w]), list(res[n_sem + n_w:n_sem + 2 * n_w]), res[-1])


def _scatter_wait(sems, parts, lands, after, name):
    n_w = len(parts)
    rows = [p.shape[0] // N_CHIP for p in parts]
    order = _scatter_order(n_w)

    def body(*refs):
        part_refs, land_refs = refs[:n_w], refs[n_w:2 * n_w]
        sem_refs = refs[2 * n_w:2 * n_w + 2 * len(order)]
        for j, (w, r) in enumerate(order):
            cp = _scatter_copy(part_refs[w], land_refs[w], sem_refs[2 * j], sem_refs[2 * j + 1], r, rows[w])
            cp.wait_send()
            cp.wait_recv()

    res = pl.pallas_call(
        body, name=name,
        out_shape=(*[pltpu.HBM(p.shape, p.dtype) for p in parts], *[pltpu.HBM(l.shape, l.dtype) for l in lands]),
        in_specs=[HBM_SPEC] * (2 * n_w) + [SEM_SPEC] * len(sems) + [ANY],
        out_specs=tuple([HBM_SPEC] * (2 * n_w)),
        input_output_aliases={i: i for i in range(2 * n_w)},
        compiler_params=pltpu.CompilerParams(has_side_effects=pltpu.SideEffectType.DATAFLOW_SIDE_EFFECTING),
    )(*parts, *lands, *sems, after)
    return list(res[:n_w]), list(res[n_w:])


def _ada_forward(c_row, w_ada, b_cols, carry):
    d = c_row.shape[1]
    wcols = w_ada.shape[1]
    ci, co = len(carry.ins), len(carry.out_shapes)

    def body(*refs):
        c_ref, w_ref, b_ref = refs[:3]
        cins = refs[3:3 + ci]
        sc_ref, mod_ref = refs[3 + ci:5 + ci]
        couts = refs[5 + ci:5 + ci + co]
        rows_ref, send_sems, recv_sems = refs[5 + ci + co:8 + ci + co]
        cscr = refs[8 + ci + co:]
        carry.start(cins, couts, cscr)
        x, y, c = _mesh_pos()
        me = 4 * x + 2 * y + c
        cv = c_ref[...]
        sc_ref[me] = cv * _sigmoid(cv)

        sends = []
        for r in range(1, N_DEV):
            px, py, pc = _peer(x, y, c, r)
            cp = pltpu.make_async_remote_copy(
                src_ref=sc_ref.at[me], dst_ref=sc_ref.at[me], send_sem=send_sems.at[0, r - 1],
                recv_sem=recv_sems.at[0, r - 1], device_id=(px, py, pc), device_id_type=MESH)
            cp.start()
            sends.append(cp)
        for r in range(1, N_DEV):
            px, py, pc = _peer(x, y, c, r)
            pid = 4 * px + 2 * py + pc
            pltpu.make_async_remote_copy(
                src_ref=sc_ref.at[pid], dst_ref=sc_ref.at[pid], send_sem=send_sems.at[0, r - 1],
                recv_sem=recv_sems.at[0, r - 1], device_id=(px, py, pc), device_id_type=MESH).wait_recv()
        for cp in sends:
            cp.wait_send()

        sc_all = jnp.concatenate([sc_ref[j] for j in range(N_DEV)], axis=0)
        rows = _dot(sc_all.astype(BF16), w_ref[...].astype(BF16)) + b_ref[...]
        for j in range(N_DEV):
            rows_ref[j] = rows[j:j + 1, :]
        mod_ref[me] = rows_ref[me]

        sends = []
        for r in range(1, N_DEV):
            px, py, pc = _peer(x, y, c, r)
            pid = 4 * px + 2 * py + pc
            cp = pltpu.make_async_remote_copy(
                src_ref=rows_ref.at[pid], dst_ref=mod_ref.at[me], send_sem=send_sems.at[1, r - 1],
                recv_sem=recv_sems.at[1, r - 1], device_id=(px, py, pc), device_id_type=MESH)
            cp.start()
            sends.append(cp)
        for r in range(1, N_DEV):
            px, py, pc = _peer(x, y, c, r)
            pid = 4 * px + 2 * py + pc
            pltpu.make_async_remote_copy(
                src_ref=rows_ref.at[pid], dst_ref=mod_ref.at[pid], send_sem=send_sems.at[1, r - 1],
                recv_sem=recv_sems.at[1, r - 1], device_id=(px, py, pc), device_id_type=MESH).wait_recv()
        for cp in sends:
            cp.wait_send()
        carry.finish(cins, couts, cscr)

    res = pl.pallas_call(
        body, name="ada_forward",
        out_shape=(jax.ShapeDtypeStruct((N_DEV, 1, d), F32), jax.ShapeDtypeStruct((N_DEV, 1, wcols), F32),
                   *carry.out_shapes),
        in_specs=[VMEM_SPEC, VMEM_SPEC, VMEM_SPEC] + [ANY] * ci, out_specs=(VMEM_SPEC, VMEM_SPEC) + (ANY,) * co,
        scratch_shapes=[pltpu.VMEM((N_DEV, 1, wcols), F32), pltpu.SemaphoreType.DMA((2, N_DEV - 1)),
                        pltpu.SemaphoreType.DMA((2, N_DEV - 1))] + carry.scratch,
        compiler_params=_params(),
    )(c_row, w_ada, b_cols, *carry.ins)
    return res[:2], res[2:]


def _all_gather_small(v):
    n = v.shape[1]

    def body(v_ref, out_ref, send_sems, recv_sems):
        x, y, c = _mesh_pos()
        me = 4 * x + 2 * y + c
        out_ref[me] = v_ref[...]
        sends = []
        for r in range(1, N_DEV):
            px, py, pc = _peer(x, y, c, r)
            cp = pltpu.make_async_remote_copy(
                src_ref=v_ref, dst_ref=out_ref.at[me], send_sem=send_sems.at[r - 1],
                recv_sem=recv_sems.at[r - 1], device_id=(px, py, pc), device_id_type=MESH)
            cp.start()
            sends.append(cp)
        for r in range(1, N_DEV):
            px, py, pc = _peer(x, y, c, r)
            pid = 4 * px + 2 * py + pc
            pltpu.make_async_remote_copy(
                src_ref=v_ref, dst_ref=out_ref.at[pid], send_sem=send_sems.at[r - 1],
                recv_sem=recv_sems.at[r - 1], device_id=(px, py, pc), device_id_type=MESH).wait_recv()
        for cp in sends:
            cp.wait_send()

    return pl.pallas_call(
        body, name="all_gather_small",
        out_shape=jax.ShapeDtypeStruct((N_DEV, 1, n), F32),
        in_specs=[VMEM_SPEC], out_specs=VMEM_SPEC,
        scratch_shapes=[pltpu.SemaphoreType.DMA((N_DEV - 1,)), pltpu.SemaphoreType.DMA((N_DEV - 1,))],
        compiler_params=_params(),
    )(v)


def _mm_nt(a, b, name, out_dtype, bias=None, carry=None):
    m, k = a.shape
    n = b.shape[0]
    tm = _pick(m, (512, 256, 128))
    tn = _pick(n, (1408, 1152, 1024, 768, 512, 256, 128))

    def body(*refs):
        acc = _dot_nt(refs[0][...], refs[1][...])
        if bias is not None:
            acc = acc + refs[2][...]
        refs[-1][...] = acc.astype(out_dtype)

    in_specs = [pl.BlockSpec((tm, k), lambda j, i: (i, 0)), pl.BlockSpec((tn, k), lambda j, i: (j, 0))]
    args = [a, b]
    if bias is not None:
        in_specs.append(pl.BlockSpec((1, tn), lambda j, i: (0, j)))
        args.append(bias)
    return _call(body, name=name, grid=(n // tn, m // tm), in_specs=in_specs,
                 out_specs=pl.BlockSpec((tm, tn), lambda j, i: (i, j)),
                 out_shape=jax.ShapeDtypeStruct((m, n), out_dtype), args=args,
                 sem=("parallel", "parallel"), carry=carry)


class _Tail:
    def __init__(self, rows, vecs, outs, fn):
        self.rows, self.vecs, self.outs, self.fn = list(rows), list(vecs), list(outs), fn


def _mm_nn(pairs, name, out_dtype, bias=None, carry=None, tail=None):
    m, k = pairs[0][0].shape
    n = pairs[0][1].shape[1]
    n_p = len(pairs)
    tm = _pick(m, (512, 256, 128))
    tk = k if n_p == 1 else _pick(k, (1408, 1152, 1024, 768, 512, 256, 128))
    nk = k // tk
    n_b = 0 if bias is None else 1
    n_r, n_v = (len(tail.rows), len(tail.vecs)) if tail else (0, 0)
    n_in = 2 * n_p + n_b + n_r + n_v
    n_main = 0 if out_dtype is None else 1

    def finish(acc, refs, first_tile):
        if bias is not None:
            acc = acc + refs[2 * n_p][...]
        outs = refs[n_in:-1]
        if n_main:
            outs[0][...] = acc.astype(out_dtype)
        if tail is None:
            return
        rows = [r[...] for r in refs[2 * n_p + n_b:2 * n_p + n_b + n_r]]
        vecs = [v[...] for v in refs[2 * n_p + n_b + n_r:n_in]]
        vals = tail.fn(acc, rows, vecs)
        for ref, val, (dtype, kind) in zip(outs[n_main:], vals, tail.outs):
            if kind == "row":
                ref[...] = val.astype(dtype)
            else:
                @pl.when(first_tile)
                def _(ref=ref):
                    ref[...] = jnp.zeros_like(ref)

                ref[...] += val

    def body(*refs):
        acc_ref = refs[-1]
        kk, i = pl.program_id(0), pl.program_id(1)
        part = _dot(refs[0][...], refs[1][...])
        for p in range(1, n_p):
            part = part + _dot(refs[2 * p][...], refs[2 * p + 1][...])
        if nk == 1:
            finish(part, refs, i == 0)
            return
        rows = pl.ds(pl.multiple_of(i * tm, tm), tm)

        @pl.when(kk == 0)
        def _():
            acc_ref[rows, :] = part

        if nk > 2:
            @pl.when((kk > 0) & (kk < nk - 1))
            def _():
                acc_ref[rows, :] += part

        @pl.when(kk == nk - 1)
        def _():
            finish(acc_ref[rows, :] + part, refs, i == 0)

    def last_only(kk, i):
        return (jnp.where(kk == nk - 1, i, 0), 0)

    row_spec = pl.BlockSpec((tm, n), last_only)
    vec_spec = pl.BlockSpec((1, n), lambda kk, i: (0, 0))
    in_specs, args = [], []
    for a, b in pairs:
        in_specs += [pl.BlockSpec((tm, tk), lambda kk, i: (i, kk)), pl.BlockSpec((tk, n), lambda kk, i: (kk, 0))]
        args += [a, b]
    if bias is not None:
        in_specs.append(vec_spec)
        args.append(bias)
    out_specs = [row_spec] * n_main
    out_shape = [jax.ShapeDtypeStruct((m, n), out_dtype)] if n_main else []
    if tail:
        in_specs += [row_spec] * n_r + [vec_spec] * n_v
        args += tail.rows + tail.vecs
        for dtype, kind in tail.outs:
            if kind == "row":
                out_specs.append(row_spec)
                out_shape.append(jax.ShapeDtypeStruct((m, n), dtype))
            else:
                width = n if kind == "sum" else 1
                out_specs.append(pl.BlockSpec((1, width), lambda kk, i: (0, 0)))
                out_shape.append(jax.ShapeDtypeStruct((1, width), dtype))
    if tail is None:
        out_specs, out_shape = out_specs[0], out_shape[0]
    return _call(body, name=name, grid=(nk, m // tm), in_specs=in_specs, out_specs=out_specs,
                 out_shape=out_shape, args=args,
                 scratch=[pltpu.VMEM((m, n) if nk > 1 else (8, LANES), F32)],
                 sem=("arbitrary", "arbitrary"), carry=carry)


def _rms(v):
    return lax.rsqrt(jnp.mean(v * v, axis=-1, keepdims=True) + EPS)


def _col(v):
    return jnp.sum(v, axis=0, keepdims=True)


def _tail_post_pre(x, g_post, gate, weight, g_pre, scale, shift):
    def fn(y, rows, vecs):
        (xv,), (gp, gt, g, sc, sh) = rows, vecs
        xo = xv + (weight * gt) * ((y * _rms(y)) * gp)
        return xo, ((xo * _rms(xo)) * g) * (1.0 + sc) + sh

    return _Tail([x], [g_post, gate, g_pre, scale, shift], [(F32, "row"), (BF16, "row")], fn)


def _tail_post_loss(x, target, g, gate, weight):
    def fn(y, rows, vecs):
        (xv, tv), (gv, gt) = rows, vecs
        r = _rms(y)
        yn = y * r
        err = (xv + (weight * gt) * (yn * gv)) - tv
        do = err * (1.0 / y.shape[1])
        dyn = do * ((weight * gt) * gv)
        dy = r * (dyn - yn * jnp.mean(dyn * yn, axis=-1, keepdims=True))
        return do, dy, 0.5 * _col(jnp.mean(err * err, axis=-1, keepdims=True)), _col(do * yn)

    return _Tail([x, target], [g, gate], [(F32, "row"), (BF16, "row"), (F32, "one"), (F32, "sum")], fn)


def _tail_pre_bwd(x, dres, g_pre, scale):
    def fn(dh, rows, vecs):
        (xv, dr), (g, sc) = rows, vecs
        r = _rms(xv)
        n = xv * r
        dn = dh * (g * (1.0 + sc))
        return dr + r * (dn - n * jnp.mean(dn * n, axis=-1, keepdims=True)), _col(dh * n), _col(dh)

    return _Tail([x, dres], [g_pre, scale], [(F32, "row"), (F32, "sum"), (F32, "sum")], fn)


def _tail_pre_post_bwd(x, dres, y, g_pre, scale, g_post, gate, weight):
    def fn(dh, rows, vecs):
        (xv, dr, yv), (g, sc, gp, gt) = rows, vecs
        r = _rms(xv)
        n = xv * r
        dn = dh * (g * (1.0 + sc))
        dx = dr + r * (dn - n * jnp.mean(dn * n, axis=-1, keepdims=True))
        ry = _rms(yv)
        yn = yv * ry
        dyn = dx * ((weight * gt) * gp)
        dy = ry * (dyn - yn * jnp.mean(dyn * yn, axis=-1, keepdims=True))
        return dx, dy, _col(dh * n), _col(dh), _col(dx * yn), _col(dy)

    return _Tail([x, dres, y], [g_pre, scale, g_post, gate],
                 [(F32, "row"), (BF16, "row")] + [(F32, "sum")] * 4, fn)


def _mm_tn(a, b, name, out_dtype=BF16, carry=None):
    k, m = a.shape
    n = b.shape[1]
    tm = _pick(m, (1408, 1152, 1024, 768, 512, 256, 128))
    tk = _pick(k, (512, 256, 128))
    nk = k // tk

    def body(a_ref, b_ref, o_ref, acc_ref):
        kk = pl.program_id(1)

        @pl.when(kk == 0)
        def _():
            acc_ref[...] = jnp.zeros_like(acc_ref)

        acc_ref[...] += _dot_tn(a_ref[...], b_ref[...])

        @pl.when(kk == nk - 1)
        def _():
            o_ref[...] = acc_ref[...].astype(out_dtype)

    return _call(body, name=name, grid=(m // tm, nk),
                 in_specs=[pl.BlockSpec((tk, tm), lambda i, kk: (kk, i)), pl.BlockSpec((tk, n), lambda i, kk: (kk, 0))],
                 out_specs=pl.BlockSpec((tm, n), lambda i, kk: (i, 0)),
                 out_shape=jax.ShapeDtypeStruct((m, n), out_dtype), args=[a, b],
                 scratch=[pltpu.VMEM((tm, n), F32)], sem=("parallel", "arbitrary"), carry=carry)


def _mm_tn_pair(a, b, name, carry=None):
    k, m = a.shape
    n = b.shape[1]
    rows = m // N_DEV
    n_chip = N_DEV // 2
    tm = 4 * rows
    tk = _pick(k, (1024, 512, 256, 128))
    nk = k // tk

    def body(a_ref, b_ref, p_ref, own_ref, acc_ref, keep_ref, send_ref, land_ref, send_sems, recv_sems):
        i, kk = pl.program_id(0), pl.program_id(1)
        x, y, c = _mesh_pos()

        def push(chip):
            return pltpu.make_async_remote_copy(
                src_ref=send_ref.at[chip], dst_ref=land_ref.at[chip], send_sem=send_sems.at[chip],
                recv_sem=recv_sems.at[chip], device_id=(x, y, 1 - c), device_id_type=MESH)

        if nk == 1:
            acc = _dot_tn(a_ref[...], b_ref[...])
        else:
            @pl.when(kk == 0)
            def _():
                acc_ref[...] = jnp.zeros_like(acc_ref)

            acc_ref[...] += _dot_tn(a_ref[...], b_ref[...])
            acc = acc_ref

        for t in range(2):
            @pl.when((kk == nk - 1) & (i == t))
            def _(t=t):
                for ob in range(4):
                    chip, core = 2 * t + ob // 2, ob % 2
                    blk = acc[ob * rows:(ob + 1) * rows, :]

                    @pl.when(c == core)
                    def _(chip=chip, blk=blk):
                        keep_ref[chip] = blk

                    @pl.when(c != core)
                    def _(chip=chip, blk=blk):
                        send_ref[chip] = blk.astype(BF16)
                        push(chip).start()

        @pl.when((kk == nk - 1) & (i == 1))
        def _():
            for chip in range(n_chip):
                push(chip).wait_recv()
                val = (keep_ref[chip] + land_ref[chip].astype(F32)).astype(BF16)
                p_ref[chip * rows:(chip + 1) * rows, :] = val

                @pl.when(2 * x + y == chip)
                def _(val=val):
                    own_ref[...] = val

            for chip in range(n_chip):
                push(chip).wait_send()

    return _call(body, name=name, grid=(2, nk),
                 in_specs=[pl.BlockSpec((tk, tm), lambda i, kk: (kk, i)), pl.BlockSpec((tk, n), lambda i, kk: (kk, 0))],
                 out_specs=(pl.BlockSpec((n_chip * rows, n), lambda i, kk: (0, 0)),
                            pl.BlockSpec((rows, n), lambda i, kk: (0, 0))),
                 out_shape=(jax.ShapeDtypeStruct((n_chip * rows, n), BF16), jax.ShapeDtypeStruct((rows, n), BF16)),
                 args=[a, b],
                 scratch=[pltpu.VMEM((tm, n) if nk > 1 else (8, LANES), F32), pltpu.VMEM((n_chip, rows, n), F32),
                          pltpu.VMEM((n_chip, rows, n), BF16), pltpu.VMEM((n_chip, rows, n), BF16),
                          pltpu.SemaphoreType.DMA((n_chip,)), pltpu.SemaphoreType.DMA((n_chip,))],
                 sem=("arbitrary", "arbitrary"), carry=carry)


def _ffn_up(h, wg_t, wu_t, name, carry=None):
    s, d = h.shape
    f = wg_t.shape[0]
    tm = _pick(s, (512, 256, 128))
    tf = _pick(f, (1408, 1024, 512, 256, 128))

    def body(h_ref, wg_ref, wu_ref, a_ref, b_ref, u_ref):
        hh = h_ref[...]
        for lo, hi in _pieces(tf):
            a = _dot_nt(hh, wg_ref[lo:hi, :])
            b = _dot_nt(hh, wu_ref[lo:hi, :])
            a_ref[:, lo:hi] = a.astype(BF16)
            b_ref[:, lo:hi] = b.astype(BF16)
            u_ref[:, lo:hi] = ((a * _sigmoid(a)) * b).astype(BF16)

    w_spec = pl.BlockSpec((tf, d), lambda j, i: (j, 0))
    o_spec = pl.BlockSpec((tm, tf), lambda j, i: (i, j))
    o_shape = jax.ShapeDtypeStruct((s, f), BF16)
    return _call(body, name=name, grid=(f // tf, s // tm),
                 in_specs=[pl.BlockSpec((tm, d), lambda j, i: (i, 0)), w_spec, w_spec],
                 out_specs=(o_spec, o_spec, o_spec), out_shape=(o_shape, o_shape, o_shape),
                 args=[h, wg_t, wu_t], sem=("parallel", "parallel"), carry=carry)


def _ffn_down_bwd(dy, wd, a, b, name, carry=None):
    s, d = dy.shape
    f = wd.shape[0]
    tm = _pick(s, (512, 256, 128))
    tf = _pick(f, (1408, 1024, 512, 256, 128))

    def body(dy_ref, wd_ref, a_ref, b_ref, da_ref, db_ref):
        dyv = dy_ref[...]
        for lo, hi in _pieces(tf):
            du = _dot_nt(dyv, wd_ref[lo:hi, :])
            a = a_ref[:, lo:hi].astype(F32)
            b = b_ref[:, lo:hi].astype(F32)
            sig = _sigmoid(a)
            da_ref[:, lo:hi] = (du * b * (sig * (1.0 + a * (1.0 - sig)))).astype(BF16)
            db_ref[:, lo:hi] = (du * (a * sig)).astype(BF16)

    t_spec = pl.BlockSpec((tm, tf), lambda j, i: (i, j))
    o_shape = jax.ShapeDtypeStruct((s, f), BF16)
    return _call(body, name=name, grid=(f // tf, s // tm),
                 in_specs=[pl.BlockSpec((tm, d), lambda j, i: (i, 0)), pl.BlockSpec((tf, d), lambda j, i: (j, 0)),
                           t_spec, t_spec],
                 out_specs=(t_spec, t_spec), out_shape=(o_shape, o_shape), args=[dy, wd, a, b],
                 sem=("parallel", "parallel"), carry=carry)


def _row_tile(s):
    return _pick(s, (256, 128, 64))


def _vec_spec(d):
    return pl.BlockSpec((1, d), lambda i: (0, 0))


def _pre_norm(x, g, scale, shift, name):
    s, d = x.shape
    ts = _row_tile(s)

    def body(x_ref, g_ref, sc_ref, sh_ref, h_ref):
        xv = x_ref[...]
        r = lax.rsqrt(jnp.mean(xv * xv, axis=-1, keepdims=True) + EPS)
        h_ref[...] = (((xv * r) * g_ref[...]) * (1.0 + sc_ref[...]) + sh_ref[...]).astype(BF16)

    row = pl.BlockSpec((ts, d), lambda i: (i, 0))
    return _call(body, name=name, grid=(s // ts,), in_specs=[row, _vec_spec(d), _vec_spec(d), _vec_spec(d)],
                 out_specs=row, out_shape=jax.ShapeDtypeStruct((s, d), BF16), args=[x, g, scale, shift],
                 sem=("parallel",))


def _post_norm_residual(x, y, g, gate, weight, name):
    s, d = x.shape
    ts = _row_tile(s)

    def body(x_ref, y_ref, g_ref, gate_ref, o_ref):
        yv = y_ref[...]
        r = lax.rsqrt(jnp.mean(yv * yv, axis=-1, keepdims=True) + EPS)
        o_ref[...] = x_ref[...] + (weight * gate_ref[...]) * ((yv * r) * g_ref[...])

    row = pl.BlockSpec((ts, d), lambda i: (i, 0))
    return _call(body, name=name, grid=(s // ts,), in_specs=[row, row, _vec_spec(d), _vec_spec(d)],
                 out_specs=row, out_shape=jax.ShapeDtypeStruct((s, d), F32), args=[x, y, g, gate],
                 sem=("parallel",))


def _post_norm_bwd(dout, y, g, gate, weight, name):
    s, d = y.shape
    ts = _row_tile(s)

    def body(do_ref, y_ref, g_ref, gate_ref, dy_ref, s1_ref, cs_ref):
        @pl.when(pl.program_id(0) == 0)
        def _():
            s1_ref[...] = jnp.zeros_like(s1_ref)
            cs_ref[...] = jnp.zeros_like(cs_ref)

        yv = y_ref[...]
        do = do_ref[...]
        r = lax.rsqrt(jnp.mean(yv * yv, axis=-1, keepdims=True) + EPS)
        yn = yv * r
        dyn = do * ((weight * gate_ref[...]) * g_ref[...])
        dy = r * (dyn - yn * jnp.mean(dyn * yn, axis=-1, keepdims=True))
        dy_ref[...] = dy.astype(BF16)
        s1_ref[...] += jnp.sum(do * yn, axis=0, keepdims=True)
        cs_ref[...] += jnp.sum(dy, axis=0, keepdims=True)

    row = pl.BlockSpec((ts, d), lambda i: (i, 0))
    vec = jax.ShapeDtypeStruct((1, d), F32)
    return _call(body, name=name, grid=(s // ts,), in_specs=[row, row, _vec_spec(d), _vec_spec(d)],
                 out_specs=(row, _vec_spec(d), _vec_spec(d)),
                 out_shape=(jax.ShapeDtypeStruct((s, d), BF16), vec, vec), args=[dout, y, g, gate],
                 sem=("arbitrary",))


def _pre_norm_bwd(dh, x, g, scale, dres, name):
    s, d = x.shape
    ts = _row_tile(s)

    def body(dh_ref, x_ref, g_ref, sc_ref, dr_ref, dx_ref, s2_ref, s3_ref):
        @pl.when(pl.program_id(0) == 0)
        def _():
            s2_ref[...] = jnp.zeros_like(s2_ref)
            s3_ref[...] = jnp.zeros_like(s3_ref)

        xv = x_ref[...]
        dh = dh_ref[...]
        r = lax.rsqrt(jnp.mean(xv * xv, axis=-1, keepdims=True) + EPS)
        n = xv * r
        dn = dh * (g_ref[...] * (1.0 + sc_ref[...]))
        dx_ref[...] = dr_ref[...] + r * (dn - n * jnp.mean(dn * n, axis=-1, keepdims=True))
        s2_ref[...] += jnp.sum(dh * n, axis=0, keepdims=True)
        s3_ref[...] += jnp.sum(dh, axis=0, keepdims=True)

    row = pl.BlockSpec((ts, d), lambda i: (i, 0))
    vec = jax.ShapeDtypeStruct((1, d), F32)
    return _call(body, name=name, grid=(s // ts,), in_specs=[row, row, _vec_spec(d), _vec_spec(d), row],
                 out_specs=(row, _vec_spec(d), _vec_spec(d)),
                 out_shape=(jax.ShapeDtypeStruct((s, d), F32), vec, vec), args=[dh, x, g, scale, dres],
                 sem=("arbitrary",))


def _post_pre_norm(x, y, g_post, gate, weight, g_pre, scale, shift, name):
    s, d = x.shape
    ts = _row_tile(s)

    def body(x_ref, y_ref, gp_ref, gate_ref, g_ref, sc_ref, sh_ref, o_ref, h_ref):
        yv = y_ref[...]
        r = lax.rsqrt(jnp.mean(yv * yv, axis=-1, keepdims=True) + EPS)
        xv = x_ref[...] + (weight * gate_ref[...]) * ((yv * r) * gp_ref[...])
        o_ref[...] = xv
        r2 = lax.rsqrt(jnp.mean(xv * xv, axis=-1, keepdims=True) + EPS)
        h_ref[...] = (((xv * r2) * g_ref[...]) * (1.0 + sc_ref[...]) + sh_ref[...]).astype(BF16)

    row = pl.BlockSpec((ts, d), lambda i: (i, 0))
    return _call(body, name=name, grid=(s // ts,), in_specs=[row, row] + [_vec_spec(d)] * 5,
                 out_specs=(row, row),
                 out_shape=(jax.ShapeDtypeStruct((s, d), F32), jax.ShapeDtypeStruct((s, d), BF16)),
                 args=[x, y, g_post, gate, g_pre, scale, shift], sem=("parallel",))


def _post_norm_loss_bwd(x, y, g, gate, weight, target, name):
    s, d = y.shape
    ts = _row_tile(s)

    def body(x_ref, y_ref, g_ref, gate_ref, t_ref, dx_ref, dy_ref, l_ref, s1_ref):
        @pl.when(pl.program_id(0) == 0)
        def _():
            l_ref[...] = jnp.zeros_like(l_ref)
            s1_ref[...] = jnp.zeros_like(s1_ref)

        yv = y_ref[...]
        r = lax.rsqrt(jnp.mean(yv * yv, axis=-1, keepdims=True) + EPS)
        yn = yv * r
        err = (x_ref[...] + (weight * gate_ref[...]) * (yn * g_ref[...])) - t_ref[...]
        do = err * (1.0 / d)
        dx_ref[...] = do
        l_ref[...] += 0.5 * jnp.sum(jnp.mean(err * err, axis=-1, keepdims=True), axis=0, keepdims=True)
        dyn = do * ((weight * gate_ref[...]) * g_ref[...])
        dy_ref[...] = (r * (dyn - yn * jnp.mean(dyn * yn, axis=-1, keepdims=True))).astype(BF16)
        s1_ref[...] += jnp.sum(do * yn, axis=0, keepdims=True)

    row = pl.BlockSpec((ts, d), lambda i: (i, 0))
    return _call(body, name=name, grid=(s // ts,), in_specs=[row, row, _vec_spec(d), _vec_spec(d), row],
                 out_specs=(row, row, pl.BlockSpec((1, 1), lambda i: (0, 0)), _vec_spec(d)),
                 out_shape=(jax.ShapeDtypeStruct((s, d), F32), jax.ShapeDtypeStruct((s, d), BF16),
                            jax.ShapeDtypeStruct((1, 1), F32), jax.ShapeDtypeStruct((1, d), F32)),
                 args=[x, y, g, gate, target], sem=("arbitrary",))


def _pre_post_norm_bwd(dh, x, g_pre, scale, dres, y, g_post, gate, weight, name):
    s, d = x.shape
    ts = _row_tile(s)

    def body(dh_ref, x_ref, g_ref, sc_ref, dr_ref, y_ref, gp_ref, gate_ref,
             dx_ref, dy_ref, s2_ref, s3_ref, s1_ref, cs_ref):
        @pl.when(pl.program_id(0) == 0)
        def _():
            for ref in (s2_ref, s3_ref, s1_ref, cs_ref):
                ref[...] = jnp.zeros_like(ref)

        xv = x_ref[...]
        dh = dh_ref[...]
        r = lax.rsqrt(jnp.mean(xv * xv, axis=-1, keepdims=True) + EPS)
        n = xv * r
        dn = dh * (g_ref[...] * (1.0 + sc_ref[...]))
        dx = dr_ref[...] + r * (dn - n * jnp.mean(dn * n, axis=-1, keepdims=True))
        dx_ref[...] = dx
        s2_ref[...] += jnp.sum(dh * n, axis=0, keepdims=True)
        s3_ref[...] += jnp.sum(dh, axis=0, keepdims=True)
        yv = y_ref[...]
        ry = lax.rsqrt(jnp.mean(yv * yv, axis=-1, keepdims=True) + EPS)
        yn = yv * ry
        dyn = dx * ((weight * gate_ref[...]) * gp_ref[...])
        dy = ry * (dyn - yn * jnp.mean(dyn * yn, axis=-1, keepdims=True))
        dy_ref[...] = dy.astype(BF16)
        s1_ref[...] += jnp.sum(dx * yn, axis=0, keepdims=True)
        cs_ref[...] += jnp.sum(dy, axis=0, keepdims=True)

    row = pl.BlockSpec((ts, d), lambda i: (i, 0))
    vec = jax.ShapeDtypeStruct((1, d), F32)
    return _call(body, name=name, grid=(s // ts,),
                 in_specs=[row, row, _vec_spec(d), _vec_spec(d), row, row, _vec_spec(d), _vec_spec(d)],
                 out_specs=(row, row) + (_vec_spec(d),) * 4,
                 out_shape=(jax.ShapeDtypeStruct((s, d), F32), jax.ShapeDtypeStruct((s, d), BF16), vec, vec, vec, vec),
                 args=[dh, x, g_pre, scale, dres, y, g_post, gate], sem=("arbitrary",))


def _group_norm_cat(oa, ob, ga, gb):
    s = oa.shape[0]
    ts = _row_tile(s)

    def body(oa_ref, ob_ref, ga_ref, gb_ref, y_ref):
        for o_ref, g_ref, lo, w in ((oa_ref, ga_ref, 0, QA), (ob_ref, gb_ref, QA, QB)):
            ov = o_ref[...]
            r = lax.rsqrt(jnp.mean(ov * ov, axis=-1, keepdims=True) + EPS)
            y_ref[:, lo:lo + w] = ((ov * r) * g_ref[...]).astype(BF16)

    return _call(body, name="group_norm_cat", grid=(s // ts,),
                 in_specs=[pl.BlockSpec((ts, QA), lambda i: (i, 0)), pl.BlockSpec((ts, QB), lambda i: (i, 0)),
                           _vec_spec(QA), _vec_spec(QB)],
                 out_specs=pl.BlockSpec((ts, QA + QB), lambda i: (i, 0)),
                 out_shape=jax.ShapeDtypeStruct((s, QA + QB), BF16), args=[oa, ob, ga, gb], sem=("parallel",))


def _group_norm_bwd(dy, oa, ob, ga, gb):
    s = oa.shape[0]
    ts = _row_tile(s)

    def body(dy_ref, oa_ref, ob_ref, ga_ref, gb_ref, doa_ref, dob_ref, dga_ref, dgb_ref):
        @pl.when(pl.program_id(0) == 0)
        def _():
            dga_ref[...] = jnp.zeros_like(dga_ref)
            dgb_ref[...] = jnp.zeros_like(dgb_ref)

        for o_ref, g_ref, do_ref, dg_ref, lo, w in ((oa_ref, ga_ref, doa_ref, dga_ref, 0, QA),
                                                    (ob_ref, gb_ref, dob_ref, dgb_ref, QA, QB)):
            ov = o_ref[...]
            dyv = dy_ref[:, lo:lo + w]
            r = lax.rsqrt(jnp.mean(ov * ov, axis=-1, keepdims=True) + EPS)
            n = ov * r
            dn = dyv * g_ref[...]
            do_ref[...] = r * (dn - n * jnp.mean(dn * n, axis=-1, keepdims=True))
            dg_ref[...] += jnp.sum(dyv * n, axis=0, keepdims=True)

    ra = pl.BlockSpec((ts, QA), lambda i: (i, 0))
    rb = pl.BlockSpec((ts, QB), lambda i: (i, 0))
    return _call(body, name="group_norm_bwd", grid=(s // ts,),
                 in_specs=[pl.BlockSpec((ts, QA + QB), lambda i: (i, 0)), ra, rb, _vec_spec(QA), _vec_spec(QB)],
                 out_specs=(ra, rb, _vec_spec(QA), _vec_spec(QB)),
                 out_shape=(jax.ShapeDtypeStruct((s, QA), F32), jax.ShapeDtypeStruct((s, QB), F32),
                            jax.ShapeDtypeStruct((1, QA), F32), jax.ShapeDtypeStruct((1, QB), F32)),
                 args=[dy, oa, ob, ga, gb], sem=("arbitrary",))


def _loss_and_grad(y, target):
    s, d = y.shape
    ts = _row_tile(s)

    def body(y_ref, t_ref, l_ref, g_ref):
        @pl.when(pl.program_id(0) == 0)
        def _():
            l_ref[...] = jnp.zeros_like(l_ref)

        err = y_ref[...] - t_ref[...]
        g_ref[...] = err * (1.0 / d)
        row = jnp.mean(err * err, axis=-1, keepdims=True)
        l_ref[...] += 0.5 * jnp.sum(row, axis=0, keepdims=True)

    row = pl.BlockSpec((ts, d), lambda i: (i, 0))
    return _call(body, name="loss_and_grad", grid=(s // ts,), in_specs=[row, row],
                 out_specs=(pl.BlockSpec((1, 1), lambda i: (0, 0)), row),
                 out_shape=(jax.ShapeDtypeStruct((1, 1), F32), jax.ShapeDtypeStruct((s, d), F32)),
                 args=[y, target], sem=("arbitrary",))


def _col_sum(x, name):
    s, n = x.shape
    ts = _row_tile(s)

    def body(x_ref, o_ref):
        @pl.when(pl.program_id(0) == 0)
        def _():
            o_ref[...] = jnp.zeros_like(o_ref)

        o_ref[...] += jnp.sum(x_ref[...].astype(F32), axis=0, keepdims=True)

    return _call(body, name=name, grid=(s // ts,), in_specs=[pl.BlockSpec((ts, n), lambda i: (i, 0))],
                 out_specs=pl.BlockSpec((1, n), lambda i: (0, 0)), out_shape=jax.ShapeDtypeStruct((1, n), F32),
                 args=[x], sem=("arbitrary",))


def _n_variants(n_back):
    return -(-n_back // QG) + 1


def _alibi_bias():
    i = np.arange(QROWS)[:, None]
    j = np.arange((QG + BACK_A) * CHUNK)[None, :]
    dist = np.abs(BACK_A * CHUNK + i - j).astype(np.float32)
    dc = j // CHUNK - i // CHUNK
    valid = (dc >= 0) & (dc <= BACK_A)
    slopes = np.array([2.0 ** (-8.0 * (h + 1) / H_A) for h in range(H_A)], dtype=np.float32)
    bias = -slopes[:, None, None] * dist[None]
    out = [np.where((valid & (j >= (BACK_A - QG * v) * CHUNK))[None], bias, np.float32(NEG_INF))
           for v in range(_n_variants(BACK_A))]
    return jnp.asarray(np.stack(out).astype(np.float32))


def _rel_index_matrix():
    cc = np.arange(SKEW)
    dist = np.where(cc < SKEW - QROWS, BACK_B * CHUNK - cc, BACK_B * CHUNK + SKEW - cc)
    idx = np.clip(dist, -REL_CLIP, REL_CLIP) + REL_CLIP
    m = np.zeros((SKEW, N_REL), np.float32)
    m[cc, idx] = 1.0
    return jnp.asarray(m)


def _toeplitz_bias(vec, carry=None):
    lk = (QG + BACK_B) * CHUNK
    nv = _n_variants(BACK_B)

    def body(v_ref, o_ref):
        xv = jnp.broadcast_to(v_ref[0], (QROWS, SKEW))
        row = lax.broadcasted_iota(jnp.int32, (QROWS, SKEW), 0)
        for bit in range(QROWS.bit_length() - 1):
            xv = jnp.where((row >> bit) & 1 == 1, pltpu.roll(xv, 1 << bit, 1), xv)
        ri = lax.broadcasted_iota(jnp.int32, (QROWS, lk), 0) // CHUNK
        col = lax.broadcasted_iota(jnp.int32, (QROWS, lk), 1)
        ci = col // CHUNK
        valid = (ci - ri >= 0) & (ci - ri <= BACK_B)
        for v in range(nv):
            o_ref[v, 0] = jnp.where(valid & (col >= (BACK_B - QG * v) * CHUNK), xv[:, :lk], NEG_INF)

    return _call(body, name="toeplitz_bias", grid=(H_B,),
                 in_specs=[pl.BlockSpec((1, 1, SKEW), lambda h: (h, 0, 0))],
                 out_specs=pl.BlockSpec((nv, 1, QROWS, lk), lambda h: (0, h, 0, 0)),
                 out_shape=jax.ShapeDtypeStruct((nv, H_B, QROWS, lk), F32), args=[vec], sem=("parallel",),
                 carry=carry)


def _diagonal_sums(dbias):
    lk = dbias.shape[2]

    def body(d_ref, o_ref):
        xp = jnp.concatenate([d_ref[0], jnp.zeros((QROWS, SKEW - lk), F32)], axis=1)
        xv = xp[0:CHUNK]
        for q in range(1, QG):
            xv = xv + pltpu.roll(xp[q * CHUNK:(q + 1) * CHUNK], SKEW - q * CHUNK, 1)
        row = lax.broadcasted_iota(jnp.int32, (CHUNK, SKEW), 0)
        for bit in range(CHUNK.bit_length() - 1):
            xv = jnp.where((row >> bit) & 1 == 1, pltpu.roll(xv, SKEW - (1 << bit), 1), xv)
        o_ref[0] = jnp.sum(xv, axis=0, keepdims=True)

    return _call(body, name="diagonal_sums", grid=(H_B,),
                 in_specs=[pl.BlockSpec((1, QROWS, lk), lambda h: (h, 0, 0))],
                 out_specs=pl.BlockSpec((1, 1, SKEW), lambda h: (h, 0, 0)),
                 out_shape=jax.ShapeDtypeStruct((H_B, 1, SKEW), F32), args=[dbias], sem=("parallel",))


def _attn_common(s, n_back, gqa, q_col, k_col, v_col):
    lk = (QG + n_back) * CHUNK
    pad = n_back * CHUNK
    wide = TPS * LANES
    q_spec = pl.BlockSpec((QROWS, wide), lambda t, g: (g, q_col // TPS + t))
    if gqa:
        k_spec = pl.BlockSpec((s, LANES), lambda t, g: (0, k_col))
        v_spec = pl.BlockSpec((s, LANES), lambda t, g: (0, v_col))
    else:
        k_spec = pl.BlockSpec((s, wide), lambda t, g: (0, k_col // TPS + t))
        v_spec = pl.BlockSpec((s, wide), lambda t, g: (0, v_col // TPS + t))
    last_variant = _n_variants(n_back) - 1
    bias_spec = pl.BlockSpec((None, 2 * TPS, QROWS, lk), lambda t, g: (jnp.minimum(g, last_variant), t, 0, 0))
    tile_spec = pl.BlockSpec((QROWS, wide), lambda t, g: (g, t))
    return lk, pad, q_spec, k_spec, v_spec, bias_spec, tile_spec


def _attention_fwd(proj, bias, sinks, *, n_back, gqa, q_col, k_col, v_col, name, carry=None):
    s = proj.shape[0]
    lk, pad, q_spec, k_spec, v_spec, bias_spec, tile_spec = _attn_common(s, n_back, gqa, q_col, k_col, v_col)
    n_t, n_g = 512 // (TPS * LANES), s // QROWS
    kv_wide = LANES if gqa else TPS * LANES

    def body(*refs):
        if gqa:
            q_ref, k_ref, v_ref, bias_ref, sink_ref, o_ref, l_ref, kpad, vpad = refs
        else:
            q_ref, k_ref, v_ref, bias_ref, o_ref, l_ref, kpad, vpad = refs
        t, g = pl.program_id(0), pl.program_id(1)

        @pl.when(g == 0)
        def _():
            kpad[0:pad, :] = jnp.zeros((pad, kv_wide), BF16)
            vpad[0:pad, :] = jnp.zeros((pad, kv_wide), BF16)
            kpad[pad:, :] = k_ref[...]
            vpad[pad:, :] = v_ref[...]

        start = pl.multiple_of(g * QROWS, QROWS)
        half = lax.broadcasted_iota(jnp.int32, (QROWS, LANES), 1) // HEAD_DIM
        for tt in range(TPS):
            lanes = slice(tt * LANES, (tt + 1) * LANES)
            kv_lanes = slice(0, LANES) if gqa else lanes
            kb = kpad[pl.ds(start, lk), kv_lanes]
            vb = vpad[pl.ds(start, lk), kv_lanes]
            q = q_ref[:, lanes] * (HEAD_DIM ** -0.5)
            if gqa:
                hk = (TPS * t + tt) // 2
                q_rolled = pltpu.roll(q.astype(F32), HEAD_DIM, 1).astype(BF16)
            outs, lses = [], []
            for e in range(2):
                if gqa:
                    kv_half = hk
                    src = jnp.where(hk == e, q, q_rolled)
                else:
                    kv_half = e
                    src = q
                qm = jnp.where(half == kv_half, src, jnp.zeros_like(src))
                sc = _dot_nt(qm, kb) + bias_ref[2 * tt + e]
                m = jnp.max(sc, axis=-1, keepdims=True)
                if gqa:
                    sk = sink_ref[2 * (TPS * t + tt) + e]
                    m = jnp.maximum(m, sk)
                p = jnp.exp(sc - m)
                l = jnp.sum(p, axis=-1, keepdims=True)
                if gqa:
                    l = l + jnp.exp(sk - m)
                pn = p / l
                outs.append(_dot(pn.astype(BF16), vb))
                lses.append(m + jnp.log(l))
            if gqa:
                same = jnp.where(hk == 0, outs[0], outs[1])
                other = jnp.where(hk == 0, outs[1], outs[0])
                o_ref[:, lanes] = jnp.where(half == hk, same, pltpu.roll(other, HEAD_DIM, 1))
            else:
                o_ref[:, lanes] = jnp.where(half == 0, outs[0], outs[1])
            l_ref[:, lanes] = jnp.where(half == 0, lses[0], lses[1])

    in_specs = [q_spec, k_spec, v_spec, bias_spec] + ([SMEM_SPEC] if gqa else [])
    args = [proj, proj, proj, bias] + ([sinks] if gqa else [])
    o_shape = jax.ShapeDtypeStruct((s, 512), F32)
    return _call(body, name=name, grid=(n_t, n_g), in_specs=in_specs, out_specs=(tile_spec, tile_spec),
                 out_shape=(o_shape, o_shape), args=args,
                 scratch=[pltpu.VMEM((s + pad, kv_wide), BF16), pltpu.VMEM((s + pad, kv_wide), BF16)],
                 sem=("arbitrary", "arbitrary"), carry=carry)


def _attention_bwd(proj, bias, sinks, do, lse, *, n_back, gqa, q_col, k_col, v_col, name, carry=None):
    s = proj.shape[0]
    lk, pad, q_spec, k_spec, v_spec, bias_spec, tile_spec = _attn_common(s, n_back, gqa, q_col, k_col, v_col)
    n_t, n_g = 512 // (TPS * LANES), s // QROWS
    kv_wide = LANES if gqa else TPS * LANES

    def body(*refs):
        if gqa:
            (q_ref, k_ref, v_ref, bias_ref, sink_ref, do_ref, l_ref,
             dq_ref, dk_ref, dv_ref, dsink_ref, kpad, vpad, dkpad, dvpad) = refs
        else:
            (q_ref, k_ref, v_ref, bias_ref, do_ref, l_ref,
             dq_ref, dk_ref, dv_ref, dbias_ref, kpad, vpad, dkpad, dvpad) = refs
        t, g = pl.program_id(0), pl.program_id(1)

        @pl.when(g == 0)
        def _():
            kpad[0:pad, :] = jnp.zeros((pad, kv_wide), BF16)
            vpad[0:pad, :] = jnp.zeros((pad, kv_wide), BF16)
            kpad[pad:, :] = k_ref[...]
            vpad[pad:, :] = v_ref[...]
            if gqa:
                dsink_ref[...] = jnp.zeros_like(dsink_ref)
            else:
                dbias_ref[...] = jnp.zeros_like(dbias_ref)

        @pl.when((g == 0) & (t == 0) if gqa else g == 0)
        def _():
            dkpad[...] = jnp.zeros_like(dkpad)
            dvpad[...] = jnp.zeros_like(dvpad)

        start = pl.multiple_of(g * QROWS, QROWS)
        half = lax.broadcasted_iota(jnp.int32, (QROWS, LANES), 1) // HEAD_DIM
        for tt in range(TPS):
            lanes = slice(tt * LANES, (tt + 1) * LANES)
            kv_lanes = slice(0, LANES) if gqa else lanes
            kb = kpad[pl.ds(start, lk), kv_lanes]
            vb = vpad[pl.ds(start, lk), kv_lanes]
            q = q_ref[:, lanes]
            dov = do_ref[:, lanes]
            lv = l_ref[:, lanes]
            if gqa:
                hk = (TPS * t + tt) // 2
                q_rolled = pltpu.roll(q.astype(F32), HEAD_DIM, 1).astype(BF16)
                do_rolled = pltpu.roll(dov, HEAD_DIM, 1)
            dqs = []
            dk_acc = jnp.zeros((lk, LANES), F32)
            dv_acc = jnp.zeros((lk, LANES), F32)
            for e in range(2):
                if gqa:
                    kv_half = hk
                    src = jnp.where(hk == e, q, q_rolled)
                    do_src = jnp.where(hk == e, dov, do_rolled)
                else:
                    kv_half = e
                    src = q
                    do_src = dov
                qm = jnp.where(half == kv_half, src, jnp.zeros_like(src))
                dom = jnp.where(half == kv_half, do_src, 0.0).astype(BF16)
                lcol = jnp.max(jnp.where(half == e, lv, -jnp.inf), axis=-1, keepdims=True)
                sc = _dot_nt(qm * (HEAD_DIM ** -0.5), kb) + bias_ref[2 * tt + e]
                pn = jnp.exp(sc - lcol)
                dp = _dot_nt(dom, vb)
                delta = jnp.sum(pn * dp, axis=-1, keepdims=True)
                ds = pn * (dp - delta)
                if gqa:
                    p_sink = jnp.exp(sink_ref[2 * (TPS * t + tt) + e] - lcol)
                    dsk = -jnp.sum(p_sink * delta, axis=0, keepdims=True)
                    row = 2 * tt + e
                    dsink_ref[0, row:row + 1, :] += jnp.broadcast_to(dsk, (1, LANES))
                else:
                    dbias_ref[2 * tt + e] += ds
                dsb = (ds * (HEAD_DIM ** -0.5)).astype(BF16)
                dqs.append(_dot(dsb, kb))
                dk_acc = dk_acc + _dot_tn(dsb, qm)
                dv_acc = dv_acc + _dot_tn(pn.astype(BF16), dom)
            dkpad[pl.ds(start, lk), kv_lanes] += dk_acc
            dvpad[pl.ds(start, lk), kv_lanes] += dv_acc
            if gqa:
                same = jnp.where(hk == 0, dqs[0], dqs[1])
                other = jnp.where(hk == 0, dqs[1], dqs[0])
                dq_ref[:, lanes] = jnp.where(half == hk, same, pltpu.roll(other, HEAD_DIM, 1)).astype(BF16)
            else:
                dq_ref[:, lanes] = jnp.where(half == 0, dqs[0], dqs[1]).astype(BF16)

        @pl.when((g == n_g - 1) & (t == n_t - 1) if gqa else g == n_g - 1)
        def _():
            dk_ref[...] = dkpad[pad:, :].astype(BF16)
            dv_ref[...] = dvpad[pad:, :].astype(BF16)

    in_specs = [q_spec, k_spec, v_spec, bias_spec] + ([SMEM_SPEC] if gqa else []) + [tile_spec, tile_spec]
    args = [proj, proj, proj, bias] + ([sinks] if gqa else []) + [do, lse]
    if gqa:
        kv_out = pl.BlockSpec((s, LANES), lambda t, g: (0, 0))
        kv_shape = jax.ShapeDtypeStruct((s, LANES), BF16)
        extra_spec = pl.BlockSpec((1, 8, LANES), lambda t, g: (t, 0, 0))
        extra_shape = jax.ShapeDtypeStruct((n_t, 8, LANES), F32)
    else:
        kv_out = pl.BlockSpec((s, kv_wide), lambda t, g: (0, t))
        kv_shape = jax.ShapeDtypeStruct((s, 512), BF16)
        extra_spec = pl.BlockSpec((2 * TPS, QROWS, lk), lambda t, g: (t, 0, 0))
        extra_shape = jax.ShapeDtypeStruct(bias.shape[1:], F32)
    return _call(body, name=name, grid=(n_t, n_g), in_specs=in_specs,
                 out_specs=(tile_spec, kv_out, kv_out, extra_spec),
                 out_shape=(jax.ShapeDtypeStruct((s, 512), BF16), kv_shape, kv_shape, extra_shape), args=args,
                 scratch=[pltpu.VMEM((s + pad, kv_wide), BF16), pltpu.VMEM((s + pad, kv_wide), BF16),
                          pltpu.VMEM((s + pad, kv_wide), F32), pltpu.VMEM((s + pad, kv_wide), F32)],
                 sem=("arbitrary", "arbitrary"), carry=carry)


def _sum_slots(r, name):
    n_slots, rows, k = r.shape

    def body(r_ref, o_ref):
        acc = r_ref[0].astype(F32)
        for j in range(1, n_slots):
            acc = acc + r_ref[j].astype(F32)
        o_ref[...] = acc

    return _call(body, name=name, grid=(k // LANES,),
                 in_specs=[pl.BlockSpec((n_slots, rows, LANES), lambda i: (0, 0, i))],
                 out_specs=pl.BlockSpec((rows, LANES), lambda i: (0, i)),
                 out_shape=jax.ShapeDtypeStruct((rows, k), F32), args=[r], sem=("parallel",))


def _sum_rows8(g):
    n = g.shape[2]

    def body(g_ref, o_ref):
        acc = g_ref[0]
        for j in range(1, N_DEV):
            acc = acc + g_ref[j]
        o_ref[...] = acc

    return pl.pallas_call(
        body, name="sum_small_grads", in_specs=[VMEM_SPEC], out_specs=VMEM_SPEC,
        out_shape=jax.ShapeDtypeStruct((1, n), F32), compiler_params=_params(),
    )(g)


def _ada_weight_grad(sc_t, dmod_cols):
    d = sc_t.shape[0]
    w = dmod_cols.shape[1]
    td = _pick(d, (256, 128))

    def body(sc_ref, dm_ref, o_ref):
        scv = sc_ref[...]
        dmv = dm_ref[...]
        acc = scv[:, 0:1] * dmv[0:1, :]
        for b in range(1, N_DEV):
            acc = acc + scv[:, b:b + 1] * dmv[b:b + 1, :]
        o_ref[...] = acc

    return _call(body, name="ada_weight_grad", grid=(d // td,),
                 in_specs=[pl.BlockSpec((td, N_DEV), lambda i: (i, 0)), pl.BlockSpec((N_DEV, w), lambda i: (0, 0))],
                 out_specs=pl.BlockSpec((td, w), lambda i: (i, 0)), out_shape=jax.ShapeDtypeStruct((d, w), F32),
                 args=[sc_t, dmod_cols], sem=("parallel",))


def _adamw_update(w, gv, m, v):
    nm = ADAM_B1 * m + (1.0 - ADAM_B1) * gv
    nv = ADAM_B2 * v + (1.0 - ADAM_B2) * (gv * gv)
    m_hat = nm / (1.0 - ADAM_B1 ** ADAM_STEP)
    v_hat = nv / (1.0 - ADAM_B2 ** ADAM_STEP)
    return -ADAM_LR * (m_hat / (jnp.sqrt(v_hat) + ADAM_EPS) + ADAM_WD * w), nm, nv


def _adamw(w, g, m, v, name):
    rows, cols = w.shape
    tr = _pick(rows, (256, 176, 128, 88, 64)) if rows > 256 else rows

    def body(w_ref, g_ref, m_ref, v_ref, d_ref, nm_ref, nv_ref):
        d_ref[...], nm_ref[...], nv_ref[...] = _adamw_update(w_ref[...], g_ref[...], m_ref[...], v_ref[...])

    spec = pl.BlockSpec((tr, cols), lambda i: (i, 0))
    shape = jax.ShapeDtypeStruct((rows, cols), F32)
    return _call(body, name=name, grid=(rows // tr,), in_specs=[spec] * 4, out_specs=(spec, spec, spec),
                 out_shape=(shape, shape, shape), args=[w, g, m, v], sem=("parallel",))


def _adamw_from_slots(w, own, slots, m, v, name):
    n_slots, rows, k = slots.shape

    def body(o_ref, s_ref, w_ref, m_ref, v_ref, g_ref, d_ref, nm_ref, nv_ref):
        gv = o_ref[...].astype(F32)
        for j in range(n_slots):
            gv = gv + s_ref[j].astype(F32)
        g_ref[...] = gv
        d_ref[...], nm_ref[...], nv_ref[...] = _adamw_update(w_ref[...], gv, m_ref[...], v_ref[...])

    tr = rows // 2 if rows % 32 == 0 else rows
    spec = pl.BlockSpec((tr, k), lambda i: (i, 0))
    shape = jax.ShapeDtypeStruct((rows, k), F32)
    return _call(body, name=name, grid=(rows // tr,),
                 in_specs=[spec, pl.BlockSpec((n_slots, tr, k), lambda i: (0, i, 0)), spec, spec, spec],
                 out_specs=(spec, spec, spec, spec), out_shape=(shape, shape, shape, shape),
                 args=[own, slots, w, m, v], sem=("parallel",))


def _adamw_small(g, w, m, v, sizes):
    n = w.shape[1]
    offs, off = [], 0
    for size in sizes:
        offs.append(off)
        off += size + (-size % LANES)

    def body(g_ref, w_ref, m_ref, v_ref, *out_refs):
        gv = g_ref[:, 0:n]
        dv, nm, nv = _adamw_update(w_ref[...], gv, m_ref[...], v_ref[...])
        for j, (o, size) in enumerate(zip(offs, sizes)):
            for k, val in enumerate((gv, dv, nm, nv)):
                out_refs[4 * j + k][...] = val[:, o:o + size]

    shapes = [jax.ShapeDtypeStruct((1, size), F32) for size in sizes for _ in range(4)]
    return pl.pallas_call(
        body, name="adamw_small", in_specs=[VMEM_SPEC] * 4, out_specs=tuple([VMEM_SPEC] * len(shapes)),
        out_shape=tuple(shapes), compiler_params=_params(),
    )(g, w, m, v)


SMALL = ("b_ada", "g_pre_ffn1", "g_post_ffn1", "g_pre_mix", "b_in", "sinks_a", "rel_bias_b", "g_grp_a",
         "g_grp_b", "b_out", "g_post_mix", "g_pre_ffn2", "g_post_ffn2")
WEIGHTS = ("w_ada", "b_ada", "g_pre_ffn1", "w_gate1", "w_up1", "w_down1", "g_post_ffn1", "g_pre_mix", "w_in",
           "b_in", "sinks_a", "rel_bias_b", "g_grp_a", "g_grp_b", "w_out", "b_out", "g_post_mix", "g_pre_ffn2",
           "w_gate2", "w_up2", "w_down2", "g_post_ffn2")


def kernel(x, c, w_ada, b_ada, g_pre_ffn1, w_gate1, w_up1, w_down1, g_post_ffn1, g_pre_mix, w_in, b_in, sinks_a, rel_bias_b, g_grp_a, g_grp_b, w_out, b_out, g_post_mix, g_pre_ffn2, w_gate2, w_up2, w_down2, g_post_ffn2, loss_target, m_w_ada, m_b_ada, m_g_pre_ffn1, m_w_gate1, m_w_up1, m_w_down1, m_g_post_ffn1, m_g_pre_mix, m_w_in, m_b_in, m_sinks_a, m_rel_bias_b, m_g_grp_a, m_g_grp_b, m_w_out, m_b_out, m_g_post_mix, m_g_pre_ffn2, m_w_gate2, m_w_up2, m_w_down2, m_g_post_ffn2, v_w_ada, v_b_ada, v_g_pre_ffn1, v_w_gate1, v_w_up1, v_w_down1, v_g_post_ffn1, v_g_pre_mix, v_w_in, v_b_in, v_sinks_a, v_rel_bias_b, v_g_grp_a, v_g_grp_b, v_w_out, v_b_out, v_g_post_mix, v_g_pre_ffn2, v_w_gate2, v_w_up2, v_w_down2, v_g_post_ffn2):
    given = dict(locals())
    weights = {n: given[n] for n in WEIGHTS}
    mom_m = {n: given["m_" + n] for n in WEIGHTS}
    mom_v = {n: given["v_" + n] for n in WEIGHTS}

    me = 4 * lax.axis_index("x") + 2 * lax.axis_index("y") + lax.axis_index("c")
    xs = x[0]
    tgt = loss_target[0]
    d_model = xs.shape[1]
    ada_cols = w_ada.shape[2]

    sh = {"wg1": w_gate1[0].T, "wu1": w_up1[0].T, "wd1": w_down1[0], "win": w_in[0].T, "wo": w_out[0],
          "wg2": w_gate2[0].T, "wu2": w_up2[0].T, "wd2": w_down2[0]}
    sh = {k: v.astype(BF16) for k, v in sh.items()}

    def gather(*names):
        return _gather_carry([sh[n] for n in names])

    bias_a = _alibi_bias()
    rel_m = _rel_index_matrix()
    rel_vec = jnp.dot(rel_bias_b[0], rel_m.T, precision=lax.Precision.HIGHEST)
    bias_b, (wg1, wu1) = _toeplitz_bias(rel_vec.reshape(H_B, 1, SKEW), carry=gather("wg1", "wu1"))

    b_cols = lax.dynamic_slice(b_ada, (0, me * ada_cols), (1, ada_cols))
    (sc_all, mod_rows), _ = _ada_forward(c, w_ada[0], b_cols, _Carry([], [], [], lambda *a: None, lambda *a: None))
    mod = mod_rows.reshape(N_MOD, d_model)
    shift1, scale1, gate1, shift2, scale2, gate2, shift3, scale3, gate3 = (mod[i:i + 1] for i in range(N_MOD))

    h1 = _pre_norm(xs, g_pre_ffn1, scale1, shift1, "pre_norm_ffn1")
    (a1, b1, u1), (wd1,) = _ffn_up(h1, wg1, wu1, "ffn_up_ffn1", carry=gather("wd1"))
    (y1, x1, h2), (win,) = _mm_nn(
        [(u1, wd1)], "ffn_down_ffn1", F32, carry=gather("win"),
        tail=_tail_post_pre(xs, g_post_ffn1, gate1, 0.5, g_pre_mix, scale2, shift2))

    proj, (wo,) = _mm_nt(h2, win, "in_proj", BF16, bias=b_in, carry=gather("wo"))
    sinks = sinks_a[0]
    cfg_a = dict(n_back=BACK_A, gqa=True, q_col=0, k_col=QA // LANES, v_col=(QA + KVA) // LANES)
    cfg_b = dict(n_back=BACK_B, gqa=False, q_col=(QA + 2 * KVA) // LANES, k_col=(QA + 2 * KVA + QB) // LANES,
                 v_col=(QA + 2 * KVA + 2 * QB) // LANES)
    (oa, lse_a), (wg2,) = _attention_fwd(proj, bias_a, sinks, name="attn_a", carry=gather("wg2"), **cfg_a)
    (ob, lse_b), (wu2,) = _attention_fwd(proj, bias_b, None, name="attn_b", carry=gather("wu2"), **cfg_b)
    ycat = _group_norm_cat(oa, ob, g_grp_a, g_grp_b)
    ymix, x2, h3 = _mm_nn([(ycat, wo)], "out_proj", F32, bias=b_out,
                          tail=_tail_post_pre(x1, g_post_mix, gate2, 1.0, g_pre_ffn2, scale3, shift3))

    (a3, b3, u3), (wd2,) = _ffn_up(h3, wg2, wu2, "ffn_up_ffn2", carry=gather("wd2"))

    flights, own = {}, {}

    def grad_pair(key, a_mat, b_mat, name):
        part, own[key] = _mm_tn_pair(a_mat, b_mat, name)
        return part

    def scatter_start(tag, after_vec, **parts):
        names = list(parts)
        sems, p_thru, lands, token = _scatter_start([parts[n] for n in names], "scatter_start_" + tag)
        flights[tag] = (names, sems, p_thru, lands)
        return after_vec + token[0:1, 0:1]

    dx3, dy, loss_part, s1 = _mm_nn([(u3, wd2)], "ffn_down_ffn2", None,
                                    tail=_tail_post_loss(x2, tgt, g_post_ffn2, gate3, 0.5))
    da, db = _ffn_down_bwd(dy, wd2, a3, b3, "ffn_down_bwd_ffn2")
    dwd2 = grad_pair("wd2", u3, dy, "grad_wd_ffn2")
    dwg2 = grad_pair("wg2", da, h3, "grad_wg_ffn2")
    dwu2 = grad_pair("wu2", db, h3, "grad_wu_ffn2")
    g_pre_tied = scatter_start("ffn2", g_pre_ffn2, wd2=dwd2, wg2=dwg2, wu2=dwu2)
    dx2, dymix, s2, s3, s1m, db_out = _mm_nn(
        [(da, wg2), (db, wu2)], "ffn_up_bwd_ffn2", None,
        tail=_tail_pre_post_bwd(x2, dx3, ymix, g_pre_tied, scale3, g_post_mix, gate2, 1.0))
    sm3 = dict(shift=s3, scale=s2 * g_pre_ffn2, gate=0.5 * g_post_ffn2 * s1,
               g_pre=(1.0 + scale3) * s2, g_post=(0.5 * gate3) * s1)

    dycat = _mm_nt(dymix, wo, "out_proj_bwd", F32)
    dwo = grad_pair("wo", ycat, dymix, "grad_wo")
    doa, dob, dg_a, dg_b = _group_norm_bwd(dycat, oa, ob, g_grp_a, g_grp_b)
    dqa, dka, dva, dsink = _attention_bwd(proj, bias_a, sinks, doa, lse_a, name="attn_a_bwd", **cfg_a)
    dqb, dkb, dvb, dbias = _attention_bwd(proj, bias_b, None, dob, lse_b, name="attn_b_bwd", **cfg_b)
    dproj = jnp.concatenate([dqa, dka, dva, dqb, dkb, dvb], axis=1)
    db_in = _col_sum(dproj, "grad_b_in")
    dwin = grad_pair("win", dproj, h2, "grad_win")
    g_pre_tied = scatter_start("mix", g_pre_mix, wo=dwo, win=dwin)
    dx1, dy, s2m, s3m, s1, _ = _mm_nn(
        [(dproj, win)], "in_proj_bwd", None,
        tail=_tail_pre_post_bwd(x1, dx2, y1, g_pre_tied, scale2, g_post_ffn1, gate1, 0.5))
    d_rel = jnp.dot(_diagonal_sums(dbias).reshape(H_B, SKEW), rel_m, precision=lax.Precision.HIGHEST)
    d_sinks = dsink[:, :2 * TPS, 0].reshape(1, H_A)

    da, db = _ffn_down_bwd(dy, wd1, a1, b1, "ffn_down_bwd_ffn1")
    dwd1 = grad_pair("wd1", u1, dy, "grad_wd_ffn1")
    dwg1 = grad_pair("wg1", da, h1, "grad_wg_ffn1")
    dwu1 = grad_pair("wu1", db, h1, "grad_wu_ffn1")
    g_pre_tied = scatter_start("ffn1", g_pre_ffn1, wd1=dwd1, wg1=dwg1, wu1=dwu1)
    dx0, s2, s3 = _mm_nn([(da, wg1), (db, wu1)], "ffn_up_bwd_ffn1", None,
                         tail=_tail_pre_bwd(xs, dx1, g_pre_tied, scale1))
    sm1 = dict(shift=s3, scale=s2 * g_pre_ffn1, gate=0.5 * g_post_ffn1 * s1,
               g_pre=(1.0 + scale1) * s2, g_post=(0.5 * gate1) * s1)

    dmod = jnp.concatenate([sm1["shift"], sm1["scale"], sm1["gate"],
                            s3m, s2m * g_pre_mix, g_post_mix * s1m,
                            sm3["shift"], sm3["scale"], sm3["gate"]], axis=1)
    small_parts = {
        "b_ada": dmod, "g_pre_ffn1": sm1["g_pre"], "g_post_ffn1": sm1["g_post"],
        "g_pre_mix": (1.0 + scale2) * s2m, "b_in": db_in, "sinks_a": d_sinks,
        "rel_bias_b": d_rel.reshape(1, H_B * N_REL), "g_grp_a": dg_a, "g_grp_b": dg_b, "b_out": db_out,
        "g_post_mix": gate2 * s1m, "g_pre_ffn2": sm3["g_pre"], "g_post_ffn2": sm3["g_post"]}
    sizes = [small_parts[n].shape[1] for n in SMALL]

    def pack(parts):
        cells = []
        for p in parts:
            cells.append(p)
            if p.shape[1] % LANES:
                cells.append(jnp.zeros((1, -p.shape[1] % LANES), F32))
        return jnp.concatenate(cells, axis=1)

    packed = pack([small_parts[n] for n in SMALL] + [loss_part])
    n_packed = packed.shape[1]
    gathered = _all_gather_small(packed)
    small_sum = _sum_rows8(gathered)
    loss = small_sum[0, n_packed - LANES]
    dmod_cols = lax.dynamic_slice(gathered.reshape(N_DEV, n_packed), (0, me * ada_cols), (N_DEV, ada_cols))
    g_ada = _ada_weight_grad(sc_all.reshape(N_DEV, d_model).T, dmod_cols)

    slots = {}
    for tag, after in (("ffn2", dx0), ("mix", dx0), ("ffn1", small_sum)):
        names, sems, p_thru, lands = flights[tag]
        _, l_done = _scatter_wait(sems, p_thru, lands, after, "scatter_wait_" + tag)
        slots.update(zip(names, l_done))

    out_g, out_d, out_m, out_v = {}, {}, {}, {}
    d_, m_, v_ = _adamw(w_ada[0], g_ada, m_w_ada[0], v_w_ada[0], "adamw_w_ada")
    out_g["w_ada"], out_d["w_ada"], out_m["w_ada"], out_v["w_ada"] = g_ada[None], d_[None], m_[None], v_[None]
    for n, key, transposed in (("w_gate1", "wg1", True), ("w_up1", "wu1", True), ("w_down1", "wd1", False),
                               ("w_in", "win", True), ("w_out", "wo", False), ("w_gate2", "wg2", True),
                               ("w_up2", "wu2", True), ("w_down2", "wd2", False)):
        view = (lambda t: t.T) if transposed else (lambda t: t)
        res = _adamw_from_slots(view(weights[n][0]), own[key], slots[key], view(mom_m[n][0]), view(mom_v[n][0]),
                                "adamw_" + n)
        out_g[n], out_d[n], out_m[n], out_v[n] = (view(t)[None] for t in res)

    small_out = _adamw_small(small_sum, *(pack([tree[n].reshape(1, -1) for n in SMALL])
                                          for tree in (weights, mom_m, mom_v)), sizes)
    for j, n in enumerate(SMALL):
        shape = weights[n].shape
        out_g[n], out_d[n], out_m[n], out_v[n] = (t.reshape(shape) for t in small_out[4 * j:4 * j + 4])

    return (loss, dx0[None], *[out_g[n] for n in WEIGHTS], *[out_d[n] for n in WEIGHTS],
            *[out_m[n] for n in WEIGHTS], *[out_v[n] for n in WEIGHTS])
```

```python
import numpy as np
import jax
import jax.numpy as jnp
from jax import lax
from jax.experimental import pallas as pl
from jax.experimental.pallas import tpu as pltpu

F32 = jnp.float32
BF16 = jnp.bfloat16
MESH = pl.DeviceIdType.MESH
ANY = pl.BlockSpec(memory_space=pl.ANY)
VMEM_SPEC = pl.BlockSpec(memory_space=pltpu.VMEM)
SMEM_SPEC = pl.BlockSpec(memory_space=pltpu.SMEM)

N_DEV = 8
CHUNK = 64
HEAD_DIM = 64
LANES = 128
H_A, KV_A, H_B = 8, 2, 8
BACK_A, BACK_B = 2, 8
REL_CLIP = 128
N_REL = 2 * REL_CLIP + 1
QA, KVA, QB = H_A * HEAD_DIM, KV_A * HEAD_DIM, H_B * HEAD_DIM
D_IN = QA + 2 * KVA + 3 * QB
N_MOD = 9
EPS = 1e-6
NEG_INF = -1e30
QG = 4
QROWS = QG * CHUNK
TPS = 2
SKEW = 1024
ADAM_LR, ADAM_B1, ADAM_B2, ADAM_EPS, ADAM_WD, ADAM_STEP = 0.001, 0.9, 0.999, 1e-08, 0.01, 10
VMEM_LIMIT = 56 * 2 ** 20


def _pick(n, cands):
    for c in cands:
        if n % c == 0:
            return c
    return n


def _pieces(n, width=2 * LANES):
    return [(lo, min(lo + width, n)) for lo in range(0, n, width)]


def _params(sem=None):
    return pltpu.CompilerParams(dimension_semantics=sem, vmem_limit_bytes=VMEM_LIMIT)


def _dot_nt(a, b):
    return lax.dot_general(a, b, (((1,), (1,)), ((), ())), preferred_element_type=F32)


def _dot_tn(a, b):
    return lax.dot_general(a, b, (((0,), (0,)), ((), ())), preferred_element_type=F32)


def _dot(a, b):
    return jnp.dot(a, b, preferred_element_type=F32)


def _sigmoid(a):
    return 0.5 * (jnp.tanh(0.5 * a) + 1.0)


def _mesh_pos():
    return lax.axis_index("x"), lax.axis_index("y"), lax.axis_index("c")


def _peer(x, y, c, r):
    px = 1 - x if r & 4 else x
    py = 1 - y if r & 2 else y
    pc = 1 - c if r & 1 else c
    return px, py, pc


class _Carry:
    def __init__(self, ins, out_shapes, scratch, start, finish):
        self.ins, self.out_shapes, self.scratch = list(ins), list(out_shapes), list(scratch)
        self.start, self.finish = start, finish


def _call(body, *, name, grid, in_specs, out_specs, out_shape, args, scratch=(), sem=None, carry=None):
    single = not isinstance(out_shape, (tuple, list))
    out_specs = (out_specs,) if single else tuple(out_specs)
    out_shape = (out_shape,) if single else tuple(out_shape)
    if carry is None:
        res = pl.pallas_call(body, name=name, grid=grid, in_specs=list(in_specs), out_specs=out_specs,
                             out_shape=out_shape, scratch_shapes=list(scratch), compiler_params=_params(sem))(*args)
        return res[0] if single else res
    n_in, n_out, n_s = len(in_specs), len(out_shape), len(scratch)
    ci, co = len(carry.ins), len(carry.out_shapes)

    def wrapped(*refs):
        ins, cins = refs[:n_in], refs[n_in:n_in + ci]
        outs = refs[n_in + ci:n_in + ci + n_out]
        couts = refs[n_in + ci + n_out:n_in + ci + n_out + co]
        scr = refs[n_in + ci + n_out + co:n_in + ci + n_out + co + n_s]
        cscr = refs[n_in + ci + n_out + co + n_s:]
        first, last = None, None
        for ax, n in enumerate(grid):
            f, l = pl.program_id(ax) == 0, pl.program_id(ax) == n - 1
            first = f if first is None else first & f
            last = l if last is None else last & l
        pl.when(first)(lambda: carry.start(cins, couts, cscr))
        body(*ins, *outs, *scr)
        pl.when(last)(lambda: carry.finish(cins, couts, cscr))

    res = pl.pallas_call(
        wrapped, name=name, grid=grid, in_specs=list(in_specs) + [ANY] * ci, out_specs=out_specs + (ANY,) * co,
        out_shape=out_shape + tuple(carry.out_shapes), scratch_shapes=list(scratch) + carry.scratch,
        compiler_params=_params(("arbitrary",) * len(grid)))(*args, *carry.ins)
    main = res[:n_out]
    return (main[0] if single else main), res[n_out:]


def _gather_carry(shards):
    n_w = len(shards)
    rows = [s.shape[0] for s in shards]

    def plan(ins, outs, scr):
        send_sems, recv_sems, local_sems = scr
        x, y, c = _mesh_pos()
        me, sibling = (x, y, c), (x, y, 1 - c)
        chips = [(1 - x, y), (x, 1 - y), (1 - x, 1 - y)]

        def block(w, dev):
            start = pl.multiple_of((4 * dev[0] + 2 * dev[1] + dev[2]) * rows[w], 16)
            return outs[w].at[pl.ds(start, rows[w]), :]

        def copy(w, k, dev, to, src=None):
            return pltpu.make_async_remote_copy(
                src_ref=block(w, dev) if src is None else src, dst_ref=block(w, dev),
                send_sem=send_sems.at[w, k], recv_sem=recv_sems.at[w, k], device_id=to, device_id_type=MESH)

        mine = [pltpu.make_async_copy(ins[w], block(w, me), local_sems.at[w]) for w in range(n_w)]
        first = []
        for j, chip in enumerate(chips):
            first += [copy(w, 1 + j, me, (*chip, c), src=ins[w]) for w in range(n_w)]
        first += [copy(w, 0, me, sibling, src=ins[w]) for w in range(n_w)]
        return c, me, sibling, chips, copy, mine, first

    def start(ins, outs, scr):
        _, _, _, _, _, mine, first = plan(ins, outs, scr)
        for cp in mine + first:
            cp.start()

    def finish(ins, outs, scr):
        c, me, sibling, chips, copy, mine, first = plan(ins, outs, scr)
        passed = []
        for j, chip in enumerate(chips):
            for w in range(n_w):
                copy(w, 1 + j, (*chip, c), me).wait_recv()
                cp = copy(w, 4 + j, (*chip, c), sibling)
                cp.start()
                passed.append(cp)
        for w in range(n_w):
            copy(w, 0, sibling, me).wait_recv()
        for j, chip in enumerate(chips):
            for w in range(n_w):
                copy(w, 4 + j, (*chip, 1 - c), me).wait_recv()
        for cp in first + passed:
            cp.wait_send()
        for cp in mine:
            cp.wait()

    return _Carry(
        shards, [jax.ShapeDtypeStruct((N_DEV * s.shape[0], s.shape[1]), s.dtype) for s in shards],
        [pltpu.SemaphoreType.DMA((n_w, N_DEV - 1)), pltpu.SemaphoreType.DMA((n_w, N_DEV - 1)),
         pltpu.SemaphoreType.DMA((n_w,))], start, finish)


def _scatter_carry(parts):
    n_w = len(parts)
    n_chip = N_DEV // 2
    rows = [g.shape[0] // n_chip for g in parts]

    def plan(ins, outs, scr):
        send_sems, recv_sems, local_sems = scr
        x, y, c = _mesh_pos()

        def src(w, chip_index):
            return ins[w].at[pl.ds(pl.multiple_of(chip_index * rows[w], 16), rows[w]), :]

        mine = [pltpu.make_async_copy(src(w, 2 * x + y), outs[w].at[0], local_sems.at[w]) for w in range(n_w)]
        copies = []
        for r in (3, 2, 1):
            px, py, _ = _peer(x, y, c, 2 * r)
            for w in range(n_w):
                copies.append(pltpu.make_async_remote_copy(
                    src_ref=src(w, 2 * px + py), dst_ref=outs[w].at[r], send_sem=send_sems.at[w, r - 1],
                    recv_sem=recv_sems.at[w, r - 1], device_id=(px, py, c), device_id_type=MESH))
        return mine, copies

    def start(ins, outs, scr):
        mine, copies = plan(ins, outs, scr)
        for cp in mine + copies:
            cp.start()

    def finish(ins, outs, scr):
        mine, copies = plan(ins, outs, scr)
        for cp in copies:
            cp.wait_recv()
        for cp in copies:
            cp.wait_send()
        for cp in mine:
            cp.wait()

    return _Carry(
        parts, [jax.ShapeDtypeStruct((n_chip, r, g.shape[1]), g.dtype) for r, g in zip(rows, parts)],
        [pltpu.SemaphoreType.DMA((n_w, n_chip - 1)), pltpu.SemaphoreType.DMA((n_w, n_chip - 1)),
         pltpu.SemaphoreType.DMA((n_w,))], start, finish)


HBM_SPEC = pl.BlockSpec(memory_space=pltpu.HBM)
SEM_SPEC = pl.BlockSpec(memory_space=pltpu.SEMAPHORE)
N_CHIP = N_DEV // 2


def _scatter_copy(part_ref, land_ref, send_sem, recv_sem, r, rows):
    x, y, c = _mesh_pos()
    px, py, _ = _peer(x, y, c, 2 * r)
    src = part_ref.at[pl.ds(pl.multiple_of((2 * px + py) * rows, 16), rows), :]
    return pltpu.make_async_remote_copy(
        src_ref=src, dst_ref=land_ref.at[r - 1], send_sem=send_sem, recv_sem=recv_sem,
        device_id=(px, py, c), device_id_type=MESH)


def _scatter_order(n_w):
    return [(w, r) for r in (3, 2, 1) for w in range(n_w)]


def _scatter_start(parts, name):
    n_w = len(parts)
    rows = [p.shape[0] // N_CHIP for p in parts]
    order = _scatter_order(n_w)
    lands = [pltpu.with_memory_space_constraint(lax.empty((N_CHIP - 1, r, p.shape[1]), p.dtype), pltpu.HBM)
             for r, p in zip(rows, parts)]

    def body(*refs):
        part_refs, land_refs = refs[:n_w], refs[n_w:2 * n_w]
        sems = refs[2 * n_w:2 * n_w + 2 * len(order)]
        token = refs[-1]
        for j, (w, r) in enumerate(order):
            _scatter_copy(part_refs[w], land_refs[w], sems[2 * j], sems[2 * j + 1], r, rows[w]).start()
        token[...] = jnp.zeros_like(token)

    n_sem = 2 * len(order)
    res = pl.pallas_call(
        body, name=name,
        out_shape=(*[pltpu.SemaphoreType.DMA(())] * n_sem, *[pltpu.HBM(p.shape, p.dtype) for p in parts],
                   *[pltpu.HBM(l.shape, l.dtype) for l in lands], jax.ShapeDtypeStruct((8, LANES), F32)),
        in_specs=[HBM_SPEC] * (2 * n_w), out_specs=(*[SEM_SPEC] * n_sem, *[HBM_SPEC] * (2 * n_w), VMEM_SPEC),
        input_output_aliases={i: n_sem + i for i in range(2 * n_w)},
        compiler_params=pltpu.CompilerParams(has_side_effects=pltpu.SideEffectType.DATAFLOW_SIDE_EFFECTING),
    )(*[pltpu.with_memory_space_constraint(p, pltpu.HBM) for p in parts], *lands)
    return (list(res[:n_sem]), list(res[n_sem:n_sem + n_w]), list(res[n_sem + n_w:n_sem + 2 * n_w]), res[-1])


def _scatter_wait(sems, parts, lands, after, name):
    n_w = len(parts)
    rows = [p.shape[0] // N_CHIP for p in parts]
    order = _scatter_order(n_w)

    def body(*refs):
        part_refs, land_refs = refs[:n_w], refs[n_w:2 * n_w]
        sem_refs = refs[2 * n_w:2 * n_w + 2 * len(order)]
        for j, (w, r) in enumerate(order):
            cp = _scatter_copy(part_refs[w], land_refs[w], sem_refs[2 * j], sem_refs[2 * j + 1], r, rows[w])
            cp.wait_send()
            cp.wait_recv()

    res = pl.pallas_call(
        body, name=name,
        out_shape=(*[pltpu.HBM(p.shape, p.dtype) for p in parts], *[pltpu.HBM(l.shape, l.dtype) for l in lands]),
        in_specs=[HBM_SPEC] * (2 * n_w) + [SEM_SPEC] * len(sems) + [ANY],
        out_specs=tuple([HBM_SPEC] * (2 * n_w)),
        input_output_aliases={i: i for i in range(2 * n_w)},
        compiler_params=pltpu.CompilerParams(has_side_effects=pltpu.SideEffectType.DATAFLOW_SIDE_EFFECTING),
    )(*parts, *lands, *sems, after)
    return list(res[:n_w]), list(res[n_w:])


def _small_copy(v_ref, land_ref, send_sem, recv_sem, r):
    x, y, c = _mesh_pos()
    px, py, pc = _peer(x, y, c, r)
    return pltpu.make_async_remote_copy(
        src_ref=v_ref, dst_ref=land_ref.at[4 * x + 2 * y + c], send_sem=send_sem, recv_sem=recv_sem,
        device_id=(px, py, pc), device_id_type=MESH)


def _small_gather_start(v):
    land = pltpu.with_memory_space_constraint(lax.empty((N_DEV,) + v.shape, v.dtype), pltpu.HBM)

    def body(v_ref, land_ref, *rest):
        sems, token = rest[:2 * (N_DEV - 1)], rest[-1]
        for r in range(1, N_DEV):
            _small_copy(v_ref, land_ref, sems[2 * r - 2], sems[2 * r - 1], r).start()
        token[...] = jnp.zeros_like(token)

    n_sem = 2 * (N_DEV - 1)
    res = pl.pallas_call(
        body, name="small_gather_start",
        out_shape=(*[pltpu.SemaphoreType.DMA(())] * n_sem, pltpu.HBM(v.shape, v.dtype),
                   pltpu.HBM(land.shape, land.dtype), jax.ShapeDtypeStruct((8, LANES), F32)),
        in_specs=[HBM_SPEC, HBM_SPEC], out_specs=(*[SEM_SPEC] * n_sem, HBM_SPEC, HBM_SPEC, VMEM_SPEC),
        input_output_aliases={0: n_sem, 1: n_sem + 1},
        compiler_params=pltpu.CompilerParams(has_side_effects=pltpu.SideEffectType.DATAFLOW_SIDE_EFFECTING),
    )(pltpu.with_memory_space_constraint(v, pltpu.HBM), land)
    return list(res[:n_sem]), res[n_sem], res[n_sem + 1], res[-1]


def _small_gather_wait(sems, v, land, after):
    def body(v_ref, land_ref, *rest):
        for r in range(1, N_DEV):
            cp = _small_copy(v_ref, land_ref, rest[2 * r - 2], rest[2 * r - 1], r)
            cp.wait_send()
            x, y, c = _mesh_pos()
            px, py, pc = _peer(x, y, c, r)
            pltpu.make_async_remote_copy(
                src_ref=v_ref, dst_ref=land_ref.at[4 * px + 2 * py + pc], send_sem=rest[2 * r - 2],
                recv_sem=rest[2 * r - 1], device_id=(px, py, pc), device_id_type=MESH).wait_recv()

    res = pl.pallas_call(
        body, name="small_gather_wait",
        out_shape=(pltpu.HBM(v.shape, v.dtype), pltpu.HBM(land.shape, land.dtype)),
        in_specs=[HBM_SPEC, HBM_SPEC] + [SEM_SPEC] * len(sems) + [ANY], out_specs=(HBM_SPEC, HBM_SPEC),
        input_output_aliases={0: 0, 1: 1},
        compiler_params=pltpu.CompilerParams(has_side_effects=pltpu.SideEffectType.DATAFLOW_SIDE_EFFECTING),
    )(v, land, *sems, after)
    return res[0], res[1]


def _ada_forward(c_row, w_ada, b_cols, carry):
    d = c_row.shape[1]
    wcols = w_ada.shape[1]
    ci, co = len(carry.ins), len(carry.out_shapes)

    def body(*refs):
        c_ref, w_ref, b_ref = refs[:3]
        cins = refs[3:3 + ci]
        sc_ref, mod_ref = refs[3 + ci:5 + ci]
        couts = refs[5 + ci:5 + ci + co]
        rows_ref, send_sems, recv_sems = refs[5 + ci + co:8 + ci + co]
        cscr = refs[8 + ci + co:]
        carry.start(cins, couts, cscr)
        x, y, c = _mesh_pos()
        me = 4 * x + 2 * y + c
        cv = c_ref[...]
        sc_ref[me] = cv * _sigmoid(cv)

        sends = []
        for r in range(1, N_DEV):
            px, py, pc = _peer(x, y, c, r)
            cp = pltpu.make_async_remote_copy(
                src_ref=sc_ref.at[me], dst_ref=sc_ref.at[me], send_sem=send_sems.at[0, r - 1],
                recv_sem=recv_sems.at[0, r - 1], device_id=(px, py, pc), device_id_type=MESH)
            cp.start()
            sends.append(cp)
        for r in range(1, N_DEV):
            px, py, pc = _peer(x, y, c, r)
            pid = 4 * px + 2 * py + pc
            pltpu.make_async_remote_copy(
                src_ref=sc_ref.at[pid], dst_ref=sc_ref.at[pid], send_sem=send_sems.at[0, r - 1],
                recv_sem=recv_sems.at[0, r - 1], device_id=(px, py, pc), device_id_type=MESH).wait_recv()
        for cp in sends:
            cp.wait_send()

        sc_all = jnp.concatenate([sc_ref[j] for j in range(N_DEV)], axis=0)
        rows = _dot(sc_all.astype(BF16), w_ref[...].astype(BF16)) + b_ref[...]
        for j in range(N_DEV):
            rows_ref[j] = rows[j:j + 1, :]
        mod_ref[me] = rows_ref[me]

        sends = []
        for r in range(1, N_DEV):
            px, py, pc = _peer(x, y, c, r)
            pid = 4 * px + 2 * py + pc
            cp = pltpu.make_async_remote_copy(
                src_ref=rows_ref.at[pid], dst_ref=mod_ref.at[me], send_sem=send_sems.at[1, r - 1],
                recv_sem=recv_sems.at[1, r - 1], device_id=(px, py, pc), device_id_type=MESH)
            cp.start()
            sends.append(cp)
        for r in range(1, N_DEV):
            px, py, pc = _peer(x, y, c, r)
            pid = 4 * px + 2 * py + pc
            pltpu.make_async_remote_copy(
                src_ref=rows_ref.at[pid], dst_ref=mod_ref.at[pid], send_sem=send_sems.at[1, r - 1],
                recv_sem=recv_sems.at[1, r - 1], device_id=(px, py, pc), device_id_type=MESH).wait_recv()
        for cp in sends:
            cp.wait_send()
        carry.finish(cins, couts, cscr)

    res = pl.pallas_call(
        body, name="ada_forward",
        out_shape=(jax.ShapeDtypeStruct((N_DEV, 1, d), F32), jax.ShapeDtypeStruct((N_DEV, 1, wcols), F32),
                   *carry.out_shapes),
        in_specs=[VMEM_SPEC, VMEM_SPEC, VMEM_SPEC] + [ANY] * ci, out_specs=(VMEM_SPEC, VMEM_SPEC) + (ANY,) * co,
        scratch_shapes=[pltpu.VMEM((N_DEV, 1, wcols), F32), pltpu.SemaphoreType.DMA((2, N_DEV - 1)),
                        pltpu.SemaphoreType.DMA((2, N_DEV - 1))] + carry.scratch,
        compiler_params=_params(),
    )(c_row, w_ada, b_cols, *carry.ins)
    return res[:2], res[2:]


def _all_gather_small(v):
    n = v.shape[1]

    def body(v_ref, out_ref, send_sems, recv_sems):
        x, y, c = _mesh_pos()
        me = 4 * x + 2 * y + c
        out_ref[me] = v_ref[...]
        sends = []
        for r in range(1, N_DEV):
            px, py, pc = _peer(x, y, c, r)
            cp = pltpu.make_async_remote_copy(
                src_ref=v_ref, dst_ref=out_ref.at[me], send_sem=send_sems.at[r - 1],
                recv_sem=recv_sems.at[r - 1], device_id=(px, py, pc), device_id_type=MESH)
            cp.start()
            sends.append(cp)
        for r in range(1, N_DEV):
            px, py, pc = _peer(x, y, c, r)
            pid = 4 * px + 2 * py + pc
            pltpu.make_async_remote_copy(
                src_ref=v_ref, dst_ref=out_ref.at[pid], send_sem=send_sems.at[r - 1],
                recv_sem=recv_sems.at[r - 1], device_id=(px, py, pc), device_id_type=MESH).wait_recv()
        for cp in sends:
            cp.wait_send()

    return pl.pallas_call(
        body, name="all_gather_small",
        out_shape=jax.ShapeDtypeStruct((N_DEV, 1, n), F32),
        in_specs=[VMEM_SPEC], out_specs=VMEM_SPEC,
        scratch_shapes=[pltpu.SemaphoreType.DMA((N_DEV - 1,)), pltpu.SemaphoreType.DMA((N_DEV - 1,))],
        compiler_params=_params(),
    )(v)


def _mm_nt(a, b, name, out_dtype, bias=None, carry=None):
    m, k = a.shape
    n = b.shape[0]
    tm = _pick(m, (512, 256, 128))
    tn = _pick(n, (1408, 1152, 1024, 768, 512, 256, 128))

    def body(*refs):
        acc = _dot_nt(refs[0][...], refs[1][...])
        if bias is not None:
            acc = acc + refs[2][...]
        refs[-1][...] = acc.astype(out_dtype)

    in_specs = [pl.BlockSpec((tm, k), lambda j, i: (i, 0)), pl.BlockSpec((tn, k), lambda j, i: (j, 0))]
    args = [a, b]
    if bias is not None:
        in_specs.append(pl.BlockSpec((1, tn), lambda j, i: (0, j)))
        args.append(bias)
    return _call(body, name=name, grid=(n // tn, m // tm), in_specs=in_specs,
                 out_specs=pl.BlockSpec((tm, tn), lambda j, i: (i, j)),
                 out_shape=jax.ShapeDtypeStruct((m, n), out_dtype), args=args,
                 sem=("parallel", "parallel"), carry=carry)


class _Tail:
    def __init__(self, rows, vecs, outs, fn):
        self.rows, self.vecs, self.outs, self.fn = list(rows), list(vecs), list(outs), fn


def _mm_nn(pairs, name, out_dtype, bias=None, carry=None, tail=None):
    m, k = pairs[0][0].shape
    n = pairs[0][1].shape[1]
    n_p = len(pairs)
    tm = _pick(m, (512, 256, 128))
    tk = k if n_p == 1 else _pick(k, (1408, 1152, 1024, 768, 512, 256, 128))
    nk = k // tk
    n_b = 0 if bias is None else 1
    n_r, n_v = (len(tail.rows), len(tail.vecs)) if tail else (0, 0)
    n_in = 2 * n_p + n_b + n_r + n_v
    n_main = 0 if out_dtype is None else 1

    def finish(acc, refs, first_tile):
        if bias is not None:
            acc = acc + refs[2 * n_p][...]
        outs = refs[n_in:-1]
        if n_main:
            outs[0][...] = acc.astype(out_dtype)
        if tail is None:
            return
        rows = [r[...] for r in refs[2 * n_p + n_b:2 * n_p + n_b + n_r]]
        vecs = [v[...] for v in refs[2 * n_p + n_b + n_r:n_in]]
        vals = tail.fn(acc, rows, vecs)
        for ref, val, (dtype, kind) in zip(outs[n_main:], vals, tail.outs):
            if kind == "row":
                ref[...] = val.astype(dtype)
            else:
                @pl.when(first_tile)
                def _(ref=ref):
                    ref[...] = jnp.zeros_like(ref)

                ref[...] += val

    def body(*refs):
        acc_ref = refs[-1]
        kk, i = pl.program_id(0), pl.program_id(1)
        part = _dot(refs[0][...], refs[1][...])
        for p in range(1, n_p):
            part = part + _dot(refs[2 * p][...], refs[2 * p + 1][...])
        if nk == 1:
            finish(part, refs, i == 0)
            return
        rows = pl.ds(pl.multiple_of(i * tm, tm), tm)

        @pl.when(kk == 0)
        def _():
            acc_ref[rows, :] = part

        if nk > 2:
            @pl.when((kk > 0) & (kk < nk - 1))
            def _():
                acc_ref[rows, :] += part

        @pl.when(kk == nk - 1)
        def _():
            finish(acc_ref[rows, :] + part, refs, i == 0)

    def last_only(kk, i):
        return (jnp.where(kk == nk - 1, i, 0), 0)

    row_spec = pl.BlockSpec((tm, n), last_only)
    vec_spec = pl.BlockSpec((1, n), lambda kk, i: (0, 0))
    in_specs, args = [], []
    for a, b in pairs:
        in_specs += [pl.BlockSpec((tm, tk), lambda kk, i: (i, kk)), pl.BlockSpec((tk, n), lambda kk, i: (kk, 0))]
        args += [a, b]
    if bias is not None:
        in_specs.append(vec_spec)
        args.append(bias)
    out_specs = [row_spec] * n_main
    out_shape = [jax.ShapeDtypeStruct((m, n), out_dtype)] if n_main else []
    if tail:
        in_specs += [row_spec] * n_r + [vec_spec] * n_v
        args += tail.rows + tail.vecs
        for dtype, kind in tail.outs:
            if kind == "row":
                out_specs.append(row_spec)
                out_shape.append(jax.ShapeDtypeStruct((m, n), dtype))
            else:
                width = n if kind == "sum" else 1
                out_specs.append(pl.BlockSpec((1, width), lambda kk, i: (0, 0)))
                out_shape.append(jax.ShapeDtypeStruct((1, width), dtype))
    if tail is None:
        out_specs, out_shape = out_specs[0], out_shape[0]
    return _call(body, name=name, grid=(nk, m // tm), in_specs=in_specs, out_specs=out_specs,
                 out_shape=out_shape, args=args,
                 scratch=[pltpu.VMEM((m, n) if nk > 1 else (8, LANES), F32)],
                 sem=("arbitrary", "arbitrary"), carry=carry)


def _rms(v):
    return lax.rsqrt(jnp.mean(v * v, axis=-1, keepdims=True) + EPS)


def _col(v):
    return jnp.sum(v, axis=0, keepdims=True)


def _tail_post_pre(x, g_post, gate, weight, g_pre, scale, shift):
    def fn(y, rows, vecs):
        (xv,), (gp, gt, g, sc, sh) = rows, vecs
        xo = xv + (weight * gt) * ((y * _rms(y)) * gp)
        return xo, ((xo * _rms(xo)) * g) * (1.0 + sc) + sh

    return _Tail([x], [g_post, gate, g_pre, scale, shift], [(F32, "row"), (BF16, "row")], fn)


def _tail_post_loss(x, target, g, gate, weight):
    def fn(y, rows, vecs):
        (xv, tv), (gv, gt) = rows, vecs
        r = _rms(y)
        yn = y * r
        err = (xv + (weight * gt) * (yn * gv)) - tv
        do = err * (1.0 / y.shape[1])
        dyn = do * ((weight * gt) * gv)
        dy = r * (dyn - yn * jnp.mean(dyn * yn, axis=-1, keepdims=True))
        return do, dy, 0.5 * _col(jnp.mean(err * err, axis=-1, keepdims=True)), _col(do * yn)

    return _Tail([x, target], [g, gate], [(F32, "row"), (BF16, "row"), (F32, "one"), (F32, "sum")], fn)


def _tail_pre_bwd(x, dres, g_pre, scale):
    def fn(dh, rows, vecs):
        (xv, dr), (g, sc) = rows, vecs
        r = _rms(xv)
        n = xv * r
        dn = dh * (g * (1.0 + sc))
        return dr + r * (dn - n * jnp.mean(dn * n, axis=-1, keepdims=True)), _col(dh * n), _col(dh)

    return _Tail([x, dres], [g_pre, scale], [(F32, "row"), (F32, "sum"), (F32, "sum")], fn)


def _tail_pre_post_bwd(x, dres, y, g_pre, scale, g_post, gate, weight):
    def fn(dh, rows, vecs):
        (xv, dr, yv), (g, sc, gp, gt) = rows, vecs
        r = _rms(xv)
        n = xv * r
        dn = dh * (g * (1.0 + sc))
        dx = dr + r * (dn - n * jnp.mean(dn * n, axis=-1, keepdims=True))
        ry = _rms(yv)
        yn = yv * ry
        dyn = dx * ((weight * gt) * gp)
        dy = ry * (dyn - yn * jnp.mean(dyn * yn, axis=-1, keepdims=True))
        return dx, dy, _col(dh * n), _col(dh), _col(dx * yn), _col(dy)

    return _Tail([x, dres, y], [g_pre, scale, g_post, gate],
                 [(F32, "row"), (BF16, "row")] + [(F32, "sum")] * 4, fn)


def _mm_tn(a, b, name, out_dtype=BF16, carry=None):
    k, m = a.shape
    n = b.shape[1]
    tm = _pick(m, (1408, 1152, 1024, 768, 512, 256, 128))
    tk = _pick(k, (512, 256, 128))
    nk = k // tk

    def body(a_ref, b_ref, o_ref, acc_ref):
        kk = pl.program_id(1)

        @pl.when(kk == 0)
        def _():
            acc_ref[...] = jnp.zeros_like(acc_ref)

        acc_ref[...] += _dot_tn(a_ref[...], b_ref[...])

        @pl.when(kk == nk - 1)
        def _():
            o_ref[...] = acc_ref[...].astype(out_dtype)

    return _call(body, name=name, grid=(m // tm, nk),
                 in_specs=[pl.BlockSpec((tk, tm), lambda i, kk: (kk, i)), pl.BlockSpec((tk, n), lambda i, kk: (kk, 0))],
                 out_specs=pl.BlockSpec((tm, n), lambda i, kk: (i, 0)),
                 out_shape=jax.ShapeDtypeStruct((m, n), out_dtype), args=[a, b],
                 scratch=[pltpu.VMEM((tm, n), F32)], sem=("parallel", "arbitrary"), carry=carry)


def _mm_tn_pair(a, b, name, carry=None):
    k, m = a.shape
    n = b.shape[1]
    rows = m // N_DEV
    n_chip = N_DEV // 2
    tm = 4 * rows
    tk = _pick(k, (1024, 512, 256, 128))
    nk = k // tk

    def body(a_ref, b_ref, p_ref, own_ref, acc_ref, keep_ref, send_ref, land_ref, send_sems, recv_sems):
        i, kk = pl.program_id(0), pl.program_id(1)
        x, y, c = _mesh_pos()

        def push(chip):
            return pltpu.make_async_remote_copy(
                src_ref=send_ref.at[chip], dst_ref=land_ref.at[chip], send_sem=send_sems.at[chip],
                recv_sem=recv_sems.at[chip], device_id=(x, y, 1 - c), device_id_type=MESH)

        if nk == 1:
            acc = _dot_tn(a_ref[...], b_ref[...])
        else:
            @pl.when(kk == 0)
            def _():
                acc_ref[...] = jnp.zeros_like(acc_ref)

            acc_ref[...] += _dot_tn(a_ref[...], b_ref[...])
            acc = acc_ref

        for t in range(2):
            @pl.when((kk == nk - 1) & (i == t))
            def _(t=t):
                for ob in range(4):
                    chip, core = 2 * t + ob // 2, ob % 2
                    blk = acc[ob * rows:(ob + 1) * rows, :]

                    @pl.when(c == core)
                    def _(chip=chip, blk=blk):
                        keep_ref[chip] = blk

                    @pl.when(c != core)
                    def _(chip=chip, blk=blk):
                        send_ref[chip] = blk.astype(BF16)
                        push(chip).start()

        @pl.when((kk == nk - 1) & (i == 1))
        def _():
            for chip in range(n_chip):
                push(chip).wait_recv()
                val = (keep_ref[chip] + land_ref[chip].astype(F32)).astype(BF16)
                p_ref[chip * rows:(chip + 1) * rows, :] = val

                @pl.when(2 * x + y == chip)
                def _(val=val):
                    own_ref[...] = val

            for chip in range(n_chip):
                push(chip).wait_send()

    return _call(body, name=name, grid=(2, nk),
                 in_specs=[pl.BlockSpec((tk, tm), lambda i, kk: (kk, i)), pl.BlockSpec((tk, n), lambda i, kk: (kk, 0))],
                 out_specs=(pl.BlockSpec((n_chip * rows, n), lambda i, kk: (0, 0)),
                            pl.BlockSpec((rows, n), lambda i, kk: (0, 0))),
                 out_shape=(jax.ShapeDtypeStruct((n_chip * rows, n), BF16), jax.ShapeDtypeStruct((rows, n), BF16)),
                 args=[a, b],
                 scratch=[pltpu.VMEM((tm, n) if nk > 1 else (8, LANES), F32), pltpu.VMEM((n_chip, rows, n), F32),
                          pltpu.VMEM((n_chip, rows, n), BF16), pltpu.VMEM((n_chip, rows, n), BF16),
                          pltpu.SemaphoreType.DMA((n_chip,)), pltpu.SemaphoreType.DMA((n_chip,))],
                 sem=("arbitrary", "arbitrary"), carry=carry)


def _ffn_up(h, wg_t, wu_t, name, carry=None):
    s, d = h.shape
    f = wg_t.shape[0]
    tm = _pick(s, (512, 256, 128))
    tf = _pick(f, (1408, 1024, 512, 256, 128))

    def body(h_ref, wg_ref, wu_ref, a_ref, b_ref, u_ref):
        hh = h_ref[...]
        for lo, hi in _pieces(tf):
            a = _dot_nt(hh, wg_ref[lo:hi, :])
            b = _dot_nt(hh, wu_ref[lo:hi, :])
            a_ref[:, lo:hi] = a.astype(BF16)
            b_ref[:, lo:hi] = b.astype(BF16)
            u_ref[:, lo:hi] = ((a * _sigmoid(a)) * b).astype(BF16)

    w_spec = pl.BlockSpec((tf, d), lambda j, i: (j, 0))
    o_spec = pl.BlockSpec((tm, tf), lambda j, i: (i, j))
    o_shape = jax.ShapeDtypeStruct((s, f), BF16)
    return _call(body, name=name, grid=(f // tf, s // tm),
                 in_specs=[pl.BlockSpec((tm, d), lambda j, i: (i, 0)), w_spec, w_spec],
                 out_specs=(o_spec, o_spec, o_spec), out_shape=(o_shape, o_shape, o_shape),
                 args=[h, wg_t, wu_t], sem=("parallel", "parallel"), carry=carry)


def _ffn_down_bwd(dy, wd, a, b, name, carry=None):
    s, d = dy.shape
    f = wd.shape[0]
    tm = _pick(s, (512, 256, 128))
    tf = _pick(f, (1408, 1024, 512, 256, 128))

    def body(dy_ref, wd_ref, a_ref, b_ref, da_ref, db_ref):
        dyv = dy_ref[...]
        for lo, hi in _pieces(tf):
            du = _dot_nt(dyv, wd_ref[lo:hi, :])
            a = a_ref[:, lo:hi].astype(F32)
            b = b_ref[:, lo:hi].astype(F32)
            sig = _sigmoid(a)
            da_ref[:, lo:hi] = (du * b * (sig * (1.0 + a * (1.0 - sig)))).astype(BF16)
            db_ref[:, lo:hi] = (du * (a * sig)).astype(BF16)

    t_spec = pl.BlockSpec((tm, tf), lambda j, i: (i, j))
    o_shape = jax.ShapeDtypeStruct((s, f), BF16)
    return _call(body, name=name, grid=(f // tf, s // tm),
                 in_specs=[pl.BlockSpec((tm, d), lambda j, i: (i, 0)), pl.BlockSpec((tf, d), lambda j, i: (j, 0)),
                           t_spec, t_spec],
                 out_specs=(t_spec, t_spec), out_shape=(o_shape, o_shape), args=[dy, wd, a, b],
                 sem=("parallel", "parallel"), carry=carry)


def _row_tile(s):
    return _pick(s, (256, 128, 64))


def _vec_spec(d):
    return pl.BlockSpec((1, d), lambda i: (0, 0))


def _pre_norm(x, g, scale, shift, name):
    s, d = x.shape
    ts = _row_tile(s)

    def body(x_ref, g_ref, sc_ref, sh_ref, h_ref):
        xv = x_ref[...]
        r = lax.rsqrt(jnp.mean(xv * xv, axis=-1, keepdims=True) + EPS)
        h_ref[...] = (((xv * r) * g_ref[...]) * (1.0 + sc_ref[...]) + sh_ref[...]).astype(BF16)

    row = pl.BlockSpec((ts, d), lambda i: (i, 0))
    return _call(body, name=name, grid=(s // ts,), in_specs=[row, _vec_spec(d), _vec_spec(d), _vec_spec(d)],
                 out_specs=row, out_shape=jax.ShapeDtypeStruct((s, d), BF16), args=[x, g, scale, shift],
                 sem=("parallel",))


def _post_norm_residual(x, y, g, gate, weight, name):
    s, d = x.shape
    ts = _row_tile(s)

    def body(x_ref, y_ref, g_ref, gate_ref, o_ref):
        yv = y_ref[...]
        r = lax.rsqrt(jnp.mean(yv * yv, axis=-1, keepdims=True) + EPS)
        o_ref[...] = x_ref[...] + (weight * gate_ref[...]) * ((yv * r) * g_ref[...])

    row = pl.BlockSpec((ts, d), lambda i: (i, 0))
    return _call(body, name=name, grid=(s // ts,), in_specs=[row, row, _vec_spec(d), _vec_spec(d)],
                 out_specs=row, out_shape=jax.ShapeDtypeStruct((s, d), F32), args=[x, y, g, gate],
                 sem=("parallel",))


def _post_norm_bwd(dout, y, g, gate, weight, name):
    s, d = y.shape
    ts = _row_tile(s)

    def body(do_ref, y_ref, g_ref, gate_ref, dy_ref, s1_ref, cs_ref):
        @pl.when(pl.program_id(0) == 0)
        def _():
            s1_ref[...] = jnp.zeros_like(s1_ref)
            cs_ref[...] = jnp.zeros_like(cs_ref)

        yv = y_ref[...]
        do = do_ref[...]
        r = lax.rsqrt(jnp.mean(yv * yv, axis=-1, keepdims=True) + EPS)
        yn = yv * r
        dyn = do * ((weight * gate_ref[...]) * g_ref[...])
        dy = r * (dyn - yn * jnp.mean(dyn * yn, axis=-1, keepdims=True))
        dy_ref[...] = dy.astype(BF16)
        s1_ref[...] += jnp.sum(do * yn, axis=0, keepdims=True)
        cs_ref[...] += jnp.sum(dy, axis=0, keepdims=True)

    row = pl.BlockSpec((ts, d), lambda i: (i, 0))
    vec = jax.ShapeDtypeStruct((1, d), F32)
    return _call(body, name=name, grid=(s // ts,), in_specs=[row, row, _vec_spec(d), _vec_spec(d)],
                 out_specs=(row, _vec_spec(d), _vec_spec(d)),
                 out_shape=(jax.ShapeDtypeStruct((s, d), BF16), vec, vec), args=[dout, y, g, gate],
                 sem=("arbitrary",))


def _pre_norm_bwd(dh, x, g, scale, dres, name):
    s, d = x.shape
    ts = _row_tile(s)

    def body(dh_ref, x_ref, g_ref, sc_ref, dr_ref, dx_ref, s2_ref, s3_ref):
        @pl.when(pl.program_id(0) == 0)
        def _():
            s2_ref[...] = jnp.zeros_like(s2_ref)
            s3_ref[...] = jnp.zeros_like(s3_ref)

        xv = x_ref[...]
        dh = dh_ref[...]
        r = lax.rsqrt(jnp.mean(xv * xv, axis=-1, keepdims=True) + EPS)
        n = xv * r
        dn = dh * (g_ref[...] * (1.0 + sc_ref[...]))
        dx_ref[...] = dr_ref[...] + r * (dn - n * jnp.mean(dn * n, axis=-1, keepdims=True))
        s2_ref[...] += jnp.sum(dh * n, axis=0, keepdims=True)
        s3_ref[...] += jnp.sum(dh, axis=0, keepdims=True)

    row = pl.BlockSpec((ts, d), lambda i: (i, 0))
    vec = jax.ShapeDtypeStruct((1, d), F32)
    return _call(body, name=name, grid=(s // ts,), in_specs=[row, row, _vec_spec(d), _vec_spec(d), row],
                 out_specs=(row, _vec_spec(d), _vec_spec(d)),
                 out_shape=(jax.ShapeDtypeStruct((s, d), F32), vec, vec), args=[dh, x, g, scale, dres],
                 sem=("arbitrary",))


def _post_pre_norm(x, y, g_post, gate, weight, g_pre, scale, shift, name):
    s, d = x.shape
    ts = _row_tile(s)

    def body(x_ref, y_ref, gp_ref, gate_ref, g_ref, sc_ref, sh_ref, o_ref, h_ref):
        yv = y_ref[...]
        r = lax.rsqrt(jnp.mean(yv * yv, axis=-1, keepdims=True) + EPS)
        xv = x_ref[...] + (weight * gate_ref[...]) * ((yv * r) * gp_ref[...])
        o_ref[...] = xv
        r2 = lax.rsqrt(jnp.mean(xv * xv, axis=-1, keepdims=True) + EPS)
        h_ref[...] = (((xv * r2) * g_ref[...]) * (1.0 + sc_ref[...]) + sh_ref[...]).astype(BF16)

    row = pl.BlockSpec((ts, d), lambda i: (i, 0))
    return _call(body, name=name, grid=(s // ts,), in_specs=[row, row] + [_vec_spec(d)] * 5,
                 out_specs=(row, row),
                 out_shape=(jax.ShapeDtypeStruct((s, d), F32), jax.ShapeDtypeStruct((s, d), BF16)),
                 args=[x, y, g_post, gate, g_pre, scale, shift], sem=("parallel",))


def _post_norm_loss_bwd(x, y, g, gate, weight, target, name):
    s, d = y.shape
    ts = _row_tile(s)

    def body(x_ref, y_ref, g_ref, gate_ref, t_ref, dx_ref, dy_ref, l_ref, s1_ref):
        @pl.when(pl.program_id(0) == 0)
        def _():
            l_ref[...] = jnp.zeros_like(l_ref)
            s1_ref[...] = jnp.zeros_like(s1_ref)

        yv = y_ref[...]
        r = lax.rsqrt(jnp.mean(yv * yv, axis=-1, keepdims=True) + EPS)
        yn = yv * r
        err = (x_ref[...] + (weight * gate_ref[...]) * (yn * g_ref[...])) - t_ref[...]
        do = err * (1.0 / d)
        dx_ref[...] = do
        l_ref[...] += 0.5 * jnp.sum(jnp.mean(err * err, axis=-1, keepdims=True), axis=0, keepdims=True)
        dyn = do * ((weight * gate_ref[...]) * g_ref[...])
        dy_ref[...] = (r * (dyn - yn * jnp.mean(dyn * yn, axis=-1, keepdims=True))).astype(BF16)
        s1_ref[...] += jnp.sum(do * yn, axis=0, keepdims=True)

    row = pl.BlockSpec((ts, d), lambda i: (i, 0))
    return _call(body, name=name, grid=(s // ts,), in_specs=[row, row, _vec_spec(d), _vec_spec(d), row],
                 out_specs=(row, row, pl.BlockSpec((1, 1), lambda i: (0, 0)), _vec_spec(d)),
                 out_shape=(jax.ShapeDtypeStruct((s, d), F32), jax.ShapeDtypeStruct((s, d), BF16),
                            jax.ShapeDtypeStruct((1, 1), F32), jax.ShapeDtypeStruct((1, d), F32)),
                 args=[x, y, g, gate, target], sem=("arbitrary",))


def _pre_post_norm_bwd(dh, x, g_pre, scale, dres, y, g_post, gate, weight, name):
    s, d = x.shape
    ts = _row_tile(s)

    def body(dh_ref, x_ref, g_ref, sc_ref, dr_ref, y_ref, gp_ref, gate_ref,
             dx_ref, dy_ref, s2_ref, s3_ref, s1_ref, cs_ref):
        @pl.when(pl.program_id(0) == 0)
        def _():
            for ref in (s2_ref, s3_ref, s1_ref, cs_ref):
                ref[...] = jnp.zeros_like(ref)

        xv = x_ref[...]
        dh = dh_ref[...]
        r = lax.rsqrt(jnp.mean(xv * xv, axis=-1, keepdims=True) + EPS)
        n = xv * r
        dn = dh * (g_ref[...] * (1.0 + sc_ref[...]))
        dx = dr_ref[...] + r * (dn - n * jnp.mean(dn * n, axis=-1, keepdims=True))
        dx_ref[...] = dx
        s2_ref[...] += jnp.sum(dh * n, axis=0, keepdims=True)
        s3_ref[...] += jnp.sum(dh, axis=0, keepdims=True)
        yv = y_ref[...]
        ry = lax.rsqrt(jnp.mean(yv * yv, axis=-1, keepdims=True) + EPS)
        yn = yv * ry
        dyn = dx * ((weight * gate_ref[...]) * gp_ref[...])
        dy = ry * (dyn - yn * jnp.mean(dyn * yn, axis=-1, keepdims=True))
        dy_ref[...] = dy.astype(BF16)
        s1_ref[...] += jnp.sum(dx * yn, axis=0, keepdims=True)
        cs_ref[...] += jnp.sum(dy, axis=0, keepdims=True)

    row = pl.BlockSpec((ts, d), lambda i: (i, 0))
    vec = jax.ShapeDtypeStruct((1, d), F32)
    return _call(body, name=name, grid=(s // ts,),
                 in_specs=[row, row, _vec_spec(d), _vec_spec(d), row, row, _vec_spec(d), _vec_spec(d)],
                 out_specs=(row, row) + (_vec_spec(d),) * 4,
                 out_shape=(jax.ShapeDtypeStruct((s, d), F32), jax.ShapeDtypeStruct((s, d), BF16), vec, vec, vec, vec),
                 args=[dh, x, g_pre, scale, dres, y, g_post, gate], sem=("arbitrary",))


def _group_norm_cat(oa, ob, ga, gb):
    s = oa.shape[0]
    ts = _row_tile(s)

    def body(oa_ref, ob_ref, ga_ref, gb_ref, y_ref):
        for o_ref, g_ref, lo, w in ((oa_ref, ga_ref, 0, QA), (ob_ref, gb_ref, QA, QB)):
            ov = o_ref[...]
            r = lax.rsqrt(jnp.mean(ov * ov, axis=-1, keepdims=True) + EPS)
            y_ref[:, lo:lo + w] = ((ov * r) * g_ref[...]).astype(BF16)

    return _call(body, name="group_norm_cat", grid=(s // ts,),
                 in_specs=[pl.BlockSpec((ts, QA), lambda i: (i, 0)), pl.BlockSpec((ts, QB), lambda i: (i, 0)),
                           _vec_spec(QA), _vec_spec(QB)],
                 out_specs=pl.BlockSpec((ts, QA + QB), lambda i: (i, 0)),
                 out_shape=jax.ShapeDtypeStruct((s, QA + QB), BF16), args=[oa, ob, ga, gb], sem=("parallel",))


def _group_norm_bwd(dy, oa, ob, ga, gb):
    s = oa.shape[0]
    ts = _row_tile(s)

    def body(dy_ref, oa_ref, ob_ref, ga_ref, gb_ref, doa_ref, dob_ref, dga_ref, dgb_ref):
        @pl.when(pl.program_id(0) == 0)
        def _():
            dga_ref[...] = jnp.zeros_like(dga_ref)
            dgb_ref[...] = jnp.zeros_like(dgb_ref)

        for o_ref, g_ref, do_ref, dg_ref, lo, w in ((oa_ref, ga_ref, doa_ref, dga_ref, 0, QA),
                                                    (ob_ref, gb_ref, dob_ref, dgb_ref, QA, QB)):
            ov = o_ref[...]
            dyv = dy_ref[:, lo:lo + w]
            r = lax.rsqrt(jnp.mean(ov * ov, axis=-1, keepdims=True) + EPS)
            n = ov * r
            dn = dyv * g_ref[...]
            do_ref[...] = r * (dn - n * jnp.mean(dn * n, axis=-1, keepdims=True))
            dg_ref[...] += jnp.sum(dyv * n, axis=0, keepdims=True)

    ra = pl.BlockSpec((ts, QA), lambda i: (i, 0))
    rb = pl.BlockSpec((ts, QB), lambda i: (i, 0))
    return _call(body, name="group_norm_bwd", grid=(s // ts,),
                 in_specs=[pl.BlockSpec((ts, QA + QB), lambda i: (i, 0)), ra, rb, _vec_spec(QA), _vec_spec(QB)],
                 out_specs=(ra, rb, _vec_spec(QA), _vec_spec(QB)),
                 out_shape=(jax.ShapeDtypeStruct((s, QA), F32), jax.ShapeDtypeStruct((s, QB), F32),
                            jax.ShapeDtypeStruct((1, QA), F32), jax.ShapeDtypeStruct((1, QB), F32)),
                 args=[dy, oa, ob, ga, gb], sem=("arbitrary",))


def _loss_and_grad(y, target):
    s, d = y.shape
    ts = _row_tile(s)

    def body(y_ref, t_ref, l_ref, g_ref):
        @pl.when(pl.program_id(0) == 0)
        def _():
            l_ref[...] = jnp.zeros_like(l_ref)

        err = y_ref[...] - t_ref[...]
        g_ref[...] = err * (1.0 / d)
        row = jnp.mean(err * err, axis=-1, keepdims=True)
        l_ref[...] += 0.5 * jnp.sum(row, axis=0, keepdims=True)

    row = pl.BlockSpec((ts, d), lambda i: (i, 0))
    return _call(body, name="loss_and_grad", grid=(s // ts,), in_specs=[row, row],
                 out_specs=(pl.BlockSpec((1, 1), lambda i: (0, 0)), row),
                 out_shape=(jax.ShapeDtypeStruct((1, 1), F32), jax.ShapeDtypeStruct((s, d), F32)),
                 args=[y, target], sem=("arbitrary",))


def _col_sum(x, name):
    s, n = x.shape
    ts = _row_tile(s)

    def body(x_ref, o_ref):
        @pl.when(pl.program_id(0) == 0)
        def _():
            o_ref[...] = jnp.zeros_like(o_ref)

        o_ref[...] += jnp.sum(x_ref[...].astype(F32), axis=0, keepdims=True)

    return _call(body, name=name, grid=(s // ts,), in_specs=[pl.BlockSpec((ts, n), lambda i: (i, 0))],
                 out_specs=pl.BlockSpec((1, n), lambda i: (0, 0)), out_shape=jax.ShapeDtypeStruct((1, n), F32),
                 args=[x], sem=("arbitrary",))


def _n_variants(n_back):
    return -(-n_back // QG) + 1


def _alibi_bias():
    i = np.arange(QROWS)[:, None]
    j = np.arange((QG + BACK_A) * CHUNK)[None, :]
    dist = np.abs(BACK_A * CHUNK + i - j).astype(np.float32)
    dc = j // CHUNK - i // CHUNK
    valid = (dc >= 0) & (dc <= BACK_A)
    slopes = np.array([2.0 ** (-8.0 * (h + 1) / H_A) for h in range(H_A)], dtype=np.float32)
    bias = -slopes[:, None, None] * dist[None]
    out = [np.where((valid & (j >= (BACK_A - QG * v) * CHUNK))[None], bias, np.float32(NEG_INF))
           for v in range(_n_variants(BACK_A))]
    return jnp.asarray(np.stack(out).astype(np.float32))


def _rel_index_matrix():
    cc = np.arange(SKEW)
    dist = np.where(cc < SKEW - QROWS, BACK_B * CHUNK - cc, BACK_B * CHUNK + SKEW - cc)
    idx = np.clip(dist, -REL_CLIP, REL_CLIP) + REL_CLIP
    m = np.zeros((SKEW, N_REL), np.float32)
    m[cc, idx] = 1.0
    return jnp.asarray(m)


def _toeplitz_bias(vec, carry=None):
    lk = (QG + BACK_B) * CHUNK
    nv = _n_variants(BACK_B)

    def body(v_ref, o_ref):
        xv = jnp.broadcast_to(v_ref[0], (QROWS, SKEW))
        row = lax.broadcasted_iota(jnp.int32, (QROWS, SKEW), 0)
        for bit in range(QROWS.bit_length() - 1):
            xv = jnp.where((row >> bit) & 1 == 1, pltpu.roll(xv, 1 << bit, 1), xv)
        ri = lax.broadcasted_iota(jnp.int32, (QROWS, lk), 0) // CHUNK
        col = lax.broadcasted_iota(jnp.int32, (QROWS, lk), 1)
        ci = col // CHUNK
        valid = (ci - ri >= 0) & (ci - ri <= BACK_B)
        for v in range(nv):
            o_ref[v, 0] = jnp.where(valid & (col >= (BACK_B - QG * v) * CHUNK), xv[:, :lk], NEG_INF)

    return _call(body, name="toeplitz_bias", grid=(H_B,),
                 in_specs=[pl.BlockSpec((1, 1, SKEW), lambda h: (h, 0, 0))],
                 out_specs=pl.BlockSpec((nv, 1, QROWS, lk), lambda h: (0, h, 0, 0)),
                 out_shape=jax.ShapeDtypeStruct((nv, H_B, QROWS, lk), F32), args=[vec], sem=("parallel",),
                 carry=carry)


def _diagonal_sums(dbias):
    lk = dbias.shape[2]

    def body(d_ref, o_ref):
        xp = jnp.concatenate([d_ref[0], jnp.zeros((QROWS, SKEW - lk), F32)], axis=1)
        xv = xp[0:CHUNK]
        for q in range(1, QG):
            xv = xv + pltpu.roll(xp[q * CHUNK:(q + 1) * CHUNK], SKEW - q * CHUNK, 1)
        row = lax.broadcasted_iota(jnp.int32, (CHUNK, SKEW), 0)
        for bit in range(CHUNK.bit_length() - 1):
            xv = jnp.where((row >> bit) & 1 == 1, pltpu.roll(xv, SKEW - (1 << bit), 1), xv)
        o_ref[0] = jnp.sum(xv, axis=0, keepdims=True)

    return _call(body, name="diagonal_sums", grid=(H_B,),
                 in_specs=[pl.BlockSpec((1, QROWS, lk), lambda h: (h, 0, 0))],
                 out_specs=pl.BlockSpec((1, 1, SKEW), lambda h: (h, 0, 0)),
                 out_shape=jax.ShapeDtypeStruct((H_B, 1, SKEW), F32), args=[dbias], sem=("parallel",))


def _attn_common(s, n_back, gqa, q_col, k_col, v_col):
    lk = (QG + n_back) * CHUNK
    pad = n_back * CHUNK
    wide = TPS * LANES
    q_spec = pl.BlockSpec((QROWS, wide), lambda t, g: (g, q_col // TPS + t))
    if gqa:
        k_spec = pl.BlockSpec((s, LANES), lambda t, g: (0, k_col))
        v_spec = pl.BlockSpec((s, LANES), lambda t, g: (0, v_col))
    else:
        k_spec = pl.BlockSpec((s, wide), lambda t, g: (0, k_col // TPS + t))
        v_spec = pl.BlockSpec((s, wide), lambda t, g: (0, v_col // TPS + t))
    last_variant = _n_variants(n_back) - 1
    bias_spec = pl.BlockSpec((None, 2 * TPS, QROWS, lk), lambda t, g: (jnp.minimum(g, last_variant), t, 0, 0))
    tile_spec = pl.BlockSpec((QROWS, wide), lambda t, g: (g, t))
    return lk, pad, q_spec, k_spec, v_spec, bias_spec, tile_spec


def _attention_fwd(proj, bias, sinks, *, n_back, gqa, q_col, k_col, v_col, name, carry=None):
    s = proj.shape[0]
    lk, pad, q_spec, k_spec, v_spec, bias_spec, tile_spec = _attn_common(s, n_back, gqa, q_col, k_col, v_col)
    n_t, n_g = 512 // (TPS * LANES), s // QROWS
    kv_wide = LANES if gqa else TPS * LANES

    def body(*refs):
        if gqa:
            q_ref, k_ref, v_ref, bias_ref, sink_ref, o_ref, l_ref, kpad, vpad = refs
        else:
            q_ref, k_ref, v_ref, bias_ref, o_ref, l_ref, kpad, vpad = refs
        t, g = pl.program_id(0), pl.program_id(1)

        @pl.when(g == 0)
        def _():
            kpad[0:pad, :] = jnp.zeros((pad, kv_wide), BF16)
            vpad[0:pad, :] = jnp.zeros((pad, kv_wide), BF16)
            kpad[pad:, :] = k_ref[...]
            vpad[pad:, :] = v_ref[...]

        start = pl.multiple_of(g * QROWS, QROWS)
        half = lax.broadcasted_iota(jnp.int32, (QROWS, LANES), 1) // HEAD_DIM
        for tt in range(TPS):
            lanes = slice(tt * LANES, (tt + 1) * LANES)
            kv_lanes = slice(0, LANES) if gqa else lanes
            kb = kpad[pl.ds(start, lk), kv_lanes]
            vb = vpad[pl.ds(start, lk), kv_lanes]
            q = q_ref[:, lanes] * (HEAD_DIM ** -0.5)
            if gqa:
                hk = (TPS * t + tt) // 2
                q_rolled = pltpu.roll(q.astype(F32), HEAD_DIM, 1).astype(BF16)
            outs, lses = [], []
            for e in range(2):
                if gqa:
                    kv_half = hk
                    src = jnp.where(hk == e, q, q_rolled)
                else:
                    kv_half = e
                    src = q
                qm = jnp.where(half == kv_half, src, jnp.zeros_like(src))
                sc = _dot_nt(qm, kb) + bias_ref[2 * tt + e]
                m = jnp.max(sc, axis=-1, keepdims=True)
                if gqa:
                    sk = sink_ref[2 * (TPS * t + tt) + e]
                    m = jnp.maximum(m, sk)
                p = jnp.exp(sc - m)
                l = jnp.sum(p, axis=-1, keepdims=True)
                if gqa:
                    l = l + jnp.exp(sk - m)
                pn = p / l
                outs.append(_dot(pn.astype(BF16), vb))
                lses.append(m + jnp.log(l))
            if gqa:
                same = jnp.where(hk == 0, outs[0], outs[1])
                other = jnp.where(hk == 0, outs[1], outs[0])
                o_ref[:, lanes] = jnp.where(half == hk, same, pltpu.roll(other, HEAD_DIM, 1))
            else:
                o_ref[:, lanes] = jnp.where(half == 0, outs[0], outs[1])
            l_ref[:, lanes] = jnp.where(half == 0, lses[0], lses[1])

    in_specs = [q_spec, k_spec, v_spec, bias_spec] + ([SMEM_SPEC] if gqa else [])
    args = [proj, proj, proj, bias] + ([sinks] if gqa else [])
    o_shape = jax.ShapeDtypeStruct((s, 512), F32)
    return _call(body, name=name, grid=(n_t, n_g), in_specs=in_specs, out_specs=(tile_spec, tile_spec),
                 out_shape=(o_shape, o_shape), args=args,
                 scratch=[pltpu.VMEM((s + pad, kv_wide), BF16), pltpu.VMEM((s + pad, kv_wide), BF16)],
                 sem=("arbitrary", "arbitrary"), carry=carry)


def _attention_bwd(proj, bias, sinks, do, lse, *, n_back, gqa, q_col, k_col, v_col, name, carry=None):
    s = proj.shape[0]
    lk, pad, q_spec, k_spec, v_spec, bias_spec, tile_spec = _attn_common(s, n_back, gqa, q_col, k_col, v_col)
    n_t, n_g = 512 // (TPS * LANES), s // QROWS
    kv_wide = LANES if gqa else TPS * LANES

    def body(*refs):
        if gqa:
            (q_ref, k_ref, v_ref, bias_ref, sink_ref, do_ref, l_ref,
             dq_ref, dk_ref, dv_ref, dsink_ref, kpad, vpad, dkpad, dvpad) = refs
        else:
            (q_ref, k_ref, v_ref, bias_ref, do_ref, l_ref,
             dq_ref, dk_ref, dv_ref, dbias_ref, kpad, vpad, dkpad, dvpad) = refs
        t, g = pl.program_id(0), pl.program_id(1)

        @pl.when(g == 0)
        def _():
            kpad[0:pad, :] = jnp.zeros((pad, kv_wide), BF16)
            vpad[0:pad, :] = jnp.zeros((pad, kv_wide), BF16)
            kpad[pad:, :] = k_ref[...]
            vpad[pad:, :] = v_ref[...]
            if gqa:
                dsink_ref[...] = jnp.zeros_like(dsink_ref)
            else:
                dbias_ref[...] = jnp.zeros_like(dbias_ref)

        @pl.when((g == 0) & (t == 0) if gqa else g == 0)
        def _():
            dkpad[...] = jnp.zeros_like(dkpad)
            dvpad[...] = jnp.zeros_like(dvpad)

        start = pl.multiple_of(g * QROWS, QROWS)
        half = lax.broadcasted_iota(jnp.int32, (QROWS, LANES), 1) // HEAD_DIM
        for tt in range(TPS):
            lanes = slice(tt * LANES, (tt + 1) * LANES)
            kv_lanes = slice(0, LANES) if gqa else lanes
            kb = kpad[pl.ds(start, lk), kv_lanes]
            vb = vpad[pl.ds(start, lk), kv_lanes]
            q = q_ref[:, lanes]
            dov = do_ref[:, lanes]
            lv = l_ref[:, lanes]
            if gqa:
                hk = (TPS * t + tt) // 2
                q_rolled = pltpu.roll(q.astype(F32), HEAD_DIM, 1).astype(BF16)
                do_rolled = pltpu.roll(dov, HEAD_DIM, 1)
            dqs = []
            dk_acc = jnp.zeros((lk, LANES), F32)
            dv_acc = jnp.zeros((lk, LANES), F32)
            for e in range(2):
                if gqa:
                    kv_half = hk
                    src = jnp.where(hk == e, q, q_rolled)
                    do_src = jnp.where(hk == e, dov, do_rolled)
                else:
                    kv_half = e
                    src = q
                    do_src = dov
                qm = jnp.where(half == kv_half, src, jnp.zeros_like(src))
                dom = jnp.where(half == kv_half, do_src, 0.0).astype(BF16)
                lcol = jnp.max(jnp.where(half == e, lv, -jnp.inf), axis=-1, keepdims=True)
                sc = _dot_nt(qm * (HEAD_DIM ** -0.5), kb) + bias_ref[2 * tt + e]
                pn = jnp.exp(sc - lcol)
                dp = _dot_nt(dom, vb)
                delta = jnp.sum(pn * dp, axis=-1, keepdims=True)
                ds = pn * (dp - delta)
                if gqa:
                    p_sink = jnp.exp(sink_ref[2 * (TPS * t + tt) + e] - lcol)
                    dsk = -jnp.sum(p_sink * delta, axis=0, keepdims=True)
                    row = 2 * tt + e
                    dsink_ref[0, row:row + 1, :] += jnp.broadcast_to(dsk, (1, LANES))
                else:
                    dbias_ref[2 * tt + e] += ds
                dsb = (ds * (HEAD_DIM ** -0.5)).astype(BF16)
                dqs.append(_dot(dsb, kb))
                dk_acc = dk_acc + _dot_tn(dsb, qm)
                dv_acc = dv_acc + _dot_tn(pn.astype(BF16), dom)
            dkpad[pl.ds(start, lk), kv_lanes] += dk_acc
            dvpad[pl.ds(start, lk), kv_lanes] += dv_acc
            if gqa:
                same = jnp.where(hk == 0, dqs[0], dqs[1])
                other = jnp.where(hk == 0, dqs[1], dqs[0])
                dq_ref[:, lanes] = jnp.where(half == hk, same, pltpu.roll(other, HEAD_DIM, 1)).astype(BF16)
            else:
                dq_ref[:, lanes] = jnp.where(half == 0, dqs[0], dqs[1]).astype(BF16)

        @pl.when((g == n_g - 1) & (t == n_t - 1) if gqa else g == n_g - 1)
        def _():
            dk_ref[...] = dkpad[pad:, :].astype(BF16)
            dv_ref[...] = dvpad[pad:, :].astype(BF16)

    in_specs = [q_spec, k_spec, v_spec, bias_spec] + ([SMEM_SPEC] if gqa else []) + [tile_spec, tile_spec]
    args = [proj, proj, proj, bias] + ([sinks] if gqa else []) + [do, lse]
    if gqa:
        kv_out = pl.BlockSpec((s, LANES), lambda t, g: (0, 0))
        kv_shape = jax.ShapeDtypeStruct((s, LANES), BF16)
        extra_spec = pl.BlockSpec((1, 8, LANES), lambda t, g: (t, 0, 0))
        extra_shape = jax.ShapeDtypeStruct((n_t, 8, LANES), F32)
    else:
        kv_out = pl.BlockSpec((s, kv_wide), lambda t, g: (0, t))
        kv_shape = jax.ShapeDtypeStruct((s, 512), BF16)
        extra_spec = pl.BlockSpec((2 * TPS, QROWS, lk), lambda t, g: (t, 0, 0))
        extra_shape = jax.ShapeDtypeStruct(bias.shape[1:], F32)
    return _call(body, name=name, grid=(n_t, n_g), in_specs=in_specs,
                 out_specs=(tile_spec, kv_out, kv_out, extra_spec),
                 out_shape=(jax.ShapeDtypeStruct((s, 512), BF16), kv_shape, kv_shape, extra_shape), args=args,
                 scratch=[pltpu.VMEM((s + pad, kv_wide), BF16), pltpu.VMEM((s + pad, kv_wide), BF16),
                          pltpu.VMEM((s + pad, kv_wide), F32), pltpu.VMEM((s + pad, kv_wide), F32)],
                 sem=("arbitrary", "arbitrary"), carry=carry)


def _sum_slots(r, name):
    n_slots, rows, k = r.shape

    def body(r_ref, o_ref):
        acc = r_ref[0].astype(F32)
        for j in range(1, n_slots):
            acc = acc + r_ref[j].astype(F32)
        o_ref[...] = acc

    return _call(body, name=name, grid=(k // LANES,),
                 in_specs=[pl.BlockSpec((n_slots, rows, LANES), lambda i: (0, 0, i))],
                 out_specs=pl.BlockSpec((rows, LANES), lambda i: (0, i)),
                 out_shape=jax.ShapeDtypeStruct((rows, k), F32), args=[r], sem=("parallel",))


def _sum_rows8(g):
    n = g.shape[2]

    def body(g_ref, o_ref):
        acc = g_ref[0]
        for j in range(1, N_DEV):
            acc = acc + g_ref[j]
        o_ref[...] = acc

    return pl.pallas_call(
        body, name="sum_small_grads", in_specs=[VMEM_SPEC], out_specs=VMEM_SPEC,
        out_shape=jax.ShapeDtypeStruct((1, n), F32), compiler_params=_params(),
    )(g)


def _ada_weight_grad(sc_t, dmod_cols):
    d = sc_t.shape[0]
    w = dmod_cols.shape[1]
    td = _pick(d, (256, 128))

    def body(sc_ref, dm_ref, o_ref):
        scv = sc_ref[...]
        dmv = dm_ref[...]
        acc = scv[:, 0:1] * dmv[0:1, :]
        for b in range(1, N_DEV):
            acc = acc + scv[:, b:b + 1] * dmv[b:b + 1, :]
        o_ref[...] = acc

    return _call(body, name="ada_weight_grad", grid=(d // td,),
                 in_specs=[pl.BlockSpec((td, N_DEV), lambda i: (i, 0)), pl.BlockSpec((N_DEV, w), lambda i: (0, 0))],
                 out_specs=pl.BlockSpec((td, w), lambda i: (i, 0)), out_shape=jax.ShapeDtypeStruct((d, w), F32),
                 args=[sc_t, dmod_cols], sem=("parallel",))


def _adamw_update(w, gv, m, v):
    nm = ADAM_B1 * m + (1.0 - ADAM_B1) * gv
    nv = ADAM_B2 * v + (1.0 - ADAM_B2) * (gv * gv)
    m_hat = nm / (1.0 - ADAM_B1 ** ADAM_STEP)
    v_hat = nv / (1.0 - ADAM_B2 ** ADAM_STEP)
    return -ADAM_LR * (m_hat / (jnp.sqrt(v_hat) + ADAM_EPS) + ADAM_WD * w), nm, nv


def _adamw(w, g, m, v, name):
    rows, cols = w.shape
    tr = _pick(rows, (256, 176, 128, 88, 64)) if rows > 256 else rows

    def body(w_ref, g_ref, m_ref, v_ref, d_ref, nm_ref, nv_ref):
        d_ref[...], nm_ref[...], nv_ref[...] = _adamw_update(w_ref[...], g_ref[...], m_ref[...], v_ref[...])

    spec = pl.BlockSpec((tr, cols), lambda i: (i, 0))
    shape = jax.ShapeDtypeStruct((rows, cols), F32)
    return _call(body, name=name, grid=(rows // tr,), in_specs=[spec] * 4, out_specs=(spec, spec, spec),
                 out_shape=(shape, shape, shape), args=[w, g, m, v], sem=("parallel",))


def _adamw_from_slots(w, own, slots, m, v, name):
    n_slots, rows, k = slots.shape

    def body(o_ref, s_ref, w_ref, m_ref, v_ref, g_ref, d_ref, nm_ref, nv_ref):
        gv = o_ref[...].astype(F32)
        for j in range(n_slots):
            gv = gv + s_ref[j].astype(F32)
        g_ref[...] = gv
        d_ref[...], nm_ref[...], nv_ref[...] = _adamw_update(w_ref[...], gv, m_ref[...], v_ref[...])

    tr = rows // 2 if rows % 32 == 0 else rows
    spec = pl.BlockSpec((tr, k), lambda i: (i, 0))
    shape = jax.ShapeDtypeStruct((rows, k), F32)
    return _call(body, name=name, grid=(rows // tr,),
                 in_specs=[spec, pl.BlockSpec((n_slots, tr, k), lambda i: (0, i, 0)), spec, spec, spec],
                 out_specs=(spec, spec, spec, spec), out_shape=(shape, shape, shape, shape),
                 args=[own, slots, w, m, v], sem=("parallel",))


def _adamw_small(g, w, m, v, sizes):
    n = w.shape[1]
    offs, off = [], 0
    for size in sizes:
        offs.append(off)
        off += size + (-size % LANES)

    def body(g_ref, w_ref, m_ref, v_ref, *out_refs):
        gv = g_ref[:, 0:n]
        dv, nm, nv = _adamw_update(w_ref[...], gv, m_ref[...], v_ref[...])
        for j, (o, size) in enumerate(zip(offs, sizes)):
            for k, val in enumerate((gv, dv, nm, nv)):
                out_refs[4 * j + k][...] = val[:, o:o + size]

    shapes = [jax.ShapeDtypeStruct((1, size), F32) for size in sizes for _ in range(4)]
    return pl.pallas_call(
        body, name="adamw_small", in_specs=[VMEM_SPEC] * 4, out_specs=tuple([VMEM_SPEC] * len(shapes)),
        out_shape=tuple(shapes), compiler_params=_params(),
    )(g, w, m, v)


SMALL = ("b_ada", "g_pre_ffn1", "g_post_ffn1", "g_pre_mix", "b_in", "sinks_a", "rel_bias_b", "g_grp_a",
         "g_grp_b", "b_out", "g_post_mix", "g_pre_ffn2", "g_post_ffn2")
WEIGHTS = ("w_ada", "b_ada", "g_pre_ffn1", "w_gate1", "w_up1", "w_down1", "g_post_ffn1", "g_pre_mix", "w_in",
           "b_in", "sinks_a", "rel_bias_b", "g_grp_a", "g_grp_b", "w_out", "b_out", "g_post_mix", "g_pre_ffn2",
           "w_gate2", "w_up2", "w_down2", "g_post_ffn2")


def kernel(x, c, w_ada, b_ada, g_pre_ffn1, w_gate1, w_up1, w_down1, g_post_ffn1, g_pre_mix, w_in, b_in, sinks_a, rel_bias_b, g_grp_a, g_grp_b, w_out, b_out, g_post_mix, g_pre_ffn2, w_gate2, w_up2, w_down2, g_post_ffn2, loss_target, m_w_ada, m_b_ada, m_g_pre_ffn1, m_w_gate1, m_w_up1, m_w_down1, m_g_post_ffn1, m_g_pre_mix, m_w_in, m_b_in, m_sinks_a, m_rel_bias_b, m_g_grp_a, m_g_grp_b, m_w_out, m_b_out, m_g_post_mix, m_g_pre_ffn2, m_w_gate2, m_w_up2, m_w_down2, m_g_post_ffn2, v_w_ada, v_b_ada, v_g_pre_ffn1, v_w_gate1, v_w_up1, v_w_down1, v_g_post_ffn1, v_g_pre_mix, v_w_in, v_b_in, v_sinks_a, v_rel_bias_b, v_g_grp_a, v_g_grp_b, v_w_out, v_b_out, v_g_post_mix, v_g_pre_ffn2, v_w_gate2, v_w_up2, v_w_down2, v_g_post_ffn2):
    given = dict(locals())
    weights = {n: given[n] for n in WEIGHTS}
    mom_m = {n: given["m_" + n] for n in WEIGHTS}
    mom_v = {n: given["v_" + n] for n in WEIGHTS}

    me = 4 * lax.axis_index("x") + 2 * lax.axis_index("y") + lax.axis_index("c")
    xs = x[0]
    tgt = loss_target[0]
    d_model = xs.shape[1]
    ada_cols = w_ada.shape[2]

    sh = {"wg1": w_gate1[0].T, "wu1": w_up1[0].T, "wd1": w_down1[0], "win": w_in[0].T, "wo": w_out[0],
          "wg2": w_gate2[0].T, "wu2": w_up2[0].T, "wd2": w_down2[0]}
    sh = {k: v.astype(BF16) for k, v in sh.items()}

    def gather(*names):
        return _gather_carry([sh[n] for n in names])

    bias_a = _alibi_bias()
    rel_m = _rel_index_matrix()
    rel_vec = jnp.dot(rel_bias_b[0], rel_m.T, precision=lax.Precision.HIGHEST)
    bias_b, (wg1, wu1) = _toeplitz_bias(rel_vec.reshape(H_B, 1, SKEW), carry=gather("wg1", "wu1"))

    b_cols = lax.dynamic_slice(b_ada, (0, me * ada_cols), (1, ada_cols))
    (sc_all, mod_rows), _ = _ada_forward(c, w_ada[0], b_cols, _Carry([], [], [], lambda *a: None, lambda *a: None))
    mod = mod_rows.reshape(N_MOD, d_model)
    shift1, scale1, gate1, shift2, scale2, gate2, shift3, scale3, gate3 = (mod[i:i + 1] for i in range(N_MOD))

    h1 = _pre_norm(xs, g_pre_ffn1, scale1, shift1, "pre_norm_ffn1")
    (a1, b1, u1), (wd1,) = _ffn_up(h1, wg1, wu1, "ffn_up_ffn1", carry=gather("wd1"))
    (y1, x1, h2), (win,) = _mm_nn(
        [(u1, wd1)], "ffn_down_ffn1", F32, carry=gather("win"),
        tail=_tail_post_pre(xs, g_post_ffn1, gate1, 0.5, g_pre_mix, scale2, shift2))

    proj, (wo,) = _mm_nt(h2, win, "in_proj", BF16, bias=b_in, carry=gather("wo"))
    sinks = sinks_a[0]
    cfg_a = dict(n_back=BACK_A, gqa=True, q_col=0, k_col=QA // LANES, v_col=(QA + KVA) // LANES)
    cfg_b = dict(n_back=BACK_B, gqa=False, q_col=(QA + 2 * KVA) // LANES, k_col=(QA + 2 * KVA + QB) // LANES,
                 v_col=(QA + 2 * KVA + 2 * QB) // LANES)
    (oa, lse_a), (wg2,) = _attention_fwd(proj, bias_a, sinks, name="attn_a", carry=gather("wg2"), **cfg_a)
    (ob, lse_b), (wu2,) = _attention_fwd(proj, bias_b, None, name="attn_b", carry=gather("wu2"), **cfg_b)
    ycat = _group_norm_cat(oa, ob, g_grp_a, g_grp_b)
    ymix, x2, h3 = _mm_nn([(ycat, wo)], "out_proj", F32, bias=b_out,
                          tail=_tail_post_pre(x1, g_post_mix, gate2, 1.0, g_pre_ffn2, scale3, shift3))

    (a3, b3, u3), (wd2,) = _ffn_up(h3, wg2, wu2, "ffn_up_ffn2", carry=gather("wd2"))

    flights, own = {}, {}

    def grad_pair(key, a_mat, b_mat, name):
        part, own[key] = _mm_tn_pair(a_mat, b_mat, name)
        return part

    def scatter_start(tag, after_vec, **parts):
        names = list(parts)
        sems, p_thru, lands, token = _scatter_start([parts[n] for n in names], "scatter_start_" + tag)
        flights[tag] = (names, sems, p_thru, lands)
        return after_vec + token[0:1, 0:1]

    dx3, dy, loss_part, s1 = _mm_nn([(u3, wd2)], "ffn_down_ffn2", None,
                                    tail=_tail_post_loss(x2, tgt, g_post_ffn2, gate3, 0.5))
    da, db = _ffn_down_bwd(dy, wd2, a3, b3, "ffn_down_bwd_ffn2")
    dwd2 = grad_pair("wd2", u3, dy, "grad_wd_ffn2")
    dwg2 = grad_pair("wg2", da, h3, "grad_wg_ffn2")
    dwu2 = grad_pair("wu2", db, h3, "grad_wu_ffn2")
    g_pre_tied = scatter_start("ffn2", g_pre_ffn2, wd2=dwd2, wg2=dwg2, wu2=dwu2)
    dx2, dymix, s2, s3, s1m, db_out = _mm_nn(
        [(da, wg2), (db, wu2)], "ffn_up_bwd_ffn2", None,
        tail=_tail_pre_post_bwd(x2, dx3, ymix, g_pre_tied, scale3, g_post_mix, gate2, 1.0))
    sm3 = dict(shift=s3, scale=s2 * g_pre_ffn2, gate=0.5 * g_post_ffn2 * s1,
               g_pre=(1.0 + scale3) * s2, g_post=(0.5 * gate3) * s1)

    dycat = _mm_nt(dymix, wo, "out_proj_bwd", F32)
    dwo = grad_pair("wo", ycat, dymix, "grad_wo")
    doa, dob, dg_a, dg_b = _group_norm_bwd(dycat, oa, ob, g_grp_a, g_grp_b)
    dqa, dka, dva, dsink = _attention_bwd(proj, bias_a, sinks, doa, lse_a, name="attn_a_bwd", **cfg_a)
    dqb, dkb, dvb, dbias = _attention_bwd(proj, bias_b, None, dob, lse_b, name="attn_b_bwd", **cfg_b)
    dproj = jnp.concatenate([dqa, dka, dva, dqb, dkb, dvb], axis=1)
    db_in = _col_sum(dproj, "grad_b_in")
    dwin = grad_pair("win", dproj, h2, "grad_win")
    g_pre_tied = scatter_start("mix", g_pre_mix, wo=dwo, win=dwin)
    dx1, dy, s2m, s3m, s1, _ = _mm_nn(
        [(dproj, win)], "in_proj_bwd", None,
        tail=_tail_pre_post_bwd(x1, dx2, y1, g_pre_tied, scale2, g_post_ffn1, gate1, 0.5))
    d_rel = jnp.dot(_diagonal_sums(dbias).reshape(H_B, SKEW), rel_m, precision=lax.Precision.HIGHEST)
    d_sinks = dsink[:, :2 * TPS, 0].reshape(1, H_A)

    da, db = _ffn_down_bwd(dy, wd1, a1, b1, "ffn_down_bwd_ffn1")
    dwd1 = grad_pair("wd1", u1, dy, "grad_wd_ffn1")
    dwg1 = grad_pair("wg1", da, h1, "grad_wg_ffn1")
    dwu1 = grad_pair("wu1", db, h1, "grad_wu_ffn1")
    g_pre_tied = scatter_start("ffn1", g_pre_ffn1, wd1=dwd1, wg1=dwg1, wu1=dwu1)
    dx0, s2, s3 = _mm_nn([(da, wg1), (db, wu1)], "ffn_up_bwd_ffn1", None,
                         tail=_tail_pre_bwd(xs, dx1, g_pre_tied, scale1))
    sm1 = dict(shift=s3, scale=s2 * g_pre_ffn1, gate=0.5 * g_post_ffn1 * s1,
               g_pre=(1.0 + scale1) * s2, g_post=(0.5 * gate1) * s1)

    dmod = jnp.concatenate([sm1["shift"], sm1["scale"], sm1["gate"],
                            s3m, s2m * g_pre_mix, g_post_mix * s1m,
                            sm3["shift"], sm3["scale"], sm3["gate"]], axis=1)
    small_parts = {
        "b_ada": dmod, "g_pre_ffn1": sm1["g_pre"], "g_post_ffn1": sm1["g_post"],
        "g_pre_mix": (1.0 + scale2) * s2m, "b_in": db_in, "sinks_a": d_sinks,
        "rel_bias_b": d_rel.reshape(1, H_B * N_REL), "g_grp_a": dg_a, "g_grp_b": dg_b, "b_out": db_out,
        "g_post_mix": gate2 * s1m, "g_pre_ffn2": sm3["g_pre"], "g_post_ffn2": sm3["g_post"]}
    sizes = [small_parts[n].shape[1] for n in SMALL]

    def pack(parts):
        cells = []
        for p in parts:
            cells.append(p)
            if p.shape[1] % LANES:
                cells.append(jnp.zeros((1, -p.shape[1] % LANES), F32))
        return jnp.concatenate(cells, axis=1)

    packed = pack([small_parts[n] for n in SMALL] + [loss_part])
    n_packed = packed.shape[1]
    small_sems, packed_thru, small_land, small_token = _small_gather_start(packed)

    slots = {}
    for tag in ("ffn2", "mix", "ffn1"):
        names, sems, p_thru, lands = flights[tag]
        _, l_done = _scatter_wait(sems, p_thru, lands, small_token, "scatter_wait_" + tag)
        slots.update(zip(names, l_done))

    out_g, out_d, out_m, out_v = {}, {}, {}, {}
    for n, key, transposed in (("w_gate1", "wg1", True), ("w_up1", "wu1", True), ("w_down1", "wd1", False),
                               ("w_in", "win", True), ("w_out", "wo", False), ("w_gate2", "wg2", True),
                               ("w_up2", "wu2", True), ("w_down2", "wd2", False)):
        view = (lambda t: t.T) if transposed else (lambda t: t)
        res = _adamw_from_slots(view(weights[n][0]), own[key], slots[key], view(mom_m[n][0]), view(mom_v[n][0]),
                                "adamw_" + n)
        out_g[n], out_d[n], out_m[n], out_v[n] = (view(t)[None] for t in res)

    packed_done, small_land = _small_gather_wait(small_sems, packed_thru, small_land, out_v["w_down2"])
    gathered = lax.dynamic_update_slice(small_land, packed_done[None], (me, 0, 0))
    small_sum = _sum_rows8(gathered)
    loss = small_sum[0, n_packed - LANES]
    dmod_cols = lax.dynamic_slice(gathered.reshape(N_DEV, n_packed), (0, me * ada_cols), (N_DEV, ada_cols))
    g_ada = _ada_weight_grad(sc_all.reshape(N_DEV, d_model).T, dmod_cols)
    d_, m_, v_ = _adamw(w_ada[0], g_ada, m_w_ada[0], v_w_ada[0], "adamw_w_ada")
    out_g["w_ada"], out_d["w_ada"], out_m["w_ada"], out_v["w_ada"] = g_ada[None], d_[None], m_[None], v_[None]

    small_out = _adamw_small(small_sum, *(pack([tree[n].reshape(1, -1) for n in SMALL])
                                          for tree in (weights, mom_m, mom_v)), sizes)
    for j, n in enumerate(SMALL):
        shape = weights[n].shape
        out_g[n], out_d[n], out_m[n], out_v[n] = (t.reshape(shape) for t in small_out[4 * j:4 * j + 4])

    return (loss, dx0[None], *[out_g[n] for n in WEIGHTS], *[out_d[n] for n in WEIGHTS],
            *[out_m[n] for n in WEIGHTS], *[out_v[n] for n in WEIGHTS])
```

```python
import numpy as np
import jax
import jax.numpy as jnp
from jax import lax
from jax.experimental import pallas as pl
from jax.experimental.pallas import tpu as pltpu

F32 = jnp.float32
BF16 = jnp.bfloat16
MESH = pl.DeviceIdType.MESH
ANY = pl.BlockSpec(memory_space=pl.ANY)
VMEM_SPEC = pl.BlockSpec(memory_space=pltpu.VMEM)
SMEM_SPEC = pl.BlockSpec(memory_space=pltpu.SMEM)

N_DEV = 8
CHUNK = 64
HEAD_DIM = 64
LANES = 128
H_A, KV_A, H_B = 8, 2, 8
BACK_A, BACK_B = 2, 8
REL_CLIP = 128
N_REL = 2 * REL_CLIP + 1
QA, KVA, QB = H_A * HEAD_DIM, KV_A * HEAD_DIM, H_B * HEAD_DIM
D_IN = QA + 2 * KVA + 3 * QB
N_MOD = 9
EPS = 1e-6
NEG_INF = -1e30
QG = 4
QROWS = QG * CHUNK
TPS = 2
SKEW = 1024
ADAM_LR, ADAM_B1, ADAM_B2, ADAM_EPS, ADAM_WD, ADAM_STEP = 0.001, 0.9, 0.999, 1e-08, 0.01, 10
VMEM_LIMIT = 56 * 2 ** 20


def _pick(n, cands):
    for c in cands:
        if n % c == 0:
            return c
    return n


def _pieces(n, width=2 * LANES):
    return [(lo, min(lo + width, n)) for lo in range(0, n, width)]


def _params(sem=None):
    return pltpu.CompilerParams(dimension_semantics=sem, vmem_limit_bytes=VMEM_LIMIT)


def _dot_nt(a, b):
    return lax.dot_general(a, b, (((1,), (1,)), ((), ())), preferred_element_type=F32)


def _dot_tn(a, b):
    return lax.dot_general(a, b, (((0,), (0,)), ((), ())), preferred_element_type=F32)


def _dot(a, b):
    return jnp.dot(a, b, preferred_element_type=F32)


def _sigmoid(a):
    return 0.5 * (jnp.tanh(0.5 * a) + 1.0)


def _mesh_pos():
    return lax.axis_index("x"), lax.axis_index("y"), lax.axis_index("c")


def _peer(x, y, c, r):
    px = 1 - x if r & 4 else x
    py = 1 - y if r & 2 else y
    pc = 1 - c if r & 1 else c
    return px, py, pc


class _Carry:
    def __init__(self, ins, out_shapes, scratch, start, finish, aliased=0):
        self.ins, self.out_shapes, self.scratch = list(ins), list(out_shapes), list(scratch)
        self.start, self.finish = start, finish
        self.aliased = aliased


def _call(body, *, name, grid, in_specs, out_specs, out_shape, args, scratch=(), sem=None, carry=None):
    single = not isinstance(out_shape, (tuple, list))
    out_specs = (out_specs,) if single else tuple(out_specs)
    out_shape = (out_shape,) if single else tuple(out_shape)
    if carry is None:
        res = pl.pallas_call(body, name=name, grid=grid, in_specs=list(in_specs), out_specs=out_specs,
                             out_shape=out_shape, scratch_shapes=list(scratch), compiler_params=_params(sem))(*args)
        return res[0] if single else res
    n_in, n_out, n_s = len(in_specs), len(out_shape), len(scratch)
    ci, co = len(carry.ins), len(carry.out_shapes)

    def wrapped(*refs):
        ins, cins = refs[:n_in], refs[n_in:n_in + ci]
        outs = refs[n_in + ci:n_in + ci + n_out]
        couts = refs[n_in + ci + n_out:n_in + ci + n_out + co]
        scr = refs[n_in + ci + n_out + co:n_in + ci + n_out + co + n_s]
        cscr = refs[n_in + ci + n_out + co + n_s:]
        first, last = None, None
        for ax, n in enumerate(grid):
            f, l = pl.program_id(ax) == 0, pl.program_id(ax) == n - 1
            first = f if first is None else first & f
            last = l if last is None else last & l
        pl.when(first)(lambda: carry.start(cins, couts, cscr))
        body(*ins, *outs, *scr)
        pl.when(last)(lambda: carry.finish(cins, couts, cscr))

    res = pl.pallas_call(
        wrapped, name=name, grid=grid, in_specs=list(in_specs) + [ANY] * ci, out_specs=out_specs + (ANY,) * co,
        out_shape=out_shape + tuple(carry.out_shapes), scratch_shapes=list(scratch) + carry.scratch,
        input_output_aliases={n_in + i: n_out + i for i in range(carry.aliased)},
        compiler_params=_params(("arbitrary",) * len(grid)))(*args, *carry.ins)
    main = res[:n_out]
    return (main[0] if single else main), res[n_out:]


def _run_carry(carry, name):
    ci, co = len(carry.ins), len(carry.out_shapes)

    def body(*refs):
        carry.start(refs[:ci], refs[ci:ci + co], refs[ci + co:])
        carry.finish(refs[:ci], refs[ci:ci + co], refs[ci + co:])

    return pl.pallas_call(body, name=name, in_specs=[ANY] * ci, out_specs=(ANY,) * co,
                          out_shape=tuple(carry.out_shapes), scratch_shapes=carry.scratch,
                          input_output_aliases={i: i for i in range(carry.aliased)},
                          compiler_params=_params())(*carry.ins)


def _gather_carry(shards):
    n_w = len(shards)
    rows = [s.shape[0] for s in shards]

    def plan(ins, outs, scr):
        send_sems, recv_sems, local_sems = scr
        x, y, c = _mesh_pos()
        me, sibling = (x, y, c), (x, y, 1 - c)
        chips = [(1 - x, y), (x, 1 - y), (1 - x, 1 - y)]

        def block(w, dev):
            start = pl.multiple_of((4 * dev[0] + 2 * dev[1] + dev[2]) * rows[w], 16)
            return outs[w].at[pl.ds(start, rows[w]), :]

        def copy(w, k, dev, to, src=None):
            return pltpu.make_async_remote_copy(
                src_ref=block(w, dev) if src is None else src, dst_ref=block(w, dev),
                send_sem=send_sems.at[w, k], recv_sem=recv_sems.at[w, k], device_id=to, device_id_type=MESH)

        mine = [pltpu.make_async_copy(ins[w], block(w, me), local_sems.at[w]) for w in range(n_w)]
        first = []
        for j, chip in enumerate(chips):
            first += [copy(w, 1 + j, me, (*chip, c), src=ins[w]) for w in range(n_w)]
        first += [copy(w, 0, me, sibling, src=ins[w]) for w in range(n_w)]
        return c, me, sibling, chips, copy, mine, first

    def start(ins, outs, scr):
        _, _, _, _, _, mine, first = plan(ins, outs, scr)
        for cp in mine + first:
            cp.start()

    def finish(ins, outs, scr):
        c, me, sibling, chips, copy, mine, first = plan(ins, outs, scr)
        passed = []
        for j, chip in enumerate(chips):
            for w in range(n_w):
                copy(w, 1 + j, (*chip, c), me).wait_recv()
                cp = copy(w, 4 + j, (*chip, c), sibling)
                cp.start()
                passed.append(cp)
        for w in range(n_w):
            copy(w, 0, sibling, me).wait_recv()
        for j, chip in enumerate(chips):
            for w in range(n_w):
                copy(w, 4 + j, (*chip, 1 - c), me).wait_recv()
        for cp in first + passed:
            cp.wait_send()
        for cp in mine:
            cp.wait()

    return _Carry(
        shards, [jax.ShapeDtypeStruct((N_DEV * s.shape[0], s.shape[1]), s.dtype) for s in shards],
        [pltpu.SemaphoreType.DMA((n_w, N_DEV - 1)), pltpu.SemaphoreType.DMA((n_w, N_DEV - 1)),
         pltpu.SemaphoreType.DMA((n_w,))], start, finish)


def _scatter_carry(parts):
    n_w = len(parts)
    n_chip = N_DEV // 2
    rows = [g.shape[0] // n_chip for g in parts]

    def plan(ins, outs, scr):
        send_sems, recv_sems, local_sems = scr
        x, y, c = _mesh_pos()

        def src(w, chip_index):
            return ins[w].at[pl.ds(pl.multiple_of(chip_index * rows[w], 16), rows[w]), :]

        mine = [pltpu.make_async_copy(src(w, 2 * x + y), outs[w].at[0], local_sems.at[w]) for w in range(n_w)]
        copies = []
        for r in (3, 2, 1):
            px, py, _ = _peer(x, y, c, 2 * r)
            for w in range(n_w):
                copies.append(pltpu.make_async_remote_copy(
                    src_ref=src(w, 2 * px + py), dst_ref=outs[w].at[r], send_sem=send_sems.at[w, r - 1],
                    recv_sem=recv_sems.at[w, r - 1], device_id=(px, py, c), device_id_type=MESH))
        return mine, copies

    def start(ins, outs, scr):
        mine, copies = plan(ins, outs, scr)
        for cp in mine + copies:
            cp.start()

    def finish(ins, outs, scr):
        mine, copies = plan(ins, outs, scr)
        for cp in copies:
            cp.wait_recv()
        for cp in copies:
            cp.wait_send()
        for cp in mine:
            cp.wait()

    return _Carry(
        parts, [jax.ShapeDtypeStruct((n_chip, r, g.shape[1]), g.dtype) for r, g in zip(rows, parts)],
        [pltpu.SemaphoreType.DMA((n_w, n_chip - 1)), pltpu.SemaphoreType.DMA((n_w, n_chip - 1)),
         pltpu.SemaphoreType.DMA((n_w,))], start, finish)


HBM_SPEC = pl.BlockSpec(memory_space=pltpu.HBM)
SEM_SPEC = pl.BlockSpec(memory_space=pltpu.SEMAPHORE)
N_CHIP = N_DEV // 2


def _scatter_copy(part_ref, land_ref, send_sem, recv_sem, r, rows):
    x, y, c = _mesh_pos()
    px, py, _ = _peer(x, y, c, 2 * r)
    src = part_ref.at[pl.ds(pl.multiple_of((2 * px + py) * rows, 16), rows), :]
    return pltpu.make_async_remote_copy(
        src_ref=src, dst_ref=land_ref.at[r - 1], send_sem=send_sem, recv_sem=recv_sem,
        device_id=(px, py, c), device_id_type=MESH)


def _scatter_order(n_w):
    return [(w, r) for r in (3, 2, 1) for w in range(n_w)]


def _scatter_start(parts, name):
    n_w = len(parts)
    rows = [p.shape[0] // N_CHIP for p in parts]
    order = _scatter_order(n_w)
    lands = [pltpu.with_memory_space_constraint(lax.empty((N_CHIP - 1, r, p.shape[1]), p.dtype), pltpu.HBM)
             for r, p in zip(rows, parts)]

    def body(*refs):
        part_refs, land_refs = refs[:n_w], refs[n_w:2 * n_w]
        sems = refs[2 * n_w:2 * n_w + 2 * len(order)]
        token = refs[-1]
        for j, (w, r) in enumerate(order):
            _scatter_copy(part_refs[w], land_refs[w], sems[2 * j], sems[2 * j + 1], r, rows[w]).start()
        token[...] = jnp.zeros_like(token)

    n_sem = 2 * len(order)
    res = pl.pallas_call(
        body, name=name,
        out_shape=(*[pltpu.SemaphoreType.DMA(())] * n_sem, *[pltpu.HBM(p.shape, p.dtype) for p in parts],
                   *[pltpu.HBM(l.shape, l.dtype) for l in lands], jax.ShapeDtypeStruct((8, LANES), F32)),
        in_specs=[HBM_SPEC] * (2 * n_w), out_specs=(*[SEM_SPEC] * n_sem, *[HBM_SPEC] * (2 * n_w), VMEM_SPEC),
        input_output_aliases={i: n_sem + i for i in range(2 * n_w)},
        compiler_params=pltpu.CompilerParams(has_side_effects=pltpu.SideEffectType.DATAFLOW_SIDE_EFFECTING),
    )(*[pltpu.with_memory_space_constraint(p, pltpu.HBM) for p in parts], *lands)
    return (list(res[:n_sem]), list(res[n_sem:n_sem + n_w]), list(res[n_sem + n_w:n_sem + 2 * n_w]), res[-1])


def _scatter_wait(sems, parts, lands, after, name):
    n_w = len(parts)
    rows = [p.shape[0] // N_CHIP for p in parts]
    order = _scatter_order(n_w)

    def body(*refs):
        part_refs, land_refs = refs[:n_w], refs[n_w:2 * n_w]
        sem_refs = refs[2 * n_w:2 * n_w + 2 * len(order)]
        for j, (w, r) in enumerate(order):
            cp = _scatter_copy(part_refs[w], land_refs[w], sem_refs[2 * j], sem_refs[2 * j + 1], r, rows[w])
            cp.wait_send()
            cp.wait_recv()

    res = pl.pallas_call(
        body, name=name,
        out_shape=(*[pltpu.HBM(p.shape, p.dtype) for p in parts], *[pltpu.HBM(l.shape, l.dtype) for l in lands]),
        in_specs=[HBM_SPEC] * (2 * n_w) + [SEM_SPEC] * len(sems) + [ANY],
        out_specs=tuple([HBM_SPEC] * (2 * n_w)),
        input_output_aliases={i: i for i in range(2 * n_w)},
        compiler_params=pltpu.CompilerParams(has_side_effects=pltpu.SideEffectType.DATAFLOW_SIDE_EFFECTING),
    )(*parts, *lands, *sems, after)
    return list(res[:n_w]), list(res[n_w:])


def _rows_of(arr_ref, rows, dev):
    start = pl.multiple_of((4 * dev[0] + 2 * dev[1] + dev[2]) * rows, 16)
    return arr_ref.at[pl.ds(start, rows), :]


def _gather_peer(k):
    x, y, c = _mesh_pos()
    return [(x, y, 1 - c), (1 - x, y, c), (x, 1 - y, c), (1 - x, 1 - y, c)][k]


def _gather_send(shard_ref, arr_ref, send_sem, recv_sem, k, rows):
    return pltpu.make_async_remote_copy(
        src_ref=shard_ref, dst_ref=_rows_of(arr_ref, rows, _mesh_pos()), send_sem=send_sem, recv_sem=recv_sem,
        device_id=_gather_peer(k), device_id_type=MESH)


def _gather_arrival(shard_ref, arr_ref, send_sem, recv_sem, k, rows):
    peer = _gather_peer(k)
    return pltpu.make_async_remote_copy(
        src_ref=shard_ref, dst_ref=_rows_of(arr_ref, rows, peer), send_sem=send_sem, recv_sem=recv_sem,
        device_id=peer, device_id_type=MESH)


GATHER_ORDER = (3, 1, 2, 0)


def _gather_start(shards, name):
    n_w = len(shards)
    rows = [s.shape[0] for s in shards]
    arrays = [pltpu.with_memory_space_constraint(lax.empty((N_DEV * s.shape[0], s.shape[1]), s.dtype), pltpu.HBM)
              for s in shards]
    n_sem = 2 * 4 * n_w

    def body(*refs):
        shard_refs, arr_refs = refs[:n_w], refs[n_w:2 * n_w]
        sems = refs[2 * n_w:2 * n_w + n_sem]
        token = refs[-1]
        for w in range(n_w):
            for k in GATHER_ORDER:
                j = 2 * (4 * w + k)
                _gather_send(shard_refs[w], arr_refs[w], sems[j], sems[j + 1], k, rows[w]).start()
        token[...] = jnp.zeros_like(token)

    res = pl.pallas_call(
        body, name=name,
        out_shape=(*[pltpu.SemaphoreType.DMA(())] * n_sem, *[pltpu.HBM(s.shape, s.dtype) for s in shards],
                   *[pltpu.HBM(a.shape, a.dtype) for a in arrays], jax.ShapeDtypeStruct((8, LANES), F32)),
        in_specs=[HBM_SPEC] * (2 * n_w), out_specs=(*[SEM_SPEC] * n_sem, *[HBM_SPEC] * (2 * n_w), VMEM_SPEC),
        input_output_aliases={i: n_sem + i for i in range(2 * n_w)},
        compiler_params=pltpu.CompilerParams(has_side_effects=pltpu.SideEffectType.DATAFLOW_SIDE_EFFECTING),
    )(*[pltpu.with_memory_space_constraint(s, pltpu.HBM) for s in shards], *arrays)
    sems = [[(res[2 * (4 * w + k)], res[2 * (4 * w + k) + 1]) for k in range(4)] for w in range(n_w)]
    return sems, list(res[n_sem:n_sem + n_w]), list(res[n_sem + n_w:n_sem + 2 * n_w]), res[-1]


def _gather_wait(sems, shards, arrays, after, name):
    n_w = len(shards)
    rows = [s.shape[0] for s in shards]
    flat = [s for per_w in sems for pair in per_w for s in pair]

    def body(*refs):
        shard_refs, arr_refs = refs[:n_w], refs[n_w:2 * n_w]
        sem_refs = refs[2 * n_w:2 * n_w + len(flat)]
        for w in range(n_w):
            for k in GATHER_ORDER:
                j = 2 * (4 * w + k)
                _gather_send(shard_refs[w], arr_refs[w], sem_refs[j], sem_refs[j + 1], k, rows[w]).wait_send()
                _gather_arrival(shard_refs[w], arr_refs[w], sem_refs[j], sem_refs[j + 1], k, rows[w]).wait_recv()

    res = pl.pallas_call(
        body, name=name,
        out_shape=(*[pltpu.HBM(s.shape, s.dtype) for s in shards], *[pltpu.HBM(a.shape, a.dtype) for a in arrays]),
        in_specs=[HBM_SPEC] * (2 * n_w) + [SEM_SPEC] * len(flat) + [ANY], out_specs=tuple([HBM_SPEC] * (2 * n_w)),
        input_output_aliases={i: i for i in range(2 * n_w)},
        compiler_params=pltpu.CompilerParams(has_side_effects=pltpu.SideEffectType.DATAFLOW_SIDE_EFFECTING),
    )(*shards, *arrays, *flat, after)
    return list(res[:n_w]), list(res[n_w:])


def _forward_carry(arrays, shards):
    n_w = len(arrays)
    rows = [s.shape[0] for s in shards]

    def plan(ins, outs, scr):
        send_sems, recv_sems, local_sems = scr
        x, y, c = _mesh_pos()
        chips = [(1 - x, y), (x, 1 - y), (1 - x, 1 - y)]
        mine = [pltpu.make_async_copy(ins[n_w + w], _rows_of(outs[w], rows[w], (x, y, c)), local_sems.at[w])
                for w in range(n_w)]

        def passed(w, j, core):
            blk = _rows_of(outs[w], rows[w], (*chips[j], core))
            return pltpu.make_async_remote_copy(
                src_ref=blk, dst_ref=blk, send_sem=send_sems.at[w, j], recv_sem=recv_sems.at[w, j],
                device_id=(x, y, 1 - c), device_id_type=MESH)

        return c, mine, passed

    def start(ins, outs, scr):
        c, mine, passed = plan(ins, outs, scr)
        for cp in mine:
            cp.start()
        for j in range(3):
            for w in range(n_w):
                passed(w, j, c).start()

    def finish(ins, outs, scr):
        c, mine, passed = plan(ins, outs, scr)
        for j in range(3):
            for w in range(n_w):
                passed(w, j, 1 - c).wait_recv()
        for j in range(3):
            for w in range(n_w):
                passed(w, j, c).wait_send()
        for cp in mine:
            cp.wait()

    return _Carry(
        list(arrays) + list(shards), [jax.ShapeDtypeStruct(a.shape, a.dtype) for a in arrays],
        [pltpu.SemaphoreType.DMA((n_w, 3)), pltpu.SemaphoreType.DMA((n_w, 3)), pltpu.SemaphoreType.DMA((n_w,))],
        start, finish, aliased=n_w)


def _small_copy(v_ref, land_ref, send_sem, recv_sem, r):
    x, y, c = _mesh_pos()
    px, py, pc = _peer(x, y, c, r)
    return pltpu.make_async_remote_copy(
        src_ref=v_ref, dst_ref=land_ref.at[4 * x + 2 * y + c], send_sem=send_sem, recv_sem=recv_sem,
        device_id=(px, py, pc), device_id_type=MESH)


def _small_gather_start(v):
    land = pltpu.with_memory_space_constraint(lax.empty((N_DEV,) + v.shape, v.dtype), pltpu.HBM)

    def body(v_ref, land_ref, *rest):
        sems, token = rest[:2 * (N_DEV - 1)], rest[-1]
        for r in range(1, N_DEV):
            _small_copy(v_ref, land_ref, sems[2 * r - 2], sems[2 * r - 1], r).start()
        token[...] = jnp.zeros_like(token)

    n_sem = 2 * (N_DEV - 1)
    res = pl.pallas_call(
        body, name="small_gather_start",
        out_shape=(*[pltpu.SemaphoreType.DMA(())] * n_sem, pltpu.HBM(v.shape, v.dtype),
                   pltpu.HBM(land.shape, land.dtype), jax.ShapeDtypeStruct((8, LANES), F32)),
        in_specs=[HBM_SPEC, HBM_SPEC], out_specs=(*[SEM_SPEC] * n_sem, HBM_SPEC, HBM_SPEC, VMEM_SPEC),
        input_output_aliases={0: n_sem, 1: n_sem + 1},
        compiler_params=pltpu.CompilerParams(has_side_effects=pltpu.SideEffectType.DATAFLOW_SIDE_EFFECTING),
    )(pltpu.with_memory_space_constraint(v, pltpu.HBM), land)
    return list(res[:n_sem]), res[n_sem], res[n_sem + 1], res[-1]


def _small_gather_wait(sems, v, land, after):
    def body(v_ref, land_ref, *rest):
        for r in range(1, N_DEV):
            cp = _small_copy(v_ref, land_ref, rest[2 * r - 2], rest[2 * r - 1], r)
            cp.wait_send()
            x, y, c = _mesh_pos()
            px, py, pc = _peer(x, y, c, r)
            pltpu.make_async_remote_copy(
                src_ref=v_ref, dst_ref=land_ref.at[4 * px + 2 * py + pc], send_sem=rest[2 * r - 2],
                recv_sem=rest[2 * r - 1], device_id=(px, py, pc), device_id_type=MESH).wait_recv()

    res = pl.pallas_call(
        body, name="small_gather_wait",
        out_shape=(pltpu.HBM(v.shape, v.dtype), pltpu.HBM(land.shape, land.dtype)),
        in_specs=[HBM_SPEC, HBM_SPEC] + [SEM_SPEC] * len(sems) + [ANY], out_specs=(HBM_SPEC, HBM_SPEC),
        input_output_aliases={0: 0, 1: 1},
        compiler_params=pltpu.CompilerParams(has_side_effects=pltpu.SideEffectType.DATAFLOW_SIDE_EFFECTING),
    )(v, land, *sems, after)
    return res[0], res[1]


def _ada_forward(c_row, w_ada, b_cols, carry):
    d = c_row.shape[1]
    wcols = w_ada.shape[1]
    ci, co = len(carry.ins), len(carry.out_shapes)

    def body(*refs):
        c_ref, w_ref, b_ref = refs[:3]
        cins = refs[3:3 + ci]
        sc_ref, mod_ref = refs[3 + ci:5 + ci]
        couts = refs[5 + ci:5 + ci + co]
        rows_ref, send_sems, recv_sems = refs[5 + ci + co:8 + ci + co]
        cscr = refs[8 + ci + co:]
        carry.start(cins, couts, cscr)
        x, y, c = _mesh_pos()
        me = 4 * x + 2 * y + c
        cv = c_ref[...]
        sc_ref[me] = cv * _sigmoid(cv)

        sends = []
        for r in range(1, N_DEV):
            px, py, pc = _peer(x, y, c, r)
            cp = pltpu.make_async_remote_copy(
                src_ref=sc_ref.at[me], dst_ref=sc_ref.at[me], send_sem=send_sems.at[0, r - 1],
                recv_sem=recv_sems.at[0, r - 1], device_id=(px, py, pc), device_id_type=MESH)
            cp.start()
            sends.append(cp)
        for r in range(1, N_DEV):
            px, py, pc = _peer(x, y, c, r)
            pid = 4 * px + 2 * py + pc
            pltpu.make_async_remote_copy(
                src_ref=sc_ref.at[pid], dst_ref=sc_ref.at[pid], send_sem=send_sems.at[0, r - 1],
                recv_sem=recv_sems.at[0, r - 1], device_id=(px, py, pc), device_id_type=MESH).wait_recv()
        for cp in sends:
            cp.wait_send()

        sc_all = jnp.concatenate([sc_ref[j] for j in range(N_DEV)], axis=0)
        rows = _dot(sc_all.astype(BF16), w_ref[...].astype(BF16)) + b_ref[...]
        for j in range(N_DEV):
            rows_ref[j] = rows[j:j + 1, :]
        mod_ref[me] = rows_ref[me]

        sends = []
        for r in range(1, N_DEV):
            px, py, pc = _peer(x, y, c, r)
            pid = 4 * px + 2 * py + pc
            cp = pltpu.make_async_remote_copy(
                src_ref=rows_ref.at[pid], dst_ref=mod_ref.at[me], send_sem=send_sems.at[1, r - 1],
                recv_sem=recv_sems.at[1, r - 1], device_id=(px, py, pc), device_id_type=MESH)
            cp.start()
            sends.append(cp)
        for r in range(1, N_DEV):
            px, py, pc = _peer(x, y, c, r)
            pid = 4 * px + 2 * py + pc
            pltpu.make_async_remote_copy(
                src_ref=rows_ref.at[pid], dst_ref=mod_ref.at[pid], send_sem=send_sems.at[1, r - 1],
                recv_sem=recv_sems.at[1, r - 1], device_id=(px, py, pc), device_id_type=MESH).wait_recv()
        for cp in sends:
            cp.wait_send()
        carry.finish(cins, couts, cscr)

    res = pl.pallas_call(
        body, name="ada_forward",
        out_shape=(jax.ShapeDtypeStruct((N_DEV, 1, d), F32), jax.ShapeDtypeStruct((N_DEV, 1, wcols), F32),
                   *carry.out_shapes),
        in_specs=[VMEM_SPEC, VMEM_SPEC, VMEM_SPEC] + [ANY] * ci, out_specs=(VMEM_SPEC, VMEM_SPEC) + (ANY,) * co,
        scratch_shapes=[pltpu.VMEM((N_DEV, 1, wcols), F32), pltpu.SemaphoreType.DMA((2, N_DEV - 1)),
                        pltpu.SemaphoreType.DMA((2, N_DEV - 1))] + carry.scratch,
        compiler_params=_params(),
    )(c_row, w_ada, b_cols, *carry.ins)
    return res[:2], res[2:]


def _all_gather_small(v):
    n = v.shape[1]

    def body(v_ref, out_ref, send_sems, recv_sems):
        x, y, c = _mesh_pos()
        me = 4 * x + 2 * y + c
        out_ref[me] = v_ref[...]
        sends = []
        for r in range(1, N_DEV):
            px, py, pc = _peer(x, y, c, r)
            cp = pltpu.make_async_remote_copy(
                src_ref=v_ref, dst_ref=out_ref.at[me], send_sem=send_sems.at[r - 1],
                recv_sem=recv_sems.at[r - 1], device_id=(px, py, pc), device_id_type=MESH)
            cp.start()
            sends.append(cp)
        for r in range(1, N_DEV):
            px, py, pc = _peer(x, y, c, r)
            pid = 4 * px + 2 * py + pc
            pltpu.make_async_remote_copy(
                src_ref=v_ref, dst_ref=out_ref.at[pid], send_sem=send_sems.at[r - 1],
                recv_sem=recv_sems.at[r - 1], device_id=(px, py, pc), device_id_type=MESH).wait_recv()
        for cp in sends:
            cp.wait_send()

    return pl.pallas_call(
        body, name="all_gather_small",
        out_shape=jax.ShapeDtypeStruct((N_DEV, 1, n), F32),
        in_specs=[VMEM_SPEC], out_specs=VMEM_SPEC,
        scratch_shapes=[pltpu.SemaphoreType.DMA((N_DEV - 1,)), pltpu.SemaphoreType.DMA((N_DEV - 1,))],
        compiler_params=_params(),
    )(v)


def _mm_nt(a, b, name, out_dtype, bias=None, carry=None):
    m, k = a.shape
    n = b.shape[0]
    tm = _pick(m, (512, 256, 128))
    tn = _pick(n, (1408, 1152, 1024, 768, 512, 256, 128))

    def body(*refs):
        acc = _dot_nt(refs[0][...], refs[1][...])
        if bias is not None:
            acc = acc + refs[2][...]
        refs[-1][...] = acc.astype(out_dtype)

    in_specs = [pl.BlockSpec((tm, k), lambda j, i: (i, 0)), pl.BlockSpec((tn, k), lambda j, i: (j, 0))]
    args = [a, b]
    if bias is not None:
        in_specs.append(pl.BlockSpec((1, tn), lambda j, i: (0, j)))
        args.append(bias)
    return _call(body, name=name, grid=(n // tn, m // tm), in_specs=in_specs,
                 out_specs=pl.BlockSpec((tm, tn), lambda j, i: (i, j)),
                 out_shape=jax.ShapeDtypeStruct((m, n), out_dtype), args=args,
                 sem=("parallel", "parallel"), carry=carry)


class _Tail:
    def __init__(self, rows, vecs, outs, fn):
        self.rows, self.vecs, self.outs, self.fn = list(rows), list(vecs), list(outs), fn


def _mm_nn(pairs, name, out_dtype, bias=None, carry=None, tail=None):
    m, k = pairs[0][0].shape
    n = pairs[0][1].shape[1]
    n_p = len(pairs)
    tm = _pick(m, (512, 256, 128))
    tk = k if n_p == 1 else _pick(k, (1408, 1152, 1024, 768, 512, 256, 128))
    nk = k // tk
    n_b = 0 if bias is None else 1
    n_r, n_v = (len(tail.rows), len(tail.vecs)) if tail else (0, 0)
    n_in = 2 * n_p + n_b + n_r + n_v
    n_main = 0 if out_dtype is None else 1

    def finish(acc, refs, first_tile):
        if bias is not None:
            acc = acc + refs[2 * n_p][...]
        outs = refs[n_in:-1]
        if n_main:
            outs[0][...] = acc.astype(out_dtype)
        if tail is None:
            return
        rows = [r[...] for r in refs[2 * n_p + n_b:2 * n_p + n_b + n_r]]
        vecs = [v[...] for v in refs[2 * n_p + n_b + n_r:n_in]]
        vals = tail.fn(acc, rows, vecs)
        for ref, val, (dtype, kind) in zip(outs[n_main:], vals, tail.outs):
            if kind == "row":
                ref[...] = val.astype(dtype)
            else:
                @pl.when(first_tile)
                def _(ref=ref):
                    ref[...] = jnp.zeros_like(ref)

                ref[...] += val

    def body(*refs):
        acc_ref = refs[-1]
        kk, i = pl.program_id(0), pl.program_id(1)
        part = _dot(refs[0][...], refs[1][...])
        for p in range(1, n_p):
            part = part + _dot(refs[2 * p][...], refs[2 * p + 1][...])
        if nk == 1:
            finish(part, refs, i == 0)
            return
        rows = pl.ds(pl.multiple_of(i * tm, tm), tm)

        @pl.when(kk == 0)
        def _():
            acc_ref[rows, :] = part

        if nk > 2:
            @pl.when((kk > 0) & (kk < nk - 1))
            def _():
                acc_ref[rows, :] += part

        @pl.when(kk == nk - 1)
        def _():
            finish(acc_ref[rows, :] + part, refs, i == 0)

    def last_only(kk, i):
        return (jnp.where(kk == nk - 1, i, 0), 0)

    row_spec = pl.BlockSpec((tm, n), last_only)
    vec_spec = pl.BlockSpec((1, n), lambda kk, i: (0, 0))
    in_specs, args = [], []
    for a, b in pairs:
        in_specs += [pl.BlockSpec((tm, tk), lambda kk, i: (i, kk)), pl.BlockSpec((tk, n), lambda kk, i: (kk, 0))]
        args += [a, b]
    if bias is not None:
        in_specs.append(vec_spec)
        args.append(bias)
    out_specs = [row_spec] * n_main
    out_shape = [jax.ShapeDtypeStruct((m, n), out_dtype)] if n_main else []
    if tail:
        in_specs += [row_spec] * n_r + [vec_spec] * n_v
        args += tail.rows + tail.vecs
        for dtype, kind in tail.outs:
            if kind == "row":
                out_specs.append(row_spec)
                out_shape.append(jax.ShapeDtypeStruct((m, n), dtype))
            else:
                width = n if kind == "sum" else 1
                out_specs.append(pl.BlockSpec((1, width), lambda kk, i: (0, 0)))
                out_shape.append(jax.ShapeDtypeStruct((1, width), dtype))
    if tail is None:
        out_specs, out_shape = out_specs[0], out_shape[0]
    return _call(body, name=name, grid=(nk, m // tm), in_specs=in_specs, out_specs=out_specs,
                 out_shape=out_shape, args=args,
                 scratch=[pltpu.VMEM((m, n) if nk > 1 else (8, LANES), F32)],
                 sem=("arbitrary", "arbitrary"), carry=carry)


def _rms(v):
    return lax.rsqrt(jnp.mean(v * v, axis=-1, keepdims=True) + EPS)


def _col(v):
    return jnp.sum(v, axis=0, keepdims=True)


def _tail_post_pre(x, g_post, gate, weight, g_pre, scale, shift):
    def fn(y, rows, vecs):
        (xv,), (gp, gt, g, sc, sh) = rows, vecs
        xo = xv + (weight * gt) * ((y * _rms(y)) * gp)
        return xo, ((xo * _rms(xo)) * g) * (1.0 + sc) + sh

    return _Tail([x], [g_post, gate, g_pre, scale, shift], [(F32, "row"), (BF16, "row")], fn)


def _tail_post_loss(x, target, g, gate, weight):
    def fn(y, rows, vecs):
        (xv, tv), (gv, gt) = rows, vecs
        r = _rms(y)
        yn = y * r
        err = (xv + (weight * gt) * (yn * gv)) - tv
        do = err * (1.0 / y.shape[1])
        dyn = do * ((weight * gt) * gv)
        dy = r * (dyn - yn * jnp.mean(dyn * yn, axis=-1, keepdims=True))
        return do, dy, 0.5 * _col(jnp.mean(err * err, axis=-1, keepdims=True)), _col(do * yn)

    return _Tail([x, target], [g, gate], [(F32, "row"), (BF16, "row"), (F32, "one"), (F32, "sum")], fn)


def _tail_pre_bwd(x, dres, g_pre, scale):
    def fn(dh, rows, vecs):
        (xv, dr), (g, sc) = rows, vecs
        r = _rms(xv)
        n = xv * r
        dn = dh * (g * (1.0 + sc))
        return dr + r * (dn - n * jnp.mean(dn * n, axis=-1, keepdims=True)), _col(dh * n), _col(dh)

    return _Tail([x, dres], [g_pre, scale], [(F32, "row"), (F32, "sum"), (F32, "sum")], fn)


def _tail_pre_post_bwd(x, dres, y, g_pre, scale, g_post, gate, weight):
    def fn(dh, rows, vecs):
        (xv, dr, yv), (g, sc, gp, gt) = rows, vecs
        r = _rms(xv)
        n = xv * r
        dn = dh * (g * (1.0 + sc))
        dx = dr + r * (dn - n * jnp.mean(dn * n, axis=-1, keepdims=True))
        ry = _rms(yv)
        yn = yv * ry
        dyn = dx * ((weight * gt) * gp)
        dy = ry * (dyn - yn * jnp.mean(dyn * yn, axis=-1, keepdims=True))
        return dx, dy, _col(dh * n), _col(dh), _col(dx * yn), _col(dy)

    return _Tail([x, dres, y], [g_pre, scale, g_post, gate],
                 [(F32, "row"), (BF16, "row")] + [(F32, "sum")] * 4, fn)


def _mm_tn(a, b, name, out_dtype=BF16, carry=None):
    k, m = a.shape
    n = b.shape[1]
    tm = _pick(m, (1408, 1152, 1024, 768, 512, 256, 128))
    tk = _pick(k, (512, 256, 128))
    nk = k // tk

    def body(a_ref, b_ref, o_ref, acc_ref):
        kk = pl.program_id(1)

        @pl.when(kk == 0)
        def _():
            acc_ref[...] = jnp.zeros_like(acc_ref)

        acc_ref[...] += _dot_tn(a_ref[...], b_ref[...])

        @pl.when(kk == nk - 1)
        def _():
            o_ref[...] = acc_ref[...].astype(out_dtype)

    return _call(body, name=name, grid=(m // tm, nk),
                 in_specs=[pl.BlockSpec((tk, tm), lambda i, kk: (kk, i)), pl.BlockSpec((tk, n), lambda i, kk: (kk, 0))],
                 out_specs=pl.BlockSpec((tm, n), lambda i, kk: (i, 0)),
                 out_shape=jax.ShapeDtypeStruct((m, n), out_dtype), args=[a, b],
                 scratch=[pltpu.VMEM((tm, n), F32)], sem=("parallel", "arbitrary"), carry=carry)


def _mm_tn_pair(a, b, name, carry=None):
    k, m = a.shape
    n = b.shape[1]
    rows = m // N_DEV
    n_chip = N_DEV // 2
    tm = 4 * rows
    tk = _pick(k, (1024, 512, 256, 128))
    nk = k // tk

    def body(a_ref, b_ref, p_ref, own_ref, acc_ref, keep_ref, send_ref, land_ref, send_sems, recv_sems):
        i, kk = pl.program_id(0), pl.program_id(1)
        x, y, c = _mesh_pos()

        def push(chip):
            return pltpu.make_async_remote_copy(
                src_ref=send_ref.at[chip], dst_ref=land_ref.at[chip], send_sem=send_sems.at[chip],
                recv_sem=recv_sems.at[chip], device_id=(x, y, 1 - c), device_id_type=MESH)

        if nk == 1:
            acc = _dot_tn(a_ref[...], b_ref[...])
        else:
            @pl.when(kk == 0)
            def _():
                acc_ref[...] = jnp.zeros_like(acc_ref)

            acc_ref[...] += _dot_tn(a_ref[...], b_ref[...])
            acc = acc_ref

        for t in range(2):
            @pl.when((kk == nk - 1) & (i == t))
            def _(t=t):
                for ob in range(4):
                    chip, core = 2 * t + ob // 2, ob % 2
                    blk = acc[ob * rows:(ob + 1) * rows, :]

                    @pl.when(c == core)
                    def _(chip=chip, blk=blk):
                        keep_ref[chip] = blk

                    @pl.when(c != core)
                    def _(chip=chip, blk=blk):
                        send_ref[chip] = blk.astype(BF16)
                        push(chip).start()

        @pl.when((kk == nk - 1) & (i == 1))
        def _():
            for chip in range(n_chip):
                push(chip).wait_recv()
                val = (keep_ref[chip] + land_ref[chip].astype(F32)).astype(BF16)
                p_ref[chip * rows:(chip + 1) * rows, :] = val

                @pl.when(2 * x + y == chip)
                def _(val=val):
                    own_ref[...] = val

            for chip in range(n_chip):
                push(chip).wait_send()

    return _call(body, name=name, grid=(2, nk),
                 in_specs=[pl.BlockSpec((tk, tm), lambda i, kk: (kk, i)), pl.BlockSpec((tk, n), lambda i, kk: (kk, 0))],
                 out_specs=(pl.BlockSpec((n_chip * rows, n), lambda i, kk: (0, 0)),
                            pl.BlockSpec((rows, n), lambda i, kk: (0, 0))),
                 out_shape=(jax.ShapeDtypeStruct((n_chip * rows, n), BF16), jax.ShapeDtypeStruct((rows, n), BF16)),
                 args=[a, b],
                 scratch=[pltpu.VMEM((tm, n) if nk > 1 else (8, LANES), F32), pltpu.VMEM((n_chip, rows, n), F32),
                          pltpu.VMEM((n_chip, rows, n), BF16), pltpu.VMEM((n_chip, rows, n), BF16),
                          pltpu.SemaphoreType.DMA((n_chip,)), pltpu.SemaphoreType.DMA((n_chip,))],
                 sem=("arbitrary", "arbitrary"), carry=carry)


def _ffn_up(h, wg_t, wu_t, name, carry=None):
    s, d = h.shape
    f = wg_t.shape[0]
    tm = _pick(s, (512, 256, 128))
    tf = _pick(f, (1408, 1024, 512, 256, 128))

    def body(h_ref, wg_ref, wu_ref, a_ref, b_ref, u_ref):
        hh = h_ref[...]
        for lo, hi in _pieces(tf):
            a = _dot_nt(hh, wg_ref[lo:hi, :])
            b = _dot_nt(hh, wu_ref[lo:hi, :])
            a_ref[:, lo:hi] = a.astype(BF16)
            b_ref[:, lo:hi] = b.astype(BF16)
            u_ref[:, lo:hi] = ((a * _sigmoid(a)) * b).astype(BF16)

    w_spec = pl.BlockSpec((tf, d), lambda j, i: (j, 0))
    o_spec = pl.BlockSpec((tm, tf), lambda j, i: (i, j))
    o_shape = jax.ShapeDtypeStruct((s, f), BF16)
    return _call(body, name=name, grid=(f // tf, s // tm),
                 in_specs=[pl.BlockSpec((tm, d), lambda j, i: (i, 0)), w_spec, w_spec],
                 out_specs=(o_spec, o_spec, o_spec), out_shape=(o_shape, o_shape, o_shape),
                 args=[h, wg_t, wu_t], sem=("parallel", "parallel"), carry=carry)


def _ffn_down_bwd(dy, wd, a, b, name, carry=None):
    s, d = dy.shape
    f = wd.shape[0]
    tm = _pick(s, (512, 256, 128))
    tf = _pick(f, (1408, 1024, 512, 256, 128))

    def body(dy_ref, wd_ref, a_ref, b_ref, da_ref, db_ref):
        dyv = dy_ref[...]
        for lo, hi in _pieces(tf):
            du = _dot_nt(dyv, wd_ref[lo:hi, :])
            a = a_ref[:, lo:hi].astype(F32)
            b = b_ref[:, lo:hi].astype(F32)
            sig = _sigmoid(a)
            da_ref[:, lo:hi] = (du * b * (sig * (1.0 + a * (1.0 - sig)))).astype(BF16)
            db_ref[:, lo:hi] = (du * (a * sig)).astype(BF16)

    t_spec = pl.BlockSpec((tm, tf), lambda j, i: (i, j))
    o_shape = jax.ShapeDtypeStruct((s, f), BF16)
    return _call(body, name=name, grid=(f // tf, s // tm),
                 in_specs=[pl.BlockSpec((tm, d), lambda j, i: (i, 0)), pl.BlockSpec((tf, d), lambda j, i: (j, 0)),
                           t_spec, t_spec],
                 out_specs=(t_spec, t_spec), out_shape=(o_shape, o_shape), args=[dy, wd, a, b],
                 sem=("parallel", "parallel"), carry=carry)


def _row_tile(s):
    return _pick(s, (256, 128, 64))


def _vec_spec(d):
    return pl.BlockSpec((1, d), lambda i: (0, 0))


def _pre_norm(x, g, scale, shift, name):
    s, d = x.shape
    ts = _row_tile(s)

    def body(x_ref, g_ref, sc_ref, sh_ref, h_ref):
        xv = x_ref[...]
        r = lax.rsqrt(jnp.mean(xv * xv, axis=-1, keepdims=True) + EPS)
        h_ref[...] = (((xv * r) * g_ref[...]) * (1.0 + sc_ref[...]) + sh_ref[...]).astype(BF16)

    row = pl.BlockSpec((ts, d), lambda i: (i, 0))
    return _call(body, name=name, grid=(s // ts,), in_specs=[row, _vec_spec(d), _vec_spec(d), _vec_spec(d)],
                 out_specs=row, out_shape=jax.ShapeDtypeStruct((s, d), BF16), args=[x, g, scale, shift],
                 sem=("parallel",))


def _post_norm_residual(x, y, g, gate, weight, name):
    s, d = x.shape
    ts = _row_tile(s)

    def body(x_ref, y_ref, g_ref, gate_ref, o_ref):
        yv = y_ref[...]
        r = lax.rsqrt(jnp.mean(yv * yv, axis=-1, keepdims=True) + EPS)
        o_ref[...] = x_ref[...] + (weight * gate_ref[...]) * ((yv * r) * g_ref[...])

    row = pl.BlockSpec((ts, d), lambda i: (i, 0))
    return _call(body, name=name, grid=(s // ts,), in_specs=[row, row, _vec_spec(d), _vec_spec(d)],
                 out_specs=row, out_shape=jax.ShapeDtypeStruct((s, d), F32), args=[x, y, g, gate],
                 sem=("parallel",))


def _post_norm_bwd(dout, y, g, gate, weight, name):
    s, d = y.shape
    ts = _row_tile(s)

    def body(do_ref, y_ref, g_ref, gate_ref, dy_ref, s1_ref, cs_ref):
        @pl.when(pl.program_id(0) == 0)
        def _():
            s1_ref[...] = jnp.zeros_like(s1_ref)
            cs_ref[...] = jnp.zeros_like(cs_ref)

        yv = y_ref[...]
        do = do_ref[...]
        r = lax.rsqrt(jnp.mean(yv * yv, axis=-1, keepdims=True) + EPS)
        yn = yv * r
        dyn = do * ((weight * gate_ref[...]) * g_ref[...])
        dy = r * (dyn - yn * jnp.mean(dyn * yn, axis=-1, keepdims=True))
        dy_ref[...] = dy.astype(BF16)
        s1_ref[...] += jnp.sum(do * yn, axis=0, keepdims=True)
        cs_ref[...] += jnp.sum(dy, axis=0, keepdims=True)

    row = pl.BlockSpec((ts, d), lambda i: (i, 0))
    vec = jax.ShapeDtypeStruct((1, d), F32)
    return _call(body, name=name, grid=(s // ts,), in_specs=[row, row, _vec_spec(d), _vec_spec(d)],
                 out_specs=(row, _vec_spec(d), _vec_spec(d)),
                 out_shape=(jax.ShapeDtypeStruct((s, d), BF16), vec, vec), args=[dout, y, g, gate],
                 sem=("arbitrary",))


def _pre_norm_bwd(dh, x, g, scale, dres, name):
    s, d = x.shape
    ts = _row_tile(s)

    def body(dh_ref, x_ref, g_ref, sc_ref, dr_ref, dx_ref, s2_ref, s3_ref):
        @pl.when(pl.program_id(0) == 0)
        def _():
            s2_ref[...] = jnp.zeros_like(s2_ref)
            s3_ref[...] = jnp.zeros_like(s3_ref)

        xv = x_ref[...]
        dh = dh_ref[...]
        r = lax.rsqrt(jnp.mean(xv * xv, axis=-1, keepdims=True) + EPS)
        n = xv * r
        dn = dh * (g_ref[...] * (1.0 + sc_ref[...]))
        dx_ref[...] = dr_ref[...] + r * (dn - n * jnp.mean(dn * n, axis=-1, keepdims=True))
        s2_ref[...] += jnp.sum(dh * n, axis=0, keepdims=True)
        s3_ref[...] += jnp.sum(dh, axis=0, keepdims=True)

    row = pl.BlockSpec((ts, d), lambda i: (i, 0))
    vec = jax.ShapeDtypeStruct((1, d), F32)
    return _call(body, name=name, grid=(s // ts,), in_specs=[row, row, _vec_spec(d), _vec_spec(d), row],
                 out_specs=(row, _vec_spec(d), _vec_spec(d)),
                 out_shape=(jax.ShapeDtypeStruct((s, d), F32), vec, vec), args=[dh, x, g, scale, dres],
                 sem=("arbitrary",))


def _post_pre_norm(x, y, g_post, gate, weight, g_pre, scale, shift, name):
    s, d = x.shape
    ts = _row_tile(s)

    def body(x_ref, y_ref, gp_ref, gate_ref, g_ref, sc_ref, sh_ref, o_ref, h_ref):
        yv = y_ref[...]
        r = lax.rsqrt(jnp.mean(yv * yv, axis=-1, keepdims=True) + EPS)
        xv = x_ref[...] + (weight * gate_ref[...]) * ((yv * r) * gp_ref[...])
        o_ref[...] = xv
        r2 = lax.rsqrt(jnp.mean(xv * xv, axis=-1, keepdims=True) + EPS)
        h_ref[...] = (((xv * r2) * g_ref[...]) * (1.0 + sc_ref[...]) + sh_ref[...]).astype(BF16)

    row = pl.BlockSpec((ts, d), lambda i: (i, 0))
    return _call(body, name=name, grid=(s // ts,), in_specs=[row, row] + [_vec_spec(d)] * 5,
                 out_specs=(row, row),
                 out_shape=(jax.ShapeDtypeStruct((s, d), F32), jax.ShapeDtypeStruct((s, d), BF16)),
                 args=[x, y, g_post, gate, g_pre, scale, shift], sem=("parallel",))


def _post_norm_loss_bwd(x, y, g, gate, weight, target, name):
    s, d = y.shape
    ts = _row_tile(s)

    def body(x_ref, y_ref, g_ref, gate_ref, t_ref, dx_ref, dy_ref, l_ref, s1_ref):
        @pl.when(pl.program_id(0) == 0)
        def _():
            l_ref[...] = jnp.zeros_like(l_ref)
            s1_ref[...] = jnp.zeros_like(s1_ref)

        yv = y_ref[...]
        r = lax.rsqrt(jnp.mean(yv * yv, axis=-1, keepdims=True) + EPS)
        yn = yv * r
        err = (x_ref[...] + (weight * gate_ref[...]) * (yn * g_ref[...])) - t_ref[...]
        do = err * (1.0 / d)
        dx_ref[...] = do
        l_ref[...] += 0.5 * jnp.sum(jnp.mean(err * err, axis=-1, keepdims=True), axis=0, keepdims=True)
        dyn = do * ((weight * gate_ref[...]) * g_ref[...])
        dy_ref[...] = (r * (dyn - yn * jnp.mean(dyn * yn, axis=-1, keepdims=True))).astype(BF16)
        s1_ref[...] += jnp.sum(do * yn, axis=0, keepdims=True)

    row = pl.BlockSpec((ts, d), lambda i: (i, 0))
    return _call(body, name=name, grid=(s // ts,), in_specs=[row, row, _vec_spec(d), _vec_spec(d), row],
                 out_specs=(row, row, pl.BlockSpec((1, 1), lambda i: (0, 0)), _vec_spec(d)),
                 out_shape=(jax.ShapeDtypeStruct((s, d), F32), jax.ShapeDtypeStruct((s, d), BF16),
                            jax.ShapeDtypeStruct((1, 1), F32), jax.ShapeDtypeStruct((1, d), F32)),
                 args=[x, y, g, gate, target], sem=("arbitrary",))


def _pre_post_norm_bwd(dh, x, g_pre, scale, dres, y, g_post, gate, weight, name):
    s, d = x.shape
    ts = _row_tile(s)

    def body(dh_ref, x_ref, g_ref, sc_ref, dr_ref, y_ref, gp_ref, gate_ref,
             dx_ref, dy_ref, s2_ref, s3_ref, s1_ref, cs_ref):
        @pl.when(pl.program_id(0) == 0)
        def _():
            for ref in (s2_ref, s3_ref, s1_ref, cs_ref):
                ref[...] = jnp.zeros_like(ref)

        xv = x_ref[...]
        dh = dh_ref[...]
        r = lax.rsqrt(jnp.mean(xv * xv, axis=-1, keepdims=True) + EPS)
        n = xv * r
        dn = dh * (g_ref[...] * (1.0 + sc_ref[...]))
        dx = dr_ref[...] + r * (dn - n * jnp.mean(dn * n, axis=-1, keepdims=True))
        dx_ref[...] = dx
        s2_ref[...] += jnp.sum(dh * n, axis=0, keepdims=True)
        s3_ref[...] += jnp.sum(dh, axis=0, keepdims=True)
        yv = y_ref[...]
        ry = lax.rsqrt(jnp.mean(yv * yv, axis=-1, keepdims=True) + EPS)
        yn = yv * ry
        dyn = dx * ((weight * gate_ref[...]) * gp_ref[...])
        dy = ry * (dyn - yn * jnp.mean(dyn * yn, axis=-1, keepdims=True))
        dy_ref[...] = dy.astype(BF16)
        s1_ref[...] += jnp.sum(dx * yn, axis=0, keepdims=True)
        cs_ref[...] += jnp.sum(dy, axis=0, keepdims=True)

    row = pl.BlockSpec((ts, d), lambda i: (i, 0))
    vec = jax.ShapeDtypeStruct((1, d), F32)
    return _call(body, name=name, grid=(s // ts,),
                 in_specs=[row, row, _vec_spec(d), _vec_spec(d), row, row, _vec_spec(d), _vec_spec(d)],
                 out_specs=(row, row) + (_vec_spec(d),) * 4,
                 out_shape=(jax.ShapeDtypeStruct((s, d), F32), jax.ShapeDtypeStruct((s, d), BF16), vec, vec, vec, vec),
                 args=[dh, x, g_pre, scale, dres, y, g_post, gate], sem=("arbitrary",))


def _group_norm_cat(oa, ob, ga, gb):
    s = oa.shape[0]
    ts = _row_tile(s)

    def body(oa_ref, ob_ref, ga_ref, gb_ref, y_ref):
        for o_ref, g_ref, lo, w in ((oa_ref, ga_ref, 0, QA), (ob_ref, gb_ref, QA, QB)):
            ov = o_ref[...]
            r = lax.rsqrt(jnp.mean(ov * ov, axis=-1, keepdims=True) + EPS)
            y_ref[:, lo:lo + w] = ((ov * r) * g_ref[...]).astype(BF16)

    return _call(body, name="group_norm_cat", grid=(s // ts,),
                 in_specs=[pl.BlockSpec((ts, QA), lambda i: (i, 0)), pl.BlockSpec((ts, QB), lambda i: (i, 0)),
                           _vec_spec(QA), _vec_spec(QB)],
                 out_specs=pl.BlockSpec((ts, QA + QB), lambda i: (i, 0)),
                 out_shape=jax.ShapeDtypeStruct((s, QA + QB), BF16), args=[oa, ob, ga, gb], sem=("parallel",))


def _group_norm_bwd(dy, oa, ob, ga, gb):
    s = oa.shape[0]
    ts = _row_tile(s)

    def body(dy_ref, oa_ref, ob_ref, ga_ref, gb_ref, doa_ref, dob_ref, dga_ref, dgb_ref):
        @pl.when(pl.program_id(0) == 0)
        def _():
            dga_ref[...] = jnp.zeros_like(dga_ref)
            dgb_ref[...] = jnp.zeros_like(dgb_ref)

        for o_ref, g_ref, do_ref, dg_ref, lo, w in ((oa_ref, ga_ref, doa_ref, dga_ref, 0, QA),
                                                    (ob_ref, gb_ref, dob_ref, dgb_ref, QA, QB)):
            ov = o_ref[...]
            dyv = dy_ref[:, lo:lo + w]
            r = lax.rsqrt(jnp.mean(ov * ov, axis=-1, keepdims=True) + EPS)
            n = ov * r
            dn = dyv * g_ref[...]
            do_ref[...] = r * (dn - n * jnp.mean(dn * n, axis=-1, keepdims=True))
            dg_ref[...] += jnp.sum(dyv * n, axis=0, keepdims=True)

    ra = pl.BlockSpec((ts, QA), lambda i: (i, 0))
    rb = pl.BlockSpec((ts, QB), lambda i: (i, 0))
    return _call(body, name="group_norm_bwd", grid=(s // ts,),
                 in_specs=[pl.BlockSpec((ts, QA + QB), lambda i: (i, 0)), ra, rb, _vec_spec(QA), _vec_spec(QB)],
                 out_specs=(ra, rb, _vec_spec(QA), _vec_spec(QB)),
                 out_shape=(jax.ShapeDtypeStruct((s, QA), F32), jax.ShapeDtypeStruct((s, QB), F32),
                            jax.ShapeDtypeStruct((1, QA), F32), jax.ShapeDtypeStruct((1, QB), F32)),
                 args=[dy, oa, ob, ga, gb], sem=("arbitrary",))


def _loss_and_grad(y, target):
    s, d = y.shape
    ts = _row_tile(s)

    def body(y_ref, t_ref, l_ref, g_ref):
        @pl.when(pl.program_id(0) == 0)
        def _():
            l_ref[...] = jnp.zeros_like(l_ref)

        err = y_ref[...] - t_ref[...]
        g_ref[...] = err * (1.0 / d)
        row = jnp.mean(err * err, axis=-1, keepdims=True)
        l_ref[...] += 0.5 * jnp.sum(row, axis=0, keepdims=True)

    row = pl.BlockSpec((ts, d), lambda i: (i, 0))
    return _call(body, name="loss_and_grad", grid=(s // ts,), in_specs=[row, row],
                 out_specs=(pl.BlockSpec((1, 1), lambda i: (0, 0)), row),
                 out_shape=(jax.ShapeDtypeStruct((1, 1), F32), jax.ShapeDtypeStruct((s, d), F32)),
                 args=[y, target], sem=("arbitrary",))


def _col_sum(x, name):
    s, n = x.shape
    ts = _row_tile(s)

    def body(x_ref, o_ref):
        @pl.when(pl.program_id(0) == 0)
        def _():
            o_ref[...] = jnp.zeros_like(o_ref)

        o_ref[...] += jnp.sum(x_ref[...].astype(F32), axis=0, keepdims=True)

    return _call(body, name=name, grid=(s // ts,), in_specs=[pl.BlockSpec((ts, n), lambda i: (i, 0))],
                 out_specs=pl.BlockSpec((1, n), lambda i: (0, 0)), out_shape=jax.ShapeDtypeStruct((1, n), F32),
                 args=[x], sem=("arbitrary",))


def _n_variants(n_back):
    return -(-n_back // QG) + 1


def _alibi_bias():
    i = np.arange(QROWS)[:, None]
    j = np.arange((QG + BACK_A) * CHUNK)[None, :]
    dist = np.abs(BACK_A * CHUNK + i - j).astype(np.float32)
    dc = j // CHUNK - i // CHUNK
    valid = (dc >= 0) & (dc <= BACK_A)
    slopes = np.array([2.0 ** (-8.0 * (h + 1) / H_A) for h in range(H_A)], dtype=np.float32)
    bias = -slopes[:, None, None] * dist[None]
    out = [np.where((valid & (j >= (BACK_A - QG * v) * CHUNK))[None], bias, np.float32(NEG_INF))
           for v in range(_n_variants(BACK_A))]
    return jnp.asarray(np.stack(out).astype(np.float32))


def _rel_index_matrix():
    cc = np.arange(SKEW)
    dist = np.where(cc < SKEW - QROWS, BACK_B * CHUNK - cc, BACK_B * CHUNK + SKEW - cc)
    idx = np.clip(dist, -REL_CLIP, REL_CLIP) + REL_CLIP
    m = np.zeros((SKEW, N_REL), np.float32)
    m[cc, idx] = 1.0
    return jnp.asarray(m)


def _toeplitz_bias(vec, carry=None):
    lk = (QG + BACK_B) * CHUNK
    nv = _n_variants(BACK_B)

    def body(v_ref, o_ref):
        xv = jnp.broadcast_to(v_ref[0], (QROWS, SKEW))
        row = lax.broadcasted_iota(jnp.int32, (QROWS, SKEW), 0)
        for bit in range(QROWS.bit_length() - 1):
            xv = jnp.where((row >> bit) & 1 == 1, pltpu.roll(xv, 1 << bit, 1), xv)
        ri = lax.broadcasted_iota(jnp.int32, (QROWS, lk), 0) // CHUNK
        col = lax.broadcasted_iota(jnp.int32, (QROWS, lk), 1)
        ci = col // CHUNK
        valid = (ci - ri >= 0) & (ci - ri <= BACK_B)
        for v in range(nv):
            o_ref[v, 0] = jnp.where(valid & (col >= (BACK_B - QG * v) * CHUNK), xv[:, :lk], NEG_INF)

    return _call(body, name="toeplitz_bias", grid=(H_B,),
                 in_specs=[pl.BlockSpec((1, 1, SKEW), lambda h: (h, 0, 0))],
                 out_specs=pl.BlockSpec((nv, 1, QROWS, lk), lambda h: (0, h, 0, 0)),
                 out_shape=jax.ShapeDtypeStruct((nv, H_B, QROWS, lk), F32), args=[vec], sem=("parallel",),
                 carry=carry)


def _diagonal_sums(dbias):
    lk = dbias.shape[2]

    def body(d_ref, o_ref):
        xp = jnp.concatenate([d_ref[0], jnp.zeros((QROWS, SKEW - lk), F32)], axis=1)
        xv = xp[0:CHUNK]
        for q in range(1, QG):
            xv = xv + pltpu.roll(xp[q * CHUNK:(q + 1) * CHUNK], SKEW - q * CHUNK, 1)
        row = lax.broadcasted_iota(jnp.int32, (CHUNK, SKEW), 0)
        for bit in range(CHUNK.bit_length() - 1):
            xv = jnp.where((row >> bit) & 1 == 1, pltpu.roll(xv, SKEW - (1 << bit), 1), xv)
        o_ref[0] = jnp.sum(xv, axis=0, keepdims=True)

    return _call(body, name="diagonal_sums", grid=(H_B,),
                 in_specs=[pl.BlockSpec((1, QROWS, lk), lambda h: (h, 0, 0))],
                 out_specs=pl.BlockSpec((1, 1, SKEW), lambda h: (h, 0, 0)),
                 out_shape=jax.ShapeDtypeStruct((H_B, 1, SKEW), F32), args=[dbias], sem=("parallel",))


def _attn_common(s, n_back, gqa, q_col, k_col, v_col):
    lk = (QG + n_back) * CHUNK
    pad = n_back * CHUNK
    wide = TPS * LANES
    q_spec = pl.BlockSpec((QROWS, wide), lambda t, g: (g, q_col // TPS + t))
    if gqa:
        k_spec = pl.BlockSpec((s, LANES), lambda t, g: (0, k_col))
        v_spec = pl.BlockSpec((s, LANES), lambda t, g: (0, v_col))
    else:
        k_spec = pl.BlockSpec((s, wide), lambda t, g: (0, k_col // TPS + t))
        v_spec = pl.BlockSpec((s, wide), lambda t, g: (0, v_col // TPS + t))
    last_variant = _n_variants(n_back) - 1
    bias_spec = pl.BlockSpec((None, 2 * TPS, QROWS, lk), lambda t, g: (jnp.minimum(g, last_variant), t, 0, 0))
    tile_spec = pl.BlockSpec((QROWS, wide), lambda t, g: (g, t))
    return lk, pad, q_spec, k_spec, v_spec, bias_spec, tile_spec


def _attention_fwd(proj, bias, sinks, *, n_back, gqa, q_col, k_col, v_col, name, carry=None):
    s = proj.shape[0]
    lk, pad, q_spec, k_spec, v_spec, bias_spec, tile_spec = _attn_common(s, n_back, gqa, q_col, k_col, v_col)
    n_t, n_g = 512 // (TPS * LANES), s // QROWS
    kv_wide = LANES if gqa else TPS * LANES

    def body(*refs):
        if gqa:
            q_ref, k_ref, v_ref, bias_ref, sink_ref, o_ref, l_ref, kpad, vpad = refs
        else:
            q_ref, k_ref, v_ref, bias_ref, o_ref, l_ref, kpad, vpad = refs
        t, g = pl.program_id(0), pl.program_id(1)

        @pl.when(g == 0)
        def _():
            kpad[0:pad, :] = jnp.zeros((pad, kv_wide), BF16)
            vpad[0:pad, :] = jnp.zeros((pad, kv_wide), BF16)
            kpad[pad:, :] = k_ref[...]
            vpad[pad:, :] = v_ref[...]

        start = pl.multiple_of(g * QROWS, QROWS)
        half = lax.broadcasted_iota(jnp.int32, (QROWS, LANES), 1) // HEAD_DIM
        for tt in range(TPS):
            lanes = slice(tt * LANES, (tt + 1) * LANES)
            kv_lanes = slice(0, LANES) if gqa else lanes
            kb = kpad[pl.ds(start, lk), kv_lanes]
            vb = vpad[pl.ds(start, lk), kv_lanes]
            q = q_ref[:, lanes] * (HEAD_DIM ** -0.5)
            if gqa:
                hk = (TPS * t + tt) // 2
                q_rolled = pltpu.roll(q.astype(F32), HEAD_DIM, 1).astype(BF16)
            outs, lses = [], []
            for e in range(2):
                if gqa:
                    kv_half = hk
                    src = jnp.where(hk == e, q, q_rolled)
                else:
                    kv_half = e
                    src = q
                qm = jnp.where(half == kv_half, src, jnp.zeros_like(src))
                sc = _dot_nt(qm, kb) + bias_ref[2 * tt + e]
                m = jnp.max(sc, axis=-1, keepdims=True)
                if gqa:
                    sk = sink_ref[2 * (TPS * t + tt) + e]
                    m = jnp.maximum(m, sk)
                p = jnp.exp(sc - m)
                l = jnp.sum(p, axis=-1, keepdims=True)
                if gqa:
                    l = l + jnp.exp(sk - m)
                pn = p / l
                outs.append(_dot(pn.astype(BF16), vb))
                lses.append(m + jnp.log(l))
            if gqa:
                same = jnp.where(hk == 0, outs[0], outs[1])
                other = jnp.where(hk == 0, outs[1], outs[0])
                o_ref[:, lanes] = jnp.where(half == hk, same, pltpu.roll(other, HEAD_DIM, 1))
            else:
                o_ref[:, lanes] = jnp.where(half == 0, outs[0], outs[1])
            l_ref[:, lanes] = jnp.where(half == 0, lses[0], lses[1])

    in_specs = [q_spec, k_spec, v_spec, bias_spec] + ([SMEM_SPEC] if gqa else [])
    args = [proj, proj, proj, bias] + ([sinks] if gqa else [])
    o_shape = jax.ShapeDtypeStruct((s, 512), F32)
    return _call(body, name=name, grid=(n_t, n_g), in_specs=in_specs, out_specs=(tile_spec, tile_spec),
                 out_shape=(o_shape, o_shape), args=args,
                 scratch=[pltpu.VMEM((s + pad, kv_wide), BF16), pltpu.VMEM((s + pad, kv_wide), BF16)],
                 sem=("arbitrary", "arbitrary"), carry=carry)


def _attention_bwd(proj, bias, sinks, do, lse, *, n_back, gqa, q_col, k_col, v_col, name, carry=None):
    s = proj.shape[0]
    lk, pad, q_spec, k_spec, v_spec, bias_spec, tile_spec = _attn_common(s, n_back, gqa, q_col, k_col, v_col)
    n_t, n_g = 512 // (TPS * LANES), s // QROWS
    kv_wide = LANES if gqa else TPS * LANES

    def body(*refs):
        if gqa:
            (q_ref, k_ref, v_ref, bias_ref, sink_ref, do_ref, l_ref,
             dq_ref, dk_ref, dv_ref, dsink_ref, kpad, vpad, dkpad, dvpad) = refs
        else:
            (q_ref, k_ref, v_ref, bias_ref, do_ref, l_ref,
             dq_ref, dk_ref, dv_ref, dbias_ref, kpad, vpad, dkpad, dvpad) = refs
        t, g = pl.program_id(0), pl.program_id(1)

        @pl.when(g == 0)
        def _():
            kpad[0:pad, :] = jnp.zeros((pad, kv_wide), BF16)
            vpad[0:pad, :] = jnp.zeros((pad, kv_wide), BF16)
            kpad[pad:, :] = k_ref[...]
            vpad[pad:, :] = v_ref[...]
            if gqa:
                dsink_ref[...] = jnp.zeros_like(dsink_ref)
            else:
                dbias_ref[...] = jnp.zeros_like(dbias_ref)

        @pl.when((g == 0) & (t == 0) if gqa else g == 0)
        def _():
            dkpad[...] = jnp.zeros_like(dkpad)
            dvpad[...] = jnp.zeros_like(dvpad)

        start = pl.multiple_of(g * QROWS, QROWS)
        half = lax.broadcasted_iota(jnp.int32, (QROWS, LANES), 1) // HEAD_DIM
        for tt in range(TPS):
            lanes = slice(tt * LANES, (tt + 1) * LANES)
            kv_lanes = slice(0, LANES) if gqa else lanes
            kb = kpad[pl.ds(start, lk), kv_lanes]
            vb = vpad[pl.ds(start, lk), kv_lanes]
            q = q_ref[:, lanes]
            dov = do_ref[:, lanes]
            lv = l_ref[:, lanes]
            if gqa:
                hk = (TPS * t + tt) // 2
                q_rolled = pltpu.roll(q.astype(F32), HEAD_DIM, 1).astype(BF16)
                do_rolled = pltpu.roll(dov, HEAD_DIM, 1)
            dqs = []
            dk_acc = jnp.zeros((lk, LANES), F32)
            dv_acc = jnp.zeros((lk, LANES), F32)
            for e in range(2):
                if gqa:
                    kv_half = hk
                    src = jnp.where(hk == e, q, q_rolled)
                    do_src = jnp.where(hk == e, dov, do_rolled)
                else:
                    kv_half = e
                    src = q
                    do_src = dov
                qm = jnp.where(half == kv_half, src, jnp.zeros_like(src))
                dom = jnp.where(half == kv_half, do_src, 0.0).astype(BF16)
                lcol = jnp.max(jnp.where(half == e, lv, -jnp.inf), axis=-1, keepdims=True)
                sc = _dot_nt(qm * (HEAD_DIM ** -0.5), kb) + bias_ref[2 * tt + e]
                pn = jnp.exp(sc - lcol)
                dp = _dot_nt(dom, vb)
                delta = jnp.sum(pn * dp, axis=-1, keepdims=True)
                ds = pn * (dp - delta)
                if gqa:
                    p_sink = jnp.exp(sink_ref[2 * (TPS * t + tt) + e] - lcol)
                    dsk = -jnp.sum(p_sink * delta, axis=0, keepdims=True)
                    row = 2 * tt + e
                    dsink_ref[0, row:row + 1, :] += jnp.broadcast_to(dsk, (1, LANES))
                else:
                    dbias_ref[2 * tt + e] += ds
                dsb = (ds * (HEAD_DIM ** -0.5)).astype(BF16)
                dqs.append(_dot(dsb, kb))
                dk_acc = dk_acc + _dot_tn(dsb, qm)
                dv_acc = dv_acc + _dot_tn(pn.astype(BF16), dom)
            dkpad[pl.ds(start, lk), kv_lanes] += dk_acc
            dvpad[pl.ds(start, lk), kv_lanes] += dv_acc
            if gqa:
                same = jnp.where(hk == 0, dqs[0], dqs[1])
                other = jnp.where(hk == 0, dqs[1], dqs[0])
                dq_ref[:, lanes] = jnp.where(half == hk, same, pltpu.roll(other, HEAD_DIM, 1)).astype(BF16)
            else:
                dq_ref[:, lanes] = jnp.where(half == 0, dqs[0], dqs[1]).astype(BF16)

        @pl.when((g == n_g - 1) & (t == n_t - 1) if gqa else g == n_g - 1)
        def _():
            dk_ref[...] = dkpad[pad:, :].astype(BF16)
            dv_ref[...] = dvpad[pad:, :].astype(BF16)

    in_specs = [q_spec, k_spec, v_spec, bias_spec] + ([SMEM_SPEC] if gqa else []) + [tile_spec, tile_spec]
    args = [proj, proj, proj, bias] + ([sinks] if gqa else []) + [do, lse]
    if gqa:
        kv_out = pl.BlockSpec((s, LANES), lambda t, g: (0, 0))
        kv_shape = jax.ShapeDtypeStruct((s, LANES), BF16)
        extra_spec = pl.BlockSpec((1, 8, LANES), lambda t, g: (t, 0, 0))
        extra_shape = jax.ShapeDtypeStruct((n_t, 8, LANES), F32)
    else:
        kv_out = pl.BlockSpec((s, kv_wide), lambda t, g: (0, t))
        kv_shape = jax.ShapeDtypeStruct((s, 512), BF16)
        extra_spec = pl.BlockSpec((2 * TPS, QROWS, lk), lambda t, g: (t, 0, 0))
        extra_shape = jax.ShapeDtypeStruct(bias.shape[1:], F32)
    return _call(body, name=name, grid=(n_t, n_g), in_specs=in_specs,
                 out_specs=(tile_spec, kv_out, kv_out, extra_spec),
                 out_shape=(jax.ShapeDtypeStruct((s, 512), BF16), kv_shape, kv_shape, extra_shape), args=args,
                 scratch=[pltpu.VMEM((s + pad, kv_wide), BF16), pltpu.VMEM((s + pad, kv_wide), BF16),
                          pltpu.VMEM((s + pad, kv_wide), F32), pltpu.VMEM((s + pad, kv_wide), F32)],
                 sem=("arbitrary", "arbitrary"), carry=carry)


def _sum_slots(r, name):
    n_slots, rows, k = r.shape

    def body(r_ref, o_ref):
        acc = r_ref[0].astype(F32)
        for j in range(1, n_slots):
            acc = acc + r_ref[j].astype(F32)
        o_ref[...] = acc

    return _call(body, name=name, grid=(k // LANES,),
                 in_specs=[pl.BlockSpec((n_slots, rows, LANES), lambda i: (0, 0, i))],
                 out_specs=pl.BlockSpec((rows, LANES), lambda i: (0, i)),
                 out_shape=jax.ShapeDtypeStruct((rows, k), F32), args=[r], sem=("parallel",))


def _sum_rows8(g):
    n = g.shape[2]

    def body(g_ref, o_ref):
        acc = g_ref[0]
        for j in range(1, N_DEV):
            acc = acc + g_ref[j]
        o_ref[...] = acc

    return pl.pallas_call(
        body, name="sum_small_grads", in_specs=[VMEM_SPEC], out_specs=VMEM_SPEC,
        out_shape=jax.ShapeDtypeStruct((1, n), F32), compiler_params=_params(),
    )(g)


def _ada_weight_grad(sc_t, dmod_cols):
    d = sc_t.shape[0]
    w = dmod_cols.shape[1]
    td = _pick(d, (256, 128))

    def body(sc_ref, dm_ref, o_ref):
        scv = sc_ref[...]
        dmv = dm_ref[...]
        acc = scv[:, 0:1] * dmv[0:1, :]
        for b in range(1, N_DEV):
            acc = acc + scv[:, b:b + 1] * dmv[b:b + 1, :]
        o_ref[...] = acc

    return _call(body, name="ada_weight_grad", grid=(d // td,),
                 in_specs=[pl.BlockSpec((td, N_DEV), lambda i: (i, 0)), pl.BlockSpec((N_DEV, w), lambda i: (0, 0))],
                 out_specs=pl.BlockSpec((td, w), lambda i: (i, 0)), out_shape=jax.ShapeDtypeStruct((d, w), F32),
                 args=[sc_t, dmod_cols], sem=("parallel",))


def _adamw_update(w, gv, m, v):
    nm = ADAM_B1 * m + (1.0 - ADAM_B1) * gv
    nv = ADAM_B2 * v + (1.0 - ADAM_B2) * (gv * gv)
    m_hat = nm / (1.0 - ADAM_B1 ** ADAM_STEP)
    v_hat = nv / (1.0 - ADAM_B2 ** ADAM_STEP)
    return -ADAM_LR * (m_hat / (jnp.sqrt(v_hat) + ADAM_EPS) + ADAM_WD * w), nm, nv


def _adamw(w, g, m, v, name):
    rows, cols = w.shape
    tr = _pick(rows, (256, 176, 128, 88, 64)) if rows > 256 else rows

    def body(w_ref, g_ref, m_ref, v_ref, d_ref, nm_ref, nv_ref):
        d_ref[...], nm_ref[...], nv_ref[...] = _adamw_update(w_ref[...], g_ref[...], m_ref[...], v_ref[...])

    spec = pl.BlockSpec((tr, cols), lambda i: (i, 0))
    shape = jax.ShapeDtypeStruct((rows, cols), F32)
    return _call(body, name=name, grid=(rows // tr,), in_specs=[spec] * 4, out_specs=(spec, spec, spec),
                 out_shape=(shape, shape, shape), args=[w, g, m, v], sem=("parallel",))


def _adamw_from_slots(w, own, slots, m, v, name):
    n_slots, rows, k = slots.shape

    def body(o_ref, s_ref, w_ref, m_ref, v_ref, g_ref, d_ref, nm_ref, nv_ref):
        gv = o_ref[...].astype(F32)
        for j in range(n_slots):
            gv = gv + s_ref[j].astype(F32)
        g_ref[...] = gv
        d_ref[...], nm_ref[...], nv_ref[...] = _adamw_update(w_ref[...], gv, m_ref[...], v_ref[...])

    tr = rows // 2 if rows % 32 == 0 else rows
    spec = pl.BlockSpec((tr, k), lambda i: (i, 0))
    shape = jax.ShapeDtypeStruct((rows, k), F32)
    return _call(body, name=name, grid=(rows // tr,),
                 in_specs=[spec, pl.BlockSpec((n_slots, tr, k), lambda i: (0, i, 0)), spec, spec, spec],
                 out_specs=(spec, spec, spec, spec), out_shape=(shape, shape, shape, shape),
                 args=[own, slots, w, m, v], sem=("parallel",))


def _adamw_small(g, w, m, v, sizes):
    n = w.shape[1]
    offs, off = [], 0
    for size in sizes:
        offs.append(off)
        off += size + (-size % LANES)

    def body(g_ref, w_ref, m_ref, v_ref, *out_refs):
        gv = g_ref[:, 0:n]
        dv, nm, nv = _adamw_update(w_ref[...], gv, m_ref[...], v_ref[...])
        for j, (o, size) in enumerate(zip(offs, sizes)):
            for k, val in enumerate((gv, dv, nm, nv)):
                out_refs[4 * j + k][...] = val[:, o:o + size]

    shapes = [jax.ShapeDtypeStruct((1, size), F32) for size in sizes for _ in range(4)]
    return pl.pallas_call(
        body, name="adamw_small", in_specs=[VMEM_SPEC] * 4, out_specs=tuple([VMEM_SPEC] * len(shapes)),
        out_shape=tuple(shapes), compiler_params=_params(),
    )(g, w, m, v)


SMALL = ("b_ada", "g_pre_ffn1", "g_post_ffn1", "g_pre_mix", "b_in", "sinks_a", "rel_bias_b", "g_grp_a",
         "g_grp_b", "b_out", "g_post_mix", "g_pre_ffn2", "g_post_ffn2")
WEIGHTS = ("w_ada", "b_ada", "g_pre_ffn1", "w_gate1", "w_up1", "w_down1", "g_post_ffn1", "g_pre_mix", "w_in",
           "b_in", "sinks_a", "rel_bias_b", "g_grp_a", "g_grp_b", "w_out", "b_out", "g_post_mix", "g_pre_ffn2",
           "w_gate2", "w_up2", "w_down2", "g_post_ffn2")


def kernel(x, c, w_ada, b_ada, g_pre_ffn1, w_gate1, w_up1, w_down1, g_post_ffn1, g_pre_mix, w_in, b_in, sinks_a, rel_bias_b, g_grp_a, g_grp_b, w_out, b_out, g_post_mix, g_pre_ffn2, w_gate2, w_up2, w_down2, g_post_ffn2, loss_target, m_w_ada, m_b_ada, m_g_pre_ffn1, m_w_gate1, m_w_up1, m_w_down1, m_g_post_ffn1, m_g_pre_mix, m_w_in, m_b_in, m_sinks_a, m_rel_bias_b, m_g_grp_a, m_g_grp_b, m_w_out, m_b_out, m_g_post_mix, m_g_pre_ffn2, m_w_gate2, m_w_up2, m_w_down2, m_g_post_ffn2, v_w_ada, v_b_ada, v_g_pre_ffn1, v_w_gate1, v_w_up1, v_w_down1, v_g_post_ffn1, v_g_pre_mix, v_w_in, v_b_in, v_sinks_a, v_rel_bias_b, v_g_grp_a, v_g_grp_b, v_w_out, v_b_out, v_g_post_mix, v_g_pre_ffn2, v_w_gate2, v_w_up2, v_w_down2, v_g_post_ffn2):
    given = dict(locals())
    weights = {n: given[n] for n in WEIGHTS}
    mom_m = {n: given["m_" + n] for n in WEIGHTS}
    mom_v = {n: given["v_" + n] for n in WEIGHTS}

    me = 4 * lax.axis_index("x") + 2 * lax.axis_index("y") + lax.axis_index("c")
    xs = x[0]
    tgt = loss_target[0]
    d_model = xs.shape[1]
    ada_cols = w_ada.shape[2]

    sh = {"wg1": w_gate1[0].T, "wu1": w_up1[0].T, "wd1": w_down1[0], "win": w_in[0].T, "wo": w_out[0],
          "wg2": w_gate2[0].T, "wu2": w_up2[0].T, "wd2": w_down2[0]}
    sh = {k: v.astype(BF16) for k, v in sh.items()}

    order = ("wg1", "wu1", "wd1", "win", "wo", "wg2", "wu2", "wd2")
    g_sems, g_shards, g_arrays, g_token = _gather_start([sh[n] for n in order], "gather_start")

    def landed(names, after, tag):
        idx = [order.index(n) for n in names]
        shards, arrays = _gather_wait([g_sems[i] for i in idx], [g_shards[i] for i in idx],
                                      [g_arrays[i] for i in idx], after, "gather_wait_" + tag)
        return _forward_carry(arrays, shards)

    bias_a = _alibi_bias()
    rel_m = _rel_index_matrix()
    rel_vec = jnp.dot(rel_bias_b[0], rel_m.T, precision=lax.Precision.HIGHEST)
    bias_b = _toeplitz_bias((rel_vec + g_token[0:1, 0:1]).reshape(H_B, 1, SKEW))

    b_cols = lax.dynamic_slice(b_ada, (0, me * ada_cols), (1, ada_cols))
    (sc_all, mod_rows), _ = _ada_forward(c, w_ada[0], b_cols, _Carry([], [], [], lambda *a: None, lambda *a: None))
    mod = mod_rows.reshape(N_MOD, d_model)
    shift1, scale1, gate1, shift2, scale2, gate2, shift3, scale3, gate3 = (mod[i:i + 1] for i in range(N_MOD))

    h1 = _pre_norm(xs, g_pre_ffn1, scale1, shift1, "pre_norm_ffn1")
    wg1, wu1 = _run_carry(landed(("wg1", "wu1"), h1, "ffn1_up"), "gather_pass_ffn1_up")
    a1, b1, u1 = _ffn_up(h1, wg1, wu1, "ffn_up_ffn1")
    (wd1,) = _run_carry(landed(("wd1",), u1, "ffn1_down"), "gather_pass_ffn1_down")
    (y1, x1, h2), (win, wo) = _mm_nn(
        [(u1, wd1)], "ffn_down_ffn1", F32, carry=landed(("win", "wo"), wd1, "mix"),
        tail=_tail_post_pre(xs, g_post_ffn1, gate1, 0.5, g_pre_mix, scale2, shift2))

    proj = _mm_nt(h2, win, "in_proj", BF16, bias=b_in)
    sinks = sinks_a[0]
    cfg_a = dict(n_back=BACK_A, gqa=True, q_col=0, k_col=QA // LANES, v_col=(QA + KVA) // LANES)
    cfg_b = dict(n_back=BACK_B, gqa=False, q_col=(QA + 2 * KVA) // LANES, k_col=(QA + 2 * KVA + QB) // LANES,
                 v_col=(QA + 2 * KVA + 2 * QB) // LANES)
    oa, lse_a = _attention_fwd(proj, bias_a, sinks, name="attn_a", **cfg_a)
    (ob, lse_b), (wg2, wu2) = _attention_fwd(proj, bias_b, None, name="attn_b",
                                             carry=landed(("wg2", "wu2"), oa, "ffn2_up"), **cfg_b)
    ycat = _group_norm_cat(oa, ob, g_grp_a, g_grp_b)
    (ymix, x2, h3), (wd2,) = _mm_nn(
        [(ycat, wo)], "out_proj", F32, bias=b_out, carry=landed(("wd2",), ycat, "ffn2_down"),
        tail=_tail_post_pre(x1, g_post_mix, gate2, 1.0, g_pre_ffn2, scale3, shift3))

    a3, b3, u3 = _ffn_up(h3, wg2, wu2, "ffn_up_ffn2")

    flights, own = {}, {}

    def grad_pair(key, a_mat, b_mat, name):
        part, own[key] = _mm_tn_pair(a_mat, b_mat, name)
        return part

    def scatter_start(tag, after_vec, **parts):
        names = list(parts)
        sems, p_thru, lands, token = _scatter_start([parts[n] for n in names], "scatter_start_" + tag)
        flights[tag] = (names, sems, p_thru, lands)
        return after_vec + token[0:1, 0:1]

    dx3, dy, loss_part, s1 = _mm_nn([(u3, wd2)], "ffn_down_ffn2", None,
                                    tail=_tail_post_loss(x2, tgt, g_post_ffn2, gate3, 0.5))
    da, db = _ffn_down_bwd(dy, wd2, a3, b3, "ffn_down_bwd_ffn2")
    dwd2 = grad_pair("wd2", u3, dy, "grad_wd_ffn2")
    dwg2 = grad_pair("wg2", da, h3, "grad_wg_ffn2")
    dwu2 = grad_pair("wu2", db, h3, "grad_wu_ffn2")
    g_pre_tied = scatter_start("ffn2", g_pre_ffn2, wd2=dwd2, wg2=dwg2, wu2=dwu2)
    dx2, dymix, s2, s3, s1m, db_out = _mm_nn(
        [(da, wg2), (db, wu2)], "ffn_up_bwd_ffn2", None,
        tail=_tail_pre_post_bwd(x2, dx3, ymix, g_pre_tied, scale3, g_post_mix, gate2, 1.0))
    sm3 = dict(shift=s3, scale=s2 * g_pre_ffn2, gate=0.5 * g_post_ffn2 * s1,
               g_pre=(1.0 + scale3) * s2, g_post=(0.5 * gate3) * s1)

    dycat = _mm_nt(dymix, wo, "out_proj_bwd", F32)
    dwo = grad_pair("wo", ycat, dymix, "grad_wo")
    doa, dob, dg_a, dg_b = _group_norm_bwd(dycat, oa, ob, g_grp_a, g_grp_b)
    dqa, dka, dva, dsink = _attention_bwd(proj, bias_a, sinks, doa, lse_a, name="attn_a_bwd", **cfg_a)
    dqb, dkb, dvb, dbias = _attention_bwd(proj, bias_b, None, dob, lse_b, name="attn_b_bwd", **cfg_b)
    dproj = jnp.concatenate([dqa, dka, dva, dqb, dkb, dvb], axis=1)
    db_in = _col_sum(dproj, "grad_b_in")
    dwin = grad_pair("win", dproj, h2, "grad_win")
    g_pre_tied = scatter_start("mix", g_pre_mix, wo=dwo, win=dwin)
    dx1, dy, s2m, s3m, s1, _ = _mm_nn(
        [(dproj, win)], "in_proj_bwd", None,
        tail=_tail_pre_post_bwd(x1, dx2, y1, g_pre_tied, scale2, g_post_ffn1, gate1, 0.5))
    d_rel = jnp.dot(_diagonal_sums(dbias).reshape(H_B, SKEW), rel_m, precision=lax.Precision.HIGHEST)
    d_sinks = dsink[:, :2 * TPS, 0].reshape(1, H_A)

    da, db = _ffn_down_bwd(dy, wd1, a1, b1, "ffn_down_bwd_ffn1")
    dwd1 = grad_pair("wd1", u1, dy, "grad_wd_ffn1")
    dwg1 = grad_pair("wg1", da, h1, "grad_wg_ffn1")
    dwu1 = grad_pair("wu1", db, h1, "grad_wu_ffn1")
    g_pre_tied = scatter_start("ffn1", g_pre_ffn1, wd1=dwd1, wg1=dwg1, wu1=dwu1)
    dx0, s2, s3 = _mm_nn([(da, wg1), (db, wu1)], "ffn_up_bwd_ffn1", None,
                         tail=_tail_pre_bwd(xs, dx1, g_pre_tied, scale1))
    sm1 = dict(shift=s3, scale=s2 * g_pre_ffn1, gate=0.5 * g_post_ffn1 * s1,
               g_pre=(1.0 + scale1) * s2, g_post=(0.5 * gate1) * s1)

    dmod = jnp.concatenate([sm1["shift"], sm1["scale"], sm1["gate"],
                            s3m, s2m * g_pre_mix, g_post_mix * s1m,
                            sm3["shift"], sm3["scale"], sm3["gate"]], axis=1)
    small_parts = {
        "b_ada": dmod, "g_pre_ffn1": sm1["g_pre"], "g_post_ffn1": sm1["g_post"],
        "g_pre_mix": (1.0 + scale2) * s2m, "b_in": db_in, "sinks_a": d_sinks,
        "rel_bias_b": d_rel.reshape(1, H_B * N_REL), "g_grp_a": dg_a, "g_grp_b": dg_b, "b_out": db_out,
        "g_post_mix": gate2 * s1m, "g_pre_ffn2": sm3["g_pre"], "g_post_ffn2": sm3["g_post"]}
    sizes = [small_parts[n].shape[1] for n in SMALL]

    def pack(parts):
        cells = []
        for p in parts:
            cells.append(p)
            if p.shape[1] % LANES:
                cells.append(jnp.zeros((1, -p.shape[1] % LANES), F32))
        return jnp.concatenate(cells, axis=1)

    packed = pack([small_parts[n] for n in SMALL] + [loss_part])
    n_packed = packed.shape[1]
    small_sems, packed_thru, small_land, small_token = _small_gather_start(packed)

    out_g, out_d, out_m, out_v = {}, {}, {}, {}
    groups = (("ffn2", (("w_gate2", "wg2", True), ("w_up2", "wu2", True), ("w_down2", "wd2", False))),
              ("mix", (("w_in", "win", True), ("w_out", "wo", False))),
              ("ffn1", (("w_gate1", "wg1", True), ("w_up1", "wu1", True), ("w_down1", "wd1", False))))
    after = small_token
    for tag, members in groups:
        names, sems, p_thru, lands = flights[tag]
        _, l_done = _scatter_wait(sems, p_thru, lands, after, "scatter_wait_" + tag)
        slots = dict(zip(names, l_done))
        for n, key, transposed in members:
            view = (lambda t: t.T) if transposed else (lambda t: t)
            res = _adamw_from_slots(view(weights[n][0]), own[key], slots[key], view(mom_m[n][0]),
                                    view(mom_v[n][0]), "adamw_" + n)
            out_g[n], out_d[n], out_m[n], out_v[n] = (view(t)[None] for t in res)
            after = res[3]

    packed_done, small_land = _small_gather_wait(small_sems, packed_thru, small_land, after)
    gathered = lax.dynamic_update_slice(small_land, packed_done[None], (me, 0, 0))
    small_sum = _sum_rows8(gathered)
    loss = small_sum[0, n_packed - LANES]
    dmod_cols = lax.dynamic_slice(gathered.reshape(N_DEV, n_packed), (0, me * ada_cols), (N_DEV, ada_cols))
    g_ada = _ada_weight_grad(sc_all.reshape(N_DEV, d_model).T, dmod_cols)
    d_, m_, v_ = _adamw(w_ada[0], g_ada, m_w_ada[0], v_w_ada[0], "adamw_w_ada")
    out_g["w_ada"], out_d["w_ada"], out_m["w_ada"], out_v["w_ada"] = g_ada[None], d_[None], m_[None], v_[None]

    small_out = _adamw_small(small_sum, *(pack([tree[n].reshape(1, -1) for n in SMALL])
                                          for tree in (weights, mom_m, mom_v)), sizes)
    for j, n in enumerate(SMALL):
        shape = weights[n].shape
        out_g[n], out_d[n], out_m[n], out_v[n] = (t.reshape(shape) for t in small_out[4 * j:4 * j + 4])

    return (loss, dx0[None], *[out_g[n] for n in WEIGHTS], *[out_d[n] for n in WEIGHTS],
            *[out_m[n] for n in WEIGHTS], *[out_v[n] for n in WEIGHTS])
```

```python
import numpy as np
import jax
import jax.numpy as jnp
from jax import lax
from jax.experimental import pallas as pl
from jax.experimental.pallas import tpu as pltpu

F32 = jnp.float32
BF16 = jnp.bfloat16
MESH = pl.DeviceIdType.MESH
ANY = pl.BlockSpec(memory_space=pl.ANY)
VMEM_SPEC = pl.BlockSpec(memory_space=pltpu.VMEM)
SMEM_SPEC = pl.BlockSpec(memory_space=pltpu.SMEM)

N_DEV = 8
CHUNK = 64
HEAD_DIM = 64
LANES = 128
H_A, KV_A, H_B = 8, 2, 8
BACK_A, BACK_B = 2, 8
REL_CLIP = 128
N_REL = 2 * REL_CLIP + 1
QA, KVA, QB = H_A * HEAD_DIM, KV_A * HEAD_DIM, H_B * HEAD_DIM
D_IN = QA + 2 * KVA + 3 * QB
N_MOD = 9
EPS = 1e-6
NEG_INF = -1e30
QG = 4
QROWS = QG * CHUNK
TPS = 2
SKEW = 1024
ADAM_LR, ADAM_B1, ADAM_B2, ADAM_EPS, ADAM_WD, ADAM_STEP = 0.001, 0.9, 0.999, 1e-08, 0.01, 10
VMEM_LIMIT = 56 * 2 ** 20


def _pick(n, cands):
    for c in cands:
        if n % c == 0:
            return c
    return n


def _pieces(n, width=2 * LANES):
    return [(lo, min(lo + width, n)) for lo in range(0, n, width)]


def _params(sem=None):
    return pltpu.CompilerParams(dimension_semantics=sem, vmem_limit_bytes=VMEM_LIMIT)


def _dot_nt(a, b):
    return lax.dot_general(a, b, (((1,), (1,)), ((), ())), preferred_element_type=F32)


def _dot_tn(a, b):
    return lax.dot_general(a, b, (((0,), (0,)), ((), ())), preferred_element_type=F32)


def _dot(a, b):
    return jnp.dot(a, b, preferred_element_type=F32)


def _sigmoid(a):
    return 0.5 * (jnp.tanh(0.5 * a) + 1.0)


def _mesh_pos():
    return lax.axis_index("x"), lax.axis_index("y"), lax.axis_index("c")


def _peer(x, y, c, r):
    px = 1 - x if r & 4 else x
    py = 1 - y if r & 2 else y
    pc = 1 - c if r & 1 else c
    return px, py, pc


class _Carry:
    def __init__(self, ins, out_shapes, scratch, start, finish, aliased=0):
        self.ins, self.out_shapes, self.scratch = list(ins), list(out_shapes), list(scratch)
        self.start, self.finish = start, finish
        self.aliased = aliased


def _call(body, *, name, grid, in_specs, out_specs, out_shape, args, scratch=(), sem=None, carry=None):
    single = not isinstance(out_shape, (tuple, list))
    out_specs = (out_specs,) if single else tuple(out_specs)
    out_shape = (out_shape,) if single else tuple(out_shape)
    if carry is None:
        res = pl.pallas_call(body, name=name, grid=grid, in_specs=list(in_specs), out_specs=out_specs,
                             out_shape=out_shape, scratch_shapes=list(scratch), compiler_params=_params(sem))(*args)
        return res[0] if single else res
    n_in, n_out, n_s = len(in_specs), len(out_shape), len(scratch)
    ci, co = len(carry.ins), len(carry.out_shapes)

    def wrapped(*refs):
        ins, cins = refs[:n_in], refs[n_in:n_in + ci]
        outs = refs[n_in + ci:n_in + ci + n_out]
        couts = refs[n_in + ci + n_out:n_in + ci + n_out + co]
        scr = refs[n_in + ci + n_out + co:n_in + ci + n_out + co + n_s]
        cscr = refs[n_in + ci + n_out + co + n_s:]
        first, last = None, None
        for ax, n in enumerate(grid):
            f, l = pl.program_id(ax) == 0, pl.program_id(ax) == n - 1
            first = f if first is None else first & f
            last = l if last is None else last & l
        pl.when(first)(lambda: carry.start(cins, couts, cscr))
        body(*ins, *outs, *scr)
        pl.when(last)(lambda: carry.finish(cins, couts, cscr))

    res = pl.pallas_call(
        wrapped, name=name, grid=grid, in_specs=list(in_specs) + [ANY] * ci, out_specs=out_specs + (ANY,) * co,
        out_shape=out_shape + tuple(carry.out_shapes), scratch_shapes=list(scratch) + carry.scratch,
        input_output_aliases={n_in + i: n_out + i for i in range(carry.aliased)},
        compiler_params=_params(("arbitrary",) * len(grid)))(*args, *carry.ins)
    main = res[:n_out]
    return (main[0] if single else main), res[n_out:]


def _run_carry(carry, name):
    ci, co = len(carry.ins), len(carry.out_shapes)

    def body(*refs):
        carry.start(refs[:ci], refs[ci:ci + co], refs[ci + co:])
        carry.finish(refs[:ci], refs[ci:ci + co], refs[ci + co:])

    return pl.pallas_call(body, name=name, in_specs=[ANY] * ci, out_specs=(ANY,) * co,
                          out_shape=tuple(carry.out_shapes), scratch_shapes=carry.scratch,
                          input_output_aliases={i: i for i in range(carry.aliased)},
                          compiler_params=_params())(*carry.ins)


def _gather_carry(shards):
    n_w = len(shards)
    rows = [s.shape[0] for s in shards]

    def plan(ins, outs, scr):
        send_sems, recv_sems, local_sems = scr
        x, y, c = _mesh_pos()
        me, sibling = (x, y, c), (x, y, 1 - c)
        chips = [(1 - x, y), (x, 1 - y), (1 - x, 1 - y)]

        def block(w, dev):
            start = pl.multiple_of((4 * dev[0] + 2 * dev[1] + dev[2]) * rows[w], 16)
            return outs[w].at[pl.ds(start, rows[w]), :]

        def copy(w, k, dev, to, src=None):
            return pltpu.make_async_remote_copy(
                src_ref=block(w, dev) if src is None else src, dst_ref=block(w, dev),
                send_sem=send_sems.at[w, k], recv_sem=recv_sems.at[w, k], device_id=to, device_id_type=MESH)

        mine = [pltpu.make_async_copy(ins[w], block(w, me), local_sems.at[w]) for w in range(n_w)]
        first = []
        for j, chip in enumerate(chips):
            first += [copy(w, 1 + j, me, (*chip, c), src=ins[w]) for w in range(n_w)]
        first += [copy(w, 0, me, sibling, src=ins[w]) for w in range(n_w)]
        return c, me, sibling, chips, copy, mine, first

    def start(ins, outs, scr):
        _, _, _, _, _, mine, first = plan(ins, outs, scr)
        for cp in mine + first:
            cp.start()

    def finish(ins, outs, scr):
        c, me, sibling, chips, copy, mine, first = plan(ins, outs, scr)
        passed = []
        for j, chip in enumerate(chips):
            for w in range(n_w):
                copy(w, 1 + j, (*chip, c), me).wait_recv()
                cp = copy(w, 4 + j, (*chip, c), sibling)
                cp.start()
                passed.append(cp)
        for w in range(n_w):
            copy(w, 0, sibling, me).wait_recv()
        for j, chip in enumerate(chips):
            for w in range(n_w):
                copy(w, 4 + j, (*chip, 1 - c), me).wait_recv()
        for cp in first + passed:
            cp.wait_send()
        for cp in mine:
            cp.wait()

    return _Carry(
        shards, [jax.ShapeDtypeStruct((N_DEV * s.shape[0], s.shape[1]), s.dtype) for s in shards],
        [pltpu.SemaphoreType.DMA((n_w, N_DEV - 1)), pltpu.SemaphoreType.DMA((n_w, N_DEV - 1)),
         pltpu.SemaphoreType.DMA((n_w,))], start, finish)


def _scatter_carry(parts):
    n_w = len(parts)
    n_chip = N_DEV // 2
    rows = [g.shape[0] // n_chip for g in parts]

    def plan(ins, outs, scr):
        send_sems, recv_sems, local_sems = scr
        x, y, c = _mesh_pos()

        def src(w, chip_index):
            return ins[w].at[pl.ds(pl.multiple_of(chip_index * rows[w], 16), rows[w]), :]

        mine = [pltpu.make_async_copy(src(w, 2 * x + y), outs[w].at[0], local_sems.at[w]) for w in range(n_w)]
        copies = []
        for r in (3, 2, 1):
            px, py, _ = _peer(x, y, c, 2 * r)
            for w in range(n_w):
                copies.append(pltpu.make_async_remote_copy(
                    src_ref=src(w, 2 * px + py), dst_ref=outs[w].at[r], send_sem=send_sems.at[w, r - 1],
                    recv_sem=recv_sems.at[w, r - 1], device_id=(px, py, c), device_id_type=MESH))
        return mine, copies

    def start(ins, outs, scr):
        mine, copies = plan(ins, outs, scr)
        for cp in mine + copies:
            cp.start()

    def finish(ins, outs, scr):
        mine, copies = plan(ins, outs, scr)
        for cp in copies:
            cp.wait_recv()
        for cp in copies:
            cp.wait_send()
        for cp in mine:
            cp.wait()

    return _Carry(
        parts, [jax.ShapeDtypeStruct((n_chip, r, g.shape[1]), g.dtype) for r, g in zip(rows, parts)],
        [pltpu.SemaphoreType.DMA((n_w, n_chip - 1)), pltpu.SemaphoreType.DMA((n_w, n_chip - 1)),
         pltpu.SemaphoreType.DMA((n_w,))], start, finish)


HBM_SPEC = pl.BlockSpec(memory_space=pltpu.HBM)
SEM_SPEC = pl.BlockSpec(memory_space=pltpu.SEMAPHORE)
N_CHIP = N_DEV // 2


def _scatter_copy(part_ref, land_ref, send_sem, recv_sem, r, rows):
    x, y, c = _mesh_pos()
    px, py, _ = _peer(x, y, c, 2 * r)
    src = part_ref.at[pl.ds(pl.multiple_of((2 * px + py) * rows, 16), rows), :]
    return pltpu.make_async_remote_copy(
        src_ref=src, dst_ref=land_ref.at[r - 1], send_sem=send_sem, recv_sem=recv_sem,
        device_id=(px, py, c), device_id_type=MESH)


def _scatter_order(n_w):
    return [(w, r) for r in (3, 2, 1) for w in range(n_w)]


def _scatter_start(parts, name):
    n_w = len(parts)
    rows = [p.shape[0] // N_CHIP for p in parts]
    order = _scatter_order(n_w)
    lands = [pltpu.with_memory_space_constraint(lax.empty((N_CHIP - 1, r, p.shape[1]), p.dtype), pltpu.HBM)
             for r, p in zip(rows, parts)]

    def body(*refs):
        part_refs, land_refs = refs[:n_w], refs[n_w:2 * n_w]
        sems = refs[2 * n_w:2 * n_w + 2 * len(order)]
        token = refs[-1]
        for j, (w, r) in enumerate(order):
            _scatter_copy(part_refs[w], land_refs[w], sems[2 * j], sems[2 * j + 1], r, rows[w]).start()
        token[...] = jnp.zeros_like(token)

    n_sem = 2 * len(order)
    res = pl.pallas_call(
        body, name=name,
        out_shape=(*[pltpu.SemaphoreType.DMA(())] * n_sem, *[pltpu.HBM(p.shape, p.dtype) for p in parts],
                   *[pltpu.HBM(l.shape, l.dtype) for l in lands], jax.ShapeDtypeStruct((8, LANES), F32)),
        in_specs=[HBM_SPEC] * (2 * n_w), out_specs=(*[SEM_SPEC] * n_sem, *[HBM_SPEC] * (2 * n_w), VMEM_SPEC),
        input_output_aliases={i: n_sem + i for i in range(2 * n_w)},
        compiler_params=pltpu.CompilerParams(has_side_effects=pltpu.SideEffectType.DATAFLOW_SIDE_EFFECTING),
    )(*[pltpu.with_memory_space_constraint(p, pltpu.HBM) for p in parts], *lands)
    return (list(res[:n_sem]), list(res[n_sem:n_sem + n_w]), list(res[n_sem + n_w:n_sem + 2 * n_w]), res[-1])


def _scatter_wait(sems, parts, lands, after, name):
    n_w = len(parts)
    rows = [p.shape[0] // N_CHIP for p in parts]
    order = _scatter_order(n_w)

    def body(*refs):
        part_refs, land_refs = refs[:n_w], refs[n_w:2 * n_w]
        sem_refs = refs[2 * n_w:2 * n_w + 2 * len(order)]
        for j, (w, r) in enumerate(order):
            cp = _scatter_copy(part_refs[w], land_refs[w], sem_refs[2 * j], sem_refs[2 * j + 1], r, rows[w])
            cp.wait_send()
            cp.wait_recv()

    res = pl.pallas_call(
        body, name=name,
        out_shape=(*[pltpu.HBM(p.shape, p.dtype) for p in parts], *[pltpu.HBM(l.shape, l.dtype) for l in lands]),
        in_specs=[HBM_SPEC] * (2 * n_w) + [SEM_SPEC] * len(sems) + [ANY],
        out_specs=tuple([HBM_SPEC] * (2 * n_w)),
        input_output_aliases={i: i for i in range(2 * n_w)},
        compiler_params=pltpu.CompilerParams(has_side_effects=pltpu.SideEffectType.DATAFLOW_SIDE_EFFECTING),
    )(*parts, *lands, *sems, after)
    return list(res[:n_w]), list(res[n_w:])


def _rows_of(arr_ref, rows, dev):
    start = pl.multiple_of((4 * dev[0] + 2 * dev[1] + dev[2]) * rows, 16)
    return arr_ref.at[pl.ds(start, rows), :]


def _gather_peer(k):
    x, y, c = _mesh_pos()
    return [(x, y, 1 - c), (1 - x, y, c), (x, 1 - y, c), (1 - x, 1 - y, c)][k]


def _gather_send(shard_ref, arr_ref, send_sem, recv_sem, k, rows):
    return pltpu.make_async_remote_copy(
        src_ref=shard_ref, dst_ref=_rows_of(arr_ref, rows, _mesh_pos()), send_sem=send_sem, recv_sem=recv_sem,
        device_id=_gather_peer(k), device_id_type=MESH)


def _gather_arrival(shard_ref, arr_ref, send_sem, recv_sem, k, rows):
    peer = _gather_peer(k)
    return pltpu.make_async_remote_copy(
        src_ref=shard_ref, dst_ref=_rows_of(arr_ref, rows, peer), send_sem=send_sem, recv_sem=recv_sem,
        device_id=peer, device_id_type=MESH)


GATHER_ORDER = (3, 1, 2, 0)


def _gather_start(shards, name):
    n_w = len(shards)
    rows = [s.shape[0] for s in shards]
    arrays = [pltpu.with_memory_space_constraint(lax.empty((N_DEV * s.shape[0], s.shape[1]), s.dtype), pltpu.HBM)
              for s in shards]
    n_sem = 2 * 4 * n_w

    def body(*refs):
        shard_refs, arr_refs = refs[:n_w], refs[n_w:2 * n_w]
        sems = refs[2 * n_w:2 * n_w + n_sem]
        token = refs[-1]
        for w in range(n_w):
            for k in GATHER_ORDER:
                j = 2 * (4 * w + k)
                _gather_send(shard_refs[w], arr_refs[w], sems[j], sems[j + 1], k, rows[w]).start()
        token[...] = jnp.zeros_like(token)

    res = pl.pallas_call(
        body, name=name,
        out_shape=(*[pltpu.SemaphoreType.DMA(())] * n_sem, *[pltpu.HBM(s.shape, s.dtype) for s in shards],
                   *[pltpu.HBM(a.shape, a.dtype) for a in arrays], jax.ShapeDtypeStruct((8, LANES), F32)),
        in_specs=[HBM_SPEC] * (2 * n_w), out_specs=(*[SEM_SPEC] * n_sem, *[HBM_SPEC] * (2 * n_w), VMEM_SPEC),
        input_output_aliases={i: n_sem + i for i in range(2 * n_w)},
        compiler_params=pltpu.CompilerParams(has_side_effects=pltpu.SideEffectType.DATAFLOW_SIDE_EFFECTING),
    )(*[pltpu.with_memory_space_constraint(s, pltpu.HBM) for s in shards], *arrays)
    sems = [[(res[2 * (4 * w + k)], res[2 * (4 * w + k) + 1]) for k in range(4)] for w in range(n_w)]
    return sems, list(res[n_sem:n_sem + n_w]), list(res[n_sem + n_w:n_sem + 2 * n_w]), res[-1]


def _gather_wait(sems, shards, arrays, after, name):
    n_w = len(shards)
    rows = [s.shape[0] for s in shards]
    flat = [s for per_w in sems for pair in per_w for s in pair]

    def body(*refs):
        shard_refs, arr_refs = refs[:n_w], refs[n_w:2 * n_w]
        sem_refs = refs[2 * n_w:2 * n_w + len(flat)]
        for w in range(n_w):
            for k in GATHER_ORDER:
                j = 2 * (4 * w + k)
                _gather_send(shard_refs[w], arr_refs[w], sem_refs[j], sem_refs[j + 1], k, rows[w]).wait_send()
                _gather_arrival(shard_refs[w], arr_refs[w], sem_refs[j], sem_refs[j + 1], k, rows[w]).wait_recv()

    res = pl.pallas_call(
        body, name=name,
        out_shape=(*[pltpu.HBM(s.shape, s.dtype) for s in shards], *[pltpu.HBM(a.shape, a.dtype) for a in arrays]),
        in_specs=[HBM_SPEC] * (2 * n_w) + [SEM_SPEC] * len(flat) + [ANY], out_specs=tuple([HBM_SPEC] * (2 * n_w)),
        input_output_aliases={i: i for i in range(2 * n_w)},
        compiler_params=pltpu.CompilerParams(has_side_effects=pltpu.SideEffectType.DATAFLOW_SIDE_EFFECTING),
    )(*shards, *arrays, *flat, after)
    return list(res[:n_w]), list(res[n_w:])


def _forward_carry(arrays, shards):
    n_w = len(arrays)
    rows = [s.shape[0] for s in shards]

    def plan(ins, outs, scr):
        send_sems, recv_sems, local_sems = scr
        x, y, c = _mesh_pos()
        chips = [(1 - x, y), (x, 1 - y), (1 - x, 1 - y)]
        mine = [pltpu.make_async_copy(ins[n_w + w], _rows_of(outs[w], rows[w], (x, y, c)), local_sems.at[w])
                for w in range(n_w)]

        def passed(w, j, core):
            blk = _rows_of(outs[w], rows[w], (*chips[j], core))
            return pltpu.make_async_remote_copy(
                src_ref=blk, dst_ref=blk, send_sem=send_sems.at[w, j], recv_sem=recv_sems.at[w, j],
                device_id=(x, y, 1 - c), device_id_type=MESH)

        return c, mine, passed

    def start(ins, outs, scr):
        c, mine, passed = plan(ins, outs, scr)
        for cp in mine:
            cp.start()
        for j in range(3):
            for w in range(n_w):
                passed(w, j, c).start()

    def finish(ins, outs, scr):
        c, mine, passed = plan(ins, outs, scr)
        for j in range(3):
            for w in range(n_w):
                passed(w, j, 1 - c).wait_recv()
        for j in range(3):
            for w in range(n_w):
                passed(w, j, c).wait_send()
        for cp in mine:
            cp.wait()

    return _Carry(
        list(arrays) + list(shards), [jax.ShapeDtypeStruct(a.shape, a.dtype) for a in arrays],
        [pltpu.SemaphoreType.DMA((n_w, 3)), pltpu.SemaphoreType.DMA((n_w, 3)), pltpu.SemaphoreType.DMA((n_w,))],
        start, finish, aliased=n_w)


def _small_copy(v_ref, land_ref, send_sem, recv_sem, r):
    x, y, c = _mesh_pos()
    px, py, pc = _peer(x, y, c, r)
    return pltpu.make_async_remote_copy(
        src_ref=v_ref, dst_ref=land_ref.at[4 * x + 2 * y + c], send_sem=send_sem, recv_sem=recv_sem,
        device_id=(px, py, pc), device_id_type=MESH)


def _small_gather_start(v):
    land = pltpu.with_memory_space_constraint(lax.empty((N_DEV,) + v.shape, v.dtype), pltpu.HBM)

    def body(v_ref, land_ref, *rest):
        sems, token = rest[:2 * (N_DEV - 1)], rest[-1]
        for r in range(1, N_DEV):
            _small_copy(v_ref, land_ref, sems[2 * r - 2], sems[2 * r - 1], r).start()
        token[...] = jnp.zeros_like(token)

    n_sem = 2 * (N_DEV - 1)
    res = pl.pallas_call(
        body, name="small_gather_start",
        out_shape=(*[pltpu.SemaphoreType.DMA(())] * n_sem, pltpu.HBM(v.shape, v.dtype),
                   pltpu.HBM(land.shape, land.dtype), jax.ShapeDtypeStruct((8, LANES), F32)),
        in_specs=[HBM_SPEC, HBM_SPEC], out_specs=(*[SEM_SPEC] * n_sem, HBM_SPEC, HBM_SPEC, VMEM_SPEC),
        input_output_aliases={0: n_sem, 1: n_sem + 1},
        compiler_params=pltpu.CompilerParams(has_side_effects=pltpu.SideEffectType.DATAFLOW_SIDE_EFFECTING),
    )(pltpu.with_memory_space_constraint(v, pltpu.HBM), land)
    return list(res[:n_sem]), res[n_sem], res[n_sem + 1], res[-1]


def _small_gather_wait(sems, v, land, after):
    def body(v_ref, land_ref, *rest):
        for r in range(1, N_DEV):
            cp = _small_copy(v_ref, land_ref, rest[2 * r - 2], rest[2 * r - 1], r)
            cp.wait_send()
            x, y, c = _mesh_pos()
            px, py, pc = _peer(x, y, c, r)
            pltpu.make_async_remote_copy(
                src_ref=v_ref, dst_ref=land_ref.at[4 * px + 2 * py + pc], send_sem=rest[2 * r - 2],
                recv_sem=rest[2 * r - 1], device_id=(px, py, pc), device_id_type=MESH).wait_recv()

    res = pl.pallas_call(
        body, name="small_gather_wait",
        out_shape=(pltpu.HBM(v.shape, v.dtype), pltpu.HBM(land.shape, land.dtype)),
        in_specs=[HBM_SPEC, HBM_SPEC] + [SEM_SPEC] * len(sems) + [ANY], out_specs=(HBM_SPEC, HBM_SPEC),
        input_output_aliases={0: 0, 1: 1},
        compiler_params=pltpu.CompilerParams(has_side_effects=pltpu.SideEffectType.DATAFLOW_SIDE_EFFECTING),
    )(v, land, *sems, after)
    return res[0], res[1]


def _ada_forward(c_row, w_ada, b_cols, carry):
    d = c_row.shape[1]
    wcols = w_ada.shape[1]
    ci, co = len(carry.ins), len(carry.out_shapes)

    def body(*refs):
        c_ref, w_ref, b_ref = refs[:3]
        cins = refs[3:3 + ci]
        sc_ref, mod_ref = refs[3 + ci:5 + ci]
        couts = refs[5 + ci:5 + ci + co]
        rows_ref, send_sems, recv_sems = refs[5 + ci + co:8 + ci + co]
        cscr = refs[8 + ci + co:]
        carry.start(cins, couts, cscr)
        x, y, c = _mesh_pos()
        me = 4 * x + 2 * y + c
        cv = c_ref[...]
        sc_ref[me] = cv * _sigmoid(cv)

        sends = []
        for r in range(1, N_DEV):
            px, py, pc = _peer(x, y, c, r)
            cp = pltpu.make_async_remote_copy(
                src_ref=sc_ref.at[me], dst_ref=sc_ref.at[me], send_sem=send_sems.at[0, r - 1],
                recv_sem=recv_sems.at[0, r - 1], device_id=(px, py, pc), device_id_type=MESH)
            cp.start()
            sends.append(cp)
        for r in range(1, N_DEV):
            px, py, pc = _peer(x, y, c, r)
            pid = 4 * px + 2 * py + pc
            pltpu.make_async_remote_copy(
                src_ref=sc_ref.at[pid], dst_ref=sc_ref.at[pid], send_sem=send_sems.at[0, r - 1],
                recv_sem=recv_sems.at[0, r - 1], device_id=(px, py, pc), device_id_type=MESH).wait_recv()
        for cp in sends:
            cp.wait_send()

        sc_all = jnp.concatenate([sc_ref[j] for j in range(N_DEV)], axis=0)
        rows = _dot(sc_all.astype(BF16), w_ref[...].astype(BF16)) + b_ref[...]
        for j in range(N_DEV):
            rows_ref[j] = rows[j:j + 1, :]
        mod_ref[me] = rows_ref[me]

        sends = []
        for r in range(1, N_DEV):
            px, py, pc = _peer(x, y, c, r)
            pid = 4 * px + 2 * py + pc
            cp = pltpu.make_async_remote_copy(
                src_ref=rows_ref.at[pid], dst_ref=mod_ref.at[me], send_sem=send_sems.at[1, r - 1],
                recv_sem=recv_sems.at[1, r - 1], device_id=(px, py, pc), device_id_type=MESH)
            cp.start()
            sends.append(cp)
        for r in range(1, N_DEV):
            px, py, pc = _peer(x, y, c, r)
            pid = 4 * px + 2 * py + pc
            pltpu.make_async_remote_copy(
                src_ref=rows_ref.at[pid], dst_ref=mod_ref.at[pid], send_sem=send_sems.at[1, r - 1],
                recv_sem=recv_sems.at[1, r - 1], device_id=(px, py, pc), device_id_type=MESH).wait_recv()
        for cp in sends:
            cp.wait_send()
        carry.finish(cins, couts, cscr)

    res = pl.pallas_call(
        body, name="ada_forward",
        out_shape=(jax.ShapeDtypeStruct((N_DEV, 1, d), F32), jax.ShapeDtypeStruct((N_DEV, 1, wcols), F32),
                   *carry.out_shapes),
        in_specs=[VMEM_SPEC, VMEM_SPEC, VMEM_SPEC] + [ANY] * ci, out_specs=(VMEM_SPEC, VMEM_SPEC) + (ANY,) * co,
        scratch_shapes=[pltpu.VMEM((N_DEV, 1, wcols), F32), pltpu.SemaphoreType.DMA((2, N_DEV - 1)),
                        pltpu.SemaphoreType.DMA((2, N_DEV - 1))] + carry.scratch,
        compiler_params=_params(),
    )(c_row, w_ada, b_cols, *carry.ins)
    return res[:2], res[2:]


def _all_gather_small(v):
    n = v.shape[1]

    def body(v_ref, out_ref, send_sems, recv_sems):
        x, y, c = _mesh_pos()
        me = 4 * x + 2 * y + c
        out_ref[me] = v_ref[...]
        sends = []
        for r in range(1, N_DEV):
            px, py, pc = _peer(x, y, c, r)
            cp = pltpu.make_async_remote_copy(
                src_ref=v_ref, dst_ref=out_ref.at[me], send_sem=send_sems.at[r - 1],
                recv_sem=recv_sems.at[r - 1], device_id=(px, py, pc), device_id_type=MESH)
            cp.start()
            sends.append(cp)
        for r in range(1, N_DEV):
            px, py, pc = _peer(x, y, c, r)
            pid = 4 * px + 2 * py + pc
            pltpu.make_async_remote_copy(
                src_ref=v_ref, dst_ref=out_ref.at[pid], send_sem=send_sems.at[r - 1],
                recv_sem=recv_sems.at[r - 1], device_id=(px, py, pc), device_id_type=MESH).wait_recv()
        for cp in sends:
            cp.wait_send()

    return pl.pallas_call(
        body, name="all_gather_small",
        out_shape=jax.ShapeDtypeStruct((N_DEV, 1, n), F32),
        in_specs=[VMEM_SPEC], out_specs=VMEM_SPEC,
        scratch_shapes=[pltpu.SemaphoreType.DMA((N_DEV - 1,)), pltpu.SemaphoreType.DMA((N_DEV - 1,))],
        compiler_params=_params(),
    )(v)


def _mm_nt(a, b, name, out_dtype, bias=None, carry=None):
    m, k = a.shape
    n = b.shape[0]
    tm = _pick(m, (512, 256, 128))
    tn = _pick(n, (1408, 1152, 1024, 768, 512, 256, 128))

    def body(*refs):
        acc = _dot_nt(refs[0][...], refs[1][...])
        if bias is not None:
            acc = acc + refs[2][...]
        refs[-1][...] = acc.astype(out_dtype)

    in_specs = [pl.BlockSpec((tm, k), lambda j, i: (i, 0)), pl.BlockSpec((tn, k), lambda j, i: (j, 0))]
    args = [a, b]
    if bias is not None:
        in_specs.append(pl.BlockSpec((1, tn), lambda j, i: (0, j)))
        args.append(bias)
    return _call(body, name=name, grid=(n // tn, m // tm), in_specs=in_specs,
                 out_specs=pl.BlockSpec((tm, tn), lambda j, i: (i, j)),
                 out_shape=jax.ShapeDtypeStruct((m, n), out_dtype), args=args,
                 sem=("parallel", "parallel"), carry=carry)


class _Tail:
    def __init__(self, rows, vecs, outs, fn):
        self.rows, self.vecs, self.outs, self.fn = list(rows), list(vecs), list(outs), fn


def _mm_nn(pairs, name, out_dtype, bias=None, carry=None, tail=None):
    m, k = pairs[0][0].shape
    n = pairs[0][1].shape[1]
    n_p = len(pairs)
    tm = _pick(m, (512, 256, 128))
    tk = k if n_p == 1 else _pick(k, (1408, 1152, 1024, 768, 512, 256, 128))
    nk = k // tk
    n_b = 0 if bias is None else 1
    n_r, n_v = (len(tail.rows), len(tail.vecs)) if tail else (0, 0)
    n_in = 2 * n_p + n_b + n_r + n_v
    n_main = 0 if out_dtype is None else 1

    def finish(acc, refs, first_tile):
        if bias is not None:
            acc = acc + refs[2 * n_p][...]
        outs = refs[n_in:-1]
        if n_main:
            outs[0][...] = acc.astype(out_dtype)
        if tail is None:
            return
        rows = [r[...] for r in refs[2 * n_p + n_b:2 * n_p + n_b + n_r]]
        vecs = [v[...] for v in refs[2 * n_p + n_b + n_r:n_in]]
        vals = tail.fn(acc, rows, vecs)
        for ref, val, (dtype, kind) in zip(outs[n_main:], vals, tail.outs):
            if kind == "row":
                ref[...] = val.astype(dtype)
            else:
                @pl.when(first_tile)
                def _(ref=ref):
                    ref[...] = jnp.zeros_like(ref)

                ref[...] += val

    def body(*refs):
        acc_ref = refs[-1]
        kk, i = pl.program_id(0), pl.program_id(1)
        part = _dot(refs[0][...], refs[1][...])
        for p in range(1, n_p):
            part = part + _dot(refs[2 * p][...], refs[2 * p + 1][...])
        if nk == 1:
            finish(part, refs, i == 0)
            return
        rows = pl.ds(pl.multiple_of(i * tm, tm), tm)

        @pl.when(kk == 0)
        def _():
            acc_ref[rows, :] = part

        if nk > 2:
            @pl.when((kk > 0) & (kk < nk - 1))
            def _():
                acc_ref[rows, :] += part

        @pl.when(kk == nk - 1)
        def _():
            finish(acc_ref[rows, :] + part, refs, i == 0)

    def last_only(kk, i):
        return (jnp.where(kk == nk - 1, i, 0), 0)

    row_spec = pl.BlockSpec((tm, n), last_only)
    vec_spec = pl.BlockSpec((1, n), lambda kk, i: (0, 0))
    in_specs, args = [], []
    for a, b in pairs:
        in_specs += [pl.BlockSpec((tm, tk), lambda kk, i: (i, kk)), pl.BlockSpec((tk, n), lambda kk, i: (kk, 0))]
        args += [a, b]
    if bias is not None:
        in_specs.append(vec_spec)
        args.append(bias)
    out_specs = [row_spec] * n_main
    out_shape = [jax.ShapeDtypeStruct((m, n), out_dtype)] if n_main else []
    if tail:
        in_specs += [row_spec] * n_r + [vec_spec] * n_v
        args += tail.rows + tail.vecs
        for dtype, kind in tail.outs:
            if kind == "row":
                out_specs.append(row_spec)
                out_shape.append(jax.ShapeDtypeStruct((m, n), dtype))
            else:
                width = n if kind == "sum" else 1
                out_specs.append(pl.BlockSpec((1, width), lambda kk, i: (0, 0)))
                out_shape.append(jax.ShapeDtypeStruct((1, width), dtype))
    if tail is None:
        out_specs, out_shape = out_specs[0], out_shape[0]
    return _call(body, name=name, grid=(nk, m // tm), in_specs=in_specs, out_specs=out_specs,
                 out_shape=out_shape, args=args,
                 scratch=[pltpu.VMEM((m, n) if nk > 1 else (8, LANES), F32)],
                 sem=("arbitrary", "arbitrary"), carry=carry)


def _rms(v):
    return lax.rsqrt(jnp.mean(v * v, axis=-1, keepdims=True) + EPS)


def _col(v):
    return jnp.sum(v, axis=0, keepdims=True)


def _tail_post_pre(x, g_post, gate, weight, g_pre, scale, shift):
    def fn(y, rows, vecs):
        (xv,), (gp, gt, g, sc, sh) = rows, vecs
        xo = xv + (weight * gt) * ((y * _rms(y)) * gp)
        return xo, ((xo * _rms(xo)) * g) * (1.0 + sc) + sh

    return _Tail([x], [g_post, gate, g_pre, scale, shift], [(F32, "row"), (BF16, "row")], fn)


def _tail_post_loss(x, target, g, gate, weight):
    def fn(y, rows, vecs):
        (xv, tv), (gv, gt) = rows, vecs
        r = _rms(y)
        yn = y * r
        err = (xv + (weight * gt) * (yn * gv)) - tv
        do = err * (1.0 / y.shape[1])
        dyn = do * ((weight * gt) * gv)
        dy = r * (dyn - yn * jnp.mean(dyn * yn, axis=-1, keepdims=True))
        return do, dy, 0.5 * _col(jnp.mean(err * err, axis=-1, keepdims=True)), _col(do * yn)

    return _Tail([x, target], [g, gate], [(F32, "row"), (BF16, "row"), (F32, "one"), (F32, "sum")], fn)


def _tail_pre_bwd(x, dres, g_pre, scale):
    def fn(dh, rows, vecs):
        (xv, dr), (g, sc) = rows, vecs
        r = _rms(xv)
        n = xv * r
        dn = dh * (g * (1.0 + sc))
        return dr + r * (dn - n * jnp.mean(dn * n, axis=-1, keepdims=True)), _col(dh * n), _col(dh)

    return _Tail([x, dres], [g_pre, scale], [(F32, "row"), (F32, "sum"), (F32, "sum")], fn)


def _tail_pre_post_bwd(x, dres, y, g_pre, scale, g_post, gate, weight):
    def fn(dh, rows, vecs):
        (xv, dr, yv), (g, sc, gp, gt) = rows, vecs
        r = _rms(xv)
        n = xv * r
        dn = dh * (g * (1.0 + sc))
        dx = dr + r * (dn - n * jnp.mean(dn * n, axis=-1, keepdims=True))
        ry = _rms(yv)
        yn = yv * ry
        dyn = dx * ((weight * gt) * gp)
        dy = ry * (dyn - yn * jnp.mean(dyn * yn, axis=-1, keepdims=True))
        return dx, dy, _col(dh * n), _col(dh), _col(dx * yn), _col(dy)

    return _Tail([x, dres, y], [g_pre, scale, g_post, gate],
                 [(F32, "row"), (BF16, "row")] + [(F32, "sum")] * 4, fn)


def _mm_tn(a, b, name, out_dtype=BF16, carry=None):
    k, m = a.shape
    n = b.shape[1]
    tm = _pick(m, (1408, 1152, 1024, 768, 512, 256, 128))
    tk = _pick(k, (512, 256, 128))
    nk = k // tk

    def body(a_ref, b_ref, o_ref, acc_ref):
        kk = pl.program_id(1)

        @pl.when(kk == 0)
        def _():
            acc_ref[...] = jnp.zeros_like(acc_ref)

        acc_ref[...] += _dot_tn(a_ref[...], b_ref[...])

        @pl.when(kk == nk - 1)
        def _():
            o_ref[...] = acc_ref[...].astype(out_dtype)

    return _call(body, name=name, grid=(m // tm, nk),
                 in_specs=[pl.BlockSpec((tk, tm), lambda i, kk: (kk, i)), pl.BlockSpec((tk, n), lambda i, kk: (kk, 0))],
                 out_specs=pl.BlockSpec((tm, n), lambda i, kk: (i, 0)),
                 out_shape=jax.ShapeDtypeStruct((m, n), out_dtype), args=[a, b],
                 scratch=[pltpu.VMEM((tm, n), F32)], sem=("parallel", "arbitrary"), carry=carry)


def _mm_tn_pair(a, b, name, carry=None):
    k, m = a.shape
    n = b.shape[1]
    rows = m // N_DEV
    n_chip = N_DEV // 2
    tm = 4 * rows
    tk = _pick(k, (1024, 512, 256, 128))
    nk = k // tk

    def body(a_ref, b_ref, p_ref, own_ref, acc_ref, keep_ref, send_ref, land_ref, send_sems, recv_sems):
        i, kk = pl.program_id(0), pl.program_id(1)
        x, y, c = _mesh_pos()

        def push(chip):
            return pltpu.make_async_remote_copy(
                src_ref=send_ref.at[chip], dst_ref=land_ref.at[chip], send_sem=send_sems.at[chip],
                recv_sem=recv_sems.at[chip], device_id=(x, y, 1 - c), device_id_type=MESH)

        if nk == 1:
            acc = _dot_tn(a_ref[...], b_ref[...])
        else:
            @pl.when(kk == 0)
            def _():
                acc_ref[...] = jnp.zeros_like(acc_ref)

            acc_ref[...] += _dot_tn(a_ref[...], b_ref[...])
            acc = acc_ref

        for t in range(2):
            @pl.when((kk == nk - 1) & (i == t))
            def _(t=t):
                for ob in range(4):
                    chip, core = 2 * t + ob // 2, ob % 2
                    blk = acc[ob * rows:(ob + 1) * rows, :]

                    @pl.when(c == core)
                    def _(chip=chip, blk=blk):
                        keep_ref[chip] = blk

                    @pl.when(c != core)
                    def _(chip=chip, blk=blk):
                        send_ref[chip] = blk.astype(BF16)
                        push(chip).start()

        @pl.when((kk == nk - 1) & (i == 1))
        def _():
            for chip in range(n_chip):
                push(chip).wait_recv()
                val = (keep_ref[chip] + land_ref[chip].astype(F32)).astype(BF16)
                p_ref[chip * rows:(chip + 1) * rows, :] = val

                @pl.when(2 * x + y == chip)
                def _(val=val):
                    own_ref[...] = val

            for chip in range(n_chip):
                push(chip).wait_send()

    return _call(body, name=name, grid=(2, nk),
                 in_specs=[pl.BlockSpec((tk, tm), lambda i, kk: (kk, i)), pl.BlockSpec((tk, n), lambda i, kk: (kk, 0))],
                 out_specs=(pl.BlockSpec((n_chip * rows, n), lambda i, kk: (0, 0)),
                            pl.BlockSpec((rows, n), lambda i, kk: (0, 0))),
                 out_shape=(jax.ShapeDtypeStruct((n_chip * rows, n), BF16), jax.ShapeDtypeStruct((rows, n), BF16)),
                 args=[a, b],
                 scratch=[pltpu.VMEM((tm, n) if nk > 1 else (8, LANES), F32), pltpu.VMEM((n_chip, rows, n), F32),
                          pltpu.VMEM((n_chip, rows, n), BF16), pltpu.VMEM((n_chip, rows, n), BF16),
                          pltpu.SemaphoreType.DMA((n_chip,)), pltpu.SemaphoreType.DMA((n_chip,))],
                 sem=("arbitrary", "arbitrary"), carry=carry)


def _ffn_up(h, wg_t, wu_t, name, carry=None):
    s, d = h.shape
    f = wg_t.shape[0]
    tm = _pick(s, (512, 256, 128))
    tf = _pick(f, (1408, 1024, 512, 256, 128))

    def body(h_ref, wg_ref, wu_ref, a_ref, b_ref, u_ref):
        hh = h_ref[...]
        for lo, hi in _pieces(tf):
            a = _dot_nt(hh, wg_ref[lo:hi, :])
            b = _dot_nt(hh, wu_ref[lo:hi, :])
            a_ref[:, lo:hi] = a.astype(BF16)
            b_ref[:, lo:hi] = b.astype(BF16)
            u_ref[:, lo:hi] = ((a * _sigmoid(a)) * b).astype(BF16)

    w_spec = pl.BlockSpec((tf, d), lambda j, i: (j, 0))
    o_spec = pl.BlockSpec((tm, tf), lambda j, i: (i, j))
    o_shape = jax.ShapeDtypeStruct((s, f), BF16)
    return _call(body, name=name, grid=(f // tf, s // tm),
                 in_specs=[pl.BlockSpec((tm, d), lambda j, i: (i, 0)), w_spec, w_spec],
                 out_specs=(o_spec, o_spec, o_spec), out_shape=(o_shape, o_shape, o_shape),
                 args=[h, wg_t, wu_t], sem=("parallel", "parallel"), carry=carry)


def _ffn_down_bwd(dy, wd, a, b, name, carry=None):
    s, d = dy.shape
    f = wd.shape[0]
    tm = _pick(s, (512, 256, 128))
    tf = _pick(f, (1408, 1024, 512, 256, 128))

    def body(dy_ref, wd_ref, a_ref, b_ref, da_ref, db_ref):
        dyv = dy_ref[...]
        for lo, hi in _pieces(tf):
            du = _dot_nt(dyv, wd_ref[lo:hi, :])
            a = a_ref[:, lo:hi].astype(F32)
            b = b_ref[:, lo:hi].astype(F32)
            sig = _sigmoid(a)
            da_ref[:, lo:hi] = (du * b * (sig * (1.0 + a * (1.0 - sig)))).astype(BF16)
            db_ref[:, lo:hi] = (du * (a * sig)).astype(BF16)

    t_spec = pl.BlockSpec((tm, tf), lambda j, i: (i, j))
    o_shape = jax.ShapeDtypeStruct((s, f), BF16)
    return _call(body, name=name, grid=(f // tf, s // tm),
                 in_specs=[pl.BlockSpec((tm, d), lambda j, i: (i, 0)), pl.BlockSpec((tf, d), lambda j, i: (j, 0)),
                           t_spec, t_spec],
                 out_specs=(t_spec, t_spec), out_shape=(o_shape, o_shape), args=[dy, wd, a, b],
                 sem=("parallel", "parallel"), carry=carry)


def _row_tile(s):
    return _pick(s, (256, 128, 64))


def _vec_spec(d):
    return pl.BlockSpec((1, d), lambda i: (0, 0))


def _pre_norm(x, g, scale, shift, name):
    s, d = x.shape
    ts = _row_tile(s)

    def body(x_ref, g_ref, sc_ref, sh_ref, h_ref):
        xv = x_ref[...]
        r = lax.rsqrt(jnp.mean(xv * xv, axis=-1, keepdims=True) + EPS)
        h_ref[...] = (((xv * r) * g_ref[...]) * (1.0 + sc_ref[...]) + sh_ref[...]).astype(BF16)

    row = pl.BlockSpec((ts, d), lambda i: (i, 0))
    return _call(body, name=name, grid=(s // ts,), in_specs=[row, _vec_spec(d), _vec_spec(d), _vec_spec(d)],
                 out_specs=row, out_shape=jax.ShapeDtypeStruct((s, d), BF16), args=[x, g, scale, shift],
                 sem=("parallel",))


def _post_norm_residual(x, y, g, gate, weight, name):
    s, d = x.shape
    ts = _row_tile(s)

    def body(x_ref, y_ref, g_ref, gate_ref, o_ref):
        yv = y_ref[...]
        r = lax.rsqrt(jnp.mean(yv * yv, axis=-1, keepdims=True) + EPS)
        o_ref[...] = x_ref[...] + (weight * gate_ref[...]) * ((yv * r) * g_ref[...])

    row = pl.BlockSpec((ts, d), lambda i: (i, 0))
    return _call(body, name=name, grid=(s // ts,), in_specs=[row, row, _vec_spec(d), _vec_spec(d)],
                 out_specs=row, out_shape=jax.ShapeDtypeStruct((s, d), F32), args=[x, y, g, gate],
                 sem=("parallel",))


def _post_norm_bwd(dout, y, g, gate, weight, name):
    s, d = y.shape
    ts = _row_tile(s)

    def body(do_ref, y_ref, g_ref, gate_ref, dy_ref, s1_ref, cs_ref):
        @pl.when(pl.program_id(0) == 0)
        def _():
            s1_ref[...] = jnp.zeros_like(s1_ref)
            cs_ref[...] = jnp.zeros_like(cs_ref)

        yv = y_ref[...]
        do = do_ref[...]
        r = lax.rsqrt(jnp.mean(yv * yv, axis=-1, keepdims=True) + EPS)
        yn = yv * r
        dyn = do * ((weight * gate_ref[...]) * g_ref[...])
        dy = r * (dyn - yn * jnp.mean(dyn * yn, axis=-1, keepdims=True))
        dy_ref[...] = dy.astype(BF16)
        s1_ref[...] += jnp.sum(do * yn, axis=0, keepdims=True)
        cs_ref[...] += jnp.sum(dy, axis=0, keepdims=True)

    row = pl.BlockSpec((ts, d), lambda i: (i, 0))
    vec = jax.ShapeDtypeStruct((1, d), F32)
    return _call(body, name=name, grid=(s // ts,), in_specs=[row, row, _vec_spec(d), _vec_spec(d)],
                 out_specs=(row, _vec_spec(d), _vec_spec(d)),
                 out_shape=(jax.ShapeDtypeStruct((s, d), BF16), vec, vec), args=[dout, y, g, gate],
                 sem=("arbitrary",))


def _pre_norm_bwd(dh, x, g, scale, dres, name):
    s, d = x.shape
    ts = _row_tile(s)

    def body(dh_ref, x_ref, g_ref, sc_ref, dr_ref, dx_ref, s2_ref, s3_ref):
        @pl.when(pl.program_id(0) == 0)
        def _():
            s2_ref[...] = jnp.zeros_like(s2_ref)
            s3_ref[...] = jnp.zeros_like(s3_ref)

        xv = x_ref[...]
        dh = dh_ref[...]
        r = lax.rsqrt(jnp.mean(xv * xv, axis=-1, keepdims=True) + EPS)
        n = xv * r
        dn = dh * (g_ref[...] * (1.0 + sc_ref[...]))
        dx_ref[...] = dr_ref[...] + r * (dn - n * jnp.mean(dn * n, axis=-1, keepdims=True))
        s2_ref[...] += jnp.sum(dh * n, axis=0, keepdims=True)
        s3_ref[...] += jnp.sum(dh, axis=0, keepdims=True)

    row = pl.BlockSpec((ts, d), lambda i: (i, 0))
    vec = jax.ShapeDtypeStruct((1, d), F32)
    return _call(body, name=name, grid=(s // ts,), in_specs=[row, row, _vec_spec(d), _vec_spec(d), row],
                 out_specs=(row, _vec_spec(d), _vec_spec(d)),
                 out_shape=(jax.ShapeDtypeStruct((s, d), F32), vec, vec), args=[dh, x, g, scale, dres],
                 sem=("arbitrary",))


def _post_pre_norm(x, y, g_post, gate, weight, g_pre, scale, shift, name):
    s, d = x.shape
    ts = _row_tile(s)

    def body(x_ref, y_ref, gp_ref, gate_ref, g_ref, sc_ref, sh_ref, o_ref, h_ref):
        yv = y_ref[...]
        r = lax.rsqrt(jnp.mean(yv * yv, axis=-1, keepdims=True) + EPS)
        xv = x_ref[...] + (weight * gate_ref[...]) * ((yv * r) * gp_ref[...])
        o_ref[...] = xv
        r2 = lax.rsqrt(jnp.mean(xv * xv, axis=-1, keepdims=True) + EPS)
        h_ref[...] = (((xv * r2) * g_ref[...]) * (1.0 + sc_ref[...]) + sh_ref[...]).astype(BF16)

    row = pl.BlockSpec((ts, d), lambda i: (i, 0))
    return _call(body, name=name, grid=(s // ts,), in_specs=[row, row] + [_vec_spec(d)] * 5,
                 out_specs=(row, row),
                 out_shape=(jax.ShapeDtypeStruct((s, d), F32), jax.ShapeDtypeStruct((s, d), BF16)),
                 args=[x, y, g_post, gate, g_pre, scale, shift], sem=("parallel",))


def _post_norm_loss_bwd(x, y, g, gate, weight, target, name):
    s, d = y.shape
    ts = _row_tile(s)

    def body(x_ref, y_ref, g_ref, gate_ref, t_ref, dx_ref, dy_ref, l_ref, s1_ref):
        @pl.when(pl.program_id(0) == 0)
        def _():
            l_ref[...] = jnp.zeros_like(l_ref)
            s1_ref[...] = jnp.zeros_like(s1_ref)

        yv = y_ref[...]
        r = lax.rsqrt(jnp.mean(yv * yv, axis=-1, keepdims=True) + EPS)
        yn = yv * r
        err = (x_ref[...] + (weight * gate_ref[...]) * (yn * g_ref[...])) - t_ref[...]
        do = err * (1.0 / d)
        dx_ref[...] = do
        l_ref[...] += 0.5 * jnp.sum(jnp.mean(err * err, axis=-1, keepdims=True), axis=0, keepdims=True)
        dyn = do * ((weight * gate_ref[...]) * g_ref[...])
        dy_ref[...] = (r * (dyn - yn * jnp.mean(dyn * yn, axis=-1, keepdims=True))).astype(BF16)
        s1_ref[...] += jnp.sum(do * yn, axis=0, keepdims=True)

    row = pl.BlockSpec((ts, d), lambda i: (i, 0))
    return _call(body, name=name, grid=(s // ts,), in_specs=[row, row, _vec_spec(d), _vec_spec(d), row],
                 out_specs=(row, row, pl.BlockSpec((1, 1), lambda i: (0, 0)), _vec_spec(d)),
                 out_shape=(jax.ShapeDtypeStruct((s, d), F32), jax.ShapeDtypeStruct((s, d), BF16),
                            jax.ShapeDtypeStruct((1, 1), F32), jax.ShapeDtypeStruct((1, d), F32)),
                 args=[x, y, g, gate, target], sem=("arbitrary",))


def _pre_post_norm_bwd(dh, x, g_pre, scale, dres, y, g_post, gate, weight, name):
    s, d = x.shape
    ts = _row_tile(s)

    def body(dh_ref, x_ref, g_ref, sc_ref, dr_ref, y_ref, gp_ref, gate_ref,
             dx_ref, dy_ref, s2_ref, s3_ref, s1_ref, cs_ref):
        @pl.when(pl.program_id(0) == 0)
        def _():
            for ref in (s2_ref, s3_ref, s1_ref, cs_ref):
                ref[...] = jnp.zeros_like(ref)

        xv = x_ref[...]
        dh = dh_ref[...]
        r = lax.rsqrt(jnp.mean(xv * xv, axis=-1, keepdims=True) + EPS)
        n = xv * r
        dn = dh * (g_ref[...] * (1.0 + sc_ref[...]))
        dx = dr_ref[...] + r * (dn - n * jnp.mean(dn * n, axis=-1, keepdims=True))
        dx_ref[...] = dx
        s2_ref[...] += jnp.sum(dh * n, axis=0, keepdims=True)
        s3_ref[...] += jnp.sum(dh, axis=0, keepdims=True)
        yv = y_ref[...]
        ry = lax.rsqrt(jnp.mean(yv * yv, axis=-1, keepdims=True) + EPS)
        yn = yv * ry
        dyn = dx * ((weight * gate_ref[...]) * gp_ref[...])
        dy = ry * (dyn - yn * jnp.mean(dyn * yn, axis=-1, keepdims=True))
        dy_ref[...] = dy.astype(BF16)
        s1_ref[...] += jnp.sum(dx * yn, axis=0, keepdims=True)
        cs_ref[...] += jnp.sum(dy, axis=0, keepdims=True)

    row = pl.BlockSpec((ts, d), lambda i: (i, 0))
    vec = jax.ShapeDtypeStruct((1, d), F32)
    return _call(body, name=name, grid=(s // ts,),
                 in_specs=[row, row, _vec_spec(d), _vec_spec(d), row, row, _vec_spec(d), _vec_spec(d)],
                 out_specs=(row, row) + (_vec_spec(d),) * 4,
                 out_shape=(jax.ShapeDtypeStruct((s, d), F32), jax.ShapeDtypeStruct((s, d), BF16), vec, vec, vec, vec),
                 args=[dh, x, g_pre, scale, dres, y, g_post, gate], sem=("arbitrary",))


def _group_norm_cat(oa, ob, ga, gb):
    s = oa.shape[0]
    ts = _row_tile(s)

    def body(oa_ref, ob_ref, ga_ref, gb_ref, y_ref):
        for o_ref, g_ref, lo, w in ((oa_ref, ga_ref, 0, QA), (ob_ref, gb_ref, QA, QB)):
            ov = o_ref[...]
            r = lax.rsqrt(jnp.mean(ov * ov, axis=-1, keepdims=True) + EPS)
            y_ref[:, lo:lo + w] = ((ov * r) * g_ref[...]).astype(BF16)

    return _call(body, name="group_norm_cat", grid=(s // ts,),
                 in_specs=[pl.BlockSpec((ts, QA), lambda i: (i, 0)), pl.BlockSpec((ts, QB), lambda i: (i, 0)),
                           _vec_spec(QA), _vec_spec(QB)],
                 out_specs=pl.BlockSpec((ts, QA + QB), lambda i: (i, 0)),
                 out_shape=jax.ShapeDtypeStruct((s, QA + QB), BF16), args=[oa, ob, ga, gb], sem=("parallel",))


def _group_norm_bwd(dy, oa, ob, ga, gb):
    s = oa.shape[0]
    ts = _row_tile(s)

    def body(dy_ref, oa_ref, ob_ref, ga_ref, gb_ref, doa_ref, dob_ref, dga_ref, dgb_ref):
        @pl.when(pl.program_id(0) == 0)
        def _():
            dga_ref[...] = jnp.zeros_like(dga_ref)
            dgb_ref[...] = jnp.zeros_like(dgb_ref)

        for o_ref, g_ref, do_ref, dg_ref, lo, w in ((oa_ref, ga_ref, doa_ref, dga_ref, 0, QA),
                                                    (ob_ref, gb_ref, dob_ref, dgb_ref, QA, QB)):
            ov = o_ref[...]
            dyv = dy_ref[:, lo:lo + w]
            r = lax.rsqrt(jnp.mean(ov * ov, axis=-1, keepdims=True) + EPS)
            n = ov * r
            dn = dyv * g_ref[...]
            do_ref[...] = r * (dn - n * jnp.mean(dn * n, axis=-1, keepdims=True))
            dg_ref[...] += jnp.sum(dyv * n, axis=0, keepdims=True)

    ra = pl.BlockSpec((ts, QA), lambda i: (i, 0))
    rb = pl.BlockSpec((ts, QB), lambda i: (i, 0))
    return _call(body, name="group_norm_bwd", grid=(s // ts,),
                 in_specs=[pl.BlockSpec((ts, QA + QB), lambda i: (i, 0)), ra, rb, _vec_spec(QA), _vec_spec(QB)],
                 out_specs=(ra, rb, _vec_spec(QA), _vec_spec(QB)),
                 out_shape=(jax.ShapeDtypeStruct((s, QA), F32), jax.ShapeDtypeStruct((s, QB), F32),
                            jax.ShapeDtypeStruct((1, QA), F32), jax.ShapeDtypeStruct((1, QB), F32)),
                 args=[dy, oa, ob, ga, gb], sem=("arbitrary",))


def _loss_and_grad(y, target):
    s, d = y.shape
    ts = _row_tile(s)

    def body(y_ref, t_ref, l_ref, g_ref):
        @pl.when(pl.program_id(0) == 0)
        def _():
            l_ref[...] = jnp.zeros_like(l_ref)

        err = y_ref[...] - t_ref[...]
        g_ref[...] = err * (1.0 / d)
        row = jnp.mean(err * err, axis=-1, keepdims=True)
        l_ref[...] += 0.5 * jnp.sum(row, axis=0, keepdims=True)

    row = pl.BlockSpec((ts, d), lambda i: (i, 0))
    return _call(body, name="loss_and_grad", grid=(s // ts,), in_specs=[row, row],
                 out_specs=(pl.BlockSpec((1, 1), lambda i: (0, 0)), row),
                 out_shape=(jax.ShapeDtypeStruct((1, 1), F32), jax.ShapeDtypeStruct((s, d), F32)),
                 args=[y, target], sem=("arbitrary",))


def _col_sum(x, name):
    s, n = x.shape
    ts = _row_tile(s)

    def body(x_ref, o_ref):
        @pl.when(pl.program_id(0) == 0)
        def _():
            o_ref[...] = jnp.zeros_like(o_ref)

        o_ref[...] += jnp.sum(x_ref[...].astype(F32), axis=0, keepdims=True)

    return _call(body, name=name, grid=(s // ts,), in_specs=[pl.BlockSpec((ts, n), lambda i: (i, 0))],
                 out_specs=pl.BlockSpec((1, n), lambda i: (0, 0)), out_shape=jax.ShapeDtypeStruct((1, n), F32),
                 args=[x], sem=("arbitrary",))


def _n_variants(n_back):
    return -(-n_back // QG) + 1


def _alibi_bias():
    i = np.arange(QROWS)[:, None]
    j = np.arange((QG + BACK_A) * CHUNK)[None, :]
    dist = np.abs(BACK_A * CHUNK + i - j).astype(np.float32)
    dc = j // CHUNK - i // CHUNK
    valid = (dc >= 0) & (dc <= BACK_A)
    slopes = np.array([2.0 ** (-8.0 * (h + 1) / H_A) for h in range(H_A)], dtype=np.float32)
    bias = -slopes[:, None, None] * dist[None]
    out = [np.where((valid & (j >= (BACK_A - QG * v) * CHUNK))[None], bias, np.float32(NEG_INF))
           for v in range(_n_variants(BACK_A))]
    return jnp.asarray(np.stack(out).astype(np.float32))


def _rel_index_matrix():
    cc = np.arange(SKEW)
    dist = np.where(cc < SKEW - QROWS, BACK_B * CHUNK - cc, BACK_B * CHUNK + SKEW - cc)
    idx = np.clip(dist, -REL_CLIP, REL_CLIP) + REL_CLIP
    m = np.zeros((SKEW, N_REL), np.float32)
    m[cc, idx] = 1.0
    return jnp.asarray(m)


def _toeplitz_bias(vec, carry=None):
    lk = (QG + BACK_B) * CHUNK
    nv = _n_variants(BACK_B)

    def body(v_ref, o_ref):
        xv = jnp.broadcast_to(v_ref[0], (QROWS, SKEW))
        row = lax.broadcasted_iota(jnp.int32, (QROWS, SKEW), 0)
        for bit in range(QROWS.bit_length() - 1):
            xv = jnp.where((row >> bit) & 1 == 1, pltpu.roll(xv, 1 << bit, 1), xv)
        ri = lax.broadcasted_iota(jnp.int32, (QROWS, lk), 0) // CHUNK
        col = lax.broadcasted_iota(jnp.int32, (QROWS, lk), 1)
        ci = col // CHUNK
        valid = (ci - ri >= 0) & (ci - ri <= BACK_B)
        for v in range(nv):
            o_ref[v, 0] = jnp.where(valid & (col >= (BACK_B - QG * v) * CHUNK), xv[:, :lk], NEG_INF)

    return _call(body, name="toeplitz_bias", grid=(H_B,),
                 in_specs=[pl.BlockSpec((1, 1, SKEW), lambda h: (h, 0, 0))],
                 out_specs=pl.BlockSpec((nv, 1, QROWS, lk), lambda h: (0, h, 0, 0)),
                 out_shape=jax.ShapeDtypeStruct((nv, H_B, QROWS, lk), F32), args=[vec], sem=("parallel",),
                 carry=carry)


def _diagonal_sums(dbias):
    lk = dbias.shape[2]

    def body(d_ref, o_ref):
        xp = jnp.concatenate([d_ref[0], jnp.zeros((QROWS, SKEW - lk), F32)], axis=1)
        xv = xp[0:CHUNK]
        for q in range(1, QG):
            xv = xv + pltpu.roll(xp[q * CHUNK:(q + 1) * CHUNK], SKEW - q * CHUNK, 1)
        row = lax.broadcasted_iota(jnp.int32, (CHUNK, SKEW), 0)
        for bit in range(CHUNK.bit_length() - 1):
            xv = jnp.where((row >> bit) & 1 == 1, pltpu.roll(xv, SKEW - (1 << bit), 1), xv)
        o_ref[0] = jnp.sum(xv, axis=0, keepdims=True)

    return _call(body, name="diagonal_sums", grid=(H_B,),
                 in_specs=[pl.BlockSpec((1, QROWS, lk), lambda h: (h, 0, 0))],
                 out_specs=pl.BlockSpec((1, 1, SKEW), lambda h: (h, 0, 0)),
                 out_shape=jax.ShapeDtypeStruct((H_B, 1, SKEW), F32), args=[dbias], sem=("parallel",))


def _attn_common(s, n_back, gqa, q_col, k_col, v_col):
    lk = (QG + n_back) * CHUNK
    pad = n_back * CHUNK
    wide = TPS * LANES
    q_spec = pl.BlockSpec((QROWS, wide), lambda t, g: (g, q_col // TPS + t))
    if gqa:
        k_spec = pl.BlockSpec((s, LANES), lambda t, g: (0, k_col))
        v_spec = pl.BlockSpec((s, LANES), lambda t, g: (0, v_col))
    else:
        k_spec = pl.BlockSpec((s, wide), lambda t, g: (0, k_col // TPS + t))
        v_spec = pl.BlockSpec((s, wide), lambda t, g: (0, v_col // TPS + t))
    last_variant = _n_variants(n_back) - 1
    bias_spec = pl.BlockSpec((None, 2 * TPS, QROWS, lk), lambda t, g: (jnp.minimum(g, last_variant), t, 0, 0))
    tile_spec = pl.BlockSpec((QROWS, wide), lambda t, g: (g, t))
    return lk, pad, q_spec, k_spec, v_spec, bias_spec, tile_spec


def _attention_fwd(proj, bias, sinks, *, n_back, gqa, q_col, k_col, v_col, name, carry=None):
    s = proj.shape[0]
    lk, pad, q_spec, k_spec, v_spec, bias_spec, tile_spec = _attn_common(s, n_back, gqa, q_col, k_col, v_col)
    n_t, n_g = 512 // (TPS * LANES), s // QROWS
    kv_wide = LANES if gqa else TPS * LANES

    def body(*refs):
        if gqa:
            q_ref, k_ref, v_ref, bias_ref, sink_ref, o_ref, l_ref, kpad, vpad = refs
        else:
            q_ref, k_ref, v_ref, bias_ref, o_ref, l_ref, kpad, vpad = refs
        t, g = pl.program_id(0), pl.program_id(1)

        @pl.when(g == 0)
        def _():
            kpad[0:pad, :] = jnp.zeros((pad, kv_wide), BF16)
            vpad[0:pad, :] = jnp.zeros((pad, kv_wide), BF16)
            kpad[pad:, :] = k_ref[...]
            vpad[pad:, :] = v_ref[...]

        start = pl.multiple_of(g * QROWS, QROWS)
        half = lax.broadcasted_iota(jnp.int32, (QROWS, LANES), 1) // HEAD_DIM
        for tt in range(TPS):
            lanes = slice(tt * LANES, (tt + 1) * LANES)
            kv_lanes = slice(0, LANES) if gqa else lanes
            kb = kpad[pl.ds(start, lk), kv_lanes]
            vb = vpad[pl.ds(start, lk), kv_lanes]
            q = q_ref[:, lanes] * (HEAD_DIM ** -0.5)
            if gqa:
                hk = (TPS * t + tt) // 2
                q_rolled = pltpu.roll(q.astype(F32), HEAD_DIM, 1).astype(BF16)
            outs, lses = [], []
            for e in range(2):
                if gqa:
                    kv_half = hk
                    src = jnp.where(hk == e, q, q_rolled)
                else:
                    kv_half = e
                    src = q
                qm = jnp.where(half == kv_half, src, jnp.zeros_like(src))
                sc = _dot_nt(qm, kb) + bias_ref[2 * tt + e]
                m = jnp.max(sc, axis=-1, keepdims=True)
                if gqa:
                    sk = sink_ref[2 * (TPS * t + tt) + e]
                    m = jnp.maximum(m, sk)
                p = jnp.exp(sc - m)
                l = jnp.sum(p, axis=-1, keepdims=True)
                if gqa:
                    l = l + jnp.exp(sk - m)
                pn = p / l
                outs.append(_dot(pn.astype(BF16), vb))
                lses.append(m + jnp.log(l))
            if gqa:
                same = jnp.where(hk == 0, outs[0], outs[1])
                other = jnp.where(hk == 0, outs[1], outs[0])
                o_ref[:, lanes] = jnp.where(half == hk, same, pltpu.roll(other, HEAD_DIM, 1))
            else:
                o_ref[:, lanes] = jnp.where(half == 0, outs[0], outs[1])
            l_ref[:, lanes] = jnp.where(half == 0, lses[0], lses[1])

    in_specs = [q_spec, k_spec, v_spec, bias_spec] + ([SMEM_SPEC] if gqa else [])
    args = [proj, proj, proj, bias] + ([sinks] if gqa else [])
    o_shape = jax.ShapeDtypeStruct((s, 512), F32)
    return _call(body, name=name, grid=(n_t, n_g), in_specs=in_specs, out_specs=(tile_spec, tile_spec),
                 out_shape=(o_shape, o_shape), args=args,
                 scratch=[pltpu.VMEM((s + pad, kv_wide), BF16), pltpu.VMEM((s + pad, kv_wide), BF16)],
                 sem=("arbitrary", "arbitrary"), carry=carry)


def _attention_bwd(proj, bias, sinks, do, lse, *, n_back, gqa, q_col, k_col, v_col, name, carry=None):
    s = proj.shape[0]
    lk, pad, q_spec, k_spec, v_spec, bias_spec, tile_spec = _attn_common(s, n_back, gqa, q_col, k_col, v_col)
    n_t, n_g = 512 // (TPS * LANES), s // QROWS
    kv_wide = LANES if gqa else TPS * LANES

    def body(*refs):
        if gqa:
            (q_ref, k_ref, v_ref, bias_ref, sink_ref, do_ref, l_ref,
             dq_ref, dk_ref, dv_ref, dsink_ref, kpad, vpad, dkpad, dvpad) = refs
        else:
            (q_ref, k_ref, v_ref, bias_ref, do_ref, l_ref,
             dq_ref, dk_ref, dv_ref, dbias_ref, kpad, vpad, dkpad, dvpad) = refs
        t, g = pl.program_id(0), pl.program_id(1)

        @pl.when(g == 0)
        def _():
            kpad[0:pad, :] = jnp.zeros((pad, kv_wide), BF16)
            vpad[0:pad, :] = jnp.zeros((pad, kv_wide), BF16)
            kpad[pad:, :] = k_ref[...]
            vpad[pad:, :] = v_ref[...]
            if gqa:
                dsink_ref[...] = jnp.zeros_like(dsink_ref)
            else:
                dbias_ref[...] = jnp.zeros_like(dbias_ref)

        @pl.when((g == 0) & (t == 0) if gqa else g == 0)
        def _():
            dkpad[...] = jnp.zeros_like(dkpad)
            dvpad[...] = jnp.zeros_like(dvpad)

        start = pl.multiple_of(g * QROWS, QROWS)
        half = lax.broadcasted_iota(jnp.int32, (QROWS, LANES), 1) // HEAD_DIM
        for tt in range(TPS):
            lanes = slice(tt * LANES, (tt + 1) * LANES)
            kv_lanes = slice(0, LANES) if gqa else lanes
            kb = kpad[pl.ds(start, lk), kv_lanes]
            vb = vpad[pl.ds(start, lk), kv_lanes]
            q = q_ref[:, lanes]
            dov = do_ref[:, lanes]
            lv = l_ref[:, lanes]
            if gqa:
                hk = (TPS * t + tt) // 2
                q_rolled = pltpu.roll(q.astype(F32), HEAD_DIM, 1).astype(BF16)
                do_rolled = pltpu.roll(dov, HEAD_DIM, 1)
            dqs = []
            dk_acc = jnp.zeros((lk, LANES), F32)
            dv_acc = jnp.zeros((lk, LANES), F32)
            for e in range(2):
                if gqa:
                    kv_half = hk
                    src = jnp.where(hk == e, q, q_rolled)
                    do_src = jnp.where(hk == e, dov, do_rolled)
                else:
                    kv_half = e
                    src = q
                    do_src = dov
                qm = jnp.where(half == kv_half, src, jnp.zeros_like(src))
                dom = jnp.where(half == kv_half, do_src, 0.0).astype(BF16)
                lcol = jnp.max(jnp.where(half == e, lv, -jnp.inf), axis=-1, keepdims=True)
                sc = _dot_nt(qm * (HEAD_DIM ** -0.5), kb) + bias_ref[2 * tt + e]
                pn = jnp.exp(sc - lcol)
                dp = _dot_nt(dom, vb)
                delta = jnp.sum(pn * dp, axis=-1, keepdims=True)
                ds = pn * (dp - delta)
                if gqa:
                    p_sink = jnp.exp(sink_ref[2 * (TPS * t + tt) + e] - lcol)
                    dsk = -jnp.sum(p_sink * delta, axis=0, keepdims=True)
                    row = 2 * tt + e
                    dsink_ref[0, row:row + 1, :] += jnp.broadcast_to(dsk, (1, LANES))
                else:
                    dbias_ref[2 * tt + e] += ds
                dsb = (ds * (HEAD_DIM ** -0.5)).astype(BF16)
                dqs.append(_dot(dsb, kb))
                dk_acc = dk_acc + _dot_tn(dsb, qm)
                dv_acc = dv_acc + _dot_tn(pn.astype(BF16), dom)
            dkpad[pl.ds(start, lk), kv_lanes] += dk_acc
            dvpad[pl.ds(start, lk), kv_lanes] += dv_acc
            if gqa:
                same = jnp.where(hk == 0, dqs[0], dqs[1])
                other = jnp.where(hk == 0, dqs[1], dqs[0])
                dq_ref[:, lanes] = jnp.where(half == hk, same, pltpu.roll(other, HEAD_DIM, 1)).astype(BF16)
            else:
                dq_ref[:, lanes] = jnp.where(half == 0, dqs[0], dqs[1]).astype(BF16)

        @pl.when((g == n_g - 1) & (t == n_t - 1) if gqa else g == n_g - 1)
        def _():
            dk_ref[...] = dkpad[pad:, :].astype(BF16)
            dv_ref[...] = dvpad[pad:, :].astype(BF16)

    in_specs = [q_spec, k_spec, v_spec, bias_spec] + ([SMEM_SPEC] if gqa else []) + [tile_spec, tile_spec]
    args = [proj, proj, proj, bias] + ([sinks] if gqa else []) + [do, lse]
    if gqa:
        kv_out = pl.BlockSpec((s, LANES), lambda t, g: (0, 0))
        kv_shape = jax.ShapeDtypeStruct((s, LANES), BF16)
        extra_spec = pl.BlockSpec((1, 8, LANES), lambda t, g: (t, 0, 0))
        extra_shape = jax.ShapeDtypeStruct((n_t, 8, LANES), F32)
    else:
        kv_out = pl.BlockSpec((s, kv_wide), lambda t, g: (0, t))
        kv_shape = jax.ShapeDtypeStruct((s, 512), BF16)
        extra_spec = pl.BlockSpec((2 * TPS, QROWS, lk), lambda t, g: (t, 0, 0))
        extra_shape = jax.ShapeDtypeStruct(bias.shape[1:], F32)
    return _call(body, name=name, grid=(n_t, n_g), in_specs=in_specs,
                 out_specs=(tile_spec, kv_out, kv_out, extra_spec),
                 out_shape=(jax.ShapeDtypeStruct((s, 512), BF16), kv_shape, kv_shape, extra_shape), args=args,
                 scratch=[pltpu.VMEM((s + pad, kv_wide), BF16), pltpu.VMEM((s + pad, kv_wide), BF16),
                          pltpu.VMEM((s + pad, kv_wide), F32), pltpu.VMEM((s + pad, kv_wide), F32)],
                 sem=("arbitrary", "arbitrary"), carry=carry)


def _sum_slots(r, name):
    n_slots, rows, k = r.shape

    def body(r_ref, o_ref):
        acc = r_ref[0].astype(F32)
        for j in range(1, n_slots):
            acc = acc + r_ref[j].astype(F32)
        o_ref[...] = acc

    return _call(body, name=name, grid=(k // LANES,),
                 in_specs=[pl.BlockSpec((n_slots, rows, LANES), lambda i: (0, 0, i))],
                 out_specs=pl.BlockSpec((rows, LANES), lambda i: (0, i)),
                 out_shape=jax.ShapeDtypeStruct((rows, k), F32), args=[r], sem=("parallel",))


def _sum_rows8(g):
    n = g.shape[2]

    def body(g_ref, o_ref):
        acc = g_ref[0]
        for j in range(1, N_DEV):
            acc = acc + g_ref[j]
        o_ref[...] = acc

    return pl.pallas_call(
        body, name="sum_small_grads", in_specs=[VMEM_SPEC], out_specs=VMEM_SPEC,
        out_shape=jax.ShapeDtypeStruct((1, n), F32), compiler_params=_params(),
    )(g)


def _ada_weight_grad(sc_t, dmod_cols):
    d = sc_t.shape[0]
    w = dmod_cols.shape[1]
    td = _pick(d, (256, 128))

    def body(sc_ref, dm_ref, o_ref):
        scv = sc_ref[...]
        dmv = dm_ref[...]
        acc = scv[:, 0:1] * dmv[0:1, :]
        for b in range(1, N_DEV):
            acc = acc + scv[:, b:b + 1] * dmv[b:b + 1, :]
        o_ref[...] = acc

    return _call(body, name="ada_weight_grad", grid=(d // td,),
                 in_specs=[pl.BlockSpec((td, N_DEV), lambda i: (i, 0)), pl.BlockSpec((N_DEV, w), lambda i: (0, 0))],
                 out_specs=pl.BlockSpec((td, w), lambda i: (i, 0)), out_shape=jax.ShapeDtypeStruct((d, w), F32),
                 args=[sc_t, dmod_cols], sem=("parallel",))


def _adamw_update(w, gv, m, v):
    nm = ADAM_B1 * m + (1.0 - ADAM_B1) * gv
    nv = ADAM_B2 * v + (1.0 - ADAM_B2) * (gv * gv)
    m_hat = nm / (1.0 - ADAM_B1 ** ADAM_STEP)
    v_hat = nv / (1.0 - ADAM_B2 ** ADAM_STEP)
    return -ADAM_LR * (m_hat / (jnp.sqrt(v_hat) + ADAM_EPS) + ADAM_WD * w), nm, nv


def _adamw(w, g, m, v, name):
    rows, cols = w.shape
    tr = _pick(rows, (256, 176, 128, 88, 64)) if rows > 256 else rows

    def body(w_ref, g_ref, m_ref, v_ref, d_ref, nm_ref, nv_ref):
        d_ref[...], nm_ref[...], nv_ref[...] = _adamw_update(w_ref[...], g_ref[...], m_ref[...], v_ref[...])

    spec = pl.BlockSpec((tr, cols), lambda i: (i, 0))
    shape = jax.ShapeDtypeStruct((rows, cols), F32)
    return _call(body, name=name, grid=(rows // tr,), in_specs=[spec] * 4, out_specs=(spec, spec, spec),
                 out_shape=(shape, shape, shape), args=[w, g, m, v], sem=("parallel",))


def _adamw_from_slots(w, own, slots, m, v, name):
    n_slots, rows, k = slots.shape

    def body(o_ref, s_ref, w_ref, m_ref, v_ref, g_ref, d_ref, nm_ref, nv_ref):
        gv = o_ref[...].astype(F32)
        for j in range(n_slots):
            gv = gv + s_ref[j].astype(F32)
        g_ref[...] = gv
        d_ref[...], nm_ref[...], nv_ref[...] = _adamw_update(w_ref[...], gv, m_ref[...], v_ref[...])

    tr = rows // 2 if rows % 32 == 0 else rows
    spec = pl.BlockSpec((tr, k), lambda i: (i, 0))
    shape = jax.ShapeDtypeStruct((rows, k), F32)
    return _call(body, name=name, grid=(rows // tr,),
                 in_specs=[spec, pl.BlockSpec((n_slots, tr, k), lambda i: (0, i, 0)), spec, spec, spec],
                 out_specs=(spec, spec, spec, spec), out_shape=(shape, shape, shape, shape),
                 args=[own, slots, w, m, v], sem=("parallel",))


def _adamw_small(g, w, m, v, sizes):
    n = w.shape[1]
    offs, off = [], 0
    for size in sizes:
        offs.append(off)
        off += size + (-size % LANES)

    def body(g_ref, w_ref, m_ref, v_ref, *out_refs):
        gv = g_ref[:, 0:n]
        dv, nm, nv = _adamw_update(w_ref[...], gv, m_ref[...], v_ref[...])
        for j, (o, size) in enumerate(zip(offs, sizes)):
            for k, val in enumerate((gv, dv, nm, nv)):
                out_refs[4 * j + k][...] = val[:, o:o + size]

    shapes = [jax.ShapeDtypeStruct((1, size), F32) for size in sizes for _ in range(4)]
    return pl.pallas_call(
        body, name="adamw_small", in_specs=[VMEM_SPEC] * 4, out_specs=tuple([VMEM_SPEC] * len(shapes)),
        out_shape=tuple(shapes), compiler_params=_params(),
    )(g, w, m, v)


SMALL = ("b_ada", "g_pre_ffn1", "g_post_ffn1", "g_pre_mix", "b_in", "sinks_a", "rel_bias_b", "g_grp_a",
         "g_grp_b", "b_out", "g_post_mix", "g_pre_ffn2", "g_post_ffn2")
WEIGHTS = ("w_ada", "b_ada", "g_pre_ffn1", "w_gate1", "w_up1", "w_down1", "g_post_ffn1", "g_pre_mix", "w_in",
           "b_in", "sinks_a", "rel_bias_b", "g_grp_a", "g_grp_b", "w_out", "b_out", "g_post_mix", "g_pre_ffn2",
           "w_gate2", "w_up2", "w_down2", "g_post_ffn2")


def kernel(x, c, w_ada, b_ada, g_pre_ffn1, w_gate1, w_up1, w_down1, g_post_ffn1, g_pre_mix, w_in, b_in, sinks_a, rel_bias_b, g_grp_a, g_grp_b, w_out, b_out, g_post_mix, g_pre_ffn2, w_gate2, w_up2, w_down2, g_post_ffn2, loss_target, m_w_ada, m_b_ada, m_g_pre_ffn1, m_w_gate1, m_w_up1, m_w_down1, m_g_post_ffn1, m_g_pre_mix, m_w_in, m_b_in, m_sinks_a, m_rel_bias_b, m_g_grp_a, m_g_grp_b, m_w_out, m_b_out, m_g_post_mix, m_g_pre_ffn2, m_w_gate2, m_w_up2, m_w_down2, m_g_post_ffn2, v_w_ada, v_b_ada, v_g_pre_ffn1, v_w_gate1, v_w_up1, v_w_down1, v_g_post_ffn1, v_g_pre_mix, v_w_in, v_b_in, v_sinks_a, v_rel_bias_b, v_g_grp_a, v_g_grp_b, v_w_out, v_b_out, v_g_post_mix, v_g_pre_ffn2, v_w_gate2, v_w_up2, v_w_down2, v_g_post_ffn2):
    given = dict(locals())
    weights = {n: given[n] for n in WEIGHTS}
    mom_m = {n: given["m_" + n] for n in WEIGHTS}
    mom_v = {n: given["v_" + n] for n in WEIGHTS}

    me = 4 * lax.axis_index("x") + 2 * lax.axis_index("y") + lax.axis_index("c")
    xs = x[0]
    tgt = loss_target[0]
    d_model = xs.shape[1]
    ada_cols = w_ada.shape[2]

    sh = {"wg1": w_gate1[0].T, "wu1": w_up1[0].T, "wd1": w_down1[0], "win": w_in[0].T, "wo": w_out[0],
          "wg2": w_gate2[0].T, "wu2": w_up2[0].T, "wd2": w_down2[0]}
    sh = {k: v.astype(BF16) for k, v in sh.items()}

    def gather(*names):
        return _gather_carry([sh[n] for n in names])

    bias_a = _alibi_bias()
    rel_m = _rel_index_matrix()
    rel_vec = jnp.dot(rel_bias_b[0], rel_m.T, precision=lax.Precision.HIGHEST)
    bias_b, (wg1, wu1) = _toeplitz_bias(rel_vec.reshape(H_B, 1, SKEW), carry=gather("wg1", "wu1"))

    b_cols = lax.dynamic_slice(b_ada, (0, me * ada_cols), (1, ada_cols))
    (sc_all, mod_rows), _ = _ada_forward(c, w_ada[0], b_cols, _Carry([], [], [], lambda *a: None, lambda *a: None))
    mod = mod_rows.reshape(N_MOD, d_model)
    shift1, scale1, gate1, shift2, scale2, gate2, shift3, scale3, gate3 = (mod[i:i + 1] for i in range(N_MOD))

    h1 = _pre_norm(xs, g_pre_ffn1, scale1, shift1, "pre_norm_ffn1")
    (a1, b1, u1), (wd1,) = _ffn_up(h1, wg1, wu1, "ffn_up_ffn1", carry=gather("wd1"))
    (y1, x1, h2), (win,) = _mm_nn(
        [(u1, wd1)], "ffn_down_ffn1", F32, carry=gather("win"),
        tail=_tail_post_pre(xs, g_post_ffn1, gate1, 0.5, g_pre_mix, scale2, shift2))

    proj, (wo,) = _mm_nt(h2, win, "in_proj", BF16, bias=b_in, carry=gather("wo"))
    sinks = sinks_a[0]
    cfg_a = dict(n_back=BACK_A, gqa=True, q_col=0, k_col=QA // LANES, v_col=(QA + KVA) // LANES)
    cfg_b = dict(n_back=BACK_B, gqa=False, q_col=(QA + 2 * KVA) // LANES, k_col=(QA + 2 * KVA + QB) // LANES,
                 v_col=(QA + 2 * KVA + 2 * QB) // LANES)
    (oa, lse_a), (wg2,) = _attention_fwd(proj, bias_a, sinks, name="attn_a", carry=gather("wg2"), **cfg_a)
    (ob, lse_b), (wu2,) = _attention_fwd(proj, bias_b, None, name="attn_b", carry=gather("wu2"), **cfg_b)
    ycat = _group_norm_cat(oa, ob, g_grp_a, g_grp_b)
    ymix, x2, h3 = _mm_nn([(ycat, wo)], "out_proj", F32, bias=b_out,
                          tail=_tail_post_pre(x1, g_post_mix, gate2, 1.0, g_pre_ffn2, scale3, shift3))

    (a3, b3, u3), (wd2,) = _ffn_up(h3, wg2, wu2, "ffn_up_ffn2", carry=gather("wd2"))

    flights, own = {}, {}

    def grad_pair(key, a_mat, b_mat, name):
        part, own[key] = _mm_tn_pair(a_mat, b_mat, name)
        return part

    def scatter_start(tag, after_vec, **parts):
        names = list(parts)
        sems, p_thru, lands, token = _scatter_start([parts[n] for n in names], "scatter_start_" + tag)
        flights[tag] = (names, sems, p_thru, lands)
        return after_vec + token[0:1, 0:1]

    dx3, dy, loss_part, s1 = _mm_nn([(u3, wd2)], "ffn_down_ffn2", None,
                                    tail=_tail_post_loss(x2, tgt, g_post_ffn2, gate3, 0.5))
    da, db = _ffn_down_bwd(dy, wd2, a3, b3, "ffn_down_bwd_ffn2")
    dwd2 = grad_pair("wd2", u3, dy, "grad_wd_ffn2")
    dwg2 = grad_pair("wg2", da, h3, "grad_wg_ffn2")
    dwu2 = grad_pair("wu2", db, h3, "grad_wu_ffn2")
    g_pre_tied = scatter_start("ffn2", g_pre_ffn2, wd2=dwd2, wg2=dwg2, wu2=dwu2)
    dx2, dymix, s2, s3, s1m, db_out = _mm_nn(
        [(da, wg2), (db, wu2)], "ffn_up_bwd_ffn2", None,
        tail=_tail_pre_post_bwd(x2, dx3, ymix, g_pre_tied, scale3, g_post_mix, gate2, 1.0))
    sm3 = dict(shift=s3, scale=s2 * g_pre_ffn2, gate=0.5 * g_post_ffn2 * s1,
               g_pre=(1.0 + scale3) * s2, g_post=(0.5 * gate3) * s1)

    dycat = _mm_nt(dymix, wo, "out_proj_bwd", F32)
    dwo = grad_pair("wo", ycat, dymix, "grad_wo")
    doa, dob, dg_a, dg_b = _group_norm_bwd(dycat, oa, ob, g_grp_a, g_grp_b)
    dqa, dka, dva, dsink = _attention_bwd(proj, bias_a, sinks, doa, lse_a, name="attn_a_bwd", **cfg_a)
    dqb, dkb, dvb, dbias = _attention_bwd(proj, bias_b, None, dob, lse_b, name="attn_b_bwd", **cfg_b)
    dproj = jnp.concatenate([dqa, dka, dva, dqb, dkb, dvb], axis=1)
    db_in = _col_sum(dproj, "grad_b_in")
    dwin = grad_pair("win", dproj, h2, "grad_win")
    g_pre_tied = scatter_start("mix", g_pre_mix, wo=dwo, win=dwin)
    dx1, dy, s2m, s3m, s1, _ = _mm_nn(
        [(dproj, win)], "in_proj_bwd", None,
        tail=_tail_pre_post_bwd(x1, dx2, y1, g_pre_tied, scale2, g_post_ffn1, gate1, 0.5))
    d_rel = jnp.dot(_diagonal_sums(dbias).reshape(H_B, SKEW), rel_m, precision=lax.Precision.HIGHEST)
    d_sinks = dsink[:, :2 * TPS, 0].reshape(1, H_A)

    da, db = _ffn_down_bwd(dy, wd1, a1, b1, "ffn_down_bwd_ffn1")
    dwd1 = grad_pair("wd1", u1, dy, "grad_wd_ffn1")
    dwg1 = grad_pair("wg1", da, h1, "grad_wg_ffn1")
    dwu1 = grad_pair("wu1", db, h1, "grad_wu_ffn1")
    g_pre_tied = scatter_start("ffn1", g_pre_ffn1, wd1=dwd1, wg1=dwg1, wu1=dwu1)
    dx0, s2, s3 = _mm_nn([(da, wg1), (db, wu1)], "ffn_up_bwd_ffn1", None,
                         tail=_tail_pre_bwd(xs, dx1, g_pre_tied, scale1))
    sm1 = dict(shift=s3, scale=s2 * g_pre_ffn1, gate=0.5 * g_post_ffn1 * s1,
               g_pre=(1.0 + scale1) * s2, g_post=(0.5 * gate1) * s1)

    dmod = jnp.concatenate([sm1["shift"], sm1["scale"], sm1["gate"],
                            s3m, s2m * g_pre_mix, g_post_mix * s1m,
                            sm3["shift"], sm3["scale"], sm3["gate"]], axis=1)
    small_parts = {
        "b_ada": dmod, "g_pre_ffn1": sm1["g_pre"], "g_post_ffn1": sm1["g_post"],
        "g_pre_mix": (1.0 + scale2) * s2m, "b_in": db_in, "sinks_a": d_sinks,
        "rel_bias_b": d_rel.reshape(1, H_B * N_REL), "g_grp_a": dg_a, "g_grp_b": dg_b, "b_out": db_out,
        "g_post_mix": gate2 * s1m, "g_pre_ffn2": sm3["g_pre"], "g_post_ffn2": sm3["g_post"]}
    sizes = [small_parts[n].shape[1] for n in SMALL]

    def pack(parts):
        cells = []
        for p in parts:
            cells.append(p)
            if p.shape[1] % LANES:
                cells.append(jnp.zeros((1, -p.shape[1] % LANES), F32))
        return jnp.concatenate(cells, axis=1)

    packed = pack([small_parts[n] for n in SMALL] + [loss_part])
    n_packed = packed.shape[1]
    small_sems, packed_thru, small_land, small_token = _small_gather_start(packed)

    out_g, out_d, out_m, out_v = {}, {}, {}, {}
    groups = (("ffn2", (("w_gate2", "wg2", True), ("w_up2", "wu2", True), ("w_down2", "wd2", False))),
              ("mix", (("w_in", "win", True), ("w_out", "wo", False))),
              ("ffn1", (("w_gate1", "wg1", True), ("w_up1", "wu1", True), ("w_down1", "wd1", False))))
    after = small_token
    for tag, members in groups:
        names, sems, p_thru, lands = flights[tag]
        _, l_done = _scatter_wait(sems, p_thru, lands, after, "scatter_wait_" + tag)
        slots = dict(zip(names, l_done))
        for n, key, transposed in members:
            view = (lambda t: t.T) if transposed else (lambda t: t)
            res = _adamw_from_slots(view(weights[n][0]), own[key], slots[key], view(mom_m[n][0]),
                                    view(mom_v[n][0]), "adamw_" + n)
            out_g[n], out_d[n], out_m[n], out_v[n] = (view(t)[None] for t in res)
            after = res[3]

    packed_done, small_land = _small_gather_wait(small_sems, packed_thru, small_land, after)
    gathered = lax.dynamic_update_slice(small_land, packed_done[None], (me, 0, 0))
    small_sum = _sum_rows8(gathered)
    loss = small_sum[0, n_packed - LANES]
    dmod_cols = lax.dynamic_slice(gathered.reshape(N_DEV, n_packed), (0, me * ada_cols), (N_DEV, ada_cols))
    g_ada = _ada_weight_grad(sc_all.reshape(N_DEV, d_model).T, dmod_cols)
    d_, m_, v_ = _adamw(w_ada[0], g_ada, m_w_ada[0], v_w_ada[0], "adamw_w_ada")
    out_g["w_ada"], out_d["w_ada"], out_m["w_ada"], out_v["w_ada"] = g_ada[None], d_[None], m_[None], v_[None]

    small_out = _adamw_small(small_sum, *(pack([tree[n].reshape(1, -1) for n in SMALL])
                                          for tree in (weights, mom_m, mom_v)), sizes)
    for j, n in enumerate(SMALL):
        shape = weights[n].shape
        out_g[n], out_d[n], out_m[n], out_v[n] = (t.reshape(shape) for t in small_out[4 * j:4 * j + 4])

    return (loss, dx0[None], *[out_g[n] for n in WEIGHTS], *[out_d[n] for n in WEIGHTS],
            *[out_m[n] for n in WEIGHTS], *[out_v[n] for n in WEIGHTS])
```

```python
import numpy as np
import jax
import jax.numpy as jnp
from jax import lax
from jax.experimental import pallas as pl
from jax.experimental.pallas import tpu as pltpu

F32 = jnp.float32
BF16 = jnp.bfloat16
MESH = pl.DeviceIdType.MESH
ANY = pl.BlockSpec(memory_space=pl.ANY)
VMEM_SPEC = pl.BlockSpec(memory_space=pltpu.VMEM)
SMEM_SPEC = pl.BlockSpec(memory_space=pltpu.SMEM)

N_DEV = 8
CHUNK = 64
HEAD_DIM = 64
LANES = 128
H_A, KV_A, H_B = 8, 2, 8
BACK_A, BACK_B = 2, 8
REL_CLIP = 128
N_REL = 2 * REL_CLIP + 1
QA, KVA, QB = H_A * HEAD_DIM, KV_A * HEAD_DIM, H_B * HEAD_DIM
D_IN = QA + 2 * KVA + 3 * QB
N_MOD = 9
EPS = 1e-6
NEG_INF = -1e30
QG = 4
QROWS = QG * CHUNK
TPS = 2
SKEW = 1024
ADAM_LR, ADAM_B1, ADAM_B2, ADAM_EPS, ADAM_WD, ADAM_STEP = 0.001, 0.9, 0.999, 1e-08, 0.01, 10
VMEM_LIMIT = 56 * 2 ** 20


def _pick(n, cands):
    for c in cands:
        if n % c == 0:
            return c
    return n


def _pieces(n, width=2 * LANES):
    return [(lo, min(lo + width, n)) for lo in range(0, n, width)]


def _params(sem=None):
    return pltpu.CompilerParams(dimension_semantics=sem, vmem_limit_bytes=VMEM_LIMIT)


def _dot_nt(a, b):
    return lax.dot_general(a, b, (((1,), (1,)), ((), ())), preferred_element_type=F32)


def _dot_tn(a, b):
    return lax.dot_general(a, b, (((0,), (0,)), ((), ())), preferred_element_type=F32)


def _dot(a, b):
    return jnp.dot(a, b, preferred_element_type=F32)


def _sigmoid(a):
    return 0.5 * (jnp.tanh(0.5 * a) + 1.0)


def _mesh_pos():
    return lax.axis_index("x"), lax.axis_index("y"), lax.axis_index("c")


def _peer(x, y, c, r):
    px = 1 - x if r & 4 else x
    py = 1 - y if r & 2 else y
    pc = 1 - c if r & 1 else c
    return px, py, pc


class _Carry:
    def __init__(self, ins, out_shapes, scratch, start, finish, aliased=0):
        self.ins, self.out_shapes, self.scratch = list(ins), list(out_shapes), list(scratch)
        self.start, self.finish = start, finish
        self.aliased = aliased


def _call(body, *, name, grid, in_specs, out_specs, out_shape, args, scratch=(), sem=None, carry=None):
    single = not isinstance(out_shape, (tuple, list))
    out_specs = (out_specs,) if single else tuple(out_specs)
    out_shape = (out_shape,) if single else tuple(out_shape)
    if carry is None:
        res = pl.pallas_call(body, name=name, grid=grid, in_specs=list(in_specs), out_specs=out_specs,
                             out_shape=out_shape, scratch_shapes=list(scratch), compiler_params=_params(sem))(*args)
        return res[0] if single else res
    n_in, n_out, n_s = len(in_specs), len(out_shape), len(scratch)
    ci, co = len(carry.ins), len(carry.out_shapes)

    def wrapped(*refs):
        ins, cins = refs[:n_in], refs[n_in:n_in + ci]
        outs = refs[n_in + ci:n_in + ci + n_out]
        couts = refs[n_in + ci + n_out:n_in + ci + n_out + co]
        scr = refs[n_in + ci + n_out + co:n_in + ci + n_out + co + n_s]
        cscr = refs[n_in + ci + n_out + co + n_s:]
        first, last = None, None
        for ax, n in enumerate(grid):
            f, l = pl.program_id(ax) == 0, pl.program_id(ax) == n - 1
            first = f if first is None else first & f
            last = l if last is None else last & l
        pl.when(first)(lambda: carry.start(cins, couts, cscr))
        body(*ins, *outs, *scr)
        pl.when(last)(lambda: carry.finish(cins, couts, cscr))

    res = pl.pallas_call(
        wrapped, name=name, grid=grid, in_specs=list(in_specs) + [ANY] * ci, out_specs=out_specs + (ANY,) * co,
        out_shape=out_shape + tuple(carry.out_shapes), scratch_shapes=list(scratch) + carry.scratch,
        input_output_aliases={n_in + i: n_out + i for i in range(carry.aliased)},
        compiler_params=_params(("arbitrary",) * len(grid)))(*args, *carry.ins)
    main = res[:n_out]
    return (main[0] if single else main), res[n_out:]


def _run_carry(carry, name):
    ci, co = len(carry.ins), len(carry.out_shapes)

    def body(*refs):
        carry.start(refs[:ci], refs[ci:ci + co], refs[ci + co:])
        carry.finish(refs[:ci], refs[ci:ci + co], refs[ci + co:])

    return pl.pallas_call(body, name=name, in_specs=[ANY] * ci, out_specs=(ANY,) * co,
                          out_shape=tuple(carry.out_shapes), scratch_shapes=carry.scratch,
                          input_output_aliases={i: i for i in range(carry.aliased)},
                          compiler_params=_params())(*carry.ins)


def _gather_carry(shards):
    n_w = len(shards)
    rows = [s.shape[0] for s in shards]

    def plan(ins, outs, scr):
        send_sems, recv_sems, local_sems = scr
        x, y, c = _mesh_pos()
        me, sibling = (x, y, c), (x, y, 1 - c)
        chips = [(1 - x, y), (x, 1 - y), (1 - x, 1 - y)]

        def block(w, dev):
            start = pl.multiple_of((4 * dev[0] + 2 * dev[1] + dev[2]) * rows[w], 16)
            return outs[w].at[pl.ds(start, rows[w]), :]

        def copy(w, k, dev, to, src=None):
            return pltpu.make_async_remote_copy(
                src_ref=block(w, dev) if src is None else src, dst_ref=block(w, dev),
                send_sem=send_sems.at[w, k], recv_sem=recv_sems.at[w, k], device_id=to, device_id_type=MESH)

        mine = [pltpu.make_async_copy(ins[w], block(w, me), local_sems.at[w]) for w in range(n_w)]
        first = []
        for j, chip in enumerate(chips):
            first += [copy(w, 1 + j, me, (*chip, c), src=ins[w]) for w in range(n_w)]
        first += [copy(w, 0, me, sibling, src=ins[w]) for w in range(n_w)]
        return c, me, sibling, chips, copy, mine, first

    def start(ins, outs, scr):
        _, _, _, _, _, mine, first = plan(ins, outs, scr)
        for cp in mine + first:
            cp.start()

    def finish(ins, outs, scr):
        c, me, sibling, chips, copy, mine, first = plan(ins, outs, scr)
        passed = []
        for j, chip in enumerate(chips):
            for w in range(n_w):
                copy(w, 1 + j, (*chip, c), me).wait_recv()
                cp = copy(w, 4 + j, (*chip, c), sibling)
                cp.start()
                passed.append(cp)
        for w in range(n_w):
            copy(w, 0, sibling, me).wait_recv()
        for j, chip in enumerate(chips):
            for w in range(n_w):
                copy(w, 4 + j, (*chip, 1 - c), me).wait_recv()
        for cp in first + passed:
            cp.wait_send()
        for cp in mine:
            cp.wait()

    return _Carry(
        shards, [jax.ShapeDtypeStruct((N_DEV * s.shape[0], s.shape[1]), s.dtype) for s in shards],
        [pltpu.SemaphoreType.DMA((n_w, N_DEV - 1)), pltpu.SemaphoreType.DMA((n_w, N_DEV - 1)),
         pltpu.SemaphoreType.DMA((n_w,))], start, finish)


def _scatter_carry(parts):
    n_w = len(parts)
    n_chip = N_DEV // 2
    rows = [g.shape[0] // n_chip for g in parts]

    def plan(ins, outs, scr):
        send_sems, recv_sems, local_sems = scr
        x, y, c = _mesh_pos()

        def src(w, chip_index):
            return ins[w].at[pl.ds(pl.multiple_of(chip_index * rows[w], 16), rows[w]), :]

        mine = [pltpu.make_async_copy(src(w, 2 * x + y), outs[w].at[0], local_sems.at[w]) for w in range(n_w)]
        copies = []
        for r in (3, 2, 1):
            px, py, _ = _peer(x, y, c, 2 * r)
            for w in range(n_w):
                copies.append(pltpu.make_async_remote_copy(
                    src_ref=src(w, 2 * px + py), dst_ref=outs[w].at[r], send_sem=send_sems.at[w, r - 1],
                    recv_sem=recv_sems.at[w, r - 1], device_id=(px, py, c), device_id_type=MESH))
        return mine, copies

    def start(ins, outs, scr):
        mine, copies = plan(ins, outs, scr)
        for cp in mine + copies:
            cp.start()

    def finish(ins, outs, scr):
        mine, copies = plan(ins, outs, scr)
        for cp in copies:
            cp.wait_recv()
        for cp in copies:
            cp.wait_send()
        for cp in mine:
            cp.wait()

    return _Carry(
        parts, [jax.ShapeDtypeStruct((n_chip, r, g.shape[1]), g.dtype) for r, g in zip(rows, parts)],
        [pltpu.SemaphoreType.DMA((n_w, n_chip - 1)), pltpu.SemaphoreType.DMA((n_w, n_chip - 1)),
         pltpu.SemaphoreType.DMA((n_w,))], start, finish)


HBM_SPEC = pl.BlockSpec(memory_space=pltpu.HBM)
SEM_SPEC = pl.BlockSpec(memory_space=pltpu.SEMAPHORE)
N_CHIP = N_DEV // 2


def _scatter_copy(part_ref, land_ref, send_sem, recv_sem, r, rows):
    x, y, c = _mesh_pos()
    px, py, _ = _peer(x, y, c, 2 * r)
    src = part_ref.at[pl.ds(pl.multiple_of((2 * px + py) * rows, 16), rows), :]
    return pltpu.make_async_remote_copy(
        src_ref=src, dst_ref=land_ref.at[r - 1], send_sem=send_sem, recv_sem=recv_sem,
        device_id=(px, py, c), device_id_type=MESH)


def _scatter_order(n_w):
    return [(w, r) for r in (3, 2, 1) for w in range(n_w)]


def _scatter_start(parts, name):
    n_w = len(parts)
    rows = [p.shape[0] // N_CHIP for p in parts]
    order = _scatter_order(n_w)
    lands = [pltpu.with_memory_space_constraint(lax.empty((N_CHIP - 1, r, p.shape[1]), p.dtype), pltpu.HBM)
             for r, p in zip(rows, parts)]

    def body(*refs):
        part_refs, land_refs = refs[:n_w], refs[n_w:2 * n_w]
        sems = refs[2 * n_w:2 * n_w + 2 * len(order)]
        token = refs[-1]
        for j, (w, r) in enumerate(order):
            _scatter_copy(part_refs[w], land_refs[w], sems[2 * j], sems[2 * j + 1], r, rows[w]).start()
        token[...] = jnp.zeros_like(token)

    n_sem = 2 * len(order)
    res = pl.pallas_call(
        body, name=name,
        out_shape=(*[pltpu.SemaphoreType.DMA(())] * n_sem, *[pltpu.HBM(p.shape, p.dtype) for p in parts],
                   *[pltpu.HBM(l.shape, l.dtype) for l in lands], jax.ShapeDtypeStruct((8, LANES), F32)),
        in_specs=[HBM_SPEC] * (2 * n_w), out_specs=(*[SEM_SPEC] * n_sem, *[HBM_SPEC] * (2 * n_w), VMEM_SPEC),
        input_output_aliases={i: n_sem + i for i in range(2 * n_w)},
        compiler_params=pltpu.CompilerParams(has_side_effects=pltpu.SideEffectType.DATAFLOW_SIDE_EFFECTING),
    )(*[pltpu.with_memory_space_constraint(p, pltpu.HBM) for p in parts], *lands)
    return (list(res[:n_sem]), list(res[n_sem:n_sem + n_w]), list(res[n_sem + n_w:n_sem + 2 * n_w]), res[-1])


def _scatter_wait(sems, parts, lands, after, name):
    n_w = len(parts)
    rows = [p.shape[0] // N_CHIP for p in parts]
    order = _scatter_order(n_w)

    def body(*refs):
        part_refs, land_refs = refs[:n_w], refs[n_w:2 * n_w]
        sem_refs = refs[2 * n_w:2 * n_w + 2 * len(order)]
        for j, (w, r) in enumerate(order):
            cp = _scatter_copy(part_refs[w], land_refs[w], sem_refs[2 * j], sem_refs[2 * j + 1], r, rows[w])
            cp.wait_send()
            cp.wait_recv()

    res = pl.pallas_call(
        body, name=name,
        out_shape=(*[pltpu.HBM(p.shape, p.dtype) for p in parts], *[pltpu.HBM(l.shape, l.dtype) for l in lands]),
        in_specs=[HBM_SPEC] * (2 * n_w) + [SEM_SPEC] * len(sems) + [ANY],
        out_specs=tuple([HBM_SPEC] * (2 * n_w)),
        input_output_aliases={i: i for i in range(2 * n_w)},
        compiler_params=pltpu.CompilerParams(has_side_effects=pltpu.SideEffectType.DATAFLOW_SIDE_EFFECTING),
    )(*parts, *lands, *sems, after)
    return list(res[:n_w]), list(res[n_w:])


def _rows_of(arr_ref, rows, dev):
    start = pl.multiple_of((4 * dev[0] + 2 * dev[1] + dev[2]) * rows, 16)
    return arr_ref.at[pl.ds(start, rows), :]


def _gather_peer(k):
    x, y, c = _mesh_pos()
    return [(x, y, 1 - c), (1 - x, y, c), (x, 1 - y, c), (1 - x, 1 - y, c)][k]


def _gather_send(shard_ref, arr_ref, send_sem, recv_sem, k, rows):
    return pltpu.make_async_remote_copy(
        src_ref=shard_ref, dst_ref=_rows_of(arr_ref, rows, _mesh_pos()), send_sem=send_sem, recv_sem=recv_sem,
        device_id=_gather_peer(k), device_id_type=MESH)


def _gather_arrival(shard_ref, arr_ref, send_sem, recv_sem, k, rows):
    peer = _gather_peer(k)
    return pltpu.make_async_remote_copy(
        src_ref=shard_ref, dst_ref=_rows_of(arr_ref, rows, peer), send_sem=send_sem, recv_sem=recv_sem,
        device_id=peer, device_id_type=MESH)


GATHER_ORDER = (3, 1, 2, 0)


def _gather_start(shards, name):
    n_w = len(shards)
    rows = [s.shape[0] for s in shards]
    arrays = [pltpu.with_memory_space_constraint(lax.empty((N_DEV * s.shape[0], s.shape[1]), s.dtype), pltpu.HBM)
              for s in shards]
    n_sem = 2 * 4 * n_w

    def body(*refs):
        shard_refs, arr_refs = refs[:n_w], refs[n_w:2 * n_w]
        sems = refs[2 * n_w:2 * n_w + n_sem]
        token = refs[-1]
        for w in range(n_w):
            for k in GATHER_ORDER:
                j = 2 * (4 * w + k)
                _gather_send(shard_refs[w], arr_refs[w], sems[j], sems[j + 1], k, rows[w]).start()
        token[...] = jnp.zeros_like(token)

    res = pl.pallas_call(
        body, name=name,
        out_shape=(*[pltpu.SemaphoreType.DMA(())] * n_sem, *[pltpu.HBM(s.shape, s.dtype) for s in shards],
                   *[pltpu.HBM(a.shape, a.dtype) for a in arrays], jax.ShapeDtypeStruct((8, LANES), F32)),
        in_specs=[HBM_SPEC] * (2 * n_w), out_specs=(*[SEM_SPEC] * n_sem, *[HBM_SPEC] * (2 * n_w), VMEM_SPEC),
        input_output_aliases={i: n_sem + i for i in range(2 * n_w)},
        compiler_params=pltpu.CompilerParams(has_side_effects=pltpu.SideEffectType.DATAFLOW_SIDE_EFFECTING),
    )(*[pltpu.with_memory_space_constraint(s, pltpu.HBM) for s in shards], *arrays)
    sems = [[(res[2 * (4 * w + k)], res[2 * (4 * w + k) + 1]) for k in range(4)] for w in range(n_w)]
    return sems, list(res[n_sem:n_sem + n_w]), list(res[n_sem + n_w:n_sem + 2 * n_w]), res[-1]


def _gather_wait(sems, shards, arrays, after, name):
    n_w = len(shards)
    rows = [s.shape[0] for s in shards]
    flat = [s for per_w in sems for pair in per_w for s in pair]

    def body(*refs):
        shard_refs, arr_refs = refs[:n_w], refs[n_w:2 * n_w]
        sem_refs = refs[2 * n_w:2 * n_w + len(flat)]
        for w in range(n_w):
            for k in GATHER_ORDER:
                j = 2 * (4 * w + k)
                _gather_send(shard_refs[w], arr_refs[w], sem_refs[j], sem_refs[j + 1], k, rows[w]).wait_send()
                _gather_arrival(shard_refs[w], arr_refs[w], sem_refs[j], sem_refs[j + 1], k, rows[w]).wait_recv()

    res = pl.pallas_call(
        body, name=name,
        out_shape=(*[pltpu.HBM(s.shape, s.dtype) for s in shards], *[pltpu.HBM(a.shape, a.dtype) for a in arrays]),
        in_specs=[HBM_SPEC] * (2 * n_w) + [SEM_SPEC] * len(flat) + [ANY], out_specs=tuple([HBM_SPEC] * (2 * n_w)),
        input_output_aliases={i: i for i in range(2 * n_w)},
        compiler_params=pltpu.CompilerParams(has_side_effects=pltpu.SideEffectType.DATAFLOW_SIDE_EFFECTING),
    )(*shards, *arrays, *flat, after)
    return list(res[:n_w]), list(res[n_w:])


def _forward_carry(arrays, shards):
    n_w = len(arrays)
    rows = [s.shape[0] for s in shards]

    def plan(ins, outs, scr):
        send_sems, recv_sems, local_sems = scr
        x, y, c = _mesh_pos()
        chips = [(1 - x, y), (x, 1 - y), (1 - x, 1 - y)]
        mine = [pltpu.make_async_copy(ins[n_w + w], _rows_of(outs[w], rows[w], (x, y, c)), local_sems.at[w])
                for w in range(n_w)]

        def passed(w, j, core):
            blk = _rows_of(outs[w], rows[w], (*chips[j], core))
            return pltpu.make_async_remote_copy(
                src_ref=blk, dst_ref=blk, send_sem=send_sems.at[w, j], recv_sem=recv_sems.at[w, j],
                device_id=(x, y, 1 - c), device_id_type=MESH)

        return c, mine, passed

    def start(ins, outs, scr):
        c, mine, passed = plan(ins, outs, scr)
        for cp in mine:
            cp.start()
        for j in range(3):
            for w in range(n_w):
                passed(w, j, c).start()

    def finish(ins, outs, scr):
        c, mine, passed = plan(ins, outs, scr)
        for j in range(3):
            for w in range(n_w):
                passed(w, j, 1 - c).wait_recv()
        for j in range(3):
            for w in range(n_w):
                passed(w, j, c).wait_send()
        for cp in mine:
            cp.wait()

    return _Carry(
        list(arrays) + list(shards), [jax.ShapeDtypeStruct(a.shape, a.dtype) for a in arrays],
        [pltpu.SemaphoreType.DMA((n_w, 3)), pltpu.SemaphoreType.DMA((n_w, 3)), pltpu.SemaphoreType.DMA((n_w,))],
        start, finish, aliased=n_w)


def _small_copy(v_ref, land_ref, send_sem, recv_sem, r):
    x, y, c = _mesh_pos()
    px, py, pc = _peer(x, y, c, r)
    return pltpu.make_async_remote_copy(
        src_ref=v_ref, dst_ref=land_ref.at[4 * x + 2 * y + c], send_sem=send_sem, recv_sem=recv_sem,
        device_id=(px, py, pc), device_id_type=MESH)


def _small_gather_start(v):
    land = pltpu.with_memory_space_constraint(lax.empty((N_DEV,) + v.shape, v.dtype), pltpu.HBM)

    def body(v_ref, land_ref, *rest):
        sems, token = rest[:2 * (N_DEV - 1)], rest[-1]
        for r in range(1, N_DEV):
            _small_copy(v_ref, land_ref, sems[2 * r - 2], sems[2 * r - 1], r).start()
        token[...] = jnp.zeros_like(token)

    n_sem = 2 * (N_DEV - 1)
    res = pl.pallas_call(
        body, name="small_gather_start",
        out_shape=(*[pltpu.SemaphoreType.DMA(())] * n_sem, pltpu.HBM(v.shape, v.dtype),
                   pltpu.HBM(land.shape, land.dtype), jax.ShapeDtypeStruct((8, LANES), F32)),
        in_specs=[HBM_SPEC, HBM_SPEC], out_specs=(*[SEM_SPEC] * n_sem, HBM_SPEC, HBM_SPEC, VMEM_SPEC),
        input_output_aliases={0: n_sem, 1: n_sem + 1},
        compiler_params=pltpu.CompilerParams(has_side_effects=pltpu.SideEffectType.DATAFLOW_SIDE_EFFECTING),
    )(pltpu.with_memory_space_constraint(v, pltpu.HBM), land)
    return list(res[:n_sem]), res[n_sem], res[n_sem + 1], res[-1]


def _small_gather_wait(sems, v, land, after):
    def body(v_ref, land_ref, *rest):
        for r in range(1, N_DEV):
            cp = _small_copy(v_ref, land_ref, rest[2 * r - 2], rest[2 * r - 1], r)
            cp.wait_send()
            x, y, c = _mesh_pos()
            px, py, pc = _peer(x, y, c, r)
            pltpu.make_async_remote_copy(
                src_ref=v_ref, dst_ref=land_ref.at[4 * px + 2 * py + pc], send_sem=rest[2 * r - 2],
                recv_sem=rest[2 * r - 1], device_id=(px, py, pc), device_id_type=MESH).wait_recv()

    res = pl.pallas_call(
        body, name="small_gather_wait",
        out_shape=(pltpu.HBM(v.shape, v.dtype), pltpu.HBM(land.shape, land.dtype)),
        in_specs=[HBM_SPEC, HBM_SPEC] + [SEM_SPEC] * len(sems) + [ANY], out_specs=(HBM_SPEC, HBM_SPEC),
        input_output_aliases={0: 0, 1: 1},
        compiler_params=pltpu.CompilerParams(has_side_effects=pltpu.SideEffectType.DATAFLOW_SIDE_EFFECTING),
    )(v, land, *sems, after)
    return res[0], res[1]


def _ada_forward(c_row, w_ada, b_cols, carry):
    d = c_row.shape[1]
    wcols = w_ada.shape[1]
    ci, co = len(carry.ins), len(carry.out_shapes)

    def body(*refs):
        c_ref, w_ref, b_ref = refs[:3]
        cins = refs[3:3 + ci]
        sc_ref, mod_ref = refs[3 + ci:5 + ci]
        couts = refs[5 + ci:5 + ci + co]
        rows_ref, send_sems, recv_sems = refs[5 + ci + co:8 + ci + co]
        cscr = refs[8 + ci + co:]
        carry.start(cins, couts, cscr)
        x, y, c = _mesh_pos()
        me = 4 * x + 2 * y + c
        cv = c_ref[...]
        sc_ref[me] = cv * _sigmoid(cv)

        sends = []
        for r in range(1, N_DEV):
            px, py, pc = _peer(x, y, c, r)
            cp = pltpu.make_async_remote_copy(
                src_ref=sc_ref.at[me], dst_ref=sc_ref.at[me], send_sem=send_sems.at[0, r - 1],
                recv_sem=recv_sems.at[0, r - 1], device_id=(px, py, pc), device_id_type=MESH)
            cp.start()
            sends.append(cp)
        for r in range(1, N_DEV):
            px, py, pc = _peer(x, y, c, r)
            pid = 4 * px + 2 * py + pc
            pltpu.make_async_remote_copy(
                src_ref=sc_ref.at[pid], dst_ref=sc_ref.at[pid], send_sem=send_sems.at[0, r - 1],
                recv_sem=recv_sems.at[0, r - 1], device_id=(px, py, pc), device_id_type=MESH).wait_recv()
        for cp in sends:
            cp.wait_send()

        sc_all = jnp.concatenate([sc_ref[j] for j in range(N_DEV)], axis=0)
        rows = _dot(sc_all.astype(BF16), w_ref[...].astype(BF16)) + b_ref[...]
        for j in range(N_DEV):
            rows_ref[j] = rows[j:j + 1, :]
        mod_ref[me] = rows_ref[me]

        sends = []
        for r in range(1, N_DEV):
            px, py, pc = _peer(x, y, c, r)
            pid = 4 * px + 2 * py + pc
            cp = pltpu.make_async_remote_copy(
                src_ref=rows_ref.at[pid], dst_ref=mod_ref.at[me], send_sem=send_sems.at[1, r - 1],
                recv_sem=recv_sems.at[1, r - 1], device_id=(px, py, pc), device_id_type=MESH)
            cp.start()
            sends.append(cp)
        for r in range(1, N_DEV):
            px, py, pc = _peer(x, y, c, r)
            pid = 4 * px + 2 * py + pc
            pltpu.make_async_remote_copy(
                src_ref=rows_ref.at[pid], dst_ref=mod_ref.at[pid], send_sem=send_sems.at[1, r - 1],
                recv_sem=recv_sems.at[1, r - 1], device_id=(px, py, pc), device_id_type=MESH).wait_recv()
        for cp in sends:
            cp.wait_send()
        carry.finish(cins, couts, cscr)

    res = pl.pallas_call(
        body, name="ada_forward",
        out_shape=(jax.ShapeDtypeStruct((N_DEV, 1, d), F32), jax.ShapeDtypeStruct((N_DEV, 1, wcols), F32),
                   *carry.out_shapes),
        in_specs=[VMEM_SPEC, VMEM_SPEC, VMEM_SPEC] + [ANY] * ci, out_specs=(VMEM_SPEC, VMEM_SPEC) + (ANY,) * co,
        scratch_shapes=[pltpu.VMEM((N_DEV, 1, wcols), F32), pltpu.SemaphoreType.DMA((2, N_DEV - 1)),
                        pltpu.SemaphoreType.DMA((2, N_DEV - 1))] + carry.scratch,
        compiler_params=_params(),
    )(c_row, w_ada, b_cols, *carry.ins)
    return res[:2], res[2:]


def _all_gather_small(v):
    n = v.shape[1]

    def body(v_ref, out_ref, send_sems, recv_sems):
        x, y, c = _mesh_pos()
        me = 4 * x + 2 * y + c
        out_ref[me] = v_ref[...]
        sends = []
        for r in range(1, N_DEV):
            px, py, pc = _peer(x, y, c, r)
            cp = pltpu.make_async_remote_copy(
                src_ref=v_ref, dst_ref=out_ref.at[me], send_sem=send_sems.at[r - 1],
                recv_sem=recv_sems.at[r - 1], device_id=(px, py, pc), device_id_type=MESH)
            cp.start()
            sends.append(cp)
        for r in range(1, N_DEV):
            px, py, pc = _peer(x, y, c, r)
            pid = 4 * px + 2 * py + pc
            pltpu.make_async_remote_copy(
                src_ref=v_ref, dst_ref=out_ref.at[pid], send_sem=send_sems.at[r - 1],
                recv_sem=recv_sems.at[r - 1], device_id=(px, py, pc), device_id_type=MESH).wait_recv()
        for cp in sends:
            cp.wait_send()

    return pl.pallas_call(
        body, name="all_gather_small",
        out_shape=jax.ShapeDtypeStruct((N_DEV, 1, n), F32),
        in_specs=[VMEM_SPEC], out_specs=VMEM_SPEC,
        scratch_shapes=[pltpu.SemaphoreType.DMA((N_DEV - 1,)), pltpu.SemaphoreType.DMA((N_DEV - 1,))],
        compiler_params=_params(),
    )(v)


def _mm_nt(a, b, name, out_dtype, bias=None, carry=None):
    m, k = a.shape
    n = b.shape[0]
    tm = _pick(m, (512, 256, 128))
    tn = _pick(n, (1408, 1152, 1024, 768, 512, 256, 128))

    def body(*refs):
        acc = _dot_nt(refs[0][...], refs[1][...])
        if bias is not None:
            acc = acc + refs[2][...]
        refs[-1][...] = acc.astype(out_dtype)

    in_specs = [pl.BlockSpec((tm, k), lambda j, i: (i, 0)), pl.BlockSpec((tn, k), lambda j, i: (j, 0))]
    args = [a, b]
    if bias is not None:
        in_specs.append(pl.BlockSpec((1, tn), lambda j, i: (0, j)))
        args.append(bias)
    return _call(body, name=name, grid=(n // tn, m // tm), in_specs=in_specs,
                 out_specs=pl.BlockSpec((tm, tn), lambda j, i: (i, j)),
                 out_shape=jax.ShapeDtypeStruct((m, n), out_dtype), args=args,
                 sem=("parallel", "parallel"), carry=carry)


class _Tail:
    def __init__(self, rows, vecs, outs, fn):
        self.rows, self.vecs, self.outs, self.fn = list(rows), list(vecs), list(outs), fn


def _mm_nn(pairs, name, out_dtype, bias=None, carry=None, tail=None):
    m, k = pairs[0][0].shape
    n = pairs[0][1].shape[1]
    n_p = len(pairs)
    tm = _pick(m, (512, 256, 128))
    tk = k if n_p == 1 else _pick(k, (1408, 1152, 1024, 768, 512, 256, 128))
    nk = k // tk
    n_b = 0 if bias is None else 1
    n_r, n_v = (len(tail.rows), len(tail.vecs)) if tail else (0, 0)
    n_in = 2 * n_p + n_b + n_r + n_v
    n_main = 0 if out_dtype is None else 1

    def finish(acc, refs, first_tile):
        if bias is not None:
            acc = acc + refs[2 * n_p][...]
        outs = refs[n_in:-1]
        if n_main:
            outs[0][...] = acc.astype(out_dtype)
        if tail is None:
            return
        rows = [r[...] for r in refs[2 * n_p + n_b:2 * n_p + n_b + n_r]]
        vecs = [v[...] for v in refs[2 * n_p + n_b + n_r:n_in]]
        vals = tail.fn(acc, rows, vecs)
        for ref, val, (dtype, kind) in zip(outs[n_main:], vals, tail.outs):
            if kind == "row":
                ref[...] = val.astype(dtype)
            else:
                @pl.when(first_tile)
                def _(ref=ref):
                    ref[...] = jnp.zeros_like(ref)

                ref[...] += val

    def body(*refs):
        acc_ref = refs[-1]
        kk, i = pl.program_id(0), pl.program_id(1)
        part = _dot(refs[0][...], refs[1][...])
        for p in range(1, n_p):
            part = part + _dot(refs[2 * p][...], refs[2 * p + 1][...])
        if nk == 1:
            finish(part, refs, i == 0)
            return
        rows = pl.ds(pl.multiple_of(i * tm, tm), tm)

        @pl.when(kk == 0)
        def _():
            acc_ref[rows, :] = part

        if nk > 2:
            @pl.when((kk > 0) & (kk < nk - 1))
            def _():
                acc_ref[rows, :] += part

        @pl.when(kk == nk - 1)
        def _():
            finish(acc_ref[rows, :] + part, refs, i == 0)

    def last_only(kk, i):
        return (jnp.where(kk == nk - 1, i, 0), 0)

    row_spec = pl.BlockSpec((tm, n), last_only)
    vec_spec = pl.BlockSpec((1, n), lambda kk, i: (0, 0))
    in_specs, args = [], []
    for a, b in pairs:
        in_specs += [pl.BlockSpec((tm, tk), lambda kk, i: (i, kk)), pl.BlockSpec((tk, n), lambda kk, i: (kk, 0))]
        args += [a, b]
    if bias is not None:
        in_specs.append(vec_spec)
        args.append(bias)
    out_specs = [row_spec] * n_main
    out_shape = [jax.ShapeDtypeStruct((m, n), out_dtype)] if n_main else []
    if tail:
        in_specs += [row_spec] * n_r + [vec_spec] * n_v
        args += tail.rows + tail.vecs
        for dtype, kind in tail.outs:
            if kind == "row":
                out_specs.append(row_spec)
                out_shape.append(jax.ShapeDtypeStruct((m, n), dtype))
            else:
                width = n if kind == "sum" else 1
                out_specs.append(pl.BlockSpec((1, width), lambda kk, i: (0, 0)))
                out_shape.append(jax.ShapeDtypeStruct((1, width), dtype))
    if tail is None:
        out_specs, out_shape = out_specs[0], out_shape[0]
    return _call(body, name=name, grid=(nk, m // tm), in_specs=in_specs, out_specs=out_specs,
                 out_shape=out_shape, args=args,
                 scratch=[pltpu.VMEM((m, n) if nk > 1 else (8, LANES), F32)],
                 sem=("arbitrary", "arbitrary"), carry=carry)


def _rms(v):
    return lax.rsqrt(jnp.mean(v * v, axis=-1, keepdims=True) + EPS)


def _col(v):
    return jnp.sum(v, axis=0, keepdims=True)


def _tail_post_pre(x, g_post, gate, weight, g_pre, scale, shift):
    def fn(y, rows, vecs):
        (xv,), (gp, gt, g, sc, sh) = rows, vecs
        xo = xv + (weight * gt) * ((y * _rms(y)) * gp)
        return xo, ((xo * _rms(xo)) * g) * (1.0 + sc) + sh

    return _Tail([x], [g_post, gate, g_pre, scale, shift], [(F32, "row"), (BF16, "row")], fn)


def _tail_post_loss(x, target, g, gate, weight):
    def fn(y, rows, vecs):
        (xv, tv), (gv, gt) = rows, vecs
        r = _rms(y)
        yn = y * r
        err = (xv + (weight * gt) * (yn * gv)) - tv
        do = err * (1.0 / y.shape[1])
        dyn = do * ((weight * gt) * gv)
        dy = r * (dyn - yn * jnp.mean(dyn * yn, axis=-1, keepdims=True))
        return do, dy, 0.5 * _col(jnp.mean(err * err, axis=-1, keepdims=True)), _col(do * yn)

    return _Tail([x, target], [g, gate], [(F32, "row"), (BF16, "row"), (F32, "one"), (F32, "sum")], fn)


def _tail_pre_bwd(x, dres, g_pre, scale):
    def fn(dh, rows, vecs):
        (xv, dr), (g, sc) = rows, vecs
        r = _rms(xv)
        n = xv * r
        dn = dh * (g * (1.0 + sc))
        return dr + r * (dn - n * jnp.mean(dn * n, axis=-1, keepdims=True)), _col(dh * n), _col(dh)

    return _Tail([x, dres], [g_pre, scale], [(F32, "row"), (F32, "sum"), (F32, "sum")], fn)


def _tail_pre_post_bwd(x, dres, y, g_pre, scale, g_post, gate, weight):
    def fn(dh, rows, vecs):
        (xv, dr, yv), (g, sc, gp, gt) = rows, vecs
        r = _rms(xv)
        n = xv * r
        dn = dh * (g * (1.0 + sc))
        dx = dr + r * (dn - n * jnp.mean(dn * n, axis=-1, keepdims=True))
        ry = _rms(yv)
        yn = yv * ry
        dyn = dx * ((weight * gt) * gp)
        dy = ry * (dyn - yn * jnp.mean(dyn * yn, axis=-1, keepdims=True))
        return dx, dy, _col(dh * n), _col(dh), _col(dx * yn), _col(dy)

    return _Tail([x, dres, y], [g_pre, scale, g_post, gate],
                 [(F32, "row"), (BF16, "row")] + [(F32, "sum")] * 4, fn)


def _mm_tn(a, b, name, out_dtype=BF16, carry=None):
    k, m = a.shape
    n = b.shape[1]
    tm = _pick(m, (1408, 1152, 1024, 768, 512, 256, 128))
    tk = _pick(k, (512, 256, 128))
    nk = k // tk

    def body(a_ref, b_ref, o_ref, acc_ref):
        kk = pl.program_id(1)

        @pl.when(kk == 0)
        def _():
            acc_ref[...] = jnp.zeros_like(acc_ref)

        acc_ref[...] += _dot_tn(a_ref[...], b_ref[...])

        @pl.when(kk == nk - 1)
        def _():
            o_ref[...] = acc_ref[...].astype(out_dtype)

    return _call(body, name=name, grid=(m // tm, nk),
                 in_specs=[pl.BlockSpec((tk, tm), lambda i, kk: (kk, i)), pl.BlockSpec((tk, n), lambda i, kk: (kk, 0))],
                 out_specs=pl.BlockSpec((tm, n), lambda i, kk: (i, 0)),
                 out_shape=jax.ShapeDtypeStruct((m, n), out_dtype), args=[a, b],
                 scratch=[pltpu.VMEM((tm, n), F32)], sem=("parallel", "arbitrary"), carry=carry)


def _mm_tn_pair(a, b, name, carry=None):
    k, m = a.shape
    n = b.shape[1]
    rows = m // N_DEV
    n_chip = N_DEV // 2
    tm = 4 * rows
    tk = _pick(k, (1024, 512, 256, 128))
    nk = k // tk

    def body(a_ref, b_ref, p_ref, own_ref, acc_ref, keep_ref, send_ref, land_ref, send_sems, recv_sems):
        i, kk = pl.program_id(0), pl.program_id(1)
        x, y, c = _mesh_pos()

        def push(chip):
            return pltpu.make_async_remote_copy(
                src_ref=send_ref.at[chip], dst_ref=land_ref.at[chip], send_sem=send_sems.at[chip],
                recv_sem=recv_sems.at[chip], device_id=(x, y, 1 - c), device_id_type=MESH)

        if nk == 1:
            acc = _dot_tn(a_ref[...], b_ref[...])
        else:
            @pl.when(kk == 0)
            def _():
                acc_ref[...] = jnp.zeros_like(acc_ref)

            acc_ref[...] += _dot_tn(a_ref[...], b_ref[...])
            acc = acc_ref

        for t in range(2):
            @pl.when((kk == nk - 1) & (i == t))
            def _(t=t):
                for ob in range(4):
                    chip, core = 2 * t + ob // 2, ob % 2
                    blk = acc[ob * rows:(ob + 1) * rows, :]

                    @pl.when(c == core)
                    def _(chip=chip, blk=blk):
                        keep_ref[chip] = blk

                    @pl.when(c != core)
                    def _(chip=chip, blk=blk):
                        send_ref[chip] = blk.astype(BF16)
                        push(chip).start()

        @pl.when((kk == nk - 1) & (i == 1))
        def _():
            for chip in range(n_chip):
                push(chip).wait_recv()
                val = (keep_ref[chip] + land_ref[chip].astype(F32)).astype(BF16)
                p_ref[chip * rows:(chip + 1) * rows, :] = val

                @pl.when(2 * x + y == chip)
                def _(val=val):
                    own_ref[...] = val

            for chip in range(n_chip):
                push(chip).wait_send()

    return _call(body, name=name, grid=(2, nk),
                 in_specs=[pl.BlockSpec((tk, tm), lambda i, kk: (kk, i)), pl.BlockSpec((tk, n), lambda i, kk: (kk, 0))],
                 out_specs=(pl.BlockSpec((n_chip * rows, n), lambda i, kk: (0, 0)),
                            pl.BlockSpec((rows, n), lambda i, kk: (0, 0))),
                 out_shape=(jax.ShapeDtypeStruct((n_chip * rows, n), BF16), jax.ShapeDtypeStruct((rows, n), BF16)),
                 args=[a, b],
                 scratch=[pltpu.VMEM((tm, n) if nk > 1 else (8, LANES), F32), pltpu.VMEM((n_chip, rows, n), F32),
                          pltpu.VMEM((n_chip, rows, n), BF16), pltpu.VMEM((n_chip, rows, n), BF16),
                          pltpu.SemaphoreType.DMA((n_chip,)), pltpu.SemaphoreType.DMA((n_chip,))],
                 sem=("arbitrary", "arbitrary"), carry=carry)


def _ffn_up(h, wg_t, wu_t, name, carry=None):
    s, d = h.shape
    f = wg_t.shape[0]
    tm = _pick(s, (512, 256, 128))
    tf = _pick(f, (1408, 1024, 512, 256, 128))

    def body(h_ref, wg_ref, wu_ref, a_ref, b_ref, u_ref):
        hh = h_ref[...]
        for lo, hi in _pieces(tf):
            a = _dot_nt(hh, wg_ref[lo:hi, :])
            b = _dot_nt(hh, wu_ref[lo:hi, :])
            a_ref[:, lo:hi] = a.astype(BF16)
            b_ref[:, lo:hi] = b.astype(BF16)
            u_ref[:, lo:hi] = ((a * _sigmoid(a)) * b).astype(BF16)

    w_spec = pl.BlockSpec((tf, d), lambda j, i: (j, 0))
    o_spec = pl.BlockSpec((tm, tf), lambda j, i: (i, j))
    o_shape = jax.ShapeDtypeStruct((s, f), BF16)
    return _call(body, name=name, grid=(f // tf, s // tm),
                 in_specs=[pl.BlockSpec((tm, d), lambda j, i: (i, 0)), w_spec, w_spec],
                 out_specs=(o_spec, o_spec, o_spec), out_shape=(o_shape, o_shape, o_shape),
                 args=[h, wg_t, wu_t], sem=("parallel", "parallel"), carry=carry)


def _ffn_down_bwd(dy, wd, a, b, name, carry=None):
    s, d = dy.shape
    f = wd.shape[0]
    tm = _pick(s, (512, 256, 128))
    tf = _pick(f, (1408, 1024, 512, 256, 128))

    def body(dy_ref, wd_ref, a_ref, b_ref, da_ref, db_ref):
        dyv = dy_ref[...]
        for lo, hi in _pieces(tf):
            du = _dot_nt(dyv, wd_ref[lo:hi, :])
            a = a_ref[:, lo:hi].astype(F32)
            b = b_ref[:, lo:hi].astype(F32)
            sig = _sigmoid(a)
            da_ref[:, lo:hi] = (du * b * (sig * (1.0 + a * (1.0 - sig)))).astype(BF16)
            db_ref[:, lo:hi] = (du * (a * sig)).astype(BF16)

    t_spec = pl.BlockSpec((tm, tf), lambda j, i: (i, j))
    o_shape = jax.ShapeDtypeStruct((s, f), BF16)
    return _call(body, name=name, grid=(f // tf, s // tm),
                 in_specs=[pl.BlockSpec((tm, d), lambda j, i: (i, 0)), pl.BlockSpec((tf, d), lambda j, i: (j, 0)),
                           t_spec, t_spec],
                 out_specs=(t_spec, t_spec), out_shape=(o_shape, o_shape), args=[dy, wd, a, b],
                 sem=("parallel", "parallel"), carry=carry)


def _row_tile(s):
    return _pick(s, (256, 128, 64))


def _vec_spec(d):
    return pl.BlockSpec((1, d), lambda i: (0, 0))


def _pre_norm(x, g, scale, shift, name):
    s, d = x.shape
    ts = _row_tile(s)

    def body(x_ref, g_ref, sc_ref, sh_ref, h_ref):
        xv = x_ref[...]
        r = lax.rsqrt(jnp.mean(xv * xv, axis=-1, keepdims=True) + EPS)
        h_ref[...] = (((xv * r) * g_ref[...]) * (1.0 + sc_ref[...]) + sh_ref[...]).astype(BF16)

    row = pl.BlockSpec((ts, d), lambda i: (i, 0))
    return _call(body, name=name, grid=(s // ts,), in_specs=[row, _vec_spec(d), _vec_spec(d), _vec_spec(d)],
                 out_specs=row, out_shape=jax.ShapeDtypeStruct((s, d), BF16), args=[x, g, scale, shift],
                 sem=("parallel",))


def _post_norm_residual(x, y, g, gate, weight, name):
    s, d = x.shape
    ts = _row_tile(s)

    def body(x_ref, y_ref, g_ref, gate_ref, o_ref):
        yv = y_ref[...]
        r = lax.rsqrt(jnp.mean(yv * yv, axis=-1, keepdims=True) + EPS)
        o_ref[...] = x_ref[...] + (weight * gate_ref[...]) * ((yv * r) * g_ref[...])

    row = pl.BlockSpec((ts, d), lambda i: (i, 0))
    return _call(body, name=name, grid=(s // ts,), in_specs=[row, row, _vec_spec(d), _vec_spec(d)],
                 out_specs=row, out_shape=jax.ShapeDtypeStruct((s, d), F32), args=[x, y, g, gate],
                 sem=("parallel",))


def _post_norm_bwd(dout, y, g, gate, weight, name):
    s, d = y.shape
    ts = _row_tile(s)

    def body(do_ref, y_ref, g_ref, gate_ref, dy_ref, s1_ref, cs_ref):
        @pl.when(pl.program_id(0) == 0)
        def _():
            s1_ref[...] = jnp.zeros_like(s1_ref)
            cs_ref[...] = jnp.zeros_like(cs_ref)

        yv = y_ref[...]
        do = do_ref[...]
        r = lax.rsqrt(jnp.mean(yv * yv, axis=-1, keepdims=True) + EPS)
        yn = yv * r
        dyn = do * ((weight * gate_ref[...]) * g_ref[...])
        dy = r * (dyn - yn * jnp.mean(dyn * yn, axis=-1, keepdims=True))
        dy_ref[...] = dy.astype(BF16)
        s1_ref[...] += jnp.sum(do * yn, axis=0, keepdims=True)
        cs_ref[...] += jnp.sum(dy, axis=0, keepdims=True)

    row = pl.BlockSpec((ts, d), lambda i: (i, 0))
    vec = jax.ShapeDtypeStruct((1, d), F32)
    return _call(body, name=name, grid=(s // ts,), in_specs=[row, row, _vec_spec(d), _vec_spec(d)],
                 out_specs=(row, _vec_spec(d), _vec_spec(d)),
                 out_shape=(jax.ShapeDtypeStruct((s, d), BF16), vec, vec), args=[dout, y, g, gate],
                 sem=("arbitrary",))


def _pre_norm_bwd(dh, x, g, scale, dres, name):
    s, d = x.shape
    ts = _row_tile(s)

    def body(dh_ref, x_ref, g_ref, sc_ref, dr_ref, dx_ref, s2_ref, s3_ref):
        @pl.when(pl.program_id(0) == 0)
        def _():
            s2_ref[...] = jnp.zeros_like(s2_ref)
            s3_ref[...] = jnp.zeros_like(s3_ref)

        xv = x_ref[...]
        dh = dh_ref[...]
        r = lax.rsqrt(jnp.mean(xv * xv, axis=-1, keepdims=True) + EPS)
        n = xv * r
        dn = dh * (g_ref[...] * (1.0 + sc_ref[...]))
        dx_ref[...] = dr_ref[...] + r * (dn - n * jnp.mean(dn * n, axis=-1, keepdims=True))
        s2_ref[...] += jnp.sum(dh * n, axis=0, keepdims=True)
        s3_ref[...] += jnp.sum(dh, axis=0, keepdims=True)

    row = pl.BlockSpec((ts, d), lambda i: (i, 0))
    vec = jax.ShapeDtypeStruct((1, d), F32)
    return _call(body, name=name, grid=(s // ts,), in_specs=[row, row, _vec_spec(d), _vec_spec(d), row],
                 out_specs=(row, _vec_spec(d), _vec_spec(d)),
                 out_shape=(jax.ShapeDtypeStruct((s, d), F32), vec, vec), args=[dh, x, g, scale, dres],
                 sem=("arbitrary",))


def _post_pre_norm(x, y, g_post, gate, weight, g_pre, scale, shift, name):
    s, d = x.shape
    ts = _row_tile(s)

    def body(x_ref, y_ref, gp_ref, gate_ref, g_ref, sc_ref, sh_ref, o_ref, h_ref):
        yv = y_ref[...]
        r = lax.rsqrt(jnp.mean(yv * yv, axis=-1, keepdims=True) + EPS)
        xv = x_ref[...] + (weight * gate_ref[...]) * ((yv * r) * gp_ref[...])
        o_ref[...] = xv
        r2 = lax.rsqrt(jnp.mean(xv * xv, axis=-1, keepdims=True) + EPS)
        h_ref[...] = (((xv * r2) * g_ref[...]) * (1.0 + sc_ref[...]) + sh_ref[...]).astype(BF16)

    row = pl.BlockSpec((ts, d), lambda i: (i, 0))
    return _call(body, name=name, grid=(s // ts,), in_specs=[row, row] + [_vec_spec(d)] * 5,
                 out_specs=(row, row),
                 out_shape=(jax.ShapeDtypeStruct((s, d), F32), jax.ShapeDtypeStruct((s, d), BF16)),
                 args=[x, y, g_post, gate, g_pre, scale, shift], sem=("parallel",))


def _post_norm_loss_bwd(x, y, g, gate, weight, target, name):
    s, d = y.shape
    ts = _row_tile(s)

    def body(x_ref, y_ref, g_ref, gate_ref, t_ref, dx_ref, dy_ref, l_ref, s1_ref):
        @pl.when(pl.program_id(0) == 0)
        def _():
            l_ref[...] = jnp.zeros_like(l_ref)
            s1_ref[...] = jnp.zeros_like(s1_ref)

        yv = y_ref[...]
        r = lax.rsqrt(jnp.mean(yv * yv, axis=-1, keepdims=True) + EPS)
        yn = yv * r
        err = (x_ref[...] + (weight * gate_ref[...]) * (yn * g_ref[...])) - t_ref[...]
        do = err * (1.0 / d)
        dx_ref[...] = do
        l_ref[...] += 0.5 * jnp.sum(jnp.mean(err * err, axis=-1, keepdims=True), axis=0, keepdims=True)
        dyn = do * ((weight * gate_ref[...]) * g_ref[...])
        dy_ref[...] = (r * (dyn - yn * jnp.mean(dyn * yn, axis=-1, keepdims=True))).astype(BF16)
        s1_ref[...] += jnp.sum(do * yn, axis=0, keepdims=True)

    row = pl.BlockSpec((ts, d), lambda i: (i, 0))
    return _call(body, name=name, grid=(s // ts,), in_specs=[row, row, _vec_spec(d), _vec_spec(d), row],
                 out_specs=(row, row, pl.BlockSpec((1, 1), lambda i: (0, 0)), _vec_spec(d)),
                 out_shape=(jax.ShapeDtypeStruct((s, d), F32), jax.ShapeDtypeStruct((s, d), BF16),
                            jax.ShapeDtypeStruct((1, 1), F32), jax.ShapeDtypeStruct((1, d), F32)),
                 args=[x, y, g, gate, target], sem=("arbitrary",))


def _pre_post_norm_bwd(dh, x, g_pre, scale, dres, y, g_post, gate, weight, name):
    s, d = x.shape
    ts = _row_tile(s)

    def body(dh_ref, x_ref, g_ref, sc_ref, dr_ref, y_ref, gp_ref, gate_ref,
             dx_ref, dy_ref, s2_ref, s3_ref, s1_ref, cs_ref):
        @pl.when(pl.program_id(0) == 0)
        def _():
            for ref in (s2_ref, s3_ref, s1_ref, cs_ref):
                ref[...] = jnp.zeros_like(ref)

        xv = x_ref[...]
        dh = dh_ref[...]
        r = lax.rsqrt(jnp.mean(xv * xv, axis=-1, keepdims=True) + EPS)
        n = xv * r
        dn = dh * (g_ref[...] * (1.0 + sc_ref[...]))
        dx = dr_ref[...] + r * (dn - n * jnp.mean(dn * n, axis=-1, keepdims=True))
        dx_ref[...] = dx
        s2_ref[...] += jnp.sum(dh * n, axis=0, keepdims=True)
        s3_ref[...] += jnp.sum(dh, axis=0, keepdims=True)
        yv = y_ref[...]
        ry = lax.rsqrt(jnp.mean(yv * yv, axis=-1, keepdims=True) + EPS)
        yn = yv * ry
        dyn = dx * ((weight * gate_ref[...]) * gp_ref[...])
        dy = ry * (dyn - yn * jnp.mean(dyn * yn, axis=-1, keepdims=True))
        dy_ref[...] = dy.astype(BF16)
        s1_ref[...] += jnp.sum(dx * yn, axis=0, keepdims=True)
        cs_ref[...] += jnp.sum(dy, axis=0, keepdims=True)

    row = pl.BlockSpec((ts, d), lambda i: (i, 0))
    vec = jax.ShapeDtypeStruct((1, d), F32)
    return _call(body, name=name, grid=(s // ts,),
                 in_specs=[row, row, _vec_spec(d), _vec_spec(d), row, row, _vec_spec(d), _vec_spec(d)],
                 out_specs=(row, row) + (_vec_spec(d),) * 4,
                 out_shape=(jax.ShapeDtypeStruct((s, d), F32), jax.ShapeDtypeStruct((s, d), BF16), vec, vec, vec, vec),
                 args=[dh, x, g_pre, scale, dres, y, g_post, gate], sem=("arbitrary",))


def _group_norm_cat(oa, ob, ga, gb):
    s = oa.shape[0]
    ts = _row_tile(s)

    def body(oa_ref, ob_ref, ga_ref, gb_ref, y_ref):
        for o_ref, g_ref, lo, w in ((oa_ref, ga_ref, 0, QA), (ob_ref, gb_ref, QA, QB)):
            ov = o_ref[...]
            r = lax.rsqrt(jnp.mean(ov * ov, axis=-1, keepdims=True) + EPS)
            y_ref[:, lo:lo + w] = ((ov * r) * g_ref[...]).astype(BF16)

    return _call(body, name="group_norm_cat", grid=(s // ts,),
                 in_specs=[pl.BlockSpec((ts, QA), lambda i: (i, 0)), pl.BlockSpec((ts, QB), lambda i: (i, 0)),
                           _vec_spec(QA), _vec_spec(QB)],
                 out_specs=pl.BlockSpec((ts, QA + QB), lambda i: (i, 0)),
                 out_shape=jax.ShapeDtypeStruct((s, QA + QB), BF16), args=[oa, ob, ga, gb], sem=("parallel",))


def _group_norm_bwd(dy, oa, ob, ga, gb):
    s = oa.shape[0]
    ts = _row_tile(s)

    def body(dy_ref, oa_ref, ob_ref, ga_ref, gb_ref, doa_ref, dob_ref, dga_ref, dgb_ref):
        @pl.when(pl.program_id(0) == 0)
        def _():
            dga_ref[...] = jnp.zeros_like(dga_ref)
            dgb_ref[...] = jnp.zeros_like(dgb_ref)

        for o_ref, g_ref, do_ref, dg_ref, lo, w in ((oa_ref, ga_ref, doa_ref, dga_ref, 0, QA),
                                                    (ob_ref, gb_ref, dob_ref, dgb_ref, QA, QB)):
            ov = o_ref[...]
            dyv = dy_ref[:, lo:lo + w]
            r = lax.rsqrt(jnp.mean(ov * ov, axis=-1, keepdims=True) + EPS)
            n = ov * r
            dn = dyv * g_ref[...]
            do_ref[...] = r * (dn - n * jnp.mean(dn * n, axis=-1, keepdims=True))
            dg_ref[...] += jnp.sum(dyv * n, axis=0, keepdims=True)

    ra = pl.BlockSpec((ts, QA), lambda i: (i, 0))
    rb = pl.BlockSpec((ts, QB), lambda i: (i, 0))
    return _call(body, name="group_norm_bwd", grid=(s // ts,),
                 in_specs=[pl.BlockSpec((ts, QA + QB), lambda i: (i, 0)), ra, rb, _vec_spec(QA), _vec_spec(QB)],
                 out_specs=(ra, rb, _vec_spec(QA), _vec_spec(QB)),
                 out_shape=(jax.ShapeDtypeStruct((s, QA), F32), jax.ShapeDtypeStruct((s, QB), F32),
                            jax.ShapeDtypeStruct((1, QA), F32), jax.ShapeDtypeStruct((1, QB), F32)),
                 args=[dy, oa, ob, ga, gb], sem=("arbitrary",))


def _loss_and_grad(y, target):
    s, d = y.shape
    ts = _row_tile(s)

    def body(y_ref, t_ref, l_ref, g_ref):
        @pl.when(pl.program_id(0) == 0)
        def _():
            l_ref[...] = jnp.zeros_like(l_ref)

        err = y_ref[...] - t_ref[...]
        g_ref[...] = err * (1.0 / d)
        row = jnp.mean(err * err, axis=-1, keepdims=True)
        l_ref[...] += 0.5 * jnp.sum(row, axis=0, keepdims=True)

    row = pl.BlockSpec((ts, d), lambda i: (i, 0))
    return _call(body, name="loss_and_grad", grid=(s // ts,), in_specs=[row, row],
                 out_specs=(pl.BlockSpec((1, 1), lambda i: (0, 0)), row),
                 out_shape=(jax.ShapeDtypeStruct((1, 1), F32), jax.ShapeDtypeStruct((s, d), F32)),
                 args=[y, target], sem=("arbitrary",))


def _col_sum(x, name):
    s, n = x.shape
    ts = _row_tile(s)

    def body(x_ref, o_ref):
        @pl.when(pl.program_id(0) == 0)
        def _():
            o_ref[...] = jnp.zeros_like(o_ref)

        o_ref[...] += jnp.sum(x_ref[...].astype(F32), axis=0, keepdims=True)

    return _call(body, name=name, grid=(s // ts,), in_specs=[pl.BlockSpec((ts, n), lambda i: (i, 0))],
                 out_specs=pl.BlockSpec((1, n), lambda i: (0, 0)), out_shape=jax.ShapeDtypeStruct((1, n), F32),
                 args=[x], sem=("arbitrary",))


def _n_variants(n_back):
    return -(-n_back // QG) + 1


def _alibi_bias():
    i = np.arange(QROWS)[:, None]
    j = np.arange((QG + BACK_A) * CHUNK)[None, :]
    dist = np.abs(BACK_A * CHUNK + i - j).astype(np.float32)
    dc = j // CHUNK - i // CHUNK
    valid = (dc >= 0) & (dc <= BACK_A)
    slopes = np.array([2.0 ** (-8.0 * (h + 1) / H_A) for h in range(H_A)], dtype=np.float32)
    bias = -slopes[:, None, None] * dist[None]
    out = [np.where((valid & (j >= (BACK_A - QG * v) * CHUNK))[None], bias, np.float32(NEG_INF))
           for v in range(_n_variants(BACK_A))]
    return jnp.asarray(np.stack(out).astype(np.float32))


def _rel_index_matrix():
    cc = np.arange(SKEW)
    dist = np.where(cc < SKEW - QROWS, BACK_B * CHUNK - cc, BACK_B * CHUNK + SKEW - cc)
    idx = np.clip(dist, -REL_CLIP, REL_CLIP) + REL_CLIP
    m = np.zeros((SKEW, N_REL), np.float32)
    m[cc, idx] = 1.0
    return jnp.asarray(m)


def _toeplitz_bias(vec, carry=None):
    lk = (QG + BACK_B) * CHUNK
    nv = _n_variants(BACK_B)

    def body(v_ref, o_ref):
        xv = jnp.broadcast_to(v_ref[0], (QROWS, SKEW))
        row = lax.broadcasted_iota(jnp.int32, (QROWS, SKEW), 0)
        for bit in range(QROWS.bit_length() - 1):
            xv = jnp.where((row >> bit) & 1 == 1, pltpu.roll(xv, 1 << bit, 1), xv)
        ri = lax.broadcasted_iota(jnp.int32, (QROWS, lk), 0) // CHUNK
        col = lax.broadcasted_iota(jnp.int32, (QROWS, lk), 1)
        ci = col // CHUNK
        valid = (ci - ri >= 0) & (ci - ri <= BACK_B)
        for v in range(nv):
            o_ref[v, 0] = jnp.where(valid & (col >= (BACK_B - QG * v) * CHUNK), xv[:, :lk], NEG_INF)

    return _call(body, name="toeplitz_bias", grid=(H_B,),
                 in_specs=[pl.BlockSpec((1, 1, SKEW), lambda h: (h, 0, 0))],
                 out_specs=pl.BlockSpec((nv, 1, QROWS, lk), lambda h: (0, h, 0, 0)),
                 out_shape=jax.ShapeDtypeStruct((nv, H_B, QROWS, lk), F32), args=[vec], sem=("parallel",),
                 carry=carry)


def _diagonal_sums(dbias):
    lk = dbias.shape[2]

    def body(d_ref, o_ref):
        xp = jnp.concatenate([d_ref[0], jnp.zeros((QROWS, SKEW - lk), F32)], axis=1)
        xv = xp[0:CHUNK]
        for q in range(1, QG):
            xv = xv + pltpu.roll(xp[q * CHUNK:(q + 1) * CHUNK], SKEW - q * CHUNK, 1)
        row = lax.broadcasted_iota(jnp.int32, (CHUNK, SKEW), 0)
        for bit in range(CHUNK.bit_length() - 1):
            xv = jnp.where((row >> bit) & 1 == 1, pltpu.roll(xv, SKEW - (1 << bit), 1), xv)
        o_ref[0] = jnp.sum(xv, axis=0, keepdims=True)

    return _call(body, name="diagonal_sums", grid=(H_B,),
                 in_specs=[pl.BlockSpec((1, QROWS, lk), lambda h: (h, 0, 0))],
                 out_specs=pl.BlockSpec((1, 1, SKEW), lambda h: (h, 0, 0)),
                 out_shape=jax.ShapeDtypeStruct((H_B, 1, SKEW), F32), args=[dbias], sem=("parallel",))


def _attn_common(s, n_back, gqa, q_col, k_col, v_col):
    lk = (QG + n_back) * CHUNK
    pad = n_back * CHUNK
    wide = TPS * LANES
    q_spec = pl.BlockSpec((QROWS, wide), lambda t, g: (g, q_col // TPS + t))
    if gqa:
        k_spec = pl.BlockSpec((s, LANES), lambda t, g: (0, k_col))
        v_spec = pl.BlockSpec((s, LANES), lambda t, g: (0, v_col))
    else:
        k_spec = pl.BlockSpec((s, wide), lambda t, g: (0, k_col // TPS + t))
        v_spec = pl.BlockSpec((s, wide), lambda t, g: (0, v_col // TPS + t))
    last_variant = _n_variants(n_back) - 1
    bias_spec = pl.BlockSpec((None, 2 * TPS, QROWS, lk), lambda t, g: (jnp.minimum(g, last_variant), t, 0, 0))
    tile_spec = pl.BlockSpec((QROWS, wide), lambda t, g: (g, t))
    return lk, pad, q_spec, k_spec, v_spec, bias_spec, tile_spec


def _attention_fwd(proj, bias, sinks, *, n_back, gqa, q_col, k_col, v_col, name, carry=None):
    s = proj.shape[0]
    lk, pad, q_spec, k_spec, v_spec, bias_spec, tile_spec = _attn_common(s, n_back, gqa, q_col, k_col, v_col)
    n_t, n_g = 512 // (TPS * LANES), s // QROWS
    kv_wide = LANES if gqa else TPS * LANES

    def body(*refs):
        if gqa:
            q_ref, k_ref, v_ref, bias_ref, sink_ref, o_ref, l_ref, kpad, vpad = refs
        else:
            q_ref, k_ref, v_ref, bias_ref, o_ref, l_ref, kpad, vpad = refs
        t, g = pl.program_id(0), pl.program_id(1)

        @pl.when(g == 0)
        def _():
            kpad[0:pad, :] = jnp.zeros((pad, kv_wide), BF16)
            vpad[0:pad, :] = jnp.zeros((pad, kv_wide), BF16)
            kpad[pad:, :] = k_ref[...]
            vpad[pad:, :] = v_ref[...]

        start = pl.multiple_of(g * QROWS, QROWS)
        half = lax.broadcasted_iota(jnp.int32, (QROWS, LANES), 1) // HEAD_DIM
        for tt in range(TPS):
            lanes = slice(tt * LANES, (tt + 1) * LANES)
            kv_lanes = slice(0, LANES) if gqa else lanes
            kb = kpad[pl.ds(start, lk), kv_lanes]
            vb = vpad[pl.ds(start, lk), kv_lanes]
            q = q_ref[:, lanes] * (HEAD_DIM ** -0.5)
            if gqa:
                hk = (TPS * t + tt) // 2
                q_rolled = pltpu.roll(q.astype(F32), HEAD_DIM, 1).astype(BF16)
            outs, lses = [], []
            for e in range(2):
                if gqa:
                    kv_half = hk
                    src = jnp.where(hk == e, q, q_rolled)
                else:
                    kv_half = e
                    src = q
                qm = jnp.where(half == kv_half, src, jnp.zeros_like(src))
                sc = _dot_nt(qm, kb) + bias_ref[2 * tt + e]
                m = jnp.max(sc, axis=-1, keepdims=True)
                if gqa:
                    sk = sink_ref[2 * (TPS * t + tt) + e]
                    m = jnp.maximum(m, sk)
                p = jnp.exp(sc - m)
                l = jnp.sum(p, axis=-1, keepdims=True)
                if gqa:
                    l = l + jnp.exp(sk - m)
                pn = p / l
                outs.append(_dot(pn.astype(BF16), vb))
                lses.append(m + jnp.log(l))
            if gqa:
                same = jnp.where(hk == 0, outs[0], outs[1])
                other = jnp.where(hk == 0, outs[1], outs[0])
                o_ref[:, lanes] = jnp.where(half == hk, same, pltpu.roll(other, HEAD_DIM, 1))
            else:
                o_ref[:, lanes] = jnp.where(half == 0, outs[0], outs[1])
            l_ref[:, lanes] = jnp.where(half == 0, lses[0], lses[1])

    in_specs = [q_spec, k_spec, v_spec, bias_spec] + ([SMEM_SPEC] if gqa else [])
    args = [proj, proj, proj, bias] + ([sinks] if gqa else [])
    o_shape = jax.ShapeDtypeStruct((s, 512), F32)
    return _call(body, name=name, grid=(n_t, n_g), in_specs=in_specs, out_specs=(tile_spec, tile_spec),
                 out_shape=(o_shape, o_shape), args=args,
                 scratch=[pltpu.VMEM((s + pad, kv_wide), BF16), pltpu.VMEM((s + pad, kv_wide), BF16)],
                 sem=("arbitrary", "arbitrary"), carry=carry)


def _attention_bwd(proj, bias, sinks, do, lse, *, n_back, gqa, q_col, k_col, v_col, name, carry=None):
    s = proj.shape[0]
    lk, pad, q_spec, k_spec, v_spec, bias_spec, tile_spec = _attn_common(s, n_back, gqa, q_col, k_col, v_col)
    n_t, n_g = 512 // (TPS * LANES), s // QROWS
    kv_wide = LANES if gqa else TPS * LANES

    def body(*refs):
        if gqa:
            (q_ref, k_ref, v_ref, bias_ref, sink_ref, do_ref, l_ref,
             dq_ref, dk_ref, dv_ref, dsink_ref, kpad, vpad, dkpad, dvpad) = refs
        else:
            (q_ref, k_ref, v_ref, bias_ref, do_ref, l_ref,
             dq_ref, dk_ref, dv_ref, dbias_ref, kpad, vpad, dkpad, dvpad) = refs
        t, g = pl.program_id(0), pl.program_id(1)

        @pl.when(g == 0)
        def _():
            kpad[0:pad, :] = jnp.zeros((pad, kv_wide), BF16)
            vpad[0:pad, :] = jnp.zeros((pad, kv_wide), BF16)
            kpad[pad:, :] = k_ref[...]
            vpad[pad:, :] = v_ref[...]
            if gqa:
                dsink_ref[...] = jnp.zeros_like(dsink_ref)
            else:
                dbias_ref[...] = jnp.zeros_like(dbias_ref)

        @pl.when((g == 0) & (t == 0) if gqa else g == 0)
        def _():
            dkpad[...] = jnp.zeros_like(dkpad)
            dvpad[...] = jnp.zeros_like(dvpad)

        start = pl.multiple_of(g * QROWS, QROWS)
        half = lax.broadcasted_iota(jnp.int32, (QROWS, LANES), 1) // HEAD_DIM
        for tt in range(TPS):
            lanes = slice(tt * LANES, (tt + 1) * LANES)
            kv_lanes = slice(0, LANES) if gqa else lanes
            kb = kpad[pl.ds(start, lk), kv_lanes]
            vb = vpad[pl.ds(start, lk), kv_lanes]
            q = q_ref[:, lanes]
            dov = do_ref[:, lanes]
            lv = l_ref[:, lanes]
            if gqa:
                hk = (TPS * t + tt) // 2
                q_rolled = pltpu.roll(q.astype(F32), HEAD_DIM, 1).astype(BF16)
                do_rolled = pltpu.roll(dov, HEAD_DIM, 1)
            dqs = []
            dk_acc = jnp.zeros((lk, LANES), F32)
            dv_acc = jnp.zeros((lk, LANES), F32)
            for e in range(2):
                if gqa:
                    kv_half = hk
                    src = jnp.where(hk == e, q, q_rolled)
                    do_src = jnp.where(hk == e, dov, do_rolled)
                else:
                    kv_half = e
                    src = q
                    do_src = dov
                qm = jnp.where(half == kv_half, src, jnp.zeros_like(src))
                dom = jnp.where(half == kv_half, do_src, 0.0).astype(BF16)
                lcol = jnp.max(jnp.where(half == e, lv, -jnp.inf), axis=-1, keepdims=True)
                sc = _dot_nt(qm * (HEAD_DIM ** -0.5), kb) + bias_ref[2 * tt + e]
                pn = jnp.exp(sc - lcol)
                dp = _dot_nt(dom, vb)
                delta = jnp.sum(pn * dp, axis=-1, keepdims=True)
                ds = pn * (dp - delta)
                if gqa:
                    p_sink = jnp.exp(sink_ref[2 * (TPS * t + tt) + e] - lcol)
                    dsk = -jnp.sum(p_sink * delta, axis=0, keepdims=True)
                    row = 2 * tt + e
                    dsink_ref[0, row:row + 1, :] += jnp.broadcast_to(dsk, (1, LANES))
                else:
                    dbias_ref[2 * tt + e] += ds
                dsb = (ds * (HEAD_DIM ** -0.5)).astype(BF16)
                dqs.append(_dot(dsb, kb))
                dk_acc = dk_acc + _dot_tn(dsb, qm)
                dv_acc = dv_acc + _dot_tn(pn.astype(BF16), dom)
            dkpad[pl.ds(start, lk), kv_lanes] += dk_acc
            dvpad[pl.ds(start, lk), kv_lanes] += dv_acc
            if gqa:
                same = jnp.where(hk == 0, dqs[0], dqs[1])
                other = jnp.where(hk == 0, dqs[1], dqs[0])
                dq_ref[:, lanes] = jnp.where(half == hk, same, pltpu.roll(other, HEAD_DIM, 1)).astype(BF16)
            else:
                dq_ref[:, lanes] = jnp.where(half == 0, dqs[0], dqs[1]).astype(BF16)

        @pl.when((g == n_g - 1) & (t == n_t - 1) if gqa else g == n_g - 1)
        def _():
            dk_ref[...] = dkpad[pad:, :].astype(BF16)
            dv_ref[...] = dvpad[pad:, :].astype(BF16)

    in_specs = [q_spec, k_spec, v_spec, bias_spec] + ([SMEM_SPEC] if gqa else []) + [tile_spec, tile_spec]
    args = [proj, proj, proj, bias] + ([sinks] if gqa else []) + [do, lse]
    if gqa:
        kv_out = pl.BlockSpec((s, LANES), lambda t, g: (0, 0))
        kv_shape = jax.ShapeDtypeStruct((s, LANES), BF16)
        extra_spec = pl.BlockSpec((1, 8, LANES), lambda t, g: (t, 0, 0))
        extra_shape = jax.ShapeDtypeStruct((n_t, 8, LANES), F32)
    else:
        kv_out = pl.BlockSpec((s, kv_wide), lambda t, g: (0, t))
        kv_shape = jax.ShapeDtypeStruct((s, 512), BF16)
        extra_spec = pl.BlockSpec((2 * TPS, QROWS, lk), lambda t, g: (t, 0, 0))
        extra_shape = jax.ShapeDtypeStruct(bias.shape[1:], F32)
    return _call(body, name=name, grid=(n_t, n_g), in_specs=in_specs,
                 out_specs=(tile_spec, kv_out, kv_out, extra_spec),
                 out_shape=(jax.ShapeDtypeStruct((s, 512), BF16), kv_shape, kv_shape, extra_shape), args=args,
                 scratch=[pltpu.VMEM((s + pad, kv_wide), BF16), pltpu.VMEM((s + pad, kv_wide), BF16),
                          pltpu.VMEM((s + pad, kv_wide), F32), pltpu.VMEM((s + pad, kv_wide), F32)],
                 sem=("arbitrary", "arbitrary"), carry=carry)


def _sum_slots(r, name):
    n_slots, rows, k = r.shape

    def body(r_ref, o_ref):
        acc = r_ref[0].astype(F32)
        for j in range(1, n_slots):
            acc = acc + r_ref[j].astype(F32)
        o_ref[...] = acc

    return _call(body, name=name, grid=(k // LANES,),
                 in_specs=[pl.BlockSpec((n_slots, rows, LANES), lambda i: (0, 0, i))],
                 out_specs=pl.BlockSpec((rows, LANES), lambda i: (0, i)),
                 out_shape=jax.ShapeDtypeStruct((rows, k), F32), args=[r], sem=("parallel",))


def _sum_rows8(g):
    n = g.shape[2]

    def body(g_ref, o_ref):
        acc = g_ref[0]
        for j in range(1, N_DEV):
            acc = acc + g_ref[j]
        o_ref[...] = acc

    return pl.pallas_call(
        body, name="sum_small_grads", in_specs=[VMEM_SPEC], out_specs=VMEM_SPEC,
        out_shape=jax.ShapeDtypeStruct((1, n), F32), compiler_params=_params(),
    )(g)


def _ada_weight_grad(sc_t, dmod_cols):
    d = sc_t.shape[0]
    w = dmod_cols.shape[1]
    td = _pick(d, (256, 128))

    def body(sc_ref, dm_ref, o_ref):
        scv = sc_ref[...]
        dmv = dm_ref[...]
        acc = scv[:, 0:1] * dmv[0:1, :]
        for b in range(1, N_DEV):
            acc = acc + scv[:, b:b + 1] * dmv[b:b + 1, :]
        o_ref[...] = acc

    return _call(body, name="ada_weight_grad", grid=(d // td,),
                 in_specs=[pl.BlockSpec((td, N_DEV), lambda i: (i, 0)), pl.BlockSpec((N_DEV, w), lambda i: (0, 0))],
                 out_specs=pl.BlockSpec((td, w), lambda i: (i, 0)), out_shape=jax.ShapeDtypeStruct((d, w), F32),
                 args=[sc_t, dmod_cols], sem=("parallel",))


def _adamw_update(w, gv, m, v):
    nm = ADAM_B1 * m + (1.0 - ADAM_B1) * gv
    nv = ADAM_B2 * v + (1.0 - ADAM_B2) * (gv * gv)
    m_hat = nm / (1.0 - ADAM_B1 ** ADAM_STEP)
    v_hat = nv / (1.0 - ADAM_B2 ** ADAM_STEP)
    return -ADAM_LR * (m_hat / (jnp.sqrt(v_hat) + ADAM_EPS) + ADAM_WD * w), nm, nv


def _adamw(w, g, m, v, name):
    rows, cols = w.shape
    tr = _pick(rows, (256, 176, 128, 88, 64)) if rows > 256 else rows

    def body(w_ref, g_ref, m_ref, v_ref, d_ref, nm_ref, nv_ref):
        d_ref[...], nm_ref[...], nv_ref[...] = _adamw_update(w_ref[...], g_ref[...], m_ref[...], v_ref[...])

    spec = pl.BlockSpec((tr, cols), lambda i: (i, 0))
    shape = jax.ShapeDtypeStruct((rows, cols), F32)
    return _call(body, name=name, grid=(rows // tr,), in_specs=[spec] * 4, out_specs=(spec, spec, spec),
                 out_shape=(shape, shape, shape), args=[w, g, m, v], sem=("parallel",))


def _adamw_from_slots(w, own, slots, m, v, name):
    n_slots, rows, k = slots.shape

    def body(o_ref, s_ref, w_ref, m_ref, v_ref, g_ref, d_ref, nm_ref, nv_ref):
        gv = o_ref[...].astype(F32)
        for j in range(n_slots):
            gv = gv + s_ref[j].astype(F32)
        g_ref[...] = gv
        d_ref[...], nm_ref[...], nv_ref[...] = _adamw_update(w_ref[...], gv, m_ref[...], v_ref[...])

    tr = rows // 2 if rows % 32 == 0 else rows
    spec = pl.BlockSpec((tr, k), lambda i: (i, 0))
    shape = jax.ShapeDtypeStruct((rows, k), F32)
    return _call(body, name=name, grid=(rows // tr,),
                 in_specs=[spec, pl.BlockSpec((n_slots, tr, k), lambda i: (0, i, 0)), spec, spec, spec],
                 out_specs=(spec, spec, spec, spec), out_shape=(shape, shape, shape, shape),
                 args=[own, slots, w, m, v], sem=("parallel",))


def _adamw_small(g, w, m, v, sizes):
    n = w.shape[1]
    offs, off = [], 0
    for size in sizes:
        offs.append(off)
        off += size + (-size % LANES)

    def body(g_ref, w_ref, m_ref, v_ref, *out_refs):
        gv = g_ref[:, 0:n]
        dv, nm, nv = _adamw_update(w_ref[...], gv, m_ref[...], v_ref[...])
        for j, (o, size) in enumerate(zip(offs, sizes)):
            for k, val in enumerate((gv, dv, nm, nv)):
                out_refs[4 * j + k][...] = val[:, o:o + size]

    shapes = [jax.ShapeDtypeStruct((1, size), F32) for size in sizes for _ in range(4)]
    return pl.pallas_call(
        body, name="adamw_small", in_specs=[VMEM_SPEC] * 4, out_specs=tuple([VMEM_SPEC] * len(shapes)),
        out_shape=tuple(shapes), compiler_params=_params(),
    )(g, w, m, v)


SMALL = ("b_ada", "g_pre_ffn1", "g_post_ffn1", "g_pre_mix", "b_in", "sinks_a", "rel_bias_b", "g_grp_a",
         "g_grp_b", "b_out", "g_post_mix", "g_pre_ffn2", "g_post_ffn2")
WEIGHTS = ("w_ada", "b_ada", "g_pre_ffn1", "w_gate1", "w_up1", "w_down1", "g_post_ffn1", "g_pre_mix", "w_in",
           "b_in", "sinks_a", "rel_bias_b", "g_grp_a", "g_grp_b", "w_out", "b_out", "g_post_mix", "g_pre_ffn2",
           "w_gate2", "w_up2", "w_down2", "g_post_ffn2")


def kernel(x, c, w_ada, b_ada, g_pre_ffn1, w_gate1, w_up1, w_down1, g_post_ffn1, g_pre_mix, w_in, b_in, sinks_a, rel_bias_b, g_grp_a, g_grp_b, w_out, b_out, g_post_mix, g_pre_ffn2, w_gate2, w_up2, w_down2, g_post_ffn2, loss_target, m_w_ada, m_b_ada, m_g_pre_ffn1, m_w_gate1, m_w_up1, m_w_down1, m_g_post_ffn1, m_g_pre_mix, m_w_in, m_b_in, m_sinks_a, m_rel_bias_b, m_g_grp_a, m_g_grp_b, m_w_out, m_b_out, m_g_post_mix, m_g_pre_ffn2, m_w_gate2, m_w_up2, m_w_down2, m_g_post_ffn2, v_w_ada, v_b_ada, v_g_pre_ffn1, v_w_gate1, v_w_up1, v_w_down1, v_g_post_ffn1, v_g_pre_mix, v_w_in, v_b_in, v_sinks_a, v_rel_bias_b, v_g_grp_a, v_g_grp_b, v_w_out, v_b_out, v_g_post_mix, v_g_pre_ffn2, v_w_gate2, v_w_up2, v_w_down2, v_g_post_ffn2):
    given = dict(locals())
    weights = {n: given[n] for n in WEIGHTS}
    mom_m = {n: given["m_" + n] for n in WEIGHTS}
    mom_v = {n: given["v_" + n] for n in WEIGHTS}

    me = 4 * lax.axis_index("x") + 2 * lax.axis_index("y") + lax.axis_index("c")
    xs = x[0]
    tgt = loss_target[0]
    d_model = xs.shape[1]
    ada_cols = w_ada.shape[2]

    sh = {"wg1": w_gate1[0].T, "wu1": w_up1[0].T, "wd1": w_down1[0], "win": w_in[0].T, "wo": w_out[0],
          "wg2": w_gate2[0].T, "wu2": w_up2[0].T, "wd2": w_down2[0]}
    sh = {k: v.astype(BF16) for k, v in sh.items()}

    def gather(*names):
        return _gather_carry([sh[n] for n in names])

    g_sems, g_shards, g_arrays, g_token = _gather_start([sh["wg1"], sh["wu1"]], "gather_start_ffn1_up")

    bias_a = _alibi_bias()
    rel_m = _rel_index_matrix()
    rel_vec = jnp.dot(rel_bias_b[0], rel_m.T, precision=lax.Precision.HIGHEST)
    bias_b = _toeplitz_bias((rel_vec + g_token[0:1, 0:1]).reshape(H_B, 1, SKEW))

    b_cols = lax.dynamic_slice(b_ada, (0, me * ada_cols), (1, ada_cols))
    (sc_all, mod_rows), _ = _ada_forward(c, w_ada[0], b_cols, _Carry([], [], [], lambda *a: None, lambda *a: None))
    mod = mod_rows.reshape(N_MOD, d_model)
    shift1, scale1, gate1, shift2, scale2, gate2, shift3, scale3, gate3 = (mod[i:i + 1] for i in range(N_MOD))

    h1 = _pre_norm(xs, g_pre_ffn1, scale1, shift1, "pre_norm_ffn1")
    g_shards, g_arrays = _gather_wait(g_sems, g_shards, g_arrays, h1, "gather_wait_ffn1_up")
    wg1, wu1 = _run_carry(_forward_carry(g_arrays, g_shards), "gather_pass_ffn1_up")
    (a1, b1, u1), (wd1,) = _ffn_up(h1, wg1, wu1, "ffn_up_ffn1", carry=gather("wd1"))
    (y1, x1, h2), (win,) = _mm_nn(
        [(u1, wd1)], "ffn_down_ffn1", F32, carry=gather("win"),
        tail=_tail_post_pre(xs, g_post_ffn1, gate1, 0.5, g_pre_mix, scale2, shift2))

    proj, (wo,) = _mm_nt(h2, win, "in_proj", BF16, bias=b_in, carry=gather("wo"))
    sinks = sinks_a[0]
    cfg_a = dict(n_back=BACK_A, gqa=True, q_col=0, k_col=QA // LANES, v_col=(QA + KVA) // LANES)
    cfg_b = dict(n_back=BACK_B, gqa=False, q_col=(QA + 2 * KVA) // LANES, k_col=(QA + 2 * KVA + QB) // LANES,
                 v_col=(QA + 2 * KVA + 2 * QB) // LANES)
    (oa, lse_a), (wg2,) = _attention_fwd(proj, bias_a, sinks, name="attn_a", carry=gather("wg2"), **cfg_a)
    (ob, lse_b), (wu2,) = _attention_fwd(proj, bias_b, None, name="attn_b", carry=gather("wu2"), **cfg_b)
    ycat = _group_norm_cat(oa, ob, g_grp_a, g_grp_b)
    ymix, x2, h3 = _mm_nn([(ycat, wo)], "out_proj", F32, bias=b_out,
                          tail=_tail_post_pre(x1, g_post_mix, gate2, 1.0, g_pre_ffn2, scale3, shift3))

    (a3, b3, u3), (wd2,) = _ffn_up(h3, wg2, wu2, "ffn_up_ffn2", carry=gather("wd2"))

    flights, own = {}, {}

    def grad_pair(key, a_mat, b_mat, name):
        part, own[key] = _mm_tn_pair(a_mat, b_mat, name)
        return part

    def scatter_start(tag, after_vec, **parts):
        names = list(parts)
        sems, p_thru, lands, token = _scatter_start([parts[n] for n in names], "scatter_start_" + tag)
        flights[tag] = (names, sems, p_thru, lands)
        return after_vec + token[0:1, 0:1]

    dx3, dy, loss_part, s1 = _mm_nn([(u3, wd2)], "ffn_down_ffn2", None,
                                    tail=_tail_post_loss(x2, tgt, g_post_ffn2, gate3, 0.5))
    da, db = _ffn_down_bwd(dy, wd2, a3, b3, "ffn_down_bwd_ffn2")
    dwd2 = grad_pair("wd2", u3, dy, "grad_wd_ffn2")
    dwg2 = grad_pair("wg2", da, h3, "grad_wg_ffn2")
    dwu2 = grad_pair("wu2", db, h3, "grad_wu_ffn2")
    g_pre_tied = scatter_start("ffn2", g_pre_ffn2, wd2=dwd2, wg2=dwg2, wu2=dwu2)
    dx2, dymix, s2, s3, s1m, db_out = _mm_nn(
        [(da, wg2), (db, wu2)], "ffn_up_bwd_ffn2", None,
        tail=_tail_pre_post_bwd(x2, dx3, ymix, g_pre_tied, scale3, g_post_mix, gate2, 1.0))
    sm3 = dict(shift=s3, scale=s2 * g_pre_ffn2, gate=0.5 * g_post_ffn2 * s1,
               g_pre=(1.0 + scale3) * s2, g_post=(0.5 * gate3) * s1)

    dycat = _mm_nt(dymix, wo, "out_proj_bwd", F32)
    dwo = grad_pair("wo", ycat, dymix, "grad_wo")
    doa, dob, dg_a, dg_b = _group_norm_bwd(dycat, oa, ob, g_grp_a, g_grp_b)
    dqa, dka, dva, dsink = _attention_bwd(proj, bias_a, sinks, doa, lse_a, name="attn_a_bwd", **cfg_a)
    dqb, dkb, dvb, dbias = _attention_bwd(proj, bias_b, None, dob, lse_b, name="attn_b_bwd", **cfg_b)
    dproj = jnp.concatenate([dqa, dka, dva, dqb, dkb, dvb], axis=1)
    db_in = _col_sum(dproj, "grad_b_in")
    dwin = grad_pair("win", dproj, h2, "grad_win")
    g_pre_tied = scatter_start("mix", g_pre_mix, wo=dwo, win=dwin)
    dx1, dy, s2m, s3m, s1, _ = _mm_nn(
        [(dproj, win)], "in_proj_bwd", None,
        tail=_tail_pre_post_bwd(x1, dx2, y1, g_pre_tied, scale2, g_post_ffn1, gate1, 0.5))
    d_rel = jnp.dot(_diagonal_sums(dbias).reshape(H_B, SKEW), rel_m, precision=lax.Precision.HIGHEST)
    d_sinks = dsink[:, :2 * TPS, 0].reshape(1, H_A)

    da, db = _ffn_down_bwd(dy, wd1, a1, b1, "ffn_down_bwd_ffn1")
    dwd1 = grad_pair("wd1", u1, dy, "grad_wd_ffn1")
    dwg1 = grad_pair("wg1", da, h1, "grad_wg_ffn1")
    dwu1 = grad_pair("wu1", db, h1, "grad_wu_ffn1")
    g_pre_tied = scatter_start("ffn1", g_pre_ffn1, wd1=dwd1, wg1=dwg1, wu1=dwu1)
    dx0, s2, s3 = _mm_nn([(da, wg1), (db, wu1)], "ffn_up_bwd_ffn1", None,
                         tail=_tail_pre_bwd(xs, dx1, g_pre_tied, scale1))
    sm1 = dict(shift=s3, scale=s2 * g_pre_ffn1, gate=0.5 * g_post_ffn1 * s1,
               g_pre=(1.0 + scale1) * s2, g_post=(0.5 * gate1) * s1)

    dmod = jnp.concatenate([sm1["shift"], sm1["scale"], sm1["gate"],
                            s3m, s2m * g_pre_mix, g_post_mix * s1m,
                            sm3["shift"], sm3["scale"], sm3["gate"]], axis=1)
    small_parts = {
        "b_ada": dmod, "g_pre_ffn1": sm1["g_pre"], "g_post_ffn1": sm1["g_post"],
        "g_pre_mix": (1.0 + scale2) * s2m, "b_in": db_in, "sinks_a": d_sinks,
        "rel_bias_b": d_rel.reshape(1, H_B * N_REL), "g_grp_a": dg_a, "g_grp_b": dg_b, "b_out": db_out,
        "g_post_mix": gate2 * s1m, "g_pre_ffn2": sm3["g_pre"], "g_post_ffn2": sm3["g_post"]}
    sizes = [small_parts[n].shape[1] for n in SMALL]

    def pack(parts):
        cells = []
        for p in parts:
            cells.append(p)
            if p.shape[1] % LANES:
                cells.append(jnp.zeros((1, -p.shape[1] % LANES), F32))
        return jnp.concatenate(cells, axis=1)

    packed = pack([small_parts[n] for n in SMALL] + [loss_part])
    n_packed = packed.shape[1]
    small_sems, packed_thru, small_land, small_token = _small_gather_start(packed)

    out_g, out_d, out_m, out_v = {}, {}, {}, {}
    groups = (("ffn2", (("w_gate2", "wg2", True), ("w_up2", "wu2", True), ("w_down2", "wd2", False))),
              ("mix", (("w_in", "win", True), ("w_out", "wo", False))),
              ("ffn1", (("w_gate1", "wg1", True), ("w_up1", "wu1", True), ("w_down1", "wd1", False))))
    after = small_token
    for tag, members in groups:
        names, sems, p_thru, lands = flights[tag]
        _, l_done = _scatter_wait(sems, p_thru, lands, after, "scatter_wait_" + tag)
        slots = dict(zip(names, l_done))
        for n, key, transposed in members:
            view = (lambda t: t.T) if transposed else (lambda t: t)
            res = _adamw_from_slots(view(weights[n][0]), own[key], slots[key], view(mom_m[n][0]),
                                    view(mom_v[n][0]), "adamw_" + n)
            out_g[n], out_d[n], out_m[n], out_v[n] = (view(t)[None] for t in res)
            after = res[3]

    packed_done, small_land = _small_gather_wait(small_sems, packed_thru, small_land, after)
    gathered = lax.dynamic_update_slice(small_land, packed_done[None], (me, 0, 0))
    small_sum = _sum_rows8(gathered)
    loss = small_sum[0, n_packed - LANES]
    dmod_cols = lax.dynamic_slice(gathered.reshape(N_DEV, n_packed), (0, me * ada_cols), (N_DEV, ada_cols))
    g_ada = _ada_weight_grad(sc_all.reshape(N_DEV, d_model).T, dmod_cols)
    d_, m_, v_ = _adamw(w_ada[0], g_ada, m_w_ada[0], v_w_ada[0], "adamw_w_ada")
    out_g["w_ada"], out_d["w_ada"], out_m["w_ada"], out_v["w_ada"] = g_ada[None], d_[None], m_[None], v_[None]

    small_out = _adamw_small(small_sum, *(pack([tree[n].reshape(1, -1) for n in SMALL])
                                          for tree in (weights, mom_m, mom_v)), sizes)
    for j, n in enumerate(SMALL):
        shape = weights[n].shape
        out_g[n], out_d[n], out_m[n], out_v[n] = (t.reshape(shape) for t in small_out[4 * j:4 * j + 4])

    return (loss, dx0[None], *[out_g[n] for n in WEIGHTS], *[out_d[n] for n in WEIGHTS],
            *[out_m[n] for n in WEIGHTS], *[out_v[n] for n in WEIGHTS])
```

```python
import numpy as np
import jax
import jax.numpy as jnp
from jax import lax
from jax.experimental import pallas as pl
from jax.experimental.pallas import tpu as pltpu

F32 = jnp.float32
BF16 = jnp.bfloat16
MESH = pl.DeviceIdType.MESH
ANY = pl.BlockSpec(memory_space=pl.ANY)
VMEM_SPEC = pl.BlockSpec(memory_space=pltpu.VMEM)
SMEM_SPEC = pl.BlockSpec(memory_space=pltpu.SMEM)

N_DEV = 8
CHUNK = 64
HEAD_DIM = 64
LANES = 128
H_A, KV_A, H_B = 8, 2, 8
BACK_A, BACK_B = 2, 8
REL_CLIP = 128
N_REL = 2 * REL_CLIP + 1
QA, KVA, QB = H_A * HEAD_DIM, KV_A * HEAD_DIM, H_B * HEAD_DIM
D_IN = QA + 2 * KVA + 3 * QB
N_MOD = 9
EPS = 1e-6
NEG_INF = -1e30
QG = 4
QROWS = QG * CHUNK
TPS = 2
SKEW = 1024
ADAM_LR, ADAM_B1, ADAM_B2, ADAM_EPS, ADAM_WD, ADAM_STEP = 0.001, 0.9, 0.999, 1e-08, 0.01, 10
VMEM_LIMIT = 56 * 2 ** 20


def _pick(n, cands):
    for c in cands:
        if n % c == 0:
            return c
    return n


def _pieces(n, width=2 * LANES):
    return [(lo, min(lo + width, n)) for lo in range(0, n, width)]


def _params(sem=None):
    return pltpu.CompilerParams(dimension_semantics=sem, vmem_limit_bytes=VMEM_LIMIT)


def _dot_nt(a, b):
    return lax.dot_general(a, b, (((1,), (1,)), ((), ())), preferred_element_type=F32)


def _dot_tn(a, b):
    return lax.dot_general(a, b, (((0,), (0,)), ((), ())), preferred_element_type=F32)


def _dot(a, b):
    return jnp.dot(a, b, preferred_element_type=F32)


def _sigmoid(a):
    return 0.5 * (jnp.tanh(0.5 * a) + 1.0)


def _mesh_pos():
    return lax.axis_index("x"), lax.axis_index("y"), lax.axis_index("c")


def _peer(x, y, c, r):
    px = 1 - x if r & 4 else x
    py = 1 - y if r & 2 else y
    pc = 1 - c if r & 1 else c
    return px, py, pc


class _Carry:
    def __init__(self, ins, out_shapes, scratch, start, finish):
        self.ins, self.out_shapes, self.scratch = list(ins), list(out_shapes), list(scratch)
        self.start, self.finish = start, finish


def _call(body, *, name, grid, in_specs, out_specs, out_shape, args, scratch=(), sem=None, carry=None):
    single = not isinstance(out_shape, (tuple, list))
    out_specs = (out_specs,) if single else tuple(out_specs)
    out_shape = (out_shape,) if single else tuple(out_shape)
    if carry is None:
        res = pl.pallas_call(body, name=name, grid=grid, in_specs=list(in_specs), out_specs=out_specs,
                             out_shape=out_shape, scratch_shapes=list(scratch), compiler_params=_params(sem))(*args)
        return res[0] if single else res
    n_in, n_out, n_s = len(in_specs), len(out_shape), len(scratch)
    ci, co = len(carry.ins), len(carry.out_shapes)

    def wrapped(*refs):
        ins, cins = refs[:n_in], refs[n_in:n_in + ci]
        outs = refs[n_in + ci:n_in + ci + n_out]
        couts = refs[n_in + ci + n_out:n_in + ci + n_out + co]
        scr = refs[n_in + ci + n_out + co:n_in + ci + n_out + co + n_s]
        cscr = refs[n_in + ci + n_out + co + n_s:]
        first, last = None, None
        for ax, n in enumerate(grid):
            f, l = pl.program_id(ax) == 0, pl.program_id(ax) == n - 1
            first = f if first is None else first & f
            last = l if last is None else last & l
        pl.when(first)(lambda: carry.start(cins, couts, cscr))
        body(*ins, *outs, *scr)
        pl.when(last)(lambda: carry.finish(cins, couts, cscr))

    res = pl.pallas_call(
        wrapped, name=name, grid=grid, in_specs=list(in_specs) + [ANY] * ci, out_specs=out_specs + (ANY,) * co,
        out_shape=out_shape + tuple(carry.out_shapes), scratch_shapes=list(scratch) + carry.scratch,
        compiler_params=_params(("arbitrary",) * len(grid)))(*args, *carry.ins)
    main = res[:n_out]
    return (main[0] if single else main), res[n_out:]


def _gather_carry(shards):
    n_w = len(shards)
    rows = [s.shape[0] for s in shards]

    def plan(ins, outs, scr):
        send_sems, recv_sems, local_sems = scr
        x, y, c = _mesh_pos()
        me, sibling = (x, y, c), (x, y, 1 - c)
        chips = [(1 - x, y), (x, 1 - y), (1 - x, 1 - y)]

        def block(w, dev):
            start = pl.multiple_of((4 * dev[0] + 2 * dev[1] + dev[2]) * rows[w], 16)
            return outs[w].at[pl.ds(start, rows[w]), :]

        def copy(w, k, dev, to, src=None):
            return pltpu.make_async_remote_copy(
                src_ref=block(w, dev) if src is None else src, dst_ref=block(w, dev),
                send_sem=send_sems.at[w, k], recv_sem=recv_sems.at[w, k], device_id=to, device_id_type=MESH)

        mine = [pltpu.make_async_copy(ins[w], block(w, me), local_sems.at[w]) for w in range(n_w)]
        first = []
        for j, chip in enumerate(chips):
            first += [copy(w, 1 + j, me, (*chip, c), src=ins[w]) for w in range(n_w)]
        first += [copy(w, 0, me, sibling, src=ins[w]) for w in range(n_w)]
        return c, me, sibling, chips, copy, mine, first

    def start(ins, outs, scr):
        _, _, _, _, _, mine, first = plan(ins, outs, scr)
        for cp in mine + first:
            cp.start()

    def finish(ins, outs, scr):
        c, me, sibling, chips, copy, mine, first = plan(ins, outs, scr)
        passed = []
        for j, chip in enumerate(chips):
            for w in range(n_w):
                copy(w, 1 + j, (*chip, c), me).wait_recv()
                cp = copy(w, 4 + j, (*chip, c), sibling)
                cp.start()
                passed.append(cp)
        for w in range(n_w):
            copy(w, 0, sibling, me).wait_recv()
        for j, chip in enumerate(chips):
            for w in range(n_w):
                copy(w, 4 + j, (*chip, 1 - c), me).wait_recv()
        for cp in first + passed:
            cp.wait_send()
        for cp in mine:
            cp.wait()

    return _Carry(
        shards, [jax.ShapeDtypeStruct((N_DEV * s.shape[0], s.shape[1]), s.dtype) for s in shards],
        [pltpu.SemaphoreType.DMA((n_w, N_DEV - 1)), pltpu.SemaphoreType.DMA((n_w, N_DEV - 1)),
         pltpu.SemaphoreType.DMA((n_w,))], start, finish)


HBM_SPEC = pl.BlockSpec(memory_space=pltpu.HBM)
SEM_SPEC = pl.BlockSpec(memory_space=pltpu.SEMAPHORE)
N_CHIP = N_DEV // 2


def _scatter_copy(part_ref, land_ref, send_sem, recv_sem, r, rows):
    x, y, c = _mesh_pos()
    px, py, _ = _peer(x, y, c, 2 * r)
    src = part_ref.at[pl.ds(pl.multiple_of((2 * px + py) * rows, 16), rows), :]
    return pltpu.make_async_remote_copy(
        src_ref=src, dst_ref=land_ref.at[r - 1], send_sem=send_sem, recv_sem=recv_sem,
        device_id=(px, py, c), device_id_type=MESH)


def _scatter_order(n_w):
    return [(w, r) for r in (3, 2, 1) for w in range(n_w)]


def _scatter_start(parts, name):
    n_w = len(parts)
    rows = [p.shape[0] // N_CHIP for p in parts]
    order = _scatter_order(n_w)
    lands = [pltpu.with_memory_space_constraint(lax.empty((N_CHIP - 1, r, p.shape[1]), p.dtype), pltpu.HBM)
             for r, p in zip(rows, parts)]

    def body(*refs):
        part_refs, land_refs = refs[:n_w], refs[n_w:2 * n_w]
        sems = refs[2 * n_w:2 * n_w + 2 * len(order)]
        token = refs[-1]
        for j, (w, r) in enumerate(order):
            _scatter_copy(part_refs[w], land_refs[w], sems[2 * j], sems[2 * j + 1], r, rows[w]).start()
        token[...] = jnp.zeros_like(token)

    n_sem = 2 * len(order)
    res = pl.pallas_call(
        body, name=name,
        out_shape=(*[pltpu.SemaphoreType.DMA(())] * n_sem, *[pltpu.HBM(p.shape, p.dtype) for p in parts],
                   *[pltpu.HBM(l.shape, l.dtype) for l in lands], jax.ShapeDtypeStruct((8, LANES), F32)),
        in_specs=[HBM_SPEC] * (2 * n_w), out_specs=(*[SEM_SPEC] * n_sem, *[HBM_SPEC] * (2 * n_w), VMEM_SPEC),
        input_output_aliases={i: n_sem + i for i in range(2 * n_w)},
        compiler_params=pltpu.CompilerParams(has_side_effects=pltpu.SideEffectType.DATAFLOW_SIDE_EFFECTING),
    )(*[pltpu.with_memory_space_constraint(p, pltpu.HBM) for p in parts], *lands)
    return (list(res[:n_sem]), list(res[n_sem:n_sem + n_w]), list(res[n_sem + n_w:n_sem + 2 * n_w]), res[-1])


def _scatter_wait(sems, parts, lands, after, name):
    n_w = len(parts)
    rows = [p.shape[0] // N_CHIP for p in parts]
    order = _scatter_order(n_w)

    def body(*refs):
        part_refs, land_refs = refs[:n_w], refs[n_w:2 * n_w]
        sem_refs = refs[2 * n_w:2 * n_w + 2 * len(order)]
        for j, (w, r) in enumerate(order):
            cp = _scatter_copy(part_refs[w], land_refs[w], sem_refs[2 * j], sem_refs[2 * j + 1], r, rows[w])
            cp.wait_send()
            cp.wait_recv()

    res = pl.pallas_call(
        body, name=name,
        out_shape=(*[pltpu.HBM(p.shape, p.dtype) for p in parts], *[pltpu.HBM(l.shape, l.dtype) for l in lands]),
        in_specs=[HBM_SPEC] * (2 * n_w) + [SEM_SPEC] * len(sems) + [ANY],
        out_specs=tuple([HBM_SPEC] * (2 * n_w)),
        input_output_aliases={i: i for i in range(2 * n_w)},
        compiler_params=pltpu.CompilerParams(has_side_effects=pltpu.SideEffectType.DATAFLOW_SIDE_EFFECTING),
    )(*parts, *lands, *sems, after)
    return list(res[:n_w]), list(res[n_w:])


def _small_copy(v_ref, land_ref, send_sem, recv_sem, r):
    x, y, c = _mesh_pos()
    px, py, pc = _peer(x, y, c, r)
    return pltpu.make_async_remote_copy(
        src_ref=v_ref, dst_ref=land_ref.at[4 * x + 2 * y + c], send_sem=send_sem, recv_sem=recv_sem,
        device_id=(px, py, pc), device_id_type=MESH)


def _small_gather_start(v):
    land = pltpu.with_memory_space_constraint(lax.empty((N_DEV,) + v.shape, v.dtype), pltpu.HBM)

    def body(v_ref, land_ref, *rest):
        sems, token = rest[:2 * (N_DEV - 1)], rest[-1]
        for r in range(1, N_DEV):
            _small_copy(v_ref, land_ref, sems[2 * r - 2], sems[2 * r - 1], r).start()
        token[...] = jnp.zeros_like(token)

    n_sem = 2 * (N_DEV - 1)
    res = pl.pallas_call(
        body, name="small_gather_start",
        out_shape=(*[pltpu.SemaphoreType.DMA(())] * n_sem, pltpu.HBM(v.shape, v.dtype),
                   pltpu.HBM(land.shape, land.dtype), jax.ShapeDtypeStruct((8, LANES), F32)),
        in_specs=[HBM_SPEC, HBM_SPEC], out_specs=(*[SEM_SPEC] * n_sem, HBM_SPEC, HBM_SPEC, VMEM_SPEC),
        input_output_aliases={0: n_sem, 1: n_sem + 1},
        compiler_params=pltpu.CompilerParams(has_side_effects=pltpu.SideEffectType.DATAFLOW_SIDE_EFFECTING),
    )(pltpu.with_memory_space_constraint(v, pltpu.HBM), land)
    return list(res[:n_sem]), res[n_sem], res[n_sem + 1], res[-1]


def _small_gather_wait(sems, v, land, after):
    def body(v_ref, land_ref, *rest):
        for r in range(1, N_DEV):
            cp = _small_copy(v_ref, land_ref, rest[2 * r - 2], rest[2 * r - 1], r)
            cp.wait_send()
            x, y, c = _mesh_pos()
            px, py, pc = _peer(x, y, c, r)
            pltpu.make_async_remote_copy(
                src_ref=v_ref, dst_ref=land_ref.at[4 * px + 2 * py + pc], send_sem=rest[2 * r - 2],
                recv_sem=rest[2 * r - 1], device_id=(px, py, pc), device_id_type=MESH).wait_recv()

    res = pl.pallas_call(
        body, name="small_gather_wait",
        out_shape=(pltpu.HBM(v.shape, v.dtype), pltpu.HBM(land.shape, land.dtype)),
        in_specs=[HBM_SPEC, HBM_SPEC] + [SEM_SPEC] * len(sems) + [ANY], out_specs=(HBM_SPEC, HBM_SPEC),
        input_output_aliases={0: 0, 1: 1},
        compiler_params=pltpu.CompilerParams(has_side_effects=pltpu.SideEffectType.DATAFLOW_SIDE_EFFECTING),
    )(v, land, *sems, after)
    return res[0], res[1]


def _ada_forward(c_row, w_ada, b_cols, carry):
    d = c_row.shape[1]
    wcols = w_ada.shape[1]
    ci, co = len(carry.ins), len(carry.out_shapes)

    def body(*refs):
        c_ref, w_ref, b_ref = refs[:3]
        cins = refs[3:3 + ci]
        sc_ref, mod_ref = refs[3 + ci:5 + ci]
        couts = refs[5 + ci:5 + ci + co]
        rows_ref, send_sems, recv_sems = refs[5 + ci + co:8 + ci + co]
        cscr = refs[8 + ci + co:]
        carry.start(cins, couts, cscr)
        x, y, c = _mesh_pos()
        me = 4 * x + 2 * y + c
        cv = c_ref[...]
        sc_ref[me] = cv * _sigmoid(cv)

        sends = []
        for r in range(1, N_DEV):
            px, py, pc = _peer(x, y, c, r)
            cp = pltpu.make_async_remote_copy(
                src_ref=sc_ref.at[me], dst_ref=sc_ref.at[me], send_sem=send_sems.at[0, r - 1],
                recv_sem=recv_sems.at[0, r - 1], device_id=(px, py, pc), device_id_type=MESH)
            cp.start()
            sends.append(cp)
        for r in range(1, N_DEV):
            px, py, pc = _peer(x, y, c, r)
            pid = 4 * px + 2 * py + pc
            pltpu.make_async_remote_copy(
                src_ref=sc_ref.at[pid], dst_ref=sc_ref.at[pid], send_sem=send_sems.at[0, r - 1],
                recv_sem=recv_sems.at[0, r - 1], device_id=(px, py, pc), device_id_type=MESH).wait_recv()
        for cp in sends:
            cp.wait_send()

        sc_all = jnp.concatenate([sc_ref[j] for j in range(N_DEV)], axis=0)
        rows = _dot(sc_all.astype(BF16), w_ref[...].astype(BF16)) + b_ref[...]
        for j in range(N_DEV):
            rows_ref[j] = rows[j:j + 1, :]
        mod_ref[me] = rows_ref[me]

        sends = []
        for r in range(1, N_DEV):
            px, py, pc = _peer(x, y, c, r)
            pid = 4 * px + 2 * py + pc
            cp = pltpu.make_async_remote_copy(
                src_ref=rows_ref.at[pid], dst_ref=mod_ref.at[me], send_sem=send_sems.at[1, r - 1],
                recv_sem=recv_sems.at[1, r - 1], device_id=(px, py, pc), device_id_type=MESH)
            cp.start()
            sends.append(cp)
        for r in range(1, N_DEV):
            px, py, pc = _peer(x, y, c, r)
            pid = 4 * px + 2 * py + pc
            pltpu.make_async_remote_copy(
                src_ref=rows_ref.at[pid], dst_ref=mod_ref.at[pid], send_sem=send_sems.at[1, r - 1],
                recv_sem=recv_sems.at[1, r - 1], device_id=(px, py, pc), device_id_type=MESH).wait_recv()
        for cp in sends:
            cp.wait_send()
        carry.finish(cins, couts, cscr)

    res = pl.pallas_call(
        body, name="ada_forward",
        out_shape=(jax.ShapeDtypeStruct((N_DEV, 1, d), F32), jax.ShapeDtypeStruct((N_DEV, 1, wcols), F32),
                   *carry.out_shapes),
        in_specs=[VMEM_SPEC, VMEM_SPEC, VMEM_SPEC] + [ANY] * ci, out_specs=(VMEM_SPEC, VMEM_SPEC) + (ANY,) * co,
        scratch_shapes=[pltpu.VMEM((N_DEV, 1, wcols), F32), pltpu.SemaphoreType.DMA((2, N_DEV - 1)),
                        pltpu.SemaphoreType.DMA((2, N_DEV - 1))] + carry.scratch,
        compiler_params=_params(),
    )(c_row, w_ada, b_cols, *carry.ins)
    return res[:2], res[2:]


def _mm_nt(a, b, name, out_dtype, bias=None, carry=None):
    m, k = a.shape
    n = b.shape[0]
    tm = _pick(m, (512, 256, 128))
    tn = _pick(n, (1408, 1152, 1024, 768, 512, 256, 128))

    def body(*refs):
        acc = _dot_nt(refs[0][...], refs[1][...])
        if bias is not None:
            acc = acc + refs[2][...]
        refs[-1][...] = acc.astype(out_dtype)

    in_specs = [pl.BlockSpec((tm, k), lambda j, i: (i, 0)), pl.BlockSpec((tn, k), lambda j, i: (j, 0))]
    args = [a, b]
    if bias is not None:
        in_specs.append(pl.BlockSpec((1, tn), lambda j, i: (0, j)))
        args.append(bias)
    return _call(body, name=name, grid=(n // tn, m // tm), in_specs=in_specs,
                 out_specs=pl.BlockSpec((tm, tn), lambda j, i: (i, j)),
                 out_shape=jax.ShapeDtypeStruct((m, n), out_dtype), args=args,
                 sem=("parallel", "parallel"), carry=carry)


class _Tail:
    def __init__(self, rows, vecs, outs, fn):
        self.rows, self.vecs, self.outs, self.fn = list(rows), list(vecs), list(outs), fn


def _mm_nn(pairs, name, out_dtype, bias=None, carry=None, tail=None):
    m, k = pairs[0][0].shape
    n = pairs[0][1].shape[1]
    n_p = len(pairs)
    tm = _pick(m, (512, 256, 128))
    tk = k if n_p == 1 else _pick(k, (1408, 1152, 1024, 768, 512, 256, 128))
    nk = k // tk
    n_b = 0 if bias is None else 1
    n_r, n_v = (len(tail.rows), len(tail.vecs)) if tail else (0, 0)
    n_in = 2 * n_p + n_b + n_r + n_v
    n_main = 0 if out_dtype is None else 1

    def finish(acc, refs, first_tile):
        if bias is not None:
            acc = acc + refs[2 * n_p][...]
        outs = refs[n_in:-1]
        if n_main:
            outs[0][...] = acc.astype(out_dtype)
        if tail is None:
            return
        rows = [r[...] for r in refs[2 * n_p + n_b:2 * n_p + n_b + n_r]]
        vecs = [v[...] for v in refs[2 * n_p + n_b + n_r:n_in]]
        vals = tail.fn(acc, rows, vecs)
        for ref, val, (dtype, kind) in zip(outs[n_main:], vals, tail.outs):
            if kind == "row":
                ref[...] = val.astype(dtype)
            else:
                @pl.when(first_tile)
                def _(ref=ref):
                    ref[...] = jnp.zeros_like(ref)

                ref[...] += val

    def body(*refs):
        acc_ref = refs[-1]
        kk, i = pl.program_id(0), pl.program_id(1)
        part = _dot(refs[0][...], refs[1][...])
        for p in range(1, n_p):
            part = part + _dot(refs[2 * p][...], refs[2 * p + 1][...])
        if nk == 1:
            finish(part, refs, i == 0)
            return
        rows = pl.ds(pl.multiple_of(i * tm, tm), tm)

        @pl.when(kk == 0)
        def _():
            acc_ref[rows, :] = part

        if nk > 2:
            @pl.when((kk > 0) & (kk < nk - 1))
            def _():
                acc_ref[rows, :] += part

        @pl.when(kk == nk - 1)
        def _():
            finish(acc_ref[rows, :] + part, refs, i == 0)

    def last_only(kk, i):
        return (jnp.where(kk == nk - 1, i, 0), 0)

    row_spec = pl.BlockSpec((tm, n), last_only)
    vec_spec = pl.BlockSpec((1, n), lambda kk, i: (0, 0))
    in_specs, args = [], []
    for a, b in pairs:
        in_specs += [pl.BlockSpec((tm, tk), lambda kk, i: (i, kk)), pl.BlockSpec((tk, n), lambda kk, i: (kk, 0))]
        args += [a, b]
    if bias is not None:
        in_specs.append(vec_spec)
        args.append(bias)
    out_specs = [row_spec] * n_main
    out_shape = [jax.ShapeDtypeStruct((m, n), out_dtype)] if n_main else []
    if tail:
        in_specs += [row_spec] * n_r + [vec_spec] * n_v
        args += tail.rows + tail.vecs
        for dtype, kind in tail.outs:
            if kind == "row":
                out_specs.append(row_spec)
                out_shape.append(jax.ShapeDtypeStruct((m, n), dtype))
            else:
                width = n if kind == "sum" else 1
                out_specs.append(pl.BlockSpec((1, width), lambda kk, i: (0, 0)))
                out_shape.append(jax.ShapeDtypeStruct((1, width), dtype))
    if tail is None:
        out_specs, out_shape = out_specs[0], out_shape[0]
    return _call(body, name=name, grid=(nk, m // tm), in_specs=in_specs, out_specs=out_specs,
                 out_shape=out_shape, args=args,
                 scratch=[pltpu.VMEM((m, n) if nk > 1 else (8, LANES), F32)],
                 sem=("arbitrary", "arbitrary"), carry=carry)


def _rms(v):
    return lax.rsqrt(jnp.mean(v * v, axis=-1, keepdims=True) + EPS)


def _col(v):
    return jnp.sum(v, axis=0, keepdims=True)


def _tail_post_pre(x, g_post, gate, weight, g_pre, scale, shift):
    def fn(y, rows, vecs):
        (xv,), (gp, gt, g, sc, sh) = rows, vecs
        xo = xv + (weight * gt) * ((y * _rms(y)) * gp)
        return xo, ((xo * _rms(xo)) * g) * (1.0 + sc) + sh

    return _Tail([x], [g_post, gate, g_pre, scale, shift], [(F32, "row"), (BF16, "row")], fn)


def _tail_post_loss(x, target, g, gate, weight):
    def fn(y, rows, vecs):
        (xv, tv), (gv, gt) = rows, vecs
        r = _rms(y)
        yn = y * r
        err = (xv + (weight * gt) * (yn * gv)) - tv
        do = err * (1.0 / y.shape[1])
        dyn = do * ((weight * gt) * gv)
        dy = r * (dyn - yn * jnp.mean(dyn * yn, axis=-1, keepdims=True))
        return do, dy, 0.5 * _col(jnp.mean(err * err, axis=-1, keepdims=True)), _col(do * yn)

    return _Tail([x, target], [g, gate], [(F32, "row"), (BF16, "row"), (F32, "one"), (F32, "sum")], fn)


def _tail_pre_bwd(x, dres, g_pre, scale):
    def fn(dh, rows, vecs):
        (xv, dr), (g, sc) = rows, vecs
        r = _rms(xv)
        n = xv * r
        dn = dh * (g * (1.0 + sc))
        return dr + r * (dn - n * jnp.mean(dn * n, axis=-1, keepdims=True)), _col(dh * n), _col(dh)

    return _Tail([x, dres], [g_pre, scale], [(F32, "row"), (F32, "sum"), (F32, "sum")], fn)


def _tail_pre_post_bwd(x, dres, y, g_pre, scale, g_post, gate, weight):
    def fn(dh, rows, vecs):
        (xv, dr, yv), (g, sc, gp, gt) = rows, vecs
        r = _rms(xv)
        n = xv * r
        dn = dh * (g * (1.0 + sc))
        dx = dr + r * (dn - n * jnp.mean(dn * n, axis=-1, keepdims=True))
        ry = _rms(yv)
        yn = yv * ry
        dyn = dx * ((weight * gt) * gp)
        dy = ry * (dyn - yn * jnp.mean(dyn * yn, axis=-1, keepdims=True))
        return dx, dy, _col(dh * n), _col(dh), _col(dx * yn), _col(dy)

    return _Tail([x, dres, y], [g_pre, scale, g_post, gate],
                 [(F32, "row"), (BF16, "row")] + [(F32, "sum")] * 4, fn)


def _mm_tn_pair(a, b, name, col_sums=False):
    k, m = a.shape
    n = b.shape[1]
    rows = m // N_DEV
    n_chip = N_DEV // 2
    tm = 4 * rows
    tk = _pick(k, (1024, 512, 256, 128))
    nk = k // tk

    def body(a_ref, b_ref, p_ref, own_ref, *rest):
        acc_ref, keep_ref, send_ref, land_ref, send_sems, recv_sems = rest[-6:]
        i, kk = pl.program_id(0), pl.program_id(1)
        x, y, c = _mesh_pos()
        if col_sums:
            cs_ref = rest[0]
            part = jnp.sum(a_ref[...].astype(F32), axis=0, keepdims=True)

            @pl.when(kk == 0)
            def _():
                cs_ref[...] = part

            @pl.when(kk > 0)
            def _():
                cs_ref[...] += part

        def push(chip):
            return pltpu.make_async_remote_copy(
                src_ref=send_ref.at[chip], dst_ref=land_ref.at[chip], send_sem=send_sems.at[chip],
                recv_sem=recv_sems.at[chip], device_id=(x, y, 1 - c), device_id_type=MESH)

        if nk == 1:
            acc = _dot_tn(a_ref[...], b_ref[...])
        else:
            @pl.when(kk == 0)
            def _():
                acc_ref[...] = jnp.zeros_like(acc_ref)

            acc_ref[...] += _dot_tn(a_ref[...], b_ref[...])
            acc = acc_ref

        for t in range(2):
            @pl.when((kk == nk - 1) & (i == t))
            def _(t=t):
                for ob in range(4):
                    chip, core = 2 * t + ob // 2, ob % 2
                    blk = acc[ob * rows:(ob + 1) * rows, :]

                    @pl.when(c == core)
                    def _(chip=chip, blk=blk):
                        keep_ref[chip] = blk

                    @pl.when(c != core)
                    def _(chip=chip, blk=blk):
                        send_ref[chip] = blk.astype(BF16)
                        push(chip).start()

        @pl.when((kk == nk - 1) & (i == 1))
        def _():
            for chip in range(n_chip):
                push(chip).wait_recv()
                val = (keep_ref[chip] + land_ref[chip].astype(F32)).astype(BF16)
                p_ref[chip * rows:(chip + 1) * rows, :] = val

                @pl.when(2 * x + y == chip)
                def _(val=val):
                    own_ref[...] = val

            for chip in range(n_chip):
                push(chip).wait_send()

    out_specs = [pl.BlockSpec((n_chip * rows, n), lambda i, kk: (0, 0)), pl.BlockSpec((rows, n), lambda i, kk: (0, 0))]
    out_shape = [jax.ShapeDtypeStruct((n_chip * rows, n), BF16), jax.ShapeDtypeStruct((rows, n), BF16)]
    if col_sums:
        out_specs.append(pl.BlockSpec((1, tm), lambda i, kk: (0, i)))
        out_shape.append(jax.ShapeDtypeStruct((1, m), F32))
    return _call(body, name=name, grid=(2, nk),
                 in_specs=[pl.BlockSpec((tk, tm), lambda i, kk: (kk, i)), pl.BlockSpec((tk, n), lambda i, kk: (kk, 0))],
                 out_specs=out_specs, out_shape=out_shape, args=[a, b],
                 scratch=[pltpu.VMEM((tm, n) if nk > 1 else (8, LANES), F32), pltpu.VMEM((n_chip, rows, n), F32),
                          pltpu.VMEM((n_chip, rows, n), BF16), pltpu.VMEM((n_chip, rows, n), BF16),
                          pltpu.SemaphoreType.DMA((n_chip,)), pltpu.SemaphoreType.DMA((n_chip,))],
                 sem=("arbitrary", "arbitrary"))


def _ffn_up(h, wg_t, wu_t, name, carry=None):
    s, d = h.shape
    f = wg_t.shape[0]
    tm = _pick(s, (512, 256, 128))
    tf = _pick(f, (1408, 1024, 512, 256, 128))

    def body(h_ref, wg_ref, wu_ref, a_ref, b_ref, u_ref):
        hh = h_ref[...]
        for lo, hi in _pieces(tf):
            a = _dot_nt(hh, wg_ref[lo:hi, :])
            b = _dot_nt(hh, wu_ref[lo:hi, :])
            a_ref[:, lo:hi] = a.astype(BF16)
            b_ref[:, lo:hi] = b.astype(BF16)
            u_ref[:, lo:hi] = ((a * _sigmoid(a)) * b).astype(BF16)

    w_spec = pl.BlockSpec((tf, d), lambda j, i: (j, 0))
    o_spec = pl.BlockSpec((tm, tf), lambda j, i: (i, j))
    o_shape = jax.ShapeDtypeStruct((s, f), BF16)
    return _call(body, name=name, grid=(f // tf, s // tm),
                 in_specs=[pl.BlockSpec((tm, d), lambda j, i: (i, 0)), w_spec, w_spec],
                 out_specs=(o_spec, o_spec, o_spec), out_shape=(o_shape, o_shape, o_shape),
                 args=[h, wg_t, wu_t], sem=("parallel", "parallel"), carry=carry)


def _ffn_down_bwd(dy, wd, a, b, name, carry=None):
    s, d = dy.shape
    f = wd.shape[0]
    tm = _pick(s, (512, 256, 128))
    tf = _pick(f, (1408, 1024, 512, 256, 128))

    def body(dy_ref, wd_ref, a_ref, b_ref, da_ref, db_ref):
        dyv = dy_ref[...]
        for lo, hi in _pieces(tf):
            du = _dot_nt(dyv, wd_ref[lo:hi, :])
            a = a_ref[:, lo:hi].astype(F32)
            b = b_ref[:, lo:hi].astype(F32)
            sig = _sigmoid(a)
            da_ref[:, lo:hi] = (du * b * (sig * (1.0 + a * (1.0 - sig)))).astype(BF16)
            db_ref[:, lo:hi] = (du * (a * sig)).astype(BF16)

    t_spec = pl.BlockSpec((tm, tf), lambda j, i: (i, j))
    o_shape = jax.ShapeDtypeStruct((s, f), BF16)
    return _call(body, name=name, grid=(f // tf, s // tm),
                 in_specs=[pl.BlockSpec((tm, d), lambda j, i: (i, 0)), pl.BlockSpec((tf, d), lambda j, i: (j, 0)),
                           t_spec, t_spec],
                 out_specs=(t_spec, t_spec), out_shape=(o_shape, o_shape), args=[dy, wd, a, b],
                 sem=("parallel", "parallel"), carry=carry)


def _row_tile(s):
    return _pick(s, (256, 128, 64))


def _vec_spec(d):
    return pl.BlockSpec((1, d), lambda i: (0, 0))


def _pre_norm(x, g, scale, shift, name):
    s, d = x.shape
    ts = _row_tile(s)

    def body(x_ref, g_ref, sc_ref, sh_ref, h_ref):
        xv = x_ref[...]
        r = lax.rsqrt(jnp.mean(xv * xv, axis=-1, keepdims=True) + EPS)
        h_ref[...] = (((xv * r) * g_ref[...]) * (1.0 + sc_ref[...]) + sh_ref[...]).astype(BF16)

    row = pl.BlockSpec((ts, d), lambda i: (i, 0))
    return _call(body, name=name, grid=(s // ts,), in_specs=[row, _vec_spec(d), _vec_spec(d), _vec_spec(d)],
                 out_specs=row, out_shape=jax.ShapeDtypeStruct((s, d), BF16), args=[x, g, scale, shift],
                 sem=("parallel",))


def _group_norm_cat(oa, ob, ga, gb):
    s = oa.shape[0]
    ts = _row_tile(s)

    def body(oa_ref, ob_ref, ga_ref, gb_ref, y_ref):
        for o_ref, g_ref, lo, w in ((oa_ref, ga_ref, 0, QA), (ob_ref, gb_ref, QA, QB)):
            ov = o_ref[...]
            r = lax.rsqrt(jnp.mean(ov * ov, axis=-1, keepdims=True) + EPS)
            y_ref[:, lo:lo + w] = ((ov * r) * g_ref[...]).astype(BF16)

    return _call(body, name="group_norm_cat", grid=(s // ts,),
                 in_specs=[pl.BlockSpec((ts, QA), lambda i: (i, 0)), pl.BlockSpec((ts, QB), lambda i: (i, 0)),
                           _vec_spec(QA), _vec_spec(QB)],
                 out_specs=pl.BlockSpec((ts, QA + QB), lambda i: (i, 0)),
                 out_shape=jax.ShapeDtypeStruct((s, QA + QB), BF16), args=[oa, ob, ga, gb], sem=("parallel",))


def _group_norm_bwd(dy, oa, ob, ga, gb):
    s = oa.shape[0]
    ts = _row_tile(s)

    def body(dy_ref, oa_ref, ob_ref, ga_ref, gb_ref, doa_ref, dob_ref, dga_ref, dgb_ref):
        @pl.when(pl.program_id(0) == 0)
        def _():
            dga_ref[...] = jnp.zeros_like(dga_ref)
            dgb_ref[...] = jnp.zeros_like(dgb_ref)

        for o_ref, g_ref, do_ref, dg_ref, lo, w in ((oa_ref, ga_ref, doa_ref, dga_ref, 0, QA),
                                                    (ob_ref, gb_ref, dob_ref, dgb_ref, QA, QB)):
            ov = o_ref[...]
            dyv = dy_ref[:, lo:lo + w]
            r = lax.rsqrt(jnp.mean(ov * ov, axis=-1, keepdims=True) + EPS)
            n = ov * r
            dn = dyv * g_ref[...]
            do_ref[...] = r * (dn - n * jnp.mean(dn * n, axis=-1, keepdims=True))
            dg_ref[...] += jnp.sum(dyv * n, axis=0, keepdims=True)

    ra = pl.BlockSpec((ts, QA), lambda i: (i, 0))
    rb = pl.BlockSpec((ts, QB), lambda i: (i, 0))
    return _call(body, name="group_norm_bwd", grid=(s // ts,),
                 in_specs=[pl.BlockSpec((ts, QA + QB), lambda i: (i, 0)), ra, rb, _vec_spec(QA), _vec_spec(QB)],
                 out_specs=(ra, rb, _vec_spec(QA), _vec_spec(QB)),
                 out_shape=(jax.ShapeDtypeStruct((s, QA), F32), jax.ShapeDtypeStruct((s, QB), F32),
                            jax.ShapeDtypeStruct((1, QA), F32), jax.ShapeDtypeStruct((1, QB), F32)),
                 args=[dy, oa, ob, ga, gb], sem=("arbitrary",))


def _n_variants(n_back):
    return -(-n_back // QG) + 1


def _alibi_bias():
    i = np.arange(QROWS)[:, None]
    j = np.arange((QG + BACK_A) * CHUNK)[None, :]
    dist = np.abs(BACK_A * CHUNK + i - j).astype(np.float32)
    dc = j // CHUNK - i // CHUNK
    valid = (dc >= 0) & (dc <= BACK_A)
    slopes = np.array([2.0 ** (-8.0 * (h + 1) / H_A) for h in range(H_A)], dtype=np.float32)
    bias = -slopes[:, None, None] * dist[None]
    out = [np.where((valid & (j >= (BACK_A - QG * v) * CHUNK))[None], bias, np.float32(NEG_INF))
           for v in range(_n_variants(BACK_A))]
    return jnp.asarray(np.stack(out).astype(np.float32))


def _rel_index_matrix():
    cc = np.arange(SKEW)
    dist = np.where(cc < SKEW - QROWS, BACK_B * CHUNK - cc, BACK_B * CHUNK + SKEW - cc)
    idx = np.clip(dist, -REL_CLIP, REL_CLIP) + REL_CLIP
    m = np.zeros((SKEW, N_REL), np.float32)
    m[cc, idx] = 1.0
    return jnp.asarray(m)


def _toeplitz_bias(vec, carry=None):
    lk = (QG + BACK_B) * CHUNK
    nv = _n_variants(BACK_B)

    def body(v_ref, o_ref):
        xv = jnp.broadcast_to(v_ref[0], (QROWS, SKEW))
        row = lax.broadcasted_iota(jnp.int32, (QROWS, SKEW), 0)
        for bit in range(QROWS.bit_length() - 1):
            xv = jnp.where((row >> bit) & 1 == 1, pltpu.roll(xv, 1 << bit, 1), xv)
        ri = lax.broadcasted_iota(jnp.int32, (QROWS, lk), 0) // CHUNK
        col = lax.broadcasted_iota(jnp.int32, (QROWS, lk), 1)
        ci = col // CHUNK
        valid = (ci - ri >= 0) & (ci - ri <= BACK_B)
        for v in range(nv):
            o_ref[v, 0] = jnp.where(valid & (col >= (BACK_B - QG * v) * CHUNK), xv[:, :lk], NEG_INF)

    return _call(body, name="toeplitz_bias", grid=(H_B,),
                 in_specs=[pl.BlockSpec((1, 1, SKEW), lambda h: (h, 0, 0))],
                 out_specs=pl.BlockSpec((nv, 1, QROWS, lk), lambda h: (0, h, 0, 0)),
                 out_shape=jax.ShapeDtypeStruct((nv, H_B, QROWS, lk), F32), args=[vec], sem=("parallel",),
                 carry=carry)


def _diagonal_sums(dbias):
    lk = dbias.shape[2]

    def body(d_ref, o_ref):
        xp = jnp.concatenate([d_ref[0], jnp.zeros((QROWS, SKEW - lk), F32)], axis=1)
        xv = xp[0:CHUNK]
        for q in range(1, QG):
            xv = xv + pltpu.roll(xp[q * CHUNK:(q + 1) * CHUNK], SKEW - q * CHUNK, 1)
        row = lax.broadcasted_iota(jnp.int32, (CHUNK, SKEW), 0)
        for bit in range(CHUNK.bit_length() - 1):
            xv = jnp.where((row >> bit) & 1 == 1, pltpu.roll(xv, SKEW - (1 << bit), 1), xv)
        o_ref[0] = jnp.sum(xv, axis=0, keepdims=True)

    return _call(body, name="diagonal_sums", grid=(H_B,),
                 in_specs=[pl.BlockSpec((1, QROWS, lk), lambda h: (h, 0, 0))],
                 out_specs=pl.BlockSpec((1, 1, SKEW), lambda h: (h, 0, 0)),
                 out_shape=jax.ShapeDtypeStruct((H_B, 1, SKEW), F32), args=[dbias], sem=("parallel",))


def _attn_common(s, n_back, gqa, q_col, k_col, v_col):
    lk = (QG + n_back) * CHUNK
    pad = n_back * CHUNK
    wide = TPS * LANES
    q_spec = pl.BlockSpec((QROWS, wide), lambda t, g: (g, q_col // TPS + t))
    if gqa:
        k_spec = pl.BlockSpec((s, LANES), lambda t, g: (0, k_col))
        v_spec = pl.BlockSpec((s, LANES), lambda t, g: (0, v_col))
    else:
        k_spec = pl.BlockSpec((s, wide), lambda t, g: (0, k_col // TPS + t))
        v_spec = pl.BlockSpec((s, wide), lambda t, g: (0, v_col // TPS + t))
    last_variant = _n_variants(n_back) - 1
    bias_spec = pl.BlockSpec((None, 2 * TPS, QROWS, lk), lambda t, g: (jnp.minimum(g, last_variant), t, 0, 0))
    tile_spec = pl.BlockSpec((QROWS, wide), lambda t, g: (g, t))
    return lk, pad, q_spec, k_spec, v_spec, bias_spec, tile_spec


def _attention_fwd(proj, bias, sinks, *, n_back, gqa, q_col, k_col, v_col, name, carry=None):
    s = proj.shape[0]
    lk, pad, q_spec, k_spec, v_spec, bias_spec, tile_spec = _attn_common(s, n_back, gqa, q_col, k_col, v_col)
    n_t, n_g = 512 // (TPS * LANES), s // QROWS
    kv_wide = LANES if gqa else TPS * LANES

    def body(*refs):
        if gqa:
            q_ref, k_ref, v_ref, bias_ref, sink_ref, o_ref, l_ref, kpad, vpad = refs
        else:
            q_ref, k_ref, v_ref, bias_ref, o_ref, l_ref, kpad, vpad = refs
        t, g = pl.program_id(0), pl.program_id(1)

        @pl.when(g == 0)
        def _():
            kpad[0:pad, :] = jnp.zeros((pad, kv_wide), BF16)
            vpad[0:pad, :] = jnp.zeros((pad, kv_wide), BF16)
            kpad[pad:, :] = k_ref[...]
            vpad[pad:, :] = v_ref[...]

        start = pl.multiple_of(g * QROWS, QROWS)
        half = lax.broadcasted_iota(jnp.int32, (QROWS, LANES), 1) // HEAD_DIM
        for tt in range(TPS):
            lanes = slice(tt * LANES, (tt + 1) * LANES)
            kv_lanes = slice(0, LANES) if gqa else lanes
            kb = kpad[pl.ds(start, lk), kv_lanes]
            vb = vpad[pl.ds(start, lk), kv_lanes]
            q = q_ref[:, lanes] * (HEAD_DIM ** -0.5)
            if gqa:
                hk = (TPS * t + tt) // 2
                q_rolled = pltpu.roll(q.astype(F32), HEAD_DIM, 1).astype(BF16)
            outs, lses = [], []
            for e in range(2):
                if gqa:
                    kv_half = hk
                    src = jnp.where(hk == e, q, q_rolled)
                else:
                    kv_half = e
                    src = q
                qm = jnp.where(half == kv_half, src, jnp.zeros_like(src))
                sc = _dot_nt(qm, kb) + bias_ref[2 * tt + e]
                m = jnp.max(sc, axis=-1, keepdims=True)
                if gqa:
                    sk = sink_ref[2 * (TPS * t + tt) + e]
                    m = jnp.maximum(m, sk)
                p = jnp.exp(sc - m)
                l = jnp.sum(p, axis=-1, keepdims=True)
                if gqa:
                    l = l + jnp.exp(sk - m)
                pn = p / l
                outs.append(_dot(pn.astype(BF16), vb))
                lses.append(m + jnp.log(l))
            if gqa:
                same = jnp.where(hk == 0, outs[0], outs[1])
                other = jnp.where(hk == 0, outs[1], outs[0])
                o_ref[:, lanes] = jnp.where(half == hk, same, pltpu.roll(other, HEAD_DIM, 1))
            else:
                o_ref[:, lanes] = jnp.where(half == 0, outs[0], outs[1])
            l_ref[:, lanes] = jnp.where(half == 0, lses[0], lses[1])

    in_specs = [q_spec, k_spec, v_spec, bias_spec] + ([SMEM_SPEC] if gqa else [])
    args = [proj, proj, proj, bias] + ([sinks] if gqa else [])
    o_shape = jax.ShapeDtypeStruct((s, 512), F32)
    return _call(body, name=name, grid=(n_t, n_g), in_specs=in_specs, out_specs=(tile_spec, tile_spec),
                 out_shape=(o_shape, o_shape), args=args,
                 scratch=[pltpu.VMEM((s + pad, kv_wide), BF16), pltpu.VMEM((s + pad, kv_wide), BF16)],
                 sem=("arbitrary", "arbitrary"), carry=carry)


def _attention_bwd(proj, bias, sinks, do, lse, *, n_back, gqa, q_col, k_col, v_col, name, carry=None):
    s = proj.shape[0]
    lk, pad, q_spec, k_spec, v_spec, bias_spec, tile_spec = _attn_common(s, n_back, gqa, q_col, k_col, v_col)
    n_t, n_g = 512 // (TPS * LANES), s // QROWS
    kv_wide = LANES if gqa else TPS * LANES

    def body(*refs):
        if gqa:
            (q_ref, k_ref, v_ref, bias_ref, sink_ref, do_ref, l_ref,
             dq_ref, dk_ref, dv_ref, dsink_ref, kpad, vpad, dkpad, dvpad) = refs
        else:
            (q_ref, k_ref, v_ref, bias_ref, do_ref, l_ref,
             dq_ref, dk_ref, dv_ref, dbias_ref, kpad, vpad, dkpad, dvpad) = refs
        t, g = pl.program_id(0), pl.program_id(1)

        @pl.when(g == 0)
        def _():
            kpad[0:pad, :] = jnp.zeros((pad, kv_wide), BF16)
            vpad[0:pad, :] = jnp.zeros((pad, kv_wide), BF16)
            kpad[pad:, :] = k_ref[...]
            vpad[pad:, :] = v_ref[...]
            if gqa:
                dsink_ref[...] = jnp.zeros_like(dsink_ref)
            else:
                dbias_ref[...] = jnp.zeros_like(dbias_ref)

        @pl.when((g == 0) & (t == 0) if gqa else g == 0)
        def _():
            dkpad[...] = jnp.zeros_like(dkpad)
            dvpad[...] = jnp.zeros_like(dvpad)

        start = pl.multiple_of(g * QROWS, QROWS)
        half = lax.broadcasted_iota(jnp.int32, (QROWS, LANES), 1) // HEAD_DIM
        for tt in range(TPS):
            lanes = slice(tt * LANES, (tt + 1) * LANES)
            kv_lanes = slice(0, LANES) if gqa else lanes
            kb = kpad[pl.ds(start, lk), kv_lanes]
            vb = vpad[pl.ds(start, lk), kv_lanes]
            q = q_ref[:, lanes]
            dov = do_ref[:, lanes]
            lv = l_ref[:, lanes]
            if gqa:
                hk = (TPS * t + tt) // 2
                q_rolled = pltpu.roll(q.astype(F32), HEAD_DIM, 1).astype(BF16)
                do_rolled = pltpu.roll(dov, HEAD_DIM, 1)
            dqs = []
            dk_acc = jnp.zeros((lk, LANES), F32)
            dv_acc = jnp.zeros((lk, LANES), F32)
            for e in range(2):
                if gqa:
                    kv_half = hk
                    src = jnp.where(hk == e, q, q_rolled)
                    do_src = jnp.where(hk == e, dov, do_rolled)
                else:
                    kv_half = e
                    src = q
                    do_src = dov
                qm = jnp.where(half == kv_half, src, jnp.zeros_like(src))
                dom = jnp.where(half == kv_half, do_src, 0.0).astype(BF16)
                lcol = jnp.max(jnp.where(half == e, lv, -jnp.inf), axis=-1, keepdims=True)
                sc = _dot_nt(qm * (HEAD_DIM ** -0.5), kb) + bias_ref[2 * tt + e]
                pn = jnp.exp(sc - lcol)
                dp = _dot_nt(dom, vb)
                delta = jnp.sum(pn * dp, axis=-1, keepdims=True)
                ds = pn * (dp - delta)
                if gqa:
                    p_sink = jnp.exp(sink_ref[2 * (TPS * t + tt) + e] - lcol)
                    dsk = -jnp.sum(p_sink * delta, axis=0, keepdims=True)
                    row = 2 * tt + e
                    dsink_ref[0, row:row + 1, :] += jnp.broadcast_to(dsk, (1, LANES))
                else:
                    dbias_ref[2 * tt + e] += ds
                dsb = (ds * (HEAD_DIM ** -0.5)).astype(BF16)
                dqs.append(_dot(dsb, kb))
                dk_acc = dk_acc + _dot_tn(dsb, qm)
                dv_acc = dv_acc + _dot_tn(pn.astype(BF16), dom)
            dkpad[pl.ds(start, lk), kv_lanes] += dk_acc
            dvpad[pl.ds(start, lk), kv_lanes] += dv_acc
            if gqa:
                same = jnp.where(hk == 0, dqs[0], dqs[1])
                other = jnp.where(hk == 0, dqs[1], dqs[0])
                dq_ref[:, lanes] = jnp.where(half == hk, same, pltpu.roll(other, HEAD_DIM, 1)).astype(BF16)
            else:
                dq_ref[:, lanes] = jnp.where(half == 0, dqs[0], dqs[1]).astype(BF16)

        @pl.when((g == n_g - 1) & (t == n_t - 1) if gqa else g == n_g - 1)
        def _():
            dk_ref[...] = dkpad[pad:, :].astype(BF16)
            dv_ref[...] = dvpad[pad:, :].astype(BF16)

    in_specs = [q_spec, k_spec, v_spec, bias_spec] + ([SMEM_SPEC] if gqa else []) + [tile_spec, tile_spec]
    args = [proj, proj, proj, bias] + ([sinks] if gqa else []) + [do, lse]
    if gqa:
        kv_out = pl.BlockSpec((s, LANES), lambda t, g: (0, 0))
        kv_shape = jax.ShapeDtypeStruct((s, LANES), BF16)
        extra_spec = pl.BlockSpec((1, 8, LANES), lambda t, g: (t, 0, 0))
        extra_shape = jax.ShapeDtypeStruct((n_t, 8, LANES), F32)
    else:
        kv_out = pl.BlockSpec((s, kv_wide), lambda t, g: (0, t))
        kv_shape = jax.ShapeDtypeStruct((s, 512), BF16)
        extra_spec = pl.BlockSpec((2 * TPS, QROWS, lk), lambda t, g: (t, 0, 0))
        extra_shape = jax.ShapeDtypeStruct(bias.shape[1:], F32)
    return _call(body, name=name, grid=(n_t, n_g), in_specs=in_specs,
                 out_specs=(tile_spec, kv_out, kv_out, extra_spec),
                 out_shape=(jax.ShapeDtypeStruct((s, 512), BF16), kv_shape, kv_shape, extra_shape), args=args,
                 scratch=[pltpu.VMEM((s + pad, kv_wide), BF16), pltpu.VMEM((s + pad, kv_wide), BF16),
                          pltpu.VMEM((s + pad, kv_wide), F32), pltpu.VMEM((s + pad, kv_wide), F32)],
                 sem=("arbitrary", "arbitrary"), carry=carry)


def _sum_rows8(g):
    n = g.shape[2]

    def body(g_ref, o_ref):
        acc = g_ref[0]
        for j in range(1, N_DEV):
            acc = acc + g_ref[j]
        o_ref[...] = acc

    return pl.pallas_call(
        body, name="sum_small_grads", in_specs=[VMEM_SPEC], out_specs=VMEM_SPEC,
        out_shape=jax.ShapeDtypeStruct((1, n), F32), compiler_params=_params(),
    )(g)


def _ada_weight_grad(sc_t, dmod_cols):
    d = sc_t.shape[0]
    w = dmod_cols.shape[1]
    td = _pick(d, (256, 128))

    def body(sc_ref, dm_ref, o_ref):
        scv = sc_ref[...]
        dmv = dm_ref[...]
        acc = scv[:, 0:1] * dmv[0:1, :]
        for b in range(1, N_DEV):
            acc = acc + scv[:, b:b + 1] * dmv[b:b + 1, :]
        o_ref[...] = acc

    return _call(body, name="ada_weight_grad", grid=(d // td,),
                 in_specs=[pl.BlockSpec((td, N_DEV), lambda i: (i, 0)), pl.BlockSpec((N_DEV, w), lambda i: (0, 0))],
                 out_specs=pl.BlockSpec((td, w), lambda i: (i, 0)), out_shape=jax.ShapeDtypeStruct((d, w), F32),
                 args=[sc_t, dmod_cols], sem=("parallel",))


def _adamw_update(w, gv, m, v):
    nm = ADAM_B1 * m + (1.0 - ADAM_B1) * gv
    nv = ADAM_B2 * v + (1.0 - ADAM_B2) * (gv * gv)
    m_hat = nm / (1.0 - ADAM_B1 ** ADAM_STEP)
    v_hat = nv / (1.0 - ADAM_B2 ** ADAM_STEP)
    return -ADAM_LR * (m_hat / (jnp.sqrt(v_hat) + ADAM_EPS) + ADAM_WD * w), nm, nv


def _adamw(w, g, m, v, name):
    rows, cols = w.shape
    tr = _pick(rows, (256, 176, 128, 88, 64)) if rows > 256 else rows

    def body(w_ref, g_ref, m_ref, v_ref, d_ref, nm_ref, nv_ref):
        d_ref[...], nm_ref[...], nv_ref[...] = _adamw_update(w_ref[...], g_ref[...], m_ref[...], v_ref[...])

    spec = pl.BlockSpec((tr, cols), lambda i: (i, 0))
    shape = jax.ShapeDtypeStruct((rows, cols), F32)
    return _call(body, name=name, grid=(rows // tr,), in_specs=[spec] * 4, out_specs=(spec, spec, spec),
                 out_shape=(shape, shape, shape), args=[w, g, m, v], sem=("parallel",))


def _adamw_from_slots(w, own, slots, m, v, name):
    n_slots, rows, k = slots.shape

    def body(o_ref, s_ref, w_ref, m_ref, v_ref, g_ref, d_ref, nm_ref, nv_ref):
        gv = o_ref[...].astype(F32)
        for j in range(n_slots):
            gv = gv + s_ref[j].astype(F32)
        g_ref[...] = gv
        d_ref[...], nm_ref[...], nv_ref[...] = _adamw_update(w_ref[...], gv, m_ref[...], v_ref[...])

    tr = rows // 2 if rows % 32 == 0 else rows
    spec = pl.BlockSpec((tr, k), lambda i: (i, 0))
    shape = jax.ShapeDtypeStruct((rows, k), F32)
    return _call(body, name=name, grid=(rows // tr,),
                 in_specs=[spec, pl.BlockSpec((n_slots, tr, k), lambda i: (0, i, 0)), spec, spec, spec],
                 out_specs=(spec, spec, spec, spec), out_shape=(shape, shape, shape, shape),
                 args=[own, slots, w, m, v], sem=("parallel",))


def _adamw_small(g, w, m, v, sizes):
    n = w.shape[1]
    offs, off = [], 0
    for size in sizes:
        offs.append(off)
        off += size + (-size % LANES)

    def body(g_ref, w_ref, m_ref, v_ref, *out_refs):
        gv = g_ref[:, 0:n]
        dv, nm, nv = _adamw_update(w_ref[...], gv, m_ref[...], v_ref[...])
        for j, (o, size) in enumerate(zip(offs, sizes)):
            for k, val in enumerate((gv, dv, nm, nv)):
                out_refs[4 * j + k][...] = val[:, o:o + size]

    shapes = [jax.ShapeDtypeStruct((1, size), F32) for size in sizes for _ in range(4)]
    return pl.pallas_call(
        body, name="adamw_small", in_specs=[VMEM_SPEC] * 4, out_specs=tuple([VMEM_SPEC] * len(shapes)),
        out_shape=tuple(shapes), compiler_params=_params(),
    )(g, w, m, v)


SMALL = ("b_ada", "g_pre_ffn1", "g_post_ffn1", "g_pre_mix", "b_in", "sinks_a", "rel_bias_b", "g_grp_a",
         "g_grp_b", "b_out", "g_post_mix", "g_pre_ffn2", "g_post_ffn2")
WEIGHTS = ("w_ada", "b_ada", "g_pre_ffn1", "w_gate1", "w_up1", "w_down1", "g_post_ffn1", "g_pre_mix", "w_in",
           "b_in", "sinks_a", "rel_bias_b", "g_grp_a", "g_grp_b", "w_out", "b_out", "g_post_mix", "g_pre_ffn2",
           "w_gate2", "w_up2", "w_down2", "g_post_ffn2")


def kernel(x, c, w_ada, b_ada, g_pre_ffn1, w_gate1, w_up1, w_down1, g_post_ffn1, g_pre_mix, w_in, b_in, sinks_a, rel_bias_b, g_grp_a, g_grp_b, w_out, b_out, g_post_mix, g_pre_ffn2, w_gate2, w_up2, w_down2, g_post_ffn2, loss_target, m_w_ada, m_b_ada, m_g_pre_ffn1, m_w_gate1, m_w_up1, m_w_down1, m_g_post_ffn1, m_g_pre_mix, m_w_in, m_b_in, m_sinks_a, m_rel_bias_b, m_g_grp_a, m_g_grp_b, m_w_out, m_b_out, m_g_post_mix, m_g_pre_ffn2, m_w_gate2, m_w_up2, m_w_down2, m_g_post_ffn2, v_w_ada, v_b_ada, v_g_pre_ffn1, v_w_gate1, v_w_up1, v_w_down1, v_g_post_ffn1, v_g_pre_mix, v_w_in, v_b_in, v_sinks_a, v_rel_bias_b, v_g_grp_a, v_g_grp_b, v_w_out, v_b_out, v_g_post_mix, v_g_pre_ffn2, v_w_gate2, v_w_up2, v_w_down2, v_g_post_ffn2):
    given = dict(locals())
    weights = {n: given[n] for n in WEIGHTS}
    mom_m = {n: given["m_" + n] for n in WEIGHTS}
    mom_v = {n: given["v_" + n] for n in WEIGHTS}

    me = 4 * lax.axis_index("x") + 2 * lax.axis_index("y") + lax.axis_index("c")
    xs = x[0]
    tgt = loss_target[0]
    d_model = xs.shape[1]
    ada_cols = w_ada.shape[2]

    sh = {"wg1": w_gate1[0].T, "wu1": w_up1[0].T, "wd1": w_down1[0], "win": w_in[0].T, "wo": w_out[0],
          "wg2": w_gate2[0].T, "wu2": w_up2[0].T, "wd2": w_down2[0]}
    sh = {k: v.astype(BF16) for k, v in sh.items()}

    def gather(*names):
        return _gather_carry([sh[n] for n in names])

    bias_a = _alibi_bias()
    rel_m = _rel_index_matrix()
    rel_vec = jnp.dot(rel_bias_b[0], rel_m.T, precision=lax.Precision.HIGHEST)
    bias_b, (wg1, wu1) = _toeplitz_bias(rel_vec.reshape(H_B, 1, SKEW), carry=gather("wg1", "wu1"))

    b_cols = lax.dynamic_slice(b_ada, (0, me * ada_cols), (1, ada_cols))
    (sc_all, mod_rows), _ = _ada_forward(c, w_ada[0], b_cols, _Carry([], [], [], lambda *a: None, lambda *a: None))
    mod = mod_rows.reshape(N_MOD, d_model)
    shift1, scale1, gate1, shift2, scale2, gate2, shift3, scale3, gate3 = (mod[i:i + 1] for i in range(N_MOD))

    h1 = _pre_norm(xs, g_pre_ffn1, scale1, shift1, "pre_norm_ffn1")
    (a1, b1, u1), (wd1,) = _ffn_up(h1, wg1, wu1, "ffn_up_ffn1", carry=gather("wd1"))
    (y1, x1, h2), (win,) = _mm_nn(
        [(u1, wd1)], "ffn_down_ffn1", F32, carry=gather("win"),
        tail=_tail_post_pre(xs, g_post_ffn1, gate1, 0.5, g_pre_mix, scale2, shift2))

    proj, (wo,) = _mm_nt(h2, win, "in_proj", BF16, bias=b_in, carry=gather("wo"))
    sinks = sinks_a[0]
    cfg_a = dict(n_back=BACK_A, gqa=True, q_col=0, k_col=QA // LANES, v_col=(QA + KVA) // LANES)
    cfg_b = dict(n_back=BACK_B, gqa=False, q_col=(QA + 2 * KVA) // LANES, k_col=(QA + 2 * KVA + QB) // LANES,
                 v_col=(QA + 2 * KVA + 2 * QB) // LANES)
    (oa, lse_a), (wg2,) = _attention_fwd(proj, bias_a, sinks, name="attn_a", carry=gather("wg2"), **cfg_a)
    (ob, lse_b), (wu2,) = _attention_fwd(proj, bias_b, None, name="attn_b", carry=gather("wu2"), **cfg_b)
    ycat = _group_norm_cat(oa, ob, g_grp_a, g_grp_b)
    ymix, x2, h3 = _mm_nn([(ycat, wo)], "out_proj", F32, bias=b_out,
                          tail=_tail_post_pre(x1, g_post_mix, gate2, 1.0, g_pre_ffn2, scale3, shift3))

    (a3, b3, u3), (wd2,) = _ffn_up(h3, wg2, wu2, "ffn_up_ffn2", carry=gather("wd2"))

    flights, own = {}, {}

    def grad_pair(key, a_mat, b_mat, name):
        part, own[key] = _mm_tn_pair(a_mat, b_mat, name)
        return part

    def scatter_start(tag, after_vec, **parts):
        names = list(parts)
        sems, p_thru, lands, token = _scatter_start([parts[n] for n in names], "scatter_start_" + tag)
        flights[tag] = (names, sems, p_thru, lands)
        return after_vec + token[0:1, 0:1]

    dx3, dy, loss_part, s1 = _mm_nn([(u3, wd2)], "ffn_down_ffn2", None,
                                    tail=_tail_post_loss(x2, tgt, g_post_ffn2, gate3, 0.5))
    da, db = _ffn_down_bwd(dy, wd2, a3, b3, "ffn_down_bwd_ffn2")
    dwd2 = grad_pair("wd2", u3, dy, "grad_wd_ffn2")
    dwg2 = grad_pair("wg2", da, h3, "grad_wg_ffn2")
    dwu2 = grad_pair("wu2", db, h3, "grad_wu_ffn2")
    g_pre_tied = scatter_start("ffn2", g_pre_ffn2, wd2=dwd2, wg2=dwg2, wu2=dwu2)
    dx2, dymix, s2, s3, s1m, db_out = _mm_nn(
        [(da, wg2), (db, wu2)], "ffn_up_bwd_ffn2", None,
        tail=_tail_pre_post_bwd(x2, dx3, ymix, g_pre_tied, scale3, g_post_mix, gate2, 1.0))
    sm3 = dict(shift=s3, scale=s2 * g_pre_ffn2, gate=0.5 * g_post_ffn2 * s1,
               g_pre=(1.0 + scale3) * s2, g_post=(0.5 * gate3) * s1)

    dycat = _mm_nt(dymix, wo, "out_proj_bwd", F32)
    dwo = grad_pair("wo", ycat, dymix, "grad_wo")
    doa, dob, dg_a, dg_b = _group_norm_bwd(dycat, oa, ob, g_grp_a, g_grp_b)
    dqa, dka, dva, dsink = _attention_bwd(proj, bias_a, sinks, doa, lse_a, name="attn_a_bwd", **cfg_a)
    dqb, dkb, dvb, dbias = _attention_bwd(proj, bias_b, None, dob, lse_b, name="attn_b_bwd", **cfg_b)
    dproj = jnp.concatenate([dqa, dka, dva, dqb, dkb, dvb], axis=1)
    dwin, own["win"], db_in = _mm_tn_pair(dproj, h2, "grad_win", col_sums=True)
    g_pre_tied = scatter_start("mix", g_pre_mix, wo=dwo, win=dwin)
    dx1, dy, s2m, s3m, s1, _ = _mm_nn(
        [(dproj, win)], "in_proj_bwd", None,
        tail=_tail_pre_post_bwd(x1, dx2, y1, g_pre_tied, scale2, g_post_ffn1, gate1, 0.5))
    d_rel = jnp.dot(_diagonal_sums(dbias).reshape(H_B, SKEW), rel_m, precision=lax.Precision.HIGHEST)
    d_sinks = dsink[:, :2 * TPS, 0].reshape(1, H_A)

    da, db = _ffn_down_bwd(dy, wd1, a1, b1, "ffn_down_bwd_ffn1")
    dwd1 = grad_pair("wd1", u1, dy, "grad_wd_ffn1")
    dwg1 = grad_pair("wg1", da, h1, "grad_wg_ffn1")
    dwu1 = grad_pair("wu1", db, h1, "grad_wu_ffn1")
    g_pre_tied = scatter_start("ffn1", g_pre_ffn1, wd1=dwd1, wg1=dwg1, wu1=dwu1)
    dx0, s2, s3 = _mm_nn([(da, wg1), (db, wu1)], "ffn_up_bwd_ffn1", None,
                         tail=_tail_pre_bwd(xs, dx1, g_pre_tied, scale1))
    sm1 = dict(shift=s3, scale=s2 * g_pre_ffn1, gate=0.5 * g_post_ffn1 * s1,
               g_pre=(1.0 + scale1) * s2, g_post=(0.5 * gate1) * s1)

    dmod = jnp.concatenate([sm1["shift"], sm1["scale"], sm1["gate"],
                            s3m, s2m * g_pre_mix, g_post_mix * s1m,
                            sm3["shift"], sm3["scale"], sm3["gate"]], axis=1)
    small_parts = {
        "b_ada": dmod, "g_pre_ffn1": sm1["g_pre"], "g_post_ffn1": sm1["g_post"],
        "g_pre_mix": (1.0 + scale2) * s2m, "b_in": db_in, "sinks_a": d_sinks,
        "rel_bias_b": d_rel.reshape(1, H_B * N_REL), "g_grp_a": dg_a, "g_grp_b": dg_b, "b_out": db_out,
        "g_post_mix": gate2 * s1m, "g_pre_ffn2": sm3["g_pre"], "g_post_ffn2": sm3["g_post"]}
    sizes = [small_parts[n].shape[1] for n in SMALL]

    def pack(parts):
        cells = []
        for p in parts:
            cells.append(p)
            if p.shape[1] % LANES:
                cells.append(jnp.zeros((1, -p.shape[1] % LANES), F32))
        return jnp.concatenate(cells, axis=1)

    packed = pack([small_parts[n] for n in SMALL] + [loss_part])
    n_packed = packed.shape[1]
    small_sems, packed_thru, small_land, small_token = _small_gather_start(packed)

    out_g, out_d, out_m, out_v = {}, {}, {}, {}
    groups = (("ffn2", (("w_gate2", "wg2", True), ("w_up2", "wu2", True), ("w_down2", "wd2", False))),
              ("mix", (("w_in", "win", True), ("w_out", "wo", False))),
              ("ffn1", (("w_gate1", "wg1", True), ("w_up1", "wu1", True), ("w_down1", "wd1", False))))
    after = small_token
    for tag, members in groups:
        names, sems, p_thru, lands = flights[tag]
        _, l_done = _scatter_wait(sems, p_thru, lands, after, "scatter_wait_" + tag)
        slots = dict(zip(names, l_done))
        for n, key, transposed in members:
            view = (lambda t: t.T) if transposed else (lambda t: t)
            res = _adamw_from_slots(view(weights[n][0]), own[key], slots[key], view(mom_m[n][0]),
                                    view(mom_v[n][0]), "adamw_" + n)
            out_g[n], out_d[n], out_m[n], out_v[n] = (view(t)[None] for t in res)
            after = res[3]

    packed_done, small_land = _small_gather_wait(small_sems, packed_thru, small_land, after)
    gathered = lax.dynamic_update_slice(small_land, packed_done[None], (me, 0, 0))
    small_sum = _sum_rows8(gathered)
    loss = small_sum[0, n_packed - LANES]
    dmod_cols = lax.dynamic_slice(gathered.reshape(N_DEV, n_packed), (0, me * ada_cols), (N_DEV, ada_cols))
    g_ada = _ada_weight_grad(sc_all.reshape(N_DEV, d_model).T, dmod_cols)
    d_, m_, v_ = _adamw(w_ada[0], g_ada, m_w_ada[0], v_w_ada[0], "adamw_w_ada")
    out_g["w_ada"], out_d["w_ada"], out_m["w_ada"], out_v["w_ada"] = g_ada[None], d_[None], m_[None], v_[None]

    small_out = _adamw_small(small_sum, *(pack([tree[n].reshape(1, -1) for n in SMALL])
                                          for tree in (weights, mom_m, mom_v)), sizes)
    for j, n in enumerate(SMALL):
        shape = weights[n].shape
        out_g[n], out_d[n], out_m[n], out_v[n] = (t.reshape(shape) for t in small_out[4 * j:4 * j + 4])

    return (loss, dx0[None], *[out_g[n] for n in WEIGHTS], *[out_d[n] for n in WEIGHTS],
            *[out_m[n] for n in WEIGHTS], *[out_v[n] for n in WEIGHTS])
```

```python
import numpy as np
import jax
import jax.numpy as jnp
from jax import lax
from jax.experimental import pallas as pl
from jax.experimental.pallas import tpu as pltpu

F32 = jnp.float32
BF16 = jnp.bfloat16
MESH = pl.DeviceIdType.MESH
ANY = pl.BlockSpec(memory_space=pl.ANY)
VMEM_SPEC = pl.BlockSpec(memory_space=pltpu.VMEM)
SMEM_SPEC = pl.BlockSpec(memory_space=pltpu.SMEM)

N_DEV = 8
CHUNK = 64
HEAD_DIM = 64
LANES = 128
H_A, KV_A, H_B = 8, 2, 8
BACK_A, BACK_B = 2, 8
REL_CLIP = 128
N_REL = 2 * REL_CLIP + 1
QA, KVA, QB = H_A * HEAD_DIM, KV_A * HEAD_DIM, H_B * HEAD_DIM
D_IN = QA + 2 * KVA + 3 * QB
N_MOD = 9
EPS = 1e-6
NEG_INF = -1e30
QG = 4
QROWS = QG * CHUNK
TPS_A, TPS_B = 4, 2
SKEW = 1024
ADAM_LR, ADAM_B1, ADAM_B2, ADAM_EPS, ADAM_WD, ADAM_STEP = 0.001, 0.9, 0.999, 1e-08, 0.01, 10
VMEM_LIMIT = 56 * 2 ** 20


def _pick(n, cands):
    for c in cands:
        if n % c == 0:
            return c
    return n


def _pieces(n, width=2 * LANES):
    return [(lo, min(lo + width, n)) for lo in range(0, n, width)]


def _params(sem=None):
    return pltpu.CompilerParams(dimension_semantics=sem, vmem_limit_bytes=VMEM_LIMIT)


def _dot_nt(a, b):
    return lax.dot_general(a, b, (((1,), (1,)), ((), ())), preferred_element_type=F32)


def _dot_tn(a, b):
    return lax.dot_general(a, b, (((0,), (0,)), ((), ())), preferred_element_type=F32)


def _dot(a, b):
    return jnp.dot(a, b, preferred_element_type=F32)


def _sigmoid(a):
    return 0.5 * (jnp.tanh(0.5 * a) + 1.0)


def _mesh_pos():
    return lax.axis_index("x"), lax.axis_index("y"), lax.axis_index("c")


def _peer(x, y, c, r):
    px = 1 - x if r & 4 else x
    py = 1 - y if r & 2 else y
    pc = 1 - c if r & 1 else c
    return px, py, pc


class _Carry:
    def __init__(self, ins, out_shapes, scratch, start, finish):
        self.ins, self.out_shapes, self.scratch = list(ins), list(out_shapes), list(scratch)
        self.start, self.finish = start, finish


def _call(body, *, name, grid, in_specs, out_specs, out_shape, args, scratch=(), sem=None, carry=None):
    single = not isinstance(out_shape, (tuple, list))
    out_specs = (out_specs,) if single else tuple(out_specs)
    out_shape = (out_shape,) if single else tuple(out_shape)
    if carry is None:
        res = pl.pallas_call(body, name=name, grid=grid, in_specs=list(in_specs), out_specs=out_specs,
                             out_shape=out_shape, scratch_shapes=list(scratch), compiler_params=_params(sem))(*args)
        return res[0] if single else res
    n_in, n_out, n_s = len(in_specs), len(out_shape), len(scratch)
    ci, co = len(carry.ins), len(carry.out_shapes)

    def wrapped(*refs):
        ins, cins = refs[:n_in], refs[n_in:n_in + ci]
        outs = refs[n_in + ci:n_in + ci + n_out]
        couts = refs[n_in + ci + n_out:n_in + ci + n_out + co]
        scr = refs[n_in + ci + n_out + co:n_in + ci + n_out + co + n_s]
        cscr = refs[n_in + ci + n_out + co + n_s:]
        first, last = None, None
        for ax, n in enumerate(grid):
            f, l = pl.program_id(ax) == 0, pl.program_id(ax) == n - 1
            first = f if first is None else first & f
            last = l if last is None else last & l
        pl.when(first)(lambda: carry.start(cins, couts, cscr))
        body(*ins, *outs, *scr)
        pl.when(last)(lambda: carry.finish(cins, couts, cscr))

    res = pl.pallas_call(
        wrapped, name=name, grid=grid, in_specs=list(in_specs) + [ANY] * ci, out_specs=out_specs + (ANY,) * co,
        out_shape=out_shape + tuple(carry.out_shapes), scratch_shapes=list(scratch) + carry.scratch,
        compiler_params=_params(("arbitrary",) * len(grid)))(*args, *carry.ins)
    main = res[:n_out]
    return (main[0] if single else main), res[n_out:]


def _gather_carry(shards):
    n_w = len(shards)
    rows = [s.shape[0] for s in shards]

    def plan(ins, outs, scr):
        send_sems, recv_sems, local_sems = scr
        x, y, c = _mesh_pos()
        me, sibling = (x, y, c), (x, y, 1 - c)
        chips = [(1 - x, y), (x, 1 - y), (1 - x, 1 - y)]

        def block(w, dev):
            start = pl.multiple_of((4 * dev[0] + 2 * dev[1] + dev[2]) * rows[w], 16)
            return outs[w].at[pl.ds(start, rows[w]), :]

        def copy(w, k, dev, to, src=None):
            return pltpu.make_async_remote_copy(
                src_ref=block(w, dev) if src is None else src, dst_ref=block(w, dev),
                send_sem=send_sems.at[w, k], recv_sem=recv_sems.at[w, k], device_id=to, device_id_type=MESH)

        mine = [pltpu.make_async_copy(ins[w], block(w, me), local_sems.at[w]) for w in range(n_w)]
        first = []
        for j, chip in enumerate(chips):
            first += [copy(w, 1 + j, me, (*chip, c), src=ins[w]) for w in range(n_w)]
        first += [copy(w, 0, me, sibling, src=ins[w]) for w in range(n_w)]
        return c, me, sibling, chips, copy, mine, first

    def start(ins, outs, scr):
        _, _, _, _, _, mine, first = plan(ins, outs, scr)
        for cp in mine + first:
            cp.start()

    def finish(ins, outs, scr):
        c, me, sibling, chips, copy, mine, first = plan(ins, outs, scr)
        passed = []
        for j, chip in enumerate(chips):
            for w in range(n_w):
                copy(w, 1 + j, (*chip, c), me).wait_recv()
                cp = copy(w, 4 + j, (*chip, c), sibling)
                cp.start()
                passed.append(cp)
        for w in range(n_w):
            copy(w, 0, sibling, me).wait_recv()
        for j, chip in enumerate(chips):
            for w in range(n_w):
                copy(w, 4 + j, (*chip, 1 - c), me).wait_recv()
        for cp in first + passed:
            cp.wait_send()
        for cp in mine:
            cp.wait()

    return _Carry(
        shards, [jax.ShapeDtypeStruct((N_DEV * s.shape[0], s.shape[1]), s.dtype) for s in shards],
        [pltpu.SemaphoreType.DMA((n_w, N_DEV - 1)), pltpu.SemaphoreType.DMA((n_w, N_DEV - 1)),
         pltpu.SemaphoreType.DMA((n_w,))], start, finish)


HBM_SPEC = pl.BlockSpec(memory_space=pltpu.HBM)
SEM_SPEC = pl.BlockSpec(memory_space=pltpu.SEMAPHORE)
N_CHIP = N_DEV // 2


def _scatter_copy(part_ref, land_ref, send_sem, recv_sem, r, rows):
    x, y, c = _mesh_pos()
    px, py, _ = _peer(x, y, c, 2 * r)
    src = part_ref.at[pl.ds(pl.multiple_of((2 * px + py) * rows, 16), rows), :]
    return pltpu.make_async_remote_copy(
        src_ref=src, dst_ref=land_ref.at[r - 1], send_sem=send_sem, recv_sem=recv_sem,
        device_id=(px, py, c), device_id_type=MESH)


def _scatter_order(n_w):
    return [(w, r) for r in (3, 2, 1) for w in range(n_w)]


def _scatter_start(parts, name):
    n_w = len(parts)
    rows = [p.shape[0] // N_CHIP for p in parts]
    order = _scatter_order(n_w)
    lands = [pltpu.with_memory_space_constraint(lax.empty((N_CHIP - 1, r, p.shape[1]), p.dtype), pltpu.HBM)
             for r, p in zip(rows, parts)]

    def body(*refs):
        part_refs, land_refs = refs[:n_w], refs[n_w:2 * n_w]
        sems = refs[2 * n_w:2 * n_w + 2 * len(order)]
        token = refs[-1]
        for j, (w, r) in enumerate(order):
            _scatter_copy(part_refs[w], land_refs[w], sems[2 * j], sems[2 * j + 1], r, rows[w]).start()
        token[...] = jnp.zeros_like(token)

    n_sem = 2 * len(order)
    res = pl.pallas_call(
        body, name=name,
        out_shape=(*[pltpu.SemaphoreType.DMA(())] * n_sem, *[pltpu.HBM(p.shape, p.dtype) for p in parts],
                   *[pltpu.HBM(l.shape, l.dtype) for l in lands], jax.ShapeDtypeStruct((8, LANES), F32)),
        in_specs=[HBM_SPEC] * (2 * n_w), out_specs=(*[SEM_SPEC] * n_sem, *[HBM_SPEC] * (2 * n_w), VMEM_SPEC),
        input_output_aliases={i: n_sem + i for i in range(2 * n_w)},
        compiler_params=pltpu.CompilerParams(has_side_effects=pltpu.SideEffectType.DATAFLOW_SIDE_EFFECTING),
    )(*[pltpu.with_memory_space_constraint(p, pltpu.HBM) for p in parts], *lands)
    return (list(res[:n_sem]), list(res[n_sem:n_sem + n_w]), list(res[n_sem + n_w:n_sem + 2 * n_w]), res[-1])


def _scatter_wait(sems, parts, lands, after, name):
    n_w = len(parts)
    rows = [p.shape[0] // N_CHIP for p in parts]
    order = _scatter_order(n_w)

    def body(*refs):
        part_refs, land_refs = refs[:n_w], refs[n_w:2 * n_w]
        sem_refs = refs[2 * n_w:2 * n_w + 2 * len(order)]
        for j, (w, r) in enumerate(order):
            cp = _scatter_copy(part_refs[w], land_refs[w], sem_refs[2 * j], sem_refs[2 * j + 1], r, rows[w])
            cp.wait_send()
            cp.wait_recv()

    res = pl.pallas_call(
        body, name=name,
        out_shape=(*[pltpu.HBM(p.shape, p.dtype) for p in parts], *[pltpu.HBM(l.shape, l.dtype) for l in lands]),
        in_specs=[HBM_SPEC] * (2 * n_w) + [SEM_SPEC] * len(sems) + [ANY],
        out_specs=tuple([HBM_SPEC] * (2 * n_w)),
        input_output_aliases={i: i for i in range(2 * n_w)},
        compiler_params=pltpu.CompilerParams(has_side_effects=pltpu.SideEffectType.DATAFLOW_SIDE_EFFECTING),
    )(*parts, *lands, *sems, after)
    return list(res[:n_w]), list(res[n_w:])


def _small_copy(v_ref, land_ref, send_sem, recv_sem, r):
    x, y, c = _mesh_pos()
    px, py, pc = _peer(x, y, c, r)
    return pltpu.make_async_remote_copy(
        src_ref=v_ref, dst_ref=land_ref.at[4 * x + 2 * y + c], send_sem=send_sem, recv_sem=recv_sem,
        device_id=(px, py, pc), device_id_type=MESH)


def _small_gather_start(v):
    land = pltpu.with_memory_space_constraint(lax.empty((N_DEV,) + v.shape, v.dtype), pltpu.HBM)

    def body(v_ref, land_ref, *rest):
        sems, token = rest[:2 * (N_DEV - 1)], rest[-1]
        for r in range(1, N_DEV):
            _small_copy(v_ref, land_ref, sems[2 * r - 2], sems[2 * r - 1], r).start()
        token[...] = jnp.zeros_like(token)

    n_sem = 2 * (N_DEV - 1)
    res = pl.pallas_call(
        body, name="small_gather_start",
        out_shape=(*[pltpu.SemaphoreType.DMA(())] * n_sem, pltpu.HBM(v.shape, v.dtype),
                   pltpu.HBM(land.shape, land.dtype), jax.ShapeDtypeStruct((8, LANES), F32)),
        in_specs=[HBM_SPEC, HBM_SPEC], out_specs=(*[SEM_SPEC] * n_sem, HBM_SPEC, HBM_SPEC, VMEM_SPEC),
        input_output_aliases={0: n_sem, 1: n_sem + 1},
        compiler_params=pltpu.CompilerParams(has_side_effects=pltpu.SideEffectType.DATAFLOW_SIDE_EFFECTING),
    )(pltpu.with_memory_space_constraint(v, pltpu.HBM), land)
    return list(res[:n_sem]), res[n_sem], res[n_sem + 1], res[-1]


def _small_gather_wait(sems, v, land, after):
    def body(v_ref, land_ref, *rest):
        for r in range(1, N_DEV):
            cp = _small_copy(v_ref, land_ref, rest[2 * r - 2], rest[2 * r - 1], r)
            cp.wait_send()
            x, y, c = _mesh_pos()
            px, py, pc = _peer(x, y, c, r)
            pltpu.make_async_remote_copy(
                src_ref=v_ref, dst_ref=land_ref.at[4 * px + 2 * py + pc], send_sem=rest[2 * r - 2],
                recv_sem=rest[2 * r - 1], device_id=(px, py, pc), device_id_type=MESH).wait_recv()

    res = pl.pallas_call(
        body, name="small_gather_wait",
        out_shape=(pltpu.HBM(v.shape, v.dtype), pltpu.HBM(land.shape, land.dtype)),
        in_specs=[HBM_SPEC, HBM_SPEC] + [SEM_SPEC] * len(sems) + [ANY], out_specs=(HBM_SPEC, HBM_SPEC),
        input_output_aliases={0: 0, 1: 1},
        compiler_params=pltpu.CompilerParams(has_side_effects=pltpu.SideEffectType.DATAFLOW_SIDE_EFFECTING),
    )(v, land, *sems, after)
    return res[0], res[1]


def _ada_forward(c_row, w_ada, b_cols, carry):
    d = c_row.shape[1]
    wcols = w_ada.shape[1]
    ci, co = len(carry.ins), len(carry.out_shapes)

    def body(*refs):
        c_ref, w_ref, b_ref = refs[:3]
        cins = refs[3:3 + ci]
        sc_ref, mod_ref = refs[3 + ci:5 + ci]
        couts = refs[5 + ci:5 + ci + co]
        rows_ref, send_sems, recv_sems = refs[5 + ci + co:8 + ci + co]
        cscr = refs[8 + ci + co:]
        carry.start(cins, couts, cscr)
        x, y, c = _mesh_pos()
        me = 4 * x + 2 * y + c
        cv = c_ref[...]
        sc_ref[me] = cv * _sigmoid(cv)

        sends = []
        for r in range(1, N_DEV):
            px, py, pc = _peer(x, y, c, r)
            cp = pltpu.make_async_remote_copy(
                src_ref=sc_ref.at[me], dst_ref=sc_ref.at[me], send_sem=send_sems.at[0, r - 1],
                recv_sem=recv_sems.at[0, r - 1], device_id=(px, py, pc), device_id_type=MESH)
            cp.start()
            sends.append(cp)
        for r in range(1, N_DEV):
            px, py, pc = _peer(x, y, c, r)
            pid = 4 * px + 2 * py + pc
            pltpu.make_async_remote_copy(
                src_ref=sc_ref.at[pid], dst_ref=sc_ref.at[pid], send_sem=send_sems.at[0, r - 1],
                recv_sem=recv_sems.at[0, r - 1], device_id=(px, py, pc), device_id_type=MESH).wait_recv()
        for cp in sends:
            cp.wait_send()

        sc_all = jnp.concatenate([sc_ref[j] for j in range(N_DEV)], axis=0)
        rows = _dot(sc_all.astype(BF16), w_ref[...].astype(BF16)) + b_ref[...]
        for j in range(N_DEV):
            rows_ref[j] = rows[j:j + 1, :]
        mod_ref[me] = rows_ref[me]

        sends = []
        for r in range(1, N_DEV):
            px, py, pc = _peer(x, y, c, r)
            pid = 4 * px + 2 * py + pc
            cp = pltpu.make_async_remote_copy(
                src_ref=rows_ref.at[pid], dst_ref=mod_ref.at[me], send_sem=send_sems.at[1, r - 1],
                recv_sem=recv_sems.at[1, r - 1], device_id=(px, py, pc), device_id_type=MESH)
            cp.start()
            sends.append(cp)
        for r in range(1, N_DEV):
            px, py, pc = _peer(x, y, c, r)
            pid = 4 * px + 2 * py + pc
            pltpu.make_async_remote_copy(
                src_ref=rows_ref.at[pid], dst_ref=mod_ref.at[pid], send_sem=send_sems.at[1, r - 1],
                recv_sem=recv_sems.at[1, r - 1], device_id=(px, py, pc), device_id_type=MESH).wait_recv()
        for cp in sends:
            cp.wait_send()
        carry.finish(cins, couts, cscr)

    res = pl.pallas_call(
        body, name="ada_forward",
        out_shape=(jax.ShapeDtypeStruct((N_DEV, 1, d), F32), jax.ShapeDtypeStruct((N_DEV, 1, wcols), F32),
                   *carry.out_shapes),
        in_specs=[VMEM_SPEC, VMEM_SPEC, VMEM_SPEC] + [ANY] * ci, out_specs=(VMEM_SPEC, VMEM_SPEC) + (ANY,) * co,
        scratch_shapes=[pltpu.VMEM((N_DEV, 1, wcols), F32), pltpu.SemaphoreType.DMA((2, N_DEV - 1)),
                        pltpu.SemaphoreType.DMA((2, N_DEV - 1))] + carry.scratch,
        compiler_params=_params(),
    )(c_row, w_ada, b_cols, *carry.ins)
    return res[:2], res[2:]


def _mm_nt(a, b, name, out_dtype, bias=None, carry=None):
    m, k = a.shape
    n = b.shape[0]
    tm = _pick(m, (512, 256, 128))
    tn = _pick(n, (1408, 1152, 1024, 768, 512, 256, 128))

    def body(*refs):
        acc = _dot_nt(refs[0][...], refs[1][...])
        if bias is not None:
            acc = acc + refs[2][...]
        refs[-1][...] = acc.astype(out_dtype)

    in_specs = [pl.BlockSpec((tm, k), lambda j, i: (i, 0)), pl.BlockSpec((tn, k), lambda j, i: (j, 0))]
    args = [a, b]
    if bias is not None:
        in_specs.append(pl.BlockSpec((1, tn), lambda j, i: (0, j)))
        args.append(bias)
    return _call(body, name=name, grid=(n // tn, m // tm), in_specs=in_specs,
                 out_specs=pl.BlockSpec((tm, tn), lambda j, i: (i, j)),
                 out_shape=jax.ShapeDtypeStruct((m, n), out_dtype), args=args,
                 sem=("parallel", "parallel"), carry=carry)


class _Tail:
    def __init__(self, rows, vecs, outs, fn):
        self.rows, self.vecs, self.outs, self.fn = list(rows), list(vecs), list(outs), fn


def _mm_nn(pairs, name, out_dtype, bias=None, carry=None, tail=None):
    m, k = pairs[0][0].shape
    n = pairs[0][1].shape[1]
    n_p = len(pairs)
    tm = _pick(m, (512, 256, 128))
    tk = k if n_p == 1 else _pick(k, (1408, 1152, 1024, 768, 512, 256, 128))
    nk = k // tk
    n_b = 0 if bias is None else 1
    n_r, n_v = (len(tail.rows), len(tail.vecs)) if tail else (0, 0)
    n_in = 2 * n_p + n_b + n_r + n_v
    n_main = 0 if out_dtype is None else 1

    def finish(acc, refs, first_tile):
        if bias is not None:
            acc = acc + refs[2 * n_p][...]
        outs = refs[n_in:-1]
        if n_main:
            outs[0][...] = acc.astype(out_dtype)
        if tail is None:
            return
        rows = [r[...] for r in refs[2 * n_p + n_b:2 * n_p + n_b + n_r]]
        vecs = [v[...] for v in refs[2 * n_p + n_b + n_r:n_in]]
        vals = tail.fn(acc, rows, vecs)
        for ref, val, (dtype, kind) in zip(outs[n_main:], vals, tail.outs):
            if kind == "row":
                ref[...] = val.astype(dtype)
            else:
                @pl.when(first_tile)
                def _(ref=ref):
                    ref[...] = jnp.zeros_like(ref)

                ref[...] += val

    def body(*refs):
        acc_ref = refs[-1]
        kk, i = pl.program_id(0), pl.program_id(1)
        part = _dot(refs[0][...], refs[1][...])
        for p in range(1, n_p):
            part = part + _dot(refs[2 * p][...], refs[2 * p + 1][...])
        if nk == 1:
            finish(part, refs, i == 0)
            return
        rows = pl.ds(pl.multiple_of(i * tm, tm), tm)

        @pl.when(kk == 0)
        def _():
            acc_ref[rows, :] = part

        if nk > 2:
            @pl.when((kk > 0) & (kk < nk - 1))
            def _():
                acc_ref[rows, :] += part

        @pl.when(kk == nk - 1)
        def _():
            finish(acc_ref[rows, :] + part, refs, i == 0)

    def last_only(kk, i):
        return (jnp.where(kk == nk - 1, i, 0), 0)

    row_spec = pl.BlockSpec((tm, n), last_only)
    vec_spec = pl.BlockSpec((1, n), lambda kk, i: (0, 0))
    in_specs, args = [], []
    for a, b in pairs:
        in_specs += [pl.BlockSpec((tm, tk), lambda kk, i: (i, kk)), pl.BlockSpec((tk, n), lambda kk, i: (kk, 0))]
        args += [a, b]
    if bias is not None:
        in_specs.append(vec_spec)
        args.append(bias)
    out_specs = [row_spec] * n_main
    out_shape = [jax.ShapeDtypeStruct((m, n), out_dtype)] if n_main else []
    if tail:
        in_specs += [row_spec] * n_r + [vec_spec] * n_v
        args += tail.rows + tail.vecs
        for dtype, kind in tail.outs:
            if kind == "row":
                out_specs.append(row_spec)
                out_shape.append(jax.ShapeDtypeStruct((m, n), dtype))
            else:
                width = n if kind == "sum" else 1
                out_specs.append(pl.BlockSpec((1, width), lambda kk, i: (0, 0)))
                out_shape.append(jax.ShapeDtypeStruct((1, width), dtype))
    if tail is None:
        out_specs, out_shape = out_specs[0], out_shape[0]
    return _call(body, name=name, grid=(nk, m // tm), in_specs=in_specs, out_specs=out_specs,
                 out_shape=out_shape, args=args,
                 scratch=[pltpu.VMEM((m, n) if nk > 1 else (8, LANES), F32)],
                 sem=("arbitrary", "arbitrary"), carry=carry)


def _rms(v):
    return lax.rsqrt(jnp.mean(v * v, axis=-1, keepdims=True) + EPS)


def _col(v):
    return jnp.sum(v, axis=0, keepdims=True)


def _tail_post_pre(x, g_post, gate, weight, g_pre, scale, shift):
    def fn(y, rows, vecs):
        (xv,), (gp, gt, g, sc, sh) = rows, vecs
        xo = xv + (weight * gt) * ((y * _rms(y)) * gp)
        return xo, ((xo * _rms(xo)) * g) * (1.0 + sc) + sh

    return _Tail([x], [g_post, gate, g_pre, scale, shift], [(F32, "row"), (BF16, "row")], fn)


def _tail_post_loss(x, target, g, gate, weight):
    def fn(y, rows, vecs):
        (xv, tv), (gv, gt) = rows, vecs
        r = _rms(y)
        yn = y * r
        err = (xv + (weight * gt) * (yn * gv)) - tv
        do = err * (1.0 / y.shape[1])
        dyn = do * ((weight * gt) * gv)
        dy = r * (dyn - yn * jnp.mean(dyn * yn, axis=-1, keepdims=True))
        return do, dy, 0.5 * _col(jnp.mean(err * err, axis=-1, keepdims=True)), _col(do * yn)

    return _Tail([x, target], [g, gate], [(F32, "row"), (BF16, "row"), (F32, "one"), (F32, "sum")], fn)


def _tail_pre_bwd(x, dres, g_pre, scale):
    def fn(dh, rows, vecs):
        (xv, dr), (g, sc) = rows, vecs
        r = _rms(xv)
        n = xv * r
        dn = dh * (g * (1.0 + sc))
        return dr + r * (dn - n * jnp.mean(dn * n, axis=-1, keepdims=True)), _col(dh * n), _col(dh)

    return _Tail([x, dres], [g_pre, scale], [(F32, "row"), (F32, "sum"), (F32, "sum")], fn)


def _tail_pre_post_bwd(x, dres, y, g_pre, scale, g_post, gate, weight):
    def fn(dh, rows, vecs):
        (xv, dr, yv), (g, sc, gp, gt) = rows, vecs
        r = _rms(xv)
        n = xv * r
        dn = dh * (g * (1.0 + sc))
        dx = dr + r * (dn - n * jnp.mean(dn * n, axis=-1, keepdims=True))
        ry = _rms(yv)
        yn = yv * ry
        dyn = dx * ((weight * gt) * gp)
        dy = ry * (dyn - yn * jnp.mean(dyn * yn, axis=-1, keepdims=True))
        return dx, dy, _col(dh * n), _col(dh), _col(dx * yn), _col(dy)

    return _Tail([x, dres, y], [g_pre, scale, g_post, gate],
                 [(F32, "row"), (BF16, "row")] + [(F32, "sum")] * 4, fn)


def _mm_tn_pair(a, b, name, col_sums=False):
    k, m = a.shape
    n = b.shape[1]
    rows = m // N_DEV
    n_chip = N_DEV // 2
    tm = 4 * rows
    tk = _pick(k, (1024, 512, 256, 128))
    nk = k // tk

    def body(a_ref, b_ref, p_ref, own_ref, *rest):
        acc_ref, keep_ref, send_ref, land_ref, send_sems, recv_sems = rest[-6:]
        i, kk = pl.program_id(0), pl.program_id(1)
        x, y, c = _mesh_pos()
        if col_sums:
            cs_ref = rest[0]
            part = jnp.sum(a_ref[...].astype(F32), axis=0, keepdims=True)

            @pl.when(kk == 0)
            def _():
                cs_ref[...] = part

            @pl.when(kk > 0)
            def _():
                cs_ref[...] += part

        def push(chip):
            return pltpu.make_async_remote_copy(
                src_ref=send_ref.at[chip], dst_ref=land_ref.at[chip], send_sem=send_sems.at[chip],
                recv_sem=recv_sems.at[chip], device_id=(x, y, 1 - c), device_id_type=MESH)

        if nk == 1:
            acc = _dot_tn(a_ref[...], b_ref[...])
        else:
            @pl.when(kk == 0)
            def _():
                acc_ref[...] = jnp.zeros_like(acc_ref)

            acc_ref[...] += _dot_tn(a_ref[...], b_ref[...])
            acc = acc_ref

        for t in range(2):
            @pl.when((kk == nk - 1) & (i == t))
            def _(t=t):
                for ob in range(4):
                    chip, core = 2 * t + ob // 2, ob % 2
                    blk = acc[ob * rows:(ob + 1) * rows, :]

                    @pl.when(c == core)
                    def _(chip=chip, blk=blk):
                        keep_ref[chip] = blk

                    @pl.when(c != core)
                    def _(chip=chip, blk=blk):
                        send_ref[chip] = blk.astype(BF16)
                        push(chip).start()

        @pl.when((kk == nk - 1) & (i == 1))
        def _():
            for chip in range(n_chip):
                push(chip).wait_recv()
                val = (keep_ref[chip] + land_ref[chip].astype(F32)).astype(BF16)
                p_ref[chip * rows:(chip + 1) * rows, :] = val

                @pl.when(2 * x + y == chip)
                def _(val=val):
                    own_ref[...] = val

            for chip in range(n_chip):
                push(chip).wait_send()

    out_specs = [pl.BlockSpec((n_chip * rows, n), lambda i, kk: (0, 0)), pl.BlockSpec((rows, n), lambda i, kk: (0, 0))]
    out_shape = [jax.ShapeDtypeStruct((n_chip * rows, n), BF16), jax.ShapeDtypeStruct((rows, n), BF16)]
    if col_sums:
        out_specs.append(pl.BlockSpec((1, tm), lambda i, kk: (0, i)))
        out_shape.append(jax.ShapeDtypeStruct((1, m), F32))
    return _call(body, name=name, grid=(2, nk),
                 in_specs=[pl.BlockSpec((tk, tm), lambda i, kk: (kk, i)), pl.BlockSpec((tk, n), lambda i, kk: (kk, 0))],
                 out_specs=out_specs, out_shape=out_shape, args=[a, b],
                 scratch=[pltpu.VMEM((tm, n) if nk > 1 else (8, LANES), F32), pltpu.VMEM((n_chip, rows, n), F32),
                          pltpu.VMEM((n_chip, rows, n), BF16), pltpu.VMEM((n_chip, rows, n), BF16),
                          pltpu.SemaphoreType.DMA((n_chip,)), pltpu.SemaphoreType.DMA((n_chip,))],
                 sem=("arbitrary", "arbitrary"))


def _ffn_up(h, wg_t, wu_t, name, carry=None):
    s, d = h.shape
    f = wg_t.shape[0]
    tm = _pick(s, (512, 256, 128))
    tf = _pick(f, (1408, 1024, 512, 256, 128))

    def body(h_ref, wg_ref, wu_ref, a_ref, b_ref, u_ref):
        hh = h_ref[...]
        for lo, hi in _pieces(tf):
            a = _dot_nt(hh, wg_ref[lo:hi, :])
            b = _dot_nt(hh, wu_ref[lo:hi, :])
            a_ref[:, lo:hi] = a.astype(BF16)
            b_ref[:, lo:hi] = b.astype(BF16)
            u_ref[:, lo:hi] = ((a * _sigmoid(a)) * b).astype(BF16)

    w_spec = pl.BlockSpec((tf, d), lambda j, i: (j, 0))
    o_spec = pl.BlockSpec((tm, tf), lambda j, i: (i, j))
    o_shape = jax.ShapeDtypeStruct((s, f), BF16)
    return _call(body, name=name, grid=(f // tf, s // tm),
                 in_specs=[pl.BlockSpec((tm, d), lambda j, i: (i, 0)), w_spec, w_spec],
                 out_specs=(o_spec, o_spec, o_spec), out_shape=(o_shape, o_shape, o_shape),
                 args=[h, wg_t, wu_t], sem=("parallel", "parallel"), carry=carry)


def _ffn_down_bwd(dy, wd, a, b, name, carry=None):
    s, d = dy.shape
    f = wd.shape[0]
    tm = _pick(s, (512, 256, 128))
    tf = _pick(f, (1408, 1024, 512, 256, 128))

    def body(dy_ref, wd_ref, a_ref, b_ref, da_ref, db_ref):
        dyv = dy_ref[...]
        for lo, hi in _pieces(tf):
            du = _dot_nt(dyv, wd_ref[lo:hi, :])
            a = a_ref[:, lo:hi].astype(F32)
            b = b_ref[:, lo:hi].astype(F32)
            sig = _sigmoid(a)
            da_ref[:, lo:hi] = (du * b * (sig * (1.0 + a * (1.0 - sig)))).astype(BF16)
            db_ref[:, lo:hi] = (du * (a * sig)).astype(BF16)

    t_spec = pl.BlockSpec((tm, tf), lambda j, i: (i, j))
    o_shape = jax.ShapeDtypeStruct((s, f), BF16)
    return _call(body, name=name, grid=(f // tf, s // tm),
                 in_specs=[pl.BlockSpec((tm, d), lambda j, i: (i, 0)), pl.BlockSpec((tf, d), lambda j, i: (j, 0)),
                           t_spec, t_spec],
                 out_specs=(t_spec, t_spec), out_shape=(o_shape, o_shape), args=[dy, wd, a, b],
                 sem=("parallel", "parallel"), carry=carry)


def _row_tile(s):
    return _pick(s, (256, 128, 64))


def _vec_spec(d):
    return pl.BlockSpec((1, d), lambda i: (0, 0))


def _pre_norm(x, g, scale, shift, name):
    s, d = x.shape
    ts = _row_tile(s)

    def body(x_ref, g_ref, sc_ref, sh_ref, h_ref):
        xv = x_ref[...]
        r = lax.rsqrt(jnp.mean(xv * xv, axis=-1, keepdims=True) + EPS)
        h_ref[...] = (((xv * r) * g_ref[...]) * (1.0 + sc_ref[...]) + sh_ref[...]).astype(BF16)

    row = pl.BlockSpec((ts, d), lambda i: (i, 0))
    return _call(body, name=name, grid=(s // ts,), in_specs=[row, _vec_spec(d), _vec_spec(d), _vec_spec(d)],
                 out_specs=row, out_shape=jax.ShapeDtypeStruct((s, d), BF16), args=[x, g, scale, shift],
                 sem=("parallel",))


def _group_norm_cat(oa, ob, ga, gb):
    s = oa.shape[0]
    ts = _row_tile(s)

    def body(oa_ref, ob_ref, ga_ref, gb_ref, y_ref):
        for o_ref, g_ref, lo, w in ((oa_ref, ga_ref, 0, QA), (ob_ref, gb_ref, QA, QB)):
            ov = o_ref[...]
            r = lax.rsqrt(jnp.mean(ov * ov, axis=-1, keepdims=True) + EPS)
            y_ref[:, lo:lo + w] = ((ov * r) * g_ref[...]).astype(BF16)

    return _call(body, name="group_norm_cat", grid=(s // ts,),
                 in_specs=[pl.BlockSpec((ts, QA), lambda i: (i, 0)), pl.BlockSpec((ts, QB), lambda i: (i, 0)),
                           _vec_spec(QA), _vec_spec(QB)],
                 out_specs=pl.BlockSpec((ts, QA + QB), lambda i: (i, 0)),
                 out_shape=jax.ShapeDtypeStruct((s, QA + QB), BF16), args=[oa, ob, ga, gb], sem=("parallel",))


def _group_norm_bwd(dy, oa, ob, ga, gb):
    s = oa.shape[0]
    ts = _row_tile(s)

    def body(dy_ref, oa_ref, ob_ref, ga_ref, gb_ref, doa_ref, dob_ref, dga_ref, dgb_ref):
        @pl.when(pl.program_id(0) == 0)
        def _():
            dga_ref[...] = jnp.zeros_like(dga_ref)
            dgb_ref[...] = jnp.zeros_like(dgb_ref)

        for o_ref, g_ref, do_ref, dg_ref, lo, w in ((oa_ref, ga_ref, doa_ref, dga_ref, 0, QA),
                                                    (ob_ref, gb_ref, dob_ref, dgb_ref, QA, QB)):
            ov = o_ref[...]
            dyv = dy_ref[:, lo:lo + w]
            r = lax.rsqrt(jnp.mean(ov * ov, axis=-1, keepdims=True) + EPS)
            n = ov * r
            dn = dyv * g_ref[...]
            do_ref[...] = r * (dn - n * jnp.mean(dn * n, axis=-1, keepdims=True))
            dg_ref[...] += jnp.sum(dyv * n, axis=0, keepdims=True)

    ra = pl.BlockSpec((ts, QA), lambda i: (i, 0))
    rb = pl.BlockSpec((ts, QB), lambda i: (i, 0))
    return _call(body, name="group_norm_bwd", grid=(s // ts,),
                 in_specs=[pl.BlockSpec((ts, QA + QB), lambda i: (i, 0)), ra, rb, _vec_spec(QA), _vec_spec(QB)],
                 out_specs=(ra, rb, _vec_spec(QA), _vec_spec(QB)),
                 out_shape=(jax.ShapeDtypeStruct((s, QA), F32), jax.ShapeDtypeStruct((s, QB), F32),
                            jax.ShapeDtypeStruct((1, QA), F32), jax.ShapeDtypeStruct((1, QB), F32)),
                 args=[dy, oa, ob, ga, gb], sem=("arbitrary",))


def _n_variants(n_back):
    return -(-n_back // QG) + 1


def _alibi_bias():
    i = np.arange(QROWS)[:, None]
    j = np.arange((QG + BACK_A) * CHUNK)[None, :]
    dist = np.abs(BACK_A * CHUNK + i - j).astype(np.float32)
    dc = j // CHUNK - i // CHUNK
    valid = (dc >= 0) & (dc <= BACK_A)
    slopes = np.array([2.0 ** (-8.0 * (h + 1) / H_A) for h in range(H_A)], dtype=np.float32)
    bias = -slopes[:, None, None] * dist[None]
    out = [np.where((valid & (j >= (BACK_A - QG * v) * CHUNK))[None], bias, np.float32(NEG_INF))
           for v in range(_n_variants(BACK_A))]
    return jnp.asarray(np.stack(out).astype(np.float32))


def _rel_index_matrix():
    cc = np.arange(SKEW)
    dist = np.where(cc < SKEW - QROWS, BACK_B * CHUNK - cc, BACK_B * CHUNK + SKEW - cc)
    idx = np.clip(dist, -REL_CLIP, REL_CLIP) + REL_CLIP
    m = np.zeros((SKEW, N_REL), np.float32)
    m[cc, idx] = 1.0
    return jnp.asarray(m)


def _toeplitz_bias(vec, carry=None):
    lk = (QG + BACK_B) * CHUNK
    nv = _n_variants(BACK_B)

    def body(v_ref, o_ref):
        xv = jnp.broadcast_to(v_ref[0], (QROWS, SKEW))
        row = lax.broadcasted_iota(jnp.int32, (QROWS, SKEW), 0)
        for bit in range(QROWS.bit_length() - 1):
            xv = jnp.where((row >> bit) & 1 == 1, pltpu.roll(xv, 1 << bit, 1), xv)
        ri = lax.broadcasted_iota(jnp.int32, (QROWS, lk), 0) // CHUNK
        col = lax.broadcasted_iota(jnp.int32, (QROWS, lk), 1)
        ci = col // CHUNK
        valid = (ci - ri >= 0) & (ci - ri <= BACK_B)
        for v in range(nv):
            o_ref[v, 0] = jnp.where(valid & (col >= (BACK_B - QG * v) * CHUNK), xv[:, :lk], NEG_INF)

    return _call(body, name="toeplitz_bias", grid=(H_B,),
                 in_specs=[pl.BlockSpec((1, 1, SKEW), lambda h: (h, 0, 0))],
                 out_specs=pl.BlockSpec((nv, 1, QROWS, lk), lambda h: (0, h, 0, 0)),
                 out_shape=jax.ShapeDtypeStruct((nv, H_B, QROWS, lk), F32), args=[vec], sem=("parallel",),
                 carry=carry)


def _diagonal_sums(dbias):
    lk = dbias.shape[2]

    def body(d_ref, o_ref):
        xp = jnp.concatenate([d_ref[0], jnp.zeros((QROWS, SKEW - lk), F32)], axis=1)
        xv = xp[0:CHUNK]
        for q in range(1, QG):
            xv = xv + pltpu.roll(xp[q * CHUNK:(q + 1) * CHUNK], SKEW - q * CHUNK, 1)
        row = lax.broadcasted_iota(jnp.int32, (CHUNK, SKEW), 0)
        for bit in range(CHUNK.bit_length() - 1):
            xv = jnp.where((row >> bit) & 1 == 1, pltpu.roll(xv, SKEW - (1 << bit), 1), xv)
        o_ref[0] = jnp.sum(xv, axis=0, keepdims=True)

    return _call(body, name="diagonal_sums", grid=(H_B,),
                 in_specs=[pl.BlockSpec((1, QROWS, lk), lambda h: (h, 0, 0))],
                 out_specs=pl.BlockSpec((1, 1, SKEW), lambda h: (h, 0, 0)),
                 out_shape=jax.ShapeDtypeStruct((H_B, 1, SKEW), F32), args=[dbias], sem=("parallel",))


def _attn_common(s, n_back, gqa, q_col, k_col, v_col, TPS):
    lk = (QG + n_back) * CHUNK
    pad = n_back * CHUNK
    wide = TPS * LANES
    q_spec = pl.BlockSpec((QROWS, wide), lambda t, g: (g, q_col // TPS + t))
    if gqa:
        k_spec = pl.BlockSpec((s, LANES), lambda t, g: (0, k_col))
        v_spec = pl.BlockSpec((s, LANES), lambda t, g: (0, v_col))
    else:
        k_spec = pl.BlockSpec((s, wide), lambda t, g: (0, k_col // TPS + t))
        v_spec = pl.BlockSpec((s, wide), lambda t, g: (0, v_col // TPS + t))
    last_variant = _n_variants(n_back) - 1
    bias_spec = pl.BlockSpec((None, 2 * TPS, QROWS, lk), lambda t, g: (jnp.minimum(g, last_variant), t, 0, 0))
    tile_spec = pl.BlockSpec((QROWS, wide), lambda t, g: (g, t))
    return lk, pad, q_spec, k_spec, v_spec, bias_spec, tile_spec


def _attention_fwd(proj, bias, sinks, *, n_back, gqa, q_col, k_col, v_col, TPS, name, carry=None):
    s = proj.shape[0]
    lk, pad, q_spec, k_spec, v_spec, bias_spec, tile_spec = _attn_common(s, n_back, gqa, q_col, k_col, v_col, TPS)
    n_t, n_g = 512 // (TPS * LANES), s // QROWS
    kv_wide = LANES if gqa else TPS * LANES

    def body(*refs):
        if gqa:
            q_ref, k_ref, v_ref, bias_ref, sink_ref, o_ref, l_ref, kpad, vpad = refs
        else:
            q_ref, k_ref, v_ref, bias_ref, o_ref, l_ref, kpad, vpad = refs
        t, g = pl.program_id(0), pl.program_id(1)

        @pl.when(g == 0)
        def _():
            kpad[0:pad, :] = jnp.zeros((pad, kv_wide), BF16)
            vpad[0:pad, :] = jnp.zeros((pad, kv_wide), BF16)
            kpad[pad:, :] = k_ref[...]
            vpad[pad:, :] = v_ref[...]

        start = pl.multiple_of(g * QROWS, QROWS)
        half = lax.broadcasted_iota(jnp.int32, (QROWS, LANES), 1) // HEAD_DIM
        for tt in range(TPS):
            lanes = slice(tt * LANES, (tt + 1) * LANES)
            kv_lanes = slice(0, LANES) if gqa else lanes
            kb = kpad[pl.ds(start, lk), kv_lanes]
            vb = vpad[pl.ds(start, lk), kv_lanes]
            q = q_ref[:, lanes] * (HEAD_DIM ** -0.5)
            if gqa:
                hk = (TPS * t + tt) // 2
                q_rolled = pltpu.roll(q.astype(F32), HEAD_DIM, 1).astype(BF16)
            outs, lses = [], []
            for e in range(2):
                if gqa:
                    kv_half = hk
                    src = jnp.where(hk == e, q, q_rolled)
                else:
                    kv_half = e
                    src = q
                qm = jnp.where(half == kv_half, src, jnp.zeros_like(src))
                sc = _dot_nt(qm, kb) + bias_ref[2 * tt + e]
                m = jnp.max(sc, axis=-1, keepdims=True)
                if gqa:
                    sk = sink_ref[2 * (TPS * t + tt) + e]
                    m = jnp.maximum(m, sk)
                p = jnp.exp(sc - m)
                l = jnp.sum(p, axis=-1, keepdims=True)
                if gqa:
                    l = l + jnp.exp(sk - m)
                pn = p / l
                outs.append(_dot(pn.astype(BF16), vb))
                lses.append(m + jnp.log(l))
            if gqa:
                same = jnp.where(hk == 0, outs[0], outs[1])
                other = jnp.where(hk == 0, outs[1], outs[0])
                o_ref[:, lanes] = jnp.where(half == hk, same, pltpu.roll(other, HEAD_DIM, 1))
            else:
                o_ref[:, lanes] = jnp.where(half == 0, outs[0], outs[1])
            l_ref[:, lanes] = jnp.where(half == 0, lses[0], lses[1])

    in_specs = [q_spec, k_spec, v_spec, bias_spec] + ([SMEM_SPEC] if gqa else [])
    args = [proj, proj, proj, bias] + ([sinks] if gqa else [])
    o_shape = jax.ShapeDtypeStruct((s, 512), F32)
    return _call(body, name=name, grid=(n_t, n_g), in_specs=in_specs, out_specs=(tile_spec, tile_spec),
                 out_shape=(o_shape, o_shape), args=args,
                 scratch=[pltpu.VMEM((s + pad, kv_wide), BF16), pltpu.VMEM((s + pad, kv_wide), BF16)],
                 sem=("arbitrary", "arbitrary"), carry=carry)


def _attention_bwd(proj, bias, sinks, do, lse, *, n_back, gqa, q_col, k_col, v_col, TPS, name, carry=None):
    s = proj.shape[0]
    lk, pad, q_spec, k_spec, v_spec, bias_spec, tile_spec = _attn_common(s, n_back, gqa, q_col, k_col, v_col, TPS)
    n_t, n_g = 512 // (TPS * LANES), s // QROWS
    kv_wide = LANES if gqa else TPS * LANES

    def body(*refs):
        if gqa:
            (q_ref, k_ref, v_ref, bias_ref, sink_ref, do_ref, l_ref,
             dq_ref, dk_ref, dv_ref, dsink_ref, kpad, vpad, dkpad, dvpad) = refs
        else:
            (q_ref, k_ref, v_ref, bias_ref, do_ref, l_ref,
             dq_ref, dk_ref, dv_ref, dbias_ref, kpad, vpad, dkpad, dvpad) = refs
        t, g = pl.program_id(0), pl.program_id(1)

        @pl.when(g == 0)
        def _():
            kpad[0:pad, :] = jnp.zeros((pad, kv_wide), BF16)
            vpad[0:pad, :] = jnp.zeros((pad, kv_wide), BF16)
            kpad[pad:, :] = k_ref[...]
            vpad[pad:, :] = v_ref[...]
            if gqa:
                dsink_ref[...] = jnp.zeros_like(dsink_ref)
            else:
                dbias_ref[...] = jnp.zeros_like(dbias_ref)

        @pl.when((g == 0) & (t == 0) if gqa else g == 0)
        def _():
            dkpad[...] = jnp.zeros_like(dkpad)
            dvpad[...] = jnp.zeros_like(dvpad)

        start = pl.multiple_of(g * QROWS, QROWS)
        half = lax.broadcasted_iota(jnp.int32, (QROWS, LANES), 1) // HEAD_DIM
        for tt in range(TPS):
            lanes = slice(tt * LANES, (tt + 1) * LANES)
            kv_lanes = slice(0, LANES) if gqa else lanes
            kb = kpad[pl.ds(start, lk), kv_lanes]
            vb = vpad[pl.ds(start, lk), kv_lanes]
            q = q_ref[:, lanes]
            dov = do_ref[:, lanes]
            lv = l_ref[:, lanes]
            if gqa:
                hk = (TPS * t + tt) // 2
                q_rolled = pltpu.roll(q.astype(F32), HEAD_DIM, 1).astype(BF16)
                do_rolled = pltpu.roll(dov, HEAD_DIM, 1)
            dqs = []
            dk_acc = jnp.zeros((lk, LANES), F32)
            dv_acc = jnp.zeros((lk, LANES), F32)
            for e in range(2):
                if gqa:
                    kv_half = hk
                    src = jnp.where(hk == e, q, q_rolled)
                    do_src = jnp.where(hk == e, dov, do_rolled)
                else:
                    kv_half = e
                    src = q
                    do_src = dov
                qm = jnp.where(half == kv_half, src, jnp.zeros_like(src))
                dom = jnp.where(half == kv_half, do_src, 0.0).astype(BF16)
                lcol = jnp.max(jnp.where(half == e, lv, -jnp.inf), axis=-1, keepdims=True)
                sc = _dot_nt(qm * (HEAD_DIM ** -0.5), kb) + bias_ref[2 * tt + e]
                pn = jnp.exp(sc - lcol)
                dp = _dot_nt(dom, vb)
                delta = jnp.sum(pn * dp, axis=-1, keepdims=True)
                ds = pn * (dp - delta)
                if gqa:
                    p_sink = jnp.exp(sink_ref[2 * (TPS * t + tt) + e] - lcol)
                    dsk = -jnp.sum(p_sink * delta, axis=0, keepdims=True)
                    row = 2 * tt + e
                    dsink_ref[0, row:row + 1, :] += jnp.broadcast_to(dsk, (1, LANES))
                else:
                    dbias_ref[2 * tt + e] += ds
                dsb = (ds * (HEAD_DIM ** -0.5)).astype(BF16)
                dqs.append(_dot(dsb, kb))
                dk_acc = dk_acc + _dot_tn(dsb, qm)
                dv_acc = dv_acc + _dot_tn(pn.astype(BF16), dom)
            dkpad[pl.ds(start, lk), kv_lanes] += dk_acc
            dvpad[pl.ds(start, lk), kv_lanes] += dv_acc
            if gqa:
                same = jnp.where(hk == 0, dqs[0], dqs[1])
                other = jnp.where(hk == 0, dqs[1], dqs[0])
                dq_ref[:, lanes] = jnp.where(half == hk, same, pltpu.roll(other, HEAD_DIM, 1)).astype(BF16)
            else:
                dq_ref[:, lanes] = jnp.where(half == 0, dqs[0], dqs[1]).astype(BF16)

        @pl.when((g == n_g - 1) & (t == n_t - 1) if gqa else g == n_g - 1)
        def _():
            dk_ref[...] = dkpad[pad:, :].astype(BF16)
            dv_ref[...] = dvpad[pad:, :].astype(BF16)

    in_specs = [q_spec, k_spec, v_spec, bias_spec] + ([SMEM_SPEC] if gqa else []) + [tile_spec, tile_spec]
    args = [proj, proj, proj, bias] + ([sinks] if gqa else []) + [do, lse]
    if gqa:
        kv_out = pl.BlockSpec((s, LANES), lambda t, g: (0, 0))
        kv_shape = jax.ShapeDtypeStruct((s, LANES), BF16)
        extra_spec = pl.BlockSpec((1, 8, LANES), lambda t, g: (t, 0, 0))
        extra_shape = jax.ShapeDtypeStruct((n_t, 8, LANES), F32)
    else:
        kv_out = pl.BlockSpec((s, kv_wide), lambda t, g: (0, t))
        kv_shape = jax.ShapeDtypeStruct((s, 512), BF16)
        extra_spec = pl.BlockSpec((2 * TPS, QROWS, lk), lambda t, g: (t, 0, 0))
        extra_shape = jax.ShapeDtypeStruct(bias.shape[1:], F32)
    return _call(body, name=name, grid=(n_t, n_g), in_specs=in_specs,
                 out_specs=(tile_spec, kv_out, kv_out, extra_spec),
                 out_shape=(jax.ShapeDtypeStruct((s, 512), BF16), kv_shape, kv_shape, extra_shape), args=args,
                 scratch=[pltpu.VMEM((s + pad, kv_wide), BF16), pltpu.VMEM((s + pad, kv_wide), BF16),
                          pltpu.VMEM((s + pad, kv_wide), F32), pltpu.VMEM((s + pad, kv_wide), F32)],
                 sem=("arbitrary", "arbitrary"), carry=carry)


def _sum_rows8(g):
    n = g.shape[2]

    def body(g_ref, o_ref):
        acc = g_ref[0]
        for j in range(1, N_DEV):
            acc = acc + g_ref[j]
        o_ref[...] = acc

    return pl.pallas_call(
        body, name="sum_small_grads", in_specs=[VMEM_SPEC], out_specs=VMEM_SPEC,
        out_shape=jax.ShapeDtypeStruct((1, n), F32), compiler_params=_params(),
    )(g)


def _ada_weight_grad(sc_t, dmod_cols):
    d = sc_t.shape[0]
    w = dmod_cols.shape[1]
    td = _pick(d, (256, 128))

    def body(sc_ref, dm_ref, o_ref):
        scv = sc_ref[...]
        dmv = dm_ref[...]
        acc = scv[:, 0:1] * dmv[0:1, :]
        for b in range(1, N_DEV):
            acc = acc + scv[:, b:b + 1] * dmv[b:b + 1, :]
        o_ref[...] = acc

    return _call(body, name="ada_weight_grad", grid=(d // td,),
                 in_specs=[pl.BlockSpec((td, N_DEV), lambda i: (i, 0)), pl.BlockSpec((N_DEV, w), lambda i: (0, 0))],
                 out_specs=pl.BlockSpec((td, w), lambda i: (i, 0)), out_shape=jax.ShapeDtypeStruct((d, w), F32),
                 args=[sc_t, dmod_cols], sem=("parallel",))


def _adamw_update(w, gv, m, v):
    nm = ADAM_B1 * m + (1.0 - ADAM_B1) * gv
    nv = ADAM_B2 * v + (1.0 - ADAM_B2) * (gv * gv)
    m_hat = nm / (1.0 - ADAM_B1 ** ADAM_STEP)
    v_hat = nv / (1.0 - ADAM_B2 ** ADAM_STEP)
    return -ADAM_LR * (m_hat / (jnp.sqrt(v_hat) + ADAM_EPS) + ADAM_WD * w), nm, nv


def _adamw(w, g, m, v, name):
    rows, cols = w.shape
    tr = _pick(rows, (256, 176, 128, 88, 64)) if rows > 256 else rows

    def body(w_ref, g_ref, m_ref, v_ref, d_ref, nm_ref, nv_ref):
        d_ref[...], nm_ref[...], nv_ref[...] = _adamw_update(w_ref[...], g_ref[...], m_ref[...], v_ref[...])

    spec = pl.BlockSpec((tr, cols), lambda i: (i, 0))
    shape = jax.ShapeDtypeStruct((rows, cols), F32)
    return _call(body, name=name, grid=(rows // tr,), in_specs=[spec] * 4, out_specs=(spec, spec, spec),
                 out_shape=(shape, shape, shape), args=[w, g, m, v], sem=("parallel",))


def _adamw_from_slots(w, own, slots, m, v, name):
    n_slots, rows, k = slots.shape

    def body(o_ref, s_ref, w_ref, m_ref, v_ref, g_ref, d_ref, nm_ref, nv_ref):
        gv = o_ref[...].astype(F32)
        for j in range(n_slots):
            gv = gv + s_ref[j].astype(F32)
        g_ref[...] = gv
        d_ref[...], nm_ref[...], nv_ref[...] = _adamw_update(w_ref[...], gv, m_ref[...], v_ref[...])

    tr = rows // 2 if rows % 32 == 0 else rows
    spec = pl.BlockSpec((tr, k), lambda i: (i, 0))
    shape = jax.ShapeDtypeStruct((rows, k), F32)
    return _call(body, name=name, grid=(rows // tr,),
                 in_specs=[spec, pl.BlockSpec((n_slots, tr, k), lambda i: (0, i, 0)), spec, spec, spec],
                 out_specs=(spec, spec, spec, spec), out_shape=(shape, shape, shape, shape),
                 args=[own, slots, w, m, v], sem=("parallel",))


def _adamw_small(g, w, m, v, sizes):
    n = w.shape[1]
    offs, off = [], 0
    for size in sizes:
        offs.append(off)
        off += size + (-size % LANES)

    def body(g_ref, w_ref, m_ref, v_ref, *out_refs):
        gv = g_ref[:, 0:n]
        dv, nm, nv = _adamw_update(w_ref[...], gv, m_ref[...], v_ref[...])
        for j, (o, size) in enumerate(zip(offs, sizes)):
            for k, val in enumerate((gv, dv, nm, nv)):
                out_refs[4 * j + k][...] = val[:, o:o + size]

    shapes = [jax.ShapeDtypeStruct((1, size), F32) for size in sizes for _ in range(4)]
    return pl.pallas_call(
        body, name="adamw_small", in_specs=[VMEM_SPEC] * 4, out_specs=tuple([VMEM_SPEC] * len(shapes)),
        out_shape=tuple(shapes), compiler_params=_params(),
    )(g, w, m, v)


SMALL = ("b_ada", "g_pre_ffn1", "g_post_ffn1", "g_pre_mix", "b_in", "sinks_a", "rel_bias_b", "g_grp_a",
         "g_grp_b", "b_out", "g_post_mix", "g_pre_ffn2", "g_post_ffn2")
WEIGHTS = ("w_ada", "b_ada", "g_pre_ffn1", "w_gate1", "w_up1", "w_down1", "g_post_ffn1", "g_pre_mix", "w_in",
           "b_in", "sinks_a", "rel_bias_b", "g_grp_a", "g_grp_b", "w_out", "b_out", "g_post_mix", "g_pre_ffn2",
           "w_gate2", "w_up2", "w_down2", "g_post_ffn2")


def kernel(x, c, w_ada, b_ada, g_pre_ffn1, w_gate1, w_up1, w_down1, g_post_ffn1, g_pre_mix, w_in, b_in, sinks_a, rel_bias_b, g_grp_a, g_grp_b, w_out, b_out, g_post_mix, g_pre_ffn2, w_gate2, w_up2, w_down2, g_post_ffn2, loss_target, m_w_ada, m_b_ada, m_g_pre_ffn1, m_w_gate1, m_w_up1, m_w_down1, m_g_post_ffn1, m_g_pre_mix, m_w_in, m_b_in, m_sinks_a, m_rel_bias_b, m_g_grp_a, m_g_grp_b, m_w_out, m_b_out, m_g_post_mix, m_g_pre_ffn2, m_w_gate2, m_w_up2, m_w_down2, m_g_post_ffn2, v_w_ada, v_b_ada, v_g_pre_ffn1, v_w_gate1, v_w_up1, v_w_down1, v_g_post_ffn1, v_g_pre_mix, v_w_in, v_b_in, v_sinks_a, v_rel_bias_b, v_g_grp_a, v_g_grp_b, v_w_out, v_b_out, v_g_post_mix, v_g_pre_ffn2, v_w_gate2, v_w_up2, v_w_down2, v_g_post_ffn2):
    given = dict(locals())
    weights = {n: given[n] for n in WEIGHTS}
    mom_m = {n: given["m_" + n] for n in WEIGHTS}
    mom_v = {n: given["v_" + n] for n in WEIGHTS}

    me = 4 * lax.axis_index("x") + 2 * lax.axis_index("y") + lax.axis_index("c")
    xs = x[0]
    tgt = loss_target[0]
    d_model = xs.shape[1]
    ada_cols = w_ada.shape[2]

    sh = {"wg1": w_gate1[0].T, "wu1": w_up1[0].T, "wd1": w_down1[0], "win": w_in[0].T, "wo": w_out[0],
          "wg2": w_gate2[0].T, "wu2": w_up2[0].T, "wd2": w_down2[0]}
    sh = {k: v.astype(BF16) for k, v in sh.items()}

    def gather(*names):
        return _gather_carry([sh[n] for n in names])

    bias_a = _alibi_bias()
    rel_m = _rel_index_matrix()
    rel_vec = jnp.dot(rel_bias_b[0], rel_m.T, precision=lax.Precision.HIGHEST)
    bias_b, (wg1, wu1) = _toeplitz_bias(rel_vec.reshape(H_B, 1, SKEW), carry=gather("wg1", "wu1"))

    b_cols = lax.dynamic_slice(b_ada, (0, me * ada_cols), (1, ada_cols))
    (sc_all, mod_rows), _ = _ada_forward(c, w_ada[0], b_cols, _Carry([], [], [], lambda *a: None, lambda *a: None))
    mod = mod_rows.reshape(N_MOD, d_model)
    shift1, scale1, gate1, shift2, scale2, gate2, shift3, scale3, gate3 = (mod[i:i + 1] for i in range(N_MOD))

    h1 = _pre_norm(xs, g_pre_ffn1, scale1, shift1, "pre_norm_ffn1")
    (a1, b1, u1), (wd1,) = _ffn_up(h1, wg1, wu1, "ffn_up_ffn1", carry=gather("wd1"))
    (y1, x1, h2), (win,) = _mm_nn(
        [(u1, wd1)], "ffn_down_ffn1", F32, carry=gather("win"),
        tail=_tail_post_pre(xs, g_post_ffn1, gate1, 0.5, g_pre_mix, scale2, shift2))

    proj, (wo,) = _mm_nt(h2, win, "in_proj", BF16, bias=b_in, carry=gather("wo"))
    sinks = sinks_a[0]
    cfg_a = dict(n_back=BACK_A, gqa=True, q_col=0, k_col=QA // LANES, v_col=(QA + KVA) // LANES, TPS=TPS_A)
    cfg_b = dict(n_back=BACK_B, gqa=False, q_col=(QA + 2 * KVA) // LANES, k_col=(QA + 2 * KVA + QB) // LANES,
                 v_col=(QA + 2 * KVA + 2 * QB) // LANES, TPS=TPS_B)
    (oa, lse_a), (wg2,) = _attention_fwd(proj, bias_a, sinks, name="attn_a", carry=gather("wg2"), **cfg_a)
    (ob, lse_b), (wu2,) = _attention_fwd(proj, bias_b, None, name="attn_b", carry=gather("wu2"), **cfg_b)
    ycat = _group_norm_cat(oa, ob, g_grp_a, g_grp_b)
    ymix, x2, h3 = _mm_nn([(ycat, wo)], "out_proj", F32, bias=b_out,
                          tail=_tail_post_pre(x1, g_post_mix, gate2, 1.0, g_pre_ffn2, scale3, shift3))

    (a3, b3, u3), (wd2,) = _ffn_up(h3, wg2, wu2, "ffn_up_ffn2", carry=gather("wd2"))

    flights, own = {}, {}

    def grad_pair(key, a_mat, b_mat, name):
        part, own[key] = _mm_tn_pair(a_mat, b_mat, name)
        return part

    def scatter_start(tag, after_vec, **parts):
        names = list(parts)
        sems, p_thru, lands, token = _scatter_start([parts[n] for n in names], "scatter_start_" + tag)
        flights[tag] = (names, sems, p_thru, lands)
        return after_vec + token[0:1, 0:1]

    dx3, dy, loss_part, s1 = _mm_nn([(u3, wd2)], "ffn_down_ffn2", None,
                                    tail=_tail_post_loss(x2, tgt, g_post_ffn2, gate3, 0.5))
    da, db = _ffn_down_bwd(dy, wd2, a3, b3, "ffn_down_bwd_ffn2")
    dwd2 = grad_pair("wd2", u3, dy, "grad_wd_ffn2")
    dwg2 = grad_pair("wg2", da, h3, "grad_wg_ffn2")
    dwu2 = grad_pair("wu2", db, h3, "grad_wu_ffn2")
    g_pre_tied = scatter_start("ffn2", g_pre_ffn2, wd2=dwd2, wg2=dwg2, wu2=dwu2)
    dx2, dymix, s2, s3, s1m, db_out = _mm_nn(
        [(da, wg2), (db, wu2)], "ffn_up_bwd_ffn2", None,
        tail=_tail_pre_post_bwd(x2, dx3, ymix, g_pre_tied, scale3, g_post_mix, gate2, 1.0))
    sm3 = dict(shift=s3, scale=s2 * g_pre_ffn2, gate=0.5 * g_post_ffn2 * s1,
               g_pre=(1.0 + scale3) * s2, g_post=(0.5 * gate3) * s1)

    dycat = _mm_nt(dymix, wo, "out_proj_bwd", F32)
    dwo = grad_pair("wo", ycat, dymix, "grad_wo")
    doa, dob, dg_a, dg_b = _group_norm_bwd(dycat, oa, ob, g_grp_a, g_grp_b)
    dqa, dka, dva, dsink = _attention_bwd(proj, bias_a, sinks, doa, lse_a, name="attn_a_bwd", **cfg_a)
    dqb, dkb, dvb, dbias = _attention_bwd(proj, bias_b, None, dob, lse_b, name="attn_b_bwd", **cfg_b)
    dproj = jnp.concatenate([dqa, dka, dva, dqb, dkb, dvb], axis=1)
    dwin, own["win"], db_in = _mm_tn_pair(dproj, h2, "grad_win", col_sums=True)
    g_pre_tied = scatter_start("mix", g_pre_mix, wo=dwo, win=dwin)
    dx1, dy, s2m, s3m, s1, _ = _mm_nn(
        [(dproj, win)], "in_proj_bwd", None,
        tail=_tail_pre_post_bwd(x1, dx2, y1, g_pre_tied, scale2, g_post_ffn1, gate1, 0.5))
    d_rel = jnp.dot(_diagonal_sums(dbias).reshape(H_B, SKEW), rel_m, precision=lax.Precision.HIGHEST)
    d_sinks = dsink[:, :2 * TPS_A, 0].reshape(1, H_A)

    da, db = _ffn_down_bwd(dy, wd1, a1, b1, "ffn_down_bwd_ffn1")
    dwd1 = grad_pair("wd1", u1, dy, "grad_wd_ffn1")
    dwg1 = grad_pair("wg1", da, h1, "grad_wg_ffn1")
    dwu1 = grad_pair("wu1", db, h1, "grad_wu_ffn1")
    g_pre_tied = scatter_start("ffn1", g_pre_ffn1, wd1=dwd1, wg1=dwg1, wu1=dwu1)
    dx0, s2, s3 = _mm_nn([(da, wg1), (db, wu1)], "ffn_up_bwd_ffn1", None,
                         tail=_tail_pre_bwd(xs, dx1, g_pre_tied, scale1))
    sm1 = dict(shift=s3, scale=s2 * g_pre_ffn1, gate=0.5 * g_post_ffn1 * s1,
               g_pre=(1.0 + scale1) * s2, g_post=(0.5 * gate1) * s1)

    dmod = jnp.concatenate([sm1["shift"], sm1["scale"], sm1["gate"],
                            s3m, s2m * g_pre_mix, g_post_mix * s1m,
                            sm3["shift"], sm3["scale"], sm3["gate"]], axis=1)
    small_parts = {
        "b_ada": dmod, "g_pre_ffn1": sm1["g_pre"], "g_post_ffn1": sm1["g_post"],
        "g_pre_mix": (1.0 + scale2) * s2m, "b_in": db_in, "sinks_a": d_sinks,
        "rel_bias_b": d_rel.reshape(1, H_B * N_REL), "g_grp_a": dg_a, "g_grp_b": dg_b, "b_out": db_out,
        "g_post_mix": gate2 * s1m, "g_pre_ffn2": sm3["g_pre"], "g_post_ffn2": sm3["g_post"]}
    sizes = [small_parts[n].shape[1] for n in SMALL]

    def pack(parts):
        cells = []
        for p in parts:
            cells.append(p)
            if p.shape[1] % LANES:
                cells.append(jnp.zeros((1, -p.shape[1] % LANES), F32))
        return jnp.concatenate(cells, axis=1)

    packed = pack([small_parts[n] for n in SMALL] + [loss_part])
    n_packed = packed.shape[1]
    small_sems, packed_thru, small_land, small_token = _small_gather_start(packed)

    out_g, out_d, out_m, out_v = {}, {}, {}, {}
    groups = (("ffn2", (("w_gate2", "wg2", True), ("w_up2", "wu2", True), ("w_down2", "wd2", False))),
              ("mix", (("w_in", "win", True), ("w_out", "wo", False))),
              ("ffn1", (("w_gate1", "wg1", True), ("w_up1", "wu1", True), ("w_down1", "wd1", False))))
    after = small_token
    for tag, members in groups:
        names, sems, p_thru, lands = flights[tag]
        _, l_done = _scatter_wait(sems, p_thru, lands, after, "scatter_wait_" + tag)
        slots = dict(zip(names, l_done))
        for n, key, transposed in members:
            view = (lambda t: t.T) if transposed else (lambda t: t)
            res = _adamw_from_slots(view(weights[n][0]), own[key], slots[key], view(mom_m[n][0]),
                                    view(mom_v[n][0]), "adamw_" + n)
            out_g[n], out_d[n], out_m[n], out_v[n] = (view(t)[None] for t in res)
            after = res[3]

    packed_done, small_land = _small_gather_wait(small_sems, packed_thru, small_land, after)
    gathered = lax.dynamic_update_slice(small_land, packed_done[None], (me, 0, 0))
    small_sum = _sum_rows8(gathered)
    loss = small_sum[0, n_packed - LANES]
    dmod_cols = lax.dynamic_slice(gathered.reshape(N_DEV, n_packed), (0, me * ada_cols), (N_DEV, ada_cols))
    g_ada = _ada_weight_grad(sc_all.reshape(N_DEV, d_model).T, dmod_cols)
    d_, m_, v_ = _adamw(w_ada[0], g_ada, m_w_ada[0], v_w_ada[0], "adamw_w_ada")
    out_g["w_ada"], out_d["w_ada"], out_m["w_ada"], out_v["w_ada"] = g_ada[None], d_[None], m_[None], v_[None]

    small_out = _adamw_small(small_sum, *(pack([tree[n].reshape(1, -1) for n in SMALL])
                                          for tree in (weights, mom_m, mom_v)), sizes)
    for j, n in enumerate(SMALL):
        shape = weights[n].shape
        out_g[n], out_d[n], out_m[n], out_v[n] = (t.reshape(shape) for t in small_out[4 * j:4 * j + 4])

    return (loss, dx0[None], *[out_g[n] for n in WEIGHTS], *[out_d[n] for n in WEIGHTS],
            *[out_m[n] for n in WEIGHTS], *[out_v[n] for n in WEIGHTS])
```

```python
import numpy as np
import jax
import jax.numpy as jnp
from jax import lax
from jax.experimental import pallas as pl
from jax.experimental.pallas import tpu as pltpu

F32 = jnp.float32
BF16 = jnp.bfloat16
MESH = pl.DeviceIdType.MESH
ANY = pl.BlockSpec(memory_space=pl.ANY)
VMEM_SPEC = pl.BlockSpec(memory_space=pltpu.VMEM)
SMEM_SPEC = pl.BlockSpec(memory_space=pltpu.SMEM)

N_DEV = 8
CHUNK = 64
HEAD_DIM = 64
LANES = 128
H_A, KV_A, H_B = 8, 2, 8
BACK_A, BACK_B = 2, 8
REL_CLIP = 128
N_REL = 2 * REL_CLIP + 1
QA, KVA, QB = H_A * HEAD_DIM, KV_A * HEAD_DIM, H_B * HEAD_DIM
D_IN = QA + 2 * KVA + 3 * QB
N_MOD = 9
EPS = 1e-6
NEG_INF = -1e30
QG = 4
QROWS = QG * CHUNK
TPS_A, TPS_B = 4, 2
SKEW = 1024
ADAM_LR, ADAM_B1, ADAM_B2, ADAM_EPS, ADAM_WD, ADAM_STEP = 0.001, 0.9, 0.999, 1e-08, 0.01, 10
VMEM_LIMIT = 56 * 2 ** 20


def _pick(n, cands):
    for c in cands:
        if n % c == 0:
            return c
    return n


def _pieces(n, width=2 * LANES):
    return [(lo, min(lo + width, n)) for lo in range(0, n, width)]


def _params(sem=None):
    return pltpu.CompilerParams(dimension_semantics=sem, vmem_limit_bytes=VMEM_LIMIT)


def _dot_nt(a, b):
    return lax.dot_general(a, b, (((1,), (1,)), ((), ())), preferred_element_type=F32)


def _dot_tn(a, b):
    return lax.dot_general(a, b, (((0,), (0,)), ((), ())), preferred_element_type=F32)


def _dot(a, b):
    return jnp.dot(a, b, preferred_element_type=F32)


def _sigmoid(a):
    return 0.5 * (jnp.tanh(0.5 * a) + 1.0)


def _mesh_pos():
    return lax.axis_index("x"), lax.axis_index("y"), lax.axis_index("c")


def _peer(x, y, c, r):
    px = 1 - x if r & 4 else x
    py = 1 - y if r & 2 else y
    pc = 1 - c if r & 1 else c
    return px, py, pc


class _Carry:
    def __init__(self, ins, out_shapes, scratch, start, finish, aliases=()):
        self.ins, self.out_shapes, self.scratch = list(ins), list(out_shapes), list(scratch)
        self.start, self.finish, self.aliases = start, finish, list(aliases)


def _call(body, *, name, grid, in_specs, out_specs, out_shape, args, scratch=(), sem=None, carry=None):
    single = not isinstance(out_shape, (tuple, list))
    out_specs = (out_specs,) if single else tuple(out_specs)
    out_shape = (out_shape,) if single else tuple(out_shape)
    if carry is None:
        res = pl.pallas_call(body, name=name, grid=grid, in_specs=list(in_specs), out_specs=out_specs,
                             out_shape=out_shape, scratch_shapes=list(scratch), compiler_params=_params(sem))(*args)
        return res[0] if single else res
    n_in, n_out, n_s = len(in_specs), len(out_shape), len(scratch)
    ci, co = len(carry.ins), len(carry.out_shapes)

    def wrapped(*refs):
        ins, cins = refs[:n_in], refs[n_in:n_in + ci]
        outs = refs[n_in + ci:n_in + ci + n_out]
        couts = refs[n_in + ci + n_out:n_in + ci + n_out + co]
        scr = refs[n_in + ci + n_out + co:n_in + ci + n_out + co + n_s]
        cscr = refs[n_in + ci + n_out + co + n_s:]
        first, last = None, None
        for ax, n in enumerate(grid):
            f, l = pl.program_id(ax) == 0, pl.program_id(ax) == n - 1
            first = f if first is None else first & f
            last = l if last is None else last & l
        pl.when(first)(lambda: carry.start(cins, couts, cscr))
        body(*ins, *outs, *scr)
        pl.when(last)(lambda: carry.finish(cins, couts, cscr))

    res = pl.pallas_call(
        wrapped, name=name, grid=grid, in_specs=list(in_specs) + [ANY] * ci, out_specs=out_specs + (ANY,) * co,
        out_shape=out_shape + tuple(carry.out_shapes), scratch_shapes=list(scratch) + carry.scratch,
        input_output_aliases={n_in + i: n_out + o for i, o in carry.aliases},
        compiler_params=_params(("arbitrary",) * len(grid)))(*args, *carry.ins)
    main = res[:n_out]
    return (main[0] if single else main), res[n_out:]


def _gather_carry(full=(), new=(), cont=()):
    full, new, cont = list(full), list(new), list(cont)
    n_full, n_one = len(full), len(full) + len(new)
    n_w = n_one + len(cont)
    rows = [s.shape[0] for s in full + new] + [b.shape[0] // N_DEV for b in cont]
    shapes = [jax.ShapeDtypeStruct((N_DEV * s.shape[0], s.shape[1]), s.dtype) for s in full + new]
    shapes += [jax.ShapeDtypeStruct(b.shape, b.dtype) for b in cont]

    def plan(ins, outs, scr):
        send_sems, recv_sems, local_sems = scr
        x, y, c = _mesh_pos()
        me, sibling = (x, y, c), (x, y, 1 - c)
        x_chip, y_chip, far_chip = (1 - x, y), (x, 1 - y), (1 - x, 1 - y)

        def block(buf, w, chip, core, half=None):
            n = rows[w] if half is None else rows[w] // 2
            start = (4 * chip[0] + 2 * chip[1] + core) * rows[w] + (0 if half is None else half * n)
            return buf.at[pl.ds(pl.multiple_of(start, 16), n), :]

        def copy(w, k, chip, core, to, half=None, src=None):
            return pltpu.make_async_remote_copy(
                src_ref=block(outs[w], w, chip, core, half) if src is None else src,
                dst_ref=block(outs[w], w, chip, core, half),
                send_sem=send_sems.at[w, k], recv_sem=recv_sems.at[w, k], device_id=to, device_id_type=MESH)

        def stage_one(w):
            return [copy(w, 1, (x, y), c, (*x_chip, c), src=ins[w]), copy(w, 2, (x, y), c, (*y_chip, c), src=ins[w]),
                    copy(w, 0, (x, y), c, sibling, src=ins[w])]

        def stage_two(w, held):
            return [copy(w, 5, x_chip, c, (*y_chip, c), half=0, src=block(held, w, x_chip, c, 0)),
                    copy(w, 6, y_chip, c, (*x_chip, c), half=1, src=block(held, w, y_chip, c, 1))]

        mine = [pltpu.make_async_copy(ins[w], block(outs[w], w, (x, y), c), local_sems.at[w]) for w in range(n_one)]
        return c, me, sibling, x_chip, y_chip, far_chip, copy, stage_one, stage_two, mine

    def start(ins, outs, scr):
        _, _, _, _, _, _, _, stage_one, stage_two, mine = plan(ins, outs, scr)
        for w in range(n_one):
            for cp in stage_one(w):
                cp.start()
        for w in range(n_one, n_w):
            for cp in stage_two(w, ins[w]):
                cp.start()
        for cp in mine:
            cp.start()

    def finish(ins, outs, scr):
        c, me, sibling, x_chip, y_chip, far_chip, copy, stage_one, stage_two, mine = plan(ins, outs, scr)
        sent = []

        def land_one(w):
            for k, chip in ((1, x_chip), (2, y_chip)):
                copy(w, k, chip, c, me).wait_recv()
                cp = copy(w, 2 + k, chip, c, sibling)
                cp.start()
                sent.append(cp)

        def land_two(w):
            copy(w, 5, far_chip, c, me, half=0).wait_recv()
            copy(w, 6, far_chip, c, me, half=1).wait_recv()
            cp = copy(w, 7, far_chip, c, sibling)
            cp.start()
            sent.append(cp)

        for w in range(n_full):
            land_one(w)
            for cp in stage_two(w, outs[w]):
                cp.start()
                sent.append(cp)
        for w in range(n_one, n_w):
            land_two(w)
            sent.extend(stage_two(w, ins[w]))
        for w in range(n_full, n_one):
            land_one(w)
        for w in range(n_full):
            land_two(w)
        for w in range(n_one):
            copy(w, 0, me[:2], 1 - c, me).wait_recv()
            for k, chip in ((3, x_chip), (4, y_chip)):
                copy(w, k, chip, 1 - c, me).wait_recv()
            sent.extend(stage_one(w))
        for w in list(range(n_full)) + list(range(n_one, n_w)):
            copy(w, 7, far_chip, 1 - c, me).wait_recv()
        for cp in sent:
            cp.wait_send()
        for cp in mine:
            cp.wait()

    return _Carry(
        full + new + cont, shapes,
        [pltpu.SemaphoreType.DMA((n_w, N_DEV)), pltpu.SemaphoreType.DMA((n_w, N_DEV)),
         pltpu.SemaphoreType.DMA((max(n_one, 1),))], start, finish,
        aliases=[(w, w) for w in range(n_one, n_w)])


HBM_SPEC = pl.BlockSpec(memory_space=pltpu.HBM)
SEM_SPEC = pl.BlockSpec(memory_space=pltpu.SEMAPHORE)
N_CHIP = N_DEV // 2


def _scatter_copy(part_ref, land_ref, send_sem, recv_sem, r, rows):
    x, y, c = _mesh_pos()
    px, py, _ = _peer(x, y, c, 2 * r)
    src = part_ref.at[pl.ds(pl.multiple_of((2 * px + py) * rows, 16), rows), :]
    return pltpu.make_async_remote_copy(
        src_ref=src, dst_ref=land_ref.at[r - 1], send_sem=send_sem, recv_sem=recv_sem,
        device_id=(px, py, c), device_id_type=MESH)


def _scatter_order(n_w):
    return [(w, r) for r in (3, 2, 1) for w in range(n_w)]


def _scatter_start(parts, name):
    n_w = len(parts)
    rows = [p.shape[0] // N_CHIP for p in parts]
    order = _scatter_order(n_w)
    lands = [pltpu.with_memory_space_constraint(lax.empty((N_CHIP - 1, r, p.shape[1]), p.dtype), pltpu.HBM)
             for r, p in zip(rows, parts)]

    def body(*refs):
        part_refs, land_refs = refs[:n_w], refs[n_w:2 * n_w]
        sems = refs[2 * n_w:2 * n_w + 2 * len(order)]
        token = refs[-1]
        for j, (w, r) in enumerate(order):
            _scatter_copy(part_refs[w], land_refs[w], sems[2 * j], sems[2 * j + 1], r, rows[w]).start()
        token[...] = jnp.zeros_like(token)

    n_sem = 2 * len(order)
    res = pl.pallas_call(
        body, name=name,
        out_shape=(*[pltpu.SemaphoreType.DMA(())] * n_sem, *[pltpu.HBM(p.shape, p.dtype) for p in parts],
                   *[pltpu.HBM(l.shape, l.dtype) for l in lands], jax.ShapeDtypeStruct((8, LANES), F32)),
        in_specs=[HBM_SPEC] * (2 * n_w), out_specs=(*[SEM_SPEC] * n_sem, *[HBM_SPEC] * (2 * n_w), VMEM_SPEC),
        input_output_aliases={i: n_sem + i for i in range(2 * n_w)},
        compiler_params=pltpu.CompilerParams(has_side_effects=pltpu.SideEffectType.DATAFLOW_SIDE_EFFECTING),
    )(*[pltpu.with_memory_space_constraint(p, pltpu.HBM) for p in parts], *lands)
    return (list(res[:n_sem]), list(res[n_sem:n_sem + n_w]), list(res[n_sem + n_w:n_sem + 2 * n_w]), res[-1])


def _scatter_wait(sems, parts, lands, after, name):
    n_w = len(parts)
    rows = [p.shape[0] // N_CHIP for p in parts]
    order = _scatter_order(n_w)

    def body(*refs):
        part_refs, land_refs = refs[:n_w], refs[n_w:2 * n_w]
        sem_refs = refs[2 * n_w:2 * n_w + 2 * len(order)]
        for j, (w, r) in enumerate(order):
            cp = _scatter_copy(part_refs[w], land_refs[w], sem_refs[2 * j], sem_refs[2 * j + 1], r, rows[w])
            cp.wait_send()
            cp.wait_recv()

    res = pl.pallas_call(
        body, name=name,
        out_shape=(*[pltpu.HBM(p.shape, p.dtype) for p in parts], *[pltpu.HBM(l.shape, l.dtype) for l in lands]),
        in_specs=[HBM_SPEC] * (2 * n_w) + [SEM_SPEC] * len(sems) + [ANY],
        out_specs=tuple([HBM_SPEC] * (2 * n_w)),
        input_output_aliases={i: i for i in range(2 * n_w)},
        compiler_params=pltpu.CompilerParams(has_side_effects=pltpu.SideEffectType.DATAFLOW_SIDE_EFFECTING),
    )(*parts, *lands, *sems, after)
    return list(res[:n_w]), list(res[n_w:])


def _small_copy(v_ref, land_ref, send_sem, recv_sem, r):
    x, y, c = _mesh_pos()
    px, py, pc = _peer(x, y, c, r)
    return pltpu.make_async_remote_copy(
        src_ref=v_ref, dst_ref=land_ref.at[4 * x + 2 * y + c], send_sem=send_sem, recv_sem=recv_sem,
        device_id=(px, py, pc), device_id_type=MESH)


def _small_gather_start(v):
    land = pltpu.with_memory_space_constraint(lax.empty((N_DEV,) + v.shape, v.dtype), pltpu.HBM)

    def body(v_ref, land_ref, *rest):
        sems, token = rest[:2 * (N_DEV - 1)], rest[-1]
        for r in range(1, N_DEV):
            _small_copy(v_ref, land_ref, sems[2 * r - 2], sems[2 * r - 1], r).start()
        token[...] = jnp.zeros_like(token)

    n_sem = 2 * (N_DEV - 1)
    res = pl.pallas_call(
        body, name="small_gather_start",
        out_shape=(*[pltpu.SemaphoreType.DMA(())] * n_sem, pltpu.HBM(v.shape, v.dtype),
                   pltpu.HBM(land.shape, land.dtype), jax.ShapeDtypeStruct((8, LANES), F32)),
        in_specs=[HBM_SPEC, HBM_SPEC], out_specs=(*[SEM_SPEC] * n_sem, HBM_SPEC, HBM_SPEC, VMEM_SPEC),
        input_output_aliases={0: n_sem, 1: n_sem + 1},
        compiler_params=pltpu.CompilerParams(has_side_effects=pltpu.SideEffectType.DATAFLOW_SIDE_EFFECTING),
    )(pltpu.with_memory_space_constraint(v, pltpu.HBM), land)
    return list(res[:n_sem]), res[n_sem], res[n_sem + 1], res[-1]


def _small_gather_wait(sems, v, land, after):
    def body(v_ref, land_ref, *rest):
        for r in range(1, N_DEV):
            cp = _small_copy(v_ref, land_ref, rest[2 * r - 2], rest[2 * r - 1], r)
            cp.wait_send()
            x, y, c = _mesh_pos()
            px, py, pc = _peer(x, y, c, r)
            pltpu.make_async_remote_copy(
                src_ref=v_ref, dst_ref=land_ref.at[4 * px + 2 * py + pc], send_sem=rest[2 * r - 2],
                recv_sem=rest[2 * r - 1], device_id=(px, py, pc), device_id_type=MESH).wait_recv()

    res = pl.pallas_call(
        body, name="small_gather_wait",
        out_shape=(pltpu.HBM(v.shape, v.dtype), pltpu.HBM(land.shape, land.dtype)),
        in_specs=[HBM_SPEC, HBM_SPEC] + [SEM_SPEC] * len(sems) + [ANY], out_specs=(HBM_SPEC, HBM_SPEC),
        input_output_aliases={0: 0, 1: 1},
        compiler_params=pltpu.CompilerParams(has_side_effects=pltpu.SideEffectType.DATAFLOW_SIDE_EFFECTING),
    )(v, land, *sems, after)
    return res[0], res[1]


def _ada_forward(c_row, w_ada, b_cols, carry):
    d = c_row.shape[1]
    wcols = w_ada.shape[1]
    ci, co = len(carry.ins), len(carry.out_shapes)

    def body(*refs):
        c_ref, w_ref, b_ref = refs[:3]
        cins = refs[3:3 + ci]
        sc_ref, mod_ref = refs[3 + ci:5 + ci]
        couts = refs[5 + ci:5 + ci + co]
        rows_ref, send_sems, recv_sems = refs[5 + ci + co:8 + ci + co]
        cscr = refs[8 + ci + co:]
        carry.start(cins, couts, cscr)
        x, y, c = _mesh_pos()
        me = 4 * x + 2 * y + c
        cv = c_ref[...]
        sc_ref[me] = cv * _sigmoid(cv)

        sends = []
        for r in range(1, N_DEV):
            px, py, pc = _peer(x, y, c, r)
            cp = pltpu.make_async_remote_copy(
                src_ref=sc_ref.at[me], dst_ref=sc_ref.at[me], send_sem=send_sems.at[0, r - 1],
                recv_sem=recv_sems.at[0, r - 1], device_id=(px, py, pc), device_id_type=MESH)
            cp.start()
            sends.append(cp)
        for r in range(1, N_DEV):
            px, py, pc = _peer(x, y, c, r)
            pid = 4 * px + 2 * py + pc
            pltpu.make_async_remote_copy(
                src_ref=sc_ref.at[pid], dst_ref=sc_ref.at[pid], send_sem=send_sems.at[0, r - 1],
                recv_sem=recv_sems.at[0, r - 1], device_id=(px, py, pc), device_id_type=MESH).wait_recv()
        for cp in sends:
            cp.wait_send()

        sc_all = jnp.concatenate([sc_ref[j] for j in range(N_DEV)], axis=0)
        rows = _dot(sc_all.astype(BF16), w_ref[...].astype(BF16)) + b_ref[...]
        for j in range(N_DEV):
            rows_ref[j] = rows[j:j + 1, :]
        mod_ref[me] = rows_ref[me]

        sends = []
        for r in range(1, N_DEV):
            px, py, pc = _peer(x, y, c, r)
            pid = 4 * px + 2 * py + pc
            cp = pltpu.make_async_remote_copy(
                src_ref=rows_ref.at[pid], dst_ref=mod_ref.at[me], send_sem=send_sems.at[1, r - 1],
                recv_sem=recv_sems.at[1, r - 1], device_id=(px, py, pc), device_id_type=MESH)
            cp.start()
            sends.append(cp)
        for r in range(1, N_DEV):
            px, py, pc = _peer(x, y, c, r)
            pid = 4 * px + 2 * py + pc
            pltpu.make_async_remote_copy(
                src_ref=rows_ref.at[pid], dst_ref=mod_ref.at[pid], send_sem=send_sems.at[1, r - 1],
                recv_sem=recv_sems.at[1, r - 1], device_id=(px, py, pc), device_id_type=MESH).wait_recv()
        for cp in sends:
            cp.wait_send()
        carry.finish(cins, couts, cscr)

    res = pl.pallas_call(
        body, name="ada_forward",
        out_shape=(jax.ShapeDtypeStruct((N_DEV, 1, d), F32), jax.ShapeDtypeStruct((N_DEV, 1, wcols), F32),
                   *carry.out_shapes),
        in_specs=[VMEM_SPEC, VMEM_SPEC, VMEM_SPEC] + [ANY] * ci, out_specs=(VMEM_SPEC, VMEM_SPEC) + (ANY,) * co,
        scratch_shapes=[pltpu.VMEM((N_DEV, 1, wcols), F32), pltpu.SemaphoreType.DMA((2, N_DEV - 1)),
                        pltpu.SemaphoreType.DMA((2, N_DEV - 1))] + carry.scratch,
        compiler_params=_params(),
    )(c_row, w_ada, b_cols, *carry.ins)
    return res[:2], res[2:]


def _mm_nt(a, b, name, out_dtype, bias=None, carry=None):
    m, k = a.shape
    n = b.shape[0]
    tm = _pick(m, (512, 256, 128))
    tn = _pick(n, (1408, 1152, 1024, 768, 512, 256, 128))

    def body(*refs):
        acc = _dot_nt(refs[0][...], refs[1][...])
        if bias is not None:
            acc = acc + refs[2][...]
        refs[-1][...] = acc.astype(out_dtype)

    in_specs = [pl.BlockSpec((tm, k), lambda j, i: (i, 0)), pl.BlockSpec((tn, k), lambda j, i: (j, 0))]
    args = [a, b]
    if bias is not None:
        in_specs.append(pl.BlockSpec((1, tn), lambda j, i: (0, j)))
        args.append(bias)
    return _call(body, name=name, grid=(n // tn, m // tm), in_specs=in_specs,
                 out_specs=pl.BlockSpec((tm, tn), lambda j, i: (i, j)),
                 out_shape=jax.ShapeDtypeStruct((m, n), out_dtype), args=args,
                 sem=("parallel", "parallel"), carry=carry)


class _Tail:
    def __init__(self, rows, vecs, outs, fn):
        self.rows, self.vecs, self.outs, self.fn = list(rows), list(vecs), list(outs), fn


def _mm_nn(pairs, name, out_dtype, bias=None, carry=None, tail=None):
    m, k = pairs[0][0].shape
    n = pairs[0][1].shape[1]
    n_p = len(pairs)
    tm = _pick(m, (512, 256, 128))
    tk = k if n_p == 1 else _pick(k, (1408, 1152, 1024, 768, 512, 256, 128))
    nk = k // tk
    n_b = 0 if bias is None else 1
    n_r, n_v = (len(tail.rows), len(tail.vecs)) if tail else (0, 0)
    n_in = 2 * n_p + n_b + n_r + n_v
    n_main = 0 if out_dtype is None else 1

    def finish(acc, refs, first_tile):
        if bias is not None:
            acc = acc + refs[2 * n_p][...]
        outs = refs[n_in:-1]
        if n_main:
            outs[0][...] = acc.astype(out_dtype)
        if tail is None:
            return
        rows = [r[...] for r in refs[2 * n_p + n_b:2 * n_p + n_b + n_r]]
        vecs = [v[...] for v in refs[2 * n_p + n_b + n_r:n_in]]
        vals = tail.fn(acc, rows, vecs)
        for ref, val, (dtype, kind) in zip(outs[n_main:], vals, tail.outs):
            if kind == "row":
                ref[...] = val.astype(dtype)
            else:
                @pl.when(first_tile)
                def _(ref=ref):
                    ref[...] = jnp.zeros_like(ref)

                ref[...] += val

    def body(*refs):
        acc_ref = refs[-1]
        kk, i = pl.program_id(0), pl.program_id(1)
        part = _dot(refs[0][...], refs[1][...])
        for p in range(1, n_p):
            part = part + _dot(refs[2 * p][...], refs[2 * p + 1][...])
        if nk == 1:
            finish(part, refs, i == 0)
            return
        rows = pl.ds(pl.multiple_of(i * tm, tm), tm)

        @pl.when(kk == 0)
        def _():
            acc_ref[rows, :] = part

        if nk > 2:
            @pl.when((kk > 0) & (kk < nk - 1))
            def _():
                acc_ref[rows, :] += part

        @pl.when(kk == nk - 1)
        def _():
            finish(acc_ref[rows, :] + part, refs, i == 0)

    def last_only(kk, i):
        return (jnp.where(kk == nk - 1, i, 0), 0)

    row_spec = pl.BlockSpec((tm, n), last_only)
    vec_spec = pl.BlockSpec((1, n), lambda kk, i: (0, 0))
    in_specs, args = [], []
    for a, b in pairs:
        in_specs += [pl.BlockSpec((tm, tk), lambda kk, i: (i, kk)), pl.BlockSpec((tk, n), lambda kk, i: (kk, 0))]
        args += [a, b]
    if bias is not None:
        in_specs.append(vec_spec)
        args.append(bias)
    out_specs = [row_spec] * n_main
    out_shape = [jax.ShapeDtypeStruct((m, n), out_dtype)] if n_main else []
    if tail:
        in_specs += [row_spec] * n_r + [vec_spec] * n_v
        args += tail.rows + tail.vecs
        for dtype, kind in tail.outs:
            if kind == "row":
                out_specs.append(row_spec)
                out_shape.append(jax.ShapeDtypeStruct((m, n), dtype))
            else:
                width = n if kind == "sum" else 1
                out_specs.append(pl.BlockSpec((1, width), lambda kk, i: (0, 0)))
                out_shape.append(jax.ShapeDtypeStruct((1, width), dtype))
    if tail is None:
        out_specs, out_shape = out_specs[0], out_shape[0]
    return _call(body, name=name, grid=(nk, m // tm), in_specs=in_specs, out_specs=out_specs,
                 out_shape=out_shape, args=args,
                 scratch=[pltpu.VMEM((m, n) if nk > 1 else (8, LANES), F32)],
                 sem=("arbitrary", "arbitrary"), carry=carry)


def _rms(v):
    return lax.rsqrt(jnp.mean(v * v, axis=-1, keepdims=True) + EPS)


def _col(v):
    return jnp.sum(v, axis=0, keepdims=True)


def _tail_post_pre(x, g_post, gate, weight, g_pre, scale, shift):
    def fn(y, rows, vecs):
        (xv,), (gp, gt, g, sc, sh) = rows, vecs
        xo = xv + (weight * gt) * ((y * _rms(y)) * gp)
        return xo, ((xo * _rms(xo)) * g) * (1.0 + sc) + sh

    return _Tail([x], [g_post, gate, g_pre, scale, shift], [(F32, "row"), (BF16, "row")], fn)


def _tail_post_loss(x, target, g, gate, weight):
    def fn(y, rows, vecs):
        (xv, tv), (gv, gt) = rows, vecs
        r = _rms(y)
        yn = y * r
        err = (xv + (weight * gt) * (yn * gv)) - tv
        do = err * (1.0 / y.shape[1])
        dyn = do * ((weight * gt) * gv)
        dy = r * (dyn - yn * jnp.mean(dyn * yn, axis=-1, keepdims=True))
        return do, dy, 0.5 * _col(jnp.mean(err * err, axis=-1, keepdims=True)), _col(do * yn)

    return _Tail([x, target], [g, gate], [(F32, "row"), (BF16, "row"), (F32, "one"), (F32, "sum")], fn)


def _tail_pre_bwd(x, dres, g_pre, scale):
    def fn(dh, rows, vecs):
        (xv, dr), (g, sc) = rows, vecs
        r = _rms(xv)
        n = xv * r
        dn = dh * (g * (1.0 + sc))
        return dr + r * (dn - n * jnp.mean(dn * n, axis=-1, keepdims=True)), _col(dh * n), _col(dh)

    return _Tail([x, dres], [g_pre, scale], [(F32, "row"), (F32, "sum"), (F32, "sum")], fn)


def _tail_pre_post_bwd(x, dres, y, g_pre, scale, g_post, gate, weight):
    def fn(dh, rows, vecs):
        (xv, dr, yv), (g, sc, gp, gt) = rows, vecs
        r = _rms(xv)
        n = xv * r
        dn = dh * (g * (1.0 + sc))
        dx = dr + r * (dn - n * jnp.mean(dn * n, axis=-1, keepdims=True))
        ry = _rms(yv)
        yn = yv * ry
        dyn = dx * ((weight * gt) * gp)
        dy = ry * (dyn - yn * jnp.mean(dyn * yn, axis=-1, keepdims=True))
        return dx, dy, _col(dh * n), _col(dh), _col(dx * yn), _col(dy)

    return _Tail([x, dres, y], [g_pre, scale, g_post, gate],
                 [(F32, "row"), (BF16, "row")] + [(F32, "sum")] * 4, fn)


def _mm_tn_pair(a, b, name, col_sums=False):
    k, m = a.shape
    n = b.shape[1]
    rows = m // N_DEV
    n_chip = N_DEV // 2
    tm = 4 * rows
    tk = _pick(k, (1024, 512, 256, 128))
    nk = k // tk

    def body(a_ref, b_ref, p_ref, own_ref, *rest):
        acc_ref, keep_ref, send_ref, land_ref, send_sems, recv_sems = rest[-6:]
        i, kk = pl.program_id(0), pl.program_id(1)
        x, y, c = _mesh_pos()
        if col_sums:
            cs_ref = rest[0]
            part = jnp.sum(a_ref[...].astype(F32), axis=0, keepdims=True)

            @pl.when(kk == 0)
            def _():
                cs_ref[...] = part

            @pl.when(kk > 0)
            def _():
                cs_ref[...] += part

        def push(chip):
            return pltpu.make_async_remote_copy(
                src_ref=send_ref.at[chip], dst_ref=land_ref.at[chip], send_sem=send_sems.at[chip],
                recv_sem=recv_sems.at[chip], device_id=(x, y, 1 - c), device_id_type=MESH)

        if nk == 1:
            acc = _dot_tn(a_ref[...], b_ref[...])
        else:
            @pl.when(kk == 0)
            def _():
                acc_ref[...] = jnp.zeros_like(acc_ref)

            acc_ref[...] += _dot_tn(a_ref[...], b_ref[...])
            acc = acc_ref

        for t in range(2):
            @pl.when((kk == nk - 1) & (i == t))
            def _(t=t):
                for ob in range(4):
                    chip, core = 2 * t + ob // 2, ob % 2
                    blk = acc[ob * rows:(ob + 1) * rows, :]

                    @pl.when(c == core)
                    def _(chip=chip, blk=blk):
                        keep_ref[chip] = blk

                    @pl.when(c != core)
                    def _(chip=chip, blk=blk):
                        send_ref[chip] = blk.astype(BF16)
                        push(chip).start()

        @pl.when((kk == nk - 1) & (i == 1))
        def _():
            for chip in range(n_chip):
                push(chip).wait_recv()
                val = (keep_ref[chip] + land_ref[chip].astype(F32)).astype(BF16)
                p_ref[chip * rows:(chip + 1) * rows, :] = val

                @pl.when(2 * x + y == chip)
                def _(val=val):
                    own_ref[...] = val

            for chip in range(n_chip):
                push(chip).wait_send()

    out_specs = [pl.BlockSpec((n_chip * rows, n), lambda i, kk: (0, 0)), pl.BlockSpec((rows, n), lambda i, kk: (0, 0))]
    out_shape = [jax.ShapeDtypeStruct((n_chip * rows, n), BF16), jax.ShapeDtypeStruct((rows, n), BF16)]
    if col_sums:
        out_specs.append(pl.BlockSpec((1, tm), lambda i, kk: (0, i)))
        out_shape.append(jax.ShapeDtypeStruct((1, m), F32))
    return _call(body, name=name, grid=(2, nk),
                 in_specs=[pl.BlockSpec((tk, tm), lambda i, kk: (kk, i)), pl.BlockSpec((tk, n), lambda i, kk: (kk, 0))],
                 out_specs=out_specs, out_shape=out_shape, args=[a, b],
                 scratch=[pltpu.VMEM((tm, n) if nk > 1 else (8, LANES), F32), pltpu.VMEM((n_chip, rows, n), F32),
                          pltpu.VMEM((n_chip, rows, n), BF16), pltpu.VMEM((n_chip, rows, n), BF16),
                          pltpu.SemaphoreType.DMA((n_chip,)), pltpu.SemaphoreType.DMA((n_chip,))],
                 sem=("arbitrary", "arbitrary"))


def _ffn_up(h, wg_t, wu_t, name, carry=None):
    s, d = h.shape
    f = wg_t.shape[0]
    tm = _pick(s, (512, 256, 128))
    tf = _pick(f, (1408, 1024, 512, 256, 128))

    def body(h_ref, wg_ref, wu_ref, a_ref, b_ref, u_ref):
        hh = h_ref[...]
        for lo, hi in _pieces(tf):
            a = _dot_nt(hh, wg_ref[lo:hi, :])
            b = _dot_nt(hh, wu_ref[lo:hi, :])
            a_ref[:, lo:hi] = a.astype(BF16)
            b_ref[:, lo:hi] = b.astype(BF16)
            u_ref[:, lo:hi] = ((a * _sigmoid(a)) * b).astype(BF16)

    w_spec = pl.BlockSpec((tf, d), lambda j, i: (j, 0))
    o_spec = pl.BlockSpec((tm, tf), lambda j, i: (i, j))
    o_shape = jax.ShapeDtypeStruct((s, f), BF16)
    return _call(body, name=name, grid=(f // tf, s // tm),
                 in_specs=[pl.BlockSpec((tm, d), lambda j, i: (i, 0)), w_spec, w_spec],
                 out_specs=(o_spec, o_spec, o_spec), out_shape=(o_shape, o_shape, o_shape),
                 args=[h, wg_t, wu_t], sem=("parallel", "parallel"), carry=carry)


def _ffn_down_bwd(dy, wd, a, b, name, carry=None):
    s, d = dy.shape
    f = wd.shape[0]
    tm = _pick(s, (512, 256, 128))
    tf = _pick(f, (1408, 1024, 512, 256, 128))

    def body(dy_ref, wd_ref, a_ref, b_ref, da_ref, db_ref):
        dyv = dy_ref[...]
        for lo, hi in _pieces(tf):
            du = _dot_nt(dyv, wd_ref[lo:hi, :])
            a = a_ref[:, lo:hi].astype(F32)
            b = b_ref[:, lo:hi].astype(F32)
            sig = _sigmoid(a)
            da_ref[:, lo:hi] = (du * b * (sig * (1.0 + a * (1.0 - sig)))).astype(BF16)
            db_ref[:, lo:hi] = (du * (a * sig)).astype(BF16)

    t_spec = pl.BlockSpec((tm, tf), lambda j, i: (i, j))
    o_shape = jax.ShapeDtypeStruct((s, f), BF16)
    return _call(body, name=name, grid=(f // tf, s // tm),
                 in_specs=[pl.BlockSpec((tm, d), lambda j, i: (i, 0)), pl.BlockSpec((tf, d), lambda j, i: (j, 0)),
                           t_spec, t_spec],
                 out_specs=(t_spec, t_spec), out_shape=(o_shape, o_shape), args=[dy, wd, a, b],
                 sem=("parallel", "parallel"), carry=carry)


def _row_tile(s):
    return _pick(s, (256, 128, 64))


def _vec_spec(d):
    return pl.BlockSpec((1, d), lambda i: (0, 0))


def _pre_norm(x, g, scale, shift, name):
    s, d = x.shape
    ts = _row_tile(s)

    def body(x_ref, g_ref, sc_ref, sh_ref, h_ref):
        xv = x_ref[...]
        r = lax.rsqrt(jnp.mean(xv * xv, axis=-1, keepdims=True) + EPS)
        h_ref[...] = (((xv * r) * g_ref[...]) * (1.0 + sc_ref[...]) + sh_ref[...]).astype(BF16)

    row = pl.BlockSpec((ts, d), lambda i: (i, 0))
    return _call(body, name=name, grid=(s // ts,), in_specs=[row, _vec_spec(d), _vec_spec(d), _vec_spec(d)],
                 out_specs=row, out_shape=jax.ShapeDtypeStruct((s, d), BF16), args=[x, g, scale, shift],
                 sem=("parallel",))


def _group_norm_cat(oa, ob, ga, gb):
    s = oa.shape[0]
    ts = _row_tile(s)

    def body(oa_ref, ob_ref, ga_ref, gb_ref, y_ref):
        for o_ref, g_ref, lo, w in ((oa_ref, ga_ref, 0, QA), (ob_ref, gb_ref, QA, QB)):
            ov = o_ref[...]
            r = lax.rsqrt(jnp.mean(ov * ov, axis=-1, keepdims=True) + EPS)
            y_ref[:, lo:lo + w] = ((ov * r) * g_ref[...]).astype(BF16)

    return _call(body, name="group_norm_cat", grid=(s // ts,),
                 in_specs=[pl.BlockSpec((ts, QA), lambda i: (i, 0)), pl.BlockSpec((ts, QB), lambda i: (i, 0)),
                           _vec_spec(QA), _vec_spec(QB)],
                 out_specs=pl.BlockSpec((ts, QA + QB), lambda i: (i, 0)),
                 out_shape=jax.ShapeDtypeStruct((s, QA + QB), BF16), args=[oa, ob, ga, gb], sem=("parallel",))


def _group_norm_bwd(dy, oa, ob, ga, gb):
    s = oa.shape[0]
    ts = _row_tile(s)

    def body(dy_ref, oa_ref, ob_ref, ga_ref, gb_ref, doa_ref, dob_ref, dga_ref, dgb_ref):
        @pl.when(pl.program_id(0) == 0)
        def _():
            dga_ref[...] = jnp.zeros_like(dga_ref)
            dgb_ref[...] = jnp.zeros_like(dgb_ref)

        for o_ref, g_ref, do_ref, dg_ref, lo, w in ((oa_ref, ga_ref, doa_ref, dga_ref, 0, QA),
                                                    (ob_ref, gb_ref, dob_ref, dgb_ref, QA, QB)):
            ov = o_ref[...]
            dyv = dy_ref[:, lo:lo + w]
            r = lax.rsqrt(jnp.mean(ov * ov, axis=-1, keepdims=True) + EPS)
            n = ov * r
            dn = dyv * g_ref[...]
            do_ref[...] = r * (dn - n * jnp.mean(dn * n, axis=-1, keepdims=True))
            dg_ref[...] += jnp.sum(dyv * n, axis=0, keepdims=True)

    ra = pl.BlockSpec((ts, QA), lambda i: (i, 0))
    rb = pl.BlockSpec((ts, QB), lambda i: (i, 0))
    return _call(body, name="group_norm_bwd", grid=(s // ts,),
                 in_specs=[pl.BlockSpec((ts, QA + QB), lambda i: (i, 0)), ra, rb, _vec_spec(QA), _vec_spec(QB)],
                 out_specs=(ra, rb, _vec_spec(QA), _vec_spec(QB)),
                 out_shape=(jax.ShapeDtypeStruct((s, QA), F32), jax.ShapeDtypeStruct((s, QB), F32),
                            jax.ShapeDtypeStruct((1, QA), F32), jax.ShapeDtypeStruct((1, QB), F32)),
                 args=[dy, oa, ob, ga, gb], sem=("arbitrary",))


def _n_variants(n_back):
    return -(-n_back // QG) + 1


def _alibi_bias():
    i = np.arange(QROWS)[:, None]
    j = np.arange((QG + BACK_A) * CHUNK)[None, :]
    dist = np.abs(BACK_A * CHUNK + i - j).astype(np.float32)
    dc = j // CHUNK - i // CHUNK
    valid = (dc >= 0) & (dc <= BACK_A)
    slopes = np.array([2.0 ** (-8.0 * (h + 1) / H_A) for h in range(H_A)], dtype=np.float32)
    bias = -slopes[:, None, None] * dist[None]
    out = [np.where((valid & (j >= (BACK_A - QG * v) * CHUNK))[None], bias, np.float32(NEG_INF))
           for v in range(_n_variants(BACK_A))]
    return jnp.asarray(np.stack(out).astype(np.float32))


def _rel_index_matrix():
    cc = np.arange(SKEW)
    dist = np.where(cc < SKEW - QROWS, BACK_B * CHUNK - cc, BACK_B * CHUNK + SKEW - cc)
    idx = np.clip(dist, -REL_CLIP, REL_CLIP) + REL_CLIP
    m = np.zeros((SKEW, N_REL), np.float32)
    m[cc, idx] = 1.0
    return jnp.asarray(m)


def _toeplitz_bias(vec, carry=None):
    lk = (QG + BACK_B) * CHUNK
    nv = _n_variants(BACK_B)

    def body(v_ref, o_ref):
        xv = jnp.broadcast_to(v_ref[0], (QROWS, SKEW))
        row = lax.broadcasted_iota(jnp.int32, (QROWS, SKEW), 0)
        for bit in range(QROWS.bit_length() - 1):
            xv = jnp.where((row >> bit) & 1 == 1, pltpu.roll(xv, 1 << bit, 1), xv)
        ri = lax.broadcasted_iota(jnp.int32, (QROWS, lk), 0) // CHUNK
        col = lax.broadcasted_iota(jnp.int32, (QROWS, lk), 1)
        ci = col // CHUNK
        valid = (ci - ri >= 0) & (ci - ri <= BACK_B)
        for v in range(nv):
            o_ref[v, 0] = jnp.where(valid & (col >= (BACK_B - QG * v) * CHUNK), xv[:, :lk], NEG_INF)

    return _call(body, name="toeplitz_bias", grid=(H_B,),
                 in_specs=[pl.BlockSpec((1, 1, SKEW), lambda h: (h, 0, 0))],
                 out_specs=pl.BlockSpec((nv, 1, QROWS, lk), lambda h: (0, h, 0, 0)),
                 out_shape=jax.ShapeDtypeStruct((nv, H_B, QROWS, lk), F32), args=[vec], sem=("parallel",),
                 carry=carry)


def _diagonal_sums(dbias):
    lk = dbias.shape[2]

    def body(d_ref, o_ref):
        xp = jnp.concatenate([d_ref[0], jnp.zeros((QROWS, SKEW - lk), F32)], axis=1)
        xv = xp[0:CHUNK]
        for q in range(1, QG):
            xv = xv + pltpu.roll(xp[q * CHUNK:(q + 1) * CHUNK], SKEW - q * CHUNK, 1)
        row = lax.broadcasted_iota(jnp.int32, (CHUNK, SKEW), 0)
        for bit in range(CHUNK.bit_length() - 1):
            xv = jnp.where((row >> bit) & 1 == 1, pltpu.roll(xv, SKEW - (1 << bit), 1), xv)
        o_ref[0] = jnp.sum(xv, axis=0, keepdims=True)

    return _call(body, name="diagonal_sums", grid=(H_B,),
                 in_specs=[pl.BlockSpec((1, QROWS, lk), lambda h: (h, 0, 0))],
                 out_specs=pl.BlockSpec((1, 1, SKEW), lambda h: (h, 0, 0)),
                 out_shape=jax.ShapeDtypeStruct((H_B, 1, SKEW), F32), args=[dbias], sem=("parallel",))


def _attn_common(s, n_back, gqa, q_col, k_col, v_col, TPS):
    lk = (QG + n_back) * CHUNK
    pad = n_back * CHUNK
    wide = TPS * LANES
    q_spec = pl.BlockSpec((QROWS, wide), lambda t, g: (g, q_col // TPS + t))
    if gqa:
        k_spec = pl.BlockSpec((s, LANES), lambda t, g: (0, k_col))
        v_spec = pl.BlockSpec((s, LANES), lambda t, g: (0, v_col))
    else:
        k_spec = pl.BlockSpec((s, wide), lambda t, g: (0, k_col // TPS + t))
        v_spec = pl.BlockSpec((s, wide), lambda t, g: (0, v_col // TPS + t))
    last_variant = _n_variants(n_back) - 1
    bias_spec = pl.BlockSpec((None, 2 * TPS, QROWS, lk), lambda t, g: (jnp.minimum(g, last_variant), t, 0, 0))
    tile_spec = pl.BlockSpec((QROWS, wide), lambda t, g: (g, t))
    return lk, pad, q_spec, k_spec, v_spec, bias_spec, tile_spec


def _attention_fwd(proj, bias, sinks, *, n_back, gqa, q_col, k_col, v_col, TPS, name, carry=None):
    s = proj.shape[0]
    lk, pad, q_spec, k_spec, v_spec, bias_spec, tile_spec = _attn_common(s, n_back, gqa, q_col, k_col, v_col, TPS)
    n_t, n_g = 512 // (TPS * LANES), s // QROWS
    kv_wide = LANES if gqa else TPS * LANES

    def body(*refs):
        if gqa:
            q_ref, k_ref, v_ref, bias_ref, sink_ref, o_ref, l_ref, kpad, vpad = refs
        else:
            q_ref, k_ref, v_ref, bias_ref, o_ref, l_ref, kpad, vpad = refs
        t, g = pl.program_id(0), pl.program_id(1)

        @pl.when(g == 0)
        def _():
            kpad[0:pad, :] = jnp.zeros((pad, kv_wide), BF16)
            vpad[0:pad, :] = jnp.zeros((pad, kv_wide), BF16)
            kpad[pad:, :] = k_ref[...]
            vpad[pad:, :] = v_ref[...]

        start = pl.multiple_of(g * QROWS, QROWS)
        half = lax.broadcasted_iota(jnp.int32, (QROWS, LANES), 1) // HEAD_DIM
        for tt in range(TPS):
            lanes = slice(tt * LANES, (tt + 1) * LANES)
            kv_lanes = slice(0, LANES) if gqa else lanes
            kb = kpad[pl.ds(start, lk), kv_lanes]
            vb = vpad[pl.ds(start, lk), kv_lanes]
            q = q_ref[:, lanes] * (HEAD_DIM ** -0.5)
            if gqa:
                hk = (TPS * t + tt) // 2
                q_rolled = pltpu.roll(q.astype(F32), HEAD_DIM, 1).astype(BF16)
            outs, lses = [], []
            for e in range(2):
                if gqa:
                    kv_half = hk
                    src = jnp.where(hk == e, q, q_rolled)
                else:
                    kv_half = e
                    src = q
                qm = jnp.where(half == kv_half, src, jnp.zeros_like(src))
                sc = _dot_nt(qm, kb) + bias_ref[2 * tt + e]
                m = jnp.max(sc, axis=-1, keepdims=True)
                if gqa:
                    sk = sink_ref[2 * (TPS * t + tt) + e]
                    m = jnp.maximum(m, sk)
                p = jnp.exp(sc - m)
                l = jnp.sum(p, axis=-1, keepdims=True)
                if gqa:
                    l = l + jnp.exp(sk - m)
                pn = p / l
                outs.append(_dot(pn.astype(BF16), vb))
                lses.append(m + jnp.log(l))
            if gqa:
                same = jnp.where(hk == 0, outs[0], outs[1])
                other = jnp.where(hk == 0, outs[1], outs[0])
                o_ref[:, lanes] = jnp.where(half == hk, same, pltpu.roll(other, HEAD_DIM, 1))
            else:
                o_ref[:, lanes] = jnp.where(half == 0, outs[0], outs[1])
            l_ref[:, lanes] = jnp.where(half == 0, lses[0], lses[1])

    in_specs = [q_spec, k_spec, v_spec, bias_spec] + ([SMEM_SPEC] if gqa else [])
    args = [proj, proj, proj, bias] + ([sinks] if gqa else [])
    o_shape = jax.ShapeDtypeStruct((s, 512), F32)
    return _call(body, name=name, grid=(n_t, n_g), in_specs=in_specs, out_specs=(tile_spec, tile_spec),
                 out_shape=(o_shape, o_shape), args=args,
                 scratch=[pltpu.VMEM((s + pad, kv_wide), BF16), pltpu.VMEM((s + pad, kv_wide), BF16)],
                 sem=("arbitrary", "arbitrary"), carry=carry)


def _attention_bwd(proj, bias, sinks, do, lse, *, n_back, gqa, q_col, k_col, v_col, TPS, name, carry=None):
    s = proj.shape[0]
    lk, pad, q_spec, k_spec, v_spec, bias_spec, tile_spec = _attn_common(s, n_back, gqa, q_col, k_col, v_col, TPS)
    n_t, n_g = 512 // (TPS * LANES), s // QROWS
    kv_wide = LANES if gqa else TPS * LANES

    def body(*refs):
        if gqa:
            (q_ref, k_ref, v_ref, bias_ref, sink_ref, do_ref, l_ref,
             dq_ref, dk_ref, dv_ref, dsink_ref, kpad, vpad, dkpad, dvpad) = refs
        else:
            (q_ref, k_ref, v_ref, bias_ref, do_ref, l_ref,
             dq_ref, dk_ref, dv_ref, dbias_ref, kpad, vpad, dkpad, dvpad) = refs
        t, g = pl.program_id(0), pl.program_id(1)

        @pl.when(g == 0)
        def _():
            kpad[0:pad, :] = jnp.zeros((pad, kv_wide), BF16)
            vpad[0:pad, :] = jnp.zeros((pad, kv_wide), BF16)
            kpad[pad:, :] = k_ref[...]
            vpad[pad:, :] = v_ref[...]
            if gqa:
                dsink_ref[...] = jnp.zeros_like(dsink_ref)
            else:
                dbias_ref[...] = jnp.zeros_like(dbias_ref)

        @pl.when((g == 0) & (t == 0) if gqa else g == 0)
        def _():
            dkpad[...] = jnp.zeros_like(dkpad)
            dvpad[...] = jnp.zeros_like(dvpad)

        start = pl.multiple_of(g * QROWS, QROWS)
        half = lax.broadcasted_iota(jnp.int32, (QROWS, LANES), 1) // HEAD_DIM
        for tt in range(TPS):
            lanes = slice(tt * LANES, (tt + 1) * LANES)
            kv_lanes = slice(0, LANES) if gqa else lanes
            kb = kpad[pl.ds(start, lk), kv_lanes]
            vb = vpad[pl.ds(start, lk), kv_lanes]
            q = q_ref[:, lanes]
            dov = do_ref[:, lanes]
            lv = l_ref[:, lanes]
            if gqa:
                hk = (TPS * t + tt) // 2
                q_rolled = pltpu.roll(q.astype(F32), HEAD_DIM, 1).astype(BF16)
                do_rolled = pltpu.roll(dov, HEAD_DIM, 1)
            dqs = []
            dk_acc = jnp.zeros((lk, LANES), F32)
            dv_acc = jnp.zeros((lk, LANES), F32)
            for e in range(2):
                if gqa:
                    kv_half = hk
                    src = jnp.where(hk == e, q, q_rolled)
                    do_src = jnp.where(hk == e, dov, do_rolled)
                else:
                    kv_half = e
                    src = q
                    do_src = dov
                qm = jnp.where(half == kv_half, src, jnp.zeros_like(src))
                dom = jnp.where(half == kv_half, do_src, 0.0).astype(BF16)
                lcol = jnp.max(jnp.where(half == e, lv, -jnp.inf), axis=-1, keepdims=True)
                sc = _dot_nt(qm * (HEAD_DIM ** -0.5), kb) + bias_ref[2 * tt + e]
                pn = jnp.exp(sc - lcol)
                dp = _dot_nt(dom, vb)
                delta = jnp.sum(pn * dp, axis=-1, keepdims=True)
                ds = pn * (dp - delta)
                if gqa:
                    p_sink = jnp.exp(sink_ref[2 * (TPS * t + tt) + e] - lcol)
                    dsk = -jnp.sum(p_sink * delta, axis=0, keepdims=True)
                    row = 2 * tt + e
                    dsink_ref[0, row:row + 1, :] += jnp.broadcast_to(dsk, (1, LANES))
                else:
                    dbias_ref[2 * tt + e] += ds
                dsb = (ds * (HEAD_DIM ** -0.5)).astype(BF16)
                dqs.append(_dot(dsb, kb))
                dk_acc = dk_acc + _dot_tn(dsb, qm)
                dv_acc = dv_acc + _dot_tn(pn.astype(BF16), dom)
            dkpad[pl.ds(start, lk), kv_lanes] += dk_acc
            dvpad[pl.ds(start, lk), kv_lanes] += dv_acc
            if gqa:
                same = jnp.where(hk == 0, dqs[0], dqs[1])
                other = jnp.where(hk == 0, dqs[1], dqs[0])
                dq_ref[:, lanes] = jnp.where(half == hk, same, pltpu.roll(other, HEAD_DIM, 1)).astype(BF16)
            else:
                dq_ref[:, lanes] = jnp.where(half == 0, dqs[0], dqs[1]).astype(BF16)

        @pl.when((g == n_g - 1) & (t == n_t - 1) if gqa else g == n_g - 1)
        def _():
            dk_ref[...] = dkpad[pad:, :].astype(BF16)
            dv_ref[...] = dvpad[pad:, :].astype(BF16)

    in_specs = [q_spec, k_spec, v_spec, bias_spec] + ([SMEM_SPEC] if gqa else []) + [tile_spec, tile_spec]
    args = [proj, proj, proj, bias] + ([sinks] if gqa else []) + [do, lse]
    if gqa:
        kv_out = pl.BlockSpec((s, LANES), lambda t, g: (0, 0))
        kv_shape = jax.ShapeDtypeStruct((s, LANES), BF16)
        extra_spec = pl.BlockSpec((1, 8, LANES), lambda t, g: (t, 0, 0))
        extra_shape = jax.ShapeDtypeStruct((n_t, 8, LANES), F32)
    else:
        kv_out = pl.BlockSpec((s, kv_wide), lambda t, g: (0, t))
        kv_shape = jax.ShapeDtypeStruct((s, 512), BF16)
        extra_spec = pl.BlockSpec((2 * TPS, QROWS, lk), lambda t, g: (t, 0, 0))
        extra_shape = jax.ShapeDtypeStruct(bias.shape[1:], F32)
    return _call(body, name=name, grid=(n_t, n_g), in_specs=in_specs,
                 out_specs=(tile_spec, kv_out, kv_out, extra_spec),
                 out_shape=(jax.ShapeDtypeStruct((s, 512), BF16), kv_shape, kv_shape, extra_shape), args=args,
                 scratch=[pltpu.VMEM((s + pad, kv_wide), BF16), pltpu.VMEM((s + pad, kv_wide), BF16),
                          pltpu.VMEM((s + pad, kv_wide), F32), pltpu.VMEM((s + pad, kv_wide), F32)],
                 sem=("arbitrary", "arbitrary"), carry=carry)


def _sum_rows8(g):
    n = g.shape[2]

    def body(g_ref, o_ref):
        acc = g_ref[0]
        for j in range(1, N_DEV):
            acc = acc + g_ref[j]
        o_ref[...] = acc

    return pl.pallas_call(
        body, name="sum_small_grads", in_specs=[VMEM_SPEC], out_specs=VMEM_SPEC,
        out_shape=jax.ShapeDtypeStruct((1, n), F32), compiler_params=_params(),
    )(g)


def _ada_weight_grad(sc_t, dmod_cols):
    d = sc_t.shape[0]
    w = dmod_cols.shape[1]
    td = _pick(d, (256, 128))

    def body(sc_ref, dm_ref, o_ref):
        scv = sc_ref[...]
        dmv = dm_ref[...]
        acc = scv[:, 0:1] * dmv[0:1, :]
        for b in range(1, N_DEV):
            acc = acc + scv[:, b:b + 1] * dmv[b:b + 1, :]
        o_ref[...] = acc

    return _call(body, name="ada_weight_grad", grid=(d // td,),
                 in_specs=[pl.BlockSpec((td, N_DEV), lambda i: (i, 0)), pl.BlockSpec((N_DEV, w), lambda i: (0, 0))],
                 out_specs=pl.BlockSpec((td, w), lambda i: (i, 0)), out_shape=jax.ShapeDtypeStruct((d, w), F32),
                 args=[sc_t, dmod_cols], sem=("parallel",))


def _adamw_update(w, gv, m, v):
    nm = ADAM_B1 * m + (1.0 - ADAM_B1) * gv
    nv = ADAM_B2 * v + (1.0 - ADAM_B2) * (gv * gv)
    m_hat = nm / (1.0 - ADAM_B1 ** ADAM_STEP)
    v_hat = nv / (1.0 - ADAM_B2 ** ADAM_STEP)
    return -ADAM_LR * (m_hat / (jnp.sqrt(v_hat) + ADAM_EPS) + ADAM_WD * w), nm, nv


def _adamw(w, g, m, v, name):
    rows, cols = w.shape
    tr = _pick(rows, (256, 176, 128, 88, 64)) if rows > 256 else rows

    def body(w_ref, g_ref, m_ref, v_ref, d_ref, nm_ref, nv_ref):
        d_ref[...], nm_ref[...], nv_ref[...] = _adamw_update(w_ref[...], g_ref[...], m_ref[...], v_ref[...])

    spec = pl.BlockSpec((tr, cols), lambda i: (i, 0))
    shape = jax.ShapeDtypeStruct((rows, cols), F32)
    return _call(body, name=name, grid=(rows // tr,), in_specs=[spec] * 4, out_specs=(spec, spec, spec),
                 out_shape=(shape, shape, shape), args=[w, g, m, v], sem=("parallel",))


def _adamw_from_slots(w, own, slots, m, v, name):
    n_slots, rows, k = slots.shape

    def body(o_ref, s_ref, w_ref, m_ref, v_ref, g_ref, d_ref, nm_ref, nv_ref):
        gv = o_ref[...].astype(F32)
        for j in range(n_slots):
            gv = gv + s_ref[j].astype(F32)
        g_ref[...] = gv
        d_ref[...], nm_ref[...], nv_ref[...] = _adamw_update(w_ref[...], gv, m_ref[...], v_ref[...])

    tr = rows // 2 if rows % 32 == 0 else rows
    spec = pl.BlockSpec((tr, k), lambda i: (i, 0))
    shape = jax.ShapeDtypeStruct((rows, k), F32)
    return _call(body, name=name, grid=(rows // tr,),
                 in_specs=[spec, pl.BlockSpec((n_slots, tr, k), lambda i: (0, i, 0)), spec, spec, spec],
                 out_specs=(spec, spec, spec, spec), out_shape=(shape, shape, shape, shape),
                 args=[own, slots, w, m, v], sem=("parallel",))


def _adamw_small(g, w, m, v, sizes):
    n = w.shape[1]
    offs, off = [], 0
    for size in sizes:
        offs.append(off)
        off += size + (-size % LANES)

    def body(g_ref, w_ref, m_ref, v_ref, *out_refs):
        gv = g_ref[:, 0:n]
        dv, nm, nv = _adamw_update(w_ref[...], gv, m_ref[...], v_ref[...])
        for j, (o, size) in enumerate(zip(offs, sizes)):
            for k, val in enumerate((gv, dv, nm, nv)):
                out_refs[4 * j + k][...] = val[:, o:o + size]

    shapes = [jax.ShapeDtypeStruct((1, size), F32) for size in sizes for _ in range(4)]
    return pl.pallas_call(
        body, name="adamw_small", in_specs=[VMEM_SPEC] * 4, out_specs=tuple([VMEM_SPEC] * len(shapes)),
        out_shape=tuple(shapes), compiler_params=_params(),
    )(g, w, m, v)


SMALL = ("b_ada", "g_pre_ffn1", "g_post_ffn1", "g_pre_mix", "b_in", "sinks_a", "rel_bias_b", "g_grp_a",
         "g_grp_b", "b_out", "g_post_mix", "g_pre_ffn2", "g_post_ffn2")
WEIGHTS = ("w_ada", "b_ada", "g_pre_ffn1", "w_gate1", "w_up1", "w_down1", "g_post_ffn1", "g_pre_mix", "w_in",
           "b_in", "sinks_a", "rel_bias_b", "g_grp_a", "g_grp_b", "w_out", "b_out", "g_post_mix", "g_pre_ffn2",
           "w_gate2", "w_up2", "w_down2", "g_post_ffn2")


def kernel(x, c, w_ada, b_ada, g_pre_ffn1, w_gate1, w_up1, w_down1, g_post_ffn1, g_pre_mix, w_in, b_in, sinks_a, rel_bias_b, g_grp_a, g_grp_b, w_out, b_out, g_post_mix, g_pre_ffn2, w_gate2, w_up2, w_down2, g_post_ffn2, loss_target, m_w_ada, m_b_ada, m_g_pre_ffn1, m_w_gate1, m_w_up1, m_w_down1, m_g_post_ffn1, m_g_pre_mix, m_w_in, m_b_in, m_sinks_a, m_rel_bias_b, m_g_grp_a, m_g_grp_b, m_w_out, m_b_out, m_g_post_mix, m_g_pre_ffn2, m_w_gate2, m_w_up2, m_w_down2, m_g_post_ffn2, v_w_ada, v_b_ada, v_g_pre_ffn1, v_w_gate1, v_w_up1, v_w_down1, v_g_post_ffn1, v_g_pre_mix, v_w_in, v_b_in, v_sinks_a, v_rel_bias_b, v_g_grp_a, v_g_grp_b, v_w_out, v_b_out, v_g_post_mix, v_g_pre_ffn2, v_w_gate2, v_w_up2, v_w_down2, v_g_post_ffn2):
    given = dict(locals())
    weights = {n: given[n] for n in WEIGHTS}
    mom_m = {n: given["m_" + n] for n in WEIGHTS}
    mom_v = {n: given["v_" + n] for n in WEIGHTS}

    me = 4 * lax.axis_index("x") + 2 * lax.axis_index("y") + lax.axis_index("c")
    xs = x[0]
    tgt = loss_target[0]
    d_model = xs.shape[1]
    ada_cols = w_ada.shape[2]

    sh = {"wg1": w_gate1[0].T, "wu1": w_up1[0].T, "wd1": w_down1[0], "win": w_in[0].T, "wo": w_out[0],
          "wg2": w_gate2[0].T, "wu2": w_up2[0].T, "wd2": w_down2[0]}
    sh = {k: v.astype(BF16) for k, v in sh.items()}

    def gather(full=(), new=(), cont=()):
        return _gather_carry([sh[n] for n in full], [sh[n] for n in new], cont)

    bias_a = _alibi_bias()
    rel_m = _rel_index_matrix()
    rel_vec = jnp.dot(rel_bias_b[0], rel_m.T, precision=lax.Precision.HIGHEST)
    bias_b, (wg1, wu1, wd1_part) = _toeplitz_bias(rel_vec.reshape(H_B, 1, SKEW),
                                                  carry=gather(full=("wg1", "wu1"), new=("wd1",)))

    b_cols = lax.dynamic_slice(b_ada, (0, me * ada_cols), (1, ada_cols))
    (sc_all, mod_rows), _ = _ada_forward(c, w_ada[0], b_cols, _Carry([], [], [], lambda *a: None, lambda *a: None))
    mod = mod_rows.reshape(N_MOD, d_model)
    shift1, scale1, gate1, shift2, scale2, gate2, shift3, scale3, gate3 = (mod[i:i + 1] for i in range(N_MOD))

    h1 = _pre_norm(xs, g_pre_ffn1, scale1, shift1, "pre_norm_ffn1")
    (a1, b1, u1), (win_part, wo_part, wd1) = _ffn_up(h1, wg1, wu1, "ffn_up_ffn1",
                                                     carry=gather(new=("win", "wo"), cont=(wd1_part,)))
    (y1, x1, h2), (wg2_part, win, wo) = _mm_nn(
        [(u1, wd1)], "ffn_down_ffn1", F32, carry=gather(new=("wg2",), cont=(win_part, wo_part)),
        tail=_tail_post_pre(xs, g_post_ffn1, gate1, 0.5, g_pre_mix, scale2, shift2))

    proj, (wg2,) = _mm_nt(h2, win, "in_proj", BF16, bias=b_in, carry=gather(cont=(wg2_part,)))
    sinks = sinks_a[0]
    cfg_a = dict(n_back=BACK_A, gqa=True, q_col=0, k_col=QA // LANES, v_col=(QA + KVA) // LANES, TPS=TPS_A)
    cfg_b = dict(n_back=BACK_B, gqa=False, q_col=(QA + 2 * KVA) // LANES, k_col=(QA + 2 * KVA + QB) // LANES,
                 v_col=(QA + 2 * KVA + 2 * QB) // LANES, TPS=TPS_B)
    (oa, lse_a), (wu2_part, wd2_part) = _attention_fwd(proj, bias_a, sinks, name="attn_a",
                                                       carry=gather(new=("wu2", "wd2")), **cfg_a)
    (ob, lse_b), (wu2, wd2) = _attention_fwd(proj, bias_b, None, name="attn_b",
                                             carry=gather(cont=(wu2_part, wd2_part)), **cfg_b)
    ycat = _group_norm_cat(oa, ob, g_grp_a, g_grp_b)
    ymix, x2, h3 = _mm_nn([(ycat, wo)], "out_proj", F32, bias=b_out,
                          tail=_tail_post_pre(x1, g_post_mix, gate2, 1.0, g_pre_ffn2, scale3, shift3))

    a3, b3, u3 = _ffn_up(h3, wg2, wu2, "ffn_up_ffn2")

    flights, own = {}, {}

    def grad_pair(key, a_mat, b_mat, name):
        part, own[key] = _mm_tn_pair(a_mat, b_mat, name)
        return part

    def scatter_start(tag, after_vec, **parts):
        names = list(parts)
        sems, p_thru, lands, token = _scatter_start([parts[n] for n in names], "scatter_start_" + tag)
        flights[tag] = (names, sems, p_thru, lands)
        return after_vec + token[0:1, 0:1]

    dx3, dy, loss_part, s1 = _mm_nn([(u3, wd2)], "ffn_down_ffn2", None,
                                    tail=_tail_post_loss(x2, tgt, g_post_ffn2, gate3, 0.5))
    da, db = _ffn_down_bwd(dy, wd2, a3, b3, "ffn_down_bwd_ffn2")
    dwd2 = grad_pair("wd2", u3, dy, "grad_wd_ffn2")
    dwg2 = grad_pair("wg2", da, h3, "grad_wg_ffn2")
    dwu2 = grad_pair("wu2", db, h3, "grad_wu_ffn2")
    g_pre_tied = scatter_start("ffn2", g_pre_ffn2, wd2=dwd2, wg2=dwg2, wu2=dwu2)
    dx2, dymix, s2, s3, s1m, db_out = _mm_nn(
        [(da, wg2), (db, wu2)], "ffn_up_bwd_ffn2", None,
        tail=_tail_pre_post_bwd(x2, dx3, ymix, g_pre_tied, scale3, g_post_mix, gate2, 1.0))
    sm3 = dict(shift=s3, scale=s2 * g_pre_ffn2, gate=0.5 * g_post_ffn2 * s1,
               g_pre=(1.0 + scale3) * s2, g_post=(0.5 * gate3) * s1)

    dycat = _mm_nt(dymix, wo, "out_proj_bwd", F32)
    dwo = grad_pair("wo", ycat, dymix, "grad_wo")
    doa, dob, dg_a, dg_b = _group_norm_bwd(dycat, oa, ob, g_grp_a, g_grp_b)
    dqa, dka, dva, dsink = _attention_bwd(proj, bias_a, sinks, doa, lse_a, name="attn_a_bwd", **cfg_a)
    dqb, dkb, dvb, dbias = _attention_bwd(proj, bias_b, None, dob, lse_b, name="attn_b_bwd", **cfg_b)
    dproj = jnp.concatenate([dqa, dka, dva, dqb, dkb, dvb], axis=1)
    dwin, own["win"], db_in = _mm_tn_pair(dproj, h2, "grad_win", col_sums=True)
    g_pre_tied = scatter_start("mix", g_pre_mix, wo=dwo, win=dwin)
    dx1, dy, s2m, s3m, s1, _ = _mm_nn(
        [(dproj, win)], "in_proj_bwd", None,
        tail=_tail_pre_post_bwd(x1, dx2, y1, g_pre_tied, scale2, g_post_ffn1, gate1, 0.5))
    d_rel = jnp.dot(_diagonal_sums(dbias).reshape(H_B, SKEW), rel_m, precision=lax.Precision.HIGHEST)
    d_sinks = dsink[:, :2 * TPS_A, 0].reshape(1, H_A)

    da, db = _ffn_down_bwd(dy, wd1, a1, b1, "ffn_down_bwd_ffn1")
    dwd1 = grad_pair("wd1", u1, dy, "grad_wd_ffn1")
    dwg1 = grad_pair("wg1", da, h1, "grad_wg_ffn1")
    dwu1 = grad_pair("wu1", db, h1, "grad_wu_ffn1")
    g_pre_tied = scatter_start("ffn1", g_pre_ffn1, wd1=dwd1, wg1=dwg1, wu1=dwu1)
    dx0, s2, s3 = _mm_nn([(da, wg1), (db, wu1)], "ffn_up_bwd_ffn1", None,
                         tail=_tail_pre_bwd(xs, dx1, g_pre_tied, scale1))
    sm1 = dict(shift=s3, scale=s2 * g_pre_ffn1, gate=0.5 * g_post_ffn1 * s1,
               g_pre=(1.0 + scale1) * s2, g_post=(0.5 * gate1) * s1)

    dmod = jnp.concatenate([sm1["shift"], sm1["scale"], sm1["gate"],
                            s3m, s2m * g_pre_mix, g_post_mix * s1m,
                            sm3["shift"], sm3["scale"], sm3["gate"]], axis=1)
    small_parts = {
        "b_ada": dmod, "g_pre_ffn1": sm1["g_pre"], "g_post_ffn1": sm1["g_post"],
        "g_pre_mix": (1.0 + scale2) * s2m, "b_in": db_in, "sinks_a": d_sinks,
        "rel_bias_b": d_rel.reshape(1, H_B * N_REL), "g_grp_a": dg_a, "g_grp_b": dg_b, "b_out": db_out,
        "g_post_mix": gate2 * s1m, "g_pre_ffn2": sm3["g_pre"], "g_post_ffn2": sm3["g_post"]}
    sizes = [small_parts[n].shape[1] for n in SMALL]

    def pack(parts):
        cells = []
        for p in parts:
            cells.append(p)
            if p.shape[1] % LANES:
                cells.append(jnp.zeros((1, -p.shape[1] % LANES), F32))
        return jnp.concatenate(cells, axis=1)

    packed = pack([small_parts[n] for n in SMALL] + [loss_part])
    n_packed = packed.shape[1]
    small_sems, packed_thru, small_land, small_token = _small_gather_start(packed)

    out_g, out_d, out_m, out_v = {}, {}, {}, {}
    groups = (("ffn2", (("w_gate2", "wg2", True), ("w_up2", "wu2", True), ("w_down2", "wd2", False))),
              ("mix", (("w_in", "win", True), ("w_out", "wo", False))),
              ("ffn1", (("w_gate1", "wg1", True), ("w_up1", "wu1", True), ("w_down1", "wd1", False))))
    after = small_token
    for tag, members in groups:
        names, sems, p_thru, lands = flights[tag]
        _, l_done = _scatter_wait(sems, p_thru, lands, after, "scatter_wait_" + tag)
        slots = dict(zip(names, l_done))
        for n, key, transposed in members:
            view = (lambda t: t.T) if transposed else (lambda t: t)
            res = _adamw_from_slots(view(weights[n][0]), own[key], slots[key], view(mom_m[n][0]),
                                    view(mom_v[n][0]), "adamw_" + n)
            out_g[n], out_d[n], out_m[n], out_v[n] = (view(t)[None] for t in res)
            after = res[3]

    packed_done, small_land = _small_gather_wait(small_sems, packed_thru, small_land, after)
    gathered = lax.dynamic_update_slice(small_land, packed_done[None], (me, 0, 0))
    small_sum = _sum_rows8(gathered)
    loss = small_sum[0, n_packed - LANES]
    dmod_cols = lax.dynamic_slice(gathered.reshape(N_DEV, n_packed), (0, me * ada_cols), (N_DEV, ada_cols))
    g_ada = _ada_weight_grad(sc_all.reshape(N_DEV, d_model).T, dmod_cols)
    d_, m_, v_ = _adamw(w_ada[0], g_ada, m_w_ada[0], v_w_ada[0], "adamw_w_ada")
    out_g["w_ada"], out_d["w_ada"], out_m["w_ada"], out_v["w_ada"] = g_ada[None], d_[None], m_[None], v_[None]

    small_out = _adamw_small(small_sum, *(pack([tree[n].reshape(1, -1) for n in SMALL])
                                          for tree in (weights, mom_m, mom_v)), sizes)
    for j, n in enumerate(SMALL):
        shape = weights[n].shape
        out_g[n], out_d[n], out_m[n], out_v[n] = (t.reshape(shape) for t in small_out[4 * j:4 * j + 4])

    return (loss, dx0[None], *[out_g[n] for n in WEIGHTS], *[out_d[n] for n in WEIGHTS],
            *[out_m[n] for n in WEIGHTS], *[out_v[n] for n in WEIGHTS])
```

```python
import numpy as np
import jax
import jax.numpy as jnp
from jax import lax
from jax.experimental import pallas as pl
from jax.experimental.pallas import tpu as pltpu

F32 = jnp.float32
BF16 = jnp.bfloat16
MESH = pl.DeviceIdType.MESH
ANY = pl.BlockSpec(memory_space=pl.ANY)
VMEM_SPEC = pl.BlockSpec(memory_space=pltpu.VMEM)
SMEM_SPEC = pl.BlockSpec(memory_space=pltpu.SMEM)

N_DEV = 8
CHUNK = 64
HEAD_DIM = 64
LANES = 128
H_A, KV_A, H_B = 8, 2, 8
BACK_A, BACK_B = 2, 8
REL_CLIP = 128
N_REL = 2 * REL_CLIP + 1
QA, KVA, QB = H_A * HEAD_DIM, KV_A * HEAD_DIM, H_B * HEAD_DIM
D_IN = QA + 2 * KVA + 3 * QB
N_MOD = 9
EPS = 1e-6
NEG_INF = -1e30
QG = 4
QROWS = QG * CHUNK
TPS_A, TPS_B = 4, 2
SKEW = 1024
ADAM_LR, ADAM_B1, ADAM_B2, ADAM_EPS, ADAM_WD, ADAM_STEP = 0.001, 0.9, 0.999, 1e-08, 0.01, 10
VMEM_LIMIT = 56 * 2 ** 20


def _pick(n, cands):
    for c in cands:
        if n % c == 0:
            return c
    return n


def _pieces(n, width=2 * LANES):
    return [(lo, min(lo + width, n)) for lo in range(0, n, width)]


def _params(sem=None):
    return pltpu.CompilerParams(dimension_semantics=sem, vmem_limit_bytes=VMEM_LIMIT)


def _dot_nt(a, b):
    return lax.dot_general(a, b, (((1,), (1,)), ((), ())), preferred_element_type=F32)


def _dot_tn(a, b):
    return lax.dot_general(a, b, (((0,), (0,)), ((), ())), preferred_element_type=F32)


def _dot(a, b):
    return jnp.dot(a, b, preferred_element_type=F32)


def _sigmoid(a):
    return 0.5 * (jnp.tanh(0.5 * a) + 1.0)


def _mesh_pos():
    return lax.axis_index("x"), lax.axis_index("y"), lax.axis_index("c")


def _peer(x, y, c, r):
    px = 1 - x if r & 4 else x
    py = 1 - y if r & 2 else y
    pc = 1 - c if r & 1 else c
    return px, py, pc


class _Carry:
    def __init__(self, ins, out_shapes, scratch, start, finish, aliases=()):
        self.ins, self.out_shapes, self.scratch = list(ins), list(out_shapes), list(scratch)
        self.start, self.finish, self.aliases = start, finish, list(aliases)


def _call(body, *, name, grid, in_specs, out_specs, out_shape, args, scratch=(), sem=None, carry=None):
    single = not isinstance(out_shape, (tuple, list))
    out_specs = (out_specs,) if single else tuple(out_specs)
    out_shape = (out_shape,) if single else tuple(out_shape)
    if carry is None:
        res = pl.pallas_call(body, name=name, grid=grid, in_specs=list(in_specs), out_specs=out_specs,
                             out_shape=out_shape, scratch_shapes=list(scratch), compiler_params=_params(sem))(*args)
        return res[0] if single else res
    n_in, n_out, n_s = len(in_specs), len(out_shape), len(scratch)
    ci, co = len(carry.ins), len(carry.out_shapes)

    def wrapped(*refs):
        ins, cins = refs[:n_in], refs[n_in:n_in + ci]
        outs = refs[n_in + ci:n_in + ci + n_out]
        couts = refs[n_in + ci + n_out:n_in + ci + n_out + co]
        scr = refs[n_in + ci + n_out + co:n_in + ci + n_out + co + n_s]
        cscr = refs[n_in + ci + n_out + co + n_s:]
        first, last = None, None
        for ax, n in enumerate(grid):
            f, l = pl.program_id(ax) == 0, pl.program_id(ax) == n - 1
            first = f if first is None else first & f
            last = l if last is None else last & l
        pl.when(first)(lambda: carry.start(cins, couts, cscr))
        body(*ins, *outs, *scr)
        pl.when(last)(lambda: carry.finish(cins, couts, cscr))

    res = pl.pallas_call(
        wrapped, name=name, grid=grid, in_specs=list(in_specs) + [ANY] * ci, out_specs=out_specs + (ANY,) * co,
        out_shape=out_shape + tuple(carry.out_shapes), scratch_shapes=list(scratch) + carry.scratch,
        input_output_aliases={n_in + i: n_out + o for i, o in carry.aliases},
        compiler_params=_params(("arbitrary",) * len(grid)))(*args, *carry.ins)
    main = res[:n_out]
    return (main[0] if single else main), res[n_out:]


def _gather_carry(full=(), new=(), cont=()):
    full, new, cont = list(full), list(new), list(cont)
    n_full, n_one = len(full), len(full) + len(new)
    n_w = n_one + len(cont)
    rows = [s.shape[0] for s in full + new] + [b.shape[0] // N_DEV for b in cont]
    shapes = [jax.ShapeDtypeStruct((N_DEV * s.shape[0], s.shape[1]), s.dtype) for s in full + new]
    shapes += [jax.ShapeDtypeStruct(b.shape, b.dtype) for b in cont]

    def plan(ins, outs, scr):
        send_sems, recv_sems, local_sems = scr
        x, y, c = _mesh_pos()
        me, sibling = (x, y, c), (x, y, 1 - c)
        x_chip, y_chip, far_chip = (1 - x, y), (x, 1 - y), (1 - x, 1 - y)

        def block(buf, w, chip, core, half=None):
            n = rows[w] if half is None else rows[w] // 2
            start = (4 * chip[0] + 2 * chip[1] + core) * rows[w] + (0 if half is None else half * n)
            return buf.at[pl.ds(pl.multiple_of(start, 16), n), :]

        def copy(w, k, chip, core, to, half=None, src=None):
            return pltpu.make_async_remote_copy(
                src_ref=block(outs[w], w, chip, core, half) if src is None else src,
                dst_ref=block(outs[w], w, chip, core, half),
                send_sem=send_sems.at[w, k], recv_sem=recv_sems.at[w, k], device_id=to, device_id_type=MESH)

        def stage_one(w):
            return [copy(w, 1, (x, y), c, (*x_chip, c), src=ins[w]), copy(w, 2, (x, y), c, (*y_chip, c), src=ins[w]),
                    copy(w, 0, (x, y), c, sibling, src=ins[w])]

        def stage_two(w, held):
            return [copy(w, 5, x_chip, c, (*y_chip, c), half=0, src=block(held, w, x_chip, c, 0)),
                    copy(w, 6, y_chip, c, (*x_chip, c), half=1, src=block(held, w, y_chip, c, 1)),
                    copy(w, 3, x_chip, c, sibling, src=block(held, w, x_chip, c)),
                    copy(w, 4, y_chip, c, sibling, src=block(held, w, y_chip, c))]

        mine = [pltpu.make_async_copy(ins[w], block(outs[w], w, (x, y), c), local_sems.at[w]) for w in range(n_one)]
        return c, me, sibling, x_chip, y_chip, far_chip, copy, stage_one, stage_two, mine

    def start(ins, outs, scr):
        _, _, _, _, _, _, _, stage_one, stage_two, mine = plan(ins, outs, scr)
        for w in range(n_one):
            for cp in stage_one(w):
                cp.start()
        for w in range(n_one, n_w):
            for cp in stage_two(w, ins[w]):
                cp.start()
        for cp in mine:
            cp.start()

    def finish(ins, outs, scr):
        c, me, sibling, x_chip, y_chip, far_chip, copy, stage_one, stage_two, mine = plan(ins, outs, scr)
        sent = []

        def land_one(w):
            copy(w, 1, x_chip, c, me).wait_recv()
            copy(w, 2, y_chip, c, me).wait_recv()

        def land_two(w):
            copy(w, 5, far_chip, c, me, half=0).wait_recv()
            copy(w, 6, far_chip, c, me, half=1).wait_recv()
            cp = copy(w, 7, far_chip, c, sibling)
            cp.start()
            sent.append(cp)

        for w in range(n_full):
            land_one(w)
            for cp in stage_two(w, outs[w]):
                cp.start()
                sent.append(cp)
        for w in range(n_one, n_w):
            land_two(w)
            sent.extend(stage_two(w, ins[w]))
        for w in range(n_full, n_one):
            land_one(w)
        for w in range(n_full):
            land_two(w)
        for w in range(n_one):
            copy(w, 0, me[:2], 1 - c, me).wait_recv()
            sent.extend(stage_one(w))
        for w in list(range(n_full)) + list(range(n_one, n_w)):
            for k, chip in ((3, x_chip), (4, y_chip), (7, far_chip)):
                copy(w, k, chip, 1 - c, me).wait_recv()
        for cp in sent:
            cp.wait_send()
        for cp in mine:
            cp.wait()

    return _Carry(
        full + new + cont, shapes,
        [pltpu.SemaphoreType.DMA((n_w, N_DEV)), pltpu.SemaphoreType.DMA((n_w, N_DEV)),
         pltpu.SemaphoreType.DMA((max(n_one, 1),))], start, finish,
        aliases=[(w, w) for w in range(n_one, n_w)])


HBM_SPEC = pl.BlockSpec(memory_space=pltpu.HBM)
SEM_SPEC = pl.BlockSpec(memory_space=pltpu.SEMAPHORE)
N_CHIP = N_DEV // 2


def _scatter_copy(part_ref, land_ref, send_sem, recv_sem, r, rows):
    x, y, c = _mesh_pos()
    px, py, _ = _peer(x, y, c, 2 * r)
    src = part_ref.at[pl.ds(pl.multiple_of((2 * px + py) * rows, 16), rows), :]
    return pltpu.make_async_remote_copy(
        src_ref=src, dst_ref=land_ref.at[r - 1], send_sem=send_sem, recv_sem=recv_sem,
        device_id=(px, py, c), device_id_type=MESH)


def _scatter_order(n_w):
    return [(w, r) for r in (3, 2, 1) for w in range(n_w)]


def _scatter_start(parts, name):
    n_w = len(parts)
    rows = [p.shape[0] // N_CHIP for p in parts]
    order = _scatter_order(n_w)
    lands = [pltpu.with_memory_space_constraint(lax.empty((N_CHIP - 1, r, p.shape[1]), p.dtype), pltpu.HBM)
             for r, p in zip(rows, parts)]

    def body(*refs):
        part_refs, land_refs = refs[:n_w], refs[n_w:2 * n_w]
        sems = refs[2 * n_w:2 * n_w + 2 * len(order)]
        token = refs[-1]
        for j, (w, r) in enumerate(order):
            _scatter_copy(part_refs[w], land_refs[w], sems[2 * j], sems[2 * j + 1], r, rows[w]).start()
        token[...] = jnp.zeros_like(token)

    n_sem = 2 * len(order)
    res = pl.pallas_call(
        body, name=name,
        out_shape=(*[pltpu.SemaphoreType.DMA(())] * n_sem, *[pltpu.HBM(p.shape, p.dtype) for p in parts],
                   *[pltpu.HBM(l.shape, l.dtype) for l in lands], jax.ShapeDtypeStruct((8, LANES), F32)),
        in_specs=[HBM_SPEC] * (2 * n_w), out_specs=(*[SEM_SPEC] * n_sem, *[HBM_SPEC] * (2 * n_w), VMEM_SPEC),
        input_output_aliases={i: n_sem + i for i in range(2 * n_w)},
        compiler_params=pltpu.CompilerParams(has_side_effects=pltpu.SideEffectType.DATAFLOW_SIDE_EFFECTING),
    )(*[pltpu.with_memory_space_constraint(p, pltpu.HBM) for p in parts], *lands)
    return (list(res[:n_sem]), list(res[n_sem:n_sem + n_w]), list(res[n_sem + n_w:n_sem + 2 * n_w]), res[-1])


def _scatter_wait(sems, parts, lands, after, name):
    n_w = len(parts)
    rows = [p.shape[0] // N_CHIP for p in parts]
    order = _scatter_order(n_w)

    def body(*refs):
        part_refs, land_refs = refs[:n_w], refs[n_w:2 * n_w]
        sem_refs = refs[2 * n_w:2 * n_w + 2 * len(order)]
        for j, (w, r) in enumerate(order):
            cp = _scatter_copy(part_refs[w], land_refs[w], sem_refs[2 * j], sem_refs[2 * j + 1], r, rows[w])
            cp.wait_send()
            cp.wait_recv()

    res = pl.pallas_call(
        body, name=name,
        out_shape=(*[pltpu.HBM(p.shape, p.dtype) for p in parts], *[pltpu.HBM(l.shape, l.dtype) for l in lands]),
        in_specs=[HBM_SPEC] * (2 * n_w) + [SEM_SPEC] * len(sems) + [ANY],
        out_specs=tuple([HBM_SPEC] * (2 * n_w)),
        input_output_aliases={i: i for i in range(2 * n_w)},
        compiler_params=pltpu.CompilerParams(has_side_effects=pltpu.SideEffectType.DATAFLOW_SIDE_EFFECTING),
    )(*parts, *lands, *sems, after)
    return list(res[:n_w]), list(res[n_w:])


def _small_copy(v_ref, land_ref, send_sem, recv_sem, r):
    x, y, c = _mesh_pos()
    px, py, pc = _peer(x, y, c, r)
    return pltpu.make_async_remote_copy(
        src_ref=v_ref, dst_ref=land_ref.at[4 * x + 2 * y + c], send_sem=send_sem, recv_sem=recv_sem,
        device_id=(px, py, pc), device_id_type=MESH)


def _small_gather_start(v):
    land = pltpu.with_memory_space_constraint(lax.empty((N_DEV,) + v.shape, v.dtype), pltpu.HBM)

    def body(v_ref, land_ref, *rest):
        sems, token = rest[:2 * (N_DEV - 1)], rest[-1]
        for r in range(1, N_DEV):
            _small_copy(v_ref, land_ref, sems[2 * r - 2], sems[2 * r - 1], r).start()
        token[...] = jnp.zeros_like(token)

    n_sem = 2 * (N_DEV - 1)
    res = pl.pallas_call(
        body, name="small_gather_start",
        out_shape=(*[pltpu.SemaphoreType.DMA(())] * n_sem, pltpu.HBM(v.shape, v.dtype),
                   pltpu.HBM(land.shape, land.dtype), jax.ShapeDtypeStruct((8, LANES), F32)),
        in_specs=[HBM_SPEC, HBM_SPEC], out_specs=(*[SEM_SPEC] * n_sem, HBM_SPEC, HBM_SPEC, VMEM_SPEC),
        input_output_aliases={0: n_sem, 1: n_sem + 1},
        compiler_params=pltpu.CompilerParams(has_side_effects=pltpu.SideEffectType.DATAFLOW_SIDE_EFFECTING),
    )(pltpu.with_memory_space_constraint(v, pltpu.HBM), land)
    return list(res[:n_sem]), res[n_sem], res[n_sem + 1], res[-1]


def _small_gather_wait(sems, v, land, after):
    def body(v_ref, land_ref, *rest):
        for r in range(1, N_DEV):
            cp = _small_copy(v_ref, land_ref, rest[2 * r - 2], rest[2 * r - 1], r)
            cp.wait_send()
            x, y, c = _mesh_pos()
            px, py, pc = _peer(x, y, c, r)
            pltpu.make_async_remote_copy(
                src_ref=v_ref, dst_ref=land_ref.at[4 * px + 2 * py + pc], send_sem=rest[2 * r - 2],
                recv_sem=rest[2 * r - 1], device_id=(px, py, pc), device_id_type=MESH).wait_recv()

    res = pl.pallas_call(
        body, name="small_gather_wait",
        out_shape=(pltpu.HBM(v.shape, v.dtype), pltpu.HBM(land.shape, land.dtype)),
        in_specs=[HBM_SPEC, HBM_SPEC] + [SEM_SPEC] * len(sems) + [ANY], out_specs=(HBM_SPEC, HBM_SPEC),
        input_output_aliases={0: 0, 1: 1},
        compiler_params=pltpu.CompilerParams(has_side_effects=pltpu.SideEffectType.DATAFLOW_SIDE_EFFECTING),
    )(v, land, *sems, after)
    return res[0], res[1]


def _ada_forward(c_row, w_ada, b_cols, carry):
    d = c_row.shape[1]
    wcols = w_ada.shape[1]
    ci, co = len(carry.ins), len(carry.out_shapes)

    def body(*refs):
        c_ref, w_ref, b_ref = refs[:3]
        cins = refs[3:3 + ci]
        sc_ref, mod_ref = refs[3 + ci:5 + ci]
        couts = refs[5 + ci:5 + ci + co]
        rows_ref, send_sems, recv_sems = refs[5 + ci + co:8 + ci + co]
        cscr = refs[8 + ci + co:]
        carry.start(cins, couts, cscr)
        x, y, c = _mesh_pos()
        me = 4 * x + 2 * y + c
        cv = c_ref[...]
        sc_ref[me] = cv * _sigmoid(cv)

        sends = []
        for r in range(1, N_DEV):
            px, py, pc = _peer(x, y, c, r)
            cp = pltpu.make_async_remote_copy(
                src_ref=sc_ref.at[me], dst_ref=sc_ref.at[me], send_sem=send_sems.at[0, r - 1],
                recv_sem=recv_sems.at[0, r - 1], device_id=(px, py, pc), device_id_type=MESH)
            cp.start()
            sends.append(cp)
        for r in range(1, N_DEV):
            px, py, pc = _peer(x, y, c, r)
            pid = 4 * px + 2 * py + pc
            pltpu.make_async_remote_copy(
                src_ref=sc_ref.at[pid], dst_ref=sc_ref.at[pid], send_sem=send_sems.at[0, r - 1],
                recv_sem=recv_sems.at[0, r - 1], device_id=(px, py, pc), device_id_type=MESH).wait_recv()
        for cp in sends:
            cp.wait_send()

        sc_all = jnp.concatenate([sc_ref[j] for j in range(N_DEV)], axis=0)
        rows = _dot(sc_all.astype(BF16), w_ref[...].astype(BF16)) + b_ref[...]
        for j in range(N_DEV):
            rows_ref[j] = rows[j:j + 1, :]
        mod_ref[me] = rows_ref[me]

        sends = []
        for r in range(1, N_DEV):
            px, py, pc = _peer(x, y, c, r)
            pid = 4 * px + 2 * py + pc
            cp = pltpu.make_async_remote_copy(
                src_ref=rows_ref.at[pid], dst_ref=mod_ref.at[me], send_sem=send_sems.at[1, r - 1],
                recv_sem=recv_sems.at[1, r - 1], device_id=(px, py, pc), device_id_type=MESH)
            cp.start()
            sends.append(cp)
        for r in range(1, N_DEV):
            px, py, pc = _peer(x, y, c, r)
            pid = 4 * px + 2 * py + pc
            pltpu.make_async_remote_copy(
                src_ref=rows_ref.at[pid], dst_ref=mod_ref.at[pid], send_sem=send_sems.at[1, r - 1],
                recv_sem=recv_sems.at[1, r - 1], device_id=(px, py, pc), device_id_type=MESH).wait_recv()
        for cp in sends:
            cp.wait_send()
        carry.finish(cins, couts, cscr)

    res = pl.pallas_call(
        body, name="ada_forward",
        out_shape=(jax.ShapeDtypeStruct((N_DEV, 1, d), F32), jax.ShapeDtypeStruct((N_DEV, 1, wcols), F32),
                   *carry.out_shapes),
        in_specs=[VMEM_SPEC, VMEM_SPEC, VMEM_SPEC] + [ANY] * ci, out_specs=(VMEM_SPEC, VMEM_SPEC) + (ANY,) * co,
        scratch_shapes=[pltpu.VMEM((N_DEV, 1, wcols), F32), pltpu.SemaphoreType.DMA((2, N_DEV - 1)),
                        pltpu.SemaphoreType.DMA((2, N_DEV - 1))] + carry.scratch,
        compiler_params=_params(),
    )(c_row, w_ada, b_cols, *carry.ins)
    return res[:2], res[2:]


def _mm_nt(a, b, name, out_dtype, bias=None, carry=None):
    m, k = a.shape
    n = b.shape[0]
    tm = _pick(m, (512, 256, 128))
    tn = _pick(n, (1408, 1152, 1024, 768, 512, 256, 128))

    def body(*refs):
        acc = _dot_nt(refs[0][...], refs[1][...])
        if bias is not None:
            acc = acc + refs[2][...]
        refs[-1][...] = acc.astype(out_dtype)

    in_specs = [pl.BlockSpec((tm, k), lambda j, i: (i, 0)), pl.BlockSpec((tn, k), lambda j, i: (j, 0))]
    args = [a, b]
    if bias is not None:
        in_specs.append(pl.BlockSpec((1, tn), lambda j, i: (0, j)))
        args.append(bias)
    return _call(body, name=name, grid=(n // tn, m // tm), in_specs=in_specs,
                 out_specs=pl.BlockSpec((tm, tn), lambda j, i: (i, j)),
                 out_shape=jax.ShapeDtypeStruct((m, n), out_dtype), args=args,
                 sem=("parallel", "parallel"), carry=carry)


class _Tail:
    def __init__(self, rows, vecs, outs, fn):
        self.rows, self.vecs, self.outs, self.fn = list(rows), list(vecs), list(outs), fn


def _mm_nn(pairs, name, out_dtype, bias=None, carry=None, tail=None):
    m, k = pairs[0][0].shape
    n = pairs[0][1].shape[1]
    n_p = len(pairs)
    tm = _pick(m, (512, 256, 128))
    tk = k if n_p == 1 else _pick(k, (1408, 1152, 1024, 768, 512, 256, 128))
    nk = k // tk
    n_b = 0 if bias is None else 1
    n_r, n_v = (len(tail.rows), len(tail.vecs)) if tail else (0, 0)
    n_in = 2 * n_p + n_b + n_r + n_v
    n_main = 0 if out_dtype is None else 1

    def finish(acc, refs, first_tile):
        if bias is not None:
            acc = acc + refs[2 * n_p][...]
        outs = refs[n_in:-1]
        if n_main:
            outs[0][...] = acc.astype(out_dtype)
        if tail is None:
            return
        rows = [r[...] for r in refs[2 * n_p + n_b:2 * n_p + n_b + n_r]]
        vecs = [v[...] for v in refs[2 * n_p + n_b + n_r:n_in]]
        vals = tail.fn(acc, rows, vecs)
        for ref, val, (dtype, kind) in zip(outs[n_main:], vals, tail.outs):
            if kind == "row":
                ref[...] = val.astype(dtype)
            else:
                @pl.when(first_tile)
                def _(ref=ref):
                    ref[...] = jnp.zeros_like(ref)

                ref[...] += val

    def body(*refs):
        acc_ref = refs[-1]
        kk, i = pl.program_id(0), pl.program_id(1)
        part = _dot(refs[0][...], refs[1][...])
        for p in range(1, n_p):
            part = part + _dot(refs[2 * p][...], refs[2 * p + 1][...])
        if nk == 1:
            finish(part, refs, i == 0)
            return
        rows = pl.ds(pl.multiple_of(i * tm, tm), tm)

        @pl.when(kk == 0)
        def _():
            acc_ref[rows, :] = part

        if nk > 2:
            @pl.when((kk > 0) & (kk < nk - 1))
            def _():
                acc_ref[rows, :] += part

        @pl.when(kk == nk - 1)
        def _():
            finish(acc_ref[rows, :] + part, refs, i == 0)

    def last_only(kk, i):
        return (jnp.where(kk == nk - 1, i, 0), 0)

    row_spec = pl.BlockSpec((tm, n), last_only)
    vec_spec = pl.BlockSpec((1, n), lambda kk, i: (0, 0))
    in_specs, args = [], []
    for a, b in pairs:
        in_specs += [pl.BlockSpec((tm, tk), lambda kk, i: (i, kk)), pl.BlockSpec((tk, n), lambda kk, i: (kk, 0))]
        args += [a, b]
    if bias is not None:
        in_specs.append(vec_spec)
        args.append(bias)
    out_specs = [row_spec] * n_main
    out_shape = [jax.ShapeDtypeStruct((m, n), out_dtype)] if n_main else []
    if tail:
        in_specs += [row_spec] * n_r + [vec_spec] * n_v
        args += tail.rows + tail.vecs
        for dtype, kind in tail.outs:
            if kind == "row":
                out_specs.append(row_spec)
                out_shape.append(jax.ShapeDtypeStruct((m, n), dtype))
            else:
                width = n if kind == "sum" else 1
                out_specs.append(pl.BlockSpec((1, width), lambda kk, i: (0, 0)))
                out_shape.append(jax.ShapeDtypeStruct((1, width), dtype))
    if tail is None:
        out_specs, out_shape = out_specs[0], out_shape[0]
    return _call(body, name=name, grid=(nk, m // tm), in_specs=in_specs, out_specs=out_specs,
                 out_shape=out_shape, args=args,
                 scratch=[pltpu.VMEM((m, n) if nk > 1 else (8, LANES), F32)],
                 sem=("arbitrary", "arbitrary"), carry=carry)


def _rms(v):
    return lax.rsqrt(jnp.mean(v * v, axis=-1, keepdims=True) + EPS)


def _col(v):
    return jnp.sum(v, axis=0, keepdims=True)


def _tail_post_pre(x, g_post, gate, weight, g_pre, scale, shift):
    def fn(y, rows, vecs):
        (xv,), (gp, gt, g, sc, sh) = rows, vecs
        xo = xv + (weight * gt) * ((y * _rms(y)) * gp)
        return xo, ((xo * _rms(xo)) * g) * (1.0 + sc) + sh

    return _Tail([x], [g_post, gate, g_pre, scale, shift], [(F32, "row"), (BF16, "row")], fn)


def _tail_post_loss(x, target, g, gate, weight):
    def fn(y, rows, vecs):
        (xv, tv), (gv, gt) = rows, vecs
        r = _rms(y)
        yn = y * r
        err = (xv + (weight * gt) * (yn * gv)) - tv
        do = err * (1.0 / y.shape[1])
        dyn = do * ((weight * gt) * gv)
        dy = r * (dyn - yn * jnp.mean(dyn * yn, axis=-1, keepdims=True))
        return do, dy, 0.5 * _col(jnp.mean(err * err, axis=-1, keepdims=True)), _col(do * yn)

    return _Tail([x, target], [g, gate], [(F32, "row"), (BF16, "row"), (F32, "one"), (F32, "sum")], fn)


def _tail_pre_bwd(x, dres, g_pre, scale):
    def fn(dh, rows, vecs):
        (xv, dr), (g, sc) = rows, vecs
        r = _rms(xv)
        n = xv * r
        dn = dh * (g * (1.0 + sc))
        return dr + r * (dn - n * jnp.mean(dn * n, axis=-1, keepdims=True)), _col(dh * n), _col(dh)

    return _Tail([x, dres], [g_pre, scale], [(F32, "row"), (F32, "sum"), (F32, "sum")], fn)


def _tail_pre_post_bwd(x, dres, y, g_pre, scale, g_post, gate, weight):
    def fn(dh, rows, vecs):
        (xv, dr, yv), (g, sc, gp, gt) = rows, vecs
        r = _rms(xv)
        n = xv * r
        dn = dh * (g * (1.0 + sc))
        dx = dr + r * (dn - n * jnp.mean(dn * n, axis=-1, keepdims=True))
        ry = _rms(yv)
        yn = yv * ry
        dyn = dx * ((weight * gt) * gp)
        dy = ry * (dyn - yn * jnp.mean(dyn * yn, axis=-1, keepdims=True))
        return dx, dy, _col(dh * n), _col(dh), _col(dx * yn), _col(dy)

    return _Tail([x, dres, y], [g_pre, scale, g_post, gate],
                 [(F32, "row"), (BF16, "row")] + [(F32, "sum")] * 4, fn)


def _mm_tn_pair(a, b, name, col_sums=False):
    k, m = a.shape
    n = b.shape[1]
    rows = m // N_DEV
    n_chip = N_DEV // 2
    tm = 4 * rows
    tk = _pick(k, (1024, 512, 256, 128))
    nk = k // tk

    def body(a_ref, b_ref, p_ref, own_ref, *rest):
        acc_ref, keep_ref, send_ref, land_ref, send_sems, recv_sems = rest[-6:]
        i, kk = pl.program_id(0), pl.program_id(1)
        x, y, c = _mesh_pos()
        if col_sums:
            cs_ref = rest[0]
            part = jnp.sum(a_ref[...].astype(F32), axis=0, keepdims=True)

            @pl.when(kk == 0)
            def _():
                cs_ref[...] = part

            @pl.when(kk > 0)
            def _():
                cs_ref[...] += part

        def push(chip):
            return pltpu.make_async_remote_copy(
                src_ref=send_ref.at[chip], dst_ref=land_ref.at[chip], send_sem=send_sems.at[chip],
                recv_sem=recv_sems.at[chip], device_id=(x, y, 1 - c), device_id_type=MESH)

        if nk == 1:
            acc = _dot_tn(a_ref[...], b_ref[...])
        else:
            @pl.when(kk == 0)
            def _():
                acc_ref[...] = jnp.zeros_like(acc_ref)

            acc_ref[...] += _dot_tn(a_ref[...], b_ref[...])
            acc = acc_ref

        for t in range(2):
            @pl.when((kk == nk - 1) & (i == t))
            def _(t=t):
                for ob in range(4):
                    chip, core = 2 * t + ob // 2, ob % 2
                    blk = acc[ob * rows:(ob + 1) * rows, :]

                    @pl.when(c == core)
                    def _(chip=chip, blk=blk):
                        keep_ref[chip] = blk

                    @pl.when(c != core)
                    def _(chip=chip, blk=blk):
                        send_ref[chip] = blk.astype(BF16)
                        push(chip).start()

        @pl.when((kk == nk - 1) & (i == 1))
        def _():
            for chip in range(n_chip):
                push(chip).wait_recv()
                val = (keep_ref[chip] + land_ref[chip].astype(F32)).astype(BF16)
                p_ref[chip * rows:(chip + 1) * rows, :] = val

                @pl.when(2 * x + y == chip)
                def _(val=val):
                    own_ref[...] = val

            for chip in range(n_chip):
                push(chip).wait_send()

    out_specs = [pl.BlockSpec((n_chip * rows, n), lambda i, kk: (0, 0)), pl.BlockSpec((rows, n), lambda i, kk: (0, 0))]
    out_shape = [jax.ShapeDtypeStruct((n_chip * rows, n), BF16), jax.ShapeDtypeStruct((rows, n), BF16)]
    if col_sums:
        out_specs.append(pl.BlockSpec((1, tm), lambda i, kk: (0, i)))
        out_shape.append(jax.ShapeDtypeStruct((1, m), F32))
    return _call(body, name=name, grid=(2, nk),
                 in_specs=[pl.BlockSpec((tk, tm), lambda i, kk: (kk, i)), pl.BlockSpec((tk, n), lambda i, kk: (kk, 0))],
                 out_specs=out_specs, out_shape=out_shape, args=[a, b],
                 scratch=[pltpu.VMEM((tm, n) if nk > 1 else (8, LANES), F32), pltpu.VMEM((n_chip, rows, n), F32),
                          pltpu.VMEM((n_chip, rows, n), BF16), pltpu.VMEM((n_chip, rows, n), BF16),
                          pltpu.SemaphoreType.DMA((n_chip,)), pltpu.SemaphoreType.DMA((n_chip,))],
                 sem=("arbitrary", "arbitrary"))


def _ffn_up(h, wg_t, wu_t, name, carry=None):
    s, d = h.shape
    f = wg_t.shape[0]
    tm = _pick(s, (512, 256, 128))
    tf = _pick(f, (1408, 1024, 512, 256, 128))

    def body(h_ref, wg_ref, wu_ref, a_ref, b_ref, u_ref):
        hh = h_ref[...]
        for lo, hi in _pieces(tf):
            a = _dot_nt(hh, wg_ref[lo:hi, :])
            b = _dot_nt(hh, wu_ref[lo:hi, :])
            a_ref[:, lo:hi] = a.astype(BF16)
            b_ref[:, lo:hi] = b.astype(BF16)
            u_ref[:, lo:hi] = ((a * _sigmoid(a)) * b).astype(BF16)

    w_spec = pl.BlockSpec((tf, d), lambda j, i: (j, 0))
    o_spec = pl.BlockSpec((tm, tf), lambda j, i: (i, j))
    o_shape = jax.ShapeDtypeStruct((s, f), BF16)
    return _call(body, name=name, grid=(f // tf, s // tm),
                 in_specs=[pl.BlockSpec((tm, d), lambda j, i: (i, 0)), w_spec, w_spec],
                 out_specs=(o_spec, o_spec, o_spec), out_shape=(o_shape, o_shape, o_shape),
                 args=[h, wg_t, wu_t], sem=("parallel", "parallel"), carry=carry)


def _ffn_down_bwd(dy, wd, a, b, name, carry=None):
    s, d = dy.shape
    f = wd.shape[0]
    tm = _pick(s, (512, 256, 128))
    tf = _pick(f, (1408, 1024, 512, 256, 128))

    def body(dy_ref, wd_ref, a_ref, b_ref, da_ref, db_ref):
        dyv = dy_ref[...]
        for lo, hi in _pieces(tf):
            du = _dot_nt(dyv, wd_ref[lo:hi, :])
            a = a_ref[:, lo:hi].astype(F32)
            b = b_ref[:, lo:hi].astype(F32)
            sig = _sigmoid(a)
            da_ref[:, lo:hi] = (du * b * (sig * (1.0 + a * (1.0 - sig)))).astype(BF16)
            db_ref[:, lo:hi] = (du * (a * sig)).astype(BF16)

    t_spec = pl.BlockSpec((tm, tf), lambda j, i: (i, j))
    o_shape = jax.ShapeDtypeStruct((s, f), BF16)
    return _call(body, name=name, grid=(f // tf, s // tm),
                 in_specs=[pl.BlockSpec((tm, d), lambda j, i: (i, 0)), pl.BlockSpec((tf, d), lambda j, i: (j, 0)),
                           t_spec, t_spec],
                 out_specs=(t_spec, t_spec), out_shape=(o_shape, o_shape), args=[dy, wd, a, b],
                 sem=("parallel", "parallel"), carry=carry)


def _row_tile(s):
    return _pick(s, (256, 128, 64))


def _vec_spec(d):
    return pl.BlockSpec((1, d), lambda i: (0, 0))


def _pre_norm(x, g, scale, shift, name):
    s, d = x.shape
    ts = _row_tile(s)

    def body(x_ref, g_ref, sc_ref, sh_ref, h_ref):
        xv = x_ref[...]
        r = lax.rsqrt(jnp.mean(xv * xv, axis=-1, keepdims=True) + EPS)
        h_ref[...] = (((xv * r) * g_ref[...]) * (1.0 + sc_ref[...]) + sh_ref[...]).astype(BF16)

    row = pl.BlockSpec((ts, d), lambda i: (i, 0))
    return _call(body, name=name, grid=(s // ts,), in_specs=[row, _vec_spec(d), _vec_spec(d), _vec_spec(d)],
                 out_specs=row, out_shape=jax.ShapeDtypeStruct((s, d), BF16), args=[x, g, scale, shift],
                 sem=("parallel",))


def _group_norm_cat(oa, ob, ga, gb):
    s = oa.shape[0]
    ts = _row_tile(s)

    def body(oa_ref, ob_ref, ga_ref, gb_ref, y_ref):
        for o_ref, g_ref, lo, w in ((oa_ref, ga_ref, 0, QA), (ob_ref, gb_ref, QA, QB)):
            ov = o_ref[...]
            r = lax.rsqrt(jnp.mean(ov * ov, axis=-1, keepdims=True) + EPS)
            y_ref[:, lo:lo + w] = ((ov * r) * g_ref[...]).astype(BF16)

    return _call(body, name="group_norm_cat", grid=(s // ts,),
                 in_specs=[pl.BlockSpec((ts, QA), lambda i: (i, 0)), pl.BlockSpec((ts, QB), lambda i: (i, 0)),
                           _vec_spec(QA), _vec_spec(QB)],
                 out_specs=pl.BlockSpec((ts, QA + QB), lambda i: (i, 0)),
                 out_shape=jax.ShapeDtypeStruct((s, QA + QB), BF16), args=[oa, ob, ga, gb], sem=("parallel",))


def _group_norm_bwd(dy, oa, ob, ga, gb):
    s = oa.shape[0]
    ts = _row_tile(s)

    def body(dy_ref, oa_ref, ob_ref, ga_ref, gb_ref, doa_ref, dob_ref, dga_ref, dgb_ref):
        @pl.when(pl.program_id(0) == 0)
        def _():
            dga_ref[...] = jnp.zeros_like(dga_ref)
            dgb_ref[...] = jnp.zeros_like(dgb_ref)

        for o_ref, g_ref, do_ref, dg_ref, lo, w in ((oa_ref, ga_ref, doa_ref, dga_ref, 0, QA),
                                                    (ob_ref, gb_ref, dob_ref, dgb_ref, QA, QB)):
            ov = o_ref[...]
            dyv = dy_ref[:, lo:lo + w]
            r = lax.rsqrt(jnp.mean(ov * ov, axis=-1, keepdims=True) + EPS)
            n = ov * r
            dn = dyv * g_ref[...]
            do_ref[...] = r * (dn - n * jnp.mean(dn * n, axis=-1, keepdims=True))
            dg_ref[...] += jnp.sum(dyv * n, axis=0, keepdims=True)

    ra = pl.BlockSpec((ts, QA), lambda i: (i, 0))
    rb = pl.BlockSpec((ts, QB), lambda i: (i, 0))
    return _call(body, name="group_norm_bwd", grid=(s // ts,),
                 in_specs=[pl.BlockSpec((ts, QA + QB), lambda i: (i, 0)), ra, rb, _vec_spec(QA), _vec_spec(QB)],
                 out_specs=(ra, rb, _vec_spec(QA), _vec_spec(QB)),
                 out_shape=(jax.ShapeDtypeStruct((s, QA), F32), jax.ShapeDtypeStruct((s, QB), F32),
                            jax.ShapeDtypeStruct((1, QA), F32), jax.ShapeDtypeStruct((1, QB), F32)),
                 args=[dy, oa, ob, ga, gb], sem=("arbitrary",))


def _n_variants(n_back):
    return -(-n_back // QG) + 1


def _alibi_bias():
    i = np.arange(QROWS)[:, None]
    j = np.arange((QG + BACK_A) * CHUNK)[None, :]
    dist = np.abs(BACK_A * CHUNK + i - j).astype(np.float32)
    dc = j // CHUNK - i // CHUNK
    valid = (dc >= 0) & (dc <= BACK_A)
    slopes = np.array([2.0 ** (-8.0 * (h + 1) / H_A) for h in range(H_A)], dtype=np.float32)
    bias = -slopes[:, None, None] * dist[None]
    out = [np.where((valid & (j >= (BACK_A - QG * v) * CHUNK))[None], bias, np.float32(NEG_INF))
           for v in range(_n_variants(BACK_A))]
    return jnp.asarray(np.stack(out).astype(np.float32))


def _rel_index_matrix():
    cc = np.arange(SKEW)
    dist = np.where(cc < SKEW - QROWS, BACK_B * CHUNK - cc, BACK_B * CHUNK + SKEW - cc)
    idx = np.clip(dist, -REL_CLIP, REL_CLIP) + REL_CLIP
    m = np.zeros((SKEW, N_REL), np.float32)
    m[cc, idx] = 1.0
    return jnp.asarray(m)


def _toeplitz_bias(vec, carry=None):
    lk = (QG + BACK_B) * CHUNK
    nv = _n_variants(BACK_B)

    def body(v_ref, o_ref):
        xv = jnp.broadcast_to(v_ref[0], (QROWS, SKEW))
        row = lax.broadcasted_iota(jnp.int32, (QROWS, SKEW), 0)
        for bit in range(QROWS.bit_length() - 1):
            xv = jnp.where((row >> bit) & 1 == 1, pltpu.roll(xv, 1 << bit, 1), xv)
        ri = lax.broadcasted_iota(jnp.int32, (QROWS, lk), 0) // CHUNK
        col = lax.broadcasted_iota(jnp.int32, (QROWS, lk), 1)
        ci = col // CHUNK
        valid = (ci - ri >= 0) & (ci - ri <= BACK_B)
        for v in range(nv):
            o_ref[v, 0] = jnp.where(valid & (col >= (BACK_B - QG * v) * CHUNK), xv[:, :lk], NEG_INF)

    return _call(body, name="toeplitz_bias", grid=(H_B,),
                 in_specs=[pl.BlockSpec((1, 1, SKEW), lambda h: (h, 0, 0))],
                 out_specs=pl.BlockSpec((nv, 1, QROWS, lk), lambda h: (0, h, 0, 0)),
                 out_shape=jax.ShapeDtypeStruct((nv, H_B, QROWS, lk), F32), args=[vec], sem=("parallel",),
                 carry=carry)


def _diagonal_sums(dbias):
    lk = dbias.shape[2]

    def body(d_ref, o_ref):
        xp = jnp.concatenate([d_ref[0], jnp.zeros((QROWS, SKEW - lk), F32)], axis=1)
        xv = xp[0:CHUNK]
        for q in range(1, QG):
            xv = xv + pltpu.roll(xp[q * CHUNK:(q + 1) * CHUNK], SKEW - q * CHUNK, 1)
        row = lax.broadcasted_iota(jnp.int32, (CHUNK, SKEW), 0)
        for bit in range(CHUNK.bit_length() - 1):
            xv = jnp.where((row >> bit) & 1 == 1, pltpu.roll(xv, SKEW - (1 << bit), 1), xv)
        o_ref[0] = jnp.sum(xv, axis=0, keepdims=True)

    return _call(body, name="diagonal_sums", grid=(H_B,),
                 in_specs=[pl.BlockSpec((1, QROWS, lk), lambda h: (h, 0, 0))],
                 out_specs=pl.BlockSpec((1, 1, SKEW), lambda h: (h, 0, 0)),
                 out_shape=jax.ShapeDtypeStruct((H_B, 1, SKEW), F32), args=[dbias], sem=("parallel",))


def _attn_common(s, n_back, gqa, q_col, k_col, v_col, TPS):
    lk = (QG + n_back) * CHUNK
    pad = n_back * CHUNK
    wide = TPS * LANES
    q_spec = pl.BlockSpec((QROWS, wide), lambda t, g: (g, q_col // TPS + t))
    if gqa:
        k_spec = pl.BlockSpec((s, LANES), lambda t, g: (0, k_col))
        v_spec = pl.BlockSpec((s, LANES), lambda t, g: (0, v_col))
    else:
        k_spec = pl.BlockSpec((s, wide), lambda t, g: (0, k_col // TPS + t))
        v_spec = pl.BlockSpec((s, wide), lambda t, g: (0, v_col // TPS + t))
    last_variant = _n_variants(n_back) - 1
    bias_spec = pl.BlockSpec((None, 2 * TPS, QROWS, lk), lambda t, g: (jnp.minimum(g, last_variant), t, 0, 0))
    tile_spec = pl.BlockSpec((QROWS, wide), lambda t, g: (g, t))
    return lk, pad, q_spec, k_spec, v_spec, bias_spec, tile_spec


def _attention_fwd(proj, bias, sinks, *, n_back, gqa, q_col, k_col, v_col, TPS, name, carry=None):
    s = proj.shape[0]
    lk, pad, q_spec, k_spec, v_spec, bias_spec, tile_spec = _attn_common(s, n_back, gqa, q_col, k_col, v_col, TPS)
    n_t, n_g = 512 // (TPS * LANES), s // QROWS
    kv_wide = LANES if gqa else TPS * LANES

    def body(*refs):
        if gqa:
            q_ref, k_ref, v_ref, bias_ref, sink_ref, o_ref, l_ref, kpad, vpad = refs
        else:
            q_ref, k_ref, v_ref, bias_ref, o_ref, l_ref, kpad, vpad = refs
        t, g = pl.program_id(0), pl.program_id(1)

        @pl.when(g == 0)
        def _():
            kpad[0:pad, :] = jnp.zeros((pad, kv_wide), BF16)
            vpad[0:pad, :] = jnp.zeros((pad, kv_wide), BF16)
            kpad[pad:, :] = k_ref[...]
            vpad[pad:, :] = v_ref[...]

        start = pl.multiple_of(g * QROWS, QROWS)
        half = lax.broadcasted_iota(jnp.int32, (QROWS, LANES), 1) // HEAD_DIM
        for tt in range(TPS):
            lanes = slice(tt * LANES, (tt + 1) * LANES)
            kv_lanes = slice(0, LANES) if gqa else lanes
            kb = kpad[pl.ds(start, lk), kv_lanes]
            vb = vpad[pl.ds(start, lk), kv_lanes]
            q = q_ref[:, lanes] * (HEAD_DIM ** -0.5)
            if gqa:
                hk = (TPS * t + tt) // 2
                q_rolled = pltpu.roll(q.astype(F32), HEAD_DIM, 1).astype(BF16)
            outs, lses = [], []
            for e in range(2):
                if gqa:
                    kv_half = hk
                    src = jnp.where(hk == e, q, q_rolled)
                else:
                    kv_half = e
                    src = q
                qm = jnp.where(half == kv_half, src, jnp.zeros_like(src))
                sc = _dot_nt(qm, kb) + bias_ref[2 * tt + e]
                m = jnp.max(sc, axis=-1, keepdims=True)
                if gqa:
                    sk = sink_ref[2 * (TPS * t + tt) + e]
                    m = jnp.maximum(m, sk)
                p = jnp.exp(sc - m)
                l = jnp.sum(p, axis=-1, keepdims=True)
                if gqa:
                    l = l + jnp.exp(sk - m)
                pn = p / l
                outs.append(_dot(pn.astype(BF16), vb))
                lses.append(m + jnp.log(l))
            if gqa:
                same = jnp.where(hk == 0, outs[0], outs[1])
                other = jnp.where(hk == 0, outs[1], outs[0])
                o_ref[:, lanes] = jnp.where(half == hk, same, pltpu.roll(other, HEAD_DIM, 1))
            else:
                o_ref[:, lanes] = jnp.where(half == 0, outs[0], outs[1])
            l_ref[:, lanes] = jnp.where(half == 0, lses[0], lses[1])

    in_specs = [q_spec, k_spec, v_spec, bias_spec] + ([SMEM_SPEC] if gqa else [])
    args = [proj, proj, proj, bias] + ([sinks] if gqa else [])
    o_shape = jax.ShapeDtypeStruct((s, 512), F32)
    return _call(body, name=name, grid=(n_t, n_g), in_specs=in_specs, out_specs=(tile_spec, tile_spec),
                 out_shape=(o_shape, o_shape), args=args,
                 scratch=[pltpu.VMEM((s + pad, kv_wide), BF16), pltpu.VMEM((s + pad, kv_wide), BF16)],
                 sem=("arbitrary", "arbitrary"), carry=carry)


def _attention_bwd(proj, bias, sinks, do, lse, *, n_back, gqa, q_col, k_col, v_col, TPS, name, carry=None):
    s = proj.shape[0]
    lk, pad, q_spec, k_spec, v_spec, bias_spec, tile_spec = _attn_common(s, n_back, gqa, q_col, k_col, v_col, TPS)
    n_t, n_g = 512 // (TPS * LANES), s // QROWS
    kv_wide = LANES if gqa else TPS * LANES

    def body(*refs):
        if gqa:
            (q_ref, k_ref, v_ref, bias_ref, sink_ref, do_ref, l_ref,
             dq_ref, dk_ref, dv_ref, dsink_ref, kpad, vpad, dkpad, dvpad) = refs
        else:
            (q_ref, k_ref, v_ref, bias_ref, do_ref, l_ref,
             dq_ref, dk_ref, dv_ref, dbias_ref, kpad, vpad, dkpad, dvpad) = refs
        t, g = pl.program_id(0), pl.program_id(1)

        @pl.when(g == 0)
        def _():
            kpad[0:pad, :] = jnp.zeros((pad, kv_wide), BF16)
            vpad[0:pad, :] = jnp.zeros((pad, kv_wide), BF16)
            kpad[pad:, :] = k_ref[...]
            vpad[pad:, :] = v_ref[...]
            if gqa:
                dsink_ref[...] = jnp.zeros_like(dsink_ref)
            else:
                dbias_ref[...] = jnp.zeros_like(dbias_ref)

        @pl.when((g == 0) & (t == 0) if gqa else g == 0)
        def _():
            dkpad[...] = jnp.zeros_like(dkpad)
            dvpad[...] = jnp.zeros_like(dvpad)

        start = pl.multiple_of(g * QROWS, QROWS)
        half = lax.broadcasted_iota(jnp.int32, (QROWS, LANES), 1) // HEAD_DIM
        for tt in range(TPS):
            lanes = slice(tt * LANES, (tt + 1) * LANES)
            kv_lanes = slice(0, LANES) if gqa else lanes
            kb = kpad[pl.ds(start, lk), kv_lanes]
            vb = vpad[pl.ds(start, lk), kv_lanes]
            q = q_ref[:, lanes]
            dov = do_ref[:, lanes]
            lv = l_ref[:, lanes]
            if gqa:
                hk = (TPS * t + tt) // 2
                q_rolled = pltpu.roll(q.astype(F32), HEAD_DIM, 1).astype(BF16)
                do_rolled = pltpu.roll(dov, HEAD_DIM, 1)
            dqs = []
            dk_acc = jnp.zeros((lk, LANES), F32)
            dv_acc = jnp.zeros((lk, LANES), F32)
            for e in range(2):
                if gqa:
                    kv_half = hk
                    src = jnp.where(hk == e, q, q_rolled)
                    do_src = jnp.where(hk == e, dov, do_rolled)
                else:
                    kv_half = e
                    src = q
                    do_src = dov
                qm = jnp.where(half == kv_half, src, jnp.zeros_like(src))
                dom = jnp.where(half == kv_half, do_src, 0.0).astype(BF16)
                lcol = jnp.max(jnp.where(half == e, lv, -jnp.inf), axis=-1, keepdims=True)
                sc = _dot_nt(qm * (HEAD_DIM ** -0.5), kb) + bias_ref[2 * tt + e]
                pn = jnp.exp(sc - lcol)
                dp = _dot_nt(dom, vb)
                delta = jnp.sum(pn * dp, axis=-1, keepdims=True)
                ds = pn * (dp - delta)
                if gqa:
                    p_sink = jnp.exp(sink_ref[2 * (TPS * t + tt) + e] - lcol)
                    dsk = -jnp.sum(p_sink * delta, axis=0, keepdims=True)
                    row = 2 * tt + e
                    dsink_ref[0, row:row + 1, :] += jnp.broadcast_to(dsk, (1, LANES))
                else:
                    dbias_ref[2 * tt + e] += ds
                dsb = (ds * (HEAD_DIM ** -0.5)).astype(BF16)
                dqs.append(_dot(dsb, kb))
                dk_acc = dk_acc + _dot_tn(dsb, qm)
                dv_acc = dv_acc + _dot_tn(pn.astype(BF16), dom)
            dkpad[pl.ds(start, lk), kv_lanes] += dk_acc
            dvpad[pl.ds(start, lk), kv_lanes] += dv_acc
            if gqa:
                same = jnp.where(hk == 0, dqs[0], dqs[1])
                other = jnp.where(hk == 0, dqs[1], dqs[0])
                dq_ref[:, lanes] = jnp.where(half == hk, same, pltpu.roll(other, HEAD_DIM, 1)).astype(BF16)
            else:
                dq_ref[:, lanes] = jnp.where(half == 0, dqs[0], dqs[1]).astype(BF16)

        @pl.when((g == n_g - 1) & (t == n_t - 1) if gqa else g == n_g - 1)
        def _():
            dk_ref[...] = dkpad[pad:, :].astype(BF16)
            dv_ref[...] = dvpad[pad:, :].astype(BF16)

    in_specs = [q_spec, k_spec, v_spec, bias_spec] + ([SMEM_SPEC] if gqa else []) + [tile_spec, tile_spec]
    args = [proj, proj, proj, bias] + ([sinks] if gqa else []) + [do, lse]
    if gqa:
        kv_out = pl.BlockSpec((s, LANES), lambda t, g: (0, 0))
        kv_shape = jax.ShapeDtypeStruct((s, LANES), BF16)
        extra_spec = pl.BlockSpec((1, 8, LANES), lambda t, g: (t, 0, 0))
        extra_shape = jax.ShapeDtypeStruct((n_t, 8, LANES), F32)
    else:
        kv_out = pl.BlockSpec((s, kv_wide), lambda t, g: (0, t))
        kv_shape = jax.ShapeDtypeStruct((s, 512), BF16)
        extra_spec = pl.BlockSpec((2 * TPS, QROWS, lk), lambda t, g: (t, 0, 0))
        extra_shape = jax.ShapeDtypeStruct(bias.shape[1:], F32)
    return _call(body, name=name, grid=(n_t, n_g), in_specs=in_specs,
                 out_specs=(tile_spec, kv_out, kv_out, extra_spec),
                 out_shape=(jax.ShapeDtypeStruct((s, 512), BF16), kv_shape, kv_shape, extra_shape), args=args,
                 scratch=[pltpu.VMEM((s + pad, kv_wide), BF16), pltpu.VMEM((s + pad, kv_wide), BF16),
                          pltpu.VMEM((s + pad, kv_wide), F32), pltpu.VMEM((s + pad, kv_wide), F32)],
                 sem=("arbitrary", "arbitrary"), carry=carry)


def _sum_rows8(g):
    n = g.shape[2]

    def body(g_ref, o_ref):
        acc = g_ref[0]
        for j in range(1, N_DEV):
            acc = acc + g_ref[j]
        o_ref[...] = acc

    return pl.pallas_call(
        body, name="sum_small_grads", in_specs=[VMEM_SPEC], out_specs=VMEM_SPEC,
        out_shape=jax.ShapeDtypeStruct((1, n), F32), compiler_params=_params(),
    )(g)


def _ada_weight_grad(sc_t, dmod_cols):
    d = sc_t.shape[0]
    w = dmod_cols.shape[1]
    td = _pick(d, (256, 128))

    def body(sc_ref, dm_ref, o_ref):
        scv = sc_ref[...]
        dmv = dm_ref[...]
        acc = scv[:, 0:1] * dmv[0:1, :]
        for b in range(1, N_DEV):
            acc = acc + scv[:, b:b + 1] * dmv[b:b + 1, :]
        o_ref[...] = acc

    return _call(body, name="ada_weight_grad", grid=(d // td,),
                 in_specs=[pl.BlockSpec((td, N_DEV), lambda i: (i, 0)), pl.BlockSpec((N_DEV, w), lambda i: (0, 0))],
                 out_specs=pl.BlockSpec((td, w), lambda i: (i, 0)), out_shape=jax.ShapeDtypeStruct((d, w), F32),
                 args=[sc_t, dmod_cols], sem=("parallel",))


def _adamw_update(w, gv, m, v):
    nm = ADAM_B1 * m + (1.0 - ADAM_B1) * gv
    nv = ADAM_B2 * v + (1.0 - ADAM_B2) * (gv * gv)
    m_hat = nm / (1.0 - ADAM_B1 ** ADAM_STEP)
    v_hat = nv / (1.0 - ADAM_B2 ** ADAM_STEP)
    return -ADAM_LR * (m_hat / (jnp.sqrt(v_hat) + ADAM_EPS) + ADAM_WD * w), nm, nv


def _adamw(w, g, m, v, name):
    rows, cols = w.shape
    tr = _pick(rows, (256, 176, 128, 88, 64)) if rows > 256 else rows

    def body(w_ref, g_ref, m_ref, v_ref, d_ref, nm_ref, nv_ref):
        d_ref[...], nm_ref[...], nv_ref[...] = _adamw_update(w_ref[...], g_ref[...], m_ref[...], v_ref[...])

    spec = pl.BlockSpec((tr, cols), lambda i: (i, 0))
    shape = jax.ShapeDtypeStruct((rows, cols), F32)
    return _call(body, name=name, grid=(rows // tr,), in_specs=[spec] * 4, out_specs=(spec, spec, spec),
                 out_shape=(shape, shape, shape), args=[w, g, m, v], sem=("parallel",))


def _adamw_from_slots(w, own, slots, m, v, name):
    n_slots, rows, k = slots.shape

    def body(o_ref, s_ref, w_ref, m_ref, v_ref, g_ref, d_ref, nm_ref, nv_ref):
        gv = o_ref[...].astype(F32)
        for j in range(n_slots):
            gv = gv + s_ref[j].astype(F32)
        g_ref[...] = gv
        d_ref[...], nm_ref[...], nv_ref[...] = _adamw_update(w_ref[...], gv, m_ref[...], v_ref[...])

    tr = rows // 2 if rows % 32 == 0 else rows
    spec = pl.BlockSpec((tr, k), lambda i: (i, 0))
    shape = jax.ShapeDtypeStruct((rows, k), F32)
    return _call(body, name=name, grid=(rows // tr,),
                 in_specs=[spec, pl.BlockSpec((n_slots, tr, k), lambda i: (0, i, 0)), spec, spec, spec],
                 out_specs=(spec, spec, spec, spec), out_shape=(shape, shape, shape, shape),
                 args=[own, slots, w, m, v], sem=("parallel",))


def _adamw_small(g, w, m, v, sizes):
    n = w.shape[1]
    offs, off = [], 0
    for size in sizes:
        offs.append(off)
        off += size + (-size % LANES)

    def body(g_ref, w_ref, m_ref, v_ref, *out_refs):
        gv = g_ref[:, 0:n]
        dv, nm, nv = _adamw_update(w_ref[...], gv, m_ref[...], v_ref[...])
        for j, (o, size) in enumerate(zip(offs, sizes)):
            for k, val in enumerate((gv, dv, nm, nv)):
                out_refs[4 * j + k][...] = val[:, o:o + size]

    shapes = [jax.ShapeDtypeStruct((1, size), F32) for size in sizes for _ in range(4)]
    return pl.pallas_call(
        body, name="adamw_small", in_specs=[VMEM_SPEC] * 4, out_specs=tuple([VMEM_SPEC] * len(shapes)),
        out_shape=tuple(shapes), compiler_params=_params(),
    )(g, w, m, v)


SMALL = ("b_ada", "g_pre_ffn1", "g_post_ffn1", "g_pre_mix", "b_in", "sinks_a", "rel_bias_b", "g_grp_a",
         "g_grp_b", "b_out", "g_post_mix", "g_pre_ffn2", "g_post_ffn2")
WEIGHTS = ("w_ada", "b_ada", "g_pre_ffn1", "w_gate1", "w_up1", "w_down1", "g_post_ffn1", "g_pre_mix", "w_in",
           "b_in", "sinks_a", "rel_bias_b", "g_grp_a", "g_grp_b", "w_out", "b_out", "g_post_mix", "g_pre_ffn2",
           "w_gate2", "w_up2", "w_down2", "g_post_ffn2")


def kernel(x, c, w_ada, b_ada, g_pre_ffn1, w_gate1, w_up1, w_down1, g_post_ffn1, g_pre_mix, w_in, b_in, sinks_a, rel_bias_b, g_grp_a, g_grp_b, w_out, b_out, g_post_mix, g_pre_ffn2, w_gate2, w_up2, w_down2, g_post_ffn2, loss_target, m_w_ada, m_b_ada, m_g_pre_ffn1, m_w_gate1, m_w_up1, m_w_down1, m_g_post_ffn1, m_g_pre_mix, m_w_in, m_b_in, m_sinks_a, m_rel_bias_b, m_g_grp_a, m_g_grp_b, m_w_out, m_b_out, m_g_post_mix, m_g_pre_ffn2, m_w_gate2, m_w_up2, m_w_down2, m_g_post_ffn2, v_w_ada, v_b_ada, v_g_pre_ffn1, v_w_gate1, v_w_up1, v_w_down1, v_g_post_ffn1, v_g_pre_mix, v_w_in, v_b_in, v_sinks_a, v_rel_bias_b, v_g_grp_a, v_g_grp_b, v_w_out, v_b_out, v_g_post_mix, v_g_pre_ffn2, v_w_gate2, v_w_up2, v_w_down2, v_g_post_ffn2):
    given = dict(locals())
    weights = {n: given[n] for n in WEIGHTS}
    mom_m = {n: given["m_" + n] for n in WEIGHTS}
    mom_v = {n: given["v_" + n] for n in WEIGHTS}

    me = 4 * lax.axis_index("x") + 2 * lax.axis_index("y") + lax.axis_index("c")
    xs = x[0]
    tgt = loss_target[0]
    d_model = xs.shape[1]
    ada_cols = w_ada.shape[2]

    sh = {"wg1": w_gate1[0].T, "wu1": w_up1[0].T, "wd1": w_down1[0], "win": w_in[0].T, "wo": w_out[0],
          "wg2": w_gate2[0].T, "wu2": w_up2[0].T, "wd2": w_down2[0]}
    sh = {k: v.astype(BF16) for k, v in sh.items()}

    def gather(full=(), new=(), cont=()):
        return _gather_carry([sh[n] for n in full], [sh[n] for n in new], cont)

    bias_a = _alibi_bias()
    rel_m = _rel_index_matrix()
    rel_vec = jnp.dot(rel_bias_b[0], rel_m.T, precision=lax.Precision.HIGHEST)
    bias_b, (wg1, wu1, wd1_part) = _toeplitz_bias(rel_vec.reshape(H_B, 1, SKEW),
                                                  carry=gather(full=("wg1", "wu1"), new=("wd1",)))

    b_cols = lax.dynamic_slice(b_ada, (0, me * ada_cols), (1, ada_cols))
    (sc_all, mod_rows), _ = _ada_forward(c, w_ada[0], b_cols, _Carry([], [], [], lambda *a: None, lambda *a: None))
    mod = mod_rows.reshape(N_MOD, d_model)
    shift1, scale1, gate1, shift2, scale2, gate2, shift3, scale3, gate3 = (mod[i:i + 1] for i in range(N_MOD))

    h1 = _pre_norm(xs, g_pre_ffn1, scale1, shift1, "pre_norm_ffn1")
    (a1, b1, u1), (win_part, wo_part, wd1) = _ffn_up(h1, wg1, wu1, "ffn_up_ffn1",
                                                     carry=gather(new=("win", "wo"), cont=(wd1_part,)))
    (y1, x1, h2), (wg2_part, win, wo) = _mm_nn(
        [(u1, wd1)], "ffn_down_ffn1", F32, carry=gather(new=("wg2",), cont=(win_part, wo_part)),
        tail=_tail_post_pre(xs, g_post_ffn1, gate1, 0.5, g_pre_mix, scale2, shift2))

    proj, (wg2,) = _mm_nt(h2, win, "in_proj", BF16, bias=b_in, carry=gather(cont=(wg2_part,)))
    sinks = sinks_a[0]
    cfg_a = dict(n_back=BACK_A, gqa=True, q_col=0, k_col=QA // LANES, v_col=(QA + KVA) // LANES, TPS=TPS_A)
    cfg_b = dict(n_back=BACK_B, gqa=False, q_col=(QA + 2 * KVA) // LANES, k_col=(QA + 2 * KVA + QB) // LANES,
                 v_col=(QA + 2 * KVA + 2 * QB) // LANES, TPS=TPS_B)
    (oa, lse_a), (wu2_part, wd2_part) = _attention_fwd(proj, bias_a, sinks, name="attn_a",
                                                       carry=gather(new=("wu2", "wd2")), **cfg_a)
    (ob, lse_b), (wu2, wd2) = _attention_fwd(proj, bias_b, None, name="attn_b",
                                             carry=gather(cont=(wu2_part, wd2_part)), **cfg_b)
    ycat = _group_norm_cat(oa, ob, g_grp_a, g_grp_b)
    ymix, x2, h3 = _mm_nn([(ycat, wo)], "out_proj", F32, bias=b_out,
                          tail=_tail_post_pre(x1, g_post_mix, gate2, 1.0, g_pre_ffn2, scale3, shift3))

    a3, b3, u3 = _ffn_up(h3, wg2, wu2, "ffn_up_ffn2")

    flights, own = {}, {}

    def grad_pair(key, a_mat, b_mat, name):
        part, own[key] = _mm_tn_pair(a_mat, b_mat, name)
        return part

    def scatter_start(tag, after_vec, **parts):
        names = list(parts)
        sems, p_thru, lands, token = _scatter_start([parts[n] for n in names], "scatter_start_" + tag)
        flights[tag] = (names, sems, p_thru, lands)
        return after_vec + token[0:1, 0:1]

    dx3, dy, loss_part, s1 = _mm_nn([(u3, wd2)], "ffn_down_ffn2", None,
                                    tail=_tail_post_loss(x2, tgt, g_post_ffn2, gate3, 0.5))
    da, db = _ffn_down_bwd(dy, wd2, a3, b3, "ffn_down_bwd_ffn2")
    dwd2 = grad_pair("wd2", u3, dy, "grad_wd_ffn2")
    dwg2 = grad_pair("wg2", da, h3, "grad_wg_ffn2")
    dwu2 = grad_pair("wu2", db, h3, "grad_wu_ffn2")
    g_pre_tied = scatter_start("ffn2", g_pre_ffn2, wd2=dwd2, wg2=dwg2, wu2=dwu2)
    dx2, dymix, s2, s3, s1m, db_out = _mm_nn(
        [(da, wg2), (db, wu2)], "ffn_up_bwd_ffn2", None,
        tail=_tail_pre_post_bwd(x2, dx3, ymix, g_pre_tied, scale3, g_post_mix, gate2, 1.0))
    sm3 = dict(shift=s3, scale=s2 * g_pre_ffn2, gate=0.5 * g_post_ffn2 * s1,
               g_pre=(1.0 + scale3) * s2, g_post=(0.5 * gate3) * s1)

    dycat = _mm_nt(dymix, wo, "out_proj_bwd", F32)
    dwo = grad_pair("wo", ycat, dymix, "grad_wo")
    doa, dob, dg_a, dg_b = _group_norm_bwd(dycat, oa, ob, g_grp_a, g_grp_b)
    dqa, dka, dva, dsink = _attention_bwd(proj, bias_a, sinks, doa, lse_a, name="attn_a_bwd", **cfg_a)
    dqb, dkb, dvb, dbias = _attention_bwd(proj, bias_b, None, dob, lse_b, name="attn_b_bwd", **cfg_b)
    dproj = jnp.concatenate([dqa, dka, dva, dqb, dkb, dvb], axis=1)
    dwin, own["win"], db_in = _mm_tn_pair(dproj, h2, "grad_win", col_sums=True)
    g_pre_tied = scatter_start("mix", g_pre_mix, wo=dwo, win=dwin)
    dx1, dy, s2m, s3m, s1, _ = _mm_nn(
        [(dproj, win)], "in_proj_bwd", None,
        tail=_tail_pre_post_bwd(x1, dx2, y1, g_pre_tied, scale2, g_post_ffn1, gate1, 0.5))
    d_rel = jnp.dot(_diagonal_sums(dbias).reshape(H_B, SKEW), rel_m, precision=lax.Precision.HIGHEST)
    d_sinks = dsink[:, :2 * TPS_A, 0].reshape(1, H_A)

    da, db = _ffn_down_bwd(dy, wd1, a1, b1, "ffn_down_bwd_ffn1")
    dwd1 = grad_pair("wd1", u1, dy, "grad_wd_ffn1")
    dwg1 = grad_pair("wg1", da, h1, "grad_wg_ffn1")
    dwu1 = grad_pair("wu1", db, h1, "grad_wu_ffn1")
    g_pre_tied = scatter_start("ffn1", g_pre_ffn1, wd1=dwd1, wg1=dwg1, wu1=dwu1)
    dx0, s2, s3 = _mm_nn([(da, wg1), (db, wu1)], "ffn_up_bwd_ffn1", None,
                         tail=_tail_pre_bwd(xs, dx1, g_pre_tied, scale1))
    sm1 = dict(shift=s3, scale=s2 * g_pre_ffn1, gate=0.5 * g_post_ffn1 * s1,
               g_pre=(1.0 + scale1) * s2, g_post=(0.5 * gate1) * s1)

    dmod = jnp.concatenate([sm1["shift"], sm1["scale"], sm1["gate"],
                            s3m, s2m * g_pre_mix, g_post_mix * s1m,
                            sm3["shift"], sm3["scale"], sm3["gate"]], axis=1)
    small_parts = {
        "b_ada": dmod, "g_pre_ffn1": sm1["g_pre"], "g_post_ffn1": sm1["g_post"],
        "g_pre_mix": (1.0 + scale2) * s2m, "b_in": db_in, "sinks_a": d_sinks,
        "rel_bias_b": d_rel.reshape(1, H_B * N_REL), "g_grp_a": dg_a, "g_grp_b": dg_b, "b_out": db_out,
        "g_post_mix": gate2 * s1m, "g_pre_ffn2": sm3["g_pre"], "g_post_ffn2": sm3["g_post"]}
    sizes = [small_parts[n].shape[1] for n in SMALL]

    def pack(parts):
        cells = []
        for p in parts:
            cells.append(p)
            if p.shape[1] % LANES:
                cells.append(jnp.zeros((1, -p.shape[1] % LANES), F32))
        return jnp.concatenate(cells, axis=1)

    packed = pack([small_parts[n] for n in SMALL] + [loss_part])
    n_packed = packed.shape[1]
    small_sems, packed_thru, small_land, small_token = _small_gather_start(packed)

    out_g, out_d, out_m, out_v = {}, {}, {}, {}
    groups = (("ffn2", (("w_gate2", "wg2", True), ("w_up2", "wu2", True), ("w_down2", "wd2", False))),
              ("mix", (("w_in", "win", True), ("w_out", "wo", False))),
              ("ffn1", (("w_gate1", "wg1", True), ("w_up1", "wu1", True), ("w_down1", "wd1", False))))
    after = small_token
    for tag, members in groups:
        names, sems, p_thru, lands = flights[tag]
        _, l_done = _scatter_wait(sems, p_thru, lands, after, "scatter_wait_" + tag)
        slots = dict(zip(names, l_done))
        for n, key, transposed in members:
            view = (lambda t: t.T) if transposed else (lambda t: t)
            res = _adamw_from_slots(view(weights[n][0]), own[key], slots[key], view(mom_m[n][0]),
                                    view(mom_v[n][0]), "adamw_" + n)
            out_g[n], out_d[n], out_m[n], out_v[n] = (view(t)[None] for t in res)
            after = res[3]

    packed_done, small_land = _small_gather_wait(small_sems, packed_thru, small_land, after)
    gathered = lax.dynamic_update_slice(small_land, packed_done[None], (me, 0, 0))
    small_sum = _sum_rows8(gathered)
    loss = small_sum[0, n_packed - LANES]
    dmod_cols = lax.dynamic_slice(gathered.reshape(N_DEV, n_packed), (0, me * ada_cols), (N_DEV, ada_cols))
    g_ada = _ada_weight_grad(sc_all.reshape(N_DEV, d_model).T, dmod_cols)
    d_, m_, v_ = _adamw(w_ada[0], g_ada, m_w_ada[0], v_w_ada[0], "adamw_w_ada")
    out_g["w_ada"], out_d["w_ada"], out_m["w_ada"], out_v["w_ada"] = g_ada[None], d_[None], m_[None], v_[None]

    small_out = _adamw_small(small_sum, *(pack([tree[n].reshape(1, -1) for n in SMALL])
                                          for tree in (weights, mom_m, mom_v)), sizes)
    for j, n in enumerate(SMALL):
        shape = weights[n].shape
        out_g[n], out_d[n], out_m[n], out_v[n] = (t.reshape(shape) for t in small_out[4 * j:4 * j + 4])

    return (loss, dx0[None], *[out_g[n] for n in WEIGHTS], *[out_d[n] for n in WEIGHTS],
            *[out_m[n] for n in WEIGHTS], *[out_v[n] for n in WEIGHTS])
```

```python
import numpy as np
import jax
import jax.numpy as jnp
from jax import lax
from jax.experimental import pallas as pl
from jax.experimental.pallas import tpu as pltpu

F32 = jnp.float32
BF16 = jnp.bfloat16
MESH = pl.DeviceIdType.MESH
ANY = pl.BlockSpec(memory_space=pl.ANY)
VMEM_SPEC = pl.BlockSpec(memory_space=pltpu.VMEM)
SMEM_SPEC = pl.BlockSpec(memory_space=pltpu.SMEM)

N_DEV = 8
CHUNK = 64
HEAD_DIM = 64
LANES = 128
H_A, KV_A, H_B = 8, 2, 8
BACK_A, BACK_B = 2, 8
REL_CLIP = 128
N_REL = 2 * REL_CLIP + 1
QA, KVA, QB = H_A * HEAD_DIM, KV_A * HEAD_DIM, H_B * HEAD_DIM
D_IN = QA + 2 * KVA + 3 * QB
N_MOD = 9
EPS = 1e-6
NEG_INF = -1e30
QG = 4
QROWS = QG * CHUNK
TPS_A, TPS_B = 4, 2
SKEW = 1024
ADAM_LR, ADAM_B1, ADAM_B2, ADAM_EPS, ADAM_WD, ADAM_STEP = 0.001, 0.9, 0.999, 1e-08, 0.01, 10
VMEM_LIMIT = 56 * 2 ** 20


def _pick(n, cands):
    for c in cands:
        if n % c == 0:
            return c
    return n


def _pieces(n, width=2 * LANES):
    return [(lo, min(lo + width, n)) for lo in range(0, n, width)]


def _params(sem=None):
    return pltpu.CompilerParams(dimension_semantics=sem, vmem_limit_bytes=VMEM_LIMIT)


def _dot_nt(a, b):
    return lax.dot_general(a, b, (((1,), (1,)), ((), ())), preferred_element_type=F32)


def _dot_tn(a, b):
    return lax.dot_general(a, b, (((0,), (0,)), ((), ())), preferred_element_type=F32)


def _dot(a, b):
    return jnp.dot(a, b, preferred_element_type=F32)


def _sigmoid(a):
    return 0.5 * (jnp.tanh(0.5 * a) + 1.0)


def _mesh_pos():
    return lax.axis_index("x"), lax.axis_index("y"), lax.axis_index("c")


def _peer(x, y, c, r):
    px = 1 - x if r & 4 else x
    py = 1 - y if r & 2 else y
    pc = 1 - c if r & 1 else c
    return px, py, pc


class _Carry:
    def __init__(self, ins, out_shapes, scratch, start, finish, aliases=()):
        self.ins, self.out_shapes, self.scratch = list(ins), list(out_shapes), list(scratch)
        self.start, self.finish, self.aliases = start, finish, list(aliases)


def _call(body, *, name, grid, in_specs, out_specs, out_shape, args, scratch=(), sem=None, carry=None):
    single = not isinstance(out_shape, (tuple, list))
    out_specs = (out_specs,) if single else tuple(out_specs)
    out_shape = (out_shape,) if single else tuple(out_shape)
    if carry is None:
        res = pl.pallas_call(body, name=name, grid=grid, in_specs=list(in_specs), out_specs=out_specs,
                             out_shape=out_shape, scratch_shapes=list(scratch), compiler_params=_params(sem))(*args)
        return res[0] if single else res
    n_in, n_out, n_s = len(in_specs), len(out_shape), len(scratch)
    ci, co = len(carry.ins), len(carry.out_shapes)

    def wrapped(*refs):
        ins, cins = refs[:n_in], refs[n_in:n_in + ci]
        outs = refs[n_in + ci:n_in + ci + n_out]
        couts = refs[n_in + ci + n_out:n_in + ci + n_out + co]
        scr = refs[n_in + ci + n_out + co:n_in + ci + n_out + co + n_s]
        cscr = refs[n_in + ci + n_out + co + n_s:]
        first, last = None, None
        for ax, n in enumerate(grid):
            f, l = pl.program_id(ax) == 0, pl.program_id(ax) == n - 1
            first = f if first is None else first & f
            last = l if last is None else last & l
        pl.when(first)(lambda: carry.start(cins, couts, cscr))
        body(*ins, *outs, *scr)
        pl.when(last)(lambda: carry.finish(cins, couts, cscr))

    res = pl.pallas_call(
        wrapped, name=name, grid=grid, in_specs=list(in_specs) + [ANY] * ci, out_specs=out_specs + (ANY,) * co,
        out_shape=out_shape + tuple(carry.out_shapes), scratch_shapes=list(scratch) + carry.scratch,
        input_output_aliases={n_in + i: n_out + o for i, o in carry.aliases},
        compiler_params=_params(("arbitrary",) * len(grid)))(*args, *carry.ins)
    main = res[:n_out]
    return (main[0] if single else main), res[n_out:]


PASS_PIECES = 4
ROW_TILE = 16


def _row_pieces(rows):
    tiles, pieces, off = rows // ROW_TILE, [], 0
    for i in range(PASS_PIECES):
        n = (tiles + i) // PASS_PIECES * ROW_TILE
        if n:
            pieces.append((off, n))
        off += n
    assert off == rows
    return pieces


def _gather_carry(full=(), new=(), cont=()):
    full, new, cont = list(full), list(new), list(cont)
    n_full, n_one = len(full), len(full) + len(new)
    n_w = n_one + len(cont)
    rows = [s.shape[0] for s in full + new] + [b.shape[0] // N_DEV for b in cont]
    shapes = [jax.ShapeDtypeStruct((N_DEV * s.shape[0], s.shape[1]), s.dtype) for s in full + new]
    shapes += [jax.ShapeDtypeStruct(b.shape, b.dtype) for b in cont]
    halves = [((0, r // 2), (r // 2, r // 2)) for r in rows]
    pieces = [_row_pieces(r) for r in rows]

    def plan(ins, outs, scr):
        send_sems, recv_sems, local_sems = scr
        x, y, c = _mesh_pos()
        me, sibling = (x, y, c), (x, y, 1 - c)
        x_chip, y_chip, far_chip = (1 - x, y), (x, 1 - y), (1 - x, 1 - y)

        def block(buf, w, chip, core, span=None):
            off, n = (0, rows[w]) if span is None else span
            start = (4 * chip[0] + 2 * chip[1] + core) * rows[w] + off
            return buf.at[pl.ds(pl.multiple_of(start, 16), n), :]

        def copy(w, k, chip, core, to, span=None, held=None, shard=None, p=0):
            if shard is None:
                src = block(outs[w] if held is None else held, w, chip, core, span)
            else:
                src = shard if span is None else shard.at[pl.ds(span[0], span[1]), :]
            return pltpu.make_async_remote_copy(
                src_ref=src, dst_ref=block(outs[w], w, chip, core, span), send_sem=send_sems.at[w, k, p],
                recv_sem=recv_sems.at[w, k, p], device_id=to, device_id_type=MESH)

        def to_sibling(w, k, chip, held=None, shard=None):
            return [copy(w, k, chip, c, sibling, span, held, shard, p) for p, span in enumerate(pieces[w])]

        def from_sibling(w, k, chip):
            return [copy(w, k, chip, 1 - c, me, span, p=p) for p, span in enumerate(pieces[w])]

        def stage_one(w):
            return [copy(w, 1, (x, y), c, (*x_chip, c), shard=ins[w]), copy(w, 2, (x, y), c, (*y_chip, c), shard=ins[w]),
                    *to_sibling(w, 0, (x, y), shard=ins[w])]

        def stage_two(w, held):
            return [copy(w, 5, x_chip, c, (*y_chip, c), halves[w][0], held),
                    copy(w, 6, y_chip, c, (*x_chip, c), halves[w][1], held),
                    *to_sibling(w, 3, x_chip, held), *to_sibling(w, 4, y_chip, held)]

        mine = [pltpu.make_async_copy(ins[w], block(outs[w], w, (x, y), c), local_sems.at[w]) for w in range(n_one)]
        return c, me, sibling, far_chip, copy, to_sibling, from_sibling, stage_one, stage_two, mine

    def start(ins, outs, scr):
        _, _, _, _, _, _, _, stage_one, stage_two, mine = plan(ins, outs, scr)
        for w in range(n_one):
            for cp in stage_one(w):
                cp.start()
        for w in range(n_one, n_w):
            for cp in stage_two(w, ins[w]):
                cp.start()
        for cp in mine:
            cp.start()

    def finish(ins, outs, scr):
        c, me, sibling, far_chip, copy, to_sibling, from_sibling, stage_one, stage_two, mine = plan(ins, outs, scr)
        x_chip, y_chip = (far_chip[0], me[1]), (me[0], far_chip[1])
        sent = []

        def land_one(w):
            copy(w, 1, x_chip, c, me).wait_recv()
            copy(w, 2, y_chip, c, me).wait_recv()

        def land_two(w):
            copy(w, 5, far_chip, c, me, halves[w][0]).wait_recv()
            copy(w, 6, far_chip, c, me, halves[w][1]).wait_recv()
            for cp in to_sibling(w, 7, far_chip):
                cp.start()
                sent.append(cp)

        for w in range(n_full):
            land_one(w)
            for cp in stage_two(w, outs[w]):
                cp.start()
                sent.append(cp)
        for w in range(n_one, n_w):
            land_two(w)
            sent.extend(stage_two(w, ins[w]))
        for w in range(n_full, n_one):
            land_one(w)
        for w in range(n_full):
            land_two(w)
        for w in range(n_one):
            for cp in from_sibling(w, 0, me[:2]):
                cp.wait_recv()
            sent.extend(stage_one(w))
        for w in list(range(n_full)) + list(range(n_one, n_w)):
            for k, chip in ((3, x_chip), (4, y_chip), (7, far_chip)):
                for cp in from_sibling(w, k, chip):
                    cp.wait_recv()
        for cp in sent:
            cp.wait_send()
        for cp in mine:
            cp.wait()

    return _Carry(
        full + new + cont, shapes,
        [pltpu.SemaphoreType.DMA((n_w, N_DEV, PASS_PIECES)), pltpu.SemaphoreType.DMA((n_w, N_DEV, PASS_PIECES)),
         pltpu.SemaphoreType.DMA((max(n_one, 1),))], start, finish,
        aliases=[(w, w) for w in range(n_one, n_w)])


HBM_SPEC = pl.BlockSpec(memory_space=pltpu.HBM)
SEM_SPEC = pl.BlockSpec(memory_space=pltpu.SEMAPHORE)
N_CHIP = N_DEV // 2


def _scatter_copy(part_ref, land_ref, send_sem, recv_sem, r, rows):
    x, y, c = _mesh_pos()
    px, py, _ = _peer(x, y, c, 2 * r)
    src = part_ref.at[pl.ds(pl.multiple_of((2 * px + py) * rows, 16), rows), :]
    return pltpu.make_async_remote_copy(
        src_ref=src, dst_ref=land_ref.at[r - 1], send_sem=send_sem, recv_sem=recv_sem,
        device_id=(px, py, c), device_id_type=MESH)


def _scatter_order(n_w):
    return [(w, r) for r in (3, 2, 1) for w in range(n_w)]


def _scatter_start(parts, name):
    n_w = len(parts)
    rows = [p.shape[0] // N_CHIP for p in parts]
    order = _scatter_order(n_w)
    lands = [pltpu.with_memory_space_constraint(lax.empty((N_CHIP - 1, r, p.shape[1]), p.dtype), pltpu.HBM)
             for r, p in zip(rows, parts)]

    def body(*refs):
        part_refs, land_refs = refs[:n_w], refs[n_w:2 * n_w]
        sems = refs[2 * n_w:2 * n_w + 2 * len(order)]
        token = refs[-1]
        for j, (w, r) in enumerate(order):
            _scatter_copy(part_refs[w], land_refs[w], sems[2 * j], sems[2 * j + 1], r, rows[w]).start()
        token[...] = jnp.zeros_like(token)

    n_sem = 2 * len(order)
    res = pl.pallas_call(
        body, name=name,
        out_shape=(*[pltpu.SemaphoreType.DMA(())] * n_sem, *[pltpu.HBM(p.shape, p.dtype) for p in parts],
                   *[pltpu.HBM(l.shape, l.dtype) for l in lands], jax.ShapeDtypeStruct((8, LANES), F32)),
        in_specs=[HBM_SPEC] * (2 * n_w), out_specs=(*[SEM_SPEC] * n_sem, *[HBM_SPEC] * (2 * n_w), VMEM_SPEC),
        input_output_aliases={i: n_sem + i for i in range(2 * n_w)},
        compiler_params=pltpu.CompilerParams(has_side_effects=pltpu.SideEffectType.DATAFLOW_SIDE_EFFECTING),
    )(*[pltpu.with_memory_space_constraint(p, pltpu.HBM) for p in parts], *lands)
    return (list(res[:n_sem]), list(res[n_sem:n_sem + n_w]), list(res[n_sem + n_w:n_sem + 2 * n_w]), res[-1])


def _scatter_wait(sems, parts, lands, after, name):
    n_w = len(parts)
    rows = [p.shape[0] // N_CHIP for p in parts]
    order = _scatter_order(n_w)

    def body(*refs):
        part_refs, land_refs = refs[:n_w], refs[n_w:2 * n_w]
        sem_refs = refs[2 * n_w:2 * n_w + 2 * len(order)]
        for j, (w, r) in enumerate(order):
            cp = _scatter_copy(part_refs[w], land_refs[w], sem_refs[2 * j], sem_refs[2 * j + 1], r, rows[w])
            cp.wait_send()
            cp.wait_recv()

    res = pl.pallas_call(
        body, name=name,
        out_shape=(*[pltpu.HBM(p.shape, p.dtype) for p in parts], *[pltpu.HBM(l.shape, l.dtype) for l in lands]),
        in_specs=[HBM_SPEC] * (2 * n_w) + [SEM_SPEC] * len(sems) + [ANY],
        out_specs=tuple([HBM_SPEC] * (2 * n_w)),
        input_output_aliases={i: i for i in range(2 * n_w)},
        compiler_params=pltpu.CompilerParams(has_side_effects=pltpu.SideEffectType.DATAFLOW_SIDE_EFFECTING),
    )(*parts, *lands, *sems, after)
    return list(res[:n_w]), list(res[n_w:])


def _small_copy(v_ref, land_ref, send_sem, recv_sem, r):
    x, y, c = _mesh_pos()
    px, py, pc = _peer(x, y, c, r)
    return pltpu.make_async_remote_copy(
        src_ref=v_ref, dst_ref=land_ref.at[4 * x + 2 * y + c], send_sem=send_sem, recv_sem=recv_sem,
        device_id=(px, py, pc), device_id_type=MESH)


def _small_gather_start(v):
    land = pltpu.with_memory_space_constraint(lax.empty((N_DEV,) + v.shape, v.dtype), pltpu.HBM)

    def body(v_ref, land_ref, *rest):
        sems, token = rest[:2 * (N_DEV - 1)], rest[-1]
        for r in range(1, N_DEV):
            _small_copy(v_ref, land_ref, sems[2 * r - 2], sems[2 * r - 1], r).start()
        token[...] = jnp.zeros_like(token)

    n_sem = 2 * (N_DEV - 1)
    res = pl.pallas_call(
        body, name="small_gather_start",
        out_shape=(*[pltpu.SemaphoreType.DMA(())] * n_sem, pltpu.HBM(v.shape, v.dtype),
                   pltpu.HBM(land.shape, land.dtype), jax.ShapeDtypeStruct((8, LANES), F32)),
        in_specs=[HBM_SPEC, HBM_SPEC], out_specs=(*[SEM_SPEC] * n_sem, HBM_SPEC, HBM_SPEC, VMEM_SPEC),
        input_output_aliases={0: n_sem, 1: n_sem + 1},
        compiler_params=pltpu.CompilerParams(has_side_effects=pltpu.SideEffectType.DATAFLOW_SIDE_EFFECTING),
    )(pltpu.with_memory_space_constraint(v, pltpu.HBM), land)
    return list(res[:n_sem]), res[n_sem], res[n_sem + 1], res[-1]


def _small_gather_wait(sems, v, land, after):
    def body(v_ref, land_ref, *rest):
        for r in range(1, N_DEV):
            cp = _small_copy(v_ref, land_ref, rest[2 * r - 2], rest[2 * r - 1], r)
            cp.wait_send()
            x, y, c = _mesh_pos()
            px, py, pc = _peer(x, y, c, r)
            pltpu.make_async_remote_copy(
                src_ref=v_ref, dst_ref=land_ref.at[4 * px + 2 * py + pc], send_sem=rest[2 * r - 2],
                recv_sem=rest[2 * r - 1], device_id=(px, py, pc), device_id_type=MESH).wait_recv()

    res = pl.pallas_call(
        body, name="small_gather_wait",
        out_shape=(pltpu.HBM(v.shape, v.dtype), pltpu.HBM(land.shape, land.dtype)),
        in_specs=[HBM_SPEC, HBM_SPEC] + [SEM_SPEC] * len(sems) + [ANY], out_specs=(HBM_SPEC, HBM_SPEC),
        input_output_aliases={0: 0, 1: 1},
        compiler_params=pltpu.CompilerParams(has_side_effects=pltpu.SideEffectType.DATAFLOW_SIDE_EFFECTING),
    )(v, land, *sems, after)
    return res[0], res[1]


def _ada_forward(c_row, w_ada, b_cols, carry):
    d = c_row.shape[1]
    wcols = w_ada.shape[1]
    ci, co = len(carry.ins), len(carry.out_shapes)

    def body(*refs):
        c_ref, w_ref, b_ref = refs[:3]
        cins = refs[3:3 + ci]
        sc_ref, mod_ref = refs[3 + ci:5 + ci]
        couts = refs[5 + ci:5 + ci + co]
        rows_ref, send_sems, recv_sems = refs[5 + ci + co:8 + ci + co]
        cscr = refs[8 + ci + co:]
        carry.start(cins, couts, cscr)
        x, y, c = _mesh_pos()
        me = 4 * x + 2 * y + c
        cv = c_ref[...]
        sc_ref[me] = cv * _sigmoid(cv)

        sends = []
        for r in range(1, N_DEV):
            px, py, pc = _peer(x, y, c, r)
            cp = pltpu.make_async_remote_copy(
                src_ref=sc_ref.at[me], dst_ref=sc_ref.at[me], send_sem=send_sems.at[0, r - 1],
                recv_sem=recv_sems.at[0, r - 1], device_id=(px, py, pc), device_id_type=MESH)
            cp.start()
            sends.append(cp)
        for r in range(1, N_DEV):
            px, py, pc = _peer(x, y, c, r)
            pid = 4 * px + 2 * py + pc
            pltpu.make_async_remote_copy(
                src_ref=sc_ref.at[pid], dst_ref=sc_ref.at[pid], send_sem=send_sems.at[0, r - 1],
                recv_sem=recv_sems.at[0, r - 1], device_id=(px, py, pc), device_id_type=MESH).wait_recv()
        for cp in sends:
            cp.wait_send()

        sc_all = jnp.concatenate([sc_ref[j] for j in range(N_DEV)], axis=0)
        rows = _dot(sc_all.astype(BF16), w_ref[...].astype(BF16)) + b_ref[...]
        for j in range(N_DEV):
            rows_ref[j] = rows[j:j + 1, :]
        mod_ref[me] = rows_ref[me]

        sends = []
        for r in range(1, N_DEV):
            px, py, pc = _peer(x, y, c, r)
            pid = 4 * px + 2 * py + pc
            cp = pltpu.make_async_remote_copy(
                src_ref=rows_ref.at[pid], dst_ref=mod_ref.at[me], send_sem=send_sems.at[1, r - 1],
                recv_sem=recv_sems.at[1, r - 1], device_id=(px, py, pc), device_id_type=MESH)
            cp.start()
            sends.append(cp)
        for r in range(1, N_DEV):
            px, py, pc = _peer(x, y, c, r)
            pid = 4 * px + 2 * py + pc
            pltpu.make_async_remote_copy(
                src_ref=rows_ref.at[pid], dst_ref=mod_ref.at[pid], send_sem=send_sems.at[1, r - 1],
                recv_sem=recv_sems.at[1, r - 1], device_id=(px, py, pc), device_id_type=MESH).wait_recv()
        for cp in sends:
            cp.wait_send()
        carry.finish(cins, couts, cscr)

    res = pl.pallas_call(
        body, name="ada_forward",
        out_shape=(jax.ShapeDtypeStruct((N_DEV, 1, d), F32), jax.ShapeDtypeStruct((N_DEV, 1, wcols), F32),
                   *carry.out_shapes),
        in_specs=[VMEM_SPEC, VMEM_SPEC, VMEM_SPEC] + [ANY] * ci, out_specs=(VMEM_SPEC, VMEM_SPEC) + (ANY,) * co,
        scratch_shapes=[pltpu.VMEM((N_DEV, 1, wcols), F32), pltpu.SemaphoreType.DMA((2, N_DEV - 1)),
                        pltpu.SemaphoreType.DMA((2, N_DEV - 1))] + carry.scratch,
        compiler_params=_params(),
    )(c_row, w_ada, b_cols, *carry.ins)
    return res[:2], res[2:]


def _mm_nt(a, b, name, out_dtype, bias=None, carry=None):
    m, k = a.shape
    n = b.shape[0]
    tm = _pick(m, (512, 256, 128))
    tn = _pick(n, (1408, 1152, 1024, 768, 512, 256, 128))

    def body(*refs):
        acc = _dot_nt(refs[0][...], refs[1][...])
        if bias is not None:
            acc = acc + refs[2][...]
        refs[-1][...] = acc.astype(out_dtype)

    in_specs = [pl.BlockSpec((tm, k), lambda j, i: (i, 0)), pl.BlockSpec((tn, k), lambda j, i: (j, 0))]
    args = [a, b]
    if bias is not None:
        in_specs.append(pl.BlockSpec((1, tn), lambda j, i: (0, j)))
        args.append(bias)
    return _call(body, name=name, grid=(n // tn, m // tm), in_specs=in_specs,
                 out_specs=pl.BlockSpec((tm, tn), lambda j, i: (i, j)),
                 out_shape=jax.ShapeDtypeStruct((m, n), out_dtype), args=args,
                 sem=("parallel", "parallel"), carry=carry)


class _Tail:
    def __init__(self, rows, vecs, outs, fn):
        self.rows, self.vecs, self.outs, self.fn = list(rows), list(vecs), list(outs), fn


def _mm_nn(pairs, name, out_dtype, bias=None, carry=None, tail=None):
    m, k = pairs[0][0].shape
    n = pairs[0][1].shape[1]
    n_p = len(pairs)
    tm = _pick(m, (512, 256, 128))
    tk = k if n_p == 1 else _pick(k, (1408, 1152, 1024, 768, 512, 256, 128))
    nk = k // tk
    n_b = 0 if bias is None else 1
    n_r, n_v = (len(tail.rows), len(tail.vecs)) if tail else (0, 0)
    n_in = 2 * n_p + n_b + n_r + n_v
    n_main = 0 if out_dtype is None else 1

    def finish(acc, refs, first_tile):
        if bias is not None:
            acc = acc + refs[2 * n_p][...]
        outs = refs[n_in:-1]
        if n_main:
            outs[0][...] = acc.astype(out_dtype)
        if tail is None:
            return
        rows = [r[...] for r in refs[2 * n_p + n_b:2 * n_p + n_b + n_r]]
        vecs = [v[...] for v in refs[2 * n_p + n_b + n_r:n_in]]
        vals = tail.fn(acc, rows, vecs)
        for ref, val, (dtype, kind) in zip(outs[n_main:], vals, tail.outs):
            if kind == "row":
                ref[...] = val.astype(dtype)
            else:
                @pl.when(first_tile)
                def _(ref=ref):
                    ref[...] = jnp.zeros_like(ref)

                ref[...] += val

    def body(*refs):
        acc_ref = refs[-1]
        kk, i = pl.program_id(0), pl.program_id(1)
        part = _dot(refs[0][...], refs[1][...])
        for p in range(1, n_p):
            part = part + _dot(refs[2 * p][...], refs[2 * p + 1][...])
        if nk == 1:
            finish(part, refs, i == 0)
            return
        rows = pl.ds(pl.multiple_of(i * tm, tm), tm)

        @pl.when(kk == 0)
        def _():
            acc_ref[rows, :] = part

        if nk > 2:
            @pl.when((kk > 0) & (kk < nk - 1))
            def _():
                acc_ref[rows, :] += part

        @pl.when(kk == nk - 1)
        def _():
            finish(acc_ref[rows, :] + part, refs, i == 0)

    def last_only(kk, i):
        return (jnp.where(kk == nk - 1, i, 0), 0)

    row_spec = pl.BlockSpec((tm, n), last_only)
    vec_spec = pl.BlockSpec((1, n), lambda kk, i: (0, 0))
    in_specs, args = [], []
    for a, b in pairs:
        in_specs += [pl.BlockSpec((tm, tk), lambda kk, i: (i, kk)), pl.BlockSpec((tk, n), lambda kk, i: (kk, 0))]
        args += [a, b]
    if bias is not None:
        in_specs.append(vec_spec)
        args.append(bias)
    out_specs = [row_spec] * n_main
    out_shape = [jax.ShapeDtypeStruct((m, n), out_dtype)] if n_main else []
    if tail:
        in_specs += [row_spec] * n_r + [vec_spec] * n_v
        args += tail.rows + tail.vecs
        for dtype, kind in tail.outs:
            if kind == "row":
                out_specs.append(row_spec)
                out_shape.append(jax.ShapeDtypeStruct((m, n), dtype))
            else:
                width = n if kind == "sum" else 1
                out_specs.append(pl.BlockSpec((1, width), lambda kk, i: (0, 0)))
                out_shape.append(jax.ShapeDtypeStruct((1, width), dtype))
    if tail is None:
        out_specs, out_shape = out_specs[0], out_shape[0]
    return _call(body, name=name, grid=(nk, m // tm), in_specs=in_specs, out_specs=out_specs,
                 out_shape=out_shape, args=args,
                 scratch=[pltpu.VMEM((m, n) if nk > 1 else (8, LANES), F32)],
                 sem=("arbitrary", "arbitrary"), carry=carry)


def _rms(v):
    return lax.rsqrt(jnp.mean(v * v, axis=-1, keepdims=True) + EPS)


def _col(v):
    return jnp.sum(v, axis=0, keepdims=True)


def _tail_post_pre(x, g_post, gate, weight, g_pre, scale, shift):
    def fn(y, rows, vecs):
        (xv,), (gp, gt, g, sc, sh) = rows, vecs
        xo = xv + (weight * gt) * ((y * _rms(y)) * gp)
        return xo, ((xo * _rms(xo)) * g) * (1.0 + sc) + sh

    return _Tail([x], [g_post, gate, g_pre, scale, shift], [(F32, "row"), (BF16, "row")], fn)


def _tail_post_loss(x, target, g, gate, weight):
    def fn(y, rows, vecs):
        (xv, tv), (gv, gt) = rows, vecs
        r = _rms(y)
        yn = y * r
        err = (xv + (weight * gt) * (yn * gv)) - tv
        do = err * (1.0 / y.shape[1])
        dyn = do * ((weight * gt) * gv)
        dy = r * (dyn - yn * jnp.mean(dyn * yn, axis=-1, keepdims=True))
        return do, dy, 0.5 * _col(jnp.mean(err * err, axis=-1, keepdims=True)), _col(do * yn)

    return _Tail([x, target], [g, gate], [(F32, "row"), (BF16, "row"), (F32, "one"), (F32, "sum")], fn)


def _tail_pre_bwd(x, dres, g_pre, scale):
    def fn(dh, rows, vecs):
        (xv, dr), (g, sc) = rows, vecs
        r = _rms(xv)
        n = xv * r
        dn = dh * (g * (1.0 + sc))
        return dr + r * (dn - n * jnp.mean(dn * n, axis=-1, keepdims=True)), _col(dh * n), _col(dh)

    return _Tail([x, dres], [g_pre, scale], [(F32, "row"), (F32, "sum"), (F32, "sum")], fn)


def _tail_pre_post_bwd(x, dres, y, g_pre, scale, g_post, gate, weight):
    def fn(dh, rows, vecs):
        (xv, dr, yv), (g, sc, gp, gt) = rows, vecs
        r = _rms(xv)
        n = xv * r
        dn = dh * (g * (1.0 + sc))
        dx = dr + r * (dn - n * jnp.mean(dn * n, axis=-1, keepdims=True))
        ry = _rms(yv)
        yn = yv * ry
        dyn = dx * ((weight * gt) * gp)
        dy = ry * (dyn - yn * jnp.mean(dyn * yn, axis=-1, keepdims=True))
        return dx, dy, _col(dh * n), _col(dh), _col(dx * yn), _col(dy)

    return _Tail([x, dres, y], [g_pre, scale, g_post, gate],
                 [(F32, "row"), (BF16, "row")] + [(F32, "sum")] * 4, fn)


def _mm_tn_pair(a, b, name, col_sums=False):
    k, m = a.shape
    n = b.shape[1]
    rows = m // N_DEV
    n_chip = N_DEV // 2
    tm = 4 * rows
    tk = _pick(k, (1024, 512, 256, 128))
    nk = k // tk

    def body(a_ref, b_ref, p_ref, own_ref, *rest):
        acc_ref, keep_ref, send_ref, land_ref, send_sems, recv_sems = rest[-6:]
        i, kk = pl.program_id(0), pl.program_id(1)
        x, y, c = _mesh_pos()
        if col_sums:
            cs_ref = rest[0]
            part = jnp.sum(a_ref[...].astype(F32), axis=0, keepdims=True)

            @pl.when(kk == 0)
            def _():
                cs_ref[...] = part

            @pl.when(kk > 0)
            def _():
                cs_ref[...] += part

        def push(chip):
            return pltpu.make_async_remote_copy(
                src_ref=send_ref.at[chip], dst_ref=land_ref.at[chip], send_sem=send_sems.at[chip],
                recv_sem=recv_sems.at[chip], device_id=(x, y, 1 - c), device_id_type=MESH)

        if nk == 1:
            acc = _dot_tn(a_ref[...], b_ref[...])
        else:
            @pl.when(kk == 0)
            def _():
                acc_ref[...] = jnp.zeros_like(acc_ref)

            acc_ref[...] += _dot_tn(a_ref[...], b_ref[...])
            acc = acc_ref

        for t in range(2):
            @pl.when((kk == nk - 1) & (i == t))
            def _(t=t):
                for ob in range(4):
                    chip, core = 2 * t + ob // 2, ob % 2
                    blk = acc[ob * rows:(ob + 1) * rows, :]

                    @pl.when(c == core)
                    def _(chip=chip, blk=blk):
                        keep_ref[chip] = blk

                    @pl.when(c != core)
                    def _(chip=chip, blk=blk):
                        send_ref[chip] = blk.astype(BF16)
                        push(chip).start()

        @pl.when((kk == nk - 1) & (i == 1))
        def _():
            for chip in range(n_chip):
                push(chip).wait_recv()
                val = (keep_ref[chip] + land_ref[chip].astype(F32)).astype(BF16)
                p_ref[chip * rows:(chip + 1) * rows, :] = val

                @pl.when(2 * x + y == chip)
                def _(val=val):
                    own_ref[...] = val

            for chip in range(n_chip):
                push(chip).wait_send()

    out_specs = [pl.BlockSpec((n_chip * rows, n), lambda i, kk: (0, 0)), pl.BlockSpec((rows, n), lambda i, kk: (0, 0))]
    out_shape = [jax.ShapeDtypeStruct((n_chip * rows, n), BF16), jax.ShapeDtypeStruct((rows, n), BF16)]
    if col_sums:
        out_specs.append(pl.BlockSpec((1, tm), lambda i, kk: (0, i)))
        out_shape.append(jax.ShapeDtypeStruct((1, m), F32))
    return _call(body, name=name, grid=(2, nk),
                 in_specs=[pl.BlockSpec((tk, tm), lambda i, kk: (kk, i)), pl.BlockSpec((tk, n), lambda i, kk: (kk, 0))],
                 out_specs=out_specs, out_shape=out_shape, args=[a, b],
                 scratch=[pltpu.VMEM((tm, n) if nk > 1 else (8, LANES), F32), pltpu.VMEM((n_chip, rows, n), F32),
                          pltpu.VMEM((n_chip, rows, n), BF16), pltpu.VMEM((n_chip, rows, n), BF16),
                          pltpu.SemaphoreType.DMA((n_chip,)), pltpu.SemaphoreType.DMA((n_chip,))],
                 sem=("arbitrary", "arbitrary"))


def _ffn_up(h, wg_t, wu_t, name, carry=None):
    s, d = h.shape
    f = wg_t.shape[0]
    tm = _pick(s, (512, 256, 128))
    tf = _pick(f, (1408, 1024, 512, 256, 128))

    def body(h_ref, wg_ref, wu_ref, a_ref, b_ref, u_ref):
        hh = h_ref[...]
        for lo, hi in _pieces(tf):
            a = _dot_nt(hh, wg_ref[lo:hi, :])
            b = _dot_nt(hh, wu_ref[lo:hi, :])
            a_ref[:, lo:hi] = a.astype(BF16)
            b_ref[:, lo:hi] = b.astype(BF16)
            u_ref[:, lo:hi] = ((a * _sigmoid(a)) * b).astype(BF16)

    w_spec = pl.BlockSpec((tf, d), lambda j, i: (j, 0))
    o_spec = pl.BlockSpec((tm, tf), lambda j, i: (i, j))
    o_shape = jax.ShapeDtypeStruct((s, f), BF16)
    return _call(body, name=name, grid=(f // tf, s // tm),
                 in_specs=[pl.BlockSpec((tm, d), lambda j, i: (i, 0)), w_spec, w_spec],
                 out_specs=(o_spec, o_spec, o_spec), out_shape=(o_shape, o_shape, o_shape),
                 args=[h, wg_t, wu_t], sem=("parallel", "parallel"), carry=carry)


def _ffn_down_bwd(dy, wd, a, b, name, carry=None):
    s, d = dy.shape
    f = wd.shape[0]
    tm = _pick(s, (512, 256, 128))
    tf = _pick(f, (1408, 1024, 512, 256, 128))

    def body(dy_ref, wd_ref, a_ref, b_ref, da_ref, db_ref):
        dyv = dy_ref[...]
        for lo, hi in _pieces(tf):
            du = _dot_nt(dyv, wd_ref[lo:hi, :])
            a = a_ref[:, lo:hi].astype(F32)
            b = b_ref[:, lo:hi].astype(F32)
            sig = _sigmoid(a)
            da_ref[:, lo:hi] = (du * b * (sig * (1.0 + a * (1.0 - sig)))).astype(BF16)
            db_ref[:, lo:hi] = (du * (a * sig)).astype(BF16)

    t_spec = pl.BlockSpec((tm, tf), lambda j, i: (i, j))
    o_shape = jax.ShapeDtypeStruct((s, f), BF16)
    return _call(body, name=name, grid=(f // tf, s // tm),
                 in_specs=[pl.BlockSpec((tm, d), lambda j, i: (i, 0)), pl.BlockSpec((tf, d), lambda j, i: (j, 0)),
                           t_spec, t_spec],
                 out_specs=(t_spec, t_spec), out_shape=(o_shape, o_shape), args=[dy, wd, a, b],
                 sem=("parallel", "parallel"), carry=carry)


def _row_tile(s):
    return _pick(s, (256, 128, 64))


def _vec_spec(d):
    return pl.BlockSpec((1, d), lambda i: (0, 0))


def _pre_norm(x, g, scale, shift, name):
    s, d = x.shape
    ts = _row_tile(s)

    def body(x_ref, g_ref, sc_ref, sh_ref, h_ref):
        xv = x_ref[...]
        r = lax.rsqrt(jnp.mean(xv * xv, axis=-1, keepdims=True) + EPS)
        h_ref[...] = (((xv * r) * g_ref[...]) * (1.0 + sc_ref[...]) + sh_ref[...]).astype(BF16)

    row = pl.BlockSpec((ts, d), lambda i: (i, 0))
    return _call(body, name=name, grid=(s // ts,), in_specs=[row, _vec_spec(d), _vec_spec(d), _vec_spec(d)],
                 out_specs=row, out_shape=jax.ShapeDtypeStruct((s, d), BF16), args=[x, g, scale, shift],
                 sem=("parallel",))


def _group_norm_cat(oa, ob, ga, gb):
    s = oa.shape[0]
    ts = _row_tile(s)

    def body(oa_ref, ob_ref, ga_ref, gb_ref, y_ref):
        for o_ref, g_ref, lo, w in ((oa_ref, ga_ref, 0, QA), (ob_ref, gb_ref, QA, QB)):
            ov = o_ref[...]
            r = lax.rsqrt(jnp.mean(ov * ov, axis=-1, keepdims=True) + EPS)
            y_ref[:, lo:lo + w] = ((ov * r) * g_ref[...]).astype(BF16)

    return _call(body, name="group_norm_cat", grid=(s // ts,),
                 in_specs=[pl.BlockSpec((ts, QA), lambda i: (i, 0)), pl.BlockSpec((ts, QB), lambda i: (i, 0)),
                           _vec_spec(QA), _vec_spec(QB)],
                 out_specs=pl.BlockSpec((ts, QA + QB), lambda i: (i, 0)),
                 out_shape=jax.ShapeDtypeStruct((s, QA + QB), BF16), args=[oa, ob, ga, gb], sem=("parallel",))


def _group_norm_bwd(dy, oa, ob, ga, gb):
    s = oa.shape[0]
    ts = _row_tile(s)

    def body(dy_ref, oa_ref, ob_ref, ga_ref, gb_ref, doa_ref, dob_ref, dga_ref, dgb_ref):
        @pl.when(pl.program_id(0) == 0)
        def _():
            dga_ref[...] = jnp.zeros_like(dga_ref)
            dgb_ref[...] = jnp.zeros_like(dgb_ref)

        for o_ref, g_ref, do_ref, dg_ref, lo, w in ((oa_ref, ga_ref, doa_ref, dga_ref, 0, QA),
                                                    (ob_ref, gb_ref, dob_ref, dgb_ref, QA, QB)):
            ov = o_ref[...]
            dyv = dy_ref[:, lo:lo + w]
            r = lax.rsqrt(jnp.mean(ov * ov, axis=-1, keepdims=True) + EPS)
            n = ov * r
            dn = dyv * g_ref[...]
            do_ref[...] = r * (dn - n * jnp.mean(dn * n, axis=-1, keepdims=True))
            dg_ref[...] += jnp.sum(dyv * n, axis=0, keepdims=True)

    ra = pl.BlockSpec((ts, QA), lambda i: (i, 0))
    rb = pl.BlockSpec((ts, QB), lambda i: (i, 0))
    return _call(body, name="group_norm_bwd", grid=(s // ts,),
                 in_specs=[pl.BlockSpec((ts, QA + QB), lambda i: (i, 0)), ra, rb, _vec_spec(QA), _vec_spec(QB)],
                 out_specs=(ra, rb, _vec_spec(QA), _vec_spec(QB)),
                 out_shape=(jax.ShapeDtypeStruct((s, QA), F32), jax.ShapeDtypeStruct((s, QB), F32),
                            jax.ShapeDtypeStruct((1, QA), F32), jax.ShapeDtypeStruct((1, QB), F32)),
                 args=[dy, oa, ob, ga, gb], sem=("arbitrary",))


def _n_variants(n_back):
    return -(-n_back // QG) + 1


def _alibi_bias():
    i = np.arange(QROWS)[:, None]
    j = np.arange((QG + BACK_A) * CHUNK)[None, :]
    dist = np.abs(BACK_A * CHUNK + i - j).astype(np.float32)
    dc = j // CHUNK - i // CHUNK
    valid = (dc >= 0) & (dc <= BACK_A)
    slopes = np.array([2.0 ** (-8.0 * (h + 1) / H_A) for h in range(H_A)], dtype=np.float32)
    bias = -slopes[:, None, None] * dist[None]
    out = [np.where((valid & (j >= (BACK_A - QG * v) * CHUNK))[None], bias, np.float32(NEG_INF))
           for v in range(_n_variants(BACK_A))]
    return jnp.asarray(np.stack(out).astype(np.float32))


def _rel_index_matrix():
    cc = np.arange(SKEW)
    dist = np.where(cc < SKEW - QROWS, BACK_B * CHUNK - cc, BACK_B * CHUNK + SKEW - cc)
    idx = np.clip(dist, -REL_CLIP, REL_CLIP) + REL_CLIP
    m = np.zeros((SKEW, N_REL), np.float32)
    m[cc, idx] = 1.0
    return jnp.asarray(m)


def _toeplitz_bias(vec, carry=None):
    lk = (QG + BACK_B) * CHUNK
    nv = _n_variants(BACK_B)

    def body(v_ref, o_ref):
        xv = jnp.broadcast_to(v_ref[0], (QROWS, SKEW))
        row = lax.broadcasted_iota(jnp.int32, (QROWS, SKEW), 0)
        for bit in range(QROWS.bit_length() - 1):
            xv = jnp.where((row >> bit) & 1 == 1, pltpu.roll(xv, 1 << bit, 1), xv)
        ri = lax.broadcasted_iota(jnp.int32, (QROWS, lk), 0) // CHUNK
        col = lax.broadcasted_iota(jnp.int32, (QROWS, lk), 1)
        ci = col // CHUNK
        valid = (ci - ri >= 0) & (ci - ri <= BACK_B)
        for v in range(nv):
            o_ref[v, 0] = jnp.where(valid & (col >= (BACK_B - QG * v) * CHUNK), xv[:, :lk], NEG_INF)

    return _call(body, name="toeplitz_bias", grid=(H_B,),
                 in_specs=[pl.BlockSpec((1, 1, SKEW), lambda h: (h, 0, 0))],
                 out_specs=pl.BlockSpec((nv, 1, QROWS, lk), lambda h: (0, h, 0, 0)),
                 out_shape=jax.ShapeDtypeStruct((nv, H_B, QROWS, lk), F32), args=[vec], sem=("parallel",),
                 carry=carry)


def _diagonal_sums(dbias):
    lk = dbias.shape[2]

    def body(d_ref, o_ref):
        xp = jnp.concatenate([d_ref[0], jnp.zeros((QROWS, SKEW - lk), F32)], axis=1)
        xv = xp[0:CHUNK]
        for q in range(1, QG):
            xv = xv + pltpu.roll(xp[q * CHUNK:(q + 1) * CHUNK], SKEW - q * CHUNK, 1)
        row = lax.broadcasted_iota(jnp.int32, (CHUNK, SKEW), 0)
        for bit in range(CHUNK.bit_length() - 1):
            xv = jnp.where((row >> bit) & 1 == 1, pltpu.roll(xv, SKEW - (1 << bit), 1), xv)
        o_ref[0] = jnp.sum(xv, axis=0, keepdims=True)

    return _call(body, name="diagonal_sums", grid=(H_B,),
                 in_specs=[pl.BlockSpec((1, QROWS, lk), lambda h: (h, 0, 0))],
                 out_specs=pl.BlockSpec((1, 1, SKEW), lambda h: (h, 0, 0)),
                 out_shape=jax.ShapeDtypeStruct((H_B, 1, SKEW), F32), args=[dbias], sem=("parallel",))


def _attn_common(s, n_back, gqa, q_col, k_col, v_col, TPS):
    lk = (QG + n_back) * CHUNK
    pad = n_back * CHUNK
    wide = TPS * LANES
    q_spec = pl.BlockSpec((QROWS, wide), lambda t, g: (g, q_col // TPS + t))
    if gqa:
        k_spec = pl.BlockSpec((s, LANES), lambda t, g: (0, k_col))
        v_spec = pl.BlockSpec((s, LANES), lambda t, g: (0, v_col))
    else:
        k_spec = pl.BlockSpec((s, wide), lambda t, g: (0, k_col // TPS + t))
        v_spec = pl.BlockSpec((s, wide), lambda t, g: (0, v_col // TPS + t))
    last_variant = _n_variants(n_back) - 1
    bias_spec = pl.BlockSpec((None, 2 * TPS, QROWS, lk), lambda t, g: (jnp.minimum(g, last_variant), t, 0, 0))
    tile_spec = pl.BlockSpec((QROWS, wide), lambda t, g: (g, t))
    return lk, pad, q_spec, k_spec, v_spec, bias_spec, tile_spec


def _attention_fwd(proj, bias, sinks, *, n_back, gqa, q_col, k_col, v_col, TPS, name, carry=None):
    s = proj.shape[0]
    lk, pad, q_spec, k_spec, v_spec, bias_spec, tile_spec = _attn_common(s, n_back, gqa, q_col, k_col, v_col, TPS)
    n_t, n_g = 512 // (TPS * LANES), s // QROWS
    kv_wide = LANES if gqa else TPS * LANES

    def body(*refs):
        if gqa:
            q_ref, k_ref, v_ref, bias_ref, sink_ref, o_ref, l_ref, kpad, vpad = refs
        else:
            q_ref, k_ref, v_ref, bias_ref, o_ref, l_ref, kpad, vpad = refs
        t, g = pl.program_id(0), pl.program_id(1)

        @pl.when(g == 0)
        def _():
            kpad[0:pad, :] = jnp.zeros((pad, kv_wide), BF16)
            vpad[0:pad, :] = jnp.zeros((pad, kv_wide), BF16)
            kpad[pad:, :] = k_ref[...]
            vpad[pad:, :] = v_ref[...]

        start = pl.multiple_of(g * QROWS, QROWS)
        half = lax.broadcasted_iota(jnp.int32, (QROWS, LANES), 1) // HEAD_DIM
        for tt in range(TPS):
            lanes = slice(tt * LANES, (tt + 1) * LANES)
            kv_lanes = slice(0, LANES) if gqa else lanes
            kb = kpad[pl.ds(start, lk), kv_lanes]
            vb = vpad[pl.ds(start, lk), kv_lanes]
            q = q_ref[:, lanes] * (HEAD_DIM ** -0.5)
            if gqa:
                hk = (TPS * t + tt) // 2
                q_rolled = pltpu.roll(q.astype(F32), HEAD_DIM, 1).astype(BF16)
            outs, lses = [], []
            for e in range(2):
                if gqa:
                    kv_half = hk
                    src = jnp.where(hk == e, q, q_rolled)
                else:
                    kv_half = e
                    src = q
                qm = jnp.where(half == kv_half, src, jnp.zeros_like(src))
                sc = _dot_nt(qm, kb) + bias_ref[2 * tt + e]
                m = jnp.max(sc, axis=-1, keepdims=True)
                if gqa:
                    sk = sink_ref[2 * (TPS * t + tt) + e]
                    m = jnp.maximum(m, sk)
                p = jnp.exp(sc - m)
                l = jnp.sum(p, axis=-1, keepdims=True)
                if gqa:
                    l = l + jnp.exp(sk - m)
                pn = p / l
                outs.append(_dot(pn.astype(BF16), vb))
                lses.append(m + jnp.log(l))
            if gqa:
                same = jnp.where(hk == 0, outs[0], outs[1])
                other = jnp.where(hk == 0, outs[1], outs[0])
                o_ref[:, lanes] = jnp.where(half == hk, same, pltpu.roll(other, HEAD_DIM, 1))
            else:
                o_ref[:, lanes] = jnp.where(half == 0, outs[0], outs[1])
            l_ref[:, lanes] = jnp.where(half == 0, lses[0], lses[1])

    in_specs = [q_spec, k_spec, v_spec, bias_spec] + ([SMEM_SPEC] if gqa else [])
    args = [proj, proj, proj, bias] + ([sinks] if gqa else [])
    o_shape = jax.ShapeDtypeStruct((s, 512), F32)
    return _call(body, name=name, grid=(n_t, n_g), in_specs=in_specs, out_specs=(tile_spec, tile_spec),
                 out_shape=(o_shape, o_shape), args=args,
                 scratch=[pltpu.VMEM((s + pad, kv_wide), BF16), pltpu.VMEM((s + pad, kv_wide), BF16)],
                 sem=("arbitrary", "arbitrary"), carry=carry)


def _attention_bwd(proj, bias, sinks, do, lse, *, n_back, gqa, q_col, k_col, v_col, TPS, name, carry=None):
    s = proj.shape[0]
    lk, pad, q_spec, k_spec, v_spec, bias_spec, tile_spec = _attn_common(s, n_back, gqa, q_col, k_col, v_col, TPS)
    n_t, n_g = 512 // (TPS * LANES), s // QROWS
    kv_wide = LANES if gqa else TPS * LANES

    def body(*refs):
        if gqa:
            (q_ref, k_ref, v_ref, bias_ref, sink_ref, do_ref, l_ref,
             dq_ref, dk_ref, dv_ref, dsink_ref, kpad, vpad, dkpad, dvpad) = refs
        else:
            (q_ref, k_ref, v_ref, bias_ref, do_ref, l_ref,
             dq_ref, dk_ref, dv_ref, dbias_ref, kpad, vpad, dkpad, dvpad) = refs
        t, g = pl.program_id(0), pl.program_id(1)

        @pl.when(g == 0)
        def _():
            kpad[0:pad, :] = jnp.zeros((pad, kv_wide), BF16)
            vpad[0:pad, :] = jnp.zeros((pad, kv_wide), BF16)
            kpad[pad:, :] = k_ref[...]
            vpad[pad:, :] = v_ref[...]
            if gqa:
                dsink_ref[...] = jnp.zeros_like(dsink_ref)
            else:
                dbias_ref[...] = jnp.zeros_like(dbias_ref)

        @pl.when((g == 0) & (t == 0) if gqa else g == 0)
        def _():
            dkpad[...] = jnp.zeros_like(dkpad)
            dvpad[...] = jnp.zeros_like(dvpad)

        start = pl.multiple_of(g * QROWS, QROWS)
        half = lax.broadcasted_iota(jnp.int32, (QROWS, LANES), 1) // HEAD_DIM
        for tt in range(TPS):
            lanes = slice(tt * LANES, (tt + 1) * LANES)
            kv_lanes = slice(0, LANES) if gqa else lanes
            kb = kpad[pl.ds(start, lk), kv_lanes]
            vb = vpad[pl.ds(start, lk), kv_lanes]
            q = q_ref[:, lanes]
            dov = do_ref[:, lanes]
            lv = l_ref[:, lanes]
            if gqa:
                hk = (TPS * t + tt) // 2
                q_rolled = pltpu.roll(q.astype(F32), HEAD_DIM, 1).astype(BF16)
                do_rolled = pltpu.roll(dov, HEAD_DIM, 1)
            dqs = []
            dk_acc = jnp.zeros((lk, LANES), F32)
            dv_acc = jnp.zeros((lk, LANES), F32)
            for e in range(2):
                if gqa:
                    kv_half = hk
                    src = jnp.where(hk == e, q, q_rolled)
                    do_src = jnp.where(hk == e, dov, do_rolled)
                else:
                    kv_half = e
                    src = q
                    do_src = dov
                qm = jnp.where(half == kv_half, src, jnp.zeros_like(src))
                dom = jnp.where(half == kv_half, do_src, 0.0).astype(BF16)
                lcol = jnp.max(jnp.where(half == e, lv, -jnp.inf), axis=-1, keepdims=True)
                sc = _dot_nt(qm * (HEAD_DIM ** -0.5), kb) + bias_ref[2 * tt + e]
                pn = jnp.exp(sc - lcol)
                dp = _dot_nt(dom, vb)
                delta = jnp.sum(pn * dp, axis=-1, keepdims=True)
                ds = pn * (dp - delta)
                if gqa:
                    p_sink = jnp.exp(sink_ref[2 * (TPS * t + tt) + e] - lcol)
                    dsk = -jnp.sum(p_sink * delta, axis=0, keepdims=True)
                    row = 2 * tt + e
                    dsink_ref[0, row:row + 1, :] += jnp.broadcast_to(dsk, (1, LANES))
                else:
                    dbias_ref[2 * tt + e] += ds
                dsb = (ds * (HEAD_DIM ** -0.5)).astype(BF16)
                dqs.append(_dot(dsb, kb))
                dk_acc = dk_acc + _dot_tn(dsb, qm)
                dv_acc = dv_acc + _dot_tn(pn.astype(BF16), dom)
            dkpad[pl.ds(start, lk), kv_lanes] += dk_acc
            dvpad[pl.ds(start, lk), kv_lanes] += dv_acc
            if gqa:
                same = jnp.where(hk == 0, dqs[0], dqs[1])
                other = jnp.where(hk == 0, dqs[1], dqs[0])
                dq_ref[:, lanes] = jnp.where(half == hk, same, pltpu.roll(other, HEAD_DIM, 1)).astype(BF16)
            else:
                dq_ref[:, lanes] = jnp.where(half == 0, dqs[0], dqs[1]).astype(BF16)

        @pl.when((g == n_g - 1) & (t == n_t - 1) if gqa else g == n_g - 1)
        def _():
            dk_ref[...] = dkpad[pad:, :].astype(BF16)
            dv_ref[...] = dvpad[pad:, :].astype(BF16)

    in_specs = [q_spec, k_spec, v_spec, bias_spec] + ([SMEM_SPEC] if gqa else []) + [tile_spec, tile_spec]
    args = [proj, proj, proj, bias] + ([sinks] if gqa else []) + [do, lse]
    if gqa:
        kv_out = pl.BlockSpec((s, LANES), lambda t, g: (0, 0))
        kv_shape = jax.ShapeDtypeStruct((s, LANES), BF16)
        extra_spec = pl.BlockSpec((1, 8, LANES), lambda t, g: (t, 0, 0))
        extra_shape = jax.ShapeDtypeStruct((n_t, 8, LANES), F32)
    else:
        kv_out = pl.BlockSpec((s, kv_wide), lambda t, g: (0, t))
        kv_shape = jax.ShapeDtypeStruct((s, 512), BF16)
        extra_spec = pl.BlockSpec((2 * TPS, QROWS, lk), lambda t, g: (t, 0, 0))
        extra_shape = jax.ShapeDtypeStruct(bias.shape[1:], F32)
    return _call(body, name=name, grid=(n_t, n_g), in_specs=in_specs,
                 out_specs=(tile_spec, kv_out, kv_out, extra_spec),
                 out_shape=(jax.ShapeDtypeStruct((s, 512), BF16), kv_shape, kv_shape, extra_shape), args=args,
                 scratch=[pltpu.VMEM((s + pad, kv_wide), BF16), pltpu.VMEM((s + pad, kv_wide), BF16),
                          pltpu.VMEM((s + pad, kv_wide), F32), pltpu.VMEM((s + pad, kv_wide), F32)],
                 sem=("arbitrary", "arbitrary"), carry=carry)


def _sum_rows8(g):
    n = g.shape[2]

    def body(g_ref, o_ref):
        acc = g_ref[0]
        for j in range(1, N_DEV):
            acc = acc + g_ref[j]
        o_ref[...] = acc

    return pl.pallas_call(
        body, name="sum_small_grads", in_specs=[VMEM_SPEC], out_specs=VMEM_SPEC,
        out_shape=jax.ShapeDtypeStruct((1, n), F32), compiler_params=_params(),
    )(g)


def _ada_weight_grad(sc_t, dmod_cols):
    d = sc_t.shape[0]
    w = dmod_cols.shape[1]
    td = _pick(d, (256, 128))

    def body(sc_ref, dm_ref, o_ref):
        scv = sc_ref[...]
        dmv = dm_ref[...]
        acc = scv[:, 0:1] * dmv[0:1, :]
        for b in range(1, N_DEV):
            acc = acc + scv[:, b:b + 1] * dmv[b:b + 1, :]
        o_ref[...] = acc

    return _call(body, name="ada_weight_grad", grid=(d // td,),
                 in_specs=[pl.BlockSpec((td, N_DEV), lambda i: (i, 0)), pl.BlockSpec((N_DEV, w), lambda i: (0, 0))],
                 out_specs=pl.BlockSpec((td, w), lambda i: (i, 0)), out_shape=jax.ShapeDtypeStruct((d, w), F32),
                 args=[sc_t, dmod_cols], sem=("parallel",))


def _adamw_update(w, gv, m, v):
    nm = ADAM_B1 * m + (1.0 - ADAM_B1) * gv
    nv = ADAM_B2 * v + (1.0 - ADAM_B2) * (gv * gv)
    m_hat = nm / (1.0 - ADAM_B1 ** ADAM_STEP)
    v_hat = nv / (1.0 - ADAM_B2 ** ADAM_STEP)
    return -ADAM_LR * (m_hat / (jnp.sqrt(v_hat) + ADAM_EPS) + ADAM_WD * w), nm, nv


def _adamw(w, g, m, v, name):
    rows, cols = w.shape
    tr = _pick(rows, (256, 176, 128, 88, 64)) if rows > 256 else rows

    def body(w_ref, g_ref, m_ref, v_ref, d_ref, nm_ref, nv_ref):
        d_ref[...], nm_ref[...], nv_ref[...] = _adamw_update(w_ref[...], g_ref[...], m_ref[...], v_ref[...])

    spec = pl.BlockSpec((tr, cols), lambda i: (i, 0))
    shape = jax.ShapeDtypeStruct((rows, cols), F32)
    return _call(body, name=name, grid=(rows // tr,), in_specs=[spec] * 4, out_specs=(spec, spec, spec),
                 out_shape=(shape, shape, shape), args=[w, g, m, v], sem=("parallel",))


def _adamw_from_slots(w, own, slots, m, v, name):
    n_slots, rows, k = slots.shape

    def body(o_ref, s_ref, w_ref, m_ref, v_ref, g_ref, d_ref, nm_ref, nv_ref):
        gv = o_ref[...].astype(F32)
        for j in range(n_slots):
            gv = gv + s_ref[j].astype(F32)
        g_ref[...] = gv
        d_ref[...], nm_ref[...], nv_ref[...] = _adamw_update(w_ref[...], gv, m_ref[...], v_ref[...])

    tr = rows // 2 if rows % 32 == 0 else rows
    spec = pl.BlockSpec((tr, k), lambda i: (i, 0))
    shape = jax.ShapeDtypeStruct((rows, k), F32)
    return _call(body, name=name, grid=(rows // tr,),
                 in_specs=[spec, pl.BlockSpec((n_slots, tr, k), lambda i: (0, i, 0)), spec, spec, spec],
                 out_specs=(spec, spec, spec, spec), out_shape=(shape, shape, shape, shape),
                 args=[own, slots, w, m, v], sem=("parallel",))


def _adamw_small(g, w, m, v, sizes):
    n = w.shape[1]
    offs, off = [], 0
    for size in sizes:
        offs.append(off)
        off += size + (-size % LANES)

    def body(g_ref, w_ref, m_ref, v_ref, *out_refs):
        gv = g_ref[:, 0:n]
        dv, nm, nv = _adamw_update(w_ref[...], gv, m_ref[...], v_ref[...])
        for j, (o, size) in enumerate(zip(offs, sizes)):
            for k, val in enumerate((gv, dv, nm, nv)):
                out_refs[4 * j + k][...] = val[:, o:o + size]

    shapes = [jax.ShapeDtypeStruct((1, size), F32) for size in sizes for _ in range(4)]
    return pl.pallas_call(
        body, name="adamw_small", in_specs=[VMEM_SPEC] * 4, out_specs=tuple([VMEM_SPEC] * len(shapes)),
        out_shape=tuple(shapes), compiler_params=_params(),
    )(g, w, m, v)


SMALL = ("b_ada", "g_pre_ffn1", "g_post_ffn1", "g_pre_mix", "b_in", "sinks_a", "rel_bias_b", "g_grp_a",
         "g_grp_b", "b_out", "g_post_mix", "g_pre_ffn2", "g_post_ffn2")
WEIGHTS = ("w_ada", "b_ada", "g_pre_ffn1", "w_gate1", "w_up1", "w_down1", "g_post_ffn1", "g_pre_mix", "w_in",
           "b_in", "sinks_a", "rel_bias_b", "g_grp_a", "g_grp_b", "w_out", "b_out", "g_post_mix", "g_pre_ffn2",
           "w_gate2", "w_up2", "w_down2", "g_post_ffn2")


def kernel(x, c, w_ada, b_ada, g_pre_ffn1, w_gate1, w_up1, w_down1, g_post_ffn1, g_pre_mix, w_in, b_in, sinks_a, rel_bias_b, g_grp_a, g_grp_b, w_out, b_out, g_post_mix, g_pre_ffn2, w_gate2, w_up2, w_down2, g_post_ffn2, loss_target, m_w_ada, m_b_ada, m_g_pre_ffn1, m_w_gate1, m_w_up1, m_w_down1, m_g_post_ffn1, m_g_pre_mix, m_w_in, m_b_in, m_sinks_a, m_rel_bias_b, m_g_grp_a, m_g_grp_b, m_w_out, m_b_out, m_g_post_mix, m_g_pre_ffn2, m_w_gate2, m_w_up2, m_w_down2, m_g_post_ffn2, v_w_ada, v_b_ada, v_g_pre_ffn1, v_w_gate1, v_w_up1, v_w_down1, v_g_post_ffn1, v_g_pre_mix, v_w_in, v_b_in, v_sinks_a, v_rel_bias_b, v_g_grp_a, v_g_grp_b, v_w_out, v_b_out, v_g_post_mix, v_g_pre_ffn2, v_w_gate2, v_w_up2, v_w_down2, v_g_post_ffn2):
    given = dict(locals())
    weights = {n: given[n] for n in WEIGHTS}
    mom_m = {n: given["m_" + n] for n in WEIGHTS}
    mom_v = {n: given["v_" + n] for n in WEIGHTS}

    me = 4 * lax.axis_index("x") + 2 * lax.axis_index("y") + lax.axis_index("c")
    xs = x[0]
    tgt = loss_target[0]
    d_model = xs.shape[1]
    ada_cols = w_ada.shape[2]

    sh = {"wg1": w_gate1[0].T, "wu1": w_up1[0].T, "wd1": w_down1[0], "win": w_in[0].T, "wo": w_out[0],
          "wg2": w_gate2[0].T, "wu2": w_up2[0].T, "wd2": w_down2[0]}
    sh = {k: v.astype(BF16) for k, v in sh.items()}

    def gather(full=(), new=(), cont=()):
        return _gather_carry([sh[n] for n in full], [sh[n] for n in new], cont)

    bias_a = _alibi_bias()
    rel_m = _rel_index_matrix()
    rel_vec = jnp.dot(rel_bias_b[0], rel_m.T, precision=lax.Precision.HIGHEST)
    bias_b, (wg1, wu1, wd1_part) = _toeplitz_bias(rel_vec.reshape(H_B, 1, SKEW),
                                                  carry=gather(full=("wg1", "wu1"), new=("wd1",)))

    b_cols = lax.dynamic_slice(b_ada, (0, me * ada_cols), (1, ada_cols))
    (sc_all, mod_rows), _ = _ada_forward(c, w_ada[0], b_cols, _Carry([], [], [], lambda *a: None, lambda *a: None))
    mod = mod_rows.reshape(N_MOD, d_model)
    shift1, scale1, gate1, shift2, scale2, gate2, shift3, scale3, gate3 = (mod[i:i + 1] for i in range(N_MOD))

    h1 = _pre_norm(xs, g_pre_ffn1, scale1, shift1, "pre_norm_ffn1")
    (a1, b1, u1), (win_part, wo_part, wd1) = _ffn_up(h1, wg1, wu1, "ffn_up_ffn1",
                                                     carry=gather(new=("win", "wo"), cont=(wd1_part,)))
    (y1, x1, h2), (wg2_part, win, wo) = _mm_nn(
        [(u1, wd1)], "ffn_down_ffn1", F32, carry=gather(new=("wg2",), cont=(win_part, wo_part)),
        tail=_tail_post_pre(xs, g_post_ffn1, gate1, 0.5, g_pre_mix, scale2, shift2))

    proj, (wg2,) = _mm_nt(h2, win, "in_proj", BF16, bias=b_in, carry=gather(cont=(wg2_part,)))
    sinks = sinks_a[0]
    cfg_a = dict(n_back=BACK_A, gqa=True, q_col=0, k_col=QA // LANES, v_col=(QA + KVA) // LANES, TPS=TPS_A)
    cfg_b = dict(n_back=BACK_B, gqa=False, q_col=(QA + 2 * KVA) // LANES, k_col=(QA + 2 * KVA + QB) // LANES,
                 v_col=(QA + 2 * KVA + 2 * QB) // LANES, TPS=TPS_B)
    (oa, lse_a), (wu2_part, wd2_part) = _attention_fwd(proj, bias_a, sinks, name="attn_a",
                                                       carry=gather(new=("wu2", "wd2")), **cfg_a)
    (ob, lse_b), (wu2, wd2) = _attention_fwd(proj, bias_b, None, name="attn_b",
                                             carry=gather(cont=(wu2_part, wd2_part)), **cfg_b)
    ycat = _group_norm_cat(oa, ob, g_grp_a, g_grp_b)
    ymix, x2, h3 = _mm_nn([(ycat, wo)], "out_proj", F32, bias=b_out,
                          tail=_tail_post_pre(x1, g_post_mix, gate2, 1.0, g_pre_ffn2, scale3, shift3))

    a3, b3, u3 = _ffn_up(h3, wg2, wu2, "ffn_up_ffn2")

    flights, own = {}, {}

    def grad_pair(key, a_mat, b_mat, name):
        part, own[key] = _mm_tn_pair(a_mat, b_mat, name)
        return part

    def scatter_start(tag, after_vec, **parts):
        names = list(parts)
        sems, p_thru, lands, token = _scatter_start([parts[n] for n in names], "scatter_start_" + tag)
        flights[tag] = (names, sems, p_thru, lands)
        return after_vec + token[0:1, 0:1]

    dx3, dy, loss_part, s1 = _mm_nn([(u3, wd2)], "ffn_down_ffn2", None,
                                    tail=_tail_post_loss(x2, tgt, g_post_ffn2, gate3, 0.5))
    da, db = _ffn_down_bwd(dy, wd2, a3, b3, "ffn_down_bwd_ffn2")
    dwd2 = grad_pair("wd2", u3, dy, "grad_wd_ffn2")
    dwg2 = grad_pair("wg2", da, h3, "grad_wg_ffn2")
    dwu2 = grad_pair("wu2", db, h3, "grad_wu_ffn2")
    g_pre_tied = scatter_start("ffn2", g_pre_ffn2, wd2=dwd2, wg2=dwg2, wu2=dwu2)
    dx2, dymix, s2, s3, s1m, db_out = _mm_nn(
        [(da, wg2), (db, wu2)], "ffn_up_bwd_ffn2", None,
        tail=_tail_pre_post_bwd(x2, dx3, ymix, g_pre_tied, scale3, g_post_mix, gate2, 1.0))
    sm3 = dict(shift=s3, scale=s2 * g_pre_ffn2, gate=0.5 * g_post_ffn2 * s1,
               g_pre=(1.0 + scale3) * s2, g_post=(0.5 * gate3) * s1)

    dycat = _mm_nt(dymix, wo, "out_proj_bwd", F32)
    dwo = grad_pair("wo", ycat, dymix, "grad_wo")
    doa, dob, dg_a, dg_b = _group_norm_bwd(dycat, oa, ob, g_grp_a, g_grp_b)
    dqa, dka, dva, dsink = _attention_bwd(proj, bias_a, sinks, doa, lse_a, name="attn_a_bwd", **cfg_a)
    dqb, dkb, dvb, dbias = _attention_bwd(proj, bias_b, None, dob, lse_b, name="attn_b_bwd", **cfg_b)
    dproj = jnp.concatenate([dqa, dka, dva, dqb, dkb, dvb], axis=1)
    dwin, own["win"], db_in = _mm_tn_pair(dproj, h2, "grad_win", col_sums=True)
    g_pre_tied = scatter_start("mix", g_pre_mix, wo=dwo, win=dwin)
    dx1, dy, s2m, s3m, s1, _ = _mm_nn(
        [(dproj, win)], "in_proj_bwd", None,
        tail=_tail_pre_post_bwd(x1, dx2, y1, g_pre_tied, scale2, g_post_ffn1, gate1, 0.5))
    d_rel = jnp.dot(_diagonal_sums(dbias).reshape(H_B, SKEW), rel_m, precision=lax.Precision.HIGHEST)
    d_sinks = dsink[:, :2 * TPS_A, 0].reshape(1, H_A)

    da, db = _ffn_down_bwd(dy, wd1, a1, b1, "ffn_down_bwd_ffn1")
    dwd1 = grad_pair("wd1", u1, dy, "grad_wd_ffn1")
    dwg1 = grad_pair("wg1", da, h1, "grad_wg_ffn1")
    dwu1 = grad_pair("wu1", db, h1, "grad_wu_ffn1")
    g_pre_tied = scatter_start("ffn1", g_pre_ffn1, wd1=dwd1, wg1=dwg1, wu1=dwu1)
    dx0, s2, s3 = _mm_nn([(da, wg1), (db, wu1)], "ffn_up_bwd_ffn1", None,
                         tail=_tail_pre_bwd(xs, dx1, g_pre_tied, scale1))
    sm1 = dict(shift=s3, scale=s2 * g_pre_ffn1, gate=0.5 * g_post_ffn1 * s1,
               g_pre=(1.0 + scale1) * s2, g_post=(0.5 * gate1) * s1)

    dmod = jnp.concatenate([sm1["shift"], sm1["scale"], sm1["gate"],
                            s3m, s2m * g_pre_mix, g_post_mix * s1m,
                            sm3["shift"], sm3["scale"], sm3["gate"]], axis=1)
    small_parts = {
        "b_ada": dmod, "g_pre_ffn1": sm1["g_pre"], "g_post_ffn1": sm1["g_post"],
        "g_pre_mix": (1.0 + scale2) * s2m, "b_in": db_in, "sinks_a": d_sinks,
        "rel_bias_b": d_rel.reshape(1, H_B * N_REL), "g_grp_a": dg_a, "g_grp_b": dg_b, "b_out": db_out,
        "g_post_mix": gate2 * s1m, "g_pre_ffn2": sm3["g_pre"], "g_post_ffn2": sm3["g_post"]}
    sizes = [small_parts[n].shape[1] for n in SMALL]

    def pack(parts):
        cells = []
        for p in parts:
            cells.append(p)
            if p.shape[1] % LANES:
                cells.append(jnp.zeros((1, -p.shape[1] % LANES), F32))
        return jnp.concatenate(cells, axis=1)

    packed = pack([small_parts[n] for n in SMALL] + [loss_part])
    n_packed = packed.shape[1]
    small_sems, packed_thru, small_land, small_token = _small_gather_start(packed)

    out_g, out_d, out_m, out_v = {}, {}, {}, {}
    groups = (("ffn2", (("w_gate2", "wg2", True), ("w_up2", "wu2", True), ("w_down2", "wd2", False))),
              ("mix", (("w_in", "win", True), ("w_out", "wo", False))),
              ("ffn1", (("w_gate1", "wg1", True), ("w_up1", "wu1", True), ("w_down1", "wd1", False))))
    after = small_token
    for tag, members in groups:
        names, sems, p_thru, lands = flights[tag]
        _, l_done = _scatter_wait(sems, p_thru, lands, after, "scatter_wait_" + tag)
        slots = dict(zip(names, l_done))
        for n, key, transposed in members:
            view = (lambda t: t.T) if transposed else (lambda t: t)
            res = _adamw_from_slots(view(weights[n][0]), own[key], slots[key], view(mom_m[n][0]),
                                    view(mom_v[n][0]), "adamw_" + n)
            out_g[n], out_d[n], out_m[n], out_v[n] = (view(t)[None] for t in res)
            after = res[3]

    packed_done, small_land = _small_gather_wait(small_sems, packed_thru, small_land, after)
    gathered = lax.dynamic_update_slice(small_land, packed_done[None], (me, 0, 0))
    small_sum = _sum_rows8(gathered)
    loss = small_sum[0, n_packed - LANES]
    dmod_cols = lax.dynamic_slice(gathered.reshape(N_DEV, n_packed), (0, me * ada_cols), (N_DEV, ada_cols))
    g_ada = _ada_weight_grad(sc_all.reshape(N_DEV, d_model).T, dmod_cols)
    d_, m_, v_ = _adamw(w_ada[0], g_ada, m_w_ada[0], v_w_ada[0], "adamw_w_ada")
    out_g["w_ada"], out_d["w_ada"], out_m["w_ada"], out_v["w_ada"] = g_ada[None], d_[None], m_[None], v_[None]

    small_out = _adamw_small(small_sum, *(pack([tree[n].reshape(1, -1) for n in SMALL])
                                          for tree in (weights, mom_m, mom_v)), sizes)
    for j, n in enumerate(SMALL):
        shape = weights[n].shape
        out_g[n], out_d[n], out_m[n], out_v[n] = (t.reshape(shape) for t in small_out[4 * j:4 * j + 4])

    return (loss, dx0[None], *[out_g[n] for n in WEIGHTS], *[out_d[n] for n in WEIGHTS],
            *[out_m[n] for n in WEIGHTS], *[out_v[n] for n in WEIGHTS])
```

```python
import numpy as np
import jax
import jax.numpy as jnp
from jax import lax
from jax.experimental import pallas as pl
from jax.experimental.pallas import tpu as pltpu

F32 = jnp.float32
BF16 = jnp.bfloat16
MESH = pl.DeviceIdType.MESH
ANY = pl.BlockSpec(memory_space=pl.ANY)
VMEM_SPEC = pl.BlockSpec(memory_space=pltpu.VMEM)
SMEM_SPEC = pl.BlockSpec(memory_space=pltpu.SMEM)

N_DEV = 8
CHUNK = 64
HEAD_DIM = 64
LANES = 128
H_A, KV_A, H_B = 8, 2, 8
BACK_A, BACK_B = 2, 8
REL_CLIP = 128
N_REL = 2 * REL_CLIP + 1
QA, KVA, QB = H_A * HEAD_DIM, KV_A * HEAD_DIM, H_B * HEAD_DIM
D_IN = QA + 2 * KVA + 3 * QB
N_MOD = 9
EPS = 1e-6
NEG_INF = -1e30
QG = 4
QROWS = QG * CHUNK
TPS_A, TPS_B = 4, 2
SKEW = 1024
ADAM_LR, ADAM_B1, ADAM_B2, ADAM_EPS, ADAM_WD, ADAM_STEP = 0.001, 0.9, 0.999, 1e-08, 0.01, 10
VMEM_LIMIT = 56 * 2 ** 20


def _pick(n, cands):
    for c in cands:
        if n % c == 0:
            return c
    return n


def _pieces(n, width=2 * LANES):
    return [(lo, min(lo + width, n)) for lo in range(0, n, width)]


def _params(sem=None):
    return pltpu.CompilerParams(dimension_semantics=sem, vmem_limit_bytes=VMEM_LIMIT)


def _dot_nt(a, b):
    return lax.dot_general(a, b, (((1,), (1,)), ((), ())), preferred_element_type=F32)


def _dot_tn(a, b):
    return lax.dot_general(a, b, (((0,), (0,)), ((), ())), preferred_element_type=F32)


def _dot(a, b):
    return jnp.dot(a, b, preferred_element_type=F32)


def _sigmoid(a):
    return 0.5 * (jnp.tanh(0.5 * a) + 1.0)


def _mesh_pos():
    return lax.axis_index("x"), lax.axis_index("y"), lax.axis_index("c")


def _peer(x, y, c, r):
    px = 1 - x if r & 4 else x
    py = 1 - y if r & 2 else y
    pc = 1 - c if r & 1 else c
    return px, py, pc


class _Carry:
    def __init__(self, ins, out_shapes, scratch, start, finish, aliases=()):
        self.ins, self.out_shapes, self.scratch = list(ins), list(out_shapes), list(scratch)
        self.start, self.finish, self.aliases = start, finish, list(aliases)


def _call(body, *, name, grid, in_specs, out_specs, out_shape, args, scratch=(), sem=None, carry=None):
    single = not isinstance(out_shape, (tuple, list))
    out_specs = (out_specs,) if single else tuple(out_specs)
    out_shape = (out_shape,) if single else tuple(out_shape)
    if carry is None:
        res = pl.pallas_call(body, name=name, grid=grid, in_specs=list(in_specs), out_specs=out_specs,
                             out_shape=out_shape, scratch_shapes=list(scratch), compiler_params=_params(sem))(*args)
        return res[0] if single else res
    n_in, n_out, n_s = len(in_specs), len(out_shape), len(scratch)
    ci, co = len(carry.ins), len(carry.out_shapes)

    def wrapped(*refs):
        ins, cins = refs[:n_in], refs[n_in:n_in + ci]
        outs = refs[n_in + ci:n_in + ci + n_out]
        couts = refs[n_in + ci + n_out:n_in + ci + n_out + co]
        scr = refs[n_in + ci + n_out + co:n_in + ci + n_out + co + n_s]
        cscr = refs[n_in + ci + n_out + co + n_s:]
        first, last = None, None
        for ax, n in enumerate(grid):
            f, l = pl.program_id(ax) == 0, pl.program_id(ax) == n - 1
            first = f if first is None else first & f
            last = l if last is None else last & l
        pl.when(first)(lambda: carry.start(cins, couts, cscr))
        body(*ins, *outs, *scr)
        pl.when(last)(lambda: carry.finish(cins, couts, cscr))

    res = pl.pallas_call(
        wrapped, name=name, grid=grid, in_specs=list(in_specs) + [ANY] * ci, out_specs=out_specs + (ANY,) * co,
        out_shape=out_shape + tuple(carry.out_shapes), scratch_shapes=list(scratch) + carry.scratch,
        input_output_aliases={n_in + i: n_out + o for i, o in carry.aliases},
        compiler_params=_params(("arbitrary",) * len(grid)))(*args, *carry.ins)
    main = res[:n_out]
    return (main[0] if single else main), res[n_out:]


PASS_PIECES = 4
ROW_TILE = 16


def _row_pieces(rows):
    tiles, pieces, off = rows // ROW_TILE, [], 0
    for i in range(PASS_PIECES):
        n = (tiles + i) // PASS_PIECES * ROW_TILE
        if n:
            pieces.append((off, n))
        off += n
    assert off == rows
    return pieces


def _gather_carry(full=(), new=(), cont=()):
    full, new, cont = list(full), list(new), list(cont)
    n_full, n_one = len(full), len(full) + len(new)
    n_w = n_one + len(cont)
    rows = [s.shape[0] for s in full + new] + [b.shape[0] // N_DEV for b in cont]
    shapes = [jax.ShapeDtypeStruct((N_DEV * s.shape[0], s.shape[1]), s.dtype) for s in full + new]
    shapes += [jax.ShapeDtypeStruct(b.shape, b.dtype) for b in cont]
    halves = [((0, r // 2), (r // 2, r // 2)) for r in rows]
    pieces = [_row_pieces(r) for r in rows]

    def plan(ins, outs, scr):
        send_sems, recv_sems, local_sems = scr
        x, y, c = _mesh_pos()
        me, sibling = (x, y, c), (x, y, 1 - c)
        x_chip, y_chip, far_chip = (1 - x, y), (x, 1 - y), (1 - x, 1 - y)

        def block(buf, w, chip, core, span=None):
            off, n = (0, rows[w]) if span is None else span
            start = (4 * chip[0] + 2 * chip[1] + core) * rows[w] + off
            return buf.at[pl.ds(pl.multiple_of(start, 16), n), :]

        def copy(w, k, chip, core, to, span=None, held=None, shard=None, p=0):
            if shard is None:
                src = block(outs[w] if held is None else held, w, chip, core, span)
            else:
                src = shard if span is None else shard.at[pl.ds(span[0], span[1]), :]
            return pltpu.make_async_remote_copy(
                src_ref=src, dst_ref=block(outs[w], w, chip, core, span), send_sem=send_sems.at[w, k, p],
                recv_sem=recv_sems.at[w, k, p], device_id=to, device_id_type=MESH)

        def spans(w, k):
            return pieces[w] if k == 7 else [None]

        def to_sibling(w, k, chip, held=None, shard=None):
            return [copy(w, k, chip, c, sibling, span, held, shard, p) for p, span in enumerate(spans(w, k))]

        def from_sibling(w, k, chip):
            return [copy(w, k, chip, 1 - c, me, span, p=p) for p, span in enumerate(spans(w, k))]

        def stage_one(w):
            return [copy(w, 1, (x, y), c, (*x_chip, c), shard=ins[w]), copy(w, 2, (x, y), c, (*y_chip, c), shard=ins[w]),
                    *to_sibling(w, 0, (x, y), shard=ins[w])]

        def stage_two(w, held):
            return [copy(w, 5, x_chip, c, (*y_chip, c), halves[w][0], held),
                    copy(w, 6, y_chip, c, (*x_chip, c), halves[w][1], held),
                    *to_sibling(w, 3, x_chip, held), *to_sibling(w, 4, y_chip, held)]

        mine = [pltpu.make_async_copy(ins[w], block(outs[w], w, (x, y), c), local_sems.at[w]) for w in range(n_one)]
        return c, me, sibling, far_chip, copy, to_sibling, from_sibling, stage_one, stage_two, mine

    def start(ins, outs, scr):
        _, _, _, _, _, _, _, stage_one, stage_two, mine = plan(ins, outs, scr)
        for w in range(n_one):
            for cp in stage_one(w):
                cp.start()
        for w in range(n_one, n_w):
            for cp in stage_two(w, ins[w]):
                cp.start()
        for cp in mine:
            cp.start()

    def finish(ins, outs, scr):
        c, me, sibling, far_chip, copy, to_sibling, from_sibling, stage_one, stage_two, mine = plan(ins, outs, scr)
        x_chip, y_chip = (far_chip[0], me[1]), (me[0], far_chip[1])
        sent = []

        def land_one(w):
            copy(w, 1, x_chip, c, me).wait_recv()
            copy(w, 2, y_chip, c, me).wait_recv()

        def land_two(w):
            copy(w, 5, far_chip, c, me, halves[w][0]).wait_recv()
            copy(w, 6, far_chip, c, me, halves[w][1]).wait_recv()
            for cp in to_sibling(w, 7, far_chip):
                cp.start()
                sent.append(cp)

        for w in range(n_full):
            land_one(w)
            for cp in stage_two(w, outs[w]):
                cp.start()
                sent.append(cp)
        for w in range(n_one, n_w):
            land_two(w)
            sent.extend(stage_two(w, ins[w]))
        for w in range(n_full, n_one):
            land_one(w)
        for w in range(n_full):
            land_two(w)
        for w in range(n_one):
            for cp in from_sibling(w, 0, me[:2]):
                cp.wait_recv()
            sent.extend(stage_one(w))
        for w in list(range(n_full)) + list(range(n_one, n_w)):
            for k, chip in ((3, x_chip), (4, y_chip), (7, far_chip)):
                for cp in from_sibling(w, k, chip):
                    cp.wait_recv()
        for cp in sent:
            cp.wait_send()
        for cp in mine:
            cp.wait()

    return _Carry(
        full + new + cont, shapes,
        [pltpu.SemaphoreType.DMA((n_w, N_DEV, PASS_PIECES)), pltpu.SemaphoreType.DMA((n_w, N_DEV, PASS_PIECES)),
         pltpu.SemaphoreType.DMA((max(n_one, 1),))], start, finish,
        aliases=[(w, w) for w in range(n_one, n_w)])


HBM_SPEC = pl.BlockSpec(memory_space=pltpu.HBM)
SEM_SPEC = pl.BlockSpec(memory_space=pltpu.SEMAPHORE)
N_CHIP = N_DEV // 2


def _scatter_copy(part_ref, land_ref, send_sem, recv_sem, r, rows):
    x, y, c = _mesh_pos()
    px, py, _ = _peer(x, y, c, 2 * r)
    src = part_ref.at[pl.ds(pl.multiple_of((2 * px + py) * rows, 16), rows), :]
    return pltpu.make_async_remote_copy(
        src_ref=src, dst_ref=land_ref.at[r - 1], send_sem=send_sem, recv_sem=recv_sem,
        device_id=(px, py, c), device_id_type=MESH)


def _scatter_order(n_w):
    return [(w, r) for r in (3, 2, 1) for w in range(n_w)]


def _scatter_start(parts, name):
    n_w = len(parts)
    rows = [p.shape[0] // N_CHIP for p in parts]
    order = _scatter_order(n_w)
    lands = [pltpu.with_memory_space_constraint(lax.empty((N_CHIP - 1, r, p.shape[1]), p.dtype), pltpu.HBM)
             for r, p in zip(rows, parts)]

    def body(*refs):
        part_refs, land_refs = refs[:n_w], refs[n_w:2 * n_w]
        sems = refs[2 * n_w:2 * n_w + 2 * len(order)]
        token = refs[-1]
        for j, (w, r) in enumerate(order):
            _scatter_copy(part_refs[w], land_refs[w], sems[2 * j], sems[2 * j + 1], r, rows[w]).start()
        token[...] = jnp.zeros_like(token)

    n_sem = 2 * len(order)
    res = pl.pallas_call(
        body, name=name,
        out_shape=(*[pltpu.SemaphoreType.DMA(())] * n_sem, *[pltpu.HBM(p.shape, p.dtype) for p in parts],
                   *[pltpu.HBM(l.shape, l.dtype) for l in lands], jax.ShapeDtypeStruct((8, LANES), F32)),
        in_specs=[HBM_SPEC] * (2 * n_w), out_specs=(*[SEM_SPEC] * n_sem, *[HBM_SPEC] * (2 * n_w), VMEM_SPEC),
        input_output_aliases={i: n_sem + i for i in range(2 * n_w)},
        compiler_params=pltpu.CompilerParams(has_side_effects=pltpu.SideEffectType.DATAFLOW_SIDE_EFFECTING),
    )(*[pltpu.with_memory_space_constraint(p, pltpu.HBM) for p in parts], *lands)
    return (list(res[:n_sem]), list(res[n_sem:n_sem + n_w]), list(res[n_sem + n_w:n_sem + 2 * n_w]), res[-1])


def _scatter_wait(sems, parts, lands, after, name):
    n_w = len(parts)
    rows = [p.shape[0] // N_CHIP for p in parts]
    order = _scatter_order(n_w)

    def body(*refs):
        part_refs, land_refs = refs[:n_w], refs[n_w:2 * n_w]
        sem_refs = refs[2 * n_w:2 * n_w + 2 * len(order)]
        for j, (w, r) in enumerate(order):
            cp = _scatter_copy(part_refs[w], land_refs[w], sem_refs[2 * j], sem_refs[2 * j + 1], r, rows[w])
            cp.wait_send()
            cp.wait_recv()

    res = pl.pallas_call(
        body, name=name,
        out_shape=(*[pltpu.HBM(p.shape, p.dtype) for p in parts], *[pltpu.HBM(l.shape, l.dtype) for l in lands]),
        in_specs=[HBM_SPEC] * (2 * n_w) + [SEM_SPEC] * len(sems) + [ANY],
        out_specs=tuple([HBM_SPEC] * (2 * n_w)),
        input_output_aliases={i: i for i in range(2 * n_w)},
        compiler_params=pltpu.CompilerParams(has_side_effects=pltpu.SideEffectType.DATAFLOW_SIDE_EFFECTING),
    )(*parts, *lands, *sems, after)
    return list(res[:n_w]), list(res[n_w:])


def _small_copy(v_ref, land_ref, send_sem, recv_sem, r):
    x, y, c = _mesh_pos()
    px, py, pc = _peer(x, y, c, r)
    return pltpu.make_async_remote_copy(
        src_ref=v_ref, dst_ref=land_ref.at[4 * x + 2 * y + c], send_sem=send_sem, recv_sem=recv_sem,
        device_id=(px, py, pc), device_id_type=MESH)


def _small_gather_start(v):
    land = pltpu.with_memory_space_constraint(lax.empty((N_DEV,) + v.shape, v.dtype), pltpu.HBM)

    def body(v_ref, land_ref, *rest):
        sems, token = rest[:2 * (N_DEV - 1)], rest[-1]
        for r in range(1, N_DEV):
            _small_copy(v_ref, land_ref, sems[2 * r - 2], sems[2 * r - 1], r).start()
        token[...] = jnp.zeros_like(token)

    n_sem = 2 * (N_DEV - 1)
    res = pl.pallas_call(
        body, name="small_gather_start",
        out_shape=(*[pltpu.SemaphoreType.DMA(())] * n_sem, pltpu.HBM(v.shape, v.dtype),
                   pltpu.HBM(land.shape, land.dtype), jax.ShapeDtypeStruct((8, LANES), F32)),
        in_specs=[HBM_SPEC, HBM_SPEC], out_specs=(*[SEM_SPEC] * n_sem, HBM_SPEC, HBM_SPEC, VMEM_SPEC),
        input_output_aliases={0: n_sem, 1: n_sem + 1},
        compiler_params=pltpu.CompilerParams(has_side_effects=pltpu.SideEffectType.DATAFLOW_SIDE_EFFECTING),
    )(pltpu.with_memory_space_constraint(v, pltpu.HBM), land)
    return list(res[:n_sem]), res[n_sem], res[n_sem + 1], res[-1]


def _small_gather_wait(sems, v, land, after):
    def body(v_ref, land_ref, *rest):
        for r in range(1, N_DEV):
            cp = _small_copy(v_ref, land_ref, rest[2 * r - 2], rest[2 * r - 1], r)
            cp.wait_send()
            x, y, c = _mesh_pos()
            px, py, pc = _peer(x, y, c, r)
            pltpu.make_async_remote_copy(
                src_ref=v_ref, dst_ref=land_ref.at[4 * px + 2 * py + pc], send_sem=rest[2 * r - 2],
                recv_sem=rest[2 * r - 1], device_id=(px, py, pc), device_id_type=MESH).wait_recv()

    res = pl.pallas_call(
        body, name="small_gather_wait",
        out_shape=(pltpu.HBM(v.shape, v.dtype), pltpu.HBM(land.shape, land.dtype)),
        in_specs=[HBM_SPEC, HBM_SPEC] + [SEM_SPEC] * len(sems) + [ANY], out_specs=(HBM_SPEC, HBM_SPEC),
        input_output_aliases={0: 0, 1: 1},
        compiler_params=pltpu.CompilerParams(has_side_effects=pltpu.SideEffectType.DATAFLOW_SIDE_EFFECTING),
    )(v, land, *sems, after)
    return res[0], res[1]


def _ada_forward(c_row, w_ada, b_cols, carry):
    d = c_row.shape[1]
    wcols = w_ada.shape[1]
    ci, co = len(carry.ins), len(carry.out_shapes)

    def body(*refs):
        c_ref, w_ref, b_ref = refs[:3]
        cins = refs[3:3 + ci]
        sc_ref, mod_ref = refs[3 + ci:5 + ci]
        couts = refs[5 + ci:5 + ci + co]
        rows_ref, send_sems, recv_sems = refs[5 + ci + co:8 + ci + co]
        cscr = refs[8 + ci + co:]
        carry.start(cins, couts, cscr)
        x, y, c = _mesh_pos()
        me = 4 * x + 2 * y + c
        cv = c_ref[...]
        sc_ref[me] = cv * _sigmoid(cv)

        sends = []
        for r in range(1, N_DEV):
            px, py, pc = _peer(x, y, c, r)
            cp = pltpu.make_async_remote_copy(
                src_ref=sc_ref.at[me], dst_ref=sc_ref.at[me], send_sem=send_sems.at[0, r - 1],
                recv_sem=recv_sems.at[0, r - 1], device_id=(px, py, pc), device_id_type=MESH)
            cp.start()
            sends.append(cp)
        for r in range(1, N_DEV):
            px, py, pc = _peer(x, y, c, r)
            pid = 4 * px + 2 * py + pc
            pltpu.make_async_remote_copy(
                src_ref=sc_ref.at[pid], dst_ref=sc_ref.at[pid], send_sem=send_sems.at[0, r - 1],
                recv_sem=recv_sems.at[0, r - 1], device_id=(px, py, pc), device_id_type=MESH).wait_recv()
        for cp in sends:
            cp.wait_send()

        sc_all = jnp.concatenate([sc_ref[j] for j in range(N_DEV)], axis=0)
        rows = _dot(sc_all.astype(BF16), w_ref[...].astype(BF16)) + b_ref[...]
        for j in range(N_DEV):
            rows_ref[j] = rows[j:j + 1, :]
        mod_ref[me] = rows_ref[me]

        sends = []
        for r in range(1, N_DEV):
            px, py, pc = _peer(x, y, c, r)
            pid = 4 * px + 2 * py + pc
            cp = pltpu.make_async_remote_copy(
                src_ref=rows_ref.at[pid], dst_ref=mod_ref.at[me], send_sem=send_sems.at[1, r - 1],
                recv_sem=recv_sems.at[1, r - 1], device_id=(px, py, pc), device_id_type=MESH)
            cp.start()
            sends.append(cp)
        for r in range(1, N_DEV):
            px, py, pc = _peer(x, y, c, r)
            pid = 4 * px + 2 * py + pc
            pltpu.make_async_remote_copy(
                src_ref=rows_ref.at[pid], dst_ref=mod_ref.at[pid], send_sem=send_sems.at[1, r - 1],
                recv_sem=recv_sems.at[1, r - 1], device_id=(px, py, pc), device_id_type=MESH).wait_recv()
        for cp in sends:
            cp.wait_send()
        carry.finish(cins, couts, cscr)

    res = pl.pallas_call(
        body, name="ada_forward",
        out_shape=(jax.ShapeDtypeStruct((N_DEV, 1, d), F32), jax.ShapeDtypeStruct((N_DEV, 1, wcols), F32),
                   *carry.out_shapes),
        in_specs=[VMEM_SPEC, VMEM_SPEC, VMEM_SPEC] + [ANY] * ci, out_specs=(VMEM_SPEC, VMEM_SPEC) + (ANY,) * co,
        scratch_shapes=[pltpu.VMEM((N_DEV, 1, wcols), F32), pltpu.SemaphoreType.DMA((2, N_DEV - 1)),
                        pltpu.SemaphoreType.DMA((2, N_DEV - 1))] + carry.scratch,
        compiler_params=_params(),
    )(c_row, w_ada, b_cols, *carry.ins)
    return res[:2], res[2:]


def _mm_nt(a, b, name, out_dtype, bias=None, carry=None):
    m, k = a.shape
    n = b.shape[0]
    tm = _pick(m, (512, 256, 128))
    tn = _pick(n, (1408, 1152, 1024, 768, 512, 256, 128))

    def body(*refs):
        acc = _dot_nt(refs[0][...], refs[1][...])
        if bias is not None:
            acc = acc + refs[2][...]
        refs[-1][...] = acc.astype(out_dtype)

    in_specs = [pl.BlockSpec((tm, k), lambda j, i: (i, 0)), pl.BlockSpec((tn, k), lambda j, i: (j, 0))]
    args = [a, b]
    if bias is not None:
        in_specs.append(pl.BlockSpec((1, tn), lambda j, i: (0, j)))
        args.append(bias)
    return _call(body, name=name, grid=(n // tn, m // tm), in_specs=in_specs,
                 out_specs=pl.BlockSpec((tm, tn), lambda j, i: (i, j)),
                 out_shape=jax.ShapeDtypeStruct((m, n), out_dtype), args=args,
                 sem=("parallel", "parallel"), carry=carry)


class _Tail:
    def __init__(self, rows, vecs, outs, fn):
        self.rows, self.vecs, self.outs, self.fn = list(rows), list(vecs), list(outs), fn


def _mm_nn(pairs, name, out_dtype, bias=None, carry=None, tail=None):
    m, k = pairs[0][0].shape
    n = pairs[0][1].shape[1]
    n_p = len(pairs)
    tm = _pick(m, (512, 256, 128))
    tk = k if n_p == 1 else _pick(k, (1408, 1152, 1024, 768, 512, 256, 128))
    nk = k // tk
    n_b = 0 if bias is None else 1
    n_r, n_v = (len(tail.rows), len(tail.vecs)) if tail else (0, 0)
    n_in = 2 * n_p + n_b + n_r + n_v
    n_main = 0 if out_dtype is None else 1

    def finish(acc, refs, first_tile):
        if bias is not None:
            acc = acc + refs[2 * n_p][...]
        outs = refs[n_in:-1]
        if n_main:
            outs[0][...] = acc.astype(out_dtype)
        if tail is None:
            return
        rows = [r[...] for r in refs[2 * n_p + n_b:2 * n_p + n_b + n_r]]
        vecs = [v[...] for v in refs[2 * n_p + n_b + n_r:n_in]]
        vals = tail.fn(acc, rows, vecs)
        for ref, val, (dtype, kind) in zip(outs[n_main:], vals, tail.outs):
            if kind == "row":
                ref[...] = val.astype(dtype)
            else:
                @pl.when(first_tile)
                def _(ref=ref):
                    ref[...] = jnp.zeros_like(ref)

                ref[...] += val

    def body(*refs):
        acc_ref = refs[-1]
        kk, i = pl.program_id(0), pl.program_id(1)
        part = _dot(refs[0][...], refs[1][...])
        for p in range(1, n_p):
            part = part + _dot(refs[2 * p][...], refs[2 * p + 1][...])
        if nk == 1:
            finish(part, refs, i == 0)
            return
        rows = pl.ds(pl.multiple_of(i * tm, tm), tm)

        @pl.when(kk == 0)
        def _():
            acc_ref[rows, :] = part

        if nk > 2:
            @pl.when((kk > 0) & (kk < nk - 1))
            def _():
                acc_ref[rows, :] += part

        @pl.when(kk == nk - 1)
        def _():
            finish(acc_ref[rows, :] + part, refs, i == 0)

    def last_only(kk, i):
        return (jnp.where(kk == nk - 1, i, 0), 0)

    row_spec = pl.BlockSpec((tm, n), last_only)
    vec_spec = pl.BlockSpec((1, n), lambda kk, i: (0, 0))
    in_specs, args = [], []
    for a, b in pairs:
        in_specs += [pl.BlockSpec((tm, tk), lambda kk, i: (i, kk)), pl.BlockSpec((tk, n), lambda kk, i: (kk, 0))]
        args += [a, b]
    if bias is not None:
        in_specs.append(vec_spec)
        args.append(bias)
    out_specs = [row_spec] * n_main
    out_shape = [jax.ShapeDtypeStruct((m, n), out_dtype)] if n_main else []
    if tail:
        in_specs += [row_spec] * n_r + [vec_spec] * n_v
        args += tail.rows + tail.vecs
        for dtype, kind in tail.outs:
            if kind == "row":
                out_specs.append(row_spec)
                out_shape.append(jax.ShapeDtypeStruct((m, n), dtype))
            else:
                width = n if kind == "sum" else 1
                out_specs.append(pl.BlockSpec((1, width), lambda kk, i: (0, 0)))
                out_shape.append(jax.ShapeDtypeStruct((1, width), dtype))
    if tail is None:
        out_specs, out_shape = out_specs[0], out_shape[0]
    return _call(body, name=name, grid=(nk, m // tm), in_specs=in_specs, out_specs=out_specs,
                 out_shape=out_shape, args=args,
                 scratch=[pltpu.VMEM((m, n) if nk > 1 else (8, LANES), F32)],
                 sem=("arbitrary", "arbitrary"), carry=carry)


def _rms(v):
    return lax.rsqrt(jnp.mean(v * v, axis=-1, keepdims=True) + EPS)


def _col(v):
    return jnp.sum(v, axis=0, keepdims=True)


def _tail_post_pre(x, g_post, gate, weight, g_pre, scale, shift):
    def fn(y, rows, vecs):
        (xv,), (gp, gt, g, sc, sh) = rows, vecs
        xo = xv + (weight * gt) * ((y * _rms(y)) * gp)
        return xo, ((xo * _rms(xo)) * g) * (1.0 + sc) + sh

    return _Tail([x], [g_post, gate, g_pre, scale, shift], [(F32, "row"), (BF16, "row")], fn)


def _tail_post_loss(x, target, g, gate, weight):
    def fn(y, rows, vecs):
        (xv, tv), (gv, gt) = rows, vecs
        r = _rms(y)
        yn = y * r
        err = (xv + (weight * gt) * (yn * gv)) - tv
        do = err * (1.0 / y.shape[1])
        dyn = do * ((weight * gt) * gv)
        dy = r * (dyn - yn * jnp.mean(dyn * yn, axis=-1, keepdims=True))
        return do, dy, 0.5 * _col(jnp.mean(err * err, axis=-1, keepdims=True)), _col(do * yn)

    return _Tail([x, target], [g, gate], [(F32, "row"), (BF16, "row"), (F32, "one"), (F32, "sum")], fn)


def _tail_pre_bwd(x, dres, g_pre, scale):
    def fn(dh, rows, vecs):
        (xv, dr), (g, sc) = rows, vecs
        r = _rms(xv)
        n = xv * r
        dn = dh * (g * (1.0 + sc))
        return dr + r * (dn - n * jnp.mean(dn * n, axis=-1, keepdims=True)), _col(dh * n), _col(dh)

    return _Tail([x, dres], [g_pre, scale], [(F32, "row"), (F32, "sum"), (F32, "sum")], fn)


def _tail_pre_post_bwd(x, dres, y, g_pre, scale, g_post, gate, weight):
    def fn(dh, rows, vecs):
        (xv, dr, yv), (g, sc, gp, gt) = rows, vecs
        r = _rms(xv)
        n = xv * r
        dn = dh * (g * (1.0 + sc))
        dx = dr + r * (dn - n * jnp.mean(dn * n, axis=-1, keepdims=True))
        ry = _rms(yv)
        yn = yv * ry
        dyn = dx * ((weight * gt) * gp)
        dy = ry * (dyn - yn * jnp.mean(dyn * yn, axis=-1, keepdims=True))
        return dx, dy, _col(dh * n), _col(dh), _col(dx * yn), _col(dy)

    return _Tail([x, dres, y], [g_pre, scale, g_post, gate],
                 [(F32, "row"), (BF16, "row")] + [(F32, "sum")] * 4, fn)


def _mm_tn_pair(a, b, name, col_sums=False):
    k, m = a.shape
    n = b.shape[1]
    rows = m // N_DEV
    n_chip = N_DEV // 2
    tm = 4 * rows
    tk = _pick(k, (1024, 512, 256, 128))
    nk = k // tk

    def body(a_ref, b_ref, p_ref, own_ref, *rest):
        acc_ref, keep_ref, send_ref, land_ref, send_sems, recv_sems = rest[-6:]
        i, kk = pl.program_id(0), pl.program_id(1)
        x, y, c = _mesh_pos()
        if col_sums:
            cs_ref = rest[0]
            part = jnp.sum(a_ref[...].astype(F32), axis=0, keepdims=True)

            @pl.when(kk == 0)
            def _():
                cs_ref[...] = part

            @pl.when(kk > 0)
            def _():
                cs_ref[...] += part

        def push(chip):
            return pltpu.make_async_remote_copy(
                src_ref=send_ref.at[chip], dst_ref=land_ref.at[chip], send_sem=send_sems.at[chip],
                recv_sem=recv_sems.at[chip], device_id=(x, y, 1 - c), device_id_type=MESH)

        if nk == 1:
            acc = _dot_tn(a_ref[...], b_ref[...])
        else:
            @pl.when(kk == 0)
            def _():
                acc_ref[...] = jnp.zeros_like(acc_ref)

            acc_ref[...] += _dot_tn(a_ref[...], b_ref[...])
            acc = acc_ref

        for t in range(2):
            @pl.when((kk == nk - 1) & (i == t))
            def _(t=t):
                for ob in range(4):
                    chip, core = 2 * t + ob // 2, ob % 2
                    blk = acc[ob * rows:(ob + 1) * rows, :]

                    @pl.when(c == core)
                    def _(chip=chip, blk=blk):
                        keep_ref[chip] = blk

                    @pl.when(c != core)
                    def _(chip=chip, blk=blk):
                        send_ref[chip] = blk.astype(BF16)
                        push(chip).start()

        @pl.when((kk == nk - 1) & (i == 1))
        def _():
            for chip in range(n_chip):
                push(chip).wait_recv()
                val = (keep_ref[chip] + land_ref[chip].astype(F32)).astype(BF16)
                p_ref[chip * rows:(chip + 1) * rows, :] = val

                @pl.when(2 * x + y == chip)
                def _(val=val):
                    own_ref[...] = val

            for chip in range(n_chip):
                push(chip).wait_send()

    out_specs = [pl.BlockSpec((n_chip * rows, n), lambda i, kk: (0, 0)), pl.BlockSpec((rows, n), lambda i, kk: (0, 0))]
    out_shape = [jax.ShapeDtypeStruct((n_chip * rows, n), BF16), jax.ShapeDtypeStruct((rows, n), BF16)]
    if col_sums:
        out_specs.append(pl.BlockSpec((1, tm), lambda i, kk: (0, i)))
        out_shape.append(jax.ShapeDtypeStruct((1, m), F32))
    return _call(body, name=name, grid=(2, nk),
                 in_specs=[pl.BlockSpec((tk, tm), lambda i, kk: (kk, i)), pl.BlockSpec((tk, n), lambda i, kk: (kk, 0))],
                 out_specs=out_specs, out_shape=out_shape, args=[a, b],
                 scratch=[pltpu.VMEM((tm, n) if nk > 1 else (8, LANES), F32), pltpu.VMEM((n_chip, rows, n), F32),
                          pltpu.VMEM((n_chip, rows, n), BF16), pltpu.VMEM((n_chip, rows, n), BF16),
                          pltpu.SemaphoreType.DMA((n_chip,)), pltpu.SemaphoreType.DMA((n_chip,))],
                 sem=("arbitrary", "arbitrary"))


def _ffn_up(h, wg_t, wu_t, name, carry=None):
    s, d = h.shape
    f = wg_t.shape[0]
    tm = _pick(s, (512, 256, 128))
    tf = _pick(f, (1408, 1024, 512, 256, 128))

    def body(h_ref, wg_ref, wu_ref, a_ref, b_ref, u_ref):
        hh = h_ref[...]
        for lo, hi in _pieces(tf):
            a = _dot_nt(hh, wg_ref[lo:hi, :])
            b = _dot_nt(hh, wu_ref[lo:hi, :])
            a_ref[:, lo:hi] = a.astype(BF16)
            b_ref[:, lo:hi] = b.astype(BF16)
            u_ref[:, lo:hi] = ((a * _sigmoid(a)) * b).astype(BF16)

    w_spec = pl.BlockSpec((tf, d), lambda j, i: (j, 0))
    o_spec = pl.BlockSpec((tm, tf), lambda j, i: (i, j))
    o_shape = jax.ShapeDtypeStruct((s, f), BF16)
    return _call(body, name=name, grid=(f // tf, s // tm),
                 in_specs=[pl.BlockSpec((tm, d), lambda j, i: (i, 0)), w_spec, w_spec],
                 out_specs=(o_spec, o_spec, o_spec), out_shape=(o_shape, o_shape, o_shape),
                 args=[h, wg_t, wu_t], sem=("parallel", "parallel"), carry=carry)


def _ffn_down_bwd(dy, wd, a, b, name, carry=None):
    s, d = dy.shape
    f = wd.shape[0]
    tm = _pick(s, (512, 256, 128))
    tf = _pick(f, (1408, 1024, 512, 256, 128))

    def body(dy_ref, wd_ref, a_ref, b_ref, da_ref, db_ref):
        dyv = dy_ref[...]
        for lo, hi in _pieces(tf):
            du = _dot_nt(dyv, wd_ref[lo:hi, :])
            a = a_ref[:, lo:hi].astype(F32)
            b = b_ref[:, lo:hi].astype(F32)
            sig = _sigmoid(a)
            da_ref[:, lo:hi] = (du * b * (sig * (1.0 + a * (1.0 - sig)))).astype(BF16)
            db_ref[:, lo:hi] = (du * (a * sig)).astype(BF16)

    t_spec = pl.BlockSpec((tm, tf), lambda j, i: (i, j))
    o_shape = jax.ShapeDtypeStruct((s, f), BF16)
    return _call(body, name=name, grid=(f // tf, s // tm),
                 in_specs=[pl.BlockSpec((tm, d), lambda j, i: (i, 0)), pl.BlockSpec((tf, d), lambda j, i: (j, 0)),
                           t_spec, t_spec],
                 out_specs=(t_spec, t_spec), out_shape=(o_shape, o_shape), args=[dy, wd, a, b],
                 sem=("parallel", "parallel"), carry=carry)


def _row_tile(s):
    return _pick(s, (256, 128, 64))


def _vec_spec(d):
    return pl.BlockSpec((1, d), lambda i: (0, 0))


def _pre_norm(x, g, scale, shift, name):
    s, d = x.shape
    ts = _row_tile(s)

    def body(x_ref, g_ref, sc_ref, sh_ref, h_ref):
        xv = x_ref[...]
        r = lax.rsqrt(jnp.mean(xv * xv, axis=-1, keepdims=True) + EPS)
        h_ref[...] = (((xv * r) * g_ref[...]) * (1.0 + sc_ref[...]) + sh_ref[...]).astype(BF16)

    row = pl.BlockSpec((ts, d), lambda i: (i, 0))
    return _call(body, name=name, grid=(s // ts,), in_specs=[row, _vec_spec(d), _vec_spec(d), _vec_spec(d)],
                 out_specs=row, out_shape=jax.ShapeDtypeStruct((s, d), BF16), args=[x, g, scale, shift],
                 sem=("parallel",))


def _group_norm_cat(oa, ob, ga, gb):
    s = oa.shape[0]
    ts = _row_tile(s)

    def body(oa_ref, ob_ref, ga_ref, gb_ref, y_ref):
        for o_ref, g_ref, lo, w in ((oa_ref, ga_ref, 0, QA), (ob_ref, gb_ref, QA, QB)):
            ov = o_ref[...]
            r = lax.rsqrt(jnp.mean(ov * ov, axis=-1, keepdims=True) + EPS)
            y_ref[:, lo:lo + w] = ((ov * r) * g_ref[...]).astype(BF16)

    return _call(body, name="group_norm_cat", grid=(s // ts,),
                 in_specs=[pl.BlockSpec((ts, QA), lambda i: (i, 0)), pl.BlockSpec((ts, QB), lambda i: (i, 0)),
                           _vec_spec(QA), _vec_spec(QB)],
                 out_specs=pl.BlockSpec((ts, QA + QB), lambda i: (i, 0)),
                 out_shape=jax.ShapeDtypeStruct((s, QA + QB), BF16), args=[oa, ob, ga, gb], sem=("parallel",))


def _group_norm_bwd(dy, oa, ob, ga, gb):
    s = oa.shape[0]
    ts = _row_tile(s)

    def body(dy_ref, oa_ref, ob_ref, ga_ref, gb_ref, doa_ref, dob_ref, dga_ref, dgb_ref):
        @pl.when(pl.program_id(0) == 0)
        def _():
            dga_ref[...] = jnp.zeros_like(dga_ref)
            dgb_ref[...] = jnp.zeros_like(dgb_ref)

        for o_ref, g_ref, do_ref, dg_ref, lo, w in ((oa_ref, ga_ref, doa_ref, dga_ref, 0, QA),
                                                    (ob_ref, gb_ref, dob_ref, dgb_ref, QA, QB)):
            ov = o_ref[...]
            dyv = dy_ref[:, lo:lo + w]
            r = lax.rsqrt(jnp.mean(ov * ov, axis=-1, keepdims=True) + EPS)
            n = ov * r
            dn = dyv * g_ref[...]
            do_ref[...] = r * (dn - n * jnp.mean(dn * n, axis=-1, keepdims=True))
            dg_ref[...] += jnp.sum(dyv * n, axis=0, keepdims=True)

    ra = pl.BlockSpec((ts, QA), lambda i: (i, 0))
    rb = pl.BlockSpec((ts, QB), lambda i: (i, 0))
    return _call(body, name="group_norm_bwd", grid=(s // ts,),
                 in_specs=[pl.BlockSpec((ts, QA + QB), lambda i: (i, 0)), ra, rb, _vec_spec(QA), _vec_spec(QB)],
                 out_specs=(ra, rb, _vec_spec(QA), _vec_spec(QB)),
                 out_shape=(jax.ShapeDtypeStruct((s, QA), F32), jax.ShapeDtypeStruct((s, QB), F32),
                            jax.ShapeDtypeStruct((1, QA), F32), jax.ShapeDtypeStruct((1, QB), F32)),
                 args=[dy, oa, ob, ga, gb], sem=("arbitrary",))


def _n_variants(n_back):
    return -(-n_back // QG) + 1


def _alibi_bias():
    i = np.arange(QROWS)[:, None]
    j = np.arange((QG + BACK_A) * CHUNK)[None, :]
    dist = np.abs(BACK_A * CHUNK + i - j).astype(np.float32)
    dc = j // CHUNK - i // CHUNK
    valid = (dc >= 0) & (dc <= BACK_A)
    slopes = np.array([2.0 ** (-8.0 * (h + 1) / H_A) for h in range(H_A)], dtype=np.float32)
    bias = -slopes[:, None, None] * dist[None]
    out = [np.where((valid & (j >= (BACK_A - QG * v) * CHUNK))[None], bias, np.float32(NEG_INF))
           for v in range(_n_variants(BACK_A))]
    return jnp.asarray(np.stack(out).astype(np.float32))


def _rel_index_matrix():
    cc = np.arange(SKEW)
    dist = np.where(cc < SKEW - QROWS, BACK_B * CHUNK - cc, BACK_B * CHUNK + SKEW - cc)
    idx = np.clip(dist, -REL_CLIP, REL_CLIP) + REL_CLIP
    m = np.zeros((SKEW, N_REL), np.float32)
    m[cc, idx] = 1.0
    return jnp.asarray(m)


def _toeplitz_bias(vec, carry=None):
    lk = (QG + BACK_B) * CHUNK
    nv = _n_variants(BACK_B)

    def body(v_ref, o_ref):
        xv = jnp.broadcast_to(v_ref[0], (QROWS, SKEW))
        row = lax.broadcasted_iota(jnp.int32, (QROWS, SKEW), 0)
        for bit in range(QROWS.bit_length() - 1):
            xv = jnp.where((row >> bit) & 1 == 1, pltpu.roll(xv, 1 << bit, 1), xv)
        ri = lax.broadcasted_iota(jnp.int32, (QROWS, lk), 0) // CHUNK
        col = lax.broadcasted_iota(jnp.int32, (QROWS, lk), 1)
        ci = col // CHUNK
        valid = (ci - ri >= 0) & (ci - ri <= BACK_B)
        for v in range(nv):
            o_ref[v, 0] = jnp.where(valid & (col >= (BACK_B - QG * v) * CHUNK), xv[:, :lk], NEG_INF)

    return _call(body, name="toeplitz_bias", grid=(H_B,),
                 in_specs=[pl.BlockSpec((1, 1, SKEW), lambda h: (h, 0, 0))],
                 out_specs=pl.BlockSpec((nv, 1, QROWS, lk), lambda h: (0, h, 0, 0)),
                 out_shape=jax.ShapeDtypeStruct((nv, H_B, QROWS, lk), F32), args=[vec], sem=("parallel",),
                 carry=carry)


def _diagonal_sums(dbias):
    lk = dbias.shape[2]

    def body(d_ref, o_ref):
        xp = jnp.concatenate([d_ref[0], jnp.zeros((QROWS, SKEW - lk), F32)], axis=1)
        xv = xp[0:CHUNK]
        for q in range(1, QG):
            xv = xv + pltpu.roll(xp[q * CHUNK:(q + 1) * CHUNK], SKEW - q * CHUNK, 1)
        row = lax.broadcasted_iota(jnp.int32, (CHUNK, SKEW), 0)
        for bit in range(CHUNK.bit_length() - 1):
            xv = jnp.where((row >> bit) & 1 == 1, pltpu.roll(xv, SKEW - (1 << bit), 1), xv)
        o_ref[0] = jnp.sum(xv, axis=0, keepdims=True)

    return _call(body, name="diagonal_sums", grid=(H_B,),
                 in_specs=[pl.BlockSpec((1, QROWS, lk), lambda h: (h, 0, 0))],
                 out_specs=pl.BlockSpec((1, 1, SKEW), lambda h: (h, 0, 0)),
                 out_shape=jax.ShapeDtypeStruct((H_B, 1, SKEW), F32), args=[dbias], sem=("parallel",))


def _attn_common(s, n_back, gqa, q_col, k_col, v_col, TPS):
    lk = (QG + n_back) * CHUNK
    pad = n_back * CHUNK
    wide = TPS * LANES
    q_spec = pl.BlockSpec((QROWS, wide), lambda t, g: (g, q_col // TPS + t))
    if gqa:
        k_spec = pl.BlockSpec((s, LANES), lambda t, g: (0, k_col))
        v_spec = pl.BlockSpec((s, LANES), lambda t, g: (0, v_col))
    else:
        k_spec = pl.BlockSpec((s, wide), lambda t, g: (0, k_col // TPS + t))
        v_spec = pl.BlockSpec((s, wide), lambda t, g: (0, v_col // TPS + t))
    last_variant = _n_variants(n_back) - 1
    bias_spec = pl.BlockSpec((None, 2 * TPS, QROWS, lk), lambda t, g: (jnp.minimum(g, last_variant), t, 0, 0))
    tile_spec = pl.BlockSpec((QROWS, wide), lambda t, g: (g, t))
    return lk, pad, q_spec, k_spec, v_spec, bias_spec, tile_spec


def _attention_fwd(proj, bias, sinks, *, n_back, gqa, q_col, k_col, v_col, TPS, name, carry=None):
    s = proj.shape[0]
    lk, pad, q_spec, k_spec, v_spec, bias_spec, tile_spec = _attn_common(s, n_back, gqa, q_col, k_col, v_col, TPS)
    n_t, n_g = 512 // (TPS * LANES), s // QROWS
    kv_wide = LANES if gqa else TPS * LANES

    def body(*refs):
        if gqa:
            q_ref, k_ref, v_ref, bias_ref, sink_ref, o_ref, l_ref, kpad, vpad = refs
        else:
            q_ref, k_ref, v_ref, bias_ref, o_ref, l_ref, kpad, vpad = refs
        t, g = pl.program_id(0), pl.program_id(1)

        @pl.when(g == 0)
        def _():
            kpad[0:pad, :] = jnp.zeros((pad, kv_wide), BF16)
            vpad[0:pad, :] = jnp.zeros((pad, kv_wide), BF16)
            kpad[pad:, :] = k_ref[...]
            vpad[pad:, :] = v_ref[...]

        start = pl.multiple_of(g * QROWS, QROWS)
        half = lax.broadcasted_iota(jnp.int32, (QROWS, LANES), 1) // HEAD_DIM
        for tt in range(TPS):
            lanes = slice(tt * LANES, (tt + 1) * LANES)
            kv_lanes = slice(0, LANES) if gqa else lanes
            kb = kpad[pl.ds(start, lk), kv_lanes]
            vb = vpad[pl.ds(start, lk), kv_lanes]
            q = q_ref[:, lanes] * (HEAD_DIM ** -0.5)
            if gqa:
                hk = (TPS * t + tt) // 2
                q_rolled = pltpu.roll(q.astype(F32), HEAD_DIM, 1).astype(BF16)
            outs, lses = [], []
            for e in range(2):
                if gqa:
                    kv_half = hk
                    src = jnp.where(hk == e, q, q_rolled)
                else:
                    kv_half = e
                    src = q
                qm = jnp.where(half == kv_half, src, jnp.zeros_like(src))
                sc = _dot_nt(qm, kb) + bias_ref[2 * tt + e]
                m = jnp.max(sc, axis=-1, keepdims=True)
                if gqa:
                    sk = sink_ref[2 * (TPS * t + tt) + e]
                    m = jnp.maximum(m, sk)
                p = jnp.exp(sc - m)
                l = jnp.sum(p, axis=-1, keepdims=True)
                if gqa:
                    l = l + jnp.exp(sk - m)
                pn = p / l
                outs.append(_dot(pn.astype(BF16), vb))
                lses.append(m + jnp.log(l))
            if gqa:
                same = jnp.where(hk == 0, outs[0], outs[1])
                other = jnp.where(hk == 0, outs[1], outs[0])
                o_ref[:, lanes] = jnp.where(half == hk, same, pltpu.roll(other, HEAD_DIM, 1))
            else:
                o_ref[:, lanes] = jnp.where(half == 0, outs[0], outs[1])
            l_ref[:, lanes] = jnp.where(half == 0, lses[0], lses[1])

    in_specs = [q_spec, k_spec, v_spec, bias_spec] + ([SMEM_SPEC] if gqa else [])
    args = [proj, proj, proj, bias] + ([sinks] if gqa else [])
    o_shape = jax.ShapeDtypeStruct((s, 512), F32)
    return _call(body, name=name, grid=(n_t, n_g), in_specs=in_specs, out_specs=(tile_spec, tile_spec),
                 out_shape=(o_shape, o_shape), args=args,
                 scratch=[pltpu.VMEM((s + pad, kv_wide), BF16), pltpu.VMEM((s + pad, kv_wide), BF16)],
                 sem=("arbitrary", "arbitrary"), carry=carry)


def _attention_bwd(proj, bias, sinks, do, lse, *, n_back, gqa, q_col, k_col, v_col, TPS, name, carry=None):
    s = proj.shape[0]
    lk, pad, q_spec, k_spec, v_spec, bias_spec, tile_spec = _attn_common(s, n_back, gqa, q_col, k_col, v_col, TPS)
    n_t, n_g = 512 // (TPS * LANES), s // QROWS
    kv_wide = LANES if gqa else TPS * LANES

    def body(*refs):
        if gqa:
            (q_ref, k_ref, v_ref, bias_ref, sink_ref, do_ref, l_ref,
             dq_ref, dk_ref, dv_ref, dsink_ref, kpad, vpad, dkpad, dvpad) = refs
        else:
            (q_ref, k_ref, v_ref, bias_ref, do_ref, l_ref,
             dq_ref, dk_ref, dv_ref, dbias_ref, kpad, vpad, dkpad, dvpad) = refs
        t, g = pl.program_id(0), pl.program_id(1)

        @pl.when(g == 0)
        def _():
            kpad[0:pad, :] = jnp.zeros((pad, kv_wide), BF16)
            vpad[0:pad, :] = jnp.zeros((pad, kv_wide), BF16)
            kpad[pad:, :] = k_ref[...]
            vpad[pad:, :] = v_ref[...]
            if gqa:
                dsink_ref[...] = jnp.zeros_like(dsink_ref)
            else:
                dbias_ref[...] = jnp.zeros_like(dbias_ref)

        @pl.when((g == 0) & (t == 0) if gqa else g == 0)
        def _():
            dkpad[...] = jnp.zeros_like(dkpad)
            dvpad[...] = jnp.zeros_like(dvpad)

        start = pl.multiple_of(g * QROWS, QROWS)
        half = lax.broadcasted_iota(jnp.int32, (QROWS, LANES), 1) // HEAD_DIM
        for tt in range(TPS):
            lanes = slice(tt * LANES, (tt + 1) * LANES)
            kv_lanes = slice(0, LANES) if gqa else lanes
            kb = kpad[pl.ds(start, lk), kv_lanes]
            vb = vpad[pl.ds(start, lk), kv_lanes]
            q = q_ref[:, lanes]
            dov = do_ref[:, lanes]
            lv = l_ref[:, lanes]
            if gqa:
                hk = (TPS * t + tt) // 2
                q_rolled = pltpu.roll(q.astype(F32), HEAD_DIM, 1).astype(BF16)
                do_rolled = pltpu.roll(dov, HEAD_DIM, 1)
            dqs = []
            dk_acc = jnp.zeros((lk, LANES), F32)
            dv_acc = jnp.zeros((lk, LANES), F32)
            for e in range(2):
                if gqa:
                    kv_half = hk
                    src = jnp.where(hk == e, q, q_rolled)
                    do_src = jnp.where(hk == e, dov, do_rolled)
                else:
                    kv_half = e
                    src = q
                    do_src = dov
                qm = jnp.where(half == kv_half, src, jnp.zeros_like(src))
                dom = jnp.where(half == kv_half, do_src, 0.0).astype(BF16)
                lcol = jnp.max(jnp.where(half == e, lv, -jnp.inf), axis=-1, keepdims=True)
                sc = _dot_nt(qm * (HEAD_DIM ** -0.5), kb) + bias_ref[2 * tt + e]
                pn = jnp.exp(sc - lcol)
                dp = _dot_nt(dom, vb)
                delta = jnp.sum(pn * dp, axis=-1, keepdims=True)
                ds = pn * (dp - delta)
                if gqa:
                    p_sink = jnp.exp(sink_ref[2 * (TPS * t + tt) + e] - lcol)
                    dsk = -jnp.sum(p_sink * delta, axis=0, keepdims=True)
                    row = 2 * tt + e
                    dsink_ref[0, row:row + 1, :] += jnp.broadcast_to(dsk, (1, LANES))
                else:
                    dbias_ref[2 * tt + e] += ds
                dsb = (ds * (HEAD_DIM ** -0.5)).astype(BF16)
                dqs.append(_dot(dsb, kb))
                dk_acc = dk_acc + _dot_tn(dsb, qm)
                dv_acc = dv_acc + _dot_tn(pn.astype(BF16), dom)
            dkpad[pl.ds(start, lk), kv_lanes] += dk_acc
            dvpad[pl.ds(start, lk), kv_lanes] += dv_acc
            if gqa:
                same = jnp.where(hk == 0, dqs[0], dqs[1])
                other = jnp.where(hk == 0, dqs[1], dqs[0])
                dq_ref[:, lanes] = jnp.where(half == hk, same, pltpu.roll(other, HEAD_DIM, 1)).astype(BF16)
            else:
                dq_ref[:, lanes] = jnp.where(half == 0, dqs[0], dqs[1]).astype(BF16)

        @pl.when((g == n_g - 1) & (t == n_t - 1) if gqa else g == n_g - 1)
        def _():
            dk_ref[...] = dkpad[pad:, :].astype(BF16)
            dv_ref[...] = dvpad[pad:, :].astype(BF16)

    in_specs = [q_spec, k_spec, v_spec, bias_spec] + ([SMEM_SPEC] if gqa else []) + [tile_spec, tile_spec]
    args = [proj, proj, proj, bias] + ([sinks] if gqa else []) + [do, lse]
    if gqa:
        kv_out = pl.BlockSpec((s, LANES), lambda t, g: (0, 0))
        kv_shape = jax.ShapeDtypeStruct((s, LANES), BF16)
        extra_spec = pl.BlockSpec((1, 8, LANES), lambda t, g: (t, 0, 0))
        extra_shape = jax.ShapeDtypeStruct((n_t, 8, LANES), F32)
    else:
        kv_out = pl.BlockSpec((s, kv_wide), lambda t, g: (0, t))
        kv_shape = jax.ShapeDtypeStruct((s, 512), BF16)
        extra_spec = pl.BlockSpec((2 * TPS, QROWS, lk), lambda t, g: (t, 0, 0))
        extra_shape = jax.ShapeDtypeStruct(bias.shape[1:], F32)
    return _call(body, name=name, grid=(n_t, n_g), in_specs=in_specs,
                 out_specs=(tile_spec, kv_out, kv_out, extra_spec),
                 out_shape=(jax.ShapeDtypeStruct((s, 512), BF16), kv_shape, kv_shape, extra_shape), args=args,
                 scratch=[pltpu.VMEM((s + pad, kv_wide), BF16), pltpu.VMEM((s + pad, kv_wide), BF16),
                          pltpu.VMEM((s + pad, kv_wide), F32), pltpu.VMEM((s + pad, kv_wide), F32)],
                 sem=("arbitrary", "arbitrary"), carry=carry)


def _sum_rows8(g):
    n = g.shape[2]

    def body(g_ref, o_ref):
        acc = g_ref[0]
        for j in range(1, N_DEV):
            acc = acc + g_ref[j]
        o_ref[...] = acc

    return pl.pallas_call(
        body, name="sum_small_grads", in_specs=[VMEM_SPEC], out_specs=VMEM_SPEC,
        out_shape=jax.ShapeDtypeStruct((1, n), F32), compiler_params=_params(),
    )(g)


def _ada_weight_grad(sc_t, dmod_cols):
    d = sc_t.shape[0]
    w = dmod_cols.shape[1]
    td = _pick(d, (256, 128))

    def body(sc_ref, dm_ref, o_ref):
        scv = sc_ref[...]
        dmv = dm_ref[...]
        acc = scv[:, 0:1] * dmv[0:1, :]
        for b in range(1, N_DEV):
            acc = acc + scv[:, b:b + 1] * dmv[b:b + 1, :]
        o_ref[...] = acc

    return _call(body, name="ada_weight_grad", grid=(d // td,),
                 in_specs=[pl.BlockSpec((td, N_DEV), lambda i: (i, 0)), pl.BlockSpec((N_DEV, w), lambda i: (0, 0))],
                 out_specs=pl.BlockSpec((td, w), lambda i: (i, 0)), out_shape=jax.ShapeDtypeStruct((d, w), F32),
                 args=[sc_t, dmod_cols], sem=("parallel",))


def _adamw_update(w, gv, m, v):
    nm = ADAM_B1 * m + (1.0 - ADAM_B1) * gv
    nv = ADAM_B2 * v + (1.0 - ADAM_B2) * (gv * gv)
    m_hat = nm / (1.0 - ADAM_B1 ** ADAM_STEP)
    v_hat = nv / (1.0 - ADAM_B2 ** ADAM_STEP)
    return -ADAM_LR * (m_hat / (jnp.sqrt(v_hat) + ADAM_EPS) + ADAM_WD * w), nm, nv


def _adamw(w, g, m, v, name):
    rows, cols = w.shape
    tr = _pick(rows, (256, 176, 128, 88, 64)) if rows > 256 else rows

    def body(w_ref, g_ref, m_ref, v_ref, d_ref, nm_ref, nv_ref):
        d_ref[...], nm_ref[...], nv_ref[...] = _adamw_update(w_ref[...], g_ref[...], m_ref[...], v_ref[...])

    spec = pl.BlockSpec((tr, cols), lambda i: (i, 0))
    shape = jax.ShapeDtypeStruct((rows, cols), F32)
    return _call(body, name=name, grid=(rows // tr,), in_specs=[spec] * 4, out_specs=(spec, spec, spec),
                 out_shape=(shape, shape, shape), args=[w, g, m, v], sem=("parallel",))


def _adamw_from_slots(w, own, slots, m, v, name):
    n_slots, rows, k = slots.shape

    def body(o_ref, s_ref, w_ref, m_ref, v_ref, g_ref, d_ref, nm_ref, nv_ref):
        gv = o_ref[...].astype(F32)
        for j in range(n_slots):
            gv = gv + s_ref[j].astype(F32)
        g_ref[...] = gv
        d_ref[...], nm_ref[...], nv_ref[...] = _adamw_update(w_ref[...], gv, m_ref[...], v_ref[...])

    tr = rows // 2 if rows % 32 == 0 else rows
    spec = pl.BlockSpec((tr, k), lambda i: (i, 0))
    shape = jax.ShapeDtypeStruct((rows, k), F32)
    return _call(body, name=name, grid=(rows // tr,),
                 in_specs=[spec, pl.BlockSpec((n_slots, tr, k), lambda i: (0, i, 0)), spec, spec, spec],
                 out_specs=(spec, spec, spec, spec), out_shape=(shape, shape, shape, shape),
                 args=[own, slots, w, m, v], sem=("parallel",))


def _adamw_small(g, w, m, v, sizes):
    n = w.shape[1]
    offs, off = [], 0
    for size in sizes:
        offs.append(off)
        off += size + (-size % LANES)

    def body(g_ref, w_ref, m_ref, v_ref, *out_refs):
        gv = g_ref[:, 0:n]
        dv, nm, nv = _adamw_update(w_ref[...], gv, m_ref[...], v_ref[...])
        for j, (o, size) in enumerate(zip(offs, sizes)):
            for k, val in enumerate((gv, dv, nm, nv)):
                out_refs[4 * j + k][...] = val[:, o:o + size]

    shapes = [jax.ShapeDtypeStruct((1, size), F32) for size in sizes for _ in range(4)]
    return pl.pallas_call(
        body, name="adamw_small", in_specs=[VMEM_SPEC] * 4, out_specs=tuple([VMEM_SPEC] * len(shapes)),
        out_shape=tuple(shapes), compiler_params=_params(),
    )(g, w, m, v)


SMALL = ("b_ada", "g_pre_ffn1", "g_post_ffn1", "g_pre_mix", "b_in", "sinks_a", "rel_bias_b", "g_grp_a",
         "g_grp_b", "b_out", "g_post_mix", "g_pre_ffn2", "g_post_ffn2")
WEIGHTS = ("w_ada", "b_ada", "g_pre_ffn1", "w_gate1", "w_up1", "w_down1", "g_post_ffn1", "g_pre_mix", "w_in",
           "b_in", "sinks_a", "rel_bias_b", "g_grp_a", "g_grp_b", "w_out", "b_out", "g_post_mix", "g_pre_ffn2",
           "w_gate2", "w_up2", "w_down2", "g_post_ffn2")


def kernel(x, c, w_ada, b_ada, g_pre_ffn1, w_gate1, w_up1, w_down1, g_post_ffn1, g_pre_mix, w_in, b_in, sinks_a, rel_bias_b, g_grp_a, g_grp_b, w_out, b_out, g_post_mix, g_pre_ffn2, w_gate2, w_up2, w_down2, g_post_ffn2, loss_target, m_w_ada, m_b_ada, m_g_pre_ffn1, m_w_gate1, m_w_up1, m_w_down1, m_g_post_ffn1, m_g_pre_mix, m_w_in, m_b_in, m_sinks_a, m_rel_bias_b, m_g_grp_a, m_g_grp_b, m_w_out, m_b_out, m_g_post_mix, m_g_pre_ffn2, m_w_gate2, m_w_up2, m_w_down2, m_g_post_ffn2, v_w_ada, v_b_ada, v_g_pre_ffn1, v_w_gate1, v_w_up1, v_w_down1, v_g_post_ffn1, v_g_pre_mix, v_w_in, v_b_in, v_sinks_a, v_rel_bias_b, v_g_grp_a, v_g_grp_b, v_w_out, v_b_out, v_g_post_mix, v_g_pre_ffn2, v_w_gate2, v_w_up2, v_w_down2, v_g_post_ffn2):
    given = dict(locals())
    weights = {n: given[n] for n in WEIGHTS}
    mom_m = {n: given["m_" + n] for n in WEIGHTS}
    mom_v = {n: given["v_" + n] for n in WEIGHTS}

    me = 4 * lax.axis_index("x") + 2 * lax.axis_index("y") + lax.axis_index("c")
    xs = x[0]
    tgt = loss_target[0]
    d_model = xs.shape[1]
    ada_cols = w_ada.shape[2]

    sh = {"wg1": w_gate1[0].T, "wu1": w_up1[0].T, "wd1": w_down1[0], "win": w_in[0].T, "wo": w_out[0],
          "wg2": w_gate2[0].T, "wu2": w_up2[0].T, "wd2": w_down2[0]}
    sh = {k: v.astype(BF16) for k, v in sh.items()}

    def gather(full=(), new=(), cont=()):
        return _gather_carry([sh[n] for n in full], [sh[n] for n in new], cont)

    bias_a = _alibi_bias()
    rel_m = _rel_index_matrix()
    rel_vec = jnp.dot(rel_bias_b[0], rel_m.T, precision=lax.Precision.HIGHEST)
    bias_b, (wg1, wu1, wd1_part) = _toeplitz_bias(rel_vec.reshape(H_B, 1, SKEW),
                                                  carry=gather(full=("wg1", "wu1"), new=("wd1",)))

    b_cols = lax.dynamic_slice(b_ada, (0, me * ada_cols), (1, ada_cols))
    (sc_all, mod_rows), _ = _ada_forward(c, w_ada[0], b_cols, _Carry([], [], [], lambda *a: None, lambda *a: None))
    mod = mod_rows.reshape(N_MOD, d_model)
    shift1, scale1, gate1, shift2, scale2, gate2, shift3, scale3, gate3 = (mod[i:i + 1] for i in range(N_MOD))

    h1 = _pre_norm(xs, g_pre_ffn1, scale1, shift1, "pre_norm_ffn1")
    (a1, b1, u1), (win_part, wo_part, wd1) = _ffn_up(h1, wg1, wu1, "ffn_up_ffn1",
                                                     carry=gather(new=("win", "wo"), cont=(wd1_part,)))
    (y1, x1, h2), (wg2_part, win, wo) = _mm_nn(
        [(u1, wd1)], "ffn_down_ffn1", F32, carry=gather(new=("wg2",), cont=(win_part, wo_part)),
        tail=_tail_post_pre(xs, g_post_ffn1, gate1, 0.5, g_pre_mix, scale2, shift2))

    proj, (wg2,) = _mm_nt(h2, win, "in_proj", BF16, bias=b_in, carry=gather(cont=(wg2_part,)))
    sinks = sinks_a[0]
    cfg_a = dict(n_back=BACK_A, gqa=True, q_col=0, k_col=QA // LANES, v_col=(QA + KVA) // LANES, TPS=TPS_A)
    cfg_b = dict(n_back=BACK_B, gqa=False, q_col=(QA + 2 * KVA) // LANES, k_col=(QA + 2 * KVA + QB) // LANES,
                 v_col=(QA + 2 * KVA + 2 * QB) // LANES, TPS=TPS_B)
    (oa, lse_a), (wu2_part, wd2_part) = _attention_fwd(proj, bias_a, sinks, name="attn_a",
                                                       carry=gather(new=("wu2", "wd2")), **cfg_a)
    (ob, lse_b), (wu2, wd2) = _attention_fwd(proj, bias_b, None, name="attn_b",
                                             carry=gather(cont=(wu2_part, wd2_part)), **cfg_b)
    ycat = _group_norm_cat(oa, ob, g_grp_a, g_grp_b)
    ymix, x2, h3 = _mm_nn([(ycat, wo)], "out_proj", F32, bias=b_out,
                          tail=_tail_post_pre(x1, g_post_mix, gate2, 1.0, g_pre_ffn2, scale3, shift3))

    a3, b3, u3 = _ffn_up(h3, wg2, wu2, "ffn_up_ffn2")

    flights, own = {}, {}

    def grad_pair(key, a_mat, b_mat, name):
        part, own[key] = _mm_tn_pair(a_mat, b_mat, name)
        return part

    def scatter_start(tag, after_vec, **parts):
        names = list(parts)
        sems, p_thru, lands, token = _scatter_start([parts[n] for n in names], "scatter_start_" + tag)
        flights[tag] = (names, sems, p_thru, lands)
        return after_vec + token[0:1, 0:1]

    dx3, dy, loss_part, s1 = _mm_nn([(u3, wd2)], "ffn_down_ffn2", None,
                                    tail=_tail_post_loss(x2, tgt, g_post_ffn2, gate3, 0.5))
    da, db = _ffn_down_bwd(dy, wd2, a3, b3, "ffn_down_bwd_ffn2")
    dwd2 = grad_pair("wd2", u3, dy, "grad_wd_ffn2")
    dwg2 = grad_pair("wg2", da, h3, "grad_wg_ffn2")
    dwu2 = grad_pair("wu2", db, h3, "grad_wu_ffn2")
    g_pre_tied = scatter_start("ffn2", g_pre_ffn2, wd2=dwd2, wg2=dwg2, wu2=dwu2)
    dx2, dymix, s2, s3, s1m, db_out = _mm_nn(
        [(da, wg2), (db, wu2)], "ffn_up_bwd_ffn2", None,
        tail=_tail_pre_post_bwd(x2, dx3, ymix, g_pre_tied, scale3, g_post_mix, gate2, 1.0))
    sm3 = dict(shift=s3, scale=s2 * g_pre_ffn2, gate=0.5 * g_post_ffn2 * s1,
               g_pre=(1.0 + scale3) * s2, g_post=(0.5 * gate3) * s1)

    dycat = _mm_nt(dymix, wo, "out_proj_bwd", F32)
    dwo = grad_pair("wo", ycat, dymix, "grad_wo")
    doa, dob, dg_a, dg_b = _group_norm_bwd(dycat, oa, ob, g_grp_a, g_grp_b)
    dqa, dka, dva, dsink = _attention_bwd(proj, bias_a, sinks, doa, lse_a, name="attn_a_bwd", **cfg_a)
    dqb, dkb, dvb, dbias = _attention_bwd(proj, bias_b, None, dob, lse_b, name="attn_b_bwd", **cfg_b)
    dproj = jnp.concatenate([dqa, dka, dva, dqb, dkb, dvb], axis=1)
    dwin, own["win"], db_in = _mm_tn_pair(dproj, h2, "grad_win", col_sums=True)
    g_pre_tied = scatter_start("mix", g_pre_mix, wo=dwo, win=dwin)
    dx1, dy, s2m, s3m, s1, _ = _mm_nn(
        [(dproj, win)], "in_proj_bwd", None,
        tail=_tail_pre_post_bwd(x1, dx2, y1, g_pre_tied, scale2, g_post_ffn1, gate1, 0.5))
    d_rel = jnp.dot(_diagonal_sums(dbias).reshape(H_B, SKEW), rel_m, precision=lax.Precision.HIGHEST)
    d_sinks = dsink[:, :2 * TPS_A, 0].reshape(1, H_A)

    da, db = _ffn_down_bwd(dy, wd1, a1, b1, "ffn_down_bwd_ffn1")
    dwd1 = grad_pair("wd1", u1, dy, "grad_wd_ffn1")
    dwg1 = grad_pair("wg1", da, h1, "grad_wg_ffn1")
    dwu1 = grad_pair("wu1", db, h1, "grad_wu_ffn1")
    g_pre_tied = scatter_start("ffn1", g_pre_ffn1, wd1=dwd1, wg1=dwg1, wu1=dwu1)
    dx0, s2, s3 = _mm_nn([(da, wg1), (db, wu1)], "ffn_up_bwd_ffn1", None,
                         tail=_tail_pre_bwd(xs, dx1, g_pre_tied, scale1))
    sm1 = dict(shift=s3, scale=s2 * g_pre_ffn1, gate=0.5 * g_post_ffn1 * s1,
               g_pre=(1.0 + scale1) * s2, g_post=(0.5 * gate1) * s1)

    dmod = jnp.concatenate([sm1["shift"], sm1["scale"], sm1["gate"],
                            s3m, s2m * g_pre_mix, g_post_mix * s1m,
                            sm3["shift"], sm3["scale"], sm3["gate"]], axis=1)
    small_parts = {
        "b_ada": dmod, "g_pre_ffn1": sm1["g_pre"], "g_post_ffn1": sm1["g_post"],
        "g_pre_mix": (1.0 + scale2) * s2m, "b_in": db_in, "sinks_a": d_sinks,
        "rel_bias_b": d_rel.reshape(1, H_B * N_REL), "g_grp_a": dg_a, "g_grp_b": dg_b, "b_out": db_out,
        "g_post_mix": gate2 * s1m, "g_pre_ffn2": sm3["g_pre"], "g_post_ffn2": sm3["g_post"]}
    sizes = [small_parts[n].shape[1] for n in SMALL]

    def pack(parts):
        cells = []
        for p in parts:
            cells.append(p)
            if p.shape[1] % LANES:
                cells.append(jnp.zeros((1, -p.shape[1] % LANES), F32))
        return jnp.concatenate(cells, axis=1)

    packed = pack([small_parts[n] for n in SMALL] + [loss_part])
    n_packed = packed.shape[1]
    small_sems, packed_thru, small_land, small_token = _small_gather_start(packed)

    out_g, out_d, out_m, out_v = {}, {}, {}, {}
    groups = (("ffn2", (("w_gate2", "wg2", True), ("w_up2", "wu2", True), ("w_down2", "wd2", False))),
              ("mix", (("w_in", "win", True), ("w_out", "wo", False))),
              ("ffn1", (("w_gate1", "wg1", True), ("w_up1", "wu1", True), ("w_down1", "wd1", False))))
    after = small_token
    for tag, members in groups:
        names, sems, p_thru, lands = flights[tag]
        _, l_done = _scatter_wait(sems, p_thru, lands, after, "scatter_wait_" + tag)
        slots = dict(zip(names, l_done))
        for n, key, transposed in members:
            view = (lambda t: t.T) if transposed else (lambda t: t)
            res = _adamw_from_slots(view(weights[n][0]), own[key], slots[key], view(mom_m[n][0]),
                                    view(mom_v[n][0]), "adamw_" + n)
            out_g[n], out_d[n], out_m[n], out_v[n] = (view(t)[None] for t in res)
            after = res[3]

    packed_done, small_land = _small_gather_wait(small_sems, packed_thru, small_land, after)
    gathered = lax.dynamic_update_slice(small_land, packed_done[None], (me, 0, 0))
    small_sum = _sum_rows8(gathered)
    loss = small_sum[0, n_packed - LANES]
    dmod_cols = lax.dynamic_slice(gathered.reshape(N_DEV, n_packed), (0, me * ada_cols), (N_DEV, ada_cols))
    g_ada = _ada_weight_grad(sc_all.reshape(N_DEV, d_model).T, dmod_cols)
    d_, m_, v_ = _adamw(w_ada[0], g_ada, m_w_ada[0], v_w_ada[0], "adamw_w_ada")
    out_g["w_ada"], out_d["w_ada"], out_m["w_ada"], out_v["w_ada"] = g_ada[None], d_[None], m_[None], v_[None]

    small_out = _adamw_small(small_sum, *(pack([tree[n].reshape(1, -1) for n in SMALL])
                                          for tree in (weights, mom_m, mom_v)), sizes)
    for j, n in enumerate(SMALL):
        shape = weights[n].shape
        out_g[n], out_d[n], out_m[n], out_v[n] = (t.reshape(shape) for t in small_out[4 * j:4 * j + 4])

    return (loss, dx0[None], *[out_g[n] for n in WEIGHTS], *[out_d[n] for n in WEIGHTS],
            *[out_m[n] for n in WEIGHTS], *[out_v[n] for n in WEIGHTS])
```

```python
import numpy as np
import jax
import jax.numpy as jnp
from jax import lax
from jax.experimental import pallas as pl
from jax.experimental.pallas import tpu as pltpu

F32 = jnp.float32
BF16 = jnp.bfloat16
MESH = pl.DeviceIdType.MESH
ANY = pl.BlockSpec(memory_space=pl.ANY)
VMEM_SPEC = pl.BlockSpec(memory_space=pltpu.VMEM)
SMEM_SPEC = pl.BlockSpec(memory_space=pltpu.SMEM)

N_DEV = 8
CHUNK = 64
HEAD_DIM = 64
LANES = 128
H_A, KV_A, H_B = 8, 2, 8
BACK_A, BACK_B = 2, 8
REL_CLIP = 128
N_REL = 2 * REL_CLIP + 1
QA, KVA, QB = H_A * HEAD_DIM, KV_A * HEAD_DIM, H_B * HEAD_DIM
D_IN = QA + 2 * KVA + 3 * QB
N_MOD = 9
EPS = 1e-6
NEG_INF = -1e30
QG = 4
QROWS = QG * CHUNK
TPS_A, TPS_B = 4, 2
SKEW = 1024
ADAM_LR, ADAM_B1, ADAM_B2, ADAM_EPS, ADAM_WD, ADAM_STEP = 0.001, 0.9, 0.999, 1e-08, 0.01, 10
VMEM_LIMIT = 56 * 2 ** 20


def _pick(n, cands):
    for c in cands:
        if n % c == 0:
            return c
    return n


def _pieces(n, width=2 * LANES):
    return [(lo, min(lo + width, n)) for lo in range(0, n, width)]


def _params(sem=None):
    return pltpu.CompilerParams(dimension_semantics=sem, vmem_limit_bytes=VMEM_LIMIT)


def _dot_nt(a, b):
    return lax.dot_general(a, b, (((1,), (1,)), ((), ())), preferred_element_type=F32)


def _dot_tn(a, b):
    return lax.dot_general(a, b, (((0,), (0,)), ((), ())), preferred_element_type=F32)


def _dot(a, b):
    return jnp.dot(a, b, preferred_element_type=F32)


def _sigmoid(a):
    return 0.5 * (jnp.tanh(0.5 * a) + 1.0)


def _mesh_pos():
    return lax.axis_index("x"), lax.axis_index("y"), lax.axis_index("c")


def _peer(x, y, c, r):
    px = 1 - x if r & 4 else x
    py = 1 - y if r & 2 else y
    pc = 1 - c if r & 1 else c
    return px, py, pc


class _Carry:
    def __init__(self, ins, out_shapes, scratch, start, finish, aliases=()):
        self.ins, self.out_shapes, self.scratch = list(ins), list(out_shapes), list(scratch)
        self.start, self.finish, self.aliases = start, finish, list(aliases)


def _call(body, *, name, grid, in_specs, out_specs, out_shape, args, scratch=(), sem=None, carry=None):
    single = not isinstance(out_shape, (tuple, list))
    out_specs = (out_specs,) if single else tuple(out_specs)
    out_shape = (out_shape,) if single else tuple(out_shape)
    if carry is None:
        res = pl.pallas_call(body, name=name, grid=grid, in_specs=list(in_specs), out_specs=out_specs,
                             out_shape=out_shape, scratch_shapes=list(scratch), compiler_params=_params(sem))(*args)
        return res[0] if single else res
    n_in, n_out, n_s = len(in_specs), len(out_shape), len(scratch)
    ci, co = len(carry.ins), len(carry.out_shapes)

    def wrapped(*refs):
        ins, cins = refs[:n_in], refs[n_in:n_in + ci]
        outs = refs[n_in + ci:n_in + ci + n_out]
        couts = refs[n_in + ci + n_out:n_in + ci + n_out + co]
        scr = refs[n_in + ci + n_out + co:n_in + ci + n_out + co + n_s]
        cscr = refs[n_in + ci + n_out + co + n_s:]
        first, last = None, None
        for ax, n in enumerate(grid):
            f, l = pl.program_id(ax) == 0, pl.program_id(ax) == n - 1
            first = f if first is None else first & f
            last = l if last is None else last & l
        pl.when(first)(lambda: carry.start(cins, couts, cscr))
        body(*ins, *outs, *scr)
        pl.when(last)(lambda: carry.finish(cins, couts, cscr))

    res = pl.pallas_call(
        wrapped, name=name, grid=grid, in_specs=list(in_specs) + [ANY] * ci, out_specs=out_specs + (ANY,) * co,
        out_shape=out_shape + tuple(carry.out_shapes), scratch_shapes=list(scratch) + carry.scratch,
        input_output_aliases={n_in + i: n_out + o for i, o in carry.aliases},
        compiler_params=_params(("arbitrary",) * len(grid)))(*args, *carry.ins)
    main = res[:n_out]
    return (main[0] if single else main), res[n_out:]


PASS_PIECES = 4
ROW_TILE = 16


def _row_pieces(rows):
    tiles, pieces, off = rows // ROW_TILE, [], 0
    for i in range(PASS_PIECES):
        n = (tiles + i) // PASS_PIECES * ROW_TILE
        if n:
            pieces.append((off, n))
        off += n
    assert off == rows
    return pieces


def _gather_carry(full=(), new=(), cont=()):
    full, new, cont = list(full), list(new), list(cont)
    n_full, n_one = len(full), len(full) + len(new)
    n_w = n_one + len(cont)
    rows = [s.shape[0] for s in full + new] + [b.shape[0] // N_DEV for b in cont]
    shapes = [jax.ShapeDtypeStruct((N_DEV * s.shape[0], s.shape[1]), s.dtype) for s in full + new]
    shapes += [jax.ShapeDtypeStruct(b.shape, b.dtype) for b in cont]
    halves = [((0, r // 2), (r // 2, r // 2)) for r in rows]
    pieces = [_row_pieces(r) for r in rows]

    def plan(ins, outs, scr):
        send_sems, recv_sems, local_sems = scr
        x, y, c = _mesh_pos()
        me, sibling = (x, y, c), (x, y, 1 - c)
        x_chip, y_chip, far_chip = (1 - x, y), (x, 1 - y), (1 - x, 1 - y)

        def block(buf, w, chip, core, span=None):
            off, n = (0, rows[w]) if span is None else span
            start = (4 * chip[0] + 2 * chip[1] + core) * rows[w] + off
            return buf.at[pl.ds(pl.multiple_of(start, 16), n), :]

        def copy(w, k, chip, core, to, span=None, held=None, shard=None, p=0):
            if shard is None:
                src = block(outs[w] if held is None else held, w, chip, core, span)
            else:
                src = shard if span is None else shard.at[pl.ds(span[0], span[1]), :]
            return pltpu.make_async_remote_copy(
                src_ref=src, dst_ref=block(outs[w], w, chip, core, span), send_sem=send_sems.at[w, k, p],
                recv_sem=recv_sems.at[w, k, p], device_id=to, device_id_type=MESH)

        def spans(w, k):
            return pieces[w] if k == 7 else [None]

        def to_sibling(w, k, chip, held=None, shard=None):
            return [copy(w, k, chip, c, sibling, span, held, shard, p) for p, span in enumerate(spans(w, k))]

        def from_sibling(w, k, chip):
            return [copy(w, k, chip, 1 - c, me, span, p=p) for p, span in enumerate(spans(w, k))]

        def stage_one(w):
            return [copy(w, 1, (x, y), c, (*x_chip, c), shard=ins[w]), copy(w, 2, (x, y), c, (*y_chip, c), shard=ins[w]),
                    *to_sibling(w, 0, (x, y), shard=ins[w])]

        def stage_two(w, held):
            return [copy(w, 5, x_chip, c, (*y_chip, c), halves[w][0], held),
                    copy(w, 6, y_chip, c, (*x_chip, c), halves[w][1], held),
                    *to_sibling(w, 3, x_chip, held), *to_sibling(w, 4, y_chip, held)]

        mine = [pltpu.make_async_copy(ins[w], block(outs[w], w, (x, y), c), local_sems.at[w]) for w in range(n_one)]
        return c, me, sibling, far_chip, copy, to_sibling, from_sibling, stage_one, stage_two, mine

    def start(ins, outs, scr):
        _, _, _, _, _, _, _, stage_one, stage_two, mine = plan(ins, outs, scr)
        for w in range(n_one):
            for cp in stage_one(w):
                cp.start()
        for w in range(n_one, n_w):
            for cp in stage_two(w, ins[w]):
                cp.start()
        for cp in mine:
            cp.start()

    def finish(ins, outs, scr):
        c, me, sibling, far_chip, copy, to_sibling, from_sibling, stage_one, stage_two, mine = plan(ins, outs, scr)
        x_chip, y_chip = (far_chip[0], me[1]), (me[0], far_chip[1])
        sent = []

        def land_one(w):
            copy(w, 1, x_chip, c, me).wait_recv()
            copy(w, 2, y_chip, c, me).wait_recv()

        def land_two(w):
            copy(w, 5, far_chip, c, me, halves[w][0]).wait_recv()
            copy(w, 6, far_chip, c, me, halves[w][1]).wait_recv()
            for cp in to_sibling(w, 7, far_chip):
                cp.start()
                sent.append(cp)

        for w in range(n_full):
            land_one(w)
            for cp in stage_two(w, outs[w]):
                cp.start()
                sent.append(cp)
        for w in range(n_one, n_w):
            land_two(w)
            sent.extend(stage_two(w, ins[w]))
        for w in range(n_full, n_one):
            land_one(w)
        for w in range(n_full):
            land_two(w)
        for w in range(n_one):
            for cp in from_sibling(w, 0, me[:2]):
                cp.wait_recv()
            sent.extend(stage_one(w))
        for w in list(range(n_full)) + list(range(n_one, n_w)):
            for k, chip in ((3, x_chip), (4, y_chip), (7, far_chip)):
                for cp in from_sibling(w, k, chip):
                    cp.wait_recv()
        for cp in sent:
            cp.wait_send()
        for cp in mine:
            cp.wait()

    return _Carry(
        full + new + cont, shapes,
        [pltpu.SemaphoreType.DMA((n_w, N_DEV, PASS_PIECES)), pltpu.SemaphoreType.DMA((n_w, N_DEV, PASS_PIECES)),
         pltpu.SemaphoreType.DMA((max(n_one, 1),))], start, finish,
        aliases=[(w, w) for w in range(n_one, n_w)])


HBM_SPEC = pl.BlockSpec(memory_space=pltpu.HBM)
SEM_SPEC = pl.BlockSpec(memory_space=pltpu.SEMAPHORE)
N_CHIP = N_DEV // 2


def _scatter_copy(part_ref, land_ref, send_sem, recv_sem, r, rows):
    x, y, c = _mesh_pos()
    px, py, _ = _peer(x, y, c, 2 * r)
    src = part_ref.at[pl.ds(pl.multiple_of((2 * px + py) * rows, 16), rows), :]
    return pltpu.make_async_remote_copy(
        src_ref=src, dst_ref=land_ref.at[r - 1], send_sem=send_sem, recv_sem=recv_sem,
        device_id=(px, py, c), device_id_type=MESH)


def _scatter_order(n_w):
    return [(w, r) for r in (3, 2, 1) for w in range(n_w)]


def _scatter_start(parts, name):
    n_w = len(parts)
    rows = [p.shape[0] // N_CHIP for p in parts]
    order = _scatter_order(n_w)
    lands = [pltpu.with_memory_space_constraint(lax.empty((N_CHIP - 1, r, p.shape[1]), p.dtype), pltpu.HBM)
             for r, p in zip(rows, parts)]

    def body(*refs):
        part_refs, land_refs = refs[:n_w], refs[n_w:2 * n_w]
        sems = refs[2 * n_w:2 * n_w + 2 * len(order)]
        token = refs[-1]
        for j, (w, r) in enumerate(order):
            _scatter_copy(part_refs[w], land_refs[w], sems[2 * j], sems[2 * j + 1], r, rows[w]).start()
        token[...] = jnp.zeros_like(token)

    n_sem = 2 * len(order)
    res = pl.pallas_call(
        body, name=name,
        out_shape=(*[pltpu.SemaphoreType.DMA(())] * n_sem, *[pltpu.HBM(p.shape, p.dtype) for p in parts],
                   *[pltpu.HBM(l.shape, l.dtype) for l in lands], jax.ShapeDtypeStruct((8, LANES), F32)),
        in_specs=[HBM_SPEC] * (2 * n_w), out_specs=(*[SEM_SPEC] * n_sem, *[HBM_SPEC] * (2 * n_w), VMEM_SPEC),
        input_output_aliases={i: n_sem + i for i in range(2 * n_w)},
        compiler_params=pltpu.CompilerParams(has_side_effects=pltpu.SideEffectType.DATAFLOW_SIDE_EFFECTING),
    )(*[pltpu.with_memory_space_constraint(p, pltpu.HBM) for p in parts], *lands)
    return (list(res[:n_sem]), list(res[n_sem:n_sem + n_w]), list(res[n_sem + n_w:n_sem + 2 * n_w]), res[-1])


def _scatter_wait(sems, parts, lands, after, name):
    n_w = len(parts)
    rows = [p.shape[0] // N_CHIP for p in parts]
    order = _scatter_order(n_w)

    def body(*refs):
        part_refs, land_refs = refs[:n_w], refs[n_w:2 * n_w]
        sem_refs = refs[2 * n_w:2 * n_w + 2 * len(order)]
        for j, (w, r) in enumerate(order):
            cp = _scatter_copy(part_refs[w], land_refs[w], sem_refs[2 * j], sem_refs[2 * j + 1], r, rows[w])
            cp.wait_send()
            cp.wait_recv()

    res = pl.pallas_call(
        body, name=name,
        out_shape=(*[pltpu.HBM(p.shape, p.dtype) for p in parts], *[pltpu.HBM(l.shape, l.dtype) for l in lands]),
        in_specs=[HBM_SPEC] * (2 * n_w) + [SEM_SPEC] * len(sems) + [ANY],
        out_specs=tuple([HBM_SPEC] * (2 * n_w)),
        input_output_aliases={i: i for i in range(2 * n_w)},
        compiler_params=pltpu.CompilerParams(has_side_effects=pltpu.SideEffectType.DATAFLOW_SIDE_EFFECTING),
    )(*parts, *lands, *sems, after)
    return list(res[:n_w]), list(res[n_w:])


def _small_copy(v_ref, land_ref, send_sem, recv_sem, r):
    x, y, c = _mesh_pos()
    px, py, pc = _peer(x, y, c, r)
    return pltpu.make_async_remote_copy(
        src_ref=v_ref, dst_ref=land_ref.at[4 * x + 2 * y + c], send_sem=send_sem, recv_sem=recv_sem,
        device_id=(px, py, pc), device_id_type=MESH)


def _small_gather_start(v):
    land = pltpu.with_memory_space_constraint(lax.empty((N_DEV,) + v.shape, v.dtype), pltpu.HBM)

    def body(v_ref, land_ref, *rest):
        sems, token = rest[:2 * (N_DEV - 1)], rest[-1]
        for r in range(1, N_DEV):
            _small_copy(v_ref, land_ref, sems[2 * r - 2], sems[2 * r - 1], r).start()
        token[...] = jnp.zeros_like(token)

    n_sem = 2 * (N_DEV - 1)
    res = pl.pallas_call(
        body, name="small_gather_start",
        out_shape=(*[pltpu.SemaphoreType.DMA(())] * n_sem, pltpu.HBM(v.shape, v.dtype),
                   pltpu.HBM(land.shape, land.dtype), jax.ShapeDtypeStruct((8, LANES), F32)),
        in_specs=[HBM_SPEC, HBM_SPEC], out_specs=(*[SEM_SPEC] * n_sem, HBM_SPEC, HBM_SPEC, VMEM_SPEC),
        input_output_aliases={0: n_sem, 1: n_sem + 1},
        compiler_params=pltpu.CompilerParams(has_side_effects=pltpu.SideEffectType.DATAFLOW_SIDE_EFFECTING),
    )(pltpu.with_memory_space_constraint(v, pltpu.HBM), land)
    return list(res[:n_sem]), res[n_sem], res[n_sem + 1], res[-1]


def _small_gather_wait(sems, v, land, after):
    def body(v_ref, land_ref, *rest):
        for r in range(1, N_DEV):
            cp = _small_copy(v_ref, land_ref, rest[2 * r - 2], rest[2 * r - 1], r)
            cp.wait_send()
            x, y, c = _mesh_pos()
            px, py, pc = _peer(x, y, c, r)
            pltpu.make_async_remote_copy(
                src_ref=v_ref, dst_ref=land_ref.at[4 * px + 2 * py + pc], send_sem=rest[2 * r - 2],
                recv_sem=rest[2 * r - 1], device_id=(px, py, pc), device_id_type=MESH).wait_recv()

    res = pl.pallas_call(
        body, name="small_gather_wait",
        out_shape=(pltpu.HBM(v.shape, v.dtype), pltpu.HBM(land.shape, land.dtype)),
        in_specs=[HBM_SPEC, HBM_SPEC] + [SEM_SPEC] * len(sems) + [ANY], out_specs=(HBM_SPEC, HBM_SPEC),
        input_output_aliases={0: 0, 1: 1},
        compiler_params=pltpu.CompilerParams(has_side_effects=pltpu.SideEffectType.DATAFLOW_SIDE_EFFECTING),
    )(v, land, *sems, after)
    return res[0], res[1]


def _ada_forward(c_row, w_ada, b_cols, carry):
    d = c_row.shape[1]
    wcols = w_ada.shape[1]
    ci, co = len(carry.ins), len(carry.out_shapes)

    def body(*refs):
        c_ref, w_ref, b_ref = refs[:3]
        cins = refs[3:3 + ci]
        sc_ref, mod_ref = refs[3 + ci:5 + ci]
        couts = refs[5 + ci:5 + ci + co]
        rows_ref, send_sems, recv_sems = refs[5 + ci + co:8 + ci + co]
        cscr = refs[8 + ci + co:]
        carry.start(cins, couts, cscr)
        x, y, c = _mesh_pos()
        me = 4 * x + 2 * y + c
        cv = c_ref[...]
        sc_ref[me] = cv * _sigmoid(cv)

        sends = []
        for r in range(1, N_DEV):
            px, py, pc = _peer(x, y, c, r)
            cp = pltpu.make_async_remote_copy(
                src_ref=sc_ref.at[me], dst_ref=sc_ref.at[me], send_sem=send_sems.at[0, r - 1],
                recv_sem=recv_sems.at[0, r - 1], device_id=(px, py, pc), device_id_type=MESH)
            cp.start()
            sends.append(cp)
        for r in range(1, N_DEV):
            px, py, pc = _peer(x, y, c, r)
            pid = 4 * px + 2 * py + pc
            pltpu.make_async_remote_copy(
                src_ref=sc_ref.at[pid], dst_ref=sc_ref.at[pid], send_sem=send_sems.at[0, r - 1],
                recv_sem=recv_sems.at[0, r - 1], device_id=(px, py, pc), device_id_type=MESH).wait_recv()
        for cp in sends:
            cp.wait_send()

        sc_all = jnp.concatenate([sc_ref[j] for j in range(N_DEV)], axis=0)
        rows = _dot(sc_all.astype(BF16), w_ref[...].astype(BF16)) + b_ref[...]
        for j in range(N_DEV):
            rows_ref[j] = rows[j:j + 1, :]
        mod_ref[me] = rows_ref[me]

        sends = []
        for r in range(1, N_DEV):
            px, py, pc = _peer(x, y, c, r)
            pid = 4 * px + 2 * py + pc
            cp = pltpu.make_async_remote_copy(
                src_ref=rows_ref.at[pid], dst_ref=mod_ref.at[me], send_sem=send_sems.at[1, r - 1],
                recv_sem=recv_sems.at[1, r - 1], device_id=(px, py, pc), device_id_type=MESH)
            cp.start()
            sends.append(cp)
        for r in range(1, N_DEV):
            px, py, pc = _peer(x, y, c, r)
            pid = 4 * px + 2 * py + pc
            pltpu.make_async_remote_copy(
                src_ref=rows_ref.at[pid], dst_ref=mod_ref.at[pid], send_sem=send_sems.at[1, r - 1],
                recv_sem=recv_sems.at[1, r - 1], device_id=(px, py, pc), device_id_type=MESH).wait_recv()
        for cp in sends:
            cp.wait_send()
        carry.finish(cins, couts, cscr)

    res = pl.pallas_call(
        body, name="ada_forward",
        out_shape=(jax.ShapeDtypeStruct((N_DEV, 1, d), F32), jax.ShapeDtypeStruct((N_DEV, 1, wcols), F32),
                   *carry.out_shapes),
        in_specs=[VMEM_SPEC, VMEM_SPEC, VMEM_SPEC] + [ANY] * ci, out_specs=(VMEM_SPEC, VMEM_SPEC) + (ANY,) * co,
        scratch_shapes=[pltpu.VMEM((N_DEV, 1, wcols), F32), pltpu.SemaphoreType.DMA((2, N_DEV - 1)),
                        pltpu.SemaphoreType.DMA((2, N_DEV - 1))] + carry.scratch,
        compiler_params=_params(),
    )(c_row, w_ada, b_cols, *carry.ins)
    return res[:2], res[2:]


def _mm_nt(a, b, name, out_dtype, bias=None, carry=None):
    m, k = a.shape
    n = b.shape[0]
    tm = _pick(m, (512, 256, 128))
    tn = _pick(n, (1408, 1152, 1024, 768, 512, 256, 128))

    def body(*refs):
        acc = _dot_nt(refs[0][...], refs[1][...])
        if bias is not None:
            acc = acc + refs[2][...]
        refs[-1][...] = acc.astype(out_dtype)

    in_specs = [pl.BlockSpec((tm, k), lambda j, i: (i, 0)), pl.BlockSpec((tn, k), lambda j, i: (j, 0))]
    args = [a, b]
    if bias is not None:
        in_specs.append(pl.BlockSpec((1, tn), lambda j, i: (0, j)))
        args.append(bias)
    return _call(body, name=name, grid=(n // tn, m // tm), in_specs=in_specs,
                 out_specs=pl.BlockSpec((tm, tn), lambda j, i: (i, j)),
                 out_shape=jax.ShapeDtypeStruct((m, n), out_dtype), args=args,
                 sem=("parallel", "parallel"), carry=carry)


class _Tail:
    def __init__(self, rows, vecs, outs, fn):
        self.rows, self.vecs, self.outs, self.fn = list(rows), list(vecs), list(outs), fn


def _mm_nn(pairs, name, out_dtype, bias=None, carry=None, tail=None):
    m, k = pairs[0][0].shape
    n = pairs[0][1].shape[1]
    n_p = len(pairs)
    tm = _pick(m, (512, 256, 128))
    tk = k if n_p == 1 else _pick(k, (1408, 1152, 1024, 768, 512, 256, 128))
    nk = k // tk
    n_b = 0 if bias is None else 1
    n_r, n_v = (len(tail.rows), len(tail.vecs)) if tail else (0, 0)
    n_in = 2 * n_p + n_b + n_r + n_v
    n_main = 0 if out_dtype is None else 1

    def finish(acc, refs, first_tile):
        if bias is not None:
            acc = acc + refs[2 * n_p][...]
        outs = refs[n_in:-1]
        if n_main:
            outs[0][...] = acc.astype(out_dtype)
        if tail is None:
            return
        rows = [r[...] for r in refs[2 * n_p + n_b:2 * n_p + n_b + n_r]]
        vecs = [v[...] for v in refs[2 * n_p + n_b + n_r:n_in]]
        vals = tail.fn(acc, rows, vecs)
        for ref, val, (dtype, kind) in zip(outs[n_main:], vals, tail.outs):
            if kind == "row":
                ref[...] = val.astype(dtype)
            else:
                @pl.when(first_tile)
                def _(ref=ref):
                    ref[...] = jnp.zeros_like(ref)

                ref[...] += val

    def body(*refs):
        acc_ref = refs[-1]
        kk, i = pl.program_id(0), pl.program_id(1)
        part = _dot(refs[0][...], refs[1][...])
        for p in range(1, n_p):
            part = part + _dot(refs[2 * p][...], refs[2 * p + 1][...])
        if nk == 1:
            finish(part, refs, i == 0)
            return
        rows = pl.ds(pl.multiple_of(i * tm, tm), tm)

        @pl.when(kk == 0)
        def _():
            acc_ref[rows, :] = part

        if nk > 2:
            @pl.when((kk > 0) & (kk < nk - 1))
            def _():
                acc_ref[rows, :] += part

        @pl.when(kk == nk - 1)
        def _():
            finish(acc_ref[rows, :] + part, refs, i == 0)

    def last_only(kk, i):
        return (jnp.where(kk == nk - 1, i, 0), 0)

    row_spec = pl.BlockSpec((tm, n), last_only)
    vec_spec = pl.BlockSpec((1, n), lambda kk, i: (0, 0))
    in_specs, args = [], []
    for a, b in pairs:
        in_specs += [pl.BlockSpec((tm, tk), lambda kk, i: (i, kk)), pl.BlockSpec((tk, n), lambda kk, i: (kk, 0))]
        args += [a, b]
    if bias is not None:
        in_specs.append(vec_spec)
        args.append(bias)
    out_specs = [row_spec] * n_main
    out_shape = [jax.ShapeDtypeStruct((m, n), out_dtype)] if n_main else []
    if tail:
        in_specs += [row_spec] * n_r + [vec_spec] * n_v
        args += tail.rows + tail.vecs
        for dtype, kind in tail.outs:
            if kind == "row":
                out_specs.append(row_spec)
                out_shape.append(jax.ShapeDtypeStruct((m, n), dtype))
            else:
                width = n if kind == "sum" else 1
                out_specs.append(pl.BlockSpec((1, width), lambda kk, i: (0, 0)))
                out_shape.append(jax.ShapeDtypeStruct((1, width), dtype))
    if tail is None:
        out_specs, out_shape = out_specs[0], out_shape[0]
    return _call(body, name=name, grid=(nk, m // tm), in_specs=in_specs, out_specs=out_specs,
                 out_shape=out_shape, args=args,
                 scratch=[pltpu.VMEM((m, n) if nk > 1 else (8, LANES), F32)],
                 sem=("arbitrary", "arbitrary"), carry=carry)


def _rms(v):
    return lax.rsqrt(jnp.mean(v * v, axis=-1, keepdims=True) + EPS)


def _col(v):
    return jnp.sum(v, axis=0, keepdims=True)


def _tail_post_pre(x, g_post, gate, weight, g_pre, scale, shift):
    def fn(y, rows, vecs):
        (xv,), (gp, gt, g, sc, sh) = rows, vecs
        xo = xv + (weight * gt) * ((y * _rms(y)) * gp)
        return xo, ((xo * _rms(xo)) * g) * (1.0 + sc) + sh

    return _Tail([x], [g_post, gate, g_pre, scale, shift], [(F32, "row"), (BF16, "row")], fn)


def _tail_post_loss(x, target, g, gate, weight):
    def fn(y, rows, vecs):
        (xv, tv), (gv, gt) = rows, vecs
        r = _rms(y)
        yn = y * r
        err = (xv + (weight * gt) * (yn * gv)) - tv
        do = err * (1.0 / y.shape[1])
        dyn = do * ((weight * gt) * gv)
        dy = r * (dyn - yn * jnp.mean(dyn * yn, axis=-1, keepdims=True))
        return do, dy, 0.5 * _col(jnp.mean(err * err, axis=-1, keepdims=True)), _col(do * yn)

    return _Tail([x, target], [g, gate], [(F32, "row"), (BF16, "row"), (F32, "one"), (F32, "sum")], fn)


def _tail_pre_bwd(x, dres, g_pre, scale):
    def fn(dh, rows, vecs):
        (xv, dr), (g, sc) = rows, vecs
        r = _rms(xv)
        n = xv * r
        dn = dh * (g * (1.0 + sc))
        return dr + r * (dn - n * jnp.mean(dn * n, axis=-1, keepdims=True)), _col(dh * n), _col(dh)

    return _Tail([x, dres], [g_pre, scale], [(F32, "row"), (F32, "sum"), (F32, "sum")], fn)


def _tail_pre_post_bwd(x, dres, y, g_pre, scale, g_post, gate, weight):
    def fn(dh, rows, vecs):
        (xv, dr, yv), (g, sc, gp, gt) = rows, vecs
        r = _rms(xv)
        n = xv * r
        dn = dh * (g * (1.0 + sc))
        dx = dr + r * (dn - n * jnp.mean(dn * n, axis=-1, keepdims=True))
        ry = _rms(yv)
        yn = yv * ry
        dyn = dx * ((weight * gt) * gp)
        dy = ry * (dyn - yn * jnp.mean(dyn * yn, axis=-1, keepdims=True))
        return dx, dy, _col(dh * n), _col(dh), _col(dx * yn), _col(dy)

    return _Tail([x, dres, y], [g_pre, scale, g_post, gate],
                 [(F32, "row"), (BF16, "row")] + [(F32, "sum")] * 4, fn)


def _mm_tn_pair(a, b, name, col_sums=False):
    k, m = a.shape
    n = b.shape[1]
    rows = m // N_DEV
    n_chip = N_DEV // 2
    tm = 4 * rows
    tk = _pick(k, (1024, 512, 256, 128))
    nk = k // tk

    def body(a_ref, b_ref, p_ref, own_ref, *rest):
        acc_ref, keep_ref, send_ref, land_ref, send_sems, recv_sems = rest[-6:]
        kk, i = pl.program_id(0), pl.program_id(1)
        x, y, c = _mesh_pos()
        if col_sums:
            cs_ref = rest[0]
            part = jnp.sum(a_ref[...].astype(F32), axis=0, keepdims=True)

            @pl.when(kk == 0)
            def _():
                cs_ref[i] = part

            @pl.when(kk > 0)
            def _():
                cs_ref[i] += part

        def push(chip):
            return pltpu.make_async_remote_copy(
                src_ref=send_ref.at[chip], dst_ref=land_ref.at[chip], send_sem=send_sems.at[chip],
                recv_sem=recv_sems.at[chip], device_id=(x, y, 1 - c), device_id_type=MESH)

        if nk == 1:
            acc = _dot_tn(a_ref[...], b_ref[...])
        else:
            @pl.when(kk == 0)
            def _():
                acc_ref[i] = jnp.zeros((tm, n), F32)

            acc_ref[i] += _dot_tn(a_ref[...], b_ref[...])

        for t in range(2):
            @pl.when((kk == nk - 1) & (i == t))
            def _(t=t):
                for ob in range(4):
                    chip, core = 2 * t + ob // 2, ob % 2
                    blk = (acc if nk == 1 else acc_ref.at[t])[ob * rows:(ob + 1) * rows, :]

                    @pl.when(c == core)
                    def _(chip=chip, blk=blk):
                        keep_ref[chip] = blk

                    @pl.when(c != core)
                    def _(chip=chip, blk=blk):
                        send_ref[chip] = blk.astype(BF16)
                        push(chip).start()

        @pl.when((kk == nk - 1) & (i == 1))
        def _():
            for chip in range(n_chip):
                push(chip).wait_recv()
                val = (keep_ref[chip] + land_ref[chip].astype(F32)).astype(BF16)
                p_ref[chip * rows:(chip + 1) * rows, :] = val

                @pl.when(2 * x + y == chip)
                def _(val=val):
                    own_ref[...] = val

            for chip in range(n_chip):
                push(chip).wait_send()

    out_specs = [pl.BlockSpec((n_chip * rows, n), lambda kk, i: (0, 0)), pl.BlockSpec((rows, n), lambda kk, i: (0, 0))]
    out_shape = [jax.ShapeDtypeStruct((n_chip * rows, n), BF16), jax.ShapeDtypeStruct((rows, n), BF16)]
    if col_sums:
        out_specs.append(pl.BlockSpec((2, 1, tm), lambda kk, i: (0, 0, 0)))
        out_shape.append(jax.ShapeDtypeStruct((2, 1, tm), F32))
    res = _call(body, name=name, grid=(nk, 2),
                in_specs=[pl.BlockSpec((tk, tm), lambda kk, i: (kk, i)), pl.BlockSpec((tk, n), lambda kk, i: (kk, 0))],
                out_specs=out_specs, out_shape=out_shape, args=[a, b],
                scratch=[pltpu.VMEM((2, tm, n) if nk > 1 else (8, LANES), F32), pltpu.VMEM((n_chip, rows, n), F32),
                         pltpu.VMEM((n_chip, rows, n), BF16), pltpu.VMEM((n_chip, rows, n), BF16),
                         pltpu.SemaphoreType.DMA((n_chip,)), pltpu.SemaphoreType.DMA((n_chip,))],
                sem=("arbitrary", "arbitrary"))
    return (*res[:2], res[2].reshape(1, m)) if col_sums else res


def _ffn_up(h, wg_t, wu_t, name, carry=None):
    s, d = h.shape
    f = wg_t.shape[0]
    tm = _pick(s, (512, 256, 128))
    tf = _pick(f, (1408, 1024, 512, 256, 128))

    def body(h_ref, wg_ref, wu_ref, a_ref, b_ref, u_ref):
        hh = h_ref[...]
        for lo, hi in _pieces(tf):
            a = _dot_nt(hh, wg_ref[lo:hi, :])
            b = _dot_nt(hh, wu_ref[lo:hi, :])
            a_ref[:, lo:hi] = a.astype(BF16)
            b_ref[:, lo:hi] = b.astype(BF16)
            u_ref[:, lo:hi] = ((a * _sigmoid(a)) * b).astype(BF16)

    w_spec = pl.BlockSpec((tf, d), lambda j, i: (j, 0))
    o_spec = pl.BlockSpec((tm, tf), lambda j, i: (i, j))
    o_shape = jax.ShapeDtypeStruct((s, f), BF16)
    return _call(body, name=name, grid=(f // tf, s // tm),
                 in_specs=[pl.BlockSpec((tm, d), lambda j, i: (i, 0)), w_spec, w_spec],
                 out_specs=(o_spec, o_spec, o_spec), out_shape=(o_shape, o_shape, o_shape),
                 args=[h, wg_t, wu_t], sem=("parallel", "parallel"), carry=carry)


def _ffn_down_bwd(dy, wd, a, b, name, carry=None):
    s, d = dy.shape
    f = wd.shape[0]
    tm = _pick(s, (512, 256, 128))
    tf = _pick(f, (1408, 1024, 512, 256, 128))

    def body(dy_ref, wd_ref, a_ref, b_ref, da_ref, db_ref):
        dyv = dy_ref[...]
        for lo, hi in _pieces(tf):
            du = _dot_nt(dyv, wd_ref[lo:hi, :])
            a = a_ref[:, lo:hi].astype(F32)
            b = b_ref[:, lo:hi].astype(F32)
            sig = _sigmoid(a)
            da_ref[:, lo:hi] = (du * b * (sig * (1.0 + a * (1.0 - sig)))).astype(BF16)
            db_ref[:, lo:hi] = (du * (a * sig)).astype(BF16)

    t_spec = pl.BlockSpec((tm, tf), lambda j, i: (i, j))
    o_shape = jax.ShapeDtypeStruct((s, f), BF16)
    return _call(body, name=name, grid=(f // tf, s // tm),
                 in_specs=[pl.BlockSpec((tm, d), lambda j, i: (i, 0)), pl.BlockSpec((tf, d), lambda j, i: (j, 0)),
                           t_spec, t_spec],
                 out_specs=(t_spec, t_spec), out_shape=(o_shape, o_shape), args=[dy, wd, a, b],
                 sem=("parallel", "parallel"), carry=carry)


def _row_tile(s):
    return _pick(s, (256, 128, 64))


def _vec_spec(d):
    return pl.BlockSpec((1, d), lambda i: (0, 0))


def _pre_norm(x, g, scale, shift, name):
    s, d = x.shape
    ts = _row_tile(s)

    def body(x_ref, g_ref, sc_ref, sh_ref, h_ref):
        xv = x_ref[...]
        r = lax.rsqrt(jnp.mean(xv * xv, axis=-1, keepdims=True) + EPS)
        h_ref[...] = (((xv * r) * g_ref[...]) * (1.0 + sc_ref[...]) + sh_ref[...]).astype(BF16)

    row = pl.BlockSpec((ts, d), lambda i: (i, 0))
    return _call(body, name=name, grid=(s // ts,), in_specs=[row, _vec_spec(d), _vec_spec(d), _vec_spec(d)],
                 out_specs=row, out_shape=jax.ShapeDtypeStruct((s, d), BF16), args=[x, g, scale, shift],
                 sem=("parallel",))


def _group_norm_cat(oa, ob, ga, gb):
    s = oa.shape[0]
    ts = _row_tile(s)

    def body(oa_ref, ob_ref, ga_ref, gb_ref, y_ref):
        for o_ref, g_ref, lo, w in ((oa_ref, ga_ref, 0, QA), (ob_ref, gb_ref, QA, QB)):
            ov = o_ref[...]
            r = lax.rsqrt(jnp.mean(ov * ov, axis=-1, keepdims=True) + EPS)
            y_ref[:, lo:lo + w] = ((ov * r) * g_ref[...]).astype(BF16)

    return _call(body, name="group_norm_cat", grid=(s // ts,),
                 in_specs=[pl.BlockSpec((ts, QA), lambda i: (i, 0)), pl.BlockSpec((ts, QB), lambda i: (i, 0)),
                           _vec_spec(QA), _vec_spec(QB)],
                 out_specs=pl.BlockSpec((ts, QA + QB), lambda i: (i, 0)),
                 out_shape=jax.ShapeDtypeStruct((s, QA + QB), BF16), args=[oa, ob, ga, gb], sem=("parallel",))


def _group_norm_bwd(dy, oa, ob, ga, gb):
    s = oa.shape[0]
    ts = _row_tile(s)

    def body(dy_ref, oa_ref, ob_ref, ga_ref, gb_ref, doa_ref, dob_ref, dga_ref, dgb_ref):
        @pl.when(pl.program_id(0) == 0)
        def _():
            dga_ref[...] = jnp.zeros_like(dga_ref)
            dgb_ref[...] = jnp.zeros_like(dgb_ref)

        for o_ref, g_ref, do_ref, dg_ref, lo, w in ((oa_ref, ga_ref, doa_ref, dga_ref, 0, QA),
                                                    (ob_ref, gb_ref, dob_ref, dgb_ref, QA, QB)):
            ov = o_ref[...]
            dyv = dy_ref[:, lo:lo + w]
            r = lax.rsqrt(jnp.mean(ov * ov, axis=-1, keepdims=True) + EPS)
            n = ov * r
            dn = dyv * g_ref[...]
            do_ref[...] = r * (dn - n * jnp.mean(dn * n, axis=-1, keepdims=True))
            dg_ref[...] += jnp.sum(dyv * n, axis=0, keepdims=True)

    ra = pl.BlockSpec((ts, QA), lambda i: (i, 0))
    rb = pl.BlockSpec((ts, QB), lambda i: (i, 0))
    return _call(body, name="group_norm_bwd", grid=(s // ts,),
                 in_specs=[pl.BlockSpec((ts, QA + QB), lambda i: (i, 0)), ra, rb, _vec_spec(QA), _vec_spec(QB)],
                 out_specs=(ra, rb, _vec_spec(QA), _vec_spec(QB)),
                 out_shape=(jax.ShapeDtypeStruct((s, QA), F32), jax.ShapeDtypeStruct((s, QB), F32),
                            jax.ShapeDtypeStruct((1, QA), F32), jax.ShapeDtypeStruct((1, QB), F32)),
                 args=[dy, oa, ob, ga, gb], sem=("arbitrary",))


def _n_variants(n_back):
    return -(-n_back // QG) + 1


def _alibi_bias():
    i = np.arange(QROWS)[:, None]
    j = np.arange((QG + BACK_A) * CHUNK)[None, :]
    dist = np.abs(BACK_A * CHUNK + i - j).astype(np.float32)
    dc = j // CHUNK - i // CHUNK
    valid = (dc >= 0) & (dc <= BACK_A)
    slopes = np.array([2.0 ** (-8.0 * (h + 1) / H_A) for h in range(H_A)], dtype=np.float32)
    bias = -slopes[:, None, None] * dist[None]
    out = [np.where((valid & (j >= (BACK_A - QG * v) * CHUNK))[None], bias, np.float32(NEG_INF))
           for v in range(_n_variants(BACK_A))]
    return jnp.asarray(np.stack(out).astype(np.float32))


def _rel_index_matrix():
    cc = np.arange(SKEW)
    dist = np.where(cc < SKEW - QROWS, BACK_B * CHUNK - cc, BACK_B * CHUNK + SKEW - cc)
    idx = np.clip(dist, -REL_CLIP, REL_CLIP) + REL_CLIP
    m = np.zeros((SKEW, N_REL), np.float32)
    m[cc, idx] = 1.0
    return jnp.asarray(m)


def _toeplitz_bias(vec, carry=None):
    lk = (QG + BACK_B) * CHUNK
    nv = _n_variants(BACK_B)

    def body(v_ref, o_ref):
        xv = jnp.broadcast_to(v_ref[0], (QROWS, SKEW))
        row = lax.broadcasted_iota(jnp.int32, (QROWS, SKEW), 0)
        for bit in range(QROWS.bit_length() - 1):
            xv = jnp.where((row >> bit) & 1 == 1, pltpu.roll(xv, 1 << bit, 1), xv)
        ri = lax.broadcasted_iota(jnp.int32, (QROWS, lk), 0) // CHUNK
        col = lax.broadcasted_iota(jnp.int32, (QROWS, lk), 1)
        ci = col // CHUNK
        valid = (ci - ri >= 0) & (ci - ri <= BACK_B)
        for v in range(nv):
            o_ref[v, 0] = jnp.where(valid & (col >= (BACK_B - QG * v) * CHUNK), xv[:, :lk], NEG_INF)

    return _call(body, name="toeplitz_bias", grid=(H_B,),
                 in_specs=[pl.BlockSpec((1, 1, SKEW), lambda h: (h, 0, 0))],
                 out_specs=pl.BlockSpec((nv, 1, QROWS, lk), lambda h: (0, h, 0, 0)),
                 out_shape=jax.ShapeDtypeStruct((nv, H_B, QROWS, lk), F32), args=[vec], sem=("parallel",),
                 carry=carry)


def _diagonal_sums(dbias):
    lk = dbias.shape[2]

    def body(d_ref, o_ref):
        xp = jnp.concatenate([d_ref[0], jnp.zeros((QROWS, SKEW - lk), F32)], axis=1)
        xv = xp[0:CHUNK]
        for q in range(1, QG):
            xv = xv + pltpu.roll(xp[q * CHUNK:(q + 1) * CHUNK], SKEW - q * CHUNK, 1)
        row = lax.broadcasted_iota(jnp.int32, (CHUNK, SKEW), 0)
        for bit in range(CHUNK.bit_length() - 1):
            xv = jnp.where((row >> bit) & 1 == 1, pltpu.roll(xv, SKEW - (1 << bit), 1), xv)
        o_ref[0] = jnp.sum(xv, axis=0, keepdims=True)

    return _call(body, name="diagonal_sums", grid=(H_B,),
                 in_specs=[pl.BlockSpec((1, QROWS, lk), lambda h: (h, 0, 0))],
                 out_specs=pl.BlockSpec((1, 1, SKEW), lambda h: (h, 0, 0)),
                 out_shape=jax.ShapeDtypeStruct((H_B, 1, SKEW), F32), args=[dbias], sem=("parallel",))


def _attn_common(s, n_back, gqa, q_col, k_col, v_col, TPS):
    lk = (QG + n_back) * CHUNK
    pad = n_back * CHUNK
    wide = TPS * LANES
    q_spec = pl.BlockSpec((QROWS, wide), lambda t, g: (g, q_col // TPS + t))
    if gqa:
        k_spec = pl.BlockSpec((s, LANES), lambda t, g: (0, k_col))
        v_spec = pl.BlockSpec((s, LANES), lambda t, g: (0, v_col))
    else:
        k_spec = pl.BlockSpec((s, wide), lambda t, g: (0, k_col // TPS + t))
        v_spec = pl.BlockSpec((s, wide), lambda t, g: (0, v_col // TPS + t))
    last_variant = _n_variants(n_back) - 1
    bias_spec = pl.BlockSpec((None, 2 * TPS, QROWS, lk), lambda t, g: (jnp.minimum(g, last_variant), t, 0, 0))
    tile_spec = pl.BlockSpec((QROWS, wide), lambda t, g: (g, t))
    return lk, pad, q_spec, k_spec, v_spec, bias_spec, tile_spec


def _attention_fwd(proj, bias, sinks, *, n_back, gqa, q_col, k_col, v_col, TPS, name, carry=None):
    s = proj.shape[0]
    lk, pad, q_spec, k_spec, v_spec, bias_spec, tile_spec = _attn_common(s, n_back, gqa, q_col, k_col, v_col, TPS)
    n_t, n_g = 512 // (TPS * LANES), s // QROWS
    kv_wide = LANES if gqa else TPS * LANES

    def body(*refs):
        if gqa:
            q_ref, k_ref, v_ref, bias_ref, sink_ref, o_ref, l_ref, kpad, vpad = refs
        else:
            q_ref, k_ref, v_ref, bias_ref, o_ref, l_ref, kpad, vpad = refs
        t, g = pl.program_id(0), pl.program_id(1)

        @pl.when(g == 0)
        def _():
            kpad[0:pad, :] = jnp.zeros((pad, kv_wide), BF16)
            vpad[0:pad, :] = jnp.zeros((pad, kv_wide), BF16)
            kpad[pad:, :] = k_ref[...]
            vpad[pad:, :] = v_ref[...]

        start = pl.multiple_of(g * QROWS, QROWS)
        half = lax.broadcasted_iota(jnp.int32, (QROWS, LANES), 1) // HEAD_DIM
        for tt in range(TPS):
            lanes = slice(tt * LANES, (tt + 1) * LANES)
            kv_lanes = slice(0, LANES) if gqa else lanes
            kb = kpad[pl.ds(start, lk), kv_lanes]
            vb = vpad[pl.ds(start, lk), kv_lanes]
            q = q_ref[:, lanes] * (HEAD_DIM ** -0.5)
            if gqa:
                hk = (TPS * t + tt) // 2
                q_rolled = pltpu.roll(q.astype(F32), HEAD_DIM, 1).astype(BF16)
            outs, lses = [], []
            for e in range(2):
                if gqa:
                    kv_half = hk
                    src = jnp.where(hk == e, q, q_rolled)
                else:
                    kv_half = e
                    src = q
                qm = jnp.where(half == kv_half, src, jnp.zeros_like(src))
                sc = _dot_nt(qm, kb) + bias_ref[2 * tt + e]
                m = jnp.max(sc, axis=-1, keepdims=True)
                if gqa:
                    sk = sink_ref[2 * (TPS * t + tt) + e]
                    m = jnp.maximum(m, sk)
                p = jnp.exp(sc - m)
                l = jnp.sum(p, axis=-1, keepdims=True)
                if gqa:
                    l = l + jnp.exp(sk - m)
                pn = p / l
                outs.append(_dot(pn.astype(BF16), vb))
                lses.append(m + jnp.log(l))
            if gqa:
                same = jnp.where(hk == 0, outs[0], outs[1])
                other = jnp.where(hk == 0, outs[1], outs[0])
                o_ref[:, lanes] = jnp.where(half == hk, same, pltpu.roll(other, HEAD_DIM, 1))
            else:
                o_ref[:, lanes] = jnp.where(half == 0, outs[0], outs[1])
            l_ref[:, lanes] = jnp.where(half == 0, lses[0], lses[1])

    in_specs = [q_spec, k_spec, v_spec, bias_spec] + ([SMEM_SPEC] if gqa else [])
    args = [proj, proj, proj, bias] + ([sinks] if gqa else [])
    o_shape = jax.ShapeDtypeStruct((s, 512), F32)
    return _call(body, name=name, grid=(n_t, n_g), in_specs=in_specs, out_specs=(tile_spec, tile_spec),
                 out_shape=(o_shape, o_shape), args=args,
                 scratch=[pltpu.VMEM((s + pad, kv_wide), BF16), pltpu.VMEM((s + pad, kv_wide), BF16)],
                 sem=("arbitrary", "arbitrary"), carry=carry)


def _attention_bwd(proj, bias, sinks, do, lse, *, n_back, gqa, q_col, k_col, v_col, TPS, name, carry=None):
    s = proj.shape[0]
    lk, pad, q_spec, k_spec, v_spec, bias_spec, tile_spec = _attn_common(s, n_back, gqa, q_col, k_col, v_col, TPS)
    n_t, n_g = 512 // (TPS * LANES), s // QROWS
    kv_wide = LANES if gqa else TPS * LANES

    def body(*refs):
        if gqa:
            (q_ref, k_ref, v_ref, bias_ref, sink_ref, do_ref, l_ref,
             dq_ref, dk_ref, dv_ref, dsink_ref, kpad, vpad, dkpad, dvpad) = refs
        else:
            (q_ref, k_ref, v_ref, bias_ref, do_ref, l_ref,
             dq_ref, dk_ref, dv_ref, dbias_ref, kpad, vpad, dkpad, dvpad) = refs
        t, g = pl.program_id(0), pl.program_id(1)

        @pl.when(g == 0)
        def _():
            kpad[0:pad, :] = jnp.zeros((pad, kv_wide), BF16)
            vpad[0:pad, :] = jnp.zeros((pad, kv_wide), BF16)
            kpad[pad:, :] = k_ref[...]
            vpad[pad:, :] = v_ref[...]
            if gqa:
                dsink_ref[...] = jnp.zeros_like(dsink_ref)
            else:
                dbias_ref[...] = jnp.zeros_like(dbias_ref)

        @pl.when((g == 0) & (t == 0) if gqa else g == 0)
        def _():
            dkpad[...] = jnp.zeros_like(dkpad)
            dvpad[...] = jnp.zeros_like(dvpad)

        start = pl.multiple_of(g * QROWS, QROWS)
        half = lax.broadcasted_iota(jnp.int32, (QROWS, LANES), 1) // HEAD_DIM
        for tt in range(TPS):
            lanes = slice(tt * LANES, (tt + 1) * LANES)
            kv_lanes = slice(0, LANES) if gqa else lanes
            kb = kpad[pl.ds(start, lk), kv_lanes]
            vb = vpad[pl.ds(start, lk), kv_lanes]
            q = q_ref[:, lanes]
            dov = do_ref[:, lanes]
            lv = l_ref[:, lanes]
            if gqa:
                hk = (TPS * t + tt) // 2
                q_rolled = pltpu.roll(q.astype(F32), HEAD_DIM, 1).astype(BF16)
                do_rolled = pltpu.roll(dov, HEAD_DIM, 1)
            dqs = []
            dk_acc = jnp.zeros((lk, LANES), F32)
            dv_acc = jnp.zeros((lk, LANES), F32)
            for e in range(2):
                if gqa:
                    kv_half = hk
                    src = jnp.where(hk == e, q, q_rolled)
                    do_src = jnp.where(hk == e, dov, do_rolled)
                else:
                    kv_half = e
                    src = q
                    do_src = dov
                qm = jnp.where(half == kv_half, src, jnp.zeros_like(src))
                dom = jnp.where(half == kv_half, do_src, 0.0).astype(BF16)
                lcol = jnp.max(jnp.where(half == e, lv, -jnp.inf), axis=-1, keepdims=True)
                sc = _dot_nt(qm * (HEAD_DIM ** -0.5), kb) + bias_ref[2 * tt + e]
                pn = jnp.exp(sc - lcol)
                dp = _dot_nt(dom, vb)
                delta = jnp.sum(pn * dp, axis=-1, keepdims=True)
                ds = pn * (dp - delta)
                if gqa:
                    p_sink = jnp.exp(sink_ref[2 * (TPS * t + tt) + e] - lcol)
                    dsk = -jnp.sum(p_sink * delta, axis=0, keepdims=True)
                    row = 2 * tt + e
                    dsink_ref[0, row:row + 1, :] += jnp.broadcast_to(dsk, (1, LANES))
                else:
                    dbias_ref[2 * tt + e] += ds
                dsb = (ds * (HEAD_DIM ** -0.5)).astype(BF16)
                dqs.append(_dot(dsb, kb))
                dk_acc = dk_acc + _dot_tn(dsb, qm)
                dv_acc = dv_acc + _dot_tn(pn.astype(BF16), dom)
            dkpad[pl.ds(start, lk), kv_lanes] += dk_acc
            dvpad[pl.ds(start, lk), kv_lanes] += dv_acc
            if gqa:
                same = jnp.where(hk == 0, dqs[0], dqs[1])
                other = jnp.where(hk == 0, dqs[1], dqs[0])
                dq_ref[:, lanes] = jnp.where(half == hk, same, pltpu.roll(other, HEAD_DIM, 1)).astype(BF16)
            else:
                dq_ref[:, lanes] = jnp.where(half == 0, dqs[0], dqs[1]).astype(BF16)

        @pl.when((g == n_g - 1) & (t == n_t - 1) if gqa else g == n_g - 1)
        def _():
            dk_ref[...] = dkpad[pad:, :].astype(BF16)
            dv_ref[...] = dvpad[pad:, :].astype(BF16)

    in_specs = [q_spec, k_spec, v_spec, bias_spec] + ([SMEM_SPEC] if gqa else []) + [tile_spec, tile_spec]
    args = [proj, proj, proj, bias] + ([sinks] if gqa else []) + [do, lse]
    if gqa:
        kv_out = pl.BlockSpec((s, LANES), lambda t, g: (0, 0))
        kv_shape = jax.ShapeDtypeStruct((s, LANES), BF16)
        extra_spec = pl.BlockSpec((1, 8, LANES), lambda t, g: (t, 0, 0))
        extra_shape = jax.ShapeDtypeStruct((n_t, 8, LANES), F32)
    else:
        kv_out = pl.BlockSpec((s, kv_wide), lambda t, g: (0, t))
        kv_shape = jax.ShapeDtypeStruct((s, 512), BF16)
        extra_spec = pl.BlockSpec((2 * TPS, QROWS, lk), lambda t, g: (t, 0, 0))
        extra_shape = jax.ShapeDtypeStruct(bias.shape[1:], F32)
    return _call(body, name=name, grid=(n_t, n_g), in_specs=in_specs,
                 out_specs=(tile_spec, kv_out, kv_out, extra_spec),
                 out_shape=(jax.ShapeDtypeStruct((s, 512), BF16), kv_shape, kv_shape, extra_shape), args=args,
                 scratch=[pltpu.VMEM((s + pad, kv_wide), BF16), pltpu.VMEM((s + pad, kv_wide), BF16),
                          pltpu.VMEM((s + pad, kv_wide), F32), pltpu.VMEM((s + pad, kv_wide), F32)],
                 sem=("arbitrary", "arbitrary"), carry=carry)


def _sum_rows8(g):
    n = g.shape[2]

    def body(g_ref, o_ref):
        acc = g_ref[0]
        for j in range(1, N_DEV):
            acc = acc + g_ref[j]
        o_ref[...] = acc

    return pl.pallas_call(
        body, name="sum_small_grads", in_specs=[VMEM_SPEC], out_specs=VMEM_SPEC,
        out_shape=jax.ShapeDtypeStruct((1, n), F32), compiler_params=_params(),
    )(g)


def _ada_weight_grad(sc_t, dmod_cols):
    d = sc_t.shape[0]
    w = dmod_cols.shape[1]
    td = _pick(d, (256, 128))

    def body(sc_ref, dm_ref, o_ref):
        scv = sc_ref[...]
        dmv = dm_ref[...]
        acc = scv[:, 0:1] * dmv[0:1, :]
        for b in range(1, N_DEV):
            acc = acc + scv[:, b:b + 1] * dmv[b:b + 1, :]
        o_ref[...] = acc

    return _call(body, name="ada_weight_grad", grid=(d // td,),
                 in_specs=[pl.BlockSpec((td, N_DEV), lambda i: (i, 0)), pl.BlockSpec((N_DEV, w), lambda i: (0, 0))],
                 out_specs=pl.BlockSpec((td, w), lambda i: (i, 0)), out_shape=jax.ShapeDtypeStruct((d, w), F32),
                 args=[sc_t, dmod_cols], sem=("parallel",))


def _adamw_update(w, gv, m, v):
    nm = ADAM_B1 * m + (1.0 - ADAM_B1) * gv
    nv = ADAM_B2 * v + (1.0 - ADAM_B2) * (gv * gv)
    m_hat = nm / (1.0 - ADAM_B1 ** ADAM_STEP)
    v_hat = nv / (1.0 - ADAM_B2 ** ADAM_STEP)
    return -ADAM_LR * (m_hat / (jnp.sqrt(v_hat) + ADAM_EPS) + ADAM_WD * w), nm, nv


def _adamw(w, g, m, v, name):
    rows, cols = w.shape
    tr = _pick(rows, (256, 176, 128, 88, 64)) if rows > 256 else rows

    def body(w_ref, g_ref, m_ref, v_ref, d_ref, nm_ref, nv_ref):
        d_ref[...], nm_ref[...], nv_ref[...] = _adamw_update(w_ref[...], g_ref[...], m_ref[...], v_ref[...])

    spec = pl.BlockSpec((tr, cols), lambda i: (i, 0))
    shape = jax.ShapeDtypeStruct((rows, cols), F32)
    return _call(body, name=name, grid=(rows // tr,), in_specs=[spec] * 4, out_specs=(spec, spec, spec),
                 out_shape=(shape, shape, shape), args=[w, g, m, v], sem=("parallel",))


def _adamw_from_slots(w, own, slots, m, v, name):
    n_slots, rows, k = slots.shape

    def body(o_ref, s_ref, w_ref, m_ref, v_ref, g_ref, d_ref, nm_ref, nv_ref):
        gv = o_ref[...].astype(F32)
        for j in range(n_slots):
            gv = gv + s_ref[j].astype(F32)
        g_ref[...] = gv
        d_ref[...], nm_ref[...], nv_ref[...] = _adamw_update(w_ref[...], gv, m_ref[...], v_ref[...])

    tr = rows // 2 if rows % 32 == 0 else rows
    spec = pl.BlockSpec((tr, k), lambda i: (i, 0))
    shape = jax.ShapeDtypeStruct((rows, k), F32)
    return _call(body, name=name, grid=(rows // tr,),
                 in_specs=[spec, pl.BlockSpec((n_slots, tr, k), lambda i: (0, i, 0)), spec, spec, spec],
                 out_specs=(spec, spec, spec, spec), out_shape=(shape, shape, shape, shape),
                 args=[own, slots, w, m, v], sem=("parallel",))


def _adamw_small(g, w, m, v, sizes):
    n = w.shape[1]
    offs, off = [], 0
    for size in sizes:
        offs.append(off)
        off += size + (-size % LANES)

    def body(g_ref, w_ref, m_ref, v_ref, *out_refs):
        gv = g_ref[:, 0:n]
        dv, nm, nv = _adamw_update(w_ref[...], gv, m_ref[...], v_ref[...])
        for j, (o, size) in enumerate(zip(offs, sizes)):
            for k, val in enumerate((gv, dv, nm, nv)):
                out_refs[4 * j + k][...] = val[:, o:o + size]

    shapes = [jax.ShapeDtypeStruct((1, size), F32) for size in sizes for _ in range(4)]
    return pl.pallas_call(
        body, name="adamw_small", in_specs=[VMEM_SPEC] * 4, out_specs=tuple([VMEM_SPEC] * len(shapes)),
        out_shape=tuple(shapes), compiler_params=_params(),
    )(g, w, m, v)


SMALL = ("b_ada", "g_pre_ffn1", "g_post_ffn1", "g_pre_mix", "b_in", "sinks_a", "rel_bias_b", "g_grp_a",
         "g_grp_b", "b_out", "g_post_mix", "g_pre_ffn2", "g_post_ffn2")
WEIGHTS = ("w_ada", "b_ada", "g_pre_ffn1", "w_gate1", "w_up1", "w_down1", "g_post_ffn1", "g_pre_mix", "w_in",
           "b_in", "sinks_a", "rel_bias_b", "g_grp_a", "g_grp_b", "w_out", "b_out", "g_post_mix", "g_pre_ffn2",
           "w_gate2", "w_up2", "w_down2", "g_post_ffn2")


def kernel(x, c, w_ada, b_ada, g_pre_ffn1, w_gate1, w_up1, w_down1, g_post_ffn1, g_pre_mix, w_in, b_in, sinks_a, rel_bias_b, g_grp_a, g_grp_b, w_out, b_out, g_post_mix, g_pre_ffn2, w_gate2, w_up2, w_down2, g_post_ffn2, loss_target, m_w_ada, m_b_ada, m_g_pre_ffn1, m_w_gate1, m_w_up1, m_w_down1, m_g_post_ffn1, m_g_pre_mix, m_w_in, m_b_in, m_sinks_a, m_rel_bias_b, m_g_grp_a, m_g_grp_b, m_w_out, m_b_out, m_g_post_mix, m_g_pre_ffn2, m_w_gate2, m_w_up2, m_w_down2, m_g_post_ffn2, v_w_ada, v_b_ada, v_g_pre_ffn1, v_w_gate1, v_w_up1, v_w_down1, v_g_post_ffn1, v_g_pre_mix, v_w_in, v_b_in, v_sinks_a, v_rel_bias_b, v_g_grp_a, v_g_grp_b, v_w_out, v_b_out, v_g_post_mix, v_g_pre_ffn2, v_w_gate2, v_w_up2, v_w_down2, v_g_post_ffn2):
    given = dict(locals())
    weights = {n: given[n] for n in WEIGHTS}
    mom_m = {n: given["m_" + n] for n in WEIGHTS}
    mom_v = {n: given["v_" + n] for n in WEIGHTS}

    me = 4 * lax.axis_index("x") + 2 * lax.axis_index("y") + lax.axis_index("c")
    xs = x[0]
    tgt = loss_target[0]
    d_model = xs.shape[1]
    ada_cols = w_ada.shape[2]

    sh = {"wg1": w_gate1[0].T, "wu1": w_up1[0].T, "wd1": w_down1[0], "win": w_in[0].T, "wo": w_out[0],
          "wg2": w_gate2[0].T, "wu2": w_up2[0].T, "wd2": w_down2[0]}
    sh = {k: v.astype(BF16) for k, v in sh.items()}

    def gather(full=(), new=(), cont=()):
        return _gather_carry([sh[n] for n in full], [sh[n] for n in new], cont)

    bias_a = _alibi_bias()
    rel_m = _rel_index_matrix()
    rel_vec = jnp.dot(rel_bias_b[0], rel_m.T, precision=lax.Precision.HIGHEST)
    bias_b, (wg1, wu1, wd1_part) = _toeplitz_bias(rel_vec.reshape(H_B, 1, SKEW),
                                                  carry=gather(full=("wg1", "wu1"), new=("wd1",)))

    b_cols = lax.dynamic_slice(b_ada, (0, me * ada_cols), (1, ada_cols))
    (sc_all, mod_rows), _ = _ada_forward(c, w_ada[0], b_cols, _Carry([], [], [], lambda *a: None, lambda *a: None))
    mod = mod_rows.reshape(N_MOD, d_model)
    shift1, scale1, gate1, shift2, scale2, gate2, shift3, scale3, gate3 = (mod[i:i + 1] for i in range(N_MOD))

    h1 = _pre_norm(xs, g_pre_ffn1, scale1, shift1, "pre_norm_ffn1")
    (a1, b1, u1), (win_part, wo_part, wd1) = _ffn_up(h1, wg1, wu1, "ffn_up_ffn1",
                                                     carry=gather(new=("win", "wo"), cont=(wd1_part,)))
    (y1, x1, h2), (wg2_part, win, wo) = _mm_nn(
        [(u1, wd1)], "ffn_down_ffn1", F32, carry=gather(new=("wg2",), cont=(win_part, wo_part)),
        tail=_tail_post_pre(xs, g_post_ffn1, gate1, 0.5, g_pre_mix, scale2, shift2))

    proj, (wg2,) = _mm_nt(h2, win, "in_proj", BF16, bias=b_in, carry=gather(cont=(wg2_part,)))
    sinks = sinks_a[0]
    cfg_a = dict(n_back=BACK_A, gqa=True, q_col=0, k_col=QA // LANES, v_col=(QA + KVA) // LANES, TPS=TPS_A)
    cfg_b = dict(n_back=BACK_B, gqa=False, q_col=(QA + 2 * KVA) // LANES, k_col=(QA + 2 * KVA + QB) // LANES,
                 v_col=(QA + 2 * KVA + 2 * QB) // LANES, TPS=TPS_B)
    (oa, lse_a), (wu2_part, wd2_part) = _attention_fwd(proj, bias_a, sinks, name="attn_a",
                                                       carry=gather(new=("wu2", "wd2")), **cfg_a)
    (ob, lse_b), (wu2, wd2) = _attention_fwd(proj, bias_b, None, name="attn_b",
                                             carry=gather(cont=(wu2_part, wd2_part)), **cfg_b)
    ycat = _group_norm_cat(oa, ob, g_grp_a, g_grp_b)
    ymix, x2, h3 = _mm_nn([(ycat, wo)], "out_proj", F32, bias=b_out,
                          tail=_tail_post_pre(x1, g_post_mix, gate2, 1.0, g_pre_ffn2, scale3, shift3))

    a3, b3, u3 = _ffn_up(h3, wg2, wu2, "ffn_up_ffn2")

    flights, own = {}, {}

    def grad_pair(key, a_mat, b_mat, name):
        part, own[key] = _mm_tn_pair(a_mat, b_mat, name)
        return part

    def scatter_start(tag, after_vec, **parts):
        names = list(parts)
        sems, p_thru, lands, token = _scatter_start([parts[n] for n in names], "scatter_start_" + tag)
        flights[tag] = (names, sems, p_thru, lands)
        return after_vec + token[0:1, 0:1]

    dx3, dy, loss_part, s1 = _mm_nn([(u3, wd2)], "ffn_down_ffn2", None,
                                    tail=_tail_post_loss(x2, tgt, g_post_ffn2, gate3, 0.5))
    da, db = _ffn_down_bwd(dy, wd2, a3, b3, "ffn_down_bwd_ffn2")
    dwd2 = grad_pair("wd2", u3, dy, "grad_wd_ffn2")
    dwg2 = grad_pair("wg2", da, h3, "grad_wg_ffn2")
    dwu2 = grad_pair("wu2", db, h3, "grad_wu_ffn2")
    g_pre_tied = scatter_start("ffn2", g_pre_ffn2, wd2=dwd2, wg2=dwg2, wu2=dwu2)
    dx2, dymix, s2, s3, s1m, db_out = _mm_nn(
        [(da, wg2), (db, wu2)], "ffn_up_bwd_ffn2", None,
        tail=_tail_pre_post_bwd(x2, dx3, ymix, g_pre_tied, scale3, g_post_mix, gate2, 1.0))
    sm3 = dict(shift=s3, scale=s2 * g_pre_ffn2, gate=0.5 * g_post_ffn2 * s1,
               g_pre=(1.0 + scale3) * s2, g_post=(0.5 * gate3) * s1)

    dycat = _mm_nt(dymix, wo, "out_proj_bwd", F32)
    dwo = grad_pair("wo", ycat, dymix, "grad_wo")
    doa, dob, dg_a, dg_b = _group_norm_bwd(dycat, oa, ob, g_grp_a, g_grp_b)
    dqa, dka, dva, dsink = _attention_bwd(proj, bias_a, sinks, doa, lse_a, name="attn_a_bwd", **cfg_a)
    dqb, dkb, dvb, dbias = _attention_bwd(proj, bias_b, None, dob, lse_b, name="attn_b_bwd", **cfg_b)
    dproj = jnp.concatenate([dqa, dka, dva, dqb, dkb, dvb], axis=1)
    dwin, own["win"], db_in = _mm_tn_pair(dproj, h2, "grad_win", col_sums=True)
    g_pre_tied = scatter_start("mix", g_pre_mix, wo=dwo, win=dwin)
    dx1, dy, s2m, s3m, s1, _ = _mm_nn(
        [(dproj, win)], "in_proj_bwd", None,
        tail=_tail_pre_post_bwd(x1, dx2, y1, g_pre_tied, scale2, g_post_ffn1, gate1, 0.5))
    d_rel = jnp.dot(_diagonal_sums(dbias).reshape(H_B, SKEW), rel_m, precision=lax.Precision.HIGHEST)
    d_sinks = dsink[:, :2 * TPS_A, 0].reshape(1, H_A)

    da, db = _ffn_down_bwd(dy, wd1, a1, b1, "ffn_down_bwd_ffn1")
    dwd1 = grad_pair("wd1", u1, dy, "grad_wd_ffn1")
    dwg1 = grad_pair("wg1", da, h1, "grad_wg_ffn1")
    dwu1 = grad_pair("wu1", db, h1, "grad_wu_ffn1")
    g_pre_tied = scatter_start("ffn1", g_pre_ffn1, wd1=dwd1, wg1=dwg1, wu1=dwu1)
    dx0, s2, s3 = _mm_nn([(da, wg1), (db, wu1)], "ffn_up_bwd_ffn1", None,
                         tail=_tail_pre_bwd(xs, dx1, g_pre_tied, scale1))
    sm1 = dict(shift=s3, scale=s2 * g_pre_ffn1, gate=0.5 * g_post_ffn1 * s1,
               g_pre=(1.0 + scale1) * s2, g_post=(0.5 * gate1) * s1)

    dmod = jnp.concatenate([sm1["shift"], sm1["scale"], sm1["gate"],
                            s3m, s2m * g_pre_mix, g_post_mix * s1m,
                            sm3["shift"], sm3["scale"], sm3["gate"]], axis=1)
    small_parts = {
        "b_ada": dmod, "g_pre_ffn1": sm1["g_pre"], "g_post_ffn1": sm1["g_post"],
        "g_pre_mix": (1.0 + scale2) * s2m, "b_in": db_in, "sinks_a": d_sinks,
        "rel_bias_b": d_rel.reshape(1, H_B * N_REL), "g_grp_a": dg_a, "g_grp_b": dg_b, "b_out": db_out,
        "g_post_mix": gate2 * s1m, "g_pre_ffn2": sm3["g_pre"], "g_post_ffn2": sm3["g_post"]}
    sizes = [small_parts[n].shape[1] for n in SMALL]

    def pack(parts):
        cells = []
        for p in parts:
            cells.append(p)
            if p.shape[1] % LANES:
                cells.append(jnp.zeros((1, -p.shape[1] % LANES), F32))
        return jnp.concatenate(cells, axis=1)

    packed = pack([small_parts[n] for n in SMALL] + [loss_part])
    n_packed = packed.shape[1]
    small_sems, packed_thru, small_land, small_token = _small_gather_start(packed)

    out_g, out_d, out_m, out_v = {}, {}, {}, {}
    groups = (("ffn2", (("w_gate2", "wg2", True), ("w_up2", "wu2", True), ("w_down2", "wd2", False))),
              ("mix", (("w_in", "win", True), ("w_out", "wo", False))),
              ("ffn1", (("w_gate1", "wg1", True), ("w_up1", "wu1", True), ("w_down1", "wd1", False))))
    after = small_token
    for tag, members in groups:
        names, sems, p_thru, lands = flights[tag]
        _, l_done = _scatter_wait(sems, p_thru, lands, after, "scatter_wait_" + tag)
        slots = dict(zip(names, l_done))
        for n, key, transposed in members:
            view = (lambda t: t.T) if transposed else (lambda t: t)
            res = _adamw_from_slots(view(weights[n][0]), own[key], slots[key], view(mom_m[n][0]),
                                    view(mom_v[n][0]), "adamw_" + n)
            out_g[n], out_d[n], out_m[n], out_v[n] = (view(t)[None] for t in res)
            after = res[3]

    packed_done, small_land = _small_gather_wait(small_sems, packed_thru, small_land, after)
    gathered = lax.dynamic_update_slice(small_land, packed_done[None], (me, 0, 0))
    small_sum = _sum_rows8(gathered)
    loss = small_sum[0, n_packed - LANES]
    dmod_cols = lax.dynamic_slice(gathered.reshape(N_DEV, n_packed), (0, me * ada_cols), (N_DEV, ada_cols))
    g_ada = _ada_weight_grad(sc_all.reshape(N_DEV, d_model).T, dmod_cols)
    d_, m_, v_ = _adamw(w_ada[0], g_ada, m_w_ada[0], v_w_ada[0], "adamw_w_ada")
    out_g["w_ada"], out_d["w_ada"], out_m["w_ada"], out_v["w_ada"] = g_ada[None], d_[None], m_[None], v_[None]

    small_out = _adamw_small(small_sum, *(pack([tree[n].reshape(1, -1) for n in SMALL])
                                          for tree in (weights, mom_m, mom_v)), sizes)
    for j, n in enumerate(SMALL):
        shape = weights[n].shape
        out_g[n], out_d[n], out_m[n], out_v[n] = (t.reshape(shape) for t in small_out[4 * j:4 * j + 4])

    return (loss, dx0[None], *[out_g[n] for n in WEIGHTS], *[out_d[n] for n in WEIGHTS],
            *[out_m[n] for n in WEIGHTS], *[out_v[n] for n in WEIGHTS])
```

```python
import numpy as np
import jax
import jax.numpy as jnp
from jax import lax
from jax.experimental import pallas as pl
from jax.experimental.pallas import tpu as pltpu

F32 = jnp.float32
BF16 = jnp.bfloat16
MESH = pl.DeviceIdType.MESH
ANY = pl.BlockSpec(memory_space=pl.ANY)
VMEM_SPEC = pl.BlockSpec(memory_space=pltpu.VMEM)
SMEM_SPEC = pl.BlockSpec(memory_space=pltpu.SMEM)

N_DEV = 8
CHUNK = 64
HEAD_DIM = 64
LANES = 128
H_A, KV_A, H_B = 8, 2, 8
BACK_A, BACK_B = 2, 8
REL_CLIP = 128
N_REL = 2 * REL_CLIP + 1
QA, KVA, QB = H_A * HEAD_DIM, KV_A * HEAD_DIM, H_B * HEAD_DIM
D_IN = QA + 2 * KVA + 3 * QB
N_MOD = 9
EPS = 1e-6
NEG_INF = -1e30
QG = 4
QROWS = QG * CHUNK
TPS_A, TPS_B = 4, 2
SKEW = 1024
ADAM_LR, ADAM_B1, ADAM_B2, ADAM_EPS, ADAM_WD, ADAM_STEP = 0.001, 0.9, 0.999, 1e-08, 0.01, 10
VMEM_LIMIT = 56 * 2 ** 20


def _pick(n, cands):
    for c in cands:
        if n % c == 0:
            return c
    return n


def _pieces(n, width=2 * LANES):
    return [(lo, min(lo + width, n)) for lo in range(0, n, width)]


def _params(sem=None):
    return pltpu.CompilerParams(dimension_semantics=sem, vmem_limit_bytes=VMEM_LIMIT)


def _dot_nt(a, b):
    return lax.dot_general(a, b, (((1,), (1,)), ((), ())), preferred_element_type=F32)


def _dot_tn(a, b):
    return lax.dot_general(a, b, (((0,), (0,)), ((), ())), preferred_element_type=F32)


def _dot(a, b):
    return jnp.dot(a, b, preferred_element_type=F32)


def _sigmoid(a):
    return 0.5 * (jnp.tanh(0.5 * a) + 1.0)


def _mesh_pos():
    return lax.axis_index("x"), lax.axis_index("y"), lax.axis_index("c")


def _peer(x, y, c, r):
    px = 1 - x if r & 4 else x
    py = 1 - y if r & 2 else y
    pc = 1 - c if r & 1 else c
    return px, py, pc


class _Carry:
    def __init__(self, ins, out_shapes, scratch, start, finish, aliases=()):
        self.ins, self.out_shapes, self.scratch = list(ins), list(out_shapes), list(scratch)
        self.start, self.finish, self.aliases = start, finish, list(aliases)


def _call(body, *, name, grid, in_specs, out_specs, out_shape, args, scratch=(), sem=None, carry=None):
    single = not isinstance(out_shape, (tuple, list))
    out_specs = (out_specs,) if single else tuple(out_specs)
    out_shape = (out_shape,) if single else tuple(out_shape)
    if carry is None:
        res = pl.pallas_call(body, name=name, grid=grid, in_specs=list(in_specs), out_specs=out_specs,
                             out_shape=out_shape, scratch_shapes=list(scratch), compiler_params=_params(sem))(*args)
        return res[0] if single else res
    n_in, n_out, n_s = len(in_specs), len(out_shape), len(scratch)
    ci, co = len(carry.ins), len(carry.out_shapes)

    def wrapped(*refs):
        ins, cins = refs[:n_in], refs[n_in:n_in + ci]
        outs = refs[n_in + ci:n_in + ci + n_out]
        couts = refs[n_in + ci + n_out:n_in + ci + n_out + co]
        scr = refs[n_in + ci + n_out + co:n_in + ci + n_out + co + n_s]
        cscr = refs[n_in + ci + n_out + co + n_s:]
        first, last = None, None
        for ax, n in enumerate(grid):
            f, l = pl.program_id(ax) == 0, pl.program_id(ax) == n - 1
            first = f if first is None else first & f
            last = l if last is None else last & l
        pl.when(first)(lambda: carry.start(cins, couts, cscr))
        body(*ins, *outs, *scr)
        pl.when(last)(lambda: carry.finish(cins, couts, cscr))

    res = pl.pallas_call(
        wrapped, name=name, grid=grid, in_specs=list(in_specs) + [ANY] * ci, out_specs=out_specs + (ANY,) * co,
        out_shape=out_shape + tuple(carry.out_shapes), scratch_shapes=list(scratch) + carry.scratch,
        input_output_aliases={n_in + i: n_out + o for i, o in carry.aliases},
        compiler_params=_params(("arbitrary",) * len(grid)))(*args, *carry.ins)
    main = res[:n_out]
    return (main[0] if single else main), res[n_out:]


PASS_PIECES = 4
ROW_TILE = 16


def _row_pieces(rows):
    tiles, pieces, off = rows // ROW_TILE, [], 0
    for i in range(PASS_PIECES):
        n = (tiles + i) // PASS_PIECES * ROW_TILE
        if n:
            pieces.append((off, n))
        off += n
    assert off == rows
    return pieces


def _gather_carry(full=(), new=(), cont=()):
    full, new, cont = list(full), list(new), list(cont)
    n_full, n_one = len(full), len(full) + len(new)
    n_w = n_one + len(cont)
    rows = [s.shape[0] for s in full + new] + [b.shape[0] // N_DEV for b in cont]
    shapes = [jax.ShapeDtypeStruct((N_DEV * s.shape[0], s.shape[1]), s.dtype) for s in full + new]
    shapes += [jax.ShapeDtypeStruct(b.shape, b.dtype) for b in cont]
    halves = [((0, r // 2), (r // 2, r // 2)) for r in rows]
    pieces = [_row_pieces(r) for r in rows]

    def plan(ins, outs, scr):
        send_sems, recv_sems, local_sems = scr
        x, y, c = _mesh_pos()
        me, sibling = (x, y, c), (x, y, 1 - c)
        x_chip, y_chip, far_chip = (1 - x, y), (x, 1 - y), (1 - x, 1 - y)

        def block(buf, w, chip, core, span=None):
            off, n = (0, rows[w]) if span is None else span
            start = (4 * chip[0] + 2 * chip[1] + core) * rows[w] + off
            return buf.at[pl.ds(pl.multiple_of(start, 16), n), :]

        def copy(w, k, chip, core, to, span=None, held=None, shard=None, p=0):
            if shard is None:
                src = block(outs[w] if held is None else held, w, chip, core, span)
            else:
                src = shard if span is None else shard.at[pl.ds(span[0], span[1]), :]
            return pltpu.make_async_remote_copy(
                src_ref=src, dst_ref=block(outs[w], w, chip, core, span), send_sem=send_sems.at[w, k, p],
                recv_sem=recv_sems.at[w, k, p], device_id=to, device_id_type=MESH)

        def spans(w, k):
            return pieces[w] if k == 7 else [None]

        def to_sibling(w, k, chip, held=None, shard=None):
            return [copy(w, k, chip, c, sibling, span, held, shard, p) for p, span in enumerate(spans(w, k))]

        def from_sibling(w, k, chip):
            return [copy(w, k, chip, 1 - c, me, span, p=p) for p, span in enumerate(spans(w, k))]

        def stage_one(w):
            return [copy(w, 1, (x, y), c, (*x_chip, c), shard=ins[w]), copy(w, 2, (x, y), c, (*y_chip, c), shard=ins[w]),
                    *to_sibling(w, 0, (x, y), shard=ins[w])]

        def stage_two(w, held):
            return [copy(w, 5, x_chip, c, (*y_chip, c), halves[w][0], held),
                    copy(w, 6, y_chip, c, (*x_chip, c), halves[w][1], held),
                    *to_sibling(w, 3, x_chip, held), *to_sibling(w, 4, y_chip, held)]

        mine = [pltpu.make_async_copy(ins[w], block(outs[w], w, (x, y), c), local_sems.at[w]) for w in range(n_one)]
        return c, me, sibling, far_chip, copy, to_sibling, from_sibling, stage_one, stage_two, mine

    def start(ins, outs, scr):
        _, _, _, _, _, _, _, stage_one, stage_two, mine = plan(ins, outs, scr)
        for w in range(n_one):
            for cp in stage_one(w):
                cp.start()
        for w in range(n_one, n_w):
            for cp in stage_two(w, ins[w]):
                cp.start()
        for cp in mine:
            cp.start()

    def finish(ins, outs, scr):
        c, me, sibling, far_chip, copy, to_sibling, from_sibling, stage_one, stage_two, mine = plan(ins, outs, scr)
        x_chip, y_chip = (far_chip[0], me[1]), (me[0], far_chip[1])
        sent = []

        def land_one(w):
            copy(w, 1, x_chip, c, me).wait_recv()
            copy(w, 2, y_chip, c, me).wait_recv()

        def land_two(w):
            copy(w, 5, far_chip, c, me, halves[w][0]).wait_recv()
            copy(w, 6, far_chip, c, me, halves[w][1]).wait_recv()
            for cp in to_sibling(w, 7, far_chip):
                cp.start()
                sent.append(cp)

        for w in range(n_full):
            land_one(w)
            for cp in stage_two(w, outs[w]):
                cp.start()
                sent.append(cp)
        for w in range(n_one, n_w):
            land_two(w)
            sent.extend(stage_two(w, ins[w]))
        for w in range(n_full, n_one):
            land_one(w)
        for w in range(n_full):
            land_two(w)
        for w in range(n_one):
            for cp in from_sibling(w, 0, me[:2]):
                cp.wait_recv()
            sent.extend(stage_one(w))
        for w in list(range(n_full)) + list(range(n_one, n_w)):
            for k, chip in ((3, x_chip), (4, y_chip), (7, far_chip)):
                for cp in from_sibling(w, k, chip):
                    cp.wait_recv()
        for cp in sent:
            cp.wait_send()
        for cp in mine:
            cp.wait()

    return _Carry(
        full + new + cont, shapes,
        [pltpu.SemaphoreType.DMA((n_w, N_DEV, PASS_PIECES)), pltpu.SemaphoreType.DMA((n_w, N_DEV, PASS_PIECES)),
         pltpu.SemaphoreType.DMA((max(n_one, 1),))], start, finish,
        aliases=[(w, w) for w in range(n_one, n_w)])


HBM_SPEC = pl.BlockSpec(memory_space=pltpu.HBM)
SEM_SPEC = pl.BlockSpec(memory_space=pltpu.SEMAPHORE)
N_CHIP = N_DEV // 2


def _scatter_copy(part_ref, land_ref, send_sem, recv_sem, r, rows):
    x, y, c = _mesh_pos()
    px, py, _ = _peer(x, y, c, 2 * r)
    src = part_ref.at[pl.ds(pl.multiple_of((2 * px + py) * rows, 16), rows), :]
    return pltpu.make_async_remote_copy(
        src_ref=src, dst_ref=land_ref.at[r - 1], send_sem=send_sem, recv_sem=recv_sem,
        device_id=(px, py, c), device_id_type=MESH)


def _scatter_order(n_w):
    return [(w, r) for r in (3, 2, 1) for w in range(n_w)]


def _scatter_start(parts, name):
    n_w = len(parts)
    rows = [p.shape[0] // N_CHIP for p in parts]
    order = _scatter_order(n_w)
    lands = [pltpu.with_memory_space_constraint(lax.empty((N_CHIP - 1, r, p.shape[1]), p.dtype), pltpu.HBM)
             for r, p in zip(rows, parts)]

    def body(*refs):
        part_refs, land_refs = refs[:n_w], refs[n_w:2 * n_w]
        sems = refs[2 * n_w:2 * n_w + 2 * len(order)]
        token = refs[-1]
        for j, (w, r) in enumerate(order):
            _scatter_copy(part_refs[w], land_refs[w], sems[2 * j], sems[2 * j + 1], r, rows[w]).start()
        token[...] = jnp.zeros_like(token)

    n_sem = 2 * len(order)
    res = pl.pallas_call(
        body, name=name,
        out_shape=(*[pltpu.SemaphoreType.DMA(())] * n_sem, *[pltpu.HBM(p.shape, p.dtype) for p in parts],
                   *[pltpu.HBM(l.shape, l.dtype) for l in lands], jax.ShapeDtypeStruct((8, LANES), F32)),
        in_specs=[HBM_SPEC] * (2 * n_w), out_specs=(*[SEM_SPEC] * n_sem, *[HBM_SPEC] * (2 * n_w), VMEM_SPEC),
        input_output_aliases={i: n_sem + i for i in range(2 * n_w)},
        compiler_params=pltpu.CompilerParams(has_side_effects=pltpu.SideEffectType.DATAFLOW_SIDE_EFFECTING),
    )(*[pltpu.with_memory_space_constraint(p, pltpu.HBM) for p in parts], *lands)
    return (list(res[:n_sem]), list(res[n_sem:n_sem + n_w]), list(res[n_sem + n_w:n_sem + 2 * n_w]), res[-1])


def _scatter_wait(sems, parts, lands, after, name):
    n_w = len(parts)
    rows = [p.shape[0] // N_CHIP for p in parts]
    order = _scatter_order(n_w)

    def body(*refs):
        part_refs, land_refs = refs[:n_w], refs[n_w:2 * n_w]
        sem_refs = refs[2 * n_w:2 * n_w + 2 * len(order)]
        for j, (w, r) in enumerate(order):
            cp = _scatter_copy(part_refs[w], land_refs[w], sem_refs[2 * j], sem_refs[2 * j + 1], r, rows[w])
            cp.wait_send()
            cp.wait_recv()

    res = pl.pallas_call(
        body, name=name,
        out_shape=(*[pltpu.HBM(p.shape, p.dtype) for p in parts], *[pltpu.HBM(l.shape, l.dtype) for l in lands]),
        in_specs=[HBM_SPEC] * (2 * n_w) + [SEM_SPEC] * len(sems) + [ANY],
        out_specs=tuple([HBM_SPEC] * (2 * n_w)),
        input_output_aliases={i: i for i in range(2 * n_w)},
        compiler_params=pltpu.CompilerParams(has_side_effects=pltpu.SideEffectType.DATAFLOW_SIDE_EFFECTING),
    )(*parts, *lands, *sems, after)
    return list(res[:n_w]), list(res[n_w:])


def _small_copy(v_ref, land_ref, send_sem, recv_sem, r):
    x, y, c = _mesh_pos()
    px, py, pc = _peer(x, y, c, r)
    return pltpu.make_async_remote_copy(
        src_ref=v_ref, dst_ref=land_ref.at[4 * x + 2 * y + c], send_sem=send_sem, recv_sem=recv_sem,
        device_id=(px, py, pc), device_id_type=MESH)


def _small_gather_start(v):
    land = pltpu.with_memory_space_constraint(lax.empty((N_DEV,) + v.shape, v.dtype), pltpu.HBM)

    def body(v_ref, land_ref, *rest):
        sems, token = rest[:2 * (N_DEV - 1)], rest[-1]
        for r in range(1, N_DEV):
            _small_copy(v_ref, land_ref, sems[2 * r - 2], sems[2 * r - 1], r).start()
        token[...] = jnp.zeros_like(token)

    n_sem = 2 * (N_DEV - 1)
    res = pl.pallas_call(
        body, name="small_gather_start",
        out_shape=(*[pltpu.SemaphoreType.DMA(())] * n_sem, pltpu.HBM(v.shape, v.dtype),
                   pltpu.HBM(land.shape, land.dtype), jax.ShapeDtypeStruct((8, LANES), F32)),
        in_specs=[HBM_SPEC, HBM_SPEC], out_specs=(*[SEM_SPEC] * n_sem, HBM_SPEC, HBM_SPEC, VMEM_SPEC),
        input_output_aliases={0: n_sem, 1: n_sem + 1},
        compiler_params=pltpu.CompilerParams(has_side_effects=pltpu.SideEffectType.DATAFLOW_SIDE_EFFECTING),
    )(pltpu.with_memory_space_constraint(v, pltpu.HBM), land)
    return list(res[:n_sem]), res[n_sem], res[n_sem + 1], res[-1]


def _small_gather_wait(sems, v, land, after):
    def body(v_ref, land_ref, *rest):
        for r in range(1, N_DEV):
            cp = _small_copy(v_ref, land_ref, rest[2 * r - 2], rest[2 * r - 1], r)
            cp.wait_send()
            x, y, c = _mesh_pos()
            px, py, pc = _peer(x, y, c, r)
            pltpu.make_async_remote_copy(
                src_ref=v_ref, dst_ref=land_ref.at[4 * px + 2 * py + pc], send_sem=rest[2 * r - 2],
                recv_sem=rest[2 * r - 1], device_id=(px, py, pc), device_id_type=MESH).wait_recv()

    res = pl.pallas_call(
        body, name="small_gather_wait",
        out_shape=(pltpu.HBM(v.shape, v.dtype), pltpu.HBM(land.shape, land.dtype)),
        in_specs=[HBM_SPEC, HBM_SPEC] + [SEM_SPEC] * len(sems) + [ANY], out_specs=(HBM_SPEC, HBM_SPEC),
        input_output_aliases={0: 0, 1: 1},
        compiler_params=pltpu.CompilerParams(has_side_effects=pltpu.SideEffectType.DATAFLOW_SIDE_EFFECTING),
    )(v, land, *sems, after)
    return res[0], res[1]


def _ada_forward(c_row, w_ada, b_cols, carry):
    d = c_row.shape[1]
    wcols = w_ada.shape[1]
    ci, co = len(carry.ins), len(carry.out_shapes)

    def body(*refs):
        c_ref, w_ref, b_ref = refs[:3]
        cins = refs[3:3 + ci]
        sc_ref, mod_ref = refs[3 + ci:5 + ci]
        couts = refs[5 + ci:5 + ci + co]
        rows_ref, send_sems, recv_sems = refs[5 + ci + co:8 + ci + co]
        cscr = refs[8 + ci + co:]
        carry.start(cins, couts, cscr)
        x, y, c = _mesh_pos()
        me = 4 * x + 2 * y + c
        cv = c_ref[...]
        sc_ref[me] = cv * _sigmoid(cv)

        sends = []
        for r in range(1, N_DEV):
            px, py, pc = _peer(x, y, c, r)
            cp = pltpu.make_async_remote_copy(
                src_ref=sc_ref.at[me], dst_ref=sc_ref.at[me], send_sem=send_sems.at[0, r - 1],
                recv_sem=recv_sems.at[0, r - 1], device_id=(px, py, pc), device_id_type=MESH)
            cp.start()
            sends.append(cp)
        for r in range(1, N_DEV):
            px, py, pc = _peer(x, y, c, r)
            pid = 4 * px + 2 * py + pc
            pltpu.make_async_remote_copy(
                src_ref=sc_ref.at[pid], dst_ref=sc_ref.at[pid], send_sem=send_sems.at[0, r - 1],
                recv_sem=recv_sems.at[0, r - 1], device_id=(px, py, pc), device_id_type=MESH).wait_recv()
        for cp in sends:
            cp.wait_send()

        sc_all = jnp.concatenate([sc_ref[j] for j in range(N_DEV)], axis=0)
        rows = _dot(sc_all.astype(BF16), w_ref[...].astype(BF16)) + b_ref[...]
        for j in range(N_DEV):
            rows_ref[j] = rows[j:j + 1, :]
        mod_ref[me] = rows_ref[me]

        sends = []
        for r in range(1, N_DEV):
            px, py, pc = _peer(x, y, c, r)
            pid = 4 * px + 2 * py + pc
            cp = pltpu.make_async_remote_copy(
                src_ref=rows_ref.at[pid], dst_ref=mod_ref.at[me], send_sem=send_sems.at[1, r - 1],
                recv_sem=recv_sems.at[1, r - 1], device_id=(px, py, pc), device_id_type=MESH)
            cp.start()
            sends.append(cp)
        for r in range(1, N_DEV):
            px, py, pc = _peer(x, y, c, r)
            pid = 4 * px + 2 * py + pc
            pltpu.make_async_remote_copy(
                src_ref=rows_ref.at[pid], dst_ref=mod_ref.at[pid], send_sem=send_sems.at[1, r - 1],
                recv_sem=recv_sems.at[1, r - 1], device_id=(px, py, pc), device_id_type=MESH).wait_recv()
        for cp in sends:
            cp.wait_send()
        carry.finish(cins, couts, cscr)

    res = pl.pallas_call(
        body, name="ada_forward",
        out_shape=(jax.ShapeDtypeStruct((N_DEV, 1, d), F32), jax.ShapeDtypeStruct((N_DEV, 1, wcols), F32),
                   *carry.out_shapes),
        in_specs=[VMEM_SPEC, VMEM_SPEC, VMEM_SPEC] + [ANY] * ci, out_specs=(VMEM_SPEC, VMEM_SPEC) + (ANY,) * co,
        scratch_shapes=[pltpu.VMEM((N_DEV, 1, wcols), F32), pltpu.SemaphoreType.DMA((2, N_DEV - 1)),
                        pltpu.SemaphoreType.DMA((2, N_DEV - 1))] + carry.scratch,
        compiler_params=_params(),
    )(c_row, w_ada, b_cols, *carry.ins)
    return res[:2], res[2:]


def _mm_nt(a, b, name, out_dtype, bias=None, carry=None):
    m, k = a.shape
    n = b.shape[0]
    tm = _pick(m, (512, 256, 128))
    tn = _pick(n, (1408, 1152, 1024, 768, 512, 256, 128))

    def body(*refs):
        acc = _dot_nt(refs[0][...], refs[1][...])
        if bias is not None:
            acc = acc + refs[2][...]
        refs[-1][...] = acc.astype(out_dtype)

    in_specs = [pl.BlockSpec((tm, k), lambda j, i: (i, 0)), pl.BlockSpec((tn, k), lambda j, i: (j, 0))]
    args = [a, b]
    if bias is not None:
        in_specs.append(pl.BlockSpec((1, tn), lambda j, i: (0, j)))
        args.append(bias)
    return _call(body, name=name, grid=(n // tn, m // tm), in_specs=in_specs,
                 out_specs=pl.BlockSpec((tm, tn), lambda j, i: (i, j)),
                 out_shape=jax.ShapeDtypeStruct((m, n), out_dtype), args=args,
                 sem=("parallel", "parallel"), carry=carry)


class _Tail:
    def __init__(self, rows, vecs, outs, fn):
        self.rows, self.vecs, self.outs, self.fn = list(rows), list(vecs), list(outs), fn


def _mm_nn(pairs, name, out_dtype, bias=None, carry=None, tail=None):
    m, k = pairs[0][0].shape
    n = pairs[0][1].shape[1]
    n_p = len(pairs)
    tm = _pick(m, (512, 256, 128))
    tk = k if n_p == 1 else _pick(k, (1408, 1152, 1024, 768, 512, 256, 128))
    nk = k // tk
    n_b = 0 if bias is None else 1
    n_r, n_v = (len(tail.rows), len(tail.vecs)) if tail else (0, 0)
    n_in = 2 * n_p + n_b + n_r + n_v
    n_main = 0 if out_dtype is None else 1

    def finish(acc, refs, first_tile):
        if bias is not None:
            acc = acc + refs[2 * n_p][...]
        outs = refs[n_in:-1]
        if n_main:
            outs[0][...] = acc.astype(out_dtype)
        if tail is None:
            return
        rows = [r[...] for r in refs[2 * n_p + n_b:2 * n_p + n_b + n_r]]
        vecs = [v[...] for v in refs[2 * n_p + n_b + n_r:n_in]]
        vals = tail.fn(acc, rows, vecs)
        for ref, val, (dtype, kind) in zip(outs[n_main:], vals, tail.outs):
            if kind == "row":
                ref[...] = val.astype(dtype)
            else:
                @pl.when(first_tile)
                def _(ref=ref):
                    ref[...] = jnp.zeros_like(ref)

                ref[...] += val

    def body(*refs):
        acc_ref = refs[-1]
        kk, i = pl.program_id(0), pl.program_id(1)
        part = _dot(refs[0][...], refs[1][...])
        for p in range(1, n_p):
            part = part + _dot(refs[2 * p][...], refs[2 * p + 1][...])
        if nk == 1:
            finish(part, refs, i == 0)
            return
        rows = pl.ds(pl.multiple_of(i * tm, tm), tm)

        @pl.when(kk == 0)
        def _():
            acc_ref[rows, :] = part

        if nk > 2:
            @pl.when((kk > 0) & (kk < nk - 1))
            def _():
                acc_ref[rows, :] += part

        @pl.when(kk == nk - 1)
        def _():
            finish(acc_ref[rows, :] + part, refs, i == 0)

    def last_only(kk, i):
        return (jnp.where(kk == nk - 1, i, 0), 0)

    row_spec = pl.BlockSpec((tm, n), last_only)
    vec_spec = pl.BlockSpec((1, n), lambda kk, i: (0, 0))
    in_specs, args = [], []
    for a, b in pairs:
        in_specs += [pl.BlockSpec((tm, tk), lambda kk, i: (i, kk)), pl.BlockSpec((tk, n), lambda kk, i: (kk, 0))]
        args += [a, b]
    if bias is not None:
        in_specs.append(vec_spec)
        args.append(bias)
    out_specs = [row_spec] * n_main
    out_shape = [jax.ShapeDtypeStruct((m, n), out_dtype)] if n_main else []
    if tail:
        in_specs += [row_spec] * n_r + [vec_spec] * n_v
        args += tail.rows + tail.vecs
        for dtype, kind in tail.outs:
            if kind == "row":
                out_specs.append(row_spec)
                out_shape.append(jax.ShapeDtypeStruct((m, n), dtype))
            else:
                width = n if kind == "sum" else 1
                out_specs.append(pl.BlockSpec((1, width), lambda kk, i: (0, 0)))
                out_shape.append(jax.ShapeDtypeStruct((1, width), dtype))
    if tail is None:
        out_specs, out_shape = out_specs[0], out_shape[0]
    return _call(body, name=name, grid=(nk, m // tm), in_specs=in_specs, out_specs=out_specs,
                 out_shape=out_shape, args=args,
                 scratch=[pltpu.VMEM((m, n) if nk > 1 else (8, LANES), F32)],
                 sem=("arbitrary", "arbitrary"), carry=carry)


def _rms(v):
    return lax.rsqrt(jnp.mean(v * v, axis=-1, keepdims=True) + EPS)


def _col(v):
    return jnp.sum(v, axis=0, keepdims=True)


def _tail_post_pre(x, g_post, gate, weight, g_pre, scale, shift):
    def fn(y, rows, vecs):
        (xv,), (gp, gt, g, sc, sh) = rows, vecs
        xo = xv + (weight * gt) * ((y * _rms(y)) * gp)
        return xo, ((xo * _rms(xo)) * g) * (1.0 + sc) + sh

    return _Tail([x], [g_post, gate, g_pre, scale, shift], [(F32, "row"), (BF16, "row")], fn)


def _tail_post_loss(x, target, g, gate, weight):
    def fn(y, rows, vecs):
        (xv, tv), (gv, gt) = rows, vecs
        r = _rms(y)
        yn = y * r
        err = (xv + (weight * gt) * (yn * gv)) - tv
        do = err * (1.0 / y.shape[1])
        dyn = do * ((weight * gt) * gv)
        dy = r * (dyn - yn * jnp.mean(dyn * yn, axis=-1, keepdims=True))
        return do, dy, 0.5 * _col(jnp.mean(err * err, axis=-1, keepdims=True)), _col(do * yn)

    return _Tail([x, target], [g, gate], [(F32, "row"), (BF16, "row"), (F32, "one"), (F32, "sum")], fn)


def _tail_pre_bwd(x, dres, g_pre, scale):
    def fn(dh, rows, vecs):
        (xv, dr), (g, sc) = rows, vecs
        r = _rms(xv)
        n = xv * r
        dn = dh * (g * (1.0 + sc))
        return dr + r * (dn - n * jnp.mean(dn * n, axis=-1, keepdims=True)), _col(dh * n), _col(dh)

    return _Tail([x, dres], [g_pre, scale], [(F32, "row"), (F32, "sum"), (F32, "sum")], fn)


def _tail_pre_post_bwd(x, dres, y, g_pre, scale, g_post, gate, weight):
    def fn(dh, rows, vecs):
        (xv, dr, yv), (g, sc, gp, gt) = rows, vecs
        r = _rms(xv)
        n = xv * r
        dn = dh * (g * (1.0 + sc))
        dx = dr + r * (dn - n * jnp.mean(dn * n, axis=-1, keepdims=True))
        ry = _rms(yv)
        yn = yv * ry
        dyn = dx * ((weight * gt) * gp)
        dy = ry * (dyn - yn * jnp.mean(dyn * yn, axis=-1, keepdims=True))
        return dx, dy, _col(dh * n), _col(dh), _col(dx * yn), _col(dy)

    return _Tail([x, dres, y], [g_pre, scale, g_post, gate],
                 [(F32, "row"), (BF16, "row")] + [(F32, "sum")] * 4, fn)


def _mm_tn_pair(a, b, name, col_sums=False):
    k, m = a.shape
    n = b.shape[1]
    rows = m // N_DEV
    n_chip = N_DEV // 2
    tm = 4 * rows
    tk = _pick(k, (1024, 512, 256, 128))
    nk = k // tk

    def body(a_ref, b_ref, p_ref, own_ref, *rest):
        acc_ref, keep_ref, send_ref, land_ref, send_sems, recv_sems = rest[-6:]
        kk, i = pl.program_id(0), pl.program_id(1)
        x, y, c = _mesh_pos()
        if col_sums:
            cs_ref = rest[0]
            part = jnp.sum(a_ref[...].astype(F32), axis=0, keepdims=True)

            @pl.when(kk == 0)
            def _():
                cs_ref[i] = part

            @pl.when(kk > 0)
            def _():
                cs_ref[i] += part

        def push(chip):
            return [pltpu.make_async_remote_copy(
                src_ref=send_ref.at[chip, pl.ds(off, nr)], dst_ref=land_ref.at[chip, pl.ds(off, nr)],
                send_sem=send_sems.at[chip, p], recv_sem=recv_sems.at[chip, p], device_id=(x, y, 1 - c),
                device_id_type=MESH) for p, (off, nr) in enumerate(_row_pieces(rows))]

        if nk == 1:
            acc = _dot_tn(a_ref[...], b_ref[...])
        else:
            @pl.when(kk == 0)
            def _():
                acc_ref[i] = jnp.zeros((tm, n), F32)

            acc_ref[i] += _dot_tn(a_ref[...], b_ref[...])

        for t in range(2):
            @pl.when((kk == nk - 1) & (i == t))
            def _(t=t):
                for ob in range(4):
                    chip, core = 2 * t + ob // 2, ob % 2
                    blk = (acc if nk == 1 else acc_ref.at[t])[ob * rows:(ob + 1) * rows, :]

                    @pl.when(c == core)
                    def _(chip=chip, blk=blk):
                        keep_ref[chip] = blk

                    @pl.when(c != core)
                    def _(chip=chip, blk=blk):
                        send_ref[chip] = blk.astype(BF16)
                        for cp in push(chip):
                            cp.start()

        @pl.when((kk == nk - 1) & (i == 1))
        def _():
            for chip in range(n_chip):
                for cp in push(chip):
                    cp.wait_recv()
                val =(keep_ref[chip] + land_ref[chip].astype(F32)).astype(BF16)
                p_ref[chip * rows:(chip + 1) * rows, :] = val

                @pl.when(2 * x + y == chip)
                def _(val=val):
                    own_ref[...] = val

            for chip in range(n_chip):
                for cp in push(chip):
                    cp.wait_send()

    out_specs = [pl.BlockSpec((n_chip * rows, n), lambda kk, i: (0, 0)), pl.BlockSpec((rows, n), lambda kk, i: (0, 0))]
    out_shape = [jax.ShapeDtypeStruct((n_chip * rows, n), BF16), jax.ShapeDtypeStruct((rows, n), BF16)]
    if col_sums:
        out_specs.append(pl.BlockSpec((2, 1, tm), lambda kk, i: (0, 0, 0)))
        out_shape.append(jax.ShapeDtypeStruct((2, 1, tm), F32))
    res = _call(body, name=name, grid=(nk, 2),
                in_specs=[pl.BlockSpec((tk, tm), lambda kk, i: (kk, i)), pl.BlockSpec((tk, n), lambda kk, i: (kk, 0))],
                out_specs=out_specs, out_shape=out_shape, args=[a, b],
                scratch=[pltpu.VMEM((2, tm, n) if nk > 1 else (8, LANES), F32), pltpu.VMEM((n_chip, rows, n), F32),
                         pltpu.VMEM((n_chip, rows, n), BF16), pltpu.VMEM((n_chip, rows, n), BF16),
                         pltpu.SemaphoreType.DMA((n_chip, PASS_PIECES)), pltpu.SemaphoreType.DMA((n_chip, PASS_PIECES))],
                sem=("arbitrary", "arbitrary"))
    return (*res[:2], res[2].reshape(1, m)) if col_sums else res


def _ffn_up(h, wg_t, wu_t, name, carry=None):
    s, d = h.shape
    f = wg_t.shape[0]
    tm = _pick(s, (512, 256, 128))
    tf = _pick(f, (1408, 1024, 512, 256, 128))

    def body(h_ref, wg_ref, wu_ref, a_ref, b_ref, u_ref):
        hh = h_ref[...]
        for lo, hi in _pieces(tf):
            a = _dot_nt(hh, wg_ref[lo:hi, :])
            b = _dot_nt(hh, wu_ref[lo:hi, :])
            a_ref[:, lo:hi] = a.astype(BF16)
            b_ref[:, lo:hi] = b.astype(BF16)
            u_ref[:, lo:hi] = ((a * _sigmoid(a)) * b).astype(BF16)

    w_spec = pl.BlockSpec((tf, d), lambda j, i: (j, 0))
    o_spec = pl.BlockSpec((tm, tf), lambda j, i: (i, j))
    o_shape = jax.ShapeDtypeStruct((s, f), BF16)
    return _call(body, name=name, grid=(f // tf, s // tm),
                 in_specs=[pl.BlockSpec((tm, d), lambda j, i: (i, 0)), w_spec, w_spec],
                 out_specs=(o_spec, o_spec, o_spec), out_shape=(o_shape, o_shape, o_shape),
                 args=[h, wg_t, wu_t], sem=("parallel", "parallel"), carry=carry)


def _ffn_down_bwd(dy, wd, a, b, name, carry=None):
    s, d = dy.shape
    f = wd.shape[0]
    tm = _pick(s, (512, 256, 128))
    tf = _pick(f, (1408, 1024, 512, 256, 128))

    def body(dy_ref, wd_ref, a_ref, b_ref, da_ref, db_ref):
        dyv = dy_ref[...]
        for lo, hi in _pieces(tf):
            du = _dot_nt(dyv, wd_ref[lo:hi, :])
            a = a_ref[:, lo:hi].astype(F32)
            b = b_ref[:, lo:hi].astype(F32)
            sig = _sigmoid(a)
            da_ref[:, lo:hi] = (du * b * (sig * (1.0 + a * (1.0 - sig)))).astype(BF16)
            db_ref[:, lo:hi] = (du * (a * sig)).astype(BF16)

    t_spec = pl.BlockSpec((tm, tf), lambda j, i: (i, j))
    o_shape = jax.ShapeDtypeStruct((s, f), BF16)
    return _call(body, name=name, grid=(f // tf, s // tm),
                 in_specs=[pl.BlockSpec((tm, d), lambda j, i: (i, 0)), pl.BlockSpec((tf, d), lambda j, i: (j, 0)),
                           t_spec, t_spec],
                 out_specs=(t_spec, t_spec), out_shape=(o_shape, o_shape), args=[dy, wd, a, b],
                 sem=("parallel", "parallel"), carry=carry)


def _row_tile(s):
    return _pick(s, (256, 128, 64))


def _vec_spec(d):
    return pl.BlockSpec((1, d), lambda i: (0, 0))


def _pre_norm(x, g, scale, shift, name):
    s, d = x.shape
    ts = _row_tile(s)

    def body(x_ref, g_ref, sc_ref, sh_ref, h_ref):
        xv = x_ref[...]
        r = lax.rsqrt(jnp.mean(xv * xv, axis=-1, keepdims=True) + EPS)
        h_ref[...] = (((xv * r) * g_ref[...]) * (1.0 + sc_ref[...]) + sh_ref[...]).astype(BF16)

    row = pl.BlockSpec((ts, d), lambda i: (i, 0))
    return _call(body, name=name, grid=(s // ts,), in_specs=[row, _vec_spec(d), _vec_spec(d), _vec_spec(d)],
                 out_specs=row, out_shape=jax.ShapeDtypeStruct((s, d), BF16), args=[x, g, scale, shift],
                 sem=("parallel",))


def _group_norm_cat(oa, ob, ga, gb):
    s = oa.shape[0]
    ts = _row_tile(s)

    def body(oa_ref, ob_ref, ga_ref, gb_ref, y_ref):
        for o_ref, g_ref, lo, w in ((oa_ref, ga_ref, 0, QA), (ob_ref, gb_ref, QA, QB)):
            ov = o_ref[...]
            r = lax.rsqrt(jnp.mean(ov * ov, axis=-1, keepdims=True) + EPS)
            y_ref[:, lo:lo + w] = ((ov * r) * g_ref[...]).astype(BF16)

    return _call(body, name="group_norm_cat", grid=(s // ts,),
                 in_specs=[pl.BlockSpec((ts, QA), lambda i: (i, 0)), pl.BlockSpec((ts, QB), lambda i: (i, 0)),
                           _vec_spec(QA), _vec_spec(QB)],
                 out_specs=pl.BlockSpec((ts, QA + QB), lambda i: (i, 0)),
                 out_shape=jax.ShapeDtypeStruct((s, QA + QB), BF16), args=[oa, ob, ga, gb], sem=("parallel",))


def _group_norm_bwd(dy, oa, ob, ga, gb):
    s = oa.shape[0]
    ts = _row_tile(s)

    def body(dy_ref, oa_ref, ob_ref, ga_ref, gb_ref, doa_ref, dob_ref, dga_ref, dgb_ref):
        @pl.when(pl.program_id(0) == 0)
        def _():
            dga_ref[...] = jnp.zeros_like(dga_ref)
            dgb_ref[...] = jnp.zeros_like(dgb_ref)

        for o_ref, g_ref, do_ref, dg_ref, lo, w in ((oa_ref, ga_ref, doa_ref, dga_ref, 0, QA),
                                                    (ob_ref, gb_ref, dob_ref, dgb_ref, QA, QB)):
            ov = o_ref[...]
            dyv = dy_ref[:, lo:lo + w]
            r = lax.rsqrt(jnp.mean(ov * ov, axis=-1, keepdims=True) + EPS)
            n = ov * r
            dn = dyv * g_ref[...]
            do_ref[...] = r * (dn - n * jnp.mean(dn * n, axis=-1, keepdims=True))
            dg_ref[...] += jnp.sum(dyv * n, axis=0, keepdims=True)

    ra = pl.BlockSpec((ts, QA), lambda i: (i, 0))
    rb = pl.BlockSpec((ts, QB), lambda i: (i, 0))
    return _call(body, name="group_norm_bwd", grid=(s // ts,),
                 in_specs=[pl.BlockSpec((ts, QA + QB), lambda i: (i, 0)), ra, rb, _vec_spec(QA), _vec_spec(QB)],
                 out_specs=(ra, rb, _vec_spec(QA), _vec_spec(QB)),
                 out_shape=(jax.ShapeDtypeStruct((s, QA), F32), jax.ShapeDtypeStruct((s, QB), F32),
                            jax.ShapeDtypeStruct((1, QA), F32), jax.ShapeDtypeStruct((1, QB), F32)),
                 args=[dy, oa, ob, ga, gb], sem=("arbitrary",))


def _n_variants(n_back):
    return -(-n_back // QG) + 1


def _alibi_bias():
    i = np.arange(QROWS)[:, None]
    j = np.arange((QG + BACK_A) * CHUNK)[None, :]
    dist = np.abs(BACK_A * CHUNK + i - j).astype(np.float32)
    dc = j // CHUNK - i // CHUNK
    valid = (dc >= 0) & (dc <= BACK_A)
    slopes = np.array([2.0 ** (-8.0 * (h + 1) / H_A) for h in range(H_A)], dtype=np.float32)
    bias = -slopes[:, None, None] * dist[None]
    out = [np.where((valid & (j >= (BACK_A - QG * v) * CHUNK))[None], bias, np.float32(NEG_INF))
           for v in range(_n_variants(BACK_A))]
    return jnp.asarray(np.stack(out).astype(np.float32))


def _rel_index_matrix():
    cc = np.arange(SKEW)
    dist = np.where(cc < SKEW - QROWS, BACK_B * CHUNK - cc, BACK_B * CHUNK + SKEW - cc)
    idx = np.clip(dist, -REL_CLIP, REL_CLIP) + REL_CLIP
    m = np.zeros((SKEW, N_REL), np.float32)
    m[cc, idx] = 1.0
    return jnp.asarray(m)


def _toeplitz_bias(vec, carry=None):
    lk = (QG + BACK_B) * CHUNK
    nv = _n_variants(BACK_B)

    def body(v_ref, o_ref):
        xv = jnp.broadcast_to(v_ref[0], (QROWS, SKEW))
        row = lax.broadcasted_iota(jnp.int32, (QROWS, SKEW), 0)
        for bit in range(QROWS.bit_length() - 1):
            xv = jnp.where((row >> bit) & 1 == 1, pltpu.roll(xv, 1 << bit, 1), xv)
        ri = lax.broadcasted_iota(jnp.int32, (QROWS, lk), 0) // CHUNK
        col = lax.broadcasted_iota(jnp.int32, (QROWS, lk), 1)
        ci = col // CHUNK
        valid = (ci - ri >= 0) & (ci - ri <= BACK_B)
        for v in range(nv):
            o_ref[v, 0] = jnp.where(valid & (col >= (BACK_B - QG * v) * CHUNK), xv[:, :lk], NEG_INF)

    return _call(body, name="toeplitz_bias", grid=(H_B,),
                 in_specs=[pl.BlockSpec((1, 1, SKEW), lambda h: (h, 0, 0))],
                 out_specs=pl.BlockSpec((nv, 1, QROWS, lk), lambda h: (0, h, 0, 0)),
                 out_shape=jax.ShapeDtypeStruct((nv, H_B, QROWS, lk), F32), args=[vec], sem=("parallel",),
                 carry=carry)


def _diagonal_sums(dbias):
    lk = dbias.shape[2]

    def body(d_ref, o_ref):
        xp = jnp.concatenate([d_ref[0], jnp.zeros((QROWS, SKEW - lk), F32)], axis=1)
        xv = xp[0:CHUNK]
        for q in range(1, QG):
            xv = xv + pltpu.roll(xp[q * CHUNK:(q + 1) * CHUNK], SKEW - q * CHUNK, 1)
        row = lax.broadcasted_iota(jnp.int32, (CHUNK, SKEW), 0)
        for bit in range(CHUNK.bit_length() - 1):
            xv = jnp.where((row >> bit) & 1 == 1, pltpu.roll(xv, SKEW - (1 << bit), 1), xv)
        o_ref[0] = jnp.sum(xv, axis=0, keepdims=True)

    return _call(body, name="diagonal_sums", grid=(H_B,),
                 in_specs=[pl.BlockSpec((1, QROWS, lk), lambda h: (h, 0, 0))],
                 out_specs=pl.BlockSpec((1, 1, SKEW), lambda h: (h, 0, 0)),
                 out_shape=jax.ShapeDtypeStruct((H_B, 1, SKEW), F32), args=[dbias], sem=("parallel",))


def _attn_common(s, n_back, gqa, q_col, k_col, v_col, TPS):
    lk = (QG + n_back) * CHUNK
    pad = n_back * CHUNK
    wide = TPS * LANES
    q_spec = pl.BlockSpec((QROWS, wide), lambda t, g: (g, q_col // TPS + t))
    if gqa:
        k_spec = pl.BlockSpec((s, LANES), lambda t, g: (0, k_col))
        v_spec = pl.BlockSpec((s, LANES), lambda t, g: (0, v_col))
    else:
        k_spec = pl.BlockSpec((s, wide), lambda t, g: (0, k_col // TPS + t))
        v_spec = pl.BlockSpec((s, wide), lambda t, g: (0, v_col // TPS + t))
    last_variant = _n_variants(n_back) - 1
    bias_spec = pl.BlockSpec((None, 2 * TPS, QROWS, lk), lambda t, g: (jnp.minimum(g, last_variant), t, 0, 0))
    tile_spec = pl.BlockSpec((QROWS, wide), lambda t, g: (g, t))
    return lk, pad, q_spec, k_spec, v_spec, bias_spec, tile_spec


def _attention_fwd(proj, bias, sinks, *, n_back, gqa, q_col, k_col, v_col, TPS, name, carry=None):
    s = proj.shape[0]
    lk, pad, q_spec, k_spec, v_spec, bias_spec, tile_spec = _attn_common(s, n_back, gqa, q_col, k_col, v_col, TPS)
    n_t, n_g = 512 // (TPS * LANES), s // QROWS
    kv_wide = LANES if gqa else TPS * LANES

    def body(*refs):
        if gqa:
            q_ref, k_ref, v_ref, bias_ref, sink_ref, o_ref, l_ref, kpad, vpad = refs
        else:
            q_ref, k_ref, v_ref, bias_ref, o_ref, l_ref, kpad, vpad = refs
        t, g = pl.program_id(0), pl.program_id(1)

        @pl.when(g == 0)
        def _():
            kpad[0:pad, :] = jnp.zeros((pad, kv_wide), BF16)
            vpad[0:pad, :] = jnp.zeros((pad, kv_wide), BF16)
            kpad[pad:, :] = k_ref[...]
            vpad[pad:, :] = v_ref[...]

        start = pl.multiple_of(g * QROWS, QROWS)
        half = lax.broadcasted_iota(jnp.int32, (QROWS, LANES), 1) // HEAD_DIM
        for tt in range(TPS):
            lanes = slice(tt * LANES, (tt + 1) * LANES)
            kv_lanes = slice(0, LANES) if gqa else lanes
            kb = kpad[pl.ds(start, lk), kv_lanes]
            vb = vpad[pl.ds(start, lk), kv_lanes]
            q = q_ref[:, lanes] * (HEAD_DIM ** -0.5)
            if gqa:
                hk = (TPS * t + tt) // 2
                q_rolled = pltpu.roll(q.astype(F32), HEAD_DIM, 1).astype(BF16)
            outs, lses = [], []
            for e in range(2):
                if gqa:
                    kv_half = hk
                    src = jnp.where(hk == e, q, q_rolled)
                else:
                    kv_half = e
                    src = q
                qm = jnp.where(half == kv_half, src, jnp.zeros_like(src))
                sc = _dot_nt(qm, kb) + bias_ref[2 * tt + e]
                m = jnp.max(sc, axis=-1, keepdims=True)
                if gqa:
                    sk = sink_ref[2 * (TPS * t + tt) + e]
                    m = jnp.maximum(m, sk)
                p = jnp.exp(sc - m)
                l = jnp.sum(p, axis=-1, keepdims=True)
                if gqa:
                    l = l + jnp.exp(sk - m)
                pn = p / l
                outs.append(_dot(pn.astype(BF16), vb))
                lses.append(m + jnp.log(l))
            if gqa:
                same = jnp.where(hk == 0, outs[0], outs[1])
                other = jnp.where(hk == 0, outs[1], outs[0])
                o_ref[:, lanes] = jnp.where(half == hk, same, pltpu.roll(other, HEAD_DIM, 1))
            else:
                o_ref[:, lanes] = jnp.where(half == 0, outs[0], outs[1])
            l_ref[:, lanes] = jnp.where(half == 0, lses[0], lses[1])

    in_specs = [q_spec, k_spec, v_spec, bias_spec] + ([SMEM_SPEC] if gqa else [])
    args = [proj, proj, proj, bias] + ([sinks] if gqa else [])
    o_shape = jax.ShapeDtypeStruct((s, 512), F32)
    return _call(body, name=name, grid=(n_t, n_g), in_specs=in_specs, out_specs=(tile_spec, tile_spec),
                 out_shape=(o_shape, o_shape), args=args,
                 scratch=[pltpu.VMEM((s + pad, kv_wide), BF16), pltpu.VMEM((s + pad, kv_wide), BF16)],
                 sem=("arbitrary", "arbitrary"), carry=carry)


def _attention_bwd(proj, bias, sinks, do, lse, *, n_back, gqa, q_col, k_col, v_col, TPS, name, carry=None):
    s = proj.shape[0]
    lk, pad, q_spec, k_spec, v_spec, bias_spec, tile_spec = _attn_common(s, n_back, gqa, q_col, k_col, v_col, TPS)
    n_t, n_g = 512 // (TPS * LANES), s // QROWS
    kv_wide = LANES if gqa else TPS * LANES

    def body(*refs):
        if gqa:
            (q_ref, k_ref, v_ref, bias_ref, sink_ref, do_ref, l_ref,
             dq_ref, dk_ref, dv_ref, dsink_ref, kpad, vpad, dkpad, dvpad) = refs
        else:
            (q_ref, k_ref, v_ref, bias_ref, do_ref, l_ref,
             dq_ref, dk_ref, dv_ref, dbias_ref, kpad, vpad, dkpad, dvpad) = refs
        t, g = pl.program_id(0), pl.program_id(1)

        @pl.when(g == 0)
        def _():
            kpad[0:pad, :] = jnp.zeros((pad, kv_wide), BF16)
            vpad[0:pad, :] = jnp.zeros((pad, kv_wide), BF16)
            kpad[pad:, :] = k_ref[...]
            vpad[pad:, :] = v_ref[...]
            if gqa:
                dsink_ref[...] = jnp.zeros_like(dsink_ref)
            else:
                dbias_ref[...] = jnp.zeros_like(dbias_ref)

        @pl.when((g == 0) & (t == 0) if gqa else g == 0)
        def _():
            dkpad[...] = jnp.zeros_like(dkpad)
            dvpad[...] = jnp.zeros_like(dvpad)

        start = pl.multiple_of(g * QROWS, QROWS)
        half = lax.broadcasted_iota(jnp.int32, (QROWS, LANES), 1) // HEAD_DIM
        for tt in range(TPS):
            lanes = slice(tt * LANES, (tt + 1) * LANES)
            kv_lanes = slice(0, LANES) if gqa else lanes
            kb = kpad[pl.ds(start, lk), kv_lanes]
            vb = vpad[pl.ds(start, lk), kv_lanes]
            q = q_ref[:, lanes]
            dov = do_ref[:, lanes]
            lv = l_ref[:, lanes]
            if gqa:
                hk = (TPS * t + tt) // 2
                q_rolled = pltpu.roll(q.astype(F32), HEAD_DIM, 1).astype(BF16)
                do_rolled = pltpu.roll(dov, HEAD_DIM, 1)
            dqs = []
            dk_acc = jnp.zeros((lk, LANES), F32)
            dv_acc = jnp.zeros((lk, LANES), F32)
            for e in range(2):
                if gqa:
                    kv_half = hk
                    src = jnp.where(hk == e, q, q_rolled)
                    do_src = jnp.where(hk == e, dov, do_rolled)
                else:
                    kv_half = e
                    src = q
                    do_src = dov
                qm = jnp.where(half == kv_half, src, jnp.zeros_like(src))
                dom = jnp.where(half == kv_half, do_src, 0.0).astype(BF16)
                lcol = jnp.max(jnp.where(half == e, lv, -jnp.inf), axis=-1, keepdims=True)
                sc = _dot_nt(qm * (HEAD_DIM ** -0.5), kb) + bias_ref[2 * tt + e]
                pn = jnp.exp(sc - lcol)
                dp = _dot_nt(dom, vb)
                delta = jnp.sum(pn * dp, axis=-1, keepdims=True)
                ds = pn * (dp - delta)
                if gqa:
                    p_sink = jnp.exp(sink_ref[2 * (TPS * t + tt) + e] - lcol)
                    dsk = -jnp.sum(p_sink * delta, axis=0, keepdims=True)
                    row = 2 * tt + e
                    dsink_ref[0, row:row + 1, :] += jnp.broadcast_to(dsk, (1, LANES))
                else:
                    dbias_ref[2 * tt + e] += ds
                dsb = (ds * (HEAD_DIM ** -0.5)).astype(BF16)
                dqs.append(_dot(dsb, kb))
                dk_acc = dk_acc + _dot_tn(dsb, qm)
                dv_acc = dv_acc + _dot_tn(pn.astype(BF16), dom)
            dkpad[pl.ds(start, lk), kv_lanes] += dk_acc
            dvpad[pl.ds(start, lk), kv_lanes] += dv_acc
            if gqa:
                same = jnp.where(hk == 0, dqs[0], dqs[1])
                other = jnp.where(hk == 0, dqs[1], dqs[0])
                dq_ref[:, lanes] = jnp.where(half == hk, same, pltpu.roll(other, HEAD_DIM, 1)).astype(BF16)
            else:
                dq_ref[:, lanes] = jnp.where(half == 0, dqs[0], dqs[1]).astype(BF16)

        @pl.when((g == n_g - 1) & (t == n_t - 1) if gqa else g == n_g - 1)
        def _():
            dk_ref[...] = dkpad[pad:, :].astype(BF16)
            dv_ref[...] = dvpad[pad:, :].astype(BF16)

    in_specs = [q_spec, k_spec, v_spec, bias_spec] + ([SMEM_SPEC] if gqa else []) + [tile_spec, tile_spec]
    args = [proj, proj, proj, bias] + ([sinks] if gqa else []) + [do, lse]
    if gqa:
        kv_out = pl.BlockSpec((s, LANES), lambda t, g: (0, 0))
        kv_shape = jax.ShapeDtypeStruct((s, LANES), BF16)
        extra_spec = pl.BlockSpec((1, 8, LANES), lambda t, g: (t, 0, 0))
        extra_shape = jax.ShapeDtypeStruct((n_t, 8, LANES), F32)
    else:
        kv_out = pl.BlockSpec((s, kv_wide), lambda t, g: (0, t))
        kv_shape = jax.ShapeDtypeStruct((s, 512), BF16)
        extra_spec = pl.BlockSpec((2 * TPS, QROWS, lk), lambda t, g: (t, 0, 0))
        extra_shape = jax.ShapeDtypeStruct(bias.shape[1:], F32)
    return _call(body, name=name, grid=(n_t, n_g), in_specs=in_specs,
                 out_specs=(tile_spec, kv_out, kv_out, extra_spec),
                 out_shape=(jax.ShapeDtypeStruct((s, 512), BF16), kv_shape, kv_shape, extra_shape), args=args,
                 scratch=[pltpu.VMEM((s + pad, kv_wide), BF16), pltpu.VMEM((s + pad, kv_wide), BF16),
                          pltpu.VMEM((s + pad, kv_wide), F32), pltpu.VMEM((s + pad, kv_wide), F32)],
                 sem=("arbitrary", "arbitrary"), carry=carry)


def _sum_rows8(g):
    n = g.shape[2]

    def body(g_ref, o_ref):
        acc = g_ref[0]
        for j in range(1, N_DEV):
            acc = acc + g_ref[j]
        o_ref[...] = acc

    return pl.pallas_call(
        body, name="sum_small_grads", in_specs=[VMEM_SPEC], out_specs=VMEM_SPEC,
        out_shape=jax.ShapeDtypeStruct((1, n), F32), compiler_params=_params(),
    )(g)


def _ada_weight_grad(sc_t, dmod_cols):
    d = sc_t.shape[0]
    w = dmod_cols.shape[1]
    td = _pick(d, (256, 128))

    def body(sc_ref, dm_ref, o_ref):
        scv = sc_ref[...]
        dmv = dm_ref[...]
        acc = scv[:, 0:1] * dmv[0:1, :]
        for b in range(1, N_DEV):
            acc = acc + scv[:, b:b + 1] * dmv[b:b + 1, :]
        o_ref[...] = acc

    return _call(body, name="ada_weight_grad", grid=(d // td,),
                 in_specs=[pl.BlockSpec((td, N_DEV), lambda i: (i, 0)), pl.BlockSpec((N_DEV, w), lambda i: (0, 0))],
                 out_specs=pl.BlockSpec((td, w), lambda i: (i, 0)), out_shape=jax.ShapeDtypeStruct((d, w), F32),
                 args=[sc_t, dmod_cols], sem=("parallel",))


def _adamw_update(w, gv, m, v):
    nm = ADAM_B1 * m + (1.0 - ADAM_B1) * gv
    nv = ADAM_B2 * v + (1.0 - ADAM_B2) * (gv * gv)
    m_hat = nm / (1.0 - ADAM_B1 ** ADAM_STEP)
    v_hat = nv / (1.0 - ADAM_B2 ** ADAM_STEP)
    return -ADAM_LR * (m_hat / (jnp.sqrt(v_hat) + ADAM_EPS) + ADAM_WD * w), nm, nv


def _adamw(w, g, m, v, name):
    rows, cols = w.shape
    tr = _pick(rows, (256, 176, 128, 88, 64)) if rows > 256 else rows

    def body(w_ref, g_ref, m_ref, v_ref, d_ref, nm_ref, nv_ref):
        d_ref[...], nm_ref[...], nv_ref[...] = _adamw_update(w_ref[...], g_ref[...], m_ref[...], v_ref[...])

    spec = pl.BlockSpec((tr, cols), lambda i: (i, 0))
    shape = jax.ShapeDtypeStruct((rows, cols), F32)
    return _call(body, name=name, grid=(rows // tr,), in_specs=[spec] * 4, out_specs=(spec, spec, spec),
                 out_shape=(shape, shape, shape), args=[w, g, m, v], sem=("parallel",))


def _adamw_from_slots(w, own, slots, m, v, name):
    n_slots, rows, k = slots.shape

    def body(o_ref, s_ref, w_ref, m_ref, v_ref, g_ref, d_ref, nm_ref, nv_ref):
        gv = o_ref[...].astype(F32)
        for j in range(n_slots):
            gv = gv + s_ref[j].astype(F32)
        g_ref[...] = gv
        d_ref[...], nm_ref[...], nv_ref[...] = _adamw_update(w_ref[...], gv, m_ref[...], v_ref[...])

    tr = rows // 2 if rows % 32 == 0 else rows
    spec = pl.BlockSpec((tr, k), lambda i: (i, 0))
    shape = jax.ShapeDtypeStruct((rows, k), F32)
    return _call(body, name=name, grid=(rows // tr,),
                 in_specs=[spec, pl.BlockSpec((n_slots, tr, k), lambda i: (0, i, 0)), spec, spec, spec],
                 out_specs=(spec, spec, spec, spec), out_shape=(shape, shape, shape, shape),
                 args=[own, slots, w, m, v], sem=("parallel",))


def _adamw_small(g, w, m, v, sizes):
    n = w.shape[1]
    offs, off = [], 0
    for size in sizes:
        offs.append(off)
        off += size + (-size % LANES)

    def body(g_ref, w_ref, m_ref, v_ref, *out_refs):
        gv = g_ref[:, 0:n]
        dv, nm, nv = _adamw_update(w_ref[...], gv, m_ref[...], v_ref[...])
        for j, (o, size) in enumerate(zip(offs, sizes)):
            for k, val in enumerate((gv, dv, nm, nv)):
                out_refs[4 * j + k][...] = val[:, o:o + size]

    shapes = [jax.ShapeDtypeStruct((1, size), F32) for size in sizes for _ in range(4)]
    return pl.pallas_call(
        body, name="adamw_small", in_specs=[VMEM_SPEC] * 4, out_specs=tuple([VMEM_SPEC] * len(shapes)),
        out_shape=tuple(shapes), compiler_params=_params(),
    )(g, w, m, v)


SMALL = ("b_ada", "g_pre_ffn1", "g_post_ffn1", "g_pre_mix", "b_in", "sinks_a", "rel_bias_b", "g_grp_a",
         "g_grp_b", "b_out", "g_post_mix", "g_pre_ffn2", "g_post_ffn2")
WEIGHTS = ("w_ada", "b_ada", "g_pre_ffn1", "w_gate1", "w_up1", "w_down1", "g_post_ffn1", "g_pre_mix", "w_in",
           "b_in", "sinks_a", "rel_bias_b", "g_grp_a", "g_grp_b", "w_out", "b_out", "g_post_mix", "g_pre_ffn2",
           "w_gate2", "w_up2", "w_down2", "g_post_ffn2")


def kernel(x, c, w_ada, b_ada, g_pre_ffn1, w_gate1, w_up1, w_down1, g_post_ffn1, g_pre_mix, w_in, b_in, sinks_a, rel_bias_b, g_grp_a, g_grp_b, w_out, b_out, g_post_mix, g_pre_ffn2, w_gate2, w_up2, w_down2, g_post_ffn2, loss_target, m_w_ada, m_b_ada, m_g_pre_ffn1, m_w_gate1, m_w_up1, m_w_down1, m_g_post_ffn1, m_g_pre_mix, m_w_in, m_b_in, m_sinks_a, m_rel_bias_b, m_g_grp_a, m_g_grp_b, m_w_out, m_b_out, m_g_post_mix, m_g_pre_ffn2, m_w_gate2, m_w_up2, m_w_down2, m_g_post_ffn2, v_w_ada, v_b_ada, v_g_pre_ffn1, v_w_gate1, v_w_up1, v_w_down1, v_g_post_ffn1, v_g_pre_mix, v_w_in, v_b_in, v_sinks_a, v_rel_bias_b, v_g_grp_a, v_g_grp_b, v_w_out, v_b_out, v_g_post_mix, v_g_pre_ffn2, v_w_gate2, v_w_up2, v_w_down2, v_g_post_ffn2):
    given = dict(locals())
    weights = {n: given[n] for n in WEIGHTS}
    mom_m = {n: given["m_" + n] for n in WEIGHTS}
    mom_v = {n: given["v_" + n] for n in WEIGHTS}

    me = 4 * lax.axis_index("x") + 2 * lax.axis_index("y") + lax.axis_index("c")
    xs = x[0]
    tgt = loss_target[0]
    d_model = xs.shape[1]
    ada_cols = w_ada.shape[2]

    sh = {"wg1": w_gate1[0].T, "wu1": w_up1[0].T, "wd1": w_down1[0], "win": w_in[0].T, "wo": w_out[0],
          "wg2": w_gate2[0].T, "wu2": w_up2[0].T, "wd2": w_down2[0]}
    sh = {k: v.astype(BF16) for k, v in sh.items()}

    def gather(full=(), new=(), cont=()):
        return _gather_carry([sh[n] for n in full], [sh[n] for n in new], cont)

    bias_a = _alibi_bias()
    rel_m = _rel_index_matrix()
    rel_vec = jnp.dot(rel_bias_b[0], rel_m.T, precision=lax.Precision.HIGHEST)
    bias_b, (wg1, wu1, wd1_part) = _toeplitz_bias(rel_vec.reshape(H_B, 1, SKEW),
                                                  carry=gather(full=("wg1", "wu1"), new=("wd1",)))

    b_cols = lax.dynamic_slice(b_ada, (0, me * ada_cols), (1, ada_cols))
    (sc_all, mod_rows), _ = _ada_forward(c, w_ada[0], b_cols, _Carry([], [], [], lambda *a: None, lambda *a: None))
    mod = mod_rows.reshape(N_MOD, d_model)
    shift1, scale1, gate1, shift2, scale2, gate2, shift3, scale3, gate3 = (mod[i:i + 1] for i in range(N_MOD))

    h1 = _pre_norm(xs, g_pre_ffn1, scale1, shift1, "pre_norm_ffn1")
    (a1, b1, u1), (win_part, wo_part, wd1) = _ffn_up(h1, wg1, wu1, "ffn_up_ffn1",
                                                     carry=gather(new=("win", "wo"), cont=(wd1_part,)))
    (y1, x1, h2), (wg2_part, win, wo) = _mm_nn(
        [(u1, wd1)], "ffn_down_ffn1", F32, carry=gather(new=("wg2",), cont=(win_part, wo_part)),
        tail=_tail_post_pre(xs, g_post_ffn1, gate1, 0.5, g_pre_mix, scale2, shift2))

    proj, (wg2,) = _mm_nt(h2, win, "in_proj", BF16, bias=b_in, carry=gather(cont=(wg2_part,)))
    sinks = sinks_a[0]
    cfg_a = dict(n_back=BACK_A, gqa=True, q_col=0, k_col=QA // LANES, v_col=(QA + KVA) // LANES, TPS=TPS_A)
    cfg_b = dict(n_back=BACK_B, gqa=False, q_col=(QA + 2 * KVA) // LANES, k_col=(QA + 2 * KVA + QB) // LANES,
                 v_col=(QA + 2 * KVA + 2 * QB) // LANES, TPS=TPS_B)
    (oa, lse_a), (wu2_part, wd2_part) = _attention_fwd(proj, bias_a, sinks, name="attn_a",
                                                       carry=gather(new=("wu2", "wd2")), **cfg_a)
    (ob, lse_b), (wu2, wd2) = _attention_fwd(proj, bias_b, None, name="attn_b",
                                             carry=gather(cont=(wu2_part, wd2_part)), **cfg_b)
    ycat = _group_norm_cat(oa, ob, g_grp_a, g_grp_b)
    ymix, x2, h3 = _mm_nn([(ycat, wo)], "out_proj", F32, bias=b_out,
                          tail=_tail_post_pre(x1, g_post_mix, gate2, 1.0, g_pre_ffn2, scale3, shift3))

    a3, b3, u3 = _ffn_up(h3, wg2, wu2, "ffn_up_ffn2")

    flights, own = {}, {}

    def grad_pair(key, a_mat, b_mat, name):
        part, own[key] = _mm_tn_pair(a_mat, b_mat, name)
        return part

    def scatter_start(tag, after_vec, **parts):
        names = list(parts)
        sems, p_thru, lands, token = _scatter_start([parts[n] for n in names], "scatter_start_" + tag)
        flights[tag] = (names, sems, p_thru, lands)
        return after_vec + token[0:1, 0:1]

    dx3, dy, loss_part, s1 = _mm_nn([(u3, wd2)], "ffn_down_ffn2", None,
                                    tail=_tail_post_loss(x2, tgt, g_post_ffn2, gate3, 0.5))
    da, db = _ffn_down_bwd(dy, wd2, a3, b3, "ffn_down_bwd_ffn2")
    dwd2 = grad_pair("wd2", u3, dy, "grad_wd_ffn2")
    dwg2 = grad_pair("wg2", da, h3, "grad_wg_ffn2")
    dwu2 = grad_pair("wu2", db, h3, "grad_wu_ffn2")
    g_pre_tied = scatter_start("ffn2", g_pre_ffn2, wd2=dwd2, wg2=dwg2, wu2=dwu2)
    dx2, dymix, s2, s3, s1m, db_out = _mm_nn(
        [(da, wg2), (db, wu2)], "ffn_up_bwd_ffn2", None,
        tail=_tail_pre_post_bwd(x2, dx3, ymix, g_pre_tied, scale3, g_post_mix, gate2, 1.0))
    sm3 = dict(shift=s3, scale=s2 * g_pre_ffn2, gate=0.5 * g_post_ffn2 * s1,
               g_pre=(1.0 + scale3) * s2, g_post=(0.5 * gate3) * s1)

    dycat = _mm_nt(dymix, wo, "out_proj_bwd", F32)
    dwo = grad_pair("wo", ycat, dymix, "grad_wo")
    doa, dob, dg_a, dg_b = _group_norm_bwd(dycat, oa, ob, g_grp_a, g_grp_b)
    dqa, dka, dva, dsink = _attention_bwd(proj, bias_a, sinks, doa, lse_a, name="attn_a_bwd", **cfg_a)
    dqb, dkb, dvb, dbias = _attention_bwd(proj, bias_b, None, dob, lse_b, name="attn_b_bwd", **cfg_b)
    dproj = jnp.concatenate([dqa, dka, dva, dqb, dkb, dvb], axis=1)
    dwin, own["win"], db_in = _mm_tn_pair(dproj, h2, "grad_win", col_sums=True)
    g_pre_tied = scatter_start("mix", g_pre_mix, wo=dwo, win=dwin)
    dx1, dy, s2m, s3m, s1, _ = _mm_nn(
        [(dproj, win)], "in_proj_bwd", None,
        tail=_tail_pre_post_bwd(x1, dx2, y1, g_pre_tied, scale2, g_post_ffn1, gate1, 0.5))
    d_rel = jnp.dot(_diagonal_sums(dbias).reshape(H_B, SKEW), rel_m, precision=lax.Precision.HIGHEST)
    d_sinks = dsink[:, :2 * TPS_A, 0].reshape(1, H_A)

    da, db = _ffn_down_bwd(dy, wd1, a1, b1, "ffn_down_bwd_ffn1")
    dwd1 = grad_pair("wd1", u1, dy, "grad_wd_ffn1")
    dwg1 = grad_pair("wg1", da, h1, "grad_wg_ffn1")
    dwu1 = grad_pair("wu1", db, h1, "grad_wu_ffn1")
    g_pre_tied = scatter_start("ffn1", g_pre_ffn1, wd1=dwd1, wg1=dwg1, wu1=dwu1)
    dx0, s2, s3 = _mm_nn([(da, wg1), (db, wu1)], "ffn_up_bwd_ffn1", None,
                         tail=_tail_pre_bwd(xs, dx1, g_pre_tied, scale1))
    sm1 = dict(shift=s3, scale=s2 * g_pre_ffn1, gate=0.5 * g_post_ffn1 * s1,
               g_pre=(1.0 + scale1) * s2, g_post=(0.5 * gate1) * s1)

    dmod = jnp.concatenate([sm1["shift"], sm1["scale"], sm1["gate"],
                            s3m, s2m * g_pre_mix, g_post_mix * s1m,
                            sm3["shift"], sm3["scale"], sm3["gate"]], axis=1)
    small_parts = {
        "b_ada": dmod, "g_pre_ffn1": sm1["g_pre"], "g_post_ffn1": sm1["g_post"],
        "g_pre_mix": (1.0 + scale2) * s2m, "b_in": db_in, "sinks_a": d_sinks,
        "rel_bias_b": d_rel.reshape(1, H_B * N_REL), "g_grp_a": dg_a, "g_grp_b": dg_b, "b_out": db_out,
        "g_post_mix": gate2 * s1m, "g_pre_ffn2": sm3["g_pre"], "g_post_ffn2": sm3["g_post"]}
    sizes = [small_parts[n].shape[1] for n in SMALL]

    def pack(parts):
        cells = []
        for p in parts:
            cells.append(p)
            if p.shape[1] % LANES:
                cells.append(jnp.zeros((1, -p.shape[1] % LANES), F32))
        return jnp.concatenate(cells, axis=1)

    packed = pack([small_parts[n] for n in SMALL] + [loss_part])
    n_packed = packed.shape[1]
    small_sems, packed_thru, small_land, small_token = _small_gather_start(packed)

    out_g, out_d, out_m, out_v = {}, {}, {}, {}
    groups = (("ffn2", (("w_gate2", "wg2", True), ("w_up2", "wu2", True), ("w_down2", "wd2", False))),
              ("mix", (("w_in", "win", True), ("w_out", "wo", False))),
              ("ffn1", (("w_gate1", "wg1", True), ("w_up1", "wu1", True), ("w_down1", "wd1", False))))
    after = small_token
    for tag, members in groups:
        names, sems, p_thru, lands = flights[tag]
        _, l_done = _scatter_wait(sems, p_thru, lands, after, "scatter_wait_" + tag)
        slots = dict(zip(names, l_done))
        for n, key, transposed in members:
            view = (lambda t: t.T) if transposed else (lambda t: t)
            res = _adamw_from_slots(view(weights[n][0]), own[key], slots[key], view(mom_m[n][0]),
                                    view(mom_v[n][0]), "adamw_" + n)
            out_g[n], out_d[n], out_m[n], out_v[n] = (view(t)[None] for t in res)
            after = res[3]

    packed_done, small_land = _small_gather_wait(small_sems, packed_thru, small_land, after)
    gathered = lax.dynamic_update_slice(small_land, packed_done[None], (me, 0, 0))
    small_sum = _sum_rows8(gathered)
    loss = small_sum[0, n_packed - LANES]
    dmod_cols = lax.dynamic_slice(gathered.reshape(N_DEV, n_packed), (0, me * ada_cols), (N_DEV, ada_cols))
    g_ada = _ada_weight_grad(sc_all.reshape(N_DEV, d_model).T, dmod_cols)
    d_, m_, v_ = _adamw(w_ada[0], g_ada, m_w_ada[0], v_w_ada[0], "adamw_w_ada")
    out_g["w_ada"], out_d["w_ada"], out_m["w_ada"], out_v["w_ada"] = g_ada[None], d_[None], m_[None], v_[None]

    small_out = _adamw_small(small_sum, *(pack([tree[n].reshape(1, -1) for n in SMALL])
                                          for tree in (weights, mom_m, mom_v)), sizes)
    for j, n in enumerate(SMALL):
        shape = weights[n].shape
        out_g[n], out_d[n], out_m[n], out_v[n] = (t.reshape(shape) for t in small_out[4 * j:4 * j + 4])

    return (loss, dx0[None], *[out_g[n] for n in WEIGHTS], *[out_d[n] for n in WEIGHTS],
            *[out_m[n] for n in WEIGHTS], *[out_v[n] for n in WEIGHTS])
```

```python
import numpy as np
import jax
import jax.numpy as jnp
from jax import lax
from jax.experimental import pallas as pl
from jax.experimental.pallas import tpu as pltpu

F32 = jnp.float32
BF16 = jnp.bfloat16
MESH = pl.DeviceIdType.MESH
ANY = pl.BlockSpec(memory_space=pl.ANY)
VMEM_SPEC = pl.BlockSpec(memory_space=pltpu.VMEM)
SMEM_SPEC = pl.BlockSpec(memory_space=pltpu.SMEM)

N_DEV = 8
CHUNK = 64
HEAD_DIM = 64
LANES = 128
H_A, KV_A, H_B = 8, 2, 8
BACK_A, BACK_B = 2, 8
REL_CLIP = 128
N_REL = 2 * REL_CLIP + 1
QA, KVA, QB = H_A * HEAD_DIM, KV_A * HEAD_DIM, H_B * HEAD_DIM
D_IN = QA + 2 * KVA + 3 * QB
N_MOD = 9
EPS = 1e-6
NEG_INF = -1e30
QG = 4
QROWS = QG * CHUNK
TPS_A, TPS_B = 4, 2
SKEW = 1024
ADAM_LR, ADAM_B1, ADAM_B2, ADAM_EPS, ADAM_WD, ADAM_STEP = 0.001, 0.9, 0.999, 1e-08, 0.01, 10
VMEM_LIMIT = 56 * 2 ** 20


def _pick(n, cands):
    for c in cands:
        if n % c == 0:
            return c
    return n


def _pieces(n, width=2 * LANES):
    return [(lo, min(lo + width, n)) for lo in range(0, n, width)]


def _params(sem=None):
    return pltpu.CompilerParams(dimension_semantics=sem, vmem_limit_bytes=VMEM_LIMIT)


def _dot_nt(a, b):
    return lax.dot_general(a, b, (((1,), (1,)), ((), ())), preferred_element_type=F32)


def _dot_tn(a, b):
    return lax.dot_general(a, b, (((0,), (0,)), ((), ())), preferred_element_type=F32)


def _dot(a, b):
    return jnp.dot(a, b, preferred_element_type=F32)


def _sigmoid(a):
    return 0.5 * (jnp.tanh(0.5 * a) + 1.0)


def _mesh_pos():
    return lax.axis_index("x"), lax.axis_index("y"), lax.axis_index("c")


def _peer(x, y, c, r):
    px = 1 - x if r & 4 else x
    py = 1 - y if r & 2 else y
    pc = 1 - c if r & 1 else c
    return px, py, pc


class _Carry:
    def __init__(self, ins, out_shapes, scratch, start, finish):
        self.ins, self.out_shapes, self.scratch = list(ins), list(out_shapes), list(scratch)
        self.start, self.finish = start, finish


def _call(body, *, name, grid, in_specs, out_specs, out_shape, args, scratch=(), sem=None, carry=None):
    single = not isinstance(out_shape, (tuple, list))
    out_specs = (out_specs,) if single else tuple(out_specs)
    out_shape = (out_shape,) if single else tuple(out_shape)
    if carry is None:
        res = pl.pallas_call(body, name=name, grid=grid, in_specs=list(in_specs), out_specs=out_specs,
                             out_shape=out_shape, scratch_shapes=list(scratch), compiler_params=_params(sem))(*args)
        return res[0] if single else res
    n_in, n_out, n_s = len(in_specs), len(out_shape), len(scratch)
    ci, co = len(carry.ins), len(carry.out_shapes)

    def wrapped(*refs):
        ins, cins = refs[:n_in], refs[n_in:n_in + ci]
        outs = refs[n_in + ci:n_in + ci + n_out]
        couts = refs[n_in + ci + n_out:n_in + ci + n_out + co]
        scr = refs[n_in + ci + n_out + co:n_in + ci + n_out + co + n_s]
        cscr = refs[n_in + ci + n_out + co + n_s:]
        first, last = None, None
        for ax, n in enumerate(grid):
            f, l = pl.program_id(ax) == 0, pl.program_id(ax) == n - 1
            first = f if first is None else first & f
            last = l if last is None else last & l
        pl.when(first)(lambda: carry.start(cins, couts, cscr))
        body(*ins, *outs, *scr)
        pl.when(last)(lambda: carry.finish(cins, couts, cscr))

    res = pl.pallas_call(
        wrapped, name=name, grid=grid, in_specs=list(in_specs) + [ANY] * ci, out_specs=out_specs + (ANY,) * co,
        out_shape=out_shape + tuple(carry.out_shapes), scratch_shapes=list(scratch) + carry.scratch,
        compiler_params=_params(("arbitrary",) * len(grid)))(*args, *carry.ins)
    main = res[:n_out]
    return (main[0] if single else main), res[n_out:]


def _gather_carry(shards):
    n_w = len(shards)
    rows = [s.shape[0] for s in shards]

    def plan(ins, outs, scr):
        send_sems, recv_sems, local_sems = scr
        x, y, c = _mesh_pos()
        me, sibling = (x, y, c), (x, y, 1 - c)
        chips = [(1 - x, y), (x, 1 - y), (1 - x, 1 - y)]

        def block(w, dev):
            start = pl.multiple_of((4 * dev[0] + 2 * dev[1] + dev[2]) * rows[w], 16)
            return outs[w].at[pl.ds(start, rows[w]), :]

        def copy(w, k, dev, to, src=None):
            return pltpu.make_async_remote_copy(
                src_ref=block(w, dev) if src is None else src, dst_ref=block(w, dev),
                send_sem=send_sems.at[w, k], recv_sem=recv_sems.at[w, k], device_id=to, device_id_type=MESH)

        mine = [pltpu.make_async_copy(ins[w], block(w, me), local_sems.at[w]) for w in range(n_w)]
        first = []
        for j, chip in enumerate(chips):
            first += [copy(w, 1 + j, me, (*chip, c), src=ins[w]) for w in range(n_w)]
        first += [copy(w, 0, me, sibling, src=ins[w]) for w in range(n_w)]
        return c, me, sibling, chips, copy, mine, first

    def start(ins, outs, scr):
        _, _, _, _, _, mine, first = plan(ins, outs, scr)
        for cp in mine + first:
            cp.start()

    def finish(ins, outs, scr):
        c, me, sibling, chips, copy, mine, first = plan(ins, outs, scr)
        passed = []
        for j, chip in enumerate(chips):
            for w in range(n_w):
                copy(w, 1 + j, (*chip, c), me).wait_recv()
                cp = copy(w, 4 + j, (*chip, c), sibling)
                cp.start()
                passed.append(cp)
        for w in range(n_w):
            copy(w, 0, sibling, me).wait_recv()
        for j, chip in enumerate(chips):
            for w in range(n_w):
                copy(w, 4 + j, (*chip, 1 - c), me).wait_recv()
        for cp in first + passed:
            cp.wait_send()
        for cp in mine:
            cp.wait()

    return _Carry(
        shards, [jax.ShapeDtypeStruct((N_DEV * s.shape[0], s.shape[1]), s.dtype) for s in shards],
        [pltpu.SemaphoreType.DMA((n_w, N_DEV - 1)), pltpu.SemaphoreType.DMA((n_w, N_DEV - 1)),
         pltpu.SemaphoreType.DMA((n_w,))], start, finish)


HBM_SPEC = pl.BlockSpec(memory_space=pltpu.HBM)
SEM_SPEC = pl.BlockSpec(memory_space=pltpu.SEMAPHORE)
N_CHIP = N_DEV // 2


def _scatter_copy(part_ref, land_ref, send_sem, recv_sem, r, rows):
    x, y, c = _mesh_pos()
    px, py, _ = _peer(x, y, c, 2 * r)
    src = part_ref.at[pl.ds(pl.multiple_of((2 * px + py) * rows, 16), rows), :]
    return pltpu.make_async_remote_copy(
        src_ref=src, dst_ref=land_ref.at[r - 1], send_sem=send_sem, recv_sem=recv_sem,
        device_id=(px, py, c), device_id_type=MESH)


def _scatter_order(n_w):
    return [(w, r) for r in (3, 2, 1) for w in range(n_w)]


def _scatter_start(parts, name):
    n_w = len(parts)
    rows = [p.shape[0] // N_CHIP for p in parts]
    order = _scatter_order(n_w)
    lands = [pltpu.with_memory_space_constraint(lax.empty((N_CHIP - 1, r, p.shape[1]), p.dtype), pltpu.HBM)
             for r, p in zip(rows, parts)]

    def body(*refs):
        part_refs, land_refs = refs[:n_w], refs[n_w:2 * n_w]
        sems = refs[2 * n_w:2 * n_w + 2 * len(order)]
        token = refs[-1]
        for j, (w, r) in enumerate(order):
            _scatter_copy(part_refs[w], land_refs[w], sems[2 * j], sems[2 * j + 1], r, rows[w]).start()
        token[...] = jnp.zeros_like(token)

    n_sem = 2 * len(order)
    res = pl.pallas_call(
        body, name=name,
        out_shape=(*[pltpu.SemaphoreType.DMA(())] * n_sem, *[pltpu.HBM(p.shape, p.dtype) for p in parts],
                   *[pltpu.HBM(l.shape, l.dtype) for l in lands], jax.ShapeDtypeStruct((8, LANES), F32)),
        in_specs=[HBM_SPEC] * (2 * n_w), out_specs=(*[SEM_SPEC] * n_sem, *[HBM_SPEC] * (2 * n_w), VMEM_SPEC),
        input_output_aliases={i: n_sem + i for i in range(2 * n_w)},
        compiler_params=pltpu.CompilerParams(has_side_effects=pltpu.SideEffectType.DATAFLOW_SIDE_EFFECTING),
    )(*[pltpu.with_memory_space_constraint(p, pltpu.HBM) for p in parts], *lands)
    return (list(res[:n_sem]), list(res[n_sem:n_sem + n_w]), list(res[n_sem + n_w:n_sem + 2 * n_w]), res[-1])


def _scatter_wait(sems, parts, lands, after, name):
    n_w = len(parts)
    rows = [p.shape[0] // N_CHIP for p in parts]
    order = _scatter_order(n_w)

    def body(*refs):
        part_refs, land_refs = refs[:n_w], refs[n_w:2 * n_w]
        sem_refs = refs[2 * n_w:2 * n_w + 2 * len(order)]
        for j, (w, r) in enumerate(order):
            cp = _scatter_copy(part_refs[w], land_refs[w], sem_refs[2 * j], sem_refs[2 * j + 1], r, rows[w])
            cp.wait_send()
            cp.wait_recv()

    res = pl.pallas_call(
        body, name=name,
        out_shape=(*[pltpu.HBM(p.shape, p.dtype) for p in parts], *[pltpu.HBM(l.shape, l.dtype) for l in lands]),
        in_specs=[HBM_SPEC] * (2 * n_w) + [SEM_SPEC] * len(sems) + [ANY],
        out_specs=tuple([HBM_SPEC] * (2 * n_w)),
        input_output_aliases={i: i for i in range(2 * n_w)},
        compiler_params=pltpu.CompilerParams(has_side_effects=pltpu.SideEffectType.DATAFLOW_SIDE_EFFECTING),
    )(*parts, *lands, *sems, after)
    return list(res[:n_w]), list(res[n_w:])


def _small_copy(v_ref, land_ref, send_sem, recv_sem, r):
    x, y, c = _mesh_pos()
    px, py, pc = _peer(x, y, c, r)
    return pltpu.make_async_remote_copy(
        src_ref=v_ref, dst_ref=land_ref.at[4 * x + 2 * y + c], send_sem=send_sem, recv_sem=recv_sem,
        device_id=(px, py, pc), device_id_type=MESH)


def _small_gather_start(v):
    land = pltpu.with_memory_space_constraint(lax.empty((N_DEV,) + v.shape, v.dtype), pltpu.HBM)

    def body(v_ref, land_ref, *rest):
        sems, token = rest[:2 * (N_DEV - 1)], rest[-1]
        for r in range(1, N_DEV):
            _small_copy(v_ref, land_ref, sems[2 * r - 2], sems[2 * r - 1], r).start()
        token[...] = jnp.zeros_like(token)

    n_sem = 2 * (N_DEV - 1)
    res = pl.pallas_call(
        body, name="small_gather_start",
        out_shape=(*[pltpu.SemaphoreType.DMA(())] * n_sem, pltpu.HBM(v.shape, v.dtype),
                   pltpu.HBM(land.shape, land.dtype), jax.ShapeDtypeStruct((8, LANES), F32)),
        in_specs=[HBM_SPEC, HBM_SPEC], out_specs=(*[SEM_SPEC] * n_sem, HBM_SPEC, HBM_SPEC, VMEM_SPEC),
        input_output_aliases={0: n_sem, 1: n_sem + 1},
        compiler_params=pltpu.CompilerParams(has_side_effects=pltpu.SideEffectType.DATAFLOW_SIDE_EFFECTING),
    )(pltpu.with_memory_space_constraint(v, pltpu.HBM), land)
    return list(res[:n_sem]), res[n_sem], res[n_sem + 1], res[-1]


def _small_gather_wait(sems, v, land, after):
    def body(v_ref, land_ref, *rest):
        for r in range(1, N_DEV):
            cp = _small_copy(v_ref, land_ref, rest[2 * r - 2], rest[2 * r - 1], r)
            cp.wait_send()
            x, y, c = _mesh_pos()
            px, py, pc = _peer(x, y, c, r)
            pltpu.make_async_remote_copy(
                src_ref=v_ref, dst_ref=land_ref.at[4 * px + 2 * py + pc], send_sem=rest[2 * r - 2],
                recv_sem=rest[2 * r - 1], device_id=(px, py, pc), device_id_type=MESH).wait_recv()

    res = pl.pallas_call(
        body, name="small_gather_wait",
        out_shape=(pltpu.HBM(v.shape, v.dtype), pltpu.HBM(land.shape, land.dtype)),
        in_specs=[HBM_SPEC, HBM_SPEC] + [SEM_SPEC] * len(sems) + [ANY], out_specs=(HBM_SPEC, HBM_SPEC),
        input_output_aliases={0: 0, 1: 1},
        compiler_params=pltpu.CompilerParams(has_side_effects=pltpu.SideEffectType.DATAFLOW_SIDE_EFFECTING),
    )(v, land, *sems, after)
    return res[0], res[1]


def _ada_forward(c_row, w_ada, b_cols, carry):
    d = c_row.shape[1]
    wcols = w_ada.shape[1]
    ci, co = len(carry.ins), len(carry.out_shapes)

    def body(*refs):
        c_ref, w_ref, b_ref = refs[:3]
        cins = refs[3:3 + ci]
        sc_ref, mod_ref = refs[3 + ci:5 + ci]
        couts = refs[5 + ci:5 + ci + co]
        rows_ref, send_sems, recv_sems = refs[5 + ci + co:8 + ci + co]
        cscr = refs[8 + ci + co:]
        carry.start(cins, couts, cscr)
        x, y, c = _mesh_pos()
        me = 4 * x + 2 * y + c
        cv = c_ref[...]
        sc_ref[me] = cv * _sigmoid(cv)

        sends = []
        for r in range(1, N_DEV):
            px, py, pc = _peer(x, y, c, r)
            cp = pltpu.make_async_remote_copy(
                src_ref=sc_ref.at[me], dst_ref=sc_ref.at[me], send_sem=send_sems.at[0, r - 1],
                recv_sem=recv_sems.at[0, r - 1], device_id=(px, py, pc), device_id_type=MESH)
            cp.start()
            sends.append(cp)
        for r in range(1, N_DEV):
            px, py, pc = _peer(x, y, c, r)
            pid = 4 * px + 2 * py + pc
            pltpu.make_async_remote_copy(
                src_ref=sc_ref.at[pid], dst_ref=sc_ref.at[pid], send_sem=send_sems.at[0, r - 1],
                recv_sem=recv_sems.at[0, r - 1], device_id=(px, py, pc), device_id_type=MESH).wait_recv()
        for cp in sends:
            cp.wait_send()

        sc_all = jnp.concatenate([sc_ref[j] for j in range(N_DEV)], axis=0)
        rows = _dot(sc_all.astype(BF16), w_ref[...].astype(BF16)) + b_ref[...]
        for j in range(N_DEV):
            rows_ref[j] = rows[j:j + 1, :]
        mod_ref[me] = rows_ref[me]

        sends = []
        for r in range(1, N_DEV):
            px, py, pc = _peer(x, y, c, r)
            pid = 4 * px + 2 * py + pc
            cp = pltpu.make_async_remote_copy(
                src_ref=rows_ref.at[pid], dst_ref=mod_ref.at[me], send_sem=send_sems.at[1, r - 1],
                recv_sem=recv_sems.at[1, r - 1], device_id=(px, py, pc), device_id_type=MESH)
            cp.start()
            sends.append(cp)
        for r in range(1, N_DEV):
            px, py, pc = _peer(x, y, c, r)
            pid = 4 * px + 2 * py + pc
            pltpu.make_async_remote_copy(
                src_ref=rows_ref.at[pid], dst_ref=mod_ref.at[pid], send_sem=send_sems.at[1, r - 1],
                recv_sem=recv_sems.at[1, r - 1], device_id=(px, py, pc), device_id_type=MESH).wait_recv()
        for cp in sends:
            cp.wait_send()
        carry.finish(cins, couts, cscr)

    res = pl.pallas_call(
        body, name="ada_forward",
        out_shape=(jax.ShapeDtypeStruct((N_DEV, 1, d), F32), jax.ShapeDtypeStruct((N_DEV, 1, wcols), F32),
                   *carry.out_shapes),
        in_specs=[VMEM_SPEC, VMEM_SPEC, VMEM_SPEC] + [ANY] * ci, out_specs=(VMEM_SPEC, VMEM_SPEC) + (ANY,) * co,
        scratch_shapes=[pltpu.VMEM((N_DEV, 1, wcols), F32), pltpu.SemaphoreType.DMA((2, N_DEV - 1)),
                        pltpu.SemaphoreType.DMA((2, N_DEV - 1))] + carry.scratch,
        compiler_params=_params(),
    )(c_row, w_ada, b_cols, *carry.ins)
    return res[:2], res[2:]


def _mm_nt(a, b, name, out_dtype, bias=None, carry=None):
    m, k = a.shape
    n = b.shape[0]
    tm = _pick(m, (512, 256, 128))
    tn = _pick(n, (1408, 1152, 1024, 768, 512, 256, 128))

    def body(*refs):
        acc = _dot_nt(refs[0][...], refs[1][...])
        if bias is not None:
            acc = acc + refs[2][...]
        refs[-1][...] = acc.astype(out_dtype)

    in_specs = [pl.BlockSpec((tm, k), lambda j, i: (i, 0)), pl.BlockSpec((tn, k), lambda j, i: (j, 0))]
    args = [a, b]
    if bias is not None:
        in_specs.append(pl.BlockSpec((1, tn), lambda j, i: (0, j)))
        args.append(bias)
    return _call(body, name=name, grid=(n // tn, m // tm), in_specs=in_specs,
                 out_specs=pl.BlockSpec((tm, tn), lambda j, i: (i, j)),
                 out_shape=jax.ShapeDtypeStruct((m, n), out_dtype), args=args,
                 sem=("parallel", "parallel"), carry=carry)


class _Tail:
    def __init__(self, rows, vecs, outs, fn):
        self.rows, self.vecs, self.outs, self.fn = list(rows), list(vecs), list(outs), fn


def _mm_nn(pairs, name, out_dtype, bias=None, carry=None, tail=None):
    m, k = pairs[0][0].shape
    n = pairs[0][1].shape[1]
    n_p = len(pairs)
    tm = _pick(m, (512, 256, 128))
    tk = k if n_p == 1 else _pick(k, (1408, 1152, 1024, 768, 512, 256, 128))
    nk = k // tk
    n_b = 0 if bias is None else 1
    n_r, n_v = (len(tail.rows), len(tail.vecs)) if tail else (0, 0)
    n_in = 2 * n_p + n_b + n_r + n_v
    n_main = 0 if out_dtype is None else 1

    def finish(acc, refs, first_tile):
        if bias is not None:
            acc = acc + refs[2 * n_p][...]
        outs = refs[n_in:-1]
        if n_main:
            outs[0][...] = acc.astype(out_dtype)
        if tail is None:
            return
        rows = [r[...] for r in refs[2 * n_p + n_b:2 * n_p + n_b + n_r]]
        vecs = [v[...] for v in refs[2 * n_p + n_b + n_r:n_in]]
        vals = tail.fn(acc, rows, vecs)
        for ref, val, (dtype, kind) in zip(outs[n_main:], vals, tail.outs):
            if kind == "row":
                ref[...] = val.astype(dtype)
            else:
                @pl.when(first_tile)
                def _(ref=ref):
                    ref[...] = jnp.zeros_like(ref)

                ref[...] += val

    def body(*refs):
        acc_ref = refs[-1]
        kk, i = pl.program_id(0), pl.program_id(1)
        part = _dot(refs[0][...], refs[1][...])
        for p in range(1, n_p):
            part = part + _dot(refs[2 * p][...], refs[2 * p + 1][...])
        if nk == 1:
            finish(part, refs, i == 0)
            return
        rows = pl.ds(pl.multiple_of(i * tm, tm), tm)

        @pl.when(kk == 0)
        def _():
            acc_ref[rows, :] = part

        if nk > 2:
            @pl.when((kk > 0) & (kk < nk - 1))
            def _():
                acc_ref[rows, :] += part

        @pl.when(kk == nk - 1)
        def _():
            finish(acc_ref[rows, :] + part, refs, i == 0)

    def last_only(kk, i):
        return (jnp.where(kk == nk - 1, i, 0), 0)

    row_spec = pl.BlockSpec((tm, n), last_only)
    vec_spec = pl.BlockSpec((1, n), lambda kk, i: (0, 0))
    in_specs, args = [], []
    for a, b in pairs:
        in_specs += [pl.BlockSpec((tm, tk), lambda kk, i: (i, kk)), pl.BlockSpec((tk, n), lambda kk, i: (kk, 0))]
        args += [a, b]
    if bias is not None:
        in_specs.append(vec_spec)
        args.append(bias)
    out_specs = [row_spec] * n_main
    out_shape = [jax.ShapeDtypeStruct((m, n), out_dtype)] if n_main else []
    if tail:
        in_specs += [row_spec] * n_r + [vec_spec] * n_v
        args += tail.rows + tail.vecs
        for dtype, kind in tail.outs:
            if kind == "row":
                out_specs.append(row_spec)
                out_shape.append(jax.ShapeDtypeStruct((m, n), dtype))
            else:
                width = n if kind == "sum" else 1
                out_specs.append(pl.BlockSpec((1, width), lambda kk, i: (0, 0)))
                out_shape.append(jax.ShapeDtypeStruct((1, width), dtype))
    if tail is None:
        out_specs, out_shape = out_specs[0], out_shape[0]
    return _call(body, name=name, grid=(nk, m // tm), in_specs=in_specs, out_specs=out_specs,
                 out_shape=out_shape, args=args,
                 scratch=[pltpu.VMEM((m, n) if nk > 1 else (8, LANES), F32)],
                 sem=("arbitrary", "arbitrary"), carry=carry)


def _rms(v):
    return lax.rsqrt(jnp.mean(v * v, axis=-1, keepdims=True) + EPS)


def _col(v):
    return jnp.sum(v, axis=0, keepdims=True)


def _tail_post_pre(x, g_post, gate, weight, g_pre, scale, shift):
    def fn(y, rows, vecs):
        (xv,), (gp, gt, g, sc, sh) = rows, vecs
        xo = xv + (weight * gt) * ((y * _rms(y)) * gp)
        return xo, ((xo * _rms(xo)) * g) * (1.0 + sc) + sh

    return _Tail([x], [g_post, gate, g_pre, scale, shift], [(F32, "row"), (BF16, "row")], fn)


def _tail_post_loss(x, target, g, gate, weight):
    def fn(y, rows, vecs):
        (xv, tv), (gv, gt) = rows, vecs
        r = _rms(y)
        yn = y * r
        err = (xv + (weight * gt) * (yn * gv)) - tv
        do = err * (1.0 / y.shape[1])
        dyn = do * ((weight * gt) * gv)
        dy = r * (dyn - yn * jnp.mean(dyn * yn, axis=-1, keepdims=True))
        return do, dy, 0.5 * _col(jnp.mean(err * err, axis=-1, keepdims=True)), _col(do * yn)

    return _Tail([x, target], [g, gate], [(F32, "row"), (BF16, "row"), (F32, "one"), (F32, "sum")], fn)


def _tail_pre_bwd(x, dres, g_pre, scale):
    def fn(dh, rows, vecs):
        (xv, dr), (g, sc) = rows, vecs
        r = _rms(xv)
        n = xv * r
        dn = dh * (g * (1.0 + sc))
        return dr + r * (dn - n * jnp.mean(dn * n, axis=-1, keepdims=True)), _col(dh * n), _col(dh)

    return _Tail([x, dres], [g_pre, scale], [(F32, "row"), (F32, "sum"), (F32, "sum")], fn)


def _tail_pre_post_bwd(x, dres, y, g_pre, scale, g_post, gate, weight):
    def fn(dh, rows, vecs):
        (xv, dr, yv), (g, sc, gp, gt) = rows, vecs
        r = _rms(xv)
        n = xv * r
        dn = dh * (g * (1.0 + sc))
        dx = dr + r * (dn - n * jnp.mean(dn * n, axis=-1, keepdims=True))
        ry = _rms(yv)
        yn = yv * ry
        dyn = dx * ((weight * gt) * gp)
        dy = ry * (dyn - yn * jnp.mean(dyn * yn, axis=-1, keepdims=True))
        return dx, dy, _col(dh * n), _col(dh), _col(dx * yn), _col(dy)

    return _Tail([x, dres, y], [g_pre, scale, g_post, gate],
                 [(F32, "row"), (BF16, "row")] + [(F32, "sum")] * 4, fn)


def _mm_tn_pair(a, b, name, col_sums=False):
    k, m = a.shape
    n = b.shape[1]
    rows = m // N_DEV
    n_chip = N_DEV // 2
    tm = 4 * rows
    tk = _pick(k, (1024, 512, 256, 128))
    nk = k // tk

    def body(a_ref, b_ref, p_ref, own_ref, *rest):
        acc_ref, keep_ref, send_ref, land_ref, send_sems, recv_sems = rest[-6:]
        i, kk = pl.program_id(0), pl.program_id(1)
        x, y, c = _mesh_pos()
        if col_sums:
            cs_ref = rest[0]
            part = jnp.sum(a_ref[...].astype(F32), axis=0, keepdims=True)

            @pl.when(kk == 0)
            def _():
                cs_ref[...] = part

            @pl.when(kk > 0)
            def _():
                cs_ref[...] += part

        def push(chip):
            return pltpu.make_async_remote_copy(
                src_ref=send_ref.at[chip], dst_ref=land_ref.at[chip], send_sem=send_sems.at[chip],
                recv_sem=recv_sems.at[chip], device_id=(x, y, 1 - c), device_id_type=MESH)

        if nk == 1:
            acc = _dot_tn(a_ref[...], b_ref[...])
        else:
            @pl.when(kk == 0)
            def _():
                acc_ref[...] = jnp.zeros_like(acc_ref)

            acc_ref[...] += _dot_tn(a_ref[...], b_ref[...])
            acc = acc_ref

        for t in range(2):
            @pl.when((kk == nk - 1) & (i == t))
            def _(t=t):
                for ob in range(4):
                    chip, core = 2 * t + ob // 2, ob % 2
                    blk = acc[ob * rows:(ob + 1) * rows, :]

                    @pl.when(c == core)
                    def _(chip=chip, blk=blk):
                        keep_ref[chip] = blk

                    @pl.when(c != core)
                    def _(chip=chip, blk=blk):
                        send_ref[chip] = blk.astype(BF16)
                        push(chip).start()

        @pl.when((kk == nk - 1) & (i == 1))
        def _():
            for chip in range(n_chip):
                push(chip).wait_recv()
                val = (keep_ref[chip] + land_ref[chip].astype(F32)).astype(BF16)
                p_ref[chip * rows:(chip + 1) * rows, :] = val

                @pl.when(2 * x + y == chip)
                def _(val=val):
                    own_ref[...] = val

            for chip in range(n_chip):
                push(chip).wait_send()

    out_specs = [pl.BlockSpec((n_chip * rows, n), lambda i, kk: (0, 0)), pl.BlockSpec((rows, n), lambda i, kk: (0, 0))]
    out_shape = [jax.ShapeDtypeStruct((n_chip * rows, n), BF16), jax.ShapeDtypeStruct((rows, n), BF16)]
    if col_sums:
        out_specs.append(pl.BlockSpec((1, tm), lambda i, kk: (0, i)))
        out_shape.append(jax.ShapeDtypeStruct((1, m), F32))
    return _call(body, name=name, grid=(2, nk),
                 in_specs=[pl.BlockSpec((tk, tm), lambda i, kk: (kk, i)), pl.BlockSpec((tk, n), lambda i, kk: (kk, 0))],
                 out_specs=out_specs, out_shape=out_shape, args=[a, b],
                 scratch=[pltpu.VMEM((tm, n) if nk > 1 else (8, LANES), F32), pltpu.VMEM((n_chip, rows, n), F32),
                          pltpu.VMEM((n_chip, rows, n), BF16), pltpu.VMEM((n_chip, rows, n), BF16),
                          pltpu.SemaphoreType.DMA((n_chip,)), pltpu.SemaphoreType.DMA((n_chip,))],
                 sem=("arbitrary", "arbitrary"))


def _ffn_up(h, wg_t, wu_t, name, carry=None):
    s, d = h.shape
    f = wg_t.shape[0]
    tm = _pick(s, (512, 256, 128))
    tf = _pick(f, (1408, 1024, 512, 256, 128))

    def body(h_ref, wg_ref, wu_ref, a_ref, b_ref, u_ref):
        hh = h_ref[...]
        for lo, hi in _pieces(tf):
            a = _dot_nt(hh, wg_ref[lo:hi, :])
            b = _dot_nt(hh, wu_ref[lo:hi, :])
            a_ref[:, lo:hi] = a.astype(BF16)
            b_ref[:, lo:hi] = b.astype(BF16)
            u_ref[:, lo:hi] = ((a * _sigmoid(a)) * b).astype(BF16)

    w_spec = pl.BlockSpec((tf, d), lambda j, i: (j, 0))
    o_spec = pl.BlockSpec((tm, tf), lambda j, i: (i, j))
    o_shape = jax.ShapeDtypeStruct((s, f), BF16)
    return _call(body, name=name, grid=(f // tf, s // tm),
                 in_specs=[pl.BlockSpec((tm, d), lambda j, i: (i, 0)), w_spec, w_spec],
                 out_specs=(o_spec, o_spec, o_spec), out_shape=(o_shape, o_shape, o_shape),
                 args=[h, wg_t, wu_t], sem=("parallel", "parallel"), carry=carry)


def _ffn_down_bwd(dy, wd, a, b, name, carry=None):
    s, d = dy.shape
    f = wd.shape[0]
    tm = _pick(s, (512, 256, 128))
    tf = _pick(f, (1408, 1024, 512, 256, 128))

    def body(dy_ref, wd_ref, a_ref, b_ref, da_ref, db_ref):
        dyv = dy_ref[...]
        for lo, hi in _pieces(tf):
            du = _dot_nt(dyv, wd_ref[lo:hi, :])
            a = a_ref[:, lo:hi].astype(F32)
            b = b_ref[:, lo:hi].astype(F32)
            sig = _sigmoid(a)
            da_ref[:, lo:hi] = (du * b * (sig * (1.0 + a * (1.0 - sig)))).astype(BF16)
            db_ref[:, lo:hi] = (du * (a * sig)).astype(BF16)

    t_spec = pl.BlockSpec((tm, tf), lambda j, i: (i, j))
    o_shape = jax.ShapeDtypeStruct((s, f), BF16)
    return _call(body, name=name, grid=(f // tf, s // tm),
                 in_specs=[pl.BlockSpec((tm, d), lambda j, i: (i, 0)), pl.BlockSpec((tf, d), lambda j, i: (j, 0)),
                           t_spec, t_spec],
                 out_specs=(t_spec, t_spec), out_shape=(o_shape, o_shape), args=[dy, wd, a, b],
                 sem=("parallel", "parallel"), carry=carry)


def _row_tile(s):
    return _pick(s, (256, 128, 64))


def _vec_spec(d):
    return pl.BlockSpec((1, d), lambda i: (0, 0))


def _pre_norm(x, g, scale, shift, name):
    s, d = x.shape
    ts = _row_tile(s)

    def body(x_ref, g_ref, sc_ref, sh_ref, h_ref):
        xv = x_ref[...]
        r = lax.rsqrt(jnp.mean(xv * xv, axis=-1, keepdims=True) + EPS)
        h_ref[...] = (((xv * r) * g_ref[...]) * (1.0 + sc_ref[...]) + sh_ref[...]).astype(BF16)

    row = pl.BlockSpec((ts, d), lambda i: (i, 0))
    return _call(body, name=name, grid=(s // ts,), in_specs=[row, _vec_spec(d), _vec_spec(d), _vec_spec(d)],
                 out_specs=row, out_shape=jax.ShapeDtypeStruct((s, d), BF16), args=[x, g, scale, shift],
                 sem=("parallel",))


def _group_norm_cat(oa, ob, ga, gb):
    s = oa.shape[0]
    ts = _row_tile(s)

    def body(oa_ref, ob_ref, ga_ref, gb_ref, y_ref):
        for o_ref, g_ref, lo, w in ((oa_ref, ga_ref, 0, QA), (ob_ref, gb_ref, QA, QB)):
            ov = o_ref[...]
            r = lax.rsqrt(jnp.mean(ov * ov, axis=-1, keepdims=True) + EPS)
            y_ref[:, lo:lo + w] = ((ov * r) * g_ref[...]).astype(BF16)

    return _call(body, name="group_norm_cat", grid=(s // ts,),
                 in_specs=[pl.BlockSpec((ts, QA), lambda i: (i, 0)), pl.BlockSpec((ts, QB), lambda i: (i, 0)),
                           _vec_spec(QA), _vec_spec(QB)],
                 out_specs=pl.BlockSpec((ts, QA + QB), lambda i: (i, 0)),
                 out_shape=jax.ShapeDtypeStruct((s, QA + QB), BF16), args=[oa, ob, ga, gb], sem=("parallel",))


def _group_norm_bwd(dy, oa, ob, ga, gb):
    s = oa.shape[0]
    ts = _row_tile(s)

    def body(dy_ref, oa_ref, ob_ref, ga_ref, gb_ref, doa_ref, dob_ref, dga_ref, dgb_ref):
        @pl.when(pl.program_id(0) == 0)
        def _():
            dga_ref[...] = jnp.zeros_like(dga_ref)
            dgb_ref[...] = jnp.zeros_like(dgb_ref)

        for o_ref, g_ref, do_ref, dg_ref, lo, w in ((oa_ref, ga_ref, doa_ref, dga_ref, 0, QA),
                                                    (ob_ref, gb_ref, dob_ref, dgb_ref, QA, QB)):
            ov = o_ref[...]
            dyv = dy_ref[:, lo:lo + w]
            r = lax.rsqrt(jnp.mean(ov * ov, axis=-1, keepdims=True) + EPS)
            n = ov * r
            dn = dyv * g_ref[...]
            do_ref[...] = r * (dn - n * jnp.mean(dn * n, axis=-1, keepdims=True))
            dg_ref[...] += jnp.sum(dyv * n, axis=0, keepdims=True)

    ra = pl.BlockSpec((ts, QA), lambda i: (i, 0))
    rb = pl.BlockSpec((ts, QB), lambda i: (i, 0))
    return _call(body, name="group_norm_bwd", grid=(s // ts,),
                 in_specs=[pl.BlockSpec((ts, QA + QB), lambda i: (i, 0)), ra, rb, _vec_spec(QA), _vec_spec(QB)],
                 out_specs=(ra, rb, _vec_spec(QA), _vec_spec(QB)),
                 out_shape=(jax.ShapeDtypeStruct((s, QA), F32), jax.ShapeDtypeStruct((s, QB), F32),
                            jax.ShapeDtypeStruct((1, QA), F32), jax.ShapeDtypeStruct((1, QB), F32)),
                 args=[dy, oa, ob, ga, gb], sem=("arbitrary",))


def _n_variants(n_back):
    return -(-n_back // QG) + 1


def _alibi_bias():
    i = np.arange(QROWS)[:, None]
    j = np.arange((QG + BACK_A) * CHUNK)[None, :]
    dist = np.abs(BACK_A * CHUNK + i - j).astype(np.float32)
    dc = j // CHUNK - i // CHUNK
    valid = (dc >= 0) & (dc <= BACK_A)
    slopes = np.array([2.0 ** (-8.0 * (h + 1) / H_A) for h in range(H_A)], dtype=np.float32)
    bias = -slopes[:, None, None] * dist[None]
    out = [np.where((valid & (j >= (BACK_A - QG * v) * CHUNK))[None], bias, np.float32(NEG_INF))
           for v in range(_n_variants(BACK_A))]
    return jnp.asarray(np.stack(out).astype(np.float32))


def _rel_index_matrix():
    cc = np.arange(SKEW)
    dist = np.where(cc < SKEW - QROWS, BACK_B * CHUNK - cc, BACK_B * CHUNK + SKEW - cc)
    idx = np.clip(dist, -REL_CLIP, REL_CLIP) + REL_CLIP
    m = np.zeros((SKEW, N_REL), np.float32)
    m[cc, idx] = 1.0
    return jnp.asarray(m)


def _toeplitz_bias(vec, carry=None):
    lk = (QG + BACK_B) * CHUNK
    nv = _n_variants(BACK_B)

    def body(v_ref, o_ref):
        xv = jnp.broadcast_to(v_ref[0], (QROWS, SKEW))
        row = lax.broadcasted_iota(jnp.int32, (QROWS, SKEW), 0)
        for bit in range(QROWS.bit_length() - 1):
            xv = jnp.where((row >> bit) & 1 == 1, pltpu.roll(xv, 1 << bit, 1), xv)
        ri = lax.broadcasted_iota(jnp.int32, (QROWS, lk), 0) // CHUNK
        col = lax.broadcasted_iota(jnp.int32, (QROWS, lk), 1)
        ci = col // CHUNK
        valid = (ci - ri >= 0) & (ci - ri <= BACK_B)
        for v in range(nv):
            o_ref[v, 0] = jnp.where(valid & (col >= (BACK_B - QG * v) * CHUNK), xv[:, :lk], NEG_INF)

    return _call(body, name="toeplitz_bias", grid=(H_B,),
                 in_specs=[pl.BlockSpec((1, 1, SKEW), lambda h: (h, 0, 0))],
                 out_specs=pl.BlockSpec((nv, 1, QROWS, lk), lambda h: (0, h, 0, 0)),
                 out_shape=jax.ShapeDtypeStruct((nv, H_B, QROWS, lk), F32), args=[vec], sem=("parallel",),
                 carry=carry)


def _diagonal_sums(dbias):
    lk = dbias.shape[2]

    def body(d_ref, o_ref):
        xp = jnp.concatenate([d_ref[0], jnp.zeros((QROWS, SKEW - lk), F32)], axis=1)
        xv = xp[0:CHUNK]
        for q in range(1, QG):
            xv = xv + pltpu.roll(xp[q * CHUNK:(q + 1) * CHUNK], SKEW - q * CHUNK, 1)
        row = lax.broadcasted_iota(jnp.int32, (CHUNK, SKEW), 0)
        for bit in range(CHUNK.bit_length() - 1):
            xv = jnp.where((row >> bit) & 1 == 1, pltpu.roll(xv, SKEW - (1 << bit), 1), xv)
        o_ref[0] = jnp.sum(xv, axis=0, keepdims=True)

    return _call(body, name="diagonal_sums", grid=(H_B,),
                 in_specs=[pl.BlockSpec((1, QROWS, lk), lambda h: (h, 0, 0))],
                 out_specs=pl.BlockSpec((1, 1, SKEW), lambda h: (h, 0, 0)),
                 out_shape=jax.ShapeDtypeStruct((H_B, 1, SKEW), F32), args=[dbias], sem=("parallel",))


def _attn_common(s, n_back, gqa, q_col, k_col, v_col, TPS):
    lk = (QG + n_back) * CHUNK
    pad = n_back * CHUNK
    wide = TPS * LANES
    q_spec = pl.BlockSpec((QROWS, wide), lambda t, g: (g, q_col // TPS + t))
    if gqa:
        k_spec = pl.BlockSpec((s, LANES), lambda t, g: (0, k_col))
        v_spec = pl.BlockSpec((s, LANES), lambda t, g: (0, v_col))
    else:
        k_spec = pl.BlockSpec((s, wide), lambda t, g: (0, k_col // TPS + t))
        v_spec = pl.BlockSpec((s, wide), lambda t, g: (0, v_col // TPS + t))
    last_variant = _n_variants(n_back) - 1
    bias_spec = pl.BlockSpec((None, 2 * TPS, QROWS, lk), lambda t, g: (jnp.minimum(g, last_variant), t, 0, 0))
    tile_spec = pl.BlockSpec((QROWS, wide), lambda t, g: (g, t))
    return lk, pad, q_spec, k_spec, v_spec, bias_spec, tile_spec


def _attention_fwd(proj, bias, sinks, *, n_back, gqa, q_col, k_col, v_col, TPS, name, carry=None):
    s = proj.shape[0]
    lk, pad, q_spec, k_spec, v_spec, bias_spec, tile_spec = _attn_common(s, n_back, gqa, q_col, k_col, v_col, TPS)
    n_t, n_g = 512 // (TPS * LANES), s // QROWS
    kv_wide = LANES if gqa else TPS * LANES

    def body(*refs):
        if gqa:
            q_ref, k_ref, v_ref, bias_ref, sink_ref, o_ref, l_ref, kpad, vpad = refs
        else:
            q_ref, k_ref, v_ref, bias_ref, o_ref, l_ref, kpad, vpad = refs
        t, g = pl.program_id(0), pl.program_id(1)

        @pl.when(g == 0)
        def _():
            kpad[0:pad, :] = jnp.zeros((pad, kv_wide), BF16)
            vpad[0:pad, :] = jnp.zeros((pad, kv_wide), BF16)
            kpad[pad:, :] = k_ref[...]
            vpad[pad:, :] = v_ref[...]

        start = pl.multiple_of(g * QROWS, QROWS)
        half = lax.broadcasted_iota(jnp.int32, (QROWS, LANES), 1) // HEAD_DIM
        for tt in range(TPS):
            lanes = slice(tt * LANES, (tt + 1) * LANES)
            kv_lanes = slice(0, LANES) if gqa else lanes
            kb = kpad[pl.ds(start, lk), kv_lanes]
            vb = vpad[pl.ds(start, lk), kv_lanes]
            q = q_ref[:, lanes] * (HEAD_DIM ** -0.5)
            if gqa:
                hk = (TPS * t + tt) // 2
                q_rolled = pltpu.roll(q.astype(F32), HEAD_DIM, 1).astype(BF16)
            outs, lses = [], []
            for e in range(2):
                if gqa:
                    kv_half = hk
                    src = jnp.where(hk == e, q, q_rolled)
                else:
                    kv_half = e
                    src = q
                qm = jnp.where(half == kv_half, src, jnp.zeros_like(src))
                sc = _dot_nt(qm, kb) + bias_ref[2 * tt + e]
                m = jnp.max(sc, axis=-1, keepdims=True)
                if gqa:
                    sk = sink_ref[2 * (TPS * t + tt) + e]
                    m = jnp.maximum(m, sk)
                p = jnp.exp(sc - m)
                l = jnp.sum(p, axis=-1, keepdims=True)
                if gqa:
                    l = l + jnp.exp(sk - m)
                pn = p / l
                outs.append(_dot(pn.astype(BF16), vb))
                lses.append(m + jnp.log(l))
            if gqa:
                same = jnp.where(hk == 0, outs[0], outs[1])
                other = jnp.where(hk == 0, outs[1], outs[0])
                o_ref[:, lanes] = jnp.where(half == hk, same, pltpu.roll(other, HEAD_DIM, 1))
            else:
                o_ref[:, lanes] = jnp.where(half == 0, outs[0], outs[1])
            l_ref[:, lanes] = jnp.where(half == 0, lses[0], lses[1])

    in_specs = [q_spec, k_spec, v_spec, bias_spec] + ([SMEM_SPEC] if gqa else [])
    args = [proj, proj, proj, bias] + ([sinks] if gqa else [])
    o_shape = jax.ShapeDtypeStruct((s, 512), F32)
    return _call(body, name=name, grid=(n_t, n_g), in_specs=in_specs, out_specs=(tile_spec, tile_spec),
                 out_shape=(o_shape, o_shape), args=args,
                 scratch=[pltpu.VMEM((s + pad, kv_wide), BF16), pltpu.VMEM((s + pad, kv_wide), BF16)],
                 sem=("arbitrary", "arbitrary"), carry=carry)


def _attention_bwd(proj, bias, sinks, do, lse, *, n_back, gqa, q_col, k_col, v_col, TPS, name, carry=None):
    s = proj.shape[0]
    lk, pad, q_spec, k_spec, v_spec, bias_spec, tile_spec = _attn_common(s, n_back, gqa, q_col, k_col, v_col, TPS)
    n_t, n_g = 512 // (TPS * LANES), s // QROWS
    kv_wide = LANES if gqa else TPS * LANES

    def body(*refs):
        if gqa:
            (q_ref, k_ref, v_ref, bias_ref, sink_ref, do_ref, l_ref,
             dq_ref, dk_ref, dv_ref, dsink_ref, kpad, vpad, dkpad, dvpad) = refs
        else:
            (q_ref, k_ref, v_ref, bias_ref, do_ref, l_ref,
             dq_ref, dk_ref, dv_ref, dbias_ref, kpad, vpad, dkpad, dvpad) = refs
        t, g = pl.program_id(0), pl.program_id(1)

        @pl.when(g == 0)
        def _():
            kpad[0:pad, :] = jnp.zeros((pad, kv_wide), BF16)
            vpad[0:pad, :] = jnp.zeros((pad, kv_wide), BF16)
            kpad[pad:, :] = k_ref[...]
            vpad[pad:, :] = v_ref[...]
            if gqa:
                dsink_ref[...] = jnp.zeros_like(dsink_ref)
            else:
                dbias_ref[...] = jnp.zeros_like(dbias_ref)

        @pl.when((g == 0) & (t == 0) if gqa else g == 0)
        def _():
            dkpad[...] = jnp.zeros_like(dkpad)
            dvpad[...] = jnp.zeros_like(dvpad)

        start = pl.multiple_of(g * QROWS, QROWS)
        half = lax.broadcasted_iota(jnp.int32, (QROWS, LANES), 1) // HEAD_DIM
        for tt in range(TPS):
            lanes = slice(tt * LANES, (tt + 1) * LANES)
            kv_lanes = slice(0, LANES) if gqa else lanes
            kb = kpad[pl.ds(start, lk), kv_lanes]
            vb = vpad[pl.ds(start, lk), kv_lanes]
            q = q_ref[:, lanes]
            dov = do_ref[:, lanes]
            lv = l_ref[:, lanes]
            if gqa:
                hk = (TPS * t + tt) // 2
                q_rolled = pltpu.roll(q.astype(F32), HEAD_DIM, 1).astype(BF16)
                do_rolled = pltpu.roll(dov, HEAD_DIM, 1)
            dqs = []
            dk_acc = jnp.zeros((lk, LANES), F32)
            dv_acc = jnp.zeros((lk, LANES), F32)
            for e in range(2):
                if gqa:
                    kv_half = hk
                    src = jnp.where(hk == e, q, q_rolled)
                    do_src = jnp.where(hk == e, dov, do_rolled)
                else:
                    kv_half = e
                    src = q
                    do_src = dov
                qm = jnp.where(half == kv_half, src, jnp.zeros_like(src))
                dom = jnp.where(half == kv_half, do_src, 0.0).astype(BF16)
                lcol = jnp.max(jnp.where(half == e, lv, -jnp.inf), axis=-1, keepdims=True)
                sc = _dot_nt(qm * (HEAD_DIM ** -0.5), kb) + bias_ref[2 * tt + e]
                pn = jnp.exp(sc - lcol)
                dp = _dot_nt(dom, vb)
                delta = jnp.sum(pn * dp, axis=-1, keepdims=True)
                ds = pn * (dp - delta)
                if gqa:
                    p_sink = jnp.exp(sink_ref[2 * (TPS * t + tt) + e] - lcol)
                    dsk = -jnp.sum(p_sink * delta, axis=0, keepdims=True)
                    row = 2 * tt + e
                    dsink_ref[0, row:row + 1, :] += jnp.broadcast_to(dsk, (1, LANES))
                else:
                    dbias_ref[2 * tt + e] += ds
                dsb = (ds * (HEAD_DIM ** -0.5)).astype(BF16)
                dqs.append(_dot(dsb, kb))
                dk_acc = dk_acc + _dot_tn(dsb, qm)
                dv_acc = dv_acc + _dot_tn(pn.astype(BF16), dom)
            dkpad[pl.ds(start, lk), kv_lanes] += dk_acc
            dvpad[pl.ds(start, lk), kv_lanes] += dv_acc
            if gqa:
                same = jnp.where(hk == 0, dqs[0], dqs[1])
                other = jnp.where(hk == 0, dqs[1], dqs[0])
                dq_ref[:, lanes] = jnp.where(half == hk, same, pltpu.roll(other, HEAD_DIM, 1)).astype(BF16)
            else:
                dq_ref[:, lanes] = jnp.where(half == 0, dqs[0], dqs[1]).astype(BF16)

        @pl.when((g == n_g - 1) & (t == n_t - 1) if gqa else g == n_g - 1)
        def _():
            dk_ref[...] = dkpad[pad:, :].astype(BF16)
            dv_ref[...] = dvpad[pad:, :].astype(BF16)

    in_specs = [q_spec, k_spec, v_spec, bias_spec] + ([SMEM_SPEC] if gqa else []) + [tile_spec, tile_spec]
    args = [proj, proj, proj, bias] + ([sinks] if gqa else []) + [do, lse]
    if gqa:
        kv_out = pl.BlockSpec((s, LANES), lambda t, g: (0, 0))
        kv_shape = jax.ShapeDtypeStruct((s, LANES), BF16)
        extra_spec = pl.BlockSpec((1, 8, LANES), lambda t, g: (t, 0, 0))
        extra_shape = jax.ShapeDtypeStruct((n_t, 8, LANES), F32)
    else:
        kv_out = pl.BlockSpec((s, kv_wide), lambda t, g: (0, t))
        kv_shape = jax.ShapeDtypeStruct((s, 512), BF16)
        extra_spec = pl.BlockSpec((2 * TPS, QROWS, lk), lambda t, g: (t, 0, 0))
        extra_shape = jax.ShapeDtypeStruct(bias.shape[1:], F32)
    return _call(body, name=name, grid=(n_t, n_g), in_specs=in_specs,
                 out_specs=(tile_spec, kv_out, kv_out, extra_spec),
                 out_shape=(jax.ShapeDtypeStruct((s, 512), BF16), kv_shape, kv_shape, extra_shape), args=args,
                 scratch=[pltpu.VMEM((s + pad, kv_wide), BF16), pltpu.VMEM((s + pad, kv_wide), BF16),
                          pltpu.VMEM((s + pad, kv_wide), F32), pltpu.VMEM((s + pad, kv_wide), F32)],
                 sem=("arbitrary", "arbitrary"), carry=carry)


def _sum_rows8(g):
    n = g.shape[2]

    def body(g_ref, o_ref):
        acc = g_ref[0]
        for j in range(1, N_DEV):
            acc = acc + g_ref[j]
        o_ref[...] = acc

    return pl.pallas_call(
        body, name="sum_small_grads", in_specs=[VMEM_SPEC], out_specs=VMEM_SPEC,
        out_shape=jax.ShapeDtypeStruct((1, n), F32), compiler_params=_params(),
    )(g)


def _ada_weight_grad(sc_t, dmod_cols):
    d = sc_t.shape[0]
    w = dmod_cols.shape[1]
    td = _pick(d, (256, 128))

    def body(sc_ref, dm_ref, o_ref):
        scv = sc_ref[...]
        dmv = dm_ref[...]
        acc = scv[:, 0:1] * dmv[0:1, :]
        for b in range(1, N_DEV):
            acc = acc + scv[:, b:b + 1] * dmv[b:b + 1, :]
        o_ref[...] = acc

    return _call(body, name="ada_weight_grad", grid=(d // td,),
                 in_specs=[pl.BlockSpec((td, N_DEV), lambda i: (i, 0)), pl.BlockSpec((N_DEV, w), lambda i: (0, 0))],
                 out_specs=pl.BlockSpec((td, w), lambda i: (i, 0)), out_shape=jax.ShapeDtypeStruct((d, w), F32),
                 args=[sc_t, dmod_cols], sem=("parallel",))


def _adamw_update(w, gv, m, v):
    nm = ADAM_B1 * m + (1.0 - ADAM_B1) * gv
    nv = ADAM_B2 * v + (1.0 - ADAM_B2) * (gv * gv)
    m_hat = nm / (1.0 - ADAM_B1 ** ADAM_STEP)
    v_hat = nv / (1.0 - ADAM_B2 ** ADAM_STEP)
    return -ADAM_LR * (m_hat / (jnp.sqrt(v_hat) + ADAM_EPS) + ADAM_WD * w), nm, nv


def _adamw(w, g, m, v, name):
    rows, cols = w.shape
    tr = _pick(rows, (256, 176, 128, 88, 64)) if rows > 256 else rows

    def body(w_ref, g_ref, m_ref, v_ref, d_ref, nm_ref, nv_ref):
        d_ref[...], nm_ref[...], nv_ref[...] = _adamw_update(w_ref[...], g_ref[...], m_ref[...], v_ref[...])

    spec = pl.BlockSpec((tr, cols), lambda i: (i, 0))
    shape = jax.ShapeDtypeStruct((rows, cols), F32)
    return _call(body, name=name, grid=(rows // tr,), in_specs=[spec] * 4, out_specs=(spec, spec, spec),
                 out_shape=(shape, shape, shape), args=[w, g, m, v], sem=("parallel",))


def _adamw_from_slots(w, own, slots, m, v, name):
    n_slots, rows, k = slots.shape

    def body(o_ref, s_ref, w_ref, m_ref, v_ref, g_ref, d_ref, nm_ref, nv_ref):
        gv = o_ref[...].astype(F32)
        for j in range(n_slots):
            gv = gv + s_ref[j].astype(F32)
        g_ref[...] = gv
        d_ref[...], nm_ref[...], nv_ref[...] = _adamw_update(w_ref[...], gv, m_ref[...], v_ref[...])

    tr = rows // 2 if rows % 32 == 0 else rows
    spec = pl.BlockSpec((tr, k), lambda i: (i, 0))
    shape = jax.ShapeDtypeStruct((rows, k), F32)
    return _call(body, name=name, grid=(rows // tr,),
                 in_specs=[spec, pl.BlockSpec((n_slots, tr, k), lambda i: (0, i, 0)), spec, spec, spec],
                 out_specs=(spec, spec, spec, spec), out_shape=(shape, shape, shape, shape),
                 args=[own, slots, w, m, v], sem=("parallel",))


def _adamw_small(g, w, m, v, sizes):
    n = w.shape[1]
    offs, off = [], 0
    for size in sizes:
        offs.append(off)
        off += size + (-size % LANES)

    def body(g_ref, w_ref, m_ref, v_ref, *out_refs):
        gv = g_ref[:, 0:n]
        dv, nm, nv = _adamw_update(w_ref[...], gv, m_ref[...], v_ref[...])
        for j, (o, size) in enumerate(zip(offs, sizes)):
            for k, val in enumerate((gv, dv, nm, nv)):
                out_refs[4 * j + k][...] = val[:, o:o + size]

    shapes = [jax.ShapeDtypeStruct((1, size), F32) for size in sizes for _ in range(4)]
    return pl.pallas_call(
        body, name="adamw_small", in_specs=[VMEM_SPEC] * 4, out_specs=tuple([VMEM_SPEC] * len(shapes)),
        out_shape=tuple(shapes), compiler_params=_params(),
    )(g, w, m, v)


SMALL = ("b_ada", "g_pre_ffn1", "g_post_ffn1", "g_pre_mix", "b_in", "sinks_a", "rel_bias_b", "g_grp_a",
         "g_grp_b", "b_out", "g_post_mix", "g_pre_ffn2", "g_post_ffn2")
WEIGHTS = ("w_ada", "b_ada", "g_pre_ffn1", "w_gate1", "w_up1", "w_down1", "g_post_ffn1", "g_pre_mix", "w_in",
           "b_in", "sinks_a", "rel_bias_b", "g_grp_a", "g_grp_b", "w_out", "b_out", "g_post_mix", "g_pre_ffn2",
           "w_gate2", "w_up2", "w_down2", "g_post_ffn2")


def kernel(x, c, w_ada, b_ada, g_pre_ffn1, w_gate1, w_up1, w_down1, g_post_ffn1, g_pre_mix, w_in, b_in, sinks_a, rel_bias_b, g_grp_a, g_grp_b, w_out, b_out, g_post_mix, g_pre_ffn2, w_gate2, w_up2, w_down2, g_post_ffn2, loss_target, m_w_ada, m_b_ada, m_g_pre_ffn1, m_w_gate1, m_w_up1, m_w_down1, m_g_post_ffn1, m_g_pre_mix, m_w_in, m_b_in, m_sinks_a, m_rel_bias_b, m_g_grp_a, m_g_grp_b, m_w_out, m_b_out, m_g_post_mix, m_g_pre_ffn2, m_w_gate2, m_w_up2, m_w_down2, m_g_post_ffn2, v_w_ada, v_b_ada, v_g_pre_ffn1, v_w_gate1, v_w_up1, v_w_down1, v_g_post_ffn1, v_g_pre_mix, v_w_in, v_b_in, v_sinks_a, v_rel_bias_b, v_g_grp_a, v_g_grp_b, v_w_out, v_b_out, v_g_post_mix, v_g_pre_ffn2, v_w_gate2, v_w_up2, v_w_down2, v_g_post_ffn2):
    given = dict(locals())
    weights = {n: given[n] for n in WEIGHTS}
    mom_m = {n: given["m_" + n] for n in WEIGHTS}
    mom_v = {n: given["v_" + n] for n in WEIGHTS}

    me = 4 * lax.axis_index("x") + 2 * lax.axis_index("y") + lax.axis_index("c")
    xs = x[0]
    tgt = loss_target[0]
    d_model = xs.shape[1]
    ada_cols = w_ada.shape[2]

    sh = {"wg1": w_gate1[0].T, "wu1": w_up1[0].T, "wd1": w_down1[0], "win": w_in[0].T, "wo": w_out[0],
          "wg2": w_gate2[0].T, "wu2": w_up2[0].T, "wd2": w_down2[0]}
    sh = {k: v.astype(BF16) for k, v in sh.items()}

    def gather(*names):
        return _gather_carry([sh[n] for n in names])

    bias_a = _alibi_bias()
    rel_m = _rel_index_matrix()
    rel_vec = jnp.dot(rel_bias_b[0], rel_m.T, precision=lax.Precision.HIGHEST)
    bias_b, (wg1, wu1) = _toeplitz_bias(rel_vec.reshape(H_B, 1, SKEW), carry=gather("wg1", "wu1"))

    b_cols = lax.dynamic_slice(b_ada, (0, me * ada_cols), (1, ada_cols))
    (sc_all, mod_rows), _ = _ada_forward(c, w_ada[0], b_cols, _Carry([], [], [], lambda *a: None, lambda *a: None))
    mod = mod_rows.reshape(N_MOD, d_model)
    shift1, scale1, gate1, shift2, scale2, gate2, shift3, scale3, gate3 = (mod[i:i + 1] for i in range(N_MOD))

    h1 = _pre_norm(xs, g_pre_ffn1, scale1, shift1, "pre_norm_ffn1")
    (a1, b1, u1), (wd1,) = _ffn_up(h1, wg1, wu1, "ffn_up_ffn1", carry=gather("wd1"))
    (y1, x1, h2), (win,) = _mm_nn(
        [(u1, wd1)], "ffn_down_ffn1", F32, carry=gather("win"),
        tail=_tail_post_pre(xs, g_post_ffn1, gate1, 0.5, g_pre_mix, scale2, shift2))

    proj, (wo,) = _mm_nt(h2, win, "in_proj", BF16, bias=b_in, carry=gather("wo"))
    sinks = sinks_a[0]
    cfg_a = dict(n_back=BACK_A, gqa=True, q_col=0, k_col=QA // LANES, v_col=(QA + KVA) // LANES, TPS=TPS_A)
    cfg_b = dict(n_back=BACK_B, gqa=False, q_col=(QA + 2 * KVA) // LANES, k_col=(QA + 2 * KVA + QB) // LANES,
                 v_col=(QA + 2 * KVA + 2 * QB) // LANES, TPS=TPS_B)
    (oa, lse_a), (wg2,) = _attention_fwd(proj, bias_a, sinks, name="attn_a", carry=gather("wg2"), **cfg_a)
    (ob, lse_b), (wu2,) = _attention_fwd(proj, bias_b, None, name="attn_b", carry=gather("wu2"), **cfg_b)
    ycat = _group_norm_cat(oa, ob, g_grp_a, g_grp_b)
    ymix, x2, h3 = _mm_nn([(ycat, wo)], "out_proj", F32, bias=b_out,
                          tail=_tail_post_pre(x1, g_post_mix, gate2, 1.0, g_pre_ffn2, scale3, shift3))

    (a3, b3, u3), (wd2,) = _ffn_up(h3, wg2, wu2, "ffn_up_ffn2", carry=gather("wd2"))

    flights, own = {}, {}

    def grad_pair(key, a_mat, b_mat, name):
        part, own[key] = _mm_tn_pair(a_mat, b_mat, name)
        return part

    def scatter_start(tag, after_vec, **parts):
        names = list(parts)
        sems, p_thru, lands, token = _scatter_start([parts[n] for n in names], "scatter_start_" + tag)
        flights[tag] = (names, sems, p_thru, lands)
        return after_vec + token[0:1, 0:1]

    dx3, dy, loss_part, s1 = _mm_nn([(u3, wd2)], "ffn_down_ffn2", None,
                                    tail=_tail_post_loss(x2, tgt, g_post_ffn2, gate3, 0.5))
    da, db = _ffn_down_bwd(dy, wd2, a3, b3, "ffn_down_bwd_ffn2")
    dwd2 = grad_pair("wd2", u3, dy, "grad_wd_ffn2")
    dwg2 = grad_pair("wg2", da, h3, "grad_wg_ffn2")
    dwu2 = grad_pair("wu2", db, h3, "grad_wu_ffn2")
    g_pre_tied = scatter_start("ffn2", g_pre_ffn2, wd2=dwd2, wg2=dwg2, wu2=dwu2)
    dx2, dymix, s2, s3, s1m, db_out = _mm_nn(
        [(da, wg2), (db, wu2)], "ffn_up_bwd_ffn2", None,
        tail=_tail_pre_post_bwd(x2, dx3, ymix, g_pre_tied, scale3, g_post_mix, gate2, 1.0))
    sm3 = dict(shift=s3, scale=s2 * g_pre_ffn2, gate=0.5 * g_post_ffn2 * s1,
               g_pre=(1.0 + scale3) * s2, g_post=(0.5 * gate3) * s1)

    dycat = _mm_nt(dymix, wo, "out_proj_bwd", F32)
    dwo = grad_pair("wo", ycat, dymix, "grad_wo")
    doa, dob, dg_a, dg_b = _group_norm_bwd(dycat, oa, ob, g_grp_a, g_grp_b)
    dqa, dka, dva, dsink = _attention_bwd(proj, bias_a, sinks, doa, lse_a, name="attn_a_bwd", **cfg_a)
    dqb, dkb, dvb, dbias = _attention_bwd(proj, bias_b, None, dob, lse_b, name="attn_b_bwd", **cfg_b)
    dproj = jnp.concatenate([dqa, dka, dva, dqb, dkb, dvb], axis=1)
    dwin, own["win"], db_in = _mm_tn_pair(dproj, h2, "grad_win", col_sums=True)
    g_pre_tied = scatter_start("mix", g_pre_mix, wo=dwo, win=dwin)
    dx1, dy, s2m, s3m, s1, _ = _mm_nn(
        [(dproj, win)], "in_proj_bwd", None,
        tail=_tail_pre_post_bwd(x1, dx2, y1, g_pre_tied, scale2, g_post_ffn1, gate1, 0.5))
    d_rel = jnp.dot(_diagonal_sums(dbias).reshape(H_B, SKEW), rel_m, precision=lax.Precision.HIGHEST)
    d_sinks = dsink[:, :2 * TPS_A, 0].reshape(1, H_A)

    da, db = _ffn_down_bwd(dy, wd1, a1, b1, "ffn_down_bwd_ffn1")
    dwd1 = grad_pair("wd1", u1, dy, "grad_wd_ffn1")
    dwg1 = grad_pair("wg1", da, h1, "grad_wg_ffn1")
    dwu1 = grad_pair("wu1", db, h1, "grad_wu_ffn1")
    g_pre_tied = scatter_start("ffn1", g_pre_ffn1, wd1=dwd1, wg1=dwg1, wu1=dwu1)
    dx0, s2, s3 = _mm_nn([(da, wg1), (db, wu1)], "ffn_up_bwd_ffn1", None,
                         tail=_tail_pre_bwd(xs, dx1, g_pre_tied, scale1))
    sm1 = dict(shift=s3, scale=s2 * g_pre_ffn1, gate=0.5 * g_post_ffn1 * s1,
               g_pre=(1.0 + scale1) * s2, g_post=(0.5 * gate1) * s1)

    dmod = jnp.concatenate([sm1["shift"], sm1["scale"], sm1["gate"],
                            s3m, s2m * g_pre_mix, g_post_mix * s1m,
                            sm3["shift"], sm3["scale"], sm3["gate"]], axis=1)
    small_parts = {
        "b_ada": dmod, "g_pre_ffn1": sm1["g_pre"], "g_post_ffn1": sm1["g_post"],
        "g_pre_mix": (1.0 + scale2) * s2m, "b_in": db_in, "sinks_a": d_sinks,
        "rel_bias_b": d_rel.reshape(1, H_B * N_REL), "g_grp_a": dg_a, "g_grp_b": dg_b, "b_out": db_out,
        "g_post_mix": gate2 * s1m, "g_pre_ffn2": sm3["g_pre"], "g_post_ffn2": sm3["g_post"]}
    sizes = [small_parts[n].shape[1] for n in SMALL]

    def pack(parts):
        cells = []
        for p in parts:
            cells.append(p)
            if p.shape[1] % LANES:
                cells.append(jnp.zeros((1, -p.shape[1] % LANES), F32))
        return jnp.concatenate(cells, axis=1)

    packed = pack([small_parts[n] for n in SMALL] + [loss_part])
    n_packed = packed.shape[1]
    small_sems, packed_thru, small_land, small_token = _small_gather_start(packed)

    out_g, out_d, out_m, out_v = {}, {}, {}, {}
    groups = (("ffn2", (("w_gate2", "wg2", True), ("w_up2", "wu2", True), ("w_down2", "wd2", False))),
              ("mix", (("w_in", "win", True), ("w_out", "wo", False))),
              ("ffn1", (("w_gate1", "wg1", True), ("w_up1", "wu1", True), ("w_down1", "wd1", False))))
    def small_update(after):
        packed_done, land = _small_gather_wait(small_sems, packed_thru, small_land, after)
        gathered = lax.dynamic_update_slice(land, packed_done[None], (me, 0, 0))
        small_sum = _sum_rows8(gathered)
        dmod_cols = lax.dynamic_slice(gathered.reshape(N_DEV, n_packed), (0, me * ada_cols), (N_DEV, ada_cols))
        g_ada = _ada_weight_grad(sc_all.reshape(N_DEV, d_model).T, dmod_cols)
        d_, m_, v_ = _adamw(w_ada[0], g_ada, m_w_ada[0], v_w_ada[0], "adamw_w_ada")
        out_g["w_ada"], out_d["w_ada"], out_m["w_ada"], out_v["w_ada"] = g_ada[None], d_[None], m_[None], v_[None]
        small_out = _adamw_small(small_sum, *(pack([tree[n].reshape(1, -1) for n in SMALL])
                                              for tree in (weights, mom_m, mom_v)), sizes)
        for j, n in enumerate(SMALL):
            shape = weights[n].shape
            out_g[n], out_d[n], out_m[n], out_v[n] = (t.reshape(shape) for t in small_out[4 * j:4 * j + 4])
        return small_sum[0, n_packed - LANES], v_

    after = small_token
    for tag, members in groups:
        if tag == groups[-1][0]:
            loss, after = small_update(after)
        names, sems, p_thru, lands = flights[tag]
        _, l_done = _scatter_wait(sems, p_thru, lands, after, "scatter_wait_" + tag)
        slots = dict(zip(names, l_done))
        for n, key, transposed in members:
            view = (lambda t: t.T) if transposed else (lambda t: t)
            res = _adamw_from_slots(view(weights[n][0]), own[key], slots[key], view(mom_m[n][0]),
                                    view(mom_v[n][0]), "adamw_" + n)
            out_g[n], out_d[n], out_m[n], out_v[n] = (view(t)[None] for t in res)
            after = res[3]

    return (loss, dx0[None], *[out_g[n] for n in WEIGHTS], *[out_d[n] for n in WEIGHTS],
            *[out_m[n] for n in WEIGHTS], *[out_v[n] for n in WEIGHTS])
```

```python
import numpy as np
import jax
import jax.numpy as jnp
from jax import lax
from jax.experimental import pallas as pl
from jax.experimental.pallas import tpu as pltpu

F32 = jnp.float32
BF16 = jnp.bfloat16
MESH = pl.DeviceIdType.MESH
ANY = pl.BlockSpec(memory_space=pl.ANY)
VMEM_SPEC = pl.BlockSpec(memory_space=pltpu.VMEM)
SMEM_SPEC = pl.BlockSpec(memory_space=pltpu.SMEM)

N_DEV = 8
CHUNK = 64
HEAD_DIM = 64
LANES = 128
H_A, KV_A, H_B = 8, 2, 8
BACK_A, BACK_B = 2, 8
REL_CLIP = 128
N_REL = 2 * REL_CLIP + 1
QA, KVA, QB = H_A * HEAD_DIM, KV_A * HEAD_DIM, H_B * HEAD_DIM
D_IN = QA + 2 * KVA + 3 * QB
N_MOD = 9
EPS = 1e-6
NEG_INF = -1e30
QG = 4
QROWS = QG * CHUNK
TPS_A, TPS_B = 4, 2
SKEW = 1024
ADAM_LR, ADAM_B1, ADAM_B2, ADAM_EPS, ADAM_WD, ADAM_STEP = 0.001, 0.9, 0.999, 1e-08, 0.01, 10
VMEM_LIMIT = 56 * 2 ** 20


def _pick(n, cands):
    for c in cands:
        if n % c == 0:
            return c
    return n


def _pieces(n, width=2 * LANES):
    return [(lo, min(lo + width, n)) for lo in range(0, n, width)]


def _params(sem=None):
    return pltpu.CompilerParams(dimension_semantics=sem, vmem_limit_bytes=VMEM_LIMIT)


def _dot_nt(a, b):
    return lax.dot_general(a, b, (((1,), (1,)), ((), ())), preferred_element_type=F32)


def _dot_tn(a, b):
    return lax.dot_general(a, b, (((0,), (0,)), ((), ())), preferred_element_type=F32)


def _dot(a, b):
    return jnp.dot(a, b, preferred_element_type=F32)


def _sigmoid(a):
    return 0.5 * (jnp.tanh(0.5 * a) + 1.0)


def _mesh_pos():
    return lax.axis_index("x"), lax.axis_index("y"), lax.axis_index("c")


def _peer(x, y, c, r):
    px = 1 - x if r & 4 else x
    py = 1 - y if r & 2 else y
    pc = 1 - c if r & 1 else c
    return px, py, pc


class _Carry:
    def __init__(self, ins, out_shapes, scratch, start, finish):
        self.ins, self.out_shapes, self.scratch = list(ins), list(out_shapes), list(scratch)
        self.start, self.finish = start, finish


def _call(body, *, name, grid, in_specs, out_specs, out_shape, args, scratch=(), sem=None, carry=None):
    single = not isinstance(out_shape, (tuple, list))
    out_specs = (out_specs,) if single else tuple(out_specs)
    out_shape = (out_shape,) if single else tuple(out_shape)
    if carry is None:
        res = pl.pallas_call(body, name=name, grid=grid, in_specs=list(in_specs), out_specs=out_specs,
                             out_shape=out_shape, scratch_shapes=list(scratch), compiler_params=_params(sem))(*args)
        return res[0] if single else res
    n_in, n_out, n_s = len(in_specs), len(out_shape), len(scratch)
    ci, co = len(carry.ins), len(carry.out_shapes)

    def wrapped(*refs):
        ins, cins = refs[:n_in], refs[n_in:n_in + ci]
        outs = refs[n_in + ci:n_in + ci + n_out]
        couts = refs[n_in + ci + n_out:n_in + ci + n_out + co]
        scr = refs[n_in + ci + n_out + co:n_in + ci + n_out + co + n_s]
        cscr = refs[n_in + ci + n_out + co + n_s:]
        first, last = None, None
        for ax, n in enumerate(grid):
            f, l = pl.program_id(ax) == 0, pl.program_id(ax) == n - 1
            first = f if first is None else first & f
            last = l if last is None else last & l
        pl.when(first)(lambda: carry.start(cins, couts, cscr))
        body(*ins, *outs, *scr)
        pl.when(last)(lambda: carry.finish(cins, couts, cscr))

    res = pl.pallas_call(
        wrapped, name=name, grid=grid, in_specs=list(in_specs) + [ANY] * ci, out_specs=out_specs + (ANY,) * co,
        out_shape=out_shape + tuple(carry.out_shapes), scratch_shapes=list(scratch) + carry.scratch,
        compiler_params=_params(("arbitrary",) * len(grid)))(*args, *carry.ins)
    main = res[:n_out]
    return (main[0] if single else main), res[n_out:]


PASS_PIECES = 4
ROW_TILE = 16


def _row_pieces(rows):
    tiles, pieces, off = rows // ROW_TILE, [], 0
    for i in range(PASS_PIECES):
        n = (tiles + i) // PASS_PIECES * ROW_TILE
        if n:
            pieces.append((off, n))
        off += n
    assert off == rows
    return pieces


def _gather_carry(shards):
    n_w = len(shards)
    rows = [s.shape[0] for s in shards]
    pieces = [_row_pieces(r) for r in rows]

    def plan(ins, outs, scr):
        send_sems, recv_sems, local_sems = scr
        x, y, c = _mesh_pos()
        me, sibling = (x, y, c), (x, y, 1 - c)
        chips = [(1 - x, y), (x, 1 - y), (1 - x, 1 - y)]

        def block(w, dev, span=None):
            off, n = (0, rows[w]) if span is None else span
            start = pl.multiple_of((4 * dev[0] + 2 * dev[1] + dev[2]) * rows[w] + off, 16)
            return outs[w].at[pl.ds(start, n), :]

        def copy(w, k, dev, to, src=None, span=None, p=0):
            return pltpu.make_async_remote_copy(
                src_ref=block(w, dev, span) if src is None else src, dst_ref=block(w, dev, span),
                send_sem=send_sems.at[w, k, p], recv_sem=recv_sems.at[w, k, p], device_id=to, device_id_type=MESH)

        mine = [pltpu.make_async_copy(ins[w], block(w, me), local_sems.at[w]) for w in range(n_w)]
        first = []
        for j, chip in enumerate(chips):
            first += [copy(w, 1 + j, me, (*chip, c), src=ins[w]) for w in range(n_w)]
        first += [copy(w, 0, me, sibling, src=ins[w]) for w in range(n_w)]
        return c, me, sibling, chips, copy, mine, first

    def start(ins, outs, scr):
        _, _, _, _, _, mine, first = plan(ins, outs, scr)
        for cp in mine + first:
            cp.start()

    def finish(ins, outs, scr):
        c, me, sibling, chips, copy, mine, first = plan(ins, outs, scr)
        passed = []
        for j, chip in enumerate(chips):
            for w in range(n_w):
                copy(w, 1 + j, (*chip, c), me).wait_recv()
                for p, span in enumerate(pieces[w]):
                    cp = copy(w, 4 + j, (*chip, c), sibling, span=span, p=p)
                    cp.start()
                    passed.append(cp)
        for w in range(n_w):
            copy(w, 0, sibling, me).wait_recv()
        for j, chip in enumerate(chips):
            for w in range(n_w):
                for p, span in enumerate(pieces[w]):
                    copy(w, 4 + j, (*chip, 1 - c), me, span=span, p=p).wait_recv()
        for cp in first + passed:
            cp.wait_send()
        for cp in mine:
            cp.wait()

    return _Carry(
        shards, [jax.ShapeDtypeStruct((N_DEV * s.shape[0], s.shape[1]), s.dtype) for s in shards],
        [pltpu.SemaphoreType.DMA((n_w, N_DEV - 1, PASS_PIECES)), pltpu.SemaphoreType.DMA((n_w, N_DEV - 1, PASS_PIECES)),
         pltpu.SemaphoreType.DMA((n_w,))], start, finish)


HBM_SPEC = pl.BlockSpec(memory_space=pltpu.HBM)
SEM_SPEC = pl.BlockSpec(memory_space=pltpu.SEMAPHORE)
N_CHIP = N_DEV // 2


def _scatter_copy(part_ref, land_ref, send_sem, recv_sem, r, rows):
    x, y, c = _mesh_pos()
    px, py, _ = _peer(x, y, c, 2 * r)
    src = part_ref.at[pl.ds(pl.multiple_of((2 * px + py) * rows, 16), rows), :]
    return pltpu.make_async_remote_copy(
        src_ref=src, dst_ref=land_ref.at[r - 1], send_sem=send_sem, recv_sem=recv_sem,
        device_id=(px, py, c), device_id_type=MESH)


def _scatter_order(n_w):
    return [(w, r) for r in (3, 2, 1) for w in range(n_w)]


def _scatter_start(parts, name):
    n_w = len(parts)
    rows = [p.shape[0] // N_CHIP for p in parts]
    order = _scatter_order(n_w)
    lands = [pltpu.with_memory_space_constraint(lax.empty((N_CHIP - 1, r, p.shape[1]), p.dtype), pltpu.HBM)
             for r, p in zip(rows, parts)]

    def body(*refs):
        part_refs, land_refs = refs[:n_w], refs[n_w:2 * n_w]
        sems = refs[2 * n_w:2 * n_w + 2 * len(order)]
        token = refs[-1]
        for j, (w, r) in enumerate(order):
            _scatter_copy(part_refs[w], land_refs[w], sems[2 * j], sems[2 * j + 1], r, rows[w]).start()
        token[...] = jnp.zeros_like(token)

    n_sem = 2 * len(order)
    res = pl.pallas_call(
        body, name=name,
        out_shape=(*[pltpu.SemaphoreType.DMA(())] * n_sem, *[pltpu.HBM(p.shape, p.dtype) for p in parts],
                   *[pltpu.HBM(l.shape, l.dtype) for l in lands], jax.ShapeDtypeStruct((8, LANES), F32)),
        in_specs=[HBM_SPEC] * (2 * n_w), out_specs=(*[SEM_SPEC] * n_sem, *[HBM_SPEC] * (2 * n_w), VMEM_SPEC),
        input_output_aliases={i: n_sem + i for i in range(2 * n_w)},
        compiler_params=pltpu.CompilerParams(has_side_effects=pltpu.SideEffectType.DATAFLOW_SIDE_EFFECTING),
    )(*[pltpu.with_memory_space_constraint(p, pltpu.HBM) for p in parts], *lands)
    return (list(res[:n_sem]), list(res[n_sem:n_sem + n_w]), list(res[n_sem + n_w:n_sem + 2 * n_w]), res[-1])


def _scatter_wait(sems, parts, lands, after, name):
    n_w = len(parts)
    rows = [p.shape[0] // N_CHIP for p in parts]
    order = _scatter_order(n_w)

    def body(*refs):
        part_refs, land_refs = refs[:n_w], refs[n_w:2 * n_w]
        sem_refs = refs[2 * n_w:2 * n_w + 2 * len(order)]
        for j, (w, r) in enumerate(order):
            cp = _scatter_copy(part_refs[w], land_refs[w], sem_refs[2 * j], sem_refs[2 * j + 1], r, rows[w])
            cp.wait_send()
            cp.wait_recv()

    res = pl.pallas_call(
        body, name=name,
        out_shape=(*[pltpu.HBM(p.shape, p.dtype) for p in parts], *[pltpu.HBM(l.shape, l.dtype) for l in lands]),
        in_specs=[HBM_SPEC] * (2 * n_w) + [SEM_SPEC] * len(sems) + [ANY],
        out_specs=tuple([HBM_SPEC] * (2 * n_w)),
        input_output_aliases={i: i for i in range(2 * n_w)},
        compiler_params=pltpu.CompilerParams(has_side_effects=pltpu.SideEffectType.DATAFLOW_SIDE_EFFECTING),
    )(*parts, *lands, *sems, after)
    return list(res[:n_w]), list(res[n_w:])


def _small_copy(v_ref, land_ref, send_sem, recv_sem, r):
    x, y, c = _mesh_pos()
    px, py, pc = _peer(x, y, c, r)
    return pltpu.make_async_remote_copy(
        src_ref=v_ref, dst_ref=land_ref.at[4 * x + 2 * y + c], send_sem=send_sem, recv_sem=recv_sem,
        device_id=(px, py, pc), device_id_type=MESH)


def _small_gather_start(v):
    land = pltpu.with_memory_space_constraint(lax.empty((N_DEV,) + v.shape, v.dtype), pltpu.HBM)

    def body(v_ref, land_ref, *rest):
        sems, token = rest[:2 * (N_DEV - 1)], rest[-1]
        for r in range(1, N_DEV):
            _small_copy(v_ref, land_ref, sems[2 * r - 2], sems[2 * r - 1], r).start()
        token[...] = jnp.zeros_like(token)

    n_sem = 2 * (N_DEV - 1)
    res = pl.pallas_call(
        body, name="small_gather_start",
        out_shape=(*[pltpu.SemaphoreType.DMA(())] * n_sem, pltpu.HBM(v.shape, v.dtype),
                   pltpu.HBM(land.shape, land.dtype), jax.ShapeDtypeStruct((8, LANES), F32)),
        in_specs=[HBM_SPEC, HBM_SPEC], out_specs=(*[SEM_SPEC] * n_sem, HBM_SPEC, HBM_SPEC, VMEM_SPEC),
        input_output_aliases={0: n_sem, 1: n_sem + 1},
        compiler_params=pltpu.CompilerParams(has_side_effects=pltpu.SideEffectType.DATAFLOW_SIDE_EFFECTING),
    )(pltpu.with_memory_space_constraint(v, pltpu.HBM), land)
    return list(res[:n_sem]), res[n_sem], res[n_sem + 1], res[-1]


def _small_gather_wait(sems, v, land, after):
    def body(v_ref, land_ref, *rest):
        for r in range(1, N_DEV):
            cp = _small_copy(v_ref, land_ref, rest[2 * r - 2], rest[2 * r - 1], r)
            cp.wait_send()
            x, y, c = _mesh_pos()
            px, py, pc = _peer(x, y, c, r)
            pltpu.make_async_remote_copy(
                src_ref=v_ref, dst_ref=land_ref.at[4 * px + 2 * py + pc], send_sem=rest[2 * r - 2],
                recv_sem=rest[2 * r - 1], device_id=(px, py, pc), device_id_type=MESH).wait_recv()

    res = pl.pallas_call(
        body, name="small_gather_wait",
        out_shape=(pltpu.HBM(v.shape, v.dtype), pltpu.HBM(land.shape, land.dtype)),
        in_specs=[HBM_SPEC, HBM_SPEC] + [SEM_SPEC] * len(sems) + [ANY], out_specs=(HBM_SPEC, HBM_SPEC),
        input_output_aliases={0: 0, 1: 1},
        compiler_params=pltpu.CompilerParams(has_side_effects=pltpu.SideEffectType.DATAFLOW_SIDE_EFFECTING),
    )(v, land, *sems, after)
    return res[0], res[1]


def _ada_forward(c_row, w_ada, b_cols, carry):
    d = c_row.shape[1]
    wcols = w_ada.shape[1]
    ci, co = len(carry.ins), len(carry.out_shapes)

    def body(*refs):
        c_ref, w_ref, b_ref = refs[:3]
        cins = refs[3:3 + ci]
        sc_ref, mod_ref = refs[3 + ci:5 + ci]
        couts = refs[5 + ci:5 + ci + co]
        rows_ref, send_sems, recv_sems = refs[5 + ci + co:8 + ci + co]
        cscr = refs[8 + ci + co:]
        carry.start(cins, couts, cscr)
        x, y, c = _mesh_pos()
        me = 4 * x + 2 * y + c
        cv = c_ref[...]
        sc_ref[me] = cv * _sigmoid(cv)

        sends = []
        for r in range(1, N_DEV):
            px, py, pc = _peer(x, y, c, r)
            cp = pltpu.make_async_remote_copy(
                src_ref=sc_ref.at[me], dst_ref=sc_ref.at[me], send_sem=send_sems.at[0, r - 1],
                recv_sem=recv_sems.at[0, r - 1], device_id=(px, py, pc), device_id_type=MESH)
            cp.start()
            sends.append(cp)
        for r in range(1, N_DEV):
            px, py, pc = _peer(x, y, c, r)
            pid = 4 * px + 2 * py + pc
            pltpu.make_async_remote_copy(
                src_ref=sc_ref.at[pid], dst_ref=sc_ref.at[pid], send_sem=send_sems.at[0, r - 1],
                recv_sem=recv_sems.at[0, r - 1], device_id=(px, py, pc), device_id_type=MESH).wait_recv()
        for cp in sends:
            cp.wait_send()

        sc_all = jnp.concatenate([sc_ref[j] for j in range(N_DEV)], axis=0)
        rows = _dot(sc_all.astype(BF16), w_ref[...].astype(BF16)) + b_ref[...]
        for j in range(N_DEV):
            rows_ref[j] = rows[j:j + 1, :]
        mod_ref[me] = rows_ref[me]

        sends = []
        for r in range(1, N_DEV):
            px, py, pc = _peer(x, y, c, r)
            pid = 4 * px + 2 * py + pc
            cp = pltpu.make_async_remote_copy(
                src_ref=rows_ref.at[pid], dst_ref=mod_ref.at[me], send_sem=send_sems.at[1, r - 1],
                recv_sem=recv_sems.at[1, r - 1], device_id=(px, py, pc), device_id_type=MESH)
            cp.start()
            sends.append(cp)
        for r in range(1, N_DEV):
            px, py, pc = _peer(x, y, c, r)
            pid = 4 * px + 2 * py + pc
            pltpu.make_async_remote_copy(
                src_ref=rows_ref.at[pid], dst_ref=mod_ref.at[pid], send_sem=send_sems.at[1, r - 1],
                recv_sem=recv_sems.at[1, r - 1], device_id=(px, py, pc), device_id_type=MESH).wait_recv()
        for cp in sends:
            cp.wait_send()
        carry.finish(cins, couts, cscr)

    res = pl.pallas_call(
        body, name="ada_forward",
        out_shape=(jax.ShapeDtypeStruct((N_DEV, 1, d), F32), jax.ShapeDtypeStruct((N_DEV, 1, wcols), F32),
                   *carry.out_shapes),
        in_specs=[VMEM_SPEC, VMEM_SPEC, VMEM_SPEC] + [ANY] * ci, out_specs=(VMEM_SPEC, VMEM_SPEC) + (ANY,) * co,
        scratch_shapes=[pltpu.VMEM((N_DEV, 1, wcols), F32), pltpu.SemaphoreType.DMA((2, N_DEV - 1)),
                        pltpu.SemaphoreType.DMA((2, N_DEV - 1))] + carry.scratch,
        compiler_params=_params(),
    )(c_row, w_ada, b_cols, *carry.ins)
    return res[:2], res[2:]


def _mm_nt(a, b, name, out_dtype, bias=None, carry=None):
    m, k = a.shape
    n = b.shape[0]
    tm = _pick(m, (512, 256, 128))
    tn = _pick(n, (1408, 1152, 1024, 768, 512, 256, 128))

    def body(*refs):
        acc = _dot_nt(refs[0][...], refs[1][...])
        if bias is not None:
            acc = acc + refs[2][...]
        refs[-1][...] = acc.astype(out_dtype)

    in_specs = [pl.BlockSpec((tm, k), lambda j, i: (i, 0)), pl.BlockSpec((tn, k), lambda j, i: (j, 0))]
    args = [a, b]
    if bias is not None:
        in_specs.append(pl.BlockSpec((1, tn), lambda j, i: (0, j)))
        args.append(bias)
    return _call(body, name=name, grid=(n // tn, m // tm), in_specs=in_specs,
                 out_specs=pl.BlockSpec((tm, tn), lambda j, i: (i, j)),
                 out_shape=jax.ShapeDtypeStruct((m, n), out_dtype), args=args,
                 sem=("parallel", "parallel"), carry=carry)


class _Tail:
    def __init__(self, rows, vecs, outs, fn):
        self.rows, self.vecs, self.outs, self.fn = list(rows), list(vecs), list(outs), fn


def _mm_nn(pairs, name, out_dtype, bias=None, carry=None, tail=None):
    m, k = pairs[0][0].shape
    n = pairs[0][1].shape[1]
    n_p = len(pairs)
    tm = _pick(m, (512, 256, 128))
    tk = k if n_p == 1 else _pick(k, (1408, 1152, 1024, 768, 512, 256, 128))
    nk = k // tk
    n_b = 0 if bias is None else 1
    n_r, n_v = (len(tail.rows), len(tail.vecs)) if tail else (0, 0)
    n_in = 2 * n_p + n_b + n_r + n_v
    n_main = 0 if out_dtype is None else 1

    def finish(acc, refs, first_tile):
        if bias is not None:
            acc = acc + refs[2 * n_p][...]
        outs = refs[n_in:-1]
        if n_main:
            outs[0][...] = acc.astype(out_dtype)
        if tail is None:
            return
        rows = [r[...] for r in refs[2 * n_p + n_b:2 * n_p + n_b + n_r]]
        vecs = [v[...] for v in refs[2 * n_p + n_b + n_r:n_in]]
        vals = tail.fn(acc, rows, vecs)
        for ref, val, (dtype, kind) in zip(outs[n_main:], vals, tail.outs):
            if kind == "row":
                ref[...] = val.astype(dtype)
            else:
                @pl.when(first_tile)
                def _(ref=ref):
                    ref[...] = jnp.zeros_like(ref)

                ref[...] += val

    def body(*refs):
        acc_ref = refs[-1]
        kk, i = pl.program_id(0), pl.program_id(1)
        part = _dot(refs[0][...], refs[1][...])
        for p in range(1, n_p):
            part = part + _dot(refs[2 * p][...], refs[2 * p + 1][...])
        if nk == 1:
            finish(part, refs, i == 0)
            return
        rows = pl.ds(pl.multiple_of(i * tm, tm), tm)

        @pl.when(kk == 0)
        def _():
            acc_ref[rows, :] = part

        if nk > 2:
            @pl.when((kk > 0) & (kk < nk - 1))
            def _():
                acc_ref[rows, :] += part

        @pl.when(kk == nk - 1)
        def _():
            finish(acc_ref[rows, :] + part, refs, i == 0)

    def last_only(kk, i):
        return (jnp.where(kk == nk - 1, i, 0), 0)

    row_spec = pl.BlockSpec((tm, n), last_only)
    vec_spec = pl.BlockSpec((1, n), lambda kk, i: (0, 0))
    in_specs, args = [], []
    for a, b in pairs:
        in_specs += [pl.BlockSpec((tm, tk), lambda kk, i: (i, kk)), pl.BlockSpec((tk, n), lambda kk, i: (kk, 0))]
        args += [a, b]
    if bias is not None:
        in_specs.append(vec_spec)
        args.append(bias)
    out_specs = [row_spec] * n_main
    out_shape = [jax.ShapeDtypeStruct((m, n), out_dtype)] if n_main else []
    if tail:
        in_specs += [row_spec] * n_r + [vec_spec] * n_v
        args += tail.rows + tail.vecs
        for dtype, kind in tail.outs:
            if kind == "row":
                out_specs.append(row_spec)
                out_shape.append(jax.ShapeDtypeStruct((m, n), dtype))
            else:
                width = n if kind == "sum" else 1
                out_specs.append(pl.BlockSpec((1, width), lambda kk, i: (0, 0)))
                out_shape.append(jax.ShapeDtypeStruct((1, width), dtype))
    if tail is None:
        out_specs, out_shape = out_specs[0], out_shape[0]
    return _call(body, name=name, grid=(nk, m // tm), in_specs=in_specs, out_specs=out_specs,
                 out_shape=out_shape, args=args,
                 scratch=[pltpu.VMEM((m, n) if nk > 1 else (8, LANES), F32)],
                 sem=("arbitrary", "arbitrary"), carry=carry)


def _rms(v):
    return lax.rsqrt(jnp.mean(v * v, axis=-1, keepdims=True) + EPS)


def _col(v):
    return jnp.sum(v, axis=0, keepdims=True)


def _tail_post_pre(x, g_post, gate, weight, g_pre, scale, shift):
    def fn(y, rows, vecs):
        (xv,), (gp, gt, g, sc, sh) = rows, vecs
        xo = xv + (weight * gt) * ((y * _rms(y)) * gp)
        return xo, ((xo * _rms(xo)) * g) * (1.0 + sc) + sh

    return _Tail([x], [g_post, gate, g_pre, scale, shift], [(F32, "row"), (BF16, "row")], fn)


def _tail_post_loss(x, target, g, gate, weight):
    def fn(y, rows, vecs):
        (xv, tv), (gv, gt) = rows, vecs
        r = _rms(y)
        yn = y * r
        err = (xv + (weight * gt) * (yn * gv)) - tv
        do = err * (1.0 / y.shape[1])
        dyn = do * ((weight * gt) * gv)
        dy = r * (dyn - yn * jnp.mean(dyn * yn, axis=-1, keepdims=True))
        return do, dy, 0.5 * _col(jnp.mean(err * err, axis=-1, keepdims=True)), _col(do * yn)

    return _Tail([x, target], [g, gate], [(F32, "row"), (BF16, "row"), (F32, "one"), (F32, "sum")], fn)


def _tail_pre_bwd(x, dres, g_pre, scale):
    def fn(dh, rows, vecs):
        (xv, dr), (g, sc) = rows, vecs
        r = _rms(xv)
        n = xv * r
        dn = dh * (g * (1.0 + sc))
        return dr + r * (dn - n * jnp.mean(dn * n, axis=-1, keepdims=True)), _col(dh * n), _col(dh)

    return _Tail([x, dres], [g_pre, scale], [(F32, "row"), (F32, "sum"), (F32, "sum")], fn)


def _tail_pre_post_bwd(x, dres, y, g_pre, scale, g_post, gate, weight):
    def fn(dh, rows, vecs):
        (xv, dr, yv), (g, sc, gp, gt) = rows, vecs
        r = _rms(xv)
        n = xv * r
        dn = dh * (g * (1.0 + sc))
        dx = dr + r * (dn - n * jnp.mean(dn * n, axis=-1, keepdims=True))
        ry = _rms(yv)
        yn = yv * ry
        dyn = dx * ((weight * gt) * gp)
        dy = ry * (dyn - yn * jnp.mean(dyn * yn, axis=-1, keepdims=True))
        return dx, dy, _col(dh * n), _col(dh), _col(dx * yn), _col(dy)

    return _Tail([x, dres, y], [g_pre, scale, g_post, gate],
                 [(F32, "row"), (BF16, "row")] + [(F32, "sum")] * 4, fn)


def _mm_tn_pair(a, b, name, col_sums=False):
    k, m = a.shape
    n = b.shape[1]
    rows = m // N_DEV
    n_chip = N_DEV // 2
    tm = 4 * rows
    tk = _pick(k, (1024, 512, 256, 128))
    nk = k // tk

    def body(a_ref, b_ref, p_ref, own_ref, *rest):
        acc_ref, keep_ref, send_ref, land_ref, send_sems, recv_sems = rest[-6:]
        i, kk = pl.program_id(0), pl.program_id(1)
        x, y, c = _mesh_pos()
        if col_sums:
            cs_ref = rest[0]
            part = jnp.sum(a_ref[...].astype(F32), axis=0, keepdims=True)

            @pl.when(kk == 0)
            def _():
                cs_ref[...] = part

            @pl.when(kk > 0)
            def _():
                cs_ref[...] += part

        def push(chip):
            return pltpu.make_async_remote_copy(
                src_ref=send_ref.at[chip], dst_ref=land_ref.at[chip], send_sem=send_sems.at[chip],
                recv_sem=recv_sems.at[chip], device_id=(x, y, 1 - c), device_id_type=MESH)

        if nk == 1:
            acc = _dot_tn(a_ref[...], b_ref[...])
        else:
            @pl.when(kk == 0)
            def _():
                acc_ref[...] = jnp.zeros_like(acc_ref)

            acc_ref[...] += _dot_tn(a_ref[...], b_ref[...])
            acc = acc_ref

        for t in range(2):
            @pl.when((kk == nk - 1) & (i == t))
            def _(t=t):
                for ob in range(4):
                    chip, core = 2 * t + ob // 2, ob % 2
                    blk = acc[ob * rows:(ob + 1) * rows, :]

                    @pl.when(c == core)
                    def _(chip=chip, blk=blk):
                        keep_ref[chip] = blk

                    @pl.when(c != core)
                    def _(chip=chip, blk=blk):
                        send_ref[chip] = blk.astype(BF16)
                        push(chip).start()

        @pl.when((kk == nk - 1) & (i == 1))
        def _():
            for chip in range(n_chip):
                push(chip).wait_recv()
                val = (keep_ref[chip] + land_ref[chip].astype(F32)).astype(BF16)
                p_ref[chip * rows:(chip + 1) * rows, :] = val

                @pl.when(2 * x + y == chip)
                def _(val=val):
                    own_ref[...] = val

            for chip in range(n_chip):
                push(chip).wait_send()

    out_specs = [pl.BlockSpec((n_chip * rows, n), lambda i, kk: (0, 0)), pl.BlockSpec((rows, n), lambda i, kk: (0, 0))]
    out_shape = [jax.ShapeDtypeStruct((n_chip * rows, n), BF16), jax.ShapeDtypeStruct((rows, n), BF16)]
    if col_sums:
        out_specs.append(pl.BlockSpec((1, tm), lambda i, kk: (0, i)))
        out_shape.append(jax.ShapeDtypeStruct((1, m), F32))
    return _call(body, name=name, grid=(2, nk),
                 in_specs=[pl.BlockSpec((tk, tm), lambda i, kk: (kk, i)), pl.BlockSpec((tk, n), lambda i, kk: (kk, 0))],
                 out_specs=out_specs, out_shape=out_shape, args=[a, b],
                 scratch=[pltpu.VMEM((tm, n) if nk > 1 else (8, LANES), F32), pltpu.VMEM((n_chip, rows, n), F32),
                          pltpu.VMEM((n_chip, rows, n), BF16), pltpu.VMEM((n_chip, rows, n), BF16),
                          pltpu.SemaphoreType.DMA((n_chip,)), pltpu.SemaphoreType.DMA((n_chip,))],
                 sem=("arbitrary", "arbitrary"))


def _ffn_up(h, wg_t, wu_t, name, carry=None):
    s, d = h.shape
    f = wg_t.shape[0]
    tm = _pick(s, (512, 256, 128))
    tf = _pick(f, (1408, 1024, 512, 256, 128))

    def body(h_ref, wg_ref, wu_ref, a_ref, b_ref, u_ref):
        hh = h_ref[...]
        for lo, hi in _pieces(tf):
            a = _dot_nt(hh, wg_ref[lo:hi, :])
            b = _dot_nt(hh, wu_ref[lo:hi, :])
            a_ref[:, lo:hi] = a.astype(BF16)
            b_ref[:, lo:hi] = b.astype(BF16)
            u_ref[:, lo:hi] = ((a * _sigmoid(a)) * b).astype(BF16)

    w_spec = pl.BlockSpec((tf, d), lambda j, i: (j, 0))
    o_spec = pl.BlockSpec((tm, tf), lambda j, i: (i, j))
    o_shape = jax.ShapeDtypeStruct((s, f), BF16)
    return _call(body, name=name, grid=(f // tf, s // tm),
                 in_specs=[pl.BlockSpec((tm, d), lambda j, i: (i, 0)), w_spec, w_spec],
                 out_specs=(o_spec, o_spec, o_spec), out_shape=(o_shape, o_shape, o_shape),
                 args=[h, wg_t, wu_t], sem=("parallel", "parallel"), carry=carry)


def _ffn_down_bwd(dy, wd, a, b, name, carry=None):
    s, d = dy.shape
    f = wd.shape[0]
    tm = _pick(s, (512, 256, 128))
    tf = _pick(f, (1408, 1024, 512, 256, 128))

    def body(dy_ref, wd_ref, a_ref, b_ref, da_ref, db_ref):
        dyv = dy_ref[...]
        for lo, hi in _pieces(tf):
            du = _dot_nt(dyv, wd_ref[lo:hi, :])
            a = a_ref[:, lo:hi].astype(F32)
            b = b_ref[:, lo:hi].astype(F32)
            sig = _sigmoid(a)
            da_ref[:, lo:hi] = (du * b * (sig * (1.0 + a * (1.0 - sig)))).astype(BF16)
            db_ref[:, lo:hi] = (du * (a * sig)).astype(BF16)

    t_spec = pl.BlockSpec((tm, tf), lambda j, i: (i, j))
    o_shape = jax.ShapeDtypeStruct((s, f), BF16)
    return _call(body, name=name, grid=(f // tf, s // tm),
                 in_specs=[pl.BlockSpec((tm, d), lambda j, i: (i, 0)), pl.BlockSpec((tf, d), lambda j, i: (j, 0)),
                           t_spec, t_spec],
                 out_specs=(t_spec, t_spec), out_shape=(o_shape, o_shape), args=[dy, wd, a, b],
                 sem=("parallel", "parallel"), carry=carry)


def _row_tile(s):
    return _pick(s, (256, 128, 64))


def _vec_spec(d):
    return pl.BlockSpec((1, d), lambda i: (0, 0))


def _pre_norm(x, g, scale, shift, name):
    s, d = x.shape
    ts = _row_tile(s)

    def body(x_ref, g_ref, sc_ref, sh_ref, h_ref):
        xv = x_ref[...]
        r = lax.rsqrt(jnp.mean(xv * xv, axis=-1, keepdims=True) + EPS)
        h_ref[...] = (((xv * r) * g_ref[...]) * (1.0 + sc_ref[...]) + sh_ref[...]).astype(BF16)

    row = pl.BlockSpec((ts, d), lambda i: (i, 0))
    return _call(body, name=name, grid=(s // ts,), in_specs=[row, _vec_spec(d), _vec_spec(d), _vec_spec(d)],
                 out_specs=row, out_shape=jax.ShapeDtypeStruct((s, d), BF16), args=[x, g, scale, shift],
                 sem=("parallel",))


def _group_norm_cat(oa, ob, ga, gb):
    s = oa.shape[0]
    ts = _row_tile(s)

    def body(oa_ref, ob_ref, ga_ref, gb_ref, y_ref):
        for o_ref, g_ref, lo, w in ((oa_ref, ga_ref, 0, QA), (ob_ref, gb_ref, QA, QB)):
            ov = o_ref[...]
            r = lax.rsqrt(jnp.mean(ov * ov, axis=-1, keepdims=True) + EPS)
            y_ref[:, lo:lo + w] = ((ov * r) * g_ref[...]).astype(BF16)

    return _call(body, name="group_norm_cat", grid=(s // ts,),
                 in_specs=[pl.BlockSpec((ts, QA), lambda i: (i, 0)), pl.BlockSpec((ts, QB), lambda i: (i, 0)),
                           _vec_spec(QA), _vec_spec(QB)],
                 out_specs=pl.BlockSpec((ts, QA + QB), lambda i: (i, 0)),
                 out_shape=jax.ShapeDtypeStruct((s, QA + QB), BF16), args=[oa, ob, ga, gb], sem=("parallel",))


def _group_norm_bwd(dy, oa, ob, ga, gb):
    s = oa.shape[0]
    ts = _row_tile(s)

    def body(dy_ref, oa_ref, ob_ref, ga_ref, gb_ref, doa_ref, dob_ref, dga_ref, dgb_ref):
        @pl.when(pl.program_id(0) == 0)
        def _():
            dga_ref[...] = jnp.zeros_like(dga_ref)
            dgb_ref[...] = jnp.zeros_like(dgb_ref)

        for o_ref, g_ref, do_ref, dg_ref, lo, w in ((oa_ref, ga_ref, doa_ref, dga_ref, 0, QA),
                                                    (ob_ref, gb_ref, dob_ref, dgb_ref, QA, QB)):
            ov = o_ref[...]
            dyv = dy_ref[:, lo:lo + w]
            r = lax.rsqrt(jnp.mean(ov * ov, axis=-1, keepdims=True) + EPS)
            n = ov * r
            dn = dyv * g_ref[...]
            do_ref[...] = r * (dn - n * jnp.mean(dn * n, axis=-1, keepdims=True))
            dg_ref[...] += jnp.sum(dyv * n, axis=0, keepdims=True)

    ra = pl.BlockSpec((ts, QA), lambda i: (i, 0))
    rb = pl.BlockSpec((ts, QB), lambda i: (i, 0))
    return _call(body, name="group_norm_bwd", grid=(s // ts,),
                 in_specs=[pl.BlockSpec((ts, QA + QB), lambda i: (i, 0)), ra, rb, _vec_spec(QA), _vec_spec(QB)],
                 out_specs=(ra, rb, _vec_spec(QA), _vec_spec(QB)),
                 out_shape=(jax.ShapeDtypeStruct((s, QA), F32), jax.ShapeDtypeStruct((s, QB), F32),
                            jax.ShapeDtypeStruct((1, QA), F32), jax.ShapeDtypeStruct((1, QB), F32)),
                 args=[dy, oa, ob, ga, gb], sem=("arbitrary",))


def _n_variants(n_back):
    return -(-n_back // QG) + 1


def _alibi_bias():
    i = np.arange(QROWS)[:, None]
    j = np.arange((QG + BACK_A) * CHUNK)[None, :]
    dist = np.abs(BACK_A * CHUNK + i - j).astype(np.float32)
    dc = j // CHUNK - i // CHUNK
    valid = (dc >= 0) & (dc <= BACK_A)
    slopes = np.array([2.0 ** (-8.0 * (h + 1) / H_A) for h in range(H_A)], dtype=np.float32)
    bias = -slopes[:, None, None] * dist[None]
    out = [np.where((valid & (j >= (BACK_A - QG * v) * CHUNK))[None], bias, np.float32(NEG_INF))
           for v in range(_n_variants(BACK_A))]
    return jnp.asarray(np.stack(out).astype(np.float32))


def _rel_index_matrix():
    cc = np.arange(SKEW)
    dist = np.where(cc < SKEW - QROWS, BACK_B * CHUNK - cc, BACK_B * CHUNK + SKEW - cc)
    idx = np.clip(dist, -REL_CLIP, REL_CLIP) + REL_CLIP
    m = np.zeros((SKEW, N_REL), np.float32)
    m[cc, idx] = 1.0
    return jnp.asarray(m)


def _toeplitz_bias(vec, carry=None):
    lk = (QG + BACK_B) * CHUNK
    nv = _n_variants(BACK_B)

    def body(v_ref, o_ref):
        xv = jnp.broadcast_to(v_ref[0], (QROWS, SKEW))
        row = lax.broadcasted_iota(jnp.int32, (QROWS, SKEW), 0)
        for bit in range(QROWS.bit_length() - 1):
            xv = jnp.where((row >> bit) & 1 == 1, pltpu.roll(xv, 1 << bit, 1), xv)
        ri = lax.broadcasted_iota(jnp.int32, (QROWS, lk), 0) // CHUNK
        col = lax.broadcasted_iota(jnp.int32, (QROWS, lk), 1)
        ci = col // CHUNK
        valid = (ci - ri >= 0) & (ci - ri <= BACK_B)
        for v in range(nv):
            o_ref[v, 0] = jnp.where(valid & (col >= (BACK_B - QG * v) * CHUNK), xv[:, :lk], NEG_INF)

    return _call(body, name="toeplitz_bias", grid=(H_B,),
                 in_specs=[pl.BlockSpec((1, 1, SKEW), lambda h: (h, 0, 0))],
                 out_specs=pl.BlockSpec((nv, 1, QROWS, lk), lambda h: (0, h, 0, 0)),
                 out_shape=jax.ShapeDtypeStruct((nv, H_B, QROWS, lk), F32), args=[vec], sem=("parallel",),
                 carry=carry)


def _diagonal_sums(dbias):
    lk = dbias.shape[2]

    def body(d_ref, o_ref):
        xp = jnp.concatenate([d_ref[0], jnp.zeros((QROWS, SKEW - lk), F32)], axis=1)
        xv = xp[0:CHUNK]
        for q in range(1, QG):
            xv = xv + pltpu.roll(xp[q * CHUNK:(q + 1) * CHUNK], SKEW - q * CHUNK, 1)
        row = lax.broadcasted_iota(jnp.int32, (CHUNK, SKEW), 0)
        for bit in range(CHUNK.bit_length() - 1):
            xv = jnp.where((row >> bit) & 1 == 1, pltpu.roll(xv, SKEW - (1 << bit), 1), xv)
        o_ref[0] = jnp.sum(xv, axis=0, keepdims=True)

    return _call(body, name="diagonal_sums", grid=(H_B,),
                 in_specs=[pl.BlockSpec((1, QROWS, lk), lambda h: (h, 0, 0))],
                 out_specs=pl.BlockSpec((1, 1, SKEW), lambda h: (h, 0, 0)),
                 out_shape=jax.ShapeDtypeStruct((H_B, 1, SKEW), F32), args=[dbias], sem=("parallel",))


def _attn_common(s, n_back, gqa, q_col, k_col, v_col, TPS):
    assert q_col % TPS == 0 and (gqa or (k_col % TPS == 0 and v_col % TPS == 0)), "blocks of TPS lane tiles"
    lk = (QG + n_back) * CHUNK
    pad = n_back * CHUNK
    wide = TPS * LANES
    q_spec = pl.BlockSpec((QROWS, wide), lambda t, g: (g, q_col // TPS + t))
    if gqa:
        k_spec = pl.BlockSpec((s, LANES), lambda t, g: (0, k_col))
        v_spec = pl.BlockSpec((s, LANES), lambda t, g: (0, v_col))
    else:
        k_spec = pl.BlockSpec((s, wide), lambda t, g: (0, k_col // TPS + t))
        v_spec = pl.BlockSpec((s, wide), lambda t, g: (0, v_col // TPS + t))
    last_variant = _n_variants(n_back) - 1
    bias_spec = pl.BlockSpec((None, 2 * TPS, QROWS, lk), lambda t, g: (jnp.minimum(g, last_variant), t, 0, 0))
    tile_spec = pl.BlockSpec((QROWS, wide), lambda t, g: (g, t))
    return lk, pad, q_spec, k_spec, v_spec, bias_spec, tile_spec


def _attention_fwd(proj, bias, sinks, *, n_back, gqa, q_col, k_col, v_col, TPS, name, carry=None):
    s = proj.shape[0]
    lk, pad, q_spec, k_spec, v_spec, bias_spec, tile_spec = _attn_common(s, n_back, gqa, q_col, k_col, v_col, TPS)
    n_t, n_g = 512 // (TPS * LANES), s // QROWS
    kv_wide = LANES if gqa else TPS * LANES

    def body(*refs):
        if gqa:
            q_ref, k_ref, v_ref, bias_ref, sink_ref, o_ref, l_ref, kpad, vpad = refs
        else:
            q_ref, k_ref, v_ref, bias_ref, o_ref, l_ref, kpad, vpad = refs
        t, g = pl.program_id(0), pl.program_id(1)

        @pl.when(g == 0)
        def _():
            kpad[0:pad, :] = jnp.zeros((pad, kv_wide), BF16)
            vpad[0:pad, :] = jnp.zeros((pad, kv_wide), BF16)
            kpad[pad:, :] = k_ref[...]
            vpad[pad:, :] = v_ref[...]

        start = pl.multiple_of(g * QROWS, QROWS)
        half = lax.broadcasted_iota(jnp.int32, (QROWS, LANES), 1) // HEAD_DIM
        for tt in range(TPS):
            lanes = slice(tt * LANES, (tt + 1) * LANES)
            kv_lanes = slice(0, LANES) if gqa else lanes
            kb = kpad[pl.ds(start, lk), kv_lanes]
            vb = vpad[pl.ds(start, lk), kv_lanes]
            q = q_ref[:, lanes] * (HEAD_DIM ** -0.5)
            if gqa:
                hk = (TPS * t + tt) // 2
                q_rolled = pltpu.roll(q.astype(F32), HEAD_DIM, 1).astype(BF16)
            outs, lses = [], []
            for e in range(2):
                if gqa:
                    kv_half = hk
                    src = jnp.where(hk == e, q, q_rolled)
                else:
                    kv_half = e
                    src = q
                qm = jnp.where(half == kv_half, src, jnp.zeros_like(src))
                sc = _dot_nt(qm, kb) + bias_ref[2 * tt + e]
                m = jnp.max(sc, axis=-1, keepdims=True)
                if gqa:
                    sk = sink_ref[2 * (TPS * t + tt) + e]
                    m = jnp.maximum(m, sk)
                p = jnp.exp(sc - m)
                l = jnp.sum(p, axis=-1, keepdims=True)
                if gqa:
                    l = l + jnp.exp(sk - m)
                pn = p / l
                outs.append(_dot(pn.astype(BF16), vb))
                lses.append(m + jnp.log(l))
            if gqa:
                same = jnp.where(hk == 0, outs[0], outs[1])
                other = jnp.where(hk == 0, outs[1], outs[0])
                o_ref[:, lanes] = jnp.where(half == hk, same, pltpu.roll(other, HEAD_DIM, 1))
            else:
                o_ref[:, lanes] = jnp.where(half == 0, outs[0], outs[1])
            l_ref[:, lanes] = jnp.where(half == 0, lses[0], lses[1])

    in_specs = [q_spec, k_spec, v_spec, bias_spec] + ([SMEM_SPEC] if gqa else [])
    args = [proj, proj, proj, bias] + ([sinks] if gqa else [])
    o_shape = jax.ShapeDtypeStruct((s, 512), F32)
    return _call(body, name=name, grid=(n_t, n_g), in_specs=in_specs, out_specs=(tile_spec, tile_spec),
                 out_shape=(o_shape, o_shape), args=args,
                 scratch=[pltpu.VMEM((s + pad, kv_wide), BF16), pltpu.VMEM((s + pad, kv_wide), BF16)],
                 sem=("arbitrary", "arbitrary"), carry=carry)


def _attention_bwd(proj, bias, sinks, do, lse, *, n_back, gqa, q_col, k_col, v_col, TPS, name, carry=None):
    s = proj.shape[0]
    lk, pad, q_spec, k_spec, v_spec, bias_spec, tile_spec = _attn_common(s, n_back, gqa, q_col, k_col, v_col, TPS)
    n_t, n_g = 512 // (TPS * LANES), s // QROWS
    kv_wide = LANES if gqa else TPS * LANES

    def body(*refs):
        if gqa:
            (q_ref, k_ref, v_ref, bias_ref, sink_ref, do_ref, l_ref,
             dq_ref, dk_ref, dv_ref, dsink_ref, kpad, vpad, dkpad, dvpad) = refs
        else:
            (q_ref, k_ref, v_ref, bias_ref, do_ref, l_ref,
             dq_ref, dk_ref, dv_ref, dbias_ref, kpad, vpad, dkpad, dvpad) = refs
        t, g = pl.program_id(0), pl.program_id(1)

        @pl.when(g == 0)
        def _():
            kpad[0:pad, :] = jnp.zeros((pad, kv_wide), BF16)
            vpad[0:pad, :] = jnp.zeros((pad, kv_wide), BF16)
            kpad[pad:, :] = k_ref[...]
            vpad[pad:, :] = v_ref[...]
            if gqa:
                dsink_ref[...] = jnp.zeros_like(dsink_ref)
            else:
                dbias_ref[...] = jnp.zeros_like(dbias_ref)

        @pl.when((g == 0) & (t == 0) if gqa else g == 0)
        def _():
            dkpad[...] = jnp.zeros_like(dkpad)
            dvpad[...] = jnp.zeros_like(dvpad)

        start = pl.multiple_of(g * QROWS, QROWS)
        half = lax.broadcasted_iota(jnp.int32, (QROWS, LANES), 1) // HEAD_DIM
        for tt in range(TPS):
            lanes = slice(tt * LANES, (tt + 1) * LANES)
            kv_lanes = slice(0, LANES) if gqa else lanes
            kb = kpad[pl.ds(start, lk), kv_lanes]
            vb = vpad[pl.ds(start, lk), kv_lanes]
            q = q_ref[:, lanes]
            dov = do_ref[:, lanes]
            lv = l_ref[:, lanes]
            if gqa:
                hk = (TPS * t + tt) // 2
                q_rolled = pltpu.roll(q.astype(F32), HEAD_DIM, 1).astype(BF16)
                do_rolled = pltpu.roll(dov, HEAD_DIM, 1)
            dqs = []
            dk_acc = jnp.zeros((lk, LANES), F32)
            dv_acc = jnp.zeros((lk, LANES), F32)
            for e in range(2):
                if gqa:
                    kv_half = hk
                    src = jnp.where(hk == e, q, q_rolled)
                    do_src = jnp.where(hk == e, dov, do_rolled)
                else:
                    kv_half = e
                    src = q
                    do_src = dov
                qm = jnp.where(half == kv_half, src, jnp.zeros_like(src))
                dom = jnp.where(half == kv_half, do_src, 0.0).astype(BF16)
                lcol = jnp.max(jnp.where(half == e, lv, -jnp.inf), axis=-1, keepdims=True)
                sc = _dot_nt(qm * (HEAD_DIM ** -0.5), kb) + bias_ref[2 * tt + e]
                pn = jnp.exp(sc - lcol)
                dp = _dot_nt(dom, vb)
                delta = jnp.sum(pn * dp, axis=-1, keepdims=True)
                ds = pn * (dp - delta)
                if gqa:
                    p_sink = jnp.exp(sink_ref[2 * (TPS * t + tt) + e] - lcol)
                    dsk = -jnp.sum(p_sink * delta, axis=0, keepdims=True)
                    row = 2 * tt + e
                    dsink_ref[0, row:row + 1, :] += jnp.broadcast_to(dsk, (1, LANES))
                else:
                    dbias_ref[2 * tt + e] += ds
                dsb = (ds * (HEAD_DIM ** -0.5)).astype(BF16)
                dqs.append(_dot(dsb, kb))
                dk_acc = dk_acc + _dot_tn(dsb, qm)
                dv_acc = dv_acc + _dot_tn(pn.astype(BF16), dom)
            dkpad[pl.ds(start, lk), kv_lanes] += dk_acc
            dvpad[pl.ds(start, lk), kv_lanes] += dv_acc
            if gqa:
                same = jnp.where(hk == 0, dqs[0], dqs[1])
                other = jnp.where(hk == 0, dqs[1], dqs[0])
                dq_ref[:, lanes] = jnp.where(half == hk, same, pltpu.roll(other, HEAD_DIM, 1)).astype(BF16)
            else:
                dq_ref[:, lanes] = jnp.where(half == 0, dqs[0], dqs[1]).astype(BF16)

        @pl.when((g == n_g - 1) & (t == n_t - 1) if gqa else g == n_g - 1)
        def _():
            dk_ref[...] = dkpad[pad:, :].astype(BF16)
            dv_ref[...] = dvpad[pad:, :].astype(BF16)

    in_specs = [q_spec, k_spec, v_spec, bias_spec] + ([SMEM_SPEC] if gqa else []) + [tile_spec, tile_spec]
    args = [proj, proj, proj, bias] + ([sinks] if gqa else []) + [do, lse]
    if gqa:
        kv_out = pl.BlockSpec((s, LANES), lambda t, g: (0, 0))
        kv_shape = jax.ShapeDtypeStruct((s, LANES), BF16)
        extra_spec = pl.BlockSpec((1, 8, LANES), lambda t, g: (t, 0, 0))
        extra_shape = jax.ShapeDtypeStruct((n_t, 8, LANES), F32)
    else:
        kv_out = pl.BlockSpec((s, kv_wide), lambda t, g: (0, t))
        kv_shape = jax.ShapeDtypeStruct((s, 512), BF16)
        extra_spec = pl.BlockSpec((2 * TPS, QROWS, lk), lambda t, g: (t, 0, 0))
        extra_shape = jax.ShapeDtypeStruct(bias.shape[1:], F32)
    return _call(body, name=name, grid=(n_t, n_g), in_specs=in_specs,
                 out_specs=(tile_spec, kv_out, kv_out, extra_spec),
                 out_shape=(jax.ShapeDtypeStruct((s, 512), BF16), kv_shape, kv_shape, extra_shape), args=args,
                 scratch=[pltpu.VMEM((s + pad, kv_wide), BF16), pltpu.VMEM((s + pad, kv_wide), BF16),
                          pltpu.VMEM((s + pad, kv_wide), F32), pltpu.VMEM((s + pad, kv_wide), F32)],
                 sem=("arbitrary", "arbitrary"), carry=carry)


def _sum_rows8(g):
    n = g.shape[2]

    def body(g_ref, o_ref):
        acc = g_ref[0]
        for j in range(1, N_DEV):
            acc = acc + g_ref[j]
        o_ref[...] = acc

    return pl.pallas_call(
        body, name="sum_small_grads", in_specs=[VMEM_SPEC], out_specs=VMEM_SPEC,
        out_shape=jax.ShapeDtypeStruct((1, n), F32), compiler_params=_params(),
    )(g)


def _ada_weight_grad(sc_t, dmod_cols):
    d = sc_t.shape[0]
    w = dmod_cols.shape[1]
    td = _pick(d, (256, 128))

    def body(sc_ref, dm_ref, o_ref):
        scv = sc_ref[...]
        dmv = dm_ref[...]
        acc = scv[:, 0:1] * dmv[0:1, :]
        for b in range(1, N_DEV):
            acc = acc + scv[:, b:b + 1] * dmv[b:b + 1, :]
        o_ref[...] = acc

    return _call(body, name="ada_weight_grad", grid=(d // td,),
                 in_specs=[pl.BlockSpec((td, N_DEV), lambda i: (i, 0)), pl.BlockSpec((N_DEV, w), lambda i: (0, 0))],
                 out_specs=pl.BlockSpec((td, w), lambda i: (i, 0)), out_shape=jax.ShapeDtypeStruct((d, w), F32),
                 args=[sc_t, dmod_cols], sem=("parallel",))


def _adamw_update(w, gv, m, v):
    nm = ADAM_B1 * m + (1.0 - ADAM_B1) * gv
    nv = ADAM_B2 * v + (1.0 - ADAM_B2) * (gv * gv)
    m_hat = nm / (1.0 - ADAM_B1 ** ADAM_STEP)
    v_hat = nv / (1.0 - ADAM_B2 ** ADAM_STEP)
    return -ADAM_LR * (m_hat / (jnp.sqrt(v_hat) + ADAM_EPS) + ADAM_WD * w), nm, nv


def _adamw(w, g, m, v, name):
    rows, cols = w.shape
    tr = _pick(rows, (256, 176, 128, 88, 64)) if rows > 256 else rows

    def body(w_ref, g_ref, m_ref, v_ref, d_ref, nm_ref, nv_ref):
        d_ref[...], nm_ref[...], nv_ref[...] = _adamw_update(w_ref[...], g_ref[...], m_ref[...], v_ref[...])

    spec = pl.BlockSpec((tr, cols), lambda i: (i, 0))
    shape = jax.ShapeDtypeStruct((rows, cols), F32)
    return _call(body, name=name, grid=(rows // tr,), in_specs=[spec] * 4, out_specs=(spec, spec, spec),
                 out_shape=(shape, shape, shape), args=[w, g, m, v], sem=("parallel",))


def _adamw_from_slots(w, own, slots, m, v, name):
    n_slots, rows, k = slots.shape

    def body(o_ref, s_ref, w_ref, m_ref, v_ref, g_ref, d_ref, nm_ref, nv_ref):
        gv = o_ref[...].astype(F32)
        for j in range(n_slots):
            gv = gv + s_ref[j].astype(F32)
        g_ref[...] = gv
        d_ref[...], nm_ref[...], nv_ref[...] = _adamw_update(w_ref[...], gv, m_ref[...], v_ref[...])

    tr = rows // 2 if rows % 32 == 0 else rows
    spec = pl.BlockSpec((tr, k), lambda i: (i, 0))
    shape = jax.ShapeDtypeStruct((rows, k), F32)
    return _call(body, name=name, grid=(rows // tr,),
                 in_specs=[spec, pl.BlockSpec((n_slots, tr, k), lambda i: (0, i, 0)), spec, spec, spec],
                 out_specs=(spec, spec, spec, spec), out_shape=(shape, shape, shape, shape),
                 args=[own, slots, w, m, v], sem=("parallel",))


def _adamw_small(g, w, m, v, sizes):
    n = w.shape[1]
    offs, off = [], 0
    for size in sizes:
        offs.append(off)
        off += size + (-size % LANES)

    def body(g_ref, w_ref, m_ref, v_ref, *out_refs):
        gv = g_ref[:, 0:n]
        dv, nm, nv = _adamw_update(w_ref[...], gv, m_ref[...], v_ref[...])
        for j, (o, size) in enumerate(zip(offs, sizes)):
            for k, val in enumerate((gv, dv, nm, nv)):
                out_refs[4 * j + k][...] = val[:, o:o + size]

    shapes = [jax.ShapeDtypeStruct((1, size), F32) for size in sizes for _ in range(4)]
    return pl.pallas_call(
        body, name="adamw_small", in_specs=[VMEM_SPEC] * 4, out_specs=tuple([VMEM_SPEC] * len(shapes)),
        out_shape=tuple(shapes), compiler_params=_params(),
    )(g, w, m, v)


SMALL = ("b_ada", "g_pre_ffn1", "g_post_ffn1", "g_pre_mix", "b_in", "sinks_a", "rel_bias_b", "g_grp_a",
         "g_grp_b", "b_out", "g_post_mix", "g_pre_ffn2", "g_post_ffn2")
WEIGHTS = ("w_ada", "b_ada", "g_pre_ffn1", "w_gate1", "w_up1", "w_down1", "g_post_ffn1", "g_pre_mix", "w_in",
           "b_in", "sinks_a", "rel_bias_b", "g_grp_a", "g_grp_b", "w_out", "b_out", "g_post_mix", "g_pre_ffn2",
           "w_gate2", "w_up2", "w_down2", "g_post_ffn2")


def kernel(x, c, w_ada, b_ada, g_pre_ffn1, w_gate1, w_up1, w_down1, g_post_ffn1, g_pre_mix, w_in, b_in, sinks_a, rel_bias_b, g_grp_a, g_grp_b, w_out, b_out, g_post_mix, g_pre_ffn2, w_gate2, w_up2, w_down2, g_post_ffn2, loss_target, m_w_ada, m_b_ada, m_g_pre_ffn1, m_w_gate1, m_w_up1, m_w_down1, m_g_post_ffn1, m_g_pre_mix, m_w_in, m_b_in, m_sinks_a, m_rel_bias_b, m_g_grp_a, m_g_grp_b, m_w_out, m_b_out, m_g_post_mix, m_g_pre_ffn2, m_w_gate2, m_w_up2, m_w_down2, m_g_post_ffn2, v_w_ada, v_b_ada, v_g_pre_ffn1, v_w_gate1, v_w_up1, v_w_down1, v_g_post_ffn1, v_g_pre_mix, v_w_in, v_b_in, v_sinks_a, v_rel_bias_b, v_g_grp_a, v_g_grp_b, v_w_out, v_b_out, v_g_post_mix, v_g_pre_ffn2, v_w_gate2, v_w_up2, v_w_down2, v_g_post_ffn2):
    given = dict(locals())
    weights = {n: given[n] for n in WEIGHTS}
    mom_m = {n: given["m_" + n] for n in WEIGHTS}
    mom_v = {n: given["v_" + n] for n in WEIGHTS}

    me = 4 * lax.axis_index("x") + 2 * lax.axis_index("y") + lax.axis_index("c")
    xs = x[0]
    tgt = loss_target[0]
    d_model = xs.shape[1]
    ada_cols = w_ada.shape[2]

    sh = {"wg1": w_gate1[0].T, "wu1": w_up1[0].T, "wd1": w_down1[0], "win": w_in[0].T, "wo": w_out[0],
          "wg2": w_gate2[0].T, "wu2": w_up2[0].T, "wd2": w_down2[0]}
    sh = {k: v.astype(BF16) for k, v in sh.items()}

    def gather(*names):
        return _gather_carry([sh[n] for n in names])

    bias_a = _alibi_bias()
    rel_m = _rel_index_matrix()
    rel_vec = jnp.dot(rel_bias_b[0], rel_m.T, precision=lax.Precision.HIGHEST)
    bias_b, (wg1, wu1) = _toeplitz_bias(rel_vec.reshape(H_B, 1, SKEW), carry=gather("wg1", "wu1"))

    b_cols = lax.dynamic_slice(b_ada, (0, me * ada_cols), (1, ada_cols))
    (sc_all, mod_rows), _ = _ada_forward(c, w_ada[0], b_cols, _Carry([], [], [], lambda *a: None, lambda *a: None))
    mod = mod_rows.reshape(N_MOD, d_model)
    shift1, scale1, gate1, shift2, scale2, gate2, shift3, scale3, gate3 = (mod[i:i + 1] for i in range(N_MOD))

    h1 = _pre_norm(xs, g_pre_ffn1, scale1, shift1, "pre_norm_ffn1")
    (a1, b1, u1), (wd1,) = _ffn_up(h1, wg1, wu1, "ffn_up_ffn1", carry=gather("wd1"))
    (y1, x1, h2), (win,) = _mm_nn(
        [(u1, wd1)], "ffn_down_ffn1", F32, carry=gather("win"),
        tail=_tail_post_pre(xs, g_post_ffn1, gate1, 0.5, g_pre_mix, scale2, shift2))

    proj, (wo,) = _mm_nt(h2, win, "in_proj", BF16, bias=b_in, carry=gather("wo"))
    sinks = sinks_a[0]
    cfg_a = dict(n_back=BACK_A, gqa=True, q_col=0, k_col=QA // LANES, v_col=(QA + KVA) // LANES, TPS=TPS_A)
    cfg_b = dict(n_back=BACK_B, gqa=False, q_col=(QA + 2 * KVA) // LANES, k_col=(QA + 2 * KVA + QB) // LANES,
                 v_col=(QA + 2 * KVA + 2 * QB) // LANES, TPS=TPS_B)
    (oa, lse_a), (wg2,) = _attention_fwd(proj, bias_a, sinks, name="attn_a", carry=gather("wg2"), **cfg_a)
    (ob, lse_b), (wu2,) = _attention_fwd(proj, bias_b, None, name="attn_b", carry=gather("wu2"), **cfg_b)
    ycat = _group_norm_cat(oa, ob, g_grp_a, g_grp_b)
    ymix, x2, h3 = _mm_nn([(ycat, wo)], "out_proj", F32, bias=b_out,
                          tail=_tail_post_pre(x1, g_post_mix, gate2, 1.0, g_pre_ffn2, scale3, shift3))

    (a3, b3, u3), (wd2,) = _ffn_up(h3, wg2, wu2, "ffn_up_ffn2", carry=gather("wd2"))

    flights, own = {}, {}

    def grad_pair(key, a_mat, b_mat, name):
        part, own[key] = _mm_tn_pair(a_mat, b_mat, name)
        return part

    def scatter_start(tag, after_vec, **parts):
        names = list(parts)
        sems, p_thru, lands, token = _scatter_start([parts[n] for n in names], "scatter_start_" + tag)
        flights[tag] = (names, sems, p_thru, lands)
        return after_vec + token[0:1, 0:1]

    dx3, dy, loss_part, s1 = _mm_nn([(u3, wd2)], "ffn_down_ffn2", None,
                                    tail=_tail_post_loss(x2, tgt, g_post_ffn2, gate3, 0.5))
    da, db = _ffn_down_bwd(dy, wd2, a3, b3, "ffn_down_bwd_ffn2")
    dwd2 = grad_pair("wd2", u3, dy, "grad_wd_ffn2")
    dwg2 = grad_pair("wg2", da, h3, "grad_wg_ffn2")
    dwu2 = grad_pair("wu2", db, h3, "grad_wu_ffn2")
    g_pre_tied = scatter_start("ffn2", g_pre_ffn2, wd2=dwd2, wg2=dwg2, wu2=dwu2)
    dx2, dymix, s2, s3, s1m, db_out = _mm_nn(
        [(da, wg2), (db, wu2)], "ffn_up_bwd_ffn2", None,
        tail=_tail_pre_post_bwd(x2, dx3, ymix, g_pre_tied, scale3, g_post_mix, gate2, 1.0))
    sm3 = dict(shift=s3, scale=s2 * g_pre_ffn2, gate=0.5 * g_post_ffn2 * s1,
               g_pre=(1.0 + scale3) * s2, g_post=(0.5 * gate3) * s1)

    dycat = _mm_nt(dymix, wo, "out_proj_bwd", F32)
    dwo = grad_pair("wo", ycat, dymix, "grad_wo")
    doa, dob, dg_a, dg_b = _group_norm_bwd(dycat, oa, ob, g_grp_a, g_grp_b)
    dqa, dka, dva, dsink = _attention_bwd(proj, bias_a, sinks, doa, lse_a, name="attn_a_bwd", **cfg_a)
    dqb, dkb, dvb, dbias = _attention_bwd(proj, bias_b, None, dob, lse_b, name="attn_b_bwd", **cfg_b)
    dproj = jnp.concatenate([dqa, dka, dva, dqb, dkb, dvb], axis=1)
    dwin, own["win"], db_in = _mm_tn_pair(dproj, h2, "grad_win", col_sums=True)
    g_pre_tied = scatter_start("mix", g_pre_mix, wo=dwo, win=dwin)
    dx1, dy, s2m, s3m, s1, _ = _mm_nn(
        [(dproj, win)], "in_proj_bwd", None,
        tail=_tail_pre_post_bwd(x1, dx2, y1, g_pre_tied, scale2, g_post_ffn1, gate1, 0.5))
    d_rel = jnp.dot(_diagonal_sums(dbias).reshape(H_B, SKEW), rel_m, precision=lax.Precision.HIGHEST)
    d_sinks = dsink[:, :2 * TPS_A, 0].reshape(1, H_A)

    da, db = _ffn_down_bwd(dy, wd1, a1, b1, "ffn_down_bwd_ffn1")
    dwd1 = grad_pair("wd1", u1, dy, "grad_wd_ffn1")
    dwg1 = grad_pair("wg1", da, h1, "grad_wg_ffn1")
    dwu1 = grad_pair("wu1", db, h1, "grad_wu_ffn1")
    g_pre_tied = scatter_start("ffn1", g_pre_ffn1, wd1=dwd1, wg1=dwg1, wu1=dwu1)
    dx0, s2, s3 = _mm_nn([(da, wg1), (db, wu1)], "ffn_up_bwd_ffn1", None,
                         tail=_tail_pre_bwd(xs, dx1, g_pre_tied, scale1))
    sm1 = dict(shift=s3, scale=s2 * g_pre_ffn1, gate=0.5 * g_post_ffn1 * s1,
               g_pre=(1.0 + scale1) * s2, g_post=(0.5 * gate1) * s1)

    dmod = jnp.concatenate([sm1["shift"], sm1["scale"], sm1["gate"],
                            s3m, s2m * g_pre_mix, g_post_mix * s1m,
                            sm3["shift"], sm3["scale"], sm3["gate"]], axis=1)
    small_parts = {
        "b_ada": dmod, "g_pre_ffn1": sm1["g_pre"], "g_post_ffn1": sm1["g_post"],
        "g_pre_mix": (1.0 + scale2) * s2m, "b_in": db_in, "sinks_a": d_sinks,
        "rel_bias_b": d_rel.reshape(1, H_B * N_REL), "g_grp_a": dg_a, "g_grp_b": dg_b, "b_out": db_out,
        "g_post_mix": gate2 * s1m, "g_pre_ffn2": sm3["g_pre"], "g_post_ffn2": sm3["g_post"]}
    sizes = [small_parts[n].shape[1] for n in SMALL]

    def pack(parts):
        cells = []
        for p in parts:
            cells.append(p)
            if p.shape[1] % LANES:
                cells.append(jnp.zeros((1, -p.shape[1] % LANES), F32))
        return jnp.concatenate(cells, axis=1)

    packed = pack([small_parts[n] for n in SMALL] + [loss_part])
    n_packed = packed.shape[1]
    small_sems, packed_thru, small_land, small_token = _small_gather_start(packed)

    out_g, out_d, out_m, out_v = {}, {}, {}, {}
    groups = (("ffn2", (("w_gate2", "wg2", True), ("w_up2", "wu2", True), ("w_down2", "wd2", False))),
              ("mix", (("w_in", "win", True), ("w_out", "wo", False))),
              ("ffn1", (("w_gate1", "wg1", True), ("w_up1", "wu1", True), ("w_down1", "wd1", False))))
    after = small_token
    for tag, members in groups:
        names, sems, p_thru, lands = flights[tag]
        _, l_done = _scatter_wait(sems, p_thru, lands, after, "scatter_wait_" + tag)
        slots = dict(zip(names, l_done))
        for n, key, transposed in members:
            view = (lambda t: t.T) if transposed else (lambda t: t)
            res = _adamw_from_slots(view(weights[n][0]), own[key], slots[key], view(mom_m[n][0]),
                                    view(mom_v[n][0]), "adamw_" + n)
            out_g[n], out_d[n], out_m[n], out_v[n] = (view(t)[None] for t in res)
            after = res[3]

    packed_done, small_land = _small_gather_wait(small_sems, packed_thru, small_land, after)
    gathered = lax.dynamic_update_slice(small_land, packed_done[None], (me, 0, 0))
    small_sum = _sum_rows8(gathered)
    loss = small_sum[0, n_packed - LANES]
    dmod_cols = lax.dynamic_slice(gathered.reshape(N_DEV, n_packed), (0, me * ada_cols), (N_DEV, ada_cols))
    g_ada = _ada_weight_grad(sc_all.reshape(N_DEV, d_model).T, dmod_cols)
    d_, m_, v_ = _adamw(w_ada[0], g_ada, m_w_ada[0], v_w_ada[0], "adamw_w_ada")
    out_g["w_ada"], out_d["w_ada"], out_m["w_ada"], out_v["w_ada"] = g_ada[None], d_[None], m_[None], v_[None]

    small_out = _adamw_small(small_sum, *(pack([tree[n].reshape(1, -1) for n in SMALL])
                                          for tree in (weights, mom_m, mom_v)), sizes)
    for j, n in enumerate(SMALL):
        shape = weights[n].shape
        out_g[n], out_d[n], out_m[n], out_v[n] = (t.reshape(shape) for t in small_out[4 * j:4 * j + 4])

    return (loss, dx0[None], *[out_g[n] for n in WEIGHTS], *[out_d[n] for n in WEIGHTS],
            *[out_m[n] for n in WEIGHTS], *[out_v[n] for n in WEIGHTS])
```

```python
import numpy as np
import jax
import jax.numpy as jnp
from jax import lax
from jax.experimental import pallas as pl
from jax.experimental.pallas import tpu as pltpu

F32 = jnp.float32
BF16 = jnp.bfloat16
MESH = pl.DeviceIdType.MESH
ANY = pl.BlockSpec(memory_space=pl.ANY)
VMEM_SPEC = pl.BlockSpec(memory_space=pltpu.VMEM)
SMEM_SPEC = pl.BlockSpec(memory_space=pltpu.SMEM)

N_DEV = 8
CHUNK = 64
HEAD_DIM = 64
LANES = 128
H_A, KV_A, H_B = 8, 2, 8
BACK_A, BACK_B = 2, 8
REL_CLIP = 128
N_REL = 2 * REL_CLIP + 1
QA, KVA, QB = H_A * HEAD_DIM, KV_A * HEAD_DIM, H_B * HEAD_DIM
D_IN = QA + 2 * KVA + 3 * QB
N_MOD = 9
EPS = 1e-6
NEG_INF = -1e30
QG = 4
QROWS = QG * CHUNK
TPS_A, TPS_B = 4, 2
SKEW = 1024
ADAM_LR, ADAM_B1, ADAM_B2, ADAM_EPS, ADAM_WD, ADAM_STEP = 0.001, 0.9, 0.999, 1e-08, 0.01, 10
VMEM_LIMIT = 56 * 2 ** 20


def _pick(n, cands):
    for c in cands:
        if n % c == 0:
            return c
    return n


def _pieces(n, width=2 * LANES):
    return [(lo, min(lo + width, n)) for lo in range(0, n, width)]


def _params(sem=None):
    return pltpu.CompilerParams(dimension_semantics=sem, vmem_limit_bytes=VMEM_LIMIT)


def _dot_nt(a, b):
    return lax.dot_general(a, b, (((1,), (1,)), ((), ())), preferred_element_type=F32)


def _dot_tn(a, b):
    return lax.dot_general(a, b, (((0,), (0,)), ((), ())), preferred_element_type=F32)


def _dot(a, b):
    return jnp.dot(a, b, preferred_element_type=F32)


def _sigmoid(a):
    return 0.5 * (jnp.tanh(0.5 * a) + 1.0)


def _mesh_pos():
    return lax.axis_index("x"), lax.axis_index("y"), lax.axis_index("c")


def _peer(x, y, c, r):
    px = 1 - x if r & 4 else x
    py = 1 - y if r & 2 else y
    pc = 1 - c if r & 1 else c
    return px, py, pc


class _Carry:
    def __init__(self, ins, out_shapes, scratch, start, finish):
        self.ins, self.out_shapes, self.scratch = list(ins), list(out_shapes), list(scratch)
        self.start, self.finish = start, finish


def _call(body, *, name, grid, in_specs, out_specs, out_shape, args, scratch=(), sem=None, carry=None):
    single = not isinstance(out_shape, (tuple, list))
    out_specs = (out_specs,) if single else tuple(out_specs)
    out_shape = (out_shape,) if single else tuple(out_shape)
    if carry is None:
        res = pl.pallas_call(body, name=name, grid=grid, in_specs=list(in_specs), out_specs=out_specs,
                             out_shape=out_shape, scratch_shapes=list(scratch), compiler_params=_params(sem))(*args)
        return res[0] if single else res
    n_in, n_out, n_s = len(in_specs), len(out_shape), len(scratch)
    ci, co = len(carry.ins), len(carry.out_shapes)

    def wrapped(*refs):
        ins, cins = refs[:n_in], refs[n_in:n_in + ci]
        outs = refs[n_in + ci:n_in + ci + n_out]
        couts = refs[n_in + ci + n_out:n_in + ci + n_out + co]
        scr = refs[n_in + ci + n_out + co:n_in + ci + n_out + co + n_s]
        cscr = refs[n_in + ci + n_out + co + n_s:]
        first, last = None, None
        for ax, n in enumerate(grid):
            f, l = pl.program_id(ax) == 0, pl.program_id(ax) == n - 1
            first = f if first is None else first & f
            last = l if last is None else last & l
        pl.when(first)(lambda: carry.start(cins, couts, cscr))
        body(*ins, *outs, *scr)
        pl.when(last)(lambda: carry.finish(cins, couts, cscr))

    res = pl.pallas_call(
        wrapped, name=name, grid=grid, in_specs=list(in_specs) + [ANY] * ci, out_specs=out_specs + (ANY,) * co,
        out_shape=out_shape + tuple(carry.out_shapes), scratch_shapes=list(scratch) + carry.scratch,
        compiler_params=_params(("arbitrary",) * len(grid)))(*args, *carry.ins)
    main = res[:n_out]
    return (main[0] if single else main), res[n_out:]


def _gather_carry(shards):
    n_w = len(shards)
    rows = [s.shape[0] for s in shards]

    def plan(ins, outs, scr):
        send_sems, recv_sems, local_sems = scr
        x, y, c = _mesh_pos()
        me, sibling = (x, y, c), (x, y, 1 - c)
        chips = [(1 - x, 1 - y), (1 - x, y), (x, 1 - y)]

        def block(w, dev):
            start = pl.multiple_of((4 * dev[0] + 2 * dev[1] + dev[2]) * rows[w], 16)
            return outs[w].at[pl.ds(start, rows[w]), :]

        def copy(w, k, dev, to, src=None):
            return pltpu.make_async_remote_copy(
                src_ref=block(w, dev) if src is None else src, dst_ref=block(w, dev),
                send_sem=send_sems.at[w, k], recv_sem=recv_sems.at[w, k], device_id=to, device_id_type=MESH)

        mine = [pltpu.make_async_copy(ins[w], block(w, me), local_sems.at[w]) for w in range(n_w)]
        first = []
        for j, chip in enumerate(chips):
            first += [copy(w, 1 + j, me, (*chip, c), src=ins[w]) for w in range(n_w)]
        first += [copy(w, 0, me, sibling, src=ins[w]) for w in range(n_w)]
        return c, me, sibling, chips, copy, mine, first

    def start(ins, outs, scr):
        _, _, _, _, _, mine, first = plan(ins, outs, scr)
        for cp in mine + first:
            cp.start()

    def finish(ins, outs, scr):
        c, me, sibling, chips, copy, mine, first = plan(ins, outs, scr)
        passed = []
        for j, chip in enumerate(chips):
            for w in range(n_w):
                copy(w, 1 + j, (*chip, c), me).wait_recv()
                cp = copy(w, 4 + j, (*chip, c), sibling)
                cp.start()
                passed.append(cp)
        for w in range(n_w):
            copy(w, 0, sibling, me).wait_recv()
        for j, chip in enumerate(chips):
            for w in range(n_w):
                copy(w, 4 + j, (*chip, 1 - c), me).wait_recv()
        for cp in first + passed:
            cp.wait_send()
        for cp in mine:
            cp.wait()

    return _Carry(
        shards, [jax.ShapeDtypeStruct((N_DEV * s.shape[0], s.shape[1]), s.dtype) for s in shards],
        [pltpu.SemaphoreType.DMA((n_w, N_DEV - 1)), pltpu.SemaphoreType.DMA((n_w, N_DEV - 1)),
         pltpu.SemaphoreType.DMA((n_w,))], start, finish)


HBM_SPEC = pl.BlockSpec(memory_space=pltpu.HBM)
SEM_SPEC = pl.BlockSpec(memory_space=pltpu.SEMAPHORE)
N_CHIP = N_DEV // 2


def _scatter_copy(part_ref, land_ref, send_sem, recv_sem, r, rows):
    x, y, c = _mesh_pos()
    px, py, _ = _peer(x, y, c, 2 * r)
    src = part_ref.at[pl.ds(pl.multiple_of((2 * px + py) * rows, 16), rows), :]
    return pltpu.make_async_remote_copy(
        src_ref=src, dst_ref=land_ref.at[r - 1], send_sem=send_sem, recv_sem=recv_sem,
        device_id=(px, py, c), device_id_type=MESH)


def _scatter_order(n_w):
    return [(w, r) for r in (3, 2, 1) for w in range(n_w)]


def _scatter_start(parts, name):
    n_w = len(parts)
    rows = [p.shape[0] // N_CHIP for p in parts]
    order = _scatter_order(n_w)
    lands = [pltpu.with_memory_space_constraint(lax.empty((N_CHIP - 1, r, p.shape[1]), p.dtype), pltpu.HBM)
             for r, p in zip(rows, parts)]

    def body(*refs):
        part_refs, land_refs = refs[:n_w], refs[n_w:2 * n_w]
        sems = refs[2 * n_w:2 * n_w + 2 * len(order)]
        token = refs[-1]
        for j, (w, r) in enumerate(order):
            _scatter_copy(part_refs[w], land_refs[w], sems[2 * j], sems[2 * j + 1], r, rows[w]).start()
        token[...] = jnp.zeros_like(token)

    n_sem = 2 * len(order)
    res = pl.pallas_call(
        body, name=name,
        out_shape=(*[pltpu.SemaphoreType.DMA(())] * n_sem, *[pltpu.HBM(p.shape, p.dtype) for p in parts],
                   *[pltpu.HBM(l.shape, l.dtype) for l in lands], jax.ShapeDtypeStruct((8, LANES), F32)),
        in_specs=[HBM_SPEC] * (2 * n_w), out_specs=(*[SEM_SPEC] * n_sem, *[HBM_SPEC] * (2 * n_w), VMEM_SPEC),
        input_output_aliases={i: n_sem + i for i in range(2 * n_w)},
        compiler_params=pltpu.CompilerParams(has_side_effects=pltpu.SideEffectType.DATAFLOW_SIDE_EFFECTING),
    )(*[pltpu.with_memory_space_constraint(p, pltpu.HBM) for p in parts], *lands)
    return (list(res[:n_sem]), list(res[n_sem:n_sem + n_w]), list(res[n_sem + n_w:n_sem + 2 * n_w]), res[-1])


def _scatter_wait(sems, parts, lands, after, name):
    n_w = len(parts)
    rows = [p.shape[0] // N_CHIP for p in parts]
    order = _scatter_order(n_w)

    def body(*refs):
        part_refs, land_refs = refs[:n_w], refs[n_w:2 * n_w]
        sem_refs = refs[2 * n_w:2 * n_w + 2 * len(order)]
        for j, (w, r) in enumerate(order):
            cp = _scatter_copy(part_refs[w], land_refs[w], sem_refs[2 * j], sem_refs[2 * j + 1], r, rows[w])
            cp.wait_send()
            cp.wait_recv()

    res = pl.pallas_call(
        body, name=name,
        out_shape=(*[pltpu.HBM(p.shape, p.dtype) for p in parts], *[pltpu.HBM(l.shape, l.dtype) for l in lands]),
        in_specs=[HBM_SPEC] * (2 * n_w) + [SEM_SPEC] * len(sems) + [ANY],
        out_specs=tuple([HBM_SPEC] * (2 * n_w)),
        input_output_aliases={i: i for i in range(2 * n_w)},
        compiler_params=pltpu.CompilerParams(has_side_effects=pltpu.SideEffectType.DATAFLOW_SIDE_EFFECTING),
    )(*parts, *lands, *sems, after)
    return list(res[:n_w]), list(res[n_w:])


def _small_copy(v_ref, land_ref, send_sem, recv_sem, r):
    x, y, c = _mesh_pos()
    px, py, pc = _peer(x, y, c, r)
    return pltpu.make_async_remote_copy(
        src_ref=v_ref, dst_ref=land_ref.at[4 * x + 2 * y + c], send_sem=send_sem, recv_sem=recv_sem,
        device_id=(px, py, pc), device_id_type=MESH)


def _small_gather_start(v):
    land = pltpu.with_memory_space_constraint(lax.empty((N_DEV,) + v.shape, v.dtype), pltpu.HBM)

    def body(v_ref, land_ref, *rest):
        sems, token = rest[:2 * (N_DEV - 1)], rest[-1]
        for r in range(1, N_DEV):
            _small_copy(v_ref, land_ref, sems[2 * r - 2], sems[2 * r - 1], r).start()
        token[...] = jnp.zeros_like(token)

    n_sem = 2 * (N_DEV - 1)
    res = pl.pallas_call(
        body, name="small_gather_start",
        out_shape=(*[pltpu.SemaphoreType.DMA(())] * n_sem, pltpu.HBM(v.shape, v.dtype),
                   pltpu.HBM(land.shape, land.dtype), jax.ShapeDtypeStruct((8, LANES), F32)),
        in_specs=[HBM_SPEC, HBM_SPEC], out_specs=(*[SEM_SPEC] * n_sem, HBM_SPEC, HBM_SPEC, VMEM_SPEC),
        input_output_aliases={0: n_sem, 1: n_sem + 1},
        compiler_params=pltpu.CompilerParams(has_side_effects=pltpu.SideEffectType.DATAFLOW_SIDE_EFFECTING),
    )(pltpu.with_memory_space_constraint(v, pltpu.HBM), land)
    return list(res[:n_sem]), res[n_sem], res[n_sem + 1], res[-1]


def _small_gather_wait(sems, v, land, after):
    def body(v_ref, land_ref, *rest):
        for r in range(1, N_DEV):
            cp = _small_copy(v_ref, land_ref, rest[2 * r - 2], rest[2 * r - 1], r)
            cp.wait_send()
            x, y, c = _mesh_pos()
            px, py, pc = _peer(x, y, c, r)
            pltpu.make_async_remote_copy(
                src_ref=v_ref, dst_ref=land_ref.at[4 * px + 2 * py + pc], send_sem=rest[2 * r - 2],
                recv_sem=rest[2 * r - 1], device_id=(px, py, pc), device_id_type=MESH).wait_recv()

    res = pl.pallas_call(
        body, name="small_gather_wait",
        out_shape=(pltpu.HBM(v.shape, v.dtype), pltpu.HBM(land.shape, land.dtype)),
        in_specs=[HBM_SPEC, HBM_SPEC] + [SEM_SPEC] * len(sems) + [ANY], out_specs=(HBM_SPEC, HBM_SPEC),
        input_output_aliases={0: 0, 1: 1},
        compiler_params=pltpu.CompilerParams(has_side_effects=pltpu.SideEffectType.DATAFLOW_SIDE_EFFECTING),
    )(v, land, *sems, after)
    return res[0], res[1]


def _ada_forward(c_row, w_ada, b_cols, carry):
    d = c_row.shape[1]
    wcols = w_ada.shape[1]
    ci, co = len(carry.ins), len(carry.out_shapes)

    def body(*refs):
        c_ref, w_ref, b_ref = refs[:3]
        cins = refs[3:3 + ci]
        sc_ref, mod_ref = refs[3 + ci:5 + ci]
        couts = refs[5 + ci:5 + ci + co]
        rows_ref, send_sems, recv_sems = refs[5 + ci + co:8 + ci + co]
        cscr = refs[8 + ci + co:]
        carry.start(cins, couts, cscr)
        x, y, c = _mesh_pos()
        me = 4 * x + 2 * y + c
        cv = c_ref[...]
        sc_ref[me] = cv * _sigmoid(cv)

        sends = []
        for r in range(1, N_DEV):
            px, py, pc = _peer(x, y, c, r)
            cp = pltpu.make_async_remote_copy(
                src_ref=sc_ref.at[me], dst_ref=sc_ref.at[me], send_sem=send_sems.at[0, r - 1],
                recv_sem=recv_sems.at[0, r - 1], device_id=(px, py, pc), device_id_type=MESH)
            cp.start()
            sends.append(cp)
        for r in range(1, N_DEV):
            px, py, pc = _peer(x, y, c, r)
            pid = 4 * px + 2 * py + pc
            pltpu.make_async_remote_copy(
                src_ref=sc_ref.at[pid], dst_ref=sc_ref.at[pid], send_sem=send_sems.at[0, r - 1],
                recv_sem=recv_sems.at[0, r - 1], device_id=(px, py, pc), device_id_type=MESH).wait_recv()
        for cp in sends:
            cp.wait_send()

        sc_all = jnp.concatenate([sc_ref[j] for j in range(N_DEV)], axis=0)
        rows = _dot(sc_all.astype(BF16), w_ref[...].astype(BF16)) + b_ref[...]
        for j in range(N_DEV):
            rows_ref[j] = rows[j:j + 1, :]
        mod_ref[me] = rows_ref[me]

        sends = []
        for r in range(1, N_DEV):
            px, py, pc = _peer(x, y, c, r)
            pid = 4 * px + 2 * py + pc
            cp = pltpu.make_async_remote_copy(
                src_ref=rows_ref.at[pid], dst_ref=mod_ref.at[me], send_sem=send_sems.at[1, r - 1],
                recv_sem=recv_sems.at[1, r - 1], device_id=(px, py, pc), device_id_type=MESH)
            cp.start()
            sends.append(cp)
        for r in range(1, N_DEV):
            px, py, pc = _peer(x, y, c, r)
            pid = 4 * px + 2 * py + pc
            pltpu.make_async_remote_copy(
                src_ref=rows_ref.at[pid], dst_ref=mod_ref.at[pid], send_sem=send_sems.at[1, r - 1],
                recv_sem=recv_sems.at[1, r - 1], device_id=(px, py, pc), device_id_type=MESH).wait_recv()
        for cp in sends:
            cp.wait_send()
        carry.finish(cins, couts, cscr)

    res = pl.pallas_call(
        body, name="ada_forward",
        out_shape=(jax.ShapeDtypeStruct((N_DEV, 1, d), F32), jax.ShapeDtypeStruct((N_DEV, 1, wcols), F32),
                   *carry.out_shapes),
        in_specs=[VMEM_SPEC, VMEM_SPEC, VMEM_SPEC] + [ANY] * ci, out_specs=(VMEM_SPEC, VMEM_SPEC) + (ANY,) * co,
        scratch_shapes=[pltpu.VMEM((N_DEV, 1, wcols), F32), pltpu.SemaphoreType.DMA((2, N_DEV - 1)),
                        pltpu.SemaphoreType.DMA((2, N_DEV - 1))] + carry.scratch,
        compiler_params=_params(),
    )(c_row, w_ada, b_cols, *carry.ins)
    return res[:2], res[2:]


def _mm_nt(a, b, name, out_dtype, bias=None, carry=None):
    m, k = a.shape
    n = b.shape[0]
    tm = _pick(m, (512, 256, 128))
    tn = _pick(n, (1408, 1152, 1024, 768, 512, 256, 128))

    def body(*refs):
        acc = _dot_nt(refs[0][...], refs[1][...])
        if bias is not None:
            acc = acc + refs[2][...]
        refs[-1][...] = acc.astype(out_dtype)

    in_specs = [pl.BlockSpec((tm, k), lambda j, i: (i, 0)), pl.BlockSpec((tn, k), lambda j, i: (j, 0))]
    args = [a, b]
    if bias is not None:
        in_specs.append(pl.BlockSpec((1, tn), lambda j, i: (0, j)))
        args.append(bias)
    return _call(body, name=name, grid=(n // tn, m // tm), in_specs=in_specs,
                 out_specs=pl.BlockSpec((tm, tn), lambda j, i: (i, j)),
                 out_shape=jax.ShapeDtypeStruct((m, n), out_dtype), args=args,
                 sem=("parallel", "parallel"), carry=carry)


class _Tail:
    def __init__(self, rows, vecs, outs, fn):
        self.rows, self.vecs, self.outs, self.fn = list(rows), list(vecs), list(outs), fn


def _mm_nn(pairs, name, out_dtype, bias=None, carry=None, tail=None):
    m, k = pairs[0][0].shape
    n = pairs[0][1].shape[1]
    n_p = len(pairs)
    tm = _pick(m, (512, 256, 128))
    tk = k if n_p == 1 else _pick(k, (1408, 1152, 1024, 768, 512, 256, 128))
    nk = k // tk
    n_b = 0 if bias is None else 1
    n_r, n_v = (len(tail.rows), len(tail.vecs)) if tail else (0, 0)
    n_in = 2 * n_p + n_b + n_r + n_v
    n_main = 0 if out_dtype is None else 1

    def finish(acc, refs, first_tile):
        if bias is not None:
            acc = acc + refs[2 * n_p][...]
        outs = refs[n_in:-1]
        if n_main:
            outs[0][...] = acc.astype(out_dtype)
        if tail is None:
            return
        rows = [r[...] for r in refs[2 * n_p + n_b:2 * n_p + n_b + n_r]]
        vecs = [v[...] for v in refs[2 * n_p + n_b + n_r:n_in]]
        vals = tail.fn(acc, rows, vecs)
        for ref, val, (dtype, kind) in zip(outs[n_main:], vals, tail.outs):
            if kind == "row":
                ref[...] = val.astype(dtype)
            else:
                @pl.when(first_tile)
                def _(ref=ref):
                    ref[...] = jnp.zeros_like(ref)

                ref[...] += val

    def body(*refs):
        acc_ref = refs[-1]
        kk, i = pl.program_id(0), pl.program_id(1)
        part = _dot(refs[0][...], refs[1][...])
        for p in range(1, n_p):
            part = part + _dot(refs[2 * p][...], refs[2 * p + 1][...])
        if nk == 1:
            finish(part, refs, i == 0)
            return
        rows = pl.ds(pl.multiple_of(i * tm, tm), tm)

        @pl.when(kk == 0)
        def _():
            acc_ref[rows, :] = part

        if nk > 2:
            @pl.when((kk > 0) & (kk < nk - 1))
            def _():
                acc_ref[rows, :] += part

        @pl.when(kk == nk - 1)
        def _():
            finish(acc_ref[rows, :] + part, refs, i == 0)

    def last_only(kk, i):
        return (jnp.where(kk == nk - 1, i, 0), 0)

    row_spec = pl.BlockSpec((tm, n), last_only)
    vec_spec = pl.BlockSpec((1, n), lambda kk, i: (0, 0))
    in_specs, args = [], []
    for a, b in pairs:
        in_specs += [pl.BlockSpec((tm, tk), lambda kk, i: (i, kk)), pl.BlockSpec((tk, n), lambda kk, i: (kk, 0))]
        args += [a, b]
    if bias is not None:
        in_specs.append(vec_spec)
        args.append(bias)
    out_specs = [row_spec] * n_main
    out_shape = [jax.ShapeDtypeStruct((m, n), out_dtype)] if n_main else []
    if tail:
        in_specs += [row_spec] * n_r + [vec_spec] * n_v
        args += tail.rows + tail.vecs
        for dtype, kind in tail.outs:
            if kind == "row":
                out_specs.append(row_spec)
                out_shape.append(jax.ShapeDtypeStruct((m, n), dtype))
            else:
                width = n if kind == "sum" else 1
                out_specs.append(pl.BlockSpec((1, width), lambda kk, i: (0, 0)))
                out_shape.append(jax.ShapeDtypeStruct((1, width), dtype))
    if tail is None:
        out_specs, out_shape = out_specs[0], out_shape[0]
    return _call(body, name=name, grid=(nk, m // tm), in_specs=in_specs, out_specs=out_specs,
                 out_shape=out_shape, args=args,
                 scratch=[pltpu.VMEM((m, n) if nk > 1 else (8, LANES), F32)],
                 sem=("arbitrary", "arbitrary"), carry=carry)


def _rms(v):
    return lax.rsqrt(jnp.mean(v * v, axis=-1, keepdims=True) + EPS)


def _col(v):
    return jnp.sum(v, axis=0, keepdims=True)


def _tail_post_pre(x, g_post, gate, weight, g_pre, scale, shift):
    def fn(y, rows, vecs):
        (xv,), (gp, gt, g, sc, sh) = rows, vecs
        xo = xv + (weight * gt) * ((y * _rms(y)) * gp)
        return xo, ((xo * _rms(xo)) * g) * (1.0 + sc) + sh

    return _Tail([x], [g_post, gate, g_pre, scale, shift], [(F32, "row"), (BF16, "row")], fn)


def _tail_post_loss(x, target, g, gate, weight):
    def fn(y, rows, vecs):
        (xv, tv), (gv, gt) = rows, vecs
        r = _rms(y)
        yn = y * r
        err = (xv + (weight * gt) * (yn * gv)) - tv
        do = err * (1.0 / y.shape[1])
        dyn = do * ((weight * gt) * gv)
        dy = r * (dyn - yn * jnp.mean(dyn * yn, axis=-1, keepdims=True))
        return do, dy, 0.5 * _col(jnp.mean(err * err, axis=-1, keepdims=True)), _col(do * yn)

    return _Tail([x, target], [g, gate], [(F32, "row"), (BF16, "row"), (F32, "one"), (F32, "sum")], fn)


def _tail_pre_bwd(x, dres, g_pre, scale):
    def fn(dh, rows, vecs):
        (xv, dr), (g, sc) = rows, vecs
        r = _rms(xv)
        n = xv * r
        dn = dh * (g * (1.0 + sc))
        return dr + r * (dn - n * jnp.mean(dn * n, axis=-1, keepdims=True)), _col(dh * n), _col(dh)

    return _Tail([x, dres], [g_pre, scale], [(F32, "row"), (F32, "sum"), (F32, "sum")], fn)


def _tail_pre_post_bwd(x, dres, y, g_pre, scale, g_post, gate, weight):
    def fn(dh, rows, vecs):
        (xv, dr, yv), (g, sc, gp, gt) = rows, vecs
        r = _rms(xv)
        n = xv * r
        dn = dh * (g * (1.0 + sc))
        dx = dr + r * (dn - n * jnp.mean(dn * n, axis=-1, keepdims=True))
        ry = _rms(yv)
        yn = yv * ry
        dyn = dx * ((weight * gt) * gp)
        dy = ry * (dyn - yn * jnp.mean(dyn * yn, axis=-1, keepdims=True))
        return dx, dy, _col(dh * n), _col(dh), _col(dx * yn), _col(dy)

    return _Tail([x, dres, y], [g_pre, scale, g_post, gate],
                 [(F32, "row"), (BF16, "row")] + [(F32, "sum")] * 4, fn)


def _mm_tn_pair(a, b, name, col_sums=False):
    k, m = a.shape
    n = b.shape[1]
    rows = m // N_DEV
    n_chip = N_DEV // 2
    tm = 4 * rows
    tk = _pick(k, (1024, 512, 256, 128))
    nk = k // tk

    def body(a_ref, b_ref, p_ref, own_ref, *rest):
        acc_ref, keep_ref, send_ref, land_ref, send_sems, recv_sems = rest[-6:]
        i, kk = pl.program_id(0), pl.program_id(1)
        x, y, c = _mesh_pos()
        if col_sums:
            cs_ref = rest[0]
            part = jnp.sum(a_ref[...].astype(F32), axis=0, keepdims=True)

            @pl.when(kk == 0)
            def _():
                cs_ref[...] = part

            @pl.when(kk > 0)
            def _():
                cs_ref[...] += part

        def push(chip):
            return pltpu.make_async_remote_copy(
                src_ref=send_ref.at[chip], dst_ref=land_ref.at[chip], send_sem=send_sems.at[chip],
                recv_sem=recv_sems.at[chip], device_id=(x, y, 1 - c), device_id_type=MESH)

        if nk == 1:
            acc = _dot_tn(a_ref[...], b_ref[...])
        else:
            @pl.when(kk == 0)
            def _():
                acc_ref[...] = jnp.zeros_like(acc_ref)

            acc_ref[...] += _dot_tn(a_ref[...], b_ref[...])
            acc = acc_ref

        for t in range(2):
            @pl.when((kk == nk - 1) & (i == t))
            def _(t=t):
                for ob in range(4):
                    chip, core = 2 * t + ob // 2, ob % 2
                    blk = acc[ob * rows:(ob + 1) * rows, :]

                    @pl.when(c == core)
                    def _(chip=chip, blk=blk):
                        keep_ref[chip] = blk

                    @pl.when(c != core)
                    def _(chip=chip, blk=blk):
                        send_ref[chip] = blk.astype(BF16)
                        push(chip).start()

        @pl.when((kk == nk - 1) & (i == 1))
        def _():
            for chip in range(n_chip):
                push(chip).wait_recv()
                val = (keep_ref[chip] + land_ref[chip].astype(F32)).astype(BF16)
                p_ref[chip * rows:(chip + 1) * rows, :] = val

                @pl.when(2 * x + y == chip)
                def _(val=val):
                    own_ref[...] = val

            for chip in range(n_chip):
                push(chip).wait_send()

    out_specs = [pl.BlockSpec((n_chip * rows, n), lambda i, kk: (0, 0)), pl.BlockSpec((rows, n), lambda i, kk: (0, 0))]
    out_shape = [jax.ShapeDtypeStruct((n_chip * rows, n), BF16), jax.ShapeDtypeStruct((rows, n), BF16)]
    if col_sums:
        out_specs.append(pl.BlockSpec((1, tm), lambda i, kk: (0, i)))
        out_shape.append(jax.ShapeDtypeStruct((1, m), F32))
    return _call(body, name=name, grid=(2, nk),
                 in_specs=[pl.BlockSpec((tk, tm), lambda i, kk: (kk, i)), pl.BlockSpec((tk, n), lambda i, kk: (kk, 0))],
                 out_specs=out_specs, out_shape=out_shape, args=[a, b],
                 scratch=[pltpu.VMEM((tm, n) if nk > 1 else (8, LANES), F32), pltpu.VMEM((n_chip, rows, n), F32),
                          pltpu.VMEM((n_chip, rows, n), BF16), pltpu.VMEM((n_chip, rows, n), BF16),
                          pltpu.SemaphoreType.DMA((n_chip,)), pltpu.SemaphoreType.DMA((n_chip,))],
                 sem=("arbitrary", "arbitrary"))


def _ffn_up(h, wg_t, wu_t, name, carry=None):
    s, d = h.shape
    f = wg_t.shape[0]
    tm = _pick(s, (512, 256, 128))
    tf = _pick(f, (1408, 1024, 512, 256, 128))

    def body(h_ref, wg_ref, wu_ref, a_ref, b_ref, u_ref):
        hh = h_ref[...]
        for lo, hi in _pieces(tf):
            a = _dot_nt(hh, wg_ref[lo:hi, :])
            b = _dot_nt(hh, wu_ref[lo:hi, :])
            a_ref[:, lo:hi] = a.astype(BF16)
            b_ref[:, lo:hi] = b.astype(BF16)
            u_ref[:, lo:hi] = ((a * _sigmoid(a)) * b).astype(BF16)

    w_spec = pl.BlockSpec((tf, d), lambda j, i: (j, 0))
    o_spec = pl.BlockSpec((tm, tf), lambda j, i: (i, j))
    o_shape = jax.ShapeDtypeStruct((s, f), BF16)
    return _call(body, name=name, grid=(f // tf, s // tm),
                 in_specs=[pl.BlockSpec((tm, d), lambda j, i: (i, 0)), w_spec, w_spec],
                 out_specs=(o_spec, o_spec, o_spec), out_shape=(o_shape, o_shape, o_shape),
                 args=[h, wg_t, wu_t], sem=("parallel", "parallel"), carry=carry)


def _ffn_down_bwd(dy, wd, a, b, name, carry=None):
    s, d = dy.shape
    f = wd.shape[0]
    tm = _pick(s, (512, 256, 128))
    tf = _pick(f, (1408, 1024, 512, 256, 128))

    def body(dy_ref, wd_ref, a_ref, b_ref, da_ref, db_ref):
        dyv = dy_ref[...]
        for lo, hi in _pieces(tf):
            du = _dot_nt(dyv, wd_ref[lo:hi, :])
            a = a_ref[:, lo:hi].astype(F32)
            b = b_ref[:, lo:hi].astype(F32)
            sig = _sigmoid(a)
            da_ref[:, lo:hi] = (du * b * (sig * (1.0 + a * (1.0 - sig)))).astype(BF16)
            db_ref[:, lo:hi] = (du * (a * sig)).astype(BF16)

    t_spec = pl.BlockSpec((tm, tf), lambda j, i: (i, j))
    o_shape = jax.ShapeDtypeStruct((s, f), BF16)
    return _call(body, name=name, grid=(f // tf, s // tm),
                 in_specs=[pl.BlockSpec((tm, d), lambda j, i: (i, 0)), pl.BlockSpec((tf, d), lambda j, i: (j, 0)),
                           t_spec, t_spec],
                 out_specs=(t_spec, t_spec), out_shape=(o_shape, o_shape), args=[dy, wd, a, b],
                 sem=("parallel", "parallel"), carry=carry)


def _row_tile(s):
    return _pick(s, (256, 128, 64))


def _vec_spec(d):
    return pl.BlockSpec((1, d), lambda i: (0, 0))


def _pre_norm(x, g, scale, shift, name):
    s, d = x.shape
    ts = _row_tile(s)

    def body(x_ref, g_ref, sc_ref, sh_ref, h_ref):
        xv = x_ref[...]
        r = lax.rsqrt(jnp.mean(xv * xv, axis=-1, keepdims=True) + EPS)
        h_ref[...] = (((xv * r) * g_ref[...]) * (1.0 + sc_ref[...]) + sh_ref[...]).astype(BF16)

    row = pl.BlockSpec((ts, d), lambda i: (i, 0))
    return _call(body, name=name, grid=(s // ts,), in_specs=[row, _vec_spec(d), _vec_spec(d), _vec_spec(d)],
                 out_specs=row, out_shape=jax.ShapeDtypeStruct((s, d), BF16), args=[x, g, scale, shift],
                 sem=("parallel",))


def _group_norm_cat(oa, ob, ga, gb):
    s = oa.shape[0]
    ts = _row_tile(s)

    def body(oa_ref, ob_ref, ga_ref, gb_ref, y_ref):
        for o_ref, g_ref, lo, w in ((oa_ref, ga_ref, 0, QA), (ob_ref, gb_ref, QA, QB)):
            ov = o_ref[...]
            r = lax.rsqrt(jnp.mean(ov * ov, axis=-1, keepdims=True) + EPS)
            y_ref[:, lo:lo + w] = ((ov * r) * g_ref[...]).astype(BF16)

    return _call(body, name="group_norm_cat", grid=(s // ts,),
                 in_specs=[pl.BlockSpec((ts, QA), lambda i: (i, 0)), pl.BlockSpec((ts, QB), lambda i: (i, 0)),
                           _vec_spec(QA), _vec_spec(QB)],
                 out_specs=pl.BlockSpec((ts, QA + QB), lambda i: (i, 0)),
                 out_shape=jax.ShapeDtypeStruct((s, QA + QB), BF16), args=[oa, ob, ga, gb], sem=("parallel",))


def _group_norm_bwd(dy, oa, ob, ga, gb):
    s = oa.shape[0]
    ts = _row_tile(s)

    def body(dy_ref, oa_ref, ob_ref, ga_ref, gb_ref, doa_ref, dob_ref, dga_ref, dgb_ref):
        @pl.when(pl.program_id(0) == 0)
        def _():
            dga_ref[...] = jnp.zeros_like(dga_ref)
            dgb_ref[...] = jnp.zeros_like(dgb_ref)

        for o_ref, g_ref, do_ref, dg_ref, lo, w in ((oa_ref, ga_ref, doa_ref, dga_ref, 0, QA),
                                                    (ob_ref, gb_ref, dob_ref, dgb_ref, QA, QB)):
            ov = o_ref[...]
            dyv = dy_ref[:, lo:lo + w]
            r = lax.rsqrt(jnp.mean(ov * ov, axis=-1, keepdims=True) + EPS)
            n = ov * r
            dn = dyv * g_ref[...]
            do_ref[...] = r * (dn - n * jnp.mean(dn * n, axis=-1, keepdims=True))
            dg_ref[...] += jnp.sum(dyv * n, axis=0, keepdims=True)

    ra = pl.BlockSpec((ts, QA), lambda i: (i, 0))
    rb = pl.BlockSpec((ts, QB), lambda i: (i, 0))
    return _call(body, name="group_norm_bwd", grid=(s // ts,),
                 in_specs=[pl.BlockSpec((ts, QA + QB), lambda i: (i, 0)), ra, rb, _vec_spec(QA), _vec_spec(QB)],
                 out_specs=(ra, rb, _vec_spec(QA), _vec_spec(QB)),
                 out_shape=(jax.ShapeDtypeStruct((s, QA), F32), jax.ShapeDtypeStruct((s, QB), F32),
                            jax.ShapeDtypeStruct((1, QA), F32), jax.ShapeDtypeStruct((1, QB), F32)),
                 args=[dy, oa, ob, ga, gb], sem=("arbitrary",))


def _n_variants(n_back):
    return -(-n_back // QG) + 1


def _alibi_bias():
    i = np.arange(QROWS)[:, None]
    j = np.arange((QG + BACK_A) * CHUNK)[None, :]
    dist = np.abs(BACK_A * CHUNK + i - j).astype(np.float32)
    dc = j // CHUNK - i // CHUNK
    valid = (dc >= 0) & (dc <= BACK_A)
    slopes = np.array([2.0 ** (-8.0 * (h + 1) / H_A) for h in range(H_A)], dtype=np.float32)
    bias = -slopes[:, None, None] * dist[None]
    out = [np.where((valid & (j >= (BACK_A - QG * v) * CHUNK))[None], bias, np.float32(NEG_INF))
           for v in range(_n_variants(BACK_A))]
    return jnp.asarray(np.stack(out).astype(np.float32))


def _rel_index_matrix():
    cc = np.arange(SKEW)
    dist = np.where(cc < SKEW - QROWS, BACK_B * CHUNK - cc, BACK_B * CHUNK + SKEW - cc)
    idx = np.clip(dist, -REL_CLIP, REL_CLIP) + REL_CLIP
    m = np.zeros((SKEW, N_REL), np.float32)
    m[cc, idx] = 1.0
    return jnp.asarray(m)


def _toeplitz_bias(vec, carry=None):
    lk = (QG + BACK_B) * CHUNK
    nv = _n_variants(BACK_B)

    def body(v_ref, o_ref):
        xv = jnp.broadcast_to(v_ref[0], (QROWS, SKEW))
        row = lax.broadcasted_iota(jnp.int32, (QROWS, SKEW), 0)
        for bit in range(QROWS.bit_length() - 1):
            xv = jnp.where((row >> bit) & 1 == 1, pltpu.roll(xv, 1 << bit, 1), xv)
        ri = lax.broadcasted_iota(jnp.int32, (QROWS, lk), 0) // CHUNK
        col = lax.broadcasted_iota(jnp.int32, (QROWS, lk), 1)
        ci = col // CHUNK
        valid = (ci - ri >= 0) & (ci - ri <= BACK_B)
        for v in range(nv):
            o_ref[v, 0] = jnp.where(valid & (col >= (BACK_B - QG * v) * CHUNK), xv[:, :lk], NEG_INF)

    return _call(body, name="toeplitz_bias", grid=(H_B,),
                 in_specs=[pl.BlockSpec((1, 1, SKEW), lambda h: (h, 0, 0))],
                 out_specs=pl.BlockSpec((nv, 1, QROWS, lk), lambda h: (0, h, 0, 0)),
                 out_shape=jax.ShapeDtypeStruct((nv, H_B, QROWS, lk), F32), args=[vec], sem=("parallel",),
                 carry=carry)


def _diagonal_sums(dbias):
    lk = dbias.shape[2]

    def body(d_ref, o_ref):
        xp = jnp.concatenate([d_ref[0], jnp.zeros((QROWS, SKEW - lk), F32)], axis=1)
        xv = xp[0:CHUNK]
        for q in range(1, QG):
            xv = xv + pltpu.roll(xp[q * CHUNK:(q + 1) * CHUNK], SKEW - q * CHUNK, 1)
        row = lax.broadcasted_iota(jnp.int32, (CHUNK, SKEW), 0)
        for bit in range(CHUNK.bit_length() - 1):
            xv = jnp.where((row >> bit) & 1 == 1, pltpu.roll(xv, SKEW - (1 << bit), 1), xv)
        o_ref[0] = jnp.sum(xv, axis=0, keepdims=True)

    return _call(body, name="diagonal_sums", grid=(H_B,),
                 in_specs=[pl.BlockSpec((1, QROWS, lk), lambda h: (h, 0, 0))],
                 out_specs=pl.BlockSpec((1, 1, SKEW), lambda h: (h, 0, 0)),
                 out_shape=jax.ShapeDtypeStruct((H_B, 1, SKEW), F32), args=[dbias], sem=("parallel",))


def _attn_common(s, n_back, gqa, q_col, k_col, v_col, TPS):
    assert q_col % TPS == 0 and (gqa or (k_col % TPS == 0 and v_col % TPS == 0)), "blocks of TPS lane tiles"
    lk = (QG + n_back) * CHUNK
    pad = n_back * CHUNK
    wide = TPS * LANES
    q_spec = pl.BlockSpec((QROWS, wide), lambda t, g: (g, q_col // TPS + t))
    if gqa:
        k_spec = pl.BlockSpec((s, LANES), lambda t, g: (0, k_col))
        v_spec = pl.BlockSpec((s, LANES), lambda t, g: (0, v_col))
    else:
        k_spec = pl.BlockSpec((s, wide), lambda t, g: (0, k_col // TPS + t))
        v_spec = pl.BlockSpec((s, wide), lambda t, g: (0, v_col // TPS + t))
    last_variant = _n_variants(n_back) - 1
    bias_spec = pl.BlockSpec((None, 2 * TPS, QROWS, lk), lambda t, g: (jnp.minimum(g, last_variant), t, 0, 0))
    tile_spec = pl.BlockSpec((QROWS, wide), lambda t, g: (g, t))
    return lk, pad, q_spec, k_spec, v_spec, bias_spec, tile_spec


def _attention_fwd(proj, bias, sinks, *, n_back, gqa, q_col, k_col, v_col, TPS, name, carry=None):
    s = proj.shape[0]
    lk, pad, q_spec, k_spec, v_spec, bias_spec, tile_spec = _attn_common(s, n_back, gqa, q_col, k_col, v_col, TPS)
    n_t, n_g = 512 // (TPS * LANES), s // QROWS
    kv_wide = LANES if gqa else TPS * LANES

    def body(*refs):
        if gqa:
            q_ref, k_ref, v_ref, bias_ref, sink_ref, o_ref, l_ref, kpad, vpad = refs
        else:
            q_ref, k_ref, v_ref, bias_ref, o_ref, l_ref, kpad, vpad = refs
        t, g = pl.program_id(0), pl.program_id(1)

        @pl.when(g == 0)
        def _():
            kpad[0:pad, :] = jnp.zeros((pad, kv_wide), BF16)
            vpad[0:pad, :] = jnp.zeros((pad, kv_wide), BF16)
            kpad[pad:, :] = k_ref[...]
            vpad[pad:, :] = v_ref[...]

        start = pl.multiple_of(g * QROWS, QROWS)
        half = lax.broadcasted_iota(jnp.int32, (QROWS, LANES), 1) // HEAD_DIM
        for tt in range(TPS):
            lanes = slice(tt * LANES, (tt + 1) * LANES)
            kv_lanes = slice(0, LANES) if gqa else lanes
            kb = kpad[pl.ds(start, lk), kv_lanes]
            vb = vpad[pl.ds(start, lk), kv_lanes]
            q = q_ref[:, lanes] * (HEAD_DIM ** -0.5)
            if gqa:
                hk = (TPS * t + tt) // 2
                q_rolled = pltpu.roll(q.astype(F32), HEAD_DIM, 1).astype(BF16)
            outs, lses = [], []
            for e in range(2):
                if gqa:
                    kv_half = hk
                    src = jnp.where(hk == e, q, q_rolled)
                else:
                    kv_half = e
                    src = q
                qm = jnp.where(half == kv_half, src, jnp.zeros_like(src))
                sc = _dot_nt(qm, kb) + bias_ref[2 * tt + e]
                m = jnp.max(sc, axis=-1, keepdims=True)
                if gqa:
                    sk = sink_ref[2 * (TPS * t + tt) + e]
                    m = jnp.maximum(m, sk)
                p = jnp.exp(sc - m)
                l = jnp.sum(p, axis=-1, keepdims=True)
                if gqa:
                    l = l + jnp.exp(sk - m)
                pn = p / l
                outs.append(_dot(pn.astype(BF16), vb))
                lses.append(m + jnp.log(l))
            if gqa:
                same = jnp.where(hk == 0, outs[0], outs[1])
                other = jnp.where(hk == 0, outs[1], outs[0])
                o_ref[:, lanes] = jnp.where(half == hk, same, pltpu.roll(other, HEAD_DIM, 1))
            else:
                o_ref[:, lanes] = jnp.where(half == 0, outs[0], outs[1])
            l_ref[:, lanes] = jnp.where(half == 0, lses[0], lses[1])

    in_specs = [q_spec, k_spec, v_spec, bias_spec] + ([SMEM_SPEC] if gqa else [])
    args = [proj, proj, proj, bias] + ([sinks] if gqa else [])
    o_shape = jax.ShapeDtypeStruct((s, 512), F32)
    return _call(body, name=name, grid=(n_t, n_g), in_specs=in_specs, out_specs=(tile_spec, tile_spec),
                 out_shape=(o_shape, o_shape), args=args,
                 scratch=[pltpu.VMEM((s + pad, kv_wide), BF16), pltpu.VMEM((s + pad, kv_wide), BF16)],
                 sem=("arbitrary", "arbitrary"), carry=carry)


def _attention_bwd(proj, bias, sinks, do, lse, *, n_back, gqa, q_col, k_col, v_col, TPS, name, carry=None):
    s = proj.shape[0]
    lk, pad, q_spec, k_spec, v_spec, bias_spec, tile_spec = _attn_common(s, n_back, gqa, q_col, k_col, v_col, TPS)
    n_t, n_g = 512 // (TPS * LANES), s // QROWS
    kv_wide = LANES if gqa else TPS * LANES

    def body(*refs):
        if gqa:
            (q_ref, k_ref, v_ref, bias_ref, sink_ref, do_ref, l_ref,
             dq_ref, dk_ref, dv_ref, dsink_ref, kpad, vpad, dkpad, dvpad) = refs
        else:
            (q_ref, k_ref, v_ref, bias_ref, do_ref, l_ref,
             dq_ref, dk_ref, dv_ref, dbias_ref, kpad, vpad, dkpad, dvpad) = refs
        t, g = pl.program_id(0), pl.program_id(1)

        @pl.when(g == 0)
        def _():
            kpad[0:pad, :] = jnp.zeros((pad, kv_wide), BF16)
            vpad[0:pad, :] = jnp.zeros((pad, kv_wide), BF16)
            kpad[pad:, :] = k_ref[...]
            vpad[pad:, :] = v_ref[...]
            if gqa:
                dsink_ref[...] = jnp.zeros_like(dsink_ref)
            else:
                dbias_ref[...] = jnp.zeros_like(dbias_ref)

        @pl.when((g == 0) & (t == 0) if gqa else g == 0)
        def _():
            dkpad[...] = jnp.zeros_like(dkpad)
            dvpad[...] = jnp.zeros_like(dvpad)

        start = pl.multiple_of(g * QROWS, QROWS)
        half = lax.broadcasted_iota(jnp.int32, (QROWS, LANES), 1) // HEAD_DIM
        for tt in range(TPS):
            lanes = slice(tt * LANES, (tt + 1) * LANES)
            kv_lanes = slice(0, LANES) if gqa else lanes
            kb = kpad[pl.ds(start, lk), kv_lanes]
            vb = vpad[pl.ds(start, lk), kv_lanes]
            q = q_ref[:, lanes]
            dov = do_ref[:, lanes]
            lv = l_ref[:, lanes]
            if gqa:
                hk = (TPS * t + tt) // 2
                q_rolled = pltpu.roll(q.astype(F32), HEAD_DIM, 1).astype(BF16)
                do_rolled = pltpu.roll(dov, HEAD_DIM, 1)
            dqs = []
            dk_acc = jnp.zeros((lk, LANES), F32)
            dv_acc = jnp.zeros((lk, LANES), F32)
            for e in range(2):
                if gqa:
                    kv_half = hk
                    src = jnp.where(hk == e, q, q_rolled)
                    do_src = jnp.where(hk == e, dov, do_rolled)
                else:
                    kv_half = e
                    src = q
                    do_src = dov
                qm = jnp.where(half == kv_half, src, jnp.zeros_like(src))
                dom = jnp.where(half == kv_half, do_src, 0.0).astype(BF16)
                lcol = jnp.max(jnp.where(half == e, lv, -jnp.inf), axis=-1, keepdims=True)
                sc = _dot_nt(qm * (HEAD_DIM ** -0.5), kb) + bias_ref[2 * tt + e]
                pn = jnp.exp(sc - lcol)
                dp = _dot_nt(dom, vb)
                delta = jnp.sum(pn * dp, axis=-1, keepdims=True)
                ds = pn * (dp - delta)
                if gqa:
                    p_sink = jnp.exp(sink_ref[2 * (TPS * t + tt) + e] - lcol)
                    dsk = -jnp.sum(p_sink * delta, axis=0, keepdims=True)
                    row = 2 * tt + e
                    dsink_ref[0, row:row + 1, :] += jnp.broadcast_to(dsk, (1, LANES))
                else:
                    dbias_ref[2 * tt + e] += ds
                dsb = (ds * (HEAD_DIM ** -0.5)).astype(BF16)
                dqs.append(_dot(dsb, kb))
                dk_acc = dk_acc + _dot_tn(dsb, qm)
                dv_acc = dv_acc + _dot_tn(pn.astype(BF16), dom)
            dkpad[pl.ds(start, lk), kv_lanes] += dk_acc
            dvpad[pl.ds(start, lk), kv_lanes] += dv_acc
            if gqa:
                same = jnp.where(hk == 0, dqs[0], dqs[1])
                other = jnp.where(hk == 0, dqs[1], dqs[0])
                dq_ref[:, lanes] = jnp.where(half == hk, same, pltpu.roll(other, HEAD_DIM, 1)).astype(BF16)
            else:
                dq_ref[:, lanes] = jnp.where(half == 0, dqs[0], dqs[1]).astype(BF16)

        @pl.when((g == n_g - 1) & (t == n_t - 1) if gqa else g == n_g - 1)
        def _():
            dk_ref[...] = dkpad[pad:, :].astype(BF16)
            dv_ref[...] = dvpad[pad:, :].astype(BF16)

    in_specs = [q_spec, k_spec, v_spec, bias_spec] + ([SMEM_SPEC] if gqa else []) + [tile_spec, tile_spec]
    args = [proj, proj, proj, bias] + ([sinks] if gqa else []) + [do, lse]
    if gqa:
        kv_out = pl.BlockSpec((s, LANES), lambda t, g: (0, 0))
        kv_shape = jax.ShapeDtypeStruct((s, LANES), BF16)
        extra_spec = pl.BlockSpec((1, 8, LANES), lambda t, g: (t, 0, 0))
        extra_shape = jax.ShapeDtypeStruct((n_t, 8, LANES), F32)
    else:
        kv_out = pl.BlockSpec((s, kv_wide), lambda t, g: (0, t))
        kv_shape = jax.ShapeDtypeStruct((s, 512), BF16)
        extra_spec = pl.BlockSpec((2 * TPS, QROWS, lk), lambda t, g: (t, 0, 0))
        extra_shape = jax.ShapeDtypeStruct(bias.shape[1:], F32)
    return _call(body, name=name, grid=(n_t, n_g), in_specs=in_specs,
                 out_specs=(tile_spec, kv_out, kv_out, extra_spec),
                 out_shape=(jax.ShapeDtypeStruct((s, 512), BF16), kv_shape, kv_shape, extra_shape), args=args,
                 scratch=[pltpu.VMEM((s + pad, kv_wide), BF16), pltpu.VMEM((s + pad, kv_wide), BF16),
                          pltpu.VMEM((s + pad, kv_wide), F32), pltpu.VMEM((s + pad, kv_wide), F32)],
                 sem=("arbitrary", "arbitrary"), carry=carry)


def _sum_rows8(g):
    n = g.shape[2]

    def body(g_ref, o_ref):
        acc = g_ref[0]
        for j in range(1, N_DEV):
            acc = acc + g_ref[j]
        o_ref[...] = acc

    return pl.pallas_call(
        body, name="sum_small_grads", in_specs=[VMEM_SPEC], out_specs=VMEM_SPEC,
        out_shape=jax.ShapeDtypeStruct((1, n), F32), compiler_params=_params(),
    )(g)


def _ada_weight_grad(sc_t, dmod_cols):
    d = sc_t.shape[0]
    w = dmod_cols.shape[1]
    td = _pick(d, (256, 128))

    def body(sc_ref, dm_ref, o_ref):
        scv = sc_ref[...]
        dmv = dm_ref[...]
        acc = scv[:, 0:1] * dmv[0:1, :]
        for b in range(1, N_DEV):
            acc = acc + scv[:, b:b + 1] * dmv[b:b + 1, :]
        o_ref[...] = acc

    return _call(body, name="ada_weight_grad", grid=(d // td,),
                 in_specs=[pl.BlockSpec((td, N_DEV), lambda i: (i, 0)), pl.BlockSpec((N_DEV, w), lambda i: (0, 0))],
                 out_specs=pl.BlockSpec((td, w), lambda i: (i, 0)), out_shape=jax.ShapeDtypeStruct((d, w), F32),
                 args=[sc_t, dmod_cols], sem=("parallel",))


def _adamw_update(w, gv, m, v):
    nm = ADAM_B1 * m + (1.0 - ADAM_B1) * gv
    nv = ADAM_B2 * v + (1.0 - ADAM_B2) * (gv * gv)
    m_hat = nm / (1.0 - ADAM_B1 ** ADAM_STEP)
    v_hat = nv / (1.0 - ADAM_B2 ** ADAM_STEP)
    return -ADAM_LR * (m_hat / (jnp.sqrt(v_hat) + ADAM_EPS) + ADAM_WD * w), nm, nv


def _adamw(w, g, m, v, name):
    rows, cols = w.shape
    tr = _pick(rows, (256, 176, 128, 88, 64)) if rows > 256 else rows

    def body(w_ref, g_ref, m_ref, v_ref, d_ref, nm_ref, nv_ref):
        d_ref[...], nm_ref[...], nv_ref[...] = _adamw_update(w_ref[...], g_ref[...], m_ref[...], v_ref[...])

    spec = pl.BlockSpec((tr, cols), lambda i: (i, 0))
    shape = jax.ShapeDtypeStruct((rows, cols), F32)
    return _call(body, name=name, grid=(rows // tr,), in_specs=[spec] * 4, out_specs=(spec, spec, spec),
                 out_shape=(shape, shape, shape), args=[w, g, m, v], sem=("parallel",))


def _adamw_from_slots(w, own, slots, m, v, name):
    n_slots, rows, k = slots.shape

    def body(o_ref, s_ref, w_ref, m_ref, v_ref, g_ref, d_ref, nm_ref, nv_ref):
        gv = o_ref[...].astype(F32)
        for j in range(n_slots):
            gv = gv + s_ref[j].astype(F32)
        g_ref[...] = gv
        d_ref[...], nm_ref[...], nv_ref[...] = _adamw_update(w_ref[...], gv, m_ref[...], v_ref[...])

    tr = rows // 2 if rows % 32 == 0 else rows
    spec = pl.BlockSpec((tr, k), lambda i: (i, 0))
    shape = jax.ShapeDtypeStruct((rows, k), F32)
    return _call(body, name=name, grid=(rows // tr,),
                 in_specs=[spec, pl.BlockSpec((n_slots, tr, k), lambda i: (0, i, 0)), spec, spec, spec],
                 out_specs=(spec, spec, spec, spec), out_shape=(shape, shape, shape, shape),
                 args=[own, slots, w, m, v], sem=("parallel",))


def _adamw_small(g, w, m, v, sizes):
    n = w.shape[1]
    offs, off = [], 0
    for size in sizes:
        offs.append(off)
        off += size + (-size % LANES)

    def body(g_ref, w_ref, m_ref, v_ref, *out_refs):
        gv = g_ref[:, 0:n]
        dv, nm, nv = _adamw_update(w_ref[...], gv, m_ref[...], v_ref[...])
        for j, (o, size) in enumerate(zip(offs, sizes)):
            for k, val in enumerate((gv, dv, nm, nv)):
                out_refs[4 * j + k][...] = val[:, o:o + size]

    shapes = [jax.ShapeDtypeStruct((1, size), F32) for size in sizes for _ in range(4)]
    return pl.pallas_call(
        body, name="adamw_small", in_specs=[VMEM_SPEC] * 4, out_specs=tuple([VMEM_SPEC] * len(shapes)),
        out_shape=tuple(shapes), compiler_params=_params(),
    )(g, w, m, v)


SMALL = ("b_ada", "g_pre_ffn1", "g_post_ffn1", "g_pre_mix", "b_in", "sinks_a", "rel_bias_b", "g_grp_a",
         "g_grp_b", "b_out", "g_post_mix", "g_pre_ffn2", "g_post_ffn2")
WEIGHTS = ("w_ada", "b_ada", "g_pre_ffn1", "w_gate1", "w_up1", "w_down1", "g_post_ffn1", "g_pre_mix", "w_in",
           "b_in", "sinks_a", "rel_bias_b", "g_grp_a", "g_grp_b", "w_out", "b_out", "g_post_mix", "g_pre_ffn2",
           "w_gate2", "w_up2", "w_down2", "g_post_ffn2")


def kernel(x, c, w_ada, b_ada, g_pre_ffn1, w_gate1, w_up1, w_down1, g_post_ffn1, g_pre_mix, w_in, b_in, sinks_a, rel_bias_b, g_grp_a, g_grp_b, w_out, b_out, g_post_mix, g_pre_ffn2, w_gate2, w_up2, w_down2, g_post_ffn2, loss_target, m_w_ada, m_b_ada, m_g_pre_ffn1, m_w_gate1, m_w_up1, m_w_down1, m_g_post_ffn1, m_g_pre_mix, m_w_in, m_b_in, m_sinks_a, m_rel_bias_b, m_g_grp_a, m_g_grp_b, m_w_out, m_b_out, m_g_post_mix, m_g_pre_ffn2, m_w_gate2, m_w_up2, m_w_down2, m_g_post_ffn2, v_w_ada, v_b_ada, v_g_pre_ffn1, v_w_gate1, v_w_up1, v_w_down1, v_g_post_ffn1, v_g_pre_mix, v_w_in, v_b_in, v_sinks_a, v_rel_bias_b, v_g_grp_a, v_g_grp_b, v_w_out, v_b_out, v_g_post_mix, v_g_pre_ffn2, v_w_gate2, v_w_up2, v_w_down2, v_g_post_ffn2):
    given = dict(locals())
    weights = {n: given[n] for n in WEIGHTS}
    mom_m = {n: given["m_" + n] for n in WEIGHTS}
    mom_v = {n: given["v_" + n] for n in WEIGHTS}

    me = 4 * lax.axis_index("x") + 2 * lax.axis_index("y") + lax.axis_index("c")
    xs = x[0]
    tgt = loss_target[0]
    d_model = xs.shape[1]
    ada_cols = w_ada.shape[2]

    sh = {"wg1": w_gate1[0].T, "wu1": w_up1[0].T, "wd1": w_down1[0], "win": w_in[0].T, "wo": w_out[0],
          "wg2": w_gate2[0].T, "wu2": w_up2[0].T, "wd2": w_down2[0]}
    sh = {k: v.astype(BF16) for k, v in sh.items()}

    def gather(*names):
        return _gather_carry([sh[n] for n in names])

    bias_a = _alibi_bias()
    rel_m = _rel_index_matrix()
    rel_vec = jnp.dot(rel_bias_b[0], rel_m.T, precision=lax.Precision.HIGHEST)
    bias_b, (wg1, wu1) = _toeplitz_bias(rel_vec.reshape(H_B, 1, SKEW), carry=gather("wg1", "wu1"))

    b_cols = lax.dynamic_slice(b_ada, (0, me * ada_cols), (1, ada_cols))
    (sc_all, mod_rows), _ = _ada_forward(c, w_ada[0], b_cols, _Carry([], [], [], lambda *a: None, lambda *a: None))
    mod = mod_rows.reshape(N_MOD, d_model)
    shift1, scale1, gate1, shift2, scale2, gate2, shift3, scale3, gate3 = (mod[i:i + 1] for i in range(N_MOD))

    h1 = _pre_norm(xs, g_pre_ffn1, scale1, shift1, "pre_norm_ffn1")
    (a1, b1, u1), (wd1,) = _ffn_up(h1, wg1, wu1, "ffn_up_ffn1", carry=gather("wd1"))
    (y1, x1, h2), (win,) = _mm_nn(
        [(u1, wd1)], "ffn_down_ffn1", F32, carry=gather("win"),
        tail=_tail_post_pre(xs, g_post_ffn1, gate1, 0.5, g_pre_mix, scale2, shift2))

    proj, (wo,) = _mm_nt(h2, win, "in_proj", BF16, bias=b_in, carry=gather("wo"))
    sinks = sinks_a[0]
    cfg_a = dict(n_back=BACK_A, gqa=True, q_col=0, k_col=QA // LANES, v_col=(QA + KVA) // LANES, TPS=TPS_A)
    cfg_b = dict(n_back=BACK_B, gqa=False, q_col=(QA + 2 * KVA) // LANES, k_col=(QA + 2 * KVA + QB) // LANES,
                 v_col=(QA + 2 * KVA + 2 * QB) // LANES, TPS=TPS_B)
    (oa, lse_a), (wg2,) = _attention_fwd(proj, bias_a, sinks, name="attn_a", carry=gather("wg2"), **cfg_a)
    (ob, lse_b), (wu2,) = _attention_fwd(proj, bias_b, None, name="attn_b", carry=gather("wu2"), **cfg_b)
    ycat = _group_norm_cat(oa, ob, g_grp_a, g_grp_b)
    ymix, x2, h3 = _mm_nn([(ycat, wo)], "out_proj", F32, bias=b_out,
                          tail=_tail_post_pre(x1, g_post_mix, gate2, 1.0, g_pre_ffn2, scale3, shift3))

    (a3, b3, u3), (wd2,) = _ffn_up(h3, wg2, wu2, "ffn_up_ffn2", carry=gather("wd2"))

    flights, own = {}, {}

    def grad_pair(key, a_mat, b_mat, name):
        part, own[key] = _mm_tn_pair(a_mat, b_mat, name)
        return part

    def scatter_start(tag, after_vec, **parts):
        names = list(parts)
        sems, p_thru, lands, token = _scatter_start([parts[n] for n in names], "scatter_start_" + tag)
        flights[tag] = (names, sems, p_thru, lands)
        return after_vec + token[0:1, 0:1]

    dx3, dy, loss_part, s1 = _mm_nn([(u3, wd2)], "ffn_down_ffn2", None,
                                    tail=_tail_post_loss(x2, tgt, g_post_ffn2, gate3, 0.5))
    da, db = _ffn_down_bwd(dy, wd2, a3, b3, "ffn_down_bwd_ffn2")
    dwd2 = grad_pair("wd2", u3, dy, "grad_wd_ffn2")
    dwg2 = grad_pair("wg2", da, h3, "grad_wg_ffn2")
    dwu2 = grad_pair("wu2", db, h3, "grad_wu_ffn2")
    g_pre_tied = scatter_start("ffn2", g_pre_ffn2, wd2=dwd2, wg2=dwg2, wu2=dwu2)
    dx2, dymix, s2, s3, s1m, db_out = _mm_nn(
        [(da, wg2), (db, wu2)], "ffn_up_bwd_ffn2", None,
        tail=_tail_pre_post_bwd(x2, dx3, ymix, g_pre_tied, scale3, g_post_mix, gate2, 1.0))
    sm3 = dict(shift=s3, scale=s2 * g_pre_ffn2, gate=0.5 * g_post_ffn2 * s1,
               g_pre=(1.0 + scale3) * s2, g_post=(0.5 * gate3) * s1)

    dycat = _mm_nt(dymix, wo, "out_proj_bwd", F32)
    dwo = grad_pair("wo", ycat, dymix, "grad_wo")
    doa, dob, dg_a, dg_b = _group_norm_bwd(dycat, oa, ob, g_grp_a, g_grp_b)
    dqa, dka, dva, dsink = _attention_bwd(proj, bias_a, sinks, doa, lse_a, name="attn_a_bwd", **cfg_a)
    dqb, dkb, dvb, dbias = _attention_bwd(proj, bias_b, None, dob, lse_b, name="attn_b_bwd", **cfg_b)
    dproj = jnp.concatenate([dqa, dka, dva, dqb, dkb, dvb], axis=1)
    dwin, own["win"], db_in = _mm_tn_pair(dproj, h2, "grad_win", col_sums=True)
    g_pre_tied = scatter_start("mix", g_pre_mix, wo=dwo, win=dwin)
    dx1, dy, s2m, s3m, s1, _ = _mm_nn(
        [(dproj, win)], "in_proj_bwd", None,
        tail=_tail_pre_post_bwd(x1, dx2, y1, g_pre_tied, scale2, g_post_ffn1, gate1, 0.5))
    d_rel = jnp.dot(_diagonal_sums(dbias).reshape(H_B, SKEW), rel_m, precision=lax.Precision.HIGHEST)
    d_sinks = dsink[:, :2 * TPS_A, 0].reshape(1, H_A)

    da, db = _ffn_down_bwd(dy, wd1, a1, b1, "ffn_down_bwd_ffn1")
    dwd1 = grad_pair("wd1", u1, dy, "grad_wd_ffn1")
    dwg1 = grad_pair("wg1", da, h1, "grad_wg_ffn1")
    dwu1 = grad_pair("wu1", db, h1, "grad_wu_ffn1")
    g_pre_tied = scatter_start("ffn1", g_pre_ffn1, wd1=dwd1, wg1=dwg1, wu1=dwu1)
    dx0, s2, s3 = _mm_nn([(da, wg1), (db, wu1)], "ffn_up_bwd_ffn1", None,
                         tail=_tail_pre_bwd(xs, dx1, g_pre_tied, scale1))
    sm1 = dict(shift=s3, scale=s2 * g_pre_ffn1, gate=0.5 * g_post_ffn1 * s1,
               g_pre=(1.0 + scale1) * s2, g_post=(0.5 * gate1) * s1)

    dmod = jnp.concatenate([sm1["shift"], sm1["scale"], sm1["gate"],
                            s3m, s2m * g_pre_mix, g_post_mix * s1m,
                            sm3["shift"], sm3["scale"], sm3["gate"]], axis=1)
    small_parts = {
        "b_ada": dmod, "g_pre_ffn1": sm1["g_pre"], "g_post_ffn1": sm1["g_post"],
        "g_pre_mix": (1.0 + scale2) * s2m, "b_in": db_in, "sinks_a": d_sinks,
        "rel_bias_b": d_rel.reshape(1, H_B * N_REL), "g_grp_a": dg_a, "g_grp_b": dg_b, "b_out": db_out,
        "g_post_mix": gate2 * s1m, "g_pre_ffn2": sm3["g_pre"], "g_post_ffn2": sm3["g_post"]}
    sizes = [small_parts[n].shape[1] for n in SMALL]

    def pack(parts):
        cells = []
        for p in parts:
            cells.append(p)
            if p.shape[1] % LANES:
                cells.append(jnp.zeros((1, -p.shape[1] % LANES), F32))
        return jnp.concatenate(cells, axis=1)

    packed = pack([small_parts[n] for n in SMALL] + [loss_part])
    n_packed = packed.shape[1]
    small_sems, packed_thru, small_land, small_token = _small_gather_start(packed)

    out_g, out_d, out_m, out_v = {}, {}, {}, {}
    groups = (("ffn2", (("w_gate2", "wg2", True), ("w_up2", "wu2", True), ("w_down2", "wd2", False))),
              ("mix", (("w_in", "win", True), ("w_out", "wo", False))),
              ("ffn1", (("w_gate1", "wg1", True), ("w_up1", "wu1", True), ("w_down1", "wd1", False))))
    after = small_token
    for tag, members in groups:
        names, sems, p_thru, lands = flights[tag]
        _, l_done = _scatter_wait(sems, p_thru, lands, after, "scatter_wait_" + tag)
        slots = dict(zip(names, l_done))
        for n, key, transposed in members:
            view = (lambda t: t.T) if transposed else (lambda t: t)
            res = _adamw_from_slots(view(weights[n][0]), own[key], slots[key], view(mom_m[n][0]),
                                    view(mom_v[n][0]), "adamw_" + n)
            out_g[n], out_d[n], out_m[n], out_v[n] = (view(t)[None] for t in res)
            after = res[3]

    packed_done, small_land = _small_gather_wait(small_sems, packed_thru, small_land, after)
    gathered = lax.dynamic_update_slice(small_land, packed_done[None], (me, 0, 0))
    small_sum = _sum_rows8(gathered)
    loss = small_sum[0, n_packed - LANES]
    dmod_cols = lax.dynamic_slice(gathered.reshape(N_DEV, n_packed), (0, me * ada_cols), (N_DEV, ada_cols))
    g_ada = _ada_weight_grad(sc_all.reshape(N_DEV, d_model).T, dmod_cols)
    d_, m_, v_ = _adamw(w_ada[0], g_ada, m_w_ada[0], v_w_ada[0], "adamw_w_ada")
    out_g["w_ada"], out_d["w_ada"], out_m["w_ada"], out_v["w_ada"] = g_ada[None], d_[None], m_[None], v_[None]

    small_out = _adamw_small(small_sum, *(pack([tree[n].reshape(1, -1) for n in SMALL])
                                          for tree in (weights, mom_m, mom_v)), sizes)
    for j, n in enumerate(SMALL):
        shape = weights[n].shape
        out_g[n], out_d[n], out_m[n], out_v[n] = (t.reshape(shape) for t in small_out[4 * j:4 * j + 4])

    return (loss, dx0[None], *[out_g[n] for n in WEIGHTS], *[out_d[n] for n in WEIGHTS],
            *[out_m[n] for n in WEIGHTS], *[out_v[n] for n in WEIGHTS])
```

```python
import numpy as np
import jax
import jax.numpy as jnp
from jax import lax
from jax.experimental import pallas as pl
from jax.experimental.pallas import tpu as pltpu

F32 = jnp.float32
BF16 = jnp.bfloat16
MESH = pl.DeviceIdType.MESH
ANY = pl.BlockSpec(memory_space=pl.ANY)
VMEM_SPEC = pl.BlockSpec(memory_space=pltpu.VMEM)
SMEM_SPEC = pl.BlockSpec(memory_space=pltpu.SMEM)

N_DEV = 8
CHUNK = 64
HEAD_DIM = 64
LANES = 128
H_A, KV_A, H_B = 8, 2, 8
BACK_A, BACK_B = 2, 8
REL_CLIP = 128
N_REL = 2 * REL_CLIP + 1
QA, KVA, QB = H_A * HEAD_DIM, KV_A * HEAD_DIM, H_B * HEAD_DIM
D_IN = QA + 2 * KVA + 3 * QB
N_MOD = 9
EPS = 1e-6
NEG_INF = -1e30
QG = 4
QROWS = QG * CHUNK
TPS_A, TPS_B = 4, 2
SKEW = 1024
ADAM_LR, ADAM_B1, ADAM_B2, ADAM_EPS, ADAM_WD, ADAM_STEP = 0.001, 0.9, 0.999, 1e-08, 0.01, 10
VMEM_LIMIT = 56 * 2 ** 20


def _pick(n, cands):
    for c in cands:
        if n % c == 0:
            return c
    return n


def _pieces(n, width=2 * LANES):
    return [(lo, min(lo + width, n)) for lo in range(0, n, width)]


def _params(sem=None):
    return pltpu.CompilerParams(dimension_semantics=sem, vmem_limit_bytes=VMEM_LIMIT)


def _dot_nt(a, b):
    return lax.dot_general(a, b, (((1,), (1,)), ((), ())), preferred_element_type=F32)


def _dot_tn(a, b):
    return lax.dot_general(a, b, (((0,), (0,)), ((), ())), preferred_element_type=F32)


def _dot(a, b):
    return jnp.dot(a, b, preferred_element_type=F32)


def _sigmoid(a):
    return 0.5 * (jnp.tanh(0.5 * a) + 1.0)


def _mesh_pos():
    return lax.axis_index("x"), lax.axis_index("y"), lax.axis_index("c")


def _peer(x, y, c, r):
    px = 1 - x if r & 4 else x
    py = 1 - y if r & 2 else y
    pc = 1 - c if r & 1 else c
    return px, py, pc


class _Carry:
    def __init__(self, ins, out_shapes, scratch, start, finish):
        self.ins, self.out_shapes, self.scratch = list(ins), list(out_shapes), list(scratch)
        self.start, self.finish = start, finish


def _call(body, *, name, grid, in_specs, out_specs, out_shape, args, scratch=(), sem=None, carry=None):
    single = not isinstance(out_shape, (tuple, list))
    out_specs = (out_specs,) if single else tuple(out_specs)
    out_shape = (out_shape,) if single else tuple(out_shape)
    if carry is None:
        res = pl.pallas_call(body, name=name, grid=grid, in_specs=list(in_specs), out_specs=out_specs,
                             out_shape=out_shape, scratch_shapes=list(scratch), compiler_params=_params(sem))(*args)
        return res[0] if single else res
    n_in, n_out, n_s = len(in_specs), len(out_shape), len(scratch)
    ci, co = len(carry.ins), len(carry.out_shapes)

    def wrapped(*refs):
        ins, cins = refs[:n_in], refs[n_in:n_in + ci]
        outs = refs[n_in + ci:n_in + ci + n_out]
        couts = refs[n_in + ci + n_out:n_in + ci + n_out + co]
        scr = refs[n_in + ci + n_out + co:n_in + ci + n_out + co + n_s]
        cscr = refs[n_in + ci + n_out + co + n_s:]
        first, last = None, None
        for ax, n in enumerate(grid):
            f, l = pl.program_id(ax) == 0, pl.program_id(ax) == n - 1
            first = f if first is None else first & f
            last = l if last is None else last & l
        pl.when(first)(lambda: carry.start(cins, couts, cscr))
        body(*ins, *outs, *scr)
        pl.when(last)(lambda: carry.finish(cins, couts, cscr))

    res = pl.pallas_call(
        wrapped, name=name, grid=grid, in_specs=list(in_specs) + [ANY] * ci, out_specs=out_specs + (ANY,) * co,
        out_shape=out_shape + tuple(carry.out_shapes), scratch_shapes=list(scratch) + carry.scratch,
        compiler_params=_params(("arbitrary",) * len(grid)))(*args, *carry.ins)
    main = res[:n_out]
    return (main[0] if single else main), res[n_out:]


def _gather_carry(shards):
    n_w = len(shards)
    rows = [s.shape[0] for s in shards]

    def plan(ins, outs, scr):
        send_sems, recv_sems, local_sems = scr
        x, y, c = _mesh_pos()
        me, sibling = (x, y, c), (x, y, 1 - c)
        chips = [(1 - x, y), (x, 1 - y), (1 - x, 1 - y)]

        def block(w, dev):
            start = pl.multiple_of((4 * dev[0] + 2 * dev[1] + dev[2]) * rows[w], 16)
            return outs[w].at[pl.ds(start, rows[w]), :]

        def copy(w, k, dev, to, src=None):
            return pltpu.make_async_remote_copy(
                src_ref=block(w, dev) if src is None else src, dst_ref=block(w, dev),
                send_sem=send_sems.at[w, k], recv_sem=recv_sems.at[w, k], device_id=to, device_id_type=MESH)

        mine = [pltpu.make_async_copy(ins[w], block(w, me), local_sems.at[w]) for w in range(n_w)]
        first = []
        for j, chip in enumerate(chips):
            first += [copy(w, 1 + j, me, (*chip, c), src=ins[w]) for w in range(n_w)]
        first += [copy(w, 0, me, sibling, src=ins[w]) for w in range(n_w)]
        return c, me, sibling, chips, copy, mine, first

    def start(ins, outs, scr):
        _, _, _, _, _, mine, first = plan(ins, outs, scr)
        for cp in mine + first:
            cp.start()

    def finish(ins, outs, scr):
        c, me, sibling, chips, copy, mine, first = plan(ins, outs, scr)
        passed = []
        for j, chip in enumerate(chips):
            for w in range(n_w):
                copy(w, 1 + j, (*chip, c), me).wait_recv()
                cp = copy(w, 4 + j, (*chip, c), sibling)
                cp.start()
                passed.append(cp)
        for w in range(n_w):
            copy(w, 0, sibling, me).wait_recv()
        for j, chip in enumerate(chips):
            for w in range(n_w):
                copy(w, 4 + j, (*chip, 1 - c), me).wait_recv()
        for cp in first + passed:
            cp.wait_send()
        for cp in mine:
            cp.wait()

    return _Carry(
        shards, [jax.ShapeDtypeStruct((N_DEV * s.shape[0], s.shape[1]), s.dtype) for s in shards],
        [pltpu.SemaphoreType.DMA((n_w, N_DEV - 1)), pltpu.SemaphoreType.DMA((n_w, N_DEV - 1)),
         pltpu.SemaphoreType.DMA((n_w,))], start, finish)


HBM_SPEC = pl.BlockSpec(memory_space=pltpu.HBM)
SEM_SPEC = pl.BlockSpec(memory_space=pltpu.SEMAPHORE)
N_CHIP = N_DEV // 2


def _scatter_copy(part_ref, land_ref, send_sem, recv_sem, r, rows):
    x, y, c = _mesh_pos()
    px, py, _ = _peer(x, y, c, 2 * r)
    src = part_ref.at[pl.ds(pl.multiple_of((2 * px + py) * rows, 16), rows), :]
    return pltpu.make_async_remote_copy(
        src_ref=src, dst_ref=land_ref.at[r - 1], send_sem=send_sem, recv_sem=recv_sem,
        device_id=(px, py, c), device_id_type=MESH)


def _scatter_order(n_w):
    return [(w, r) for r in (3, 2, 1) for w in range(n_w)]


def _scatter_start(parts, name):
    n_w = len(parts)
    rows = [p.shape[0] // N_CHIP for p in parts]
    order = _scatter_order(n_w)
    lands = [pltpu.with_memory_space_constraint(lax.empty((N_CHIP - 1, r, p.shape[1]), p.dtype), pltpu.HBM)
             for r, p in zip(rows, parts)]

    def body(*refs):
        part_refs, land_refs = refs[:n_w], refs[n_w:2 * n_w]
        sems = refs[2 * n_w:2 * n_w + 2 * len(order)]
        token = refs[-1]
        for j, (w, r) in enumerate(order):
            _scatter_copy(part_refs[w], land_refs[w], sems[2 * j], sems[2 * j + 1], r, rows[w]).start()
        token[...] = jnp.zeros_like(token)

    n_sem = 2 * len(order)
    res = pl.pallas_call(
        body, name=name,
        out_shape=(*[pltpu.SemaphoreType.DMA(())] * n_sem, *[pltpu.HBM(p.shape, p.dtype) for p in parts],
                   *[pltpu.HBM(l.shape, l.dtype) for l in lands], jax.ShapeDtypeStruct((8, LANES), F32)),
        in_specs=[HBM_SPEC] * (2 * n_w), out_specs=(*[SEM_SPEC] * n_sem, *[HBM_SPEC] * (2 * n_w), VMEM_SPEC),
        input_output_aliases={i: n_sem + i for i in range(2 * n_w)},
        compiler_params=pltpu.CompilerParams(has_side_effects=pltpu.SideEffectType.DATAFLOW_SIDE_EFFECTING),
    )(*[pltpu.with_memory_space_constraint(p, pltpu.HBM) for p in parts], *lands)
    return (list(res[:n_sem]), list(res[n_sem:n_sem + n_w]), list(res[n_sem + n_w:n_sem + 2 * n_w]), res[-1])


def _scatter_wait(sems, parts, lands, after, name):
    n_w = len(parts)
    rows = [p.shape[0] // N_CHIP for p in parts]
    order = _scatter_order(n_w)

    def body(*refs):
        part_refs, land_refs = refs[:n_w], refs[n_w:2 * n_w]
        sem_refs = refs[2 * n_w:2 * n_w + 2 * len(order)]
        for j, (w, r) in enumerate(order):
            cp = _scatter_copy(part_refs[w], land_refs[w], sem_refs[2 * j], sem_refs[2 * j + 1], r, rows[w])
            cp.wait_send()
            cp.wait_recv()

    res = pl.pallas_call(
        body, name=name,
        out_shape=(*[pltpu.HBM(p.shape, p.dtype) for p in parts], *[pltpu.HBM(l.shape, l.dtype) for l in lands]),
        in_specs=[HBM_SPEC] * (2 * n_w) + [SEM_SPEC] * len(sems) + [ANY],
        out_specs=tuple([HBM_SPEC] * (2 * n_w)),
        input_output_aliases={i: i for i in range(2 * n_w)},
        compiler_params=pltpu.CompilerParams(has_side_effects=pltpu.SideEffectType.DATAFLOW_SIDE_EFFECTING),
    )(*parts, *lands, *sems, after)
    return list(res[:n_w]), list(res[n_w:])


def _small_copy(v_ref, land_ref, send_sem, recv_sem, r):
    x, y, c = _mesh_pos()
    px, py, pc = _peer(x, y, c, r)
    return pltpu.make_async_remote_copy(
        src_ref=v_ref, dst_ref=land_ref.at[4 * x + 2 * y + c], send_sem=send_sem, recv_sem=recv_sem,
        device_id=(px, py, pc), device_id_type=MESH)


def _small_gather_start(v):
    land = pltpu.with_memory_space_constraint(lax.empty((N_DEV,) + v.shape, v.dtype), pltpu.HBM)

    def body(v_ref, land_ref, *rest):
        sems, token = rest[:2 * (N_DEV - 1)], rest[-1]
        for r in range(1, N_DEV):
            _small_copy(v_ref, land_ref, sems[2 * r - 2], sems[2 * r - 1], r).start()
        token[...] = jnp.zeros_like(token)

    n_sem = 2 * (N_DEV - 1)
    res = pl.pallas_call(
        body, name="small_gather_start",
        out_shape=(*[pltpu.SemaphoreType.DMA(())] * n_sem, pltpu.HBM(v.shape, v.dtype),
                   pltpu.HBM(land.shape, land.dtype), jax.ShapeDtypeStruct((8, LANES), F32)),
        in_specs=[HBM_SPEC, HBM_SPEC], out_specs=(*[SEM_SPEC] * n_sem, HBM_SPEC, HBM_SPEC, VMEM_SPEC),
        input_output_aliases={0: n_sem, 1: n_sem + 1},
        compiler_params=pltpu.CompilerParams(has_side_effects=pltpu.SideEffectType.DATAFLOW_SIDE_EFFECTING),
    )(pltpu.with_memory_space_constraint(v, pltpu.HBM), land)
    return list(res[:n_sem]), res[n_sem], res[n_sem + 1], res[-1]


def _small_gather_wait(sems, v, land, after):
    def body(v_ref, land_ref, *rest):
        for r in range(1, N_DEV):
            cp = _small_copy(v_ref, land_ref, rest[2 * r - 2], rest[2 * r - 1], r)
            cp.wait_send()
            x, y, c = _mesh_pos()
            px, py, pc = _peer(x, y, c, r)
            pltpu.make_async_remote_copy(
                src_ref=v_ref, dst_ref=land_ref.at[4 * px + 2 * py + pc], send_sem=rest[2 * r - 2],
                recv_sem=rest[2 * r - 1], device_id=(px, py, pc), device_id_type=MESH).wait_recv()

    res = pl.pallas_call(
        body, name="small_gather_wait",
        out_shape=(pltpu.HBM(v.shape, v.dtype), pltpu.HBM(land.shape, land.dtype)),
        in_specs=[HBM_SPEC, HBM_SPEC] + [SEM_SPEC] * len(sems) + [ANY], out_specs=(HBM_SPEC, HBM_SPEC),
        input_output_aliases={0: 0, 1: 1},
        compiler_params=pltpu.CompilerParams(has_side_effects=pltpu.SideEffectType.DATAFLOW_SIDE_EFFECTING),
    )(v, land, *sems, after)
    return res[0], res[1]


def _ada_forward(c_row, w_ada, b_cols, carry):
    d = c_row.shape[1]
    wcols = w_ada.shape[1]
    ci, co = len(carry.ins), len(carry.out_shapes)

    def body(*refs):
        c_ref, w_ref, b_ref = refs[:3]
        cins = refs[3:3 + ci]
        sc_ref, mod_ref = refs[3 + ci:5 + ci]
        couts = refs[5 + ci:5 + ci + co]
        rows_ref, send_sems, recv_sems = refs[5 + ci + co:8 + ci + co]
        cscr = refs[8 + ci + co:]
        carry.start(cins, couts, cscr)
        x, y, c = _mesh_pos()
        me = 4 * x + 2 * y + c
        cv = c_ref[...]
        sc_ref[me] = cv * _sigmoid(cv)

        sends = []
        for r in range(1, N_DEV):
            px, py, pc = _peer(x, y, c, r)
            cp = pltpu.make_async_remote_copy(
                src_ref=sc_ref.at[me], dst_ref=sc_ref.at[me], send_sem=send_sems.at[0, r - 1],
                recv_sem=recv_sems.at[0, r - 1], device_id=(px, py, pc), device_id_type=MESH)
            cp.start()
            sends.append(cp)
        for r in range(1, N_DEV):
            px, py, pc = _peer(x, y, c, r)
            pid = 4 * px + 2 * py + pc
            pltpu.make_async_remote_copy(
                src_ref=sc_ref.at[pid], dst_ref=sc_ref.at[pid], send_sem=send_sems.at[0, r - 1],
                recv_sem=recv_sems.at[0, r - 1], device_id=(px, py, pc), device_id_type=MESH).wait_recv()
        for cp in sends:
            cp.wait_send()

        sc_all = jnp.concatenate([sc_ref[j] for j in range(N_DEV)], axis=0)
        rows = _dot(sc_all.astype(BF16), w_ref[...].astype(BF16)) + b_ref[...]
        for j in range(N_DEV):
            rows_ref[j] = rows[j:j + 1, :]
        mod_ref[me] = rows_ref[me]

        sends = []
        for r in range(1, N_DEV):
            px, py, pc = _peer(x, y, c, r)
            pid = 4 * px + 2 * py + pc
            cp = pltpu.make_async_remote_copy(
                src_ref=rows_ref.at[pid], dst_ref=mod_ref.at[me], send_sem=send_sems.at[1, r - 1],
                recv_sem=recv_sems.at[1, r - 1], device_id=(px, py, pc), device_id_type=MESH)
            cp.start()
            sends.append(cp)
        for r in range(1, N_DEV):
            px, py, pc = _peer(x, y, c, r)
            pid = 4 * px + 2 * py + pc
            pltpu.make_async_remote_copy(
                src_ref=rows_ref.at[pid], dst_ref=mod_ref.at[pid], send_sem=send_sems.at[1, r - 1],
                recv_sem=recv_sems.at[1, r - 1], device_id=(px, py, pc), device_id_type=MESH).wait_recv()
        for cp in sends:
            cp.wait_send()
        carry.finish(cins, couts, cscr)

    res = pl.pallas_call(
        body, name="ada_forward",
        out_shape=(jax.ShapeDtypeStruct((N_DEV, 1, d), F32), jax.ShapeDtypeStruct((N_DEV, 1, wcols), F32),
                   *carry.out_shapes),
        in_specs=[VMEM_SPEC, VMEM_SPEC, VMEM_SPEC] + [ANY] * ci, out_specs=(VMEM_SPEC, VMEM_SPEC) + (ANY,) * co,
        scratch_shapes=[pltpu.VMEM((N_DEV, 1, wcols), F32), pltpu.SemaphoreType.DMA((2, N_DEV - 1)),
                        pltpu.SemaphoreType.DMA((2, N_DEV - 1))] + carry.scratch,
        compiler_params=_params(),
    )(c_row, w_ada, b_cols, *carry.ins)
    return res[:2], res[2:]


def _mm_nt(a, b, name, out_dtype, bias=None, carry=None):
    m, k = a.shape
    n = b.shape[0]
    tm = _pick(m, (512, 256, 128))
    tn = _pick(n, (1408, 1152, 1024, 768, 512, 256, 128))

    def body(*refs):
        acc = _dot_nt(refs[0][...], refs[1][...])
        if bias is not None:
            acc = acc + refs[2][...]
        refs[-1][...] = acc.astype(out_dtype)

    in_specs = [pl.BlockSpec((tm, k), lambda j, i: (i, 0)), pl.BlockSpec((tn, k), lambda j, i: (j, 0))]
    args = [a, b]
    if bias is not None:
        in_specs.append(pl.BlockSpec((1, tn), lambda j, i: (0, j)))
        args.append(bias)
    return _call(body, name=name, grid=(n // tn, m // tm), in_specs=in_specs,
                 out_specs=pl.BlockSpec((tm, tn), lambda j, i: (i, j)),
                 out_shape=jax.ShapeDtypeStruct((m, n), out_dtype), args=args,
                 sem=("parallel", "parallel"), carry=carry)


class _Tail:
    def __init__(self, rows, vecs, outs, fn):
        self.rows, self.vecs, self.outs, self.fn = list(rows), list(vecs), list(outs), fn


def _mm_nn(pairs, name, out_dtype, bias=None, carry=None, tail=None):
    m, k = pairs[0][0].shape
    n = pairs[0][1].shape[1]
    n_p = len(pairs)
    tm = _pick(m, (512, 256, 128))
    tk = k if n_p == 1 else _pick(k, (1408, 1152, 1024, 768, 512, 256, 128))
    nk = k // tk
    n_b = 0 if bias is None else 1
    n_r, n_v = (len(tail.rows), len(tail.vecs)) if tail else (0, 0)
    n_in = 2 * n_p + n_b + n_r + n_v
    n_main = 0 if out_dtype is None else 1

    def finish(acc, refs, first_tile):
        if bias is not None:
            acc = acc + refs[2 * n_p][...]
        outs = refs[n_in:-1]
        if n_main:
            outs[0][...] = acc.astype(out_dtype)
        if tail is None:
            return
        rows = [r[...] for r in refs[2 * n_p + n_b:2 * n_p + n_b + n_r]]
        vecs = [v[...] for v in refs[2 * n_p + n_b + n_r:n_in]]
        vals = tail.fn(acc, rows, vecs)
        for ref, val, (dtype, kind) in zip(outs[n_main:], vals, tail.outs):
            if kind == "row":
                ref[...] = val.astype(dtype)
            else:
                @pl.when(first_tile)
                def _(ref=ref):
                    ref[...] = jnp.zeros_like(ref)

                ref[...] += val

    def body(*refs):
        acc_ref = refs[-1]
        kk, i = pl.program_id(0), pl.program_id(1)
        part = _dot(refs[0][...], refs[1][...])
        for p in range(1, n_p):
            part = part + _dot(refs[2 * p][...], refs[2 * p + 1][...])
        if nk == 1:
            finish(part, refs, i == 0)
            return
        rows = pl.ds(pl.multiple_of(i * tm, tm), tm)

        @pl.when(kk == 0)
        def _():
            acc_ref[rows, :] = part

        if nk > 2:
            @pl.when((kk > 0) & (kk < nk - 1))
            def _():
                acc_ref[rows, :] += part

        @pl.when(kk == nk - 1)
        def _():
            finish(acc_ref[rows, :] + part, refs, i == 0)

    def last_only(kk, i):
        return (jnp.where(kk == nk - 1, i, 0), 0)

    row_spec = pl.BlockSpec((tm, n), last_only)
    vec_spec = pl.BlockSpec((1, n), lambda kk, i: (0, 0))
    in_specs, args = [], []
    for a, b in pairs:
        in_specs += [pl.BlockSpec((tm, tk), lambda kk, i: (i, kk)), pl.BlockSpec((tk, n), lambda kk, i: (kk, 0))]
        args += [a, b]
    if bias is not None:
        in_specs.append(vec_spec)
        args.append(bias)
    out_specs = [row_spec] * n_main
    out_shape = [jax.ShapeDtypeStruct((m, n), out_dtype)] if n_main else []
    if tail:
        in_specs += [row_spec] * n_r + [vec_spec] * n_v
        args += tail.rows + tail.vecs
        for dtype, kind in tail.outs:
            if kind == "row":
                out_specs.append(row_spec)
                out_shape.append(jax.ShapeDtypeStruct((m, n), dtype))
            else:
                width = n if kind == "sum" else 1
                out_specs.append(pl.BlockSpec((1, width), lambda kk, i: (0, 0)))
                out_shape.append(jax.ShapeDtypeStruct((1, width), dtype))
    if tail is None:
        out_specs, out_shape = out_specs[0], out_shape[0]
    return _call(body, name=name, grid=(nk, m // tm), in_specs=in_specs, out_specs=out_specs,
                 out_shape=out_shape, args=args,
                 scratch=[pltpu.VMEM((m, n) if nk > 1 else (8, LANES), F32)],
                 sem=("arbitrary", "arbitrary"), carry=carry)


def _rms(v):
    return lax.rsqrt(jnp.mean(v * v, axis=-1, keepdims=True) + EPS)


def _col(v):
    return jnp.sum(v, axis=0, keepdims=True)


def _tail_post_pre(x, g_post, gate, weight, g_pre, scale, shift):
    def fn(y, rows, vecs):
        (xv,), (gp, gt, g, sc, sh) = rows, vecs
        xo = xv + (weight * gt) * ((y * _rms(y)) * gp)
        return xo, ((xo * _rms(xo)) * g) * (1.0 + sc) + sh

    return _Tail([x], [g_post, gate, g_pre, scale, shift], [(F32, "row"), (BF16, "row")], fn)


def _tail_post_loss(x, target, g, gate, weight):
    def fn(y, rows, vecs):
        (xv, tv), (gv, gt) = rows, vecs
        r = _rms(y)
        yn = y * r
        err = (xv + (weight * gt) * (yn * gv)) - tv
        do = err * (1.0 / y.shape[1])
        dyn = do * ((weight * gt) * gv)
        dy = r * (dyn - yn * jnp.mean(dyn * yn, axis=-1, keepdims=True))
        return do, dy, 0.5 * _col(jnp.mean(err * err, axis=-1, keepdims=True)), _col(do * yn)

    return _Tail([x, target], [g, gate], [(F32, "row"), (BF16, "row"), (F32, "one"), (F32, "sum")], fn)


def _tail_pre_bwd(x, dres, g_pre, scale):
    def fn(dh, rows, vecs):
        (xv, dr), (g, sc) = rows, vecs
        r = _rms(xv)
        n = xv * r
        dn = dh * (g * (1.0 + sc))
        return dr + r * (dn - n * jnp.mean(dn * n, axis=-1, keepdims=True)), _col(dh * n), _col(dh)

    return _Tail([x, dres], [g_pre, scale], [(F32, "row"), (F32, "sum"), (F32, "sum")], fn)


def _tail_pre_post_bwd(x, dres, y, g_pre, scale, g_post, gate, weight):
    def fn(dh, rows, vecs):
        (xv, dr, yv), (g, sc, gp, gt) = rows, vecs
        r = _rms(xv)
        n = xv * r
        dn = dh * (g * (1.0 + sc))
        dx = dr + r * (dn - n * jnp.mean(dn * n, axis=-1, keepdims=True))
        ry = _rms(yv)
        yn = yv * ry
        dyn = dx * ((weight * gt) * gp)
        dy = ry * (dyn - yn * jnp.mean(dyn * yn, axis=-1, keepdims=True))
        return dx, dy, _col(dh * n), _col(dh), _col(dx * yn), _col(dy)

    return _Tail([x, dres, y], [g_pre, scale, g_post, gate],
                 [(F32, "row"), (BF16, "row")] + [(F32, "sum")] * 4, fn)


def _mm_tn_pair(a, b, name, col_sums=False):
    k, m = a.shape
    n = b.shape[1]
    rows = m // N_DEV
    n_chip = N_DEV // 2
    tm = 4 * rows
    tk = _pick(k, (1024, 512, 256, 128))
    nk = k // tk

    def body(a_ref, b_ref, p_ref, own_ref, *rest):
        acc_ref, keep_ref, send_ref, land_ref, send_sems, recv_sems = rest[-6:]
        i, kk = pl.program_id(0), pl.program_id(1)
        x, y, c = _mesh_pos()
        if col_sums:
            cs_ref = rest[0]
            part = jnp.sum(a_ref[...].astype(F32), axis=0, keepdims=True)

            @pl.when(kk == 0)
            def _():
                cs_ref[...] = part

            @pl.when(kk > 0)
            def _():
                cs_ref[...] += part

        def push(chip):
            return pltpu.make_async_remote_copy(
                src_ref=send_ref.at[chip], dst_ref=land_ref.at[chip], send_sem=send_sems.at[chip],
                recv_sem=recv_sems.at[chip], device_id=(x, y, 1 - c), device_id_type=MESH)

        if nk == 1:
            acc = _dot_tn(a_ref[...], b_ref[...])
        else:
            @pl.when(kk == 0)
            def _():
                acc_ref[...] = jnp.zeros_like(acc_ref)

            acc_ref[...] += _dot_tn(a_ref[...], b_ref[...])
            acc = acc_ref

        for t in range(2):
            @pl.when((kk == nk - 1) & (i == t))
            def _(t=t):
                for ob in range(4):
                    chip, core = 2 * t + ob // 2, ob % 2
                    blk = acc[ob * rows:(ob + 1) * rows, :]

                    @pl.when(c == core)
                    def _(chip=chip, blk=blk):
                        keep_ref[chip] = blk

                    @pl.when(c != core)
                    def _(chip=chip, blk=blk):
                        send_ref[chip] = blk.astype(BF16)
                        push(chip).start()

        @pl.when((kk == nk - 1) & (i == 1))
        def _():
            for chip in range(n_chip):
                push(chip).wait_recv()
                val = (keep_ref[chip] + land_ref[chip].astype(F32)).astype(BF16)
                p_ref[chip * rows:(chip + 1) * rows, :] = val

                @pl.when(2 * x + y == chip)
                def _(val=val):
                    own_ref[...] = val

            for chip in range(n_chip):
                push(chip).wait_send()

    out_specs = [pl.BlockSpec((n_chip * rows, n), lambda i, kk: (0, 0)), pl.BlockSpec((rows, n), lambda i, kk: (0, 0))]
    out_shape = [jax.ShapeDtypeStruct((n_chip * rows, n), BF16), jax.ShapeDtypeStruct((rows, n), BF16)]
    if col_sums:
        out_specs.append(pl.BlockSpec((1, tm), lambda i, kk: (0, i)))
        out_shape.append(jax.ShapeDtypeStruct((1, m), F32))
    return _call(body, name=name, grid=(2, nk),
                 in_specs=[pl.BlockSpec((tk, tm), lambda i, kk: (kk, i)), pl.BlockSpec((tk, n), lambda i, kk: (kk, 0))],
                 out_specs=out_specs, out_shape=out_shape, args=[a, b],
                 scratch=[pltpu.VMEM((tm, n) if nk > 1 else (8, LANES), F32), pltpu.VMEM((n_chip, rows, n), F32),
                          pltpu.VMEM((n_chip, rows, n), BF16), pltpu.VMEM((n_chip, rows, n), BF16),
                          pltpu.SemaphoreType.DMA((n_chip,)), pltpu.SemaphoreType.DMA((n_chip,))],
                 sem=("arbitrary", "arbitrary"))


def _ffn_up(h, wg_t, wu_t, name, carry=None):
    s, d = h.shape
    f = wg_t.shape[0]
    tm = _pick(s, (512, 256, 128))
    tf = _pick(f, (1408, 1024, 512, 256, 128))

    def body(h_ref, wg_ref, wu_ref, a_ref, b_ref, u_ref):
        hh = h_ref[...]
        for lo, hi in _pieces(tf):
            a = _dot_nt(hh, wg_ref[lo:hi, :])
            b = _dot_nt(hh, wu_ref[lo:hi, :])
            a_ref[:, lo:hi] = a.astype(BF16)
            b_ref[:, lo:hi] = b.astype(BF16)
            u_ref[:, lo:hi] = ((a * _sigmoid(a)) * b).astype(BF16)

    w_spec = pl.BlockSpec((tf, d), lambda j, i: (j, 0))
    o_spec = pl.BlockSpec((tm, tf), lambda j, i: (i, j))
    o_shape = jax.ShapeDtypeStruct((s, f), BF16)
    return _call(body, name=name, grid=(f // tf, s // tm),
                 in_specs=[pl.BlockSpec((tm, d), lambda j, i: (i, 0)), w_spec, w_spec],
                 out_specs=(o_spec, o_spec, o_spec), out_shape=(o_shape, o_shape, o_shape),
                 args=[h, wg_t, wu_t], sem=("parallel", "parallel"), carry=carry)


def _ffn_down_bwd(dy, wd, a, b, name, carry=None):
    s, d = dy.shape
    f = wd.shape[0]
    tm = _pick(s, (512, 256, 128))
    tf = _pick(f, (1408, 1024, 512, 256, 128))

    def body(dy_ref, wd_ref, a_ref, b_ref, da_ref, db_ref):
        dyv = dy_ref[...]
        for lo, hi in _pieces(tf):
            du = _dot_nt(dyv, wd_ref[lo:hi, :])
            a = a_ref[:, lo:hi].astype(F32)
            b = b_ref[:, lo:hi].astype(F32)
            sig = _sigmoid(a)
            da_ref[:, lo:hi] = (du * b * (sig * (1.0 + a * (1.0 - sig)))).astype(BF16)
            db_ref[:, lo:hi] = (du * (a * sig)).astype(BF16)

    t_spec = pl.BlockSpec((tm, tf), lambda j, i: (i, j))
    o_shape = jax.ShapeDtypeStruct((s, f), BF16)
    return _call(body, name=name, grid=(f // tf, s // tm),
                 in_specs=[pl.BlockSpec((tm, d), lambda j, i: (i, 0)), pl.BlockSpec((tf, d), lambda j, i: (j, 0)),
                           t_spec, t_spec],
                 out_specs=(t_spec, t_spec), out_shape=(o_shape, o_shape), args=[dy, wd, a, b],
                 sem=("parallel", "parallel"), carry=carry)


def _row_tile(s):
    return _pick(s, (256, 128, 64))


def _vec_spec(d):
    return pl.BlockSpec((1, d), lambda i: (0, 0))


def _pre_norm(x, g, scale, shift, name):
    s, d = x.shape
    ts = _row_tile(s)

    def body(x_ref, g_ref, sc_ref, sh_ref, h_ref):
        xv = x_ref[...]
        r = lax.rsqrt(jnp.mean(xv * xv, axis=-1, keepdims=True) + EPS)
        h_ref[...] = (((xv * r) * g_ref[...]) * (1.0 + sc_ref[...]) + sh_ref[...]).astype(BF16)

    row = pl.BlockSpec((ts, d), lambda i: (i, 0))
    return _call(body, name=name, grid=(s // ts,), in_specs=[row, _vec_spec(d), _vec_spec(d), _vec_spec(d)],
                 out_specs=row, out_shape=jax.ShapeDtypeStruct((s, d), BF16), args=[x, g, scale, shift],
                 sem=("parallel",))


def _group_norm_cat(oa, ob, ga, gb):
    s = oa.shape[0]
    ts = _row_tile(s)

    def body(oa_ref, ob_ref, ga_ref, gb_ref, y_ref):
        for o_ref, g_ref, lo, w in ((oa_ref, ga_ref, 0, QA), (ob_ref, gb_ref, QA, QB)):
            ov = o_ref[...]
            r = lax.rsqrt(jnp.mean(ov * ov, axis=-1, keepdims=True) + EPS)
            y_ref[:, lo:lo + w] = ((ov * r) * g_ref[...]).astype(BF16)

    return _call(body, name="group_norm_cat", grid=(s // ts,),
                 in_specs=[pl.BlockSpec((ts, QA), lambda i: (i, 0)), pl.BlockSpec((ts, QB), lambda i: (i, 0)),
                           _vec_spec(QA), _vec_spec(QB)],
                 out_specs=pl.BlockSpec((ts, QA + QB), lambda i: (i, 0)),
                 out_shape=jax.ShapeDtypeStruct((s, QA + QB), BF16), args=[oa, ob, ga, gb], sem=("parallel",))


def _group_norm_bwd(dy, oa, ob, ga, gb):
    s = oa.shape[0]
    ts = _row_tile(s)

    def body(dy_ref, oa_ref, ob_ref, ga_ref, gb_ref, doa_ref, dob_ref, dga_ref, dgb_ref):
        @pl.when(pl.program_id(0) == 0)
        def _():
            dga_ref[...] = jnp.zeros_like(dga_ref)
            dgb_ref[...] = jnp.zeros_like(dgb_ref)

        for o_ref, g_ref, do_ref, dg_ref, lo, w in ((oa_ref, ga_ref, doa_ref, dga_ref, 0, QA),
                                                    (ob_ref, gb_ref, dob_ref, dgb_ref, QA, QB)):
            ov = o_ref[...]
            dyv = dy_ref[:, lo:lo + w]
            r = lax.rsqrt(jnp.mean(ov * ov, axis=-1, keepdims=True) + EPS)
            n = ov * r
            dn = dyv * g_ref[...]
            do_ref[...] = r * (dn - n * jnp.mean(dn * n, axis=-1, keepdims=True))
            dg_ref[...] += jnp.sum(dyv * n, axis=0, keepdims=True)

    ra = pl.BlockSpec((ts, QA), lambda i: (i, 0))
    rb = pl.BlockSpec((ts, QB), lambda i: (i, 0))
    return _call(body, name="group_norm_bwd", grid=(s // ts,),
                 in_specs=[pl.BlockSpec((ts, QA + QB), lambda i: (i, 0)), ra, rb, _vec_spec(QA), _vec_spec(QB)],
                 out_specs=(ra, rb, _vec_spec(QA), _vec_spec(QB)),
                 out_shape=(jax.ShapeDtypeStruct((s, QA), F32), jax.ShapeDtypeStruct((s, QB), F32),
                            jax.ShapeDtypeStruct((1, QA), F32), jax.ShapeDtypeStruct((1, QB), F32)),
                 args=[dy, oa, ob, ga, gb], sem=("arbitrary",))


def _n_variants(n_back):
    return -(-n_back // QG) + 1


def _alibi_bias():
    i = np.arange(QROWS)[:, None]
    j = np.arange((QG + BACK_A) * CHUNK)[None, :]
    dist = np.abs(BACK_A * CHUNK + i - j).astype(np.float32)
    dc = j // CHUNK - i // CHUNK
    valid = (dc >= 0) & (dc <= BACK_A)
    slopes = np.array([2.0 ** (-8.0 * (h + 1) / H_A) for h in range(H_A)], dtype=np.float32)
    bias = -slopes[:, None, None] * dist[None]
    out = [np.where((valid & (j >= (BACK_A - QG * v) * CHUNK))[None], bias, np.float32(NEG_INF))
           for v in range(_n_variants(BACK_A))]
    return jnp.asarray(np.stack(out).astype(np.float32))


def _rel_index_matrix():
    cc = np.arange(SKEW)
    dist = np.where(cc < SKEW - QROWS, BACK_B * CHUNK - cc, BACK_B * CHUNK + SKEW - cc)
    idx = np.clip(dist, -REL_CLIP, REL_CLIP) + REL_CLIP
    m = np.zeros((SKEW, N_REL), np.float32)
    m[cc, idx] = 1.0
    return jnp.asarray(m)


def _toeplitz_bias(vec, carry=None):
    lk = (QG + BACK_B) * CHUNK
    nv = _n_variants(BACK_B)

    def body(v_ref, o_ref):
        xv = jnp.broadcast_to(v_ref[0], (QROWS, SKEW))
        row = lax.broadcasted_iota(jnp.int32, (QROWS, SKEW), 0)
        for bit in range(QROWS.bit_length() - 1):
            xv = jnp.where((row >> bit) & 1 == 1, pltpu.roll(xv, 1 << bit, 1), xv)
        ri = lax.broadcasted_iota(jnp.int32, (QROWS, lk), 0) // CHUNK
        col = lax.broadcasted_iota(jnp.int32, (QROWS, lk), 1)
        ci = col // CHUNK
        valid = (ci - ri >= 0) & (ci - ri <= BACK_B)
        for v in range(nv):
            o_ref[v, 0] = jnp.where(valid & (col >= (BACK_B - QG * v) * CHUNK), xv[:, :lk], NEG_INF)

    return _call(body, name="toeplitz_bias", grid=(H_B,),
                 in_specs=[pl.BlockSpec((1, 1, SKEW), lambda h: (h, 0, 0))],
                 out_specs=pl.BlockSpec((nv, 1, QROWS, lk), lambda h: (0, h, 0, 0)),
                 out_shape=jax.ShapeDtypeStruct((nv, H_B, QROWS, lk), F32), args=[vec], sem=("parallel",),
                 carry=carry)


def _diagonal_sums(dbias):
    lk = dbias.shape[2]

    def body(d_ref, o_ref):
        xp = jnp.concatenate([d_ref[0], jnp.zeros((QROWS, SKEW - lk), F32)], axis=1)
        xv = xp[0:CHUNK]
        for q in range(1, QG):
            xv = xv + pltpu.roll(xp[q * CHUNK:(q + 1) * CHUNK], SKEW - q * CHUNK, 1)
        row = lax.broadcasted_iota(jnp.int32, (CHUNK, SKEW), 0)
        for bit in range(CHUNK.bit_length() - 1):
            xv = jnp.where((row >> bit) & 1 == 1, pltpu.roll(xv, SKEW - (1 << bit), 1), xv)
        o_ref[0] = jnp.sum(xv, axis=0, keepdims=True)

    return _call(body, name="diagonal_sums", grid=(H_B,),
                 in_specs=[pl.BlockSpec((1, QROWS, lk), lambda h: (h, 0, 0))],
                 out_specs=pl.BlockSpec((1, 1, SKEW), lambda h: (h, 0, 0)),
                 out_shape=jax.ShapeDtypeStruct((H_B, 1, SKEW), F32), args=[dbias], sem=("parallel",))


def _attn_common(s, n_back, gqa, q_col, k_col, v_col, TPS):
    assert q_col % TPS == 0 and (gqa or (k_col % TPS == 0 and v_col % TPS == 0)), "blocks of TPS lane tiles"
    lk = (QG + n_back) * CHUNK
    pad = n_back * CHUNK
    wide = TPS * LANES
    q_spec = pl.BlockSpec((QROWS, wide), lambda t, g: (g, q_col // TPS + t))
    if gqa:
        k_spec = pl.BlockSpec((s, LANES), lambda t, g: (0, k_col))
        v_spec = pl.BlockSpec((s, LANES), lambda t, g: (0, v_col))
    else:
        k_spec = pl.BlockSpec((s, wide), lambda t, g: (0, k_col // TPS + t))
        v_spec = pl.BlockSpec((s, wide), lambda t, g: (0, v_col // TPS + t))
    last_variant = _n_variants(n_back) - 1
    bias_spec = pl.BlockSpec((None, 2 * TPS, QROWS, lk), lambda t, g: (jnp.minimum(g, last_variant), t, 0, 0))
    tile_spec = pl.BlockSpec((QROWS, wide), lambda t, g: (g, t))
    return lk, pad, q_spec, k_spec, v_spec, bias_spec, tile_spec


def _attention_fwd(proj, bias, sinks, *, n_back, gqa, q_col, k_col, v_col, TPS, name, carry=None):
    s = proj.shape[0]
    lk, pad, q_spec, k_spec, v_spec, bias_spec, tile_spec = _attn_common(s, n_back, gqa, q_col, k_col, v_col, TPS)
    n_t, n_g = 512 // (TPS * LANES), s // QROWS
    kv_wide = LANES if gqa else TPS * LANES

    def body(*refs):
        if gqa:
            q_ref, k_ref, v_ref, bias_ref, sink_ref, o_ref, l_ref, kpad, vpad = refs
        else:
            q_ref, k_ref, v_ref, bias_ref, o_ref, l_ref, kpad, vpad = refs
        t, g = pl.program_id(0), pl.program_id(1)

        @pl.when(g == 0)
        def _():
            kpad[0:pad, :] = jnp.zeros((pad, kv_wide), BF16)
            vpad[0:pad, :] = jnp.zeros((pad, kv_wide), BF16)
            kpad[pad:, :] = k_ref[...]
            vpad[pad:, :] = v_ref[...]

        start = pl.multiple_of(g * QROWS, QROWS)
        half = lax.broadcasted_iota(jnp.int32, (QROWS, LANES), 1) // HEAD_DIM
        for tt in range(TPS):
            lanes = slice(tt * LANES, (tt + 1) * LANES)
            kv_lanes = slice(0, LANES) if gqa else lanes
            kb = kpad[pl.ds(start, lk), kv_lanes]
            vb = vpad[pl.ds(start, lk), kv_lanes]
            q = q_ref[:, lanes] * (HEAD_DIM ** -0.5)
            if gqa:
                hk = (TPS * t + tt) // 2
                q_rolled = pltpu.roll(q.astype(F32), HEAD_DIM, 1).astype(BF16)
            outs, lses = [], []
            for e in range(2):
                if gqa:
                    kv_half = hk
                    src = jnp.where(hk == e, q, q_rolled)
                else:
                    kv_half = e
                    src = q
                qm = jnp.where(half == kv_half, src, jnp.zeros_like(src))
                sc = _dot_nt(qm, kb) + bias_ref[2 * tt + e]
                m = jnp.max(sc, axis=-1, keepdims=True)
                if gqa:
                    sk = sink_ref[2 * (TPS * t + tt) + e]
                    m = jnp.maximum(m, sk)
                p = jnp.exp(sc - m)
                l = jnp.sum(p, axis=-1, keepdims=True)
                if gqa:
                    l = l + jnp.exp(sk - m)
                pn = p / l
                outs.append(_dot(pn.astype(BF16), vb))
                lses.append(m + jnp.log(l))
            if gqa:
                same = jnp.where(hk == 0, outs[0], outs[1])
                other = jnp.where(hk == 0, outs[1], outs[0])
                o_ref[:, lanes] = jnp.where(half == hk, same, pltpu.roll(other, HEAD_DIM, 1))
            else:
                o_ref[:, lanes] = jnp.where(half == 0, outs[0], outs[1])
            l_ref[:, lanes] = jnp.where(half == 0, lses[0], lses[1])

    in_specs = [q_spec, k_spec, v_spec, bias_spec] + ([SMEM_SPEC] if gqa else [])
    args = [proj, proj, proj, bias] + ([sinks] if gqa else [])
    o_shape = jax.ShapeDtypeStruct((s, 512), F32)
    return _call(body, name=name, grid=(n_t, n_g), in_specs=in_specs, out_specs=(tile_spec, tile_spec),
                 out_shape=(o_shape, o_shape), args=args,
                 scratch=[pltpu.VMEM((s + pad, kv_wide), BF16), pltpu.VMEM((s + pad, kv_wide), BF16)],
                 sem=("arbitrary", "arbitrary"), carry=carry)


def _attention_bwd(proj, bias, sinks, do, lse, *, n_back, gqa, q_col, k_col, v_col, TPS, name, carry=None):
    s = proj.shape[0]
    lk, pad, q_spec, k_spec, v_spec, bias_spec, tile_spec = _attn_common(s, n_back, gqa, q_col, k_col, v_col, TPS)
    n_t, n_g = 512 // (TPS * LANES), s // QROWS
    kv_wide = LANES if gqa else TPS * LANES

    def body(*refs):
        if gqa:
            (q_ref, k_ref, v_ref, bias_ref, sink_ref, do_ref, l_ref,
             dq_ref, dk_ref, dv_ref, dsink_ref, kpad, vpad, dkpad, dvpad) = refs
        else:
            (q_ref, k_ref, v_ref, bias_ref, do_ref, l_ref,
             dq_ref, dk_ref, dv_ref, dbias_ref, kpad, vpad, dkpad, dvpad) = refs
        t, g = pl.program_id(0), pl.program_id(1)

        @pl.when(g == 0)
        def _():
            kpad[0:pad, :] = jnp.zeros((pad, kv_wide), BF16)
            vpad[0:pad, :] = jnp.zeros((pad, kv_wide), BF16)
            kpad[pad:, :] = k_ref[...]
            vpad[pad:, :] = v_ref[...]
            if gqa:
                dsink_ref[...] = jnp.zeros_like(dsink_ref)
            else:
                dbias_ref[...] = jnp.zeros_like(dbias_ref)

        @pl.when((g == 0) & (t == 0) if gqa else g == 0)
        def _():
            dkpad[...] = jnp.zeros_like(dkpad)
            dvpad[...] = jnp.zeros_like(dvpad)

        start = pl.multiple_of(g * QROWS, QROWS)
        half = lax.broadcasted_iota(jnp.int32, (QROWS, LANES), 1) // HEAD_DIM
        for tt in range(TPS):
            lanes = slice(tt * LANES, (tt + 1) * LANES)
            kv_lanes = slice(0, LANES) if gqa else lanes
            kb = kpad[pl.ds(start, lk), kv_lanes]
            vb = vpad[pl.ds(start, lk), kv_lanes]
            q = q_ref[:, lanes]
            dov = do_ref[:, lanes]
            lv = l_ref[:, lanes]
            if gqa:
                hk = (TPS * t + tt) // 2
                q_rolled = pltpu.roll(q.astype(F32), HEAD_DIM, 1).astype(BF16)
                do_rolled = pltpu.roll(dov, HEAD_DIM, 1)
            dqs = []
            dk_acc = jnp.zeros((lk, LANES), F32)
            dv_acc = jnp.zeros((lk, LANES), F32)
            for e in range(2):
                if gqa:
                    kv_half = hk
                    src = jnp.where(hk == e, q, q_rolled)
                    do_src = jnp.where(hk == e, dov, do_rolled)
                else:
                    kv_half = e
                    src = q
                    do_src = dov
                qm = jnp.where(half == kv_half, src, jnp.zeros_like(src))
                dom = jnp.where(half == kv_half, do_src, 0.0).astype(BF16)
                lcol = jnp.max(jnp.where(half == e, lv, -jnp.inf), axis=-1, keepdims=True)
                sc = _dot_nt(qm * (HEAD_DIM ** -0.5), kb) + bias_ref[2 * tt + e]
                pn = jnp.exp(sc - lcol)
                dp = _dot_nt(dom, vb)
                delta = jnp.sum(pn * dp, axis=-1, keepdims=True)
                ds = pn * (dp - delta)
                if gqa:
                    p_sink = jnp.exp(sink_ref[2 * (TPS * t + tt) + e] - lcol)
                    dsk = -jnp.sum(p_sink * delta, axis=0, keepdims=True)
                    row = 2 * tt + e
                    dsink_ref[0, row:row + 1, :] += jnp.broadcast_to(dsk, (1, LANES))
                else:
                    dbias_ref[2 * tt + e] += ds
                dsb = (ds * (HEAD_DIM ** -0.5)).astype(BF16)
                dqs.append(_dot(dsb, kb))
                dk_acc = dk_acc + _dot_tn(dsb, qm)
                dv_acc = dv_acc + _dot_tn(pn.astype(BF16), dom)
            dkpad[pl.ds(start, lk), kv_lanes] += dk_acc
            dvpad[pl.ds(start, lk), kv_lanes] += dv_acc
            if gqa:
                same = jnp.where(hk == 0, dqs[0], dqs[1])
                other = jnp.where(hk == 0, dqs[1], dqs[0])
                dq_ref[:, lanes] = jnp.where(half == hk, same, pltpu.roll(other, HEAD_DIM, 1)).astype(BF16)
            else:
                dq_ref[:, lanes] = jnp.where(half == 0, dqs[0], dqs[1]).astype(BF16)

        @pl.when((g == n_g - 1) & (t == n_t - 1) if gqa else g == n_g - 1)
        def _():
            dk_ref[...] = dkpad[pad:, :].astype(BF16)
            dv_ref[...] = dvpad[pad:, :].astype(BF16)

    in_specs = [q_spec, k_spec, v_spec, bias_spec] + ([SMEM_SPEC] if gqa else []) + [tile_spec, tile_spec]
    args = [proj, proj, proj, bias] + ([sinks] if gqa else []) + [do, lse]
    if gqa:
        kv_out = pl.BlockSpec((s, LANES), lambda t, g: (0, 0))
        kv_shape = jax.ShapeDtypeStruct((s, LANES), BF16)
        extra_spec = pl.BlockSpec((1, 8, LANES), lambda t, g: (t, 0, 0))
        extra_shape = jax.ShapeDtypeStruct((n_t, 8, LANES), F32)
    else:
        kv_out = pl.BlockSpec((s, kv_wide), lambda t, g: (0, t))
        kv_shape = jax.ShapeDtypeStruct((s, 512), BF16)
        extra_spec = pl.BlockSpec((2 * TPS, QROWS, lk), lambda t, g: (t, 0, 0))
        extra_shape = jax.ShapeDtypeStruct(bias.shape[1:], F32)
    return _call(body, name=name, grid=(n_t, n_g), in_specs=in_specs,
                 out_specs=(tile_spec, kv_out, kv_out, extra_spec),
                 out_shape=(jax.ShapeDtypeStruct((s, 512), BF16), kv_shape, kv_shape, extra_shape), args=args,
                 scratch=[pltpu.VMEM((s + pad, kv_wide), BF16), pltpu.VMEM((s + pad, kv_wide), BF16),
                          pltpu.VMEM((s + pad, kv_wide), F32), pltpu.VMEM((s + pad, kv_wide), F32)],
                 sem=("arbitrary", "arbitrary"), carry=carry)


def _sum_rows8(g):
    n = g.shape[2]

    def body(g_ref, o_ref):
        acc = g_ref[0]
        for j in range(1, N_DEV):
            acc = acc + g_ref[j]
        o_ref[...] = acc

    return pl.pallas_call(
        body, name="sum_small_grads", in_specs=[VMEM_SPEC], out_specs=VMEM_SPEC,
        out_shape=jax.ShapeDtypeStruct((1, n), F32), compiler_params=_params(),
    )(g)


def _ada_weight_grad(sc_t, dmod_cols):
    d = sc_t.shape[0]
    w = dmod_cols.shape[1]
    td = _pick(d, (256, 128))

    def body(sc_ref, dm_ref, o_ref):
        scv = sc_ref[...]
        dmv = dm_ref[...]
        acc = scv[:, 0:1] * dmv[0:1, :]
        for b in range(1, N_DEV):
            acc = acc + scv[:, b:b + 1] * dmv[b:b + 1, :]
        o_ref[...] = acc

    return _call(body, name="ada_weight_grad", grid=(d // td,),
                 in_specs=[pl.BlockSpec((td, N_DEV), lambda i: (i, 0)), pl.BlockSpec((N_DEV, w), lambda i: (0, 0))],
                 out_specs=pl.BlockSpec((td, w), lambda i: (i, 0)), out_shape=jax.ShapeDtypeStruct((d, w), F32),
                 args=[sc_t, dmod_cols], sem=("parallel",))


def _adamw_update(w, gv, m, v):
    nm = ADAM_B1 * m + (1.0 - ADAM_B1) * gv
    nv = ADAM_B2 * v + (1.0 - ADAM_B2) * (gv * gv)
    m_hat = nm / (1.0 - ADAM_B1 ** ADAM_STEP)
    v_hat = nv / (1.0 - ADAM_B2 ** ADAM_STEP)
    return -ADAM_LR * (m_hat / (jnp.sqrt(v_hat) + ADAM_EPS) + ADAM_WD * w), nm, nv


def _adamw(w, g, m, v, name):
    rows, cols = w.shape
    tr = _pick(rows, (256, 176, 128, 88, 64)) if rows > 256 else rows

    def body(w_ref, g_ref, m_ref, v_ref, d_ref, nm_ref, nv_ref):
        d_ref[...], nm_ref[...], nv_ref[...] = _adamw_update(w_ref[...], g_ref[...], m_ref[...], v_ref[...])

    spec = pl.BlockSpec((tr, cols), lambda i: (i, 0))
    shape = jax.ShapeDtypeStruct((rows, cols), F32)
    return _call(body, name=name, grid=(rows // tr,), in_specs=[spec] * 4, out_specs=(spec, spec, spec),
                 out_shape=(shape, shape, shape), args=[w, g, m, v], sem=("parallel",))


def _adamw_from_slots(w, own, slots, m, v, name):
    n_slots, rows, k = slots.shape

    def body(o_ref, s_ref, w_ref, m_ref, v_ref, g_ref, d_ref, nm_ref, nv_ref):
        gv = o_ref[...].astype(F32)
        for j in range(n_slots):
            gv = gv + s_ref[j].astype(F32)
        g_ref[...] = gv
        d_ref[...], nm_ref[...], nv_ref[...] = _adamw_update(w_ref[...], gv, m_ref[...], v_ref[...])

    tr = rows // 2 if rows % 32 == 0 else rows
    spec = pl.BlockSpec((tr, k), lambda i: (i, 0))
    shape = jax.ShapeDtypeStruct((rows, k), F32)
    return _call(body, name=name, grid=(rows // tr,),
                 in_specs=[spec, pl.BlockSpec((n_slots, tr, k), lambda i: (0, i, 0)), spec, spec, spec],
                 out_specs=(spec, spec, spec, spec), out_shape=(shape, shape, shape, shape),
                 args=[own, slots, w, m, v], sem=("parallel",))


def _adamw_small(g, w, m, v, sizes):
    n = w.shape[1]
    offs, off = [], 0
    for size in sizes:
        offs.append(off)
        off += size + (-size % LANES)

    def body(g_ref, w_ref, m_ref, v_ref, *out_refs):
        gv = g_ref[:, 0:n]
        dv, nm, nv = _adamw_update(w_ref[...], gv, m_ref[...], v_ref[...])
        for j, (o, size) in enumerate(zip(offs, sizes)):
            for k, val in enumerate((gv, dv, nm, nv)):
                out_refs[4 * j + k][...] = val[:, o:o + size]

    shapes = [jax.ShapeDtypeStruct((1, size), F32) for size in sizes for _ in range(4)]
    return pl.pallas_call(
        body, name="adamw_small", in_specs=[VMEM_SPEC] * 4, out_specs=tuple([VMEM_SPEC] * len(shapes)),
        out_shape=tuple(shapes), compiler_params=_params(),
    )(g, w, m, v)


SMALL = ("b_ada", "g_pre_ffn1", "g_post_ffn1", "g_pre_mix", "b_in", "sinks_a", "rel_bias_b", "g_grp_a",
         "g_grp_b", "b_out", "g_post_mix", "g_pre_ffn2", "g_post_ffn2")
WEIGHTS = ("w_ada", "b_ada", "g_pre_ffn1", "w_gate1", "w_up1", "w_down1", "g_post_ffn1", "g_pre_mix", "w_in",
           "b_in", "sinks_a", "rel_bias_b", "g_grp_a", "g_grp_b", "w_out", "b_out", "g_post_mix", "g_pre_ffn2",
           "w_gate2", "w_up2", "w_down2", "g_post_ffn2")


def kernel(x, c, w_ada, b_ada, g_pre_ffn1, w_gate1, w_up1, w_down1, g_post_ffn1, g_pre_mix, w_in, b_in, sinks_a, rel_bias_b, g_grp_a, g_grp_b, w_out, b_out, g_post_mix, g_pre_ffn2, w_gate2, w_up2, w_down2, g_post_ffn2, loss_target, m_w_ada, m_b_ada, m_g_pre_ffn1, m_w_gate1, m_w_up1, m_w_down1, m_g_post_ffn1, m_g_pre_mix, m_w_in, m_b_in, m_sinks_a, m_rel_bias_b, m_g_grp_a, m_g_grp_b, m_w_out, m_b_out, m_g_post_mix, m_g_pre_ffn2, m_w_gate2, m_w_up2, m_w_down2, m_g_post_ffn2, v_w_ada, v_b_ada, v_g_pre_ffn1, v_w_gate1, v_w_up1, v_w_down1, v_g_post_ffn1, v_g_pre_mix, v_w_in, v_b_in, v_sinks_a, v_rel_bias_b, v_g_grp_a, v_g_grp_b, v_w_out, v_b_out, v_g_post_mix, v_g_pre_ffn2, v_w_gate2, v_w_up2, v_w_down2, v_g_post_ffn2):
    given = dict(locals())
    weights = {n: given[n] for n in WEIGHTS}
    mom_m = {n: given["m_" + n] for n in WEIGHTS}
    mom_v = {n: given["v_" + n] for n in WEIGHTS}

    me = 4 * lax.axis_index("x") + 2 * lax.axis_index("y") + lax.axis_index("c")
    xs = x[0]
    tgt = loss_target[0]
    d_model = xs.shape[1]
    ada_cols = w_ada.shape[2]

    sh = {"wg1": w_gate1[0].T, "wu1": w_up1[0].T, "wd1": w_down1[0], "win": w_in[0].T, "wo": w_out[0],
          "wg2": w_gate2[0].T, "wu2": w_up2[0].T, "wd2": w_down2[0]}
    sh = {k: v.astype(BF16) for k, v in sh.items()}

    def gather(*names):
        return _gather_carry([sh[n] for n in names])

    bias_a = _alibi_bias()
    rel_m = _rel_index_matrix()
    rel_vec = jnp.dot(rel_bias_b[0], rel_m.T, precision=lax.Precision.HIGHEST)
    bias_b, (wg1, wu1) = _toeplitz_bias(rel_vec.reshape(H_B, 1, SKEW), carry=gather("wg1", "wu1"))

    b_cols = lax.dynamic_slice(b_ada, (0, me * ada_cols), (1, ada_cols))
    (sc_all, mod_rows), _ = _ada_forward(c, w_ada[0], b_cols, _Carry([], [], [], lambda *a: None, lambda *a: None))
    mod = mod_rows.reshape(N_MOD, d_model)
    shift1, scale1, gate1, shift2, scale2, gate2, shift3, scale3, gate3 = (mod[i:i + 1] for i in range(N_MOD))

    h1 = _pre_norm(xs, g_pre_ffn1, scale1, shift1, "pre_norm_ffn1")
    (a1, b1, u1), (wd1,) = _ffn_up(h1, wg1, wu1, "ffn_up_ffn1", carry=gather("wd1"))
    (y1, x1, h2), (win,) = _mm_nn(
        [(u1, wd1)], "ffn_down_ffn1", F32, carry=gather("win"),
        tail=_tail_post_pre(xs, g_post_ffn1, gate1, 0.5, g_pre_mix, scale2, shift2))

    proj = _mm_nt(h2, win, "in_proj", BF16, bias=b_in)
    sinks = sinks_a[0]
    cfg_a = dict(n_back=BACK_A, gqa=True, q_col=0, k_col=QA // LANES, v_col=(QA + KVA) // LANES, TPS=TPS_A)
    cfg_b = dict(n_back=BACK_B, gqa=False, q_col=(QA + 2 * KVA) // LANES, k_col=(QA + 2 * KVA + QB) // LANES,
                 v_col=(QA + 2 * KVA + 2 * QB) // LANES, TPS=TPS_B)
    (oa, lse_a), (wg2,) = _attention_fwd(proj, bias_a, sinks, name="attn_a", carry=gather("wg2"), **cfg_a)
    (ob, lse_b), (wu2, wo) = _attention_fwd(proj, bias_b, None, name="attn_b", carry=gather("wu2", "wo"), **cfg_b)
    ycat = _group_norm_cat(oa, ob, g_grp_a, g_grp_b)
    ymix, x2, h3 = _mm_nn([(ycat, wo)], "out_proj", F32, bias=b_out,
                          tail=_tail_post_pre(x1, g_post_mix, gate2, 1.0, g_pre_ffn2, scale3, shift3))

    (a3, b3, u3), (wd2,) = _ffn_up(h3, wg2, wu2, "ffn_up_ffn2", carry=gather("wd2"))

    flights, own = {}, {}

    def grad_pair(key, a_mat, b_mat, name):
        part, own[key] = _mm_tn_pair(a_mat, b_mat, name)
        return part

    def scatter_start(tag, after_vec, **parts):
        names = list(parts)
        sems, p_thru, lands, token = _scatter_start([parts[n] for n in names], "scatter_start_" + tag)
        flights[tag] = (names, sems, p_thru, lands)
        return after_vec + token[0:1, 0:1]

    dx3, dy, loss_part, s1 = _mm_nn([(u3, wd2)], "ffn_down_ffn2", None,
                                    tail=_tail_post_loss(x2, tgt, g_post_ffn2, gate3, 0.5))
    da, db = _ffn_down_bwd(dy, wd2, a3, b3, "ffn_down_bwd_ffn2")
    dwd2 = grad_pair("wd2", u3, dy, "grad_wd_ffn2")
    dwg2 = grad_pair("wg2", da, h3, "grad_wg_ffn2")
    dwu2 = grad_pair("wu2", db, h3, "grad_wu_ffn2")
    g_pre_tied = scatter_start("ffn2", g_pre_ffn2, wd2=dwd2, wg2=dwg2, wu2=dwu2)
    dx2, dymix, s2, s3, s1m, db_out = _mm_nn(
        [(da, wg2), (db, wu2)], "ffn_up_bwd_ffn2", None,
        tail=_tail_pre_post_bwd(x2, dx3, ymix, g_pre_tied, scale3, g_post_mix, gate2, 1.0))
    sm3 = dict(shift=s3, scale=s2 * g_pre_ffn2, gate=0.5 * g_post_ffn2 * s1,
               g_pre=(1.0 + scale3) * s2, g_post=(0.5 * gate3) * s1)

    dycat = _mm_nt(dymix, wo, "out_proj_bwd", F32)
    dwo = grad_pair("wo", ycat, dymix, "grad_wo")
    doa, dob, dg_a, dg_b = _group_norm_bwd(dycat, oa, ob, g_grp_a, g_grp_b)
    dqa, dka, dva, dsink = _attention_bwd(proj, bias_a, sinks, doa, lse_a, name="attn_a_bwd", **cfg_a)
    dqb, dkb, dvb, dbias = _attention_bwd(proj, bias_b, None, dob, lse_b, name="attn_b_bwd", **cfg_b)
    dproj = jnp.concatenate([dqa, dka, dva, dqb, dkb, dvb], axis=1)
    dwin, own["win"], db_in = _mm_tn_pair(dproj, h2, "grad_win", col_sums=True)
    g_pre_tied = scatter_start("mix", g_pre_mix, wo=dwo, win=dwin)
    dx1, dy, s2m, s3m, s1, _ = _mm_nn(
        [(dproj, win)], "in_proj_bwd", None,
        tail=_tail_pre_post_bwd(x1, dx2, y1, g_pre_tied, scale2, g_post_ffn1, gate1, 0.5))
    d_rel = jnp.dot(_diagonal_sums(dbias).reshape(H_B, SKEW), rel_m, precision=lax.Precision.HIGHEST)
    d_sinks = dsink[:, :2 * TPS_A, 0].reshape(1, H_A)

    da, db = _ffn_down_bwd(dy, wd1, a1, b1, "ffn_down_bwd_ffn1")
    dwd1 = grad_pair("wd1", u1, dy, "grad_wd_ffn1")
    dwg1 = grad_pair("wg1", da, h1, "grad_wg_ffn1")
    dwu1 = grad_pair("wu1", db, h1, "grad_wu_ffn1")
    g_pre_tied = scatter_start("ffn1", g_pre_ffn1, wd1=dwd1, wg1=dwg1, wu1=dwu1)
    dx0, s2, s3 = _mm_nn([(da, wg1), (db, wu1)], "ffn_up_bwd_ffn1", None,
                         tail=_tail_pre_bwd(xs, dx1, g_pre_tied, scale1))
    sm1 = dict(shift=s3, scale=s2 * g_pre_ffn1, gate=0.5 * g_post_ffn1 * s1,
               g_pre=(1.0 + scale1) * s2, g_post=(0.5 * gate1) * s1)

    dmod = jnp.concatenate([sm1["shift"], sm1["scale"], sm1["gate"],
                            s3m, s2m * g_pre_mix, g_post_mix * s1m,
                            sm3["shift"], sm3["scale"], sm3["gate"]], axis=1)
    small_parts = {
        "b_ada": dmod, "g_pre_ffn1": sm1["g_pre"], "g_post_ffn1": sm1["g_post"],
        "g_pre_mix": (1.0 + scale2) * s2m, "b_in": db_in, "sinks_a": d_sinks,
        "rel_bias_b": d_rel.reshape(1, H_B * N_REL), "g_grp_a": dg_a, "g_grp_b": dg_b, "b_out": db_out,
        "g_post_mix": gate2 * s1m, "g_pre_ffn2": sm3["g_pre"], "g_post_ffn2": sm3["g_post"]}
    sizes = [small_parts[n].shape[1] for n in SMALL]

    def pack(parts):
        cells = []
        for p in parts:
            cells.append(p)
            if p.shape[1] % LANES:
                cells.append(jnp.zeros((1, -p.shape[1] % LANES), F32))
        return jnp.concatenate(cells, axis=1)

    packed = pack([small_parts[n] for n in SMALL] + [loss_part])
    n_packed = packed.shape[1]
    small_sems, packed_thru, small_land, small_token = _small_gather_start(packed)

    out_g, out_d, out_m, out_v = {}, {}, {}, {}
    groups = (("ffn2", (("w_gate2", "wg2", True), ("w_up2", "wu2", True), ("w_down2", "wd2", False))),
              ("mix", (("w_in", "win", True), ("w_out", "wo", False))),
              ("ffn1", (("w_gate1", "wg1", True), ("w_up1", "wu1", True), ("w_down1", "wd1", False))))
    after = small_token
    for tag, members in groups:
        names, sems, p_thru, lands = flights[tag]
        _, l_done = _scatter_wait(sems, p_thru, lands, after, "scatter_wait_" + tag)
        slots = dict(zip(names, l_done))
        for n, key, transposed in members:
            view = (lambda t: t.T) if transposed else (lambda t: t)
            res = _adamw_from_slots(view(weights[n][0]), own[key], slots[key], view(mom_m[n][0]),
                                    view(mom_v[n][0]), "adamw_" + n)
            out_g[n], out_d[n], out_m[n], out_v[n] = (view(t)[None] for t in res)
            after = res[3]

    packed_done, small_land = _small_gather_wait(small_sems, packed_thru, small_land, after)
    gathered = lax.dynamic_update_slice(small_land, packed_done[None], (me, 0, 0))
    small_sum = _sum_rows8(gathered)
    loss = small_sum[0, n_packed - LANES]
    dmod_cols = lax.dynamic_slice(gathered.reshape(N_DEV, n_packed), (0, me * ada_cols), (N_DEV, ada_cols))
    g_ada = _ada_weight_grad(sc_all.reshape(N_DEV, d_model).T, dmod_cols)
    d_, m_, v_ = _adamw(w_ada[0], g_ada, m_w_ada[0], v_w_ada[0], "adamw_w_ada")
    out_g["w_ada"], out_d["w_ada"], out_m["w_ada"], out_v["w_ada"] = g_ada[None], d_[None], m_[None], v_[None]

    small_out = _adamw_small(small_sum, *(pack([tree[n].reshape(1, -1) for n in SMALL])
                                          for tree in (weights, mom_m, mom_v)), sizes)
    for j, n in enumerate(SMALL):
        shape = weights[n].shape
        out_g[n], out_d[n], out_m[n], out_v[n] = (t.reshape(shape) for t in small_out[4 * j:4 * j + 4])

    return (loss, dx0[None], *[out_g[n] for n in WEIGHTS], *[out_d[n] for n in WEIGHTS],
            *[out_m[n] for n in WEIGHTS], *[out_v[n] for n in WEIGHTS])
```

```python
import numpy as np
import jax
import jax.numpy as jnp
from jax import lax
from jax.experimental import pallas as pl
from jax.experimental.pallas import tpu as pltpu

F32 = jnp.float32
BF16 = jnp.bfloat16
MESH = pl.DeviceIdType.MESH
ANY = pl.BlockSpec(memory_space=pl.ANY)
VMEM_SPEC = pl.BlockSpec(memory_space=pltpu.VMEM)
SMEM_SPEC = pl.BlockSpec(memory_space=pltpu.SMEM)

N_DEV = 8
CHUNK = 64
HEAD_DIM = 64
LANES = 128
H_A, KV_A, H_B = 8, 2, 8
BACK_A, BACK_B = 2, 8
REL_CLIP = 128
N_REL = 2 * REL_CLIP + 1
QA, KVA, QB = H_A * HEAD_DIM, KV_A * HEAD_DIM, H_B * HEAD_DIM
D_IN = QA + 2 * KVA + 3 * QB
N_MOD = 9
EPS = 1e-6
NEG_INF = -1e30
QG = 4
QROWS = QG * CHUNK
TPS_A, TPS_B = 4, 2
SKEW = 1024
ADAM_LR, ADAM_B1, ADAM_B2, ADAM_EPS, ADAM_WD, ADAM_STEP = 0.001, 0.9, 0.999, 1e-08, 0.01, 10
VMEM_LIMIT = 56 * 2 ** 20


def _pick(n, cands):
    for c in cands:
        if n % c == 0:
            return c
    return n


def _pieces(n, width=2 * LANES):
    return [(lo, min(lo + width, n)) for lo in range(0, n, width)]


def _params(sem=None):
    return pltpu.CompilerParams(dimension_semantics=sem, vmem_limit_bytes=VMEM_LIMIT)


def _dot_nt(a, b):
    return lax.dot_general(a, b, (((1,), (1,)), ((), ())), preferred_element_type=F32)


def _dot_tn(a, b):
    return lax.dot_general(a, b, (((0,), (0,)), ((), ())), preferred_element_type=F32)


def _dot(a, b):
    return jnp.dot(a, b, preferred_element_type=F32)


def _sigmoid(a):
    return 0.5 * (jnp.tanh(0.5 * a) + 1.0)


def _mesh_pos():
    return lax.axis_index("x"), lax.axis_index("y"), lax.axis_index("c")


def _peer(x, y, c, r):
    px = 1 - x if r & 4 else x
    py = 1 - y if r & 2 else y
    pc = 1 - c if r & 1 else c
    return px, py, pc


class _Carry:
    def __init__(self, ins, out_shapes, scratch, start, finish):
        self.ins, self.out_shapes, self.scratch = list(ins), list(out_shapes), list(scratch)
        self.start, self.finish = start, finish


def _call(body, *, name, grid, in_specs, out_specs, out_shape, args, scratch=(), sem=None, carry=None):
    single = not isinstance(out_shape, (tuple, list))
    out_specs = (out_specs,) if single else tuple(out_specs)
    out_shape = (out_shape,) if single else tuple(out_shape)
    if carry is None:
        res = pl.pallas_call(body, name=name, grid=grid, in_specs=list(in_specs), out_specs=out_specs,
                             out_shape=out_shape, scratch_shapes=list(scratch), compiler_params=_params(sem))(*args)
        return res[0] if single else res
    n_in, n_out, n_s = len(in_specs), len(out_shape), len(scratch)
    ci, co = len(carry.ins), len(carry.out_shapes)

    def wrapped(*refs):
        ins, cins = refs[:n_in], refs[n_in:n_in + ci]
        outs = refs[n_in + ci:n_in + ci + n_out]
        couts = refs[n_in + ci + n_out:n_in + ci + n_out + co]
        scr = refs[n_in + ci + n_out + co:n_in + ci + n_out + co + n_s]
        cscr = refs[n_in + ci + n_out + co + n_s:]
        first, last = None, None
        for ax, n in enumerate(grid):
            f, l = pl.program_id(ax) == 0, pl.program_id(ax) == n - 1
            first = f if first is None else first & f
            last = l if last is None else last & l
        pl.when(first)(lambda: carry.start(cins, couts, cscr))
        body(*ins, *outs, *scr)
        pl.when(last)(lambda: carry.finish(cins, couts, cscr))

    res = pl.pallas_call(
        wrapped, name=name, grid=grid, in_specs=list(in_specs) + [ANY] * ci, out_specs=out_specs + (ANY,) * co,
        out_shape=out_shape + tuple(carry.out_shapes), scratch_shapes=list(scratch) + carry.scratch,
        compiler_params=_params(("arbitrary",) * len(grid)))(*args, *carry.ins)
    main = res[:n_out]
    return (main[0] if single else main), res[n_out:]


def _gather_carry(shards):
    n_w = len(shards)
    rows = [s.shape[0] for s in shards]

    def plan(ins, outs, scr):
        send_sems, recv_sems, local_sems = scr
        x, y, c = _mesh_pos()
        me, sibling = (x, y, c), (x, y, 1 - c)
        chips = [(1 - x, y), (x, 1 - y), (1 - x, 1 - y)]

        def block(w, dev):
            start = pl.multiple_of((4 * dev[0] + 2 * dev[1] + dev[2]) * rows[w], 16)
            return outs[w].at[pl.ds(start, rows[w]), :]

        def copy(w, k, dev, to, src=None):
            return pltpu.make_async_remote_copy(
                src_ref=block(w, dev) if src is None else src, dst_ref=block(w, dev),
                send_sem=send_sems.at[w, k], recv_sem=recv_sems.at[w, k], device_id=to, device_id_type=MESH)

        mine = [pltpu.make_async_copy(ins[w], block(w, me), local_sems.at[w]) for w in range(n_w)]
        first = []
        for j, chip in enumerate(chips):
            first += [copy(w, 1 + j, me, (*chip, c), src=ins[w]) for w in range(n_w)]
        first += [copy(w, 0, me, sibling, src=ins[w]) for w in range(n_w)]
        return c, me, sibling, chips, copy, mine, first

    def start(ins, outs, scr):
        _, _, _, _, _, mine, first = plan(ins, outs, scr)
        for cp in mine + first:
            cp.start()

    def finish(ins, outs, scr):
        c, me, sibling, chips, copy, mine, first = plan(ins, outs, scr)
        passed = []
        for j, chip in enumerate(chips):
            for w in range(n_w):
                copy(w, 1 + j, (*chip, c), me).wait_recv()
                cp = copy(w, 4 + j, (*chip, c), sibling)
                cp.start()
                passed.append(cp)
        for w in range(n_w):
            copy(w, 0, sibling, me).wait_recv()
        for j, chip in enumerate(chips):
            for w in range(n_w):
                copy(w, 4 + j, (*chip, 1 - c), me).wait_recv()
        for cp in first + passed:
            cp.wait_send()
        for cp in mine:
            cp.wait()

    return _Carry(
        shards, [jax.ShapeDtypeStruct((N_DEV * s.shape[0], s.shape[1]), s.dtype) for s in shards],
        [pltpu.SemaphoreType.DMA((n_w, N_DEV - 1)), pltpu.SemaphoreType.DMA((n_w, N_DEV - 1)),
         pltpu.SemaphoreType.DMA((n_w,))], start, finish)


HBM_SPEC = pl.BlockSpec(memory_space=pltpu.HBM)
SEM_SPEC = pl.BlockSpec(memory_space=pltpu.SEMAPHORE)
N_CHIP = N_DEV // 2


def _scatter_copy(part_ref, land_ref, send_sem, recv_sem, r, rows):
    x, y, c = _mesh_pos()
    px, py, _ = _peer(x, y, c, 2 * r)
    src = part_ref.at[pl.ds(pl.multiple_of((2 * px + py) * rows, 16), rows), :]
    return pltpu.make_async_remote_copy(
        src_ref=src, dst_ref=land_ref.at[r - 1], send_sem=send_sem, recv_sem=recv_sem,
        device_id=(px, py, c), device_id_type=MESH)


def _scatter_order(n_w):
    return [(w, r) for r in (3, 2, 1) for w in range(n_w)]


def _scatter_start(parts, name):
    n_w = len(parts)
    rows = [p.shape[0] // N_CHIP for p in parts]
    order = _scatter_order(n_w)
    lands = [pltpu.with_memory_space_constraint(lax.empty((N_CHIP - 1, r, p.shape[1]), p.dtype), pltpu.HBM)
             for r, p in zip(rows, parts)]

    def body(*refs):
        part_refs, land_refs = refs[:n_w], refs[n_w:2 * n_w]
        sems = refs[2 * n_w:2 * n_w + 2 * len(order)]
        token = refs[-1]
        for j, (w, r) in enumerate(order):
            _scatter_copy(part_refs[w], land_refs[w], sems[2 * j], sems[2 * j + 1], r, rows[w]).start()
        token[...] = jnp.zeros_like(token)

    n_sem = 2 * len(order)
    res = pl.pallas_call(
        body, name=name,
        out_shape=(*[pltpu.SemaphoreType.DMA(())] * n_sem, *[pltpu.HBM(p.shape, p.dtype) for p in parts],
                   *[pltpu.HBM(l.shape, l.dtype) for l in lands], jax.ShapeDtypeStruct((8, LANES), F32)),
        in_specs=[HBM_SPEC] * (2 * n_w), out_specs=(*[SEM_SPEC] * n_sem, *[HBM_SPEC] * (2 * n_w), VMEM_SPEC),
        input_output_aliases={i: n_sem + i for i in range(2 * n_w)},
        compiler_params=pltpu.CompilerParams(has_side_effects=pltpu.SideEffectType.DATAFLOW_SIDE_EFFECTING),
    )(*[pltpu.with_memory_space_constraint(p, pltpu.HBM) for p in parts], *lands)
    return (list(res[:n_sem]), list(res[n_sem:n_sem + n_w]), list(res[n_sem + n_w:n_sem + 2 * n_w]), res[-1])


def _scatter_wait(sems, parts, lands, after, name):
    n_w = len(parts)
    rows = [p.shape[0] // N_CHIP for p in parts]
    order = _scatter_order(n_w)

    def body(*refs):
        part_refs, land_refs = refs[:n_w], refs[n_w:2 * n_w]
        sem_refs = refs[2 * n_w:2 * n_w + 2 * len(order)]
        for j, (w, r) in enumerate(order):
            cp = _scatter_copy(part_refs[w], land_refs[w], sem_refs[2 * j], sem_refs[2 * j + 1], r, rows[w])
            cp.wait_send()
            cp.wait_recv()

    res = pl.pallas_call(
        body, name=name,
        out_shape=(*[pltpu.HBM(p.shape, p.dtype) for p in parts], *[pltpu.HBM(l.shape, l.dtype) for l in lands]),
        in_specs=[HBM_SPEC] * (2 * n_w) + [SEM_SPEC] * len(sems) + [ANY],
        out_specs=tuple([HBM_SPEC] * (2 * n_w)),
        input_output_aliases={i: i for i in range(2 * n_w)},
        compiler_params=pltpu.CompilerParams(has_side_effects=pltpu.SideEffectType.DATAFLOW_SIDE_EFFECTING),
    )(*parts, *lands, *sems, after)
    return list(res[:n_w]), list(res[n_w:])


def _small_copy(v_ref, land_ref, send_sem, recv_sem, r):
    x, y, c = _mesh_pos()
    px, py, pc = _peer(x, y, c, r)
    return pltpu.make_async_remote_copy(
        src_ref=v_ref, dst_ref=land_ref.at[4 * x + 2 * y + c], send_sem=send_sem, recv_sem=recv_sem,
        device_id=(px, py, pc), device_id_type=MESH)


def _small_gather_start(v):
    land = pltpu.with_memory_space_constraint(lax.empty((N_DEV,) + v.shape, v.dtype), pltpu.HBM)

    def body(v_ref, land_ref, *rest):
        sems, token = rest[:2 * (N_DEV - 1)], rest[-1]
        for r in range(1, N_DEV):
            _small_copy(v_ref, land_ref, sems[2 * r - 2], sems[2 * r - 1], r).start()
        token[...] = jnp.zeros_like(token)

    n_sem = 2 * (N_DEV - 1)
    res = pl.pallas_call(
        body, name="small_gather_start",
        out_shape=(*[pltpu.SemaphoreType.DMA(())] * n_sem, pltpu.HBM(v.shape, v.dtype),
                   pltpu.HBM(land.shape, land.dtype), jax.ShapeDtypeStruct((8, LANES), F32)),
        in_specs=[HBM_SPEC, HBM_SPEC], out_specs=(*[SEM_SPEC] * n_sem, HBM_SPEC, HBM_SPEC, VMEM_SPEC),
        input_output_aliases={0: n_sem, 1: n_sem + 1},
        compiler_params=pltpu.CompilerParams(has_side_effects=pltpu.SideEffectType.DATAFLOW_SIDE_EFFECTING),
    )(pltpu.with_memory_space_constraint(v, pltpu.HBM), land)
    return list(res[:n_sem]), res[n_sem], res[n_sem + 1], res[-1]


def _small_gather_wait(sems, v, land, after):
    def body(v_ref, land_ref, *rest):
        for r in range(1, N_DEV):
            cp = _small_copy(v_ref, land_ref, rest[2 * r - 2], rest[2 * r - 1], r)
            cp.wait_send()
            x, y, c = _mesh_pos()
            px, py, pc = _peer(x, y, c, r)
            pltpu.make_async_remote_copy(
                src_ref=v_ref, dst_ref=land_ref.at[4 * px + 2 * py + pc], send_sem=rest[2 * r - 2],
                recv_sem=rest[2 * r - 1], device_id=(px, py, pc), device_id_type=MESH).wait_recv()

    res = pl.pallas_call(
        body, name="small_gather_wait",
        out_shape=(pltpu.HBM(v.shape, v.dtype), pltpu.HBM(land.shape, land.dtype)),
        in_specs=[HBM_SPEC, HBM_SPEC] + [SEM_SPEC] * len(sems) + [ANY], out_specs=(HBM_SPEC, HBM_SPEC),
        input_output_aliases={0: 0, 1: 1},
        compiler_params=pltpu.CompilerParams(has_side_effects=pltpu.SideEffectType.DATAFLOW_SIDE_EFFECTING),
    )(v, land, *sems, after)
    return res[0], res[1]


def _ada_forward(c_row, w_ada, b_cols, carry):
    d = c_row.shape[1]
    wcols = w_ada.shape[1]
    ci, co = len(carry.ins), len(carry.out_shapes)

    def body(*refs):
        c_ref, w_ref, b_ref = refs[:3]
        cins = refs[3:3 + ci]
        sc_ref, mod_ref = refs[3 + ci:5 + ci]
        couts = refs[5 + ci:5 + ci + co]
        rows_ref, send_sems, recv_sems = refs[5 + ci + co:8 + ci + co]
        cscr = refs[8 + ci + co:]
        carry.start(cins, couts, cscr)
        x, y, c = _mesh_pos()
        me = 4 * x + 2 * y + c
        cv = c_ref[...]
        sc_ref[me] = cv * _sigmoid(cv)

        sends = []
        for r in range(1, N_DEV):
            px, py, pc = _peer(x, y, c, r)
            cp = pltpu.make_async_remote_copy(
                src_ref=sc_ref.at[me], dst_ref=sc_ref.at[me], send_sem=send_sems.at[0, r - 1],
                recv_sem=recv_sems.at[0, r - 1], device_id=(px, py, pc), device_id_type=MESH)
            cp.start()
            sends.append(cp)
        for r in range(1, N_DEV):
            px, py, pc = _peer(x, y, c, r)
            pid = 4 * px + 2 * py + pc
            pltpu.make_async_remote_copy(
                src_ref=sc_ref.at[pid], dst_ref=sc_ref.at[pid], send_sem=send_sems.at[0, r - 1],
                recv_sem=recv_sems.at[0, r - 1], device_id=(px, py, pc), device_id_type=MESH).wait_recv()
        for cp in sends:
            cp.wait_send()

        sc_all = jnp.concatenate([sc_ref[j] for j in range(N_DEV)], axis=0)
        rows = _dot(sc_all.astype(BF16), w_ref[...].astype(BF16)) + b_ref[...]
        for j in range(N_DEV):
            rows_ref[j] = rows[j:j + 1, :]
        mod_ref[me] = rows_ref[me]

        sends = []
        for r in range(1, N_DEV):
            px, py, pc = _peer(x, y, c, r)
            pid = 4 * px + 2 * py + pc
            cp = pltpu.make_async_remote_copy(
                src_ref=rows_ref.at[pid], dst_ref=mod_ref.at[me], send_sem=send_sems.at[1, r - 1],
                recv_sem=recv_sems.at[1, r - 1], device_id=(px, py, pc), device_id_type=MESH)
            cp.start()
            sends.append(cp)
        for r in range(1, N_DEV):
            px, py, pc = _peer(x, y, c, r)
            pid = 4 * px + 2 * py + pc
            pltpu.make_async_remote_copy(
                src_ref=rows_ref.at[pid], dst_ref=mod_ref.at[pid], send_sem=send_sems.at[1, r - 1],
                recv_sem=recv_sems.at[1, r - 1], device_id=(px, py, pc), device_id_type=MESH).wait_recv()
        for cp in sends:
            cp.wait_send()
        carry.finish(cins, couts, cscr)

    res = pl.pallas_call(
        body, name="ada_forward",
        out_shape=(jax.ShapeDtypeStruct((N_DEV, 1, d), F32), jax.ShapeDtypeStruct((N_DEV, 1, wcols), F32),
                   *carry.out_shapes),
        in_specs=[VMEM_SPEC, VMEM_SPEC, VMEM_SPEC] + [ANY] * ci, out_specs=(VMEM_SPEC, VMEM_SPEC) + (ANY,) * co,
        scratch_shapes=[pltpu.VMEM((N_DEV, 1, wcols), F32), pltpu.SemaphoreType.DMA((2, N_DEV - 1)),
                        pltpu.SemaphoreType.DMA((2, N_DEV - 1))] + carry.scratch,
        compiler_params=_params(),
    )(c_row, w_ada, b_cols, *carry.ins)
    return res[:2], res[2:]


def _mm_nt(a, b, name, out_dtype, bias=None, carry=None):
    m, k = a.shape
    n = b.shape[0]
    tm = _pick(m, (512, 256, 128))
    tn = _pick(n, (1408, 1152, 1024, 768, 512, 256, 128))

    def body(*refs):
        acc = _dot_nt(refs[0][...], refs[1][...])
        if bias is not None:
            acc = acc + refs[2][...]
        refs[-1][...] = acc.astype(out_dtype)

    in_specs = [pl.BlockSpec((tm, k), lambda j, i: (i, 0)), pl.BlockSpec((tn, k), lambda j, i: (j, 0))]
    args = [a, b]
    if bias is not None:
        in_specs.append(pl.BlockSpec((1, tn), lambda j, i: (0, j)))
        args.append(bias)
    return _call(body, name=name, grid=(n // tn, m // tm), in_specs=in_specs,
                 out_specs=pl.BlockSpec((tm, tn), lambda j, i: (i, j)),
                 out_shape=jax.ShapeDtypeStruct((m, n), out_dtype), args=args,
                 sem=("parallel", "parallel"), carry=carry)


class _Tail:
    def __init__(self, rows, vecs, outs, fn):
        self.rows, self.vecs, self.outs, self.fn = list(rows), list(vecs), list(outs), fn


def _mm_nn(pairs, name, out_dtype, bias=None, carry=None, tail=None):
    m, k = pairs[0][0].shape
    n = pairs[0][1].shape[1]
    n_p = len(pairs)
    tm = _pick(m, (512, 256, 128))
    tk = k if n_p == 1 else _pick(k, (1408, 1152, 1024, 768, 512, 256, 128))
    nk = k // tk
    n_b = 0 if bias is None else 1
    n_r, n_v = (len(tail.rows), len(tail.vecs)) if tail else (0, 0)
    n_in = 2 * n_p + n_b + n_r + n_v
    n_main = 0 if out_dtype is None else 1

    def finish(acc, refs, first_tile):
        if bias is not None:
            acc = acc + refs[2 * n_p][...]
        outs = refs[n_in:-1]
        if n_main:
            outs[0][...] = acc.astype(out_dtype)
        if tail is None:
            return
        rows = [r[...] for r in refs[2 * n_p + n_b:2 * n_p + n_b + n_r]]
        vecs = [v[...] for v in refs[2 * n_p + n_b + n_r:n_in]]
        vals = tail.fn(acc, rows, vecs)
        for ref, val, (dtype, kind) in zip(outs[n_main:], vals, tail.outs):
            if kind == "row":
                ref[...] = val.astype(dtype)
            else:
                @pl.when(first_tile)
                def _(ref=ref):
                    ref[...] = jnp.zeros_like(ref)

                ref[...] += val

    def body(*refs):
        acc_ref = refs[-1]
        kk, i = pl.program_id(0), pl.program_id(1)
        part = _dot(refs[0][...], refs[1][...])
        for p in range(1, n_p):
            part = part + _dot(refs[2 * p][...], refs[2 * p + 1][...])
        if nk == 1:
            finish(part, refs, i == 0)
            return
        rows = pl.ds(pl.multiple_of(i * tm, tm), tm)

        @pl.when(kk == 0)
        def _():
            acc_ref[rows, :] = part

        if nk > 2:
            @pl.when((kk > 0) & (kk < nk - 1))
            def _():
                acc_ref[rows, :] += part

        @pl.when(kk == nk - 1)
        def _():
            finish(acc_ref[rows, :] + part, refs, i == 0)

    def last_only(kk, i):
        return (jnp.where(kk == nk - 1, i, 0), 0)

    row_spec = pl.BlockSpec((tm, n), last_only)
    vec_spec = pl.BlockSpec((1, n), lambda kk, i: (0, 0))
    in_specs, args = [], []
    for a, b in pairs:
        in_specs += [pl.BlockSpec((tm, tk), lambda kk, i: (i, kk)), pl.BlockSpec((tk, n), lambda kk, i: (kk, 0))]
        args += [a, b]
    if bias is not None:
        in_specs.append(vec_spec)
        args.append(bias)
    out_specs = [row_spec] * n_main
    out_shape = [jax.ShapeDtypeStruct((m, n), out_dtype)] if n_main else []
    if tail:
        in_specs += [row_spec] * n_r + [vec_spec] * n_v
        args += tail.rows + tail.vecs
        for dtype, kind in tail.outs:
            if kind == "row":
                out_specs.append(row_spec)
                out_shape.append(jax.ShapeDtypeStruct((m, n), dtype))
            else:
                width = n if kind == "sum" else 1
                out_specs.append(pl.BlockSpec((1, width), lambda kk, i: (0, 0)))
                out_shape.append(jax.ShapeDtypeStruct((1, width), dtype))
    if tail is None:
        out_specs, out_shape = out_specs[0], out_shape[0]
    return _call(body, name=name, grid=(nk, m // tm), in_specs=in_specs, out_specs=out_specs,
                 out_shape=out_shape, args=args,
                 scratch=[pltpu.VMEM((m, n) if nk > 1 else (8, LANES), F32)],
                 sem=("arbitrary", "arbitrary"), carry=carry)


def _rms(v):
    return lax.rsqrt(jnp.mean(v * v, axis=-1, keepdims=True) + EPS)


def _col(v):
    return jnp.sum(v, axis=0, keepdims=True)


def _tail_post_pre(x, g_post, gate, weight, g_pre, scale, shift):
    def fn(y, rows, vecs):
        (xv,), (gp, gt, g, sc, sh) = rows, vecs
        xo = xv + (weight * gt) * ((y * _rms(y)) * gp)
        return xo, ((xo * _rms(xo)) * g) * (1.0 + sc) + sh

    return _Tail([x], [g_post, gate, g_pre, scale, shift], [(F32, "row"), (BF16, "row")], fn)


def _tail_post_loss(x, target, g, gate, weight):
    def fn(y, rows, vecs):
        (xv, tv), (gv, gt) = rows, vecs
        r = _rms(y)
        yn = y * r
        err = (xv + (weight * gt) * (yn * gv)) - tv
        do = err * (1.0 / y.shape[1])
        dyn = do * ((weight * gt) * gv)
        dy = r * (dyn - yn * jnp.mean(dyn * yn, axis=-1, keepdims=True))
        return do, dy, 0.5 * _col(jnp.mean(err * err, axis=-1, keepdims=True)), _col(do * yn)

    return _Tail([x, target], [g, gate], [(F32, "row"), (BF16, "row"), (F32, "one"), (F32, "sum")], fn)


def _tail_pre_bwd(x, dres, g_pre, scale):
    def fn(dh, rows, vecs):
        (xv, dr), (g, sc) = rows, vecs
        r = _rms(xv)
        n = xv * r
        dn = dh * (g * (1.0 + sc))
        return dr + r * (dn - n * jnp.mean(dn * n, axis=-1, keepdims=True)), _col(dh * n), _col(dh)

    return _Tail([x, dres], [g_pre, scale], [(F32, "row"), (F32, "sum"), (F32, "sum")], fn)


def _tail_pre_post_bwd(x, dres, y, g_pre, scale, g_post, gate, weight):
    def fn(dh, rows, vecs):
        (xv, dr, yv), (g, sc, gp, gt) = rows, vecs
        r = _rms(xv)
        n = xv * r
        dn = dh * (g * (1.0 + sc))
        dx = dr + r * (dn - n * jnp.mean(dn * n, axis=-1, keepdims=True))
        ry = _rms(yv)
        yn = yv * ry
        dyn = dx * ((weight * gt) * gp)
        dy = ry * (dyn - yn * jnp.mean(dyn * yn, axis=-1, keepdims=True))
        return dx, dy, _col(dh * n), _col(dh), _col(dx * yn), _col(dy)

    return _Tail([x, dres, y], [g_pre, scale, g_post, gate],
                 [(F32, "row"), (BF16, "row")] + [(F32, "sum")] * 4, fn)


def _mm_tn_pair(a, b, name, col_sums=False):
    k, m = a.shape
    n = b.shape[1]
    rows = m // N_DEV
    n_chip = N_DEV // 2
    tm = 4 * rows
    tk = _pick(k, (512, 256, 128))
    nk = k // tk

    def body(a_ref, b_ref, p_ref, own_ref, *rest):
        acc_ref, keep_ref, send_ref, land_ref, send_sems, recv_sems = rest[-6:]
        i, kk = pl.program_id(0), pl.program_id(1)
        x, y, c = _mesh_pos()
        if col_sums:
            cs_ref = rest[0]
            part = jnp.sum(a_ref[...].astype(F32), axis=0, keepdims=True)

            @pl.when(kk == 0)
            def _():
                cs_ref[...] = part

            @pl.when(kk > 0)
            def _():
                cs_ref[...] += part

        def push(chip):
            return pltpu.make_async_remote_copy(
                src_ref=send_ref.at[chip], dst_ref=land_ref.at[chip], send_sem=send_sems.at[chip],
                recv_sem=recv_sems.at[chip], device_id=(x, y, 1 - c), device_id_type=MESH)

        if nk == 1:
            acc = _dot_tn(a_ref[...], b_ref[...])
        else:
            @pl.when(kk == 0)
            def _():
                acc_ref[...] = jnp.zeros_like(acc_ref)

            acc_ref[...] += _dot_tn(a_ref[...], b_ref[...])
            acc = acc_ref

        for t in range(2):
            @pl.when((kk == nk - 1) & (i == t))
            def _(t=t):
                for ob in range(4):
                    chip, core = 2 * t + ob // 2, ob % 2
                    blk = acc[ob * rows:(ob + 1) * rows, :]

                    @pl.when(c == core)
                    def _(chip=chip, blk=blk):
                        keep_ref[chip] = blk

                    @pl.when(c != core)
                    def _(chip=chip, blk=blk):
                        send_ref[chip] = blk.astype(BF16)
                        push(chip).start()

        @pl.when((kk == nk - 1) & (i == 1))
        def _():
            for chip in range(n_chip):
                push(chip).wait_recv()
                val = (keep_ref[chip] + land_ref[chip].astype(F32)).astype(BF16)
                p_ref[chip * rows:(chip + 1) * rows, :] = val

                @pl.when(2 * x + y == chip)
                def _(val=val):
                    own_ref[...] = val

            for chip in range(n_chip):
                push(chip).wait_send()

    out_specs = [pl.BlockSpec((n_chip * rows, n), lambda i, kk: (0, 0)), pl.BlockSpec((rows, n), lambda i, kk: (0, 0))]
    out_shape = [jax.ShapeDtypeStruct((n_chip * rows, n), BF16), jax.ShapeDtypeStruct((rows, n), BF16)]
    if col_sums:
        out_specs.append(pl.BlockSpec((1, tm), lambda i, kk: (0, i)))
        out_shape.append(jax.ShapeDtypeStruct((1, m), F32))
    return _call(body, name=name, grid=(2, nk),
                 in_specs=[pl.BlockSpec((tk, tm), lambda i, kk: (kk, i)), pl.BlockSpec((tk, n), lambda i, kk: (kk, 0))],
                 out_specs=out_specs, out_shape=out_shape, args=[a, b],
                 scratch=[pltpu.VMEM((tm, n) if nk > 1 else (8, LANES), F32), pltpu.VMEM((n_chip, rows, n), F32),
                          pltpu.VMEM((n_chip, rows, n), BF16), pltpu.VMEM((n_chip, rows, n), BF16),
                          pltpu.SemaphoreType.DMA((n_chip,)), pltpu.SemaphoreType.DMA((n_chip,))],
                 sem=("arbitrary", "arbitrary"))


def _ffn_up(h, wg_t, wu_t, name, carry=None):
    s, d = h.shape
    f = wg_t.shape[0]
    tm = _pick(s, (512, 256, 128))
    tf = _pick(f, (1408, 1024, 512, 256, 128))

    def body(h_ref, wg_ref, wu_ref, a_ref, b_ref, u_ref):
        hh = h_ref[...]
        for lo, hi in _pieces(tf):
            a = _dot_nt(hh, wg_ref[lo:hi, :])
            b = _dot_nt(hh, wu_ref[lo:hi, :])
            a_ref[:, lo:hi] = a.astype(BF16)
            b_ref[:, lo:hi] = b.astype(BF16)
            u_ref[:, lo:hi] = ((a * _sigmoid(a)) * b).astype(BF16)

    w_spec = pl.BlockSpec((tf, d), lambda j, i: (j, 0))
    o_spec = pl.BlockSpec((tm, tf), lambda j, i: (i, j))
    o_shape = jax.ShapeDtypeStruct((s, f), BF16)
    return _call(body, name=name, grid=(f // tf, s // tm),
                 in_specs=[pl.BlockSpec((tm, d), lambda j, i: (i, 0)), w_spec, w_spec],
                 out_specs=(o_spec, o_spec, o_spec), out_shape=(o_shape, o_shape, o_shape),
                 args=[h, wg_t, wu_t], sem=("parallel", "parallel"), carry=carry)


def _ffn_down_bwd(dy, wd, a, b, name, carry=None):
    s, d = dy.shape
    f = wd.shape[0]
    tm = _pick(s, (512, 256, 128))
    tf = _pick(f, (1408, 1024, 512, 256, 128))

    def body(dy_ref, wd_ref, a_ref, b_ref, da_ref, db_ref):
        dyv = dy_ref[...]
        for lo, hi in _pieces(tf):
            du = _dot_nt(dyv, wd_ref[lo:hi, :])
            a = a_ref[:, lo:hi].astype(F32)
            b = b_ref[:, lo:hi].astype(F32)
            sig = _sigmoid(a)
            da_ref[:, lo:hi] = (du * b * (sig * (1.0 + a * (1.0 - sig)))).astype(BF16)
            db_ref[:, lo:hi] = (du * (a * sig)).astype(BF16)

    t_spec = pl.BlockSpec((tm, tf), lambda j, i: (i, j))
    o_shape = jax.ShapeDtypeStruct((s, f), BF16)
    return _call(body, name=name, grid=(f // tf, s // tm),
                 in_specs=[pl.BlockSpec((tm, d), lambda j, i: (i, 0)), pl.BlockSpec((tf, d), lambda j, i: (j, 0)),
                           t_spec, t_spec],
                 out_specs=(t_spec, t_spec), out_shape=(o_shape, o_shape), args=[dy, wd, a, b],
                 sem=("parallel", "parallel"), carry=carry)


def _row_tile(s):
    return _pick(s, (256, 128, 64))


def _vec_spec(d):
    return pl.BlockSpec((1, d), lambda i: (0, 0))


def _pre_norm(x, g, scale, shift, name):
    s, d = x.shape
    ts = _row_tile(s)

    def body(x_ref, g_ref, sc_ref, sh_ref, h_ref):
        xv = x_ref[...]
        r = lax.rsqrt(jnp.mean(xv * xv, axis=-1, keepdims=True) + EPS)
        h_ref[...] = (((xv * r) * g_ref[...]) * (1.0 + sc_ref[...]) + sh_ref[...]).astype(BF16)

    row = pl.BlockSpec((ts, d), lambda i: (i, 0))
    return _call(body, name=name, grid=(s // ts,), in_specs=[row, _vec_spec(d), _vec_spec(d), _vec_spec(d)],
                 out_specs=row, out_shape=jax.ShapeDtypeStruct((s, d), BF16), args=[x, g, scale, shift],
                 sem=("parallel",))


def _group_norm_cat(oa, ob, ga, gb):
    s = oa.shape[0]
    ts = _row_tile(s)

    def body(oa_ref, ob_ref, ga_ref, gb_ref, y_ref):
        for o_ref, g_ref, lo, w in ((oa_ref, ga_ref, 0, QA), (ob_ref, gb_ref, QA, QB)):
            ov = o_ref[...]
            r = lax.rsqrt(jnp.mean(ov * ov, axis=-1, keepdims=True) + EPS)
            y_ref[:, lo:lo + w] = ((ov * r) * g_ref[...]).astype(BF16)

    return _call(body, name="group_norm_cat", grid=(s // ts,),
                 in_specs=[pl.BlockSpec((ts, QA), lambda i: (i, 0)), pl.BlockSpec((ts, QB), lambda i: (i, 0)),
                           _vec_spec(QA), _vec_spec(QB)],
                 out_specs=pl.BlockSpec((ts, QA + QB), lambda i: (i, 0)),
                 out_shape=jax.ShapeDtypeStruct((s, QA + QB), BF16), args=[oa, ob, ga, gb], sem=("parallel",))


def _group_norm_bwd(dy, oa, ob, ga, gb):
    s = oa.shape[0]
    ts = _row_tile(s)

    def body(dy_ref, oa_ref, ob_ref, ga_ref, gb_ref, doa_ref, dob_ref, dga_ref, dgb_ref):
        @pl.when(pl.program_id(0) == 0)
        def _():
            dga_ref[...] = jnp.zeros_like(dga_ref)
            dgb_ref[...] = jnp.zeros_like(dgb_ref)

        for o_ref, g_ref, do_ref, dg_ref, lo, w in ((oa_ref, ga_ref, doa_ref, dga_ref, 0, QA),
                                                    (ob_ref, gb_ref, dob_ref, dgb_ref, QA, QB)):
            ov = o_ref[...]
            dyv = dy_ref[:, lo:lo + w]
            r = lax.rsqrt(jnp.mean(ov * ov, axis=-1, keepdims=True) + EPS)
            n = ov * r
            dn = dyv * g_ref[...]
            do_ref[...] = r * (dn - n * jnp.mean(dn * n, axis=-1, keepdims=True))
            dg_ref[...] += jnp.sum(dyv * n, axis=0, keepdims=True)

    ra = pl.BlockSpec((ts, QA), lambda i: (i, 0))
    rb = pl.BlockSpec((ts, QB), lambda i: (i, 0))
    return _call(body, name="group_norm_bwd", grid=(s // ts,),
                 in_specs=[pl.BlockSpec((ts, QA + QB), lambda i: (i, 0)), ra, rb, _vec_spec(QA), _vec_spec(QB)],
                 out_specs=(ra, rb, _vec_spec(QA), _vec_spec(QB)),
                 out_shape=(jax.ShapeDtypeStruct((s, QA), F32), jax.ShapeDtypeStruct((s, QB), F32),
                            jax.ShapeDtypeStruct((1, QA), F32), jax.ShapeDtypeStruct((1, QB), F32)),
                 args=[dy, oa, ob, ga, gb], sem=("arbitrary",))


def _n_variants(n_back):
    return -(-n_back // QG) + 1


def _alibi_bias():
    i = np.arange(QROWS)[:, None]
    j = np.arange((QG + BACK_A) * CHUNK)[None, :]
    dist = np.abs(BACK_A * CHUNK + i - j).astype(np.float32)
    dc = j // CHUNK - i // CHUNK
    valid = (dc >= 0) & (dc <= BACK_A)
    slopes = np.array([2.0 ** (-8.0 * (h + 1) / H_A) for h in range(H_A)], dtype=np.float32)
    bias = -slopes[:, None, None] * dist[None]
    out = [np.where((valid & (j >= (BACK_A - QG * v) * CHUNK))[None], bias, np.float32(NEG_INF))
           for v in range(_n_variants(BACK_A))]
    return jnp.asarray(np.stack(out).astype(np.float32))


def _rel_index_matrix():
    cc = np.arange(SKEW)
    dist = np.where(cc < SKEW - QROWS, BACK_B * CHUNK - cc, BACK_B * CHUNK + SKEW - cc)
    idx = np.clip(dist, -REL_CLIP, REL_CLIP) + REL_CLIP
    m = np.zeros((SKEW, N_REL), np.float32)
    m[cc, idx] = 1.0
    return jnp.asarray(m)


def _toeplitz_bias(vec, carry=None):
    lk = (QG + BACK_B) * CHUNK
    nv = _n_variants(BACK_B)

    def body(v_ref, o_ref):
        xv = jnp.broadcast_to(v_ref[0], (QROWS, SKEW))
        row = lax.broadcasted_iota(jnp.int32, (QROWS, SKEW), 0)
        for bit in range(QROWS.bit_length() - 1):
            xv = jnp.where((row >> bit) & 1 == 1, pltpu.roll(xv, 1 << bit, 1), xv)
        ri = lax.broadcasted_iota(jnp.int32, (QROWS, lk), 0) // CHUNK
        col = lax.broadcasted_iota(jnp.int32, (QROWS, lk), 1)
        ci = col // CHUNK
        valid = (ci - ri >= 0) & (ci - ri <= BACK_B)
        for v in range(nv):
            o_ref[v, 0] = jnp.where(valid & (col >= (BACK_B - QG * v) * CHUNK), xv[:, :lk], NEG_INF)

    return _call(body, name="toeplitz_bias", grid=(H_B,),
                 in_specs=[pl.BlockSpec((1, 1, SKEW), lambda h: (h, 0, 0))],
                 out_specs=pl.BlockSpec((nv, 1, QROWS, lk), lambda h: (0, h, 0, 0)),
                 out_shape=jax.ShapeDtypeStruct((nv, H_B, QROWS, lk), F32), args=[vec], sem=("parallel",),
                 carry=carry)


def _diagonal_sums(dbias):
    lk = dbias.shape[2]

    def body(d_ref, o_ref):
        xp = jnp.concatenate([d_ref[0], jnp.zeros((QROWS, SKEW - lk), F32)], axis=1)
        xv = xp[0:CHUNK]
        for q in range(1, QG):
            xv = xv + pltpu.roll(xp[q * CHUNK:(q + 1) * CHUNK], SKEW - q * CHUNK, 1)
        row = lax.broadcasted_iota(jnp.int32, (CHUNK, SKEW), 0)
        for bit in range(CHUNK.bit_length() - 1):
            xv = jnp.where((row >> bit) & 1 == 1, pltpu.roll(xv, SKEW - (1 << bit), 1), xv)
        o_ref[0] = jnp.sum(xv, axis=0, keepdims=True)

    return _call(body, name="diagonal_sums", grid=(H_B,),
                 in_specs=[pl.BlockSpec((1, QROWS, lk), lambda h: (h, 0, 0))],
                 out_specs=pl.BlockSpec((1, 1, SKEW), lambda h: (h, 0, 0)),
                 out_shape=jax.ShapeDtypeStruct((H_B, 1, SKEW), F32), args=[dbias], sem=("parallel",))


def _attn_common(s, n_back, gqa, q_col, k_col, v_col, TPS):
    assert q_col % TPS == 0 and (gqa or (k_col % TPS == 0 and v_col % TPS == 0)), "blocks of TPS lane tiles"
    lk = (QG + n_back) * CHUNK
    pad = n_back * CHUNK
    wide = TPS * LANES
    q_spec = pl.BlockSpec((QROWS, wide), lambda t, g: (g, q_col // TPS + t))
    if gqa:
        k_spec = pl.BlockSpec((s, LANES), lambda t, g: (0, k_col))
        v_spec = pl.BlockSpec((s, LANES), lambda t, g: (0, v_col))
    else:
        k_spec = pl.BlockSpec((s, wide), lambda t, g: (0, k_col // TPS + t))
        v_spec = pl.BlockSpec((s, wide), lambda t, g: (0, v_col // TPS + t))
    last_variant = _n_variants(n_back) - 1
    bias_spec = pl.BlockSpec((None, 2 * TPS, QROWS, lk), lambda t, g: (jnp.minimum(g, last_variant), t, 0, 0))
    tile_spec = pl.BlockSpec((QROWS, wide), lambda t, g: (g, t))
    return lk, pad, q_spec, k_spec, v_spec, bias_spec, tile_spec


def _attention_fwd(proj, bias, sinks, *, n_back, gqa, q_col, k_col, v_col, TPS, name, carry=None):
    s = proj.shape[0]
    lk, pad, q_spec, k_spec, v_spec, bias_spec, tile_spec = _attn_common(s, n_back, gqa, q_col, k_col, v_col, TPS)
    n_t, n_g = 512 // (TPS * LANES), s // QROWS
    kv_wide = LANES if gqa else TPS * LANES

    def body(*refs):
        if gqa:
            q_ref, k_ref, v_ref, bias_ref, sink_ref, o_ref, l_ref, kpad, vpad = refs
        else:
            q_ref, k_ref, v_ref, bias_ref, o_ref, l_ref, kpad, vpad = refs
        t, g = pl.program_id(0), pl.program_id(1)

        @pl.when(g == 0)
        def _():
            kpad[0:pad, :] = jnp.zeros((pad, kv_wide), BF16)
            vpad[0:pad, :] = jnp.zeros((pad, kv_wide), BF16)
            kpad[pad:, :] = k_ref[...]
            vpad[pad:, :] = v_ref[...]

        start = pl.multiple_of(g * QROWS, QROWS)
        half = lax.broadcasted_iota(jnp.int32, (QROWS, LANES), 1) // HEAD_DIM
        for tt in range(TPS):
            lanes = slice(tt * LANES, (tt + 1) * LANES)
            kv_lanes = slice(0, LANES) if gqa else lanes
            kb = kpad[pl.ds(start, lk), kv_lanes]
            vb = vpad[pl.ds(start, lk), kv_lanes]
            q = q_ref[:, lanes] * (HEAD_DIM ** -0.5)
            if gqa:
                hk = (TPS * t + tt) // 2
                q_rolled = pltpu.roll(q.astype(F32), HEAD_DIM, 1).astype(BF16)
            outs, lses = [], []
            for e in range(2):
                if gqa:
                    kv_half = hk
                    src = jnp.where(hk == e, q, q_rolled)
                else:
                    kv_half = e
                    src = q
                qm = jnp.where(half == kv_half, src, jnp.zeros_like(src))
                sc = _dot_nt(qm, kb) + bias_ref[2 * tt + e]
                m = jnp.max(sc, axis=-1, keepdims=True)
                if gqa:
                    sk = sink_ref[2 * (TPS * t + tt) + e]
                    m = jnp.maximum(m, sk)
                p = jnp.exp(sc - m)
                l = jnp.sum(p, axis=-1, keepdims=True)
                if gqa:
                    l = l + jnp.exp(sk - m)
                pn = p / l
                outs.append(_dot(pn.astype(BF16), vb))
                lses.append(m + jnp.log(l))
            if gqa:
                same = jnp.where(hk == 0, outs[0], outs[1])
                other = jnp.where(hk == 0, outs[1], outs[0])
                o_ref[:, lanes] = jnp.where(half == hk, same, pltpu.roll(other, HEAD_DIM, 1))
            else:
                o_ref[:, lanes] = jnp.where(half == 0, outs[0], outs[1])
            l_ref[:, lanes] = jnp.where(half == 0, lses[0], lses[1])

    in_specs = [q_spec, k_spec, v_spec, bias_spec] + ([SMEM_SPEC] if gqa else [])
    args = [proj, proj, proj, bias] + ([sinks] if gqa else [])
    o_shape = jax.ShapeDtypeStruct((s, 512), F32)
    return _call(body, name=name, grid=(n_t, n_g), in_specs=in_specs, out_specs=(tile_spec, tile_spec),
                 out_shape=(o_shape, o_shape), args=args,
                 scratch=[pltpu.VMEM((s + pad, kv_wide), BF16), pltpu.VMEM((s + pad, kv_wide), BF16)],
                 sem=("arbitrary", "arbitrary"), carry=carry)


def _attention_bwd(proj, bias, sinks, do, lse, *, n_back, gqa, q_col, k_col, v_col, TPS, name, carry=None):
    s = proj.shape[0]
    lk, pad, q_spec, k_spec, v_spec, bias_spec, tile_spec = _attn_common(s, n_back, gqa, q_col, k_col, v_col, TPS)
    n_t, n_g = 512 // (TPS * LANES), s // QROWS
    kv_wide = LANES if gqa else TPS * LANES

    def body(*refs):
        if gqa:
            (q_ref, k_ref, v_ref, bias_ref, sink_ref, do_ref, l_ref,
             dq_ref, dk_ref, dv_ref, dsink_ref, kpad, vpad, dkpad, dvpad) = refs
        else:
            (q_ref, k_ref, v_ref, bias_ref, do_ref, l_ref,
             dq_ref, dk_ref, dv_ref, dbias_ref, kpad, vpad, dkpad, dvpad) = refs
        t, g = pl.program_id(0), pl.program_id(1)

        @pl.when(g == 0)
        def _():
            kpad[0:pad, :] = jnp.zeros((pad, kv_wide), BF16)
            vpad[0:pad, :] = jnp.zeros((pad, kv_wide), BF16)
            kpad[pad:, :] = k_ref[...]
            vpad[pad:, :] = v_ref[...]
            if gqa:
                dsink_ref[...] = jnp.zeros_like(dsink_ref)
            else:
                dbias_ref[...] = jnp.zeros_like(dbias_ref)

        @pl.when((g == 0) & (t == 0) if gqa else g == 0)
        def _():
            dkpad[...] = jnp.zeros_like(dkpad)
            dvpad[...] = jnp.zeros_like(dvpad)

        start = pl.multiple_of(g * QROWS, QROWS)
        half = lax.broadcasted_iota(jnp.int32, (QROWS, LANES), 1) // HEAD_DIM
        for tt in range(TPS):
            lanes = slice(tt * LANES, (tt + 1) * LANES)
            kv_lanes = slice(0, LANES) if gqa else lanes
            kb = kpad[pl.ds(start, lk), kv_lanes]
            vb = vpad[pl.ds(start, lk), kv_lanes]
            q = q_ref[:, lanes]
            dov = do_ref[:, lanes]
            lv = l_ref[:, lanes]
            if gqa:
                hk = (TPS * t + tt) // 2
                q_rolled = pltpu.roll(q.astype(F32), HEAD_DIM, 1).astype(BF16)
                do_rolled = pltpu.roll(dov, HEAD_DIM, 1)
            dqs = []
            dk_acc = jnp.zeros((lk, LANES), F32)
            dv_acc = jnp.zeros((lk, LANES), F32)
            for e in range(2):
                if gqa:
                    kv_half = hk
                    src = jnp.where(hk == e, q, q_rolled)
                    do_src = jnp.where(hk == e, dov, do_rolled)
                else:
                    kv_half = e
                    src = q
                    do_src = dov
                qm = jnp.where(half == kv_half, src, jnp.zeros_like(src))
                dom = jnp.where(half == kv_half, do_src, 0.0).astype(BF16)
                lcol = jnp.max(jnp.where(half == e, lv, -jnp.inf), axis=-1, keepdims=True)
                sc = _dot_nt(qm * (HEAD_DIM ** -0.5), kb) + bias_ref[2 * tt + e]
                pn = jnp.exp(sc - lcol)
                dp = _dot_nt(dom, vb)
                delta = jnp.sum(pn * dp, axis=-1, keepdims=True)
                ds = pn * (dp - delta)
                if gqa:
                    p_sink = jnp.exp(sink_ref[2 * (TPS * t + tt) + e] - lcol)
                    dsk = -jnp.sum(p_sink * delta, axis=0, keepdims=True)
                    row = 2 * tt + e
                    dsink_ref[0, row:row + 1, :] += jnp.broadcast_to(dsk, (1, LANES))
                else:
                    dbias_ref[2 * tt + e] += ds
                dsb = (ds * (HEAD_DIM ** -0.5)).astype(BF16)
                dqs.append(_dot(dsb, kb))
                dk_acc = dk_acc + _dot_tn(dsb, qm)
                dv_acc = dv_acc + _dot_tn(pn.astype(BF16), dom)
            dkpad[pl.ds(start, lk), kv_lanes] += dk_acc
            dvpad[pl.ds(start, lk), kv_lanes] += dv_acc
            if gqa:
                same = jnp.where(hk == 0, dqs[0], dqs[1])
                other = jnp.where(hk == 0, dqs[1], dqs[0])
                dq_ref[:, lanes] = jnp.where(half == hk, same, pltpu.roll(other, HEAD_DIM, 1)).astype(BF16)
            else:
                dq_ref[:, lanes] = jnp.where(half == 0, dqs[0], dqs[1]).astype(BF16)

        @pl.when((g == n_g - 1) & (t == n_t - 1) if gqa else g == n_g - 1)
        def _():
            dk_ref[...] = dkpad[pad:, :].astype(BF16)
            dv_ref[...] = dvpad[pad:, :].astype(BF16)

    in_specs = [q_spec, k_spec, v_spec, bias_spec] + ([SMEM_SPEC] if gqa else []) + [tile_spec, tile_spec]
    args = [proj, proj, proj, bias] + ([sinks] if gqa else []) + [do, lse]
    if gqa:
        kv_out = pl.BlockSpec((s, LANES), lambda t, g: (0, 0))
        kv_shape = jax.ShapeDtypeStruct((s, LANES), BF16)
        extra_spec = pl.BlockSpec((1, 8, LANES), lambda t, g: (t, 0, 0))
        extra_shape = jax.ShapeDtypeStruct((n_t, 8, LANES), F32)
    else:
        kv_out = pl.BlockSpec((s, kv_wide), lambda t, g: (0, t))
        kv_shape = jax.ShapeDtypeStruct((s, 512), BF16)
        extra_spec = pl.BlockSpec((2 * TPS, QROWS, lk), lambda t, g: (t, 0, 0))
        extra_shape = jax.ShapeDtypeStruct(bias.shape[1:], F32)
    return _call(body, name=name, grid=(n_t, n_g), in_specs=in_specs,
                 out_specs=(tile_spec, kv_out, kv_out, extra_spec),
                 out_shape=(jax.ShapeDtypeStruct((s, 512), BF16), kv_shape, kv_shape, extra_shape), args=args,
                 scratch=[pltpu.VMEM((s + pad, kv_wide), BF16), pltpu.VMEM((s + pad, kv_wide), BF16),
                          pltpu.VMEM((s + pad, kv_wide), F32), pltpu.VMEM((s + pad, kv_wide), F32)],
                 sem=("arbitrary", "arbitrary"), carry=carry)


def _sum_rows8(g):
    n = g.shape[2]

    def body(g_ref, o_ref):
        acc = g_ref[0]
        for j in range(1, N_DEV):
            acc = acc + g_ref[j]
        o_ref[...] = acc

    return pl.pallas_call(
        body, name="sum_small_grads", in_specs=[VMEM_SPEC], out_specs=VMEM_SPEC,
        out_shape=jax.ShapeDtypeStruct((1, n), F32), compiler_params=_params(),
    )(g)


def _ada_weight_grad(sc_t, dmod_cols):
    d = sc_t.shape[0]
    w = dmod_cols.shape[1]
    td = _pick(d, (256, 128))

    def body(sc_ref, dm_ref, o_ref):
        scv = sc_ref[...]
        dmv = dm_ref[...]
        acc = scv[:, 0:1] * dmv[0:1, :]
        for b in range(1, N_DEV):
            acc = acc + scv[:, b:b + 1] * dmv[b:b + 1, :]
        o_ref[...] = acc

    return _call(body, name="ada_weight_grad", grid=(d // td,),
                 in_specs=[pl.BlockSpec((td, N_DEV), lambda i: (i, 0)), pl.BlockSpec((N_DEV, w), lambda i: (0, 0))],
                 out_specs=pl.BlockSpec((td, w), lambda i: (i, 0)), out_shape=jax.ShapeDtypeStruct((d, w), F32),
                 args=[sc_t, dmod_cols], sem=("parallel",))


def _adamw_update(w, gv, m, v):
    nm = ADAM_B1 * m + (1.0 - ADAM_B1) * gv
    nv = ADAM_B2 * v + (1.0 - ADAM_B2) * (gv * gv)
    m_hat = nm / (1.0 - ADAM_B1 ** ADAM_STEP)
    v_hat = nv / (1.0 - ADAM_B2 ** ADAM_STEP)
    return -ADAM_LR * (m_hat / (jnp.sqrt(v_hat) + ADAM_EPS) + ADAM_WD * w), nm, nv


def _adamw(w, g, m, v, name):
    rows, cols = w.shape
    tr = _pick(rows, (256, 176, 128, 88, 64)) if rows > 256 else rows

    def body(w_ref, g_ref, m_ref, v_ref, d_ref, nm_ref, nv_ref):
        d_ref[...], nm_ref[...], nv_ref[...] = _adamw_update(w_ref[...], g_ref[...], m_ref[...], v_ref[...])

    spec = pl.BlockSpec((tr, cols), lambda i: (i, 0))
    shape = jax.ShapeDtypeStruct((rows, cols), F32)
    return _call(body, name=name, grid=(rows // tr,), in_specs=[spec] * 4, out_specs=(spec, spec, spec),
                 out_shape=(shape, shape, shape), args=[w, g, m, v], sem=("parallel",))


def _adamw_from_slots(w, own, slots, m, v, name):
    n_slots, rows, k = slots.shape

    def body(o_ref, s_ref, w_ref, m_ref, v_ref, g_ref, d_ref, nm_ref, nv_ref):
        gv = o_ref[...].astype(F32)
        for j in range(n_slots):
            gv = gv + s_ref[j].astype(F32)
        g_ref[...] = gv
        d_ref[...], nm_ref[...], nv_ref[...] = _adamw_update(w_ref[...], gv, m_ref[...], v_ref[...])

    tr = rows // 2 if rows % 32 == 0 else rows
    spec = pl.BlockSpec((tr, k), lambda i: (i, 0))
    shape = jax.ShapeDtypeStruct((rows, k), F32)
    return _call(body, name=name, grid=(rows // tr,),
                 in_specs=[spec, pl.BlockSpec((n_slots, tr, k), lambda i: (0, i, 0)), spec, spec, spec],
                 out_specs=(spec, spec, spec, spec), out_shape=(shape, shape, shape, shape),
                 args=[own, slots, w, m, v], sem=("parallel",))


def _adamw_small(g, w, m, v, sizes):
    n = w.shape[1]
    offs, off = [], 0
    for size in sizes:
        offs.append(off)
        off += size + (-size % LANES)

    def body(g_ref, w_ref, m_ref, v_ref, *out_refs):
        gv = g_ref[:, 0:n]
        dv, nm, nv = _adamw_update(w_ref[...], gv, m_ref[...], v_ref[...])
        for j, (o, size) in enumerate(zip(offs, sizes)):
            for k, val in enumerate((gv, dv, nm, nv)):
                out_refs[4 * j + k][...] = val[:, o:o + size]

    shapes = [jax.ShapeDtypeStruct((1, size), F32) for size in sizes for _ in range(4)]
    return pl.pallas_call(
        body, name="adamw_small", in_specs=[VMEM_SPEC] * 4, out_specs=tuple([VMEM_SPEC] * len(shapes)),
        out_shape=tuple(shapes), compiler_params=_params(),
    )(g, w, m, v)


SMALL = ("b_ada", "g_pre_ffn1", "g_post_ffn1", "g_pre_mix", "b_in", "sinks_a", "rel_bias_b", "g_grp_a",
         "g_grp_b", "b_out", "g_post_mix", "g_pre_ffn2", "g_post_ffn2")
WEIGHTS = ("w_ada", "b_ada", "g_pre_ffn1", "w_gate1", "w_up1", "w_down1", "g_post_ffn1", "g_pre_mix", "w_in",
           "b_in", "sinks_a", "rel_bias_b", "g_grp_a", "g_grp_b", "w_out", "b_out", "g_post_mix", "g_pre_ffn2",
           "w_gate2", "w_up2", "w_down2", "g_post_ffn2")


def kernel(x, c, w_ada, b_ada, g_pre_ffn1, w_gate1, w_up1, w_down1, g_post_ffn1, g_pre_mix, w_in, b_in, sinks_a, rel_bias_b, g_grp_a, g_grp_b, w_out, b_out, g_post_mix, g_pre_ffn2, w_gate2, w_up2, w_down2, g_post_ffn2, loss_target, m_w_ada, m_b_ada, m_g_pre_ffn1, m_w_gate1, m_w_up1, m_w_down1, m_g_post_ffn1, m_g_pre_mix, m_w_in, m_b_in, m_sinks_a, m_rel_bias_b, m_g_grp_a, m_g_grp_b, m_w_out, m_b_out, m_g_post_mix, m_g_pre_ffn2, m_w_gate2, m_w_up2, m_w_down2, m_g_post_ffn2, v_w_ada, v_b_ada, v_g_pre_ffn1, v_w_gate1, v_w_up1, v_w_down1, v_g_post_ffn1, v_g_pre_mix, v_w_in, v_b_in, v_sinks_a, v_rel_bias_b, v_g_grp_a, v_g_grp_b, v_w_out, v_b_out, v_g_post_mix, v_g_pre_ffn2, v_w_gate2, v_w_up2, v_w_down2, v_g_post_ffn2):
    given = dict(locals())
    weights = {n: given[n] for n in WEIGHTS}
    mom_m = {n: given["m_" + n] for n in WEIGHTS}
    mom_v = {n: given["v_" + n] for n in WEIGHTS}

    me = 4 * lax.axis_index("x") + 2 * lax.axis_index("y") + lax.axis_index("c")
    xs = x[0]
    tgt = loss_target[0]
    d_model = xs.shape[1]
    ada_cols = w_ada.shape[2]

    sh = {"wg1": w_gate1[0].T, "wu1": w_up1[0].T, "wd1": w_down1[0], "win": w_in[0].T, "wo": w_out[0],
          "wg2": w_gate2[0].T, "wu2": w_up2[0].T, "wd2": w_down2[0]}
    sh = {k: v.astype(BF16) for k, v in sh.items()}

    def gather(*names):
        return _gather_carry([sh[n] for n in names])

    bias_a = _alibi_bias()
    rel_m = _rel_index_matrix()
    rel_vec = jnp.dot(rel_bias_b[0], rel_m.T, precision=lax.Precision.HIGHEST)
    bias_b, (wg1, wu1) = _toeplitz_bias(rel_vec.reshape(H_B, 1, SKEW), carry=gather("wg1", "wu1"))

    b_cols = lax.dynamic_slice(b_ada, (0, me * ada_cols), (1, ada_cols))
    (sc_all, mod_rows), _ = _ada_forward(c, w_ada[0], b_cols, _Carry([], [], [], lambda *a: None, lambda *a: None))
    mod = mod_rows.reshape(N_MOD, d_model)
    shift1, scale1, gate1, shift2, scale2, gate2, shift3, scale3, gate3 = (mod[i:i + 1] for i in range(N_MOD))

    h1 = _pre_norm(xs, g_pre_ffn1, scale1, shift1, "pre_norm_ffn1")
    (a1, b1, u1), (wd1,) = _ffn_up(h1, wg1, wu1, "ffn_up_ffn1", carry=gather("wd1"))
    (y1, x1, h2), (win,) = _mm_nn(
        [(u1, wd1)], "ffn_down_ffn1", F32, carry=gather("win"),
        tail=_tail_post_pre(xs, g_post_ffn1, gate1, 0.5, g_pre_mix, scale2, shift2))

    proj = _mm_nt(h2, win, "in_proj", BF16, bias=b_in)
    sinks = sinks_a[0]
    cfg_a = dict(n_back=BACK_A, gqa=True, q_col=0, k_col=QA // LANES, v_col=(QA + KVA) // LANES, TPS=TPS_A)
    cfg_b = dict(n_back=BACK_B, gqa=False, q_col=(QA + 2 * KVA) // LANES, k_col=(QA + 2 * KVA + QB) // LANES,
                 v_col=(QA + 2 * KVA + 2 * QB) // LANES, TPS=TPS_B)
    (oa, lse_a), (wg2,) = _attention_fwd(proj, bias_a, sinks, name="attn_a", carry=gather("wg2"), **cfg_a)
    (ob, lse_b), (wu2, wo) = _attention_fwd(proj, bias_b, None, name="attn_b", carry=gather("wu2", "wo"), **cfg_b)
    ycat = _group_norm_cat(oa, ob, g_grp_a, g_grp_b)
    ymix, x2, h3 = _mm_nn([(ycat, wo)], "out_proj", F32, bias=b_out,
                          tail=_tail_post_pre(x1, g_post_mix, gate2, 1.0, g_pre_ffn2, scale3, shift3))

    (a3, b3, u3), (wd2,) = _ffn_up(h3, wg2, wu2, "ffn_up_ffn2", carry=gather("wd2"))

    flights, own = {}, {}

    def grad_pair(key, a_mat, b_mat, name):
        part, own[key] = _mm_tn_pair(a_mat, b_mat, name)
        return part

    def scatter_start(tag, after_vec, **parts):
        names = list(parts)
        sems, p_thru, lands, token = _scatter_start([parts[n] for n in names], "scatter_start_" + tag)
        flights[tag] = (names, sems, p_thru, lands)
        return after_vec + token[0:1, 0:1]

    dx3, dy, loss_part, s1 = _mm_nn([(u3, wd2)], "ffn_down_ffn2", None,
                                    tail=_tail_post_loss(x2, tgt, g_post_ffn2, gate3, 0.5))
    da, db = _ffn_down_bwd(dy, wd2, a3, b3, "ffn_down_bwd_ffn2")
    dwd2 = grad_pair("wd2", u3, dy, "grad_wd_ffn2")
    dwg2 = grad_pair("wg2", da, h3, "grad_wg_ffn2")
    dwu2 = grad_pair("wu2", db, h3, "grad_wu_ffn2")
    g_pre_tied = scatter_start("ffn2", g_pre_ffn2, wd2=dwd2, wg2=dwg2, wu2=dwu2)
    dx2, dymix, s2, s3, s1m, db_out = _mm_nn(
        [(da, wg2), (db, wu2)], "ffn_up_bwd_ffn2", None,
        tail=_tail_pre_post_bwd(x2, dx3, ymix, g_pre_tied, scale3, g_post_mix, gate2, 1.0))
    sm3 = dict(shift=s3, scale=s2 * g_pre_ffn2, gate=0.5 * g_post_ffn2 * s1,
               g_pre=(1.0 + scale3) * s2, g_post=(0.5 * gate3) * s1)

    dycat = _mm_nt(dymix, wo, "out_proj_bwd", F32)
    dwo = grad_pair("wo", ycat, dymix, "grad_wo")
    doa, dob, dg_a, dg_b = _group_norm_bwd(dycat, oa, ob, g_grp_a, g_grp_b)
    dqa, dka, dva, dsink = _attention_bwd(proj, bias_a, sinks, doa, lse_a, name="attn_a_bwd", **cfg_a)
    dqb, dkb, dvb, dbias = _attention_bwd(proj, bias_b, None, dob, lse_b, name="attn_b_bwd", **cfg_b)
    dproj = jnp.concatenate([dqa, dka, dva, dqb, dkb, dvb], axis=1)
    dwin, own["win"], db_in = _mm_tn_pair(dproj, h2, "grad_win", col_sums=True)
    g_pre_tied = scatter_start("mix", g_pre_mix, wo=dwo, win=dwin)
    dx1, dy, s2m, s3m, s1, _ = _mm_nn(
        [(dproj, win)], "in_proj_bwd", None,
        tail=_tail_pre_post_bwd(x1, dx2, y1, g_pre_tied, scale2, g_post_ffn1, gate1, 0.5))
    d_rel = jnp.dot(_diagonal_sums(dbias).reshape(H_B, SKEW), rel_m, precision=lax.Precision.HIGHEST)
    d_sinks = dsink[:, :2 * TPS_A, 0].reshape(1, H_A)

    da, db = _ffn_down_bwd(dy, wd1, a1, b1, "ffn_down_bwd_ffn1")
    dwd1 = grad_pair("wd1", u1, dy, "grad_wd_ffn1")
    dwg1 = grad_pair("wg1", da, h1, "grad_wg_ffn1")
    dwu1 = grad_pair("wu1", db, h1, "grad_wu_ffn1")
    g_pre_tied = scatter_start("ffn1", g_pre_ffn1, wd1=dwd1, wg1=dwg1, wu1=dwu1)
    dx0, s2, s3 = _mm_nn([(da, wg1), (db, wu1)], "ffn_up_bwd_ffn1", None,
                         tail=_tail_pre_bwd(xs, dx1, g_pre_tied, scale1))
    sm1 = dict(shift=s3, scale=s2 * g_pre_ffn1, gate=0.5 * g_post_ffn1 * s1,
               g_pre=(1.0 + scale1) * s2, g_post=(0.5 * gate1) * s1)

    dmod = jnp.concatenate([sm1["shift"], sm1["scale"], sm1["gate"],
                            s3m, s2m * g_pre_mix, g_post_mix * s1m,
                            sm3["shift"], sm3["scale"], sm3["gate"]], axis=1)
    small_parts = {
        "b_ada": dmod, "g_pre_ffn1": sm1["g_pre"], "g_post_ffn1": sm1["g_post"],
        "g_pre_mix": (1.0 + scale2) * s2m, "b_in": db_in, "sinks_a": d_sinks,
        "rel_bias_b": d_rel.reshape(1, H_B * N_REL), "g_grp_a": dg_a, "g_grp_b": dg_b, "b_out": db_out,
        "g_post_mix": gate2 * s1m, "g_pre_ffn2": sm3["g_pre"], "g_post_ffn2": sm3["g_post"]}
    sizes = [small_parts[n].shape[1] for n in SMALL]

    def pack(parts):
        cells = []
        for p in parts:
            cells.append(p)
            if p.shape[1] % LANES:
                cells.append(jnp.zeros((1, -p.shape[1] % LANES), F32))
        return jnp.concatenate(cells, axis=1)

    packed = pack([small_parts[n] for n in SMALL] + [loss_part])
    n_packed = packed.shape[1]
    small_sems, packed_thru, small_land, small_token = _small_gather_start(packed)

    out_g, out_d, out_m, out_v = {}, {}, {}, {}
    groups = (("ffn2", (("w_gate2", "wg2", True), ("w_up2", "wu2", True), ("w_down2", "wd2", False))),
              ("mix", (("w_in", "win", True), ("w_out", "wo", False))),
              ("ffn1", (("w_gate1", "wg1", True), ("w_up1", "wu1", True), ("w_down1", "wd1", False))))
    after = small_token
    for tag, members in groups:
        names, sems, p_thru, lands = flights[tag]
        _, l_done = _scatter_wait(sems, p_thru, lands, after, "scatter_wait_" + tag)
        slots = dict(zip(names, l_done))
        for n, key, transposed in members:
            view = (lambda t: t.T) if transposed else (lambda t: t)
            res = _adamw_from_slots(view(weights[n][0]), own[key], slots[key], view(mom_m[n][0]),
                                    view(mom_v[n][0]), "adamw_" + n)
            out_g[n], out_d[n], out_m[n], out_v[n] = (view(t)[None] for t in res)
            after = res[3]

    packed_done, small_land = _small_gather_wait(small_sems, packed_thru, small_land, after)
    gathered = lax.dynamic_update_slice(small_land, packed_done[None], (me, 0, 0))
    small_sum = _sum_rows8(gathered)
    loss = small_sum[0, n_packed - LANES]
    dmod_cols = lax.dynamic_slice(gathered.reshape(N_DEV, n_packed), (0, me * ada_cols), (N_DEV, ada_cols))
    g_ada = _ada_weight_grad(sc_all.reshape(N_DEV, d_model).T, dmod_cols)
    d_, m_, v_ = _adamw(w_ada[0], g_ada, m_w_ada[0], v_w_ada[0], "adamw_w_ada")
    out_g["w_ada"], out_d["w_ada"], out_m["w_ada"], out_v["w_ada"] = g_ada[None], d_[None], m_[None], v_[None]

    small_out = _adamw_small(small_sum, *(pack([tree[n].reshape(1, -1) for n in SMALL])
                                          for tree in (weights, mom_m, mom_v)), sizes)
    for j, n in enumerate(SMALL):
        shape = weights[n].shape
        out_g[n], out_d[n], out_m[n], out_v[n] = (t.reshape(shape) for t in small_out[4 * j:4 * j + 4])

    return (loss, dx0[None], *[out_g[n] for n in WEIGHTS], *[out_d[n] for n in WEIGHTS],
            *[out_m[n] for n in WEIGHTS], *[out_v[n] for n in WEIGHTS])
```

```python
import numpy as np
import jax
import jax.numpy as jnp
from jax import lax
from jax.experimental import pallas as pl
from jax.experimental.pallas import tpu as pltpu

F32 = jnp.float32
BF16 = jnp.bfloat16
MESH = pl.DeviceIdType.MESH
ANY = pl.BlockSpec(memory_space=pl.ANY)
VMEM_SPEC = pl.BlockSpec(memory_space=pltpu.VMEM)
SMEM_SPEC = pl.BlockSpec(memory_space=pltpu.SMEM)

N_DEV = 8
CHUNK = 64
HEAD_DIM = 64
LANES = 128
H_A, KV_A, H_B = 8, 2, 8
BACK_A, BACK_B = 2, 8
REL_CLIP = 128
N_REL = 2 * REL_CLIP + 1
QA, KVA, QB = H_A * HEAD_DIM, KV_A * HEAD_DIM, H_B * HEAD_DIM
D_IN = QA + 2 * KVA + 3 * QB
N_MOD = 9
EPS = 1e-6
NEG_INF = -1e30
QG = 4
QROWS = QG * CHUNK
TPS_A, TPS_B = 4, 2
SKEW = 1024
ADAM_LR, ADAM_B1, ADAM_B2, ADAM_EPS, ADAM_WD, ADAM_STEP = 0.001, 0.9, 0.999, 1e-08, 0.01, 10
VMEM_LIMIT = 56 * 2 ** 20


def _pick(n, cands):
    for c in cands:
        if n % c == 0:
            return c
    return n


def _pieces(n, width=2 * LANES):
    return [(lo, min(lo + width, n)) for lo in range(0, n, width)]


def _params(sem=None):
    return pltpu.CompilerParams(dimension_semantics=sem, vmem_limit_bytes=VMEM_LIMIT)


def _dot_nt(a, b):
    return lax.dot_general(a, b, (((1,), (1,)), ((), ())), preferred_element_type=F32)


def _dot_tn(a, b):
    return lax.dot_general(a, b, (((0,), (0,)), ((), ())), preferred_element_type=F32)


def _dot(a, b):
    return jnp.dot(a, b, preferred_element_type=F32)


def _sigmoid(a):
    return 0.5 * (jnp.tanh(0.5 * a) + 1.0)


def _mesh_pos():
    return lax.axis_index("x"), lax.axis_index("y"), lax.axis_index("c")


def _peer(x, y, c, r):
    px = 1 - x if r & 4 else x
    py = 1 - y if r & 2 else y
    pc = 1 - c if r & 1 else c
    return px, py, pc


class _Carry:
    def __init__(self, ins, out_shapes, scratch, start, finish):
        self.ins, self.out_shapes, self.scratch = list(ins), list(out_shapes), list(scratch)
        self.start, self.finish = start, finish


def _call(body, *, name, grid, in_specs, out_specs, out_shape, args, scratch=(), sem=None, carry=None):
    single = not isinstance(out_shape, (tuple, list))
    out_specs = (out_specs,) if single else tuple(out_specs)
    out_shape = (out_shape,) if single else tuple(out_shape)
    if carry is None:
        res = pl.pallas_call(body, name=name, grid=grid, in_specs=list(in_specs), out_specs=out_specs,
                             out_shape=out_shape, scratch_shapes=list(scratch), compiler_params=_params(sem))(*args)
        return res[0] if single else res
    n_in, n_out, n_s = len(in_specs), len(out_shape), len(scratch)
    ci, co = len(carry.ins), len(carry.out_shapes)

    def wrapped(*refs):
        ins, cins = refs[:n_in], refs[n_in:n_in + ci]
        outs = refs[n_in + ci:n_in + ci + n_out]
        couts = refs[n_in + ci + n_out:n_in + ci + n_out + co]
        scr = refs[n_in + ci + n_out + co:n_in + ci + n_out + co + n_s]
        cscr = refs[n_in + ci + n_out + co + n_s:]
        first, last = None, None
        for ax, n in enumerate(grid):
            f, l = pl.program_id(ax) == 0, pl.program_id(ax) == n - 1
            first = f if first is None else first & f
            last = l if last is None else last & l
        pl.when(first)(lambda: carry.start(cins, couts, cscr))
        body(*ins, *outs, *scr)
        pl.when(last)(lambda: carry.finish(cins, couts, cscr))

    res = pl.pallas_call(
        wrapped, name=name, grid=grid, in_specs=list(in_specs) + [ANY] * ci, out_specs=out_specs + (ANY,) * co,
        out_shape=out_shape + tuple(carry.out_shapes), scratch_shapes=list(scratch) + carry.scratch,
        compiler_params=_params(("arbitrary",) * len(grid)))(*args, *carry.ins)
    main = res[:n_out]
    return (main[0] if single else main), res[n_out:]


def _gather_carry(shards):
    n_w = len(shards)
    rows = [s.shape[0] for s in shards]

    def plan(ins, outs, scr):
        send_sems, recv_sems, local_sems = scr
        x, y, c = _mesh_pos()
        me, sibling = (x, y, c), (x, y, 1 - c)
        chips = [(1 - x, y), (x, 1 - y), (1 - x, 1 - y)]

        def block(w, dev):
            start = pl.multiple_of((4 * dev[0] + 2 * dev[1] + dev[2]) * rows[w], 16)
            return outs[w].at[pl.ds(start, rows[w]), :]

        def copy(w, k, dev, to, src=None):
            return pltpu.make_async_remote_copy(
                src_ref=block(w, dev) if src is None else src, dst_ref=block(w, dev),
                send_sem=send_sems.at[w, k], recv_sem=recv_sems.at[w, k], device_id=to, device_id_type=MESH)

        mine = [pltpu.make_async_copy(ins[w], block(w, me), local_sems.at[w]) for w in range(n_w)]
        first = []
        for j, chip in enumerate(chips):
            first += [copy(w, 1 + j, me, (*chip, c), src=ins[w]) for w in range(n_w)]
        first += [copy(w, 0, me, sibling, src=ins[w]) for w in range(n_w)]
        return c, me, sibling, chips, copy, mine, first

    def start(ins, outs, scr):
        _, _, _, _, _, mine, first = plan(ins, outs, scr)
        for cp in mine + first:
            cp.start()

    def finish(ins, outs, scr):
        c, me, sibling, chips, copy, mine, first = plan(ins, outs, scr)
        passed = []
        for j, chip in enumerate(chips):
            for w in range(n_w):
                copy(w, 1 + j, (*chip, c), me).wait_recv()
                cp = copy(w, 4 + j, (*chip, c), sibling)
                cp.start()
                passed.append(cp)
        for w in range(n_w):
            copy(w, 0, sibling, me).wait_recv()
        for j, chip in enumerate(chips):
            for w in range(n_w):
                copy(w, 4 + j, (*chip, 1 - c), me).wait_recv()
        for cp in first + passed:
            cp.wait_send()
        for cp in mine:
            cp.wait()

    return _Carry(
        shards, [jax.ShapeDtypeStruct((N_DEV * s.shape[0], s.shape[1]), s.dtype) for s in shards],
        [pltpu.SemaphoreType.DMA((n_w, N_DEV - 1)), pltpu.SemaphoreType.DMA((n_w, N_DEV - 1)),
         pltpu.SemaphoreType.DMA((n_w,))], start, finish)


HBM_SPEC = pl.BlockSpec(memory_space=pltpu.HBM)
SEM_SPEC = pl.BlockSpec(memory_space=pltpu.SEMAPHORE)
N_CHIP = N_DEV // 2


def _scatter_copy(part_ref, land_ref, send_sem, recv_sem, r, rows):
    x, y, c = _mesh_pos()
    px, py, _ = _peer(x, y, c, 2 * r)
    src = part_ref.at[pl.ds(pl.multiple_of((2 * px + py) * rows, 16), rows), :]
    return pltpu.make_async_remote_copy(
        src_ref=src, dst_ref=land_ref.at[r - 1], send_sem=send_sem, recv_sem=recv_sem,
        device_id=(px, py, c), device_id_type=MESH)


def _scatter_order(n_w):
    return [(w, r) for w in range(n_w) for r in (3, 2, 1)]


def _scatter_start(parts, name):
    n_w = len(parts)
    rows = [p.shape[0] // N_CHIP for p in parts]
    order = _scatter_order(n_w)
    lands = [pltpu.with_memory_space_constraint(lax.empty((N_CHIP - 1, r, p.shape[1]), p.dtype), pltpu.HBM)
             for r, p in zip(rows, parts)]

    def body(*refs):
        part_refs, land_refs = refs[:n_w], refs[n_w:2 * n_w]
        sems = refs[2 * n_w:2 * n_w + 2 * len(order)]
        token = refs[-1]
        for j, (w, r) in enumerate(order):
            _scatter_copy(part_refs[w], land_refs[w], sems[2 * j], sems[2 * j + 1], r, rows[w]).start()
        token[...] = jnp.zeros_like(token)

    n_sem = 2 * len(order)
    res = pl.pallas_call(
        body, name=name,
        out_shape=(*[pltpu.SemaphoreType.DMA(())] * n_sem, *[pltpu.HBM(p.shape, p.dtype) for p in parts],
                   *[pltpu.HBM(l.shape, l.dtype) for l in lands], jax.ShapeDtypeStruct((8, LANES), F32)),
        in_specs=[HBM_SPEC] * (2 * n_w), out_specs=(*[SEM_SPEC] * n_sem, *[HBM_SPEC] * (2 * n_w), VMEM_SPEC),
        input_output_aliases={i: n_sem + i for i in range(2 * n_w)},
        compiler_params=pltpu.CompilerParams(has_side_effects=pltpu.SideEffectType.DATAFLOW_SIDE_EFFECTING),
    )(*[pltpu.with_memory_space_constraint(p, pltpu.HBM) for p in parts], *lands)
    return (list(res[:n_sem]), list(res[n_sem:n_sem + n_w]), list(res[n_sem + n_w:n_sem + 2 * n_w]), res[-1])


def _scatter_wait(sems, parts, lands, after, name):
    n_w = len(parts)
    rows = [p.shape[0] // N_CHIP for p in parts]
    order = _scatter_order(n_w)

    def body(*refs):
        part_refs, land_refs = refs[:n_w], refs[n_w:2 * n_w]
        sem_refs = refs[2 * n_w:2 * n_w + 2 * len(order)]
        for j, (w, r) in enumerate(order):
            cp = _scatter_copy(part_refs[w], land_refs[w], sem_refs[2 * j], sem_refs[2 * j + 1], r, rows[w])
            cp.wait_send()
            cp.wait_recv()

    res = pl.pallas_call(
        body, name=name,
        out_shape=(*[pltpu.HBM(p.shape, p.dtype) for p in parts], *[pltpu.HBM(l.shape, l.dtype) for l in lands]),
        in_specs=[HBM_SPEC] * (2 * n_w) + [SEM_SPEC] * len(sems) + [ANY],
        out_specs=tuple([HBM_SPEC] * (2 * n_w)),
        input_output_aliases={i: i for i in range(2 * n_w)},
        compiler_params=pltpu.CompilerParams(has_side_effects=pltpu.SideEffectType.DATAFLOW_SIDE_EFFECTING),
    )(*parts, *lands, *sems, after)
    return list(res[:n_w]), list(res[n_w:])


def _small_copy(v_ref, land_ref, send_sem, recv_sem, r):
    x, y, c = _mesh_pos()
    px, py, pc = _peer(x, y, c, r)
    return pltpu.make_async_remote_copy(
        src_ref=v_ref, dst_ref=land_ref.at[4 * x + 2 * y + c], send_sem=send_sem, recv_sem=recv_sem,
        device_id=(px, py, pc), device_id_type=MESH)


def _small_gather_start(v):
    land = pltpu.with_memory_space_constraint(lax.empty((N_DEV,) + v.shape, v.dtype), pltpu.HBM)

    def body(v_ref, land_ref, *rest):
        sems, token = rest[:2 * (N_DEV - 1)], rest[-1]
        for r in range(1, N_DEV):
            _small_copy(v_ref, land_ref, sems[2 * r - 2], sems[2 * r - 1], r).start()
        token[...] = jnp.zeros_like(token)

    n_sem = 2 * (N_DEV - 1)
    res = pl.pallas_call(
        body, name="small_gather_start",
        out_shape=(*[pltpu.SemaphoreType.DMA(())] * n_sem, pltpu.HBM(v.shape, v.dtype),
                   pltpu.HBM(land.shape, land.dtype), jax.ShapeDtypeStruct((8, LANES), F32)),
        in_specs=[HBM_SPEC, HBM_SPEC], out_specs=(*[SEM_SPEC] * n_sem, HBM_SPEC, HBM_SPEC, VMEM_SPEC),
        input_output_aliases={0: n_sem, 1: n_sem + 1},
        compiler_params=pltpu.CompilerParams(has_side_effects=pltpu.SideEffectType.DATAFLOW_SIDE_EFFECTING),
    )(pltpu.with_memory_space_constraint(v, pltpu.HBM), land)
    return list(res[:n_sem]), res[n_sem], res[n_sem + 1], res[-1]


def _small_gather_wait(sems, v, land, after):
    def body(v_ref, land_ref, *rest):
        for r in range(1, N_DEV):
            cp = _small_copy(v_ref, land_ref, rest[2 * r - 2], rest[2 * r - 1], r)
            cp.wait_send()
            x, y, c = _mesh_pos()
            px, py, pc = _peer(x, y, c, r)
            pltpu.make_async_remote_copy(
                src_ref=v_ref, dst_ref=land_ref.at[4 * px + 2 * py + pc], send_sem=rest[2 * r - 2],
                recv_sem=rest[2 * r - 1], device_id=(px, py, pc), device_id_type=MESH).wait_recv()

    res = pl.pallas_call(
        body, name="small_gather_wait",
        out_shape=(pltpu.HBM(v.shape, v.dtype), pltpu.HBM(land.shape, land.dtype)),
        in_specs=[HBM_SPEC, HBM_SPEC] + [SEM_SPEC] * len(sems) + [ANY], out_specs=(HBM_SPEC, HBM_SPEC),
        input_output_aliases={0: 0, 1: 1},
        compiler_params=pltpu.CompilerParams(has_side_effects=pltpu.SideEffectType.DATAFLOW_SIDE_EFFECTING),
    )(v, land, *sems, after)
    return res[0], res[1]


def _ada_forward(c_row, w_ada, b_cols, carry):
    d = c_row.shape[1]
    wcols = w_ada.shape[1]
    ci, co = len(carry.ins), len(carry.out_shapes)

    def body(*refs):
        c_ref, w_ref, b_ref = refs[:3]
        cins = refs[3:3 + ci]
        sc_ref, mod_ref = refs[3 + ci:5 + ci]
        couts = refs[5 + ci:5 + ci + co]
        rows_ref, send_sems, recv_sems = refs[5 + ci + co:8 + ci + co]
        cscr = refs[8 + ci + co:]
        carry.start(cins, couts, cscr)
        x, y, c = _mesh_pos()
        me = 4 * x + 2 * y + c
        cv = c_ref[...]
        sc_ref[me] = cv * _sigmoid(cv)

        sends = []
        for r in range(1, N_DEV):
            px, py, pc = _peer(x, y, c, r)
            cp = pltpu.make_async_remote_copy(
                src_ref=sc_ref.at[me], dst_ref=sc_ref.at[me], send_sem=send_sems.at[0, r - 1],
                recv_sem=recv_sems.at[0, r - 1], device_id=(px, py, pc), device_id_type=MESH)
            cp.start()
            sends.append(cp)
        for r in range(1, N_DEV):
            px, py, pc = _peer(x, y, c, r)
            pid = 4 * px + 2 * py + pc
            pltpu.make_async_remote_copy(
                src_ref=sc_ref.at[pid], dst_ref=sc_ref.at[pid], send_sem=send_sems.at[0, r - 1],
                recv_sem=recv_sems.at[0, r - 1], device_id=(px, py, pc), device_id_type=MESH).wait_recv()
        for cp in sends:
            cp.wait_send()

        sc_all = jnp.concatenate([sc_ref[j] for j in range(N_DEV)], axis=0)
        rows = _dot(sc_all.astype(BF16), w_ref[...].astype(BF16)) + b_ref[...]
        for j in range(N_DEV):
            rows_ref[j] = rows[j:j + 1, :]
        mod_ref[me] = rows_ref[me]

        sends = []
        for r in range(1, N_DEV):
            px, py, pc = _peer(x, y, c, r)
            pid = 4 * px + 2 * py + pc
            cp = pltpu.make_async_remote_copy(
                src_ref=rows_ref.at[pid], dst_ref=mod_ref.at[me], send_sem=send_sems.at[1, r - 1],
                recv_sem=recv_sems.at[1, r - 1], device_id=(px, py, pc), device_id_type=MESH)
            cp.start()
            sends.append(cp)
        for r in range(1, N_DEV):
            px, py, pc = _peer(x, y, c, r)
            pid = 4 * px + 2 * py + pc
            pltpu.make_async_remote_copy(
                src_ref=rows_ref.at[pid], dst_ref=mod_ref.at[pid], send_sem=send_sems.at[1, r - 1],
                recv_sem=recv_sems.at[1, r - 1], device_id=(px, py, pc), device_id_type=MESH).wait_recv()
        for cp in sends:
            cp.wait_send()
        carry.finish(cins, couts, cscr)

    res = pl.pallas_call(
        body, name="ada_forward",
        out_shape=(jax.ShapeDtypeStruct((N_DEV, 1, d), F32), jax.ShapeDtypeStruct((N_DEV, 1, wcols), F32),
                   *carry.out_shapes),
        in_specs=[VMEM_SPEC, VMEM_SPEC, VMEM_SPEC] + [ANY] * ci, out_specs=(VMEM_SPEC, VMEM_SPEC) + (ANY,) * co,
        scratch_shapes=[pltpu.VMEM((N_DEV, 1, wcols), F32), pltpu.SemaphoreType.DMA((2, N_DEV - 1)),
                        pltpu.SemaphoreType.DMA((2, N_DEV - 1))] + carry.scratch,
        compiler_params=_params(),
    )(c_row, w_ada, b_cols, *carry.ins)
    return res[:2], res[2:]


def _mm_nt(a, b, name, out_dtype, bias=None, carry=None):
    m, k = a.shape
    n = b.shape[0]
    tm = _pick(m, (512, 256, 128))
    tn = _pick(n, (1408, 1152, 1024, 768, 512, 256, 128))

    def body(*refs):
        acc = _dot_nt(refs[0][...], refs[1][...])
        if bias is not None:
            acc = acc + refs[2][...]
        refs[-1][...] = acc.astype(out_dtype)

    in_specs = [pl.BlockSpec((tm, k), lambda j, i: (i, 0)), pl.BlockSpec((tn, k), lambda j, i: (j, 0))]
    args = [a, b]
    if bias is not None:
        in_specs.append(pl.BlockSpec((1, tn), lambda j, i: (0, j)))
        args.append(bias)
    return _call(body, name=name, grid=(n // tn, m // tm), in_specs=in_specs,
                 out_specs=pl.BlockSpec((tm, tn), lambda j, i: (i, j)),
                 out_shape=jax.ShapeDtypeStruct((m, n), out_dtype), args=args,
                 sem=("parallel", "parallel"), carry=carry)


class _Tail:
    def __init__(self, rows, vecs, outs, fn):
        self.rows, self.vecs, self.outs, self.fn = list(rows), list(vecs), list(outs), fn


def _mm_nn(pairs, name, out_dtype, bias=None, carry=None, tail=None):
    m, k = pairs[0][0].shape
    n = pairs[0][1].shape[1]
    n_p = len(pairs)
    tm = _pick(m, (512, 256, 128))
    tk = k if n_p == 1 else _pick(k, (1408, 1152, 1024, 768, 512, 256, 128))
    nk = k // tk
    n_b = 0 if bias is None else 1
    n_r, n_v = (len(tail.rows), len(tail.vecs)) if tail else (0, 0)
    n_in = 2 * n_p + n_b + n_r + n_v
    n_main = 0 if out_dtype is None else 1

    def finish(acc, refs, first_tile):
        if bias is not None:
            acc = acc + refs[2 * n_p][...]
        outs = refs[n_in:-1]
        if n_main:
            outs[0][...] = acc.astype(out_dtype)
        if tail is None:
            return
        rows = [r[...] for r in refs[2 * n_p + n_b:2 * n_p + n_b + n_r]]
        vecs = [v[...] for v in refs[2 * n_p + n_b + n_r:n_in]]
        vals = tail.fn(acc, rows, vecs)
        for ref, val, (dtype, kind) in zip(outs[n_main:], vals, tail.outs):
            if kind == "row":
                ref[...] = val.astype(dtype)
            else:
                @pl.when(first_tile)
                def _(ref=ref):
                    ref[...] = jnp.zeros_like(ref)

                ref[...] += val

    def body(*refs):
        acc_ref = refs[-1]
        kk, i = pl.program_id(0), pl.program_id(1)
        part = _dot(refs[0][...], refs[1][...])
        for p in range(1, n_p):
            part = part + _dot(refs[2 * p][...], refs[2 * p + 1][...])
        if nk == 1:
            finish(part, refs, i == 0)
            return
        rows = pl.ds(pl.multiple_of(i * tm, tm), tm)

        @pl.when(kk == 0)
        def _():
            acc_ref[rows, :] = part

        if nk > 2:
            @pl.when((kk > 0) & (kk < nk - 1))
            def _():
                acc_ref[rows, :] += part

        @pl.when(kk == nk - 1)
        def _():
            finish(acc_ref[rows, :] + part, refs, i == 0)

    def last_only(kk, i):
        return (jnp.where(kk == nk - 1, i, 0), 0)

    row_spec = pl.BlockSpec((tm, n), last_only)
    vec_spec = pl.BlockSpec((1, n), lambda kk, i: (0, 0))
    in_specs, args = [], []
    for a, b in pairs:
        in_specs += [pl.BlockSpec((tm, tk), lambda kk, i: (i, kk)), pl.BlockSpec((tk, n), lambda kk, i: (kk, 0))]
        args += [a, b]
    if bias is not None:
        in_specs.append(vec_spec)
        args.append(bias)
    out_specs = [row_spec] * n_main
    out_shape = [jax.ShapeDtypeStruct((m, n), out_dtype)] if n_main else []
    if tail:
        in_specs += [row_spec] * n_r + [vec_spec] * n_v
        args += tail.rows + tail.vecs
        for dtype, kind in tail.outs:
            if kind == "row":
                out_specs.append(row_spec)
                out_shape.append(jax.ShapeDtypeStruct((m, n), dtype))
            else:
                width = n if kind == "sum" else 1
                out_specs.append(pl.BlockSpec((1, width), lambda kk, i: (0, 0)))
                out_shape.append(jax.ShapeDtypeStruct((1, width), dtype))
    if tail is None:
        out_specs, out_shape = out_specs[0], out_shape[0]
    return _call(body, name=name, grid=(nk, m // tm), in_specs=in_specs, out_specs=out_specs,
                 out_shape=out_shape, args=args,
                 scratch=[pltpu.VMEM((m, n) if nk > 1 else (8, LANES), F32)],
                 sem=("arbitrary", "arbitrary"), carry=carry)


def _rms(v):
    return lax.rsqrt(jnp.mean(v * v, axis=-1, keepdims=True) + EPS)


def _col(v):
    return jnp.sum(v, axis=0, keepdims=True)


def _tail_post_pre(x, g_post, gate, weight, g_pre, scale, shift):
    def fn(y, rows, vecs):
        (xv,), (gp, gt, g, sc, sh) = rows, vecs
        xo = xv + (weight * gt) * ((y * _rms(y)) * gp)
        return xo, ((xo * _rms(xo)) * g) * (1.0 + sc) + sh

    return _Tail([x], [g_post, gate, g_pre, scale, shift], [(F32, "row"), (BF16, "row")], fn)


def _tail_post_loss(x, target, g, gate, weight):
    def fn(y, rows, vecs):
        (xv, tv), (gv, gt) = rows, vecs
        r = _rms(y)
        yn = y * r
        err = (xv + (weight * gt) * (yn * gv)) - tv
        do = err * (1.0 / y.shape[1])
        dyn = do * ((weight * gt) * gv)
        dy = r * (dyn - yn * jnp.mean(dyn * yn, axis=-1, keepdims=True))
        return do, dy, 0.5 * _col(jnp.mean(err * err, axis=-1, keepdims=True)), _col(do * yn)

    return _Tail([x, target], [g, gate], [(F32, "row"), (BF16, "row"), (F32, "one"), (F32, "sum")], fn)


def _tail_pre_bwd(x, dres, g_pre, scale):
    def fn(dh, rows, vecs):
        (xv, dr), (g, sc) = rows, vecs
        r = _rms(xv)
        n = xv * r
        dn = dh * (g * (1.0 + sc))
        return dr + r * (dn - n * jnp.mean(dn * n, axis=-1, keepdims=True)), _col(dh * n), _col(dh)

    return _Tail([x, dres], [g_pre, scale], [(F32, "row"), (F32, "sum"), (F32, "sum")], fn)


def _tail_pre_post_bwd(x, dres, y, g_pre, scale, g_post, gate, weight):
    def fn(dh, rows, vecs):
        (xv, dr, yv), (g, sc, gp, gt) = rows, vecs
        r = _rms(xv)
        n = xv * r
        dn = dh * (g * (1.0 + sc))
        dx = dr + r * (dn - n * jnp.mean(dn * n, axis=-1, keepdims=True))
        ry = _rms(yv)
        yn = yv * ry
        dyn = dx * ((weight * gt) * gp)
        dy = ry * (dyn - yn * jnp.mean(dyn * yn, axis=-1, keepdims=True))
        return dx, dy, _col(dh * n), _col(dh), _col(dx * yn), _col(dy)

    return _Tail([x, dres, y], [g_pre, scale, g_post, gate],
                 [(F32, "row"), (BF16, "row")] + [(F32, "sum")] * 4, fn)


def _mm_tn_pair(a, b, name, col_sums=False):
    k, m = a.shape
    n = b.shape[1]
    rows = m // N_DEV
    n_chip = N_DEV // 2
    tm = 4 * rows
    tk = _pick(k, (1024, 512, 256, 128))
    nk = k // tk

    def body(a_ref, b_ref, p_ref, own_ref, *rest):
        acc_ref, keep_ref, send_ref, land_ref, send_sems, recv_sems = rest[-6:]
        i, kk = pl.program_id(0), pl.program_id(1)
        x, y, c = _mesh_pos()
        if col_sums:
            cs_ref = rest[0]
            part = jnp.sum(a_ref[...].astype(F32), axis=0, keepdims=True)

            @pl.when(kk == 0)
            def _():
                cs_ref[...] = part

            @pl.when(kk > 0)
            def _():
                cs_ref[...] += part

        def push(chip):
            return pltpu.make_async_remote_copy(
                src_ref=send_ref.at[chip], dst_ref=land_ref.at[chip], send_sem=send_sems.at[chip],
                recv_sem=recv_sems.at[chip], device_id=(x, y, 1 - c), device_id_type=MESH)

        if nk == 1:
            acc = _dot_tn(a_ref[...], b_ref[...])
        else:
            @pl.when(kk == 0)
            def _():
                acc_ref[...] = jnp.zeros_like(acc_ref)

            acc_ref[...] += _dot_tn(a_ref[...], b_ref[...])
            acc = acc_ref

        for t in range(2):
            @pl.when((kk == nk - 1) & (i == t))
            def _(t=t):
                for ob in range(4):
                    chip, core = 2 * t + ob // 2, ob % 2
                    blk = acc[ob * rows:(ob + 1) * rows, :]

                    @pl.when(c == core)
                    def _(chip=chip, blk=blk):
                        keep_ref[chip] = blk

                    @pl.when(c != core)
                    def _(chip=chip, blk=blk):
                        send_ref[chip] = blk.astype(BF16)
                        push(chip).start()

        @pl.when((kk == nk - 1) & (i == 1))
        def _():
            for chip in range(n_chip):
                push(chip).wait_recv()
                val = (keep_ref[chip] + land_ref[chip].astype(F32)).astype(BF16)
                p_ref[chip * rows:(chip + 1) * rows, :] = val

                @pl.when(2 * x + y == chip)
                def _(val=val):
                    own_ref[...] = val

            for chip in range(n_chip):
                push(chip).wait_send()

    out_specs = [pl.BlockSpec((n_chip * rows, n), lambda i, kk: (0, 0)), pl.BlockSpec((rows, n), lambda i, kk: (0, 0))]
    out_shape = [jax.ShapeDtypeStruct((n_chip * rows, n), BF16), jax.ShapeDtypeStruct((rows, n), BF16)]
    if col_sums:
        out_specs.append(pl.BlockSpec((1, tm), lambda i, kk: (0, i)))
        out_shape.append(jax.ShapeDtypeStruct((1, m), F32))
    return _call(body, name=name, grid=(2, nk),
                 in_specs=[pl.BlockSpec((tk, tm), lambda i, kk: (kk, i)), pl.BlockSpec((tk, n), lambda i, kk: (kk, 0))],
                 out_specs=out_specs, out_shape=out_shape, args=[a, b],
                 scratch=[pltpu.VMEM((tm, n) if nk > 1 else (8, LANES), F32), pltpu.VMEM((n_chip, rows, n), F32),
                          pltpu.VMEM((n_chip, rows, n), BF16), pltpu.VMEM((n_chip, rows, n), BF16),
                          pltpu.SemaphoreType.DMA((n_chip,)), pltpu.SemaphoreType.DMA((n_chip,))],
                 sem=("arbitrary", "arbitrary"))


def _ffn_up(h, wg_t, wu_t, name, carry=None):
    s, d = h.shape
    f = wg_t.shape[0]
    tm = _pick(s, (512, 256, 128))
    tf = _pick(f, (1408, 1024, 512, 256, 128))

    def body(h_ref, wg_ref, wu_ref, a_ref, b_ref, u_ref):
        hh = h_ref[...]
        for lo, hi in _pieces(tf):
            a = _dot_nt(hh, wg_ref[lo:hi, :])
            b = _dot_nt(hh, wu_ref[lo:hi, :])
            a_ref[:, lo:hi] = a.astype(BF16)
            b_ref[:, lo:hi] = b.astype(BF16)
            u_ref[:, lo:hi] = ((a * _sigmoid(a)) * b).astype(BF16)

    w_spec = pl.BlockSpec((tf, d), lambda j, i: (j, 0))
    o_spec = pl.BlockSpec((tm, tf), lambda j, i: (i, j))
    o_shape = jax.ShapeDtypeStruct((s, f), BF16)
    return _call(body, name=name, grid=(f // tf, s // tm),
                 in_specs=[pl.BlockSpec((tm, d), lambda j, i: (i, 0)), w_spec, w_spec],
                 out_specs=(o_spec, o_spec, o_spec), out_shape=(o_shape, o_shape, o_shape),
                 args=[h, wg_t, wu_t], sem=("parallel", "parallel"), carry=carry)


def _ffn_down_bwd(dy, wd, a, b, name, carry=None):
    s, d = dy.shape
    f = wd.shape[0]
    tm = _pick(s, (512, 256, 128))
    tf = _pick(f, (1408, 1024, 512, 256, 128))

    def body(dy_ref, wd_ref, a_ref, b_ref, da_ref, db_ref):
        dyv = dy_ref[...]
        for lo, hi in _pieces(tf):
            du = _dot_nt(dyv, wd_ref[lo:hi, :])
            a = a_ref[:, lo:hi].astype(F32)
            b = b_ref[:, lo:hi].astype(F32)
            sig = _sigmoid(a)
            da_ref[:, lo:hi] = (du * b * (sig * (1.0 + a * (1.0 - sig)))).astype(BF16)
            db_ref[:, lo:hi] = (du * (a * sig)).astype(BF16)

    t_spec = pl.BlockSpec((tm, tf), lambda j, i: (i, j))
    o_shape = jax.ShapeDtypeStruct((s, f), BF16)
    return _call(body, name=name, grid=(f // tf, s // tm),
                 in_specs=[pl.BlockSpec((tm, d), lambda j, i: (i, 0)), pl.BlockSpec((tf, d), lambda j, i: (j, 0)),
                           t_spec, t_spec],
                 out_specs=(t_spec, t_spec), out_shape=(o_shape, o_shape), args=[dy, wd, a, b],
                 sem=("parallel", "parallel"), carry=carry)


def _row_tile(s):
    return _pick(s, (256, 128, 64))


def _vec_spec(d):
    return pl.BlockSpec((1, d), lambda i: (0, 0))


def _pre_norm(x, g, scale, shift, name):
    s, d = x.shape
    ts = _row_tile(s)

    def body(x_ref, g_ref, sc_ref, sh_ref, h_ref):
        xv = x_ref[...]
        r = lax.rsqrt(jnp.mean(xv * xv, axis=-1, keepdims=True) + EPS)
        h_ref[...] = (((xv * r) * g_ref[...]) * (1.0 + sc_ref[...]) + sh_ref[...]).astype(BF16)

    row = pl.BlockSpec((ts, d), lambda i: (i, 0))
    return _call(body, name=name, grid=(s // ts,), in_specs=[row, _vec_spec(d), _vec_spec(d), _vec_spec(d)],
                 out_specs=row, out_shape=jax.ShapeDtypeStruct((s, d), BF16), args=[x, g, scale, shift],
                 sem=("parallel",))


def _group_norm_cat(oa, ob, ga, gb):
    s = oa.shape[0]
    ts = _row_tile(s)

    def body(oa_ref, ob_ref, ga_ref, gb_ref, y_ref):
        for o_ref, g_ref, lo, w in ((oa_ref, ga_ref, 0, QA), (ob_ref, gb_ref, QA, QB)):
            ov = o_ref[...]
            r = lax.rsqrt(jnp.mean(ov * ov, axis=-1, keepdims=True) + EPS)
            y_ref[:, lo:lo + w] = ((ov * r) * g_ref[...]).astype(BF16)

    return _call(body, name="group_norm_cat", grid=(s // ts,),
                 in_specs=[pl.BlockSpec((ts, QA), lambda i: (i, 0)), pl.BlockSpec((ts, QB), lambda i: (i, 0)),
                           _vec_spec(QA), _vec_spec(QB)],
                 out_specs=pl.BlockSpec((ts, QA + QB), lambda i: (i, 0)),
                 out_shape=jax.ShapeDtypeStruct((s, QA + QB), BF16), args=[oa, ob, ga, gb], sem=("parallel",))


def _group_norm_bwd(dy, oa, ob, ga, gb):
    s = oa.shape[0]
    ts = _row_tile(s)

    def body(dy_ref, oa_ref, ob_ref, ga_ref, gb_ref, doa_ref, dob_ref, dga_ref, dgb_ref):
        @pl.when(pl.program_id(0) == 0)
        def _():
            dga_ref[...] = jnp.zeros_like(dga_ref)
            dgb_ref[...] = jnp.zeros_like(dgb_ref)

        for o_ref, g_ref, do_ref, dg_ref, lo, w in ((oa_ref, ga_ref, doa_ref, dga_ref, 0, QA),
                                                    (ob_ref, gb_ref, dob_ref, dgb_ref, QA, QB)):
            ov = o_ref[...]
            dyv = dy_ref[:, lo:lo + w]
            r = lax.rsqrt(jnp.mean(ov * ov, axis=-1, keepdims=True) + EPS)
            n = ov * r
            dn = dyv * g_ref[...]
            do_ref[...] = r * (dn - n * jnp.mean(dn * n, axis=-1, keepdims=True))
            dg_ref[...] += jnp.sum(dyv * n, axis=0, keepdims=True)

    ra = pl.BlockSpec((ts, QA), lambda i: (i, 0))
    rb = pl.BlockSpec((ts, QB), lambda i: (i, 0))
    return _call(body, name="group_norm_bwd", grid=(s // ts,),
                 in_specs=[pl.BlockSpec((ts, QA + QB), lambda i: (i, 0)), ra, rb, _vec_spec(QA), _vec_spec(QB)],
                 out_specs=(ra, rb, _vec_spec(QA), _vec_spec(QB)),
                 out_shape=(jax.ShapeDtypeStruct((s, QA), F32), jax.ShapeDtypeStruct((s, QB), F32),
                            jax.ShapeDtypeStruct((1, QA), F32), jax.ShapeDtypeStruct((1, QB), F32)),
                 args=[dy, oa, ob, ga, gb], sem=("arbitrary",))


def _n_variants(n_back):
    return -(-n_back // QG) + 1


def _alibi_bias():
    i = np.arange(QROWS)[:, None]
    j = np.arange((QG + BACK_A) * CHUNK)[None, :]
    dist = np.abs(BACK_A * CHUNK + i - j).astype(np.float32)
    dc = j // CHUNK - i // CHUNK
    valid = (dc >= 0) & (dc <= BACK_A)
    slopes = np.array([2.0 ** (-8.0 * (h + 1) / H_A) for h in range(H_A)], dtype=np.float32)
    bias = -slopes[:, None, None] * dist[None]
    out = [np.where((valid & (j >= (BACK_A - QG * v) * CHUNK))[None], bias, np.float32(NEG_INF))
           for v in range(_n_variants(BACK_A))]
    return jnp.asarray(np.stack(out).astype(np.float32))


def _rel_index_matrix():
    cc = np.arange(SKEW)
    dist = np.where(cc < SKEW - QROWS, BACK_B * CHUNK - cc, BACK_B * CHUNK + SKEW - cc)
    idx = np.clip(dist, -REL_CLIP, REL_CLIP) + REL_CLIP
    m = np.zeros((SKEW, N_REL), np.float32)
    m[cc, idx] = 1.0
    return jnp.asarray(m)


def _toeplitz_bias(vec, carry=None):
    lk = (QG + BACK_B) * CHUNK
    nv = _n_variants(BACK_B)

    def body(v_ref, o_ref):
        xv = jnp.broadcast_to(v_ref[0], (QROWS, SKEW))
        row = lax.broadcasted_iota(jnp.int32, (QROWS, SKEW), 0)
        for bit in range(QROWS.bit_length() - 1):
            xv = jnp.where((row >> bit) & 1 == 1, pltpu.roll(xv, 1 << bit, 1), xv)
        ri = lax.broadcasted_iota(jnp.int32, (QROWS, lk), 0) // CHUNK
        col = lax.broadcasted_iota(jnp.int32, (QROWS, lk), 1)
        ci = col // CHUNK
        valid = (ci - ri >= 0) & (ci - ri <= BACK_B)
        for v in range(nv):
            o_ref[v, 0] = jnp.where(valid & (col >= (BACK_B - QG * v) * CHUNK), xv[:, :lk], NEG_INF)

    return _call(body, name="toeplitz_bias", grid=(H_B,),
                 in_specs=[pl.BlockSpec((1, 1, SKEW), lambda h: (h, 0, 0))],
                 out_specs=pl.BlockSpec((nv, 1, QROWS, lk), lambda h: (0, h, 0, 0)),
                 out_shape=jax.ShapeDtypeStruct((nv, H_B, QROWS, lk), F32), args=[vec], sem=("parallel",),
                 carry=carry)


def _diagonal_sums(dbias):
    lk = dbias.shape[2]

    def body(d_ref, o_ref):
        xp = jnp.concatenate([d_ref[0], jnp.zeros((QROWS, SKEW - lk), F32)], axis=1)
        xv = xp[0:CHUNK]
        for q in range(1, QG):
            xv = xv + pltpu.roll(xp[q * CHUNK:(q + 1) * CHUNK], SKEW - q * CHUNK, 1)
        row = lax.broadcasted_iota(jnp.int32, (CHUNK, SKEW), 0)
        for bit in range(CHUNK.bit_length() - 1):
            xv = jnp.where((row >> bit) & 1 == 1, pltpu.roll(xv, SKEW - (1 << bit), 1), xv)
        o_ref[0] = jnp.sum(xv, axis=0, keepdims=True)

    return _call(body, name="diagonal_sums", grid=(H_B,),
                 in_specs=[pl.BlockSpec((1, QROWS, lk), lambda h: (h, 0, 0))],
                 out_specs=pl.BlockSpec((1, 1, SKEW), lambda h: (h, 0, 0)),
                 out_shape=jax.ShapeDtypeStruct((H_B, 1, SKEW), F32), args=[dbias], sem=("parallel",))


def _attn_common(s, n_back, gqa, q_col, k_col, v_col, TPS):
    assert q_col % TPS == 0 and (gqa or (k_col % TPS == 0 and v_col % TPS == 0)), "blocks of TPS lane tiles"
    lk = (QG + n_back) * CHUNK
    pad = n_back * CHUNK
    wide = TPS * LANES
    q_spec = pl.BlockSpec((QROWS, wide), lambda t, g: (g, q_col // TPS + t))
    if gqa:
        k_spec = pl.BlockSpec((s, LANES), lambda t, g: (0, k_col))
        v_spec = pl.BlockSpec((s, LANES), lambda t, g: (0, v_col))
    else:
        k_spec = pl.BlockSpec((s, wide), lambda t, g: (0, k_col // TPS + t))
        v_spec = pl.BlockSpec((s, wide), lambda t, g: (0, v_col // TPS + t))
    last_variant = _n_variants(n_back) - 1
    bias_spec = pl.BlockSpec((None, 2 * TPS, QROWS, lk), lambda t, g: (jnp.minimum(g, last_variant), t, 0, 0))
    tile_spec = pl.BlockSpec((QROWS, wide), lambda t, g: (g, t))
    return lk, pad, q_spec, k_spec, v_spec, bias_spec, tile_spec


def _attention_fwd(proj, bias, sinks, *, n_back, gqa, q_col, k_col, v_col, TPS, name, carry=None):
    s = proj.shape[0]
    lk, pad, q_spec, k_spec, v_spec, bias_spec, tile_spec = _attn_common(s, n_back, gqa, q_col, k_col, v_col, TPS)
    n_t, n_g = 512 // (TPS * LANES), s // QROWS
    kv_wide = LANES if gqa else TPS * LANES

    def body(*refs):
        if gqa:
            q_ref, k_ref, v_ref, bias_ref, sink_ref, o_ref, l_ref, kpad, vpad = refs
        else:
            q_ref, k_ref, v_ref, bias_ref, o_ref, l_ref, kpad, vpad = refs
        t, g = pl.program_id(0), pl.program_id(1)

        @pl.when(g == 0)
        def _():
            kpad[0:pad, :] = jnp.zeros((pad, kv_wide), BF16)
            vpad[0:pad, :] = jnp.zeros((pad, kv_wide), BF16)
            kpad[pad:, :] = k_ref[...]
            vpad[pad:, :] = v_ref[...]

        start = pl.multiple_of(g * QROWS, QROWS)
        half = lax.broadcasted_iota(jnp.int32, (QROWS, LANES), 1) // HEAD_DIM
        for tt in range(TPS):
            lanes = slice(tt * LANES, (tt + 1) * LANES)
            kv_lanes = slice(0, LANES) if gqa else lanes
            kb = kpad[pl.ds(start, lk), kv_lanes]
            vb = vpad[pl.ds(start, lk), kv_lanes]
            q = q_ref[:, lanes] * (HEAD_DIM ** -0.5)
            if gqa:
                hk = (TPS * t + tt) // 2
                q_rolled = pltpu.roll(q.astype(F32), HEAD_DIM, 1).astype(BF16)
            outs, lses = [], []
            for e in range(2):
                if gqa:
                    kv_half = hk
                    src = jnp.where(hk == e, q, q_rolled)
                else:
                    kv_half = e
                    src = q
                qm = jnp.where(half == kv_half, src, jnp.zeros_like(src))
                sc = _dot_nt(qm, kb) + bias_ref[2 * tt + e]
                m = jnp.max(sc, axis=-1, keepdims=True)
                if gqa:
                    sk = sink_ref[2 * (TPS * t + tt) + e]
                    m = jnp.maximum(m, sk)
                p = jnp.exp(sc - m)
                l = jnp.sum(p, axis=-1, keepdims=True)
                if gqa:
                    l = l + jnp.exp(sk - m)
                pn = p / l
                outs.append(_dot(pn.astype(BF16), vb))
                lses.append(m + jnp.log(l))
            if gqa:
                same = jnp.where(hk == 0, outs[0], outs[1])
                other = jnp.where(hk == 0, outs[1], outs[0])
                o_ref[:, lanes] = jnp.where(half == hk, same, pltpu.roll(other, HEAD_DIM, 1))
            else:
                o_ref[:, lanes] = jnp.where(half == 0, outs[0], outs[1])
            l_ref[:, lanes] = jnp.where(half == 0, lses[0], lses[1])

    in_specs = [q_spec, k_spec, v_spec, bias_spec] + ([SMEM_SPEC] if gqa else [])
    args = [proj, proj, proj, bias] + ([sinks] if gqa else [])
    o_shape = jax.ShapeDtypeStruct((s, 512), F32)
    return _call(body, name=name, grid=(n_t, n_g), in_specs=in_specs, out_specs=(tile_spec, tile_spec),
                 out_shape=(o_shape, o_shape), args=args,
                 scratch=[pltpu.VMEM((s + pad, kv_wide), BF16), pltpu.VMEM((s + pad, kv_wide), BF16)],
                 sem=("arbitrary", "arbitrary"), carry=carry)


def _attention_bwd(proj, bias, sinks, do, lse, *, n_back, gqa, q_col, k_col, v_col, TPS, name, carry=None):
    s = proj.shape[0]
    lk, pad, q_spec, k_spec, v_spec, bias_spec, tile_spec = _attn_common(s, n_back, gqa, q_col, k_col, v_col, TPS)
    n_t, n_g = 512 // (TPS * LANES), s // QROWS
    kv_wide = LANES if gqa else TPS * LANES

    def body(*refs):
        if gqa:
            (q_ref, k_ref, v_ref, bias_ref, sink_ref, do_ref, l_ref,
             dq_ref, dk_ref, dv_ref, dsink_ref, kpad, vpad, dkpad, dvpad) = refs
        else:
            (q_ref, k_ref, v_ref, bias_ref, do_ref, l_ref,
             dq_ref, dk_ref, dv_ref, dbias_ref, kpad, vpad, dkpad, dvpad) = refs
        t, g = pl.program_id(0), pl.program_id(1)

        @pl.when(g == 0)
        def _():
            kpad[0:pad, :] = jnp.zeros((pad, kv_wide), BF16)
            vpad[0:pad, :] = jnp.zeros((pad, kv_wide), BF16)
            kpad[pad:, :] = k_ref[...]
            vpad[pad:, :] = v_ref[...]
            if gqa:
                dsink_ref[...] = jnp.zeros_like(dsink_ref)
            else:
                dbias_ref[...] = jnp.zeros_like(dbias_ref)

        @pl.when((g == 0) & (t == 0) if gqa else g == 0)
        def _():
            dkpad[...] = jnp.zeros_like(dkpad)
            dvpad[...] = jnp.zeros_like(dvpad)

        start = pl.multiple_of(g * QROWS, QROWS)
        half = lax.broadcasted_iota(jnp.int32, (QROWS, LANES), 1) // HEAD_DIM
        for tt in range(TPS):
            lanes = slice(tt * LANES, (tt + 1) * LANES)
            kv_lanes = slice(0, LANES) if gqa else lanes
            kb = kpad[pl.ds(start, lk), kv_lanes]
            vb = vpad[pl.ds(start, lk), kv_lanes]
            q = q_ref[:, lanes]
            dov = do_ref[:, lanes]
            lv = l_ref[:, lanes]
            if gqa:
                hk = (TPS * t + tt) // 2
                q_rolled = pltpu.roll(q.astype(F32), HEAD_DIM, 1).astype(BF16)
                do_rolled = pltpu.roll(dov, HEAD_DIM, 1)
            dqs = []
            dk_acc = jnp.zeros((lk, LANES), F32)
            dv_acc = jnp.zeros((lk, LANES), F32)
            for e in range(2):
                if gqa:
                    kv_half = hk
                    src = jnp.where(hk == e, q, q_rolled)
                    do_src = jnp.where(hk == e, dov, do_rolled)
                else:
                    kv_half = e
                    src = q
                    do_src = dov
                qm = jnp.where(half == kv_half, src, jnp.zeros_like(src))
                dom = jnp.where(half == kv_half, do_src, 0.0).astype(BF16)
                lcol = jnp.max(jnp.where(half == e, lv, -jnp.inf), axis=-1, keepdims=True)
                sc = _dot_nt(qm * (HEAD_DIM ** -0.5), kb) + bias_ref[2 * tt + e]
                pn = jnp.exp(sc - lcol)
                dp = _dot_nt(dom, vb)
                delta = jnp.sum(pn * dp, axis=-1, keepdims=True)
                ds = pn * (dp - delta)
                if gqa:
                    p_sink = jnp.exp(sink_ref[2 * (TPS * t + tt) + e] - lcol)
                    dsk = -jnp.sum(p_sink * delta, axis=0, keepdims=True)
                    row = 2 * tt + e
                    dsink_ref[0, row:row + 1, :] += jnp.broadcast_to(dsk, (1, LANES))
                else:
                    dbias_ref[2 * tt + e] += ds
                dsb = (ds * (HEAD_DIM ** -0.5)).astype(BF16)
                dqs.append(_dot(dsb, kb))
                dk_acc = dk_acc + _dot_tn(dsb, qm)
                dv_acc = dv_acc + _dot_tn(pn.astype(BF16), dom)
            dkpad[pl.ds(start, lk), kv_lanes] += dk_acc
            dvpad[pl.ds(start, lk), kv_lanes] += dv_acc
            if gqa:
                same = jnp.where(hk == 0, dqs[0], dqs[1])
                other = jnp.where(hk == 0, dqs[1], dqs[0])
                dq_ref[:, lanes] = jnp.where(half == hk, same, pltpu.roll(other, HEAD_DIM, 1)).astype(BF16)
            else:
                dq_ref[:, lanes] = jnp.where(half == 0, dqs[0], dqs[1]).astype(BF16)

        @pl.when((g == n_g - 1) & (t == n_t - 1) if gqa else g == n_g - 1)
        def _():
            dk_ref[...] = dkpad[pad:, :].astype(BF16)
            dv_ref[...] = dvpad[pad:, :].astype(BF16)

    in_specs = [q_spec, k_spec, v_spec, bias_spec] + ([SMEM_SPEC] if gqa else []) + [tile_spec, tile_spec]
    args = [proj, proj, proj, bias] + ([sinks] if gqa else []) + [do, lse]
    if gqa:
        kv_out = pl.BlockSpec((s, LANES), lambda t, g: (0, 0))
        kv_shape = jax.ShapeDtypeStruct((s, LANES), BF16)
        extra_spec = pl.BlockSpec((1, 8, LANES), lambda t, g: (t, 0, 0))
        extra_shape = jax.ShapeDtypeStruct((n_t, 8, LANES), F32)
    else:
        kv_out = pl.BlockSpec((s, kv_wide), lambda t, g: (0, t))
        kv_shape = jax.ShapeDtypeStruct((s, 512), BF16)
        extra_spec = pl.BlockSpec((2 * TPS, QROWS, lk), lambda t, g: (t, 0, 0))
        extra_shape = jax.ShapeDtypeStruct(bias.shape[1:], F32)
    return _call(body, name=name, grid=(n_t, n_g), in_specs=in_specs,
                 out_specs=(tile_spec, kv_out, kv_out, extra_spec),
                 out_shape=(jax.ShapeDtypeStruct((s, 512), BF16), kv_shape, kv_shape, extra_shape), args=args,
                 scratch=[pltpu.VMEM((s + pad, kv_wide), BF16), pltpu.VMEM((s + pad, kv_wide), BF16),
                          pltpu.VMEM((s + pad, kv_wide), F32), pltpu.VMEM((s + pad, kv_wide), F32)],
                 sem=("arbitrary", "arbitrary"), carry=carry)


def _sum_rows8(g):
    n = g.shape[2]

    def body(g_ref, o_ref):
        acc = g_ref[0]
        for j in range(1, N_DEV):
            acc = acc + g_ref[j]
        o_ref[...] = acc

    return pl.pallas_call(
        body, name="sum_small_grads", in_specs=[VMEM_SPEC], out_specs=VMEM_SPEC,
        out_shape=jax.ShapeDtypeStruct((1, n), F32), compiler_params=_params(),
    )(g)


def _ada_weight_grad(sc_t, dmod_cols):
    d = sc_t.shape[0]
    w = dmod_cols.shape[1]
    td = _pick(d, (256, 128))

    def body(sc_ref, dm_ref, o_ref):
        scv = sc_ref[...]
        dmv = dm_ref[...]
        acc = scv[:, 0:1] * dmv[0:1, :]
        for b in range(1, N_DEV):
            acc = acc + scv[:, b:b + 1] * dmv[b:b + 1, :]
        o_ref[...] = acc

    return _call(body, name="ada_weight_grad", grid=(d // td,),
                 in_specs=[pl.BlockSpec((td, N_DEV), lambda i: (i, 0)), pl.BlockSpec((N_DEV, w), lambda i: (0, 0))],
                 out_specs=pl.BlockSpec((td, w), lambda i: (i, 0)), out_shape=jax.ShapeDtypeStruct((d, w), F32),
                 args=[sc_t, dmod_cols], sem=("parallel",))


def _adamw_update(w, gv, m, v):
    nm = ADAM_B1 * m + (1.0 - ADAM_B1) * gv
    nv = ADAM_B2 * v + (1.0 - ADAM_B2) * (gv * gv)
    m_hat = nm / (1.0 - ADAM_B1 ** ADAM_STEP)
    v_hat = nv / (1.0 - ADAM_B2 ** ADAM_STEP)
    return -ADAM_LR * (m_hat / (jnp.sqrt(v_hat) + ADAM_EPS) + ADAM_WD * w), nm, nv


def _adamw(w, g, m, v, name):
    rows, cols = w.shape
    tr = _pick(rows, (256, 176, 128, 88, 64)) if rows > 256 else rows

    def body(w_ref, g_ref, m_ref, v_ref, d_ref, nm_ref, nv_ref):
        d_ref[...], nm_ref[...], nv_ref[...] = _adamw_update(w_ref[...], g_ref[...], m_ref[...], v_ref[...])

    spec = pl.BlockSpec((tr, cols), lambda i: (i, 0))
    shape = jax.ShapeDtypeStruct((rows, cols), F32)
    return _call(body, name=name, grid=(rows // tr,), in_specs=[spec] * 4, out_specs=(spec, spec, spec),
                 out_shape=(shape, shape, shape), args=[w, g, m, v], sem=("parallel",))


def _adamw_from_slots(w, own, slots, m, v, name):
    n_slots, rows, k = slots.shape

    def body(o_ref, s_ref, w_ref, m_ref, v_ref, g_ref, d_ref, nm_ref, nv_ref):
        gv = o_ref[...].astype(F32)
        for j in range(n_slots):
            gv = gv + s_ref[j].astype(F32)
        g_ref[...] = gv
        d_ref[...], nm_ref[...], nv_ref[...] = _adamw_update(w_ref[...], gv, m_ref[...], v_ref[...])

    tr = rows // 2 if rows % 32 == 0 else rows
    spec = pl.BlockSpec((tr, k), lambda i: (i, 0))
    shape = jax.ShapeDtypeStruct((rows, k), F32)
    return _call(body, name=name, grid=(rows // tr,),
                 in_specs=[spec, pl.BlockSpec((n_slots, tr, k), lambda i: (0, i, 0)), spec, spec, spec],
                 out_specs=(spec, spec, spec, spec), out_shape=(shape, shape, shape, shape),
                 args=[own, slots, w, m, v], sem=("parallel",))


def _adamw_small(g, w, m, v, sizes):
    n = w.shape[1]
    offs, off = [], 0
    for size in sizes:
        offs.append(off)
        off += size + (-size % LANES)

    def body(g_ref, w_ref, m_ref, v_ref, *out_refs):
        gv = g_ref[:, 0:n]
        dv, nm, nv = _adamw_update(w_ref[...], gv, m_ref[...], v_ref[...])
        for j, (o, size) in enumerate(zip(offs, sizes)):
            for k, val in enumerate((gv, dv, nm, nv)):
                out_refs[4 * j + k][...] = val[:, o:o + size]

    shapes = [jax.ShapeDtypeStruct((1, size), F32) for size in sizes for _ in range(4)]
    return pl.pallas_call(
        body, name="adamw_small", in_specs=[VMEM_SPEC] * 4, out_specs=tuple([VMEM_SPEC] * len(shapes)),
        out_shape=tuple(shapes), compiler_params=_params(),
    )(g, w, m, v)


SMALL = ("b_ada", "g_pre_ffn1", "g_post_ffn1", "g_pre_mix", "b_in", "sinks_a", "rel_bias_b", "g_grp_a",
         "g_grp_b", "b_out", "g_post_mix", "g_pre_ffn2", "g_post_ffn2")
WEIGHTS = ("w_ada", "b_ada", "g_pre_ffn1", "w_gate1", "w_up1", "w_down1", "g_post_ffn1", "g_pre_mix", "w_in",
           "b_in", "sinks_a", "rel_bias_b", "g_grp_a", "g_grp_b", "w_out", "b_out", "g_post_mix", "g_pre_ffn2",
           "w_gate2", "w_up2", "w_down2", "g_post_ffn2")


def kernel(x, c, w_ada, b_ada, g_pre_ffn1, w_gate1, w_up1, w_down1, g_post_ffn1, g_pre_mix, w_in, b_in, sinks_a, rel_bias_b, g_grp_a, g_grp_b, w_out, b_out, g_post_mix, g_pre_ffn2, w_gate2, w_up2, w_down2, g_post_ffn2, loss_target, m_w_ada, m_b_ada, m_g_pre_ffn1, m_w_gate1, m_w_up1, m_w_down1, m_g_post_ffn1, m_g_pre_mix, m_w_in, m_b_in, m_sinks_a, m_rel_bias_b, m_g_grp_a, m_g_grp_b, m_w_out, m_b_out, m_g_post_mix, m_g_pre_ffn2, m_w_gate2, m_w_up2, m_w_down2, m_g_post_ffn2, v_w_ada, v_b_ada, v_g_pre_ffn1, v_w_gate1, v_w_up1, v_w_down1, v_g_post_ffn1, v_g_pre_mix, v_w_in, v_b_in, v_sinks_a, v_rel_bias_b, v_g_grp_a, v_g_grp_b, v_w_out, v_b_out, v_g_post_mix, v_g_pre_ffn2, v_w_gate2, v_w_up2, v_w_down2, v_g_post_ffn2):
    given = dict(locals())
    weights = {n: given[n] for n in WEIGHTS}
    mom_m = {n: given["m_" + n] for n in WEIGHTS}
    mom_v = {n: given["v_" + n] for n in WEIGHTS}

    me = 4 * lax.axis_index("x") + 2 * lax.axis_index("y") + lax.axis_index("c")
    xs = x[0]
    tgt = loss_target[0]
    d_model = xs.shape[1]
    ada_cols = w_ada.shape[2]

    sh = {"wg1": w_gate1[0].T, "wu1": w_up1[0].T, "wd1": w_down1[0], "win": w_in[0].T, "wo": w_out[0],
          "wg2": w_gate2[0].T, "wu2": w_up2[0].T, "wd2": w_down2[0]}
    sh = {k: v.astype(BF16) for k, v in sh.items()}

    def gather(*names):
        return _gather_carry([sh[n] for n in names])

    bias_a = _alibi_bias()
    rel_m = _rel_index_matrix()
    rel_vec = jnp.dot(rel_bias_b[0], rel_m.T, precision=lax.Precision.HIGHEST)
    bias_b, (wg1, wu1) = _toeplitz_bias(rel_vec.reshape(H_B, 1, SKEW), carry=gather("wg1", "wu1"))

    b_cols = lax.dynamic_slice(b_ada, (0, me * ada_cols), (1, ada_cols))
    (sc_all, mod_rows), _ = _ada_forward(c, w_ada[0], b_cols, _Carry([], [], [], lambda *a: None, lambda *a: None))
    mod = mod_rows.reshape(N_MOD, d_model)
    shift1, scale1, gate1, shift2, scale2, gate2, shift3, scale3, gate3 = (mod[i:i + 1] for i in range(N_MOD))

    h1 = _pre_norm(xs, g_pre_ffn1, scale1, shift1, "pre_norm_ffn1")
    (a1, b1, u1), (wd1,) = _ffn_up(h1, wg1, wu1, "ffn_up_ffn1", carry=gather("wd1"))
    (y1, x1, h2), (win,) = _mm_nn(
        [(u1, wd1)], "ffn_down_ffn1", F32, carry=gather("win"),
        tail=_tail_post_pre(xs, g_post_ffn1, gate1, 0.5, g_pre_mix, scale2, shift2))

    proj = _mm_nt(h2, win, "in_proj", BF16, bias=b_in)
    sinks = sinks_a[0]
    cfg_a = dict(n_back=BACK_A, gqa=True, q_col=0, k_col=QA // LANES, v_col=(QA + KVA) // LANES, TPS=TPS_A)
    cfg_b = dict(n_back=BACK_B, gqa=False, q_col=(QA + 2 * KVA) // LANES, k_col=(QA + 2 * KVA + QB) // LANES,
                 v_col=(QA + 2 * KVA + 2 * QB) // LANES, TPS=TPS_B)
    (oa, lse_a), (wg2,) = _attention_fwd(proj, bias_a, sinks, name="attn_a", carry=gather("wg2"), **cfg_a)
    (ob, lse_b), (wu2, wo) = _attention_fwd(proj, bias_b, None, name="attn_b", carry=gather("wu2", "wo"), **cfg_b)
    ycat = _group_norm_cat(oa, ob, g_grp_a, g_grp_b)
    ymix, x2, h3 = _mm_nn([(ycat, wo)], "out_proj", F32, bias=b_out,
                          tail=_tail_post_pre(x1, g_post_mix, gate2, 1.0, g_pre_ffn2, scale3, shift3))

    (a3, b3, u3), (wd2,) = _ffn_up(h3, wg2, wu2, "ffn_up_ffn2", carry=gather("wd2"))

    flights, own = {}, {}

    def grad_pair(key, a_mat, b_mat, name):
        part, own[key] = _mm_tn_pair(a_mat, b_mat, name)
        return part

    def scatter_start(tag, after_vec, **parts):
        names = list(parts)
        sems, p_thru, lands, token = _scatter_start([parts[n] for n in names], "scatter_start_" + tag)
        flights[tag] = (names, sems, p_thru, lands)
        return after_vec + token[0:1, 0:1]

    dx3, dy, loss_part, s1 = _mm_nn([(u3, wd2)], "ffn_down_ffn2", None,
                                    tail=_tail_post_loss(x2, tgt, g_post_ffn2, gate3, 0.5))
    da, db = _ffn_down_bwd(dy, wd2, a3, b3, "ffn_down_bwd_ffn2")
    dwd2 = grad_pair("wd2", u3, dy, "grad_wd_ffn2")
    dwg2 = grad_pair("wg2", da, h3, "grad_wg_ffn2")
    dwu2 = grad_pair("wu2", db, h3, "grad_wu_ffn2")
    g_pre_tied = scatter_start("ffn2", g_pre_ffn2, wd2=dwd2, wg2=dwg2, wu2=dwu2)
    dx2, dymix, s2, s3, s1m, db_out = _mm_nn(
        [(da, wg2), (db, wu2)], "ffn_up_bwd_ffn2", None,
        tail=_tail_pre_post_bwd(x2, dx3, ymix, g_pre_tied, scale3, g_post_mix, gate2, 1.0))
    sm3 = dict(shift=s3, scale=s2 * g_pre_ffn2, gate=0.5 * g_post_ffn2 * s1,
               g_pre=(1.0 + scale3) * s2, g_post=(0.5 * gate3) * s1)

    dycat = _mm_nt(dymix, wo, "out_proj_bwd", F32)
    dwo = grad_pair("wo", ycat, dymix, "grad_wo")
    doa, dob, dg_a, dg_b = _group_norm_bwd(dycat, oa, ob, g_grp_a, g_grp_b)
    dqa, dka, dva, dsink = _attention_bwd(proj, bias_a, sinks, doa, lse_a, name="attn_a_bwd", **cfg_a)
    dqb, dkb, dvb, dbias = _attention_bwd(proj, bias_b, None, dob, lse_b, name="attn_b_bwd", **cfg_b)
    dproj = jnp.concatenate([dqa, dka, dva, dqb, dkb, dvb], axis=1)
    dwin, own["win"], db_in = _mm_tn_pair(dproj, h2, "grad_win", col_sums=True)
    g_pre_tied = scatter_start("mix", g_pre_mix, wo=dwo, win=dwin)
    dx1, dy, s2m, s3m, s1, _ = _mm_nn(
        [(dproj, win)], "in_proj_bwd", None,
        tail=_tail_pre_post_bwd(x1, dx2, y1, g_pre_tied, scale2, g_post_ffn1, gate1, 0.5))
    d_rel = jnp.dot(_diagonal_sums(dbias).reshape(H_B, SKEW), rel_m, precision=lax.Precision.HIGHEST)
    d_sinks = dsink[:, :2 * TPS_A, 0].reshape(1, H_A)

    da, db = _ffn_down_bwd(dy, wd1, a1, b1, "ffn_down_bwd_ffn1")
    dwd1 = grad_pair("wd1", u1, dy, "grad_wd_ffn1")
    dwg1 = grad_pair("wg1", da, h1, "grad_wg_ffn1")
    dwu1 = grad_pair("wu1", db, h1, "grad_wu_ffn1")
    g_pre_tied = scatter_start("ffn1", g_pre_ffn1, wd1=dwd1, wg1=dwg1, wu1=dwu1)
    dx0, s2, s3 = _mm_nn([(da, wg1), (db, wu1)], "ffn_up_bwd_ffn1", None,
                         tail=_tail_pre_bwd(xs, dx1, g_pre_tied, scale1))
    sm1 = dict(shift=s3, scale=s2 * g_pre_ffn1, gate=0.5 * g_post_ffn1 * s1,
               g_pre=(1.0 + scale1) * s2, g_post=(0.5 * gate1) * s1)

    dmod = jnp.concatenate([sm1["shift"], sm1["scale"], sm1["gate"],
                            s3m, s2m * g_pre_mix, g_post_mix * s1m,
                            sm3["shift"], sm3["scale"], sm3["gate"]], axis=1)
    small_parts = {
        "b_ada": dmod, "g_pre_ffn1": sm1["g_pre"], "g_post_ffn1": sm1["g_post"],
        "g_pre_mix": (1.0 + scale2) * s2m, "b_in": db_in, "sinks_a": d_sinks,
        "rel_bias_b": d_rel.reshape(1, H_B * N_REL), "g_grp_a": dg_a, "g_grp_b": dg_b, "b_out": db_out,
        "g_post_mix": gate2 * s1m, "g_pre_ffn2": sm3["g_pre"], "g_post_ffn2": sm3["g_post"]}
    sizes = [small_parts[n].shape[1] for n in SMALL]

    def pack(parts):
        cells = []
        for p in parts:
            cells.append(p)
            if p.shape[1] % LANES:
                cells.append(jnp.zeros((1, -p.shape[1] % LANES), F32))
        return jnp.concatenate(cells, axis=1)

    packed = pack([small_parts[n] for n in SMALL] + [loss_part])
    n_packed = packed.shape[1]
    small_sems, packed_thru, small_land, small_token = _small_gather_start(packed)

    out_g, out_d, out_m, out_v = {}, {}, {}, {}
    groups = (("ffn2", (("w_gate2", "wg2", True), ("w_up2", "wu2", True), ("w_down2", "wd2", False))),
              ("mix", (("w_in", "win", True), ("w_out", "wo", False))),
              ("ffn1", (("w_gate1", "wg1", True), ("w_up1", "wu1", True), ("w_down1", "wd1", False))))
    after = small_token
    for tag, members in groups:
        names, sems, p_thru, lands = flights[tag]
        _, l_done = _scatter_wait(sems, p_thru, lands, after, "scatter_wait_" + tag)
        slots = dict(zip(names, l_done))
        for n, key, transposed in members:
            view = (lambda t: t.T) if transposed else (lambda t: t)
            res = _adamw_from_slots(view(weights[n][0]), own[key], slots[key], view(mom_m[n][0]),
                                    view(mom_v[n][0]), "adamw_" + n)
            out_g[n], out_d[n], out_m[n], out_v[n] = (view(t)[None] for t in res)
            after = res[3]

    packed_done, small_land = _small_gather_wait(small_sems, packed_thru, small_land, after)
    gathered = lax.dynamic_update_slice(small_land, packed_done[None], (me, 0, 0))
    small_sum = _sum_rows8(gathered)
    loss = small_sum[0, n_packed - LANES]
    dmod_cols = lax.dynamic_slice(gathered.reshape(N_DEV, n_packed), (0, me * ada_cols), (N_DEV, ada_cols))
    g_ada = _ada_weight_grad(sc_all.reshape(N_DEV, d_model).T, dmod_cols)
    d_, m_, v_ = _adamw(w_ada[0], g_ada, m_w_ada[0], v_w_ada[0], "adamw_w_ada")
    out_g["w_ada"], out_d["w_ada"], out_m["w_ada"], out_v["w_ada"] = g_ada[None], d_[None], m_[None], v_[None]

    small_out = _adamw_small(small_sum, *(pack([tree[n].reshape(1, -1) for n in SMALL])
                                          for tree in (weights, mom_m, mom_v)), sizes)
    for j, n in enumerate(SMALL):
        shape = weights[n].shape
        out_g[n], out_d[n], out_m[n], out_v[n] = (t.reshape(shape) for t in small_out[4 * j:4 * j + 4])

    return (loss, dx0[None], *[out_g[n] for n in WEIGHTS], *[out_d[n] for n in WEIGHTS],
            *[out_m[n] for n in WEIGHTS], *[out_v[n] for n in WEIGHTS])
```
